```python
import jax, jax.numpy as jnp
from jax import lax
import numpy as np

D_MODEL = 1024
BATCH = 8
SEQ = 2048
DEPTH = 4

N_META = 16
D_LRU = D_MODEL // 2
LRU_BLOCKS = 8
LRU_BLOCK_DIM = D_LRU // LRU_BLOCKS
CONV_WIDTH = 4
LRU_C = 8.0
D_RET = D_MODEL // 2
RET_HEADS = 4
RET_HEAD_DIM = D_RET // RET_HEADS
RET_CHUNK = 128
ROPE_BASE = 10000.0
D_MIX = D_LRU + D_RET
D_IN = 2 * D_LRU + 4 * D_RET
D_FF = -(-8 * D_MODEL // (3 * 256)) * 256
EPS = 1e-6

kernel_name = "hymba_rglru_retention_swiglu"


def rmsnorm(x, gain):
    x32 = x.astype(jnp.float32)
    y = x32 * lax.rsqrt(jnp.mean(x32 * x32, axis=-1, keepdims=True) + EPS)
    return (y * gain.astype(jnp.float32)).astype(x.dtype)


def rope(x, pos):
    half = x.shape[-1] // 2
    inv = ROPE_BASE ** (-jnp.arange(half, dtype=jnp.float32) / half)
    ang = pos[:, None] * inv[None, :]
    cos = jnp.cos(ang)[None, :, None, :]
    sin = jnp.sin(ang)[None, :, None, :]
    x1, x2 = x[..., :half], x[..., half:]
    return jnp.concatenate([x1 * cos - x2 * sin, x1 * sin + x2 * cos], axis=-1)


def rglru_group(xb, gate_b, conv_w, conv_b, wa, ba, wx, bx, lam, out_gain):
    B, T, _ = xb.shape
    xf = xb.astype(jnp.float32)
    xp = jnp.pad(xf, ((0, 0), (CONV_WIDTH - 1, 0), (0, 0)))
    xc = conv_b.astype(jnp.float32) + sum(
        xp[:, i:i + T, :] * conv_w[i].astype(jnp.float32) for i in range(CONV_WIDTH))
    blocks = xc.reshape(B, T, LRU_BLOCKS, LRU_BLOCK_DIM)
    r = jax.nn.sigmoid(jnp.einsum('btgi,gij->btgj', blocks, wa.astype(jnp.float32))
                       + ba.astype(jnp.float32)).reshape(B, T, D_LRU)
    ig = jax.nn.sigmoid(jnp.einsum('btgi,gij->btgj', blocks, wx.astype(jnp.float32))
                        + bx.astype(jnp.float32)).reshape(B, T, D_LRU)
    log_a = -LRU_C * r * jax.nn.softplus(-lam.astype(jnp.float32))
    a = jnp.exp(log_a)
    mult = jnp.sqrt(-jnp.expm1(2.0 * log_a))
    b = mult * (ig * xc)

    def combine(left, right):
        a1, b1 = left
        a2, b2 = right
        return a1 * a2, a2 * b1 + b2

    _, h = lax.associative_scan(combine, (a, b), axis=1)
    y = h * jax.nn.gelu(gate_b.astype(jnp.float32))
    return rmsnorm(y, out_gain).astype(xb.dtype)


def retention_group(q, k, v, g, gn_gain):
    B, T, _ = q.shape
    H, d, C = RET_HEADS, RET_HEAD_DIM, RET_CHUNK
    qf = q.astype(jnp.float32).reshape(B, T, H, d)
    kf = k.astype(jnp.float32).reshape(B, T, H, d)
    vf = v.astype(jnp.float32).reshape(B, T, H, d)
    pos = jnp.arange(T, dtype=jnp.float32)
    qf = rope(qf, pos)
    kf = rope(kf, pos) * (d ** -0.5)
    pad = (-T) % C
    padw = ((0, 0), (pad, 0), (0, 0), (0, 0))
    Tp = T + pad
    N = Tp // C
    qc = jnp.pad(qf, padw).reshape(B, N, C, H, d)
    kc = jnp.pad(kf, padw).reshape(B, N, C, H, d)
    vc = jnp.pad(vf, padw).reshape(B, N, C, H, d)

    log_g = jnp.log(1.0 - 2.0 ** (-5.0 - jnp.arange(H, dtype=jnp.float32)))
    idx = jnp.arange(C, dtype=jnp.float32)
    diff = idx[:, None] - idx[None, :]
    dmask = jnp.where(diff[None] >= 0,
                      jnp.exp(jnp.maximum(diff, 0.0)[None] * log_g[:, None, None]), 0.0)
    xi = jnp.exp((idx + 1.0)[:, None] * log_g[None, :])
    zeta = jnp.exp((C - 1.0 - idx)[:, None] * log_g[None, :])
    g_chunk = jnp.exp(C * log_g)

    scores = jnp.einsum('bnchd,bnshd->bnhcs', qc, kc) * dmask[None, None]
    y_intra = jnp.einsum('bnhcs,bnshe->bnche', scores, vc)

    def step(S, xs):
        qi, ki, vi = xs
        cross = jnp.einsum('bchd,bhde->bche', qi, S) * xi[None, :, :, None]
        S = S * g_chunk[None, :, None, None] + jnp.einsum(
            'bchd,bche->bhde', ki * zeta[None, :, :, None], vi)
        return S, cross

    S0 = jnp.zeros((B, H, d, d), jnp.float32)
    _, y_cross = lax.scan(step, S0, (jnp.moveaxis(qc, 1, 0),
                                     jnp.moveaxis(kc, 1, 0),
                                     jnp.moveaxis(vc, 1, 0)))
    y = (y_intra + jnp.moveaxis(y_cross, 0, 1)).reshape(B, Tp, H, d)[:, pad:]
    mu = jnp.mean(y, axis=-1, keepdims=True)
    var = jnp.mean(jnp.square(y - mu), axis=-1, keepdims=True)
    y = ((y - mu) * lax.rsqrt(var + EPS)).reshape(B, T, D_RET) * gn_gain.astype(jnp.float32)
    return (jax.nn.silu(g.astype(jnp.float32)) * y).astype(q.dtype)


def _fwd_setup_inputs(seed: int = 0) -> dict:
    key = jax.random.key(seed)
    ks = jax.random.split(key, 24)
    f32 = jnp.float32

    def nrm(k, shape, scale):
        return jax.random.normal(k, shape, f32) * scale

    def gain(k, shape):
        return 1.0 + 0.02 * jax.random.normal(k, shape, f32)

    u = jax.random.uniform(ks[10], (DEPTH, D_LRU), f32, minval=0.9, maxval=0.999)
    s = u ** (1.0 / LRU_C)
    lru_lambda = jnp.log(s) - jnp.log1p(-s)
    return {
        "x": nrm(ks[0], (BATCH, SEQ, D_MODEL), 1.0),
        "meta_tokens": nrm(ks[1], (N_META, D_MODEL), 1.0),
        "norm_mix": gain(ks[2], (DEPTH, D_MODEL)),
        "w_in": nrm(ks[3], (DEPTH, D_MODEL, D_IN), D_MODEL ** -0.5),
        "conv_w": nrm(ks[4], (DEPTH, CONV_WIDTH, D_LRU), CONV_WIDTH ** -0.5),
        "conv_b": nrm(ks[5], (DEPTH, D_LRU), 0.01),
        "gate_a_w": nrm(ks[6], (DEPTH, LRU_BLOCKS, LRU_BLOCK_DIM, LRU_BLOCK_DIM), LRU_BLOCK_DIM ** -0.5),
        "gate_a_b": nrm(ks[7], (DEPTH, LRU_BLOCKS, LRU_BLOCK_DIM), 0.01),
        "gate_x_w": nrm(ks[8], (DEPTH, LRU_BLOCKS, LRU_BLOCK_DIM, LRU_BLOCK_DIM), LRU_BLOCK_DIM ** -0.5),
        "gate_x_b": nrm(ks[9], (DEPTH, LRU_BLOCKS, LRU_BLOCK_DIM), 0.01),
        "lru_lambda": lru_lambda,
        "lru_out_norm": gain(ks[11], (DEPTH, D_LRU)),
        "ret_out_norm": gain(ks[12], (DEPTH, D_RET)),
        "w_out": nrm(ks[13], (DEPTH, D_MIX, D_MODEL), D_MIX ** -0.5),
        "norm_ffn": gain(ks[14], (DEPTH, D_MODEL)),
        "w_gate": nrm(ks[15], (DEPTH, D_MODEL, D_FF), D_MODEL ** -0.5),
        "w_up": nrm(ks[16], (DEPTH, D_MODEL, D_FF), D_MODEL ** -0.5),
        "w_down": nrm(ks[17], (DEPTH, D_FF, D_MODEL), D_FF ** -0.5),
        "norm_final": gain(ks[18], (D_MODEL,)),
    }


def _fwd_reference(x, meta_tokens, norm_mix, w_in, conv_w, conv_b, gate_a_w, gate_a_b,
              gate_x_w, gate_x_b, lru_lambda, lru_out_norm, ret_out_norm, w_out,
              norm_ffn, w_gate, w_up, w_down, norm_final):
    B = x.shape[0]
    meta = jnp.broadcast_to(meta_tokens[None].astype(x.dtype), (B, N_META, x.shape[-1]))
    h_res = jnp.concatenate([meta, x], axis=1)
    split_at = [D_LRU, 2 * D_LRU, 2 * D_LRU + D_RET, 2 * D_LRU + 2 * D_RET,
                2 * D_LRU + 3 * D_RET]
    for l in range(DEPTH):
        hn = rmsnorm(h_res, norm_mix[l])
        proj = hn @ w_in[l]
        x_lru, g_lru, q, k, v, g_ret = jnp.split(proj, split_at, axis=-1)
        y_lru = rglru_group(x_lru, g_lru, conv_w[l], conv_b[l], gate_a_w[l], gate_a_b[l],
                            gate_x_w[l], gate_x_b[l], lru_lambda[l], lru_out_norm[l])
        y_ret = retention_group(q, k, v, g_ret, ret_out_norm[l])
        h_res = h_res + jnp.concatenate([y_lru, y_ret], axis=-1) @ w_out[l]
        hn = rmsnorm(h_res, norm_ffn[l])
        h_res = h_res + (jax.nn.silu(hn @ w_gate[l]) * (hn @ w_up[l])) @ w_down[l]
    out = rmsnorm(h_res, norm_final)
    return out[:, N_META:, :]


import jax as _jax
import jax.numpy as _jnp

TWIN_FORMAT = 'train_step'
FWD_PARAMS = ['x', 'meta_tokens', 'norm_mix', 'w_in', 'conv_w', 'conv_b', 'gate_a_w', 'gate_a_b', 'gate_x_w', 'gate_x_b', 'lru_lambda', 'lru_out_norm', 'ret_out_norm', 'w_out', 'norm_ffn', 'w_gate', 'w_up', 'w_down', 'norm_final']
TWIN_WEIGHTS = ['meta_tokens', 'norm_mix', 'w_in', 'conv_w', 'conv_b', 'gate_a_w', 'gate_a_b', 'gate_x_w', 'gate_x_b', 'lru_lambda', 'lru_out_norm', 'ret_out_norm', 'w_out', 'norm_ffn', 'w_gate', 'w_up', 'w_down', 'norm_final']
TWIN_DIFF_INPUT = 'x'
TWIN_INPUTS = ['x', 'meta_tokens', 'norm_mix', 'w_in', 'conv_w', 'conv_b', 'gate_a_w', 'gate_a_b', 'gate_x_w', 'gate_x_b', 'lru_lambda', 'lru_out_norm', 'ret_out_norm', 'w_out', 'norm_ffn', 'w_gate', 'w_up', 'w_down', 'norm_final', 'loss_target', 'm_meta_tokens', 'm_norm_mix', 'm_w_in', 'm_conv_w', 'm_conv_b', 'm_gate_a_w', 'm_gate_a_b', 'm_gate_x_w', 'm_gate_x_b', 'm_lru_lambda', 'm_lru_out_norm', 'm_ret_out_norm', 'm_w_out', 'm_norm_ffn', 'm_w_gate', 'm_w_up', 'm_w_down', 'm_norm_final', 'v_meta_tokens', 'v_norm_mix', 'v_w_in', 'v_conv_w', 'v_conv_b', 'v_gate_a_w', 'v_gate_a_b', 'v_gate_x_w', 'v_gate_x_b', 'v_lru_lambda', 'v_lru_out_norm', 'v_ret_out_norm', 'v_w_out', 'v_norm_ffn', 'v_w_gate', 'v_w_up', 'v_w_down', 'v_norm_final']
TWIN_OUTPUTS = ['loss', 'grad_x', 'grad_meta_tokens', 'grad_norm_mix', 'grad_w_in', 'grad_conv_w', 'grad_conv_b', 'grad_gate_a_w', 'grad_gate_a_b', 'grad_gate_x_w', 'grad_gate_x_b', 'grad_lru_lambda', 'grad_lru_out_norm', 'grad_ret_out_norm', 'grad_w_out', 'grad_norm_ffn', 'grad_w_gate', 'grad_w_up', 'grad_w_down', 'grad_norm_final', 'delta_meta_tokens', 'delta_norm_mix', 'delta_w_in', 'delta_conv_w', 'delta_conv_b', 'delta_gate_a_w', 'delta_gate_a_b', 'delta_gate_x_w', 'delta_gate_x_b', 'delta_lru_lambda', 'delta_lru_out_norm', 'delta_ret_out_norm', 'delta_w_out', 'delta_norm_ffn', 'delta_w_gate', 'delta_w_up', 'delta_w_down', 'delta_norm_final', 'new_m_meta_tokens', 'new_m_norm_mix', 'new_m_w_in', 'new_m_conv_w', 'new_m_conv_b', 'new_m_gate_a_w', 'new_m_gate_a_b', 'new_m_gate_x_w', 'new_m_gate_x_b', 'new_m_lru_lambda', 'new_m_lru_out_norm', 'new_m_ret_out_norm', 'new_m_w_out', 'new_m_norm_ffn', 'new_m_w_gate', 'new_m_w_up', 'new_m_w_down', 'new_m_norm_final', 'new_v_meta_tokens', 'new_v_norm_mix', 'new_v_w_in', 'new_v_conv_w', 'new_v_conv_b', 'new_v_gate_a_w', 'new_v_gate_a_b', 'new_v_gate_x_w', 'new_v_gate_x_b', 'new_v_lru_lambda', 'new_v_lru_out_norm', 'new_v_ret_out_norm', 'new_v_w_out', 'new_v_norm_ffn', 'new_v_w_gate', 'new_v_w_up', 'new_v_w_down', 'new_v_norm_final']
TWIN_LEAF_KINDS = {'loss': 'loss', 'grad_x': 'grad_x', 'grad_meta_tokens': 'grad_w', 'grad_norm_mix': 'grad_w', 'grad_w_in': 'grad_w', 'grad_conv_w': 'grad_w', 'grad_conv_b': 'grad_w', 'grad_gate_a_w': 'grad_w', 'grad_gate_a_b': 'grad_w', 'grad_gate_x_w': 'grad_w', 'grad_gate_x_b': 'grad_w', 'grad_lru_lambda': 'grad_w', 'grad_lru_out_norm': 'grad_w', 'grad_ret_out_norm': 'grad_w', 'grad_w_out': 'grad_w', 'grad_norm_ffn': 'grad_w', 'grad_w_gate': 'grad_w', 'grad_w_up': 'grad_w', 'grad_w_down': 'grad_w', 'grad_norm_final': 'grad_w', 'delta_meta_tokens': 'delta_w', 'delta_norm_mix': 'delta_w', 'delta_w_in': 'delta_w', 'delta_conv_w': 'delta_w', 'delta_conv_b': 'delta_w', 'delta_gate_a_w': 'delta_w', 'delta_gate_a_b': 'delta_w', 'delta_gate_x_w': 'delta_w', 'delta_gate_x_b': 'delta_w', 'delta_lru_lambda': 'delta_w', 'delta_lru_out_norm': 'delta_w', 'delta_ret_out_norm': 'delta_w', 'delta_w_out': 'delta_w', 'delta_norm_ffn': 'delta_w', 'delta_w_gate': 'delta_w', 'delta_w_up': 'delta_w', 'delta_w_down': 'delta_w', 'delta_norm_final': 'delta_w', 'new_m_meta_tokens': 'new_m', 'new_m_norm_mix': 'new_m', 'new_m_w_in': 'new_m', 'new_m_conv_w': 'new_m', 'new_m_conv_b': 'new_m', 'new_m_gate_a_w': 'new_m', 'new_m_gate_a_b': 'new_m', 'new_m_gate_x_w': 'new_m', 'new_m_gate_x_b': 'new_m', 'new_m_lru_lambda': 'new_m', 'new_m_lru_out_norm': 'new_m', 'new_m_ret_out_norm': 'new_m', 'new_m_w_out': 'new_m', 'new_m_norm_ffn': 'new_m', 'new_m_w_gate': 'new_m', 'new_m_w_up': 'new_m', 'new_m_w_down': 'new_m', 'new_m_norm_final': 'new_m', 'new_v_meta_tokens': 'new_v', 'new_v_norm_mix': 'new_v', 'new_v_w_in': 'new_v', 'new_v_conv_w': 'new_v', 'new_v_conv_b': 'new_v', 'new_v_gate_a_w': 'new_v', 'new_v_gate_a_b': 'new_v', 'new_v_gate_x_w': 'new_v', 'new_v_gate_x_b': 'new_v', 'new_v_lru_lambda': 'new_v', 'new_v_lru_out_norm': 'new_v', 'new_v_ret_out_norm': 'new_v', 'new_v_w_out': 'new_v', 'new_v_norm_ffn': 'new_v', 'new_v_w_gate': 'new_v', 'new_v_w_up': 'new_v', 'new_v_w_down': 'new_v', 'new_v_norm_final': 'new_v'}


def _forward(args):
    return _fwd_reference(*[args[k] for k in FWD_PARAMS])


def _output_shape():
    out = _jax.eval_shape(lambda: _forward(_fwd_setup_inputs(0)))
    return out.shape, out.dtype

N_MICROBATCH = 1
ADAM_LR = 0.001
ADAM_B1 = 0.9
ADAM_B2 = 0.999
ADAM_EPS = 1e-08
ADAM_WD = 0.01
ADAM_STEP = 10
PER_EXAMPLE_BATCH_AXIS = {'x': 0, 'loss_target': 0}
SHARED_INPUTS = []
_WEIGHT_DTYPES = {'meta_tokens': _jnp.float32, 'norm_mix': _jnp.float32, 'w_in': _jnp.float32, 'conv_w': _jnp.float32, 'conv_b': _jnp.float32, 'gate_a_w': _jnp.float32, 'gate_a_b': _jnp.float32, 'gate_x_w': _jnp.float32, 'gate_x_b': _jnp.float32, 'lru_lambda': _jnp.float32, 'lru_out_norm': _jnp.float32, 'ret_out_norm': _jnp.float32, 'w_out': _jnp.float32, 'norm_ffn': _jnp.float32, 'w_gate': _jnp.float32, 'w_up': _jnp.float32, 'w_down': _jnp.float32, 'norm_final': _jnp.float32}
MOMENT_SCALE = {'meta_tokens': 2.460166e-02, 'norm_mix': 1.564130e-01, 'w_in': 8.819742e-02, 'conv_w': 1.304070e-01, 'conv_b': 1.281638e+00, 'gate_a_w': 4.376202e-02, 'gate_a_b': 3.650978e-02, 'gate_x_w': 8.100455e-02, 'gate_x_b': 4.796674e-02, 'lru_lambda': 6.784141e-02, 'lru_out_norm': 1.364812e-01, 'ret_out_norm': 6.469861e-02, 'w_out': 1.054290e-01, 'norm_ffn': 8.094714e-02, 'w_gate': 3.491784e-02, 'w_up': 3.410638e-02, 'w_down': 5.654330e-02, 'norm_final': 1.604404e+01}


def _to_microbatches(a, axis):
    t = _jnp.moveaxis(a, axis, 0)
    t = t.reshape((N_MICROBATCH, t.shape[0] // N_MICROBATCH) + t.shape[1:])
    return _jnp.moveaxis(t, 1, axis + 1)


def setup_inputs(seed: int = 0) -> dict:
    inp = _fwd_setup_inputs(seed)
    key = _jax.random.fold_in(_jax.random.key(seed), 7919)
    shape, _ = _output_shape()
    out = dict(inp)
    out["loss_target"] = _jax.random.normal(_jax.random.fold_in(key, 0), shape, _jnp.float32)
    for i, name in enumerate(TWIN_WEIGHTS):
        w = inp[name].astype(_jnp.float32)
        if MOMENT_SCALE is None:
            s = _jnp.sqrt(_jnp.mean(_jnp.square(w)) + 1e-30)
        else:
            s = MOMENT_SCALE[name]
        km, kv = _jax.random.split(_jax.random.fold_in(key, i + 1))
        out[name] = w
        out["m_" + name] = s * _jax.random.normal(km, w.shape, _jnp.float32)
        out["v_" + name] = (s * s) * _jax.random.uniform(kv, w.shape, _jnp.float32, 0.5, 1.5)
    if N_MICROBATCH > 1:
        for name, axis in PER_EXAMPLE_BATCH_AXIS.items():
            out[name] = _to_microbatches(out[name], axis)
    return {'x': out['x'], 'meta_tokens': out['meta_tokens'], 'norm_mix': out['norm_mix'], 'w_in': out['w_in'], 'conv_w': out['conv_w'], 'conv_b': out['conv_b'], 'gate_a_w': out['gate_a_w'], 'gate_a_b': out['gate_a_b'], 'gate_x_w': out['gate_x_w'], 'gate_x_b': out['gate_x_b'], 'lru_lambda': out['lru_lambda'], 'lru_out_norm': out['lru_out_norm'], 'ret_out_norm': out['ret_out_norm'], 'w_out': out['w_out'], 'norm_ffn': out['norm_ffn'], 'w_gate': out['w_gate'], 'w_up': out['w_up'], 'w_down': out['w_down'], 'norm_final': out['norm_final'], 'loss_target': out['loss_target'], 'm_meta_tokens': out['m_meta_tokens'], 'm_norm_mix': out['m_norm_mix'], 'm_w_in': out['m_w_in'], 'm_conv_w': out['m_conv_w'], 'm_conv_b': out['m_conv_b'], 'm_gate_a_w': out['m_gate_a_w'], 'm_gate_a_b': out['m_gate_a_b'], 'm_gate_x_w': out['m_gate_x_w'], 'm_gate_x_b': out['m_gate_x_b'], 'm_lru_lambda': out['m_lru_lambda'], 'm_lru_out_norm': out['m_lru_out_norm'], 'm_ret_out_norm': out['m_ret_out_norm'], 'm_w_out': out['m_w_out'], 'm_norm_ffn': out['m_norm_ffn'], 'm_w_gate': out['m_w_gate'], 'm_w_up': out['m_w_up'], 'm_w_down': out['m_w_down'], 'm_norm_final': out['m_norm_final'], 'v_meta_tokens': out['v_meta_tokens'], 'v_norm_mix': out['v_norm_mix'], 'v_w_in': out['v_w_in'], 'v_conv_w': out['v_conv_w'], 'v_conv_b': out['v_conv_b'], 'v_gate_a_w': out['v_gate_a_w'], 'v_gate_a_b': out['v_gate_a_b'], 'v_gate_x_w': out['v_gate_x_w'], 'v_gate_x_b': out['v_gate_x_b'], 'v_lru_lambda': out['v_lru_lambda'], 'v_lru_out_norm': out['v_lru_out_norm'], 'v_ret_out_norm': out['v_ret_out_norm'], 'v_w_out': out['v_w_out'], 'v_norm_ffn': out['v_norm_ffn'], 'v_w_gate': out['v_w_gate'], 'v_w_up': out['v_w_up'], 'v_w_down': out['v_w_down'], 'v_norm_final': out['v_norm_final']}


def _loss(weights, diff, rest, loss_target):
    with _jax.named_scope("forward"):
        args = {**rest, TWIN_DIFF_INPUT: diff, **{k: w.astype(_WEIGHT_DTYPES[k]) for k, w in weights.items()}}
        y = _forward(args)
    with _jax.named_scope("loss_head"):
        err = _jnp.square(y.astype(_jnp.float32) - loss_target)
        return 0.5 * _jnp.sum(_jnp.mean(err, axis=-1)) if err.ndim else 0.5 * err


def _adamw(w, g, m, v):
    m = ADAM_B1 * m + (1.0 - ADAM_B1) * g
    v = ADAM_B2 * v + (1.0 - ADAM_B2) * _jnp.square(g)
    m_hat = m / (1.0 - ADAM_B1 ** ADAM_STEP)
    v_hat = v / (1.0 - ADAM_B2 ** ADAM_STEP)
    delta = -ADAM_LR * (m_hat / (_jnp.sqrt(v_hat) + ADAM_EPS) + ADAM_WD * w)
    return delta, m, v


def reference(x, meta_tokens, norm_mix, w_in, conv_w, conv_b, gate_a_w, gate_a_b, gate_x_w, gate_x_b, lru_lambda, lru_out_norm, ret_out_norm, w_out, norm_ffn, w_gate, w_up, w_down, norm_final, loss_target, m_meta_tokens, m_norm_mix, m_w_in, m_conv_w, m_conv_b, m_gate_a_w, m_gate_a_b, m_gate_x_w, m_gate_x_b, m_lru_lambda, m_lru_out_norm, m_ret_out_norm, m_w_out, m_norm_ffn, m_w_gate, m_w_up, m_w_down, m_norm_final, v_meta_tokens, v_norm_mix, v_w_in, v_conv_w, v_conv_b, v_gate_a_w, v_gate_a_b, v_gate_x_w, v_gate_x_b, v_lru_lambda, v_lru_out_norm, v_ret_out_norm, v_w_out, v_norm_ffn, v_w_gate, v_w_up, v_w_down, v_norm_final):
    given = dict(x=x, meta_tokens=meta_tokens, norm_mix=norm_mix, w_in=w_in, conv_w=conv_w, conv_b=conv_b, gate_a_w=gate_a_w, gate_a_b=gate_a_b, gate_x_w=gate_x_w, gate_x_b=gate_x_b, lru_lambda=lru_lambda, lru_out_norm=lru_out_norm, ret_out_norm=ret_out_norm, w_out=w_out, norm_ffn=norm_ffn, w_gate=w_gate, w_up=w_up, w_down=w_down, norm_final=norm_final, loss_target=loss_target, m_meta_tokens=m_meta_tokens, m_norm_mix=m_norm_mix, m_w_in=m_w_in, m_conv_w=m_conv_w, m_conv_b=m_conv_b, m_gate_a_w=m_gate_a_w, m_gate_a_b=m_gate_a_b, m_gate_x_w=m_gate_x_w, m_gate_x_b=m_gate_x_b, m_lru_lambda=m_lru_lambda, m_lru_out_norm=m_lru_out_norm, m_ret_out_norm=m_ret_out_norm, m_w_out=m_w_out, m_norm_ffn=m_norm_ffn, m_w_gate=m_w_gate, m_w_up=m_w_up, m_w_down=m_w_down, m_norm_final=m_norm_final, v_meta_tokens=v_meta_tokens, v_norm_mix=v_norm_mix, v_w_in=v_w_in, v_conv_w=v_conv_w, v_conv_b=v_conv_b, v_gate_a_w=v_gate_a_w, v_gate_a_b=v_gate_a_b, v_gate_x_w=v_gate_x_w, v_gate_x_b=v_gate_x_b, v_lru_lambda=v_lru_lambda, v_lru_out_norm=v_lru_out_norm, v_ret_out_norm=v_ret_out_norm, v_w_out=v_w_out, v_norm_ffn=v_norm_ffn, v_w_gate=v_w_gate, v_w_up=v_w_up, v_w_down=v_w_down, v_norm_final=v_norm_final)
    weights = {n: given[n] for n in TWIN_WEIGHTS}
    shared = {n: given[n] for n in SHARED_INPUTS}
    per_example = {n: given[n] for n in ['x']}
    grad_fn = _jax.value_and_grad(_loss, argnums=(0, 1))

    def one_microbatch(ex, loss_target):
        ex = dict(ex)
        diff = ex.pop(TWIN_DIFF_INPUT)
        return grad_fn(weights, diff, {**shared, **ex}, loss_target)

    if N_MICROBATCH == 1:
        loss, (grad_w, grad_x) = one_microbatch(per_example, given["loss_target"])
    else:
        def body(carry, xs):
            loss_sum, grad_sum = carry
            l_k, (gw_k, gx_k) = one_microbatch(xs[0], xs[1])
            with _jax.named_scope("update"):
                return (loss_sum + l_k, _jax.tree.map(_jnp.add, grad_sum, gw_k)), gx_k

        init = (_jnp.zeros((), _jnp.float32), _jax.tree.map(_jnp.zeros_like, weights))
        (loss, grad_w), grad_x = _jax.lax.scan(body, init, (per_example, given["loss_target"]))
    with _jax.named_scope("update"):
        delta_w, new_m, new_v = {}, {}, {}
        for n in TWIN_WEIGHTS:
            delta_w[n], new_m[n], new_v[n] = _adamw(weights[n], grad_w[n], given["m_" + n], given["v_" + n])
    return (loss, grad_x, *[grad_w[n] for n in TWIN_WEIGHTS], *[delta_w[n] for n in TWIN_WEIGHTS],
            *[new_m[n] for n in TWIN_WEIGHTS], *[new_v[n] for n in TWIN_WEIGHTS])
```

```python
import functools

import numpy as np
import jax
import jax.numpy as jnp
from jax import lax
from jax.experimental import pallas as pl
from jax.experimental.pallas import tpu as pltpu

F32, BF16 = jnp.float32, jnp.bfloat16
MXU_DTYPE = BF16
WIRE_DTYPE = BF16

D = 1024
SEQ = 2048
DEPTH = 4
N_META = 16
CH = 128
PAD = (-(SEQ + N_META)) % CH
T = SEQ + N_META + PAD
NCH = T // CH
X0 = PAD + N_META
D_LRU = 512
LRU_BLOCKS = 8
LRU_BD = 64
CONV_W = 4
LRU_C = 8.0
D_RET = 512
HEADS = 4
HD = 128
ROPE_BASE = 10000.0
D_IN = 3072
D_FF = 2816
NDEV = 8
IN_SH = D_IN // NDEV
FF_SH = D_FF // NDEV
FF_SHP = 384
D_FFP = NDEV * FF_SHP
OUT_SH = D // NDEV
EPS = 1e-6
TM = 544
VMEM_LIMIT = 56 * 2**20
MESH = pl.DeviceIdType.MESH

ADAM_LR, ADAM_B1, ADAM_B2, ADAM_EPS, ADAM_WD, ADAM_STEP = 0.001, 0.9, 0.999, 1e-08, 0.01, 10

NN = ((1,), (0,))
NT = ((1,), (1,))
TN = ((0,), (0,))


def _dot(a, b, dims):
    return lax.dot_general(a.astype(MXU_DTYPE), b.astype(MXU_DTYPE), (dims, ((), ())), preferred_element_type=F32)


def _sds(shape, dtype):
    return jax.ShapeDtypeStruct(shape, dtype)


def _params(sem=None):
    return pltpu.CompilerParams(dimension_semantics=sem, vmem_limit_bytes=VMEM_LIMIT)


def _full(shape):
    n = len(shape)
    return pl.BlockSpec(shape, lambda *_: (0,) * n)


def rmsnorm_fwd(h, gain, name):
    def body(h_ref, g_ref, o_ref):
        x = h_ref[...]
        ms = jnp.mean(x * x, axis=-1, keepdims=True)
        o_ref[...] = (x * lax.rsqrt(ms + EPS) * g_ref[...]).astype(o_ref.dtype)

    return pl.pallas_call(
        body, name=name, grid=(T // TM,),
        in_specs=[pl.BlockSpec((TM, D), lambda i: (i, 0)), _full((1, D))],
        out_specs=pl.BlockSpec((TM, D), lambda i: (i, 0)),
        out_shape=_sds((T, D), MXU_DTYPE), compiler_params=_params(("parallel",)),
    )(h, gain)


def rmsnorm_bwd(h, gain, dhn, dres, name):
    def body(h_ref, g_ref, dhn_ref, dres_ref, dh_ref, dg_ref):
        x = h_ref[...]
        rstd = lax.rsqrt(jnp.mean(x * x, axis=-1, keepdims=True) + EPS)
        xhat = x * rstd
        dy = dhn_ref[...]
        dyg = dy * g_ref[...]
        dh_ref[...] = dres_ref[...] + rstd * (dyg - xhat * jnp.mean(dyg * xhat, axis=-1, keepdims=True))

        @pl.when(pl.program_id(0) == 0)
        def _():
            dg_ref[...] = jnp.zeros_like(dg_ref)
        dg_ref[...] += jnp.sum(dy * xhat, axis=0, keepdims=True)

    row = pl.BlockSpec((TM, D), lambda i: (i, 0))
    return pl.pallas_call(
        body, name=name, grid=(T // TM,),
        in_specs=[row, _full((1, D)), row, row],
        out_specs=[row, _full((1, D))],
        out_shape=[_sds((T, D), F32), _sds((1, D), F32)], compiler_params=_params(("arbitrary",)),
    )(h, gain, dhn, dres)


def loss_head(h, gain, target, name):
    def body(h_ref, g_ref, t_ref, loss_ref, dh_ref, dg_ref):
        i = pl.program_id(0)

        @pl.when(i == 0)
        def _():
            loss_ref[...] = jnp.zeros_like(loss_ref)
            dg_ref[...] = jnp.zeros_like(dg_ref)
            dh_ref[...] = jnp.zeros_like(dh_ref)

        @pl.when(i > 0)
        def _():
            x = h_ref[...]
            g = g_ref[...]
            rstd = lax.rsqrt(jnp.mean(x * x, axis=-1, keepdims=True) + EPS)
            xhat = x * rstd
            err = xhat * g - t_ref[...]
            loss_ref[...] += 0.5 * jnp.sum(jnp.mean(err * err, axis=-1, keepdims=True), axis=0, keepdims=True)
            dy = err * (1.0 / D)
            dyg = dy * g
            dh_ref[...] = rstd * (dyg - xhat * jnp.mean(dyg * xhat, axis=-1, keepdims=True))
            dg_ref[...] += jnp.sum(dy * xhat, axis=0, keepdims=True)

    row = pl.BlockSpec((CH, D), lambda i: (i, 0))
    return pl.pallas_call(
        body, name=name, grid=(NCH,),
        in_specs=[row, _full((1, D)), pl.BlockSpec((CH, D), lambda i: (jnp.maximum(i - 1, 0), 0))],
        out_specs=[_full((8, 128)), row, _full((1, D))],
        out_shape=[_sds((8, 128), F32), _sds((T, D), F32), _sds((1, D), F32)],
        compiler_params=_params(("arbitrary",)),
    )(h, gain, target)


def mm_blocked_nn(a, w, out_dtype, name):
    k = a.shape[1]

    def body(a_ref, w_ref, o_ref):
        o_ref[...] = _dot(a_ref[...], w_ref[...], NN).astype(o_ref.dtype)

    return pl.pallas_call(
        body, name=name, grid=(NDEV, T // TM),
        in_specs=[pl.BlockSpec((TM, k), lambda j, i: (i, 0)), pl.BlockSpec((None, k, IN_SH), lambda j, i: (j, 0, 0))],
        out_specs=pl.BlockSpec((TM, IN_SH), lambda j, i: (i, j)),
        out_shape=_sds((T, NDEV * IN_SH), out_dtype), compiler_params=_params(("parallel", "parallel")),
    )(a, w)


def mm_nn_res(a, w, res, name):
    k = a.shape[1]
    bn = 512

    def body(a_ref, w_ref, r_ref, o_ref):
        o_ref[...] = r_ref[...] + _dot(a_ref[...], w_ref[...], NN)

    return pl.pallas_call(
        body, name=name, grid=(D // bn, T // TM),
        in_specs=[pl.BlockSpec((TM, k), lambda j, i: (i, 0)), pl.BlockSpec((k, bn), lambda j, i: (0, j)),
                  pl.BlockSpec((TM, bn), lambda j, i: (i, j))],
        out_specs=pl.BlockSpec((TM, bn), lambda j, i: (i, j)),
        out_shape=_sds((T, D), F32), compiler_params=_params(("parallel", "parallel")),
    )(a, w, res)


def ffn_up(hn, wg, wu, name):
    def body(a_ref, wg_ref, wu_ref, g_ref, u_ref, act_ref):
        a = a_ref[...]
        g = _dot(a, wg_ref[...], NN)
        u = _dot(a, wu_ref[...], NN)
        g_ref[...] = g.astype(g_ref.dtype)
        u_ref[...] = u.astype(u_ref.dtype)
        act_ref[...] = (jax.nn.silu(g) * u).astype(act_ref.dtype)

    wspec = pl.BlockSpec((None, D, FF_SHP), lambda j, i: (j, 0, 0))
    ospec = pl.BlockSpec((TM, FF_SHP), lambda j, i: (i, j))
    return pl.pallas_call(
        body, name=name, grid=(NDEV, T // TM),
        in_specs=[pl.BlockSpec((TM, D), lambda j, i: (i, 0)), wspec, wspec],
        out_specs=[ospec, ospec, ospec],
        out_shape=[_sds((T, D_FFP), MXU_DTYPE)] * 3, compiler_params=_params(("parallel", "parallel")),
    )(hn, wg, wu)


def ffn_down_bwd(dh, wd, gate, up, name):
    def body(dh_ref, wd_ref, g_ref, u_ref, dg_ref, du_ref):
        dact = _dot(dh_ref[...], wd_ref[...], NT)
        g = g_ref[...].astype(F32)
        u = u_ref[...].astype(F32)
        sg = jax.nn.sigmoid(g)
        dg_ref[...] = (dact * u * (sg * (1.0 + g * (1.0 - sg)))).astype(dg_ref.dtype)
        du_ref[...] = (dact * (g * sg)).astype(du_ref.dtype)

    blk = pl.BlockSpec((TM, FF_SHP), lambda j, i: (i, j))
    return pl.pallas_call(
        body, name=name, grid=(NDEV, T // TM),
        in_specs=[pl.BlockSpec((TM, D), lambda j, i: (i, 0)), pl.BlockSpec((FF_SHP, D), lambda j, i: (j, 0)), blk, blk],
        out_specs=[blk, blk],
        out_shape=[_sds((T, D_FFP), MXU_DTYPE)] * 2, compiler_params=_params(("parallel", "parallel")),
    )(dh, wd, gate, up)


def mm_nt(a, w, out_dtype, name):
    n = a.shape[1]
    bk = 512

    def body(a_ref, w_ref, o_ref):
        o_ref[...] = _dot(a_ref[...], w_ref[...], NT).astype(o_ref.dtype)

    return pl.pallas_call(
        body, name=name, grid=(D // bk, T // TM),
        in_specs=[pl.BlockSpec((TM, n), lambda j, i: (i, 0)), pl.BlockSpec((bk, n), lambda j, i: (j, 0))],
        out_specs=pl.BlockSpec((TM, bk), lambda j, i: (i, j)),
        out_shape=_sds((T, D), out_dtype), compiler_params=_params(("parallel", "parallel")),
    )(a, w)


def mm_blocked_nt(pairs, name):
    n = len(pairs)

    def body(*refs):
        o_ref = refs[2 * n]

        @pl.when(pl.program_id(1) == 0)
        def _():
            o_ref[...] = jnp.zeros_like(o_ref)
        acc = _dot(refs[0][...], refs[1][...], NT)
        for p in range(1, n):
            acc += _dot(refs[2 * p][...], refs[2 * p + 1][...], NT)
        o_ref[...] += acc

    specs, args = [], []
    for a, w in pairs:
        specs += [pl.BlockSpec((TM, IN_SH), lambda i, j: (i, j)), pl.BlockSpec((None, D, IN_SH), lambda i, j: (j, 0, 0))]
        args += [a, w]
    return pl.pallas_call(
        body, name=name, grid=(T // TM, NDEV), in_specs=specs,
        out_specs=pl.BlockSpec((TM, D), lambda i, j: (i, 0)),
        out_shape=_sds((T, D), F32), compiler_params=_params(("parallel", "arbitrary")),
    )(*args)


def mm_tn_blocked(a, b, name):
    def body(a_ref, b_ref, o_ref):
        o_ref[...] = _dot(a_ref[...], b_ref[...], TN).astype(o_ref.dtype)

    return pl.pallas_call(
        body, name=name, grid=(NDEV,),
        in_specs=[_full((T, D)), pl.BlockSpec((T, IN_SH), lambda j: (0, j))],
        out_specs=pl.BlockSpec((None, D, IN_SH), lambda j: (j, 0, 0)),
        out_shape=_sds((NDEV, D, IN_SH), WIRE_DTYPE), compiler_params=_params(("parallel",)),
    )(a, b)


def mm_tn(a, b, bm, name):
    m = a.shape[1]
    bn = 512

    def body(a_ref, b_ref, o_ref):
        o_ref[...] = _dot(a_ref[...], b_ref[...], TN).astype(o_ref.dtype)

    return pl.pallas_call(
        body, name=name, grid=(m // bm, D // bn),
        in_specs=[pl.BlockSpec((T, bm), lambda i, j: (0, i)), pl.BlockSpec((T, bn), lambda i, j: (0, j))],
        out_specs=pl.BlockSpec((bm, bn), lambda i, j: (i, j)),
        out_shape=_sds((m, D), WIRE_DTYPE), compiler_params=_params(("parallel", "parallel")),
    )(a, b)


def _softplus_neg(lam):
    return jnp.maximum(-lam, 0.0) + jnp.log1p(jnp.exp(-jnp.abs(lam)))


def _lru_gates(pa, px, xc, lam):
    r = jax.nn.sigmoid(pa)
    ig = jax.nn.sigmoid(px)
    log_a = -LRU_C * r * _softplus_neg(lam)
    a = jnp.exp(log_a)
    mult = jnp.sqrt(-jnp.tanh(log_a) * (jnp.exp(2.0 * log_a) + 1.0))
    return a, mult * (ig * xc)


def _lru_out(h, g, gain):
    z = h * jax.nn.gelu(g)
    return z * lax.rsqrt(jnp.mean(z * z, axis=-1, keepdims=True) + EPS) * gain


def _conv_taps(x, xprev, row):
    taps = [x]
    for s in range(1, CONV_W):
        taps.append(jnp.where(row < s, pltpu.roll(xprev, s, 0), pltpu.roll(x, s, 0)))
    return taps


def _conv(taps, cw_ref, cb):
    xc = cb + cw_ref[CONV_W - 1:CONV_W, :] * taps[0]
    for s in range(1, CONV_W):
        xc = xc + cw_ref[CONV_W - 1 - s:CONV_W - s, :] * taps[s]
    return xc


def lru_fwd(proj, cw, cb, wa, ba, wx, bx, lam, gain, name):
    def body(x_ref, g_ref, cw_ref, cb_ref, wa_ref, ba_ref, wx_ref, bx_ref, lam_ref, gain_ref,
             y_ref, h_ref, xprev_scr, a_scr, b_scr, carry_scr):
        i = pl.program_id(0)

        @pl.when(i == 0)
        def _():
            xprev_scr[...] = jnp.zeros_like(xprev_scr)
            carry_scr[...] = jnp.zeros_like(carry_scr)

        x = x_ref[...]
        row = lax.broadcasted_iota(jnp.int32, (CH, D_LRU), 0)
        xc = _conv(_conv_taps(x, xprev_scr[...], row), cw_ref, cb_ref[...])
        pa = _dot(xc, wa_ref[...], NN) + ba_ref[...]
        px = _dot(xc, wx_ref[...], NN) + bx_ref[...]
        a, b = _lru_gates(pa, px, xc, lam_ref[...])
        a_scr[...] = a
        b_scr[...] = jnp.where(i * CH + row >= PAD, b, 0.0)
        h = carry_scr[...]
        for t in range(CH):
            h = a_scr[t:t + 1, :] * h + b_scr[t:t + 1, :]
            h_ref[t:t + 1, :] = h
        carry_scr[...] = h
        xprev_scr[...] = x
        y_ref[...] = _lru_out(h_ref[...], g_ref[...], gain_ref[...]).astype(y_ref.dtype)

    vec = _full((1, D_LRU))
    mat = _full((D_LRU, D_LRU))
    return pl.pallas_call(
        body, name=name, grid=(NCH,),
        in_specs=[pl.BlockSpec((CH, D_LRU), lambda i: (i, 0)), pl.BlockSpec((CH, D_LRU), lambda i: (i, 1)),
                  _full((CONV_W, D_LRU)), vec, mat, vec, mat, vec, vec, vec],
        out_specs=[pl.BlockSpec((CH, D_LRU), lambda i: (i, 0)), pl.BlockSpec((CH, D_LRU), lambda i: (i, 0))],
        out_shape=[_sds((T, D_LRU), MXU_DTYPE), _sds((T, D_LRU), F32)],
        scratch_shapes=[pltpu.VMEM((CH, D_LRU), F32), pltpu.VMEM((CH, D_LRU), F32), pltpu.VMEM((CH, D_LRU), F32),
                        pltpu.VMEM((1, D_LRU), F32)],
        compiler_params=_params(("arbitrary",)),
    )(proj, proj, cw, cb, wa, ba, wx, bx, lam, gain)


LRU_VEC_ROWS = 16


def lru_bwd(proj, hst, dymix, cw, cb, wa, ba, wx, bx, lam, gain, name):
    last = NCH - 1

    def body(x_ref, xp_ref, g_ref, h_ref, hp_ref, dy_ref, cw_ref, cb_ref, wa_ref, ba_ref, wx_ref, bx_ref, lam_ref,
             gain_ref, dxg_ref, vec_ref, dwa_ref, dwx_ref, a_scr, dh_scr, g_scr, carry_scr, dxcn_scr):
        i = pl.program_id(0)
        ib = last - i

        @pl.when(i == 0)
        def _():
            carry_scr[...] = jnp.zeros_like(carry_scr)
            dxcn_scr[...] = jnp.zeros_like(dxcn_scr)
            vec_ref[...] = jnp.zeros_like(vec_ref)
            dwa_ref[...] = jnp.zeros_like(dwa_ref)
            dwx_ref[...] = jnp.zeros_like(dwx_ref)

        x = x_ref[...]
        row = lax.broadcasted_iota(jnp.int32, (CH, D_LRU), 0)
        valid = ib * CH + row >= PAD
        taps = _conv_taps(x, xp_ref[...], row)
        xc = _conv(taps, cw_ref, cb_ref[...])
        pa = _dot(xc, wa_ref[...], NN) + ba_ref[...]
        px = _dot(xc, wx_ref[...], NN) + bx_ref[...]
        (a, _), vjp_gates = jax.vjp(_lru_gates, pa, px, xc, lam_ref[...])
        h = h_ref[...]
        _, vjp_out = jax.vjp(_lru_out, h, g_ref[...], gain_ref[...])
        dh, dg, dgain = vjp_out(dy_ref[...].astype(F32))
        a_scr[...] = a
        dh_scr[...] = dh
        c = carry_scr[...]
        for t in range(CH - 1, -1, -1):
            gt = dh_scr[t:t + 1, :] + c
            g_scr[t:t + 1, :] = gt
            c = a_scr[t:t + 1, :] * gt
        carry_scr[...] = c
        gg = g_scr[...]
        hprev = jnp.where(row < 1, pltpu.roll(hp_ref[...], 1, 0), pltpu.roll(h, 1, 0))
        da = jnp.where(valid, gg * hprev, 0.0)
        db = jnp.where(valid, gg, 0.0)
        dpa, dpx, dxc, dlam = vjp_gates((da, db))
        dxc = dxc + _dot(dpa, wa_ref[...], NT) + _dot(dpx, wx_ref[...], NT)
        dwa_ref[...] += _dot(xc, dpa, TN)
        dwx_ref[...] += _dot(xc, dpx, TN)
        for s in range(CONV_W):
            vec_ref[CONV_W - 1 - s:CONV_W - s, :] += jnp.sum(dxc * taps[s], axis=0, keepdims=True)
        vec_ref[4:5, :] += jnp.sum(dxc, axis=0, keepdims=True)
        vec_ref[5:6, :] += jnp.sum(dpa, axis=0, keepdims=True)
        vec_ref[6:7, :] += jnp.sum(dpx, axis=0, keepdims=True)
        vec_ref[7:8, :] += dlam
        vec_ref[8:9, :] += dgain
        dxn = dxcn_scr[...]
        dx = cw_ref[CONV_W - 1:CONV_W, :] * dxc
        for s in range(1, CONV_W):
            ahead = jnp.where(row >= CH - s, pltpu.roll(dxn, CH - s, 0), pltpu.roll(dxc, CH - s, 0))
            dx = dx + cw_ref[CONV_W - 1 - s:CONV_W - s, :] * ahead
        dxcn_scr[...] = dxc
        dxg_ref[:, :D_LRU] = jnp.where(valid, dx, 0.0).astype(dxg_ref.dtype)
        dxg_ref[:, D_LRU:] = dg.astype(dxg_ref.dtype)

    vec = _full((1, D_LRU))
    mat = _full((D_LRU, D_LRU))

    def blk(col, shift=0):
        return pl.BlockSpec((CH, D_LRU), lambda i: (jnp.maximum(last - i - shift, 0), col))

    return pl.pallas_call(
        body, name=name, grid=(NCH,),
        in_specs=[blk(0), blk(0, 1), blk(1), blk(0), blk(0, 1), blk(0),
                  _full((CONV_W, D_LRU)), vec, mat, vec, mat, vec, vec, vec],
        out_specs=[pl.BlockSpec((CH, 2 * D_LRU), lambda i: (last - i, 0)), _full((LRU_VEC_ROWS, D_LRU)), mat, mat],
        out_shape=[_sds((T, 2 * D_LRU), MXU_DTYPE), _sds((LRU_VEC_ROWS, D_LRU), F32),
                   _sds((D_LRU, D_LRU), F32), _sds((D_LRU, D_LRU), F32)],
        scratch_shapes=[pltpu.VMEM((CH, D_LRU), F32), pltpu.VMEM((CH, D_LRU), F32), pltpu.VMEM((CH, D_LRU), F32),
                        pltpu.VMEM((1, D_LRU), F32), pltpu.VMEM((CH, D_LRU), F32)],
        compiler_params=_params(("arbitrary",)),
    )(proj, proj, proj, hst, hst, dymix, cw, cb, wa, ba, wx, bx, lam, gain)


def _ret_tables():
    half = HD // 2
    pos = jnp.arange(T, dtype=F32) - float(PAD)
    inv = ROPE_BASE ** (-jnp.arange(half, dtype=F32) / half)
    ang = pos[:, None] * inv[None, :]
    cos = jnp.concatenate([jnp.cos(ang), jnp.cos(ang)], axis=-1)
    sin = jnp.concatenate([-jnp.sin(ang), jnp.sin(ang)], axis=-1)
    log_g = jnp.log(1.0 - 2.0 ** (-5.0 - jnp.arange(HEADS, dtype=F32)))
    idx = jnp.arange(CH, dtype=F32)
    diff = idx[:, None] - idx[None, :]
    dmask = jnp.where(diff[None] >= 0, jnp.exp(jnp.maximum(diff, 0.0)[None] * log_g[:, None, None]), 0.0)
    xi = jnp.exp((idx + 1.0)[None, :] * log_g[:, None])
    zeta = jnp.exp((CH - 1.0 - idx)[None, :] * log_g[:, None])
    xi = jnp.broadcast_to(xi[:, :, None], (HEADS, CH, HD))
    zeta = jnp.broadcast_to(zeta[:, :, None], (HEADS, CH, HD))
    return cos, sin, dmask, xi, zeta


def _chunk_decay():
    log_g = np.log(np.float32(1.0) - np.float32(2.0) ** (np.float32(-5.0) - np.arange(HEADS, dtype=np.float32)))
    return [float(v) for v in np.exp(np.float32(CH) * log_g.astype(np.float32))]


def _rope(x, cos, sin):
    return x * cos + pltpu.roll(x, HD // 2, 1) * sin


def ret_fwd(proj, ylru, tables, gain, name):
    cos, sin, dmask, xi, zeta = tables
    gch = _chunk_decay()
    scale = HD ** -0.5

    def body(q_ref, k_ref, v_ref, g_ref, cos_ref, sin_ref, dm_ref, xi_ref, zt_ref, gain_ref, ylru_ref,
             y_ref, st_ref, s_scr):
        @pl.when(pl.program_id(0) == 0)
        def _():
            s_scr[...] = jnp.zeros_like(s_scr)

        y_ref[:, :D_LRU] = ylru_ref[...]
        cs, sn = cos_ref[...], sin_ref[...]
        for h in range(HEADS):
            sl = slice(HD * h, HD * (h + 1))
            so = slice(D_LRU + HD * h, D_LRU + HD * (h + 1))
            qr = _rope(q_ref[:, sl], cs, sn)
            kr = _rope(k_ref[:, sl], cs, sn) * scale
            v = v_ref[:, sl]
            s = s_scr[h]
            st_ref[h] = s
            sc = _dot(qr, kr, NT) * dm_ref[h]
            y = _dot(sc, v, NN) + _dot(qr, s, NN) * xi_ref[h]
            s_scr[h] = s * gch[h] + _dot(kr * zt_ref[h], v, TN)
            yc = y - jnp.mean(y, axis=-1, keepdims=True)
            yn = yc * lax.rsqrt(jnp.mean(yc * yc, axis=-1, keepdims=True) + EPS)
            y_ref[:, so] = (jax.nn.silu(g_ref[:, sl]) * (yn * gain_ref[:, sl])).astype(y_ref.dtype)

    def col(c):
        return pl.BlockSpec((CH, D_RET), lambda n: (n, c))

    tab = pl.BlockSpec((CH, HD), lambda n: (n, 0))
    cst = _full((HEADS, CH, HD))
    return pl.pallas_call(
        body, name=name, grid=(NCH,),
        in_specs=[col(2), col(3), col(4), col(5), tab, tab, cst, cst, cst, _full((1, D_RET)), col(0)],
        out_specs=[pl.BlockSpec((CH, D), lambda n: (n, 0)), pl.BlockSpec((None, HEADS, HD, HD), lambda n: (n, 0, 0, 0))],
        out_shape=[_sds((T, D), MXU_DTYPE), _sds((NCH, HEADS, HD, HD), F32)],
        scratch_shapes=[pltpu.VMEM((HEADS, HD, HD), F32)],
        compiler_params=_params(("arbitrary",)),
    )(proj, proj, proj, proj, cos, sin, dmask, xi, zeta, gain, ylru)


def ret_bwd(proj, states, dymix, dxg, tables, gain, name):
    cos, sin, dmask, xi, zeta = tables
    gch = _chunk_decay()
    scale = HD ** -0.5
    last = NCH - 1

    def body(q_ref, k_ref, v_ref, g_ref, st_ref, do_ref, cos_ref, sin_ref, dm_ref, xi_ref, zt_ref, gain_ref, dxg_ref,
             dp_ref, dgain_ref, ds_scr):
        @pl.when(pl.program_id(0) == 0)
        def _():
            ds_scr[...] = jnp.zeros_like(ds_scr)
            dgain_ref[...] = jnp.zeros_like(dgain_ref)

        dp_ref[:, :2 * D_LRU] = dxg_ref[...]
        cs, sn = cos_ref[...], sin_ref[...]
        for h in range(HEADS):
            sl = slice(HD * h, HD * (h + 1))
            oq, ok, ov, og = (slice(2 * D_LRU + j * D_RET + HD * h, 2 * D_LRU + j * D_RET + HD * (h + 1)) for j in range(4))
            qr = _rope(q_ref[:, sl], cs, sn)
            kr = _rope(k_ref[:, sl], cs, sn) * scale
            v = v_ref[:, sl]
            g = g_ref[:, sl]
            gain = gain_ref[:, sl]
            dm, x_i, zt = dm_ref[h], xi_ref[h], zt_ref[h]
            s = st_ref[h]
            ds = ds_scr[h]
            kz = kr * zt
            sc = _dot(qr, kr, NT) * dm
            y = _dot(sc, v, NN) + _dot(qr, s, NN) * x_i
            yc = y - jnp.mean(y, axis=-1, keepdims=True)
            rstd = lax.rsqrt(jnp.mean(yc * yc, axis=-1, keepdims=True) + EPS)
            yn = yc * rstd
            sg = jax.nn.sigmoid(g)
            silu = g * sg
            dout = do_ref[:, sl].astype(F32)
            dgain_ref[:, sl] += jnp.sum(dout * silu * yn, axis=0, keepdims=True)
            dp_ref[:, og] = (dout * yn * gain * (sg * (1.0 + g * (1.0 - sg)))).astype(dp_ref.dtype)
            dyn = dout * silu * gain
            dy = rstd * (dyn - jnp.mean(dyn, axis=-1, keepdims=True) - yn * jnp.mean(dyn * yn, axis=-1, keepdims=True))
            dp = _dot(dy, v, NT) * dm
            dv = _dot(sc, dy, TN) + _dot(kz, ds, NN)
            dqs = dy * x_i
            dqr = _dot(dp, kr, NN) + _dot(dqs, s, NT)
            dkr = _dot(dp, qr, TN) + _dot(v, ds, NT) * zt
            ds_scr[h] = gch[h] * ds + _dot(qr, dqs, TN)
            dp_ref[:, oq] = (dqr * cs + pltpu.roll(dqr * sn, HD // 2, 1)).astype(dp_ref.dtype)
            dp_ref[:, ok] = ((dkr * cs + pltpu.roll(dkr * sn, HD // 2, 1)) * scale).astype(dp_ref.dtype)
            dp_ref[:, ov] = dv.astype(dp_ref.dtype)

    def col(c):
        return pl.BlockSpec((CH, D_RET), lambda n: (last - n, c))

    tab = pl.BlockSpec((CH, HD), lambda n: (last - n, 0))
    cst = _full((HEADS, CH, HD))
    return pl.pallas_call(
        body, name=name, grid=(NCH,),
        in_specs=[col(2), col(3), col(4), col(5), pl.BlockSpec((None, HEADS, HD, HD), lambda n: (last - n, 0, 0, 0)), col(1),
                  tab, tab, cst, cst, cst, _full((1, D_RET)), pl.BlockSpec((CH, 2 * D_LRU), lambda n: (last - n, 0))],
        out_specs=[pl.BlockSpec((CH, D_IN), lambda n: (last - n, 0)), _full((1, D_RET))],
        out_shape=[_sds((T, D_IN), MXU_DTYPE), _sds((1, D_RET), F32)],
        scratch_shapes=[pltpu.VMEM((HEADS, HD, HD), F32)],
        compiler_params=_params(("arbitrary",)),
    )(proj, proj, proj, proj, states, dymix, cos, sin, dmask, xi, zeta, gain, dxg)


HBM = pl.BlockSpec(memory_space=pltpu.HBM)


def _place():
    return lax.axis_index("x"), lax.axis_index("y"), lax.axis_index("c")


def all_gather(arrs, name):
    n = len(arrs)

    def body(*refs):
        ins, outs = refs[:n], refs[n:2 * n]
        send_sems, recv_sems, local_sems = refs[2 * n:]
        x, y, c = _place()
        me, sibling = (x, y, c), (x, y, 1 - c)
        chips = [(1 - x, y), (x, 1 - y), (1 - x, 1 - y)]

        def copy(a, k, block, to, src=None):
            px, py, pc = block
            dst = outs[a].at[4 * px + 2 * py + pc]
            return pltpu.make_async_remote_copy(
                src_ref=dst if src is None else src, dst_ref=dst, send_sem=send_sems.at[a, k], recv_sem=recv_sems.at[a, k],
                device_id=to, device_id_type=MESH)

        mine = [pltpu.make_async_copy(ins[a], outs[a].at[4 * x + 2 * y + c], local_sems.at[a]) for a in range(n)]
        for cp in mine:
            cp.start()
        first = []
        for a in range(n):
            first.append(copy(a, 0, me, sibling, src=ins[a]))
            first += [copy(a, 1 + j, me, (*chip, c), src=ins[a]) for j, chip in enumerate(chips)]
        for cp in first:
            cp.start()
        passed = []
        for j, chip in enumerate(chips):
            for a in range(n):
                copy(a, 1 + j, (*chip, c), me).wait_recv()
                passed.append(copy(a, 4 + j, (*chip, c), sibling))
                passed[-1].start()
        for a in range(n):
            copy(a, 0, sibling, me).wait_recv()
            for j, chip in enumerate(chips):
                copy(a, 4 + j, (*chip, 1 - c), me).wait_recv()
        for cp in first + passed:
            cp.wait_send()
        for cp in mine:
            cp.wait()

    return pl.pallas_call(
        body, name=name,
        in_specs=[HBM] * n, out_specs=[HBM] * n,
        out_shape=[_sds((NDEV,) + a.shape, a.dtype) for a in arrs],
        scratch_shapes=[pltpu.SemaphoreType.DMA((n, 7)), pltpu.SemaphoreType.DMA((n, 7)), pltpu.SemaphoreType.DMA((n,))],
    )(*arrs)


def rs_sibling(arrs, name):
    n = len(arrs)

    def body(*refs):
        ins, outs = refs[:n], refs[n:2 * n]
        send_sems, recv_sems = refs[2 * n:]
        x, y, c = _place()
        sibling = (x, y, 1 - c)
        for a in range(n):
            for p in range(4):
                pltpu.make_async_remote_copy(
                    src_ref=ins[a].at[2 * p + 1 - c], dst_ref=outs[a].at[p], send_sem=send_sems.at[a],
                    recv_sem=recv_sems.at[a], device_id=sibling, device_id_type=MESH).start()
        for a in range(n):
            allfour = pltpu.make_async_remote_copy(
                src_ref=ins[a].at[pl.ds(0, 4)], dst_ref=outs[a], send_sem=send_sems.at[a], recv_sem=recv_sems.at[a],
                device_id=sibling, device_id_type=MESH)
            allfour.wait_send()
            allfour.wait_recv()

    return pl.pallas_call(
        body, name=name,
        in_specs=[HBM] * n, out_specs=[HBM] * n,
        out_shape=[_sds((4,) + a.shape[1:], a.dtype) for a in arrs],
        scratch_shapes=[pltpu.SemaphoreType.DMA((n,)), pltpu.SemaphoreType.DMA((n,))],
    )(*arrs)


def pair_sum(a, r, c, name):
    _, rr, cc = a.shape

    def body(c_ref, a_ref, r_ref, o_ref):
        del c_ref
        o_ref[...] = (a_ref[...].astype(F32) + r_ref[...].astype(F32)).astype(o_ref.dtype)

    return pl.pallas_call(
        body, name=name,
        grid_spec=pltpu.PrefetchScalarGridSpec(
            num_scalar_prefetch=1, grid=(4,),
            in_specs=[pl.BlockSpec((None, rr, cc), lambda p, c_ref: (2 * p + c_ref[0], 0, 0)),
                      pl.BlockSpec((None, rr, cc), lambda p, c_ref: (p, 0, 0))],
            out_specs=pl.BlockSpec((None, rr, cc), lambda p, c_ref: (p, 0, 0))),
        out_shape=_sds((4, rr, cc), a.dtype), compiler_params=_params(("parallel",)),
    )(c, a, r)


def rs_chips(parts, name):
    n = len(parts)

    def body(*refs):
        ins, outs = refs[:n], refs[n:2 * n]
        send_sems, recv_sems, local_sems = refs[2 * n:]
        x, y, c = _place()
        chips = [(1 - x, y), (x, 1 - y), (1 - x, 1 - y)]
        mine = [pltpu.make_async_copy(ins[a].at[2 * x + y], outs[a].at[3], local_sems.at[a]) for a in range(n)]
        for cp in mine:
            cp.start()
        sent = []
        for a in range(n):
            for k, (tx, ty) in enumerate(chips):
                sent.append(pltpu.make_async_remote_copy(
                    src_ref=ins[a].at[2 * tx + ty], dst_ref=outs[a].at[k], send_sem=send_sems.at[a, k],
                    recv_sem=recv_sems.at[a, k], device_id=(tx, ty, c), device_id_type=MESH))
                sent[-1].start()
        for cp in sent:
            cp.wait_send()
            cp.wait_recv()
        for cp in mine:
            cp.wait()

    return pl.pallas_call(
        body, name=name,
        in_specs=[HBM] * n, out_specs=[HBM] * n,
        out_shape=[_sds(a.shape, a.dtype) for a in parts],
        scratch_shapes=[pltpu.SemaphoreType.DMA((n, 3)), pltpu.SemaphoreType.DMA((n, 3)), pltpu.SemaphoreType.DMA((n,))],
    )(*parts)


def _adamw(w, g, m, v):
    m = ADAM_B1 * m + (1.0 - ADAM_B1) * g
    v = ADAM_B2 * v + (1.0 - ADAM_B2) * jnp.square(g)
    m_hat = m / (1.0 - ADAM_B1 ** ADAM_STEP)
    v_hat = v / (1.0 - ADAM_B2 ** ADAM_STEP)
    return -ADAM_LR * (m_hat / (jnp.sqrt(v_hat) + ADAM_EPS) + ADAM_WD * w), m, v


def adamw_big(r, w, m, v, tr, name):
    nl, rr, cc = w.shape

    def body(r_ref, w_ref, m_ref, v_ref, g_out, d_out, m_out, v_out):
        g = ((r_ref[3].astype(F32) + r_ref[0].astype(F32)) + r_ref[1].astype(F32)) + r_ref[2].astype(F32)
        g_out[...] = g
        d_out[...], m_out[...], v_out[...] = _adamw(w_ref[...], g, m_ref[...], v_ref[...])

    blk = pl.BlockSpec((None, tr, cc), lambda l, i: (l, i, 0))
    return pl.pallas_call(
        body, name=name, grid=(nl, rr // tr),
        in_specs=[pl.BlockSpec((None, 4, tr, cc), lambda l, i: (l, 0, i, 0)), blk, blk, blk],
        out_specs=[blk] * 4, out_shape=[_sds(w.shape, F32)] * 4, compiler_params=_params(("parallel", "parallel")),
    )(r, w, m, v)


def sum_devices(g, name):
    _, rr, cc = g.shape

    def body(g_ref, o_ref):
        acc = g_ref[0]
        for j in range(1, NDEV):
            acc = acc + g_ref[j]
        o_ref[...] = acc

    return pl.pallas_call(
        body, name=name, grid=(1,), in_specs=[_full(g.shape)], out_specs=_full((rr, cc)), out_shape=_sds((rr, cc), F32),
        compiler_params=_params(("arbitrary",)),
    )(g)


def adamw_rows(g, w, m, v, name):
    rr, cc = w.shape

    def body(g_ref, w_ref, m_ref, v_ref, d_out, m_out, v_out):
        d_out[...], m_out[...], v_out[...] = _adamw(w_ref[...], g_ref[...], m_ref[...], v_ref[...])

    blk = _full((rr, cc))
    return pl.pallas_call(
        body, name=name, grid=(1,), in_specs=[blk] * 4, out_specs=[blk] * 3, out_shape=[_sds((rr, cc), F32)] * 3,
        compiler_params=_params(("arbitrary",)),
    )(g, w, m, v)


def _block_diag(w):
    eye = jnp.eye(LRU_BLOCKS, dtype=w.dtype)
    return (w[:, :, None, :] * eye[:, None, :, None]).reshape(D_LRU, D_LRU)


def _diag_blocks(wd):
    w4 = wd.reshape(LRU_BLOCKS, LRU_BD, LRU_BLOCKS, LRU_BD)
    return jnp.stack([w4[g, :, g, :] for g in range(LRU_BLOCKS)])


def _pack(arrs):
    flat = jnp.concatenate([a.reshape(-1) for a in arrs])
    return flat.reshape(-1, 128)


def _unpack(packed, shapes):
    flat = packed.reshape(-1)
    out, o = [], 0
    for s in shapes:
        n = int(np.prod(s))
        out.append(flat[o:o + n].reshape(s))
        o += n
    return out


REP_NAMES = ["norm_mix", "conv_b", "gate_a_w", "gate_a_b", "gate_x_w", "gate_x_b", "lru_lambda", "lru_out_norm",
             "ret_out_norm", "norm_ffn", "norm_final"]


def kernel(x, meta_tokens, norm_mix, w_in, conv_w, conv_b, gate_a_w, gate_a_b, gate_x_w, gate_x_b, lru_lambda, lru_out_norm, ret_out_norm, w_out, norm_ffn, w_gate, w_up, w_down, norm_final, loss_target, m_meta_tokens, m_norm_mix, m_w_in, m_conv_w, m_conv_b, m_gate_a_w, m_gate_a_b, m_gate_x_w, m_gate_x_b, m_lru_lambda, m_lru_out_norm, m_ret_out_norm, m_w_out, m_norm_ffn, m_w_gate, m_w_up, m_w_down, m_norm_final, v_meta_tokens, v_norm_mix, v_w_in, v_conv_w, v_conv_b, v_gate_a_w, v_gate_a_b, v_gate_x_w, v_gate_x_b, v_lru_lambda, v_lru_out_norm, v_ret_out_norm, v_w_out, v_norm_ffn, v_w_gate, v_w_up, v_w_down, v_norm_final):
    xi, yi, ci = _place()
    dev = 4 * xi + 2 * yi + ci
    c_arr = jnp.reshape(ci, (1,)).astype(jnp.int32)

    ffpad = FF_SHP - FF_SH
    wire = dict(
        w_in=w_in.astype(WIRE_DTYPE),
        w_gate=jnp.pad(w_gate.astype(WIRE_DTYPE), ((0, 0), (0, 0), (0, ffpad))),
        w_up=jnp.pad(w_up.astype(WIRE_DTYPE), ((0, 0), (0, 0), (0, ffpad))),
        w_out=w_out.astype(WIRE_DTYPE),
        w_down=jnp.pad(w_down.astype(WIRE_DTYPE), ((0, 0), (0, ffpad), (0, 0))),
    )
    meta_g, conv_g = all_gather([meta_tokens, conv_w], "ag_small")
    meta_full = jnp.transpose(meta_g, (1, 0, 2)).reshape(N_META, D)
    conv_full = jnp.transpose(conv_g, (1, 2, 0, 3)).reshape(DEPTH, CONV_W, D_LRU)
    gathered = []
    for l in range(DEPTH):
        gi, gg, gu, go, gd = all_gather([wire[k][l] for k in ("w_in", "w_gate", "w_up", "w_out", "w_down")], f"ag_layer")
        gathered.append(dict(w_in=gi, w_gate=gg, w_up=gu, w_out=go.reshape(D, D), w_down=gd.reshape(D_FFP, D)))

    tables = _ret_tables()
    row = lambda a: a.reshape(1, -1)

    h = jnp.concatenate([jnp.zeros((PAD, D), F32), meta_full, x[0]], axis=0)
    saved = []
    for l in range(DEPTH):
        w = gathered[l]
        small = dict(cw=conv_full[l], cb=row(conv_b[l]), wa=_block_diag(gate_a_w[l]).astype(MXU_DTYPE), ba=row(gate_a_b[l]),
                     wx=_block_diag(gate_x_w[l]).astype(MXU_DTYPE), bx=row(gate_x_b[l]), lam=row(lru_lambda[l]),
                     gain=row(lru_out_norm[l]))
        hn1 = rmsnorm_fwd(h, row(norm_mix[l]), "rms_fwd")
        proj = mm_blocked_nn(hn1, w["w_in"], F32, "proj")
        ylru, hst = lru_fwd(proj, name="lru_fwd", **small)
        ymix, states = ret_fwd(proj, ylru, tables, row(ret_out_norm[l]), "ret_fwd")
        h_mid = mm_nn_res(ymix, w["w_out"], h, "out_proj")
        hn2 = rmsnorm_fwd(h_mid, row(norm_ffn[l]), "rms_fwd")
        gate, up, act = ffn_up(hn2, w["w_gate"], w["w_up"], "ffn_up")
        h_out = mm_nn_res(act, w["w_down"], h_mid, "ffn_down")
        saved.append(dict(h=h, hn1=hn1, proj=proj, hst=hst, states=states, ymix=ymix, h_mid=h_mid, hn2=hn2, gate=gate, up=up,
                          act=act, small=small))
        h = h_out

    loss_p, dh, g_norm_final = loss_head(h, row(norm_final), loss_target[0], "loss_head")
    loss = lax.psum(loss_p[0, 0], ("x", "y", "c"))

    rep = [None] * DEPTH
    convw_g = [None] * DEPTH
    big = [None] * DEPTH
    for l in reversed(range(DEPTH)):
        w, s = gathered[l], saved[l]
        dgate, dup = ffn_down_bwd(dh, w["w_down"], s["gate"], s["up"], "ffn_down_bwd")
        dwd = mm_tn(s["act"], dh, FF_SHP, "dw_down")
        dwg = mm_tn_blocked(s["hn2"], dgate, "dw_blocked")
        dwu = mm_tn_blocked(s["hn2"], dup, "dw_blocked")
        dhn2 = mm_blocked_nt([(dgate, w["w_gate"]), (dup, w["w_up"])], "ffn_up_bwd")
        dh_mid, g_norm_ffn = rmsnorm_bwd(s["h_mid"], row(norm_ffn[l]), dhn2, dh, "rms_bwd")
        dymix = mm_nt(dh_mid, w["w_out"], F32, "out_proj_bwd")
        dwo = mm_tn(s["ymix"], dh_mid, 512, "dw_out")
        dxg, lvec, dwa, dwx = lru_bwd(s["proj"], s["hst"], dymix, name="lru_bwd", **s["small"])
        dproj, g_ret_norm = ret_bwd(s["proj"], s["states"], dymix, dxg, tables, row(ret_out_norm[l]), "ret_bwd")
        dwi = mm_tn_blocked(s["hn1"], dproj, "dw_blocked")
        dhn1 = mm_blocked_nt([(dproj, w["w_in"])], "proj_bwd")
        dh, g_norm_mix = rmsnorm_bwd(s["h"], row(norm_mix[l]), dhn1, dh_mid, "rms_bwd")

        rep[l] = [g_norm_mix, lvec[4], _diag_blocks(dwa), lvec[5], _diag_blocks(dwx), lvec[6], lvec[7], lvec[8], g_ret_norm,
                  g_norm_ffn]
        convw_g[l] = lvec[0:CONV_W]
        parts = [dwi, dwg, dwu, dwo.reshape(NDEV, OUT_SH, D), dwd.reshape(NDEV, FF_SHP, D)]
        got = rs_sibling(parts, "rs_sibling")
        sums = [pair_sum(a, r, c_arr, "pair_sum") for a, r in zip(parts, got)]
        big[l] = rs_chips(sums, "rs_chips")

    grad_x = dh[X0:][None]
    g_meta = dh[PAD:X0]

    def finish(idx, wname, w_, m_, v_, tr, cut_rows=None, cut_cols=None):
        r = jnp.stack([big[l][idx] for l in range(DEPTH)])
        if cut_rows is not None:
            r = r[:, :, :cut_rows]
        if cut_cols is not None:
            r = r[..., :cut_cols]
        return adamw_big(r, w_, m_, v_, tr, "adamw_" + wname)

    o_in = finish(0, "w_in", w_in, m_w_in, v_w_in, 256)
    o_gate = finish(1, "w_gate", w_gate, m_w_gate, v_w_gate, 256, cut_cols=FF_SH)
    o_up = finish(2, "w_up", w_up, m_w_up, v_w_up, 256, cut_cols=FF_SH)
    o_out = finish(3, "w_out", w_out, m_w_out, v_w_out, OUT_SH)
    o_down = finish(4, "w_down", w_down, m_w_down, v_w_down, FF_SH, cut_rows=FF_SH)

    rep_shapes = [(D,), (D_LRU,), (LRU_BLOCKS, LRU_BD, LRU_BD), (LRU_BLOCKS, LRU_BD), (LRU_BLOCKS, LRU_BD, LRU_BD),
                  (LRU_BLOCKS, LRU_BD), (D_LRU,), (D_LRU,), (D_RET,), (D,)]
    flat = [a for l in range(DEPTH) for a in rep[l]] + [g_norm_final] + [convw_g[l] for l in range(DEPTH)] + [g_meta]
    (gath,) = all_gather([_pack(flat)], "ag_grads")
    gsum = sum_devices(gath, "sum_devices")
    shapes = rep_shapes * DEPTH + [(D,)] + [(CONV_W, D_LRU)] * DEPTH + [(N_META, D)]
    parts = _unpack(gsum, shapes)
    nrep = len(rep_shapes)
    g_rep = {n: jnp.stack([parts[l * nrep + i] for l in range(DEPTH)]) for i, n in enumerate(REP_NAMES[:-1])}
    g_rep["norm_final"] = parts[DEPTH * nrep]
    g_convw = lax.dynamic_slice_in_dim(jnp.stack(parts[DEPTH * nrep + 1:DEPTH * nrep + 1 + DEPTH]), dev * (D_LRU // NDEV),
                                       D_LRU // NDEV, axis=2)
    g_metatok = lax.dynamic_slice_in_dim(parts[-1], dev * (D // NDEV), D // NDEV, axis=1)

    given = dict(norm_mix=(norm_mix, m_norm_mix, v_norm_mix), conv_b=(conv_b, m_conv_b, v_conv_b),
                 gate_a_w=(gate_a_w, m_gate_a_w, v_gate_a_w), gate_a_b=(gate_a_b, m_gate_a_b, v_gate_a_b),
                 gate_x_w=(gate_x_w, m_gate_x_w, v_gate_x_w), gate_x_b=(gate_x_b, m_gate_x_b, v_gate_x_b),
                 lru_lambda=(lru_lambda, m_lru_lambda, v_lru_lambda), lru_out_norm=(lru_out_norm, m_lru_out_norm, v_lru_out_norm),
                 ret_out_norm=(ret_out_norm, m_ret_out_norm, v_ret_out_norm), norm_ffn=(norm_ffn, m_norm_ffn, v_norm_ffn),
                 norm_final=(norm_final, m_norm_final, v_norm_final),
                 conv_w=(conv_w, m_conv_w, v_conv_w), meta_tokens=(meta_tokens, m_meta_tokens, v_meta_tokens))
    small_names = REP_NAMES + ["conv_w", "meta_tokens"]
    small_g = dict(g_rep, conv_w=g_convw, meta_tokens=g_metatok)
    small_shapes = [given[n][0].shape for n in small_names]
    packs = [_pack([small_g[n] for n in small_names])] + [_pack([given[n][k] for n in small_names]) for k in range(3)]
    upd = adamw_rows(*packs, "adamw_small")
    small_out = [dict(zip(small_names, _unpack(p, small_shapes))) for p in upd]

    bigs = dict(w_in=o_in, w_out=o_out, w_gate=o_gate, w_up=o_up, w_down=o_down)
    order = ["meta_tokens", "norm_mix", "w_in", "conv_w", "conv_b", "gate_a_w", "gate_a_b", "gate_x_w", "gate_x_b", "lru_lambda",
             "lru_out_norm", "ret_out_norm", "w_out", "norm_ffn", "w_gate", "w_up", "w_down", "norm_final"]
    grads = [bigs[n][0] if n in bigs else small_g[n] for n in order]
    rest = [[bigs[n][k + 1] if n in bigs else small_out[k][n] for n in order] for k in range(3)]
    return (loss, grad_x, *grads, *rest[0], *rest[1], *rest[2])
```

```python
import functools

import numpy as np
import jax
import jax.numpy as jnp
from jax import lax
from jax.experimental import pallas as pl
from jax.experimental.pallas import tpu as pltpu

F32, BF16 = jnp.float32, jnp.bfloat16
MXU_DTYPE = BF16
WIRE_DTYPE = BF16

D = 1024
SEQ = 2048
DEPTH = 4
N_META = 16
CH = 128
PAD = (-(SEQ + N_META)) % CH
T = SEQ + N_META + PAD
NCH = T // CH
X0 = PAD + N_META
D_LRU = 512
LRU_BLOCKS = 8
LRU_BD = 64
CONV_W = 4
LRU_C = 8.0
D_RET = 512
HEADS = 4
HD = 128
ROPE_BASE = 10000.0
D_IN = 3072
D_FF = 2816
NDEV = 8
IN_SH = D_IN // NDEV
FF_SH = D_FF // NDEV
FF_SHP = 384
D_FFP = NDEV * FF_SHP
OUT_SH = D // NDEV
EPS = 1e-6
TM = 544
VMEM_LIMIT = 56 * 2**20
MESH = pl.DeviceIdType.MESH

ADAM_LR, ADAM_B1, ADAM_B2, ADAM_EPS, ADAM_WD, ADAM_STEP = 0.001, 0.9, 0.999, 1e-08, 0.01, 10

NN = ((1,), (0,))
NT = ((1,), (1,))
TN = ((0,), (0,))


def _dot(a, b, dims):
    return lax.dot_general(a.astype(MXU_DTYPE), b.astype(MXU_DTYPE), (dims, ((), ())), preferred_element_type=F32)


def _sds(shape, dtype):
    return jax.ShapeDtypeStruct(shape, dtype)


def _params(sem=None):
    return pltpu.CompilerParams(dimension_semantics=sem, vmem_limit_bytes=VMEM_LIMIT)


def _full(shape):
    n = len(shape)
    return pl.BlockSpec(shape, lambda *_: (0,) * n)


def rmsnorm_fwd(h, gain, name):
    def body(h_ref, g_ref, o_ref):
        x = h_ref[...]
        ms = jnp.mean(x * x, axis=-1, keepdims=True)
        o_ref[...] = (x * lax.rsqrt(ms + EPS) * g_ref[...]).astype(o_ref.dtype)

    return pl.pallas_call(
        body, name=name, grid=(T // TM,),
        in_specs=[pl.BlockSpec((TM, D), lambda i: (i, 0)), _full((1, D))],
        out_specs=pl.BlockSpec((TM, D), lambda i: (i, 0)),
        out_shape=_sds((T, D), MXU_DTYPE), compiler_params=_params(("parallel",)),
    )(h, gain)


def rmsnorm_bwd(h, gain, dhn, dres, name):
    def body(h_ref, g_ref, dhn_ref, dres_ref, dh_ref, dg_ref):
        x = h_ref[...]
        rstd = lax.rsqrt(jnp.mean(x * x, axis=-1, keepdims=True) + EPS)
        xhat = x * rstd
        dy = dhn_ref[...]
        dyg = dy * g_ref[...]
        dh_ref[...] = dres_ref[...] + rstd * (dyg - xhat * jnp.mean(dyg * xhat, axis=-1, keepdims=True))

        @pl.when(pl.program_id(0) == 0)
        def _():
            dg_ref[...] = jnp.zeros_like(dg_ref)
        dg_ref[...] += jnp.sum(dy * xhat, axis=0, keepdims=True)

    row = pl.BlockSpec((TM, D), lambda i: (i, 0))
    return pl.pallas_call(
        body, name=name, grid=(T // TM,),
        in_specs=[row, _full((1, D)), row, row],
        out_specs=[row, _full((1, D))],
        out_shape=[_sds((T, D), F32), _sds((1, D), F32)], compiler_params=_params(("arbitrary",)),
    )(h, gain, dhn, dres)


def loss_head(h, gain, target, name):
    def body(h_ref, g_ref, t_ref, loss_ref, dh_ref, dg_ref):
        i = pl.program_id(0)

        @pl.when(i == 0)
        def _():
            loss_ref[...] = jnp.zeros_like(loss_ref)
            dg_ref[...] = jnp.zeros_like(dg_ref)
            dh_ref[...] = jnp.zeros_like(dh_ref)

        @pl.when(i > 0)
        def _():
            x = h_ref[...]
            g = g_ref[...]
            rstd = lax.rsqrt(jnp.mean(x * x, axis=-1, keepdims=True) + EPS)
            xhat = x * rstd
            err = xhat * g - t_ref[...]
            loss_ref[...] += 0.5 * jnp.sum(jnp.mean(err * err, axis=-1, keepdims=True), axis=0, keepdims=True)
            dy = err * (1.0 / D)
            dyg = dy * g
            dh_ref[...] = rstd * (dyg - xhat * jnp.mean(dyg * xhat, axis=-1, keepdims=True))
            dg_ref[...] += jnp.sum(dy * xhat, axis=0, keepdims=True)

    row = pl.BlockSpec((CH, D), lambda i: (i, 0))
    return pl.pallas_call(
        body, name=name, grid=(NCH,),
        in_specs=[row, _full((1, D)), pl.BlockSpec((CH, D), lambda i: (jnp.maximum(i - 1, 0), 0))],
        out_specs=[_full((8, 128)), row, _full((1, D))],
        out_shape=[_sds((8, 128), F32), _sds((T, D), F32), _sds((1, D), F32)],
        compiler_params=_params(("arbitrary",)),
    )(h, gain, target)


def mm_blocked_nn(a, w, out_dtype, name):
    k = a.shape[1]

    def body(a_ref, w_ref, o_ref):
        o_ref[...] = _dot(a_ref[...], w_ref[...], NN).astype(o_ref.dtype)

    return pl.pallas_call(
        body, name=name, grid=(NDEV, T // TM),
        in_specs=[pl.BlockSpec((TM, k), lambda j, i: (i, 0)), pl.BlockSpec((None, k, IN_SH), lambda j, i: (j, 0, 0))],
        out_specs=pl.BlockSpec((TM, IN_SH), lambda j, i: (i, j)),
        out_shape=_sds((T, NDEV * IN_SH), out_dtype), compiler_params=_params(("parallel", "parallel")),
    )(a, w)


def mm_nn_res(a, w, res, after, name):
    k = a.shape[1]
    bn = 512

    def body(a_ref, w_ref, r_ref, after_ref, o_ref):
        del after_ref
        o_ref[...] = r_ref[...] + _dot(a_ref[...], w_ref[...], NN)

    return pl.pallas_call(
        body, name=name, grid=(D // bn, T // TM),
        in_specs=[pl.BlockSpec((TM, k), lambda j, i: (i, 0)), pl.BlockSpec((k, bn), lambda j, i: (0, j)),
                  pl.BlockSpec((TM, bn), lambda j, i: (i, j)), pl.BlockSpec(memory_space=pl.ANY)],
        out_specs=pl.BlockSpec((TM, bn), lambda j, i: (i, j)),
        out_shape=_sds((T, D), F32), compiler_params=_params(("parallel", "parallel")),
    )(a, w, res, after)


def ffn_up(hn, wg, wu, name):
    def body(a_ref, wg_ref, wu_ref, g_ref, u_ref, act_ref):
        a = a_ref[...]
        g = _dot(a, wg_ref[...], NN)
        u = _dot(a, wu_ref[...], NN)
        g_ref[...] = g.astype(g_ref.dtype)
        u_ref[...] = u.astype(u_ref.dtype)
        act_ref[...] = (jax.nn.silu(g) * u).astype(act_ref.dtype)

    wspec = pl.BlockSpec((None, D, FF_SHP), lambda j, i: (j, 0, 0))
    ospec = pl.BlockSpec((TM, FF_SHP), lambda j, i: (i, j))
    return pl.pallas_call(
        body, name=name, grid=(NDEV, T // TM),
        in_specs=[pl.BlockSpec((TM, D), lambda j, i: (i, 0)), wspec, wspec],
        out_specs=[ospec, ospec, ospec],
        out_shape=[_sds((T, D_FFP), MXU_DTYPE)] * 3, compiler_params=_params(("parallel", "parallel")),
    )(hn, wg, wu)


def ffn_down_bwd(dh, wd, gate, up, name):
    def body(dh_ref, wd_ref, g_ref, u_ref, dg_ref, du_ref):
        dact = _dot(dh_ref[...], wd_ref[...], NT)
        g = g_ref[...].astype(F32)
        u = u_ref[...].astype(F32)
        sg = jax.nn.sigmoid(g)
        dg_ref[...] = (dact * u * (sg * (1.0 + g * (1.0 - sg)))).astype(dg_ref.dtype)
        du_ref[...] = (dact * (g * sg)).astype(du_ref.dtype)

    blk = pl.BlockSpec((TM, FF_SHP), lambda j, i: (i, j))
    return pl.pallas_call(
        body, name=name, grid=(NDEV, T // TM),
        in_specs=[pl.BlockSpec((TM, D), lambda j, i: (i, 0)), pl.BlockSpec((FF_SHP, D), lambda j, i: (j, 0)), blk, blk],
        out_specs=[blk, blk],
        out_shape=[_sds((T, D_FFP), MXU_DTYPE)] * 2, compiler_params=_params(("parallel", "parallel")),
    )(dh, wd, gate, up)


def mm_nt(a, w, out_dtype, name):
    n = a.shape[1]
    bk = 512

    def body(a_ref, w_ref, o_ref):
        o_ref[...] = _dot(a_ref[...], w_ref[...], NT).astype(o_ref.dtype)

    return pl.pallas_call(
        body, name=name, grid=(D // bk, T // TM),
        in_specs=[pl.BlockSpec((TM, n), lambda j, i: (i, 0)), pl.BlockSpec((bk, n), lambda j, i: (j, 0))],
        out_specs=pl.BlockSpec((TM, bk), lambda j, i: (i, j)),
        out_shape=_sds((T, D), out_dtype), compiler_params=_params(("parallel", "parallel")),
    )(a, w)


def mm_blocked_nt(pairs, name):
    n = len(pairs)

    def body(*refs):
        o_ref = refs[2 * n]

        @pl.when(pl.program_id(1) == 0)
        def _():
            o_ref[...] = jnp.zeros_like(o_ref)
        acc = _dot(refs[0][...], refs[1][...], NT)
        for p in range(1, n):
            acc += _dot(refs[2 * p][...], refs[2 * p + 1][...], NT)
        o_ref[...] += acc

    specs, args = [], []
    for a, w in pairs:
        specs += [pl.BlockSpec((TM, IN_SH), lambda i, j: (i, j)), pl.BlockSpec((None, D, IN_SH), lambda i, j: (j, 0, 0))]
        args += [a, w]
    return pl.pallas_call(
        body, name=name, grid=(T // TM, NDEV), in_specs=specs,
        out_specs=pl.BlockSpec((TM, D), lambda i, j: (i, 0)),
        out_shape=_sds((T, D), F32), compiler_params=_params(("parallel", "arbitrary")),
    )(*args)


def mm_tn_blocked(a, b, name):
    def body(a_ref, b_ref, o_ref):
        o_ref[...] = _dot(a_ref[...], b_ref[...], TN).astype(o_ref.dtype)

    return pl.pallas_call(
        body, name=name, grid=(NDEV,),
        in_specs=[_full((T, D)), pl.BlockSpec((T, IN_SH), lambda j: (0, j))],
        out_specs=pl.BlockSpec((None, D, IN_SH), lambda j: (j, 0, 0)),
        out_shape=_sds((NDEV, D, IN_SH), WIRE_DTYPE), compiler_params=_params(("parallel",)),
    )(a, b)


def mm_tn(a, b, bm, name):
    m = a.shape[1]
    bn = 512

    def body(a_ref, b_ref, o_ref):
        o_ref[...] = _dot(a_ref[...], b_ref[...], TN).astype(o_ref.dtype)

    return pl.pallas_call(
        body, name=name, grid=(m // bm, D // bn),
        in_specs=[pl.BlockSpec((T, bm), lambda i, j: (0, i)), pl.BlockSpec((T, bn), lambda i, j: (0, j))],
        out_specs=pl.BlockSpec((bm, bn), lambda i, j: (i, j)),
        out_shape=_sds((m, D), WIRE_DTYPE), compiler_params=_params(("parallel", "parallel")),
    )(a, b)


def _softplus_neg(lam):
    return jnp.maximum(-lam, 0.0) + jnp.log1p(jnp.exp(-jnp.abs(lam)))


def _lru_gates(pa, px, xc, lam):
    r = jax.nn.sigmoid(pa)
    ig = jax.nn.sigmoid(px)
    log_a = -LRU_C * r * _softplus_neg(lam)
    a = jnp.exp(log_a)
    mult = jnp.sqrt(-jnp.tanh(log_a) * (jnp.exp(2.0 * log_a) + 1.0))
    return a, mult * (ig * xc)


def _lru_out(h, g, gain):
    z = h * jax.nn.gelu(g)
    return z * lax.rsqrt(jnp.mean(z * z, axis=-1, keepdims=True) + EPS) * gain


def _conv_taps(x, xprev, row):
    taps = [x]
    for s in range(1, CONV_W):
        taps.append(jnp.where(row < s, pltpu.roll(xprev, s, 0), pltpu.roll(x, s, 0)))
    return taps


def _conv(taps, cw_ref, cb):
    xc = cb + cw_ref[CONV_W - 1:CONV_W, :] * taps[0]
    for s in range(1, CONV_W):
        xc = xc + cw_ref[CONV_W - 1 - s:CONV_W - s, :] * taps[s]
    return xc


def lru_fwd(proj, cw, cb, wa, ba, wx, bx, lam, gain, name):
    def body(x_ref, g_ref, cw_ref, cb_ref, wa_ref, ba_ref, wx_ref, bx_ref, lam_ref, gain_ref,
             y_ref, h_ref, xprev_scr, a_scr, b_scr, carry_scr):
        i = pl.program_id(0)

        @pl.when(i == 0)
        def _():
            xprev_scr[...] = jnp.zeros_like(xprev_scr)
            carry_scr[...] = jnp.zeros_like(carry_scr)

        x = x_ref[...]
        row = lax.broadcasted_iota(jnp.int32, (CH, D_LRU), 0)
        xc = _conv(_conv_taps(x, xprev_scr[...], row), cw_ref, cb_ref[...])
        pa = _dot(xc, wa_ref[...], NN) + ba_ref[...]
        px = _dot(xc, wx_ref[...], NN) + bx_ref[...]
        a, b = _lru_gates(pa, px, xc, lam_ref[...])
        a_scr[...] = a
        b_scr[...] = jnp.where(i * CH + row >= PAD, b, 0.0)
        h = carry_scr[...]
        for t in range(CH):
            h = a_scr[t:t + 1, :] * h + b_scr[t:t + 1, :]
            h_ref[t:t + 1, :] = h
        carry_scr[...] = h
        xprev_scr[...] = x
        y_ref[...] = _lru_out(h_ref[...], g_ref[...], gain_ref[...]).astype(y_ref.dtype)

    vec = _full((1, D_LRU))
    mat = _full((D_LRU, D_LRU))
    return pl.pallas_call(
        body, name=name, grid=(NCH,),
        in_specs=[pl.BlockSpec((CH, D_LRU), lambda i: (i, 0)), pl.BlockSpec((CH, D_LRU), lambda i: (i, 1)),
                  _full((CONV_W, D_LRU)), vec, mat, vec, mat, vec, vec, vec],
        out_specs=[pl.BlockSpec((CH, D_LRU), lambda i: (i, 0)), pl.BlockSpec((CH, D_LRU), lambda i: (i, 0))],
        out_shape=[_sds((T, D_LRU), MXU_DTYPE), _sds((T, D_LRU), F32)],
        scratch_shapes=[pltpu.VMEM((CH, D_LRU), F32), pltpu.VMEM((CH, D_LRU), F32), pltpu.VMEM((CH, D_LRU), F32),
                        pltpu.VMEM((1, D_LRU), F32)],
        compiler_params=_params(("arbitrary",)),
    )(proj, proj, cw, cb, wa, ba, wx, bx, lam, gain)


LRU_VEC_ROWS = 16


def lru_bwd(proj, hst, dymix, cw, cb, wa, ba, wx, bx, lam, gain, name):
    last = NCH - 1

    def body(x_ref, xp_ref, g_ref, h_ref, hp_ref, dy_ref, cw_ref, cb_ref, wa_ref, ba_ref, wx_ref, bx_ref, lam_ref,
             gain_ref, dxg_ref, vec_ref, dwa_ref, dwx_ref, a_scr, dh_scr, g_scr, carry_scr, dxcn_scr):
        i = pl.program_id(0)
        ib = last - i

        @pl.when(i == 0)
        def _():
            carry_scr[...] = jnp.zeros_like(carry_scr)
            dxcn_scr[...] = jnp.zeros_like(dxcn_scr)
            vec_ref[...] = jnp.zeros_like(vec_ref)
            dwa_ref[...] = jnp.zeros_like(dwa_ref)
            dwx_ref[...] = jnp.zeros_like(dwx_ref)

        x = x_ref[...]
        row = lax.broadcasted_iota(jnp.int32, (CH, D_LRU), 0)
        valid = ib * CH + row >= PAD
        taps = _conv_taps(x, xp_ref[...], row)
        xc = _conv(taps, cw_ref, cb_ref[...])
        pa = _dot(xc, wa_ref[...], NN) + ba_ref[...]
        px = _dot(xc, wx_ref[...], NN) + bx_ref[...]
        (a, _), vjp_gates = jax.vjp(_lru_gates, pa, px, xc, lam_ref[...])
        h = h_ref[...]
        _, vjp_out = jax.vjp(_lru_out, h, g_ref[...], gain_ref[...])
        dh, dg, dgain = vjp_out(dy_ref[...].astype(F32))
        a_scr[...] = a
        dh_scr[...] = dh
        c = carry_scr[...]
        for t in range(CH - 1, -1, -1):
            gt = dh_scr[t:t + 1, :] + c
            g_scr[t:t + 1, :] = gt
            c = a_scr[t:t + 1, :] * gt
        carry_scr[...] = c
        gg = g_scr[...]
        hprev = jnp.where(row < 1, pltpu.roll(hp_ref[...], 1, 0), pltpu.roll(h, 1, 0))
        da = jnp.where(valid, gg * hprev, 0.0)
        db = jnp.where(valid, gg, 0.0)
        dpa, dpx, dxc, dlam = vjp_gates((da, db))
        dxc = dxc + _dot(dpa, wa_ref[...], NT) + _dot(dpx, wx_ref[...], NT)
        dwa_ref[...] += _dot(xc, dpa, TN)
        dwx_ref[...] += _dot(xc, dpx, TN)
        for s in range(CONV_W):
            vec_ref[CONV_W - 1 - s:CONV_W - s, :] += jnp.sum(dxc * taps[s], axis=0, keepdims=True)
        vec_ref[4:5, :] += jnp.sum(dxc, axis=0, keepdims=True)
        vec_ref[5:6, :] += jnp.sum(dpa, axis=0, keepdims=True)
        vec_ref[6:7, :] += jnp.sum(dpx, axis=0, keepdims=True)
        vec_ref[7:8, :] += dlam
        vec_ref[8:9, :] += dgain
        dxn = dxcn_scr[...]
        dx = cw_ref[CONV_W - 1:CONV_W, :] * dxc
        for s in range(1, CONV_W):
            ahead = jnp.where(row >= CH - s, pltpu.roll(dxn, CH - s, 0), pltpu.roll(dxc, CH - s, 0))
            dx = dx + cw_ref[CONV_W - 1 - s:CONV_W - s, :] * ahead
        dxcn_scr[...] = dxc
        dxg_ref[:, :D_LRU] = jnp.where(valid, dx, 0.0).astype(dxg_ref.dtype)
        dxg_ref[:, D_LRU:] = dg.astype(dxg_ref.dtype)

    vec = _full((1, D_LRU))
    mat = _full((D_LRU, D_LRU))

    def blk(col, shift=0):
        return pl.BlockSpec((CH, D_LRU), lambda i: (jnp.maximum(last - i - shift, 0), col))

    return pl.pallas_call(
        body, name=name, grid=(NCH,),
        in_specs=[blk(0), blk(0, 1), blk(1), blk(0), blk(0, 1), blk(0),
                  _full((CONV_W, D_LRU)), vec, mat, vec, mat, vec, vec, vec],
        out_specs=[pl.BlockSpec((CH, 2 * D_LRU), lambda i: (last - i, 0)), _full((LRU_VEC_ROWS, D_LRU)), mat, mat],
        out_shape=[_sds((T, 2 * D_LRU), MXU_DTYPE), _sds((LRU_VEC_ROWS, D_LRU), F32),
                   _sds((D_LRU, D_LRU), F32), _sds((D_LRU, D_LRU), F32)],
        scratch_shapes=[pltpu.VMEM((CH, D_LRU), F32), pltpu.VMEM((CH, D_LRU), F32), pltpu.VMEM((CH, D_LRU), F32),
                        pltpu.VMEM((1, D_LRU), F32), pltpu.VMEM((CH, D_LRU), F32)],
        compiler_params=_params(("arbitrary",)),
    )(proj, proj, proj, hst, hst, dymix, cw, cb, wa, ba, wx, bx, lam, gain)


def _ret_tables():
    half = HD // 2
    pos = jnp.arange(T, dtype=F32) - float(PAD)
    inv = ROPE_BASE ** (-jnp.arange(half, dtype=F32) / half)
    ang = pos[:, None] * inv[None, :]
    cos = jnp.concatenate([jnp.cos(ang), jnp.cos(ang)], axis=-1)
    sin = jnp.concatenate([-jnp.sin(ang), jnp.sin(ang)], axis=-1)
    log_g = jnp.log(1.0 - 2.0 ** (-5.0 - jnp.arange(HEADS, dtype=F32)))
    idx = jnp.arange(CH, dtype=F32)
    diff = idx[:, None] - idx[None, :]
    dmask = jnp.where(diff[None] >= 0, jnp.exp(jnp.maximum(diff, 0.0)[None] * log_g[:, None, None]), 0.0)
    xi = jnp.exp((idx + 1.0)[None, :] * log_g[:, None])
    zeta = jnp.exp((CH - 1.0 - idx)[None, :] * log_g[:, None])
    xi = jnp.broadcast_to(xi[:, :, None], (HEADS, CH, HD))
    zeta = jnp.broadcast_to(zeta[:, :, None], (HEADS, CH, HD))
    return cos, sin, dmask, xi, zeta


def _chunk_decay():
    log_g = np.log(np.float32(1.0) - np.float32(2.0) ** (np.float32(-5.0) - np.arange(HEADS, dtype=np.float32)))
    return [float(v) for v in np.exp(np.float32(CH) * log_g.astype(np.float32))]


def _rope(x, cos, sin):
    return x * cos + pltpu.roll(x, HD // 2, 1) * sin


def ret_fwd(proj, ylru, tables, gain, name):
    cos, sin, dmask, xi, zeta = tables
    gch = _chunk_decay()
    scale = HD ** -0.5

    def body(q_ref, k_ref, v_ref, g_ref, cos_ref, sin_ref, dm_ref, xi_ref, zt_ref, gain_ref, ylru_ref,
             y_ref, st_ref, s_scr):
        @pl.when(pl.program_id(0) == 0)
        def _():
            s_scr[...] = jnp.zeros_like(s_scr)

        y_ref[:, :D_LRU] = ylru_ref[...]
        cs, sn = cos_ref[...], sin_ref[...]
        for h in range(HEADS):
            sl = slice(HD * h, HD * (h + 1))
            so = slice(D_LRU + HD * h, D_LRU + HD * (h + 1))
            qr = _rope(q_ref[:, sl], cs, sn)
            kr = _rope(k_ref[:, sl], cs, sn) * scale
            v = v_ref[:, sl]
            s = s_scr[h]
            st_ref[h] = s
            sc = _dot(qr, kr, NT) * dm_ref[h]
            y = _dot(sc, v, NN) + _dot(qr, s, NN) * xi_ref[h]
            s_scr[h] = s * gch[h] + _dot(kr * zt_ref[h], v, TN)
            yc = y - jnp.mean(y, axis=-1, keepdims=True)
            yn = yc * lax.rsqrt(jnp.mean(yc * yc, axis=-1, keepdims=True) + EPS)
            y_ref[:, so] = (jax.nn.silu(g_ref[:, sl]) * (yn * gain_ref[:, sl])).astype(y_ref.dtype)

    def col(c):
        return pl.BlockSpec((CH, D_RET), lambda n: (n, c))

    tab = pl.BlockSpec((CH, HD), lambda n: (n, 0))
    cst = _full((HEADS, CH, HD))
    return pl.pallas_call(
        body, name=name, grid=(NCH,),
        in_specs=[col(2), col(3), col(4), col(5), tab, tab, cst, cst, cst, _full((1, D_RET)), col(0)],
        out_specs=[pl.BlockSpec((CH, D), lambda n: (n, 0)), pl.BlockSpec((None, HEADS, HD, HD), lambda n: (n, 0, 0, 0))],
        out_shape=[_sds((T, D), MXU_DTYPE), _sds((NCH, HEADS, HD, HD), F32)],
        scratch_shapes=[pltpu.VMEM((HEADS, HD, HD), F32)],
        compiler_params=_params(("arbitrary",)),
    )(proj, proj, proj, proj, cos, sin, dmask, xi, zeta, gain, ylru)


def ret_bwd(proj, states, dymix, dxg, tables, gain, name):
    cos, sin, dmask, xi, zeta = tables
    gch = _chunk_decay()
    scale = HD ** -0.5
    last = NCH - 1

    def body(q_ref, k_ref, v_ref, g_ref, st_ref, do_ref, cos_ref, sin_ref, dm_ref, xi_ref, zt_ref, gain_ref, dxg_ref,
             dp_ref, dgain_ref, ds_scr):
        @pl.when(pl.program_id(0) == 0)
        def _():
            ds_scr[...] = jnp.zeros_like(ds_scr)
            dgain_ref[...] = jnp.zeros_like(dgain_ref)

        dp_ref[:, :2 * D_LRU] = dxg_ref[...]
        cs, sn = cos_ref[...], sin_ref[...]
        for h in range(HEADS):
            sl = slice(HD * h, HD * (h + 1))
            oq, ok, ov, og = (slice(2 * D_LRU + j * D_RET + HD * h, 2 * D_LRU + j * D_RET + HD * (h + 1)) for j in range(4))
            qr = _rope(q_ref[:, sl], cs, sn)
            kr = _rope(k_ref[:, sl], cs, sn) * scale
            v = v_ref[:, sl]
            g = g_ref[:, sl]
            gain = gain_ref[:, sl]
            dm, x_i, zt = dm_ref[h], xi_ref[h], zt_ref[h]
            s = st_ref[h]
            ds = ds_scr[h]
            kz = kr * zt
            sc = _dot(qr, kr, NT) * dm
            y = _dot(sc, v, NN) + _dot(qr, s, NN) * x_i
            yc = y - jnp.mean(y, axis=-1, keepdims=True)
            rstd = lax.rsqrt(jnp.mean(yc * yc, axis=-1, keepdims=True) + EPS)
            yn = yc * rstd
            sg = jax.nn.sigmoid(g)
            silu = g * sg
            dout = do_ref[:, sl].astype(F32)
            dgain_ref[:, sl] += jnp.sum(dout * silu * yn, axis=0, keepdims=True)
            dp_ref[:, og] = (dout * yn * gain * (sg * (1.0 + g * (1.0 - sg)))).astype(dp_ref.dtype)
            dyn = dout * silu * gain
            dy = rstd * (dyn - jnp.mean(dyn, axis=-1, keepdims=True) - yn * jnp.mean(dyn * yn, axis=-1, keepdims=True))
            dp = _dot(dy, v, NT) * dm
            dv = _dot(sc, dy, TN) + _dot(kz, ds, NN)
            dqs = dy * x_i
            dqr = _dot(dp, kr, NN) + _dot(dqs, s, NT)
            dkr = _dot(dp, qr, TN) + _dot(v, ds, NT) * zt
            ds_scr[h] = gch[h] * ds + _dot(qr, dqs, TN)
            dp_ref[:, oq] = (dqr * cs + pltpu.roll(dqr * sn, HD // 2, 1)).astype(dp_ref.dtype)
            dp_ref[:, ok] = ((dkr * cs + pltpu.roll(dkr * sn, HD // 2, 1)) * scale).astype(dp_ref.dtype)
            dp_ref[:, ov] = dv.astype(dp_ref.dtype)

    def col(c):
        return pl.BlockSpec((CH, D_RET), lambda n: (last - n, c))

    tab = pl.BlockSpec((CH, HD), lambda n: (last - n, 0))
    cst = _full((HEADS, CH, HD))
    return pl.pallas_call(
        body, name=name, grid=(NCH,),
        in_specs=[col(2), col(3), col(4), col(5), pl.BlockSpec((None, HEADS, HD, HD), lambda n: (last - n, 0, 0, 0)), col(1),
                  tab, tab, cst, cst, cst, _full((1, D_RET)), pl.BlockSpec((CH, 2 * D_LRU), lambda n: (last - n, 0))],
        out_specs=[pl.BlockSpec((CH, D_IN), lambda n: (last - n, 0)), _full((1, D_RET))],
        out_shape=[_sds((T, D_IN), MXU_DTYPE), _sds((1, D_RET), F32)],
        scratch_shapes=[pltpu.VMEM((HEADS, HD, HD), F32)],
        compiler_params=_params(("arbitrary",)),
    )(proj, proj, proj, proj, states, dymix, cos, sin, dmask, xi, zeta, gain, dxg)


HBM = pl.BlockSpec(memory_space=pltpu.HBM)


def _place():
    return lax.axis_index("x"), lax.axis_index("y"), lax.axis_index("c")


def all_gather(arrs, name):
    n = len(arrs)

    def body(*refs):
        ins, outs = refs[:n], refs[n:2 * n]
        send_sems, recv_sems, local_sems = refs[2 * n:]
        x, y, c = _place()
        me, sibling = (x, y, c), (x, y, 1 - c)
        chips = [(1 - x, y), (x, 1 - y), (1 - x, 1 - y)]

        def copy(a, k, block, to, src=None):
            px, py, pc = block
            dst = outs[a].at[4 * px + 2 * py + pc]
            return pltpu.make_async_remote_copy(
                src_ref=dst if src is None else src, dst_ref=dst, send_sem=send_sems.at[a, k], recv_sem=recv_sems.at[a, k],
                device_id=to, device_id_type=MESH)

        mine = [pltpu.make_async_copy(ins[a], outs[a].at[4 * x + 2 * y + c], local_sems.at[a]) for a in range(n)]
        for cp in mine:
            cp.start()
        first = []
        for a in range(n):
            first.append(copy(a, 0, me, sibling, src=ins[a]))
            first += [copy(a, 1 + j, me, (*chip, c), src=ins[a]) for j, chip in enumerate(chips)]
        for cp in first:
            cp.start()
        passed = []
        for j, chip in enumerate(chips):
            for a in range(n):
                copy(a, 1 + j, (*chip, c), me).wait_recv()
                passed.append(copy(a, 4 + j, (*chip, c), sibling))
                passed[-1].start()
        for a in range(n):
            copy(a, 0, sibling, me).wait_recv()
            for j, chip in enumerate(chips):
                copy(a, 4 + j, (*chip, 1 - c), me).wait_recv()
        for cp in first + passed:
            cp.wait_send()
        for cp in mine:
            cp.wait()

    return pl.pallas_call(
        body, name=name,
        in_specs=[HBM] * n, out_specs=[HBM] * n,
        out_shape=[_sds((NDEV,) + a.shape, a.dtype) for a in arrs],
        scratch_shapes=[pltpu.SemaphoreType.DMA((n, 7)), pltpu.SemaphoreType.DMA((n, 7)), pltpu.SemaphoreType.DMA((n,))],
    )(*arrs)


SEM = pl.BlockSpec(memory_space=pltpu.SEMAPHORE)
ANY = pl.BlockSpec(memory_space=pl.ANY)
EFFECT = pltpu.SideEffectType.DATAFLOW_SIDE_EFFECTING


def _hbm(a):
    return pltpu.with_memory_space_constraint(a, pltpu.HBM)


def _hbm_like(arrs):
    return [pltpu.HBM(a.shape, a.dtype) for a in arrs]


def _dma_sems(count):
    return [pltpu.SemaphoreType.DMA(())] * count


def _ag_copy(lands, send_sems, recv_sems, per):
    def copy(a, k, block, to, src=None):
        px, py, pc = block
        dst = lands[a].at[4 * px + 2 * py + pc]
        return pltpu.make_async_remote_copy(
            src_ref=dst if src is None else src, dst_ref=dst, send_sem=send_sems[a * per + k], recv_sem=recv_sems[a * per + k],
            device_id=to, device_id_type=MESH)
    return copy


def ag_place(arrs, name):
    n = len(arrs)

    def body(*refs):
        ins, outs, sems = refs[:n], refs[n:2 * n], refs[2 * n]
        x, y, c = _place()
        mine = [pltpu.make_async_copy(ins[a], outs[a].at[4 * x + 2 * y + c], sems.at[a]) for a in range(n)]
        for cp in mine:
            cp.start()
        for cp in mine:
            cp.wait()

    return pl.pallas_call(
        body, name=name, in_specs=[HBM] * n, out_specs=[HBM] * n,
        out_shape=[_sds((NDEV,) + a.shape, a.dtype) for a in arrs],
        scratch_shapes=[pltpu.SemaphoreType.DMA((n,))],
    )(*arrs)


def ag_start(arrs, lands, after, name):
    n = len(arrs)
    ns = 4 * n

    def body(*refs):
        srcs, lnd = refs[:n], refs[n:2 * n]
        send_sems, recv_sems = refs[2 * n + 1:2 * n + 1 + ns], refs[2 * n + 1 + ns:2 * n + 1 + 2 * ns]
        token = refs[-1]
        x, y, c = _place()
        me, sibling = (x, y, c), (x, y, 1 - c)
        chips = [(1 - x, y), (x, 1 - y), (1 - x, 1 - y)]
        copy = _ag_copy(lnd, send_sems, recv_sems, 4)
        for a in range(n):
            copy(a, 0, me, sibling, src=srcs[a]).start()
            for j, chip in enumerate(chips):
                copy(a, 1 + j, me, (*chip, c), src=srcs[a]).start()
        token[...] = jnp.zeros_like(token)

    outs = pl.pallas_call(
        body, name=name,
        in_specs=[HBM] * (2 * n) + [ANY],
        out_specs=[SEM] * (2 * ns) + [HBM] * (2 * n) + [pl.BlockSpec(memory_space=pltpu.VMEM)],
        out_shape=_dma_sems(2 * ns) + _hbm_like(arrs) + _hbm_like(lands) + [_sds((8, 128), F32)],
        input_output_aliases={i: 2 * ns + i for i in range(2 * n)},
        compiler_params=pltpu.CompilerParams(has_side_effects=EFFECT),
    )(*[_hbm(a) for a in arrs], *[_hbm(a) for a in lands], after)
    return outs[:ns], outs[ns:2 * ns], outs[2 * ns:2 * ns + n], outs[2 * ns + n:2 * ns + 2 * n], outs[-1]


def ag_forward(send_sems, recv_sems, arrs, lands, after, name):
    n = len(arrs)
    n1, n2 = 4 * n, 3 * n

    def body(*refs):
        srcs, lnd = refs[:n], refs[n:2 * n]
        o = 2 * n
        s1, r1 = refs[o:o + n1], refs[o + n1:o + 2 * n1]
        o += 2 * n1 + 1
        s2, r2 = refs[o:o + n2], refs[o + n2:o + 2 * n2]
        token = refs[-1]
        token[...] = jnp.zeros_like(token)
        x, y, c = _place()
        me, sibling = (x, y, c), (x, y, 1 - c)
        chips = [(1 - x, y), (x, 1 - y), (1 - x, 1 - y)]
        copy1 = _ag_copy(lnd, s1, r1, 4)
        copy2 = _ag_copy(lnd, s2, r2, 3)
        for j, chip in enumerate(chips):
            for a in range(n):
                copy1(a, 1 + j, (*chip, c), me).wait_recv()
                copy2(a, j, (*chip, c), sibling).start()
        for a in range(n):
            copy1(a, 0, sibling, me).wait_recv()
            copy1(a, 0, me, sibling, src=srcs[a]).wait_send()
            for j, chip in enumerate(chips):
                copy1(a, 1 + j, me, (*chip, c), src=srcs[a]).wait_send()

    outs = pl.pallas_call(
        body, name=name,
        in_specs=[HBM] * (2 * n) + [SEM] * (2 * n1) + [ANY],
        out_specs=[SEM] * (2 * n2) + [HBM] * n + [pl.BlockSpec(memory_space=pltpu.VMEM)],
        out_shape=_dma_sems(2 * n2) + _hbm_like(lands) + [_sds((8, 128), F32)],
        input_output_aliases={n + i: 2 * n2 + i for i in range(n)},
        compiler_params=pltpu.CompilerParams(has_side_effects=EFFECT),
    )(*arrs, *lands, *send_sems, *recv_sems, after)
    return outs[:n2], outs[n2:2 * n2], outs[2 * n2:2 * n2 + n], outs[-1]


def ag_finish(send_sems, recv_sems, lands, after, name):
    n = len(lands)
    n2 = 3 * n

    def body(*refs):
        lnd = refs[:n]
        s2, r2 = refs[n:n + n2], refs[n + n2:n + 2 * n2]
        x, y, c = _place()
        me, sibling = (x, y, c), (x, y, 1 - c)
        chips = [(1 - x, y), (x, 1 - y), (1 - x, 1 - y)]
        copy2 = _ag_copy(lnd, s2, r2, 3)
        for a in range(n):
            for j, chip in enumerate(chips):
                copy2(a, j, (*chip, c), sibling).wait_send()
                copy2(a, j, (*chip, 1 - c), me).wait_recv()

    outs = pl.pallas_call(
        body, name=name,
        in_specs=[HBM] * n + [SEM] * (2 * n2) + [ANY],
        out_specs=[HBM] * n, out_shape=_hbm_like(lands),
        input_output_aliases={i: i for i in range(n)},
        compiler_params=pltpu.CompilerParams(has_side_effects=EFFECT),
    )(*lands, *send_sems, *recv_sems, after)
    return list(outs)


def rs_sibling_start(arrs, name):
    n = len(arrs)
    ns = 4 * n
    lands = [lax.empty((4,) + a.shape[1:], a.dtype) for a in arrs]

    def body(*refs):
        ins, lnd = refs[:n], refs[n:2 * n]
        send_sems, recv_sems = refs[2 * n:2 * n + ns], refs[2 * n + ns:2 * n + 2 * ns]
        x, y, c = _place()
        sibling = (x, y, 1 - c)
        for a in range(n):
            for p in range(4):
                pltpu.make_async_remote_copy(
                    src_ref=ins[a].at[2 * p + 1 - c], dst_ref=lnd[a].at[p], send_sem=send_sems[4 * a + p],
                    recv_sem=recv_sems[4 * a + p], device_id=sibling, device_id_type=MESH).start()

    outs = pl.pallas_call(
        body, name=name,
        in_specs=[HBM] * (2 * n), out_specs=[SEM] * (2 * ns) + [HBM] * (2 * n),
        out_shape=_dma_sems(2 * ns) + _hbm_like(arrs) + _hbm_like(lands),
        input_output_aliases={i: 2 * ns + i for i in range(2 * n)},
        compiler_params=pltpu.CompilerParams(has_side_effects=EFFECT),
    )(*[_hbm(a) for a in arrs], *[_hbm(a) for a in lands])
    return outs[:ns], outs[ns:2 * ns], outs[2 * ns:2 * ns + n], outs[2 * ns + n:]


def rs_sibling_wait(send_sems, recv_sems, arrs, lands, after, name):
    n = len(arrs)
    ns = 4 * n

    def body(*refs):
        ins, lnd = refs[:n], refs[n:2 * n]
        s, r = refs[2 * n:2 * n + ns], refs[2 * n + ns:2 * n + 2 * ns]
        x, y, c = _place()
        sibling = (x, y, 1 - c)
        for a in range(n):
            for p in range(4):
                cp = pltpu.make_async_remote_copy(
                    src_ref=ins[a].at[2 * p + 1 - c], dst_ref=lnd[a].at[p], send_sem=s[4 * a + p], recv_sem=r[4 * a + p],
                    device_id=sibling, device_id_type=MESH)
                cp.wait_send()
                cp.wait_recv()

    outs = pl.pallas_call(
        body, name=name,
        in_specs=[HBM] * (2 * n) + [SEM] * (2 * ns) + [ANY], out_specs=[HBM] * (2 * n),
        out_shape=_hbm_like(arrs) + _hbm_like(lands),
        input_output_aliases={i: i for i in range(2 * n)},
        compiler_params=pltpu.CompilerParams(has_side_effects=EFFECT),
    )(*arrs, *lands, *send_sems, *recv_sems, after)
    return outs[:n], outs[n:]


def rs_chips_start(parts, name):
    n = len(parts)
    ns = 3 * n
    lands = [lax.empty((3,) + a.shape[1:], a.dtype) for a in parts]

    def body(*refs):
        ins, lnd = refs[:n], refs[n:2 * n]
        send_sems, recv_sems = refs[2 * n:2 * n + ns], refs[2 * n + ns:2 * n + 2 * ns]
        x, y, c = _place()
        chips = [(1 - x, y), (x, 1 - y), (1 - x, 1 - y)]
        for a in range(n):
            for k, (tx, ty) in enumerate(chips):
                pltpu.make_async_remote_copy(
                    src_ref=ins[a].at[2 * tx + ty], dst_ref=lnd[a].at[k], send_sem=send_sems[3 * a + k],
                    recv_sem=recv_sems[3 * a + k], device_id=(tx, ty, c), device_id_type=MESH).start()

    outs = pl.pallas_call(
        body, name=name,
        in_specs=[HBM] * (2 * n), out_specs=[SEM] * (2 * ns) + [HBM] * (2 * n),
        out_shape=_dma_sems(2 * ns) + _hbm_like(parts) + _hbm_like(lands),
        input_output_aliases={i: 2 * ns + i for i in range(2 * n)},
        compiler_params=pltpu.CompilerParams(has_side_effects=EFFECT),
    )(*[_hbm(a) for a in parts], *[_hbm(a) for a in lands])
    return outs[:ns], outs[ns:2 * ns], outs[2 * ns:2 * ns + n], outs[2 * ns + n:]


def rs_chips_wait(send_sems, recv_sems, parts, lands, after, name):
    n = len(parts)
    ns = 3 * n

    def body(*refs):
        ins, lnd = refs[:n], refs[n:2 * n]
        s, r = refs[2 * n:2 * n + ns], refs[2 * n + ns:2 * n + 2 * ns]
        x, y, c = _place()
        chips = [(1 - x, y), (x, 1 - y), (1 - x, 1 - y)]
        for a in range(n):
            for k, (tx, ty) in enumerate(chips):
                cp = pltpu.make_async_remote_copy(
                    src_ref=ins[a].at[2 * tx + ty], dst_ref=lnd[a].at[k], send_sem=s[3 * a + k], recv_sem=r[3 * a + k],
                    device_id=(tx, ty, c), device_id_type=MESH)
                cp.wait_send()
                cp.wait_recv()

    outs = pl.pallas_call(
        body, name=name,
        in_specs=[HBM] * (2 * n) + [SEM] * (2 * ns) + [ANY], out_specs=[HBM] * (2 * n),
        out_shape=_hbm_like(parts) + _hbm_like(lands),
        input_output_aliases={i: i for i in range(2 * n)},
        compiler_params=pltpu.CompilerParams(has_side_effects=EFFECT),
    )(*parts, *lands, *send_sems, *recv_sems, after)
    return outs[:n], outs[n:]


def pair_sum(a, r, c, name):
    _, rr, cc = a.shape

    def body(c_ref, a_ref, r_ref, o_ref):
        del c_ref
        o_ref[...] = (a_ref[...].astype(F32) + r_ref[...].astype(F32)).astype(o_ref.dtype)

    return pl.pallas_call(
        body, name=name,
        grid_spec=pltpu.PrefetchScalarGridSpec(
            num_scalar_prefetch=1, grid=(4,),
            in_specs=[pl.BlockSpec((None, rr, cc), lambda p, c_ref: (2 * p + c_ref[0], 0, 0)),
                      pl.BlockSpec((None, rr, cc), lambda p, c_ref: (p, 0, 0))],
            out_specs=pl.BlockSpec((None, rr, cc), lambda p, c_ref: (p, 0, 0))),
        out_shape=_sds((4, rr, cc), a.dtype), compiler_params=_params(("parallel",)),
    )(c, a, r)


def _adamw(w, g, m, v):
    m = ADAM_B1 * m + (1.0 - ADAM_B1) * g
    v = ADAM_B2 * v + (1.0 - ADAM_B2) * jnp.square(g)
    m_hat = m / (1.0 - ADAM_B1 ** ADAM_STEP)
    v_hat = v / (1.0 - ADAM_B2 ** ADAM_STEP)
    return -ADAM_LR * (m_hat / (jnp.sqrt(v_hat) + ADAM_EPS) + ADAM_WD * w), m, v


def adamw_big(recv, sums, chip, w, m, v, tr, name):
    nl, rr, cc = w.shape
    cp = recv[0].shape[2]

    def body(chip_ref, *refs):
        del chip_ref
        rcv, own = refs[:nl], refs[nl:2 * nl]
        w_ref, m_ref, v_ref, g_out, d_out, m_out, v_out = refs[2 * nl:]
        for l in range(nl):
            g = ((own[l][...].astype(F32) + rcv[l][0].astype(F32)) + rcv[l][1].astype(F32)) + rcv[l][2].astype(F32)
            g = g[:, :cc]
            g_out[l] = g
            d_out[l], m_out[l], v_out[l] = _adamw(w_ref[l], g, m_ref[l], v_ref[l])

    blk = pl.BlockSpec((nl, tr, cc), lambda i, chip_ref: (0, i, 0))
    return pl.pallas_call(
        body, name=name,
        grid_spec=pltpu.PrefetchScalarGridSpec(
            num_scalar_prefetch=1, grid=(rr // tr,),
            in_specs=[pl.BlockSpec((3, tr, cp), lambda i, chip_ref: (0, i, 0))] * nl
            + [pl.BlockSpec((None, tr, cp), lambda i, chip_ref: (chip_ref[0], i, 0))] * nl + [blk, blk, blk],
            out_specs=[blk] * 4),
        out_shape=[_sds(w.shape, F32)] * 4, compiler_params=_params(("parallel",)),
    )(chip, *recv, *sums, w, m, v)


def sum_devices(g, name):
    _, rr, cc = g.shape

    def body(g_ref, o_ref):
        acc = g_ref[0]
        for j in range(1, NDEV):
            acc = acc + g_ref[j]
        o_ref[...] = acc

    return pl.pallas_call(
        body, name=name, grid=(1,), in_specs=[_full(g.shape)], out_specs=_full((rr, cc)), out_shape=_sds((rr, cc), F32),
        compiler_params=_params(("arbitrary",)),
    )(g)


def adamw_rows(g, w, m, v, name):
    rr, cc = w.shape

    def body(g_ref, w_ref, m_ref, v_ref, d_out, m_out, v_out):
        d_out[...], m_out[...], v_out[...] = _adamw(w_ref[...], g_ref[...], m_ref[...], v_ref[...])

    blk = _full((rr, cc))
    return pl.pallas_call(
        body, name=name, grid=(1,), in_specs=[blk] * 4, out_specs=[blk] * 3, out_shape=[_sds((rr, cc), F32)] * 3,
        compiler_params=_params(("arbitrary",)),
    )(g, w, m, v)


def _block_diag(w):
    eye = jnp.eye(LRU_BLOCKS, dtype=w.dtype)
    return (w[:, :, None, :] * eye[:, None, :, None]).reshape(D_LRU, D_LRU)


def _diag_blocks(wd):
    w4 = wd.reshape(LRU_BLOCKS, LRU_BD, LRU_BLOCKS, LRU_BD)
    return jnp.stack([w4[g, :, g, :] for g in range(LRU_BLOCKS)])


def _pack(arrs):
    flat = jnp.concatenate([a.reshape(-1) for a in arrs])
    return flat.reshape(-1, 128)


def _unpack(packed, shapes):
    flat = packed.reshape(-1)
    out, o = [], 0
    for s in shapes:
        n = int(np.prod(s))
        out.append(flat[o:o + n].reshape(s))
        o += n
    return out


REP_NAMES = ["norm_mix", "conv_b", "gate_a_w", "gate_a_b", "gate_x_w", "gate_x_b", "lru_lambda", "lru_out_norm",
             "ret_out_norm", "norm_ffn", "norm_final"]


def kernel(x, meta_tokens, norm_mix, w_in, conv_w, conv_b, gate_a_w, gate_a_b, gate_x_w, gate_x_b, lru_lambda, lru_out_norm, ret_out_norm, w_out, norm_ffn, w_gate, w_up, w_down, norm_final, loss_target, m_meta_tokens, m_norm_mix, m_w_in, m_conv_w, m_conv_b, m_gate_a_w, m_gate_a_b, m_gate_x_w, m_gate_x_b, m_lru_lambda, m_lru_out_norm, m_ret_out_norm, m_w_out, m_norm_ffn, m_w_gate, m_w_up, m_w_down, m_norm_final, v_meta_tokens, v_norm_mix, v_w_in, v_conv_w, v_conv_b, v_gate_a_w, v_gate_a_b, v_gate_x_w, v_gate_x_b, v_lru_lambda, v_lru_out_norm, v_ret_out_norm, v_w_out, v_norm_ffn, v_w_gate, v_w_up, v_w_down, v_norm_final):
    xi, yi, ci = _place()
    dev = 4 * xi + 2 * yi + ci
    c_arr = jnp.reshape(ci, (1,)).astype(jnp.int32)

    ffpad = FF_SHP - FF_SH
    wire = dict(
        w_in=w_in.astype(WIRE_DTYPE),
        w_gate=jnp.pad(w_gate.astype(WIRE_DTYPE), ((0, 0), (0, 0), (0, ffpad))),
        w_up=jnp.pad(w_up.astype(WIRE_DTYPE), ((0, 0), (0, 0), (0, ffpad))),
        w_out=w_out.astype(WIRE_DTYPE),
        w_down=jnp.pad(w_down.astype(WIRE_DTYPE), ((0, 0), (0, ffpad), (0, 0))),
    )
    meta_g, conv_g = all_gather([meta_tokens, conv_w], "ag_small")
    meta_full = jnp.transpose(meta_g, (1, 0, 2)).reshape(N_META, D)
    conv_full = jnp.transpose(conv_g, (1, 2, 0, 3)).reshape(DEPTH, CONV_W, D_LRU)
    big_names = ("w_in", "w_gate", "w_up", "w_out", "w_down")
    level1 = []
    token = meta_g
    for l in range(DEPTH):
        arrs = [wire[k][l] for k in big_names]
        s1, r1, arrs, lands, token = ag_start(arrs, ag_place(arrs, "ag_place"), token, f"ag_start_{l}")
        level1.append((s1, r1, arrs, lands))

    def as_weights(gi, gg, gu, go, gd):
        return dict(w_in=gi, w_gate=gg, w_up=gu, w_out=go.reshape(D, D), w_down=gd.reshape(D_FFP, D))

    tables = _ret_tables()
    row = lambda a: a.reshape(1, -1)

    h = jnp.concatenate([jnp.zeros((PAD, D), F32), meta_full, x[0]], axis=0)
    saved, gathered = [], []
    s1, r1, arrs, lands = level1[0]
    s2, r2, lands, order = ag_forward(s1, r1, arrs, lands, token, "ag_forward_0")
    w = as_weights(*ag_finish(s2, r2, lands, order, "ag_finish_0"))
    for l in range(DEPTH):
        gathered.append(w)
        small = dict(cw=conv_full[l], cb=row(conv_b[l]), wa=_block_diag(gate_a_w[l]).astype(MXU_DTYPE), ba=row(gate_a_b[l]),
                     wx=_block_diag(gate_x_w[l]).astype(MXU_DTYPE), bx=row(gate_x_b[l]), lam=row(lru_lambda[l]),
                     gain=row(lru_out_norm[l]))
        hn1 = rmsnorm_fwd(h, row(norm_mix[l]), "rms_fwd")
        proj = mm_blocked_nn(hn1, w["w_in"], F32, "proj")
        ylru, hst = lru_fwd(proj, name="lru_fwd", **small)
        ymix, states = ret_fwd(proj, ylru, tables, row(ret_out_norm[l]), "ret_fwd")
        h_mid = mm_nn_res(ymix, w["w_out"], h, order, "out_proj")
        hn2 = rmsnorm_fwd(h_mid, row(norm_ffn[l]), "rms_fwd")
        gate, up, act = ffn_up(hn2, w["w_gate"], w["w_up"], "ffn_up")
        if l + 1 < DEPTH:
            s1, r1, arrs, lands = level1[l + 1]
            s2, r2, lands, order = ag_forward(s1, r1, arrs, lands, act, f"ag_forward_{l + 1}")
        h_out = mm_nn_res(act, w["w_down"], h_mid, order, "ffn_down")
        if l + 1 < DEPTH:
            w_next = as_weights(*ag_finish(s2, r2, lands, h_out, f"ag_finish_{l + 1}"))
        saved.append(dict(h=h, hn1=hn1, proj=proj, hst=hst, states=states, ymix=ymix, h_mid=h_mid, hn2=hn2, gate=gate, up=up,
                          act=act, small=small))
        h = h_out
        if l + 1 < DEPTH:
            w = w_next

    loss_p, dh, g_norm_final = loss_head(h, row(norm_final), loss_target[0], "loss_head")
    loss = lax.psum(loss_p[0, 0], ("x", "y", "c"))

    rep = [None] * DEPTH
    convw_g = [None] * DEPTH
    chip_sums = [None] * DEPTH
    inflight = [None] * DEPTH
    sib = None

    def sibling_done(l, sib, after):
        parts, got = rs_sibling_wait(*sib, after, f"rs_sibling_wait_{l}")
        sums = [pair_sum(a, r, c_arr, "pair_sum") for a, r in zip(parts, got)]
        inflight[l] = rs_chips_start(sums, f"rs_chips_start_{l}")

    for l in reversed(range(DEPTH)):
        w, s = gathered[l], saved[l]
        dgate, dup = ffn_down_bwd(dh, w["w_down"], s["gate"], s["up"], "ffn_down_bwd")
        dwd = mm_tn(s["act"], dh, FF_SHP, "dw_down")
        dwg = mm_tn_blocked(s["hn2"], dgate, "dw_blocked")
        dwu = mm_tn_blocked(s["hn2"], dup, "dw_blocked")
        dhn2 = mm_blocked_nt([(dgate, w["w_gate"]), (dup, w["w_up"])], "ffn_up_bwd")
        if sib is not None:
            sibling_done(l + 1, sib, dhn2)
        dh_mid, g_norm_ffn = rmsnorm_bwd(s["h_mid"], row(norm_ffn[l]), dhn2, dh, "rms_bwd")
        dymix = mm_nt(dh_mid, w["w_out"], F32, "out_proj_bwd")
        dwo = mm_tn(s["ymix"], dh_mid, 512, "dw_out")
        dxg, lvec, dwa, dwx = lru_bwd(s["proj"], s["hst"], dymix, name="lru_bwd", **s["small"])
        dproj, g_ret_norm = ret_bwd(s["proj"], s["states"], dymix, dxg, tables, row(ret_out_norm[l]), "ret_bwd")
        dwi = mm_tn_blocked(s["hn1"], dproj, "dw_blocked")
        dhn1 = mm_blocked_nt([(dproj, w["w_in"])], "proj_bwd")
        dh, g_norm_mix = rmsnorm_bwd(s["h"], row(norm_mix[l]), dhn1, dh_mid, "rms_bwd")

        rep[l] = [g_norm_mix, lvec[4], _diag_blocks(dwa), lvec[5], _diag_blocks(dwx), lvec[6], lvec[7], lvec[8], g_ret_norm,
                  g_norm_ffn]
        convw_g[l] = lvec[0:CONV_W]
        parts = [dwi, dwg, dwu, dwo.reshape(NDEV, OUT_SH, D), dwd.reshape(NDEV, FF_SHP, D)]
        sib = rs_sibling_start(parts, f"rs_sibling_start_{l}")
    sibling_done(0, sib, dh)

    grad_x = dh[X0:][None]
    g_meta = dh[PAD:X0]

    rep_shapes = [(D,), (D_LRU,), (LRU_BLOCKS, LRU_BD, LRU_BD), (LRU_BLOCKS, LRU_BD), (LRU_BLOCKS, LRU_BD, LRU_BD),
                  (LRU_BLOCKS, LRU_BD), (D_LRU,), (D_LRU,), (D_RET,), (D,)]
    flat = [a for l in range(DEPTH) for a in rep[l]] + [g_norm_final] + [convw_g[l] for l in range(DEPTH)] + [g_meta]
    (gath,) = all_gather([_pack(flat)], "ag_grads")
    gsum = sum_devices(gath, "sum_devices")
    shapes = rep_shapes * DEPTH + [(D,)] + [(CONV_W, D_LRU)] * DEPTH + [(N_META, D)]
    parts = _unpack(gsum, shapes)
    nrep = len(rep_shapes)
    g_rep = {n: jnp.stack([parts[l * nrep + i] for l in range(DEPTH)]) for i, n in enumerate(REP_NAMES[:-1])}
    g_rep["norm_final"] = parts[DEPTH * nrep]
    g_convw = lax.dynamic_slice_in_dim(jnp.stack(parts[DEPTH * nrep + 1:DEPTH * nrep + 1 + DEPTH]), dev * (D_LRU // NDEV),
                                       D_LRU // NDEV, axis=2)
    g_metatok = lax.dynamic_slice_in_dim(parts[-1], dev * (D // NDEV), D // NDEV, axis=1)

    given = dict(norm_mix=(norm_mix, m_norm_mix, v_norm_mix), conv_b=(conv_b, m_conv_b, v_conv_b),
                 gate_a_w=(gate_a_w, m_gate_a_w, v_gate_a_w), gate_a_b=(gate_a_b, m_gate_a_b, v_gate_a_b),
                 gate_x_w=(gate_x_w, m_gate_x_w, v_gate_x_w), gate_x_b=(gate_x_b, m_gate_x_b, v_gate_x_b),
                 lru_lambda=(lru_lambda, m_lru_lambda, v_lru_lambda), lru_out_norm=(lru_out_norm, m_lru_out_norm, v_lru_out_norm),
                 ret_out_norm=(ret_out_norm, m_ret_out_norm, v_ret_out_norm), norm_ffn=(norm_ffn, m_norm_ffn, v_norm_ffn),
                 norm_final=(norm_final, m_norm_final, v_norm_final),
                 conv_w=(conv_w, m_conv_w, v_conv_w), meta_tokens=(meta_tokens, m_meta_tokens, v_meta_tokens))
    small_names = REP_NAMES + ["conv_w", "meta_tokens"]
    small_g = dict(g_rep, conv_w=g_convw, meta_tokens=g_metatok)
    small_shapes = [given[n][0].shape for n in small_names]
    packs = [_pack([small_g[n] for n in small_names])] + [_pack([given[n][k] for n in small_names]) for k in range(3)]
    upd = adamw_rows(*packs, "adamw_small")
    small_out = [dict(zip(small_names, _unpack(p, small_shapes))) for p in upd]

    recv = [None] * DEPTH
    for l in reversed(range(DEPTH)):
        chip_sums[l], recv[l] = rs_chips_wait(*inflight[l], upd[0], f"rs_chips_wait_{l}")
    chip = jnp.reshape(2 * xi + yi, (1,)).astype(jnp.int32)

    def finish(idx, wname, w_, m_, v_, tr):
        return adamw_big([recv[l][idx] for l in range(DEPTH)], [chip_sums[l][idx] for l in range(DEPTH)], chip, w_, m_, v_, tr,
                         "adamw_" + wname)

    o_in = finish(0, "w_in", w_in, m_w_in, v_w_in, 256)
    o_gate = finish(1, "w_gate", w_gate, m_w_gate, v_w_gate, 256)
    o_up = finish(2, "w_up", w_up, m_w_up, v_w_up, 256)
    o_out = finish(3, "w_out", w_out, m_w_out, v_w_out, 64)
    o_down = finish(4, "w_down", w_down, m_w_down, v_w_down, 32)

    bigs = dict(w_in=o_in, w_out=o_out, w_gate=o_gate, w_up=o_up, w_down=o_down)
    order = ["meta_tokens", "norm_mix", "w_in", "conv_w", "conv_b", "gate_a_w", "gate_a_b", "gate_x_w", "gate_x_b", "lru_lambda",
             "lru_out_norm", "ret_out_norm", "w_out", "norm_ffn", "w_gate", "w_up", "w_down", "norm_final"]
    grads = [bigs[n][0] if n in bigs else small_g[n] for n in order]
    rest = [[bigs[n][k + 1] if n in bigs else small_out[k][n] for n in order] for k in range(3)]
    return (loss, grad_x, *grads, *rest[0], *rest[1], *rest[2])
```

```python
import functools

import numpy as np
import jax
import jax.numpy as jnp
from jax import lax
from jax.experimental import pallas as pl
from jax.experimental.pallas import tpu as pltpu

F32, BF16 = jnp.float32, jnp.bfloat16
MXU_DTYPE = BF16
WIRE_DTYPE = BF16

D = 1024
SEQ = 2048
DEPTH = 4
N_META = 16
CH = 128
PAD = (-(SEQ + N_META)) % CH
T = SEQ + N_META + PAD
NCH = T // CH
X0 = PAD + N_META
D_LRU = 512
LRU_BLOCKS = 8
LRU_BD = 64
CONV_W = 4
LRU_C = 8.0
D_RET = 512
HEADS = 4
HD = 128
ROPE_BASE = 10000.0
D_IN = 3072
D_FF = 2816
NDEV = 8
IN_SH = D_IN // NDEV
FF_SH = D_FF // NDEV
FF_SHP = 384
D_FFP = NDEV * FF_SHP
OUT_SH = D // NDEV
EPS = 1e-6
TM = 544
VMEM_LIMIT = 56 * 2**20
MESH = pl.DeviceIdType.MESH

ADAM_LR, ADAM_B1, ADAM_B2, ADAM_EPS, ADAM_WD, ADAM_STEP = 0.001, 0.9, 0.999, 1e-08, 0.01, 10

NN = ((1,), (0,))
NT = ((1,), (1,))
TN = ((0,), (0,))


def _dot(a, b, dims):
    return lax.dot_general(a.astype(MXU_DTYPE), b.astype(MXU_DTYPE), (dims, ((), ())), preferred_element_type=F32)


def _sds(shape, dtype):
    return jax.ShapeDtypeStruct(shape, dtype)


def _params(sem=None):
    return pltpu.CompilerParams(dimension_semantics=sem, vmem_limit_bytes=VMEM_LIMIT)


def _full(shape):
    n = len(shape)
    return pl.BlockSpec(shape, lambda *_: (0,) * n)


def rmsnorm_fwd(h, gain, name):
    def body(h_ref, g_ref, o_ref):
        x = h_ref[...]
        ms = jnp.mean(x * x, axis=-1, keepdims=True)
        o_ref[...] = (x * lax.rsqrt(ms + EPS) * g_ref[...]).astype(o_ref.dtype)

    return pl.pallas_call(
        body, name=name, grid=(T // TM,),
        in_specs=[pl.BlockSpec((TM, D), lambda i: (i, 0)), _full((1, D))],
        out_specs=pl.BlockSpec((TM, D), lambda i: (i, 0)),
        out_shape=_sds((T, D), MXU_DTYPE), compiler_params=_params(("parallel",)),
    )(h, gain)


def rmsnorm_bwd(h, gain, dhn, dres, name):
    def body(h_ref, g_ref, dhn_ref, dres_ref, dh_ref, dg_ref):
        x = h_ref[...]
        rstd = lax.rsqrt(jnp.mean(x * x, axis=-1, keepdims=True) + EPS)
        xhat = x * rstd
        dy = dhn_ref[...]
        dyg = dy * g_ref[...]
        dh_ref[...] = dres_ref[...] + rstd * (dyg - xhat * jnp.mean(dyg * xhat, axis=-1, keepdims=True))

        @pl.when(pl.program_id(0) == 0)
        def _():
            dg_ref[...] = jnp.zeros_like(dg_ref)
        dg_ref[...] += jnp.sum(dy * xhat, axis=0, keepdims=True)

    row = pl.BlockSpec((TM, D), lambda i: (i, 0))
    return pl.pallas_call(
        body, name=name, grid=(T // TM,),
        in_specs=[row, _full((1, D)), row, row],
        out_specs=[row, _full((1, D))],
        out_shape=[_sds((T, D), F32), _sds((1, D), F32)], compiler_params=_params(("arbitrary",)),
    )(h, gain, dhn, dres)


def loss_head(h, gain, target, name):
    def body(h_ref, g_ref, t_ref, loss_ref, dh_ref, dg_ref):
        i = pl.program_id(0)

        @pl.when(i == 0)
        def _():
            loss_ref[...] = jnp.zeros_like(loss_ref)
            dg_ref[...] = jnp.zeros_like(dg_ref)
            dh_ref[...] = jnp.zeros_like(dh_ref)

        @pl.when(i > 0)
        def _():
            x = h_ref[...]
            g = g_ref[...]
            rstd = lax.rsqrt(jnp.mean(x * x, axis=-1, keepdims=True) + EPS)
            xhat = x * rstd
            err = xhat * g - t_ref[...]
            loss_ref[...] += 0.5 * jnp.sum(jnp.mean(err * err, axis=-1, keepdims=True), axis=0, keepdims=True)
            dy = err * (1.0 / D)
            dyg = dy * g
            dh_ref[...] = rstd * (dyg - xhat * jnp.mean(dyg * xhat, axis=-1, keepdims=True))
            dg_ref[...] += jnp.sum(dy * xhat, axis=0, keepdims=True)

    row = pl.BlockSpec((CH, D), lambda i: (i, 0))
    return pl.pallas_call(
        body, name=name, grid=(NCH,),
        in_specs=[row, _full((1, D)), pl.BlockSpec((CH, D), lambda i: (jnp.maximum(i - 1, 0), 0))],
        out_specs=[_full((8, 128)), row, _full((1, D))],
        out_shape=[_sds((8, 128), F32), _sds((T, D), F32), _sds((1, D), F32)],
        compiler_params=_params(("arbitrary",)),
    )(h, gain, target)


def mm_blocked_nn(a, w, out_dtype, name):
    k = a.shape[1]

    def body(a_ref, w_ref, o_ref):
        o_ref[...] = _dot(a_ref[...], w_ref[...], NN).astype(o_ref.dtype)

    return pl.pallas_call(
        body, name=name, grid=(NDEV, T // TM),
        in_specs=[pl.BlockSpec((TM, k), lambda j, i: (i, 0)), pl.BlockSpec((None, k, IN_SH), lambda j, i: (j, 0, 0))],
        out_specs=pl.BlockSpec((TM, IN_SH), lambda j, i: (i, j)),
        out_shape=_sds((T, NDEV * IN_SH), out_dtype), compiler_params=_params(("parallel", "parallel")),
    )(a, w)


def mm_nn_res(a, w, res, after, name):
    k = a.shape[1]
    bn = 512

    def body(a_ref, w_ref, r_ref, after_ref, o_ref):
        del after_ref
        o_ref[...] = r_ref[...] + _dot(a_ref[...], w_ref[...], NN)

    return pl.pallas_call(
        body, name=name, grid=(D // bn, T // TM),
        in_specs=[pl.BlockSpec((TM, k), lambda j, i: (i, 0)), pl.BlockSpec((k, bn), lambda j, i: (0, j)),
                  pl.BlockSpec((TM, bn), lambda j, i: (i, j)), pl.BlockSpec(memory_space=pl.ANY)],
        out_specs=pl.BlockSpec((TM, bn), lambda j, i: (i, j)),
        out_shape=_sds((T, D), F32), compiler_params=_params(("parallel", "parallel")),
    )(a, w, res, after)


def ffn_up(hn, wg, wu, name):
    def body(a_ref, wg_ref, wu_ref, g_ref, u_ref, act_ref):
        a = a_ref[...]
        g = _dot(a, wg_ref[...], NN)
        u = _dot(a, wu_ref[...], NN)
        g_ref[...] = g.astype(g_ref.dtype)
        u_ref[...] = u.astype(u_ref.dtype)
        act_ref[...] = (jax.nn.silu(g) * u).astype(act_ref.dtype)

    wspec = pl.BlockSpec((None, D, FF_SHP), lambda j, i: (j, 0, 0))
    ospec = pl.BlockSpec((TM, FF_SHP), lambda j, i: (i, j))
    return pl.pallas_call(
        body, name=name, grid=(NDEV, T // TM),
        in_specs=[pl.BlockSpec((TM, D), lambda j, i: (i, 0)), wspec, wspec],
        out_specs=[ospec, ospec, ospec],
        out_shape=[_sds((T, D_FFP), MXU_DTYPE)] * 3, compiler_params=_params(("parallel", "parallel")),
    )(hn, wg, wu)


def ffn_down_bwd(dh, wd, gate, up, after, name):
    def body(dh_ref, wd_ref, g_ref, u_ref, after_ref, dg_ref, du_ref):
        del after_ref
        dact = _dot(dh_ref[...], wd_ref[...], NT)
        g = g_ref[...].astype(F32)
        u = u_ref[...].astype(F32)
        sg = jax.nn.sigmoid(g)
        dg_ref[...] = (dact * u * (sg * (1.0 + g * (1.0 - sg)))).astype(dg_ref.dtype)
        du_ref[...] = (dact * (g * sg)).astype(du_ref.dtype)

    blk = pl.BlockSpec((TM, FF_SHP), lambda j, i: (i, j))
    return pl.pallas_call(
        body, name=name, grid=(NDEV, T // TM),
        in_specs=[pl.BlockSpec((TM, D), lambda j, i: (i, 0)), pl.BlockSpec((FF_SHP, D), lambda j, i: (j, 0)), blk, blk,
                  pl.BlockSpec(memory_space=pl.ANY)],
        out_specs=[blk, blk],
        out_shape=[_sds((T, D_FFP), MXU_DTYPE)] * 2, compiler_params=_params(("parallel", "parallel")),
    )(dh, wd, gate, up, after)


def mm_nt(a, w, out_dtype, after, name):
    n = a.shape[1]
    bk = 512

    def body(a_ref, w_ref, after_ref, o_ref):
        del after_ref
        o_ref[...] = _dot(a_ref[...], w_ref[...], NT).astype(o_ref.dtype)

    return pl.pallas_call(
        body, name=name, grid=(D // bk, T // TM),
        in_specs=[pl.BlockSpec((TM, n), lambda j, i: (i, 0)), pl.BlockSpec((bk, n), lambda j, i: (j, 0)),
                  pl.BlockSpec(memory_space=pl.ANY)],
        out_specs=pl.BlockSpec((TM, bk), lambda j, i: (i, j)),
        out_shape=_sds((T, D), out_dtype), compiler_params=_params(("parallel", "parallel")),
    )(a, w, after)


def mm_blocked_nt(pairs, name):
    n = len(pairs)

    def body(*refs):
        o_ref = refs[2 * n]

        @pl.when(pl.program_id(1) == 0)
        def _():
            o_ref[...] = jnp.zeros_like(o_ref)
        acc = _dot(refs[0][...], refs[1][...], NT)
        for p in range(1, n):
            acc += _dot(refs[2 * p][...], refs[2 * p + 1][...], NT)
        o_ref[...] += acc

    specs, args = [], []
    for a, w in pairs:
        specs += [pl.BlockSpec((TM, IN_SH), lambda i, j: (i, j)), pl.BlockSpec((None, D, IN_SH), lambda i, j: (j, 0, 0))]
        args += [a, w]
    return pl.pallas_call(
        body, name=name, grid=(T // TM, NDEV), in_specs=specs,
        out_specs=pl.BlockSpec((TM, D), lambda i, j: (i, 0)),
        out_shape=_sds((T, D), F32), compiler_params=_params(("parallel", "arbitrary")),
    )(*args)


def mm_tn_blocked(a, b, name):
    def body(a_ref, b_ref, o_ref):
        o_ref[...] = _dot(a_ref[...], b_ref[...], TN).astype(o_ref.dtype)

    return pl.pallas_call(
        body, name=name, grid=(NDEV,),
        in_specs=[_full((T, D)), pl.BlockSpec((T, IN_SH), lambda j: (0, j))],
        out_specs=pl.BlockSpec((None, D, IN_SH), lambda j: (j, 0, 0)),
        out_shape=_sds((NDEV, D, IN_SH), WIRE_DTYPE), compiler_params=_params(("parallel",)),
    )(a, b)


def mm_tn(a, b, bm, name):
    m = a.shape[1]
    bn = 512

    def body(a_ref, b_ref, o_ref):
        o_ref[...] = _dot(a_ref[...], b_ref[...], TN).astype(o_ref.dtype)

    return pl.pallas_call(
        body, name=name, grid=(m // bm, D // bn),
        in_specs=[pl.BlockSpec((T, bm), lambda i, j: (0, i)), pl.BlockSpec((T, bn), lambda i, j: (0, j))],
        out_specs=pl.BlockSpec((bm, bn), lambda i, j: (i, j)),
        out_shape=_sds((m, D), WIRE_DTYPE), compiler_params=_params(("parallel", "parallel")),
    )(a, b)


def _softplus_neg(lam):
    return jnp.maximum(-lam, 0.0) + jnp.log1p(jnp.exp(-jnp.abs(lam)))


def _lru_gates(pa, px, xc, lam):
    r = jax.nn.sigmoid(pa)
    ig = jax.nn.sigmoid(px)
    log_a = -LRU_C * r * _softplus_neg(lam)
    a = jnp.exp(log_a)
    mult = jnp.sqrt(-jnp.tanh(log_a) * (jnp.exp(2.0 * log_a) + 1.0))
    return a, mult * (ig * xc)


def _lru_out(h, g, gain):
    z = h * jax.nn.gelu(g)
    return z * lax.rsqrt(jnp.mean(z * z, axis=-1, keepdims=True) + EPS) * gain


def _conv_taps(x, xprev, row):
    taps = [x]
    for s in range(1, CONV_W):
        taps.append(jnp.where(row < s, pltpu.roll(xprev, s, 0), pltpu.roll(x, s, 0)))
    return taps


def _conv(taps, cw_ref, cb):
    xc = cb + cw_ref[CONV_W - 1:CONV_W, :] * taps[0]
    for s in range(1, CONV_W):
        xc = xc + cw_ref[CONV_W - 1 - s:CONV_W - s, :] * taps[s]
    return xc


def lru_fwd(proj, cw, cb, wa, ba, wx, bx, lam, gain, name):
    def body(x_ref, g_ref, cw_ref, cb_ref, wa_ref, ba_ref, wx_ref, bx_ref, lam_ref, gain_ref,
             y_ref, h_ref, xprev_scr, a_scr, b_scr, carry_scr):
        i = pl.program_id(0)

        @pl.when(i == 0)
        def _():
            xprev_scr[...] = jnp.zeros_like(xprev_scr)
            carry_scr[...] = jnp.zeros_like(carry_scr)

        x = x_ref[...]
        row = lax.broadcasted_iota(jnp.int32, (CH, D_LRU), 0)
        xc = _conv(_conv_taps(x, xprev_scr[...], row), cw_ref, cb_ref[...])
        pa = _dot(xc, wa_ref[...], NN) + ba_ref[...]
        px = _dot(xc, wx_ref[...], NN) + bx_ref[...]
        a, b = _lru_gates(pa, px, xc, lam_ref[...])
        a_scr[...] = a
        b_scr[...] = jnp.where(i * CH + row >= PAD, b, 0.0)
        h = carry_scr[...]
        for t in range(CH):
            h = a_scr[t:t + 1, :] * h + b_scr[t:t + 1, :]
            h_ref[t:t + 1, :] = h
        carry_scr[...] = h
        xprev_scr[...] = x
        y_ref[...] = _lru_out(h_ref[...], g_ref[...], gain_ref[...]).astype(y_ref.dtype)

    vec = _full((1, D_LRU))
    mat = _full((D_LRU, D_LRU))
    return pl.pallas_call(
        body, name=name, grid=(NCH,),
        in_specs=[pl.BlockSpec((CH, D_LRU), lambda i: (i, 0)), pl.BlockSpec((CH, D_LRU), lambda i: (i, 1)),
                  _full((CONV_W, D_LRU)), vec, mat, vec, mat, vec, vec, vec],
        out_specs=[pl.BlockSpec((CH, D_LRU), lambda i: (i, 0)), pl.BlockSpec((CH, D_LRU), lambda i: (i, 0))],
        out_shape=[_sds((T, D_LRU), MXU_DTYPE), _sds((T, D_LRU), F32)],
        scratch_shapes=[pltpu.VMEM((CH, D_LRU), F32), pltpu.VMEM((CH, D_LRU), F32), pltpu.VMEM((CH, D_LRU), F32),
                        pltpu.VMEM((1, D_LRU), F32)],
        compiler_params=_params(("arbitrary",)),
    )(proj, proj, cw, cb, wa, ba, wx, bx, lam, gain)


LRU_VEC_ROWS = 16


def lru_bwd(proj, hst, dymix, cw, cb, wa, ba, wx, bx, lam, gain, name):
    last = NCH - 1

    def body(x_ref, xp_ref, g_ref, h_ref, hp_ref, dy_ref, cw_ref, cb_ref, wa_ref, ba_ref, wx_ref, bx_ref, lam_ref,
             gain_ref, dxg_ref, vec_ref, dwa_ref, dwx_ref, a_scr, dh_scr, g_scr, carry_scr, dxcn_scr):
        i = pl.program_id(0)
        ib = last - i

        @pl.when(i == 0)
        def _():
            carry_scr[...] = jnp.zeros_like(carry_scr)
            dxcn_scr[...] = jnp.zeros_like(dxcn_scr)
            vec_ref[...] = jnp.zeros_like(vec_ref)
            dwa_ref[...] = jnp.zeros_like(dwa_ref)
            dwx_ref[...] = jnp.zeros_like(dwx_ref)

        x = x_ref[...]
        row = lax.broadcasted_iota(jnp.int32, (CH, D_LRU), 0)
        valid = ib * CH + row >= PAD
        taps = _conv_taps(x, xp_ref[...], row)
        xc = _conv(taps, cw_ref, cb_ref[...])
        pa = _dot(xc, wa_ref[...], NN) + ba_ref[...]
        px = _dot(xc, wx_ref[...], NN) + bx_ref[...]
        (a, _), vjp_gates = jax.vjp(_lru_gates, pa, px, xc, lam_ref[...])
        h = h_ref[...]
        _, vjp_out = jax.vjp(_lru_out, h, g_ref[...], gain_ref[...])
        dh, dg, dgain = vjp_out(dy_ref[...].astype(F32))
        a_scr[...] = a
        dh_scr[...] = dh
        c = carry_scr[...]
        for t in range(CH - 1, -1, -1):
            gt = dh_scr[t:t + 1, :] + c
            g_scr[t:t + 1, :] = gt
            c = a_scr[t:t + 1, :] * gt
        carry_scr[...] = c
        gg = g_scr[...]
        hprev = jnp.where(row < 1, pltpu.roll(hp_ref[...], 1, 0), pltpu.roll(h, 1, 0))
        da = jnp.where(valid, gg * hprev, 0.0)
        db = jnp.where(valid, gg, 0.0)
        dpa, dpx, dxc, dlam = vjp_gates((da, db))
        dxc = dxc + _dot(dpa, wa_ref[...], NT) + _dot(dpx, wx_ref[...], NT)
        dwa_ref[...] += _dot(xc, dpa, TN)
        dwx_ref[...] += _dot(xc, dpx, TN)
        for s in range(CONV_W):
            vec_ref[CONV_W - 1 - s:CONV_W - s, :] += jnp.sum(dxc * taps[s], axis=0, keepdims=True)
        vec_ref[4:5, :] += jnp.sum(dxc, axis=0, keepdims=True)
        vec_ref[5:6, :] += jnp.sum(dpa, axis=0, keepdims=True)
        vec_ref[6:7, :] += jnp.sum(dpx, axis=0, keepdims=True)
        vec_ref[7:8, :] += dlam
        vec_ref[8:9, :] += dgain
        dxn = dxcn_scr[...]
        dx = cw_ref[CONV_W - 1:CONV_W, :] * dxc
        for s in range(1, CONV_W):
            ahead = jnp.where(row >= CH - s, pltpu.roll(dxn, CH - s, 0), pltpu.roll(dxc, CH - s, 0))
            dx = dx + cw_ref[CONV_W - 1 - s:CONV_W - s, :] * ahead
        dxcn_scr[...] = dxc
        dxg_ref[:, :D_LRU] = jnp.where(valid, dx, 0.0).astype(dxg_ref.dtype)
        dxg_ref[:, D_LRU:] = dg.astype(dxg_ref.dtype)

    vec = _full((1, D_LRU))
    mat = _full((D_LRU, D_LRU))

    def blk(col, shift=0):
        return pl.BlockSpec((CH, D_LRU), lambda i: (jnp.maximum(last - i - shift, 0), col))

    return pl.pallas_call(
        body, name=name, grid=(NCH,),
        in_specs=[blk(0), blk(0, 1), blk(1), blk(0), blk(0, 1), blk(0),
                  _full((CONV_W, D_LRU)), vec, mat, vec, mat, vec, vec, vec],
        out_specs=[pl.BlockSpec((CH, 2 * D_LRU), lambda i: (last - i, 0)), _full((LRU_VEC_ROWS, D_LRU)), mat, mat],
        out_shape=[_sds((T, 2 * D_LRU), MXU_DTYPE), _sds((LRU_VEC_ROWS, D_LRU), F32),
                   _sds((D_LRU, D_LRU), F32), _sds((D_LRU, D_LRU), F32)],
        scratch_shapes=[pltpu.VMEM((CH, D_LRU), F32), pltpu.VMEM((CH, D_LRU), F32), pltpu.VMEM((CH, D_LRU), F32),
                        pltpu.VMEM((1, D_LRU), F32), pltpu.VMEM((CH, D_LRU), F32)],
        compiler_params=_params(("arbitrary",)),
    )(proj, proj, proj, hst, hst, dymix, cw, cb, wa, ba, wx, bx, lam, gain)


def _ret_tables():
    half = HD // 2
    pos = jnp.arange(T, dtype=F32) - float(PAD)
    inv = ROPE_BASE ** (-jnp.arange(half, dtype=F32) / half)
    ang = pos[:, None] * inv[None, :]
    cos = jnp.concatenate([jnp.cos(ang), jnp.cos(ang)], axis=-1)
    sin = jnp.concatenate([-jnp.sin(ang), jnp.sin(ang)], axis=-1)
    log_g = jnp.log(1.0 - 2.0 ** (-5.0 - jnp.arange(HEADS, dtype=F32)))
    idx = jnp.arange(CH, dtype=F32)
    diff = idx[:, None] - idx[None, :]
    dmask = jnp.where(diff[None] >= 0, jnp.exp(jnp.maximum(diff, 0.0)[None] * log_g[:, None, None]), 0.0)
    xi = jnp.exp((idx + 1.0)[None, :] * log_g[:, None])
    zeta = jnp.exp((CH - 1.0 - idx)[None, :] * log_g[:, None])
    xi = jnp.broadcast_to(xi[:, :, None], (HEADS, CH, HD))
    zeta = jnp.broadcast_to(zeta[:, :, None], (HEADS, CH, HD))
    return cos, sin, dmask, xi, zeta


def _chunk_decay():
    log_g = np.log(np.float32(1.0) - np.float32(2.0) ** (np.float32(-5.0) - np.arange(HEADS, dtype=np.float32)))
    return [float(v) for v in np.exp(np.float32(CH) * log_g.astype(np.float32))]


def _rope(x, cos, sin):
    return x * cos + pltpu.roll(x, HD // 2, 1) * sin


def ret_fwd(proj, ylru, tables, gain, name):
    cos, sin, dmask, xi, zeta = tables
    gch = _chunk_decay()
    scale = HD ** -0.5

    def body(q_ref, k_ref, v_ref, g_ref, cos_ref, sin_ref, dm_ref, xi_ref, zt_ref, gain_ref, ylru_ref,
             y_ref, st_ref, s_scr):
        @pl.when(pl.program_id(0) == 0)
        def _():
            s_scr[...] = jnp.zeros_like(s_scr)

        y_ref[:, :D_LRU] = ylru_ref[...]
        cs, sn = cos_ref[...], sin_ref[...]
        for h in range(HEADS):
            sl = slice(HD * h, HD * (h + 1))
            so = slice(D_LRU + HD * h, D_LRU + HD * (h + 1))
            qr = _rope(q_ref[:, sl], cs, sn)
            kr = _rope(k_ref[:, sl], cs, sn) * scale
            v = v_ref[:, sl]
            s = s_scr[h]
            st_ref[h] = s
            sc = _dot(qr, kr, NT) * dm_ref[h]
            y = _dot(sc, v, NN) + _dot(qr, s, NN) * xi_ref[h]
            s_scr[h] = s * gch[h] + _dot(kr * zt_ref[h], v, TN)
            yc = y - jnp.mean(y, axis=-1, keepdims=True)
            yn = yc * lax.rsqrt(jnp.mean(yc * yc, axis=-1, keepdims=True) + EPS)
            y_ref[:, so] = (jax.nn.silu(g_ref[:, sl]) * (yn * gain_ref[:, sl])).astype(y_ref.dtype)

    def col(c):
        return pl.BlockSpec((CH, D_RET), lambda n: (n, c))

    tab = pl.BlockSpec((CH, HD), lambda n: (n, 0))
    cst = _full((HEADS, CH, HD))
    return pl.pallas_call(
        body, name=name, grid=(NCH,),
        in_specs=[col(2), col(3), col(4), col(5), tab, tab, cst, cst, cst, _full((1, D_RET)), col(0)],
        out_specs=[pl.BlockSpec((CH, D), lambda n: (n, 0)), pl.BlockSpec((None, HEADS, HD, HD), lambda n: (n, 0, 0, 0))],
        out_shape=[_sds((T, D), MXU_DTYPE), _sds((NCH, HEADS, HD, HD), F32)],
        scratch_shapes=[pltpu.VMEM((HEADS, HD, HD), F32)],
        compiler_params=_params(("arbitrary",)),
    )(proj, proj, proj, proj, cos, sin, dmask, xi, zeta, gain, ylru)


def ret_bwd(proj, states, dymix, dxg, tables, gain, name):
    cos, sin, dmask, xi, zeta = tables
    gch = _chunk_decay()
    scale = HD ** -0.5
    last = NCH - 1

    def body(q_ref, k_ref, v_ref, g_ref, st_ref, do_ref, cos_ref, sin_ref, dm_ref, xi_ref, zt_ref, gain_ref, dxg_ref,
             dp_ref, dgain_ref, ds_scr):
        @pl.when(pl.program_id(0) == 0)
        def _():
            ds_scr[...] = jnp.zeros_like(ds_scr)
            dgain_ref[...] = jnp.zeros_like(dgain_ref)

        dp_ref[:, :2 * D_LRU] = dxg_ref[...]
        cs, sn = cos_ref[...], sin_ref[...]
        for h in range(HEADS):
            sl = slice(HD * h, HD * (h + 1))
            oq, ok, ov, og = (slice(2 * D_LRU + j * D_RET + HD * h, 2 * D_LRU + j * D_RET + HD * (h + 1)) for j in range(4))
            qr = _rope(q_ref[:, sl], cs, sn)
            kr = _rope(k_ref[:, sl], cs, sn) * scale
            v = v_ref[:, sl]
            g = g_ref[:, sl]
            gain = gain_ref[:, sl]
            dm, x_i, zt = dm_ref[h], xi_ref[h], zt_ref[h]
            s = st_ref[h]
            ds = ds_scr[h]
            kz = kr * zt
            sc = _dot(qr, kr, NT) * dm
            y = _dot(sc, v, NN) + _dot(qr, s, NN) * x_i
            yc = y - jnp.mean(y, axis=-1, keepdims=True)
            rstd = lax.rsqrt(jnp.mean(yc * yc, axis=-1, keepdims=True) + EPS)
            yn = yc * rstd
            sg = jax.nn.sigmoid(g)
            silu = g * sg
            dout = do_ref[:, sl].astype(F32)
            dgain_ref[:, sl] += jnp.sum(dout * silu * yn, axis=0, keepdims=True)
            dp_ref[:, og] = (dout * yn * gain * (sg * (1.0 + g * (1.0 - sg)))).astype(dp_ref.dtype)
            dyn = dout * silu * gain
            dy = rstd * (dyn - jnp.mean(dyn, axis=-1, keepdims=True) - yn * jnp.mean(dyn * yn, axis=-1, keepdims=True))
            dp = _dot(dy, v, NT) * dm
            dv = _dot(sc, dy, TN) + _dot(kz, ds, NN)
            dqs = dy * x_i
            dqr = _dot(dp, kr, NN) + _dot(dqs, s, NT)
            dkr = _dot(dp, qr, TN) + _dot(v, ds, NT) * zt
            ds_scr[h] = gch[h] * ds + _dot(qr, dqs, TN)
            dp_ref[:, oq] = (dqr * cs + pltpu.roll(dqr * sn, HD // 2, 1)).astype(dp_ref.dtype)
            dp_ref[:, ok] = ((dkr * cs + pltpu.roll(dkr * sn, HD // 2, 1)) * scale).astype(dp_ref.dtype)
            dp_ref[:, ov] = dv.astype(dp_ref.dtype)

    def col(c):
        return pl.BlockSpec((CH, D_RET), lambda n: (last - n, c))

    tab = pl.BlockSpec((CH, HD), lambda n: (last - n, 0))
    cst = _full((HEADS, CH, HD))
    return pl.pallas_call(
        body, name=name, grid=(NCH,),
        in_specs=[col(2), col(3), col(4), col(5), pl.BlockSpec((None, HEADS, HD, HD), lambda n: (last - n, 0, 0, 0)), col(1),
                  tab, tab, cst, cst, cst, _full((1, D_RET)), pl.BlockSpec((CH, 2 * D_LRU), lambda n: (last - n, 0))],
        out_specs=[pl.BlockSpec((CH, D_IN), lambda n: (last - n, 0)), _full((1, D_RET))],
        out_shape=[_sds((T, D_IN), MXU_DTYPE), _sds((1, D_RET), F32)],
        scratch_shapes=[pltpu.VMEM((HEADS, HD, HD), F32)],
        compiler_params=_params(("arbitrary",)),
    )(proj, proj, proj, proj, states, dymix, cos, sin, dmask, xi, zeta, gain, dxg)


HBM = pl.BlockSpec(memory_space=pltpu.HBM)


def _place():
    return lax.axis_index("x"), lax.axis_index("y"), lax.axis_index("c")


def all_gather(arrs, name):
    n = len(arrs)

    def body(*refs):
        ins, outs = refs[:n], refs[n:2 * n]
        send_sems, recv_sems, local_sems = refs[2 * n:]
        x, y, c = _place()
        me, sibling = (x, y, c), (x, y, 1 - c)
        chips = [(1 - x, y), (x, 1 - y), (1 - x, 1 - y)]

        def copy(a, k, block, to, src=None):
            px, py, pc = block
            dst = outs[a].at[4 * px + 2 * py + pc]
            return pltpu.make_async_remote_copy(
                src_ref=dst if src is None else src, dst_ref=dst, send_sem=send_sems.at[a, k], recv_sem=recv_sems.at[a, k],
                device_id=to, device_id_type=MESH)

        mine = [pltpu.make_async_copy(ins[a], outs[a].at[4 * x + 2 * y + c], local_sems.at[a]) for a in range(n)]
        for cp in mine:
            cp.start()
        first = []
        for a in range(n):
            first.append(copy(a, 0, me, sibling, src=ins[a]))
            first += [copy(a, 1 + j, me, (*chip, c), src=ins[a]) for j, chip in enumerate(chips)]
        for cp in first:
            cp.start()
        passed = []
        for j, chip in enumerate(chips):
            for a in range(n):
                copy(a, 1 + j, (*chip, c), me).wait_recv()
                passed.append(copy(a, 4 + j, (*chip, c), sibling))
                passed[-1].start()
        for a in range(n):
            copy(a, 0, sibling, me).wait_recv()
            for j, chip in enumerate(chips):
                copy(a, 4 + j, (*chip, 1 - c), me).wait_recv()
        for cp in first + passed:
            cp.wait_send()
        for cp in mine:
            cp.wait()

    return pl.pallas_call(
        body, name=name,
        in_specs=[HBM] * n, out_specs=[HBM] * n,
        out_shape=[_sds((NDEV,) + a.shape, a.dtype) for a in arrs],
        scratch_shapes=[pltpu.SemaphoreType.DMA((n, 7)), pltpu.SemaphoreType.DMA((n, 7)), pltpu.SemaphoreType.DMA((n,))],
    )(*arrs)


SEM = pl.BlockSpec(memory_space=pltpu.SEMAPHORE)
ANY = pl.BlockSpec(memory_space=pl.ANY)
EFFECT = pltpu.SideEffectType.DATAFLOW_SIDE_EFFECTING


def _hbm(a):
    return pltpu.with_memory_space_constraint(a, pltpu.HBM)


def _hbm_like(arrs):
    return [pltpu.HBM(a.shape, a.dtype) for a in arrs]


def _dma_sems(count):
    return [pltpu.SemaphoreType.DMA(())] * count


def _ag_copy(lands, send_sems, recv_sems, per):
    def copy(a, k, block, to, src=None):
        px, py, pc = block
        dst = lands[a].at[4 * px + 2 * py + pc]
        return pltpu.make_async_remote_copy(
            src_ref=dst if src is None else src, dst_ref=dst, send_sem=send_sems[a * per + k], recv_sem=recv_sems[a * per + k],
            device_id=to, device_id_type=MESH)
    return copy


def to_wire(sel, w_in, w_gate, w_up, w_out, w_down, name):
    ffpad = FF_SHP - FF_SH

    def body(sel_ref, i_ref, g_ref, u_ref, o_ref, d_ref, oi, og, ou, oo, od):
        del sel_ref
        oi[...] = i_ref[...].astype(oi.dtype)
        oo[...] = o_ref[...].astype(oo.dtype)
        for src, dst in ((g_ref, og), (u_ref, ou)):
            dst[:, :FF_SH] = src[...].astype(dst.dtype)
            dst[:, FF_SH:] = jnp.zeros((D, ffpad), dst.dtype)
        od[:FF_SH, :] = d_ref[...].astype(od.dtype)
        od[FF_SH:, :] = jnp.zeros((ffpad, D), od.dtype)

    shapes_in = [(D, IN_SH), (D, FF_SH), (D, FF_SH), (OUT_SH, D), (FF_SH, D)]
    shapes_out = [(D, IN_SH), (D, FF_SHP), (D, FF_SHP), (OUT_SH, D), (FF_SHP, D)]
    return pl.pallas_call(
        body, name=name,
        grid_spec=pltpu.PrefetchScalarGridSpec(
            num_scalar_prefetch=1, grid=(1,),
            in_specs=[pl.BlockSpec((None,) + s, lambda i, sel_ref: (sel_ref[1], 0, 0)) for s in shapes_in],
            out_specs=[pl.BlockSpec((None,) + s, lambda i, sel_ref: (sel_ref[0], 0, 0)) for s in shapes_out]),
        out_shape=[_sds((NDEV,) + s, WIRE_DTYPE) for s in shapes_out], compiler_params=_params(("arbitrary",)),
    )(sel, w_in, w_gate, w_up, w_out, w_down)


def ag_start(lands, after, name):
    n = len(lands)
    ns = 4 * n

    def body(*refs):
        lnd = refs[:n]
        send_sems, recv_sems = refs[n + 1:n + 1 + ns], refs[n + 1 + ns:n + 1 + 2 * ns]
        token = refs[-1]
        x, y, c = _place()
        me, sibling = (x, y, c), (x, y, 1 - c)
        chips = [(1 - x, y), (x, 1 - y), (1 - x, 1 - y)]
        copy = _ag_copy(lnd, send_sems, recv_sems, 4)
        for a in range(n):
            copy(a, 0, me, sibling).start()
            for j, chip in enumerate(chips):
                copy(a, 1 + j, me, (*chip, c)).start()
        token[...] = jnp.zeros_like(token)

    outs = pl.pallas_call(
        body, name=name,
        in_specs=[HBM] * n + [ANY],
        out_specs=[SEM] * (2 * ns) + [HBM] * n + [pl.BlockSpec(memory_space=pltpu.VMEM)],
        out_shape=_dma_sems(2 * ns) + _hbm_like(lands) + [_sds((8, 128), F32)],
        input_output_aliases={i: 2 * ns + i for i in range(n)},
        compiler_params=pltpu.CompilerParams(has_side_effects=EFFECT),
    )(*[_hbm(a) for a in lands], after)
    return outs[:ns], outs[ns:2 * ns], outs[2 * ns:2 * ns + n], outs[-1]


def ag_forward(send_sems, recv_sems, lands, after, name):
    n = len(lands)
    n1, n2 = 4 * n, 3 * n

    def body(*refs):
        lnd = refs[:n]
        o = n
        s1, r1 = refs[o:o + n1], refs[o + n1:o + 2 * n1]
        o += 2 * n1 + 1
        s2, r2 = refs[o:o + n2], refs[o + n2:o + 2 * n2]
        token = refs[-1]
        token[...] = jnp.zeros_like(token)
        x, y, c = _place()
        me, sibling = (x, y, c), (x, y, 1 - c)
        chips = [(1 - x, y), (x, 1 - y), (1 - x, 1 - y)]
        copy1 = _ag_copy(lnd, s1, r1, 4)
        copy2 = _ag_copy(lnd, s2, r2, 3)
        for j, chip in enumerate(chips):
            for a in range(n):
                copy1(a, 1 + j, (*chip, c), me).wait_recv()
                copy2(a, j, (*chip, c), sibling).start()
        for a in range(n):
            copy1(a, 0, sibling, me).wait_recv()
            copy1(a, 0, me, sibling).wait_send()
            for j, chip in enumerate(chips):
                copy1(a, 1 + j, me, (*chip, c)).wait_send()

    outs = pl.pallas_call(
        body, name=name,
        in_specs=[HBM] * n + [SEM] * (2 * n1) + [ANY],
        out_specs=[SEM] * (2 * n2) + [HBM] * n + [pl.BlockSpec(memory_space=pltpu.VMEM)],
        out_shape=_dma_sems(2 * n2) + _hbm_like(lands) + [_sds((8, 128), F32)],
        input_output_aliases={i: 2 * n2 + i for i in range(n)},
        compiler_params=pltpu.CompilerParams(has_side_effects=EFFECT),
    )(*lands, *send_sems, *recv_sems, after)
    return outs[:n2], outs[n2:2 * n2], outs[2 * n2:2 * n2 + n], outs[-1]


def ag_finish(send_sems, recv_sems, lands, after, name):
    n = len(lands)
    n2 = 3 * n

    def body(*refs):
        lnd = refs[:n]
        s2, r2 = refs[n:n + n2], refs[n + n2:n + 2 * n2]
        x, y, c = _place()
        me, sibling = (x, y, c), (x, y, 1 - c)
        chips = [(1 - x, y), (x, 1 - y), (1 - x, 1 - y)]
        copy2 = _ag_copy(lnd, s2, r2, 3)
        for a in range(n):
            for j, chip in enumerate(chips):
                copy2(a, j, (*chip, c), sibling).wait_send()
                copy2(a, j, (*chip, 1 - c), me).wait_recv()

    outs = pl.pallas_call(
        body, name=name,
        in_specs=[HBM] * n + [SEM] * (2 * n2) + [ANY],
        out_specs=[HBM] * n, out_shape=_hbm_like(lands),
        input_output_aliases={i: i for i in range(n)},
        compiler_params=pltpu.CompilerParams(has_side_effects=EFFECT),
    )(*lands, *send_sems, *recv_sems, after)
    return list(outs)


def rs_sibling_start(arrs, name):
    n = len(arrs)
    ns = 4 * n
    lands = [lax.empty((4,) + a.shape[1:], a.dtype) for a in arrs]

    def body(*refs):
        ins, lnd = refs[:n], refs[n:2 * n]
        send_sems, recv_sems = refs[2 * n:2 * n + ns], refs[2 * n + ns:2 * n + 2 * ns]
        x, y, c = _place()
        sibling = (x, y, 1 - c)
        for a in range(n):
            for p in range(4):
                pltpu.make_async_remote_copy(
                    src_ref=ins[a].at[2 * p + 1 - c], dst_ref=lnd[a].at[p], send_sem=send_sems[4 * a + p],
                    recv_sem=recv_sems[4 * a + p], device_id=sibling, device_id_type=MESH).start()
        refs[-1][...] = jnp.zeros_like(refs[-1])

    outs = pl.pallas_call(
        body, name=name,
        in_specs=[HBM] * (2 * n), out_specs=[SEM] * (2 * ns) + [HBM] * (2 * n) + [pl.BlockSpec(memory_space=pltpu.VMEM)],
        out_shape=_dma_sems(2 * ns) + _hbm_like(arrs) + _hbm_like(lands) + [_sds((8, 128), F32)],
        input_output_aliases={i: 2 * ns + i for i in range(2 * n)},
        compiler_params=pltpu.CompilerParams(has_side_effects=EFFECT),
    )(*[_hbm(a) for a in arrs], *[_hbm(a) for a in lands])
    return (outs[:ns], outs[ns:2 * ns], outs[2 * ns:2 * ns + n], outs[2 * ns + n:2 * ns + 2 * n]), outs[-1]


def rs_sibling_wait(send_sems, recv_sems, arrs, lands, after, name):
    n = len(arrs)
    ns = 4 * n

    def body(*refs):
        ins, lnd = refs[:n], refs[n:2 * n]
        s, r = refs[2 * n:2 * n + ns], refs[2 * n + ns:2 * n + 2 * ns]
        x, y, c = _place()
        sibling = (x, y, 1 - c)
        for a in range(n):
            for p in range(4):
                cp = pltpu.make_async_remote_copy(
                    src_ref=ins[a].at[2 * p + 1 - c], dst_ref=lnd[a].at[p], send_sem=s[4 * a + p], recv_sem=r[4 * a + p],
                    device_id=sibling, device_id_type=MESH)
                cp.wait_send()
                cp.wait_recv()

    outs = pl.pallas_call(
        body, name=name,
        in_specs=[HBM] * (2 * n) + [SEM] * (2 * ns) + [ANY], out_specs=[HBM] * (2 * n),
        out_shape=_hbm_like(arrs) + _hbm_like(lands),
        input_output_aliases={i: i for i in range(2 * n)},
        compiler_params=pltpu.CompilerParams(has_side_effects=EFFECT),
    )(*arrs, *lands, *send_sems, *recv_sems, after)
    return outs[:n], outs[n:]


def rs_chips_start(parts, name):
    n = len(parts)
    ns = 3 * n
    lands = [lax.empty((3,) + a.shape[1:], a.dtype) for a in parts]

    def body(*refs):
        ins, lnd = refs[:n], refs[n:2 * n]
        send_sems, recv_sems = refs[2 * n:2 * n + ns], refs[2 * n + ns:2 * n + 2 * ns]
        x, y, c = _place()
        chips = [(1 - x, y), (x, 1 - y), (1 - x, 1 - y)]
        for a in range(n):
            for k, (tx, ty) in enumerate(chips):
                pltpu.make_async_remote_copy(
                    src_ref=ins[a].at[2 * tx + ty], dst_ref=lnd[a].at[k], send_sem=send_sems[3 * a + k],
                    recv_sem=recv_sems[3 * a + k], device_id=(tx, ty, c), device_id_type=MESH).start()
        refs[-1][...] = jnp.zeros_like(refs[-1])

    outs = pl.pallas_call(
        body, name=name,
        in_specs=[HBM] * (2 * n), out_specs=[SEM] * (2 * ns) + [HBM] * (2 * n) + [pl.BlockSpec(memory_space=pltpu.VMEM)],
        out_shape=_dma_sems(2 * ns) + _hbm_like(parts) + _hbm_like(lands) + [_sds((8, 128), F32)],
        input_output_aliases={i: 2 * ns + i for i in range(2 * n)},
        compiler_params=pltpu.CompilerParams(has_side_effects=EFFECT),
    )(*[_hbm(a) for a in parts], *[_hbm(a) for a in lands])
    return (outs[:ns], outs[ns:2 * ns], outs[2 * ns:2 * ns + n], outs[2 * ns + n:2 * ns + 2 * n]), outs[-1]


def rs_chips_wait(send_sems, recv_sems, parts, lands, after, name):
    n = len(parts)
    ns = 3 * n

    def body(*refs):
        ins, lnd = refs[:n], refs[n:2 * n]
        s, r = refs[2 * n:2 * n + ns], refs[2 * n + ns:2 * n + 2 * ns]
        x, y, c = _place()
        chips = [(1 - x, y), (x, 1 - y), (1 - x, 1 - y)]
        for a in range(n):
            for k, (tx, ty) in enumerate(chips):
                cp = pltpu.make_async_remote_copy(
                    src_ref=ins[a].at[2 * tx + ty], dst_ref=lnd[a].at[k], send_sem=s[3 * a + k], recv_sem=r[3 * a + k],
                    device_id=(tx, ty, c), device_id_type=MESH)
                cp.wait_send()
                cp.wait_recv()

    outs = pl.pallas_call(
        body, name=name,
        in_specs=[HBM] * (2 * n) + [SEM] * (2 * ns) + [ANY], out_specs=[HBM] * (2 * n),
        out_shape=_hbm_like(parts) + _hbm_like(lands),
        input_output_aliases={i: i for i in range(2 * n)},
        compiler_params=pltpu.CompilerParams(has_side_effects=EFFECT),
    )(*parts, *lands, *send_sems, *recv_sems, after)
    return outs[:n], outs[n:]


def pair_sum(a, r, c, name):
    _, rr, cc = a.shape

    def body(c_ref, a_ref, r_ref, o_ref):
        del c_ref
        o_ref[...] = (a_ref[...].astype(F32) + r_ref[...].astype(F32)).astype(o_ref.dtype)

    return pl.pallas_call(
        body, name=name,
        grid_spec=pltpu.PrefetchScalarGridSpec(
            num_scalar_prefetch=1, grid=(4,),
            in_specs=[pl.BlockSpec((None, rr, cc), lambda p, c_ref: (2 * p + c_ref[0], 0, 0)),
                      pl.BlockSpec((None, rr, cc), lambda p, c_ref: (p, 0, 0))],
            out_specs=pl.BlockSpec((None, rr, cc), lambda p, c_ref: (p, 0, 0))),
        out_shape=_sds((4, rr, cc), a.dtype), compiler_params=_params(("parallel",)),
    )(c, a, r)


def _adamw(w, g, m, v):
    m = ADAM_B1 * m + (1.0 - ADAM_B1) * g
    v = ADAM_B2 * v + (1.0 - ADAM_B2) * jnp.square(g)
    m_hat = m / (1.0 - ADAM_B1 ** ADAM_STEP)
    v_hat = v / (1.0 - ADAM_B2 ** ADAM_STEP)
    return -ADAM_LR * (m_hat / (jnp.sqrt(v_hat) + ADAM_EPS) + ADAM_WD * w), m, v


def adamw_big(recv, sums, chip, w, m, v, tr, name):
    nl, rr, cc = w.shape
    cp = recv[0].shape[2]

    def body(chip_ref, *refs):
        del chip_ref
        rcv, own = refs[:nl], refs[nl:2 * nl]
        w_ref, m_ref, v_ref, g_out, d_out, m_out, v_out = refs[2 * nl:]
        for l in range(nl):
            g = ((own[l][...].astype(F32) + rcv[l][0].astype(F32)) + rcv[l][1].astype(F32)) + rcv[l][2].astype(F32)
            g = g[:, :cc]
            g_out[l] = g
            d_out[l], m_out[l], v_out[l] = _adamw(w_ref[l], g, m_ref[l], v_ref[l])

    blk = pl.BlockSpec((nl, tr, cc), lambda i, chip_ref: (0, i, 0))
    return pl.pallas_call(
        body, name=name,
        grid_spec=pltpu.PrefetchScalarGridSpec(
            num_scalar_prefetch=1, grid=(rr // tr,),
            in_specs=[pl.BlockSpec((3, tr, cp), lambda i, chip_ref: (0, i, 0))] * nl
            + [pl.BlockSpec((None, tr, cp), lambda i, chip_ref: (chip_ref[0], i, 0))] * nl + [blk, blk, blk],
            out_specs=[blk] * 4),
        out_shape=[_sds(w.shape, F32)] * 4, compiler_params=_params(("parallel",)),
    )(chip, *recv, *sums, w, m, v)


def sum_devices(g, name):
    _, rr, cc = g.shape

    def body(g_ref, o_ref):
        acc = g_ref[0]
        for j in range(1, NDEV):
            acc = acc + g_ref[j]
        o_ref[...] = acc

    return pl.pallas_call(
        body, name=name, grid=(1,), in_specs=[_full(g.shape)], out_specs=_full((rr, cc)), out_shape=_sds((rr, cc), F32),
        compiler_params=_params(("arbitrary",)),
    )(g)


def adamw_rows(g, w, m, v, name):
    rr, cc = w.shape

    def body(g_ref, w_ref, m_ref, v_ref, d_out, m_out, v_out):
        d_out[...], m_out[...], v_out[...] = _adamw(w_ref[...], g_ref[...], m_ref[...], v_ref[...])

    blk = _full((rr, cc))
    return pl.pallas_call(
        body, name=name, grid=(1,), in_specs=[blk] * 4, out_specs=[blk] * 3, out_shape=[_sds((rr, cc), F32)] * 3,
        compiler_params=_params(("arbitrary",)),
    )(g, w, m, v)


def _block_diag(w):
    eye = jnp.eye(LRU_BLOCKS, dtype=w.dtype)
    return (w[:, :, None, :] * eye[:, None, :, None]).reshape(D_LRU, D_LRU)


def _diag_blocks(wd):
    w4 = wd.reshape(LRU_BLOCKS, LRU_BD, LRU_BLOCKS, LRU_BD)
    return jnp.stack([w4[g, :, g, :] for g in range(LRU_BLOCKS)])


def _pack(arrs):
    flat = jnp.concatenate([a.reshape(-1) for a in arrs])
    return flat.reshape(-1, 128)


def _unpack(packed, shapes):
    flat = packed.reshape(-1)
    out, o = [], 0
    for s in shapes:
        n = int(np.prod(s))
        out.append(flat[o:o + n].reshape(s))
        o += n
    return out


REP_NAMES = ["norm_mix", "conv_b", "gate_a_w", "gate_a_b", "gate_x_w", "gate_x_b", "lru_lambda", "lru_out_norm",
             "ret_out_norm", "norm_ffn", "norm_final"]


def kernel(x, meta_tokens, norm_mix, w_in, conv_w, conv_b, gate_a_w, gate_a_b, gate_x_w, gate_x_b, lru_lambda, lru_out_norm, ret_out_norm, w_out, norm_ffn, w_gate, w_up, w_down, norm_final, loss_target, m_meta_tokens, m_norm_mix, m_w_in, m_conv_w, m_conv_b, m_gate_a_w, m_gate_a_b, m_gate_x_w, m_gate_x_b, m_lru_lambda, m_lru_out_norm, m_ret_out_norm, m_w_out, m_norm_ffn, m_w_gate, m_w_up, m_w_down, m_norm_final, v_meta_tokens, v_norm_mix, v_w_in, v_conv_w, v_conv_b, v_gate_a_w, v_gate_a_b, v_gate_x_w, v_gate_x_b, v_lru_lambda, v_lru_out_norm, v_ret_out_norm, v_w_out, v_norm_ffn, v_w_gate, v_w_up, v_w_down, v_norm_final):
    xi, yi, ci = _place()
    dev = 4 * xi + 2 * yi + ci
    c_arr = jnp.reshape(ci, (1,)).astype(jnp.int32)

    level1 = []
    token = c_arr
    for l in range(DEPTH):
        sel = jnp.stack([dev, jnp.int32(l)]).astype(jnp.int32)
        lands = to_wire(sel, w_in, w_gate, w_up, w_out, w_down, "to_wire")
        s1, r1, lands, token = ag_start(lands, token, f"ag_start_{l}")
        level1.append((s1, r1, lands))
    meta_g, conv_g = all_gather([meta_tokens, conv_w], "ag_small")
    meta_full = jnp.transpose(meta_g, (1, 0, 2)).reshape(N_META, D)
    conv_full = jnp.transpose(conv_g, (1, 2, 0, 3)).reshape(DEPTH, CONV_W, D_LRU)

    def as_weights(gi, gg, gu, go, gd):
        return dict(w_in=gi, w_gate=gg, w_up=gu, w_out=go.reshape(D, D), w_down=gd.reshape(D_FFP, D))

    tables = _ret_tables()
    row = lambda a: a.reshape(1, -1)

    h = jnp.concatenate([jnp.zeros((PAD, D), F32), meta_full, x[0]], axis=0)
    saved, gathered = [], []
    s1, r1, lands = level1[0]
    s2, r2, lands, order = ag_forward(s1, r1, lands, token, "ag_forward_0")
    w = as_weights(*ag_finish(s2, r2, lands, h, "ag_finish_0"))
    for l in range(DEPTH):
        gathered.append(w)
        small = dict(cw=conv_full[l], cb=row(conv_b[l]), wa=_block_diag(gate_a_w[l]).astype(MXU_DTYPE), ba=row(gate_a_b[l]),
                     wx=_block_diag(gate_x_w[l]).astype(MXU_DTYPE), bx=row(gate_x_b[l]), lam=row(lru_lambda[l]),
                     gain=row(lru_out_norm[l]))
        hn1 = rmsnorm_fwd(h, row(norm_mix[l]), "rms_fwd")
        proj = mm_blocked_nn(hn1, w["w_in"], F32, "proj")
        ylru, hst = lru_fwd(proj, name="lru_fwd", **small)
        ymix, states = ret_fwd(proj, ylru, tables, row(ret_out_norm[l]), "ret_fwd")
        h_mid = mm_nn_res(ymix, w["w_out"], h, order, "out_proj")
        hn2 = rmsnorm_fwd(h_mid, row(norm_ffn[l]), "rms_fwd")
        gate, up, act = ffn_up(hn2, w["w_gate"], w["w_up"], "ffn_up")
        if l + 1 < DEPTH:
            s1, r1, lands = level1[l + 1]
            s2, r2, lands, order = ag_forward(s1, r1, lands, act, f"ag_forward_{l + 1}")
        h_out = mm_nn_res(act, w["w_down"], h_mid, order, "ffn_down")
        if l + 1 < DEPTH:
            w_next = as_weights(*ag_finish(s2, r2, lands, h_out, f"ag_finish_{l + 1}"))
        saved.append(dict(h=h, hn1=hn1, proj=proj, hst=hst, states=states, ymix=ymix, h_mid=h_mid, hn2=hn2, gate=gate, up=up,
                          act=act, small=small))
        h = h_out
        if l + 1 < DEPTH:
            w = w_next

    loss_p, dh, g_norm_final = loss_head(h, row(norm_final), loss_target[0], "loss_head")
    loss = lax.psum(loss_p[0, 0], ("x", "y", "c"))

    rep = [None] * DEPTH
    convw_g = [None] * DEPTH
    chip_sums = [None] * DEPTH
    inflight = [None] * DEPTH
    sib = None
    order = loss_p

    def sibling_done(l, sib, after):
        parts, got = rs_sibling_wait(*sib, after, f"rs_sibling_wait_{l}")
        sums = [pair_sum(a, r, c_arr, "pair_sum") for a, r in zip(parts, got)]
        inflight[l], started = rs_chips_start(sums, f"rs_chips_start_{l}")
        return started

    for l in reversed(range(DEPTH)):
        w, s = gathered[l], saved[l]
        dgate, dup = ffn_down_bwd(dh, w["w_down"], s["gate"], s["up"], order, "ffn_down_bwd")
        dwd = mm_tn(s["act"], dh, FF_SHP, "dw_down")
        dwg = mm_tn_blocked(s["hn2"], dgate, "dw_blocked")
        dwu = mm_tn_blocked(s["hn2"], dup, "dw_blocked")
        dhn2 = mm_blocked_nt([(dgate, w["w_gate"]), (dup, w["w_up"])], "ffn_up_bwd")
        if sib is not None:
            order = sibling_done(l + 1, sib, dhn2)
        dh_mid, g_norm_ffn = rmsnorm_bwd(s["h_mid"], row(norm_ffn[l]), dhn2, dh, "rms_bwd")
        dymix = mm_nt(dh_mid, w["w_out"], F32, order, "out_proj_bwd")
        dwo = mm_tn(s["ymix"], dh_mid, 512, "dw_out")
        dxg, lvec, dwa, dwx = lru_bwd(s["proj"], s["hst"], dymix, name="lru_bwd", **s["small"])
        dproj, g_ret_norm = ret_bwd(s["proj"], s["states"], dymix, dxg, tables, row(ret_out_norm[l]), "ret_bwd")
        dwi = mm_tn_blocked(s["hn1"], dproj, "dw_blocked")
        dhn1 = mm_blocked_nt([(dproj, w["w_in"])], "proj_bwd")
        dh, g_norm_mix = rmsnorm_bwd(s["h"], row(norm_mix[l]), dhn1, dh_mid, "rms_bwd")

        rep[l] = [g_norm_mix, lvec[4], _diag_blocks(dwa), lvec[5], _diag_blocks(dwx), lvec[6], lvec[7], lvec[8], g_ret_norm,
                  g_norm_ffn]
        convw_g[l] = lvec[0:CONV_W]
        parts = [dwi, dwg, dwu, dwo.reshape(NDEV, OUT_SH, D), dwd.reshape(NDEV, FF_SHP, D)]
        sib, order = rs_sibling_start(parts, f"rs_sibling_start_{l}")
    sibling_done(0, sib, dh)

    grad_x = dh[X0:][None]
    g_meta = dh[PAD:X0]

    rep_shapes = [(D,), (D_LRU,), (LRU_BLOCKS, LRU_BD, LRU_BD), (LRU_BLOCKS, LRU_BD), (LRU_BLOCKS, LRU_BD, LRU_BD),
                  (LRU_BLOCKS, LRU_BD), (D_LRU,), (D_LRU,), (D_RET,), (D,)]
    flat = [a for l in range(DEPTH) for a in rep[l]] + [g_norm_final] + [convw_g[l] for l in range(DEPTH)] + [g_meta]
    (gath,) = all_gather([_pack(flat)], "ag_grads")
    gsum = sum_devices(gath, "sum_devices")
    shapes = rep_shapes * DEPTH + [(D,)] + [(CONV_W, D_LRU)] * DEPTH + [(N_META, D)]
    parts = _unpack(gsum, shapes)
    nrep = len(rep_shapes)
    g_rep = {n: jnp.stack([parts[l * nrep + i] for l in range(DEPTH)]) for i, n in enumerate(REP_NAMES[:-1])}
    g_rep["norm_final"] = parts[DEPTH * nrep]
    g_convw = lax.dynamic_slice_in_dim(jnp.stack(parts[DEPTH * nrep + 1:DEPTH * nrep + 1 + DEPTH]), dev * (D_LRU // NDEV),
                                       D_LRU // NDEV, axis=2)
    g_metatok = lax.dynamic_slice_in_dim(parts[-1], dev * (D // NDEV), D // NDEV, axis=1)

    given = dict(norm_mix=(norm_mix, m_norm_mix, v_norm_mix), conv_b=(conv_b, m_conv_b, v_conv_b),
                 gate_a_w=(gate_a_w, m_gate_a_w, v_gate_a_w), gate_a_b=(gate_a_b, m_gate_a_b, v_gate_a_b),
                 gate_x_w=(gate_x_w, m_gate_x_w, v_gate_x_w), gate_x_b=(gate_x_b, m_gate_x_b, v_gate_x_b),
                 lru_lambda=(lru_lambda, m_lru_lambda, v_lru_lambda), lru_out_norm=(lru_out_norm, m_lru_out_norm, v_lru_out_norm),
                 ret_out_norm=(ret_out_norm, m_ret_out_norm, v_ret_out_norm), norm_ffn=(norm_ffn, m_norm_ffn, v_norm_ffn),
                 norm_final=(norm_final, m_norm_final, v_norm_final),
                 conv_w=(conv_w, m_conv_w, v_conv_w), meta_tokens=(meta_tokens, m_meta_tokens, v_meta_tokens))
    small_names = REP_NAMES + ["conv_w", "meta_tokens"]
    small_g = dict(g_rep, conv_w=g_convw, meta_tokens=g_metatok)
    small_shapes = [given[n][0].shape for n in small_names]
    packs = [_pack([small_g[n] for n in small_names])] + [_pack([given[n][k] for n in small_names]) for k in range(3)]
    upd = adamw_rows(*packs, "adamw_small")
    small_out = [dict(zip(small_names, _unpack(p, small_shapes))) for p in upd]

    recv = [None] * DEPTH
    for l in reversed(range(DEPTH)):
        chip_sums[l], recv[l] = rs_chips_wait(*inflight[l], upd[0], f"rs_chips_wait_{l}")
    chip = jnp.reshape(2 * xi + yi, (1,)).astype(jnp.int32)

    def finish(idx, wname, w_, m_, v_, tr):
        return adamw_big([recv[l][idx] for l in range(DEPTH)], [chip_sums[l][idx] for l in range(DEPTH)], chip, w_, m_, v_, tr,
                         "adamw_" + wname)

    o_in = finish(0, "w_in", w_in, m_w_in, v_w_in, 256)
    o_gate = finish(1, "w_gate", w_gate, m_w_gate, v_w_gate, 256)
    o_up = finish(2, "w_up", w_up, m_w_up, v_w_up, 256)
    o_out = finish(3, "w_out", w_out, m_w_out, v_w_out, 64)
    o_down = finish(4, "w_down", w_down, m_w_down, v_w_down, 32)

    bigs = dict(w_in=o_in, w_out=o_out, w_gate=o_gate, w_up=o_up, w_down=o_down)
    order = ["meta_tokens", "norm_mix", "w_in", "conv_w", "conv_b", "gate_a_w", "gate_a_b", "gate_x_w", "gate_x_b", "lru_lambda",
             "lru_out_norm", "ret_out_norm", "w_out", "norm_ffn", "w_gate", "w_up", "w_down", "norm_final"]
    grads = [bigs[n][0] if n in bigs else small_g[n] for n in order]
    rest = [[bigs[n][k + 1] if n in bigs else small_out[k][n] for n in order] for k in range(3)]
    return (loss, grad_x, *grads, *rest[0], *rest[1], *rest[2])
```

```python
import functools

import numpy as np
import jax
import jax.numpy as jnp
from jax import lax
from jax.experimental import pallas as pl
from jax.experimental.pallas import tpu as pltpu

F32, BF16 = jnp.float32, jnp.bfloat16
MXU_DTYPE = BF16
WIRE_DTYPE = BF16

D = 1024
SEQ = 2048
DEPTH = 4
N_META = 16
CH = 128
PAD = (-(SEQ + N_META)) % CH
T = SEQ + N_META + PAD
NCH = T // CH
X0 = PAD + N_META
D_LRU = 512
LRU_BLOCKS = 8
LRU_BD = 64
CONV_W = 4
LRU_C = 8.0
D_RET = 512
HEADS = 4
HD = 128
ROPE_BASE = 10000.0
D_IN = 3072
D_FF = 2816
NDEV = 8
IN_SH = D_IN // NDEV
FF_SH = D_FF // NDEV
FF_SHP = 384
D_FFP = NDEV * FF_SHP
OUT_SH = D // NDEV
EPS = 1e-6
TM = 544
VMEM_LIMIT = 56 * 2**20
MESH = pl.DeviceIdType.MESH

ADAM_LR, ADAM_B1, ADAM_B2, ADAM_EPS, ADAM_WD, ADAM_STEP = 0.001, 0.9, 0.999, 1e-08, 0.01, 10

NN = ((1,), (0,))
NT = ((1,), (1,))
TN = ((0,), (0,))


def _dot(a, b, dims):
    return lax.dot_general(a.astype(MXU_DTYPE), b.astype(MXU_DTYPE), (dims, ((), ())), preferred_element_type=F32)


def _sds(shape, dtype):
    return jax.ShapeDtypeStruct(shape, dtype)


def _params(sem=None):
    return pltpu.CompilerParams(dimension_semantics=sem, vmem_limit_bytes=VMEM_LIMIT)


def _full(shape):
    n = len(shape)
    return pl.BlockSpec(shape, lambda *_: (0,) * n)


def rmsnorm_fwd(h, gain, name):
    def body(h_ref, g_ref, o_ref):
        x = h_ref[...]
        ms = jnp.mean(x * x, axis=-1, keepdims=True)
        o_ref[...] = (x * lax.rsqrt(ms + EPS) * g_ref[...]).astype(o_ref.dtype)

    return pl.pallas_call(
        body, name=name, grid=(T // TM,),
        in_specs=[pl.BlockSpec((TM, D), lambda i: (i, 0)), _full((1, D))],
        out_specs=pl.BlockSpec((TM, D), lambda i: (i, 0)),
        out_shape=_sds((T, D), MXU_DTYPE), compiler_params=_params(("parallel",)),
    )(h, gain)


def rmsnorm_bwd(h, gain, dhn, dres, name):
    def body(h_ref, g_ref, dhn_ref, dres_ref, dh_ref, dg_ref):
        x = h_ref[...]
        rstd = lax.rsqrt(jnp.mean(x * x, axis=-1, keepdims=True) + EPS)
        xhat = x * rstd
        dy = dhn_ref[...]
        dyg = dy * g_ref[...]
        dh_ref[...] = dres_ref[...] + rstd * (dyg - xhat * jnp.mean(dyg * xhat, axis=-1, keepdims=True))

        @pl.when(pl.program_id(0) == 0)
        def _():
            dg_ref[...] = jnp.zeros_like(dg_ref)
        dg_ref[...] += jnp.sum(dy * xhat, axis=0, keepdims=True)

    row = pl.BlockSpec((TM, D), lambda i: (i, 0))
    return pl.pallas_call(
        body, name=name, grid=(T // TM,),
        in_specs=[row, _full((1, D)), row, row],
        out_specs=[row, _full((1, D))],
        out_shape=[_sds((T, D), F32), _sds((1, D), F32)], compiler_params=_params(("arbitrary",)),
    )(h, gain, dhn, dres)


def loss_head(h, gain, target, name):
    def body(h_ref, g_ref, t_ref, loss_ref, dh_ref, dg_ref):
        i = pl.program_id(0)

        @pl.when(i == 0)
        def _():
            loss_ref[...] = jnp.zeros_like(loss_ref)
            dg_ref[...] = jnp.zeros_like(dg_ref)
            dh_ref[...] = jnp.zeros_like(dh_ref)

        @pl.when(i > 0)
        def _():
            x = h_ref[...]
            g = g_ref[...]
            rstd = lax.rsqrt(jnp.mean(x * x, axis=-1, keepdims=True) + EPS)
            xhat = x * rstd
            err = xhat * g - t_ref[...]
            loss_ref[...] += 0.5 * jnp.sum(jnp.mean(err * err, axis=-1, keepdims=True), axis=0, keepdims=True)
            dy = err * (1.0 / D)
            dyg = dy * g
            dh_ref[...] = rstd * (dyg - xhat * jnp.mean(dyg * xhat, axis=-1, keepdims=True))
            dg_ref[...] += jnp.sum(dy * xhat, axis=0, keepdims=True)

    row = pl.BlockSpec((CH, D), lambda i: (i, 0))
    return pl.pallas_call(
        body, name=name, grid=(NCH,),
        in_specs=[row, _full((1, D)), pl.BlockSpec((CH, D), lambda i: (jnp.maximum(i - 1, 0), 0))],
        out_specs=[_full((8, 128)), row, _full((1, D))],
        out_shape=[_sds((8, 128), F32), _sds((T, D), F32), _sds((1, D), F32)],
        compiler_params=_params(("arbitrary",)),
    )(h, gain, target)


def mm_blocked_nn(a, w, out_dtype, name):
    k = a.shape[1]

    def body(a_ref, w_ref, o_ref):
        o_ref[...] = _dot(a_ref[...], w_ref[...], NN).astype(o_ref.dtype)

    return pl.pallas_call(
        body, name=name, grid=(NDEV, T // TM),
        in_specs=[pl.BlockSpec((TM, k), lambda j, i: (i, 0)), pl.BlockSpec((None, k, IN_SH), lambda j, i: (j, 0, 0))],
        out_specs=pl.BlockSpec((TM, IN_SH), lambda j, i: (i, j)),
        out_shape=_sds((T, NDEV * IN_SH), out_dtype), compiler_params=_params(("parallel", "parallel")),
    )(a, w)


def mm_nn_res(a, w, res, after, name):
    k = a.shape[1]
    bn = 512

    def body(a_ref, w_ref, r_ref, after_ref, o_ref):
        del after_ref
        o_ref[...] = r_ref[...] + _dot(a_ref[...], w_ref[...], NN)

    return pl.pallas_call(
        body, name=name, grid=(D // bn, T // TM),
        in_specs=[pl.BlockSpec((TM, k), lambda j, i: (i, 0)), pl.BlockSpec((k, bn), lambda j, i: (0, j)),
                  pl.BlockSpec((TM, bn), lambda j, i: (i, j)), pl.BlockSpec(memory_space=pl.ANY)],
        out_specs=pl.BlockSpec((TM, bn), lambda j, i: (i, j)),
        out_shape=_sds((T, D), F32), compiler_params=_params(("parallel", "parallel")),
    )(a, w, res, after)


def ffn_up(hn, wg, wu, name):
    def body(a_ref, wg_ref, wu_ref, g_ref, u_ref, act_ref):
        a = a_ref[...]
        g = _dot(a, wg_ref[...], NN)
        u = _dot(a, wu_ref[...], NN)
        g_ref[...] = g.astype(g_ref.dtype)
        u_ref[...] = u.astype(u_ref.dtype)
        act_ref[...] = (jax.nn.silu(g) * u).astype(act_ref.dtype)

    wspec = pl.BlockSpec((None, D, FF_SHP), lambda j, i: (j, 0, 0))
    ospec = pl.BlockSpec((TM, FF_SHP), lambda j, i: (i, j))
    return pl.pallas_call(
        body, name=name, grid=(NDEV, T // TM),
        in_specs=[pl.BlockSpec((TM, D), lambda j, i: (i, 0)), wspec, wspec],
        out_specs=[ospec, ospec, ospec],
        out_shape=[_sds((T, D_FFP), MXU_DTYPE)] * 3, compiler_params=_params(("parallel", "parallel")),
    )(hn, wg, wu)


def ffn_down_bwd(dh, wd, gate, up, after, name):
    def body(dh_ref, wd_ref, g_ref, u_ref, after_ref, dg_ref, du_ref):
        del after_ref
        dact = _dot(dh_ref[...], wd_ref[...], NT)
        g = g_ref[...].astype(F32)
        u = u_ref[...].astype(F32)
        sg = jax.nn.sigmoid(g)
        dg_ref[...] = (dact * u * (sg * (1.0 + g * (1.0 - sg)))).astype(dg_ref.dtype)
        du_ref[...] = (dact * (g * sg)).astype(du_ref.dtype)

    blk = pl.BlockSpec((TM, FF_SHP), lambda j, i: (i, j))
    return pl.pallas_call(
        body, name=name, grid=(NDEV, T // TM),
        in_specs=[pl.BlockSpec((TM, D), lambda j, i: (i, 0)), pl.BlockSpec((FF_SHP, D), lambda j, i: (j, 0)), blk, blk,
                  pl.BlockSpec(memory_space=pl.ANY)],
        out_specs=[blk, blk],
        out_shape=[_sds((T, D_FFP), MXU_DTYPE)] * 2, compiler_params=_params(("parallel", "parallel")),
    )(dh, wd, gate, up, after)


def mm_nt(a, w, out_dtype, after, name):
    n = a.shape[1]
    bk = 512

    def body(a_ref, w_ref, after_ref, o_ref):
        del after_ref
        o_ref[...] = _dot(a_ref[...], w_ref[...], NT).astype(o_ref.dtype)

    return pl.pallas_call(
        body, name=name, grid=(D // bk, T // TM),
        in_specs=[pl.BlockSpec((TM, n), lambda j, i: (i, 0)), pl.BlockSpec((bk, n), lambda j, i: (j, 0)),
                  pl.BlockSpec(memory_space=pl.ANY)],
        out_specs=pl.BlockSpec((TM, bk), lambda j, i: (i, j)),
        out_shape=_sds((T, D), out_dtype), compiler_params=_params(("parallel", "parallel")),
    )(a, w, after)


def mm_blocked_nt(pairs, after, name):
    n = len(pairs)

    def body(*refs):
        o_ref = refs[2 * n + 1]

        @pl.when(pl.program_id(1) == 0)
        def _():
            o_ref[...] = jnp.zeros_like(o_ref)
        acc = _dot(refs[0][...], refs[1][...], NT)
        for p in range(1, n):
            acc += _dot(refs[2 * p][...], refs[2 * p + 1][...], NT)
        o_ref[...] += acc

    specs, args = [], []
    for a, w in pairs:
        specs += [pl.BlockSpec((TM, IN_SH), lambda i, j: (i, j)), pl.BlockSpec((None, D, IN_SH), lambda i, j: (j, 0, 0))]
        args += [a, w]
    return pl.pallas_call(
        body, name=name, grid=(T // TM, NDEV), in_specs=specs + [pl.BlockSpec(memory_space=pl.ANY)],
        out_specs=pl.BlockSpec((TM, D), lambda i, j: (i, 0)),
        out_shape=_sds((T, D), F32), compiler_params=_params(("parallel", "arbitrary")),
    )(*args, after)


def mm_tn_blocked(a, b, name):
    def body(a_ref, b_ref, o_ref):
        o_ref[...] = _dot(a_ref[...], b_ref[...], TN).astype(o_ref.dtype)

    return pl.pallas_call(
        body, name=name, grid=(NDEV,),
        in_specs=[_full((T, D)), pl.BlockSpec((T, IN_SH), lambda j: (0, j))],
        out_specs=pl.BlockSpec((None, D, IN_SH), lambda j: (j, 0, 0)),
        out_shape=_sds((NDEV, D, IN_SH), WIRE_DTYPE), compiler_params=_params(("parallel",)),
    )(a, b)


def mm_tn(a, b, bm, after, name):
    m = a.shape[1]
    bn = 512

    def body(a_ref, b_ref, after_ref, o_ref):
        del after_ref
        o_ref[...] = _dot(a_ref[...], b_ref[...], TN).astype(o_ref.dtype)

    return pl.pallas_call(
        body, name=name, grid=(m // bm, D // bn),
        in_specs=[pl.BlockSpec((T, bm), lambda i, j: (0, i)), pl.BlockSpec((T, bn), lambda i, j: (0, j)),
                  pl.BlockSpec(memory_space=pl.ANY)],
        out_specs=pl.BlockSpec((bm, bn), lambda i, j: (i, j)),
        out_shape=_sds((m, D), WIRE_DTYPE), compiler_params=_params(("parallel", "parallel")),
    )(a, b, after)


def _softplus_neg(lam):
    return jnp.maximum(-lam, 0.0) + jnp.log1p(jnp.exp(-jnp.abs(lam)))


def _lru_gates(pa, px, xc, lam):
    r = jax.nn.sigmoid(pa)
    ig = jax.nn.sigmoid(px)
    log_a = -LRU_C * r * _softplus_neg(lam)
    a = jnp.exp(log_a)
    mult = jnp.sqrt(-jnp.tanh(log_a) * (jnp.exp(2.0 * log_a) + 1.0))
    return a, mult * (ig * xc)


def _lru_out(h, g, gain):
    z = h * jax.nn.gelu(g)
    return z * lax.rsqrt(jnp.mean(z * z, axis=-1, keepdims=True) + EPS) * gain


def _conv_taps(x, xprev, row):
    taps = [x]
    for s in range(1, CONV_W):
        taps.append(jnp.where(row < s, pltpu.roll(xprev, s, 0), pltpu.roll(x, s, 0)))
    return taps


def _conv(taps, cw_ref, cb):
    xc = cb + cw_ref[CONV_W - 1:CONV_W, :] * taps[0]
    for s in range(1, CONV_W):
        xc = xc + cw_ref[CONV_W - 1 - s:CONV_W - s, :] * taps[s]
    return xc


def lru_fwd(proj, cw, cb, wa, ba, wx, bx, lam, gain, name):
    def body(x_ref, g_ref, cw_ref, cb_ref, wa_ref, ba_ref, wx_ref, bx_ref, lam_ref, gain_ref,
             y_ref, h_ref, xprev_scr, a_scr, b_scr, carry_scr):
        i = pl.program_id(0)

        @pl.when(i == 0)
        def _():
            xprev_scr[...] = jnp.zeros_like(xprev_scr)
            carry_scr[...] = jnp.zeros_like(carry_scr)

        x = x_ref[...]
        row = lax.broadcasted_iota(jnp.int32, (CH, D_LRU), 0)
        xc = _conv(_conv_taps(x, xprev_scr[...], row), cw_ref, cb_ref[...])
        pa = _dot(xc, wa_ref[...], NN) + ba_ref[...]
        px = _dot(xc, wx_ref[...], NN) + bx_ref[...]
        a, b = _lru_gates(pa, px, xc, lam_ref[...])
        a_scr[...] = a
        b_scr[...] = jnp.where(i * CH + row >= PAD, b, 0.0)
        h = carry_scr[...]
        for t in range(CH):
            h = a_scr[t:t + 1, :] * h + b_scr[t:t + 1, :]
            h_ref[t:t + 1, :] = h
        carry_scr[...] = h
        xprev_scr[...] = x
        y_ref[...] = _lru_out(h_ref[...], g_ref[...], gain_ref[...]).astype(y_ref.dtype)

    vec = _full((1, D_LRU))
    mat = _full((D_LRU, D_LRU))
    return pl.pallas_call(
        body, name=name, grid=(NCH,),
        in_specs=[pl.BlockSpec((CH, D_LRU), lambda i: (i, 0)), pl.BlockSpec((CH, D_LRU), lambda i: (i, 1)),
                  _full((CONV_W, D_LRU)), vec, mat, vec, mat, vec, vec, vec],
        out_specs=[pl.BlockSpec((CH, D_LRU), lambda i: (i, 0)), pl.BlockSpec((CH, D_LRU), lambda i: (i, 0))],
        out_shape=[_sds((T, D_LRU), MXU_DTYPE), _sds((T, D_LRU), F32)],
        scratch_shapes=[pltpu.VMEM((CH, D_LRU), F32), pltpu.VMEM((CH, D_LRU), F32), pltpu.VMEM((CH, D_LRU), F32),
                        pltpu.VMEM((1, D_LRU), F32)],
        compiler_params=_params(("arbitrary",)),
    )(proj, proj, cw, cb, wa, ba, wx, bx, lam, gain)


LRU_VEC_ROWS = 16


def lru_bwd(proj, hst, dymix, cw, cb, wa, ba, wx, bx, lam, gain, name):
    last = NCH - 1

    def body(x_ref, xp_ref, g_ref, h_ref, hp_ref, dy_ref, cw_ref, cb_ref, wa_ref, ba_ref, wx_ref, bx_ref, lam_ref,
             gain_ref, dxg_ref, vec_ref, dwa_ref, dwx_ref, a_scr, dh_scr, g_scr, carry_scr, dxcn_scr):
        i = pl.program_id(0)
        ib = last - i

        @pl.when(i == 0)
        def _():
            carry_scr[...] = jnp.zeros_like(carry_scr)
            dxcn_scr[...] = jnp.zeros_like(dxcn_scr)
            vec_ref[...] = jnp.zeros_like(vec_ref)
            dwa_ref[...] = jnp.zeros_like(dwa_ref)
            dwx_ref[...] = jnp.zeros_like(dwx_ref)

        x = x_ref[...]
        row = lax.broadcasted_iota(jnp.int32, (CH, D_LRU), 0)
        valid = ib * CH + row >= PAD
        taps = _conv_taps(x, xp_ref[...], row)
        xc = _conv(taps, cw_ref, cb_ref[...])
        pa = _dot(xc, wa_ref[...], NN) + ba_ref[...]
        px = _dot(xc, wx_ref[...], NN) + bx_ref[...]
        (a, _), vjp_gates = jax.vjp(_lru_gates, pa, px, xc, lam_ref[...])
        h = h_ref[...]
        _, vjp_out = jax.vjp(_lru_out, h, g_ref[...], gain_ref[...])
        dh, dg, dgain = vjp_out(dy_ref[...].astype(F32))
        a_scr[...] = a
        dh_scr[...] = dh
        c = carry_scr[...]
        for t in range(CH - 1, -1, -1):
            gt = dh_scr[t:t + 1, :] + c
            g_scr[t:t + 1, :] = gt
            c = a_scr[t:t + 1, :] * gt
        carry_scr[...] = c
        gg = g_scr[...]
        hprev = jnp.where(row < 1, pltpu.roll(hp_ref[...], 1, 0), pltpu.roll(h, 1, 0))
        da = jnp.where(valid, gg * hprev, 0.0)
        db = jnp.where(valid, gg, 0.0)
        dpa, dpx, dxc, dlam = vjp_gates((da, db))
        dxc = dxc + _dot(dpa, wa_ref[...], NT) + _dot(dpx, wx_ref[...], NT)
        dwa_ref[...] += _dot(xc, dpa, TN)
        dwx_ref[...] += _dot(xc, dpx, TN)
        for s in range(CONV_W):
            vec_ref[CONV_W - 1 - s:CONV_W - s, :] += jnp.sum(dxc * taps[s], axis=0, keepdims=True)
        vec_ref[4:5, :] += jnp.sum(dxc, axis=0, keepdims=True)
        vec_ref[5:6, :] += jnp.sum(dpa, axis=0, keepdims=True)
        vec_ref[6:7, :] += jnp.sum(dpx, axis=0, keepdims=True)
        vec_ref[7:8, :] += dlam
        vec_ref[8:9, :] += dgain
        dxn = dxcn_scr[...]
        dx = cw_ref[CONV_W - 1:CONV_W, :] * dxc
        for s in range(1, CONV_W):
            ahead = jnp.where(row >= CH - s, pltpu.roll(dxn, CH - s, 0), pltpu.roll(dxc, CH - s, 0))
            dx = dx + cw_ref[CONV_W - 1 - s:CONV_W - s, :] * ahead
        dxcn_scr[...] = dxc
        dxg_ref[:, :D_LRU] = jnp.where(valid, dx, 0.0).astype(dxg_ref.dtype)
        dxg_ref[:, D_LRU:] = dg.astype(dxg_ref.dtype)

    vec = _full((1, D_LRU))
    mat = _full((D_LRU, D_LRU))

    def blk(col, shift=0):
        return pl.BlockSpec((CH, D_LRU), lambda i: (jnp.maximum(last - i - shift, 0), col))

    return pl.pallas_call(
        body, name=name, grid=(NCH,),
        in_specs=[blk(0), blk(0, 1), blk(1), blk(0), blk(0, 1), blk(0),
                  _full((CONV_W, D_LRU)), vec, mat, vec, mat, vec, vec, vec],
        out_specs=[pl.BlockSpec((CH, 2 * D_LRU), lambda i: (last - i, 0)), _full((LRU_VEC_ROWS, D_LRU)), mat, mat],
        out_shape=[_sds((T, 2 * D_LRU), MXU_DTYPE), _sds((LRU_VEC_ROWS, D_LRU), F32),
                   _sds((D_LRU, D_LRU), F32), _sds((D_LRU, D_LRU), F32)],
        scratch_shapes=[pltpu.VMEM((CH, D_LRU), F32), pltpu.VMEM((CH, D_LRU), F32), pltpu.VMEM((CH, D_LRU), F32),
                        pltpu.VMEM((1, D_LRU), F32), pltpu.VMEM((CH, D_LRU), F32)],
        compiler_params=_params(("arbitrary",)),
    )(proj, proj, proj, hst, hst, dymix, cw, cb, wa, ba, wx, bx, lam, gain)


def _ret_tables():
    half = HD // 2
    pos = jnp.arange(T, dtype=F32) - float(PAD)
    inv = ROPE_BASE ** (-jnp.arange(half, dtype=F32) / half)
    ang = pos[:, None] * inv[None, :]
    cos = jnp.concatenate([jnp.cos(ang), jnp.cos(ang)], axis=-1)
    sin = jnp.concatenate([-jnp.sin(ang), jnp.sin(ang)], axis=-1)
    log_g = jnp.log(1.0 - 2.0 ** (-5.0 - jnp.arange(HEADS, dtype=F32)))
    idx = jnp.arange(CH, dtype=F32)
    diff = idx[:, None] - idx[None, :]
    dmask = jnp.where(diff[None] >= 0, jnp.exp(jnp.maximum(diff, 0.0)[None] * log_g[:, None, None]), 0.0)
    xi = jnp.exp((idx + 1.0)[None, :] * log_g[:, None])
    zeta = jnp.exp((CH - 1.0 - idx)[None, :] * log_g[:, None])
    xi = jnp.broadcast_to(xi[:, :, None], (HEADS, CH, HD))
    zeta = jnp.broadcast_to(zeta[:, :, None], (HEADS, CH, HD))
    return cos, sin, dmask, xi, zeta


def _chunk_decay():
    log_g = np.log(np.float32(1.0) - np.float32(2.0) ** (np.float32(-5.0) - np.arange(HEADS, dtype=np.float32)))
    return [float(v) for v in np.exp(np.float32(CH) * log_g.astype(np.float32))]


def _rope(x, cos, sin):
    return x * cos + pltpu.roll(x, HD // 2, 1) * sin


def ret_fwd(proj, ylru, tables, gain, name):
    cos, sin, dmask, xi, zeta = tables
    gch = _chunk_decay()
    scale = HD ** -0.5

    def body(q_ref, k_ref, v_ref, g_ref, cos_ref, sin_ref, dm_ref, xi_ref, zt_ref, gain_ref, ylru_ref,
             y_ref, st_ref, s_scr):
        @pl.when(pl.program_id(0) == 0)
        def _():
            s_scr[...] = jnp.zeros_like(s_scr)

        y_ref[:, :D_LRU] = ylru_ref[...]
        cs, sn = cos_ref[...], sin_ref[...]
        for h in range(HEADS):
            sl = slice(HD * h, HD * (h + 1))
            so = slice(D_LRU + HD * h, D_LRU + HD * (h + 1))
            qr = _rope(q_ref[:, sl], cs, sn)
            kr = _rope(k_ref[:, sl], cs, sn) * scale
            v = v_ref[:, sl]
            s = s_scr[h]
            st_ref[h] = s
            sc = _dot(qr, kr, NT) * dm_ref[h]
            y = _dot(sc, v, NN) + _dot(qr, s, NN) * xi_ref[h]
            s_scr[h] = s * gch[h] + _dot(kr * zt_ref[h], v, TN)
            yc = y - jnp.mean(y, axis=-1, keepdims=True)
            yn = yc * lax.rsqrt(jnp.mean(yc * yc, axis=-1, keepdims=True) + EPS)
            y_ref[:, so] = (jax.nn.silu(g_ref[:, sl]) * (yn * gain_ref[:, sl])).astype(y_ref.dtype)

    def col(c):
        return pl.BlockSpec((CH, D_RET), lambda n: (n, c))

    tab = pl.BlockSpec((CH, HD), lambda n: (n, 0))
    cst = _full((HEADS, CH, HD))
    return pl.pallas_call(
        body, name=name, grid=(NCH,),
        in_specs=[col(2), col(3), col(4), col(5), tab, tab, cst, cst, cst, _full((1, D_RET)), col(0)],
        out_specs=[pl.BlockSpec((CH, D), lambda n: (n, 0)), pl.BlockSpec((None, HEADS, HD, HD), lambda n: (n, 0, 0, 0))],
        out_shape=[_sds((T, D), MXU_DTYPE), _sds((NCH, HEADS, HD, HD), F32)],
        scratch_shapes=[pltpu.VMEM((HEADS, HD, HD), F32)],
        compiler_params=_params(("arbitrary",)),
    )(proj, proj, proj, proj, cos, sin, dmask, xi, zeta, gain, ylru)


def ret_bwd(proj, states, dymix, dxg, tables, gain, name):
    cos, sin, dmask, xi, zeta = tables
    gch = _chunk_decay()
    scale = HD ** -0.5
    last = NCH - 1

    def body(q_ref, k_ref, v_ref, g_ref, st_ref, do_ref, cos_ref, sin_ref, dm_ref, xi_ref, zt_ref, gain_ref, dxg_ref,
             dp_ref, dgain_ref, ds_scr):
        @pl.when(pl.program_id(0) == 0)
        def _():
            ds_scr[...] = jnp.zeros_like(ds_scr)
            dgain_ref[...] = jnp.zeros_like(dgain_ref)

        dp_ref[:, :2 * D_LRU] = dxg_ref[...]
        cs, sn = cos_ref[...], sin_ref[...]
        for h in range(HEADS):
            sl = slice(HD * h, HD * (h + 1))
            oq, ok, ov, og = (slice(2 * D_LRU + j * D_RET + HD * h, 2 * D_LRU + j * D_RET + HD * (h + 1)) for j in range(4))
            qr = _rope(q_ref[:, sl], cs, sn)
            kr = _rope(k_ref[:, sl], cs, sn) * scale
            v = v_ref[:, sl]
            g = g_ref[:, sl]
            gain = gain_ref[:, sl]
            dm, x_i, zt = dm_ref[h], xi_ref[h], zt_ref[h]
            s = st_ref[h]
            ds = ds_scr[h]
            kz = kr * zt
            sc = _dot(qr, kr, NT) * dm
            y = _dot(sc, v, NN) + _dot(qr, s, NN) * x_i
            yc = y - jnp.mean(y, axis=-1, keepdims=True)
            rstd = lax.rsqrt(jnp.mean(yc * yc, axis=-1, keepdims=True) + EPS)
            yn = yc * rstd
            sg = jax.nn.sigmoid(g)
            silu = g * sg
            dout = do_ref[:, sl].astype(F32)
            dgain_ref[:, sl] += jnp.sum(dout * silu * yn, axis=0, keepdims=True)
            dp_ref[:, og] = (dout * yn * gain * (sg * (1.0 + g * (1.0 - sg)))).astype(dp_ref.dtype)
            dyn = dout * silu * gain
            dy = rstd * (dyn - jnp.mean(dyn, axis=-1, keepdims=True) - yn * jnp.mean(dyn * yn, axis=-1, keepdims=True))
            dp = _dot(dy, v, NT) * dm
            dv = _dot(sc, dy, TN) + _dot(kz, ds, NN)
            dqs = dy * x_i
            dqr = _dot(dp, kr, NN) + _dot(dqs, s, NT)
            dkr = _dot(dp, qr, TN) + _dot(v, ds, NT) * zt
            ds_scr[h] = gch[h] * ds + _dot(qr, dqs, TN)
            dp_ref[:, oq] = (dqr * cs + pltpu.roll(dqr * sn, HD // 2, 1)).astype(dp_ref.dtype)
            dp_ref[:, ok] = ((dkr * cs + pltpu.roll(dkr * sn, HD // 2, 1)) * scale).astype(dp_ref.dtype)
            dp_ref[:, ov] = dv.astype(dp_ref.dtype)

    def col(c):
        return pl.BlockSpec((CH, D_RET), lambda n: (last - n, c))

    tab = pl.BlockSpec((CH, HD), lambda n: (last - n, 0))
    cst = _full((HEADS, CH, HD))
    return pl.pallas_call(
        body, name=name, grid=(NCH,),
        in_specs=[col(2), col(3), col(4), col(5), pl.BlockSpec((None, HEADS, HD, HD), lambda n: (last - n, 0, 0, 0)), col(1),
                  tab, tab, cst, cst, cst, _full((1, D_RET)), pl.BlockSpec((CH, 2 * D_LRU), lambda n: (last - n, 0))],
        out_specs=[pl.BlockSpec((CH, D_IN), lambda n: (last - n, 0)), _full((1, D_RET))],
        out_shape=[_sds((T, D_IN), MXU_DTYPE), _sds((1, D_RET), F32)],
        scratch_shapes=[pltpu.VMEM((HEADS, HD, HD), F32)],
        compiler_params=_params(("arbitrary",)),
    )(proj, proj, proj, proj, states, dymix, cos, sin, dmask, xi, zeta, gain, dxg)


HBM = pl.BlockSpec(memory_space=pltpu.HBM)


def _place():
    return lax.axis_index("x"), lax.axis_index("y"), lax.axis_index("c")


def all_gather(arrs, name):
    n = len(arrs)

    def body(*refs):
        ins, outs = refs[:n], refs[n:2 * n]
        send_sems, recv_sems, local_sems = refs[2 * n:]
        x, y, c = _place()
        me, sibling = (x, y, c), (x, y, 1 - c)
        chips = [(1 - x, y), (x, 1 - y), (1 - x, 1 - y)]

        def copy(a, k, block, to, src=None):
            px, py, pc = block
            dst = outs[a].at[4 * px + 2 * py + pc]
            return pltpu.make_async_remote_copy(
                src_ref=dst if src is None else src, dst_ref=dst, send_sem=send_sems.at[a, k], recv_sem=recv_sems.at[a, k],
                device_id=to, device_id_type=MESH)

        mine = [pltpu.make_async_copy(ins[a], outs[a].at[4 * x + 2 * y + c], local_sems.at[a]) for a in range(n)]
        for cp in mine:
            cp.start()
        first = []
        for a in range(n):
            first.append(copy(a, 0, me, sibling, src=ins[a]))
            first += [copy(a, 1 + j, me, (*chip, c), src=ins[a]) for j, chip in enumerate(chips)]
        for cp in first:
            cp.start()
        passed = []
        for j, chip in enumerate(chips):
            for a in range(n):
                copy(a, 1 + j, (*chip, c), me).wait_recv()
                passed.append(copy(a, 4 + j, (*chip, c), sibling))
                passed[-1].start()
        for a in range(n):
            copy(a, 0, sibling, me).wait_recv()
            for j, chip in enumerate(chips):
                copy(a, 4 + j, (*chip, 1 - c), me).wait_recv()
        for cp in first + passed:
            cp.wait_send()
        for cp in mine:
            cp.wait()

    return pl.pallas_call(
        body, name=name,
        in_specs=[HBM] * n, out_specs=[HBM] * n,
        out_shape=[_sds((NDEV,) + a.shape, a.dtype) for a in arrs],
        scratch_shapes=[pltpu.SemaphoreType.DMA((n, 7)), pltpu.SemaphoreType.DMA((n, 7)), pltpu.SemaphoreType.DMA((n,))],
    )(*arrs)


SEM = pl.BlockSpec(memory_space=pltpu.SEMAPHORE)
ANY = pl.BlockSpec(memory_space=pl.ANY)
EFFECT = pltpu.SideEffectType.DATAFLOW_SIDE_EFFECTING


def _hbm(a):
    return pltpu.with_memory_space_constraint(a, pltpu.HBM)


def _hbm_like(arrs):
    return [pltpu.HBM(a.shape, a.dtype) for a in arrs]


def _dma_sems(count):
    return [pltpu.SemaphoreType.DMA(())] * count


def _ag_copy(lands, send_sems, recv_sems, per):
    def copy(a, k, block, to, src=None):
        px, py, pc = block
        dst = lands[a].at[4 * px + 2 * py + pc]
        return pltpu.make_async_remote_copy(
            src_ref=dst if src is None else src, dst_ref=dst, send_sem=send_sems[a * per + k], recv_sem=recv_sems[a * per + k],
            device_id=to, device_id_type=MESH)
    return copy


def to_wire(sel, w_in, w_gate, w_up, w_out, w_down, name):
    ffpad = FF_SHP - FF_SH

    def body(sel_ref, i_ref, g_ref, u_ref, o_ref, d_ref, oi, og, ou, oo, od):
        del sel_ref
        oi[...] = i_ref[...].astype(oi.dtype)
        oo[...] = o_ref[...].astype(oo.dtype)
        for src, dst in ((g_ref, og), (u_ref, ou)):
            dst[:, :FF_SH] = src[...].astype(dst.dtype)
            dst[:, FF_SH:] = jnp.zeros((D, ffpad), dst.dtype)
        od[:FF_SH, :] = d_ref[...].astype(od.dtype)
        od[FF_SH:, :] = jnp.zeros((ffpad, D), od.dtype)

    shapes_in = [(D, IN_SH), (D, FF_SH), (D, FF_SH), (OUT_SH, D), (FF_SH, D)]
    shapes_out = [(D, IN_SH), (D, FF_SHP), (D, FF_SHP), (OUT_SH, D), (FF_SHP, D)]
    return pl.pallas_call(
        body, name=name,
        grid_spec=pltpu.PrefetchScalarGridSpec(
            num_scalar_prefetch=1, grid=(1,),
            in_specs=[pl.BlockSpec((None,) + s, lambda i, sel_ref: (sel_ref[1], 0, 0)) for s in shapes_in],
            out_specs=[pl.BlockSpec((None,) + s, lambda i, sel_ref: (sel_ref[0], 0, 0)) for s in shapes_out]),
        out_shape=[_sds((NDEV,) + s, WIRE_DTYPE) for s in shapes_out], compiler_params=_params(("arbitrary",)),
    )(sel, w_in, w_gate, w_up, w_out, w_down)


def ag_start(lands, after, name):
    n = len(lands)
    ns = 4 * n

    def body(*refs):
        lnd = refs[:n]
        send_sems, recv_sems = refs[n + 1:n + 1 + ns], refs[n + 1 + ns:n + 1 + 2 * ns]
        token = refs[-1]
        x, y, c = _place()
        me, sibling = (x, y, c), (x, y, 1 - c)
        chips = [(1 - x, y), (x, 1 - y), (1 - x, 1 - y)]
        copy = _ag_copy(lnd, send_sems, recv_sems, 4)
        for a in range(n):
            copy(a, 0, me, sibling).start()
            for j, chip in enumerate(chips):
                copy(a, 1 + j, me, (*chip, c)).start()
        token[...] = jnp.zeros_like(token)

    outs = pl.pallas_call(
        body, name=name,
        in_specs=[HBM] * n + [ANY],
        out_specs=[SEM] * (2 * ns) + [HBM] * n + [pl.BlockSpec(memory_space=pltpu.VMEM)],
        out_shape=_dma_sems(2 * ns) + _hbm_like(lands) + [_sds((8, 128), F32)],
        input_output_aliases={i: 2 * ns + i for i in range(n)},
        compiler_params=pltpu.CompilerParams(has_side_effects=EFFECT),
    )(*[_hbm(a) for a in lands], after)
    return outs[:ns], outs[ns:2 * ns], outs[2 * ns:2 * ns + n], outs[-1]


def ag_forward(send_sems, recv_sems, lands, after, name):
    n = len(lands)
    n1, n2 = 4 * n, 3 * n

    def body(*refs):
        lnd = refs[:n]
        o = n
        s1, r1 = refs[o:o + n1], refs[o + n1:o + 2 * n1]
        o += 2 * n1 + 1
        s2, r2 = refs[o:o + n2], refs[o + n2:o + 2 * n2]
        token = refs[-1]
        token[...] = jnp.zeros_like(token)
        x, y, c = _place()
        me, sibling = (x, y, c), (x, y, 1 - c)
        chips = [(1 - x, y), (x, 1 - y), (1 - x, 1 - y)]
        copy1 = _ag_copy(lnd, s1, r1, 4)
        copy2 = _ag_copy(lnd, s2, r2, 3)
        for j, chip in enumerate(chips):
            for a in range(n):
                copy1(a, 1 + j, (*chip, c), me).wait_recv()
                copy2(a, j, (*chip, c), sibling).start()
        for a in range(n):
            copy1(a, 0, sibling, me).wait_recv()
            copy1(a, 0, me, sibling).wait_send()
            for j, chip in enumerate(chips):
                copy1(a, 1 + j, me, (*chip, c)).wait_send()

    outs = pl.pallas_call(
        body, name=name,
        in_specs=[HBM] * n + [SEM] * (2 * n1) + [ANY],
        out_specs=[SEM] * (2 * n2) + [HBM] * n + [pl.BlockSpec(memory_space=pltpu.VMEM)],
        out_shape=_dma_sems(2 * n2) + _hbm_like(lands) + [_sds((8, 128), F32)],
        input_output_aliases={i: 2 * n2 + i for i in range(n)},
        compiler_params=pltpu.CompilerParams(has_side_effects=EFFECT),
    )(*lands, *send_sems, *recv_sems, after)
    return outs[:n2], outs[n2:2 * n2], outs[2 * n2:2 * n2 + n], outs[-1]


def ag_finish(send_sems, recv_sems, lands, after, name):
    n = len(lands)
    n2 = 3 * n

    def body(*refs):
        lnd = refs[:n]
        s2, r2 = refs[n:n + n2], refs[n + n2:n + 2 * n2]
        x, y, c = _place()
        me, sibling = (x, y, c), (x, y, 1 - c)
        chips = [(1 - x, y), (x, 1 - y), (1 - x, 1 - y)]
        copy2 = _ag_copy(lnd, s2, r2, 3)
        for a in range(n):
            for j, chip in enumerate(chips):
                copy2(a, j, (*chip, c), sibling).wait_send()
                copy2(a, j, (*chip, 1 - c), me).wait_recv()

    outs = pl.pallas_call(
        body, name=name,
        in_specs=[HBM] * n + [SEM] * (2 * n2) + [ANY],
        out_specs=[HBM] * n, out_shape=_hbm_like(lands),
        input_output_aliases={i: i for i in range(n)},
        compiler_params=pltpu.CompilerParams(has_side_effects=EFFECT),
    )(*lands, *send_sems, *recv_sems, after)
    return list(outs)


def rs_sibling_start(arrs, name):
    n = len(arrs)
    ns = 4 * n
    lands = [lax.empty((4,) + a.shape[1:], a.dtype) for a in arrs]

    def body(*refs):
        ins, lnd = refs[:n], refs[n:2 * n]
        send_sems, recv_sems = refs[2 * n:2 * n + ns], refs[2 * n + ns:2 * n + 2 * ns]
        x, y, c = _place()
        sibling = (x, y, 1 - c)
        for a in range(n):
            for p in range(4):
                pltpu.make_async_remote_copy(
                    src_ref=ins[a].at[2 * p + 1 - c], dst_ref=lnd[a].at[p], send_sem=send_sems[4 * a + p],
                    recv_sem=recv_sems[4 * a + p], device_id=sibling, device_id_type=MESH).start()
        refs[-1][...] = jnp.zeros_like(refs[-1])

    outs = pl.pallas_call(
        body, name=name,
        in_specs=[HBM] * (2 * n), out_specs=[SEM] * (2 * ns) + [HBM] * (2 * n) + [pl.BlockSpec(memory_space=pltpu.VMEM)],
        out_shape=_dma_sems(2 * ns) + _hbm_like(arrs) + _hbm_like(lands) + [_sds((8, 128), F32)],
        input_output_aliases={i: 2 * ns + i for i in range(2 * n)},
        compiler_params=pltpu.CompilerParams(has_side_effects=EFFECT),
    )(*[_hbm(a) for a in arrs], *[_hbm(a) for a in lands])
    return (outs[:ns], outs[ns:2 * ns], outs[2 * ns:2 * ns + n], outs[2 * ns + n:2 * ns + 2 * n]), outs[-1]


def rs_sibling_wait(send_sems, recv_sems, arrs, lands, after, name):
    n = len(arrs)
    ns = 4 * n

    def body(*refs):
        ins, lnd = refs[:n], refs[n:2 * n]
        s, r = refs[2 * n:2 * n + ns], refs[2 * n + ns:2 * n + 2 * ns]
        x, y, c = _place()
        sibling = (x, y, 1 - c)
        for a in range(n):
            for p in range(4):
                cp = pltpu.make_async_remote_copy(
                    src_ref=ins[a].at[2 * p + 1 - c], dst_ref=lnd[a].at[p], send_sem=s[4 * a + p], recv_sem=r[4 * a + p],
                    device_id=sibling, device_id_type=MESH)
                cp.wait_send()
                cp.wait_recv()

    outs = pl.pallas_call(
        body, name=name,
        in_specs=[HBM] * (2 * n) + [SEM] * (2 * ns) + [ANY], out_specs=[HBM] * (2 * n),
        out_shape=_hbm_like(arrs) + _hbm_like(lands),
        input_output_aliases={i: i for i in range(2 * n)},
        compiler_params=pltpu.CompilerParams(has_side_effects=EFFECT),
    )(*arrs, *lands, *send_sems, *recv_sems, after)
    return outs[:n], outs[n:]


def rs_chips_start(parts, name):
    n = len(parts)
    ns = 3 * n
    lands = [lax.empty((3,) + a.shape[1:], a.dtype) for a in parts]

    def body(*refs):
        ins, lnd = refs[:n], refs[n:2 * n]
        send_sems, recv_sems = refs[2 * n:2 * n + ns], refs[2 * n + ns:2 * n + 2 * ns]
        x, y, c = _place()
        chips = [(1 - x, y), (x, 1 - y), (1 - x, 1 - y)]
        for a in range(n):
            for k, (tx, ty) in enumerate(chips):
                pltpu.make_async_remote_copy(
                    src_ref=ins[a].at[2 * tx + ty], dst_ref=lnd[a].at[k], send_sem=send_sems[3 * a + k],
                    recv_sem=recv_sems[3 * a + k], device_id=(tx, ty, c), device_id_type=MESH).start()
        refs[-1][...] = jnp.zeros_like(refs[-1])

    outs = pl.pallas_call(
        body, name=name,
        in_specs=[HBM] * (2 * n), out_specs=[SEM] * (2 * ns) + [HBM] * (2 * n) + [pl.BlockSpec(memory_space=pltpu.VMEM)],
        out_shape=_dma_sems(2 * ns) + _hbm_like(parts) + _hbm_like(lands) + [_sds((8, 128), F32)],
        input_output_aliases={i: 2 * ns + i for i in range(2 * n)},
        compiler_params=pltpu.CompilerParams(has_side_effects=EFFECT),
    )(*[_hbm(a) for a in parts], *[_hbm(a) for a in lands])
    return (outs[:ns], outs[ns:2 * ns], outs[2 * ns:2 * ns + n], outs[2 * ns + n:2 * ns + 2 * n]), outs[-1]


def rs_chips_wait(send_sems, recv_sems, parts, lands, after, name):
    n = len(parts)
    ns = 3 * n

    def body(*refs):
        ins, lnd = refs[:n], refs[n:2 * n]
        s, r = refs[2 * n:2 * n + ns], refs[2 * n + ns:2 * n + 2 * ns]
        x, y, c = _place()
        chips = [(1 - x, y), (x, 1 - y), (1 - x, 1 - y)]
        for a in range(n):
            for k, (tx, ty) in enumerate(chips):
                cp = pltpu.make_async_remote_copy(
                    src_ref=ins[a].at[2 * tx + ty], dst_ref=lnd[a].at[k], send_sem=s[3 * a + k], recv_sem=r[3 * a + k],
                    device_id=(tx, ty, c), device_id_type=MESH)
                cp.wait_send()
                cp.wait_recv()

    outs = pl.pallas_call(
        body, name=name,
        in_specs=[HBM] * (2 * n) + [SEM] * (2 * ns) + [ANY], out_specs=[HBM] * (2 * n),
        out_shape=_hbm_like(parts) + _hbm_like(lands),
        input_output_aliases={i: i for i in range(2 * n)},
        compiler_params=pltpu.CompilerParams(has_side_effects=EFFECT),
    )(*parts, *lands, *send_sems, *recv_sems, after)
    return outs[:n], outs[n:]


def pair_sum(a, r, c, name):
    _, rr, cc = a.shape

    def body(c_ref, a_ref, r_ref, o_ref):
        del c_ref
        o_ref[...] = (a_ref[...].astype(F32) + r_ref[...].astype(F32)).astype(o_ref.dtype)

    return pl.pallas_call(
        body, name=name,
        grid_spec=pltpu.PrefetchScalarGridSpec(
            num_scalar_prefetch=1, grid=(4,),
            in_specs=[pl.BlockSpec((None, rr, cc), lambda p, c_ref: (2 * p + c_ref[0], 0, 0)),
                      pl.BlockSpec((None, rr, cc), lambda p, c_ref: (p, 0, 0))],
            out_specs=pl.BlockSpec((None, rr, cc), lambda p, c_ref: (p, 0, 0))),
        out_shape=_sds((4, rr, cc), a.dtype), compiler_params=_params(("parallel",)),
    )(c, a, r)


def _adamw(w, g, m, v):
    m = ADAM_B1 * m + (1.0 - ADAM_B1) * g
    v = ADAM_B2 * v + (1.0 - ADAM_B2) * jnp.square(g)
    m_hat = m / (1.0 - ADAM_B1 ** ADAM_STEP)
    v_hat = v / (1.0 - ADAM_B2 ** ADAM_STEP)
    return -ADAM_LR * (m_hat / (jnp.sqrt(v_hat) + ADAM_EPS) + ADAM_WD * w), m, v


def adamw_big(recv, sums, chip, w, m, v, tr, name):
    nl, rr, cc = w.shape
    cp = recv[0].shape[2]

    def body(chip_ref, *refs):
        del chip_ref
        rcv, own = refs[:nl], refs[nl:2 * nl]
        w_ref, m_ref, v_ref, g_out, d_out, m_out, v_out = refs[2 * nl:]
        for l in range(nl):
            g = ((own[l][...].astype(F32) + rcv[l][0].astype(F32)) + rcv[l][1].astype(F32)) + rcv[l][2].astype(F32)
            g = g[:, :cc]
            g_out[l] = g
            d_out[l], m_out[l], v_out[l] = _adamw(w_ref[l], g, m_ref[l], v_ref[l])

    blk = pl.BlockSpec((nl, tr, cc), lambda i, chip_ref: (0, i, 0))
    return pl.pallas_call(
        body, name=name,
        grid_spec=pltpu.PrefetchScalarGridSpec(
            num_scalar_prefetch=1, grid=(rr // tr,),
            in_specs=[pl.BlockSpec((3, tr, cp), lambda i, chip_ref: (0, i, 0))] * nl
            + [pl.BlockSpec((None, tr, cp), lambda i, chip_ref: (chip_ref[0], i, 0))] * nl + [blk, blk, blk],
            out_specs=[blk] * 4),
        out_shape=[_sds(w.shape, F32)] * 4, compiler_params=_params(("parallel",)),
    )(chip, *recv, *sums, w, m, v)


def sum_devices(g, name):
    _, rr, cc = g.shape

    def body(g_ref, o_ref):
        acc = g_ref[0]
        for j in range(1, NDEV):
            acc = acc + g_ref[j]
        o_ref[...] = acc

    return pl.pallas_call(
        body, name=name, grid=(1,), in_specs=[_full(g.shape)], out_specs=_full((rr, cc)), out_shape=_sds((rr, cc), F32),
        compiler_params=_params(("arbitrary",)),
    )(g)


def adamw_rows(g, w, m, v, name):
    rr, cc = w.shape

    def body(g_ref, w_ref, m_ref, v_ref, d_out, m_out, v_out):
        d_out[...], m_out[...], v_out[...] = _adamw(w_ref[...], g_ref[...], m_ref[...], v_ref[...])

    blk = _full((rr, cc))
    return pl.pallas_call(
        body, name=name, grid=(1,), in_specs=[blk] * 4, out_specs=[blk] * 3, out_shape=[_sds((rr, cc), F32)] * 3,
        compiler_params=_params(("arbitrary",)),
    )(g, w, m, v)


def _block_diag(w):
    eye = jnp.eye(LRU_BLOCKS, dtype=w.dtype)
    return (w[:, :, None, :] * eye[:, None, :, None]).reshape(D_LRU, D_LRU)


def _diag_blocks(wd):
    w4 = wd.reshape(LRU_BLOCKS, LRU_BD, LRU_BLOCKS, LRU_BD)
    return jnp.stack([w4[g, :, g, :] for g in range(LRU_BLOCKS)])


def _pack(arrs):
    flat = jnp.concatenate([a.reshape(-1) for a in arrs])
    return flat.reshape(-1, 128)


def _unpack(packed, shapes):
    flat = packed.reshape(-1)
    out, o = [], 0
    for s in shapes:
        n = int(np.prod(s))
        out.append(flat[o:o + n].reshape(s))
        o += n
    return out


REP_NAMES = ["norm_mix", "conv_b", "gate_a_w", "gate_a_b", "gate_x_w", "gate_x_b", "lru_lambda", "lru_out_norm",
             "ret_out_norm", "norm_ffn", "norm_final"]


def kernel(x, meta_tokens, norm_mix, w_in, conv_w, conv_b, gate_a_w, gate_a_b, gate_x_w, gate_x_b, lru_lambda, lru_out_norm, ret_out_norm, w_out, norm_ffn, w_gate, w_up, w_down, norm_final, loss_target, m_meta_tokens, m_norm_mix, m_w_in, m_conv_w, m_conv_b, m_gate_a_w, m_gate_a_b, m_gate_x_w, m_gate_x_b, m_lru_lambda, m_lru_out_norm, m_ret_out_norm, m_w_out, m_norm_ffn, m_w_gate, m_w_up, m_w_down, m_norm_final, v_meta_tokens, v_norm_mix, v_w_in, v_conv_w, v_conv_b, v_gate_a_w, v_gate_a_b, v_gate_x_w, v_gate_x_b, v_lru_lambda, v_lru_out_norm, v_ret_out_norm, v_w_out, v_norm_ffn, v_w_gate, v_w_up, v_w_down, v_norm_final):
    xi, yi, ci = _place()
    dev = 4 * xi + 2 * yi + ci
    c_arr = jnp.reshape(ci, (1,)).astype(jnp.int32)

    meta_g, conv_g = all_gather([meta_tokens, conv_w], "ag_small")
    meta_full = jnp.transpose(meta_g, (1, 0, 2)).reshape(N_META, D)
    conv_full = jnp.transpose(conv_g, (1, 2, 0, 3)).reshape(DEPTH, CONV_W, D_LRU)
    level1 = []
    token = meta_g
    for l in range(DEPTH):
        sel = jnp.stack([dev, jnp.int32(l)]).astype(jnp.int32)
        lands = to_wire(sel, w_in, w_gate, w_up, w_out, w_down, "to_wire")
        s1, r1, lands, token = ag_start(lands, token, f"ag_start_{l}")
        level1.append((s1, r1, lands))

    def as_weights(gi, gg, gu, go, gd):
        return dict(w_in=gi, w_gate=gg, w_up=gu, w_out=go.reshape(D, D), w_down=gd.reshape(D_FFP, D))

    tables = _ret_tables()
    row = lambda a: a.reshape(1, -1)

    h = jnp.concatenate([jnp.zeros((PAD, D), F32), meta_full, x[0]], axis=0)
    saved, gathered = [], []
    s1, r1, lands = level1[0]
    s2, r2, first, order = ag_forward(s1[:4], r1[:4], lands[:1], token, "ag_forward_0_w_in")
    w = dict(w_in=ag_finish(s2, r2, first, h, "ag_finish_0_w_in")[0])
    for l in range(DEPTH):
        small = dict(cw=conv_full[l], cb=row(conv_b[l]), wa=_block_diag(gate_a_w[l]).astype(MXU_DTYPE), ba=row(gate_a_b[l]),
                     wx=_block_diag(gate_x_w[l]).astype(MXU_DTYPE), bx=row(gate_x_b[l]), lam=row(lru_lambda[l]),
                     gain=row(lru_out_norm[l]))
        hn1 = rmsnorm_fwd(h, row(norm_mix[l]), "rms_fwd")
        proj = mm_blocked_nn(hn1, w["w_in"], F32, "proj")
        ylru, hst = lru_fwd(proj, name="lru_fwd", **small)
        if l == 0:
            s2, r2, rest, order = ag_forward(s1[4:], r1[4:], lands[1:], ylru, "ag_forward_0_rest")
        ymix, states = ret_fwd(proj, ylru, tables, row(ret_out_norm[l]), "ret_fwd")
        if l == 0:
            w = as_weights(w["w_in"], *ag_finish(s2, r2, rest, ymix, "ag_finish_0_rest"))
        gathered.append(w)
        h_mid = mm_nn_res(ymix, w["w_out"], h, order, "out_proj")
        hn2 = rmsnorm_fwd(h_mid, row(norm_ffn[l]), "rms_fwd")
        gate, up, act = ffn_up(hn2, w["w_gate"], w["w_up"], "ffn_up")
        if l + 1 < DEPTH:
            s1, r1, lands = level1[l + 1]
            s2, r2, lands, order = ag_forward(s1, r1, lands, act, f"ag_forward_{l + 1}")
        h_out = mm_nn_res(act, w["w_down"], h_mid, order, "ffn_down")
        if l + 1 < DEPTH:
            w_next = as_weights(*ag_finish(s2, r2, lands, h_out, f"ag_finish_{l + 1}"))
        saved.append(dict(h=h, hn1=hn1, proj=proj, hst=hst, states=states, ymix=ymix, h_mid=h_mid, hn2=hn2, gate=gate, up=up,
                          act=act, small=small))
        h = h_out
        if l + 1 < DEPTH:
            w = w_next

    loss_p, dh, g_norm_final = loss_head(h, row(norm_final), loss_target[0], "loss_head")
    loss = lax.psum(loss_p[0, 0], ("x", "y", "c"))

    rep = [None] * DEPTH
    convw_g = [None] * DEPTH
    inflight = []
    sib = None
    order = loss_p

    def sibling_done(l, tag, names, sib, after):
        parts, got = rs_sibling_wait(*sib, after, f"rs_sibling_wait_{tag}")
        sums = [pair_sum(a, r, c_arr, "pair_sum") for a, r in zip(parts, got)]
        flying, started = rs_chips_start(sums, f"rs_chips_start_{tag}")
        inflight.append((l, tag, names, flying))
        return started

    all5 = ("w_in", "w_gate", "w_up", "w_out", "w_down")
    for l in reversed(range(DEPTH)):
        w, s = gathered[l], saved[l]
        dgate, dup = ffn_down_bwd(dh, w["w_down"], s["gate"], s["up"], order, "ffn_down_bwd")
        dwd = mm_tn(s["act"], dh, FF_SHP, order, "dw_down").reshape(NDEV, FF_SHP, D)
        dwg = mm_tn_blocked(s["hn2"], dgate, "dw_blocked")
        dwu = mm_tn_blocked(s["hn2"], dup, "dw_blocked")
        if l == 0:
            ffn_sib, order = rs_sibling_start([dwg, dwu, dwd], "rs_sibling_start_0_ffn")
        dhn2 = mm_blocked_nt([(dgate, w["w_gate"]), (dup, w["w_up"])], order, "ffn_up_bwd")
        if sib is not None:
            order = sibling_done(l + 1, str(l + 1), all5, sib, dhn2)
        dh_mid, g_norm_ffn = rmsnorm_bwd(s["h_mid"], row(norm_ffn[l]), dhn2, dh, "rms_bwd")
        dymix = mm_nt(dh_mid, w["w_out"], F32, order, "out_proj_bwd")
        if l == 0:
            order = sibling_done(0, "0_ffn", ("w_gate", "w_up", "w_down"), ffn_sib, dymix)
        dwo = mm_tn(s["ymix"], dh_mid, 512, order, "dw_out").reshape(NDEV, OUT_SH, D)
        dxg, lvec, dwa, dwx = lru_bwd(s["proj"], s["hst"], dymix, name="lru_bwd", **s["small"])
        dproj, g_ret_norm = ret_bwd(s["proj"], s["states"], dymix, dxg, tables, row(ret_out_norm[l]), "ret_bwd")
        dwi = mm_tn_blocked(s["hn1"], dproj, "dw_blocked")
        dhn1 = mm_blocked_nt([(dproj, w["w_in"])], order, "proj_bwd")
        dh, g_norm_mix = rmsnorm_bwd(s["h"], row(norm_mix[l]), dhn1, dh_mid, "rms_bwd")

        rep[l] = [g_norm_mix, lvec[4], _diag_blocks(dwa), lvec[5], _diag_blocks(dwx), lvec[6], lvec[7], lvec[8], g_ret_norm,
                  g_norm_ffn]
        convw_g[l] = lvec[0:CONV_W]
        if l > 0:
            sib, order = rs_sibling_start([dwi, dwg, dwu, dwo, dwd], f"rs_sibling_start_{l}")
        else:
            sib, order = rs_sibling_start([dwi, dwo], "rs_sibling_start_0_mix")
    sibling_done(0, "0_mix", ("w_in", "w_out"), sib, dh)

    grad_x = dh[X0:][None]
    g_meta = dh[PAD:X0]

    rep_shapes = [(D,), (D_LRU,), (LRU_BLOCKS, LRU_BD, LRU_BD), (LRU_BLOCKS, LRU_BD), (LRU_BLOCKS, LRU_BD, LRU_BD),
                  (LRU_BLOCKS, LRU_BD), (D_LRU,), (D_LRU,), (D_RET,), (D,)]
    flat = [a for l in range(DEPTH) for a in rep[l]] + [g_norm_final] + [convw_g[l] for l in range(DEPTH)] + [g_meta]
    (gath,) = all_gather([_pack(flat)], "ag_grads")
    gsum = sum_devices(gath, "sum_devices")
    shapes = rep_shapes * DEPTH + [(D,)] + [(CONV_W, D_LRU)] * DEPTH + [(N_META, D)]
    parts = _unpack(gsum, shapes)
    nrep = len(rep_shapes)
    g_rep = {n: jnp.stack([parts[l * nrep + i] for l in range(DEPTH)]) for i, n in enumerate(REP_NAMES[:-1])}
    g_rep["norm_final"] = parts[DEPTH * nrep]
    g_convw = lax.dynamic_slice_in_dim(jnp.stack(parts[DEPTH * nrep + 1:DEPTH * nrep + 1 + DEPTH]), dev * (D_LRU // NDEV),
                                       D_LRU // NDEV, axis=2)
    g_metatok = lax.dynamic_slice_in_dim(parts[-1], dev * (D // NDEV), D // NDEV, axis=1)

    given = dict(norm_mix=(norm_mix, m_norm_mix, v_norm_mix), conv_b=(conv_b, m_conv_b, v_conv_b),
                 gate_a_w=(gate_a_w, m_gate_a_w, v_gate_a_w), gate_a_b=(gate_a_b, m_gate_a_b, v_gate_a_b),
                 gate_x_w=(gate_x_w, m_gate_x_w, v_gate_x_w), gate_x_b=(gate_x_b, m_gate_x_b, v_gate_x_b),
                 lru_lambda=(lru_lambda, m_lru_lambda, v_lru_lambda), lru_out_norm=(lru_out_norm, m_lru_out_norm, v_lru_out_norm),
                 ret_out_norm=(ret_out_norm, m_ret_out_norm, v_ret_out_norm), norm_ffn=(norm_ffn, m_norm_ffn, v_norm_ffn),
                 norm_final=(norm_final, m_norm_final, v_norm_final),
                 conv_w=(conv_w, m_conv_w, v_conv_w), meta_tokens=(meta_tokens, m_meta_tokens, v_meta_tokens))
    small_names = REP_NAMES + ["conv_w", "meta_tokens"]
    small_g = dict(g_rep, conv_w=g_convw, meta_tokens=g_metatok)
    small_shapes = [given[n][0].shape for n in small_names]
    packs = [_pack([small_g[n] for n in small_names])] + [_pack([given[n][k] for n in small_names]) for k in range(3)]
    upd = adamw_rows(*packs, "adamw_small")
    small_out = [dict(zip(small_names, _unpack(p, small_shapes))) for p in upd]

    arrived = {}
    for l, tag, names, flying in inflight:
        sums, recv = rs_chips_wait(*flying, upd[0], f"rs_chips_wait_{tag}")
        for i, n in enumerate(names):
            arrived[l, n] = (recv[i], sums[i])
    chip = jnp.reshape(2 * xi + yi, (1,)).astype(jnp.int32)

    def finish(wname, w_, m_, v_, tr):
        return adamw_big([arrived[l, wname][0] for l in range(DEPTH)], [arrived[l, wname][1] for l in range(DEPTH)], chip,
                         w_, m_, v_, tr, "adamw_" + wname)

    o_in = finish("w_in", w_in, m_w_in, v_w_in, 256)
    o_gate = finish("w_gate", w_gate, m_w_gate, v_w_gate, 256)
    o_up = finish("w_up", w_up, m_w_up, v_w_up, 256)
    o_out = finish("w_out", w_out, m_w_out, v_w_out, 64)
    o_down = finish("w_down", w_down, m_w_down, v_w_down, 32)

    bigs = dict(w_in=o_in, w_out=o_out, w_gate=o_gate, w_up=o_up, w_down=o_down)
    order = ["meta_tokens", "norm_mix", "w_in", "conv_w", "conv_b", "gate_a_w", "gate_a_b", "gate_x_w", "gate_x_b", "lru_lambda",
             "lru_out_norm", "ret_out_norm", "w_out", "norm_ffn", "w_gate", "w_up", "w_down", "norm_final"]
    grads = [bigs[n][0] if n in bigs else small_g[n] for n in order]
    rest = [[bigs[n][k + 1] if n in bigs else small_out[k][n] for n in order] for k in range(3)]
    return (loss, grad_x, *grads, *rest[0], *rest[1], *rest[2])
```

```python
import functools

import numpy as np
import jax
import jax.numpy as jnp
from jax import lax
from jax.experimental import pallas as pl
from jax.experimental.pallas import tpu as pltpu

F32, BF16 = jnp.float32, jnp.bfloat16
MXU_DTYPE = BF16
WIRE_DTYPE = BF16

D = 1024
SEQ = 2048
DEPTH = 4
N_META = 16
CH = 128
PAD = (-(SEQ + N_META)) % CH
T = SEQ + N_META + PAD
NCH = T // CH
X0 = PAD + N_META
D_LRU = 512
LRU_BLOCKS = 8
LRU_BD = 64
CONV_W = 4
LRU_C = 8.0
D_RET = 512
HEADS = 4
HD = 128
ROPE_BASE = 10000.0
D_IN = 3072
D_FF = 2816
NDEV = 8
IN_SH = D_IN // NDEV
FF_SH = D_FF // NDEV
FF_SHP = 384
D_FFP = NDEV * FF_SHP
OUT_SH = D // NDEV
EPS = 1e-6
TM = 544
TR = 1088
VMEM_LIMIT = 56 * 2**20
MESH = pl.DeviceIdType.MESH

ADAM_LR, ADAM_B1, ADAM_B2, ADAM_EPS, ADAM_WD, ADAM_STEP = 0.001, 0.9, 0.999, 1e-08, 0.01, 10

NN = ((1,), (0,))
NT = ((1,), (1,))
TN = ((0,), (0,))


def _dot(a, b, dims):
    return lax.dot_general(a.astype(MXU_DTYPE), b.astype(MXU_DTYPE), (dims, ((), ())), preferred_element_type=F32)


def _sds(shape, dtype):
    return jax.ShapeDtypeStruct(shape, dtype)


def _params(sem=None):
    return pltpu.CompilerParams(dimension_semantics=sem, vmem_limit_bytes=VMEM_LIMIT)


def _full(shape):
    n = len(shape)
    return pl.BlockSpec(shape, lambda *_: (0,) * n)


def rmsnorm_fwd(h, gain, name):
    def body(h_ref, g_ref, o_ref):
        x = h_ref[...]
        ms = jnp.mean(x * x, axis=-1, keepdims=True)
        o_ref[...] = (x * lax.rsqrt(ms + EPS) * g_ref[...]).astype(o_ref.dtype)

    return pl.pallas_call(
        body, name=name, grid=(T // TM,),
        in_specs=[pl.BlockSpec((TM, D), lambda i: (i, 0)), _full((1, D))],
        out_specs=pl.BlockSpec((TM, D), lambda i: (i, 0)),
        out_shape=_sds((T, D), MXU_DTYPE), compiler_params=_params(("parallel",)),
    )(h, gain)


def rmsnorm_bwd(h, gain, dhn, dres, name):
    def body(h_ref, g_ref, dhn_ref, dres_ref, dh_ref, dhb_ref, dg_ref):
        x = h_ref[...]
        rstd = lax.rsqrt(jnp.mean(x * x, axis=-1, keepdims=True) + EPS)
        xhat = x * rstd
        dy = dhn_ref[...]
        dyg = dy * g_ref[...]
        dh = dres_ref[...] + rstd * (dyg - xhat * jnp.mean(dyg * xhat, axis=-1, keepdims=True))
        dh_ref[...] = dh
        dhb_ref[...] = dh.astype(dhb_ref.dtype)

        @pl.when(pl.program_id(0) == 0)
        def _():
            dg_ref[...] = jnp.zeros_like(dg_ref)
        dg_ref[...] += jnp.sum(dy * xhat, axis=0, keepdims=True)

    row = pl.BlockSpec((TM, D), lambda i: (i, 0))
    return pl.pallas_call(
        body, name=name, grid=(T // TM,),
        in_specs=[row, _full((1, D)), row, row],
        out_specs=[row, row, _full((1, D))],
        out_shape=[_sds((T, D), F32), _sds((T, D), MXU_DTYPE), _sds((1, D), F32)], compiler_params=_params(("arbitrary",)),
    )(h, gain, dhn, dres)


def loss_head(h, gain, target, name):
    def body(h_ref, g_ref, t_ref, loss_ref, dh_ref, dhb_ref, dg_ref):
        i = pl.program_id(0)

        @pl.when(i == 0)
        def _():
            loss_ref[...] = jnp.zeros_like(loss_ref)
            dg_ref[...] = jnp.zeros_like(dg_ref)
            dh_ref[...] = jnp.zeros_like(dh_ref)
            dhb_ref[...] = jnp.zeros_like(dhb_ref)

        @pl.when(i > 0)
        def _():
            x = h_ref[...]
            g = g_ref[...]
            rstd = lax.rsqrt(jnp.mean(x * x, axis=-1, keepdims=True) + EPS)
            xhat = x * rstd
            err = xhat * g - t_ref[...]
            loss_ref[...] += 0.5 * jnp.sum(jnp.mean(err * err, axis=-1, keepdims=True), axis=0, keepdims=True)
            dy = err * (1.0 / D)
            dyg = dy * g
            dh = rstd * (dyg - xhat * jnp.mean(dyg * xhat, axis=-1, keepdims=True))
            dh_ref[...] = dh
            dhb_ref[...] = dh.astype(dhb_ref.dtype)
            dg_ref[...] += jnp.sum(dy * xhat, axis=0, keepdims=True)

    row = pl.BlockSpec((CH, D), lambda i: (i, 0))
    return pl.pallas_call(
        body, name=name, grid=(NCH,),
        in_specs=[row, _full((1, D)), pl.BlockSpec((CH, D), lambda i: (jnp.maximum(i - 1, 0), 0))],
        out_specs=[_full((8, 128)), row, row, _full((1, D))],
        out_shape=[_sds((8, 128), F32), _sds((T, D), F32), _sds((T, D), MXU_DTYPE), _sds((1, D), F32)],
        compiler_params=_params(("arbitrary",)),
    )(h, gain, target)


PAIR = 2 * IN_SH
NPAIR = NDEV // 2
BN = 256


def _pair_cols(w_ref):
    return jnp.concatenate([w_ref[0], w_ref[1]], axis=1)


W_PAIR = lambda k: pl.BlockSpec((2, k, IN_SH), lambda j: (j, 0, 0))
COLS_PAIR = pl.BlockSpec((T, PAIR), lambda j: (0, j))
ANYSPEC = pl.BlockSpec(memory_space=pl.ANY)


def mm_blocked_nn(a, w, out_dtype, name):
    k = a.shape[1]

    def body(a_ref, w_ref, o_ref):
        o_ref[...] = _dot(a_ref[...], _pair_cols(w_ref), NN).astype(o_ref.dtype)

    return pl.pallas_call(
        body, name=name, grid=(NPAIR,),
        in_specs=[_full((T, k)), W_PAIR(k)], out_specs=COLS_PAIR,
        out_shape=_sds((T, NDEV * IN_SH), out_dtype), compiler_params=_params(("parallel",)),
    )(a, w)


def mm_nn_res(a, w, res, after, name):
    k = a.shape[1]

    def body(a_ref, w_ref, r_ref, after_ref, o_ref):
        del after_ref
        o_ref[...] = r_ref[...] + _dot(a_ref[...], w_ref[...], NN)

    col = pl.BlockSpec((T, BN), lambda j: (0, j))
    return pl.pallas_call(
        body, name=name, grid=(D // BN,),
        in_specs=[_full((T, k)), pl.BlockSpec((k, BN), lambda j: (0, j)), col, ANYSPEC], out_specs=col,
        out_shape=_sds((T, D), F32), compiler_params=_params(("parallel",)),
    )(a, w, res, after)


def ffn_up(hn, wg, wu, name):
    def body(a_ref, wg_ref, wu_ref, g_ref, u_ref, act_ref):
        a = a_ref[...]
        g = _dot(a, _pair_cols(wg_ref), NN)
        u = _dot(a, _pair_cols(wu_ref), NN)
        g_ref[...] = g.astype(g_ref.dtype)
        u_ref[...] = u.astype(u_ref.dtype)
        act_ref[...] = (jax.nn.silu(g) * u).astype(act_ref.dtype)

    wspec = pl.BlockSpec((2, D, IN_SH), lambda j, i: (j, 0, 0))
    ospec = pl.BlockSpec((TR, PAIR), lambda j, i: (i, j))
    return pl.pallas_call(
        body, name=name, grid=(NPAIR, T // TR),
        in_specs=[pl.BlockSpec((TR, D), lambda j, i: (i, 0)), wspec, wspec], out_specs=[ospec] * 3,
        out_shape=[_sds((T, D_FFP), MXU_DTYPE)] * 3, compiler_params=_params(("parallel", "parallel")),
    )(hn, wg, wu)


def ffn_down_bwd(dh, wd, gate, up, after, name):
    def body(dh_ref, wd_ref, g_ref, u_ref, after_ref, dg_ref, du_ref):
        del after_ref
        dact = _dot(dh_ref[...], wd_ref[...], NT)
        g = g_ref[...].astype(F32)
        u = u_ref[...].astype(F32)
        sg = jax.nn.sigmoid(g)
        dg_ref[...] = (dact * u * (sg * (1.0 + g * (1.0 - sg)))).astype(dg_ref.dtype)
        du_ref[...] = (dact * (g * sg)).astype(du_ref.dtype)

    blk = pl.BlockSpec((TR, PAIR), lambda j, i: (i, j))
    return pl.pallas_call(
        body, name=name, grid=(NPAIR, T // TR),
        in_specs=[pl.BlockSpec((TR, D), lambda j, i: (i, 0)), pl.BlockSpec((PAIR, D), lambda j, i: (j, 0)), blk, blk, ANYSPEC],
        out_specs=[blk, blk],
        out_shape=[_sds((T, D_FFP), MXU_DTYPE)] * 2, compiler_params=_params(("parallel", "parallel")),
    )(dh, wd, gate, up, after)


def mm_nt(a, w, out_dtype, after, name):
    n = a.shape[1]

    def body(a_ref, w_ref, after_ref, o_ref):
        del after_ref
        o_ref[...] = _dot(a_ref[...], w_ref[...], NT).astype(o_ref.dtype)

    return pl.pallas_call(
        body, name=name, grid=(D // BN,),
        in_specs=[_full((T, n)), pl.BlockSpec((BN, n), lambda j: (j, 0)), ANYSPEC],
        out_specs=pl.BlockSpec((T, BN), lambda j: (0, j)),
        out_shape=_sds((T, D), out_dtype), compiler_params=_params(("parallel",)),
    )(a, w, after)


def mm_blocked_nt(pairs, after, name):
    n = len(pairs)

    def body(*refs):
        o_ref = refs[2 * n + 1]

        @pl.when(pl.program_id(1) == 0)
        def _():
            o_ref[...] = jnp.zeros_like(o_ref)
        for p in range(n):
            o_ref[...] += _dot(refs[2 * p][...], _pair_cols(refs[2 * p + 1]), NT)

    specs, args = [], []
    for a, w in pairs:
        specs += [pl.BlockSpec((TR, PAIR), lambda i, j: (i, j)), pl.BlockSpec((2, D, IN_SH), lambda i, j: (j, 0, 0))]
        args += [a, w]
    return pl.pallas_call(
        body, name=name, grid=(T // TR, NPAIR), in_specs=specs + [ANYSPEC],
        out_specs=pl.BlockSpec((TR, D), lambda i, j: (i, 0)),
        out_shape=_sds((T, D), F32), compiler_params=_params(("parallel", "arbitrary")),
    )(*args, after)


def mm_tn_blocked(a, b, name):
    def body(a_ref, b_ref, o_ref):
        o = _dot(a_ref[...], b_ref[...], TN).astype(o_ref.dtype)
        o_ref[0] = o[:, :IN_SH]
        o_ref[1] = o[:, IN_SH:]

    return pl.pallas_call(
        body, name=name, grid=(NPAIR,),
        in_specs=[_full((T, D)), COLS_PAIR], out_specs=W_PAIR(D),
        out_shape=_sds((NDEV, D, IN_SH), WIRE_DTYPE), compiler_params=_params(("parallel",)),
    )(a, b)


def mm_tn(a, b, bm, after, name):
    m = a.shape[1]

    def body(a_ref, b_ref, after_ref, o_ref):
        del after_ref
        o_ref[...] = _dot(a_ref[...], b_ref[...], TN).astype(o_ref.dtype)

    return pl.pallas_call(
        body, name=name, grid=(m // bm,),
        in_specs=[pl.BlockSpec((T, bm), lambda i: (0, i)), _full((T, D)), ANYSPEC],
        out_specs=pl.BlockSpec((bm, D), lambda i: (i, 0)),
        out_shape=_sds((m, D), WIRE_DTYPE), compiler_params=_params(("parallel",)),
    )(a, b, after)


def _softplus_neg(lam):
    return jnp.maximum(-lam, 0.0) + jnp.log1p(jnp.exp(-jnp.abs(lam)))


def _lru_gates(pa, px, xc, lam):
    r = jax.nn.sigmoid(pa)
    ig = jax.nn.sigmoid(px)
    log_a = -LRU_C * r * _softplus_neg(lam)
    a = jnp.exp(log_a)
    mult = jnp.sqrt(-jnp.tanh(log_a) * (jnp.exp(2.0 * log_a) + 1.0))
    return a, mult * (ig * xc)


def _lru_out(h, g, gain):
    z = h * jax.nn.gelu(g)
    return z * lax.rsqrt(jnp.mean(z * z, axis=-1, keepdims=True) + EPS) * gain


def _conv_taps(x, xprev, row):
    taps = [x]
    for s in range(1, CONV_W):
        taps.append(jnp.where(row < s, pltpu.roll(xprev, s, 0), pltpu.roll(x, s, 0)))
    return taps


def _conv(taps, cw_ref, cb):
    xc = cb + cw_ref[CONV_W - 1:CONV_W, :] * taps[0]
    for s in range(1, CONV_W):
        xc = xc + cw_ref[CONV_W - 1 - s:CONV_W - s, :] * taps[s]
    return xc


def lru_fwd(proj, cw, cb, wa, ba, wx, bx, lam, gain, name):
    def body(x_ref, g_ref, cw_ref, cb_ref, wa_ref, ba_ref, wx_ref, bx_ref, lam_ref, gain_ref,
             y_ref, h_ref, xprev_scr, a_scr, b_scr, carry_scr):
        i = pl.program_id(0)

        @pl.when(i == 0)
        def _():
            xprev_scr[...] = jnp.zeros_like(xprev_scr)
            carry_scr[...] = jnp.zeros_like(carry_scr)

        x = x_ref[...]
        row = lax.broadcasted_iota(jnp.int32, (CH, D_LRU), 0)
        xc = _conv(_conv_taps(x, xprev_scr[...], row), cw_ref, cb_ref[...])
        pa = _dot(xc, wa_ref[...], NN) + ba_ref[...]
        px = _dot(xc, wx_ref[...], NN) + bx_ref[...]
        a, b = _lru_gates(pa, px, xc, lam_ref[...])
        a_scr[...] = a
        b_scr[...] = jnp.where(i * CH + row >= PAD, b, 0.0)
        h = carry_scr[...]
        for t in range(CH):
            h = a_scr[t:t + 1, :] * h + b_scr[t:t + 1, :]
            h_ref[t:t + 1, :] = h
        carry_scr[...] = h
        xprev_scr[...] = x
        y_ref[...] = _lru_out(h_ref[...], g_ref[...], gain_ref[...]).astype(y_ref.dtype)

    vec = _full((1, D_LRU))
    mat = _full((D_LRU, D_LRU))
    return pl.pallas_call(
        body, name=name, grid=(NCH,),
        in_specs=[pl.BlockSpec((CH, D_LRU), lambda i: (i, 0)), pl.BlockSpec((CH, D_LRU), lambda i: (i, 1)),
                  _full((CONV_W, D_LRU)), vec, mat, vec, mat, vec, vec, vec],
        out_specs=[pl.BlockSpec((CH, D_LRU), lambda i: (i, 0)), pl.BlockSpec((CH, D_LRU), lambda i: (i, 0))],
        out_shape=[_sds((T, D_LRU), MXU_DTYPE), _sds((T, D_LRU), F32)],
        scratch_shapes=[pltpu.VMEM((CH, D_LRU), F32), pltpu.VMEM((CH, D_LRU), F32), pltpu.VMEM((CH, D_LRU), F32),
                        pltpu.VMEM((1, D_LRU), F32)],
        compiler_params=_params(("arbitrary",)),
    )(proj, proj, cw, cb, wa, ba, wx, bx, lam, gain)


LRU_VEC_ROWS = 16


def lru_bwd(proj, hst, dymix, cw, cb, wa, ba, wx, bx, lam, gain, name):
    last = NCH - 1

    def body(x_ref, xp_ref, g_ref, h_ref, hp_ref, dy_ref, cw_ref, cb_ref, wa_ref, ba_ref, wx_ref, bx_ref, lam_ref,
             gain_ref, dxg_ref, vec_ref, dwa_ref, dwx_ref, a_scr, dh_scr, g_scr, carry_scr, dxcn_scr):
        i = pl.program_id(0)
        ib = last - i

        @pl.when(i == 0)
        def _():
            carry_scr[...] = jnp.zeros_like(carry_scr)
            dxcn_scr[...] = jnp.zeros_like(dxcn_scr)
            vec_ref[...] = jnp.zeros_like(vec_ref)
            dwa_ref[...] = jnp.zeros_like(dwa_ref)
            dwx_ref[...] = jnp.zeros_like(dwx_ref)

        x = x_ref[...]
        row = lax.broadcasted_iota(jnp.int32, (CH, D_LRU), 0)
        valid = ib * CH + row >= PAD
        taps = _conv_taps(x, xp_ref[...], row)
        xc = _conv(taps, cw_ref, cb_ref[...])
        pa = _dot(xc, wa_ref[...], NN) + ba_ref[...]
        px = _dot(xc, wx_ref[...], NN) + bx_ref[...]
        (a, _), vjp_gates = jax.vjp(_lru_gates, pa, px, xc, lam_ref[...])
        h = h_ref[...]
        _, vjp_out = jax.vjp(_lru_out, h, g_ref[...], gain_ref[...])
        dh, dg, dgain = vjp_out(dy_ref[...].astype(F32))
        a_scr[...] = a
        dh_scr[...] = dh
        c = carry_scr[...]
        for t in range(CH - 1, -1, -1):
            gt = dh_scr[t:t + 1, :] + c
            g_scr[t:t + 1, :] = gt
            c = a_scr[t:t + 1, :] * gt
        carry_scr[...] = c
        gg = g_scr[...]
        hprev = jnp.where(row < 1, pltpu.roll(hp_ref[...], 1, 0), pltpu.roll(h, 1, 0))
        da = jnp.where(valid, gg * hprev, 0.0)
        db = jnp.where(valid, gg, 0.0)
        dpa, dpx, dxc, dlam = vjp_gates((da, db))
        dxc = dxc + _dot(dpa, wa_ref[...], NT) + _dot(dpx, wx_ref[...], NT)
        dwa_ref[...] += _dot(xc, dpa, TN)
        dwx_ref[...] += _dot(xc, dpx, TN)
        for s in range(CONV_W):
            vec_ref[CONV_W - 1 - s:CONV_W - s, :] += jnp.sum(dxc * taps[s], axis=0, keepdims=True)
        vec_ref[4:5, :] += jnp.sum(dxc, axis=0, keepdims=True)
        vec_ref[5:6, :] += jnp.sum(dpa, axis=0, keepdims=True)
        vec_ref[6:7, :] += jnp.sum(dpx, axis=0, keepdims=True)
        vec_ref[7:8, :] += dlam
        vec_ref[8:9, :] += dgain
        dxn = dxcn_scr[...]
        dx = cw_ref[CONV_W - 1:CONV_W, :] * dxc
        for s in range(1, CONV_W):
            ahead = jnp.where(row >= CH - s, pltpu.roll(dxn, CH - s, 0), pltpu.roll(dxc, CH - s, 0))
            dx = dx + cw_ref[CONV_W - 1 - s:CONV_W - s, :] * ahead
        dxcn_scr[...] = dxc
        dxg_ref[:, :D_LRU] = jnp.where(valid, dx, 0.0).astype(dxg_ref.dtype)
        dxg_ref[:, D_LRU:] = dg.astype(dxg_ref.dtype)

    vec = _full((1, D_LRU))
    mat = _full((D_LRU, D_LRU))

    def blk(col, shift=0):
        return pl.BlockSpec((CH, D_LRU), lambda i: (jnp.maximum(last - i - shift, 0), col))

    return pl.pallas_call(
        body, name=name, grid=(NCH,),
        in_specs=[blk(0), blk(0, 1), blk(1), blk(0), blk(0, 1), blk(0),
                  _full((CONV_W, D_LRU)), vec, mat, vec, mat, vec, vec, vec],
        out_specs=[pl.BlockSpec((CH, 2 * D_LRU), lambda i: (last - i, 0)), _full((LRU_VEC_ROWS, D_LRU)), mat, mat],
        out_shape=[_sds((T, 2 * D_LRU), MXU_DTYPE), _sds((LRU_VEC_ROWS, D_LRU), F32),
                   _sds((D_LRU, D_LRU), F32), _sds((D_LRU, D_LRU), F32)],
        scratch_shapes=[pltpu.VMEM((CH, D_LRU), F32), pltpu.VMEM((CH, D_LRU), F32), pltpu.VMEM((CH, D_LRU), F32),
                        pltpu.VMEM((1, D_LRU), F32), pltpu.VMEM((CH, D_LRU), F32)],
        compiler_params=_params(("arbitrary",)),
    )(proj, proj, proj, hst, hst, dymix, cw, cb, wa, ba, wx, bx, lam, gain)


def _ret_tables():
    half = HD // 2
    pos = jnp.arange(T, dtype=F32) - float(PAD)
    inv = ROPE_BASE ** (-jnp.arange(half, dtype=F32) / half)
    ang = pos[:, None] * inv[None, :]
    cos = jnp.concatenate([jnp.cos(ang), jnp.cos(ang)], axis=-1)
    sin = jnp.concatenate([-jnp.sin(ang), jnp.sin(ang)], axis=-1)
    log_g = jnp.log(1.0 - 2.0 ** (-5.0 - jnp.arange(HEADS, dtype=F32)))
    idx = jnp.arange(CH, dtype=F32)
    diff = idx[:, None] - idx[None, :]
    dmask = jnp.where(diff[None] >= 0, jnp.exp(jnp.maximum(diff, 0.0)[None] * log_g[:, None, None]), 0.0)
    xi = jnp.exp((idx + 1.0)[None, :] * log_g[:, None])
    zeta = jnp.exp((CH - 1.0 - idx)[None, :] * log_g[:, None])
    xi = jnp.broadcast_to(xi[:, :, None], (HEADS, CH, HD))
    zeta = jnp.broadcast_to(zeta[:, :, None], (HEADS, CH, HD))
    return cos, sin, dmask, xi, zeta


def _chunk_decay():
    log_g = np.log(np.float32(1.0) - np.float32(2.0) ** (np.float32(-5.0) - np.arange(HEADS, dtype=np.float32)))
    return [float(v) for v in np.exp(np.float32(CH) * log_g.astype(np.float32))]


def _rope(x, cos, sin):
    return x * cos + pltpu.roll(x, HD // 2, 1) * sin


def ret_fwd(proj, ylru, tables, gain, name):
    cos, sin, dmask, xi, zeta = tables
    gch = _chunk_decay()
    scale = HD ** -0.5

    def body(q_ref, k_ref, v_ref, g_ref, cos_ref, sin_ref, dm_ref, xi_ref, zt_ref, gain_ref, ylru_ref,
             y_ref, st_ref, s_scr):
        @pl.when(pl.program_id(0) == 0)
        def _():
            s_scr[...] = jnp.zeros_like(s_scr)

        y_ref[:, :D_LRU] = ylru_ref[...]
        cs, sn = cos_ref[...], sin_ref[...]
        for h in range(HEADS):
            sl = slice(HD * h, HD * (h + 1))
            so = slice(D_LRU + HD * h, D_LRU + HD * (h + 1))
            qr = _rope(q_ref[:, sl], cs, sn)
            kr = _rope(k_ref[:, sl], cs, sn) * scale
            v = v_ref[:, sl]
            s = s_scr[h]
            st_ref[h] = s
            sc = _dot(qr, kr, NT) * dm_ref[h]
            y = _dot(sc, v, NN) + _dot(qr, s, NN) * xi_ref[h]
            s_scr[h] = s * gch[h] + _dot(kr * zt_ref[h], v, TN)
            yc = y - jnp.mean(y, axis=-1, keepdims=True)
            yn = yc * lax.rsqrt(jnp.mean(yc * yc, axis=-1, keepdims=True) + EPS)
            y_ref[:, so] = (jax.nn.silu(g_ref[:, sl]) * (yn * gain_ref[:, sl])).astype(y_ref.dtype)

    def col(c):
        return pl.BlockSpec((CH, D_RET), lambda n: (n, c))

    tab = pl.BlockSpec((CH, HD), lambda n: (n, 0))
    cst = _full((HEADS, CH, HD))
    return pl.pallas_call(
        body, name=name, grid=(NCH,),
        in_specs=[col(2), col(3), col(4), col(5), tab, tab, cst, cst, cst, _full((1, D_RET)), col(0)],
        out_specs=[pl.BlockSpec((CH, D), lambda n: (n, 0)), pl.BlockSpec((None, HEADS, HD, HD), lambda n: (n, 0, 0, 0))],
        out_shape=[_sds((T, D), MXU_DTYPE), _sds((NCH, HEADS, HD, HD), F32)],
        scratch_shapes=[pltpu.VMEM((HEADS, HD, HD), F32)],
        compiler_params=_params(("arbitrary",)),
    )(proj, proj, proj, proj, cos, sin, dmask, xi, zeta, gain, ylru)


def ret_bwd(proj, states, dymix, dxg, tables, gain, name):
    cos, sin, dmask, xi, zeta = tables
    gch = _chunk_decay()
    scale = HD ** -0.5
    last = NCH - 1

    def body(q_ref, k_ref, v_ref, g_ref, st_ref, do_ref, cos_ref, sin_ref, dm_ref, xi_ref, zt_ref, gain_ref, dxg_ref,
             dp_ref, dgain_ref, ds_scr):
        @pl.when(pl.program_id(0) == 0)
        def _():
            ds_scr[...] = jnp.zeros_like(ds_scr)
            dgain_ref[...] = jnp.zeros_like(dgain_ref)

        dp_ref[:, :2 * D_LRU] = dxg_ref[...]
        cs, sn = cos_ref[...], sin_ref[...]
        for h in range(HEADS):
            sl = slice(HD * h, HD * (h + 1))
            oq, ok, ov, og = (slice(2 * D_LRU + j * D_RET + HD * h, 2 * D_LRU + j * D_RET + HD * (h + 1)) for j in range(4))
            qr = _rope(q_ref[:, sl], cs, sn)
            kr = _rope(k_ref[:, sl], cs, sn) * scale
            v = v_ref[:, sl]
            g = g_ref[:, sl]
            gain = gain_ref[:, sl]
            dm, x_i, zt = dm_ref[h], xi_ref[h], zt_ref[h]
            s = st_ref[h]
            ds = ds_scr[h]
            kz = kr * zt
            sc = _dot(qr, kr, NT) * dm
            y = _dot(sc, v, NN) + _dot(qr, s, NN) * x_i
            yc = y - jnp.mean(y, axis=-1, keepdims=True)
            rstd = lax.rsqrt(jnp.mean(yc * yc, axis=-1, keepdims=True) + EPS)
            yn = yc * rstd
            sg = jax.nn.sigmoid(g)
            silu = g * sg
            dout = do_ref[:, sl].astype(F32)
            dgain_ref[:, sl] += jnp.sum(dout * silu * yn, axis=0, keepdims=True)
            dp_ref[:, og] = (dout * yn * gain * (sg * (1.0 + g * (1.0 - sg)))).astype(dp_ref.dtype)
            dyn = dout * silu * gain
            dy = rstd * (dyn - jnp.mean(dyn, axis=-1, keepdims=True) - yn * jnp.mean(dyn * yn, axis=-1, keepdims=True))
            dp = _dot(dy, v, NT) * dm
            dv = _dot(sc, dy, TN) + _dot(kz, ds, NN)
            dqs = dy * x_i
            dqr = _dot(dp, kr, NN) + _dot(dqs, s, NT)
            dkr = _dot(dp, qr, TN) + _dot(v, ds, NT) * zt
            ds_scr[h] = gch[h] * ds + _dot(qr, dqs, TN)
            dp_ref[:, oq] = (dqr * cs + pltpu.roll(dqr * sn, HD // 2, 1)).astype(dp_ref.dtype)
            dp_ref[:, ok] = ((dkr * cs + pltpu.roll(dkr * sn, HD // 2, 1)) * scale).astype(dp_ref.dtype)
            dp_ref[:, ov] = dv.astype(dp_ref.dtype)

    def col(c):
        return pl.BlockSpec((CH, D_RET), lambda n: (last - n, c))

    tab = pl.BlockSpec((CH, HD), lambda n: (last - n, 0))
    cst = _full((HEADS, CH, HD))
    return pl.pallas_call(
        body, name=name, grid=(NCH,),
        in_specs=[col(2), col(3), col(4), col(5), pl.BlockSpec((None, HEADS, HD, HD), lambda n: (last - n, 0, 0, 0)), col(1),
                  tab, tab, cst, cst, cst, _full((1, D_RET)), pl.BlockSpec((CH, 2 * D_LRU), lambda n: (last - n, 0))],
        out_specs=[pl.BlockSpec((CH, D_IN), lambda n: (last - n, 0)), _full((1, D_RET))],
        out_shape=[_sds((T, D_IN), MXU_DTYPE), _sds((1, D_RET), F32)],
        scratch_shapes=[pltpu.VMEM((HEADS, HD, HD), F32)],
        compiler_params=_params(("arbitrary",)),
    )(proj, proj, proj, proj, states, dymix, cos, sin, dmask, xi, zeta, gain, dxg)


HBM = pl.BlockSpec(memory_space=pltpu.HBM)


def _place():
    return lax.axis_index("x"), lax.axis_index("y"), lax.axis_index("c")


def all_gather(arrs, name):
    n = len(arrs)

    def body(*refs):
        ins, outs = refs[:n], refs[n:2 * n]
        send_sems, recv_sems, local_sems = refs[2 * n:]
        x, y, c = _place()
        me, sibling = (x, y, c), (x, y, 1 - c)
        chips = [(1 - x, y), (x, 1 - y), (1 - x, 1 - y)]

        def copy(a, k, block, to, src=None):
            px, py, pc = block
            dst = outs[a].at[4 * px + 2 * py + pc]
            return pltpu.make_async_remote_copy(
                src_ref=dst if src is None else src, dst_ref=dst, send_sem=send_sems.at[a, k], recv_sem=recv_sems.at[a, k],
                device_id=to, device_id_type=MESH)

        mine = [pltpu.make_async_copy(ins[a], outs[a].at[4 * x + 2 * y + c], local_sems.at[a]) for a in range(n)]
        for cp in mine:
            cp.start()
        first = []
        for a in range(n):
            first.append(copy(a, 0, me, sibling, src=ins[a]))
            first += [copy(a, 1 + j, me, (*chip, c), src=ins[a]) for j, chip in enumerate(chips)]
        for cp in first:
            cp.start()
        passed = []
        for j, chip in enumerate(chips):
            for a in range(n):
                copy(a, 1 + j, (*chip, c), me).wait_recv()
                passed.append(copy(a, 4 + j, (*chip, c), sibling))
                passed[-1].start()
        for a in range(n):
            copy(a, 0, sibling, me).wait_recv()
            for j, chip in enumerate(chips):
                copy(a, 4 + j, (*chip, 1 - c), me).wait_recv()
        for cp in first + passed:
            cp.wait_send()
        for cp in mine:
            cp.wait()

    return pl.pallas_call(
        body, name=name,
        in_specs=[HBM] * n, out_specs=[HBM] * n,
        out_shape=[_sds((NDEV,) + a.shape, a.dtype) for a in arrs],
        scratch_shapes=[pltpu.SemaphoreType.DMA((n, 7)), pltpu.SemaphoreType.DMA((n, 7)), pltpu.SemaphoreType.DMA((n,))],
    )(*arrs)


SEM = pl.BlockSpec(memory_space=pltpu.SEMAPHORE)
ANY = pl.BlockSpec(memory_space=pl.ANY)
EFFECT = pltpu.SideEffectType.DATAFLOW_SIDE_EFFECTING


def _hbm(a):
    return pltpu.with_memory_space_constraint(a, pltpu.HBM)


def _hbm_like(arrs):
    return [pltpu.HBM(a.shape, a.dtype) for a in arrs]


def _dma_sems(count):
    return [pltpu.SemaphoreType.DMA(())] * count


def _ag_copy(lands, send_sems, recv_sems, per):
    def copy(a, k, block, to, src=None):
        px, py, pc = block
        dst = lands[a].at[4 * px + 2 * py + pc]
        return pltpu.make_async_remote_copy(
            src_ref=dst if src is None else src, dst_ref=dst, send_sem=send_sems[a * per + k], recv_sem=recv_sems[a * per + k],
            device_id=to, device_id_type=MESH)
    return copy


def to_wire(sel, w_in, w_gate, w_up, w_out, w_down, name):
    ffpad = FF_SHP - FF_SH

    def body(sel_ref, i_ref, g_ref, u_ref, o_ref, d_ref, oi, og, ou, oo, od):
        del sel_ref
        oi[...] = i_ref[...].astype(oi.dtype)
        oo[...] = o_ref[...].astype(oo.dtype)
        for src, dst in ((g_ref, og), (u_ref, ou)):
            dst[:, :FF_SH] = src[...].astype(dst.dtype)
            dst[:, FF_SH:] = jnp.zeros((D, ffpad), dst.dtype)
        od[:FF_SH, :] = d_ref[...].astype(od.dtype)
        od[FF_SH:, :] = jnp.zeros((ffpad, D), od.dtype)

    shapes_in = [(D, IN_SH), (D, FF_SH), (D, FF_SH), (OUT_SH, D), (FF_SH, D)]
    shapes_out = [(D, IN_SH), (D, FF_SHP), (D, FF_SHP), (OUT_SH, D), (FF_SHP, D)]
    return pl.pallas_call(
        body, name=name,
        grid_spec=pltpu.PrefetchScalarGridSpec(
            num_scalar_prefetch=1, grid=(1,),
            in_specs=[pl.BlockSpec((None,) + s, lambda i, sel_ref: (sel_ref[1], 0, 0)) for s in shapes_in],
            out_specs=[pl.BlockSpec((None,) + s, lambda i, sel_ref: (sel_ref[0], 0, 0)) for s in shapes_out]),
        out_shape=[_sds((NDEV,) + s, WIRE_DTYPE) for s in shapes_out], compiler_params=_params(("arbitrary",)),
    )(sel, w_in, w_gate, w_up, w_out, w_down)


def ag_start(lands, after, name):
    n = len(lands)
    ns = 4 * n

    def body(*refs):
        lnd = refs[:n]
        send_sems, recv_sems = refs[n + 1:n + 1 + ns], refs[n + 1 + ns:n + 1 + 2 * ns]
        token = refs[-1]
        x, y, c = _place()
        me, sibling = (x, y, c), (x, y, 1 - c)
        chips = [(1 - x, y), (x, 1 - y), (1 - x, 1 - y)]
        copy = _ag_copy(lnd, send_sems, recv_sems, 4)
        for a in range(n):
            copy(a, 0, me, sibling).start()
            for j, chip in enumerate(chips):
                copy(a, 1 + j, me, (*chip, c)).start()
        token[...] = jnp.zeros_like(token)

    outs = pl.pallas_call(
        body, name=name,
        in_specs=[HBM] * n + [ANY],
        out_specs=[SEM] * (2 * ns) + [HBM] * n + [pl.BlockSpec(memory_space=pltpu.VMEM)],
        out_shape=_dma_sems(2 * ns) + _hbm_like(lands) + [_sds((8, 128), F32)],
        input_output_aliases={i: 2 * ns + i for i in range(n)},
        compiler_params=pltpu.CompilerParams(has_side_effects=EFFECT),
    )(*[_hbm(a) for a in lands], after)
    return outs[:ns], outs[ns:2 * ns], outs[2 * ns:2 * ns + n], outs[-1]


def ag_forward(send_sems, recv_sems, lands, after, name):
    n = len(lands)
    n1, n2 = 4 * n, 3 * n

    def body(*refs):
        lnd = refs[:n]
        o = n
        s1, r1 = refs[o:o + n1], refs[o + n1:o + 2 * n1]
        o += 2 * n1 + 1
        s2, r2 = refs[o:o + n2], refs[o + n2:o + 2 * n2]
        token = refs[-1]
        token[...] = jnp.zeros_like(token)
        x, y, c = _place()
        me, sibling = (x, y, c), (x, y, 1 - c)
        chips = [(1 - x, y), (x, 1 - y), (1 - x, 1 - y)]
        copy1 = _ag_copy(lnd, s1, r1, 4)
        copy2 = _ag_copy(lnd, s2, r2, 3)
        for j, chip in enumerate(chips):
            for a in range(n):
                copy1(a, 1 + j, (*chip, c), me).wait_recv()
                copy2(a, j, (*chip, c), sibling).start()
        for a in range(n):
            copy1(a, 0, sibling, me).wait_recv()
            copy1(a, 0, me, sibling).wait_send()
            for j, chip in enumerate(chips):
                copy1(a, 1 + j, me, (*chip, c)).wait_send()

    outs = pl.pallas_call(
        body, name=name,
        in_specs=[HBM] * n + [SEM] * (2 * n1) + [ANY],
        out_specs=[SEM] * (2 * n2) + [HBM] * n + [pl.BlockSpec(memory_space=pltpu.VMEM)],
        out_shape=_dma_sems(2 * n2) + _hbm_like(lands) + [_sds((8, 128), F32)],
        input_output_aliases={i: 2 * n2 + i for i in range(n)},
        compiler_params=pltpu.CompilerParams(has_side_effects=EFFECT),
    )(*lands, *send_sems, *recv_sems, after)
    return outs[:n2], outs[n2:2 * n2], outs[2 * n2:2 * n2 + n], outs[-1]


def ag_finish(send_sems, recv_sems, lands, after, name):
    n = len(lands)
    n2 = 3 * n

    def body(*refs):
        lnd = refs[:n]
        s2, r2 = refs[n:n + n2], refs[n + n2:n + 2 * n2]
        x, y, c = _place()
        me, sibling = (x, y, c), (x, y, 1 - c)
        chips = [(1 - x, y), (x, 1 - y), (1 - x, 1 - y)]
        copy2 = _ag_copy(lnd, s2, r2, 3)
        for a in range(n):
            for j, chip in enumerate(chips):
                copy2(a, j, (*chip, c), sibling).wait_send()
                copy2(a, j, (*chip, 1 - c), me).wait_recv()

    outs = pl.pallas_call(
        body, name=name,
        in_specs=[HBM] * n + [SEM] * (2 * n2) + [ANY],
        out_specs=[HBM] * n, out_shape=_hbm_like(lands),
        input_output_aliases={i: i for i in range(n)},
        compiler_params=pltpu.CompilerParams(has_side_effects=EFFECT),
    )(*lands, *send_sems, *recv_sems, after)
    return list(outs)


def rs_sibling_start(arrs, name):
    n = len(arrs)
    ns = 4 * n
    lands = [lax.empty((4,) + a.shape[1:], a.dtype) for a in arrs]

    def body(*refs):
        ins, lnd = refs[:n], refs[n:2 * n]
        send_sems, recv_sems = refs[2 * n:2 * n + ns], refs[2 * n + ns:2 * n + 2 * ns]
        x, y, c = _place()
        sibling = (x, y, 1 - c)
        for a in range(n):
            for p in range(4):
                pltpu.make_async_remote_copy(
                    src_ref=ins[a].at[2 * p + 1 - c], dst_ref=lnd[a].at[p], send_sem=send_sems[4 * a + p],
                    recv_sem=recv_sems[4 * a + p], device_id=sibling, device_id_type=MESH).start()
        refs[-1][...] = jnp.zeros_like(refs[-1])

    outs = pl.pallas_call(
        body, name=name,
        in_specs=[HBM] * (2 * n), out_specs=[SEM] * (2 * ns) + [HBM] * (2 * n) + [pl.BlockSpec(memory_space=pltpu.VMEM)],
        out_shape=_dma_sems(2 * ns) + _hbm_like(arrs) + _hbm_like(lands) + [_sds((8, 128), F32)],
        input_output_aliases={i: 2 * ns + i for i in range(2 * n)},
        compiler_params=pltpu.CompilerParams(has_side_effects=EFFECT),
    )(*[_hbm(a) for a in arrs], *[_hbm(a) for a in lands])
    return (outs[:ns], outs[ns:2 * ns], outs[2 * ns:2 * ns + n], outs[2 * ns + n:2 * ns + 2 * n]), outs[-1]


def rs_sibling_wait(send_sems, recv_sems, arrs, lands, after, name):
    n = len(arrs)
    ns = 4 * n

    def body(*refs):
        ins, lnd = refs[:n], refs[n:2 * n]
        s, r = refs[2 * n:2 * n + ns], refs[2 * n + ns:2 * n + 2 * ns]
        x, y, c = _place()
        sibling = (x, y, 1 - c)
        for a in range(n):
            for p in range(4):
                cp = pltpu.make_async_remote_copy(
                    src_ref=ins[a].at[2 * p + 1 - c], dst_ref=lnd[a].at[p], send_sem=s[4 * a + p], recv_sem=r[4 * a + p],
                    device_id=sibling, device_id_type=MESH)
                cp.wait_send()
                cp.wait_recv()

    outs = pl.pallas_call(
        body, name=name,
        in_specs=[HBM] * (2 * n) + [SEM] * (2 * ns) + [ANY], out_specs=[HBM] * (2 * n),
        out_shape=_hbm_like(arrs) + _hbm_like(lands),
        input_output_aliases={i: i for i in range(2 * n)},
        compiler_params=pltpu.CompilerParams(has_side_effects=EFFECT),
    )(*arrs, *lands, *send_sems, *recv_sems, after)
    return outs[:n], outs[n:]


def rs_chips_start(parts, name):
    n = len(parts)
    ns = 3 * n
    lands = [lax.empty((3,) + a.shape[1:], a.dtype) for a in parts]

    def body(*refs):
        ins, lnd = refs[:n], refs[n:2 * n]
        send_sems, recv_sems = refs[2 * n:2 * n + ns], refs[2 * n + ns:2 * n + 2 * ns]
        x, y, c = _place()
        chips = [(1 - x, y), (x, 1 - y), (1 - x, 1 - y)]
        for a in range(n):
            for k, (tx, ty) in enumerate(chips):
                pltpu.make_async_remote_copy(
                    src_ref=ins[a].at[2 * tx + ty], dst_ref=lnd[a].at[k], send_sem=send_sems[3 * a + k],
                    recv_sem=recv_sems[3 * a + k], device_id=(tx, ty, c), device_id_type=MESH).start()
        refs[-1][...] = jnp.zeros_like(refs[-1])

    outs = pl.pallas_call(
        body, name=name,
        in_specs=[HBM] * (2 * n), out_specs=[SEM] * (2 * ns) + [HBM] * (2 * n) + [pl.BlockSpec(memory_space=pltpu.VMEM)],
        out_shape=_dma_sems(2 * ns) + _hbm_like(parts) + _hbm_like(lands) + [_sds((8, 128), F32)],
        input_output_aliases={i: 2 * ns + i for i in range(2 * n)},
        compiler_params=pltpu.CompilerParams(has_side_effects=EFFECT),
    )(*[_hbm(a) for a in parts], *[_hbm(a) for a in lands])
    return (outs[:ns], outs[ns:2 * ns], outs[2 * ns:2 * ns + n], outs[2 * ns + n:2 * ns + 2 * n]), outs[-1]


def rs_chips_wait(send_sems, recv_sems, parts, lands, after, name):
    n = len(parts)
    ns = 3 * n

    def body(*refs):
        ins, lnd = refs[:n], refs[n:2 * n]
        s, r = refs[2 * n:2 * n + ns], refs[2 * n + ns:2 * n + 2 * ns]
        x, y, c = _place()
        chips = [(1 - x, y), (x, 1 - y), (1 - x, 1 - y)]
        for a in range(n):
            for k, (tx, ty) in enumerate(chips):
                cp = pltpu.make_async_remote_copy(
                    src_ref=ins[a].at[2 * tx + ty], dst_ref=lnd[a].at[k], send_sem=s[3 * a + k], recv_sem=r[3 * a + k],
                    device_id=(tx, ty, c), device_id_type=MESH)
                cp.wait_send()
                cp.wait_recv()

    outs = pl.pallas_call(
        body, name=name,
        in_specs=[HBM] * (2 * n) + [SEM] * (2 * ns) + [ANY], out_specs=[HBM] * (2 * n),
        out_shape=_hbm_like(parts) + _hbm_like(lands),
        input_output_aliases={i: i for i in range(2 * n)},
        compiler_params=pltpu.CompilerParams(has_side_effects=EFFECT),
    )(*parts, *lands, *send_sems, *recv_sems, after)
    return outs[:n], outs[n:]


def pair_sum(a, r, c, name):
    _, rr, cc = a.shape

    def body(c_ref, a_ref, r_ref, o_ref):
        del c_ref
        o_ref[...] = (a_ref[...].astype(F32) + r_ref[...].astype(F32)).astype(o_ref.dtype)

    return pl.pallas_call(
        body, name=name,
        grid_spec=pltpu.PrefetchScalarGridSpec(
            num_scalar_prefetch=1, grid=(4,),
            in_specs=[pl.BlockSpec((None, rr, cc), lambda p, c_ref: (2 * p + c_ref[0], 0, 0)),
                      pl.BlockSpec((None, rr, cc), lambda p, c_ref: (p, 0, 0))],
            out_specs=pl.BlockSpec((None, rr, cc), lambda p, c_ref: (p, 0, 0))),
        out_shape=_sds((4, rr, cc), a.dtype), compiler_params=_params(("parallel",)),
    )(c, a, r)


def _adamw(w, g, m, v):
    m = ADAM_B1 * m + (1.0 - ADAM_B1) * g
    v = ADAM_B2 * v + (1.0 - ADAM_B2) * jnp.square(g)
    m_hat = m / (1.0 - ADAM_B1 ** ADAM_STEP)
    v_hat = v / (1.0 - ADAM_B2 ** ADAM_STEP)
    return -ADAM_LR * (m_hat / (jnp.sqrt(v_hat) + ADAM_EPS) + ADAM_WD * w), m, v


def adamw_big(recv, sums, chip, w, m, v, tr, name):
    nl, rr, cc = w.shape
    cp = recv[0].shape[2]

    def body(chip_ref, *refs):
        del chip_ref
        rcv, own = refs[:nl], refs[nl:2 * nl]
        w_ref, m_ref, v_ref, g_out, d_out, m_out, v_out = refs[2 * nl:]
        for l in range(nl):
            g = ((own[l][...].astype(F32) + rcv[l][0].astype(F32)) + rcv[l][1].astype(F32)) + rcv[l][2].astype(F32)
            g = g[:, :cc]
            g_out[l] = g
            d_out[l], m_out[l], v_out[l] = _adamw(w_ref[l], g, m_ref[l], v_ref[l])

    blk = pl.BlockSpec((nl, tr, cc), lambda i, chip_ref: (0, i, 0))
    return pl.pallas_call(
        body, name=name,
        grid_spec=pltpu.PrefetchScalarGridSpec(
            num_scalar_prefetch=1, grid=(rr // tr,),
            in_specs=[pl.BlockSpec((3, tr, cp), lambda i, chip_ref: (0, i, 0))] * nl
            + [pl.BlockSpec((None, tr, cp), lambda i, chip_ref: (chip_ref[0], i, 0))] * nl + [blk, blk, blk],
            out_specs=[blk] * 4),
        out_shape=[_sds(w.shape, F32)] * 4, compiler_params=_params(("parallel",)),
    )(chip, *recv, *sums, w, m, v)


def sum_devices(g, name):
    _, rr, cc = g.shape

    def body(g_ref, o_ref):
        acc = g_ref[0]
        for j in range(1, NDEV):
            acc = acc + g_ref[j]
        o_ref[...] = acc

    return pl.pallas_call(
        body, name=name, grid=(1,), in_specs=[_full(g.shape)], out_specs=_full((rr, cc)), out_shape=_sds((rr, cc), F32),
        compiler_params=_params(("arbitrary",)),
    )(g)


def adamw_rows(g, w, m, v, name):
    rr, cc = w.shape

    def body(g_ref, w_ref, m_ref, v_ref, d_out, m_out, v_out):
        d_out[...], m_out[...], v_out[...] = _adamw(w_ref[...], g_ref[...], m_ref[...], v_ref[...])

    blk = _full((rr, cc))
    return pl.pallas_call(
        body, name=name, grid=(1,), in_specs=[blk] * 4, out_specs=[blk] * 3, out_shape=[_sds((rr, cc), F32)] * 3,
        compiler_params=_params(("arbitrary",)),
    )(g, w, m, v)


def _block_diag(w):
    eye = jnp.eye(LRU_BLOCKS, dtype=w.dtype)
    return (w[:, :, None, :] * eye[:, None, :, None]).reshape(D_LRU, D_LRU)


def _diag_blocks(wd):
    w4 = wd.reshape(LRU_BLOCKS, LRU_BD, LRU_BLOCKS, LRU_BD)
    return jnp.stack([w4[g, :, g, :] for g in range(LRU_BLOCKS)])


def _pack(arrs):
    flat = jnp.concatenate([a.reshape(-1) for a in arrs])
    return flat.reshape(-1, 128)


def _unpack(packed, shapes):
    flat = packed.reshape(-1)
    out, o = [], 0
    for s in shapes:
        n = int(np.prod(s))
        out.append(flat[o:o + n].reshape(s))
        o += n
    return out


REP_NAMES = ["norm_mix", "conv_b", "gate_a_w", "gate_a_b", "gate_x_w", "gate_x_b", "lru_lambda", "lru_out_norm",
             "ret_out_norm", "norm_ffn", "norm_final"]


def kernel(x, meta_tokens, norm_mix, w_in, conv_w, conv_b, gate_a_w, gate_a_b, gate_x_w, gate_x_b, lru_lambda, lru_out_norm, ret_out_norm, w_out, norm_ffn, w_gate, w_up, w_down, norm_final, loss_target, m_meta_tokens, m_norm_mix, m_w_in, m_conv_w, m_conv_b, m_gate_a_w, m_gate_a_b, m_gate_x_w, m_gate_x_b, m_lru_lambda, m_lru_out_norm, m_ret_out_norm, m_w_out, m_norm_ffn, m_w_gate, m_w_up, m_w_down, m_norm_final, v_meta_tokens, v_norm_mix, v_w_in, v_conv_w, v_conv_b, v_gate_a_w, v_gate_a_b, v_gate_x_w, v_gate_x_b, v_lru_lambda, v_lru_out_norm, v_ret_out_norm, v_w_out, v_norm_ffn, v_w_gate, v_w_up, v_w_down, v_norm_final):
    xi, yi, ci = _place()
    dev = 4 * xi + 2 * yi + ci
    c_arr = jnp.reshape(ci, (1,)).astype(jnp.int32)

    meta_g, conv_g = all_gather([meta_tokens, conv_w], "ag_small")
    meta_full = jnp.transpose(meta_g, (1, 0, 2)).reshape(N_META, D)
    conv_full = jnp.transpose(conv_g, (1, 2, 0, 3)).reshape(DEPTH, CONV_W, D_LRU)
    level1 = []
    token = meta_g
    for l in range(DEPTH):
        sel = jnp.stack([dev, jnp.int32(l)]).astype(jnp.int32)
        lands = to_wire(sel, w_in, w_gate, w_up, w_out, w_down, "to_wire")
        s1, r1, lands, token = ag_start(lands, token, f"ag_start_{l}")
        level1.append((s1, r1, lands))

    def as_weights(gi, gg, gu, go, gd):
        return dict(w_in=gi, w_gate=gg, w_up=gu, w_out=go.reshape(D, D), w_down=gd.reshape(D_FFP, D))

    tables = _ret_tables()
    row = lambda a: a.reshape(1, -1)

    h = jnp.concatenate([jnp.zeros((PAD, D), F32), meta_full, x[0]], axis=0)
    saved, gathered = [], []
    s1, r1, lands = level1[0]
    s2, r2, first, order = ag_forward(s1[:4], r1[:4], lands[:1], token, "ag_forward_0_w_in")
    w = dict(w_in=ag_finish(s2, r2, first, h, "ag_finish_0_w_in")[0])
    for l in range(DEPTH):
        small = dict(cw=conv_full[l], cb=row(conv_b[l]), wa=_block_diag(gate_a_w[l]).astype(MXU_DTYPE), ba=row(gate_a_b[l]),
                     wx=_block_diag(gate_x_w[l]).astype(MXU_DTYPE), bx=row(gate_x_b[l]), lam=row(lru_lambda[l]),
                     gain=row(lru_out_norm[l]))
        hn1 = rmsnorm_fwd(h, row(norm_mix[l]), "rms_fwd")
        proj = mm_blocked_nn(hn1, w["w_in"], F32, "proj")
        ylru, hst = lru_fwd(proj, name="lru_fwd", **small)
        if l == 0:
            s2, r2, rest, order = ag_forward(s1[4:], r1[4:], lands[1:], ylru, "ag_forward_0_rest")
        ymix, states = ret_fwd(proj, ylru, tables, row(ret_out_norm[l]), "ret_fwd")
        if l == 0:
            w = as_weights(w["w_in"], *ag_finish(s2, r2, rest, ymix, "ag_finish_0_rest"))
        gathered.append(w)
        h_mid = mm_nn_res(ymix, w["w_out"], h, order, "out_proj")
        hn2 = rmsnorm_fwd(h_mid, row(norm_ffn[l]), "rms_fwd")
        gate, up, act = ffn_up(hn2, w["w_gate"], w["w_up"], "ffn_up")
        if l + 1 < DEPTH:
            s1, r1, lands = level1[l + 1]
            s2, r2, lands, order = ag_forward(s1, r1, lands, act, f"ag_forward_{l + 1}")
        h_out = mm_nn_res(act, w["w_down"], h_mid, order, "ffn_down")
        if l + 1 < DEPTH:
            w_next = as_weights(*ag_finish(s2, r2, lands, h_out, f"ag_finish_{l + 1}"))
        saved.append(dict(h=h, hn1=hn1, proj=proj, hst=hst, states=states, ymix=ymix, h_mid=h_mid, hn2=hn2, gate=gate, up=up,
                          act=act, small=small))
        h = h_out
        if l + 1 < DEPTH:
            w = w_next

    loss_p, dh, dh_b, g_norm_final = loss_head(h, row(norm_final), loss_target[0], "loss_head")
    loss = lax.psum(loss_p[0, 0], ("x", "y", "c"))

    rep = [None] * DEPTH
    convw_g = [None] * DEPTH
    inflight = []
    sib = None
    order = loss_p

    def sibling_done(l, tag, names, sib, after):
        parts, got = rs_sibling_wait(*sib, after, f"rs_sibling_wait_{tag}")
        sums = [pair_sum(a, r, c_arr, "pair_sum") for a, r in zip(parts, got)]
        flying, started = rs_chips_start(sums, f"rs_chips_start_{tag}")
        inflight.append((l, tag, names, flying))
        return started

    all5 = ("w_in", "w_gate", "w_up", "w_out", "w_down")
    for l in reversed(range(DEPTH)):
        w, s = gathered[l], saved[l]
        dgate, dup = ffn_down_bwd(dh_b, w["w_down"], s["gate"], s["up"], order, "ffn_down_bwd")
        dwd = mm_tn(s["act"], dh_b, PAIR, order, "dw_down").reshape(NDEV, FF_SHP, D)
        dwg = mm_tn_blocked(s["hn2"], dgate, "dw_blocked")
        dwu = mm_tn_blocked(s["hn2"], dup, "dw_blocked")
        if l == 0:
            ffn_sib, order = rs_sibling_start([dwg, dwu, dwd], "rs_sibling_start_0_ffn")
        dhn2 = mm_blocked_nt([(dgate, w["w_gate"]), (dup, w["w_up"])], order, "ffn_up_bwd")
        if sib is not None:
            order = sibling_done(l + 1, str(l + 1), all5, sib, dhn2)
        dh_mid, dh_mid_b, g_norm_ffn = rmsnorm_bwd(s["h_mid"], row(norm_ffn[l]), dhn2, dh, "rms_bwd")
        dymix = mm_nt(dh_mid_b, w["w_out"], F32, order, "out_proj_bwd")
        if l == 0:
            order = sibling_done(0, "0_ffn", ("w_gate", "w_up", "w_down"), ffn_sib, dymix)
        dwo = mm_tn(s["ymix"], dh_mid_b, BN, order, "dw_out").reshape(NDEV, OUT_SH, D)
        dxg, lvec, dwa, dwx = lru_bwd(s["proj"], s["hst"], dymix, name="lru_bwd", **s["small"])
        dproj, g_ret_norm = ret_bwd(s["proj"], s["states"], dymix, dxg, tables, row(ret_out_norm[l]), "ret_bwd")
        dwi = mm_tn_blocked(s["hn1"], dproj, "dw_blocked")
        dhn1 = mm_blocked_nt([(dproj, w["w_in"])], order, "proj_bwd")
        dh, dh_b, g_norm_mix = rmsnorm_bwd(s["h"], row(norm_mix[l]), dhn1, dh_mid, "rms_bwd")

        rep[l] = [g_norm_mix, lvec[4], _diag_blocks(dwa), lvec[5], _diag_blocks(dwx), lvec[6], lvec[7], lvec[8], g_ret_norm,
                  g_norm_ffn]
        convw_g[l] = lvec[0:CONV_W]
        if l > 0:
            sib, order = rs_sibling_start([dwi, dwg, dwu, dwo, dwd], f"rs_sibling_start_{l}")
        else:
            sib, order = rs_sibling_start([dwi, dwo], "rs_sibling_start_0_mix")
    sibling_done(0, "0_mix", ("w_in", "w_out"), sib, dh)

    grad_x = dh[X0:][None]
    g_meta = dh[PAD:X0]

    rep_shapes = [(D,), (D_LRU,), (LRU_BLOCKS, LRU_BD, LRU_BD), (LRU_BLOCKS, LRU_BD), (LRU_BLOCKS, LRU_BD, LRU_BD),
                  (LRU_BLOCKS, LRU_BD), (D_LRU,), (D_LRU,), (D_RET,), (D,)]
    flat = [a for l in range(DEPTH) for a in rep[l]] + [g_norm_final] + [convw_g[l] for l in range(DEPTH)] + [g_meta]
    (gath,) = all_gather([_pack(flat)], "ag_grads")
    gsum = sum_devices(gath, "sum_devices")
    shapes = rep_shapes * DEPTH + [(D,)] + [(CONV_W, D_LRU)] * DEPTH + [(N_META, D)]
    parts = _unpack(gsum, shapes)
    nrep = len(rep_shapes)
    g_rep = {n: jnp.stack([parts[l * nrep + i] for l in range(DEPTH)]) for i, n in enumerate(REP_NAMES[:-1])}
    g_rep["norm_final"] = parts[DEPTH * nrep]
    g_convw = lax.dynamic_slice_in_dim(jnp.stack(parts[DEPTH * nrep + 1:DEPTH * nrep + 1 + DEPTH]), dev * (D_LRU // NDEV),
                                       D_LRU // NDEV, axis=2)
    g_metatok = lax.dynamic_slice_in_dim(parts[-1], dev * (D // NDEV), D // NDEV, axis=1)

    given = dict(norm_mix=(norm_mix, m_norm_mix, v_norm_mix), conv_b=(conv_b, m_conv_b, v_conv_b),
                 gate_a_w=(gate_a_w, m_gate_a_w, v_gate_a_w), gate_a_b=(gate_a_b, m_gate_a_b, v_gate_a_b),
                 gate_x_w=(gate_x_w, m_gate_x_w, v_gate_x_w), gate_x_b=(gate_x_b, m_gate_x_b, v_gate_x_b),
                 lru_lambda=(lru_lambda, m_lru_lambda, v_lru_lambda), lru_out_norm=(lru_out_norm, m_lru_out_norm, v_lru_out_norm),
                 ret_out_norm=(ret_out_norm, m_ret_out_norm, v_ret_out_norm), norm_ffn=(norm_ffn, m_norm_ffn, v_norm_ffn),
                 norm_final=(norm_final, m_norm_final, v_norm_final),
                 conv_w=(conv_w, m_conv_w, v_conv_w), meta_tokens=(meta_tokens, m_meta_tokens, v_meta_tokens))
    small_names = REP_NAMES + ["conv_w", "meta_tokens"]
    small_g = dict(g_rep, conv_w=g_convw, meta_tokens=g_metatok)
    small_shapes = [given[n][0].shape for n in small_names]
    packs = [_pack([small_g[n] for n in small_names])] + [_pack([given[n][k] for n in small_names]) for k in range(3)]
    upd = adamw_rows(*packs, "adamw_small")
    small_out = [dict(zip(small_names, _unpack(p, small_shapes))) for p in upd]

    arrived = {}
    for l, tag, names, flying in inflight:
        sums, recv = rs_chips_wait(*flying, upd[0], f"rs_chips_wait_{tag}")
        for i, n in enumerate(names):
            arrived[l, n] = (recv[i], sums[i])
    chip = jnp.reshape(2 * xi + yi, (1,)).astype(jnp.int32)

    def finish(wname, w_, m_, v_, tr):
        return adamw_big([arrived[l, wname][0] for l in range(DEPTH)], [arrived[l, wname][1] for l in range(DEPTH)], chip,
                         w_, m_, v_, tr, "adamw_" + wname)

    o_in = finish("w_in", w_in, m_w_in, v_w_in, 256)
    o_gate = finish("w_gate", w_gate, m_w_gate, v_w_gate, 256)
    o_up = finish("w_up", w_up, m_w_up, v_w_up, 256)
    o_out = finish("w_out", w_out, m_w_out, v_w_out, 64)
    o_down = finish("w_down", w_down, m_w_down, v_w_down, 32)

    bigs = dict(w_in=o_in, w_out=o_out, w_gate=o_gate, w_up=o_up, w_down=o_down)
    order = ["meta_tokens", "norm_mix", "w_in", "conv_w", "conv_b", "gate_a_w", "gate_a_b", "gate_x_w", "gate_x_b", "lru_lambda",
             "lru_out_norm", "ret_out_norm", "w_out", "norm_ffn", "w_gate", "w_up", "w_down", "norm_final"]
    grads = [bigs[n][0] if n in bigs else small_g[n] for n in order]
    rest = [[bigs[n][k + 1] if n in bigs else small_out[k][n] for n in order] for k in range(3)]
    return (loss, grad_x, *grads, *rest[0], *rest[1], *rest[2])
```

```python
import functools

import numpy as np
import jax
import jax.numpy as jnp
from jax import lax
from jax.experimental import pallas as pl
from jax.experimental.pallas import tpu as pltpu

F32, BF16 = jnp.float32, jnp.bfloat16
MXU_DTYPE = BF16
WIRE_DTYPE = BF16

D = 1024
SEQ = 2048
DEPTH = 4
N_META = 16
CH = 128
PAD = (-(SEQ + N_META)) % CH
T = SEQ + N_META + PAD
NCH = T // CH
X0 = PAD + N_META
D_LRU = 512
LRU_BLOCKS = 8
LRU_BD = 64
CONV_W = 4
LRU_C = 8.0
D_RET = 512
HEADS = 4
HD = 128
ROPE_BASE = 10000.0
D_IN = 3072
D_FF = 2816
NDEV = 8
IN_SH = D_IN // NDEV
FF_SH = D_FF // NDEV
FF_SHP = 384
D_FFP = NDEV * FF_SHP
OUT_SH = D // NDEV
EPS = 1e-6
TM = 544
TR = 1088
VMEM_LIMIT = 56 * 2**20
MESH = pl.DeviceIdType.MESH

ADAM_LR, ADAM_B1, ADAM_B2, ADAM_EPS, ADAM_WD, ADAM_STEP = 0.001, 0.9, 0.999, 1e-08, 0.01, 10

NN = ((1,), (0,))
NT = ((1,), (1,))
TN = ((0,), (0,))


def _dot(a, b, dims):
    return lax.dot_general(a.astype(MXU_DTYPE), b.astype(MXU_DTYPE), (dims, ((), ())), preferred_element_type=F32)


def _sds(shape, dtype):
    return jax.ShapeDtypeStruct(shape, dtype)


def _params(sem=None):
    return pltpu.CompilerParams(dimension_semantics=sem, vmem_limit_bytes=VMEM_LIMIT)


def _full(shape):
    n = len(shape)
    return pl.BlockSpec(shape, lambda *_: (0,) * n)


def rmsnorm_fwd(h, gain, name):
    def body(h_ref, g_ref, o_ref):
        x = h_ref[...]
        ms = jnp.mean(x * x, axis=-1, keepdims=True)
        o_ref[...] = (x * lax.rsqrt(ms + EPS) * g_ref[...]).astype(o_ref.dtype)

    return pl.pallas_call(
        body, name=name, grid=(T // TM,),
        in_specs=[pl.BlockSpec((TM, D), lambda i: (i, 0)), _full((1, D))],
        out_specs=pl.BlockSpec((TM, D), lambda i: (i, 0)),
        out_shape=_sds((T, D), MXU_DTYPE), compiler_params=_params(("parallel",)),
    )(h, gain)


def rmsnorm_bwd(h, gain, dhn, dres, name):
    def body(h_ref, g_ref, dhn_ref, dres_ref, dh_ref, dhb_ref, dg_ref):
        x = h_ref[...]
        rstd = lax.rsqrt(jnp.mean(x * x, axis=-1, keepdims=True) + EPS)
        xhat = x * rstd
        dy = dhn_ref[...]
        dyg = dy * g_ref[...]
        dh = dres_ref[...] + rstd * (dyg - xhat * jnp.mean(dyg * xhat, axis=-1, keepdims=True))
        dh_ref[...] = dh
        dhb_ref[...] = dh.astype(dhb_ref.dtype)

        @pl.when(pl.program_id(0) == 0)
        def _():
            dg_ref[...] = jnp.zeros_like(dg_ref)
        dg_ref[...] += jnp.sum(dy * xhat, axis=0, keepdims=True)

    row = pl.BlockSpec((TM, D), lambda i: (i, 0))
    return pl.pallas_call(
        body, name=name, grid=(T // TM,),
        in_specs=[row, _full((1, D)), row, row],
        out_specs=[row, row, _full((1, D))],
        out_shape=[_sds((T, D), F32), _sds((T, D), MXU_DTYPE), _sds((1, D), F32)], compiler_params=_params(("arbitrary",)),
    )(h, gain, dhn, dres)


def loss_head(h, gain, target, name):
    def body(h_ref, g_ref, t_ref, loss_ref, dh_ref, dhb_ref, dg_ref):
        i = pl.program_id(0)

        @pl.when(i == 0)
        def _():
            loss_ref[...] = jnp.zeros_like(loss_ref)
            dg_ref[...] = jnp.zeros_like(dg_ref)
            dh_ref[...] = jnp.zeros_like(dh_ref)
            dhb_ref[...] = jnp.zeros_like(dhb_ref)

        @pl.when(i > 0)
        def _():
            x = h_ref[...]
            g = g_ref[...]
            rstd = lax.rsqrt(jnp.mean(x * x, axis=-1, keepdims=True) + EPS)
            xhat = x * rstd
            err = xhat * g - t_ref[...]
            loss_ref[...] += 0.5 * jnp.sum(jnp.mean(err * err, axis=-1, keepdims=True), axis=0, keepdims=True)
            dy = err * (1.0 / D)
            dyg = dy * g
            dh = rstd * (dyg - xhat * jnp.mean(dyg * xhat, axis=-1, keepdims=True))
            dh_ref[...] = dh
            dhb_ref[...] = dh.astype(dhb_ref.dtype)
            dg_ref[...] += jnp.sum(dy * xhat, axis=0, keepdims=True)

    row = pl.BlockSpec((CH, D), lambda i: (i, 0))
    return pl.pallas_call(
        body, name=name, grid=(NCH,),
        in_specs=[row, _full((1, D)), pl.BlockSpec((CH, D), lambda i: (jnp.maximum(i - 1, 0), 0))],
        out_specs=[_full((8, 128)), row, row, _full((1, D))],
        out_shape=[_sds((8, 128), F32), _sds((T, D), F32), _sds((T, D), MXU_DTYPE), _sds((1, D), F32)],
        compiler_params=_params(("arbitrary",)),
    )(h, gain, target)


PAIR = 2 * IN_SH
NPAIR = NDEV // 2
BN = 256


def _pair_cols(w_ref):
    return jnp.concatenate([w_ref[0], w_ref[1]], axis=1)


W_PAIR = lambda k: pl.BlockSpec((2, k, IN_SH), lambda j: (j, 0, 0))
COLS_PAIR = pl.BlockSpec((T, PAIR), lambda j: (0, j))
ANYSPEC = pl.BlockSpec(memory_space=pl.ANY)


def mm_blocked_nn(a, w, out_dtype, name):
    k = a.shape[1]

    def body(a_ref, w_ref, o_ref):
        o_ref[...] = _dot(a_ref[...], _pair_cols(w_ref), NN).astype(o_ref.dtype)

    return pl.pallas_call(
        body, name=name, grid=(NPAIR,),
        in_specs=[_full((T, k)), W_PAIR(k)], out_specs=COLS_PAIR,
        out_shape=_sds((T, NDEV * IN_SH), out_dtype), compiler_params=_params(("parallel",)),
    )(a, w)


def mm_nn_res(a, w, res, after, name):
    k = a.shape[1]

    def body(a_ref, w_ref, r_ref, after_ref, o_ref):
        del after_ref
        o_ref[...] = r_ref[...] + _dot(a_ref[...], w_ref[...], NN)

    col = pl.BlockSpec((T, BN), lambda j: (0, j))
    return pl.pallas_call(
        body, name=name, grid=(D // BN,),
        in_specs=[_full((T, k)), pl.BlockSpec((k, BN), lambda j: (0, j)), col, ANYSPEC], out_specs=col,
        out_shape=_sds((T, D), F32), compiler_params=_params(("parallel",)),
    )(a, w, res, after)


def ffn_up(hn, wg, wu, name):
    def body(a_ref, wg_ref, wu_ref, g_ref, u_ref, act_ref):
        a = a_ref[...]
        g = _dot(a, wg_ref[...], NT)
        u = _dot(a, wu_ref[...], NT)
        g_ref[...] = g.astype(g_ref.dtype)
        u_ref[...] = u.astype(u_ref.dtype)
        act_ref[...] = (jax.nn.silu(g) * u).astype(act_ref.dtype)

    wspec = pl.BlockSpec((PAIR, D), lambda j, i: (j, 0))
    ospec = pl.BlockSpec((TR, PAIR), lambda j, i: (i, j))
    return pl.pallas_call(
        body, name=name, grid=(NPAIR, T // TR),
        in_specs=[pl.BlockSpec((TR, D), lambda j, i: (i, 0)), wspec, wspec], out_specs=[ospec] * 3,
        out_shape=[_sds((T, D_FFP), MXU_DTYPE)] * 3, compiler_params=_params(("parallel", "parallel")),
    )(hn, wg, wu)


def ffn_down_bwd(dh, wd, gate, up, after, name):
    def body(dh_ref, wd_ref, g_ref, u_ref, after_ref, dg_ref, du_ref):
        del after_ref
        dact = _dot(dh_ref[...], wd_ref[...], NT)
        g = g_ref[...].astype(F32)
        u = u_ref[...].astype(F32)
        sg = jax.nn.sigmoid(g)
        dg_ref[...] = (dact * u * (sg * (1.0 + g * (1.0 - sg)))).astype(dg_ref.dtype)
        du_ref[...] = (dact * (g * sg)).astype(du_ref.dtype)

    blk = pl.BlockSpec((TR, PAIR), lambda j, i: (i, j))
    return pl.pallas_call(
        body, name=name, grid=(NPAIR, T // TR),
        in_specs=[pl.BlockSpec((TR, D), lambda j, i: (i, 0)), pl.BlockSpec((PAIR, D), lambda j, i: (j, 0)), blk, blk, ANYSPEC],
        out_specs=[blk, blk],
        out_shape=[_sds((T, D_FFP), MXU_DTYPE)] * 2, compiler_params=_params(("parallel", "parallel")),
    )(dh, wd, gate, up, after)


def mm_nt(a, w, out_dtype, after, name):
    n = a.shape[1]

    def body(a_ref, w_ref, after_ref, o_ref):
        del after_ref
        o_ref[...] = _dot(a_ref[...], w_ref[...], NT).astype(o_ref.dtype)

    return pl.pallas_call(
        body, name=name, grid=(D // BN,),
        in_specs=[_full((T, n)), pl.BlockSpec((BN, n), lambda j: (j, 0)), ANYSPEC],
        out_specs=pl.BlockSpec((T, BN), lambda j: (0, j)),
        out_shape=_sds((T, D), out_dtype), compiler_params=_params(("parallel",)),
    )(a, w, after)


def mm_blocked_nt(pairs, after, name):
    n = len(pairs)

    def body(*refs):
        o_ref = refs[2 * n + 1]

        @pl.when(pl.program_id(1) == 0)
        def _():
            o_ref[...] = jnp.zeros_like(o_ref)
        for p in range(n):
            o_ref[...] += _dot(refs[2 * p][...], _pair_cols(refs[2 * p + 1]), NT)

    specs, args = [], []
    for a, w in pairs:
        specs += [pl.BlockSpec((TR, PAIR), lambda i, j: (i, j)), pl.BlockSpec((2, D, IN_SH), lambda i, j: (j, 0, 0))]
        args += [a, w]
    return pl.pallas_call(
        body, name=name, grid=(T // TR, NPAIR), in_specs=specs + [ANYSPEC],
        out_specs=pl.BlockSpec((TR, D), lambda i, j: (i, 0)),
        out_shape=_sds((T, D), F32), compiler_params=_params(("parallel", "arbitrary")),
    )(*args, after)


def mm_rows_nn(pairs, after, name):
    n = len(pairs)

    def body(*refs):
        o_ref = refs[2 * n + 1]

        @pl.when(pl.program_id(1) == 0)
        def _():
            o_ref[...] = jnp.zeros_like(o_ref)
        for p in range(n):
            o_ref[...] += _dot(refs[2 * p][...], refs[2 * p + 1][...], NN)

    specs, args = [], []
    for a, w in pairs:
        specs += [pl.BlockSpec((TR, PAIR), lambda i, j: (i, j)), pl.BlockSpec((PAIR, D), lambda i, j: (j, 0))]
        args += [a, w]
    return pl.pallas_call(
        body, name=name, grid=(T // TR, NPAIR), in_specs=specs + [ANYSPEC],
        out_specs=pl.BlockSpec((TR, D), lambda i, j: (i, 0)),
        out_shape=_sds((T, D), F32), compiler_params=_params(("parallel", "arbitrary")),
    )(*args, after)


def mm_tn_blocked(a, b, name):
    def body(a_ref, b_ref, o_ref):
        o = _dot(a_ref[...], b_ref[...], TN).astype(o_ref.dtype)
        o_ref[0] = o[:, :IN_SH]
        o_ref[1] = o[:, IN_SH:]

    return pl.pallas_call(
        body, name=name, grid=(NPAIR,),
        in_specs=[_full((T, D)), COLS_PAIR], out_specs=W_PAIR(D),
        out_shape=_sds((NDEV, D, IN_SH), WIRE_DTYPE), compiler_params=_params(("parallel",)),
    )(a, b)


def mm_tn(a, b, bm, after, name):
    m = a.shape[1]

    def body(a_ref, b_ref, after_ref, o_ref):
        del after_ref
        o_ref[...] = _dot(a_ref[...], b_ref[...], TN).astype(o_ref.dtype)

    return pl.pallas_call(
        body, name=name, grid=(m // bm,),
        in_specs=[pl.BlockSpec((T, bm), lambda i: (0, i)), _full((T, D)), ANYSPEC],
        out_specs=pl.BlockSpec((bm, D), lambda i: (i, 0)),
        out_shape=_sds((m, D), WIRE_DTYPE), compiler_params=_params(("parallel",)),
    )(a, b, after)


def _softplus_neg(lam):
    return jnp.maximum(-lam, 0.0) + jnp.log1p(jnp.exp(-jnp.abs(lam)))


def _lru_gates(pa, px, xc, lam):
    r = jax.nn.sigmoid(pa)
    ig = jax.nn.sigmoid(px)
    log_a = -LRU_C * r * _softplus_neg(lam)
    a = jnp.exp(log_a)
    mult = jnp.sqrt(-jnp.tanh(log_a) * (jnp.exp(2.0 * log_a) + 1.0))
    return a, mult * (ig * xc)


def _lru_out(h, g, gain):
    z = h * jax.nn.gelu(g)
    return z * lax.rsqrt(jnp.mean(z * z, axis=-1, keepdims=True) + EPS) * gain


def _conv_taps(x, xprev, row):
    taps = [x]
    for s in range(1, CONV_W):
        taps.append(jnp.where(row < s, pltpu.roll(xprev, s, 0), pltpu.roll(x, s, 0)))
    return taps


def _conv(taps, cw_ref, cb):
    xc = cb + cw_ref[CONV_W - 1:CONV_W, :] * taps[0]
    for s in range(1, CONV_W):
        xc = xc + cw_ref[CONV_W - 1 - s:CONV_W - s, :] * taps[s]
    return xc


def lru_fwd(proj, cw, cb, wa, ba, wx, bx, lam, gain, name):
    def body(x_ref, g_ref, cw_ref, cb_ref, wa_ref, ba_ref, wx_ref, bx_ref, lam_ref, gain_ref,
             y_ref, h_ref, xprev_scr, a_scr, b_scr, carry_scr):
        i = pl.program_id(0)

        @pl.when(i == 0)
        def _():
            xprev_scr[...] = jnp.zeros_like(xprev_scr)
            carry_scr[...] = jnp.zeros_like(carry_scr)

        x = x_ref[...]
        row = lax.broadcasted_iota(jnp.int32, (CH, D_LRU), 0)
        xc = _conv(_conv_taps(x, xprev_scr[...], row), cw_ref, cb_ref[...])
        pa = _dot(xc, wa_ref[...], NN) + ba_ref[...]
        px = _dot(xc, wx_ref[...], NN) + bx_ref[...]
        a, b = _lru_gates(pa, px, xc, lam_ref[...])
        a_scr[...] = a
        b_scr[...] = jnp.where(i * CH + row >= PAD, b, 0.0)
        h = carry_scr[...]
        for t in range(CH):
            h = a_scr[t:t + 1, :] * h + b_scr[t:t + 1, :]
            h_ref[t:t + 1, :] = h
        carry_scr[...] = h
        xprev_scr[...] = x
        y_ref[...] = _lru_out(h_ref[...], g_ref[...], gain_ref[...]).astype(y_ref.dtype)

    vec = _full((1, D_LRU))
    mat = _full((D_LRU, D_LRU))
    return pl.pallas_call(
        body, name=name, grid=(NCH,),
        in_specs=[pl.BlockSpec((CH, D_LRU), lambda i: (i, 0)), pl.BlockSpec((CH, D_LRU), lambda i: (i, 1)),
                  _full((CONV_W, D_LRU)), vec, mat, vec, mat, vec, vec, vec],
        out_specs=[pl.BlockSpec((CH, D_LRU), lambda i: (i, 0)), pl.BlockSpec((CH, D_LRU), lambda i: (i, 0))],
        out_shape=[_sds((T, D_LRU), MXU_DTYPE), _sds((T, D_LRU), F32)],
        scratch_shapes=[pltpu.VMEM((CH, D_LRU), F32), pltpu.VMEM((CH, D_LRU), F32), pltpu.VMEM((CH, D_LRU), F32),
                        pltpu.VMEM((1, D_LRU), F32)],
        compiler_params=_params(("arbitrary",)),
    )(proj, proj, cw, cb, wa, ba, wx, bx, lam, gain)


LRU_VEC_ROWS = 16


def lru_bwd(proj, hst, dymix, cw, cb, wa, ba, wx, bx, lam, gain, name):
    last = NCH - 1

    def body(x_ref, xp_ref, g_ref, h_ref, hp_ref, dy_ref, cw_ref, cb_ref, wa_ref, ba_ref, wx_ref, bx_ref, lam_ref,
             gain_ref, dxg_ref, vec_ref, dwa_ref, dwx_ref, a_scr, dh_scr, g_scr, carry_scr, dxcn_scr):
        i = pl.program_id(0)
        ib = last - i

        @pl.when(i == 0)
        def _():
            carry_scr[...] = jnp.zeros_like(carry_scr)
            dxcn_scr[...] = jnp.zeros_like(dxcn_scr)
            vec_ref[...] = jnp.zeros_like(vec_ref)
            dwa_ref[...] = jnp.zeros_like(dwa_ref)
            dwx_ref[...] = jnp.zeros_like(dwx_ref)

        x = x_ref[...]
        row = lax.broadcasted_iota(jnp.int32, (CH, D_LRU), 0)
        valid = ib * CH + row >= PAD
        taps = _conv_taps(x, xp_ref[...], row)
        xc = _conv(taps, cw_ref, cb_ref[...])
        pa = _dot(xc, wa_ref[...], NN) + ba_ref[...]
        px = _dot(xc, wx_ref[...], NN) + bx_ref[...]
        (a, _), vjp_gates = jax.vjp(_lru_gates, pa, px, xc, lam_ref[...])
        h = h_ref[...]
        _, vjp_out = jax.vjp(_lru_out, h, g_ref[...], gain_ref[...])
        dh, dg, dgain = vjp_out(dy_ref[...].astype(F32))
        a_scr[...] = a
        dh_scr[...] = dh
        c = carry_scr[...]
        for t in range(CH - 1, -1, -1):
            gt = dh_scr[t:t + 1, :] + c
            g_scr[t:t + 1, :] = gt
            c = a_scr[t:t + 1, :] * gt
        carry_scr[...] = c
        gg = g_scr[...]
        hprev = jnp.where(row < 1, pltpu.roll(hp_ref[...], 1, 0), pltpu.roll(h, 1, 0))
        da = jnp.where(valid, gg * hprev, 0.0)
        db = jnp.where(valid, gg, 0.0)
        dpa, dpx, dxc, dlam = vjp_gates((da, db))
        dxc = dxc + _dot(dpa, wa_ref[...], NT) + _dot(dpx, wx_ref[...], NT)
        dwa_ref[...] += _dot(xc, dpa, TN)
        dwx_ref[...] += _dot(xc, dpx, TN)
        for s in range(CONV_W):
            vec_ref[CONV_W - 1 - s:CONV_W - s, :] += jnp.sum(dxc * taps[s], axis=0, keepdims=True)
        vec_ref[4:5, :] += jnp.sum(dxc, axis=0, keepdims=True)
        vec_ref[5:6, :] += jnp.sum(dpa, axis=0, keepdims=True)
        vec_ref[6:7, :] += jnp.sum(dpx, axis=0, keepdims=True)
        vec_ref[7:8, :] += dlam
        vec_ref[8:9, :] += dgain
        dxn = dxcn_scr[...]
        dx = cw_ref[CONV_W - 1:CONV_W, :] * dxc
        for s in range(1, CONV_W):
            ahead = jnp.where(row >= CH - s, pltpu.roll(dxn, CH - s, 0), pltpu.roll(dxc, CH - s, 0))
            dx = dx + cw_ref[CONV_W - 1 - s:CONV_W - s, :] * ahead
        dxcn_scr[...] = dxc
        dxg_ref[:, :D_LRU] = jnp.where(valid, dx, 0.0).astype(dxg_ref.dtype)
        dxg_ref[:, D_LRU:] = dg.astype(dxg_ref.dtype)

    vec = _full((1, D_LRU))
    mat = _full((D_LRU, D_LRU))

    def blk(col, shift=0):
        return pl.BlockSpec((CH, D_LRU), lambda i: (jnp.maximum(last - i - shift, 0), col))

    return pl.pallas_call(
        body, name=name, grid=(NCH,),
        in_specs=[blk(0), blk(0, 1), blk(1), blk(0), blk(0, 1), blk(0),
                  _full((CONV_W, D_LRU)), vec, mat, vec, mat, vec, vec, vec],
        out_specs=[pl.BlockSpec((CH, 2 * D_LRU), lambda i: (last - i, 0)), _full((LRU_VEC_ROWS, D_LRU)), mat, mat],
        out_shape=[_sds((T, 2 * D_LRU), MXU_DTYPE), _sds((LRU_VEC_ROWS, D_LRU), F32),
                   _sds((D_LRU, D_LRU), F32), _sds((D_LRU, D_LRU), F32)],
        scratch_shapes=[pltpu.VMEM((CH, D_LRU), F32), pltpu.VMEM((CH, D_LRU), F32), pltpu.VMEM((CH, D_LRU), F32),
                        pltpu.VMEM((1, D_LRU), F32), pltpu.VMEM((CH, D_LRU), F32)],
        compiler_params=_params(("arbitrary",)),
    )(proj, proj, proj, hst, hst, dymix, cw, cb, wa, ba, wx, bx, lam, gain)


def _ret_tables():
    half = HD // 2
    pos = jnp.arange(T, dtype=F32) - float(PAD)
    inv = ROPE_BASE ** (-jnp.arange(half, dtype=F32) / half)
    ang = pos[:, None] * inv[None, :]
    cos = jnp.concatenate([jnp.cos(ang), jnp.cos(ang)], axis=-1)
    sin = jnp.concatenate([-jnp.sin(ang), jnp.sin(ang)], axis=-1)
    log_g = jnp.log(1.0 - 2.0 ** (-5.0 - jnp.arange(HEADS, dtype=F32)))
    idx = jnp.arange(CH, dtype=F32)
    diff = idx[:, None] - idx[None, :]
    dmask = jnp.where(diff[None] >= 0, jnp.exp(jnp.maximum(diff, 0.0)[None] * log_g[:, None, None]), 0.0)
    xi = jnp.exp((idx + 1.0)[None, :] * log_g[:, None])
    zeta = jnp.exp((CH - 1.0 - idx)[None, :] * log_g[:, None])
    xi = jnp.broadcast_to(xi[:, :, None], (HEADS, CH, HD))
    zeta = jnp.broadcast_to(zeta[:, :, None], (HEADS, CH, HD))
    return cos, sin, dmask, xi, zeta


def _chunk_decay():
    log_g = np.log(np.float32(1.0) - np.float32(2.0) ** (np.float32(-5.0) - np.arange(HEADS, dtype=np.float32)))
    return [float(v) for v in np.exp(np.float32(CH) * log_g.astype(np.float32))]


def _rope(x, cos, sin):
    return x * cos + pltpu.roll(x, HD // 2, 1) * sin


def ret_fwd(proj, ylru, tables, gain, name):
    cos, sin, dmask, xi, zeta = tables
    gch = _chunk_decay()
    scale = HD ** -0.5

    def body(q_ref, k_ref, v_ref, g_ref, cos_ref, sin_ref, dm_ref, xi_ref, zt_ref, gain_ref, ylru_ref,
             y_ref, st_ref, s_scr):
        @pl.when(pl.program_id(0) == 0)
        def _():
            s_scr[...] = jnp.zeros_like(s_scr)

        y_ref[:, :D_LRU] = ylru_ref[...]
        cs, sn = cos_ref[...], sin_ref[...]
        hs = range(HEADS)
        sl = [slice(HD * h, HD * (h + 1)) for h in hs]
        qr = [_rope(q_ref[:, sl[h]], cs, sn).astype(MXU_DTYPE) for h in hs]
        kf = [_rope(k_ref[:, sl[h]], cs, sn) * scale for h in hs]
        kr = [kf[h].astype(MXU_DTYPE) for h in hs]
        v = [v_ref[:, sl[h]].astype(MXU_DTYPE) for h in hs]
        s = [s_scr[h] for h in hs]
        for h in hs:
            st_ref[h] = s[h]
        sc = [_dot(qr[h], kr[h], NT) * dm_ref[h] for h in hs]
        cross = [_dot(qr[h], s[h], NN) * xi_ref[h] for h in hs]
        for h in hs:
            s_scr[h] = s[h] * gch[h] + _dot(kf[h] * zt_ref[h], v[h], TN)
        y = [_dot(sc[h], v[h], NN) + cross[h] for h in hs]
        yc = [y[h] - jnp.mean(y[h], axis=-1, keepdims=True) for h in hs]
        yn = [yc[h] * lax.rsqrt(jnp.mean(yc[h] * yc[h], axis=-1, keepdims=True) + EPS) for h in hs]
        for h in hs:
            so = slice(D_LRU + HD * h, D_LRU + HD * (h + 1))
            y_ref[:, so] = (jax.nn.silu(g_ref[:, sl[h]]) * (yn[h] * gain_ref[:, sl[h]])).astype(y_ref.dtype)

    def col(c):
        return pl.BlockSpec((CH, D_RET), lambda n: (n, c))

    tab = pl.BlockSpec((CH, HD), lambda n: (n, 0))
    cst = _full((HEADS, CH, HD))
    return pl.pallas_call(
        body, name=name, grid=(NCH,),
        in_specs=[col(2), col(3), col(4), col(5), tab, tab, cst, cst, cst, _full((1, D_RET)), col(0)],
        out_specs=[pl.BlockSpec((CH, D), lambda n: (n, 0)), pl.BlockSpec((None, HEADS, HD, HD), lambda n: (n, 0, 0, 0))],
        out_shape=[_sds((T, D), MXU_DTYPE), _sds((NCH, HEADS, HD, HD), F32)],
        scratch_shapes=[pltpu.VMEM((HEADS, HD, HD), F32)],
        compiler_params=_params(("arbitrary",)),
    )(proj, proj, proj, proj, cos, sin, dmask, xi, zeta, gain, ylru)


def ret_bwd(proj, states, dymix, dxg, tables, gain, name):
    cos, sin, dmask, xi, zeta = tables
    gch = _chunk_decay()
    scale = HD ** -0.5
    last = NCH - 1

    def body(q_ref, k_ref, v_ref, g_ref, st_ref, do_ref, cos_ref, sin_ref, dm_ref, xi_ref, zt_ref, gain_ref, dxg_ref,
             dp_ref, dgain_ref, ds_scr):
        @pl.when(pl.program_id(0) == 0)
        def _():
            ds_scr[...] = jnp.zeros_like(ds_scr)
            dgain_ref[...] = jnp.zeros_like(dgain_ref)

        dp_ref[:, :2 * D_LRU] = dxg_ref[...]
        cs, sn = cos_ref[...], sin_ref[...]
        hs = range(HEADS)
        sl = [slice(HD * h, HD * (h + 1)) for h in hs]

        def out(j, h):
            return slice(2 * D_LRU + j * D_RET + HD * h, 2 * D_LRU + j * D_RET + HD * (h + 1))

        b16 = lambda xs: [x.astype(MXU_DTYPE) for x in xs]
        qr = b16([_rope(q_ref[:, sl[h]], cs, sn) for h in hs])
        kf = [_rope(k_ref[:, sl[h]], cs, sn) * scale for h in hs]
        kr = b16(kf)
        kz = b16([kf[h] * zt_ref[h] for h in hs])
        v = b16([v_ref[:, sl[h]] for h in hs])
        s = b16([st_ref[h] for h in hs])
        ds = [ds_scr[h] for h in hs]
        dsb = b16(ds)
        sc = [_dot(qr[h], kr[h], NT) * dm_ref[h] for h in hs]
        scb = b16(sc)
        y = [_dot(scb[h], v[h], NN) + _dot(qr[h], s[h], NN) * xi_ref[h] for h in hs]
        yc = [y[h] - jnp.mean(y[h], axis=-1, keepdims=True) for h in hs]
        rstd = [lax.rsqrt(jnp.mean(yc[h] * yc[h], axis=-1, keepdims=True) + EPS) for h in hs]
        yn = [yc[h] * rstd[h] for h in hs]
        dy = []
        for h in hs:
            g = g_ref[:, sl[h]]
            gain = gain_ref[:, sl[h]]
            sg = jax.nn.sigmoid(g)
            silu = g * sg
            dout = do_ref[:, sl[h]].astype(F32)
            dgain_ref[:, sl[h]] += jnp.sum(dout * silu * yn[h], axis=0, keepdims=True)
            dp_ref[:, out(3, h)] = (dout * yn[h] * gain * (sg * (1.0 + g * (1.0 - sg)))).astype(dp_ref.dtype)
            dyn = dout * silu * gain
            dy.append(rstd[h] * (dyn - jnp.mean(dyn, axis=-1, keepdims=True)
                                 - yn[h] * jnp.mean(dyn * yn[h], axis=-1, keepdims=True)))
        dyb = b16(dy)
        dqs = b16([dy[h] * xi_ref[h] for h in hs])
        dp = b16([_dot(dyb[h], v[h], NT) * dm_ref[h] for h in hs])
        dv = [_dot(scb[h], dyb[h], TN) + _dot(kz[h], dsb[h], NN) for h in hs]
        dqr = [_dot(dp[h], kr[h], NN) + _dot(dqs[h], s[h], NT) for h in hs]
        dkr = [_dot(dp[h], qr[h], TN) + _dot(v[h], dsb[h], NT) * zt_ref[h] for h in hs]
        for h in hs:
            ds_scr[h] = gch[h] * ds[h] + _dot(qr[h], dqs[h], TN)
        for h in hs:
            dp_ref[:, out(0, h)] = (dqr[h] * cs + pltpu.roll(dqr[h] * sn, HD // 2, 1)).astype(dp_ref.dtype)
            dp_ref[:, out(1, h)] = ((dkr[h] * cs + pltpu.roll(dkr[h] * sn, HD // 2, 1)) * scale).astype(dp_ref.dtype)
            dp_ref[:, out(2, h)] = dv[h].astype(dp_ref.dtype)

    def col(c):
        return pl.BlockSpec((CH, D_RET), lambda n: (last - n, c))

    tab = pl.BlockSpec((CH, HD), lambda n: (last - n, 0))
    cst = _full((HEADS, CH, HD))
    return pl.pallas_call(
        body, name=name, grid=(NCH,),
        in_specs=[col(2), col(3), col(4), col(5), pl.BlockSpec((None, HEADS, HD, HD), lambda n: (last - n, 0, 0, 0)), col(1),
                  tab, tab, cst, cst, cst, _full((1, D_RET)), pl.BlockSpec((CH, 2 * D_LRU), lambda n: (last - n, 0))],
        out_specs=[pl.BlockSpec((CH, D_IN), lambda n: (last - n, 0)), _full((1, D_RET))],
        out_shape=[_sds((T, D_IN), MXU_DTYPE), _sds((1, D_RET), F32)],
        scratch_shapes=[pltpu.VMEM((HEADS, HD, HD), F32)],
        compiler_params=_params(("arbitrary",)),
    )(proj, proj, proj, proj, states, dymix, cos, sin, dmask, xi, zeta, gain, dxg)


HBM = pl.BlockSpec(memory_space=pltpu.HBM)


def _place():
    return lax.axis_index("x"), lax.axis_index("y"), lax.axis_index("c")


def all_gather(arrs, name):
    n = len(arrs)

    def body(*refs):
        ins, outs = refs[:n], refs[n:2 * n]
        send_sems, recv_sems, local_sems = refs[2 * n:]
        x, y, c = _place()
        me, sibling = (x, y, c), (x, y, 1 - c)
        chips = [(1 - x, y), (x, 1 - y), (1 - x, 1 - y)]

        def copy(a, k, block, to, src=None):
            px, py, pc = block
            dst = outs[a].at[4 * px + 2 * py + pc]
            return pltpu.make_async_remote_copy(
                src_ref=dst if src is None else src, dst_ref=dst, send_sem=send_sems.at[a, k], recv_sem=recv_sems.at[a, k],
                device_id=to, device_id_type=MESH)

        mine = [pltpu.make_async_copy(ins[a], outs[a].at[4 * x + 2 * y + c], local_sems.at[a]) for a in range(n)]
        for cp in mine:
            cp.start()
        first = []
        for a in range(n):
            first.append(copy(a, 0, me, sibling, src=ins[a]))
            first += [copy(a, 1 + j, me, (*chip, c), src=ins[a]) for j, chip in enumerate(chips)]
        for cp in first:
            cp.start()
        passed = []
        for j, chip in enumerate(chips):
            for a in range(n):
                copy(a, 1 + j, (*chip, c), me).wait_recv()
                passed.append(copy(a, 4 + j, (*chip, c), sibling))
                passed[-1].start()
        for a in range(n):
            copy(a, 0, sibling, me).wait_recv()
            for j, chip in enumerate(chips):
                copy(a, 4 + j, (*chip, 1 - c), me).wait_recv()
        for cp in first + passed:
            cp.wait_send()
        for cp in mine:
            cp.wait()

    return pl.pallas_call(
        body, name=name,
        in_specs=[HBM] * n, out_specs=[HBM] * n,
        out_shape=[_sds((NDEV,) + a.shape, a.dtype) for a in arrs],
        scratch_shapes=[pltpu.SemaphoreType.DMA((n, 7)), pltpu.SemaphoreType.DMA((n, 7)), pltpu.SemaphoreType.DMA((n,))],
    )(*arrs)


SEM = pl.BlockSpec(memory_space=pltpu.SEMAPHORE)
ANY = pl.BlockSpec(memory_space=pl.ANY)
EFFECT = pltpu.SideEffectType.DATAFLOW_SIDE_EFFECTING


def _hbm(a):
    return pltpu.with_memory_space_constraint(a, pltpu.HBM)


def _hbm_like(arrs):
    return [pltpu.HBM(a.shape, a.dtype) for a in arrs]


def _dma_sems(count):
    return [pltpu.SemaphoreType.DMA(())] * count


def _ag_copy(lands, send_sems, recv_sems, per):
    def copy(a, k, block, to, src=None):
        px, py, pc = block
        dst = lands[a].at[4 * px + 2 * py + pc]
        return pltpu.make_async_remote_copy(
            src_ref=dst if src is None else src, dst_ref=dst, send_sem=send_sems[a * per + k], recv_sem=recv_sems[a * per + k],
            device_id=to, device_id_type=MESH)
    return copy


def to_wire(sel, w_in, w_gate, w_up, w_out, w_down, name):
    ffpad = FF_SHP - FF_SH

    def body(sel_ref, i_ref, g_ref, u_ref, o_ref, d_ref, oi, og, ou, oo, od):
        del sel_ref
        oi[...] = i_ref[...].astype(oi.dtype)
        oo[...] = o_ref[...].astype(oo.dtype)
        for src, dst in ((g_ref, og), (u_ref, ou), (d_ref, od)):
            dst[:FF_SH, :] = src[...].astype(dst.dtype)
            dst[FF_SH:, :] = jnp.zeros((ffpad, D), dst.dtype)

    shapes_in = [(D, IN_SH), (FF_SH, D), (FF_SH, D), (OUT_SH, D), (FF_SH, D)]
    shapes_out = [(D, IN_SH), (FF_SHP, D), (FF_SHP, D), (OUT_SH, D), (FF_SHP, D)]
    return pl.pallas_call(
        body, name=name,
        grid_spec=pltpu.PrefetchScalarGridSpec(
            num_scalar_prefetch=1, grid=(1,),
            in_specs=[pl.BlockSpec((None,) + s, lambda i, sel_ref: (sel_ref[1], 0, 0)) for s in shapes_in],
            out_specs=[pl.BlockSpec((None,) + s, lambda i, sel_ref: (sel_ref[0], 0, 0)) for s in shapes_out]),
        out_shape=[_sds((NDEV,) + s, WIRE_DTYPE) for s in shapes_out], compiler_params=_params(("arbitrary",)),
    )(sel, w_in, w_gate, w_up, w_out, w_down)


def ag_start(lands, after, name):
    n = len(lands)
    ns = 4 * n

    def body(*refs):
        lnd = refs[:n]
        send_sems, recv_sems = refs[n + 1:n + 1 + ns], refs[n + 1 + ns:n + 1 + 2 * ns]
        token = refs[-1]
        x, y, c = _place()
        me, sibling = (x, y, c), (x, y, 1 - c)
        chips = [(1 - x, y), (x, 1 - y), (1 - x, 1 - y)]
        copy = _ag_copy(lnd, send_sems, recv_sems, 4)
        for a in range(n):
            copy(a, 0, me, sibling).start()
            for j, chip in enumerate(chips):
                copy(a, 1 + j, me, (*chip, c)).start()
        token[...] = jnp.zeros_like(token)

    outs = pl.pallas_call(
        body, name=name,
        in_specs=[HBM] * n + [ANY],
        out_specs=[SEM] * (2 * ns) + [HBM] * n + [pl.BlockSpec(memory_space=pltpu.VMEM)],
        out_shape=_dma_sems(2 * ns) + _hbm_like(lands) + [_sds((8, 128), F32)],
        input_output_aliases={i: 2 * ns + i for i in range(n)},
        compiler_params=pltpu.CompilerParams(has_side_effects=EFFECT),
    )(*[_hbm(a) for a in lands], after)
    return outs[:ns], outs[ns:2 * ns], outs[2 * ns:2 * ns + n], outs[-1]


def ag_forward(send_sems, recv_sems, lands, after, name):
    n = len(lands)
    n1, n2 = 4 * n, 3 * n

    def body(*refs):
        lnd = refs[:n]
        o = n
        s1, r1 = refs[o:o + n1], refs[o + n1:o + 2 * n1]
        o += 2 * n1 + 1
        s2, r2 = refs[o:o + n2], refs[o + n2:o + 2 * n2]
        token = refs[-1]
        token[...] = jnp.zeros_like(token)
        x, y, c = _place()
        me, sibling = (x, y, c), (x, y, 1 - c)
        chips = [(1 - x, y), (x, 1 - y), (1 - x, 1 - y)]
        copy1 = _ag_copy(lnd, s1, r1, 4)
        copy2 = _ag_copy(lnd, s2, r2, 3)
        for j, chip in enumerate(chips):
            for a in range(n):
                copy1(a, 1 + j, (*chip, c), me).wait_recv()
                copy2(a, j, (*chip, c), sibling).start()
        for a in range(n):
            copy1(a, 0, sibling, me).wait_recv()
            copy1(a, 0, me, sibling).wait_send()
            for j, chip in enumerate(chips):
                copy1(a, 1 + j, me, (*chip, c)).wait_send()

    outs = pl.pallas_call(
        body, name=name,
        in_specs=[HBM] * n + [SEM] * (2 * n1) + [ANY],
        out_specs=[SEM] * (2 * n2) + [HBM] * n + [pl.BlockSpec(memory_space=pltpu.VMEM)],
        out_shape=_dma_sems(2 * n2) + _hbm_like(lands) + [_sds((8, 128), F32)],
        input_output_aliases={i: 2 * n2 + i for i in range(n)},
        compiler_params=pltpu.CompilerParams(has_side_effects=EFFECT),
    )(*lands, *send_sems, *recv_sems, after)
    return outs[:n2], outs[n2:2 * n2], outs[2 * n2:2 * n2 + n], outs[-1]


def ag_finish(send_sems, recv_sems, lands, after, name):
    n = len(lands)
    n2 = 3 * n

    def body(*refs):
        lnd = refs[:n]
        s2, r2 = refs[n:n + n2], refs[n + n2:n + 2 * n2]
        x, y, c = _place()
        me, sibling = (x, y, c), (x, y, 1 - c)
        chips = [(1 - x, y), (x, 1 - y), (1 - x, 1 - y)]
        copy2 = _ag_copy(lnd, s2, r2, 3)
        for a in range(n):
            for j, chip in enumerate(chips):
                copy2(a, j, (*chip, c), sibling).wait_send()
                copy2(a, j, (*chip, 1 - c), me).wait_recv()

    outs = pl.pallas_call(
        body, name=name,
        in_specs=[HBM] * n + [SEM] * (2 * n2) + [ANY],
        out_specs=[HBM] * n, out_shape=_hbm_like(lands),
        input_output_aliases={i: i for i in range(n)},
        compiler_params=pltpu.CompilerParams(has_side_effects=EFFECT),
    )(*lands, *send_sems, *recv_sems, after)
    return list(outs)


def rs_sibling_start(arrs, name):
    n = len(arrs)
    ns = 4 * n
    lands = [lax.empty((4,) + a.shape[1:], a.dtype) for a in arrs]

    def body(*refs):
        ins, lnd = refs[:n], refs[n:2 * n]
        send_sems, recv_sems = refs[2 * n:2 * n + ns], refs[2 * n + ns:2 * n + 2 * ns]
        x, y, c = _place()
        sibling = (x, y, 1 - c)
        for a in range(n):
            for p in range(4):
                pltpu.make_async_remote_copy(
                    src_ref=ins[a].at[2 * p + 1 - c], dst_ref=lnd[a].at[p], send_sem=send_sems[4 * a + p],
                    recv_sem=recv_sems[4 * a + p], device_id=sibling, device_id_type=MESH).start()
        refs[-1][...] = jnp.zeros_like(refs[-1])

    outs = pl.pallas_call(
        body, name=name,
        in_specs=[HBM] * (2 * n), out_specs=[SEM] * (2 * ns) + [HBM] * (2 * n) + [pl.BlockSpec(memory_space=pltpu.VMEM)],
        out_shape=_dma_sems(2 * ns) + _hbm_like(arrs) + _hbm_like(lands) + [_sds((8, 128), F32)],
        input_output_aliases={i: 2 * ns + i for i in range(2 * n)},
        compiler_params=pltpu.CompilerParams(has_side_effects=EFFECT),
    )(*[_hbm(a) for a in arrs], *[_hbm(a) for a in lands])
    return (outs[:ns], outs[ns:2 * ns], outs[2 * ns:2 * ns + n], outs[2 * ns + n:2 * ns + 2 * n]), outs[-1]


def rs_sibling_wait(send_sems, recv_sems, arrs, lands, after, name):
    n = len(arrs)
    ns = 4 * n

    def body(*refs):
        ins, lnd = refs[:n], refs[n:2 * n]
        s, r = refs[2 * n:2 * n + ns], refs[2 * n + ns:2 * n + 2 * ns]
        x, y, c = _place()
        sibling = (x, y, 1 - c)
        for a in range(n):
            for p in range(4):
                cp = pltpu.make_async_remote_copy(
                    src_ref=ins[a].at[2 * p + 1 - c], dst_ref=lnd[a].at[p], send_sem=s[4 * a + p], recv_sem=r[4 * a + p],
                    device_id=sibling, device_id_type=MESH)
                cp.wait_send()
                cp.wait_recv()

    outs = pl.pallas_call(
        body, name=name,
        in_specs=[HBM] * (2 * n) + [SEM] * (2 * ns) + [ANY], out_specs=[HBM] * (2 * n),
        out_shape=_hbm_like(arrs) + _hbm_like(lands),
        input_output_aliases={i: i for i in range(2 * n)},
        compiler_params=pltpu.CompilerParams(has_side_effects=EFFECT),
    )(*arrs, *lands, *send_sems, *recv_sems, after)
    return outs[:n], outs[n:]


def rs_chips_start(parts, name):
    n = len(parts)
    ns = 3 * n
    lands = [lax.empty((3,) + a.shape[1:], a.dtype) for a in parts]

    def body(*refs):
        ins, lnd = refs[:n], refs[n:2 * n]
        send_sems, recv_sems = refs[2 * n:2 * n + ns], refs[2 * n + ns:2 * n + 2 * ns]
        x, y, c = _place()
        chips = [(1 - x, y), (x, 1 - y), (1 - x, 1 - y)]
        for a in range(n):
            for k, (tx, ty) in enumerate(chips):
                pltpu.make_async_remote_copy(
                    src_ref=ins[a].at[2 * tx + ty], dst_ref=lnd[a].at[k], send_sem=send_sems[3 * a + k],
                    recv_sem=recv_sems[3 * a + k], device_id=(tx, ty, c), device_id_type=MESH).start()
        refs[-1][...] = jnp.zeros_like(refs[-1])

    outs = pl.pallas_call(
        body, name=name,
        in_specs=[HBM] * (2 * n), out_specs=[SEM] * (2 * ns) + [HBM] * (2 * n) + [pl.BlockSpec(memory_space=pltpu.VMEM)],
        out_shape=_dma_sems(2 * ns) + _hbm_like(parts) + _hbm_like(lands) + [_sds((8, 128), F32)],
        input_output_aliases={i: 2 * ns + i for i in range(2 * n)},
        compiler_params=pltpu.CompilerParams(has_side_effects=EFFECT),
    )(*[_hbm(a) for a in parts], *[_hbm(a) for a in lands])
    return (outs[:ns], outs[ns:2 * ns], outs[2 * ns:2 * ns + n], outs[2 * ns + n:2 * ns + 2 * n]), outs[-1]


def rs_chips_wait(send_sems, recv_sems, parts, lands, after, name):
    n = len(parts)
    ns = 3 * n

    def body(*refs):
        ins, lnd = refs[:n], refs[n:2 * n]
        s, r = refs[2 * n:2 * n + ns], refs[2 * n + ns:2 * n + 2 * ns]
        x, y, c = _place()
        chips = [(1 - x, y), (x, 1 - y), (1 - x, 1 - y)]
        for a in range(n):
            for k, (tx, ty) in enumerate(chips):
                cp = pltpu.make_async_remote_copy(
                    src_ref=ins[a].at[2 * tx + ty], dst_ref=lnd[a].at[k], send_sem=s[3 * a + k], recv_sem=r[3 * a + k],
                    device_id=(tx, ty, c), device_id_type=MESH)
                cp.wait_send()
                cp.wait_recv()

    outs = pl.pallas_call(
        body, name=name,
        in_specs=[HBM] * (2 * n) + [SEM] * (2 * ns) + [ANY], out_specs=[HBM] * (2 * n),
        out_shape=_hbm_like(parts) + _hbm_like(lands),
        input_output_aliases={i: i for i in range(2 * n)},
        compiler_params=pltpu.CompilerParams(has_side_effects=EFFECT),
    )(*parts, *lands, *send_sems, *recv_sems, after)
    return outs[:n], outs[n:]


def pair_sum(arrs, recv, c, name):
    n = len(arrs)

    def body(c_ref, *refs):
        del c_ref
        for a in range(n):
            refs[2 * n + a][...] = (refs[a][...].astype(F32) + refs[n + a][...].astype(F32)).astype(refs[2 * n + a].dtype)

    mine = [pl.BlockSpec((None,) + a.shape[1:], lambda p, c_ref: (2 * p + c_ref[0], 0, 0)) for a in arrs]
    other = [pl.BlockSpec((None,) + a.shape[1:], lambda p, c_ref: (p, 0, 0)) for a in arrs]
    return pl.pallas_call(
        body, name=name,
        grid_spec=pltpu.PrefetchScalarGridSpec(num_scalar_prefetch=1, grid=(4,), in_specs=mine + other, out_specs=other),
        out_shape=[_sds((4,) + a.shape[1:], a.dtype) for a in arrs], compiler_params=_params(("parallel",)),
    )(c, *arrs, *recv)


def _adamw(w, g, m, v):
    m = ADAM_B1 * m + (1.0 - ADAM_B1) * g
    v = ADAM_B2 * v + (1.0 - ADAM_B2) * jnp.square(g)
    m_hat = m / (1.0 - ADAM_B1 ** ADAM_STEP)
    v_hat = v / (1.0 - ADAM_B2 ** ADAM_STEP)
    return -ADAM_LR * (m_hat / (jnp.sqrt(v_hat) + ADAM_EPS) + ADAM_WD * w), m, v


def adamw_big(recv, sums, chip, w, m, v, tr, name):
    nl, rr, cc = w.shape
    cp = recv[0].shape[2]

    def body(chip_ref, *refs):
        del chip_ref
        rcv, own = refs[:nl], refs[nl:2 * nl]
        w_ref, m_ref, v_ref, g_out, d_out, m_out, v_out = refs[2 * nl:]
        for l in range(nl):
            g = ((own[l][...].astype(F32) + rcv[l][0].astype(F32)) + rcv[l][1].astype(F32)) + rcv[l][2].astype(F32)
            g = g[:, :cc]
            g_out[l] = g
            d_out[l], m_out[l], v_out[l] = _adamw(w_ref[l], g, m_ref[l], v_ref[l])

    blk = pl.BlockSpec((nl, tr, cc), lambda i, chip_ref: (0, i, 0))
    return pl.pallas_call(
        body, name=name,
        grid_spec=pltpu.PrefetchScalarGridSpec(
            num_scalar_prefetch=1, grid=(rr // tr,),
            in_specs=[pl.BlockSpec((3, tr, cp), lambda i, chip_ref: (0, i, 0))] * nl
            + [pl.BlockSpec((None, tr, cp), lambda i, chip_ref: (chip_ref[0], i, 0))] * nl + [blk, blk, blk],
            out_specs=[blk] * 4),
        out_shape=[_sds(w.shape, F32)] * 4, compiler_params=_params(("parallel",)),
    )(chip, *recv, *sums, w, m, v)


def sum_devices(g, name):
    _, rr, cc = g.shape

    def body(g_ref, o_ref):
        acc = g_ref[0]
        for j in range(1, NDEV):
            acc = acc + g_ref[j]
        o_ref[...] = acc

    return pl.pallas_call(
        body, name=name, grid=(1,), in_specs=[_full(g.shape)], out_specs=_full((rr, cc)), out_shape=_sds((rr, cc), F32),
        compiler_params=_params(("arbitrary",)),
    )(g)


def adamw_rows(g, w, m, v, name):
    rr, cc = w.shape

    def body(g_ref, w_ref, m_ref, v_ref, d_out, m_out, v_out):
        d_out[...], m_out[...], v_out[...] = _adamw(w_ref[...], g_ref[...], m_ref[...], v_ref[...])

    blk = _full((rr, cc))
    return pl.pallas_call(
        body, name=name, grid=(1,), in_specs=[blk] * 4, out_specs=[blk] * 3, out_shape=[_sds((rr, cc), F32)] * 3,
        compiler_params=_params(("arbitrary",)),
    )(g, w, m, v)


def _block_diag(w):
    eye = jnp.eye(LRU_BLOCKS, dtype=w.dtype)
    return (w[:, :, None, :] * eye[:, None, :, None]).reshape(D_LRU, D_LRU)


def _diag_blocks(wd):
    w4 = wd.reshape(LRU_BLOCKS, LRU_BD, LRU_BLOCKS, LRU_BD)
    return jnp.stack([w4[g, :, g, :] for g in range(LRU_BLOCKS)])


def _pack(arrs):
    flat = jnp.concatenate([a.reshape(-1) for a in arrs])
    return flat.reshape(-1, 128)


def _unpack(packed, shapes):
    flat = packed.reshape(-1)
    out, o = [], 0
    for s in shapes:
        n = int(np.prod(s))
        out.append(flat[o:o + n].reshape(s))
        o += n
    return out


REP_NAMES = ["norm_mix", "conv_b", "gate_a_w", "gate_a_b", "gate_x_w", "gate_x_b", "lru_lambda", "lru_out_norm",
             "ret_out_norm", "norm_ffn", "norm_final"]


def kernel(x, meta_tokens, norm_mix, w_in, conv_w, conv_b, gate_a_w, gate_a_b, gate_x_w, gate_x_b, lru_lambda, lru_out_norm, ret_out_norm, w_out, norm_ffn, w_gate, w_up, w_down, norm_final, loss_target, m_meta_tokens, m_norm_mix, m_w_in, m_conv_w, m_conv_b, m_gate_a_w, m_gate_a_b, m_gate_x_w, m_gate_x_b, m_lru_lambda, m_lru_out_norm, m_ret_out_norm, m_w_out, m_norm_ffn, m_w_gate, m_w_up, m_w_down, m_norm_final, v_meta_tokens, v_norm_mix, v_w_in, v_conv_w, v_conv_b, v_gate_a_w, v_gate_a_b, v_gate_x_w, v_gate_x_b, v_lru_lambda, v_lru_out_norm, v_ret_out_norm, v_w_out, v_norm_ffn, v_w_gate, v_w_up, v_w_down, v_norm_final):
    xi, yi, ci = _place()
    dev = 4 * xi + 2 * yi + ci
    c_arr = jnp.reshape(ci, (1,)).astype(jnp.int32)

    meta_g, conv_g = all_gather([meta_tokens, conv_w], "ag_small")
    meta_full = jnp.transpose(meta_g, (1, 0, 2)).reshape(N_META, D)
    conv_full = jnp.transpose(conv_g, (1, 2, 0, 3)).reshape(DEPTH, CONV_W, D_LRU)
    tr_ = lambda a: jnp.transpose(a, (0, 2, 1))
    w_gate_t, m_w_gate_t, v_w_gate_t = tr_(w_gate), tr_(m_w_gate), tr_(v_w_gate)
    w_up_t, m_w_up_t, v_w_up_t = tr_(w_up), tr_(m_w_up), tr_(v_w_up)
    level1 = []
    token = meta_g
    for l in range(DEPTH):
        sel = jnp.stack([dev, jnp.int32(l)]).astype(jnp.int32)
        lands = to_wire(sel, w_in, w_gate_t, w_up_t, w_out, w_down, "to_wire")
        s1, r1, lands, token = ag_start(lands, token, f"ag_start_{l}")
        level1.append((s1, r1, lands))

    def as_weights(gi, gg, gu, go, gd):
        return dict(w_in=gi, w_gate=gg.reshape(D_FFP, D), w_up=gu.reshape(D_FFP, D), w_out=go.reshape(D, D),
                    w_down=gd.reshape(D_FFP, D))

    tables = _ret_tables()
    row = lambda a: a.reshape(1, -1)

    h = jnp.concatenate([jnp.zeros((PAD, D), F32), meta_full, x[0]], axis=0)
    saved, gathered = [], []
    s1, r1, lands = level1[0]
    s2, r2, first, order = ag_forward(s1[:4], r1[:4], lands[:1], token, "ag_forward_0_w_in")
    w = dict(w_in=ag_finish(s2, r2, first, h, "ag_finish_0_w_in")[0])
    for l in range(DEPTH):
        small = dict(cw=conv_full[l], cb=row(conv_b[l]), wa=_block_diag(gate_a_w[l]).astype(MXU_DTYPE), ba=row(gate_a_b[l]),
                     wx=_block_diag(gate_x_w[l]).astype(MXU_DTYPE), bx=row(gate_x_b[l]), lam=row(lru_lambda[l]),
                     gain=row(lru_out_norm[l]))
        hn1 = rmsnorm_fwd(h, row(norm_mix[l]), "rms_fwd")
        proj = mm_blocked_nn(hn1, w["w_in"], F32, "proj")
        ylru, hst = lru_fwd(proj, name="lru_fwd", **small)
        if l == 0:
            s2, r2, rest, order = ag_forward(s1[4:], r1[4:], lands[1:], ylru, "ag_forward_0_rest")
        ymix, states = ret_fwd(proj, ylru, tables, row(ret_out_norm[l]), "ret_fwd")
        if l == 0:
            w = as_weights(w["w_in"], *ag_finish(s2, r2, rest, ymix, "ag_finish_0_rest"))
        gathered.append(w)
        h_mid = mm_nn_res(ymix, w["w_out"], h, order, "out_proj")
        hn2 = rmsnorm_fwd(h_mid, row(norm_ffn[l]), "rms_fwd")
        gate, up, act = ffn_up(hn2, w["w_gate"], w["w_up"], "ffn_up")
        if l + 1 < DEPTH:
            s1, r1, lands = level1[l + 1]
            s2, r2, lands, order = ag_forward(s1, r1, lands, act, f"ag_forward_{l + 1}")
        h_out = mm_nn_res(act, w["w_down"], h_mid, order, "ffn_down")
        if l + 1 < DEPTH:
            w_next = as_weights(*ag_finish(s2, r2, lands, h_out, f"ag_finish_{l + 1}"))
        saved.append(dict(h=h, hn1=hn1, proj=proj, hst=hst, states=states, ymix=ymix, h_mid=h_mid, hn2=hn2, gate=gate, up=up,
                          act=act, small=small))
        h = h_out
        if l + 1 < DEPTH:
            w = w_next

    loss_p, dh, dh_b, g_norm_final = loss_head(h, row(norm_final), loss_target[0], "loss_head")
    loss = lax.psum(loss_p[0, 0], ("x", "y", "c"))

    rep = [None] * DEPTH
    convw_g = [None] * DEPTH
    inflight = []
    sib = None
    order = loss_p

    def sibling_done(l, tag, names, sib, after):
        parts, got = rs_sibling_wait(*sib, after, f"rs_sibling_wait_{tag}")
        sums = pair_sum(parts, got, c_arr, "pair_sum")
        flying, started = rs_chips_start(sums, f"rs_chips_start_{tag}")
        inflight.append((l, tag, names, flying))
        return started

    all5 = ("w_in", "w_gate", "w_up", "w_out", "w_down")
    for l in reversed(range(DEPTH)):
        w, s = gathered[l], saved[l]
        dgate, dup = ffn_down_bwd(dh_b, w["w_down"], s["gate"], s["up"], order, "ffn_down_bwd")
        dwd = mm_tn(s["act"], dh_b, PAIR, order, "dw_down").reshape(NDEV, FF_SHP, D)
        dwg = mm_tn(dgate, s["hn2"], PAIR, order, "dw_rows").reshape(NDEV, FF_SHP, D)
        dwu = mm_tn(dup, s["hn2"], PAIR, order, "dw_rows").reshape(NDEV, FF_SHP, D)
        if l == 0:
            ffn_sib, order = rs_sibling_start([dwg, dwu, dwd], "rs_sibling_start_0_ffn")
        dhn2 = mm_rows_nn([(dgate, w["w_gate"]), (dup, w["w_up"])], order, "ffn_up_bwd")
        if sib is not None:
            order = sibling_done(l + 1, str(l + 1), all5, sib, dhn2)
        dh_mid, dh_mid_b, g_norm_ffn = rmsnorm_bwd(s["h_mid"], row(norm_ffn[l]), dhn2, dh, "rms_bwd")
        dymix = mm_nt(dh_mid_b, w["w_out"], F32, order, "out_proj_bwd")
        if l == 0:
            order = sibling_done(0, "0_ffn", ("w_gate", "w_up", "w_down"), ffn_sib, dymix)
        dwo = mm_tn(s["ymix"], dh_mid_b, BN, order, "dw_out").reshape(NDEV, OUT_SH, D)
        dxg, lvec, dwa, dwx = lru_bwd(s["proj"], s["hst"], dymix, name="lru_bwd", **s["small"])
        dproj, g_ret_norm = ret_bwd(s["proj"], s["states"], dymix, dxg, tables, row(ret_out_norm[l]), "ret_bwd")
        dwi = mm_tn_blocked(s["hn1"], dproj, "dw_blocked")
        dhn1 = mm_blocked_nt([(dproj, w["w_in"])], order, "proj_bwd")
        dh, dh_b, g_norm_mix = rmsnorm_bwd(s["h"], row(norm_mix[l]), dhn1, dh_mid, "rms_bwd")

        rep[l] = [g_norm_mix, lvec[4], _diag_blocks(dwa), lvec[5], _diag_blocks(dwx), lvec[6], lvec[7], lvec[8], g_ret_norm,
                  g_norm_ffn]
        convw_g[l] = lvec[0:CONV_W]
        if l > 0:
            sib, order = rs_sibling_start([dwi, dwg, dwu, dwo, dwd], f"rs_sibling_start_{l}")
        else:
            sib, order = rs_sibling_start([dwi, dwo], "rs_sibling_start_0_mix")
    sibling_done(0, "0_mix", ("w_in", "w_out"), sib, dh)

    grad_x = dh[X0:][None]
    g_meta = dh[PAD:X0]

    rep_shapes = [(D,), (D_LRU,), (LRU_BLOCKS, LRU_BD, LRU_BD), (LRU_BLOCKS, LRU_BD), (LRU_BLOCKS, LRU_BD, LRU_BD),
                  (LRU_BLOCKS, LRU_BD), (D_LRU,), (D_LRU,), (D_RET,), (D,)]
    flat = [a for l in range(DEPTH) for a in rep[l]] + [g_norm_final] + [convw_g[l] for l in range(DEPTH)] + [g_meta]
    (gath,) = all_gather([_pack(flat)], "ag_grads")
    gsum = sum_devices(gath, "sum_devices")
    shapes = rep_shapes * DEPTH + [(D,)] + [(CONV_W, D_LRU)] * DEPTH + [(N_META, D)]
    parts = _unpack(gsum, shapes)
    nrep = len(rep_shapes)
    g_rep = {n: jnp.stack([parts[l * nrep + i] for l in range(DEPTH)]) for i, n in enumerate(REP_NAMES[:-1])}
    g_rep["norm_final"] = parts[DEPTH * nrep]
    g_convw = lax.dynamic_slice_in_dim(jnp.stack(parts[DEPTH * nrep + 1:DEPTH * nrep + 1 + DEPTH]), dev * (D_LRU // NDEV),
                                       D_LRU // NDEV, axis=2)
    g_metatok = lax.dynamic_slice_in_dim(parts[-1], dev * (D // NDEV), D // NDEV, axis=1)

    given = dict(norm_mix=(norm_mix, m_norm_mix, v_norm_mix), conv_b=(conv_b, m_conv_b, v_conv_b),
                 gate_a_w=(gate_a_w, m_gate_a_w, v_gate_a_w), gate_a_b=(gate_a_b, m_gate_a_b, v_gate_a_b),
                 gate_x_w=(gate_x_w, m_gate_x_w, v_gate_x_w), gate_x_b=(gate_x_b, m_gate_x_b, v_gate_x_b),
                 lru_lambda=(lru_lambda, m_lru_lambda, v_lru_lambda), lru_out_norm=(lru_out_norm, m_lru_out_norm, v_lru_out_norm),
                 ret_out_norm=(ret_out_norm, m_ret_out_norm, v_ret_out_norm), norm_ffn=(norm_ffn, m_norm_ffn, v_norm_ffn),
                 norm_final=(norm_final, m_norm_final, v_norm_final),
                 conv_w=(conv_w, m_conv_w, v_conv_w), meta_tokens=(meta_tokens, m_meta_tokens, v_meta_tokens))
    small_names = REP_NAMES + ["conv_w", "meta_tokens"]
    small_g = dict(g_rep, conv_w=g_convw, meta_tokens=g_metatok)
    small_shapes = [given[n][0].shape for n in small_names]
    packs = [_pack([small_g[n] for n in small_names])] + [_pack([given[n][k] for n in small_names]) for k in range(3)]
    upd = adamw_rows(*packs, "adamw_small")
    small_out = [dict(zip(small_names, _unpack(p, small_shapes))) for p in upd]

    arrived = {}
    for l, tag, names, flying in inflight:
        sums, recv = rs_chips_wait(*flying, upd[0], f"rs_chips_wait_{tag}")
        for i, n in enumerate(names):
            arrived[l, n] = (recv[i], sums[i])
    chip = jnp.reshape(2 * xi + yi, (1,)).astype(jnp.int32)

    def finish(wname, w_, m_, v_, tr):
        return adamw_big([arrived[l, wname][0] for l in range(DEPTH)], [arrived[l, wname][1] for l in range(DEPTH)], chip,
                         w_, m_, v_, tr, "adamw_" + wname)

    o_in = finish("w_in", w_in, m_w_in, v_w_in, 256)
    o_gate = [tr_(o) for o in finish("w_gate", w_gate_t, m_w_gate_t, v_w_gate_t, 32)]
    o_up = [tr_(o) for o in finish("w_up", w_up_t, m_w_up_t, v_w_up_t, 32)]
    o_out = finish("w_out", w_out, m_w_out, v_w_out, 64)
    o_down = finish("w_down", w_down, m_w_down, v_w_down, 32)

    bigs = dict(w_in=o_in, w_out=o_out, w_gate=o_gate, w_up=o_up, w_down=o_down)
    order = ["meta_tokens", "norm_mix", "w_in", "conv_w", "conv_b", "gate_a_w", "gate_a_b", "gate_x_w", "gate_x_b", "lru_lambda",
             "lru_out_norm", "ret_out_norm", "w_out", "norm_ffn", "w_gate", "w_up", "w_down", "norm_final"]
    grads = [bigs[n][0] if n in bigs else small_g[n] for n in order]
    rest = [[bigs[n][k + 1] if n in bigs else small_out[k][n] for n in order] for k in range(3)]
    return (loss, grad_x, *grads, *rest[0], *rest[1], *rest[2])
```

```python
import functools

import numpy as np
import jax
import jax.numpy as jnp
from jax import lax
from jax.experimental import pallas as pl
from jax.experimental.pallas import tpu as pltpu

F32, BF16 = jnp.float32, jnp.bfloat16
MXU_DTYPE = BF16
WIRE_DTYPE = BF16

D = 1024
SEQ = 2048
DEPTH = 4
N_META = 16
CH = 128
PAD = (-(SEQ + N_META)) % CH
T = SEQ + N_META + PAD
NCH = T // CH
X0 = PAD + N_META
D_LRU = 512
LRU_BLOCKS = 8
LRU_BD = 64
CONV_W = 4
LRU_C = 8.0
D_RET = 512
HEADS = 4
HD = 128
ROPE_BASE = 10000.0
D_IN = 3072
D_FF = 2816
NDEV = 8
IN_SH = D_IN // NDEV
FF_SH = D_FF // NDEV
FF_SHP = 384
D_FFP = NDEV * FF_SHP
OUT_SH = D // NDEV
EPS = 1e-6
TM = 544
TR = 1088
VMEM_LIMIT = 56 * 2**20
MESH = pl.DeviceIdType.MESH

ADAM_LR, ADAM_B1, ADAM_B2, ADAM_EPS, ADAM_WD, ADAM_STEP = 0.001, 0.9, 0.999, 1e-08, 0.01, 10

NN = ((1,), (0,))
NT = ((1,), (1,))
TN = ((0,), (0,))


def _dot(a, b, dims):
    return lax.dot_general(a.astype(MXU_DTYPE), b.astype(MXU_DTYPE), (dims, ((), ())), preferred_element_type=F32)


def _sds(shape, dtype):
    return jax.ShapeDtypeStruct(shape, dtype)


def _params(sem=None):
    return pltpu.CompilerParams(dimension_semantics=sem, vmem_limit_bytes=VMEM_LIMIT)


def _full(shape):
    n = len(shape)
    return pl.BlockSpec(shape, lambda *_: (0,) * n)


def rmsnorm_fwd(h, gain, name):
    def body(h_ref, g_ref, o_ref):
        x = h_ref[...]
        ms = jnp.mean(x * x, axis=-1, keepdims=True)
        o_ref[...] = (x * lax.rsqrt(ms + EPS) * g_ref[...]).astype(o_ref.dtype)

    return pl.pallas_call(
        body, name=name, grid=(T // TM,),
        in_specs=[pl.BlockSpec((TM, D), lambda i: (i, 0)), _full((1, D))],
        out_specs=pl.BlockSpec((TM, D), lambda i: (i, 0)),
        out_shape=_sds((T, D), MXU_DTYPE), compiler_params=_params(("parallel",)),
    )(h, gain)


def rmsnorm_bwd(h, gain, dhn, dres, name):
    def body(h_ref, g_ref, dhn_ref, dres_ref, dh_ref, dhb_ref, dg_ref):
        x = h_ref[...]
        rstd = lax.rsqrt(jnp.mean(x * x, axis=-1, keepdims=True) + EPS)
        xhat = x * rstd
        dy = dhn_ref[...]
        dyg = dy * g_ref[...]
        dh = dres_ref[...] + rstd * (dyg - xhat * jnp.mean(dyg * xhat, axis=-1, keepdims=True))
        dh_ref[...] = dh
        dhb_ref[...] = dh.astype(dhb_ref.dtype)

        @pl.when(pl.program_id(0) == 0)
        def _():
            dg_ref[...] = jnp.zeros_like(dg_ref)
        dg_ref[...] += jnp.sum(dy * xhat, axis=0, keepdims=True)

    row = pl.BlockSpec((TM, D), lambda i: (i, 0))
    return pl.pallas_call(
        body, name=name, grid=(T // TM,),
        in_specs=[row, _full((1, D)), row, row],
        out_specs=[row, row, _full((1, D))],
        out_shape=[_sds((T, D), F32), _sds((T, D), MXU_DTYPE), _sds((1, D), F32)], compiler_params=_params(("arbitrary",)),
    )(h, gain, dhn, dres)


def loss_head(h, gain, target, name):
    def body(h_ref, g_ref, t_ref, loss_ref, dh_ref, dhb_ref, dg_ref):
        i = pl.program_id(0)

        @pl.when(i == 0)
        def _():
            loss_ref[...] = jnp.zeros_like(loss_ref)
            dg_ref[...] = jnp.zeros_like(dg_ref)
            dh_ref[...] = jnp.zeros_like(dh_ref)
            dhb_ref[...] = jnp.zeros_like(dhb_ref)

        @pl.when(i > 0)
        def _():
            x = h_ref[...]
            g = g_ref[...]
            rstd = lax.rsqrt(jnp.mean(x * x, axis=-1, keepdims=True) + EPS)
            xhat = x * rstd
            err = xhat * g - t_ref[...]
            loss_ref[...] += 0.5 * jnp.sum(jnp.mean(err * err, axis=-1, keepdims=True), axis=0, keepdims=True)
            dy = err * (1.0 / D)
            dyg = dy * g
            dh = rstd * (dyg - xhat * jnp.mean(dyg * xhat, axis=-1, keepdims=True))
            dh_ref[...] = dh
            dhb_ref[...] = dh.astype(dhb_ref.dtype)
            dg_ref[...] += jnp.sum(dy * xhat, axis=0, keepdims=True)

    row = pl.BlockSpec((CH, D), lambda i: (i, 0))
    return pl.pallas_call(
        body, name=name, grid=(NCH,),
        in_specs=[row, _full((1, D)), pl.BlockSpec((CH, D), lambda i: (jnp.maximum(i - 1, 0), 0))],
        out_specs=[_full((8, 128)), row, row, _full((1, D))],
        out_shape=[_sds((8, 128), F32), _sds((T, D), F32), _sds((T, D), MXU_DTYPE), _sds((1, D), F32)],
        compiler_params=_params(("arbitrary",)),
    )(h, gain, target)


PAIR = 2 * IN_SH
NPAIR = NDEV // 2
BN = 256


def _pair_cols(w_ref):
    return jnp.concatenate([w_ref[0], w_ref[1]], axis=1)


W_PAIR = lambda k: pl.BlockSpec((2, k, IN_SH), lambda j: (j, 0, 0))
COLS_PAIR = pl.BlockSpec((T, PAIR), lambda j: (0, j))
ANYSPEC = pl.BlockSpec(memory_space=pl.ANY)


def mm_blocked_nn(a, w, out_dtype, name):
    k = a.shape[1]

    def body(a_ref, w_ref, o_ref):
        o_ref[...] = _dot(a_ref[...], _pair_cols(w_ref), NN).astype(o_ref.dtype)

    return pl.pallas_call(
        body, name=name, grid=(NPAIR,),
        in_specs=[_full((T, k)), W_PAIR(k)], out_specs=COLS_PAIR,
        out_shape=_sds((T, NDEV * IN_SH), out_dtype), compiler_params=_params(("parallel",)),
    )(a, w)


def mm_nn_res(a, w, res, after, name):
    k = a.shape[1]

    def body(a_ref, w_ref, r_ref, after_ref, o_ref):
        del after_ref
        o_ref[...] = r_ref[...] + _dot(a_ref[...], w_ref[...], NN)

    col = pl.BlockSpec((T, BN), lambda j: (0, j))
    return pl.pallas_call(
        body, name=name, grid=(D // BN,),
        in_specs=[_full((T, k)), pl.BlockSpec((k, BN), lambda j: (0, j)), col, ANYSPEC], out_specs=col,
        out_shape=_sds((T, D), F32), compiler_params=_params(("parallel",)),
    )(a, w, res, after)


def ffn_up(hn, wg, wu, name):
    def body(a_ref, wg_ref, wu_ref, g_ref, u_ref, act_ref):
        a = a_ref[...]
        g = _dot(a, wg_ref[...], NT)
        u = _dot(a, wu_ref[...], NT)
        g_ref[...] = g.astype(g_ref.dtype)
        u_ref[...] = u.astype(u_ref.dtype)
        act_ref[...] = (jax.nn.silu(g) * u).astype(act_ref.dtype)

    wspec = pl.BlockSpec((PAIR, D), lambda j, i: (j, 0))
    ospec = pl.BlockSpec((TR, PAIR), lambda j, i: (i, j))
    return pl.pallas_call(
        body, name=name, grid=(NPAIR, T // TR),
        in_specs=[pl.BlockSpec((TR, D), lambda j, i: (i, 0)), wspec, wspec], out_specs=[ospec] * 3,
        out_shape=[_sds((T, D_FFP), MXU_DTYPE)] * 3, compiler_params=_params(("parallel", "parallel")),
    )(hn, wg, wu)


def ffn_down_bwd(dh, wd, gate, up, after, name):
    def body(dh_ref, wd_ref, g_ref, u_ref, after_ref, dg_ref, du_ref):
        del after_ref
        dact = _dot(dh_ref[...], wd_ref[...], NT)
        g = g_ref[...].astype(F32)
        u = u_ref[...].astype(F32)
        sg = jax.nn.sigmoid(g)
        dg_ref[...] = (dact * u * (sg * (1.0 + g * (1.0 - sg)))).astype(dg_ref.dtype)
        du_ref[...] = (dact * (g * sg)).astype(du_ref.dtype)

    blk = pl.BlockSpec((TR, PAIR), lambda j, i: (i, j))
    return pl.pallas_call(
        body, name=name, grid=(NPAIR, T // TR),
        in_specs=[pl.BlockSpec((TR, D), lambda j, i: (i, 0)), pl.BlockSpec((PAIR, D), lambda j, i: (j, 0)), blk, blk, ANYSPEC],
        out_specs=[blk, blk],
        out_shape=[_sds((T, D_FFP), MXU_DTYPE)] * 2, compiler_params=_params(("parallel", "parallel")),
    )(dh, wd, gate, up, after)


def mm_nt(a, w, out_dtype, after, name):
    n = a.shape[1]

    def body(a_ref, w_ref, after_ref, o_ref):
        del after_ref
        o_ref[...] = _dot(a_ref[...], w_ref[...], NT).astype(o_ref.dtype)

    return pl.pallas_call(
        body, name=name, grid=(D // BN,),
        in_specs=[_full((T, n)), pl.BlockSpec((BN, n), lambda j: (j, 0)), ANYSPEC],
        out_specs=pl.BlockSpec((T, BN), lambda j: (0, j)),
        out_shape=_sds((T, D), out_dtype), compiler_params=_params(("parallel",)),
    )(a, w, after)


def mm_blocked_nt(pairs, after, name):
    n = len(pairs)

    def body(*refs):
        o_ref = refs[2 * n + 1]

        @pl.when(pl.program_id(1) == 0)
        def _():
            o_ref[...] = jnp.zeros_like(o_ref)
        for p in range(n):
            o_ref[...] += _dot(refs[2 * p][...], _pair_cols(refs[2 * p + 1]), NT)

    specs, args = [], []
    for a, w in pairs:
        specs += [pl.BlockSpec((TR, PAIR), lambda i, j: (i, j)), pl.BlockSpec((2, D, IN_SH), lambda i, j: (j, 0, 0))]
        args += [a, w]
    return pl.pallas_call(
        body, name=name, grid=(T // TR, NPAIR), in_specs=specs + [ANYSPEC],
        out_specs=pl.BlockSpec((TR, D), lambda i, j: (i, 0)),
        out_shape=_sds((T, D), F32), compiler_params=_params(("parallel", "arbitrary")),
    )(*args, after)


def mm_rows_nn(pairs, after, name):
    n = len(pairs)

    def body(*refs):
        o_ref = refs[2 * n + 1]

        @pl.when(pl.program_id(1) == 0)
        def _():
            o_ref[...] = jnp.zeros_like(o_ref)
        for p in range(n):
            o_ref[...] += _dot(refs[2 * p][...], refs[2 * p + 1][...], NN)

    specs, args = [], []
    for a, w in pairs:
        specs += [pl.BlockSpec((TR, PAIR), lambda i, j: (i, j)), pl.BlockSpec((PAIR, D), lambda i, j: (j, 0))]
        args += [a, w]
    return pl.pallas_call(
        body, name=name, grid=(T // TR, NPAIR), in_specs=specs + [ANYSPEC],
        out_specs=pl.BlockSpec((TR, D), lambda i, j: (i, 0)),
        out_shape=_sds((T, D), F32), compiler_params=_params(("parallel", "arbitrary")),
    )(*args, after)


def mm_tn_blocked(a, b, name):
    def body(a_ref, b_ref, o_ref):
        o = _dot(a_ref[...], b_ref[...], TN).astype(o_ref.dtype)
        o_ref[0] = o[:, :IN_SH]
        o_ref[1] = o[:, IN_SH:]

    return pl.pallas_call(
        body, name=name, grid=(NPAIR,),
        in_specs=[_full((T, D)), COLS_PAIR], out_specs=W_PAIR(D),
        out_shape=_sds((NDEV, D, IN_SH), WIRE_DTYPE), compiler_params=_params(("parallel",)),
    )(a, b)


def mm_tn(a, b, bm, after, name):
    m = a.shape[1]

    def body(a_ref, b_ref, after_ref, o_ref):
        del after_ref
        o_ref[...] = _dot(a_ref[...], b_ref[...], TN).astype(o_ref.dtype)

    return pl.pallas_call(
        body, name=name, grid=(m // bm,),
        in_specs=[pl.BlockSpec((T, bm), lambda i: (0, i)), _full((T, D)), ANYSPEC],
        out_specs=pl.BlockSpec((bm, D), lambda i: (i, 0)),
        out_shape=_sds((m, D), WIRE_DTYPE), compiler_params=_params(("parallel",)),
    )(a, b, after)


def _softplus_neg(lam):
    return jnp.maximum(-lam, 0.0) + jnp.log1p(jnp.exp(-jnp.abs(lam)))


def _lru_gates(pa, px, xc, lam):
    r = jax.nn.sigmoid(pa)
    ig = jax.nn.sigmoid(px)
    log_a = -LRU_C * r * _softplus_neg(lam)
    a = jnp.exp(log_a)
    mult = jnp.sqrt(-jnp.tanh(log_a) * (jnp.exp(2.0 * log_a) + 1.0))
    return a, mult * (ig * xc)


def _lru_out(h, g, gain):
    z = h * jax.nn.gelu(g)
    return z * lax.rsqrt(jnp.mean(z * z, axis=-1, keepdims=True) + EPS) * gain


def _conv_taps(x, xprev, row):
    taps = [x]
    for s in range(1, CONV_W):
        taps.append(jnp.where(row < s, pltpu.roll(xprev, s, 0), pltpu.roll(x, s, 0)))
    return taps


def _conv(taps, cw_ref, cb):
    xc = cb + cw_ref[CONV_W - 1:CONV_W, :] * taps[0]
    for s in range(1, CONV_W):
        xc = xc + cw_ref[CONV_W - 1 - s:CONV_W - s, :] * taps[s]
    return xc


def lru_fwd(proj, cw, cb, wa, ba, wx, bx, lam, gain, name):
    def body(x_ref, g_ref, cw_ref, cb_ref, wa_ref, ba_ref, wx_ref, bx_ref, lam_ref, gain_ref,
             y_ref, h_ref, xprev_scr, a_scr, b_scr, carry_scr):
        i = pl.program_id(0)

        @pl.when(i == 0)
        def _():
            xprev_scr[...] = jnp.zeros_like(xprev_scr)
            carry_scr[...] = jnp.zeros_like(carry_scr)

        x = x_ref[...]
        row = lax.broadcasted_iota(jnp.int32, (CH, D_LRU), 0)
        xc = _conv(_conv_taps(x, xprev_scr[...], row), cw_ref, cb_ref[...])
        pa = _dot(xc, wa_ref[...], NN) + ba_ref[...]
        px = _dot(xc, wx_ref[...], NN) + bx_ref[...]
        a, b = _lru_gates(pa, px, xc, lam_ref[...])
        a_scr[...] = a
        b_scr[...] = jnp.where(i * CH + row >= PAD, b, 0.0)
        h = carry_scr[...]
        for t in range(CH):
            h = a_scr[t:t + 1, :] * h + b_scr[t:t + 1, :]
            h_ref[t:t + 1, :] = h
        carry_scr[...] = h
        xprev_scr[...] = x
        y_ref[...] = _lru_out(h_ref[...], g_ref[...], gain_ref[...]).astype(y_ref.dtype)

    vec = _full((1, D_LRU))
    mat = _full((D_LRU, D_LRU))
    return pl.pallas_call(
        body, name=name, grid=(NCH,),
        in_specs=[pl.BlockSpec((CH, D_LRU), lambda i: (i, 0)), pl.BlockSpec((CH, D_LRU), lambda i: (i, 1)),
                  _full((CONV_W, D_LRU)), vec, mat, vec, mat, vec, vec, vec],
        out_specs=[pl.BlockSpec((CH, D_LRU), lambda i: (i, 0)), pl.BlockSpec((CH, D_LRU), lambda i: (i, 0))],
        out_shape=[_sds((T, D_LRU), MXU_DTYPE), _sds((T, D_LRU), F32)],
        scratch_shapes=[pltpu.VMEM((CH, D_LRU), F32), pltpu.VMEM((CH, D_LRU), F32), pltpu.VMEM((CH, D_LRU), F32),
                        pltpu.VMEM((1, D_LRU), F32)],
        compiler_params=_params(("arbitrary",)),
    )(proj, proj, cw, cb, wa, ba, wx, bx, lam, gain)


LRU_VEC_ROWS = 16


def lru_bwd(proj, hst, dymix, cw, cb, wa, ba, wx, bx, lam, gain, name):
    last = NCH - 1

    def body(x_ref, xp_ref, g_ref, h_ref, hp_ref, dy_ref, cw_ref, cb_ref, wa_ref, ba_ref, wx_ref, bx_ref, lam_ref,
             gain_ref, dxg_ref, vec_ref, dwa_ref, dwx_ref, a_scr, dh_scr, g_scr, carry_scr, dxcn_scr):
        i = pl.program_id(0)
        ib = last - i

        @pl.when(i == 0)
        def _():
            carry_scr[...] = jnp.zeros_like(carry_scr)
            dxcn_scr[...] = jnp.zeros_like(dxcn_scr)
            vec_ref[...] = jnp.zeros_like(vec_ref)
            dwa_ref[...] = jnp.zeros_like(dwa_ref)
            dwx_ref[...] = jnp.zeros_like(dwx_ref)

        x = x_ref[...]
        row = lax.broadcasted_iota(jnp.int32, (CH, D_LRU), 0)
        valid = ib * CH + row >= PAD
        taps = _conv_taps(x, xp_ref[...], row)
        xc = _conv(taps, cw_ref, cb_ref[...])
        pa = _dot(xc, wa_ref[...], NN) + ba_ref[...]
        px = _dot(xc, wx_ref[...], NN) + bx_ref[...]
        (a, _), vjp_gates = jax.vjp(_lru_gates, pa, px, xc, lam_ref[...])
        h = h_ref[...]
        _, vjp_out = jax.vjp(_lru_out, h, g_ref[...], gain_ref[...])
        dh, dg, dgain = vjp_out(dy_ref[...].astype(F32))
        a_scr[...] = a
        dh_scr[...] = dh
        c = carry_scr[...]
        for t in range(CH - 1, -1, -1):
            gt = dh_scr[t:t + 1, :] + c
            g_scr[t:t + 1, :] = gt
            c = a_scr[t:t + 1, :] * gt
        carry_scr[...] = c
        gg = g_scr[...]
        hprev = jnp.where(row < 1, pltpu.roll(hp_ref[...], 1, 0), pltpu.roll(h, 1, 0))
        da = jnp.where(valid, gg * hprev, 0.0)
        db = jnp.where(valid, gg, 0.0)
        dpa, dpx, dxc, dlam = vjp_gates((da, db))
        dxc = dxc + _dot(dpa, wa_ref[...], NT) + _dot(dpx, wx_ref[...], NT)
        dwa_ref[...] += _dot(xc, dpa, TN)
        dwx_ref[...] += _dot(xc, dpx, TN)
        for s in range(CONV_W):
            vec_ref[CONV_W - 1 - s:CONV_W - s, :] += jnp.sum(dxc * taps[s], axis=0, keepdims=True)
        vec_ref[4:5, :] += jnp.sum(dxc, axis=0, keepdims=True)
        vec_ref[5:6, :] += jnp.sum(dpa, axis=0, keepdims=True)
        vec_ref[6:7, :] += jnp.sum(dpx, axis=0, keepdims=True)
        vec_ref[7:8, :] += dlam
        vec_ref[8:9, :] += dgain
        dxn = dxcn_scr[...]
        dx = cw_ref[CONV_W - 1:CONV_W, :] * dxc
        for s in range(1, CONV_W):
            ahead = jnp.where(row >= CH - s, pltpu.roll(dxn, CH - s, 0), pltpu.roll(dxc, CH - s, 0))
            dx = dx + cw_ref[CONV_W - 1 - s:CONV_W - s, :] * ahead
        dxcn_scr[...] = dxc
        dxg_ref[:, :D_LRU] = jnp.where(valid, dx, 0.0).astype(dxg_ref.dtype)
        dxg_ref[:, D_LRU:] = dg.astype(dxg_ref.dtype)

    vec = _full((1, D_LRU))
    mat = _full((D_LRU, D_LRU))

    def blk(col, shift=0):
        return pl.BlockSpec((CH, D_LRU), lambda i: (jnp.maximum(last - i - shift, 0), col))

    return pl.pallas_call(
        body, name=name, grid=(NCH,),
        in_specs=[blk(0), blk(0, 1), blk(1), blk(0), blk(0, 1), blk(0),
                  _full((CONV_W, D_LRU)), vec, mat, vec, mat, vec, vec, vec],
        out_specs=[pl.BlockSpec((CH, 2 * D_LRU), lambda i: (last - i, 0)), _full((LRU_VEC_ROWS, D_LRU)), mat, mat],
        out_shape=[_sds((T, 2 * D_LRU), MXU_DTYPE), _sds((LRU_VEC_ROWS, D_LRU), F32),
                   _sds((D_LRU, D_LRU), F32), _sds((D_LRU, D_LRU), F32)],
        scratch_shapes=[pltpu.VMEM((CH, D_LRU), F32), pltpu.VMEM((CH, D_LRU), F32), pltpu.VMEM((CH, D_LRU), F32),
                        pltpu.VMEM((1, D_LRU), F32), pltpu.VMEM((CH, D_LRU), F32)],
        compiler_params=_params(("arbitrary",)),
    )(proj, proj, proj, hst, hst, dymix, cw, cb, wa, ba, wx, bx, lam, gain)


def _ret_tables():
    half = HD // 2
    pos = jnp.arange(T, dtype=F32) - float(PAD)
    inv = ROPE_BASE ** (-jnp.arange(half, dtype=F32) / half)
    ang = pos[:, None] * inv[None, :]
    cos = jnp.concatenate([jnp.cos(ang), jnp.cos(ang)], axis=-1)
    sin = jnp.concatenate([-jnp.sin(ang), jnp.sin(ang)], axis=-1)
    log_g = jnp.log(1.0 - 2.0 ** (-5.0 - jnp.arange(HEADS, dtype=F32)))
    idx = jnp.arange(CH, dtype=F32)
    diff = idx[:, None] - idx[None, :]
    dmask = jnp.where(diff[None] >= 0, jnp.exp(jnp.maximum(diff, 0.0)[None] * log_g[:, None, None]), 0.0)
    xi = jnp.exp((idx + 1.0)[None, :] * log_g[:, None])
    zeta = jnp.exp((CH - 1.0 - idx)[None, :] * log_g[:, None])
    xi = jnp.broadcast_to(xi[:, :, None], (HEADS, CH, HD))
    zeta = jnp.broadcast_to(zeta[:, :, None], (HEADS, CH, HD))
    return cos, sin, dmask, xi, zeta


def _chunk_decay():
    log_g = np.log(np.float32(1.0) - np.float32(2.0) ** (np.float32(-5.0) - np.arange(HEADS, dtype=np.float32)))
    return [float(v) for v in np.exp(np.float32(CH) * log_g.astype(np.float32))]


def _rope(x, cos, sin):
    return x * cos + pltpu.roll(x, HD // 2, 1) * sin


def ret_fwd(proj, ylru, tables, gain, name):
    cos, sin, dmask, xi, zeta = tables
    gch = _chunk_decay()
    scale = HD ** -0.5

    def body(q_ref, k_ref, v_ref, g_ref, cos_ref, sin_ref, dm_ref, xi_ref, zt_ref, gain_ref, ylru_ref,
             y_ref, st_ref, s_scr):
        @pl.when(pl.program_id(0) == 0)
        def _():
            s_scr[...] = jnp.zeros_like(s_scr)

        y_ref[:, :D_LRU] = ylru_ref[...]
        cs, sn = cos_ref[...], sin_ref[...]
        hs = range(HEADS)
        sl = [slice(HD * h, HD * (h + 1)) for h in hs]
        qr = [_rope(q_ref[:, sl[h]], cs, sn).astype(MXU_DTYPE) for h in hs]
        kf = [_rope(k_ref[:, sl[h]], cs, sn) * scale for h in hs]
        kr = [kf[h].astype(MXU_DTYPE) for h in hs]
        v = [v_ref[:, sl[h]].astype(MXU_DTYPE) for h in hs]
        s = [s_scr[h] for h in hs]
        for h in hs:
            st_ref[h] = s[h]
        sc = [_dot(qr[h], kr[h], NT) * dm_ref[h] for h in hs]
        cross = [_dot(qr[h], s[h], NN) * xi_ref[h] for h in hs]
        for h in hs:
            s_scr[h] = s[h] * gch[h] + _dot(kf[h] * zt_ref[h], v[h], TN)
        y = [_dot(sc[h], v[h], NN) + cross[h] for h in hs]
        yc = [y[h] - jnp.mean(y[h], axis=-1, keepdims=True) for h in hs]
        yn = [yc[h] * lax.rsqrt(jnp.mean(yc[h] * yc[h], axis=-1, keepdims=True) + EPS) for h in hs]
        for h in hs:
            so = slice(D_LRU + HD * h, D_LRU + HD * (h + 1))
            y_ref[:, so] = (jax.nn.silu(g_ref[:, sl[h]]) * (yn[h] * gain_ref[:, sl[h]])).astype(y_ref.dtype)

    def col(c):
        return pl.BlockSpec((CH, D_RET), lambda n: (n, c))

    tab = pl.BlockSpec((CH, HD), lambda n: (n, 0))
    cst = _full((HEADS, CH, HD))
    return pl.pallas_call(
        body, name=name, grid=(NCH,),
        in_specs=[col(2), col(3), col(4), col(5), tab, tab, cst, cst, cst, _full((1, D_RET)), col(0)],
        out_specs=[pl.BlockSpec((CH, D), lambda n: (n, 0)), pl.BlockSpec((None, HEADS, HD, HD), lambda n: (n, 0, 0, 0))],
        out_shape=[_sds((T, D), MXU_DTYPE), _sds((NCH, HEADS, HD, HD), F32)],
        scratch_shapes=[pltpu.VMEM((HEADS, HD, HD), F32)],
        compiler_params=_params(("arbitrary",)),
    )(proj, proj, proj, proj, cos, sin, dmask, xi, zeta, gain, ylru)


def ret_bwd(proj, states, dymix, dxg, tables, gain, name):
    cos, sin, dmask, xi, zeta = tables
    gch = _chunk_decay()
    scale = HD ** -0.5
    last = NCH - 1

    def body(q_ref, k_ref, v_ref, g_ref, st_ref, do_ref, cos_ref, sin_ref, dm_ref, xi_ref, zt_ref, gain_ref, dxg_ref,
             dp_ref, dgain_ref, ds_scr):
        @pl.when(pl.program_id(0) == 0)
        def _():
            ds_scr[...] = jnp.zeros_like(ds_scr)
            dgain_ref[...] = jnp.zeros_like(dgain_ref)

        dp_ref[:, :2 * D_LRU] = dxg_ref[...]
        cs, sn = cos_ref[...], sin_ref[...]
        hs = range(HEADS)
        sl = [slice(HD * h, HD * (h + 1)) for h in hs]

        def out(j, h):
            return slice(2 * D_LRU + j * D_RET + HD * h, 2 * D_LRU + j * D_RET + HD * (h + 1))

        b16 = lambda xs: [x.astype(MXU_DTYPE) for x in xs]
        qr = b16([_rope(q_ref[:, sl[h]], cs, sn) for h in hs])
        kf = [_rope(k_ref[:, sl[h]], cs, sn) * scale for h in hs]
        kr = b16(kf)
        kz = b16([kf[h] * zt_ref[h] for h in hs])
        v = b16([v_ref[:, sl[h]] for h in hs])
        s = b16([st_ref[h] for h in hs])
        ds = [ds_scr[h] for h in hs]
        dsb = b16(ds)
        sc = [_dot(qr[h], kr[h], NT) * dm_ref[h] for h in hs]
        scb = b16(sc)
        y = [_dot(scb[h], v[h], NN) + _dot(qr[h], s[h], NN) * xi_ref[h] for h in hs]
        yc = [y[h] - jnp.mean(y[h], axis=-1, keepdims=True) for h in hs]
        rstd = [lax.rsqrt(jnp.mean(yc[h] * yc[h], axis=-1, keepdims=True) + EPS) for h in hs]
        yn = [yc[h] * rstd[h] for h in hs]
        dy = []
        for h in hs:
            g = g_ref[:, sl[h]]
            gain = gain_ref[:, sl[h]]
            sg = jax.nn.sigmoid(g)
            silu = g * sg
            dout = do_ref[:, sl[h]].astype(F32)
            dgain_ref[:, sl[h]] += jnp.sum(dout * silu * yn[h], axis=0, keepdims=True)
            dp_ref[:, out(3, h)] = (dout * yn[h] * gain * (sg * (1.0 + g * (1.0 - sg)))).astype(dp_ref.dtype)
            dyn = dout * silu * gain
            dy.append(rstd[h] * (dyn - jnp.mean(dyn, axis=-1, keepdims=True)
                                 - yn[h] * jnp.mean(dyn * yn[h], axis=-1, keepdims=True)))
        dyb = b16(dy)
        dqs = b16([dy[h] * xi_ref[h] for h in hs])
        dp = b16([_dot(dyb[h], v[h], NT) * dm_ref[h] for h in hs])
        dv = [_dot(scb[h], dyb[h], TN) + _dot(kz[h], dsb[h], NN) for h in hs]
        dqr = [_dot(dp[h], kr[h], NN) + _dot(dqs[h], s[h], NT) for h in hs]
        dkr = [_dot(dp[h], qr[h], TN) + _dot(v[h], dsb[h], NT) * zt_ref[h] for h in hs]
        for h in hs:
            ds_scr[h] = gch[h] * ds[h] + _dot(qr[h], dqs[h], TN)
        for h in hs:
            dp_ref[:, out(0, h)] = (dqr[h] * cs + pltpu.roll(dqr[h] * sn, HD // 2, 1)).astype(dp_ref.dtype)
            dp_ref[:, out(1, h)] = ((dkr[h] * cs + pltpu.roll(dkr[h] * sn, HD // 2, 1)) * scale).astype(dp_ref.dtype)
            dp_ref[:, out(2, h)] = dv[h].astype(dp_ref.dtype)

    def col(c):
        return pl.BlockSpec((CH, D_RET), lambda n: (last - n, c))

    tab = pl.BlockSpec((CH, HD), lambda n: (last - n, 0))
    cst = _full((HEADS, CH, HD))
    return pl.pallas_call(
        body, name=name, grid=(NCH,),
        in_specs=[col(2), col(3), col(4), col(5), pl.BlockSpec((None, HEADS, HD, HD), lambda n: (last - n, 0, 0, 0)), col(1),
                  tab, tab, cst, cst, cst, _full((1, D_RET)), pl.BlockSpec((CH, 2 * D_LRU), lambda n: (last - n, 0))],
        out_specs=[pl.BlockSpec((CH, D_IN), lambda n: (last - n, 0)), _full((1, D_RET))],
        out_shape=[_sds((T, D_IN), MXU_DTYPE), _sds((1, D_RET), F32)],
        scratch_shapes=[pltpu.VMEM((HEADS, HD, HD), F32)],
        compiler_params=_params(("arbitrary",)),
    )(proj, proj, proj, proj, states, dymix, cos, sin, dmask, xi, zeta, gain, dxg)


HBM = pl.BlockSpec(memory_space=pltpu.HBM)


def _place():
    return lax.axis_index("x"), lax.axis_index("y"), lax.axis_index("c")


def all_gather(arrs, name):
    n = len(arrs)

    def body(*refs):
        ins, outs = refs[:n], refs[n:2 * n]
        send_sems, recv_sems, local_sems = refs[2 * n:]
        x, y, c = _place()
        me, sibling = (x, y, c), (x, y, 1 - c)
        chips = [(1 - x, y), (x, 1 - y), (1 - x, 1 - y)]

        def copy(a, k, block, to, src=None):
            px, py, pc = block
            dst = outs[a].at[4 * px + 2 * py + pc]
            return pltpu.make_async_remote_copy(
                src_ref=dst if src is None else src, dst_ref=dst, send_sem=send_sems.at[a, k], recv_sem=recv_sems.at[a, k],
                device_id=to, device_id_type=MESH)

        mine = [pltpu.make_async_copy(ins[a], outs[a].at[4 * x + 2 * y + c], local_sems.at[a]) for a in range(n)]
        for cp in mine:
            cp.start()
        first = []
        for a in range(n):
            first.append(copy(a, 0, me, sibling, src=ins[a]))
            first += [copy(a, 1 + j, me, (*chip, c), src=ins[a]) for j, chip in enumerate(chips)]
        for cp in first:
            cp.start()
        passed = []
        for j, chip in enumerate(chips):
            for a in range(n):
                copy(a, 1 + j, (*chip, c), me).wait_recv()
                passed.append(copy(a, 4 + j, (*chip, c), sibling))
                passed[-1].start()
        for a in range(n):
            copy(a, 0, sibling, me).wait_recv()
            for j, chip in enumerate(chips):
                copy(a, 4 + j, (*chip, 1 - c), me).wait_recv()
        for cp in first + passed:
            cp.wait_send()
        for cp in mine:
            cp.wait()

    return pl.pallas_call(
        body, name=name,
        in_specs=[HBM] * n, out_specs=[HBM] * n,
        out_shape=[_sds((NDEV,) + a.shape, a.dtype) for a in arrs],
        scratch_shapes=[pltpu.SemaphoreType.DMA((n, 7)), pltpu.SemaphoreType.DMA((n, 7)), pltpu.SemaphoreType.DMA((n,))],
    )(*arrs)


SEM = pl.BlockSpec(memory_space=pltpu.SEMAPHORE)
ANY = pl.BlockSpec(memory_space=pl.ANY)
EFFECT = pltpu.SideEffectType.DATAFLOW_SIDE_EFFECTING


def _hbm(a):
    return pltpu.with_memory_space_constraint(a, pltpu.HBM)


def _hbm_like(arrs):
    return [pltpu.HBM(a.shape, a.dtype) for a in arrs]


def _dma_sems(count):
    return [pltpu.SemaphoreType.DMA(())] * count


def _ag_copy(lands, send_sems, recv_sems, per):
    def copy(a, k, block, to, src=None):
        px, py, pc = block
        dst = lands[a].at[4 * px + 2 * py + pc]
        return pltpu.make_async_remote_copy(
            src_ref=dst if src is None else src, dst_ref=dst, send_sem=send_sems[a * per + k], recv_sem=recv_sems[a * per + k],
            device_id=to, device_id_type=MESH)
    return copy


def to_wire(sel, w_in, w_gate, w_up, w_out, w_down, name):
    ffpad = FF_SHP - FF_SH

    def body(sel_ref, i_ref, g_ref, u_ref, o_ref, d_ref, oi, og, ou, oo, od):
        del sel_ref
        oi[...] = i_ref[...].astype(oi.dtype)
        oo[...] = o_ref[...].astype(oo.dtype)
        for src, dst in ((g_ref, og), (u_ref, ou), (d_ref, od)):
            dst[:FF_SH, :] = src[...].astype(dst.dtype)
            dst[FF_SH:, :] = jnp.zeros((ffpad, D), dst.dtype)

    shapes_in = [(D, IN_SH), (FF_SH, D), (FF_SH, D), (OUT_SH, D), (FF_SH, D)]
    shapes_out = [(D, IN_SH), (FF_SHP, D), (FF_SHP, D), (OUT_SH, D), (FF_SHP, D)]
    return pl.pallas_call(
        body, name=name,
        grid_spec=pltpu.PrefetchScalarGridSpec(
            num_scalar_prefetch=1, grid=(1,),
            in_specs=[pl.BlockSpec((None,) + s, lambda i, sel_ref: (sel_ref[1], 0, 0)) for s in shapes_in],
            out_specs=[pl.BlockSpec((None,) + s, lambda i, sel_ref: (sel_ref[0], 0, 0)) for s in shapes_out]),
        out_shape=[_sds((NDEV,) + s, WIRE_DTYPE) for s in shapes_out], compiler_params=_params(("arbitrary",)),
    )(sel, w_in, w_gate, w_up, w_out, w_down)


def place_block(sel, a, name):
    rr, cc = a.shape

    def body(sel_ref, a_ref, o_ref):
        del sel_ref
        o_ref[...] = a_ref[...]

    return pl.pallas_call(
        body, name=name,
        grid_spec=pltpu.PrefetchScalarGridSpec(
            num_scalar_prefetch=1, grid=(1,),
            in_specs=[pl.BlockSpec((rr, cc), lambda i, sel_ref: (0, 0))],
            out_specs=pl.BlockSpec((None, rr, cc), lambda i, sel_ref: (sel_ref[0], 0, 0))),
        out_shape=_sds((NDEV, rr, cc), a.dtype), compiler_params=_params(("arbitrary",)),
    )(sel, a)


def ag_start(lands, after, name):
    n = len(lands)
    ns = 4 * n

    def body(*refs):
        lnd = refs[:n]
        send_sems, recv_sems = refs[n + 1:n + 1 + ns], refs[n + 1 + ns:n + 1 + 2 * ns]
        token = refs[-1]
        x, y, c = _place()
        me, sibling = (x, y, c), (x, y, 1 - c)
        chips = [(1 - x, y), (x, 1 - y), (1 - x, 1 - y)]
        copy = _ag_copy(lnd, send_sems, recv_sems, 4)
        for a in range(n):
            copy(a, 0, me, sibling).start()
            for j, chip in enumerate(chips):
                copy(a, 1 + j, me, (*chip, c)).start()
        token[...] = jnp.zeros_like(token)

    outs = pl.pallas_call(
        body, name=name,
        in_specs=[HBM] * n + [ANY],
        out_specs=[SEM] * (2 * ns) + [HBM] * n + [pl.BlockSpec(memory_space=pltpu.VMEM)],
        out_shape=_dma_sems(2 * ns) + _hbm_like(lands) + [_sds((8, 128), F32)],
        input_output_aliases={i: 2 * ns + i for i in range(n)},
        compiler_params=pltpu.CompilerParams(has_side_effects=EFFECT),
    )(*[_hbm(a) for a in lands], after)
    return outs[:ns], outs[ns:2 * ns], outs[2 * ns:2 * ns + n], outs[-1]


def ag_forward(send_sems, recv_sems, lands, after, name):
    n = len(lands)
    n1, n2 = 4 * n, 3 * n

    def body(*refs):
        lnd = refs[:n]
        o = n
        s1, r1 = refs[o:o + n1], refs[o + n1:o + 2 * n1]
        o += 2 * n1 + 1
        s2, r2 = refs[o:o + n2], refs[o + n2:o + 2 * n2]
        token = refs[-1]
        token[...] = jnp.zeros_like(token)
        x, y, c = _place()
        me, sibling = (x, y, c), (x, y, 1 - c)
        chips = [(1 - x, y), (x, 1 - y), (1 - x, 1 - y)]
        copy1 = _ag_copy(lnd, s1, r1, 4)
        copy2 = _ag_copy(lnd, s2, r2, 3)
        for j, chip in enumerate(chips):
            for a in range(n):
                copy1(a, 1 + j, (*chip, c), me).wait_recv()
                copy2(a, j, (*chip, c), sibling).start()
        for a in range(n):
            copy1(a, 0, sibling, me).wait_recv()
            copy1(a, 0, me, sibling).wait_send()
            for j, chip in enumerate(chips):
                copy1(a, 1 + j, me, (*chip, c)).wait_send()

    outs = pl.pallas_call(
        body, name=name,
        in_specs=[HBM] * n + [SEM] * (2 * n1) + [ANY],
        out_specs=[SEM] * (2 * n2) + [HBM] * n + [pl.BlockSpec(memory_space=pltpu.VMEM)],
        out_shape=_dma_sems(2 * n2) + _hbm_like(lands) + [_sds((8, 128), F32)],
        input_output_aliases={i: 2 * n2 + i for i in range(n)},
        compiler_params=pltpu.CompilerParams(has_side_effects=EFFECT),
    )(*lands, *send_sems, *recv_sems, after)
    return outs[:n2], outs[n2:2 * n2], outs[2 * n2:2 * n2 + n], outs[-1]


def ag_finish(send_sems, recv_sems, lands, after, name):
    n = len(lands)
    n2 = 3 * n

    def body(*refs):
        lnd = refs[:n]
        s2, r2 = refs[n:n + n2], refs[n + n2:n + 2 * n2]
        x, y, c = _place()
        me, sibling = (x, y, c), (x, y, 1 - c)
        chips = [(1 - x, y), (x, 1 - y), (1 - x, 1 - y)]
        copy2 = _ag_copy(lnd, s2, r2, 3)
        for a in range(n):
            for j, chip in enumerate(chips):
                copy2(a, j, (*chip, c), sibling).wait_send()
                copy2(a, j, (*chip, 1 - c), me).wait_recv()

    outs = pl.pallas_call(
        body, name=name,
        in_specs=[HBM] * n + [SEM] * (2 * n2) + [ANY],
        out_specs=[HBM] * n, out_shape=_hbm_like(lands),
        input_output_aliases={i: i for i in range(n)},
        compiler_params=pltpu.CompilerParams(has_side_effects=EFFECT),
    )(*lands, *send_sems, *recv_sems, after)
    return list(outs)


def rs_sibling_start(arrs, name):
    n = len(arrs)
    ns = 4 * n
    lands = [lax.empty((4,) + a.shape[1:], a.dtype) for a in arrs]

    def body(*refs):
        ins, lnd = refs[:n], refs[n:2 * n]
        send_sems, recv_sems = refs[2 * n:2 * n + ns], refs[2 * n + ns:2 * n + 2 * ns]
        x, y, c = _place()
        sibling = (x, y, 1 - c)
        for a in range(n):
            for p in range(4):
                pltpu.make_async_remote_copy(
                    src_ref=ins[a].at[2 * p + 1 - c], dst_ref=lnd[a].at[p], send_sem=send_sems[4 * a + p],
                    recv_sem=recv_sems[4 * a + p], device_id=sibling, device_id_type=MESH).start()
        refs[-1][...] = jnp.zeros_like(refs[-1])

    outs = pl.pallas_call(
        body, name=name,
        in_specs=[HBM] * (2 * n), out_specs=[SEM] * (2 * ns) + [HBM] * (2 * n) + [pl.BlockSpec(memory_space=pltpu.VMEM)],
        out_shape=_dma_sems(2 * ns) + _hbm_like(arrs) + _hbm_like(lands) + [_sds((8, 128), F32)],
        input_output_aliases={i: 2 * ns + i for i in range(2 * n)},
        compiler_params=pltpu.CompilerParams(has_side_effects=EFFECT),
    )(*[_hbm(a) for a in arrs], *[_hbm(a) for a in lands])
    return (outs[:ns], outs[ns:2 * ns], outs[2 * ns:2 * ns + n], outs[2 * ns + n:2 * ns + 2 * n]), outs[-1]


def rs_sibling_wait(send_sems, recv_sems, arrs, lands, after, name):
    n = len(arrs)
    ns = 4 * n

    def body(*refs):
        ins, lnd = refs[:n], refs[n:2 * n]
        s, r = refs[2 * n:2 * n + ns], refs[2 * n + ns:2 * n + 2 * ns]
        x, y, c = _place()
        sibling = (x, y, 1 - c)
        for a in range(n):
            for p in range(4):
                cp = pltpu.make_async_remote_copy(
                    src_ref=ins[a].at[2 * p + 1 - c], dst_ref=lnd[a].at[p], send_sem=s[4 * a + p], recv_sem=r[4 * a + p],
                    device_id=sibling, device_id_type=MESH)
                cp.wait_send()
                cp.wait_recv()

    outs = pl.pallas_call(
        body, name=name,
        in_specs=[HBM] * (2 * n) + [SEM] * (2 * ns) + [ANY], out_specs=[HBM] * (2 * n),
        out_shape=_hbm_like(arrs) + _hbm_like(lands),
        input_output_aliases={i: i for i in range(2 * n)},
        compiler_params=pltpu.CompilerParams(has_side_effects=EFFECT),
    )(*arrs, *lands, *send_sems, *recv_sems, after)
    return outs[:n], outs[n:]


def rs_chips_start(parts, name):
    n = len(parts)
    ns = 3 * n
    lands = [lax.empty((3,) + a.shape[1:], a.dtype) for a in parts]

    def body(*refs):
        ins, lnd = refs[:n], refs[n:2 * n]
        send_sems, recv_sems = refs[2 * n:2 * n + ns], refs[2 * n + ns:2 * n + 2 * ns]
        x, y, c = _place()
        chips = [(1 - x, y), (x, 1 - y), (1 - x, 1 - y)]
        for a in range(n):
            for k, (tx, ty) in enumerate(chips):
                pltpu.make_async_remote_copy(
                    src_ref=ins[a].at[2 * tx + ty], dst_ref=lnd[a].at[k], send_sem=send_sems[3 * a + k],
                    recv_sem=recv_sems[3 * a + k], device_id=(tx, ty, c), device_id_type=MESH).start()
        refs[-1][...] = jnp.zeros_like(refs[-1])

    outs = pl.pallas_call(
        body, name=name,
        in_specs=[HBM] * (2 * n), out_specs=[SEM] * (2 * ns) + [HBM] * (2 * n) + [pl.BlockSpec(memory_space=pltpu.VMEM)],
        out_shape=_dma_sems(2 * ns) + _hbm_like(parts) + _hbm_like(lands) + [_sds((8, 128), F32)],
        input_output_aliases={i: 2 * ns + i for i in range(2 * n)},
        compiler_params=pltpu.CompilerParams(has_side_effects=EFFECT),
    )(*[_hbm(a) for a in parts], *[_hbm(a) for a in lands])
    return (outs[:ns], outs[ns:2 * ns], outs[2 * ns:2 * ns + n], outs[2 * ns + n:2 * ns + 2 * n]), outs[-1]


def rs_chips_wait(send_sems, recv_sems, parts, lands, after, name):
    n = len(parts)
    ns = 3 * n

    def body(*refs):
        ins, lnd = refs[:n], refs[n:2 * n]
        s, r = refs[2 * n:2 * n + ns], refs[2 * n + ns:2 * n + 2 * ns]
        x, y, c = _place()
        chips = [(1 - x, y), (x, 1 - y), (1 - x, 1 - y)]
        for a in range(n):
            for k, (tx, ty) in enumerate(chips):
                cp = pltpu.make_async_remote_copy(
                    src_ref=ins[a].at[2 * tx + ty], dst_ref=lnd[a].at[k], send_sem=s[3 * a + k], recv_sem=r[3 * a + k],
                    device_id=(tx, ty, c), device_id_type=MESH)
                cp.wait_send()
                cp.wait_recv()

    outs = pl.pallas_call(
        body, name=name,
        in_specs=[HBM] * (2 * n) + [SEM] * (2 * ns) + [ANY], out_specs=[HBM] * (2 * n),
        out_shape=_hbm_like(parts) + _hbm_like(lands),
        input_output_aliases={i: i for i in range(2 * n)},
        compiler_params=pltpu.CompilerParams(has_side_effects=EFFECT),
    )(*parts, *lands, *send_sems, *recv_sems, after)
    return outs[:n], outs[n:]


def pair_sum(arrs, recv, c, name):
    n = len(arrs)

    def body(c_ref, *refs):
        del c_ref
        for a in range(n):
            refs[2 * n + a][...] = (refs[a][...].astype(F32) + refs[n + a][...].astype(F32)).astype(refs[2 * n + a].dtype)

    mine = [pl.BlockSpec((None,) + a.shape[1:], lambda p, c_ref: (2 * p + c_ref[0], 0, 0)) for a in arrs]
    other = [pl.BlockSpec((None,) + a.shape[1:], lambda p, c_ref: (p, 0, 0)) for a in arrs]
    return pl.pallas_call(
        body, name=name,
        grid_spec=pltpu.PrefetchScalarGridSpec(num_scalar_prefetch=1, grid=(4,), in_specs=mine + other, out_specs=other),
        out_shape=[_sds((4,) + a.shape[1:], a.dtype) for a in arrs], compiler_params=_params(("parallel",)),
    )(c, *arrs, *recv)


def _adamw(w, g, m, v):
    m = ADAM_B1 * m + (1.0 - ADAM_B1) * g
    v = ADAM_B2 * v + (1.0 - ADAM_B2) * jnp.square(g)
    m_hat = m / (1.0 - ADAM_B1 ** ADAM_STEP)
    v_hat = v / (1.0 - ADAM_B2 ** ADAM_STEP)
    return -ADAM_LR * (m_hat / (jnp.sqrt(v_hat) + ADAM_EPS) + ADAM_WD * w), m, v


def adamw_big(recv, sums, chip, w, m, v, tr, name):
    nl, rr, cc = w.shape
    cp = recv[0].shape[2]

    def body(chip_ref, *refs):
        del chip_ref
        rcv, own = refs[:nl], refs[nl:2 * nl]
        w_ref, m_ref, v_ref, g_out, d_out, m_out, v_out = refs[2 * nl:]
        for l in range(nl):
            g = ((own[l][...].astype(F32) + rcv[l][0].astype(F32)) + rcv[l][1].astype(F32)) + rcv[l][2].astype(F32)
            g = g[:, :cc]
            g_out[l] = g
            d_out[l], m_out[l], v_out[l] = _adamw(w_ref[l], g, m_ref[l], v_ref[l])

    blk = pl.BlockSpec((nl, tr, cc), lambda i, chip_ref: (0, i, 0))
    return pl.pallas_call(
        body, name=name,
        grid_spec=pltpu.PrefetchScalarGridSpec(
            num_scalar_prefetch=1, grid=(rr // tr,),
            in_specs=[pl.BlockSpec((3, tr, cp), lambda i, chip_ref: (0, i, 0))] * nl
            + [pl.BlockSpec((None, tr, cp), lambda i, chip_ref: (chip_ref[0], i, 0))] * nl + [blk, blk, blk],
            out_specs=[blk] * 4),
        out_shape=[_sds(w.shape, F32)] * 4, compiler_params=_params(("parallel",)),
    )(chip, *recv, *sums, w, m, v)


def sum_devices(g, name):
    _, rr, cc = g.shape

    def body(g_ref, o_ref):
        acc = g_ref[0]
        for j in range(1, NDEV):
            acc = acc + g_ref[j]
        o_ref[...] = acc

    return pl.pallas_call(
        body, name=name, grid=(1,), in_specs=[_full(g.shape)], out_specs=_full((rr, cc)), out_shape=_sds((rr, cc), F32),
        compiler_params=_params(("arbitrary",)),
    )(g)


def adamw_rows(g, w, m, v, name):
    rr, cc = w.shape

    def body(g_ref, w_ref, m_ref, v_ref, d_out, m_out, v_out):
        d_out[...], m_out[...], v_out[...] = _adamw(w_ref[...], g_ref[...], m_ref[...], v_ref[...])

    blk = _full((rr, cc))
    return pl.pallas_call(
        body, name=name, grid=(1,), in_specs=[blk] * 4, out_specs=[blk] * 3, out_shape=[_sds((rr, cc), F32)] * 3,
        compiler_params=_params(("arbitrary",)),
    )(g, w, m, v)


def _block_diag(w):
    eye = jnp.eye(LRU_BLOCKS, dtype=w.dtype)
    return (w[:, :, None, :] * eye[:, None, :, None]).reshape(D_LRU, D_LRU)


def _diag_blocks(wd):
    w4 = wd.reshape(LRU_BLOCKS, LRU_BD, LRU_BLOCKS, LRU_BD)
    return jnp.stack([w4[g, :, g, :] for g in range(LRU_BLOCKS)])


def _pack(arrs):
    flat = jnp.concatenate([a.reshape(-1) for a in arrs])
    return flat.reshape(-1, 128)


def _unpack(packed, shapes):
    flat = packed.reshape(-1)
    out, o = [], 0
    for s in shapes:
        n = int(np.prod(s))
        out.append(flat[o:o + n].reshape(s))
        o += n
    return out


REP_NAMES = ["norm_mix", "conv_b", "gate_a_w", "gate_a_b", "gate_x_w", "gate_x_b", "lru_lambda", "lru_out_norm",
             "ret_out_norm", "norm_ffn", "norm_final"]


def kernel(x, meta_tokens, norm_mix, w_in, conv_w, conv_b, gate_a_w, gate_a_b, gate_x_w, gate_x_b, lru_lambda, lru_out_norm, ret_out_norm, w_out, norm_ffn, w_gate, w_up, w_down, norm_final, loss_target, m_meta_tokens, m_norm_mix, m_w_in, m_conv_w, m_conv_b, m_gate_a_w, m_gate_a_b, m_gate_x_w, m_gate_x_b, m_lru_lambda, m_lru_out_norm, m_ret_out_norm, m_w_out, m_norm_ffn, m_w_gate, m_w_up, m_w_down, m_norm_final, v_meta_tokens, v_norm_mix, v_w_in, v_conv_w, v_conv_b, v_gate_a_w, v_gate_a_b, v_gate_x_w, v_gate_x_b, v_lru_lambda, v_lru_out_norm, v_ret_out_norm, v_w_out, v_norm_ffn, v_w_gate, v_w_up, v_w_down, v_norm_final):
    xi, yi, ci = _place()
    dev = 4 * xi + 2 * yi + ci
    c_arr = jnp.reshape(ci, (1,)).astype(jnp.int32)
    dev_arr = jnp.reshape(dev, (1,)).astype(jnp.int32)

    meta_g, conv_g = all_gather([meta_tokens, conv_w], "ag_small")
    meta_full = jnp.transpose(meta_g, (1, 0, 2)).reshape(N_META, D)
    conv_full = jnp.transpose(conv_g, (1, 2, 0, 3)).reshape(DEPTH, CONV_W, D_LRU)
    tr_ = lambda a: jnp.transpose(a, (0, 2, 1))
    w_gate_t, m_w_gate_t, v_w_gate_t = tr_(w_gate), tr_(m_w_gate), tr_(v_w_gate)
    w_up_t, m_w_up_t, v_w_up_t = tr_(w_up), tr_(m_w_up), tr_(v_w_up)
    level1 = []
    token = meta_g
    for l in range(DEPTH):
        sel = jnp.stack([dev, jnp.int32(l)]).astype(jnp.int32)
        lands = to_wire(sel, w_in, w_gate_t, w_up_t, w_out, w_down, "to_wire")
        s1, r1, lands, token = ag_start(lands, token, f"ag_start_{l}")
        level1.append((s1, r1, lands))

    def as_weights(gi, gg, gu, go, gd):
        return dict(w_in=gi, w_gate=gg.reshape(D_FFP, D), w_up=gu.reshape(D_FFP, D), w_out=go.reshape(D, D),
                    w_down=gd.reshape(D_FFP, D))

    tables = _ret_tables()
    row = lambda a: a.reshape(1, -1)

    h = jnp.concatenate([jnp.zeros((PAD, D), F32), meta_full, x[0]], axis=0)
    saved, gathered = [], []
    s1, r1, lands = level1[0]
    s2, r2, first, order = ag_forward(s1[:4], r1[:4], lands[:1], token, "ag_forward_0_w_in")
    w = dict(w_in=ag_finish(s2, r2, first, h, "ag_finish_0_w_in")[0])
    for l in range(DEPTH):
        small = dict(cw=conv_full[l], cb=row(conv_b[l]), wa=_block_diag(gate_a_w[l]).astype(MXU_DTYPE), ba=row(gate_a_b[l]),
                     wx=_block_diag(gate_x_w[l]).astype(MXU_DTYPE), bx=row(gate_x_b[l]), lam=row(lru_lambda[l]),
                     gain=row(lru_out_norm[l]))
        hn1 = rmsnorm_fwd(h, row(norm_mix[l]), "rms_fwd")
        proj = mm_blocked_nn(hn1, w["w_in"], F32, "proj")
        ylru, hst = lru_fwd(proj, name="lru_fwd", **small)
        if l == 0:
            s2, r2, rest, order = ag_forward(s1[4:], r1[4:], lands[1:], ylru, "ag_forward_0_rest")
        ymix, states = ret_fwd(proj, ylru, tables, row(ret_out_norm[l]), "ret_fwd")
        if l == 0:
            w = as_weights(w["w_in"], *ag_finish(s2, r2, rest, ymix, "ag_finish_0_rest"))
        gathered.append(w)
        h_mid = mm_nn_res(ymix, w["w_out"], h, order, "out_proj")
        hn2 = rmsnorm_fwd(h_mid, row(norm_ffn[l]), "rms_fwd")
        gate, up, act = ffn_up(hn2, w["w_gate"], w["w_up"], "ffn_up")
        if l + 1 < DEPTH:
            s1, r1, lands = level1[l + 1]
            s2, r2, lands, order = ag_forward(s1, r1, lands, act, f"ag_forward_{l + 1}")
        h_out = mm_nn_res(act, w["w_down"], h_mid, order, "ffn_down")
        if l + 1 < DEPTH:
            w_next = as_weights(*ag_finish(s2, r2, lands, h_out, f"ag_finish_{l + 1}"))
        saved.append(dict(h=h, hn1=hn1, proj=proj, hst=hst, states=states, ymix=ymix, h_mid=h_mid, hn2=hn2, gate=gate, up=up,
                          act=act, small=small))
        h = h_out
        if l + 1 < DEPTH:
            w = w_next

    loss_p, dh, dh_b, g_norm_final = loss_head(h, row(norm_final), loss_target[0], "loss_head")
    loss = lax.psum(loss_p[0, 0], ("x", "y", "c"))

    rep = [None] * DEPTH
    convw_g = [None] * DEPTH
    inflight = []
    sib = None
    order = loss_p

    def sibling_done(l, tag, names, sib, after):
        parts, got = rs_sibling_wait(*sib, after, f"rs_sibling_wait_{tag}")
        sums = pair_sum(parts, got, c_arr, "pair_sum")
        flying, started = rs_chips_start(sums, f"rs_chips_start_{tag}")
        inflight.append((l, tag, names, flying))
        return started

    all5 = ("w_in", "w_gate", "w_up", "w_out", "w_down")
    for l in reversed(range(DEPTH)):
        w, s = gathered[l], saved[l]
        dgate, dup = ffn_down_bwd(dh_b, w["w_down"], s["gate"], s["up"], order, "ffn_down_bwd")
        dwd = mm_tn(s["act"], dh_b, PAIR, order, "dw_down").reshape(NDEV, FF_SHP, D)
        dwg = mm_tn(dgate, s["hn2"], PAIR, order, "dw_rows").reshape(NDEV, FF_SHP, D)
        dwu = mm_tn(dup, s["hn2"], PAIR, order, "dw_rows").reshape(NDEV, FF_SHP, D)
        if l == 0:
            ffn_sib, order = rs_sibling_start([dwg, dwu, dwd], "rs_sibling_start_0_ffn")
        dhn2 = mm_rows_nn([(dgate, w["w_gate"]), (dup, w["w_up"])], order, "ffn_up_bwd")
        if sib is not None:
            order = sibling_done(l + 1, str(l + 1), all5, sib, dhn2)
        dh_mid, dh_mid_b, g_norm_ffn = rmsnorm_bwd(s["h_mid"], row(norm_ffn[l]), dhn2, dh, "rms_bwd")
        dymix = mm_nt(dh_mid_b, w["w_out"], F32, order, "out_proj_bwd")
        if l == 0:
            order = sibling_done(0, "0_ffn", ("w_gate", "w_up", "w_down"), ffn_sib, dymix)
        dwo = mm_tn(s["ymix"], dh_mid_b, BN, order, "dw_out").reshape(NDEV, OUT_SH, D)
        dxg, lvec, dwa, dwx = lru_bwd(s["proj"], s["hst"], dymix, name="lru_bwd", **s["small"])
        dproj, g_ret_norm = ret_bwd(s["proj"], s["states"], dymix, dxg, tables, row(ret_out_norm[l]), "ret_bwd")
        dwi = mm_tn_blocked(s["hn1"], dproj, "dw_blocked")
        dhn1 = mm_blocked_nt([(dproj, w["w_in"])], order, "proj_bwd")
        dh, dh_b, g_norm_mix = rmsnorm_bwd(s["h"], row(norm_mix[l]), dhn1, dh_mid, "rms_bwd")

        rep[l] = [g_norm_mix, lvec[4], _diag_blocks(dwa), lvec[5], _diag_blocks(dwx), lvec[6], lvec[7], lvec[8], g_ret_norm,
                  g_norm_ffn]
        convw_g[l] = lvec[0:CONV_W]
        if l > 0:
            sib, order = rs_sibling_start([dwi, dwg, dwu, dwo, dwd], f"rs_sibling_start_{l}")
        else:
            sib, order = rs_sibling_start([dwi, dwo], "rs_sibling_start_0_mix")
        if l == 1:
            early = _pack([a for ll in range(DEPTH - 1, 0, -1) for a in rep[ll] + [convw_g[ll]]])
            early_sems = ag_start([place_block(dev_arr, early, "place_grads")], order, "ag_start_grads")
            order = early_sems[3]

    grad_x = dh[X0:][None]
    g_meta = dh[PAD:X0]

    rep_shapes = [(D,), (D_LRU,), (LRU_BLOCKS, LRU_BD, LRU_BD), (LRU_BLOCKS, LRU_BD), (LRU_BLOCKS, LRU_BD, LRU_BD),
                  (LRU_BLOCKS, LRU_BD), (D_LRU,), (D_LRU,), (D_RET,), (D,)]
    late = _pack(rep[0] + [convw_g[0], g_norm_final, g_meta])
    (gath_late,) = all_gather([late], "ag_grads")
    s2, r2, lands, _ = ag_forward(early_sems[0], early_sems[1], early_sems[2], dh, "ag_forward_grads")
    (gath_early,) = ag_finish(s2, r2, lands, gath_late, "ag_finish_grads")
    sibling_done(0, "0_mix", ("w_in", "w_out"), sib, gath_late)
    layer_shapes = rep_shapes + [(CONV_W, D_LRU)]
    parts_early = _unpack(sum_devices(gath_early, "sum_devices"), layer_shapes * (DEPTH - 1))
    parts_late = _unpack(sum_devices(gath_late, "sum_devices"), layer_shapes + [(D,), (N_META, D)])
    nl = len(layer_shapes)
    by_layer = {0: parts_late[:nl]}
    for i, ll in enumerate(range(DEPTH - 1, 0, -1)):
        by_layer[ll] = parts_early[i * nl:(i + 1) * nl]
    g_rep = {n: jnp.stack([by_layer[l][i] for l in range(DEPTH)]) for i, n in enumerate(REP_NAMES[:-1])}
    g_rep["norm_final"] = parts_late[nl]
    g_convw = lax.dynamic_slice_in_dim(jnp.stack([by_layer[l][nl - 1] for l in range(DEPTH)]), dev * (D_LRU // NDEV),
                                       D_LRU // NDEV, axis=2)
    g_metatok = lax.dynamic_slice_in_dim(parts_late[nl + 1], dev * (D // NDEV), D // NDEV, axis=1)

    given = dict(norm_mix=(norm_mix, m_norm_mix, v_norm_mix), conv_b=(conv_b, m_conv_b, v_conv_b),
                 gate_a_w=(gate_a_w, m_gate_a_w, v_gate_a_w), gate_a_b=(gate_a_b, m_gate_a_b, v_gate_a_b),
                 gate_x_w=(gate_x_w, m_gate_x_w, v_gate_x_w), gate_x_b=(gate_x_b, m_gate_x_b, v_gate_x_b),
                 lru_lambda=(lru_lambda, m_lru_lambda, v_lru_lambda), lru_out_norm=(lru_out_norm, m_lru_out_norm, v_lru_out_norm),
                 ret_out_norm=(ret_out_norm, m_ret_out_norm, v_ret_out_norm), norm_ffn=(norm_ffn, m_norm_ffn, v_norm_ffn),
                 norm_final=(norm_final, m_norm_final, v_norm_final),
                 conv_w=(conv_w, m_conv_w, v_conv_w), meta_tokens=(meta_tokens, m_meta_tokens, v_meta_tokens))
    small_names = REP_NAMES + ["conv_w", "meta_tokens"]
    small_g = dict(g_rep, conv_w=g_convw, meta_tokens=g_metatok)
    small_shapes = [given[n][0].shape for n in small_names]
    packs = [_pack([small_g[n] for n in small_names])] + [_pack([given[n][k] for n in small_names]) for k in range(3)]
    upd = adamw_rows(*packs, "adamw_small")
    small_out = [dict(zip(small_names, _unpack(p, small_shapes))) for p in upd]

    arrived = {}

    def wait_for(entries, after):
        for l, tag, names, flying in entries:
            sums, recv = rs_chips_wait(*flying, after, f"rs_chips_wait_{tag}")
            for i, n in enumerate(names):
                arrived[l, n] = (recv[i], sums[i])

    chip = jnp.reshape(2 * xi + yi, (1,)).astype(jnp.int32)

    def finish(wname, w_, m_, v_, tr):
        return adamw_big([arrived[l, wname][0] for l in range(DEPTH)], [arrived[l, wname][1] for l in range(DEPTH)], chip,
                         w_, m_, v_, tr, "adamw_" + wname)

    wait_for(inflight[:-1], upd[0])
    o_gate = [tr_(o) for o in finish("w_gate", w_gate_t, m_w_gate_t, v_w_gate_t, 32)]
    o_up = [tr_(o) for o in finish("w_up", w_up_t, m_w_up_t, v_w_up_t, 32)]
    o_down = finish("w_down", w_down, m_w_down, v_w_down, 32)
    wait_for(inflight[-1:], o_down[0])
    o_in = finish("w_in", w_in, m_w_in, v_w_in, 256)
    o_out = finish("w_out", w_out, m_w_out, v_w_out, 64)

    bigs = dict(w_in=o_in, w_out=o_out, w_gate=o_gate, w_up=o_up, w_down=o_down)
    order = ["meta_tokens", "norm_mix", "w_in", "conv_w", "conv_b", "gate_a_w", "gate_a_b", "gate_x_w", "gate_x_b", "lru_lambda",
             "lru_out_norm", "ret_out_norm", "w_out", "norm_ffn", "w_gate", "w_up", "w_down", "norm_final"]
    grads = [bigs[n][0] if n in bigs else small_g[n] for n in order]
    rest = [[bigs[n][k + 1] if n in bigs else small_out[k][n] for n in order] for k in range(3)]
    return (loss, grad_x, *grads, *rest[0], *rest[1], *rest[2])
```

```python
import functools

import numpy as np
import jax
import jax.numpy as jnp
from jax import lax
from jax.experimental import pallas as pl
from jax.experimental.pallas import tpu as pltpu

F32, BF16 = jnp.float32, jnp.bfloat16
MXU_DTYPE = BF16
WIRE_DTYPE = BF16

D = 1024
SEQ = 2048
DEPTH = 4
N_META = 16
CH = 128
PAD = (-(SEQ + N_META)) % CH
T = SEQ + N_META + PAD
NCH = T // CH
X0 = PAD + N_META
D_LRU = 512
LRU_BLOCKS = 8
LRU_BD = 64
CONV_W = 4
LRU_C = 8.0
D_RET = 512
HEADS = 4
HD = 128
ROPE_BASE = 10000.0
D_IN = 3072
D_FF = 2816
NDEV = 8
IN_SH = D_IN // NDEV
FF_SH = D_FF // NDEV
FF_SHP = 384
D_FFP = NDEV * FF_SHP
OUT_SH = D // NDEV
EPS = 1e-6
TM = 544
TR = 1088
VMEM_LIMIT = 56 * 2**20
MESH = pl.DeviceIdType.MESH

ADAM_LR, ADAM_B1, ADAM_B2, ADAM_EPS, ADAM_WD, ADAM_STEP = 0.001, 0.9, 0.999, 1e-08, 0.01, 10

NN = ((1,), (0,))
NT = ((1,), (1,))
TN = ((0,), (0,))


def _dot(a, b, dims):
    return lax.dot_general(a.astype(MXU_DTYPE), b.astype(MXU_DTYPE), (dims, ((), ())), preferred_element_type=F32)


def _sds(shape, dtype):
    return jax.ShapeDtypeStruct(shape, dtype)


def _params(sem=None):
    return pltpu.CompilerParams(dimension_semantics=sem, vmem_limit_bytes=VMEM_LIMIT)


def _full(shape):
    n = len(shape)
    return pl.BlockSpec(shape, lambda *_: (0,) * n)


def rmsnorm_fwd(h, gain, name):
    def body(h_ref, g_ref, o_ref):
        x = h_ref[...]
        ms = jnp.mean(x * x, axis=-1, keepdims=True)
        o_ref[...] = (x * lax.rsqrt(ms + EPS) * g_ref[...]).astype(o_ref.dtype)

    return pl.pallas_call(
        body, name=name, grid=(T // TM,),
        in_specs=[pl.BlockSpec((TM, D), lambda i: (i, 0)), _full((1, D))],
        out_specs=pl.BlockSpec((TM, D), lambda i: (i, 0)),
        out_shape=_sds((T, D), MXU_DTYPE), compiler_params=_params(("parallel",)),
    )(h, gain)


def rmsnorm_bwd(h, gain, dhn, dres, name):
    def body(h_ref, g_ref, dhn_ref, dres_ref, dh_ref, dhb_ref, dg_ref):
        x = h_ref[...]
        rstd = lax.rsqrt(jnp.mean(x * x, axis=-1, keepdims=True) + EPS)
        xhat = x * rstd
        dy = dhn_ref[...]
        dyg = dy * g_ref[...]
        dh = dres_ref[...] + rstd * (dyg - xhat * jnp.mean(dyg * xhat, axis=-1, keepdims=True))
        dh_ref[...] = dh
        dhb_ref[...] = dh.astype(dhb_ref.dtype)

        @pl.when(pl.program_id(0) == 0)
        def _():
            dg_ref[...] = jnp.zeros_like(dg_ref)
        dg_ref[...] += jnp.sum(dy * xhat, axis=0, keepdims=True)

    row = pl.BlockSpec((TM, D), lambda i: (i, 0))
    return pl.pallas_call(
        body, name=name, grid=(T // TM,),
        in_specs=[row, _full((1, D)), row, row],
        out_specs=[row, row, _full((1, D))],
        out_shape=[_sds((T, D), F32), _sds((T, D), MXU_DTYPE), _sds((1, D), F32)], compiler_params=_params(("arbitrary",)),
    )(h, gain, dhn, dres)


def loss_head(h, gain, target, name):
    def body(h_ref, g_ref, t_ref, loss_ref, dh_ref, dhb_ref, dg_ref):
        i = pl.program_id(0)

        @pl.when(i == 0)
        def _():
            loss_ref[...] = jnp.zeros_like(loss_ref)
            dg_ref[...] = jnp.zeros_like(dg_ref)
            dh_ref[...] = jnp.zeros_like(dh_ref)
            dhb_ref[...] = jnp.zeros_like(dhb_ref)

        @pl.when(i > 0)
        def _():
            x = h_ref[...]
            g = g_ref[...]
            rstd = lax.rsqrt(jnp.mean(x * x, axis=-1, keepdims=True) + EPS)
            xhat = x * rstd
            err = xhat * g - t_ref[...]
            loss_ref[...] += 0.5 * jnp.sum(jnp.mean(err * err, axis=-1, keepdims=True), axis=0, keepdims=True)
            dy = err * (1.0 / D)
            dyg = dy * g
            dh = rstd * (dyg - xhat * jnp.mean(dyg * xhat, axis=-1, keepdims=True))
            dh_ref[...] = dh
            dhb_ref[...] = dh.astype(dhb_ref.dtype)
            dg_ref[...] += jnp.sum(dy * xhat, axis=0, keepdims=True)

    row = pl.BlockSpec((CH, D), lambda i: (i, 0))
    return pl.pallas_call(
        body, name=name, grid=(NCH,),
        in_specs=[row, _full((1, D)), pl.BlockSpec((CH, D), lambda i: (jnp.maximum(i - 1, 0), 0))],
        out_specs=[_full((8, 128)), row, row, _full((1, D))],
        out_shape=[_sds((8, 128), F32), _sds((T, D), F32), _sds((T, D), MXU_DTYPE), _sds((1, D), F32)],
        compiler_params=_params(("arbitrary",)),
    )(h, gain, target)


PAIR = 2 * IN_SH
NPAIR = NDEV // 2
BN = 256


def _pair_cols(w_ref):
    return jnp.concatenate([w_ref[0], w_ref[1]], axis=1)


W_PAIR = lambda k: pl.BlockSpec((2, k, IN_SH), lambda j: (j, 0, 0))
COLS_PAIR = pl.BlockSpec((T, PAIR), lambda j: (0, j))
ANYSPEC = pl.BlockSpec(memory_space=pl.ANY)


def mm_blocked_nn(a, w, out_dtype, name):
    k = a.shape[1]

    def body(a_ref, w_ref, o_ref):
        o_ref[...] = _dot(a_ref[...], _pair_cols(w_ref), NN).astype(o_ref.dtype)

    return pl.pallas_call(
        body, name=name, grid=(NPAIR,),
        in_specs=[_full((T, k)), W_PAIR(k)], out_specs=COLS_PAIR,
        out_shape=_sds((T, NDEV * IN_SH), out_dtype), compiler_params=_params(("parallel",)),
    )(a, w)


def mm_nn_res(a, w, res, after, name):
    k = a.shape[1]

    def body(a_ref, w_ref, r_ref, after_ref, o_ref):
        del after_ref
        o_ref[...] = r_ref[...] + _dot(a_ref[...], w_ref[...], NN)

    col = pl.BlockSpec((T, BN), lambda j: (0, j))
    return pl.pallas_call(
        body, name=name, grid=(D // BN,),
        in_specs=[_full((T, k)), pl.BlockSpec((k, BN), lambda j: (0, j)), col, ANYSPEC], out_specs=col,
        out_shape=_sds((T, D), F32), compiler_params=_params(("parallel",)),
    )(a, w, res, after)


def ffn_up(hn, wg, wu, name):
    def body(a_ref, wg_ref, wu_ref, g_ref, u_ref, act_ref):
        a = a_ref[...]
        g = _dot(a, wg_ref[...], NT)
        u = _dot(a, wu_ref[...], NT)
        g_ref[...] = g.astype(g_ref.dtype)
        u_ref[...] = u.astype(u_ref.dtype)
        act_ref[...] = (jax.nn.silu(g) * u).astype(act_ref.dtype)

    wspec = pl.BlockSpec((PAIR, D), lambda j, i: (j, 0))
    ospec = pl.BlockSpec((TR, PAIR), lambda j, i: (i, j))
    return pl.pallas_call(
        body, name=name, grid=(NPAIR, T // TR),
        in_specs=[pl.BlockSpec((TR, D), lambda j, i: (i, 0)), wspec, wspec], out_specs=[ospec] * 3,
        out_shape=[_sds((T, D_FFP), MXU_DTYPE)] * 3, compiler_params=_params(("parallel", "parallel")),
    )(hn, wg, wu)


def ffn_down_bwd(dh, wd, gate, up, after, name):
    def body(dh_ref, wd_ref, g_ref, u_ref, after_ref, dg_ref, du_ref):
        del after_ref
        dact = _dot(dh_ref[...], wd_ref[...], NT)
        g = g_ref[...].astype(F32)
        u = u_ref[...].astype(F32)
        sg = jax.nn.sigmoid(g)
        dg_ref[...] = (dact * u * (sg * (1.0 + g * (1.0 - sg)))).astype(dg_ref.dtype)
        du_ref[...] = (dact * (g * sg)).astype(du_ref.dtype)

    blk = pl.BlockSpec((TR, PAIR), lambda j, i: (i, j))
    return pl.pallas_call(
        body, name=name, grid=(NPAIR, T // TR),
        in_specs=[pl.BlockSpec((TR, D), lambda j, i: (i, 0)), pl.BlockSpec((PAIR, D), lambda j, i: (j, 0)), blk, blk, ANYSPEC],
        out_specs=[blk, blk],
        out_shape=[_sds((T, D_FFP), MXU_DTYPE)] * 2, compiler_params=_params(("parallel", "parallel")),
    )(dh, wd, gate, up, after)


def mm_nt(a, w, out_dtype, after, name):
    n = a.shape[1]

    def body(a_ref, w_ref, after_ref, o_ref):
        del after_ref
        o_ref[...] = _dot(a_ref[...], w_ref[...], NT).astype(o_ref.dtype)

    return pl.pallas_call(
        body, name=name, grid=(D // BN,),
        in_specs=[_full((T, n)), pl.BlockSpec((BN, n), lambda j: (j, 0)), ANYSPEC],
        out_specs=pl.BlockSpec((T, BN), lambda j: (0, j)),
        out_shape=_sds((T, D), out_dtype), compiler_params=_params(("parallel",)),
    )(a, w, after)


def mm_blocked_nt(pairs, after, name):
    n = len(pairs)

    def body(*refs):
        o_ref = refs[2 * n + 1]

        @pl.when(pl.program_id(1) == 0)
        def _():
            o_ref[...] = jnp.zeros_like(o_ref)
        for p in range(n):
            o_ref[...] += _dot(refs[2 * p][...], _pair_cols(refs[2 * p + 1]), NT)

    specs, args = [], []
    for a, w in pairs:
        specs += [pl.BlockSpec((TR, PAIR), lambda i, j: (i, j)), pl.BlockSpec((2, D, IN_SH), lambda i, j: (j, 0, 0))]
        args += [a, w]
    return pl.pallas_call(
        body, name=name, grid=(T // TR, NPAIR), in_specs=specs + [ANYSPEC],
        out_specs=pl.BlockSpec((TR, D), lambda i, j: (i, 0)),
        out_shape=_sds((T, D), F32), compiler_params=_params(("parallel", "arbitrary")),
    )(*args, after)


def mm_rows_nn(pairs, after, name):
    n = len(pairs)

    def body(*refs):
        o_ref = refs[2 * n + 1]

        @pl.when(pl.program_id(1) == 0)
        def _():
            o_ref[...] = jnp.zeros_like(o_ref)
        for p in range(n):
            o_ref[...] += _dot(refs[2 * p][...], refs[2 * p + 1][...], NN)

    specs, args = [], []
    for a, w in pairs:
        specs += [pl.BlockSpec((TR, PAIR), lambda i, j: (i, j)), pl.BlockSpec((PAIR, D), lambda i, j: (j, 0))]
        args += [a, w]
    return pl.pallas_call(
        body, name=name, grid=(T // TR, NPAIR), in_specs=specs + [ANYSPEC],
        out_specs=pl.BlockSpec((TR, D), lambda i, j: (i, 0)),
        out_shape=_sds((T, D), F32), compiler_params=_params(("parallel", "arbitrary")),
    )(*args, after)


def mm_tn_blocked(a, b, name):
    def body(a_ref, b_ref, o_ref):
        o = _dot(a_ref[...], b_ref[...], TN).astype(o_ref.dtype)
        o_ref[0] = o[:, :IN_SH]
        o_ref[1] = o[:, IN_SH:]

    return pl.pallas_call(
        body, name=name, grid=(NPAIR,),
        in_specs=[_full((T, D)), COLS_PAIR], out_specs=W_PAIR(D),
        out_shape=_sds((NDEV, D, IN_SH), WIRE_DTYPE), compiler_params=_params(("parallel",)),
    )(a, b)


def mm_tn(a, b, bm, after, name):
    m = a.shape[1]

    def body(a_ref, b_ref, after_ref, o_ref):
        del after_ref
        o_ref[...] = _dot(a_ref[...], b_ref[...], TN).astype(o_ref.dtype)

    return pl.pallas_call(
        body, name=name, grid=(m // bm,),
        in_specs=[pl.BlockSpec((T, bm), lambda i: (0, i)), _full((T, D)), ANYSPEC],
        out_specs=pl.BlockSpec((bm, D), lambda i: (i, 0)),
        out_shape=_sds((m, D), WIRE_DTYPE), compiler_params=_params(("parallel",)),
    )(a, b, after)


def _softplus_neg(lam):
    return jnp.maximum(-lam, 0.0) + jnp.log1p(jnp.exp(-jnp.abs(lam)))


def _lru_gates(pa, px, xc, lam):
    r = jax.nn.sigmoid(pa)
    ig = jax.nn.sigmoid(px)
    sp = _softplus_neg(lam)
    log_a = -LRU_C * r * sp
    a = jnp.exp(log_a)
    mult = jnp.sqrt(-jnp.tanh(log_a) * (a * a + 1.0))
    return a, mult * (ig * xc), (r, ig, sp, mult)


def _lru_gates_vjp(da, db, xc, lam, a, r, ig, sp, mult):
    dmult = db * (ig * xc)
    du = db * mult
    dlog_a = da * a - dmult * (a * a) / mult
    dr = dlog_a * (-LRU_C * sp)
    dlam = jnp.sum(dlog_a * (-LRU_C * r), axis=0, keepdims=True) * (-jax.nn.sigmoid(-lam))
    dpa = dr * (r * (1.0 - r))
    dpx = (du * xc) * (ig * (1.0 - ig))
    return dpa, dpx, du * ig, dlam


def _lru_out(h, g, gain):
    z = h * jax.nn.gelu(g)
    return z * lax.rsqrt(jnp.mean(z * z, axis=-1, keepdims=True) + EPS) * gain


def _conv_taps(x, xprev, row):
    taps = [x]
    for s in range(1, CONV_W):
        taps.append(jnp.where(row < s, pltpu.roll(xprev, s, 0), pltpu.roll(x, s, 0)))
    return taps


def _conv(taps, cw_ref, cb):
    xc = cb + cw_ref[CONV_W - 1:CONV_W, :] * taps[0]
    for s in range(1, CONV_W):
        xc = xc + cw_ref[CONV_W - 1 - s:CONV_W - s, :] * taps[s]
    return xc


def lru_fwd(proj, cw, cb, wa, ba, wx, bx, lam, gain, name):
    def body(x_ref, g_ref, cw_ref, cb_ref, wa_ref, ba_ref, wx_ref, bx_ref, lam_ref, gain_ref,
             y_ref, h_ref, xprev_scr, a_scr, b_scr, carry_scr):
        i = pl.program_id(0)

        @pl.when(i == 0)
        def _():
            xprev_scr[...] = jnp.zeros_like(xprev_scr)
            carry_scr[...] = jnp.zeros_like(carry_scr)

        x = x_ref[...]
        row = lax.broadcasted_iota(jnp.int32, (CH, D_LRU), 0)
        xc = _conv(_conv_taps(x, xprev_scr[...], row), cw_ref, cb_ref[...])
        pa = _dot(xc, wa_ref[...], NN) + ba_ref[...]
        px = _dot(xc, wx_ref[...], NN) + bx_ref[...]
        a, b, _ = _lru_gates(pa, px, xc, lam_ref[...])
        a_scr[...] = a
        b_scr[...] = jnp.where(i * CH + row >= PAD, b, 0.0)
        h = carry_scr[...]
        for t in range(CH):
            h = a_scr[t:t + 1, :] * h + b_scr[t:t + 1, :]
            h_ref[t:t + 1, :] = h
        carry_scr[...] = h
        xprev_scr[...] = x
        y_ref[...] = _lru_out(h_ref[...], g_ref[...], gain_ref[...]).astype(y_ref.dtype)

    vec = _full((1, D_LRU))
    mat = _full((D_LRU, D_LRU))
    return pl.pallas_call(
        body, name=name, grid=(NCH,),
        in_specs=[pl.BlockSpec((CH, D_LRU), lambda i: (i, 0)), pl.BlockSpec((CH, D_LRU), lambda i: (i, 1)),
                  _full((CONV_W, D_LRU)), vec, mat, vec, mat, vec, vec, vec],
        out_specs=[pl.BlockSpec((CH, D_LRU), lambda i: (i, 0)), pl.BlockSpec((CH, D_LRU), lambda i: (i, 0))],
        out_shape=[_sds((T, D_LRU), MXU_DTYPE), _sds((T, D_LRU), F32)],
        scratch_shapes=[pltpu.VMEM((CH, D_LRU), F32), pltpu.VMEM((CH, D_LRU), F32), pltpu.VMEM((CH, D_LRU), F32),
                        pltpu.VMEM((1, D_LRU), F32)],
        compiler_params=_params(("arbitrary",)),
    )(proj, proj, cw, cb, wa, ba, wx, bx, lam, gain)


LRU_VEC_ROWS = 16


def lru_bwd(proj, hst, dymix, cw, cb, wa, ba, wx, bx, lam, gain, name):
    last = NCH - 1

    def body(x_ref, xp_ref, g_ref, h_ref, hp_ref, dy_ref, cw_ref, cb_ref, wa_ref, ba_ref, wx_ref, bx_ref, lam_ref,
             gain_ref, dxg_ref, vec_ref, dwa_ref, dwx_ref, a_scr, dh_scr, g_scr, carry_scr, dxcn_scr):
        i = pl.program_id(0)
        ib = last - i

        @pl.when(i == 0)
        def _():
            carry_scr[...] = jnp.zeros_like(carry_scr)
            dxcn_scr[...] = jnp.zeros_like(dxcn_scr)
            vec_ref[...] = jnp.zeros_like(vec_ref)
            dwa_ref[...] = jnp.zeros_like(dwa_ref)
            dwx_ref[...] = jnp.zeros_like(dwx_ref)

        x = x_ref[...]
        row = lax.broadcasted_iota(jnp.int32, (CH, D_LRU), 0)
        valid = ib * CH + row >= PAD
        taps = _conv_taps(x, xp_ref[...], row)
        xc = _conv(taps, cw_ref, cb_ref[...])
        pa = _dot(xc, wa_ref[...], NN) + ba_ref[...]
        px = _dot(xc, wx_ref[...], NN) + bx_ref[...]
        a, _, gate_parts = _lru_gates(pa, px, xc, lam_ref[...])
        h = h_ref[...]
        _, vjp_out = jax.vjp(_lru_out, h, g_ref[...], gain_ref[...])
        dh, dg, dgain = vjp_out(dy_ref[...].astype(F32))
        a_scr[...] = a
        dh_scr[...] = dh
        c = carry_scr[...]
        for t in range(CH - 1, -1, -1):
            gt = dh_scr[t:t + 1, :] + c
            g_scr[t:t + 1, :] = gt
            c = a_scr[t:t + 1, :] * gt
        carry_scr[...] = c
        gg = g_scr[...]
        hprev = jnp.where(row < 1, pltpu.roll(hp_ref[...], 1, 0), pltpu.roll(h, 1, 0))
        da = jnp.where(valid, gg * hprev, 0.0)
        db = jnp.where(valid, gg, 0.0)
        dpa, dpx, dxc, dlam = _lru_gates_vjp(da, db, xc, lam_ref[...], a, *gate_parts)
        dxc = dxc + _dot(dpa, wa_ref[...], NT) + _dot(dpx, wx_ref[...], NT)
        dwa_ref[...] += _dot(xc, dpa, TN)
        dwx_ref[...] += _dot(xc, dpx, TN)
        for s in range(CONV_W):
            vec_ref[CONV_W - 1 - s:CONV_W - s, :] += jnp.sum(dxc * taps[s], axis=0, keepdims=True)
        vec_ref[4:5, :] += jnp.sum(dxc, axis=0, keepdims=True)
        vec_ref[5:6, :] += jnp.sum(dpa, axis=0, keepdims=True)
        vec_ref[6:7, :] += jnp.sum(dpx, axis=0, keepdims=True)
        vec_ref[7:8, :] += dlam
        vec_ref[8:9, :] += dgain
        dxn = dxcn_scr[...]
        dx = cw_ref[CONV_W - 1:CONV_W, :] * dxc
        for s in range(1, CONV_W):
            ahead = jnp.where(row >= CH - s, pltpu.roll(dxn, CH - s, 0), pltpu.roll(dxc, CH - s, 0))
            dx = dx + cw_ref[CONV_W - 1 - s:CONV_W - s, :] * ahead
        dxcn_scr[...] = dxc
        dxg_ref[:, :D_LRU] = jnp.where(valid, dx, 0.0).astype(dxg_ref.dtype)
        dxg_ref[:, D_LRU:] = dg.astype(dxg_ref.dtype)

    vec = _full((1, D_LRU))
    mat = _full((D_LRU, D_LRU))

    def blk(col, shift=0):
        return pl.BlockSpec((CH, D_LRU), lambda i: (jnp.maximum(last - i - shift, 0), col))

    return pl.pallas_call(
        body, name=name, grid=(NCH,),
        in_specs=[blk(0), blk(0, 1), blk(1), blk(0), blk(0, 1), blk(0),
                  _full((CONV_W, D_LRU)), vec, mat, vec, mat, vec, vec, vec],
        out_specs=[pl.BlockSpec((CH, 2 * D_LRU), lambda i: (last - i, 0)), _full((LRU_VEC_ROWS, D_LRU)), mat, mat],
        out_shape=[_sds((T, 2 * D_LRU), MXU_DTYPE), _sds((LRU_VEC_ROWS, D_LRU), F32),
                   _sds((D_LRU, D_LRU), F32), _sds((D_LRU, D_LRU), F32)],
        scratch_shapes=[pltpu.VMEM((CH, D_LRU), F32), pltpu.VMEM((CH, D_LRU), F32), pltpu.VMEM((CH, D_LRU), F32),
                        pltpu.VMEM((1, D_LRU), F32), pltpu.VMEM((CH, D_LRU), F32)],
        compiler_params=_params(("arbitrary",)),
    )(proj, proj, proj, hst, hst, dymix, cw, cb, wa, ba, wx, bx, lam, gain)


def _ret_tables():
    half = HD // 2
    pos = jnp.arange(T, dtype=F32) - float(PAD)
    inv = ROPE_BASE ** (-jnp.arange(half, dtype=F32) / half)
    ang = pos[:, None] * inv[None, :]
    cos = jnp.concatenate([jnp.cos(ang), jnp.cos(ang)], axis=-1)
    sin = jnp.concatenate([-jnp.sin(ang), jnp.sin(ang)], axis=-1)
    log_g = jnp.log(1.0 - 2.0 ** (-5.0 - jnp.arange(HEADS, dtype=F32)))
    idx = jnp.arange(CH, dtype=F32)
    diff = idx[:, None] - idx[None, :]
    dmask = jnp.where(diff[None] >= 0, jnp.exp(jnp.maximum(diff, 0.0)[None] * log_g[:, None, None]), 0.0)
    xi = jnp.exp((idx + 1.0)[None, :] * log_g[:, None])
    zeta = jnp.exp((CH - 1.0 - idx)[None, :] * log_g[:, None])
    xi = jnp.broadcast_to(xi[:, :, None], (HEADS, CH, HD))
    zeta = jnp.broadcast_to(zeta[:, :, None], (HEADS, CH, HD))
    return cos, sin, dmask, xi, zeta


def _chunk_decay():
    log_g = np.log(np.float32(1.0) - np.float32(2.0) ** (np.float32(-5.0) - np.arange(HEADS, dtype=np.float32)))
    return [float(v) for v in np.exp(np.float32(CH) * log_g.astype(np.float32))]


def _rope(x, cos, sin):
    return x * cos + pltpu.roll(x, HD // 2, 1) * sin


def ret_fwd(proj, ylru, tables, gain, name):
    cos, sin, dmask, xi, zeta = tables
    gch = _chunk_decay()
    scale = HD ** -0.5

    def body(q_ref, k_ref, v_ref, g_ref, cos_ref, sin_ref, dm_ref, xi_ref, zt_ref, gain_ref, ylru_ref,
             y_ref, st_ref, s_scr):
        @pl.when(pl.program_id(0) == 0)
        def _():
            s_scr[...] = jnp.zeros_like(s_scr)

        y_ref[:, :D_LRU] = ylru_ref[...]
        cs, sn = cos_ref[...], sin_ref[...]
        hs = range(HEADS)
        sl = [slice(HD * h, HD * (h + 1)) for h in hs]
        qr = [_rope(q_ref[:, sl[h]], cs, sn).astype(MXU_DTYPE) for h in hs]
        kf = [_rope(k_ref[:, sl[h]], cs, sn) * scale for h in hs]
        kr = [kf[h].astype(MXU_DTYPE) for h in hs]
        v = [v_ref[:, sl[h]].astype(MXU_DTYPE) for h in hs]
        s = [s_scr[h] for h in hs]
        for h in hs:
            st_ref[h] = s[h]
        sc = [_dot(qr[h], kr[h], NT) * dm_ref[h] for h in hs]
        cross = [_dot(qr[h], s[h], NN) * xi_ref[h] for h in hs]
        for h in hs:
            s_scr[h] = s[h] * gch[h] + _dot(kf[h] * zt_ref[h], v[h], TN)
        y = [_dot(sc[h], v[h], NN) + cross[h] for h in hs]
        yc = [y[h] - jnp.mean(y[h], axis=-1, keepdims=True) for h in hs]
        yn = [yc[h] * lax.rsqrt(jnp.mean(yc[h] * yc[h], axis=-1, keepdims=True) + EPS) for h in hs]
        for h in hs:
            so = slice(D_LRU + HD * h, D_LRU + HD * (h + 1))
            y_ref[:, so] = (jax.nn.silu(g_ref[:, sl[h]]) * (yn[h] * gain_ref[:, sl[h]])).astype(y_ref.dtype)

    def col(c):
        return pl.BlockSpec((CH, D_RET), lambda n: (n, c))

    tab = pl.BlockSpec((CH, HD), lambda n: (n, 0))
    cst = _full((HEADS, CH, HD))
    return pl.pallas_call(
        body, name=name, grid=(NCH,),
        in_specs=[col(2), col(3), col(4), col(5), tab, tab, cst, cst, cst, _full((1, D_RET)), col(0)],
        out_specs=[pl.BlockSpec((CH, D), lambda n: (n, 0)), pl.BlockSpec((None, HEADS, HD, HD), lambda n: (n, 0, 0, 0))],
        out_shape=[_sds((T, D), MXU_DTYPE), _sds((NCH, HEADS, HD, HD), F32)],
        scratch_shapes=[pltpu.VMEM((HEADS, HD, HD), F32)],
        compiler_params=_params(("arbitrary",)),
    )(proj, proj, proj, proj, cos, sin, dmask, xi, zeta, gain, ylru)


def ret_bwd(proj, states, dymix, dxg, tables, gain, name):
    cos, sin, dmask, xi, zeta = tables
    gch = _chunk_decay()
    scale = HD ** -0.5
    last = NCH - 1

    def body(q_ref, k_ref, v_ref, g_ref, st_ref, do_ref, cos_ref, sin_ref, dm_ref, xi_ref, zt_ref, gain_ref, dxg_ref,
             dp_ref, dgain_ref, ds_scr):
        @pl.when(pl.program_id(0) == 0)
        def _():
            ds_scr[...] = jnp.zeros_like(ds_scr)
            dgain_ref[...] = jnp.zeros_like(dgain_ref)

        dp_ref[:, :2 * D_LRU] = dxg_ref[...]
        cs, sn = cos_ref[...], sin_ref[...]
        hs = range(HEADS)
        sl = [slice(HD * h, HD * (h + 1)) for h in hs]

        def out(j, h):
            return slice(2 * D_LRU + j * D_RET + HD * h, 2 * D_LRU + j * D_RET + HD * (h + 1))

        b16 = lambda xs: [x.astype(MXU_DTYPE) for x in xs]
        qr = b16([_rope(q_ref[:, sl[h]], cs, sn) for h in hs])
        kf = [_rope(k_ref[:, sl[h]], cs, sn) * scale for h in hs]
        kr = b16(kf)
        kz = b16([kf[h] * zt_ref[h] for h in hs])
        v = b16([v_ref[:, sl[h]] for h in hs])
        s = b16([st_ref[h] for h in hs])
        ds = [ds_scr[h] for h in hs]
        dsb = b16(ds)
        sc = [_dot(qr[h], kr[h], NT) * dm_ref[h] for h in hs]
        scb = b16(sc)
        y = [_dot(scb[h], v[h], NN) + _dot(qr[h], s[h], NN) * xi_ref[h] for h in hs]
        yc = [y[h] - jnp.mean(y[h], axis=-1, keepdims=True) for h in hs]
        rstd = [lax.rsqrt(jnp.mean(yc[h] * yc[h], axis=-1, keepdims=True) + EPS) for h in hs]
        yn = [yc[h] * rstd[h] for h in hs]
        dy = []
        for h in hs:
            g = g_ref[:, sl[h]]
            gain = gain_ref[:, sl[h]]
            sg = jax.nn.sigmoid(g)
            silu = g * sg
            dout = do_ref[:, sl[h]].astype(F32)
            dgain_ref[:, sl[h]] += jnp.sum(dout * silu * yn[h], axis=0, keepdims=True)
            dp_ref[:, out(3, h)] = (dout * yn[h] * gain * (sg * (1.0 + g * (1.0 - sg)))).astype(dp_ref.dtype)
            dyn = dout * silu * gain
            dy.append(rstd[h] * (dyn - jnp.mean(dyn, axis=-1, keepdims=True)
                                 - yn[h] * jnp.mean(dyn * yn[h], axis=-1, keepdims=True)))
        dyb = b16(dy)
        dqs = b16([dy[h] * xi_ref[h] for h in hs])
        dp = b16([_dot(dyb[h], v[h], NT) * dm_ref[h] for h in hs])
        dv = [_dot(scb[h], dyb[h], TN) + _dot(kz[h], dsb[h], NN) for h in hs]
        dqr = [_dot(dp[h], kr[h], NN) + _dot(dqs[h], s[h], NT) for h in hs]
        dkr = [_dot(dp[h], qr[h], TN) + _dot(v[h], dsb[h], NT) * zt_ref[h] for h in hs]
        for h in hs:
            ds_scr[h] = gch[h] * ds[h] + _dot(qr[h], dqs[h], TN)
        for h in hs:
            dp_ref[:, out(0, h)] = (dqr[h] * cs + pltpu.roll(dqr[h] * sn, HD // 2, 1)).astype(dp_ref.dtype)
            dp_ref[:, out(1, h)] = ((dkr[h] * cs + pltpu.roll(dkr[h] * sn, HD // 2, 1)) * scale).astype(dp_ref.dtype)
            dp_ref[:, out(2, h)] = dv[h].astype(dp_ref.dtype)

    def col(c):
        return pl.BlockSpec((CH, D_RET), lambda n: (last - n, c))

    tab = pl.BlockSpec((CH, HD), lambda n: (last - n, 0))
    cst = _full((HEADS, CH, HD))
    return pl.pallas_call(
        body, name=name, grid=(NCH,),
        in_specs=[col(2), col(3), col(4), col(5), pl.BlockSpec((None, HEADS, HD, HD), lambda n: (last - n, 0, 0, 0)), col(1),
                  tab, tab, cst, cst, cst, _full((1, D_RET)), pl.BlockSpec((CH, 2 * D_LRU), lambda n: (last - n, 0))],
        out_specs=[pl.BlockSpec((CH, D_IN), lambda n: (last - n, 0)), _full((1, D_RET))],
        out_shape=[_sds((T, D_IN), MXU_DTYPE), _sds((1, D_RET), F32)],
        scratch_shapes=[pltpu.VMEM((HEADS, HD, HD), F32)],
        compiler_params=_params(("arbitrary",)),
    )(proj, proj, proj, proj, states, dymix, cos, sin, dmask, xi, zeta, gain, dxg)


HBM = pl.BlockSpec(memory_space=pltpu.HBM)


def _place():
    return lax.axis_index("x"), lax.axis_index("y"), lax.axis_index("c")


def all_gather(arrs, name):
    n = len(arrs)

    def body(*refs):
        ins, outs = refs[:n], refs[n:2 * n]
        send_sems, recv_sems, local_sems = refs[2 * n:]
        x, y, c = _place()
        me, sibling = (x, y, c), (x, y, 1 - c)
        chips = [(1 - x, y), (x, 1 - y), (1 - x, 1 - y)]

        def copy(a, k, block, to, src=None):
            px, py, pc = block
            dst = outs[a].at[4 * px + 2 * py + pc]
            return pltpu.make_async_remote_copy(
                src_ref=dst if src is None else src, dst_ref=dst, send_sem=send_sems.at[a, k], recv_sem=recv_sems.at[a, k],
                device_id=to, device_id_type=MESH)

        mine = [pltpu.make_async_copy(ins[a], outs[a].at[4 * x + 2 * y + c], local_sems.at[a]) for a in range(n)]
        for cp in mine:
            cp.start()
        first = []
        for a in range(n):
            first.append(copy(a, 0, me, sibling, src=ins[a]))
            first += [copy(a, 1 + j, me, (*chip, c), src=ins[a]) for j, chip in enumerate(chips)]
        for cp in first:
            cp.start()
        passed = []
        for j, chip in enumerate(chips):
            for a in range(n):
                copy(a, 1 + j, (*chip, c), me).wait_recv()
                passed.append(copy(a, 4 + j, (*chip, c), sibling))
                passed[-1].start()
        for a in range(n):
            copy(a, 0, sibling, me).wait_recv()
            for j, chip in enumerate(chips):
                copy(a, 4 + j, (*chip, 1 - c), me).wait_recv()
        for cp in first + passed:
            cp.wait_send()
        for cp in mine:
            cp.wait()

    return pl.pallas_call(
        body, name=name,
        in_specs=[HBM] * n, out_specs=[HBM] * n,
        out_shape=[_sds((NDEV,) + a.shape, a.dtype) for a in arrs],
        scratch_shapes=[pltpu.SemaphoreType.DMA((n, 7)), pltpu.SemaphoreType.DMA((n, 7)), pltpu.SemaphoreType.DMA((n,))],
    )(*arrs)


SEM = pl.BlockSpec(memory_space=pltpu.SEMAPHORE)
ANY = pl.BlockSpec(memory_space=pl.ANY)
EFFECT = pltpu.SideEffectType.DATAFLOW_SIDE_EFFECTING


def _hbm(a):
    return pltpu.with_memory_space_constraint(a, pltpu.HBM)


def _hbm_like(arrs):
    return [pltpu.HBM(a.shape, a.dtype) for a in arrs]


def _dma_sems(count):
    return [pltpu.SemaphoreType.DMA(())] * count


def _ag_copy(lands, send_sems, recv_sems, per):
    def copy(a, k, block, to, src=None):
        px, py, pc = block
        dst = lands[a].at[4 * px + 2 * py + pc]
        return pltpu.make_async_remote_copy(
            src_ref=dst if src is None else src, dst_ref=dst, send_sem=send_sems[a * per + k], recv_sem=recv_sems[a * per + k],
            device_id=to, device_id_type=MESH)
    return copy


def to_wire(sel, w_in, w_gate, w_up, w_out, w_down, name):
    ffpad = FF_SHP - FF_SH

    def body(sel_ref, i_ref, g_ref, u_ref, o_ref, d_ref, oi, og, ou, oo, od):
        del sel_ref
        oi[...] = i_ref[...].astype(oi.dtype)
        oo[...] = o_ref[...].astype(oo.dtype)
        for src, dst in ((g_ref, og), (u_ref, ou), (d_ref, od)):
            dst[:FF_SH, :] = src[...].astype(dst.dtype)
            dst[FF_SH:, :] = jnp.zeros((ffpad, D), dst.dtype)

    shapes_in = [(D, IN_SH), (FF_SH, D), (FF_SH, D), (OUT_SH, D), (FF_SH, D)]
    shapes_out = [(D, IN_SH), (FF_SHP, D), (FF_SHP, D), (OUT_SH, D), (FF_SHP, D)]
    return pl.pallas_call(
        body, name=name,
        grid_spec=pltpu.PrefetchScalarGridSpec(
            num_scalar_prefetch=1, grid=(1,),
            in_specs=[pl.BlockSpec((None,) + s, lambda i, sel_ref: (sel_ref[1], 0, 0)) for s in shapes_in],
            out_specs=[pl.BlockSpec((None,) + s, lambda i, sel_ref: (sel_ref[0], 0, 0)) for s in shapes_out]),
        out_shape=[_sds((NDEV,) + s, WIRE_DTYPE) for s in shapes_out], compiler_params=_params(("arbitrary",)),
    )(sel, w_in, w_gate, w_up, w_out, w_down)


def place_block(sel, a, name):
    rr, cc = a.shape

    def body(sel_ref, a_ref, o_ref):
        del sel_ref
        o_ref[...] = a_ref[...]

    return pl.pallas_call(
        body, name=name,
        grid_spec=pltpu.PrefetchScalarGridSpec(
            num_scalar_prefetch=1, grid=(1,),
            in_specs=[pl.BlockSpec((rr, cc), lambda i, sel_ref: (0, 0))],
            out_specs=pl.BlockSpec((None, rr, cc), lambda i, sel_ref: (sel_ref[0], 0, 0))),
        out_shape=_sds((NDEV, rr, cc), a.dtype), compiler_params=_params(("arbitrary",)),
    )(sel, a)


def ag_start(lands, after, name):
    n = len(lands)
    ns = 4 * n

    def body(*refs):
        lnd = refs[:n]
        send_sems, recv_sems = refs[n + 1:n + 1 + ns], refs[n + 1 + ns:n + 1 + 2 * ns]
        token = refs[-1]
        x, y, c = _place()
        me, sibling = (x, y, c), (x, y, 1 - c)
        chips = [(1 - x, y), (x, 1 - y), (1 - x, 1 - y)]
        copy = _ag_copy(lnd, send_sems, recv_sems, 4)
        for a in range(n):
            copy(a, 0, me, sibling).start()
            for j, chip in enumerate(chips):
                copy(a, 1 + j, me, (*chip, c)).start()
        token[...] = jnp.zeros_like(token)

    outs = pl.pallas_call(
        body, name=name,
        in_specs=[HBM] * n + [ANY],
        out_specs=[SEM] * (2 * ns) + [HBM] * n + [pl.BlockSpec(memory_space=pltpu.VMEM)],
        out_shape=_dma_sems(2 * ns) + _hbm_like(lands) + [_sds((8, 128), F32)],
        input_output_aliases={i: 2 * ns + i for i in range(n)},
        compiler_params=pltpu.CompilerParams(has_side_effects=EFFECT),
    )(*[_hbm(a) for a in lands], after)
    return outs[:ns], outs[ns:2 * ns], outs[2 * ns:2 * ns + n], outs[-1]


def ag_forward(send_sems, recv_sems, lands, after, name):
    n = len(lands)
    n1, n2 = 4 * n, 3 * n

    def body(*refs):
        lnd = refs[:n]
        o = n
        s1, r1 = refs[o:o + n1], refs[o + n1:o + 2 * n1]
        o += 2 * n1 + 1
        s2, r2 = refs[o:o + n2], refs[o + n2:o + 2 * n2]
        token = refs[-1]
        token[...] = jnp.zeros_like(token)
        x, y, c = _place()
        me, sibling = (x, y, c), (x, y, 1 - c)
        chips = [(1 - x, y), (x, 1 - y), (1 - x, 1 - y)]
        copy1 = _ag_copy(lnd, s1, r1, 4)
        copy2 = _ag_copy(lnd, s2, r2, 3)
        for j, chip in enumerate(chips):
            for a in range(n):
                copy1(a, 1 + j, (*chip, c), me).wait_recv()
                copy2(a, j, (*chip, c), sibling).start()
        for a in range(n):
            copy1(a, 0, sibling, me).wait_recv()
            copy1(a, 0, me, sibling).wait_send()
            for j, chip in enumerate(chips):
                copy1(a, 1 + j, me, (*chip, c)).wait_send()

    outs = pl.pallas_call(
        body, name=name,
        in_specs=[HBM] * n + [SEM] * (2 * n1) + [ANY],
        out_specs=[SEM] * (2 * n2) + [HBM] * n + [pl.BlockSpec(memory_space=pltpu.VMEM)],
        out_shape=_dma_sems(2 * n2) + _hbm_like(lands) + [_sds((8, 128), F32)],
        input_output_aliases={i: 2 * n2 + i for i in range(n)},
        compiler_params=pltpu.CompilerParams(has_side_effects=EFFECT),
    )(*lands, *send_sems, *recv_sems, after)
    return outs[:n2], outs[n2:2 * n2], outs[2 * n2:2 * n2 + n], outs[-1]


def ag_finish(send_sems, recv_sems, lands, after, name):
    n = len(lands)
    n2 = 3 * n

    def body(*refs):
        lnd = refs[:n]
        s2, r2 = refs[n:n + n2], refs[n + n2:n + 2 * n2]
        x, y, c = _place()
        me, sibling = (x, y, c), (x, y, 1 - c)
        chips = [(1 - x, y), (x, 1 - y), (1 - x, 1 - y)]
        copy2 = _ag_copy(lnd, s2, r2, 3)
        for a in range(n):
            for j, chip in enumerate(chips):
                copy2(a, j, (*chip, c), sibling).wait_send()
                copy2(a, j, (*chip, 1 - c), me).wait_recv()

    outs = pl.pallas_call(
        body, name=name,
        in_specs=[HBM] * n + [SEM] * (2 * n2) + [ANY],
        out_specs=[HBM] * n, out_shape=_hbm_like(lands),
        input_output_aliases={i: i for i in range(n)},
        compiler_params=pltpu.CompilerParams(has_side_effects=EFFECT),
    )(*lands, *send_sems, *recv_sems, after)
    return list(outs)


def rs_sibling_start(arrs, name):
    n = len(arrs)
    ns = 4 * n
    lands = [lax.empty((4,) + a.shape[1:], a.dtype) for a in arrs]

    def body(*refs):
        ins, lnd = refs[:n], refs[n:2 * n]
        send_sems, recv_sems = refs[2 * n:2 * n + ns], refs[2 * n + ns:2 * n + 2 * ns]
        x, y, c = _place()
        sibling = (x, y, 1 - c)
        for a in range(n):
            for p in range(4):
                pltpu.make_async_remote_copy(
                    src_ref=ins[a].at[2 * p + 1 - c], dst_ref=lnd[a].at[p], send_sem=send_sems[4 * a + p],
                    recv_sem=recv_sems[4 * a + p], device_id=sibling, device_id_type=MESH).start()
        refs[-1][...] = jnp.zeros_like(refs[-1])

    outs = pl.pallas_call(
        body, name=name,
        in_specs=[HBM] * (2 * n), out_specs=[SEM] * (2 * ns) + [HBM] * (2 * n) + [pl.BlockSpec(memory_space=pltpu.VMEM)],
        out_shape=_dma_sems(2 * ns) + _hbm_like(arrs) + _hbm_like(lands) + [_sds((8, 128), F32)],
        input_output_aliases={i: 2 * ns + i for i in range(2 * n)},
        compiler_params=pltpu.CompilerParams(has_side_effects=EFFECT),
    )(*[_hbm(a) for a in arrs], *[_hbm(a) for a in lands])
    return (outs[:ns], outs[ns:2 * ns], outs[2 * ns:2 * ns + n], outs[2 * ns + n:2 * ns + 2 * n]), outs[-1]


def rs_sibling_wait(send_sems, recv_sems, arrs, lands, after, name):
    n = len(arrs)
    ns = 4 * n

    def body(*refs):
        ins, lnd = refs[:n], refs[n:2 * n]
        s, r = refs[2 * n:2 * n + ns], refs[2 * n + ns:2 * n + 2 * ns]
        x, y, c = _place()
        sibling = (x, y, 1 - c)
        for a in range(n):
            for p in range(4):
                cp = pltpu.make_async_remote_copy(
                    src_ref=ins[a].at[2 * p + 1 - c], dst_ref=lnd[a].at[p], send_sem=s[4 * a + p], recv_sem=r[4 * a + p],
                    device_id=sibling, device_id_type=MESH)
                cp.wait_send()
                cp.wait_recv()

    outs = pl.pallas_call(
        body, name=name,
        in_specs=[HBM] * (2 * n) + [SEM] * (2 * ns) + [ANY], out_specs=[HBM] * (2 * n),
        out_shape=_hbm_like(arrs) + _hbm_like(lands),
        input_output_aliases={i: i for i in range(2 * n)},
        compiler_params=pltpu.CompilerParams(has_side_effects=EFFECT),
    )(*arrs, *lands, *send_sems, *recv_sems, after)
    return outs[:n], outs[n:]


def rs_chips_start(parts, name):
    n = len(parts)
    ns = 3 * n
    lands = [lax.empty((3,) + a.shape[1:], a.dtype) for a in parts]

    def body(*refs):
        ins, lnd = refs[:n], refs[n:2 * n]
        send_sems, recv_sems = refs[2 * n:2 * n + ns], refs[2 * n + ns:2 * n + 2 * ns]
        x, y, c = _place()
        chips = [(1 - x, y), (x, 1 - y), (1 - x, 1 - y)]
        for a in range(n):
            for k, (tx, ty) in enumerate(chips):
                pltpu.make_async_remote_copy(
                    src_ref=ins[a].at[2 * tx + ty], dst_ref=lnd[a].at[k], send_sem=send_sems[3 * a + k],
                    recv_sem=recv_sems[3 * a + k], device_id=(tx, ty, c), device_id_type=MESH).start()
        refs[-1][...] = jnp.zeros_like(refs[-1])

    outs = pl.pallas_call(
        body, name=name,
        in_specs=[HBM] * (2 * n), out_specs=[SEM] * (2 * ns) + [HBM] * (2 * n) + [pl.BlockSpec(memory_space=pltpu.VMEM)],
        out_shape=_dma_sems(2 * ns) + _hbm_like(parts) + _hbm_like(lands) + [_sds((8, 128), F32)],
        input_output_aliases={i: 2 * ns + i for i in range(2 * n)},
        compiler_params=pltpu.CompilerParams(has_side_effects=EFFECT),
    )(*[_hbm(a) for a in parts], *[_hbm(a) for a in lands])
    return (outs[:ns], outs[ns:2 * ns], outs[2 * ns:2 * ns + n], outs[2 * ns + n:2 * ns + 2 * n]), outs[-1]


def rs_chips_wait(send_sems, recv_sems, parts, lands, after, name):
    n = len(parts)
    ns = 3 * n

    def body(*refs):
        ins, lnd = refs[:n], refs[n:2 * n]
        s, r = refs[2 * n:2 * n + ns], refs[2 * n + ns:2 * n + 2 * ns]
        x, y, c = _place()
        chips = [(1 - x, y), (x, 1 - y), (1 - x, 1 - y)]
        for a in range(n):
            for k, (tx, ty) in enumerate(chips):
                cp = pltpu.make_async_remote_copy(
                    src_ref=ins[a].at[2 * tx + ty], dst_ref=lnd[a].at[k], send_sem=s[3 * a + k], recv_sem=r[3 * a + k],
                    device_id=(tx, ty, c), device_id_type=MESH)
                cp.wait_send()
                cp.wait_recv()

    outs = pl.pallas_call(
        body, name=name,
        in_specs=[HBM] * (2 * n) + [SEM] * (2 * ns) + [ANY], out_specs=[HBM] * (2 * n),
        out_shape=_hbm_like(parts) + _hbm_like(lands),
        input_output_aliases={i: i for i in range(2 * n)},
        compiler_params=pltpu.CompilerParams(has_side_effects=EFFECT),
    )(*parts, *lands, *send_sems, *recv_sems, after)
    return outs[:n], outs[n:]


def pair_sum(arrs, recv, c, name):
    n = len(arrs)

    def body(c_ref, *refs):
        del c_ref
        for a in range(n):
            refs[2 * n + a][...] = (refs[a][...].astype(F32) + refs[n + a][...].astype(F32)).astype(refs[2 * n + a].dtype)

    mine = [pl.BlockSpec((None,) + a.shape[1:], lambda p, c_ref: (2 * p + c_ref[0], 0, 0)) for a in arrs]
    other = [pl.BlockSpec((None,) + a.shape[1:], lambda p, c_ref: (p, 0, 0)) for a in arrs]
    return pl.pallas_call(
        body, name=name,
        grid_spec=pltpu.PrefetchScalarGridSpec(num_scalar_prefetch=1, grid=(4,), in_specs=mine + other, out_specs=other),
        out_shape=[_sds((4,) + a.shape[1:], a.dtype) for a in arrs], compiler_params=_params(("parallel",)),
    )(c, *arrs, *recv)


def _adamw(w, g, m, v):
    m = ADAM_B1 * m + (1.0 - ADAM_B1) * g
    v = ADAM_B2 * v + (1.0 - ADAM_B2) * jnp.square(g)
    m_hat = m / (1.0 - ADAM_B1 ** ADAM_STEP)
    v_hat = v / (1.0 - ADAM_B2 ** ADAM_STEP)
    return -ADAM_LR * (m_hat / (jnp.sqrt(v_hat) + ADAM_EPS) + ADAM_WD * w), m, v


def adamw_big(recv, sums, chip, w, m, v, tr, name):
    nl, rr, cc = w.shape
    cp = recv[0].shape[2]

    def body(chip_ref, *refs):
        del chip_ref
        rcv, own = refs[:nl], refs[nl:2 * nl]
        w_ref, m_ref, v_ref, g_out, d_out, m_out, v_out = refs[2 * nl:]
        for l in range(nl):
            g = ((own[l][...].astype(F32) + rcv[l][0].astype(F32)) + rcv[l][1].astype(F32)) + rcv[l][2].astype(F32)
            g = g[:, :cc]
            g_out[l] = g
            d_out[l], m_out[l], v_out[l] = _adamw(w_ref[l], g, m_ref[l], v_ref[l])

    blk = pl.BlockSpec((nl, tr, cc), lambda i, chip_ref: (0, i, 0))
    return pl.pallas_call(
        body, name=name,
        grid_spec=pltpu.PrefetchScalarGridSpec(
            num_scalar_prefetch=1, grid=(rr // tr,),
            in_specs=[pl.BlockSpec((3, tr, cp), lambda i, chip_ref: (0, i, 0))] * nl
            + [pl.BlockSpec((None, tr, cp), lambda i, chip_ref: (chip_ref[0], i, 0))] * nl + [blk, blk, blk],
            out_specs=[blk] * 4),
        out_shape=[_sds(w.shape, F32)] * 4, compiler_params=_params(("parallel",)),
    )(chip, *recv, *sums, w, m, v)


def sum_devices(g, name):
    _, rr, cc = g.shape

    def body(g_ref, o_ref):
        acc = g_ref[0]
        for j in range(1, NDEV):
            acc = acc + g_ref[j]
        o_ref[...] = acc

    return pl.pallas_call(
        body, name=name, grid=(1,), in_specs=[_full(g.shape)], out_specs=_full((rr, cc)), out_shape=_sds((rr, cc), F32),
        compiler_params=_params(("arbitrary",)),
    )(g)


def adamw_rows(g, w, m, v, name):
    rr, cc = w.shape

    def body(g_ref, w_ref, m_ref, v_ref, d_out, m_out, v_out):
        d_out[...], m_out[...], v_out[...] = _adamw(w_ref[...], g_ref[...], m_ref[...], v_ref[...])

    blk = _full((rr, cc))
    return pl.pallas_call(
        body, name=name, grid=(1,), in_specs=[blk] * 4, out_specs=[blk] * 3, out_shape=[_sds((rr, cc), F32)] * 3,
        compiler_params=_params(("arbitrary",)),
    )(g, w, m, v)


def _block_diag(w):
    eye = jnp.eye(LRU_BLOCKS, dtype=w.dtype)
    return (w[:, :, :, None, :] * eye[None, :, None, :, None]).reshape(w.shape[0], D_LRU, D_LRU)


def _diag_blocks(wd):
    w4 = wd.reshape(LRU_BLOCKS, LRU_BD, LRU_BLOCKS, LRU_BD)
    return jnp.transpose(jnp.diagonal(w4, axis1=0, axis2=2), (2, 0, 1))


def _pack(arrs):
    flat = jnp.concatenate([a.reshape(-1) for a in arrs])
    return flat.reshape(-1, 128)


def _unpack(packed, shapes):
    flat = packed.reshape(-1)
    out, o = [], 0
    for s in shapes:
        n = int(np.prod(s))
        out.append(flat[o:o + n].reshape(s))
        o += n
    return out


REP_NAMES = ["norm_mix", "conv_b", "gate_a_w", "gate_a_b", "gate_x_w", "gate_x_b", "lru_lambda", "lru_out_norm",
             "ret_out_norm", "norm_ffn", "norm_final"]


def kernel(x, meta_tokens, norm_mix, w_in, conv_w, conv_b, gate_a_w, gate_a_b, gate_x_w, gate_x_b, lru_lambda, lru_out_norm, ret_out_norm, w_out, norm_ffn, w_gate, w_up, w_down, norm_final, loss_target, m_meta_tokens, m_norm_mix, m_w_in, m_conv_w, m_conv_b, m_gate_a_w, m_gate_a_b, m_gate_x_w, m_gate_x_b, m_lru_lambda, m_lru_out_norm, m_ret_out_norm, m_w_out, m_norm_ffn, m_w_gate, m_w_up, m_w_down, m_norm_final, v_meta_tokens, v_norm_mix, v_w_in, v_conv_w, v_conv_b, v_gate_a_w, v_gate_a_b, v_gate_x_w, v_gate_x_b, v_lru_lambda, v_lru_out_norm, v_ret_out_norm, v_w_out, v_norm_ffn, v_w_gate, v_w_up, v_w_down, v_norm_final):
    xi, yi, ci = _place()
    dev = 4 * xi + 2 * yi + ci
    c_arr = jnp.reshape(ci, (1,)).astype(jnp.int32)
    dev_arr = jnp.reshape(dev, (1,)).astype(jnp.int32)

    meta_g, conv_g = all_gather([meta_tokens, conv_w], "ag_small")
    meta_full = jnp.transpose(meta_g, (1, 0, 2)).reshape(N_META, D)
    conv_full = jnp.transpose(conv_g, (1, 2, 0, 3)).reshape(DEPTH, CONV_W, D_LRU)
    tr_ = lambda a: jnp.transpose(a, (0, 2, 1))
    w_gate_t, m_w_gate_t, v_w_gate_t = tr_(w_gate), tr_(m_w_gate), tr_(v_w_gate)
    w_up_t, m_w_up_t, v_w_up_t = tr_(w_up), tr_(m_w_up), tr_(v_w_up)
    level1 = []
    token = meta_g
    for l in range(DEPTH):
        sel = jnp.stack([dev, jnp.int32(l)]).astype(jnp.int32)
        lands = to_wire(sel, w_in, w_gate_t, w_up_t, w_out, w_down, "to_wire")
        s1, r1, lands, token = ag_start(lands, token, f"ag_start_{l}")
        level1.append((s1, r1, lands))

    def as_weights(gi, gg, gu, go, gd):
        return dict(w_in=gi, w_gate=gg.reshape(D_FFP, D), w_up=gu.reshape(D_FFP, D), w_out=go.reshape(D, D),
                    w_down=gd.reshape(D_FFP, D))

    tables = _ret_tables()
    row = lambda a: a.reshape(1, -1)

    h = jnp.concatenate([jnp.zeros((PAD, D), F32), meta_full, x[0]], axis=0)
    saved, gathered = [], []
    s1, r1, lands = level1[0]
    s2, r2, first, order = ag_forward(s1[:4], r1[:4], lands[:1], token, "ag_forward_0_w_in")
    w = dict(w_in=ag_finish(s2, r2, first, h, "ag_finish_0_w_in")[0])
    wa_dense = _block_diag(gate_a_w).astype(MXU_DTYPE)
    wx_dense = _block_diag(gate_x_w).astype(MXU_DTYPE)
    for l in range(DEPTH):
        small = dict(cw=conv_full[l], cb=row(conv_b[l]), wa=wa_dense[l], ba=row(gate_a_b[l]),
                     wx=wx_dense[l], bx=row(gate_x_b[l]), lam=row(lru_lambda[l]),
                     gain=row(lru_out_norm[l]))
        hn1 = rmsnorm_fwd(h, row(norm_mix[l]), "rms_fwd")
        proj = mm_blocked_nn(hn1, w["w_in"], F32, "proj")
        ylru, hst = lru_fwd(proj, name="lru_fwd", **small)
        if l == 0:
            s2, r2, rest, order = ag_forward(s1[4:], r1[4:], lands[1:], ylru, "ag_forward_0_rest")
        ymix, states = ret_fwd(proj, ylru, tables, row(ret_out_norm[l]), "ret_fwd")
        if l == 0:
            w = as_weights(w["w_in"], *ag_finish(s2, r2, rest, ymix, "ag_finish_0_rest"))
        gathered.append(w)
        h_mid = mm_nn_res(ymix, w["w_out"], h, order, "out_proj")
        hn2 = rmsnorm_fwd(h_mid, row(norm_ffn[l]), "rms_fwd")
        gate, up, act = ffn_up(hn2, w["w_gate"], w["w_up"], "ffn_up")
        if l + 1 < DEPTH:
            s1, r1, lands = level1[l + 1]
            s2, r2, lands, order = ag_forward(s1, r1, lands, act, f"ag_forward_{l + 1}")
        h_out = mm_nn_res(act, w["w_down"], h_mid, order, "ffn_down")
        if l + 1 < DEPTH:
            w_next = as_weights(*ag_finish(s2, r2, lands, h_out, f"ag_finish_{l + 1}"))
        saved.append(dict(h=h, hn1=hn1, proj=proj, hst=hst, states=states, ymix=ymix, h_mid=h_mid, hn2=hn2, gate=gate, up=up,
                          act=act, small=small))
        h = h_out
        if l + 1 < DEPTH:
            w = w_next

    loss_p, dh, dh_b, g_norm_final = loss_head(h, row(norm_final), loss_target[0], "loss_head")
    loss = lax.psum(loss_p[0, 0], ("x", "y", "c"))

    rep = [None] * DEPTH
    convw_g = [None] * DEPTH
    inflight = []
    sib = None
    order = loss_p

    def sibling_done(l, tag, names, sib, after):
        parts, got = rs_sibling_wait(*sib, after, f"rs_sibling_wait_{tag}")
        sums = pair_sum(parts, got, c_arr, "pair_sum")
        flying, started = rs_chips_start(sums, f"rs_chips_start_{tag}")
        inflight.append((l, tag, names, flying))
        return started

    for l in reversed(range(DEPTH)):
        w, s = gathered[l], saved[l]
        dgate, dup = ffn_down_bwd(dh_b, w["w_down"], s["gate"], s["up"], order, "ffn_down_bwd")
        dwd = mm_tn(s["act"], dh_b, PAIR, order, "dw_down").reshape(NDEV, FF_SHP, D)
        dwg = mm_tn(dgate, s["hn2"], PAIR, order, "dw_rows").reshape(NDEV, FF_SHP, D)
        dwu = mm_tn(dup, s["hn2"], PAIR, order, "dw_rows").reshape(NDEV, FF_SHP, D)
        ffn_sib, order = rs_sibling_start([dwg, dwu, dwd], f"rs_sibling_start_{l}_ffn")
        dhn2 = mm_rows_nn([(dgate, w["w_gate"]), (dup, w["w_up"])], order, "ffn_up_bwd")
        if sib is not None:
            order = sibling_done(l + 1, f"{l + 1}_mix", ("w_in", "w_out"), sib, dhn2)
        dh_mid, dh_mid_b, g_norm_ffn = rmsnorm_bwd(s["h_mid"], row(norm_ffn[l]), dhn2, dh, "rms_bwd")
        dymix = mm_nt(dh_mid_b, w["w_out"], F32, order, "out_proj_bwd")
        order = sibling_done(l, f"{l}_ffn", ("w_gate", "w_up", "w_down"), ffn_sib, dymix)
        dwo = mm_tn(s["ymix"], dh_mid_b, BN, order, "dw_out").reshape(NDEV, OUT_SH, D)
        dxg, lvec, dwa, dwx = lru_bwd(s["proj"], s["hst"], dymix, name="lru_bwd", **s["small"])
        dproj, g_ret_norm = ret_bwd(s["proj"], s["states"], dymix, dxg, tables, row(ret_out_norm[l]), "ret_bwd")
        dwi = mm_tn_blocked(s["hn1"], dproj, "dw_blocked")
        dhn1 = mm_blocked_nt([(dproj, w["w_in"])], order, "proj_bwd")
        dh, dh_b, g_norm_mix = rmsnorm_bwd(s["h"], row(norm_mix[l]), dhn1, dh_mid, "rms_bwd")

        rep[l] = [g_norm_mix, lvec[4], _diag_blocks(dwa), lvec[5], _diag_blocks(dwx), lvec[6], lvec[7], lvec[8], g_ret_norm,
                  g_norm_ffn]
        convw_g[l] = lvec[0:CONV_W]
        sib, order = rs_sibling_start([dwi, dwo], f"rs_sibling_start_{l}_mix")
        if l == 1:
            early = _pack([a for ll in range(DEPTH - 1, 0, -1) for a in rep[ll] + [convw_g[ll]]])
            early_sems = ag_start([place_block(dev_arr, early, "place_grads")], order, "ag_start_grads")
            order = early_sems[3]

    grad_x = dh[X0:][None]
    g_meta = dh[PAD:X0]

    rep_shapes = [(D,), (D_LRU,), (LRU_BLOCKS, LRU_BD, LRU_BD), (LRU_BLOCKS, LRU_BD), (LRU_BLOCKS, LRU_BD, LRU_BD),
                  (LRU_BLOCKS, LRU_BD), (D_LRU,), (D_LRU,), (D_RET,), (D,)]
    late = _pack(rep[0] + [convw_g[0], g_norm_final, g_meta])
    (gath_late,) = all_gather([late], "ag_grads")
    s2, r2, lands, _ = ag_forward(early_sems[0], early_sems[1], early_sems[2], dh, "ag_forward_grads")
    (gath_early,) = ag_finish(s2, r2, lands, gath_late, "ag_finish_grads")
    sibling_done(0, "0_mix", ("w_in", "w_out"), sib, gath_late)
    layer_shapes = rep_shapes + [(CONV_W, D_LRU)]
    parts_early = _unpack(sum_devices(gath_early, "sum_devices"), layer_shapes * (DEPTH - 1))
    parts_late = _unpack(sum_devices(gath_late, "sum_devices"), layer_shapes + [(D,), (N_META, D)])
    nl = len(layer_shapes)
    by_layer = {0: parts_late[:nl]}
    for i, ll in enumerate(range(DEPTH - 1, 0, -1)):
        by_layer[ll] = parts_early[i * nl:(i + 1) * nl]
    g_rep = {n: jnp.stack([by_layer[l][i] for l in range(DEPTH)]) for i, n in enumerate(REP_NAMES[:-1])}
    g_rep["norm_final"] = parts_late[nl]
    g_convw = lax.dynamic_slice_in_dim(jnp.stack([by_layer[l][nl - 1] for l in range(DEPTH)]), dev * (D_LRU // NDEV),
                                       D_LRU // NDEV, axis=2)
    g_metatok = lax.dynamic_slice_in_dim(parts_late[nl + 1], dev * (D // NDEV), D // NDEV, axis=1)

    given = dict(norm_mix=(norm_mix, m_norm_mix, v_norm_mix), conv_b=(conv_b, m_conv_b, v_conv_b),
                 gate_a_w=(gate_a_w, m_gate_a_w, v_gate_a_w), gate_a_b=(gate_a_b, m_gate_a_b, v_gate_a_b),
                 gate_x_w=(gate_x_w, m_gate_x_w, v_gate_x_w), gate_x_b=(gate_x_b, m_gate_x_b, v_gate_x_b),
                 lru_lambda=(lru_lambda, m_lru_lambda, v_lru_lambda), lru_out_norm=(lru_out_norm, m_lru_out_norm, v_lru_out_norm),
                 ret_out_norm=(ret_out_norm, m_ret_out_norm, v_ret_out_norm), norm_ffn=(norm_ffn, m_norm_ffn, v_norm_ffn),
                 norm_final=(norm_final, m_norm_final, v_norm_final),
                 conv_w=(conv_w, m_conv_w, v_conv_w), meta_tokens=(meta_tokens, m_meta_tokens, v_meta_tokens))
    small_names = REP_NAMES + ["conv_w", "meta_tokens"]
    small_g = dict(g_rep, conv_w=g_convw, meta_tokens=g_metatok)
    small_shapes = [given[n][0].shape for n in small_names]
    packs = [_pack([small_g[n] for n in small_names])] + [_pack([given[n][k] for n in small_names]) for k in range(3)]
    upd = adamw_rows(*packs, "adamw_small")
    small_out = [dict(zip(small_names, _unpack(p, small_shapes))) for p in upd]

    arrived = {}

    def wait_for(entries, after):
        for l, tag, names, flying in entries:
            sums, recv = rs_chips_wait(*flying, after, f"rs_chips_wait_{tag}")
            for i, n in enumerate(names):
                arrived[l, n] = (recv[i], sums[i])

    chip = jnp.reshape(2 * xi + yi, (1,)).astype(jnp.int32)

    def finish(wname, w_, m_, v_, tr):
        return adamw_big([arrived[l, wname][0] for l in range(DEPTH)], [arrived[l, wname][1] for l in range(DEPTH)], chip,
                         w_, m_, v_, tr, "adamw_" + wname)

    wait_for(inflight[:-1], upd[0])
    o_gate = [tr_(o) for o in finish("w_gate", w_gate_t, m_w_gate_t, v_w_gate_t, 32)]
    o_up = [tr_(o) for o in finish("w_up", w_up_t, m_w_up_t, v_w_up_t, 32)]
    o_down = finish("w_down", w_down, m_w_down, v_w_down, 32)
    wait_for(inflight[-1:], o_down[0])
    o_in = finish("w_in", w_in, m_w_in, v_w_in, 256)
    o_out = finish("w_out", w_out, m_w_out, v_w_out, 64)

    bigs = dict(w_in=o_in, w_out=o_out, w_gate=o_gate, w_up=o_up, w_down=o_down)
    order = ["meta_tokens", "norm_mix", "w_in", "conv_w", "conv_b", "gate_a_w", "gate_a_b", "gate_x_w", "gate_x_b", "lru_lambda",
             "lru_out_norm", "ret_out_norm", "w_out", "norm_ffn", "w_gate", "w_up", "w_down", "norm_final"]
    grads = [bigs[n][0] if n in bigs else small_g[n] for n in order]
    rest = [[bigs[n][k + 1] if n in bigs else small_out[k][n] for n in order] for k in range(3)]
    return (loss, grad_x, *grads, *rest[0], *rest[1], *rest[2])
```

```python
import functools

import numpy as np
import jax
import jax.numpy as jnp
from jax import lax
from jax.experimental import pallas as pl
from jax.experimental.pallas import tpu as pltpu

F32, BF16 = jnp.float32, jnp.bfloat16
MXU_DTYPE = BF16
WIRE_DTYPE = BF16

D = 1024
SEQ = 2048
DEPTH = 4
N_META = 16
CH = 128
PAD = (-(SEQ + N_META)) % CH
T = SEQ + N_META + PAD
NCH = T // CH
X0 = PAD + N_META
D_LRU = 512
LRU_BLOCKS = 8
LRU_BD = 64
CONV_W = 4
LRU_C = 8.0
D_RET = 512
HEADS = 4
HD = 128
ROPE_BASE = 10000.0
D_IN = 3072
D_FF = 2816
NDEV = 8
IN_SH = D_IN // NDEV
FF_SH = D_FF // NDEV
FF_SHP = 384
D_FFP = NDEV * FF_SHP
OUT_SH = D // NDEV
EPS = 1e-6
TM = 544
TR = 1088
VMEM_LIMIT = 56 * 2**20
MESH = pl.DeviceIdType.MESH

ADAM_LR, ADAM_B1, ADAM_B2, ADAM_EPS, ADAM_WD, ADAM_STEP = 0.001, 0.9, 0.999, 1e-08, 0.01, 10

NN = ((1,), (0,))
NT = ((1,), (1,))
TN = ((0,), (0,))


def _dot(a, b, dims):
    return lax.dot_general(a.astype(MXU_DTYPE), b.astype(MXU_DTYPE), (dims, ((), ())), preferred_element_type=F32)


def _sds(shape, dtype):
    return jax.ShapeDtypeStruct(shape, dtype)


def _params(sem=None):
    return pltpu.CompilerParams(dimension_semantics=sem, vmem_limit_bytes=VMEM_LIMIT)


def _full(shape):
    n = len(shape)
    return pl.BlockSpec(shape, lambda *_: (0,) * n)


def rmsnorm_fwd(h, gain, name):
    def body(h_ref, g_ref, o_ref):
        x = h_ref[...]
        ms = jnp.mean(x * x, axis=-1, keepdims=True)
        o_ref[...] = (x * lax.rsqrt(ms + EPS) * g_ref[...]).astype(o_ref.dtype)

    return pl.pallas_call(
        body, name=name, grid=(T // TM,),
        in_specs=[pl.BlockSpec((TM, D), lambda i: (i, 0)), _full((1, D))],
        out_specs=pl.BlockSpec((TM, D), lambda i: (i, 0)),
        out_shape=_sds((T, D), MXU_DTYPE), compiler_params=_params(("parallel",)),
    )(h, gain)


def rmsnorm_bwd(h, gain, dhn, dres, name):
    def body(h_ref, g_ref, dhn_ref, dres_ref, dh_ref, dhb_ref, dg_ref):
        x = h_ref[...]
        rstd = lax.rsqrt(jnp.mean(x * x, axis=-1, keepdims=True) + EPS)
        xhat = x * rstd
        dy = dhn_ref[...]
        dyg = dy * g_ref[...]
        dh = dres_ref[...] + rstd * (dyg - xhat * jnp.mean(dyg * xhat, axis=-1, keepdims=True))
        dh_ref[...] = dh
        dhb_ref[...] = dh.astype(dhb_ref.dtype)

        @pl.when(pl.program_id(0) == 0)
        def _():
            dg_ref[...] = jnp.zeros_like(dg_ref)
        dg_ref[...] += jnp.sum(dy * xhat, axis=0, keepdims=True)

    row = pl.BlockSpec((TM, D), lambda i: (i, 0))
    return pl.pallas_call(
        body, name=name, grid=(T // TM,),
        in_specs=[row, _full((1, D)), row, row],
        out_specs=[row, row, _full((1, D))],
        out_shape=[_sds((T, D), F32), _sds((T, D), MXU_DTYPE), _sds((1, D), F32)], compiler_params=_params(("arbitrary",)),
    )(h, gain, dhn, dres)


def loss_head(h, gain, target, name):
    def body(h_ref, g_ref, t_ref, loss_ref, dh_ref, dhb_ref, dg_ref):
        i = pl.program_id(0)

        @pl.when(i == 0)
        def _():
            loss_ref[...] = jnp.zeros_like(loss_ref)
            dg_ref[...] = jnp.zeros_like(dg_ref)
            dh_ref[...] = jnp.zeros_like(dh_ref)
            dhb_ref[...] = jnp.zeros_like(dhb_ref)

        @pl.when(i > 0)
        def _():
            x = h_ref[...]
            g = g_ref[...]
            rstd = lax.rsqrt(jnp.mean(x * x, axis=-1, keepdims=True) + EPS)
            xhat = x * rstd
            err = xhat * g - t_ref[...]
            loss_ref[...] += 0.5 * jnp.sum(jnp.mean(err * err, axis=-1, keepdims=True), axis=0, keepdims=True)
            dy = err * (1.0 / D)
            dyg = dy * g
            dh = rstd * (dyg - xhat * jnp.mean(dyg * xhat, axis=-1, keepdims=True))
            dh_ref[...] = dh
            dhb_ref[...] = dh.astype(dhb_ref.dtype)
            dg_ref[...] += jnp.sum(dy * xhat, axis=0, keepdims=True)

    row = pl.BlockSpec((CH, D), lambda i: (i, 0))
    return pl.pallas_call(
        body, name=name, grid=(NCH,),
        in_specs=[row, _full((1, D)), pl.BlockSpec((CH, D), lambda i: (jnp.maximum(i - 1, 0), 0))],
        out_specs=[_full((8, 128)), row, row, _full((1, D))],
        out_shape=[_sds((8, 128), F32), _sds((T, D), F32), _sds((T, D), MXU_DTYPE), _sds((1, D), F32)],
        compiler_params=_params(("arbitrary",)),
    )(h, gain, target)


PAIR = 2 * IN_SH
NPAIR = NDEV // 2
BN = 256


def _pair_cols(w_ref):
    return jnp.concatenate([w_ref[0], w_ref[1]], axis=1)


W_PAIR = lambda k: pl.BlockSpec((2, k, IN_SH), lambda j: (j, 0, 0))
COLS_PAIR = pl.BlockSpec((T, PAIR), lambda j: (0, j))
ANYSPEC = pl.BlockSpec(memory_space=pl.ANY)


def mm_blocked_nn(a, w, out_dtype, name):
    k = a.shape[1]

    def body(a_ref, w_ref, o_ref):
        o_ref[...] = _dot(a_ref[...], _pair_cols(w_ref), NN).astype(o_ref.dtype)

    return pl.pallas_call(
        body, name=name, grid=(NPAIR,),
        in_specs=[_full((T, k)), W_PAIR(k)], out_specs=COLS_PAIR,
        out_shape=_sds((T, NDEV * IN_SH), out_dtype), compiler_params=_params(("parallel",)),
    )(a, w)


def mm_nn_res(a, w, res, after, name):
    k = a.shape[1]

    def body(a_ref, w_ref, r_ref, after_ref, o_ref):
        del after_ref
        o_ref[...] = r_ref[...] + _dot(a_ref[...], w_ref[...], NN)

    col = pl.BlockSpec((T, BN), lambda j: (0, j))
    return pl.pallas_call(
        body, name=name, grid=(D // BN,),
        in_specs=[_full((T, k)), pl.BlockSpec((k, BN), lambda j: (0, j)), col, ANYSPEC], out_specs=col,
        out_shape=_sds((T, D), F32), compiler_params=_params(("parallel",)),
    )(a, w, res, after)


def ffn_up(hn, wg, wu, name):
    def body(a_ref, wg_ref, wu_ref, g_ref, u_ref, act_ref):
        a = a_ref[...]
        g = _dot(a, wg_ref[...], NT)
        u = _dot(a, wu_ref[...], NT)
        g_ref[...] = g.astype(g_ref.dtype)
        u_ref[...] = u.astype(u_ref.dtype)
        act_ref[...] = (jax.nn.silu(g) * u).astype(act_ref.dtype)

    wspec = pl.BlockSpec((PAIR, D), lambda j, i: (j, 0))
    ospec = pl.BlockSpec((TR, PAIR), lambda j, i: (i, j))
    return pl.pallas_call(
        body, name=name, grid=(NPAIR, T // TR),
        in_specs=[pl.BlockSpec((TR, D), lambda j, i: (i, 0)), wspec, wspec], out_specs=[ospec] * 3,
        out_shape=[_sds((T, D_FFP), MXU_DTYPE)] * 3, compiler_params=_params(("parallel", "parallel")),
    )(hn, wg, wu)


def ffn_down_bwd(dh, wd, gate, up, after, name):
    def body(dh_ref, wd_ref, g_ref, u_ref, after_ref, dg_ref, du_ref):
        del after_ref
        dact = _dot(dh_ref[...], wd_ref[...], NT)
        g = g_ref[...].astype(F32)
        u = u_ref[...].astype(F32)
        sg = jax.nn.sigmoid(g)
        dg_ref[...] = (dact * u * (sg * (1.0 + g * (1.0 - sg)))).astype(dg_ref.dtype)
        du_ref[...] = (dact * (g * sg)).astype(du_ref.dtype)

    blk = pl.BlockSpec((TR, PAIR), lambda j, i: (i, j))
    return pl.pallas_call(
        body, name=name, grid=(NPAIR, T // TR),
        in_specs=[pl.BlockSpec((TR, D), lambda j, i: (i, 0)), pl.BlockSpec((PAIR, D), lambda j, i: (j, 0)), blk, blk, ANYSPEC],
        out_specs=[blk, blk],
        out_shape=[_sds((T, D_FFP), MXU_DTYPE)] * 2, compiler_params=_params(("parallel", "parallel")),
    )(dh, wd, gate, up, after)


def mm_nt(a, w, out_dtype, after, name):
    n = a.shape[1]

    def body(a_ref, w_ref, after_ref, o_ref):
        del after_ref
        o_ref[...] = _dot(a_ref[...], w_ref[...], NT).astype(o_ref.dtype)

    return pl.pallas_call(
        body, name=name, grid=(D // BN,),
        in_specs=[_full((T, n)), pl.BlockSpec((BN, n), lambda j: (j, 0)), ANYSPEC],
        out_specs=pl.BlockSpec((T, BN), lambda j: (0, j)),
        out_shape=_sds((T, D), out_dtype), compiler_params=_params(("parallel",)),
    )(a, w, after)


def mm_blocked_nt(pairs, after, name):
    n = len(pairs)

    def body(*refs):
        o_ref = refs[2 * n + 1]

        @pl.when(pl.program_id(1) == 0)
        def _():
            o_ref[...] = jnp.zeros_like(o_ref)
        for p in range(n):
            o_ref[...] += _dot(refs[2 * p][...], _pair_cols(refs[2 * p + 1]), NT)

    specs, args = [], []
    for a, w in pairs:
        specs += [pl.BlockSpec((TR, PAIR), lambda i, j: (i, j)), pl.BlockSpec((2, D, IN_SH), lambda i, j: (j, 0, 0))]
        args += [a, w]
    return pl.pallas_call(
        body, name=name, grid=(T // TR, NPAIR), in_specs=specs + [ANYSPEC],
        out_specs=pl.BlockSpec((TR, D), lambda i, j: (i, 0)),
        out_shape=_sds((T, D), F32), compiler_params=_params(("parallel", "arbitrary")),
    )(*args, after)


def mm_tn_two(a1, a2, b, bm, after, name):
    m = a1.shape[1]

    def body(a1_ref, a2_ref, b_ref, after_ref, o1_ref, o2_ref):
        del after_ref
        b = b_ref[...]
        o1_ref[...] = _dot(a1_ref[...], b, TN).astype(o1_ref.dtype)
        o2_ref[...] = _dot(a2_ref[...], b, TN).astype(o2_ref.dtype)

    blk = pl.BlockSpec((T, bm), lambda i: (0, i))
    out = pl.BlockSpec((bm, D), lambda i: (i, 0))
    return pl.pallas_call(
        body, name=name, grid=(m // bm,),
        in_specs=[blk, blk, _full((T, D)), ANYSPEC], out_specs=[out, out],
        out_shape=[_sds((m, D), WIRE_DTYPE)] * 2, compiler_params=_params(("parallel",)),
    )(a1, a2, b, after)


def out_proj_bwd(dh, w, ymix, after, name):
    def body(dh_ref, w_ref, y_ref, after_ref, dy_ref, dw_ref):
        del after_ref
        dh_ = dh_ref[...]
        dy_ref[...] = _dot(dh_, w_ref[...], NT)
        dw_ref[...] = _dot(y_ref[...], dh_, TN).astype(dw_ref.dtype)

    return pl.pallas_call(
        body, name=name, grid=(D // BN,),
        in_specs=[_full((T, D)), pl.BlockSpec((BN, D), lambda j: (j, 0)), pl.BlockSpec((T, BN), lambda j: (0, j)), ANYSPEC],
        out_specs=[pl.BlockSpec((T, BN), lambda j: (0, j)), pl.BlockSpec((BN, D), lambda j: (j, 0))],
        out_shape=[_sds((T, D), F32), _sds((D, D), WIRE_DTYPE)], compiler_params=_params(("parallel",)),
    )(dh, w, ymix, after)


def mm_rows_nn(pairs, after, name):
    n = len(pairs)

    def body(*refs):
        o_ref = refs[2 * n + 1]

        @pl.when(pl.program_id(1) == 0)
        def _():
            o_ref[...] = jnp.zeros_like(o_ref)
        for p in range(n):
            o_ref[...] += _dot(refs[2 * p][...], refs[2 * p + 1][...], NN)

    specs, args = [], []
    for a, w in pairs:
        specs += [pl.BlockSpec((TR, PAIR), lambda i, j: (i, j)), pl.BlockSpec((PAIR, D), lambda i, j: (j, 0))]
        args += [a, w]
    return pl.pallas_call(
        body, name=name, grid=(T // TR, NPAIR), in_specs=specs + [ANYSPEC],
        out_specs=pl.BlockSpec((TR, D), lambda i, j: (i, 0)),
        out_shape=_sds((T, D), F32), compiler_params=_params(("parallel", "arbitrary")),
    )(*args, after)


def mm_tn_blocked(a, b, name):
    def body(a_ref, b_ref, o_ref):
        o = _dot(a_ref[...], b_ref[...], TN).astype(o_ref.dtype)
        o_ref[0] = o[:, :IN_SH]
        o_ref[1] = o[:, IN_SH:]

    return pl.pallas_call(
        body, name=name, grid=(NPAIR,),
        in_specs=[_full((T, D)), COLS_PAIR], out_specs=W_PAIR(D),
        out_shape=_sds((NDEV, D, IN_SH), WIRE_DTYPE), compiler_params=_params(("parallel",)),
    )(a, b)


def mm_tn(a, b, bm, after, name):
    m = a.shape[1]

    def body(a_ref, b_ref, after_ref, o_ref):
        del after_ref
        o_ref[...] = _dot(a_ref[...], b_ref[...], TN).astype(o_ref.dtype)

    return pl.pallas_call(
        body, name=name, grid=(m // bm,),
        in_specs=[pl.BlockSpec((T, bm), lambda i: (0, i)), _full((T, D)), ANYSPEC],
        out_specs=pl.BlockSpec((bm, D), lambda i: (i, 0)),
        out_shape=_sds((m, D), WIRE_DTYPE), compiler_params=_params(("parallel",)),
    )(a, b, after)


def _softplus_neg(lam):
    return jnp.maximum(-lam, 0.0) + jnp.log1p(jnp.exp(-jnp.abs(lam)))


def _lru_gates(pa, px, xc, lam):
    r = jax.nn.sigmoid(pa)
    ig = jax.nn.sigmoid(px)
    sp = _softplus_neg(lam)
    log_a = -LRU_C * r * sp
    a = jnp.exp(log_a)
    mult = jnp.sqrt(-jnp.tanh(log_a) * (a * a + 1.0))
    return a, mult * (ig * xc), (r, ig, sp, mult)


def _lru_gates_vjp(da, db, xc, lam, a, r, ig, sp, mult):
    dmult = db * (ig * xc)
    du = db * mult
    dlog_a = da * a - dmult * (a * a) / mult
    dr = dlog_a * (-LRU_C * sp)
    dlam = jnp.sum(dlog_a * (-LRU_C * r), axis=0, keepdims=True) * (-jax.nn.sigmoid(-lam))
    dpa = dr * (r * (1.0 - r))
    dpx = (du * xc) * (ig * (1.0 - ig))
    return dpa, dpx, du * ig, dlam


def _lru_out(h, g, gain):
    z = h * jax.nn.gelu(g)
    return z * lax.rsqrt(jnp.mean(z * z, axis=-1, keepdims=True) + EPS) * gain


def _conv_taps(x, xprev, row):
    taps = [x]
    for s in range(1, CONV_W):
        taps.append(jnp.where(row < s, pltpu.roll(xprev, s, 0), pltpu.roll(x, s, 0)))
    return taps


def _conv(taps, cw_ref, cb):
    xc = cb + cw_ref[CONV_W - 1:CONV_W, :] * taps[0]
    for s in range(1, CONV_W):
        xc = xc + cw_ref[CONV_W - 1 - s:CONV_W - s, :] * taps[s]
    return xc


def lru_fwd(proj, cw, cb, wa, ba, wx, bx, lam, gain, name):
    def body(x_ref, g_ref, cw_ref, cb_ref, wa_ref, ba_ref, wx_ref, bx_ref, lam_ref, gain_ref,
             y_ref, h_ref, xprev_scr, a_scr, b_scr, carry_scr):
        i = pl.program_id(0)

        @pl.when(i == 0)
        def _():
            xprev_scr[...] = jnp.zeros_like(xprev_scr)
            carry_scr[...] = jnp.zeros_like(carry_scr)

        x = x_ref[...]
        row = lax.broadcasted_iota(jnp.int32, (CH, D_LRU), 0)
        xc = _conv(_conv_taps(x, xprev_scr[...], row), cw_ref, cb_ref[...])
        pa = _dot(xc, wa_ref[...], NN) + ba_ref[...]
        px = _dot(xc, wx_ref[...], NN) + bx_ref[...]
        a, b, _ = _lru_gates(pa, px, xc, lam_ref[...])
        a_scr[...] = a
        b_scr[...] = jnp.where(i * CH + row >= PAD, b, 0.0)
        h = carry_scr[...]
        for t in range(CH):
            h = a_scr[t:t + 1, :] * h + b_scr[t:t + 1, :]
            h_ref[t:t + 1, :] = h
        carry_scr[...] = h
        xprev_scr[...] = x
        y_ref[...] = _lru_out(h_ref[...], g_ref[...], gain_ref[...]).astype(y_ref.dtype)

    vec = _full((1, D_LRU))
    mat = _full((D_LRU, D_LRU))
    return pl.pallas_call(
        body, name=name, grid=(NCH,),
        in_specs=[pl.BlockSpec((CH, D_LRU), lambda i: (i, 0)), pl.BlockSpec((CH, D_LRU), lambda i: (i, 1)),
                  _full((CONV_W, D_LRU)), vec, mat, vec, mat, vec, vec, vec],
        out_specs=[pl.BlockSpec((CH, D_LRU), lambda i: (i, 0)), pl.BlockSpec((CH, D_LRU), lambda i: (i, 0))],
        out_shape=[_sds((T, D_LRU), MXU_DTYPE), _sds((T, D_LRU), F32)],
        scratch_shapes=[pltpu.VMEM((CH, D_LRU), F32), pltpu.VMEM((CH, D_LRU), F32), pltpu.VMEM((CH, D_LRU), F32),
                        pltpu.VMEM((1, D_LRU), F32)],
        compiler_params=_params(("arbitrary",)),
    )(proj, proj, cw, cb, wa, ba, wx, bx, lam, gain)


LRU_VEC_ROWS = 16


def lru_bwd(proj, hst, dymix, cw, cb, wa, ba, wx, bx, lam, gain, after, name):
    last = NCH - 1

    def body(x_ref, xp_ref, g_ref, h_ref, hp_ref, dy_ref, cw_ref, cb_ref, wa_ref, ba_ref, wx_ref, bx_ref, lam_ref,
             gain_ref, after_ref, dxg_ref, vec_ref, dwa_ref, dwx_ref, a_scr, dh_scr, g_scr, carry_scr, dxcn_scr):
        del after_ref
        i = pl.program_id(0)
        ib = last - i

        @pl.when(i == 0)
        def _():
            carry_scr[...] = jnp.zeros_like(carry_scr)
            dxcn_scr[...] = jnp.zeros_like(dxcn_scr)
            vec_ref[...] = jnp.zeros_like(vec_ref)
            dwa_ref[...] = jnp.zeros_like(dwa_ref)
            dwx_ref[...] = jnp.zeros_like(dwx_ref)

        x = x_ref[...]
        row = lax.broadcasted_iota(jnp.int32, (CH, D_LRU), 0)
        valid = ib * CH + row >= PAD
        taps = _conv_taps(x, xp_ref[...], row)
        xc = _conv(taps, cw_ref, cb_ref[...])
        pa = _dot(xc, wa_ref[...], NN) + ba_ref[...]
        px = _dot(xc, wx_ref[...], NN) + bx_ref[...]
        a, _, gate_parts = _lru_gates(pa, px, xc, lam_ref[...])
        h = h_ref[...]
        _, vjp_out = jax.vjp(_lru_out, h, g_ref[...], gain_ref[...])
        dh, dg, dgain = vjp_out(dy_ref[...].astype(F32))
        a_scr[...] = a
        dh_scr[...] = dh
        c = carry_scr[...]
        for t in range(CH - 1, -1, -1):
            gt = dh_scr[t:t + 1, :] + c
            g_scr[t:t + 1, :] = gt
            c = a_scr[t:t + 1, :] * gt
        carry_scr[...] = c
        gg = g_scr[...]
        hprev = jnp.where(row < 1, pltpu.roll(hp_ref[...], 1, 0), pltpu.roll(h, 1, 0))
        da = jnp.where(valid, gg * hprev, 0.0)
        db = jnp.where(valid, gg, 0.0)
        dpa, dpx, dxc, dlam = _lru_gates_vjp(da, db, xc, lam_ref[...], a, *gate_parts)
        dxc = dxc + _dot(dpa, wa_ref[...], NT) + _dot(dpx, wx_ref[...], NT)
        dwa_ref[...] += _dot(xc, dpa, TN)
        dwx_ref[...] += _dot(xc, dpx, TN)
        for s in range(CONV_W):
            vec_ref[CONV_W - 1 - s:CONV_W - s, :] += jnp.sum(dxc * taps[s], axis=0, keepdims=True)
        vec_ref[4:5, :] += jnp.sum(dxc, axis=0, keepdims=True)
        vec_ref[5:6, :] += jnp.sum(dpa, axis=0, keepdims=True)
        vec_ref[6:7, :] += jnp.sum(dpx, axis=0, keepdims=True)
        vec_ref[7:8, :] += dlam
        vec_ref[8:9, :] += dgain
        dxn = dxcn_scr[...]
        dx = cw_ref[CONV_W - 1:CONV_W, :] * dxc
        for s in range(1, CONV_W):
            ahead = jnp.where(row >= CH - s, pltpu.roll(dxn, CH - s, 0), pltpu.roll(dxc, CH - s, 0))
            dx = dx + cw_ref[CONV_W - 1 - s:CONV_W - s, :] * ahead
        dxcn_scr[...] = dxc
        dxg_ref[:, :D_LRU] = jnp.where(valid, dx, 0.0).astype(dxg_ref.dtype)
        dxg_ref[:, D_LRU:] = dg.astype(dxg_ref.dtype)

    vec = _full((1, D_LRU))
    mat = _full((D_LRU, D_LRU))

    def blk(col, shift=0):
        return pl.BlockSpec((CH, D_LRU), lambda i: (jnp.maximum(last - i - shift, 0), col))

    return pl.pallas_call(
        body, name=name, grid=(NCH,),
        in_specs=[blk(0), blk(0, 1), blk(1), blk(0), blk(0, 1), blk(0),
                  _full((CONV_W, D_LRU)), vec, mat, vec, mat, vec, vec, vec, pl.BlockSpec(memory_space=pl.ANY)],
        out_specs=[pl.BlockSpec((CH, 2 * D_LRU), lambda i: (last - i, 0)), _full((LRU_VEC_ROWS, D_LRU)), mat, mat],
        out_shape=[_sds((T, 2 * D_LRU), MXU_DTYPE), _sds((LRU_VEC_ROWS, D_LRU), F32),
                   _sds((D_LRU, D_LRU), F32), _sds((D_LRU, D_LRU), F32)],
        scratch_shapes=[pltpu.VMEM((CH, D_LRU), F32), pltpu.VMEM((CH, D_LRU), F32), pltpu.VMEM((CH, D_LRU), F32),
                        pltpu.VMEM((1, D_LRU), F32), pltpu.VMEM((CH, D_LRU), F32)],
        compiler_params=_params(("arbitrary",)),
    )(proj, proj, proj, hst, hst, dymix, cw, cb, wa, ba, wx, bx, lam, gain, after)


def _ret_tables():
    half = HD // 2
    pos = jnp.arange(T, dtype=F32) - float(PAD)
    inv = ROPE_BASE ** (-jnp.arange(half, dtype=F32) / half)
    ang = pos[:, None] * inv[None, :]
    cos = jnp.concatenate([jnp.cos(ang), jnp.cos(ang)], axis=-1)
    sin = jnp.concatenate([-jnp.sin(ang), jnp.sin(ang)], axis=-1)
    log_g = jnp.log(1.0 - 2.0 ** (-5.0 - jnp.arange(HEADS, dtype=F32)))
    idx = jnp.arange(CH, dtype=F32)
    diff = idx[:, None] - idx[None, :]
    dmask = jnp.where(diff[None] >= 0, jnp.exp(jnp.maximum(diff, 0.0)[None] * log_g[:, None, None]), 0.0)
    xi = jnp.exp((idx + 1.0)[None, :] * log_g[:, None])
    zeta = jnp.exp((CH - 1.0 - idx)[None, :] * log_g[:, None])
    xi = jnp.broadcast_to(xi[:, :, None], (HEADS, CH, HD))
    zeta = jnp.broadcast_to(zeta[:, :, None], (HEADS, CH, HD))
    return cos, sin, dmask, xi, zeta


def _chunk_decay():
    log_g = np.log(np.float32(1.0) - np.float32(2.0) ** (np.float32(-5.0) - np.arange(HEADS, dtype=np.float32)))
    return [float(v) for v in np.exp(np.float32(CH) * log_g.astype(np.float32))]


def _rope(x, cos, sin):
    return x * cos + pltpu.roll(x, HD // 2, 1) * sin


def ret_fwd(proj, ylru, tables, gain, name):
    cos, sin, dmask, xi, zeta = tables
    gch = _chunk_decay()
    scale = HD ** -0.5

    def body(q_ref, k_ref, v_ref, g_ref, cos_ref, sin_ref, dm_ref, xi_ref, zt_ref, gain_ref, ylru_ref,
             y_ref, st_ref, s_scr):
        @pl.when(pl.program_id(0) == 0)
        def _():
            s_scr[...] = jnp.zeros_like(s_scr)

        y_ref[:, :D_LRU] = ylru_ref[...]
        cs, sn = cos_ref[...], sin_ref[...]
        hs = range(HEADS)
        sl = [slice(HD * h, HD * (h + 1)) for h in hs]
        qr = [_rope(q_ref[:, sl[h]], cs, sn).astype(MXU_DTYPE) for h in hs]
        kf = [_rope(k_ref[:, sl[h]], cs, sn) * scale for h in hs]
        kr = [kf[h].astype(MXU_DTYPE) for h in hs]
        v = [v_ref[:, sl[h]].astype(MXU_DTYPE) for h in hs]
        s = [s_scr[h] for h in hs]
        for h in hs:
            st_ref[h] = s[h]
        sc = [_dot(qr[h], kr[h], NT) * dm_ref[h] for h in hs]
        cross = [_dot(qr[h], s[h], NN) * xi_ref[h] for h in hs]
        for h in hs:
            s_scr[h] = s[h] * gch[h] + _dot(kf[h] * zt_ref[h], v[h], TN)
        y = [_dot(sc[h], v[h], NN) + cross[h] for h in hs]
        yc = [y[h] - jnp.mean(y[h], axis=-1, keepdims=True) for h in hs]
        yn = [yc[h] * lax.rsqrt(jnp.mean(yc[h] * yc[h], axis=-1, keepdims=True) + EPS) for h in hs]
        for h in hs:
            so = slice(D_LRU + HD * h, D_LRU + HD * (h + 1))
            y_ref[:, so] = (jax.nn.silu(g_ref[:, sl[h]]) * (yn[h] * gain_ref[:, sl[h]])).astype(y_ref.dtype)

    def col(c):
        return pl.BlockSpec((CH, D_RET), lambda n: (n, c))

    tab = pl.BlockSpec((CH, HD), lambda n: (n, 0))
    cst = _full((HEADS, CH, HD))
    return pl.pallas_call(
        body, name=name, grid=(NCH,),
        in_specs=[col(2), col(3), col(4), col(5), tab, tab, cst, cst, cst, _full((1, D_RET)), col(0)],
        out_specs=[pl.BlockSpec((CH, D), lambda n: (n, 0)), pl.BlockSpec((None, HEADS, HD, HD), lambda n: (n, 0, 0, 0))],
        out_shape=[_sds((T, D), MXU_DTYPE), _sds((NCH, HEADS, HD, HD), F32)],
        scratch_shapes=[pltpu.VMEM((HEADS, HD, HD), F32)],
        compiler_params=_params(("arbitrary",)),
    )(proj, proj, proj, proj, cos, sin, dmask, xi, zeta, gain, ylru)


def ret_bwd(proj, states, dymix, dxg, tables, gain, name):
    cos, sin, dmask, xi, zeta = tables
    gch = _chunk_decay()
    scale = HD ** -0.5
    last = NCH - 1

    def body(q_ref, k_ref, v_ref, g_ref, st_ref, do_ref, cos_ref, sin_ref, dm_ref, xi_ref, zt_ref, gain_ref, dxg_ref,
             dp_ref, dgain_ref, ds_scr):
        @pl.when(pl.program_id(0) == 0)
        def _():
            ds_scr[...] = jnp.zeros_like(ds_scr)
            dgain_ref[...] = jnp.zeros_like(dgain_ref)

        dp_ref[:, :2 * D_LRU] = dxg_ref[...]
        cs, sn = cos_ref[...], sin_ref[...]
        hs = range(HEADS)
        sl = [slice(HD * h, HD * (h + 1)) for h in hs]

        def out(j, h):
            return slice(2 * D_LRU + j * D_RET + HD * h, 2 * D_LRU + j * D_RET + HD * (h + 1))

        b16 = lambda xs: [x.astype(MXU_DTYPE) for x in xs]
        qr = b16([_rope(q_ref[:, sl[h]], cs, sn) for h in hs])
        kf = [_rope(k_ref[:, sl[h]], cs, sn) * scale for h in hs]
        kr = b16(kf)
        kz = b16([kf[h] * zt_ref[h] for h in hs])
        v = b16([v_ref[:, sl[h]] for h in hs])
        s = b16([st_ref[h] for h in hs])
        ds = [ds_scr[h] for h in hs]
        dsb = b16(ds)
        sc = [_dot(qr[h], kr[h], NT) * dm_ref[h] for h in hs]
        scb = b16(sc)
        y = [_dot(scb[h], v[h], NN) + _dot(qr[h], s[h], NN) * xi_ref[h] for h in hs]
        yc = [y[h] - jnp.mean(y[h], axis=-1, keepdims=True) for h in hs]
        rstd = [lax.rsqrt(jnp.mean(yc[h] * yc[h], axis=-1, keepdims=True) + EPS) for h in hs]
        yn = [yc[h] * rstd[h] for h in hs]
        dy = []
        for h in hs:
            g = g_ref[:, sl[h]]
            gain = gain_ref[:, sl[h]]
            sg = jax.nn.sigmoid(g)
            silu = g * sg
            dout = do_ref[:, sl[h]].astype(F32)
            dgain_ref[:, sl[h]] += jnp.sum(dout * silu * yn[h], axis=0, keepdims=True)
            dp_ref[:, out(3, h)] = (dout * yn[h] * gain * (sg * (1.0 + g * (1.0 - sg)))).astype(dp_ref.dtype)
            dyn = dout * silu * gain
            dy.append(rstd[h] * (dyn - jnp.mean(dyn, axis=-1, keepdims=True)
                                 - yn[h] * jnp.mean(dyn * yn[h], axis=-1, keepdims=True)))
        dyb = b16(dy)
        dqs = b16([dy[h] * xi_ref[h] for h in hs])
        dp = b16([_dot(dyb[h], v[h], NT) * dm_ref[h] for h in hs])
        dv = [_dot(scb[h], dyb[h], TN) + _dot(kz[h], dsb[h], NN) for h in hs]
        dqr = [_dot(dp[h], kr[h], NN) + _dot(dqs[h], s[h], NT) for h in hs]
        dkr = [_dot(dp[h], qr[h], TN) + _dot(v[h], dsb[h], NT) * zt_ref[h] for h in hs]
        for h in hs:
            ds_scr[h] = gch[h] * ds[h] + _dot(qr[h], dqs[h], TN)
        for h in hs:
            dp_ref[:, out(0, h)] = (dqr[h] * cs + pltpu.roll(dqr[h] * sn, HD // 2, 1)).astype(dp_ref.dtype)
            dp_ref[:, out(1, h)] = ((dkr[h] * cs + pltpu.roll(dkr[h] * sn, HD // 2, 1)) * scale).astype(dp_ref.dtype)
            dp_ref[:, out(2, h)] = dv[h].astype(dp_ref.dtype)

    def col(c):
        return pl.BlockSpec((CH, D_RET), lambda n: (last - n, c))

    tab = pl.BlockSpec((CH, HD), lambda n: (last - n, 0))
    cst = _full((HEADS, CH, HD))
    return pl.pallas_call(
        body, name=name, grid=(NCH,),
        in_specs=[col(2), col(3), col(4), col(5), pl.BlockSpec((None, HEADS, HD, HD), lambda n: (last - n, 0, 0, 0)), col(1),
                  tab, tab, cst, cst, cst, _full((1, D_RET)), pl.BlockSpec((CH, 2 * D_LRU), lambda n: (last - n, 0))],
        out_specs=[pl.BlockSpec((CH, D_IN), lambda n: (last - n, 0)), _full((1, D_RET))],
        out_shape=[_sds((T, D_IN), MXU_DTYPE), _sds((1, D_RET), F32)],
        scratch_shapes=[pltpu.VMEM((HEADS, HD, HD), F32)],
        compiler_params=_params(("arbitrary",)),
    )(proj, proj, proj, proj, states, dymix, cos, sin, dmask, xi, zeta, gain, dxg)


HBM = pl.BlockSpec(memory_space=pltpu.HBM)


def _place():
    return lax.axis_index("x"), lax.axis_index("y"), lax.axis_index("c")


def all_gather(arrs, name):
    n = len(arrs)

    def body(*refs):
        ins, outs = refs[:n], refs[n:2 * n]
        send_sems, recv_sems, local_sems = refs[2 * n:]
        x, y, c = _place()
        me, sibling = (x, y, c), (x, y, 1 - c)
        chips = [(1 - x, y), (x, 1 - y), (1 - x, 1 - y)]

        def copy(a, k, block, to, src=None):
            px, py, pc = block
            dst = outs[a].at[4 * px + 2 * py + pc]
            return pltpu.make_async_remote_copy(
                src_ref=dst if src is None else src, dst_ref=dst, send_sem=send_sems.at[a, k], recv_sem=recv_sems.at[a, k],
                device_id=to, device_id_type=MESH)

        mine = [pltpu.make_async_copy(ins[a], outs[a].at[4 * x + 2 * y + c], local_sems.at[a]) for a in range(n)]
        for cp in mine:
            cp.start()
        first = []
        for a in range(n):
            first.append(copy(a, 0, me, sibling, src=ins[a]))
            first += [copy(a, 1 + j, me, (*chip, c), src=ins[a]) for j, chip in enumerate(chips)]
        for cp in first:
            cp.start()
        passed = []
        for j, chip in enumerate(chips):
            for a in range(n):
                copy(a, 1 + j, (*chip, c), me).wait_recv()
                passed.append(copy(a, 4 + j, (*chip, c), sibling))
                passed[-1].start()
        for a in range(n):
            copy(a, 0, sibling, me).wait_recv()
            for j, chip in enumerate(chips):
                copy(a, 4 + j, (*chip, 1 - c), me).wait_recv()
        for cp in first + passed:
            cp.wait_send()
        for cp in mine:
            cp.wait()

    return pl.pallas_call(
        body, name=name,
        in_specs=[HBM] * n, out_specs=[HBM] * n,
        out_shape=[_sds((NDEV,) + a.shape, a.dtype) for a in arrs],
        scratch_shapes=[pltpu.SemaphoreType.DMA((n, 7)), pltpu.SemaphoreType.DMA((n, 7)), pltpu.SemaphoreType.DMA((n,))],
    )(*arrs)


SEM = pl.BlockSpec(memory_space=pltpu.SEMAPHORE)
ANY = pl.BlockSpec(memory_space=pl.ANY)
EFFECT = pltpu.SideEffectType.DATAFLOW_SIDE_EFFECTING


def _hbm(a):
    return pltpu.with_memory_space_constraint(a, pltpu.HBM)


def _hbm_like(arrs):
    return [pltpu.HBM(a.shape, a.dtype) for a in arrs]


def _dma_sems(count):
    return [pltpu.SemaphoreType.DMA(())] * count


def _ag_copy(lands, send_sems, recv_sems, per):
    def copy(a, k, block, to, src=None):
        px, py, pc = block
        dst = lands[a].at[4 * px + 2 * py + pc]
        return pltpu.make_async_remote_copy(
            src_ref=dst if src is None else src, dst_ref=dst, send_sem=send_sems[a * per + k], recv_sem=recv_sems[a * per + k],
            device_id=to, device_id_type=MESH)
    return copy


def to_wire(sel, w_in, w_gate, w_up, w_out, w_down, name):
    ffpad = FF_SHP - FF_SH

    def body(sel_ref, i_ref, g_ref, u_ref, o_ref, d_ref, oi, og, ou, oo, od):
        del sel_ref
        oi[...] = i_ref[...].astype(oi.dtype)
        oo[...] = o_ref[...].astype(oo.dtype)
        for src, dst in ((g_ref, og), (u_ref, ou), (d_ref, od)):
            dst[:FF_SH, :] = src[...].astype(dst.dtype)
            dst[FF_SH:, :] = jnp.zeros((ffpad, D), dst.dtype)

    shapes_in = [(D, IN_SH), (FF_SH, D), (FF_SH, D), (OUT_SH, D), (FF_SH, D)]
    shapes_out = [(D, IN_SH), (FF_SHP, D), (FF_SHP, D), (OUT_SH, D), (FF_SHP, D)]
    return pl.pallas_call(
        body, name=name,
        grid_spec=pltpu.PrefetchScalarGridSpec(
            num_scalar_prefetch=1, grid=(1,),
            in_specs=[pl.BlockSpec((None,) + s, lambda i, sel_ref: (sel_ref[1], 0, 0)) for s in shapes_in],
            out_specs=[pl.BlockSpec((None,) + s, lambda i, sel_ref: (sel_ref[0], 0, 0)) for s in shapes_out]),
        out_shape=[_sds((NDEV,) + s, WIRE_DTYPE) for s in shapes_out], compiler_params=_params(("arbitrary",)),
    )(sel, w_in, w_gate, w_up, w_out, w_down)


def place_block(sel, a, name):
    rr, cc = a.shape

    def body(sel_ref, a_ref, o_ref):
        del sel_ref
        o_ref[...] = a_ref[...]

    return pl.pallas_call(
        body, name=name,
        grid_spec=pltpu.PrefetchScalarGridSpec(
            num_scalar_prefetch=1, grid=(1,),
            in_specs=[pl.BlockSpec((rr, cc), lambda i, sel_ref: (0, 0))],
            out_specs=pl.BlockSpec((None, rr, cc), lambda i, sel_ref: (sel_ref[0], 0, 0))),
        out_shape=_sds((NDEV, rr, cc), a.dtype), compiler_params=_params(("arbitrary",)),
    )(sel, a)


def ag_start(lands, after, name):
    n = len(lands)
    ns = 4 * n

    def body(*refs):
        lnd = refs[:n]
        send_sems, recv_sems = refs[n + 1:n + 1 + ns], refs[n + 1 + ns:n + 1 + 2 * ns]
        token = refs[-1]
        x, y, c = _place()
        me, sibling = (x, y, c), (x, y, 1 - c)
        chips = [(1 - x, y), (x, 1 - y), (1 - x, 1 - y)]
        copy = _ag_copy(lnd, send_sems, recv_sems, 4)
        for a in range(n):
            copy(a, 0, me, sibling).start()
            for j, chip in enumerate(chips):
                copy(a, 1 + j, me, (*chip, c)).start()
        token[...] = jnp.zeros_like(token)

    outs = pl.pallas_call(
        body, name=name,
        in_specs=[HBM] * n + [ANY],
        out_specs=[SEM] * (2 * ns) + [HBM] * n + [pl.BlockSpec(memory_space=pltpu.VMEM)],
        out_shape=_dma_sems(2 * ns) + _hbm_like(lands) + [_sds((8, 128), F32)],
        input_output_aliases={i: 2 * ns + i for i in range(n)},
        compiler_params=pltpu.CompilerParams(has_side_effects=EFFECT),
    )(*[_hbm(a) for a in lands], after)
    return outs[:ns], outs[ns:2 * ns], outs[2 * ns:2 * ns + n], outs[-1]


def ag_forward(send_sems, recv_sems, lands, after, name):
    n = len(lands)
    n1, n2 = 4 * n, 3 * n

    def body(*refs):
        lnd = refs[:n]
        o = n
        s1, r1 = refs[o:o + n1], refs[o + n1:o + 2 * n1]
        o += 2 * n1 + 1
        s2, r2 = refs[o:o + n2], refs[o + n2:o + 2 * n2]
        token = refs[-1]
        token[...] = jnp.zeros_like(token)
        x, y, c = _place()
        me, sibling = (x, y, c), (x, y, 1 - c)
        chips = [(1 - x, y), (x, 1 - y), (1 - x, 1 - y)]
        copy1 = _ag_copy(lnd, s1, r1, 4)
        copy2 = _ag_copy(lnd, s2, r2, 3)
        for j, chip in enumerate(chips):
            for a in range(n):
                copy1(a, 1 + j, (*chip, c), me).wait_recv()
                copy2(a, j, (*chip, c), sibling).start()
        for a in range(n):
            copy1(a, 0, sibling, me).wait_recv()
            copy1(a, 0, me, sibling).wait_send()
            for j, chip in enumerate(chips):
                copy1(a, 1 + j, me, (*chip, c)).wait_send()

    outs = pl.pallas_call(
        body, name=name,
        in_specs=[HBM] * n + [SEM] * (2 * n1) + [ANY],
        out_specs=[SEM] * (2 * n2) + [HBM] * n + [pl.BlockSpec(memory_space=pltpu.VMEM)],
        out_shape=_dma_sems(2 * n2) + _hbm_like(lands) + [_sds((8, 128), F32)],
        input_output_aliases={i: 2 * n2 + i for i in range(n)},
        compiler_params=pltpu.CompilerParams(has_side_effects=EFFECT),
    )(*lands, *send_sems, *recv_sems, after)
    return outs[:n2], outs[n2:2 * n2], outs[2 * n2:2 * n2 + n], outs[-1]


def ag_finish(send_sems, recv_sems, lands, after, name):
    n = len(lands)
    n2 = 3 * n

    def body(*refs):
        lnd = refs[:n]
        s2, r2 = refs[n:n + n2], refs[n + n2:n + 2 * n2]
        x, y, c = _place()
        me, sibling = (x, y, c), (x, y, 1 - c)
        chips = [(1 - x, y), (x, 1 - y), (1 - x, 1 - y)]
        copy2 = _ag_copy(lnd, s2, r2, 3)
        for a in range(n):
            for j, chip in enumerate(chips):
                copy2(a, j, (*chip, c), sibling).wait_send()
                copy2(a, j, (*chip, 1 - c), me).wait_recv()

    outs = pl.pallas_call(
        body, name=name,
        in_specs=[HBM] * n + [SEM] * (2 * n2) + [ANY],
        out_specs=[HBM] * n, out_shape=_hbm_like(lands),
        input_output_aliases={i: i for i in range(n)},
        compiler_params=pltpu.CompilerParams(has_side_effects=EFFECT),
    )(*lands, *send_sems, *recv_sems, after)
    return list(outs)


def rs_sibling_start(arrs, name):
    n = len(arrs)
    ns = 4 * n
    lands = [lax.empty((4,) + a.shape[1:], a.dtype) for a in arrs]

    def body(*refs):
        ins, lnd = refs[:n], refs[n:2 * n]
        send_sems, recv_sems = refs[2 * n:2 * n + ns], refs[2 * n + ns:2 * n + 2 * ns]
        x, y, c = _place()
        sibling = (x, y, 1 - c)
        for a in range(n):
            for p in range(4):
                pltpu.make_async_remote_copy(
                    src_ref=ins[a].at[2 * p + 1 - c], dst_ref=lnd[a].at[p], send_sem=send_sems[4 * a + p],
                    recv_sem=recv_sems[4 * a + p], device_id=sibling, device_id_type=MESH).start()
        refs[-1][...] = jnp.zeros_like(refs[-1])

    outs = pl.pallas_call(
        body, name=name,
        in_specs=[HBM] * (2 * n), out_specs=[SEM] * (2 * ns) + [HBM] * (2 * n) + [pl.BlockSpec(memory_space=pltpu.VMEM)],
        out_shape=_dma_sems(2 * ns) + _hbm_like(arrs) + _hbm_like(lands) + [_sds((8, 128), F32)],
        input_output_aliases={i: 2 * ns + i for i in range(2 * n)},
        compiler_params=pltpu.CompilerParams(has_side_effects=EFFECT),
    )(*[_hbm(a) for a in arrs], *[_hbm(a) for a in lands])
    return (outs[:ns], outs[ns:2 * ns], outs[2 * ns:2 * ns + n], outs[2 * ns + n:2 * ns + 2 * n]), outs[-1]


def rs_sibling_wait(send_sems, recv_sems, arrs, lands, after, name):
    n = len(arrs)
    ns = 4 * n

    def body(*refs):
        ins, lnd = refs[:n], refs[n:2 * n]
        s, r = refs[2 * n:2 * n + ns], refs[2 * n + ns:2 * n + 2 * ns]
        x, y, c = _place()
        sibling = (x, y, 1 - c)
        for a in range(n):
            for p in range(4):
                cp = pltpu.make_async_remote_copy(
                    src_ref=ins[a].at[2 * p + 1 - c], dst_ref=lnd[a].at[p], send_sem=s[4 * a + p], recv_sem=r[4 * a + p],
                    device_id=sibling, device_id_type=MESH)
                cp.wait_send()
                cp.wait_recv()

    outs = pl.pallas_call(
        body, name=name,
        in_specs=[HBM] * (2 * n) + [SEM] * (2 * ns) + [ANY], out_specs=[HBM] * (2 * n),
        out_shape=_hbm_like(arrs) + _hbm_like(lands),
        input_output_aliases={i: i for i in range(2 * n)},
        compiler_params=pltpu.CompilerParams(has_side_effects=EFFECT),
    )(*arrs, *lands, *send_sems, *recv_sems, after)
    return outs[:n], outs[n:]


def rs_chips_start(parts, name):
    n = len(parts)
    ns = 3 * n
    lands = [lax.empty((3,) + a.shape[1:], a.dtype) for a in parts]

    def body(*refs):
        ins, lnd = refs[:n], refs[n:2 * n]
        send_sems, recv_sems = refs[2 * n:2 * n + ns], refs[2 * n + ns:2 * n + 2 * ns]
        x, y, c = _place()
        chips = [(1 - x, y), (x, 1 - y), (1 - x, 1 - y)]
        for a in range(n):
            for k, (tx, ty) in enumerate(chips):
                pltpu.make_async_remote_copy(
                    src_ref=ins[a].at[2 * tx + ty], dst_ref=lnd[a].at[k], send_sem=send_sems[3 * a + k],
                    recv_sem=recv_sems[3 * a + k], device_id=(tx, ty, c), device_id_type=MESH).start()
        refs[-1][...] = jnp.zeros_like(refs[-1])

    outs = pl.pallas_call(
        body, name=name,
        in_specs=[HBM] * (2 * n), out_specs=[SEM] * (2 * ns) + [HBM] * (2 * n) + [pl.BlockSpec(memory_space=pltpu.VMEM)],
        out_shape=_dma_sems(2 * ns) + _hbm_like(parts) + _hbm_like(lands) + [_sds((8, 128), F32)],
        input_output_aliases={i: 2 * ns + i for i in range(2 * n)},
        compiler_params=pltpu.CompilerParams(has_side_effects=EFFECT),
    )(*[_hbm(a) for a in parts], *[_hbm(a) for a in lands])
    return (outs[:ns], outs[ns:2 * ns], outs[2 * ns:2 * ns + n], outs[2 * ns + n:2 * ns + 2 * n]), outs[-1]


def rs_chips_wait(send_sems, recv_sems, parts, lands, after, name):
    n = len(parts)
    ns = 3 * n

    def body(*refs):
        ins, lnd = refs[:n], refs[n:2 * n]
        s, r = refs[2 * n:2 * n + ns], refs[2 * n + ns:2 * n + 2 * ns]
        x, y, c = _place()
        chips = [(1 - x, y), (x, 1 - y), (1 - x, 1 - y)]
        for a in range(n):
            for k, (tx, ty) in enumerate(chips):
                cp = pltpu.make_async_remote_copy(
                    src_ref=ins[a].at[2 * tx + ty], dst_ref=lnd[a].at[k], send_sem=s[3 * a + k], recv_sem=r[3 * a + k],
                    device_id=(tx, ty, c), device_id_type=MESH)
                cp.wait_send()
                cp.wait_recv()

    outs = pl.pallas_call(
        body, name=name,
        in_specs=[HBM] * (2 * n) + [SEM] * (2 * ns) + [ANY], out_specs=[HBM] * (2 * n),
        out_shape=_hbm_like(parts) + _hbm_like(lands),
        input_output_aliases={i: i for i in range(2 * n)},
        compiler_params=pltpu.CompilerParams(has_side_effects=EFFECT),
    )(*parts, *lands, *send_sems, *recv_sems, after)
    return outs[:n], outs[n:]


def pair_sum(arrs, recv, c, name):
    n = len(arrs)

    def body(c_ref, *refs):
        del c_ref
        for a in range(n):
            refs[2 * n + a][...] = (refs[a][...].astype(F32) + refs[n + a][...].astype(F32)).astype(refs[2 * n + a].dtype)

    mine = [pl.BlockSpec((None,) + a.shape[1:], lambda p, c_ref: (2 * p + c_ref[0], 0, 0)) for a in arrs]
    other = [pl.BlockSpec((None,) + a.shape[1:], lambda p, c_ref: (p, 0, 0)) for a in arrs]
    return pl.pallas_call(
        body, name=name,
        grid_spec=pltpu.PrefetchScalarGridSpec(num_scalar_prefetch=1, grid=(4,), in_specs=mine + other, out_specs=other),
        out_shape=[_sds((4,) + a.shape[1:], a.dtype) for a in arrs], compiler_params=_params(("parallel",)),
    )(c, *arrs, *recv)


def _adamw(w, g, m, v):
    m = ADAM_B1 * m + (1.0 - ADAM_B1) * g
    v = ADAM_B2 * v + (1.0 - ADAM_B2) * jnp.square(g)
    m_hat = m / (1.0 - ADAM_B1 ** ADAM_STEP)
    v_hat = v / (1.0 - ADAM_B2 ** ADAM_STEP)
    return -ADAM_LR * (m_hat / (jnp.sqrt(v_hat) + ADAM_EPS) + ADAM_WD * w), m, v


def adamw_big(recv, sums, chip, w, m, v, tr, name):
    nl, rr, cc = w.shape
    cp = recv[0].shape[2]

    def body(chip_ref, *refs):
        del chip_ref
        rcv, own = refs[:nl], refs[nl:2 * nl]
        w_ref, m_ref, v_ref, g_out, d_out, m_out, v_out = refs[2 * nl:]
        for l in range(nl):
            g = ((own[l][...].astype(F32) + rcv[l][0].astype(F32)) + rcv[l][1].astype(F32)) + rcv[l][2].astype(F32)
            g = g[:, :cc]
            g_out[l] = g
            d_out[l], m_out[l], v_out[l] = _adamw(w_ref[l], g, m_ref[l], v_ref[l])

    blk = pl.BlockSpec((nl, tr, cc), lambda i, chip_ref: (0, i, 0))
    return pl.pallas_call(
        body, name=name,
        grid_spec=pltpu.PrefetchScalarGridSpec(
            num_scalar_prefetch=1, grid=(rr // tr,),
            in_specs=[pl.BlockSpec((3, tr, cp), lambda i, chip_ref: (0, i, 0))] * nl
            + [pl.BlockSpec((None, tr, cp), lambda i, chip_ref: (chip_ref[0], i, 0))] * nl + [blk, blk, blk],
            out_specs=[blk] * 4),
        out_shape=[_sds(w.shape, F32)] * 4, compiler_params=_params(("parallel",)),
    )(chip, *recv, *sums, w, m, v)


def sum_devices(g, name):
    _, rr, cc = g.shape

    def body(g_ref, o_ref):
        acc = g_ref[0]
        for j in range(1, NDEV):
            acc = acc + g_ref[j]
        o_ref[...] = acc

    return pl.pallas_call(
        body, name=name, grid=(1,), in_specs=[_full(g.shape)], out_specs=_full((rr, cc)), out_shape=_sds((rr, cc), F32),
        compiler_params=_params(("arbitrary",)),
    )(g)


def adamw_rows(g, w, m, v, name):
    rr, cc = w.shape

    def body(g_ref, w_ref, m_ref, v_ref, d_out, m_out, v_out):
        d_out[...], m_out[...], v_out[...] = _adamw(w_ref[...], g_ref[...], m_ref[...], v_ref[...])

    blk = _full((rr, cc))
    return pl.pallas_call(
        body, name=name, grid=(1,), in_specs=[blk] * 4, out_specs=[blk] * 3, out_shape=[_sds((rr, cc), F32)] * 3,
        compiler_params=_params(("arbitrary",)),
    )(g, w, m, v)


def _block_diag(w):
    eye = jnp.eye(LRU_BLOCKS, dtype=w.dtype)
    return (w[:, :, :, None, :] * eye[None, :, None, :, None]).reshape(w.shape[0], D_LRU, D_LRU)


def _diag_blocks(wd):
    w4 = wd.reshape(LRU_BLOCKS, LRU_BD, LRU_BLOCKS, LRU_BD)
    return jnp.transpose(jnp.diagonal(w4, axis1=0, axis2=2), (2, 0, 1))


def _pack(arrs):
    flat = jnp.concatenate([a.reshape(-1) for a in arrs])
    return flat.reshape(-1, 128)


def _unpack(packed, shapes):
    flat = packed.reshape(-1)
    out, o = [], 0
    for s in shapes:
        n = int(np.prod(s))
        out.append(flat[o:o + n].reshape(s))
        o += n
    return out


REP_NAMES = ["norm_mix", "conv_b", "gate_a_w", "gate_a_b", "gate_x_w", "gate_x_b", "lru_lambda", "lru_out_norm",
             "ret_out_norm", "norm_ffn", "norm_final"]


def kernel(x, meta_tokens, norm_mix, w_in, conv_w, conv_b, gate_a_w, gate_a_b, gate_x_w, gate_x_b, lru_lambda, lru_out_norm, ret_out_norm, w_out, norm_ffn, w_gate, w_up, w_down, norm_final, loss_target, m_meta_tokens, m_norm_mix, m_w_in, m_conv_w, m_conv_b, m_gate_a_w, m_gate_a_b, m_gate_x_w, m_gate_x_b, m_lru_lambda, m_lru_out_norm, m_ret_out_norm, m_w_out, m_norm_ffn, m_w_gate, m_w_up, m_w_down, m_norm_final, v_meta_tokens, v_norm_mix, v_w_in, v_conv_w, v_conv_b, v_gate_a_w, v_gate_a_b, v_gate_x_w, v_gate_x_b, v_lru_lambda, v_lru_out_norm, v_ret_out_norm, v_w_out, v_norm_ffn, v_w_gate, v_w_up, v_w_down, v_norm_final):
    xi, yi, ci = _place()
    dev = 4 * xi + 2 * yi + ci
    c_arr = jnp.reshape(ci, (1,)).astype(jnp.int32)
    dev_arr = jnp.reshape(dev, (1,)).astype(jnp.int32)

    meta_g, conv_g = all_gather([meta_tokens, conv_w], "ag_small")
    meta_full = jnp.transpose(meta_g, (1, 0, 2)).reshape(N_META, D)
    conv_full = jnp.transpose(conv_g, (1, 2, 0, 3)).reshape(DEPTH, CONV_W, D_LRU)
    tr_ = lambda a: jnp.transpose(a, (0, 2, 1))
    w_gate_t, m_w_gate_t, v_w_gate_t = tr_(w_gate), tr_(m_w_gate), tr_(v_w_gate)
    w_up_t, m_w_up_t, v_w_up_t = tr_(w_up), tr_(m_w_up), tr_(v_w_up)
    level1 = []
    token = meta_g
    for l in range(DEPTH):
        sel = jnp.stack([dev, jnp.int32(l)]).astype(jnp.int32)
        lands = to_wire(sel, w_in, w_gate_t, w_up_t, w_out, w_down, "to_wire")
        s1, r1, lands, token = ag_start(lands, token, f"ag_start_{l}")
        level1.append((s1, r1, lands))

    def as_weights(gi, gg, gu, go, gd):
        return dict(w_in=gi, w_gate=gg.reshape(D_FFP, D), w_up=gu.reshape(D_FFP, D), w_out=go.reshape(D, D),
                    w_down=gd.reshape(D_FFP, D))

    tables = _ret_tables()
    row = lambda a: a.reshape(1, -1)

    h = jnp.concatenate([jnp.zeros((PAD, D), F32), meta_full, x[0]], axis=0)
    saved, gathered = [], []
    s1, r1, lands = level1[0]
    s2, r2, first, order = ag_forward(s1[:4], r1[:4], lands[:1], token, "ag_forward_0_w_in")
    w_in_next = ag_finish(s2, r2, first, h, "ag_finish_0_w_in")[0]
    wa_dense = _block_diag(gate_a_w).astype(MXU_DTYPE)
    wx_dense = _block_diag(gate_x_w).astype(MXU_DTYPE)
    for l in range(DEPTH):
        small = dict(cw=conv_full[l], cb=row(conv_b[l]), wa=wa_dense[l], ba=row(gate_a_b[l]),
                     wx=wx_dense[l], bx=row(gate_x_b[l]), lam=row(lru_lambda[l]),
                     gain=row(lru_out_norm[l]))
        s1, r1, lands = level1[l]
        hn1 = rmsnorm_fwd(h, row(norm_mix[l]), "rms_fwd")
        proj = mm_blocked_nn(hn1, w_in_next, F32, "proj")
        ylru, hst = lru_fwd(proj, name="lru_fwd", **small)
        s2, r2, rest, order = ag_forward(s1[4:], r1[4:], lands[1:], ylru, f"ag_forward_{l}_rest")
        ymix, states = ret_fwd(proj, ylru, tables, row(ret_out_norm[l]), "ret_fwd")
        w = as_weights(w_in_next, *ag_finish(s2, r2, rest, ymix, f"ag_finish_{l}_rest"))
        gathered.append(w)
        h_mid = mm_nn_res(ymix, w["w_out"], h, order, "out_proj")
        hn2 = rmsnorm_fwd(h_mid, row(norm_ffn[l]), "rms_fwd")
        gate, up, act = ffn_up(hn2, w["w_gate"], w["w_up"], "ffn_up")
        if l + 1 < DEPTH:
            s1n, r1n, landsn = level1[l + 1]
            s2, r2, first, order = ag_forward(s1n[:4], r1n[:4], landsn[:1], act, f"ag_forward_{l + 1}_w_in")
        h_out = mm_nn_res(act, w["w_down"], h_mid, order, "ffn_down")
        if l + 1 < DEPTH:
            w_in_next = ag_finish(s2, r2, first, h_out, f"ag_finish_{l + 1}_w_in")[0]
        saved.append(dict(h=h, hn1=hn1, proj=proj, hst=hst, states=states, ymix=ymix, h_mid=h_mid, hn2=hn2, gate=gate, up=up,
                          act=act, small=small))
        h = h_out

    loss_p, dh, dh_b, g_norm_final = loss_head(h, row(norm_final), loss_target[0], "loss_head")
    loss = lax.psum(loss_p[0, 0], ("x", "y", "c"))

    rep = [None] * DEPTH
    convw_g = [None] * DEPTH
    inflight = []
    sib = None
    order = loss_p

    def sibling_done(l, tag, names, sib, after):
        parts, got = rs_sibling_wait(*sib, after, f"rs_sibling_wait_{tag}")
        sums = pair_sum(parts, got, c_arr, "pair_sum")
        flying, started = rs_chips_start(sums, f"rs_chips_start_{tag}")
        inflight.append((l, tag, names, flying))
        return started

    for l in reversed(range(DEPTH)):
        w, s = gathered[l], saved[l]
        dgate, dup = ffn_down_bwd(dh_b, w["w_down"], s["gate"], s["up"], order, "ffn_down_bwd")
        dwd = mm_tn(s["act"], dh_b, PAIR, order, "dw_down").reshape(NDEV, FF_SHP, D)
        dwg, dwu = (g.reshape(NDEV, FF_SHP, D) for g in mm_tn_two(dgate, dup, s["hn2"], PAIR, order, "dw_rows"))
        ffn_sib, order = rs_sibling_start([dwg, dwu, dwd], f"rs_sibling_start_{l}_ffn")
        dhn2 = mm_rows_nn([(dgate, w["w_gate"]), (dup, w["w_up"])], order, "ffn_up_bwd")
        if sib is not None:
            order = sibling_done(l + 1, f"{l + 1}_mix", ("w_in", "w_out"), sib, dhn2)
        dh_mid, dh_mid_b, g_norm_ffn = rmsnorm_bwd(s["h_mid"], row(norm_ffn[l]), dhn2, dh, "rms_bwd")
        dymix, dwo = out_proj_bwd(dh_mid_b, w["w_out"], s["ymix"], order, "out_proj_bwd")
        dwo = dwo.reshape(NDEV, OUT_SH, D)
        order = sibling_done(l, f"{l}_ffn", ("w_gate", "w_up", "w_down"), ffn_sib, dymix)
        dxg, lvec, dwa, dwx = lru_bwd(s["proj"], s["hst"], dymix, after=order, name="lru_bwd", **s["small"])
        dproj, g_ret_norm = ret_bwd(s["proj"], s["states"], dymix, dxg, tables, row(ret_out_norm[l]), "ret_bwd")
        dwi = mm_tn_blocked(s["hn1"], dproj, "dw_blocked")
        dhn1 = mm_blocked_nt([(dproj, w["w_in"])], order, "proj_bwd")
        dh, dh_b, g_norm_mix = rmsnorm_bwd(s["h"], row(norm_mix[l]), dhn1, dh_mid, "rms_bwd")

        rep[l] = [g_norm_mix, lvec[4], _diag_blocks(dwa), lvec[5], _diag_blocks(dwx), lvec[6], lvec[7], lvec[8], g_ret_norm,
                  g_norm_ffn]
        convw_g[l] = lvec[0:CONV_W]
        sib, order = rs_sibling_start([dwi, dwo], f"rs_sibling_start_{l}_mix")
        if l == 1:
            early = _pack([a for ll in range(DEPTH - 1, 0, -1) for a in rep[ll] + [convw_g[ll]]])
            early_sems = ag_start([place_block(dev_arr, early, "place_grads")], order, "ag_start_grads")
            order = early_sems[3]

    grad_x = dh[X0:][None]
    g_meta = dh[PAD:X0]

    rep_shapes = [(D,), (D_LRU,), (LRU_BLOCKS, LRU_BD, LRU_BD), (LRU_BLOCKS, LRU_BD), (LRU_BLOCKS, LRU_BD, LRU_BD),
                  (LRU_BLOCKS, LRU_BD), (D_LRU,), (D_LRU,), (D_RET,), (D,)]
    late = _pack(rep[0] + [convw_g[0], g_norm_final, g_meta])
    (gath_late,) = all_gather([late], "ag_grads")
    s2, r2, lands, _ = ag_forward(early_sems[0], early_sems[1], early_sems[2], dh, "ag_forward_grads")
    (gath_early,) = ag_finish(s2, r2, lands, gath_late, "ag_finish_grads")
    sibling_done(0, "0_mix", ("w_in", "w_out"), sib, gath_late)
    layer_shapes = rep_shapes + [(CONV_W, D_LRU)]
    parts_early = _unpack(sum_devices(gath_early, "sum_devices"), layer_shapes * (DEPTH - 1))
    parts_late = _unpack(sum_devices(gath_late, "sum_devices"), layer_shapes + [(D,), (N_META, D)])
    nl = len(layer_shapes)
    by_layer = {0: parts_late[:nl]}
    for i, ll in enumerate(range(DEPTH - 1, 0, -1)):
        by_layer[ll] = parts_early[i * nl:(i + 1) * nl]
    g_rep = {n: jnp.stack([by_layer[l][i] for l in range(DEPTH)]) for i, n in enumerate(REP_NAMES[:-1])}
    g_rep["norm_final"] = parts_late[nl]
    g_convw = lax.dynamic_slice_in_dim(jnp.stack([by_layer[l][nl - 1] for l in range(DEPTH)]), dev * (D_LRU // NDEV),
                                       D_LRU // NDEV, axis=2)
    g_metatok = lax.dynamic_slice_in_dim(parts_late[nl + 1], dev * (D // NDEV), D // NDEV, axis=1)

    given = dict(norm_mix=(norm_mix, m_norm_mix, v_norm_mix), conv_b=(conv_b, m_conv_b, v_conv_b),
                 gate_a_w=(gate_a_w, m_gate_a_w, v_gate_a_w), gate_a_b=(gate_a_b, m_gate_a_b, v_gate_a_b),
                 gate_x_w=(gate_x_w, m_gate_x_w, v_gate_x_w), gate_x_b=(gate_x_b, m_gate_x_b, v_gate_x_b),
                 lru_lambda=(lru_lambda, m_lru_lambda, v_lru_lambda), lru_out_norm=(lru_out_norm, m_lru_out_norm, v_lru_out_norm),
                 ret_out_norm=(ret_out_norm, m_ret_out_norm, v_ret_out_norm), norm_ffn=(norm_ffn, m_norm_ffn, v_norm_ffn),
                 norm_final=(norm_final, m_norm_final, v_norm_final),
                 conv_w=(conv_w, m_conv_w, v_conv_w), meta_tokens=(meta_tokens, m_meta_tokens, v_meta_tokens))
    small_names = REP_NAMES + ["conv_w", "meta_tokens"]
    small_g = dict(g_rep, conv_w=g_convw, meta_tokens=g_metatok)
    small_shapes = [given[n][0].shape for n in small_names]
    packs = [_pack([small_g[n] for n in small_names])] + [_pack([given[n][k] for n in small_names]) for k in range(3)]
    upd = adamw_rows(*packs, "adamw_small")
    small_out = [dict(zip(small_names, _unpack(p, small_shapes))) for p in upd]

    arrived = {}

    def wait_for(entries, after):
        for l, tag, names, flying in entries:
            sums, recv = rs_chips_wait(*flying, after, f"rs_chips_wait_{tag}")
            for i, n in enumerate(names):
                arrived[l, n] = (recv[i], sums[i])

    chip = jnp.reshape(2 * xi + yi, (1,)).astype(jnp.int32)

    def finish(wname, w_, m_, v_, tr):
        return adamw_big([arrived[l, wname][0] for l in range(DEPTH)], [arrived[l, wname][1] for l in range(DEPTH)], chip,
                         w_, m_, v_, tr, "adamw_" + wname)

    wait_for(inflight[:-1], upd[0])
    o_gate = [tr_(o) for o in finish("w_gate", w_gate_t, m_w_gate_t, v_w_gate_t, 32)]
    o_up = [tr_(o) for o in finish("w_up", w_up_t, m_w_up_t, v_w_up_t, 32)]
    o_down = finish("w_down", w_down, m_w_down, v_w_down, 32)
    wait_for(inflight[-1:], o_down[0])
    o_in = finish("w_in", w_in, m_w_in, v_w_in, 256)
    o_out = finish("w_out", w_out, m_w_out, v_w_out, 64)

    bigs = dict(w_in=o_in, w_out=o_out, w_gate=o_gate, w_up=o_up, w_down=o_down)
    order = ["meta_tokens", "norm_mix", "w_in", "conv_w", "conv_b", "gate_a_w", "gate_a_b", "gate_x_w", "gate_x_b", "lru_lambda",
             "lru_out_norm", "ret_out_norm", "w_out", "norm_ffn", "w_gate", "w_up", "w_down", "norm_final"]
    grads = [bigs[n][0] if n in bigs else small_g[n] for n in order]
    rest = [[bigs[n][k + 1] if n in bigs else small_out[k][n] for n in order] for k in range(3)]
    return (loss, grad_x, *grads, *rest[0], *rest[1], *rest[2])
```

```python
import functools

import numpy as np
import jax
import jax.numpy as jnp
from jax import lax
from jax.experimental import pallas as pl
from jax.experimental.pallas import tpu as pltpu

F32, BF16 = jnp.float32, jnp.bfloat16
MXU_DTYPE = BF16
WIRE_DTYPE = BF16

D = 1024
SEQ = 2048
DEPTH = 4
N_META = 16
CH = 128
PAD = (-(SEQ + N_META)) % CH
T = SEQ + N_META + PAD
NCH = T // CH
X0 = PAD + N_META
D_LRU = 512
LRU_BLOCKS = 8
LRU_BD = 64
CONV_W = 4
LRU_C = 8.0
D_RET = 512
HEADS = 4
HD = 128
ROPE_BASE = 10000.0
D_IN = 3072
D_FF = 2816
NDEV = 8
IN_SH = D_IN // NDEV
FF_SH = D_FF // NDEV
FF_SHP = 384
D_FFP = NDEV * FF_SHP
OUT_SH = D // NDEV
EPS = 1e-6
TM = 544
TR = 1088
VMEM_LIMIT = 56 * 2**20
MESH = pl.DeviceIdType.MESH

ADAM_LR, ADAM_B1, ADAM_B2, ADAM_EPS, ADAM_WD, ADAM_STEP = 0.001, 0.9, 0.999, 1e-08, 0.01, 10

NN = ((1,), (0,))
NT = ((1,), (1,))
TN = ((0,), (0,))


def _dot(a, b, dims):
    return lax.dot_general(a.astype(MXU_DTYPE), b.astype(MXU_DTYPE), (dims, ((), ())), preferred_element_type=F32)


def _sds(shape, dtype):
    return jax.ShapeDtypeStruct(shape, dtype)


def _params(sem=None):
    return pltpu.CompilerParams(dimension_semantics=sem, vmem_limit_bytes=VMEM_LIMIT)


def _full(shape):
    n = len(shape)
    return pl.BlockSpec(shape, lambda *_: (0,) * n)


def rmsnorm_fwd(h, gain, name):
    def body(h_ref, g_ref, o_ref):
        x = h_ref[...]
        ms = jnp.mean(x * x, axis=-1, keepdims=True)
        o_ref[...] = (x * lax.rsqrt(ms + EPS) * g_ref[...]).astype(o_ref.dtype)

    return pl.pallas_call(
        body, name=name, grid=(T // TM,),
        in_specs=[pl.BlockSpec((TM, D), lambda i: (i, 0)), _full((1, D))],
        out_specs=pl.BlockSpec((TM, D), lambda i: (i, 0)),
        out_shape=_sds((T, D), MXU_DTYPE), compiler_params=_params(("parallel",)),
    )(h, gain)


def rmsnorm_bwd(h, gain, dhn, dres, name):
    def body(h_ref, g_ref, dhn_ref, dres_ref, dh_ref, dhb_ref, dg_ref):
        x = h_ref[...]
        rstd = lax.rsqrt(jnp.mean(x * x, axis=-1, keepdims=True) + EPS)
        xhat = x * rstd
        dy = dhn_ref[...]
        dyg = dy * g_ref[...]
        dh = dres_ref[...] + rstd * (dyg - xhat * jnp.mean(dyg * xhat, axis=-1, keepdims=True))
        dh_ref[...] = dh
        dhb_ref[...] = dh.astype(dhb_ref.dtype)

        @pl.when(pl.program_id(0) == 0)
        def _():
            dg_ref[...] = jnp.zeros_like(dg_ref)
        dg_ref[...] += jnp.sum(dy * xhat, axis=0, keepdims=True)

    row = pl.BlockSpec((TM, D), lambda i: (i, 0))
    return pl.pallas_call(
        body, name=name, grid=(T // TM,),
        in_specs=[row, _full((1, D)), row, row],
        out_specs=[row, row, _full((1, D))],
        out_shape=[_sds((T, D), F32), _sds((T, D), MXU_DTYPE), _sds((1, D), F32)], compiler_params=_params(("arbitrary",)),
    )(h, gain, dhn, dres)


def loss_head(h, gain, target, name):
    def body(h_ref, g_ref, t_ref, loss_ref, dh_ref, dhb_ref, dg_ref):
        i = pl.program_id(0)

        @pl.when(i == 0)
        def _():
            loss_ref[...] = jnp.zeros_like(loss_ref)
            dg_ref[...] = jnp.zeros_like(dg_ref)
            dh_ref[...] = jnp.zeros_like(dh_ref)
            dhb_ref[...] = jnp.zeros_like(dhb_ref)

        @pl.when(i > 0)
        def _():
            x = h_ref[...]
            g = g_ref[...]
            rstd = lax.rsqrt(jnp.mean(x * x, axis=-1, keepdims=True) + EPS)
            xhat = x * rstd
            err = xhat * g - t_ref[...]
            loss_ref[...] += 0.5 * jnp.sum(jnp.mean(err * err, axis=-1, keepdims=True), axis=0, keepdims=True)
            dy = err * (1.0 / D)
            dyg = dy * g
            dh = rstd * (dyg - xhat * jnp.mean(dyg * xhat, axis=-1, keepdims=True))
            dh_ref[...] = dh
            dhb_ref[...] = dh.astype(dhb_ref.dtype)
            dg_ref[...] += jnp.sum(dy * xhat, axis=0, keepdims=True)

    row = pl.BlockSpec((CH, D), lambda i: (i, 0))
    return pl.pallas_call(
        body, name=name, grid=(NCH,),
        in_specs=[row, _full((1, D)), pl.BlockSpec((CH, D), lambda i: (jnp.maximum(i - 1, 0), 0))],
        out_specs=[_full((8, 128)), row, row, _full((1, D))],
        out_shape=[_sds((8, 128), F32), _sds((T, D), F32), _sds((T, D), MXU_DTYPE), _sds((1, D), F32)],
        compiler_params=_params(("arbitrary",)),
    )(h, gain, target)


PAIR = 2 * IN_SH
NPAIR = NDEV // 2
BN = 256


def _pair_cols(w_ref):
    return jnp.concatenate([w_ref[0], w_ref[1]], axis=1)


W_PAIR = lambda k: pl.BlockSpec((2, k, IN_SH), lambda j: (j, 0, 0))
COLS_PAIR = pl.BlockSpec((T, PAIR), lambda j: (0, j))
ANYSPEC = pl.BlockSpec(memory_space=pl.ANY)


def mm_blocked_nn(a, w, out_dtype, name):
    k = a.shape[1]

    def body(a_ref, w_ref, o_ref):
        o_ref[...] = _dot(a_ref[...], _pair_cols(w_ref), NN).astype(o_ref.dtype)

    return pl.pallas_call(
        body, name=name, grid=(NPAIR,),
        in_specs=[_full((T, k)), W_PAIR(k)], out_specs=COLS_PAIR,
        out_shape=_sds((T, NDEV * IN_SH), out_dtype), compiler_params=_params(("parallel",)),
    )(a, w)


def mm_nn_res(a, w, res, after, name):
    k = a.shape[1]

    def body(a_ref, w_ref, r_ref, after_ref, o_ref):
        del after_ref
        o_ref[...] = r_ref[...] + _dot(a_ref[...], w_ref[...], NN)

    col = pl.BlockSpec((T, BN), lambda j: (0, j))
    return pl.pallas_call(
        body, name=name, grid=(D // BN,),
        in_specs=[_full((T, k)), pl.BlockSpec((k, BN), lambda j: (0, j)), col, ANYSPEC], out_specs=col,
        out_shape=_sds((T, D), F32), compiler_params=_params(("parallel",)),
    )(a, w, res, after)


def ffn_up(hn, wg, wu, name):
    def body(a_ref, wg_ref, wu_ref, g_ref, u_ref, act_ref):
        a = a_ref[...]
        g = _dot(a, wg_ref[...], NT)
        u = _dot(a, wu_ref[...], NT)
        g_ref[...] = g.astype(g_ref.dtype)
        u_ref[...] = u.astype(u_ref.dtype)
        act_ref[...] = (jax.nn.silu(g) * u).astype(act_ref.dtype)

    wspec = pl.BlockSpec((PAIR, D), lambda j, i: (j, 0))
    ospec = pl.BlockSpec((TR, PAIR), lambda j, i: (i, j))
    return pl.pallas_call(
        body, name=name, grid=(NPAIR, T // TR),
        in_specs=[pl.BlockSpec((TR, D), lambda j, i: (i, 0)), wspec, wspec], out_specs=[ospec] * 3,
        out_shape=[_sds((T, D_FFP), MXU_DTYPE)] * 3, compiler_params=_params(("parallel", "parallel")),
    )(hn, wg, wu)


def ffn_down_bwd(dh, wd, gate, up, after, name):
    def body(dh_ref, wd_ref, g_ref, u_ref, after_ref, dg_ref, du_ref):
        del after_ref
        dact = _dot(dh_ref[...], wd_ref[...], NT)
        g = g_ref[...].astype(F32)
        u = u_ref[...].astype(F32)
        sg = jax.nn.sigmoid(g)
        dg_ref[...] = (dact * u * (sg * (1.0 + g * (1.0 - sg)))).astype(dg_ref.dtype)
        du_ref[...] = (dact * (g * sg)).astype(du_ref.dtype)

    blk = pl.BlockSpec((TR, PAIR), lambda j, i: (i, j))
    return pl.pallas_call(
        body, name=name, grid=(NPAIR, T // TR),
        in_specs=[pl.BlockSpec((TR, D), lambda j, i: (i, 0)), pl.BlockSpec((PAIR, D), lambda j, i: (j, 0)), blk, blk, ANYSPEC],
        out_specs=[blk, blk],
        out_shape=[_sds((T, D_FFP), MXU_DTYPE)] * 2, compiler_params=_params(("parallel", "parallel")),
    )(dh, wd, gate, up, after)


def mm_blocked_nt(pairs, after, name):
    n = len(pairs)

    def body(*refs):
        o_ref = refs[2 * n + 1]

        @pl.when(pl.program_id(1) == 0)
        def _():
            o_ref[...] = jnp.zeros_like(o_ref)
        for p in range(n):
            o_ref[...] += _dot(refs[2 * p][...], _pair_cols(refs[2 * p + 1]), NT)

    specs, args = [], []
    for a, w in pairs:
        specs += [pl.BlockSpec((TR, PAIR), lambda i, j: (i, j)), pl.BlockSpec((2, D, IN_SH), lambda i, j: (j, 0, 0))]
        args += [a, w]
    return pl.pallas_call(
        body, name=name, grid=(T // TR, NPAIR), in_specs=specs + [ANYSPEC],
        out_specs=pl.BlockSpec((TR, D), lambda i, j: (i, 0)),
        out_shape=_sds((T, D), F32), compiler_params=_params(("parallel", "arbitrary")),
    )(*args, after)


def mm_tn_two(a1, a2, b, bm, after, name):
    m = a1.shape[1]

    def body(a1_ref, a2_ref, b_ref, after_ref, o1_ref, o2_ref):
        del after_ref
        b = b_ref[...]
        o1_ref[...] = _dot(a1_ref[...], b, TN).astype(o1_ref.dtype)
        o2_ref[...] = _dot(a2_ref[...], b, TN).astype(o2_ref.dtype)

    blk = pl.BlockSpec((T, bm), lambda i: (0, i))
    out = pl.BlockSpec((bm, D), lambda i: (i, 0))
    return pl.pallas_call(
        body, name=name, grid=(m // bm,),
        in_specs=[blk, blk, _full((T, D)), ANYSPEC], out_specs=[out, out],
        out_shape=[_sds((m, D), WIRE_DTYPE)] * 2, compiler_params=_params(("parallel",)),
    )(a1, a2, b, after)


def out_proj_bwd(dh, w, ymix, after, name):
    def body(dh_ref, w_ref, y_ref, after_ref, dy_ref, dw_ref):
        del after_ref
        dh_ = dh_ref[...]
        dy_ref[...] = _dot(dh_, w_ref[...], NT)
        dw_ref[...] = _dot(y_ref[...], dh_, TN).astype(dw_ref.dtype)

    return pl.pallas_call(
        body, name=name, grid=(D // BN,),
        in_specs=[_full((T, D)), pl.BlockSpec((BN, D), lambda j: (j, 0)), pl.BlockSpec((T, BN), lambda j: (0, j)), ANYSPEC],
        out_specs=[pl.BlockSpec((T, BN), lambda j: (0, j)), pl.BlockSpec((BN, D), lambda j: (j, 0))],
        out_shape=[_sds((T, D), F32), _sds((D, D), WIRE_DTYPE)], compiler_params=_params(("parallel",)),
    )(dh, w, ymix, after)


def mm_rows_nn(pairs, after, name):
    n = len(pairs)

    def body(*refs):
        o_ref = refs[2 * n + 1]

        @pl.when(pl.program_id(1) == 0)
        def _():
            o_ref[...] = jnp.zeros_like(o_ref)
        for p in range(n):
            o_ref[...] += _dot(refs[2 * p][...], refs[2 * p + 1][...], NN)

    specs, args = [], []
    for a, w in pairs:
        specs += [pl.BlockSpec((TR, PAIR), lambda i, j: (i, j)), pl.BlockSpec((PAIR, D), lambda i, j: (j, 0))]
        args += [a, w]
    return pl.pallas_call(
        body, name=name, grid=(T // TR, NPAIR), in_specs=specs + [ANYSPEC],
        out_specs=pl.BlockSpec((TR, D), lambda i, j: (i, 0)),
        out_shape=_sds((T, D), F32), compiler_params=_params(("parallel", "arbitrary")),
    )(*args, after)


def mm_tn_blocked(a, b, name):
    def body(a_ref, b_ref, o_ref):
        o = _dot(a_ref[...], b_ref[...], TN).astype(o_ref.dtype)
        o_ref[0] = o[:, :IN_SH]
        o_ref[1] = o[:, IN_SH:]

    return pl.pallas_call(
        body, name=name, grid=(NPAIR,),
        in_specs=[_full((T, D)), COLS_PAIR], out_specs=W_PAIR(D),
        out_shape=_sds((NDEV, D, IN_SH), WIRE_DTYPE), compiler_params=_params(("parallel",)),
    )(a, b)


def mm_tn(a, b, bm, after, name):
    m = a.shape[1]

    def body(a_ref, b_ref, after_ref, o_ref):
        del after_ref
        o_ref[...] = _dot(a_ref[...], b_ref[...], TN).astype(o_ref.dtype)

    return pl.pallas_call(
        body, name=name, grid=(m // bm,),
        in_specs=[pl.BlockSpec((T, bm), lambda i: (0, i)), _full((T, D)), ANYSPEC],
        out_specs=pl.BlockSpec((bm, D), lambda i: (i, 0)),
        out_shape=_sds((m, D), WIRE_DTYPE), compiler_params=_params(("parallel",)),
    )(a, b, after)


def _softplus_neg(lam):
    return jnp.maximum(-lam, 0.0) + jnp.log1p(jnp.exp(-jnp.abs(lam)))


def _lru_gates(pa, px, xc, lam):
    r = jax.nn.sigmoid(pa)
    ig = jax.nn.sigmoid(px)
    sp = _softplus_neg(lam)
    log_a = -LRU_C * r * sp
    a = jnp.exp(log_a)
    mult = jnp.sqrt(-jnp.tanh(log_a) * (a * a + 1.0))
    return a, mult * (ig * xc), (r, ig, sp, mult)


def _lru_gates_vjp(da, db, xc, lam, a, r, ig, sp, mult):
    dmult = db * (ig * xc)
    du = db * mult
    dlog_a = da * a - dmult * (a * a) / mult
    dr = dlog_a * (-LRU_C * sp)
    dlam = jnp.sum(dlog_a * (-LRU_C * r), axis=0, keepdims=True) * (-jax.nn.sigmoid(-lam))
    dpa = dr * (r * (1.0 - r))
    dpx = (du * xc) * (ig * (1.0 - ig))
    return dpa, dpx, du * ig, dlam


def _lru_out(h, g, gain):
    z = h * jax.nn.gelu(g)
    return z * lax.rsqrt(jnp.mean(z * z, axis=-1, keepdims=True) + EPS) * gain


def _conv_taps(x, xprev, row):
    taps = [x]
    for s in range(1, CONV_W):
        taps.append(jnp.where(row < s, pltpu.roll(xprev, s, 0), pltpu.roll(x, s, 0)))
    return taps


def _conv(taps, cw_ref, cb):
    xc = cb + cw_ref[CONV_W - 1:CONV_W, :] * taps[0]
    for s in range(1, CONV_W):
        xc = xc + cw_ref[CONV_W - 1 - s:CONV_W - s, :] * taps[s]
    return xc


def lru_fwd(proj, cw, cb, wa, ba, wx, bx, lam, gain, name):
    def body(x_ref, g_ref, cw_ref, cb_ref, wa_ref, ba_ref, wx_ref, bx_ref, lam_ref, gain_ref,
             y_ref, h_ref, xprev_scr, a_scr, b_scr, carry_scr):
        i = pl.program_id(0)

        @pl.when(i == 0)
        def _():
            xprev_scr[...] = jnp.zeros_like(xprev_scr)
            carry_scr[...] = jnp.zeros_like(carry_scr)

        x = x_ref[...]
        row = lax.broadcasted_iota(jnp.int32, (CH, D_LRU), 0)
        xc = _conv(_conv_taps(x, xprev_scr[...], row), cw_ref, cb_ref[...])
        pa = _dot(xc, wa_ref[...], NN) + ba_ref[...]
        px = _dot(xc, wx_ref[...], NN) + bx_ref[...]
        a, b, _ = _lru_gates(pa, px, xc, lam_ref[...])
        a_scr[...] = a
        b_scr[...] = jnp.where(i * CH + row >= PAD, b, 0.0)
        h = carry_scr[...]
        for t in range(CH):
            h = a_scr[t:t + 1, :] * h + b_scr[t:t + 1, :]
            h_ref[t:t + 1, :] = h
        carry_scr[...] = h
        xprev_scr[...] = x
        y_ref[...] = _lru_out(h_ref[...], g_ref[...], gain_ref[...]).astype(y_ref.dtype)

    vec = _full((1, D_LRU))
    mat = _full((D_LRU, D_LRU))
    return pl.pallas_call(
        body, name=name, grid=(NCH,),
        in_specs=[pl.BlockSpec((CH, D_LRU), lambda i: (i, 0)), pl.BlockSpec((CH, D_LRU), lambda i: (i, 1)),
                  _full((CONV_W, D_LRU)), vec, mat, vec, mat, vec, vec, vec],
        out_specs=[pl.BlockSpec((CH, D_LRU), lambda i: (i, 0)), pl.BlockSpec((CH, D_LRU), lambda i: (i, 0))],
        out_shape=[_sds((T, D_LRU), MXU_DTYPE), _sds((T, D_LRU), F32)],
        scratch_shapes=[pltpu.VMEM((CH, D_LRU), F32), pltpu.VMEM((CH, D_LRU), F32), pltpu.VMEM((CH, D_LRU), F32),
                        pltpu.VMEM((1, D_LRU), F32)],
        compiler_params=_params(("arbitrary",)),
    )(proj, proj, cw, cb, wa, ba, wx, bx, lam, gain)


LRU_VEC_ROWS = 16


def lru_bwd(proj, hst, dymix, cw, cb, wa, ba, wx, bx, lam, gain, after, name):
    last = NCH - 1

    def body(x_ref, xp_ref, g_ref, h_ref, hp_ref, dy_ref, cw_ref, cb_ref, wa_ref, ba_ref, wx_ref, bx_ref, lam_ref,
             gain_ref, after_ref, dxg_ref, vec_ref, dwa_ref, dwx_ref, a_scr, dh_scr, g_scr, carry_scr, dxcn_scr):
        del after_ref
        i = pl.program_id(0)
        ib = last - i

        @pl.when(i == 0)
        def _():
            carry_scr[...] = jnp.zeros_like(carry_scr)
            dxcn_scr[...] = jnp.zeros_like(dxcn_scr)
            vec_ref[...] = jnp.zeros_like(vec_ref)
            dwa_ref[...] = jnp.zeros_like(dwa_ref)
            dwx_ref[...] = jnp.zeros_like(dwx_ref)

        x = x_ref[...]
        row = lax.broadcasted_iota(jnp.int32, (CH, D_LRU), 0)
        valid = ib * CH + row >= PAD
        taps = _conv_taps(x, xp_ref[...], row)
        xc = _conv(taps, cw_ref, cb_ref[...])
        pa = _dot(xc, wa_ref[...], NN) + ba_ref[...]
        px = _dot(xc, wx_ref[...], NN) + bx_ref[...]
        a, _, gate_parts = _lru_gates(pa, px, xc, lam_ref[...])
        h = h_ref[...]
        _, vjp_out = jax.vjp(_lru_out, h, g_ref[...], gain_ref[...])
        dh, dg, dgain = vjp_out(dy_ref[...].astype(F32))
        a_scr[...] = a
        dh_scr[...] = dh
        c = carry_scr[...]
        for t in range(CH - 1, -1, -1):
            gt = dh_scr[t:t + 1, :] + c
            g_scr[t:t + 1, :] = gt
            c = a_scr[t:t + 1, :] * gt
        carry_scr[...] = c
        gg = g_scr[...]
        hprev = jnp.where(row < 1, pltpu.roll(hp_ref[...], 1, 0), pltpu.roll(h, 1, 0))
        da = jnp.where(valid, gg * hprev, 0.0)
        db = jnp.where(valid, gg, 0.0)
        dpa, dpx, dxc, dlam = _lru_gates_vjp(da, db, xc, lam_ref[...], a, *gate_parts)
        dxc = dxc + _dot(dpa, wa_ref[...], NT) + _dot(dpx, wx_ref[...], NT)
        dwa_ref[...] += _dot(xc, dpa, TN)
        dwx_ref[...] += _dot(xc, dpx, TN)
        for s in range(CONV_W):
            vec_ref[CONV_W - 1 - s:CONV_W - s, :] += jnp.sum(dxc * taps[s], axis=0, keepdims=True)
        vec_ref[4:5, :] += jnp.sum(dxc, axis=0, keepdims=True)
        vec_ref[5:6, :] += jnp.sum(dpa, axis=0, keepdims=True)
        vec_ref[6:7, :] += jnp.sum(dpx, axis=0, keepdims=True)
        vec_ref[7:8, :] += dlam
        vec_ref[8:9, :] += dgain
        dxn = dxcn_scr[...]
        dx = cw_ref[CONV_W - 1:CONV_W, :] * dxc
        for s in range(1, CONV_W):
            ahead = jnp.where(row >= CH - s, pltpu.roll(dxn, CH - s, 0), pltpu.roll(dxc, CH - s, 0))
            dx = dx + cw_ref[CONV_W - 1 - s:CONV_W - s, :] * ahead
        dxcn_scr[...] = dxc
        dxg_ref[:, :D_LRU] = jnp.where(valid, dx, 0.0).astype(dxg_ref.dtype)
        dxg_ref[:, D_LRU:] = dg.astype(dxg_ref.dtype)

    vec = _full((1, D_LRU))
    mat = _full((D_LRU, D_LRU))

    def blk(col, shift=0):
        return pl.BlockSpec((CH, D_LRU), lambda i: (jnp.maximum(last - i - shift, 0), col))

    return pl.pallas_call(
        body, name=name, grid=(NCH,),
        in_specs=[blk(0), blk(0, 1), blk(1), blk(0), blk(0, 1), blk(0),
                  _full((CONV_W, D_LRU)), vec, mat, vec, mat, vec, vec, vec, pl.BlockSpec(memory_space=pl.ANY)],
        out_specs=[pl.BlockSpec((CH, 2 * D_LRU), lambda i: (last - i, 0)), _full((LRU_VEC_ROWS, D_LRU)), mat, mat],
        out_shape=[_sds((T, 2 * D_LRU), MXU_DTYPE), _sds((LRU_VEC_ROWS, D_LRU), F32),
                   _sds((D_LRU, D_LRU), F32), _sds((D_LRU, D_LRU), F32)],
        scratch_shapes=[pltpu.VMEM((CH, D_LRU), F32), pltpu.VMEM((CH, D_LRU), F32), pltpu.VMEM((CH, D_LRU), F32),
                        pltpu.VMEM((1, D_LRU), F32), pltpu.VMEM((CH, D_LRU), F32)],
        compiler_params=_params(("arbitrary",)),
    )(proj, proj, proj, hst, hst, dymix, cw, cb, wa, ba, wx, bx, lam, gain, after)


def _ret_tables():
    half = HD // 2
    pos = jnp.arange(T, dtype=F32) - float(PAD)
    inv = ROPE_BASE ** (-jnp.arange(half, dtype=F32) / half)
    ang = pos[:, None] * inv[None, :]
    cos = jnp.concatenate([jnp.cos(ang), jnp.cos(ang)], axis=-1)
    sin = jnp.concatenate([-jnp.sin(ang), jnp.sin(ang)], axis=-1)
    log_g = jnp.log(1.0 - 2.0 ** (-5.0 - jnp.arange(HEADS, dtype=F32)))
    idx = jnp.arange(CH, dtype=F32)
    diff = idx[:, None] - idx[None, :]
    dmask = jnp.where(diff[None] >= 0, jnp.exp(jnp.maximum(diff, 0.0)[None] * log_g[:, None, None]), 0.0)
    xi = jnp.exp((idx + 1.0)[None, :] * log_g[:, None])
    zeta = jnp.exp((CH - 1.0 - idx)[None, :] * log_g[:, None])
    xi = jnp.broadcast_to(xi[:, :, None], (HEADS, CH, HD))
    zeta = jnp.broadcast_to(zeta[:, :, None], (HEADS, CH, HD))
    return cos, sin, dmask, xi, zeta


def _chunk_decay():
    log_g = np.log(np.float32(1.0) - np.float32(2.0) ** (np.float32(-5.0) - np.arange(HEADS, dtype=np.float32)))
    return [float(v) for v in np.exp(np.float32(CH) * log_g.astype(np.float32))]


def _rope(x, cos, sin):
    return x * cos + pltpu.roll(x, HD // 2, 1) * sin


def ret_fwd(proj, ylru, tables, gain, after, name):
    cos, sin, dmask, xi, zeta = tables
    gch = _chunk_decay()
    scale = HD ** -0.5

    def body(q_ref, k_ref, v_ref, g_ref, cos_ref, sin_ref, dm_ref, xi_ref, zt_ref, gain_ref, ylru_ref, after_ref,
             y_ref, st_ref, s_scr):
        del after_ref

        @pl.when(pl.program_id(0) == 0)
        def _():
            s_scr[...] = jnp.zeros_like(s_scr)

        y_ref[:, :D_LRU] = ylru_ref[...]
        cs, sn = cos_ref[...], sin_ref[...]
        hs = range(HEADS)
        sl = [slice(HD * h, HD * (h + 1)) for h in hs]
        qr = [_rope(q_ref[:, sl[h]], cs, sn).astype(MXU_DTYPE) for h in hs]
        kf = [_rope(k_ref[:, sl[h]], cs, sn) * scale for h in hs]
        kr = [kf[h].astype(MXU_DTYPE) for h in hs]
        v = [v_ref[:, sl[h]].astype(MXU_DTYPE) for h in hs]
        s = [s_scr[h] for h in hs]
        for h in hs:
            st_ref[h] = s[h]
        sc = [_dot(qr[h], kr[h], NT) * dm_ref[h] for h in hs]
        cross = [_dot(qr[h], s[h], NN) * xi_ref[h] for h in hs]
        for h in hs:
            s_scr[h] = s[h] * gch[h] + _dot(kf[h] * zt_ref[h], v[h], TN)
        y = [_dot(sc[h], v[h], NN) + cross[h] for h in hs]
        yc = [y[h] - jnp.mean(y[h], axis=-1, keepdims=True) for h in hs]
        yn = [yc[h] * lax.rsqrt(jnp.mean(yc[h] * yc[h], axis=-1, keepdims=True) + EPS) for h in hs]
        for h in hs:
            so = slice(D_LRU + HD * h, D_LRU + HD * (h + 1))
            y_ref[:, so] = (jax.nn.silu(g_ref[:, sl[h]]) * (yn[h] * gain_ref[:, sl[h]])).astype(y_ref.dtype)

    def col(c):
        return pl.BlockSpec((CH, D_RET), lambda n: (n, c))

    tab = pl.BlockSpec((CH, HD), lambda n: (n, 0))
    cst = _full((HEADS, CH, HD))
    return pl.pallas_call(
        body, name=name, grid=(NCH,),
        in_specs=[col(2), col(3), col(4), col(5), tab, tab, cst, cst, cst, _full((1, D_RET)), col(0),
                  pl.BlockSpec(memory_space=pl.ANY)],
        out_specs=[pl.BlockSpec((CH, D), lambda n: (n, 0)), pl.BlockSpec((None, HEADS, HD, HD), lambda n: (n, 0, 0, 0))],
        out_shape=[_sds((T, D), MXU_DTYPE), _sds((NCH, HEADS, HD, HD), F32)],
        scratch_shapes=[pltpu.VMEM((HEADS, HD, HD), F32)],
        compiler_params=_params(("arbitrary",)),
    )(proj, proj, proj, proj, cos, sin, dmask, xi, zeta, gain, ylru, after)


def ret_bwd(proj, states, dymix, dxg, tables, gain, name):
    cos, sin, dmask, xi, zeta = tables
    gch = _chunk_decay()
    scale = HD ** -0.5
    last = NCH - 1

    def body(q_ref, k_ref, v_ref, g_ref, st_ref, do_ref, cos_ref, sin_ref, dm_ref, xi_ref, zt_ref, gain_ref, dxg_ref,
             dp_ref, dgain_ref, ds_scr):
        @pl.when(pl.program_id(0) == 0)
        def _():
            ds_scr[...] = jnp.zeros_like(ds_scr)
            dgain_ref[...] = jnp.zeros_like(dgain_ref)

        dp_ref[:, :2 * D_LRU] = dxg_ref[...]
        cs, sn = cos_ref[...], sin_ref[...]
        hs = range(HEADS)
        sl = [slice(HD * h, HD * (h + 1)) for h in hs]

        def out(j, h):
            return slice(2 * D_LRU + j * D_RET + HD * h, 2 * D_LRU + j * D_RET + HD * (h + 1))

        b16 = lambda xs: [x.astype(MXU_DTYPE) for x in xs]
        qr = b16([_rope(q_ref[:, sl[h]], cs, sn) for h in hs])
        kf = [_rope(k_ref[:, sl[h]], cs, sn) * scale for h in hs]
        kr = b16(kf)
        kz = b16([kf[h] * zt_ref[h] for h in hs])
        v = b16([v_ref[:, sl[h]] for h in hs])
        s = b16([st_ref[h] for h in hs])
        ds = [ds_scr[h] for h in hs]
        dsb = b16(ds)
        sc = [_dot(qr[h], kr[h], NT) * dm_ref[h] for h in hs]
        scb = b16(sc)
        y = [_dot(scb[h], v[h], NN) + _dot(qr[h], s[h], NN) * xi_ref[h] for h in hs]
        yc = [y[h] - jnp.mean(y[h], axis=-1, keepdims=True) for h in hs]
        rstd = [lax.rsqrt(jnp.mean(yc[h] * yc[h], axis=-1, keepdims=True) + EPS) for h in hs]
        yn = [yc[h] * rstd[h] for h in hs]
        dy = []
        for h in hs:
            g = g_ref[:, sl[h]]
            gain = gain_ref[:, sl[h]]
            sg = jax.nn.sigmoid(g)
            silu = g * sg
            dout = do_ref[:, sl[h]].astype(F32)
            dgain_ref[:, sl[h]] += jnp.sum(dout * silu * yn[h], axis=0, keepdims=True)
            dp_ref[:, out(3, h)] = (dout * yn[h] * gain * (sg * (1.0 + g * (1.0 - sg)))).astype(dp_ref.dtype)
            dyn = dout * silu * gain
            dy.append(rstd[h] * (dyn - jnp.mean(dyn, axis=-1, keepdims=True)
                                 - yn[h] * jnp.mean(dyn * yn[h], axis=-1, keepdims=True)))
        dyb = b16(dy)
        dqs = b16([dy[h] * xi_ref[h] for h in hs])
        dp = b16([_dot(dyb[h], v[h], NT) * dm_ref[h] for h in hs])
        dv = [_dot(scb[h], dyb[h], TN) + _dot(kz[h], dsb[h], NN) for h in hs]
        dqr = [_dot(dp[h], kr[h], NN) + _dot(dqs[h], s[h], NT) for h in hs]
        dkr = [_dot(dp[h], qr[h], TN) + _dot(v[h], dsb[h], NT) * zt_ref[h] for h in hs]
        for h in hs:
            ds_scr[h] = gch[h] * ds[h] + _dot(qr[h], dqs[h], TN)
        for h in hs:
            dp_ref[:, out(0, h)] = (dqr[h] * cs + pltpu.roll(dqr[h] * sn, HD // 2, 1)).astype(dp_ref.dtype)
            dp_ref[:, out(1, h)] = ((dkr[h] * cs + pltpu.roll(dkr[h] * sn, HD // 2, 1)) * scale).astype(dp_ref.dtype)
            dp_ref[:, out(2, h)] = dv[h].astype(dp_ref.dtype)

    def col(c):
        return pl.BlockSpec((CH, D_RET), lambda n: (last - n, c))

    tab = pl.BlockSpec((CH, HD), lambda n: (last - n, 0))
    cst = _full((HEADS, CH, HD))
    return pl.pallas_call(
        body, name=name, grid=(NCH,),
        in_specs=[col(2), col(3), col(4), col(5), pl.BlockSpec((None, HEADS, HD, HD), lambda n: (last - n, 0, 0, 0)), col(1),
                  tab, tab, cst, cst, cst, _full((1, D_RET)), pl.BlockSpec((CH, 2 * D_LRU), lambda n: (last - n, 0))],
        out_specs=[pl.BlockSpec((CH, D_IN), lambda n: (last - n, 0)), _full((1, D_RET))],
        out_shape=[_sds((T, D_IN), MXU_DTYPE), _sds((1, D_RET), F32)],
        scratch_shapes=[pltpu.VMEM((HEADS, HD, HD), F32)],
        compiler_params=_params(("arbitrary",)),
    )(proj, proj, proj, proj, states, dymix, cos, sin, dmask, xi, zeta, gain, dxg)


HBM = pl.BlockSpec(memory_space=pltpu.HBM)


def _place():
    return lax.axis_index("x"), lax.axis_index("y"), lax.axis_index("c")


def all_gather(arrs, name):
    n = len(arrs)

    def body(*refs):
        ins, outs = refs[:n], refs[n:2 * n]
        send_sems, recv_sems, local_sems = refs[2 * n:]
        x, y, c = _place()
        me, sibling = (x, y, c), (x, y, 1 - c)
        chips = [(1 - x, y), (x, 1 - y), (1 - x, 1 - y)]

        def copy(a, k, block, to, src=None):
            px, py, pc = block
            dst = outs[a].at[4 * px + 2 * py + pc]
            return pltpu.make_async_remote_copy(
                src_ref=dst if src is None else src, dst_ref=dst, send_sem=send_sems.at[a, k], recv_sem=recv_sems.at[a, k],
                device_id=to, device_id_type=MESH)

        mine = [pltpu.make_async_copy(ins[a], outs[a].at[4 * x + 2 * y + c], local_sems.at[a]) for a in range(n)]
        for cp in mine:
            cp.start()
        first = []
        for a in range(n):
            first.append(copy(a, 0, me, sibling, src=ins[a]))
            first += [copy(a, 1 + j, me, (*chip, c), src=ins[a]) for j, chip in enumerate(chips)]
        for cp in first:
            cp.start()
        passed = []
        for j, chip in enumerate(chips):
            for a in range(n):
                copy(a, 1 + j, (*chip, c), me).wait_recv()
                passed.append(copy(a, 4 + j, (*chip, c), sibling))
                passed[-1].start()
        for a in range(n):
            copy(a, 0, sibling, me).wait_recv()
            for j, chip in enumerate(chips):
                copy(a, 4 + j, (*chip, 1 - c), me).wait_recv()
        for cp in first + passed:
            cp.wait_send()
        for cp in mine:
            cp.wait()

    return pl.pallas_call(
        body, name=name,
        in_specs=[HBM] * n, out_specs=[HBM] * n,
        out_shape=[_sds((NDEV,) + a.shape, a.dtype) for a in arrs],
        scratch_shapes=[pltpu.SemaphoreType.DMA((n, 7)), pltpu.SemaphoreType.DMA((n, 7)), pltpu.SemaphoreType.DMA((n,))],
    )(*arrs)


SEM = pl.BlockSpec(memory_space=pltpu.SEMAPHORE)
ANY = pl.BlockSpec(memory_space=pl.ANY)
EFFECT = pltpu.SideEffectType.DATAFLOW_SIDE_EFFECTING


def _hbm(a):
    return pltpu.with_memory_space_constraint(a, pltpu.HBM)


def _hbm_like(arrs):
    return [pltpu.HBM(a.shape, a.dtype) for a in arrs]


def _dma_sems(count):
    return [pltpu.SemaphoreType.DMA(())] * count


def _ag_copy(lands, send_sems, recv_sems, per):
    def copy(a, k, block, to, src=None):
        px, py, pc = block
        dst = lands[a].at[4 * px + 2 * py + pc]
        return pltpu.make_async_remote_copy(
            src_ref=dst if src is None else src, dst_ref=dst, send_sem=send_sems[a * per + k], recv_sem=recv_sems[a * per + k],
            device_id=to, device_id_type=MESH)
    return copy


def to_wire(sel, w_in, w_gate, w_up, w_out, w_down, name):
    ffpad = FF_SHP - FF_SH

    def body(sel_ref, i_ref, g_ref, u_ref, o_ref, d_ref, oi, og, ou, oo, od):
        del sel_ref
        oi[...] = i_ref[...].astype(oi.dtype)
        oo[...] = o_ref[...].astype(oo.dtype)
        for src, dst in ((g_ref, og), (u_ref, ou), (d_ref, od)):
            dst[:FF_SH, :] = src[...].astype(dst.dtype)
            dst[FF_SH:, :] = jnp.zeros((ffpad, D), dst.dtype)

    shapes_in = [(D, IN_SH), (FF_SH, D), (FF_SH, D), (OUT_SH, D), (FF_SH, D)]
    shapes_out = [(D, IN_SH), (FF_SHP, D), (FF_SHP, D), (OUT_SH, D), (FF_SHP, D)]
    return pl.pallas_call(
        body, name=name,
        grid_spec=pltpu.PrefetchScalarGridSpec(
            num_scalar_prefetch=1, grid=(1,),
            in_specs=[pl.BlockSpec((None,) + s, lambda i, sel_ref: (sel_ref[1], 0, 0)) for s in shapes_in],
            out_specs=[pl.BlockSpec((None,) + s, lambda i, sel_ref: (sel_ref[0], 0, 0)) for s in shapes_out]),
        out_shape=[_sds((NDEV,) + s, WIRE_DTYPE) for s in shapes_out], compiler_params=_params(("arbitrary",)),
    )(sel, w_in, w_gate, w_up, w_out, w_down)


def place_blocks(sel, arrs, name):
    n = len(arrs)

    def body(sel_ref, *refs):
        del sel_ref
        for a in range(n):
            refs[n + a][...] = refs[a][...]

    def whole(a):
        nd = a.ndim
        return pl.BlockSpec(a.shape, lambda i, sel_ref: (0,) * nd)

    def mine(a):
        nd = a.ndim
        return pl.BlockSpec((None,) + a.shape, lambda i, sel_ref: (sel_ref[0],) + (0,) * nd)

    return pl.pallas_call(
        body, name=name,
        grid_spec=pltpu.PrefetchScalarGridSpec(
            num_scalar_prefetch=1, grid=(1,), in_specs=[whole(a) for a in arrs], out_specs=[mine(a) for a in arrs]),
        out_shape=[_sds((NDEV,) + a.shape, a.dtype) for a in arrs], compiler_params=_params(("arbitrary",)),
    )(sel, *arrs)


def ag_start(lands, after, name):
    n = len(lands)
    ns = 4 * n

    def body(*refs):
        lnd = refs[:n]
        send_sems, recv_sems = refs[n + 1:n + 1 + ns], refs[n + 1 + ns:n + 1 + 2 * ns]
        token = refs[-1]
        x, y, c = _place()
        me, sibling = (x, y, c), (x, y, 1 - c)
        chips = [(1 - x, y), (x, 1 - y), (1 - x, 1 - y)]
        copy = _ag_copy(lnd, send_sems, recv_sems, 4)
        for a in range(n):
            copy(a, 0, me, sibling).start()
            for j, chip in enumerate(chips):
                copy(a, 1 + j, me, (*chip, c)).start()
        token[...] = jnp.zeros_like(token)

    outs = pl.pallas_call(
        body, name=name,
        in_specs=[HBM] * n + [ANY],
        out_specs=[SEM] * (2 * ns) + [HBM] * n + [pl.BlockSpec(memory_space=pltpu.VMEM)],
        out_shape=_dma_sems(2 * ns) + _hbm_like(lands) + [_sds((8, 128), F32)],
        input_output_aliases={i: 2 * ns + i for i in range(n)},
        compiler_params=pltpu.CompilerParams(has_side_effects=EFFECT),
    )(*[_hbm(a) for a in lands], after)
    return outs[:ns], outs[ns:2 * ns], outs[2 * ns:2 * ns + n], outs[-1]


def ag_forward(send_sems, recv_sems, lands, after, name):
    n = len(lands)
    n1, n2 = 4 * n, 3 * n

    def body(*refs):
        lnd = refs[:n]
        o = n
        s1, r1 = refs[o:o + n1], refs[o + n1:o + 2 * n1]
        o += 2 * n1 + 1
        s2, r2 = refs[o:o + n2], refs[o + n2:o + 2 * n2]
        token = refs[-1]
        token[...] = jnp.zeros_like(token)
        x, y, c = _place()
        me, sibling = (x, y, c), (x, y, 1 - c)
        chips = [(1 - x, y), (x, 1 - y), (1 - x, 1 - y)]
        copy1 = _ag_copy(lnd, s1, r1, 4)
        copy2 = _ag_copy(lnd, s2, r2, 3)
        for j, chip in enumerate(chips):
            for a in range(n):
                copy1(a, 1 + j, (*chip, c), me).wait_recv()
                copy2(a, j, (*chip, c), sibling).start()
        for a in range(n):
            copy1(a, 0, sibling, me).wait_recv()
            copy1(a, 0, me, sibling).wait_send()
            for j, chip in enumerate(chips):
                copy1(a, 1 + j, me, (*chip, c)).wait_send()

    outs = pl.pallas_call(
        body, name=name,
        in_specs=[HBM] * n + [SEM] * (2 * n1) + [ANY],
        out_specs=[SEM] * (2 * n2) + [HBM] * n + [pl.BlockSpec(memory_space=pltpu.VMEM)],
        out_shape=_dma_sems(2 * n2) + _hbm_like(lands) + [_sds((8, 128), F32)],
        input_output_aliases={i: 2 * n2 + i for i in range(n)},
        compiler_params=pltpu.CompilerParams(has_side_effects=EFFECT),
    )(*lands, *send_sems, *recv_sems, after)
    return outs[:n2], outs[n2:2 * n2], outs[2 * n2:2 * n2 + n], outs[-1]


def ag_finish(send_sems, recv_sems, lands, after, name):
    n = len(lands)
    n2 = 3 * n

    def body(*refs):
        lnd = refs[:n]
        s2, r2 = refs[n:n + n2], refs[n + n2:n + 2 * n2]
        x, y, c = _place()
        me, sibling = (x, y, c), (x, y, 1 - c)
        chips = [(1 - x, y), (x, 1 - y), (1 - x, 1 - y)]
        copy2 = _ag_copy(lnd, s2, r2, 3)
        for a in range(n):
            for j, chip in enumerate(chips):
                copy2(a, j, (*chip, c), sibling).wait_send()
                copy2(a, j, (*chip, 1 - c), me).wait_recv()

    outs = pl.pallas_call(
        body, name=name,
        in_specs=[HBM] * n + [SEM] * (2 * n2) + [ANY],
        out_specs=[HBM] * n, out_shape=_hbm_like(lands),
        input_output_aliases={i: i for i in range(n)},
        compiler_params=pltpu.CompilerParams(has_side_effects=EFFECT),
    )(*lands, *send_sems, *recv_sems, after)
    return list(outs)


def rs_sibling_start(arrs, name):
    n = len(arrs)
    ns = 4 * n
    lands = [lax.empty((4,) + a.shape[1:], a.dtype) for a in arrs]

    def body(*refs):
        ins, lnd = refs[:n], refs[n:2 * n]
        send_sems, recv_sems = refs[2 * n:2 * n + ns], refs[2 * n + ns:2 * n + 2 * ns]
        x, y, c = _place()
        sibling = (x, y, 1 - c)
        for a in range(n):
            for p in range(4):
                pltpu.make_async_remote_copy(
                    src_ref=ins[a].at[2 * p + 1 - c], dst_ref=lnd[a].at[p], send_sem=send_sems[4 * a + p],
                    recv_sem=recv_sems[4 * a + p], device_id=sibling, device_id_type=MESH).start()
        refs[-1][...] = jnp.zeros_like(refs[-1])

    outs = pl.pallas_call(
        body, name=name,
        in_specs=[HBM] * (2 * n), out_specs=[SEM] * (2 * ns) + [HBM] * (2 * n) + [pl.BlockSpec(memory_space=pltpu.VMEM)],
        out_shape=_dma_sems(2 * ns) + _hbm_like(arrs) + _hbm_like(lands) + [_sds((8, 128), F32)],
        input_output_aliases={i: 2 * ns + i for i in range(2 * n)},
        compiler_params=pltpu.CompilerParams(has_side_effects=EFFECT),
    )(*[_hbm(a) for a in arrs], *[_hbm(a) for a in lands])
    return (outs[:ns], outs[ns:2 * ns], outs[2 * ns:2 * ns + n], outs[2 * ns + n:2 * ns + 2 * n]), outs[-1]


def rs_sibling_wait(send_sems, recv_sems, arrs, lands, after, name):
    n = len(arrs)
    ns = 4 * n

    def body(*refs):
        ins, lnd = refs[:n], refs[n:2 * n]
        s, r = refs[2 * n:2 * n + ns], refs[2 * n + ns:2 * n + 2 * ns]
        x, y, c = _place()
        sibling = (x, y, 1 - c)
        for a in range(n):
            for p in range(4):
                cp = pltpu.make_async_remote_copy(
                    src_ref=ins[a].at[2 * p + 1 - c], dst_ref=lnd[a].at[p], send_sem=s[4 * a + p], recv_sem=r[4 * a + p],
                    device_id=sibling, device_id_type=MESH)
                cp.wait_send()
                cp.wait_recv()

    outs = pl.pallas_call(
        body, name=name,
        in_specs=[HBM] * (2 * n) + [SEM] * (2 * ns) + [ANY], out_specs=[HBM] * (2 * n),
        out_shape=_hbm_like(arrs) + _hbm_like(lands),
        input_output_aliases={i: i for i in range(2 * n)},
        compiler_params=pltpu.CompilerParams(has_side_effects=EFFECT),
    )(*arrs, *lands, *send_sems, *recv_sems, after)
    return outs[:n], outs[n:]


def rs_chips_start(parts, name):
    n = len(parts)
    ns = 3 * n
    lands = [lax.empty((3,) + a.shape[1:], a.dtype) for a in parts]

    def body(*refs):
        ins, lnd = refs[:n], refs[n:2 * n]
        send_sems, recv_sems = refs[2 * n:2 * n + ns], refs[2 * n + ns:2 * n + 2 * ns]
        x, y, c = _place()
        chips = [(1 - x, y), (x, 1 - y), (1 - x, 1 - y)]
        for a in range(n):
            for k, (tx, ty) in enumerate(chips):
                pltpu.make_async_remote_copy(
                    src_ref=ins[a].at[2 * tx + ty], dst_ref=lnd[a].at[k], send_sem=send_sems[3 * a + k],
                    recv_sem=recv_sems[3 * a + k], device_id=(tx, ty, c), device_id_type=MESH).start()
        refs[-1][...] = jnp.zeros_like(refs[-1])

    outs = pl.pallas_call(
        body, name=name,
        in_specs=[HBM] * (2 * n), out_specs=[SEM] * (2 * ns) + [HBM] * (2 * n) + [pl.BlockSpec(memory_space=pltpu.VMEM)],
        out_shape=_dma_sems(2 * ns) + _hbm_like(parts) + _hbm_like(lands) + [_sds((8, 128), F32)],
        input_output_aliases={i: 2 * ns + i for i in range(2 * n)},
        compiler_params=pltpu.CompilerParams(has_side_effects=EFFECT),
    )(*[_hbm(a) for a in parts], *[_hbm(a) for a in lands])
    return (outs[:ns], outs[ns:2 * ns], outs[2 * ns:2 * ns + n], outs[2 * ns + n:2 * ns + 2 * n]), outs[-1]


def rs_chips_wait(send_sems, recv_sems, parts, lands, after, name):
    n = len(parts)
    ns = 3 * n

    def body(*refs):
        ins, lnd = refs[:n], refs[n:2 * n]
        s, r = refs[2 * n:2 * n + ns], refs[2 * n + ns:2 * n + 2 * ns]
        x, y, c = _place()
        chips = [(1 - x, y), (x, 1 - y), (1 - x, 1 - y)]
        for a in range(n):
            for k, (tx, ty) in enumerate(chips):
                cp = pltpu.make_async_remote_copy(
                    src_ref=ins[a].at[2 * tx + ty], dst_ref=lnd[a].at[k], send_sem=s[3 * a + k], recv_sem=r[3 * a + k],
                    device_id=(tx, ty, c), device_id_type=MESH)
                cp.wait_send()
                cp.wait_recv()

    outs = pl.pallas_call(
        body, name=name,
        in_specs=[HBM] * (2 * n) + [SEM] * (2 * ns) + [ANY], out_specs=[HBM] * (2 * n),
        out_shape=_hbm_like(parts) + _hbm_like(lands),
        input_output_aliases={i: i for i in range(2 * n)},
        compiler_params=pltpu.CompilerParams(has_side_effects=EFFECT),
    )(*parts, *lands, *send_sems, *recv_sems, after)
    return outs[:n], outs[n:]


def pair_sum(arrs, recv, c, name):
    n = len(arrs)

    def body(c_ref, *refs):
        del c_ref
        for a in range(n):
            refs[2 * n + a][...] = (refs[a][...].astype(F32) + refs[n + a][...].astype(F32)).astype(refs[2 * n + a].dtype)

    mine = [pl.BlockSpec((None,) + a.shape[1:], lambda p, c_ref: (2 * p + c_ref[0], 0, 0)) for a in arrs]
    other = [pl.BlockSpec((None,) + a.shape[1:], lambda p, c_ref: (p, 0, 0)) for a in arrs]
    return pl.pallas_call(
        body, name=name,
        grid_spec=pltpu.PrefetchScalarGridSpec(num_scalar_prefetch=1, grid=(4,), in_specs=mine + other, out_specs=other),
        out_shape=[_sds((4,) + a.shape[1:], a.dtype) for a in arrs], compiler_params=_params(("parallel",)),
    )(c, *arrs, *recv)


def _adamw(w, g, m, v):
    m = ADAM_B1 * m + (1.0 - ADAM_B1) * g
    v = ADAM_B2 * v + (1.0 - ADAM_B2) * jnp.square(g)
    m_hat = m / (1.0 - ADAM_B1 ** ADAM_STEP)
    v_hat = v / (1.0 - ADAM_B2 ** ADAM_STEP)
    return -ADAM_LR * (m_hat / (jnp.sqrt(v_hat) + ADAM_EPS) + ADAM_WD * w), m, v


def adamw_big(recv, sums, chip, w, m, v, tr, name):
    nl, rr, cc = w.shape
    cp = recv[0].shape[2]

    def body(chip_ref, *refs):
        del chip_ref
        rcv, own = refs[:nl], refs[nl:2 * nl]
        w_ref, m_ref, v_ref, g_out, d_out, m_out, v_out = refs[2 * nl:]
        for l in range(nl):
            g = ((own[l][...].astype(F32) + rcv[l][0].astype(F32)) + rcv[l][1].astype(F32)) + rcv[l][2].astype(F32)
            g = g[:, :cc]
            g_out[l] = g
            d_out[l], m_out[l], v_out[l] = _adamw(w_ref[l], g, m_ref[l], v_ref[l])

    blk = pl.BlockSpec((nl, tr, cc), lambda i, chip_ref: (0, i, 0))
    return pl.pallas_call(
        body, name=name,
        grid_spec=pltpu.PrefetchScalarGridSpec(
            num_scalar_prefetch=1, grid=(rr // tr,),
            in_specs=[pl.BlockSpec((3, tr, cp), lambda i, chip_ref: (0, i, 0))] * nl
            + [pl.BlockSpec((None, tr, cp), lambda i, chip_ref: (chip_ref[0], i, 0))] * nl + [blk, blk, blk],
            out_specs=[blk] * 4),
        out_shape=[_sds(w.shape, F32)] * 4, compiler_params=_params(("parallel",)),
    )(chip, *recv, *sums, w, m, v)


SMALL_ROWS = 16


def small_grads(lvec, g_ret, g_mix, g_ffn, g_final, dwa, dwx, name):
    def body(lvec_ref, ret_ref, mix_ref, ffn_ref, fin_ref, dwa_ref, dwx_ref, v_ref, g_ref):
        v_ref[0:9, :] = lvec_ref[0:9, :]
        v_ref[9:10, :] = ret_ref[...]
        for r, src in ((10, mix_ref), (12, ffn_ref), (14, fin_ref)):
            v_ref[r:r + 1, :] = src[:, :D_LRU]
            v_ref[r + 1:r + 2, :] = src[:, D_LRU:]
        for k, src in enumerate((dwa_ref, dwx_ref)):
            for g in range(LRU_BLOCKS):
                rows = slice(LRU_BD * g, LRU_BD * (g + 1))
                g_ref[D_LRU * k + LRU_BD * g:D_LRU * k + LRU_BD * (g + 1), :] = src[rows, rows]

    ins = [lvec, g_ret, g_mix, g_ffn, g_final, dwa, dwx]
    return pl.pallas_call(
        body, name=name, grid=(1,), in_specs=[_full(a.shape) for a in ins],
        out_specs=[_full((SMALL_ROWS, D_LRU)), _full((2 * D_LRU, LRU_BD))],
        out_shape=[_sds((SMALL_ROWS, D_LRU), F32), _sds((2 * D_LRU, LRU_BD), F32)], compiler_params=_params(("arbitrary",)),
    )(*ins)


def sum_devices(arrs, name):
    n = len(arrs)

    def body(*refs):
        for a in range(n):
            acc = refs[a][0]
            for j in range(1, NDEV):
                acc = acc + refs[a][j]
            refs[n + a][...] = acc

    return pl.pallas_call(
        body, name=name, grid=(1,), in_specs=[_full(a.shape) for a in arrs], out_specs=[_full(a.shape[1:]) for a in arrs],
        out_shape=[_sds(a.shape[1:], F32) for a in arrs], compiler_params=_params(("arbitrary",)),
    )(*arrs)


def adamw_small(gs, ws, ms, vs, name):
    n = len(gs)

    def body(*refs):
        for a in range(n):
            g, w, m, v = (refs[k * n + a][...] for k in range(4))
            refs[4 * n + a][...], refs[5 * n + a][...], refs[6 * n + a][...] = _adamw(w, g, m, v)

    specs = [_full(a.shape) for a in ws]
    outs = pl.pallas_call(
        body, name=name, grid=(1,), in_specs=specs * 4, out_specs=specs * 3, out_shape=[_sds(a.shape, F32) for a in ws] * 3,
        compiler_params=_params(("arbitrary",)),
    )(*gs, *ws, *ms, *vs)
    return outs[:n], outs[n:2 * n], outs[2 * n:]


def _block_diag(w):
    eye = jnp.eye(LRU_BLOCKS, dtype=w.dtype)
    return (w[:, :, :, None, :] * eye[None, :, None, :, None]).reshape(w.shape[0], D_LRU, D_LRU)


REP_NAMES = ["norm_mix", "conv_b", "gate_a_w", "gate_a_b", "gate_x_w", "gate_x_b", "lru_lambda", "lru_out_norm",
             "ret_out_norm", "norm_ffn", "norm_final"]


def kernel(x, meta_tokens, norm_mix, w_in, conv_w, conv_b, gate_a_w, gate_a_b, gate_x_w, gate_x_b, lru_lambda, lru_out_norm, ret_out_norm, w_out, norm_ffn, w_gate, w_up, w_down, norm_final, loss_target, m_meta_tokens, m_norm_mix, m_w_in, m_conv_w, m_conv_b, m_gate_a_w, m_gate_a_b, m_gate_x_w, m_gate_x_b, m_lru_lambda, m_lru_out_norm, m_ret_out_norm, m_w_out, m_norm_ffn, m_w_gate, m_w_up, m_w_down, m_norm_final, v_meta_tokens, v_norm_mix, v_w_in, v_conv_w, v_conv_b, v_gate_a_w, v_gate_a_b, v_gate_x_w, v_gate_x_b, v_lru_lambda, v_lru_out_norm, v_ret_out_norm, v_w_out, v_norm_ffn, v_w_gate, v_w_up, v_w_down, v_norm_final):
    xi, yi, ci = _place()
    dev = 4 * xi + 2 * yi + ci
    c_arr = jnp.reshape(ci, (1,)).astype(jnp.int32)
    dev_arr = jnp.reshape(dev, (1,)).astype(jnp.int32)

    meta_g, conv_g = all_gather([meta_tokens, conv_w], "ag_small")
    meta_full = jnp.transpose(meta_g, (1, 0, 2)).reshape(N_META, D)
    conv_full = jnp.transpose(conv_g, (1, 2, 0, 3)).reshape(DEPTH, CONV_W, D_LRU)
    tr_ = lambda a: jnp.transpose(a, (0, 2, 1))
    w_gate_t, m_w_gate_t, v_w_gate_t = tr_(w_gate), tr_(m_w_gate), tr_(v_w_gate)
    w_up_t, m_w_up_t, v_w_up_t = tr_(w_up), tr_(m_w_up), tr_(v_w_up)
    level1 = []
    token = meta_g
    for l in range(DEPTH):
        sel = jnp.stack([dev, jnp.int32(l)]).astype(jnp.int32)
        lands = to_wire(sel, w_in, w_gate_t, w_up_t, w_out, w_down, "to_wire")
        s1, r1, lands, token = ag_start(lands, token, f"ag_start_{l}")
        level1.append((s1, r1, lands))

    def as_weights(gi, gg, gu, go, gd):
        return dict(w_in=gi, w_gate=gg.reshape(D_FFP, D), w_up=gu.reshape(D_FFP, D), w_out=go.reshape(D, D),
                    w_down=gd.reshape(D_FFP, D))

    tables = _ret_tables()
    row = lambda a: a.reshape(1, -1)

    h = jnp.concatenate([jnp.zeros((PAD, D), F32), meta_full, x[0]], axis=0)
    saved, gathered = [], []
    s1, r1, lands = level1[0]
    s2, r2, first, order = ag_forward(s1[:4], r1[:4], lands[:1], token, "ag_forward_0_w_in")
    w_in_next = ag_finish(s2, r2, first, h, "ag_finish_0_w_in")[0]
    wa_dense = _block_diag(gate_a_w).astype(MXU_DTYPE)
    wx_dense = _block_diag(gate_x_w).astype(MXU_DTYPE)
    for l in range(DEPTH):
        small = dict(cw=conv_full[l], cb=row(conv_b[l]), wa=wa_dense[l], ba=row(gate_a_b[l]),
                     wx=wx_dense[l], bx=row(gate_x_b[l]), lam=row(lru_lambda[l]),
                     gain=row(lru_out_norm[l]))
        s1, r1, lands = level1[l]
        hn1 = rmsnorm_fwd(h, row(norm_mix[l]), "rms_fwd")
        proj = mm_blocked_nn(hn1, w_in_next, F32, "proj")
        ylru, hst = lru_fwd(proj, name="lru_fwd", **small)
        s2, r2, rest, order = ag_forward(s1[4:], r1[4:], lands[1:], ylru, f"ag_forward_{l}_rest")
        ymix, states = ret_fwd(proj, ylru, tables, row(ret_out_norm[l]), order, "ret_fwd")
        w = as_weights(w_in_next, *ag_finish(s2, r2, rest, ymix, f"ag_finish_{l}_rest"))
        gathered.append(w)
        h_mid = mm_nn_res(ymix, w["w_out"], h, order, "out_proj")
        hn2 = rmsnorm_fwd(h_mid, row(norm_ffn[l]), "rms_fwd")
        gate, up, act = ffn_up(hn2, w["w_gate"], w["w_up"], "ffn_up")
        if l + 1 < DEPTH:
            s1n, r1n, landsn = level1[l + 1]
            s2, r2, first, order = ag_forward(s1n[:4], r1n[:4], landsn[:1], act, f"ag_forward_{l + 1}_w_in")
        h_out = mm_nn_res(act, w["w_down"], h_mid, order, "ffn_down")
        if l + 1 < DEPTH:
            w_in_next = ag_finish(s2, r2, first, h_out, f"ag_finish_{l + 1}_w_in")[0]
        saved.append(dict(h=h, hn1=hn1, proj=proj, hst=hst, states=states, ymix=ymix, h_mid=h_mid, hn2=hn2, gate=gate, up=up,
                          act=act, small=small))
        h = h_out

    loss_p, dh, dh_b, g_norm_final = loss_head(h, row(norm_final), loss_target[0], "loss_head")
    loss = lax.psum(loss_p[0, 0], ("x", "y", "c"))

    small_v = [None] * DEPTH
    small_w = [None] * DEPTH
    inflight = []
    sib = None
    order = loss_p

    def sibling_done(l, tag, names, sib, after):
        parts, got = rs_sibling_wait(*sib, after, f"rs_sibling_wait_{tag}")
        sums = pair_sum(parts, got, c_arr, "pair_sum")
        flying, started = rs_chips_start(sums, f"rs_chips_start_{tag}")
        inflight.append((l, tag, names, flying))
        return started

    for l in reversed(range(DEPTH)):
        w, s = gathered[l], saved[l]
        dgate, dup = ffn_down_bwd(dh_b, w["w_down"], s["gate"], s["up"], order, "ffn_down_bwd")
        dwd = mm_tn(s["act"], dh_b, PAIR, order, "dw_down").reshape(NDEV, FF_SHP, D)
        dwg, dwu = (g.reshape(NDEV, FF_SHP, D) for g in mm_tn_two(dgate, dup, s["hn2"], PAIR, order, "dw_rows"))
        ffn_sib, order = rs_sibling_start([dwg, dwu, dwd], f"rs_sibling_start_{l}_ffn")
        dhn2 = mm_rows_nn([(dgate, w["w_gate"]), (dup, w["w_up"])], order, "ffn_up_bwd")
        if sib is not None:
            order = sibling_done(l + 1, f"{l + 1}_mix", ("w_in", "w_out"), sib, dhn2)
        dh_mid, dh_mid_b, g_norm_ffn = rmsnorm_bwd(s["h_mid"], row(norm_ffn[l]), dhn2, dh, "rms_bwd")
        dymix, dwo = out_proj_bwd(dh_mid_b, w["w_out"], s["ymix"], order, "out_proj_bwd")
        dwo = dwo.reshape(NDEV, OUT_SH, D)
        order = sibling_done(l, f"{l}_ffn", ("w_gate", "w_up", "w_down"), ffn_sib, dymix)
        dxg, lvec, dwa, dwx = lru_bwd(s["proj"], s["hst"], dymix, after=order, name="lru_bwd", **s["small"])
        dproj, g_ret_norm = ret_bwd(s["proj"], s["states"], dymix, dxg, tables, row(ret_out_norm[l]), "ret_bwd")
        dwi = mm_tn_blocked(s["hn1"], dproj, "dw_blocked")
        dhn1 = mm_blocked_nt([(dproj, w["w_in"])], order, "proj_bwd")
        dh, dh_b, g_norm_mix = rmsnorm_bwd(s["h"], row(norm_mix[l]), dhn1, dh_mid, "rms_bwd")

        g_fin = g_norm_final if l == 0 else jnp.zeros((1, D), F32)
        small_v[l], small_w[l] = small_grads(lvec, g_ret_norm, g_norm_mix, g_norm_ffn, g_fin, dwa, dwx, "small_grads")
        sib, order = rs_sibling_start([dwi, dwo], f"rs_sibling_start_{l}_mix")
        if l == 1:
            early = place_blocks(dev_arr, [jnp.stack(small_v[1:]), jnp.stack(small_w[1:])], "place_grads")
            early_sems = ag_start(early, order, "ag_start_grads")
            order = early_sems[3]

    grad_x = dh[X0:][None]
    g_meta = dh[PAD:X0]

    late = all_gather([small_v[0], small_w[0], g_meta], "ag_grads")
    s2, r2, lands, _ = ag_forward(early_sems[0], early_sems[1], early_sems[2], dh, "ag_forward_grads")
    gath_early = ag_finish(s2, r2, lands, late[0], "ag_finish_grads")
    sibling_done(0, "0_mix", ("w_in", "w_out"), sib, late[0])
    v0, w0, meta_sum, v123, w123 = sum_devices(list(late) + list(gath_early), "sum_devices")
    vecs = jnp.concatenate([v0[None], v123])
    gws = jnp.concatenate([w0[None], w123])
    blocks = (DEPTH, LRU_BLOCKS, LRU_BD)
    small_g = dict(
        conv_w=lax.dynamic_slice_in_dim(vecs[:, 0:CONV_W], dev * (D_LRU // NDEV), D_LRU // NDEV, axis=2),
        conv_b=vecs[:, 4], gate_a_b=vecs[:, 5].reshape(blocks), gate_x_b=vecs[:, 6].reshape(blocks),
        lru_lambda=vecs[:, 7], lru_out_norm=vecs[:, 8], ret_out_norm=vecs[:, 9],
        norm_mix=vecs[:, 10:12].reshape(DEPTH, D), norm_ffn=vecs[:, 12:14].reshape(DEPTH, D),
        norm_final=v0[14:16].reshape(1, D),
        gate_a_w=gws[:, :D_LRU].reshape(blocks + (LRU_BD,)), gate_x_w=gws[:, D_LRU:].reshape(blocks + (LRU_BD,)),
        meta_tokens=lax.dynamic_slice_in_dim(meta_sum, dev * (D // NDEV), D // NDEV, axis=1))
    given = dict(norm_mix=(norm_mix, m_norm_mix, v_norm_mix), conv_b=(conv_b, m_conv_b, v_conv_b),
                 gate_a_w=(gate_a_w, m_gate_a_w, v_gate_a_w), gate_a_b=(gate_a_b, m_gate_a_b, v_gate_a_b),
                 gate_x_w=(gate_x_w, m_gate_x_w, v_gate_x_w), gate_x_b=(gate_x_b, m_gate_x_b, v_gate_x_b),
                 lru_lambda=(lru_lambda, m_lru_lambda, v_lru_lambda), lru_out_norm=(lru_out_norm, m_lru_out_norm, v_lru_out_norm),
                 ret_out_norm=(ret_out_norm, m_ret_out_norm, v_ret_out_norm), norm_ffn=(norm_ffn, m_norm_ffn, v_norm_ffn),
                 norm_final=tuple(a.reshape(1, D) for a in (norm_final, m_norm_final, v_norm_final)),
                 conv_w=(conv_w, m_conv_w, v_conv_w), meta_tokens=(meta_tokens, m_meta_tokens, v_meta_tokens))
    small_names = REP_NAMES + ["conv_w", "meta_tokens"]
    upd = adamw_small([small_g[n] for n in small_names], *[[given[n][k] for n in small_names] for k in range(3)],
                      "adamw_small")
    small_out = [dict(zip(small_names, u)) for u in upd]
    for d_ in [small_g] + small_out:
        d_["norm_final"] = d_["norm_final"].reshape(D)

    arrived = {}

    def wait_for(entries, after):
        for l, tag, names, flying in entries:
            sums, recv = rs_chips_wait(*flying, after, f"rs_chips_wait_{tag}")
            for i, n in enumerate(names):
                arrived[l, n] = (recv[i], sums[i])

    chip = jnp.reshape(2 * xi + yi, (1,)).astype(jnp.int32)

    def finish(wname, w_, m_, v_, tr):
        return adamw_big([arrived[l, wname][0] for l in range(DEPTH)], [arrived[l, wname][1] for l in range(DEPTH)], chip,
                         w_, m_, v_, tr, "adamw_" + wname)

    wait_for(inflight[:-1], upd[0][0])
    o_gate = [tr_(o) for o in finish("w_gate", w_gate_t, m_w_gate_t, v_w_gate_t, 32)]
    o_up = [tr_(o) for o in finish("w_up", w_up_t, m_w_up_t, v_w_up_t, 32)]
    o_down = finish("w_down", w_down, m_w_down, v_w_down, 32)
    wait_for(inflight[-1:], o_down[0])
    o_in = finish("w_in", w_in, m_w_in, v_w_in, 256)
    o_out = finish("w_out", w_out, m_w_out, v_w_out, 64)

    bigs = dict(w_in=o_in, w_out=o_out, w_gate=o_gate, w_up=o_up, w_down=o_down)
    order = ["meta_tokens", "norm_mix", "w_in", "conv_w", "conv_b", "gate_a_w", "gate_a_b", "gate_x_w", "gate_x_b", "lru_lambda",
             "lru_out_norm", "ret_out_norm", "w_out", "norm_ffn", "w_gate", "w_up", "w_down", "norm_final"]
    grads = [bigs[n][0] if n in bigs else small_g[n] for n in order]
    rest = [[bigs[n][k + 1] if n in bigs else small_out[k][n] for n in order] for k in range(3)]
    return (loss, grad_x, *grads, *rest[0], *rest[1], *rest[2])
```

```python
import functools

import numpy as np
import jax
import jax.numpy as jnp
from jax import lax
from jax.experimental import pallas as pl
from jax.experimental.pallas import tpu as pltpu

F32, BF16 = jnp.float32, jnp.bfloat16
MXU_DTYPE = BF16
WIRE_DTYPE = BF16

D = 1024
SEQ = 2048
DEPTH = 4
N_META = 16
CH = 128
GROUP = 8
PAD = (-(SEQ + N_META)) % CH
T = SEQ + N_META + PAD
NCH = T // CH
X0 = PAD + N_META
D_LRU = 512
LRU_BLOCKS = 8
LRU_BD = 64
CONV_W = 4
LRU_C = 8.0
D_RET = 512
HEADS = 4
HD = 128
ROPE_BASE = 10000.0
D_IN = 3072
D_FF = 2816
NDEV = 8
IN_SH = D_IN // NDEV
FF_SH = D_FF // NDEV
FF_SHP = 384
D_FFP = NDEV * FF_SHP
OUT_SH = D // NDEV
EPS = 1e-6
TM = 272
TR = 1088
VMEM_LIMIT = 56 * 2**20
MESH = pl.DeviceIdType.MESH

ADAM_LR, ADAM_B1, ADAM_B2, ADAM_EPS, ADAM_WD, ADAM_STEP = 0.001, 0.9, 0.999, 1e-08, 0.01, 10

NN = ((1,), (0,))
NT = ((1,), (1,))
TN = ((0,), (0,))


def _dot(a, b, dims):
    return lax.dot_general(a.astype(MXU_DTYPE), b.astype(MXU_DTYPE), (dims, ((), ())), preferred_element_type=F32)


def _sds(shape, dtype):
    return jax.ShapeDtypeStruct(shape, dtype)


def _params(sem=None):
    return pltpu.CompilerParams(dimension_semantics=sem, vmem_limit_bytes=VMEM_LIMIT)


def _full(shape):
    n = len(shape)
    return pl.BlockSpec(shape, lambda *_: (0,) * n)


def rmsnorm_fwd(h, gain, name):
    def body(h_ref, g_ref, o_ref):
        x = h_ref[...]
        ms = jnp.mean(x * x, axis=-1, keepdims=True)
        o_ref[...] = (x * lax.rsqrt(ms + EPS) * g_ref[...]).astype(o_ref.dtype)

    return pl.pallas_call(
        body, name=name, grid=(T // TM,),
        in_specs=[pl.BlockSpec((TM, D), lambda i: (i, 0)), _full((1, D))],
        out_specs=pl.BlockSpec((TM, D), lambda i: (i, 0)),
        out_shape=_sds((T, D), MXU_DTYPE), compiler_params=_params(("parallel",)),
    )(h, gain)


def rmsnorm_bwd(h, gain, dhn, dres, name):
    def body(h_ref, g_ref, dhn_ref, dres_ref, dh_ref, dhb_ref, dg_ref):
        x = h_ref[...]
        rstd = lax.rsqrt(jnp.mean(x * x, axis=-1, keepdims=True) + EPS)
        xhat = x * rstd
        dy = dhn_ref[...]
        dyg = dy * g_ref[...]
        dh = dres_ref[...] + rstd * (dyg - xhat * jnp.mean(dyg * xhat, axis=-1, keepdims=True))
        dh_ref[...] = dh
        dhb_ref[...] = dh.astype(dhb_ref.dtype)

        @pl.when(pl.program_id(0) == 0)
        def _():
            dg_ref[...] = jnp.zeros_like(dg_ref)
        dg_ref[...] += jnp.sum(dy * xhat, axis=0, keepdims=True)

    row = pl.BlockSpec((TM, D), lambda i: (i, 0))
    return pl.pallas_call(
        body, name=name, grid=(T // TM,),
        in_specs=[row, _full((1, D)), row, row],
        out_specs=[row, row, _full((1, D))],
        out_shape=[_sds((T, D), F32), _sds((T, D), MXU_DTYPE), _sds((1, D), F32)], compiler_params=_params(("arbitrary",)),
    )(h, gain, dhn, dres)


def loss_head(h, gain, target, name):
    def body(h_ref, g_ref, t_ref, loss_ref, dh_ref, dhb_ref, dg_ref):
        i = pl.program_id(0)

        @pl.when(i == 0)
        def _():
            loss_ref[...] = jnp.zeros_like(loss_ref)
            dg_ref[...] = jnp.zeros_like(dg_ref)
            dh_ref[...] = jnp.zeros_like(dh_ref)
            dhb_ref[...] = jnp.zeros_like(dhb_ref)

        @pl.when(i > 0)
        def _():
            x = h_ref[...]
            g = g_ref[...]
            rstd = lax.rsqrt(jnp.mean(x * x, axis=-1, keepdims=True) + EPS)
            xhat = x * rstd
            err = xhat * g - t_ref[...]
            loss_ref[...] += 0.5 * jnp.sum(jnp.mean(err * err, axis=-1, keepdims=True), axis=0, keepdims=True)
            dy = err * (1.0 / D)
            dyg = dy * g
            dh = rstd * (dyg - xhat * jnp.mean(dyg * xhat, axis=-1, keepdims=True))
            dh_ref[...] = dh
            dhb_ref[...] = dh.astype(dhb_ref.dtype)
            dg_ref[...] += jnp.sum(dy * xhat, axis=0, keepdims=True)

    row = pl.BlockSpec((CH, D), lambda i: (i, 0))
    return pl.pallas_call(
        body, name=name, grid=(NCH,),
        in_specs=[row, _full((1, D)), pl.BlockSpec((CH, D), lambda i: (jnp.maximum(i - 1, 0), 0))],
        out_specs=[_full((8, 128)), row, row, _full((1, D))],
        out_shape=[_sds((8, 128), F32), _sds((T, D), F32), _sds((T, D), MXU_DTYPE), _sds((1, D), F32)],
        compiler_params=_params(("arbitrary",)),
    )(h, gain, target)


PAIR = 2 * IN_SH
NPAIR = NDEV // 2
BN = 256


def _pair_cols(w_ref):
    return jnp.concatenate([w_ref[0], w_ref[1]], axis=1)


W_PAIR = lambda k: pl.BlockSpec((2, k, IN_SH), lambda j: (j, 0, 0))
COLS_PAIR = pl.BlockSpec((T, PAIR), lambda j: (0, j))
ANYSPEC = pl.BlockSpec(memory_space=pl.ANY)


def mm_blocked_nn(a, w, out_dtype, name):
    k = a.shape[1]

    def body(a_ref, w_ref, o_ref):
        o_ref[...] = _dot(a_ref[...], _pair_cols(w_ref), NN).astype(o_ref.dtype)

    return pl.pallas_call(
        body, name=name, grid=(NPAIR,),
        in_specs=[_full((T, k)), W_PAIR(k)], out_specs=COLS_PAIR,
        out_shape=_sds((T, NDEV * IN_SH), out_dtype), compiler_params=_params(("parallel",)),
    )(a, w)


def mm_nn_res(a, w, res, after, name):
    k = a.shape[1]

    def body(a_ref, w_ref, r_ref, after_ref, o_ref):
        del after_ref
        o_ref[...] = r_ref[...] + _dot(a_ref[...], w_ref[...], NN)

    col = pl.BlockSpec((T, BN), lambda j: (0, j))
    return pl.pallas_call(
        body, name=name, grid=(D // BN,),
        in_specs=[_full((T, k)), pl.BlockSpec((k, BN), lambda j: (0, j)), col, ANYSPEC], out_specs=col,
        out_shape=_sds((T, D), F32), compiler_params=_params(("parallel",)),
    )(a, w, res, after)


def ffn_up(hn, wg, wu, name):
    def body(a_ref, wg_ref, wu_ref, g_ref, u_ref, act_ref):
        a = a_ref[...]
        g = _dot(a, wg_ref[...], NT)
        u = _dot(a, wu_ref[...], NT)
        g_ref[...] = g.astype(g_ref.dtype)
        u_ref[...] = u.astype(u_ref.dtype)
        act_ref[...] = (jax.nn.silu(g) * u).astype(act_ref.dtype)

    wspec = pl.BlockSpec((PAIR, D), lambda j, i: (j, 0))
    ospec = pl.BlockSpec((TR, PAIR), lambda j, i: (i, j))
    return pl.pallas_call(
        body, name=name, grid=(NPAIR, T // TR),
        in_specs=[pl.BlockSpec((TR, D), lambda j, i: (i, 0)), wspec, wspec], out_specs=[ospec] * 3,
        out_shape=[_sds((T, D_FFP), MXU_DTYPE)] * 3, compiler_params=_params(("parallel", "parallel")),
    )(hn, wg, wu)


def ffn_down_bwd(dh, wd, gate, up, after, name):
    def body(dh_ref, wd_ref, g_ref, u_ref, after_ref, dg_ref, du_ref):
        del after_ref
        dact = _dot(dh_ref[...], wd_ref[...], NT)
        g = g_ref[...].astype(F32)
        u = u_ref[...].astype(F32)
        sg = jax.nn.sigmoid(g)
        dg_ref[...] = (dact * u * (sg * (1.0 + g * (1.0 - sg)))).astype(dg_ref.dtype)
        du_ref[...] = (dact * (g * sg)).astype(du_ref.dtype)

    blk = pl.BlockSpec((TR, PAIR), lambda j, i: (i, j))
    return pl.pallas_call(
        body, name=name, grid=(NPAIR, T // TR),
        in_specs=[pl.BlockSpec((TR, D), lambda j, i: (i, 0)), pl.BlockSpec((PAIR, D), lambda j, i: (j, 0)), blk, blk, ANYSPEC],
        out_specs=[blk, blk],
        out_shape=[_sds((T, D_FFP), MXU_DTYPE)] * 2, compiler_params=_params(("parallel", "parallel")),
    )(dh, wd, gate, up, after)


def mm_blocked_nt(pairs, after, name):
    n = len(pairs)

    def body(*refs):
        o_ref = refs[2 * n + 1]

        @pl.when(pl.program_id(1) == 0)
        def _():
            o_ref[...] = jnp.zeros_like(o_ref)
        for p in range(n):
            o_ref[...] += _dot(refs[2 * p][...], _pair_cols(refs[2 * p + 1]), NT)

    specs, args = [], []
    for a, w in pairs:
        specs += [pl.BlockSpec((TR, PAIR), lambda i, j: (i, j)), pl.BlockSpec((2, D, IN_SH), lambda i, j: (j, 0, 0))]
        args += [a, w]
    return pl.pallas_call(
        body, name=name, grid=(T // TR, NPAIR), in_specs=specs + [ANYSPEC],
        out_specs=pl.BlockSpec((TR, D), lambda i, j: (i, 0)),
        out_shape=_sds((T, D), F32), compiler_params=_params(("parallel", "arbitrary")),
    )(*args, after)


def mm_tn_two(a1, a2, b, bm, after, name):
    m = a1.shape[1]

    def body(a1_ref, a2_ref, b_ref, after_ref, o1_ref, o2_ref):
        del after_ref
        b = b_ref[...]
        o1_ref[...] = _dot(a1_ref[...], b, TN).astype(o1_ref.dtype)
        o2_ref[...] = _dot(a2_ref[...], b, TN).astype(o2_ref.dtype)

    blk = pl.BlockSpec((T, bm), lambda i: (0, i))
    out = pl.BlockSpec((bm, D), lambda i: (i, 0))
    return pl.pallas_call(
        body, name=name, grid=(m // bm,),
        in_specs=[blk, blk, _full((T, D)), ANYSPEC], out_specs=[out, out],
        out_shape=[_sds((m, D), WIRE_DTYPE)] * 2, compiler_params=_params(("parallel",)),
    )(a1, a2, b, after)


def out_proj_bwd(dh, w, ymix, after, name):
    def body(dh_ref, w_ref, y_ref, after_ref, dy_ref, dw_ref):
        del after_ref
        dh_ = dh_ref[...]
        dy_ref[...] = _dot(dh_, w_ref[...], NT)
        dw_ref[...] = _dot(y_ref[...], dh_, TN).astype(dw_ref.dtype)

    return pl.pallas_call(
        body, name=name, grid=(D // BN,),
        in_specs=[_full((T, D)), pl.BlockSpec((BN, D), lambda j: (j, 0)), pl.BlockSpec((T, BN), lambda j: (0, j)), ANYSPEC],
        out_specs=[pl.BlockSpec((T, BN), lambda j: (0, j)), pl.BlockSpec((BN, D), lambda j: (j, 0))],
        out_shape=[_sds((T, D), F32), _sds((D, D), WIRE_DTYPE)], compiler_params=_params(("parallel",)),
    )(dh, w, ymix, after)


def mm_rows_nn(pairs, after, name):
    n = len(pairs)

    def body(*refs):
        o_ref = refs[2 * n + 1]

        @pl.when(pl.program_id(1) == 0)
        def _():
            o_ref[...] = jnp.zeros_like(o_ref)
        for p in range(n):
            o_ref[...] += _dot(refs[2 * p][...], refs[2 * p + 1][...], NN)

    specs, args = [], []
    for a, w in pairs:
        specs += [pl.BlockSpec((TR, PAIR), lambda i, j: (i, j)), pl.BlockSpec((PAIR, D), lambda i, j: (j, 0))]
        args += [a, w]
    return pl.pallas_call(
        body, name=name, grid=(T // TR, NPAIR), in_specs=specs + [ANYSPEC],
        out_specs=pl.BlockSpec((TR, D), lambda i, j: (i, 0)),
        out_shape=_sds((T, D), F32), compiler_params=_params(("parallel", "arbitrary")),
    )(*args, after)


def mm_tn_blocked(a, b, name):
    def body(a_ref, b_ref, o_ref):
        o = _dot(a_ref[...], b_ref[...], TN).astype(o_ref.dtype)
        o_ref[0] = o[:, :IN_SH]
        o_ref[1] = o[:, IN_SH:]

    return pl.pallas_call(
        body, name=name, grid=(NPAIR,),
        in_specs=[_full((T, D)), COLS_PAIR], out_specs=W_PAIR(D),
        out_shape=_sds((NDEV, D, IN_SH), WIRE_DTYPE), compiler_params=_params(("parallel",)),
    )(a, b)


def mm_tn(a, b, bm, after, name):
    m = a.shape[1]

    def body(a_ref, b_ref, after_ref, o_ref):
        del after_ref
        o_ref[...] = _dot(a_ref[...], b_ref[...], TN).astype(o_ref.dtype)

    return pl.pallas_call(
        body, name=name, grid=(m // bm,),
        in_specs=[pl.BlockSpec((T, bm), lambda i: (0, i)), _full((T, D)), ANYSPEC],
        out_specs=pl.BlockSpec((bm, D), lambda i: (i, 0)),
        out_shape=_sds((m, D), WIRE_DTYPE), compiler_params=_params(("parallel",)),
    )(a, b, after)


def _softplus_neg(lam):
    return jnp.maximum(-lam, 0.0) + jnp.log1p(jnp.exp(-jnp.abs(lam)))


def _lru_gates(pa, px, xc, lam):
    r = jax.nn.sigmoid(pa)
    ig = jax.nn.sigmoid(px)
    sp = _softplus_neg(lam)
    log_a = -LRU_C * r * sp
    a = jnp.exp(log_a)
    mult = jnp.sqrt(-jnp.tanh(log_a) * (a * a + 1.0))
    return a, mult * (ig * xc), (r, ig, sp, mult)


def _lru_gates_vjp(da, db, xc, lam, a, r, ig, sp, mult):
    dmult = db * (ig * xc)
    du = db * mult
    dlog_a = da * a - dmult * (a * a) / mult
    dr = dlog_a * (-LRU_C * sp)
    dlam = jnp.sum(dlog_a * (-LRU_C * r), axis=0, keepdims=True) * (-jax.nn.sigmoid(-lam))
    dpa = dr * (r * (1.0 - r))
    dpx = (du * xc) * (ig * (1.0 - ig))
    return dpa, dpx, du * ig, dlam


def _lru_out(h, g, gain):
    z = h * jax.nn.gelu(g)
    return z * lax.rsqrt(jnp.mean(z * z, axis=-1, keepdims=True) + EPS) * gain


def _conv_taps(x, xprev, row):
    taps = [x]
    for s in range(1, CONV_W):
        taps.append(jnp.where(row < s, pltpu.roll(xprev, s, 0), pltpu.roll(x, s, 0)))
    return taps


def _conv(taps, cw_ref, cb):
    xc = cb + cw_ref[CONV_W - 1:CONV_W, :] * taps[0]
    for s in range(1, CONV_W):
        xc = xc + cw_ref[CONV_W - 1 - s:CONV_W - s, :] * taps[s]
    return xc


def lru_fwd(proj, cw, cb, wa, ba, wx, bx, lam, gain, name):
    def body(x_ref, g_ref, cw_ref, cb_ref, wa_ref, ba_ref, wx_ref, bx_ref, lam_ref, gain_ref,
             y_ref, h_ref, xprev_scr, a_scr, b_scr, carry_scr):
        i = pl.program_id(0)

        @pl.when(i == 0)
        def _():
            xprev_scr[...] = jnp.zeros_like(xprev_scr)
            carry_scr[...] = jnp.zeros_like(carry_scr)

        x = x_ref[...]
        row = lax.broadcasted_iota(jnp.int32, (CH, D_LRU), 0)
        xc = _conv(_conv_taps(x, xprev_scr[...], row), cw_ref, cb_ref[...])
        pa = _dot(xc, wa_ref[...], NN) + ba_ref[...]
        px = _dot(xc, wx_ref[...], NN) + bx_ref[...]
        a, b, _ = _lru_gates(pa, px, xc, lam_ref[...])
        b = jnp.where(i * CH + row >= PAD, b, 0.0)
        pos = row & (GROUP - 1)
        for d in (1, 2, 4):
            inside = pos >= d
            b = jnp.where(inside, a * pltpu.roll(b, d, 0) + b, b)
            a = jnp.where(inside, a * pltpu.roll(a, d, 0), a)
        a_scr[...] = a
        b_scr[...] = b
        h = carry_scr[...]
        for g in range(CH // GROUP):
            rows = slice(GROUP * g, GROUP * (g + 1))
            h_ref[rows, :] = b_scr[rows, :] + a_scr[rows, :] * h
            h = h_ref[GROUP * (g + 1) - 1:GROUP * (g + 1), :]
        carry_scr[...] = h
        xprev_scr[...] = x
        y_ref[...] = _lru_out(h_ref[...], g_ref[...], gain_ref[...]).astype(y_ref.dtype)

    vec = _full((1, D_LRU))
    mat = _full((D_LRU, D_LRU))
    return pl.pallas_call(
        body, name=name, grid=(NCH,),
        in_specs=[pl.BlockSpec((CH, D_LRU), lambda i: (i, 0)), pl.BlockSpec((CH, D_LRU), lambda i: (i, 1)),
                  _full((CONV_W, D_LRU)), vec, mat, vec, mat, vec, vec, vec],
        out_specs=[pl.BlockSpec((CH, D_LRU), lambda i: (i, 0)), pl.BlockSpec((CH, D_LRU), lambda i: (i, 0))],
        out_shape=[_sds((T, D_LRU), MXU_DTYPE), _sds((T, D_LRU), F32)],
        scratch_shapes=[pltpu.VMEM((CH, D_LRU), F32), pltpu.VMEM((CH, D_LRU), F32), pltpu.VMEM((CH, D_LRU), F32),
                        pltpu.VMEM((1, D_LRU), F32)],
        compiler_params=_params(("arbitrary",)),
    )(proj, proj, cw, cb, wa, ba, wx, bx, lam, gain)


LRU_VEC_ROWS = 16


def lru_bwd(proj, hst, dymix, cw, cb, wa, ba, wx, bx, lam, gain, after, name):
    last = NCH - 1

    def body(x_ref, xp_ref, g_ref, h_ref, hp_ref, dy_ref, cw_ref, cb_ref, wa_ref, ba_ref, wx_ref, bx_ref, lam_ref,
             gain_ref, after_ref, dxg_ref, vec_ref, dwa_ref, dwx_ref, a_scr, dh_scr, g_scr, carry_scr, dxcn_scr):
        del after_ref
        i = pl.program_id(0)
        ib = last - i

        @pl.when(i == 0)
        def _():
            carry_scr[...] = jnp.zeros_like(carry_scr)
            dxcn_scr[...] = jnp.zeros_like(dxcn_scr)
            vec_ref[...] = jnp.zeros_like(vec_ref)
            dwa_ref[...] = jnp.zeros_like(dwa_ref)
            dwx_ref[...] = jnp.zeros_like(dwx_ref)

        x = x_ref[...]
        row = lax.broadcasted_iota(jnp.int32, (CH, D_LRU), 0)
        valid = ib * CH + row >= PAD
        taps = _conv_taps(x, xp_ref[...], row)
        xc = _conv(taps, cw_ref, cb_ref[...])
        pa = _dot(xc, wa_ref[...], NN) + ba_ref[...]
        px = _dot(xc, wx_ref[...], NN) + bx_ref[...]
        a, _, gate_parts = _lru_gates(pa, px, xc, lam_ref[...])
        h = h_ref[...]
        _, vjp_out = jax.vjp(_lru_out, h, g_ref[...], gain_ref[...])
        dh, dg, dgain = vjp_out(dy_ref[...].astype(F32))
        pos = row & (GROUP - 1)
        an = jnp.where(row == CH - 1, 1.0, pltpu.roll(a, CH - 1, 0))
        gl = dh
        for d in (1, 2, 4):
            inside = pos < GROUP - d
            gl = jnp.where(inside, an * pltpu.roll(gl, CH - d, 0) + gl, gl)
            an = jnp.where(inside, an * pltpu.roll(an, CH - d, 0), an)
        a_scr[...] = an
        dh_scr[...] = gl
        c = carry_scr[...]
        for g in range(CH // GROUP - 1, -1, -1):
            rows = slice(GROUP * g, GROUP * (g + 1))
            g_scr[rows, :] = dh_scr[rows, :] + a_scr[rows, :] * c
            c = g_scr[GROUP * g:GROUP * g + 1, :]
        gg = g_scr[...]
        a_scr[0:GROUP, :] = a[0:GROUP, :]
        carry_scr[...] = a_scr[0:1, :] * c
        hprev = jnp.where(row < 1, pltpu.roll(hp_ref[...], 1, 0), pltpu.roll(h, 1, 0))
        da = jnp.where(valid, gg * hprev, 0.0)
        db = jnp.where(valid, gg, 0.0)
        dpa, dpx, dxc, dlam = _lru_gates_vjp(da, db, xc, lam_ref[...], a, *gate_parts)
        dxc = dxc + _dot(dpa, wa_ref[...], NT) + _dot(dpx, wx_ref[...], NT)
        dwa_ref[...] += _dot(xc, dpa, TN)
        dwx_ref[...] += _dot(xc, dpx, TN)
        for s in range(CONV_W):
            vec_ref[CONV_W - 1 - s:CONV_W - s, :] += jnp.sum(dxc * taps[s], axis=0, keepdims=True)
        vec_ref[4:5, :] += jnp.sum(dxc, axis=0, keepdims=True)
        vec_ref[5:6, :] += jnp.sum(dpa, axis=0, keepdims=True)
        vec_ref[6:7, :] += jnp.sum(dpx, axis=0, keepdims=True)
        vec_ref[7:8, :] += dlam
        vec_ref[8:9, :] += dgain
        dxn = dxcn_scr[...]
        dx = cw_ref[CONV_W - 1:CONV_W, :] * dxc
        for s in range(1, CONV_W):
            ahead = jnp.where(row >= CH - s, pltpu.roll(dxn, CH - s, 0), pltpu.roll(dxc, CH - s, 0))
            dx = dx + cw_ref[CONV_W - 1 - s:CONV_W - s, :] * ahead
        dxcn_scr[...] = dxc
        dxg_ref[:, :D_LRU] = jnp.where(valid, dx, 0.0).astype(dxg_ref.dtype)
        dxg_ref[:, D_LRU:] = dg.astype(dxg_ref.dtype)

    vec = _full((1, D_LRU))
    mat = _full((D_LRU, D_LRU))

    def blk(col, shift=0):
        return pl.BlockSpec((CH, D_LRU), lambda i: (jnp.maximum(last - i - shift, 0), col))

    return pl.pallas_call(
        body, name=name, grid=(NCH,),
        in_specs=[blk(0), blk(0, 1), blk(1), blk(0), blk(0, 1), blk(0),
                  _full((CONV_W, D_LRU)), vec, mat, vec, mat, vec, vec, vec, pl.BlockSpec(memory_space=pl.ANY)],
        out_specs=[pl.BlockSpec((CH, 2 * D_LRU), lambda i: (last - i, 0)), _full((LRU_VEC_ROWS, D_LRU)), mat, mat],
        out_shape=[_sds((T, 2 * D_LRU), MXU_DTYPE), _sds((LRU_VEC_ROWS, D_LRU), F32),
                   _sds((D_LRU, D_LRU), F32), _sds((D_LRU, D_LRU), F32)],
        scratch_shapes=[pltpu.VMEM((CH, D_LRU), F32), pltpu.VMEM((CH, D_LRU), F32), pltpu.VMEM((CH, D_LRU), F32),
                        pltpu.VMEM((1, D_LRU), F32), pltpu.VMEM((CH, D_LRU), F32)],
        compiler_params=_params(("arbitrary",)),
    )(proj, proj, proj, hst, hst, dymix, cw, cb, wa, ba, wx, bx, lam, gain, after)


def _ret_tables():
    half = HD // 2
    pos = jnp.arange(T, dtype=F32) - float(PAD)
    inv = ROPE_BASE ** (-jnp.arange(half, dtype=F32) / half)
    ang = pos[:, None] * inv[None, :]
    cos = jnp.concatenate([jnp.cos(ang), jnp.cos(ang)], axis=-1)
    sin = jnp.concatenate([-jnp.sin(ang), jnp.sin(ang)], axis=-1)
    log_g = jnp.log(1.0 - 2.0 ** (-5.0 - jnp.arange(HEADS, dtype=F32)))
    idx = jnp.arange(CH, dtype=F32)
    diff = idx[:, None] - idx[None, :]
    dmask = jnp.where(diff[None] >= 0, jnp.exp(jnp.maximum(diff, 0.0)[None] * log_g[:, None, None]), 0.0)
    xi = jnp.exp((idx + 1.0)[None, :] * log_g[:, None])
    zeta = jnp.exp((CH - 1.0 - idx)[None, :] * log_g[:, None])
    xi = jnp.broadcast_to(xi[:, :, None], (HEADS, CH, HD))
    zeta = jnp.broadcast_to(zeta[:, :, None], (HEADS, CH, HD))
    return cos, sin, dmask, xi, zeta


def _chunk_decay():
    log_g = np.log(np.float32(1.0) - np.float32(2.0) ** (np.float32(-5.0) - np.arange(HEADS, dtype=np.float32)))
    return [float(v) for v in np.exp(np.float32(CH) * log_g.astype(np.float32))]


def _rope(x, cos, sin):
    return x * cos + pltpu.roll(x, HD // 2, 1) * sin


def ret_fwd(proj, ylru, tables, gain, after, name):
    cos, sin, dmask, xi, zeta = tables
    gch = _chunk_decay()
    scale = HD ** -0.5

    def body(q_ref, k_ref, v_ref, g_ref, cos_ref, sin_ref, dm_ref, xi_ref, zt_ref, gain_ref, ylru_ref, after_ref,
             y_ref, st_ref, s_scr):
        del after_ref

        @pl.when(pl.program_id(0) == 0)
        def _():
            s_scr[...] = jnp.zeros_like(s_scr)

        y_ref[:, :D_LRU] = ylru_ref[...]
        cs, sn = cos_ref[...], sin_ref[...]
        hs = range(HEADS)
        sl = [slice(HD * h, HD * (h + 1)) for h in hs]
        qr = [_rope(q_ref[:, sl[h]], cs, sn).astype(MXU_DTYPE) for h in hs]
        kf = [_rope(k_ref[:, sl[h]], cs, sn) * scale for h in hs]
        kr = [kf[h].astype(MXU_DTYPE) for h in hs]
        v = [v_ref[:, sl[h]].astype(MXU_DTYPE) for h in hs]
        s = [s_scr[h] for h in hs]
        for h in hs:
            st_ref[h] = s[h]
        sc = [_dot(qr[h], kr[h], NT) * dm_ref[h] for h in hs]
        cross = [_dot(qr[h], s[h], NN) * xi_ref[h] for h in hs]
        for h in hs:
            s_scr[h] = s[h] * gch[h] + _dot(kf[h] * zt_ref[h], v[h], TN)
        y = [_dot(sc[h], v[h], NN) + cross[h] for h in hs]
        yc = [y[h] - jnp.mean(y[h], axis=-1, keepdims=True) for h in hs]
        yn = [yc[h] * lax.rsqrt(jnp.mean(yc[h] * yc[h], axis=-1, keepdims=True) + EPS) for h in hs]
        for h in hs:
            so = slice(D_LRU + HD * h, D_LRU + HD * (h + 1))
            y_ref[:, so] = (jax.nn.silu(g_ref[:, sl[h]]) * (yn[h] * gain_ref[:, sl[h]])).astype(y_ref.dtype)

    def col(c):
        return pl.BlockSpec((CH, D_RET), lambda n: (n, c))

    tab = pl.BlockSpec((CH, HD), lambda n: (n, 0))
    cst = _full((HEADS, CH, HD))
    return pl.pallas_call(
        body, name=name, grid=(NCH,),
        in_specs=[col(2), col(3), col(4), col(5), tab, tab, cst, cst, cst, _full((1, D_RET)), col(0),
                  pl.BlockSpec(memory_space=pl.ANY)],
        out_specs=[pl.BlockSpec((CH, D), lambda n: (n, 0)), pl.BlockSpec((None, HEADS, HD, HD), lambda n: (n, 0, 0, 0))],
        out_shape=[_sds((T, D), MXU_DTYPE), _sds((NCH, HEADS, HD, HD), F32)],
        scratch_shapes=[pltpu.VMEM((HEADS, HD, HD), F32)],
        compiler_params=_params(("arbitrary",)),
    )(proj, proj, proj, proj, cos, sin, dmask, xi, zeta, gain, ylru, after)


def ret_bwd(proj, states, dymix, dxg, tables, gain, name):
    cos, sin, dmask, xi, zeta = tables
    gch = _chunk_decay()
    scale = HD ** -0.5
    last = NCH - 1

    def body(q_ref, k_ref, v_ref, g_ref, st_ref, do_ref, cos_ref, sin_ref, dm_ref, xi_ref, zt_ref, gain_ref, dxg_ref,
             dp_ref, dgain_ref, ds_scr):
        @pl.when(pl.program_id(0) == 0)
        def _():
            ds_scr[...] = jnp.zeros_like(ds_scr)
            dgain_ref[...] = jnp.zeros_like(dgain_ref)

        dp_ref[:, :2 * D_LRU] = dxg_ref[...]
        cs, sn = cos_ref[...], sin_ref[...]
        hs = range(HEADS)
        sl = [slice(HD * h, HD * (h + 1)) for h in hs]

        def out(j, h):
            return slice(2 * D_LRU + j * D_RET + HD * h, 2 * D_LRU + j * D_RET + HD * (h + 1))

        b16 = lambda xs: [x.astype(MXU_DTYPE) for x in xs]
        qr = b16([_rope(q_ref[:, sl[h]], cs, sn) for h in hs])
        kf = [_rope(k_ref[:, sl[h]], cs, sn) * scale for h in hs]
        kr = b16(kf)
        kz = b16([kf[h] * zt_ref[h] for h in hs])
        v = b16([v_ref[:, sl[h]] for h in hs])
        s = b16([st_ref[h] for h in hs])
        ds = [ds_scr[h] for h in hs]
        dsb = b16(ds)
        sc = [_dot(qr[h], kr[h], NT) * dm_ref[h] for h in hs]
        scb = b16(sc)
        y = [_dot(scb[h], v[h], NN) + _dot(qr[h], s[h], NN) * xi_ref[h] for h in hs]
        yc = [y[h] - jnp.mean(y[h], axis=-1, keepdims=True) for h in hs]
        rstd = [lax.rsqrt(jnp.mean(yc[h] * yc[h], axis=-1, keepdims=True) + EPS) for h in hs]
        yn = [yc[h] * rstd[h] for h in hs]
        dy = []
        for h in hs:
            g = g_ref[:, sl[h]]
            gain = gain_ref[:, sl[h]]
            sg = jax.nn.sigmoid(g)
            silu = g * sg
            dout = do_ref[:, sl[h]].astype(F32)
            dgain_ref[:, sl[h]] += jnp.sum(dout * silu * yn[h], axis=0, keepdims=True)
            dp_ref[:, out(3, h)] = (dout * yn[h] * gain * (sg * (1.0 + g * (1.0 - sg)))).astype(dp_ref.dtype)
            dyn = dout * silu * gain
            dy.append(rstd[h] * (dyn - jnp.mean(dyn, axis=-1, keepdims=True)
                                 - yn[h] * jnp.mean(dyn * yn[h], axis=-1, keepdims=True)))
        dyb = b16(dy)
        dqs = b16([dy[h] * xi_ref[h] for h in hs])
        dp = b16([_dot(dyb[h], v[h], NT) * dm_ref[h] for h in hs])
        dv = [_dot(scb[h], dyb[h], TN) + _dot(kz[h], dsb[h], NN) for h in hs]
        dqr = [_dot(dp[h], kr[h], NN) + _dot(dqs[h], s[h], NT) for h in hs]
        dkr = [_dot(dp[h], qr[h], TN) + _dot(v[h], dsb[h], NT) * zt_ref[h] for h in hs]
        for h in hs:
            ds_scr[h] = gch[h] * ds[h] + _dot(qr[h], dqs[h], TN)
        for h in hs:
            dp_ref[:, out(0, h)] = (dqr[h] * cs + pltpu.roll(dqr[h] * sn, HD // 2, 1)).astype(dp_ref.dtype)
            dp_ref[:, out(1, h)] = ((dkr[h] * cs + pltpu.roll(dkr[h] * sn, HD // 2, 1)) * scale).astype(dp_ref.dtype)
            dp_ref[:, out(2, h)] = dv[h].astype(dp_ref.dtype)

    def col(c):
        return pl.BlockSpec((CH, D_RET), lambda n: (last - n, c))

    tab = pl.BlockSpec((CH, HD), lambda n: (last - n, 0))
    cst = _full((HEADS, CH, HD))
    return pl.pallas_call(
        body, name=name, grid=(NCH,),
        in_specs=[col(2), col(3), col(4), col(5), pl.BlockSpec((None, HEADS, HD, HD), lambda n: (last - n, 0, 0, 0)), col(1),
                  tab, tab, cst, cst, cst, _full((1, D_RET)), pl.BlockSpec((CH, 2 * D_LRU), lambda n: (last - n, 0))],
        out_specs=[pl.BlockSpec((CH, D_IN), lambda n: (last - n, 0)), _full((1, D_RET))],
        out_shape=[_sds((T, D_IN), MXU_DTYPE), _sds((1, D_RET), F32)],
        scratch_shapes=[pltpu.VMEM((HEADS, HD, HD), F32)],
        compiler_params=_params(("arbitrary",)),
    )(proj, proj, proj, proj, states, dymix, cos, sin, dmask, xi, zeta, gain, dxg)


HBM = pl.BlockSpec(memory_space=pltpu.HBM)


def _place():
    return lax.axis_index("x"), lax.axis_index("y"), lax.axis_index("c")


def all_gather(arrs, name):
    n = len(arrs)

    def body(*refs):
        ins, outs = refs[:n], refs[n:2 * n]
        send_sems, recv_sems, local_sems = refs[2 * n:]
        x, y, c = _place()
        me, sibling = (x, y, c), (x, y, 1 - c)
        chips = [(1 - x, y), (x, 1 - y), (1 - x, 1 - y)]

        def copy(a, k, block, to, src=None):
            px, py, pc = block
            dst = outs[a].at[4 * px + 2 * py + pc]
            return pltpu.make_async_remote_copy(
                src_ref=dst if src is None else src, dst_ref=dst, send_sem=send_sems.at[a, k], recv_sem=recv_sems.at[a, k],
                device_id=to, device_id_type=MESH)

        mine = [pltpu.make_async_copy(ins[a], outs[a].at[4 * x + 2 * y + c], local_sems.at[a]) for a in range(n)]
        for cp in mine:
            cp.start()
        first = []
        for a in range(n):
            first.append(copy(a, 0, me, sibling, src=ins[a]))
            first += [copy(a, 1 + j, me, (*chip, c), src=ins[a]) for j, chip in enumerate(chips)]
        for cp in first:
            cp.start()
        passed = []
        for j, chip in enumerate(chips):
            for a in range(n):
                copy(a, 1 + j, (*chip, c), me).wait_recv()
                passed.append(copy(a, 4 + j, (*chip, c), sibling))
                passed[-1].start()
        for a in range(n):
            copy(a, 0, sibling, me).wait_recv()
            for j, chip in enumerate(chips):
                copy(a, 4 + j, (*chip, 1 - c), me).wait_recv()
        for cp in first + passed:
            cp.wait_send()
        for cp in mine:
            cp.wait()

    return pl.pallas_call(
        body, name=name,
        in_specs=[HBM] * n, out_specs=[HBM] * n,
        out_shape=[_sds((NDEV,) + a.shape, a.dtype) for a in arrs],
        scratch_shapes=[pltpu.SemaphoreType.DMA((n, 7)), pltpu.SemaphoreType.DMA((n, 7)), pltpu.SemaphoreType.DMA((n,))],
    )(*arrs)


SEM = pl.BlockSpec(memory_space=pltpu.SEMAPHORE)
ANY = pl.BlockSpec(memory_space=pl.ANY)
EFFECT = pltpu.SideEffectType.DATAFLOW_SIDE_EFFECTING


def _hbm(a):
    return pltpu.with_memory_space_constraint(a, pltpu.HBM)


def _hbm_like(arrs):
    return [pltpu.HBM(a.shape, a.dtype) for a in arrs]


def _dma_sems(count):
    return [pltpu.SemaphoreType.DMA(())] * count


def _ag_copy(lands, send_sems, recv_sems, per):
    def copy(a, k, block, to, src=None):
        px, py, pc = block
        dst = lands[a].at[4 * px + 2 * py + pc]
        return pltpu.make_async_remote_copy(
            src_ref=dst if src is None else src, dst_ref=dst, send_sem=send_sems[a * per + k], recv_sem=recv_sems[a * per + k],
            device_id=to, device_id_type=MESH)
    return copy


def to_wire(sel, w_in, w_gate, w_up, w_out, w_down, name):
    ffpad = FF_SHP - FF_SH

    def body(sel_ref, i_ref, g_ref, u_ref, o_ref, d_ref, oi, og, ou, oo, od):
        del sel_ref
        oi[...] = i_ref[...].astype(oi.dtype)
        oo[...] = o_ref[...].astype(oo.dtype)
        for src, dst in ((g_ref, og), (u_ref, ou), (d_ref, od)):
            dst[:FF_SH, :] = src[...].astype(dst.dtype)
            dst[FF_SH:, :] = jnp.zeros((ffpad, D), dst.dtype)

    shapes_in = [(D, IN_SH), (FF_SH, D), (FF_SH, D), (OUT_SH, D), (FF_SH, D)]
    shapes_out = [(D, IN_SH), (FF_SHP, D), (FF_SHP, D), (OUT_SH, D), (FF_SHP, D)]
    return pl.pallas_call(
        body, name=name,
        grid_spec=pltpu.PrefetchScalarGridSpec(
            num_scalar_prefetch=1, grid=(1,),
            in_specs=[pl.BlockSpec((None,) + s, lambda i, sel_ref: (sel_ref[1], 0, 0)) for s in shapes_in],
            out_specs=[pl.BlockSpec((None,) + s, lambda i, sel_ref: (sel_ref[0], 0, 0)) for s in shapes_out]),
        out_shape=[_sds((NDEV,) + s, WIRE_DTYPE) for s in shapes_out], compiler_params=_params(("arbitrary",)),
    )(sel, w_in, w_gate, w_up, w_out, w_down)


def place_blocks(sel, arrs, name):
    n = len(arrs)

    def body(sel_ref, *refs):
        del sel_ref
        for a in range(n):
            refs[n + a][...] = refs[a][...]

    def whole(a):
        nd = a.ndim
        return pl.BlockSpec(a.shape, lambda i, sel_ref: (0,) * nd)

    def mine(a):
        nd = a.ndim
        return pl.BlockSpec((None,) + a.shape, lambda i, sel_ref: (sel_ref[0],) + (0,) * nd)

    return pl.pallas_call(
        body, name=name,
        grid_spec=pltpu.PrefetchScalarGridSpec(
            num_scalar_prefetch=1, grid=(1,), in_specs=[whole(a) for a in arrs], out_specs=[mine(a) for a in arrs]),
        out_shape=[_sds((NDEV,) + a.shape, a.dtype) for a in arrs], compiler_params=_params(("arbitrary",)),
    )(sel, *arrs)


def ag_start(lands, after, name):
    n = len(lands)
    ns = 4 * n

    def body(*refs):
        lnd = refs[:n]
        send_sems, recv_sems = refs[n + 1:n + 1 + ns], refs[n + 1 + ns:n + 1 + 2 * ns]
        token = refs[-1]
        x, y, c = _place()
        me, sibling = (x, y, c), (x, y, 1 - c)
        chips = [(1 - x, y), (x, 1 - y), (1 - x, 1 - y)]
        copy = _ag_copy(lnd, send_sems, recv_sems, 4)
        for a in range(n):
            copy(a, 0, me, sibling).start()
            for j, chip in enumerate(chips):
                copy(a, 1 + j, me, (*chip, c)).start()
        token[...] = jnp.zeros_like(token)

    outs = pl.pallas_call(
        body, name=name,
        in_specs=[HBM] * n + [ANY],
        out_specs=[SEM] * (2 * ns) + [HBM] * n + [pl.BlockSpec(memory_space=pltpu.VMEM)],
        out_shape=_dma_sems(2 * ns) + _hbm_like(lands) + [_sds((8, 128), F32)],
        input_output_aliases={i: 2 * ns + i for i in range(n)},
        compiler_params=pltpu.CompilerParams(has_side_effects=EFFECT),
    )(*[_hbm(a) for a in lands], after)
    return outs[:ns], outs[ns:2 * ns], outs[2 * ns:2 * ns + n], outs[-1]


def ag_forward(send_sems, recv_sems, lands, after, name):
    n = len(lands)
    n1, n2 = 4 * n, 3 * n

    def body(*refs):
        lnd = refs[:n]
        o = n
        s1, r1 = refs[o:o + n1], refs[o + n1:o + 2 * n1]
        o += 2 * n1 + 1
        s2, r2 = refs[o:o + n2], refs[o + n2:o + 2 * n2]
        token = refs[-1]
        token[...] = jnp.zeros_like(token)
        x, y, c = _place()
        me, sibling = (x, y, c), (x, y, 1 - c)
        chips = [(1 - x, y), (x, 1 - y), (1 - x, 1 - y)]
        copy1 = _ag_copy(lnd, s1, r1, 4)
        copy2 = _ag_copy(lnd, s2, r2, 3)
        for j, chip in enumerate(chips):
            for a in range(n):
                copy1(a, 1 + j, (*chip, c), me).wait_recv()
                copy2(a, j, (*chip, c), sibling).start()
        for a in range(n):
            copy1(a, 0, sibling, me).wait_recv()
            copy1(a, 0, me, sibling).wait_send()
            for j, chip in enumerate(chips):
                copy1(a, 1 + j, me, (*chip, c)).wait_send()

    outs = pl.pallas_call(
        body, name=name,
        in_specs=[HBM] * n + [SEM] * (2 * n1) + [ANY],
        out_specs=[SEM] * (2 * n2) + [HBM] * n + [pl.BlockSpec(memory_space=pltpu.VMEM)],
        out_shape=_dma_sems(2 * n2) + _hbm_like(lands) + [_sds((8, 128), F32)],
        input_output_aliases={i: 2 * n2 + i for i in range(n)},
        compiler_params=pltpu.CompilerParams(has_side_effects=EFFECT),
    )(*lands, *send_sems, *recv_sems, after)
    return outs[:n2], outs[n2:2 * n2], outs[2 * n2:2 * n2 + n], outs[-1]


def ag_finish(send_sems, recv_sems, lands, after, name):
    n = len(lands)
    n2 = 3 * n

    def body(*refs):
        lnd = refs[:n]
        s2, r2 = refs[n:n + n2], refs[n + n2:n + 2 * n2]
        x, y, c = _place()
        me, sibling = (x, y, c), (x, y, 1 - c)
        chips = [(1 - x, y), (x, 1 - y), (1 - x, 1 - y)]
        copy2 = _ag_copy(lnd, s2, r2, 3)
        for a in range(n):
            for j, chip in enumerate(chips):
                copy2(a, j, (*chip, c), sibling).wait_send()
                copy2(a, j, (*chip, 1 - c), me).wait_recv()

    outs = pl.pallas_call(
        body, name=name,
        in_specs=[HBM] * n + [SEM] * (2 * n2) + [ANY],
        out_specs=[HBM] * n, out_shape=_hbm_like(lands),
        input_output_aliases={i: i for i in range(n)},
        compiler_params=pltpu.CompilerParams(has_side_effects=EFFECT),
    )(*lands, *send_sems, *recv_sems, after)
    return list(outs)


def rs_sibling_start(arrs, name):
    n = len(arrs)
    ns = 4 * n
    lands = [lax.empty((4,) + a.shape[1:], a.dtype) for a in arrs]

    def body(*refs):
        ins, lnd = refs[:n], refs[n:2 * n]
        send_sems, recv_sems = refs[2 * n:2 * n + ns], refs[2 * n + ns:2 * n + 2 * ns]
        x, y, c = _place()
        sibling = (x, y, 1 - c)
        for a in range(n):
            for p in range(4):
                pltpu.make_async_remote_copy(
                    src_ref=ins[a].at[2 * p + 1 - c], dst_ref=lnd[a].at[p], send_sem=send_sems[4 * a + p],
                    recv_sem=recv_sems[4 * a + p], device_id=sibling, device_id_type=MESH).start()
        refs[-1][...] = jnp.zeros_like(refs[-1])

    outs = pl.pallas_call(
        body, name=name,
        in_specs=[HBM] * (2 * n), out_specs=[SEM] * (2 * ns) + [HBM] * (2 * n) + [pl.BlockSpec(memory_space=pltpu.VMEM)],
        out_shape=_dma_sems(2 * ns) + _hbm_like(arrs) + _hbm_like(lands) + [_sds((8, 128), F32)],
        input_output_aliases={i: 2 * ns + i for i in range(2 * n)},
        compiler_params=pltpu.CompilerParams(has_side_effects=EFFECT),
    )(*[_hbm(a) for a in arrs], *[_hbm(a) for a in lands])
    return (outs[:ns], outs[ns:2 * ns], outs[2 * ns:2 * ns + n], outs[2 * ns + n:2 * ns + 2 * n]), outs[-1]


def rs_sibling_wait(send_sems, recv_sems, arrs, lands, after, name):
    n = len(arrs)
    ns = 4 * n

    def body(*refs):
        ins, lnd = refs[:n], refs[n:2 * n]
        s, r = refs[2 * n:2 * n + ns], refs[2 * n + ns:2 * n + 2 * ns]
        x, y, c = _place()
        sibling = (x, y, 1 - c)
        for a in range(n):
            for p in range(4):
                cp = pltpu.make_async_remote_copy(
                    src_ref=ins[a].at[2 * p + 1 - c], dst_ref=lnd[a].at[p], send_sem=s[4 * a + p], recv_sem=r[4 * a + p],
                    device_id=sibling, device_id_type=MESH)
                cp.wait_send()
                cp.wait_recv()

    outs = pl.pallas_call(
        body, name=name,
        in_specs=[HBM] * (2 * n) + [SEM] * (2 * ns) + [ANY], out_specs=[HBM] * (2 * n),
        out_shape=_hbm_like(arrs) + _hbm_like(lands),
        input_output_aliases={i: i for i in range(2 * n)},
        compiler_params=pltpu.CompilerParams(has_side_effects=EFFECT),
    )(*arrs, *lands, *send_sems, *recv_sems, after)
    return outs[:n], outs[n:]


def rs_chips_start(parts, name):
    n = len(parts)
    ns = 3 * n
    lands = [lax.empty((3,) + a.shape[1:], a.dtype) for a in parts]

    def body(*refs):
        ins, lnd = refs[:n], refs[n:2 * n]
        send_sems, recv_sems = refs[2 * n:2 * n + ns], refs[2 * n + ns:2 * n + 2 * ns]
        x, y, c = _place()
        chips = [(1 - x, y), (x, 1 - y), (1 - x, 1 - y)]
        for a in range(n):
            for k, (tx, ty) in enumerate(chips):
                pltpu.make_async_remote_copy(
                    src_ref=ins[a].at[2 * tx + ty], dst_ref=lnd[a].at[k], send_sem=send_sems[3 * a + k],
                    recv_sem=recv_sems[3 * a + k], device_id=(tx, ty, c), device_id_type=MESH).start()
        refs[-1][...] = jnp.zeros_like(refs[-1])

    outs = pl.pallas_call(
        body, name=name,
        in_specs=[HBM] * (2 * n), out_specs=[SEM] * (2 * ns) + [HBM] * (2 * n) + [pl.BlockSpec(memory_space=pltpu.VMEM)],
        out_shape=_dma_sems(2 * ns) + _hbm_like(parts) + _hbm_like(lands) + [_sds((8, 128), F32)],
        input_output_aliases={i: 2 * ns + i for i in range(2 * n)},
        compiler_params=pltpu.CompilerParams(has_side_effects=EFFECT),
    )(*[_hbm(a) for a in parts], *[_hbm(a) for a in lands])
    return (outs[:ns], outs[ns:2 * ns], outs[2 * ns:2 * ns + n], outs[2 * ns + n:2 * ns + 2 * n]), outs[-1]


def rs_chips_wait(send_sems, recv_sems, parts, lands, after, name):
    n = len(parts)
    ns = 3 * n

    def body(*refs):
        ins, lnd = refs[:n], refs[n:2 * n]
        s, r = refs[2 * n:2 * n + ns], refs[2 * n + ns:2 * n + 2 * ns]
        x, y, c = _place()
        chips = [(1 - x, y), (x, 1 - y), (1 - x, 1 - y)]
        for a in range(n):
            for k, (tx, ty) in enumerate(chips):
                cp = pltpu.make_async_remote_copy(
                    src_ref=ins[a].at[2 * tx + ty], dst_ref=lnd[a].at[k], send_sem=s[3 * a + k], recv_sem=r[3 * a + k],
                    device_id=(tx, ty, c), device_id_type=MESH)
                cp.wait_send()
                cp.wait_recv()

    outs = pl.pallas_call(
        body, name=name,
        in_specs=[HBM] * (2 * n) + [SEM] * (2 * ns) + [ANY], out_specs=[HBM] * (2 * n),
        out_shape=_hbm_like(parts) + _hbm_like(lands),
        input_output_aliases={i: i for i in range(2 * n)},
        compiler_params=pltpu.CompilerParams(has_side_effects=EFFECT),
    )(*parts, *lands, *send_sems, *recv_sems, after)
    return outs[:n], outs[n:]


def pair_sum(arrs, recv, c, name):
    n = len(arrs)

    def body(c_ref, *refs):
        del c_ref
        for a in range(n):
            refs[2 * n + a][...] = (refs[a][...].astype(F32) + refs[n + a][...].astype(F32)).astype(refs[2 * n + a].dtype)

    mine = [pl.BlockSpec((None,) + a.shape[1:], lambda p, c_ref: (2 * p + c_ref[0], 0, 0)) for a in arrs]
    other = [pl.BlockSpec((None,) + a.shape[1:], lambda p, c_ref: (p, 0, 0)) for a in arrs]
    return pl.pallas_call(
        body, name=name,
        grid_spec=pltpu.PrefetchScalarGridSpec(num_scalar_prefetch=1, grid=(4,), in_specs=mine + other, out_specs=other),
        out_shape=[_sds((4,) + a.shape[1:], a.dtype) for a in arrs], compiler_params=_params(("parallel",)),
    )(c, *arrs, *recv)


def _adamw(w, g, m, v):
    m = ADAM_B1 * m + (1.0 - ADAM_B1) * g
    v = ADAM_B2 * v + (1.0 - ADAM_B2) * jnp.square(g)
    m_hat = m / (1.0 - ADAM_B1 ** ADAM_STEP)
    v_hat = v / (1.0 - ADAM_B2 ** ADAM_STEP)
    return -ADAM_LR * (m_hat / (jnp.sqrt(v_hat) + ADAM_EPS) + ADAM_WD * w), m, v


def adamw_big(recv, sums, chip, w, m, v, tr, name):
    nl, rr, cc = w.shape
    cp = recv[0].shape[2]

    def body(chip_ref, *refs):
        del chip_ref
        rcv, own = refs[:nl], refs[nl:2 * nl]
        w_ref, m_ref, v_ref, g_out, d_out, m_out, v_out = refs[2 * nl:]
        for l in range(nl):
            g = ((own[l][...].astype(F32) + rcv[l][0].astype(F32)) + rcv[l][1].astype(F32)) + rcv[l][2].astype(F32)
            g = g[:, :cc]
            g_out[l] = g
            d_out[l], m_out[l], v_out[l] = _adamw(w_ref[l], g, m_ref[l], v_ref[l])

    blk = pl.BlockSpec((nl, tr, cc), lambda i, chip_ref: (0, i, 0))
    return pl.pallas_call(
        body, name=name,
        grid_spec=pltpu.PrefetchScalarGridSpec(
            num_scalar_prefetch=1, grid=(rr // tr,),
            in_specs=[pl.BlockSpec((3, tr, cp), lambda i, chip_ref: (0, i, 0))] * nl
            + [pl.BlockSpec((None, tr, cp), lambda i, chip_ref: (chip_ref[0], i, 0))] * nl + [blk, blk, blk],
            out_specs=[blk] * 4),
        out_shape=[_sds(w.shape, F32)] * 4, compiler_params=_params(("parallel",)),
    )(chip, *recv, *sums, w, m, v)


SMALL_ROWS = 16


def small_grads(lvec, g_ret, g_mix, g_ffn, g_final, dwa, dwx, name):
    def body(lvec_ref, ret_ref, mix_ref, ffn_ref, fin_ref, dwa_ref, dwx_ref, v_ref, g_ref):
        v_ref[0:9, :] = lvec_ref[0:9, :]
        v_ref[9:10, :] = ret_ref[...]
        for r, src in ((10, mix_ref), (12, ffn_ref), (14, fin_ref)):
            v_ref[r:r + 1, :] = src[:, :D_LRU]
            v_ref[r + 1:r + 2, :] = src[:, D_LRU:]
        for k, src in enumerate((dwa_ref, dwx_ref)):
            for g in range(LRU_BLOCKS):
                rows = slice(LRU_BD * g, LRU_BD * (g + 1))
                g_ref[D_LRU * k + LRU_BD * g:D_LRU * k + LRU_BD * (g + 1), :] = src[rows, rows]

    ins = [lvec, g_ret, g_mix, g_ffn, g_final, dwa, dwx]
    return pl.pallas_call(
        body, name=name, grid=(1,), in_specs=[_full(a.shape) for a in ins],
        out_specs=[_full((SMALL_ROWS, D_LRU)), _full((2 * D_LRU, LRU_BD))],
        out_shape=[_sds((SMALL_ROWS, D_LRU), F32), _sds((2 * D_LRU, LRU_BD), F32)], compiler_params=_params(("arbitrary",)),
    )(*ins)


def sum_devices(arrs, name):
    n = len(arrs)

    def body(*refs):
        for a in range(n):
            acc = refs[a][0]
            for j in range(1, NDEV):
                acc = acc + refs[a][j]
            refs[n + a][...] = acc

    return pl.pallas_call(
        body, name=name, grid=(1,), in_specs=[_full(a.shape) for a in arrs], out_specs=[_full(a.shape[1:]) for a in arrs],
        out_shape=[_sds(a.shape[1:], F32) for a in arrs], compiler_params=_params(("arbitrary",)),
    )(*arrs)


def adamw_small(gs, ws, ms, vs, name):
    n = len(gs)

    def body(*refs):
        for a in range(n):
            g, w, m, v = (refs[k * n + a][...] for k in range(4))
            refs[4 * n + a][...], refs[5 * n + a][...], refs[6 * n + a][...] = _adamw(w, g, m, v)

    specs = [_full(a.shape) for a in ws]
    outs = pl.pallas_call(
        body, name=name, grid=(1,), in_specs=specs * 4, out_specs=specs * 3, out_shape=[_sds(a.shape, F32) for a in ws] * 3,
        compiler_params=_params(("arbitrary",)),
    )(*gs, *ws, *ms, *vs)
    return outs[:n], outs[n:2 * n], outs[2 * n:]


def block_diag(wa, wx, name):
    def body(wa_ref, wx_ref, oa_ref, ox_ref):
        for src, dst in ((wa_ref, oa_ref), (wx_ref, ox_ref)):
            dst[...] = jnp.zeros_like(dst)
            for g in range(LRU_BLOCKS):
                rows = slice(LRU_BD * g, LRU_BD * (g + 1))
                dst[rows, rows] = src[g].astype(dst.dtype)

    ispec = pl.BlockSpec((None, LRU_BLOCKS, LRU_BD, LRU_BD), lambda l: (l, 0, 0, 0))
    ospec = pl.BlockSpec((None, D_LRU, D_LRU), lambda l: (l, 0, 0))
    return pl.pallas_call(
        body, name=name, grid=(wa.shape[0],), in_specs=[ispec, ispec], out_specs=[ospec, ospec],
        out_shape=[_sds((wa.shape[0], D_LRU, D_LRU), MXU_DTYPE)] * 2, compiler_params=_params(("parallel",)),
    )(wa, wx)


REP_NAMES = ["norm_mix", "conv_b", "gate_a_w", "gate_a_b", "gate_x_w", "gate_x_b", "lru_lambda", "lru_out_norm",
             "ret_out_norm", "norm_ffn", "norm_final"]


def kernel(x, meta_tokens, norm_mix, w_in, conv_w, conv_b, gate_a_w, gate_a_b, gate_x_w, gate_x_b, lru_lambda, lru_out_norm, ret_out_norm, w_out, norm_ffn, w_gate, w_up, w_down, norm_final, loss_target, m_meta_tokens, m_norm_mix, m_w_in, m_conv_w, m_conv_b, m_gate_a_w, m_gate_a_b, m_gate_x_w, m_gate_x_b, m_lru_lambda, m_lru_out_norm, m_ret_out_norm, m_w_out, m_norm_ffn, m_w_gate, m_w_up, m_w_down, m_norm_final, v_meta_tokens, v_norm_mix, v_w_in, v_conv_w, v_conv_b, v_gate_a_w, v_gate_a_b, v_gate_x_w, v_gate_x_b, v_lru_lambda, v_lru_out_norm, v_ret_out_norm, v_w_out, v_norm_ffn, v_w_gate, v_w_up, v_w_down, v_norm_final):
    xi, yi, ci = _place()
    dev = 4 * xi + 2 * yi + ci
    c_arr = jnp.reshape(ci, (1,)).astype(jnp.int32)
    dev_arr = jnp.reshape(dev, (1,)).astype(jnp.int32)

    meta_g, conv_g = all_gather([meta_tokens, conv_w], "ag_small")
    meta_full = jnp.transpose(meta_g, (1, 0, 2)).reshape(N_META, D)
    conv_full = jnp.transpose(conv_g, (1, 2, 0, 3)).reshape(DEPTH, CONV_W, D_LRU)
    tr_ = lambda a: jnp.transpose(a, (0, 2, 1))
    w_gate_t, m_w_gate_t, v_w_gate_t = tr_(w_gate), tr_(m_w_gate), tr_(v_w_gate)
    w_up_t, m_w_up_t, v_w_up_t = tr_(w_up), tr_(m_w_up), tr_(v_w_up)
    level1 = []
    token = meta_g
    for l in range(DEPTH):
        sel = jnp.stack([dev, jnp.int32(l)]).astype(jnp.int32)
        lands = to_wire(sel, w_in, w_gate_t, w_up_t, w_out, w_down, "to_wire")
        s1, r1, lands, token = ag_start(lands, token, f"ag_start_{l}")
        level1.append((s1, r1, lands))

    def as_weights(gi, gg, gu, go, gd):
        return dict(w_in=gi, w_gate=gg.reshape(D_FFP, D), w_up=gu.reshape(D_FFP, D), w_out=go.reshape(D, D),
                    w_down=gd.reshape(D_FFP, D))

    tables = _ret_tables()
    row = lambda a: a.reshape(1, -1)

    h = jnp.concatenate([jnp.zeros((PAD, D), F32), meta_full, x[0]], axis=0)
    saved, gathered = [], []
    s1, r1, lands = level1[0]
    s2, r2, first, order = ag_forward(s1[:4], r1[:4], lands[:1], token, "ag_forward_0_w_in")
    w_in_next = ag_finish(s2, r2, first, h, "ag_finish_0_w_in")[0]
    wa_dense, wx_dense = block_diag(gate_a_w, gate_x_w, "block_diag")
    for l in range(DEPTH):
        small = dict(cw=conv_full[l], cb=row(conv_b[l]), wa=wa_dense[l], ba=row(gate_a_b[l]),
                     wx=wx_dense[l], bx=row(gate_x_b[l]), lam=row(lru_lambda[l]),
                     gain=row(lru_out_norm[l]))
        s1, r1, lands = level1[l]
        hn1 = rmsnorm_fwd(h, row(norm_mix[l]), "rms_fwd")
        proj = mm_blocked_nn(hn1, w_in_next, F32, "proj")
        ylru, hst = lru_fwd(proj, name="lru_fwd", **small)
        s2, r2, rest, order = ag_forward(s1[4:], r1[4:], lands[1:], ylru, f"ag_forward_{l}_rest")
        ymix, states = ret_fwd(proj, ylru, tables, row(ret_out_norm[l]), order, "ret_fwd")
        w = as_weights(w_in_next, *ag_finish(s2, r2, rest, ymix, f"ag_finish_{l}_rest"))
        gathered.append(w)
        h_mid = mm_nn_res(ymix, w["w_out"], h, order, "out_proj")
        hn2 = rmsnorm_fwd(h_mid, row(norm_ffn[l]), "rms_fwd")
        gate, up, act = ffn_up(hn2, w["w_gate"], w["w_up"], "ffn_up")
        if l + 1 < DEPTH:
            s1n, r1n, landsn = level1[l + 1]
            s2, r2, first, order = ag_forward(s1n[:4], r1n[:4], landsn[:1], act, f"ag_forward_{l + 1}_w_in")
        h_out = mm_nn_res(act, w["w_down"], h_mid, order, "ffn_down")
        if l + 1 < DEPTH:
            w_in_next = ag_finish(s2, r2, first, h_out, f"ag_finish_{l + 1}_w_in")[0]
        saved.append(dict(h=h, hn1=hn1, proj=proj, hst=hst, states=states, ymix=ymix, h_mid=h_mid, hn2=hn2, gate=gate, up=up,
                          act=act, small=small))
        h = h_out

    loss_p, dh, dh_b, g_norm_final = loss_head(h, row(norm_final), loss_target[0], "loss_head")
    loss = lax.psum(loss_p[0, 0], ("x", "y", "c"))

    small_v = [None] * DEPTH
    small_w = [None] * DEPTH
    inflight = []
    sib = None
    order = loss_p

    def sibling_done(l, tag, names, sib, after):
        parts, got = rs_sibling_wait(*sib, after, f"rs_sibling_wait_{tag}")
        sums = pair_sum(parts, got, c_arr, "pair_sum")
        flying, started = rs_chips_start(sums, f"rs_chips_start_{tag}")
        inflight.append((l, tag, names, flying))
        return started

    for l in reversed(range(DEPTH)):
        w, s = gathered[l], saved[l]
        dgate, dup = ffn_down_bwd(dh_b, w["w_down"], s["gate"], s["up"], order, "ffn_down_bwd")
        dwd = mm_tn(s["act"], dh_b, PAIR, order, "dw_down").reshape(NDEV, FF_SHP, D)
        dwg, dwu = (g.reshape(NDEV, FF_SHP, D) for g in mm_tn_two(dgate, dup, s["hn2"], PAIR, order, "dw_rows"))
        ffn_sib, order = rs_sibling_start([dwg, dwu, dwd], f"rs_sibling_start_{l}_ffn")
        dhn2 = mm_rows_nn([(dgate, w["w_gate"]), (dup, w["w_up"])], order, "ffn_up_bwd")
        if sib is not None:
            order = sibling_done(l + 1, f"{l + 1}_mix", ("w_in", "w_out"), sib, dhn2)
        dh_mid, dh_mid_b, g_norm_ffn = rmsnorm_bwd(s["h_mid"], row(norm_ffn[l]), dhn2, dh, "rms_bwd")
        dymix, dwo = out_proj_bwd(dh_mid_b, w["w_out"], s["ymix"], order, "out_proj_bwd")
        dwo = dwo.reshape(NDEV, OUT_SH, D)
        order = sibling_done(l, f"{l}_ffn", ("w_gate", "w_up", "w_down"), ffn_sib, dymix)
        dxg, lvec, dwa, dwx = lru_bwd(s["proj"], s["hst"], dymix, after=order, name="lru_bwd", **s["small"])
        dproj, g_ret_norm = ret_bwd(s["proj"], s["states"], dymix, dxg, tables, row(ret_out_norm[l]), "ret_bwd")
        dwi = mm_tn_blocked(s["hn1"], dproj, "dw_blocked")
        dhn1 = mm_blocked_nt([(dproj, w["w_in"])], order, "proj_bwd")
        dh, dh_b, g_norm_mix = rmsnorm_bwd(s["h"], row(norm_mix[l]), dhn1, dh_mid, "rms_bwd")

        g_fin = g_norm_final if l == 0 else jnp.zeros((1, D), F32)
        small_v[l], small_w[l] = small_grads(lvec, g_ret_norm, g_norm_mix, g_norm_ffn, g_fin, dwa, dwx, "small_grads")
        sib, order = rs_sibling_start([dwi, dwo], f"rs_sibling_start_{l}_mix")
        if l == 1:
            early = place_blocks(dev_arr, [jnp.stack(small_v[1:]), jnp.stack(small_w[1:])], "place_grads")
            early_sems = ag_start(early, order, "ag_start_grads")
            order = early_sems[3]

    grad_x = dh[X0:][None]
    g_meta = dh[PAD:X0]

    late = all_gather([small_v[0], small_w[0], g_meta], "ag_grads")
    s2, r2, lands, _ = ag_forward(early_sems[0], early_sems[1], early_sems[2], dh, "ag_forward_grads")
    gath_early = ag_finish(s2, r2, lands, late[0], "ag_finish_grads")
    sibling_done(0, "0_mix", ("w_in", "w_out"), sib, late[0])
    v0, w0, meta_sum, v123, w123 = sum_devices(list(late) + list(gath_early), "sum_devices")
    vecs = jnp.concatenate([v0[None], v123])
    gws = jnp.concatenate([w0[None], w123])
    blocks = (DEPTH, LRU_BLOCKS, LRU_BD)
    small_g = dict(
        conv_w=lax.dynamic_slice_in_dim(vecs[:, 0:CONV_W], dev * (D_LRU // NDEV), D_LRU // NDEV, axis=2),
        conv_b=vecs[:, 4], gate_a_b=vecs[:, 5].reshape(blocks), gate_x_b=vecs[:, 6].reshape(blocks),
        lru_lambda=vecs[:, 7], lru_out_norm=vecs[:, 8], ret_out_norm=vecs[:, 9],
        norm_mix=vecs[:, 10:12].reshape(DEPTH, D), norm_ffn=vecs[:, 12:14].reshape(DEPTH, D),
        norm_final=v0[14:16].reshape(1, D),
        gate_a_w=gws[:, :D_LRU].reshape(blocks + (LRU_BD,)), gate_x_w=gws[:, D_LRU:].reshape(blocks + (LRU_BD,)),
        meta_tokens=lax.dynamic_slice_in_dim(meta_sum, dev * (D // NDEV), D // NDEV, axis=1))
    given = dict(norm_mix=(norm_mix, m_norm_mix, v_norm_mix), conv_b=(conv_b, m_conv_b, v_conv_b),
                 gate_a_w=(gate_a_w, m_gate_a_w, v_gate_a_w), gate_a_b=(gate_a_b, m_gate_a_b, v_gate_a_b),
                 gate_x_w=(gate_x_w, m_gate_x_w, v_gate_x_w), gate_x_b=(gate_x_b, m_gate_x_b, v_gate_x_b),
                 lru_lambda=(lru_lambda, m_lru_lambda, v_lru_lambda), lru_out_norm=(lru_out_norm, m_lru_out_norm, v_lru_out_norm),
                 ret_out_norm=(ret_out_norm, m_ret_out_norm, v_ret_out_norm), norm_ffn=(norm_ffn, m_norm_ffn, v_norm_ffn),
                 norm_final=tuple(a.reshape(1, D) for a in (norm_final, m_norm_final, v_norm_final)),
                 conv_w=(conv_w, m_conv_w, v_conv_w), meta_tokens=(meta_tokens, m_meta_tokens, v_meta_tokens))
    small_names = REP_NAMES + ["conv_w", "meta_tokens"]
    upd = adamw_small([small_g[n] for n in small_names], *[[given[n][k] for n in small_names] for k in range(3)],
                      "adamw_small")
    small_out = [dict(zip(small_names, u)) for u in upd]
    for d_ in [small_g] + small_out:
        d_["norm_final"] = d_["norm_final"].reshape(D)

    arrived = {}

    def wait_for(entries, after):
        for l, tag, names, flying in entries:
            sums, recv = rs_chips_wait(*flying, after, f"rs_chips_wait_{tag}")
            for i, n in enumerate(names):
                arrived[l, n] = (recv[i], sums[i])

    chip = jnp.reshape(2 * xi + yi, (1,)).astype(jnp.int32)

    def finish(wname, w_, m_, v_, tr):
        return adamw_big([arrived[l, wname][0] for l in range(DEPTH)], [arrived[l, wname][1] for l in range(DEPTH)], chip,
                         w_, m_, v_, tr, "adamw_" + wname)

    wait_for(inflight[:-1], upd[0][0])
    o_gate = [tr_(o) for o in finish("w_gate", w_gate_t, m_w_gate_t, v_w_gate_t, 32)]
    o_up = [tr_(o) for o in finish("w_up", w_up_t, m_w_up_t, v_w_up_t, 32)]
    o_down = finish("w_down", w_down, m_w_down, v_w_down, 32)
    wait_for(inflight[-1:], o_down[0])
    o_in = finish("w_in", w_in, m_w_in, v_w_in, 256)
    o_out = finish("w_out", w_out, m_w_out, v_w_out, 64)

    bigs = dict(w_in=o_in, w_out=o_out, w_gate=o_gate, w_up=o_up, w_down=o_down)
    order = ["meta_tokens", "norm_mix", "w_in", "conv_w", "conv_b", "gate_a_w", "gate_a_b", "gate_x_w", "gate_x_b", "lru_lambda",
             "lru_out_norm", "ret_out_norm", "w_out", "norm_ffn", "w_gate", "w_up", "w_down", "norm_final"]
    grads = [bigs[n][0] if n in bigs else small_g[n] for n in order]
    rest = [[bigs[n][k + 1] if n in bigs else small_out[k][n] for n in order] for k in range(3)]
    return (loss, grad_x, *grads, *rest[0], *rest[1], *rest[2])
```

```python
import functools

import numpy as np
import jax
import jax.numpy as jnp
from jax import lax
from jax.experimental import pallas as pl
from jax.experimental.pallas import tpu as pltpu

F32, BF16 = jnp.float32, jnp.bfloat16
MXU_DTYPE = BF16
WIRE_DTYPE = BF16

D = 1024
SEQ = 2048
DEPTH = 4
N_META = 16
CH = 128
PAD = (-(SEQ + N_META)) % CH
T = SEQ + N_META + PAD
NCH = T // CH
X0 = PAD + N_META
D_LRU = 512
LRU_BLOCKS = 8
LRU_BD = 64
CONV_W = 4
LRU_C = 8.0
D_RET = 512
HEADS = 4
HD = 128
ROPE_BASE = 10000.0
D_IN = 3072
D_FF = 2816
NDEV = 8
IN_SH = D_IN // NDEV
FF_SH = D_FF // NDEV
FF_SHP = 384
D_FFP = NDEV * FF_SHP
OUT_SH = D // NDEV
EPS = 1e-6
TM = 544
TR = 1088
VMEM_LIMIT = 56 * 2**20
MESH = pl.DeviceIdType.MESH

ADAM_LR, ADAM_B1, ADAM_B2, ADAM_EPS, ADAM_WD, ADAM_STEP = 0.001, 0.9, 0.999, 1e-08, 0.01, 10

NN = ((1,), (0,))
NT = ((1,), (1,))
TN = ((0,), (0,))


def _dot(a, b, dims):
    return lax.dot_general(a.astype(MXU_DTYPE), b.astype(MXU_DTYPE), (dims, ((), ())), preferred_element_type=F32)


def _sds(shape, dtype):
    return jax.ShapeDtypeStruct(shape, dtype)


def _params(sem=None):
    return pltpu.CompilerParams(dimension_semantics=sem, vmem_limit_bytes=VMEM_LIMIT)


def _full(shape):
    n = len(shape)
    return pl.BlockSpec(shape, lambda *_: (0,) * n)


def rmsnorm_fwd(h, gain, name):
    def body(h_ref, g_ref, o_ref):
        x = h_ref[...]
        ms = jnp.mean(x * x, axis=-1, keepdims=True)
        o_ref[...] = (x * lax.rsqrt(ms + EPS) * g_ref[...]).astype(o_ref.dtype)

    return pl.pallas_call(
        body, name=name, grid=(T // TM,),
        in_specs=[pl.BlockSpec((TM, D), lambda i: (i, 0)), _full((1, D))],
        out_specs=pl.BlockSpec((TM, D), lambda i: (i, 0)),
        out_shape=_sds((T, D), MXU_DTYPE), compiler_params=_params(("parallel",)),
    )(h, gain)


def rmsnorm_bwd(h, gain, dhn, dres, name):
    def body(h_ref, g_ref, dhn_ref, dres_ref, dh_ref, dhb_ref, dg_ref):
        x = h_ref[...]
        rstd = lax.rsqrt(jnp.mean(x * x, axis=-1, keepdims=True) + EPS)
        xhat = x * rstd
        dy = dhn_ref[...]
        dyg = dy * g_ref[...]
        dh = dres_ref[...] + rstd * (dyg - xhat * jnp.mean(dyg * xhat, axis=-1, keepdims=True))
        dh_ref[...] = dh
        dhb_ref[...] = dh.astype(dhb_ref.dtype)

        @pl.when(pl.program_id(0) == 0)
        def _():
            dg_ref[...] = jnp.zeros_like(dg_ref)
        dg_ref[...] += jnp.sum(dy * xhat, axis=0, keepdims=True)

    row = pl.BlockSpec((TM, D), lambda i: (i, 0))
    return pl.pallas_call(
        body, name=name, grid=(T // TM,),
        in_specs=[row, _full((1, D)), row, row],
        out_specs=[row, row, _full((1, D))],
        out_shape=[_sds((T, D), F32), _sds((T, D), MXU_DTYPE), _sds((1, D), F32)], compiler_params=_params(("arbitrary",)),
    )(h, gain, dhn, dres)


def loss_head(h, gain, target, name):
    def body(h_ref, g_ref, t_ref, loss_ref, dh_ref, dhb_ref, dg_ref):
        i = pl.program_id(0)

        @pl.when(i == 0)
        def _():
            loss_ref[...] = jnp.zeros_like(loss_ref)
            dg_ref[...] = jnp.zeros_like(dg_ref)
            dh_ref[...] = jnp.zeros_like(dh_ref)
            dhb_ref[...] = jnp.zeros_like(dhb_ref)

        @pl.when(i > 0)
        def _():
            x = h_ref[...]
            g = g_ref[...]
            rstd = lax.rsqrt(jnp.mean(x * x, axis=-1, keepdims=True) + EPS)
            xhat = x * rstd
            err = xhat * g - t_ref[...]
            loss_ref[...] += 0.5 * jnp.sum(jnp.mean(err * err, axis=-1, keepdims=True), axis=0, keepdims=True)
            dy = err * (1.0 / D)
            dyg = dy * g
            dh = rstd * (dyg - xhat * jnp.mean(dyg * xhat, axis=-1, keepdims=True))
            dh_ref[...] = dh
            dhb_ref[...] = dh.astype(dhb_ref.dtype)
            dg_ref[...] += jnp.sum(dy * xhat, axis=0, keepdims=True)

    row = pl.BlockSpec((CH, D), lambda i: (i, 0))
    return pl.pallas_call(
        body, name=name, grid=(NCH,),
        in_specs=[row, _full((1, D)), pl.BlockSpec((CH, D), lambda i: (jnp.maximum(i - 1, 0), 0))],
        out_specs=[_full((8, 128)), row, row, _full((1, D))],
        out_shape=[_sds((8, 128), F32), _sds((T, D), F32), _sds((T, D), MXU_DTYPE), _sds((1, D), F32)],
        compiler_params=_params(("arbitrary",)),
    )(h, gain, target)


PAIR = 2 * IN_SH
NPAIR = NDEV // 2
BN = 256
FB = 512


def _pair_cols(w_ref):
    return jnp.concatenate([w_ref[0], w_ref[1]], axis=1)


W_PAIR = lambda k: pl.BlockSpec((2, k, IN_SH), lambda j: (j, 0, 0))
COLS_PAIR = pl.BlockSpec((T, PAIR), lambda j: (0, j))
ANYSPEC = pl.BlockSpec(memory_space=pl.ANY)


def mm_blocked_nn(a, w, out_dtype, name):
    k = a.shape[1]

    def body(a_ref, w_ref, o_ref):
        o_ref[...] = _dot(a_ref[...], _pair_cols(w_ref), NN).astype(o_ref.dtype)

    return pl.pallas_call(
        body, name=name, grid=(NPAIR,),
        in_specs=[_full((T, k)), W_PAIR(k)], out_specs=COLS_PAIR,
        out_shape=_sds((T, NDEV * IN_SH), out_dtype), compiler_params=_params(("parallel",)),
    )(a, w)


def mm_nn_res(a, w, res, after, name):
    k = a.shape[1]

    def body(a_ref, w_ref, r_ref, after_ref, o_ref):
        del after_ref
        o_ref[...] = r_ref[...] + _dot(a_ref[...], w_ref[...], NN)

    col = pl.BlockSpec((T, BN), lambda j: (0, j))
    return pl.pallas_call(
        body, name=name, grid=(D // BN,),
        in_specs=[_full((T, k)), pl.BlockSpec((k, BN), lambda j: (0, j)), col, ANYSPEC], out_specs=col,
        out_shape=_sds((T, D), F32), compiler_params=_params(("parallel",)),
    )(a, w, res, after)


def ffn_up(hn, wg, wu, name):
    def body(a_ref, wg_ref, wu_ref, g_ref, u_ref, act_ref):
        a = a_ref[...]
        g = _dot(a, wg_ref[...], NT)
        u = _dot(a, wu_ref[...], NT)
        for r in range(T // TR):
            rows = slice(TR * r, TR * (r + 1))
            gr, ur = g[rows], u[rows]
            g_ref[rows, :] = gr.astype(g_ref.dtype)
            u_ref[rows, :] = ur.astype(u_ref.dtype)
            act_ref[rows, :] = (jax.nn.silu(gr) * ur).astype(act_ref.dtype)

    wspec = pl.BlockSpec((FB, D), lambda j: (j, 0))
    ospec = pl.BlockSpec((T, FB), lambda j: (0, j))
    return pl.pallas_call(
        body, name=name, grid=(D_FFP // FB,),
        in_specs=[_full((T, D)), wspec, wspec], out_specs=[ospec] * 3,
        out_shape=[_sds((T, D_FFP), MXU_DTYPE)] * 3, compiler_params=_params(("parallel",)),
    )(hn, wg, wu)


def ffn_down_bwd(dh, wd, gate, up, after, name):
    def body(dh_ref, wd_ref, g_ref, u_ref, after_ref, dg_ref, du_ref):
        del after_ref
        dact_all = _dot(dh_ref[...], wd_ref[...], NT)
        for r in range(T // TR):
            rows = slice(TR * r, TR * (r + 1))
            dact = dact_all[rows]
            g = g_ref[rows, :].astype(F32)
            u = u_ref[rows, :].astype(F32)
            sg = jax.nn.sigmoid(g)
            dg_ref[rows, :] = (dact * u * (sg * (1.0 + g * (1.0 - sg)))).astype(dg_ref.dtype)
            du_ref[rows, :] = (dact * (g * sg)).astype(du_ref.dtype)

    blk = pl.BlockSpec((T, FB), lambda j: (0, j))
    return pl.pallas_call(
        body, name=name, grid=(D_FFP // FB,),
        in_specs=[_full((T, D)), pl.BlockSpec((FB, D), lambda j: (j, 0)), blk, blk, ANYSPEC],
        out_specs=[blk, blk],
        out_shape=[_sds((T, D_FFP), MXU_DTYPE)] * 2, compiler_params=_params(("parallel",)),
    )(dh, wd, gate, up, after)


def mm_blocked_nt(pairs, after, name):
    n = len(pairs)

    def body(*refs):
        o_ref = refs[2 * n + 1]

        @pl.when(pl.program_id(0) == 0)
        def _():
            o_ref[...] = jnp.zeros_like(o_ref)
        for p in range(n):
            o_ref[...] += _dot(refs[2 * p][...], _pair_cols(refs[2 * p + 1]), NT)

    specs, args = [], []
    for a, w in pairs:
        specs += [COLS_PAIR, W_PAIR(D)]
        args += [a, w]
    return pl.pallas_call(
        body, name=name, grid=(NPAIR,), in_specs=specs + [ANYSPEC], out_specs=_full((T, D)),
        out_shape=_sds((T, D), F32), compiler_params=_params(("arbitrary",)),
    )(*args, after)


def mm_tn_two(a1, a2, b, bm, after, name):
    m = a1.shape[1]

    def body(a1_ref, a2_ref, b_ref, after_ref, o1_ref, o2_ref):
        del after_ref
        b = b_ref[...]
        o1_ref[...] = _dot(a1_ref[...], b, TN).astype(o1_ref.dtype)
        o2_ref[...] = _dot(a2_ref[...], b, TN).astype(o2_ref.dtype)

    blk = pl.BlockSpec((T, bm), lambda i: (0, i))
    out = pl.BlockSpec((bm, D), lambda i: (i, 0))
    return pl.pallas_call(
        body, name=name, grid=(m // bm,),
        in_specs=[blk, blk, _full((T, D)), ANYSPEC], out_specs=[out, out],
        out_shape=[_sds((m, D), WIRE_DTYPE)] * 2, compiler_params=_params(("parallel",)),
    )(a1, a2, b, after)


def out_proj_bwd(dh, w, ymix, after, name):
    def body(dh_ref, w_ref, y_ref, after_ref, dy_ref, dw_ref):
        del after_ref
        dh_ = dh_ref[...]
        dy_ref[...] = _dot(dh_, w_ref[...], NT)
        dw_ref[...] = _dot(y_ref[...], dh_, TN).astype(dw_ref.dtype)

    return pl.pallas_call(
        body, name=name, grid=(D // BN,),
        in_specs=[_full((T, D)), pl.BlockSpec((BN, D), lambda j: (j, 0)), pl.BlockSpec((T, BN), lambda j: (0, j)), ANYSPEC],
        out_specs=[pl.BlockSpec((T, BN), lambda j: (0, j)), pl.BlockSpec((BN, D), lambda j: (j, 0))],
        out_shape=[_sds((T, D), F32), _sds((D, D), WIRE_DTYPE)], compiler_params=_params(("parallel",)),
    )(dh, w, ymix, after)


def mm_rows_nn(pairs, after, name):
    n = len(pairs)

    def body(*refs):
        o_ref = refs[2 * n + 1]

        @pl.when(pl.program_id(0) == 0)
        def _():
            o_ref[...] = jnp.zeros_like(o_ref)
        for p in range(n):
            o_ref[...] += _dot(refs[2 * p][...], refs[2 * p + 1][...], NN)

    specs, args = [], []
    for a, w in pairs:
        specs += [pl.BlockSpec((T, FB), lambda j: (0, j)), pl.BlockSpec((FB, D), lambda j: (j, 0))]
        args += [a, w]
    return pl.pallas_call(
        body, name=name, grid=(D_FFP // FB,), in_specs=specs + [ANYSPEC], out_specs=_full((T, D)),
        out_shape=_sds((T, D), F32), compiler_params=_params(("arbitrary",)),
    )(*args, after)


def mm_tn_blocked(a, b, name):
    def body(a_ref, b_ref, o_ref):
        o = _dot(a_ref[...], b_ref[...], TN).astype(o_ref.dtype)
        o_ref[0] = o[:, :IN_SH]
        o_ref[1] = o[:, IN_SH:]

    return pl.pallas_call(
        body, name=name, grid=(NPAIR,),
        in_specs=[_full((T, D)), COLS_PAIR], out_specs=W_PAIR(D),
        out_shape=_sds((NDEV, D, IN_SH), WIRE_DTYPE), compiler_params=_params(("parallel",)),
    )(a, b)


def mm_tn(a, b, bm, after, name):
    m = a.shape[1]

    def body(a_ref, b_ref, after_ref, o_ref):
        del after_ref
        o_ref[...] = _dot(a_ref[...], b_ref[...], TN).astype(o_ref.dtype)

    return pl.pallas_call(
        body, name=name, grid=(m // bm,),
        in_specs=[pl.BlockSpec((T, bm), lambda i: (0, i)), _full((T, D)), ANYSPEC],
        out_specs=pl.BlockSpec((bm, D), lambda i: (i, 0)),
        out_shape=_sds((m, D), WIRE_DTYPE), compiler_params=_params(("parallel",)),
    )(a, b, after)


def _softplus_neg(lam):
    return jnp.maximum(-lam, 0.0) + jnp.log1p(jnp.exp(-jnp.abs(lam)))


def _lru_gates(pa, px, xc, lam):
    r = jax.nn.sigmoid(pa)
    ig = jax.nn.sigmoid(px)
    sp = _softplus_neg(lam)
    log_a = -LRU_C * r * sp
    a = jnp.exp(log_a)
    mult = jnp.sqrt(-jnp.tanh(log_a) * (a * a + 1.0))
    return a, mult * (ig * xc), (r, ig, sp, mult)


def _lru_gates_vjp(da, db, xc, lam, a, r, ig, sp, mult):
    dmult = db * (ig * xc)
    du = db * mult
    dlog_a = da * a - dmult * (a * a) / mult
    dr = dlog_a * (-LRU_C * sp)
    dlam = jnp.sum(dlog_a * (-LRU_C * r), axis=0, keepdims=True) * (-jax.nn.sigmoid(-lam))
    dpa = dr * (r * (1.0 - r))
    dpx = (du * xc) * (ig * (1.0 - ig))
    return dpa, dpx, du * ig, dlam


def _lru_out(h, g, gain):
    z = h * jax.nn.gelu(g)
    return z * lax.rsqrt(jnp.mean(z * z, axis=-1, keepdims=True) + EPS) * gain


def _conv_taps(x, xprev, row):
    taps = [x]
    for s in range(1, CONV_W):
        taps.append(jnp.where(row < s, pltpu.roll(xprev, s, 0), pltpu.roll(x, s, 0)))
    return taps


def _conv(taps, cw_ref, cb):
    xc = cb + cw_ref[CONV_W - 1:CONV_W, :] * taps[0]
    for s in range(1, CONV_W):
        xc = xc + cw_ref[CONV_W - 1 - s:CONV_W - s, :] * taps[s]
    return xc


def lru_fwd(proj, cw, cb, wa, ba, wx, bx, lam, gain, name):
    def body(x_ref, g_ref, cw_ref, cb_ref, wa_ref, ba_ref, wx_ref, bx_ref, lam_ref, gain_ref,
             y_ref, h_ref, xprev_scr, a_scr, b_scr, carry_scr):
        i = pl.program_id(0)

        @pl.when(i == 0)
        def _():
            xprev_scr[...] = jnp.zeros_like(xprev_scr)
            carry_scr[...] = jnp.zeros_like(carry_scr)

        x = x_ref[...]
        row = lax.broadcasted_iota(jnp.int32, (CH, D_LRU), 0)
        xc = _conv(_conv_taps(x, xprev_scr[...], row), cw_ref, cb_ref[...])
        pa = _dot(xc, wa_ref[...], NN) + ba_ref[...]
        px = _dot(xc, wx_ref[...], NN) + bx_ref[...]
        a, b, _ = _lru_gates(pa, px, xc, lam_ref[...])
        a_scr[...] = a
        b_scr[...] = jnp.where(i * CH + row >= PAD, b, 0.0)
        h = carry_scr[...]
        for t in range(CH):
            h = a_scr[t:t + 1, :] * h + b_scr[t:t + 1, :]
            h_ref[t:t + 1, :] = h
        carry_scr[...] = h
        xprev_scr[...] = x
        y_ref[...] = _lru_out(h_ref[...], g_ref[...], gain_ref[...]).astype(y_ref.dtype)

    vec = _full((1, D_LRU))
    mat = _full((D_LRU, D_LRU))
    return pl.pallas_call(
        body, name=name, grid=(NCH,),
        in_specs=[pl.BlockSpec((CH, D_LRU), lambda i: (i, 0)), pl.BlockSpec((CH, D_LRU), lambda i: (i, 1)),
                  _full((CONV_W, D_LRU)), vec, mat, vec, mat, vec, vec, vec],
        out_specs=[pl.BlockSpec((CH, D_LRU), lambda i: (i, 0)), pl.BlockSpec((CH, D_LRU), lambda i: (i, 0))],
        out_shape=[_sds((T, D_LRU), MXU_DTYPE), _sds((T, D_LRU), F32)],
        scratch_shapes=[pltpu.VMEM((CH, D_LRU), F32), pltpu.VMEM((CH, D_LRU), F32), pltpu.VMEM((CH, D_LRU), F32),
                        pltpu.VMEM((1, D_LRU), F32)],
        compiler_params=_params(("arbitrary",)),
    )(proj, proj, cw, cb, wa, ba, wx, bx, lam, gain)


LRU_VEC_ROWS = 16


def lru_bwd(proj, hst, dymix, cw, cb, wa, ba, wx, bx, lam, gain, after, name):
    last = NCH - 1

    def body(x_ref, xp_ref, g_ref, h_ref, hp_ref, dy_ref, cw_ref, cb_ref, wa_ref, ba_ref, wx_ref, bx_ref, lam_ref,
             gain_ref, after_ref, dxg_ref, vec_ref, dwa_ref, dwx_ref, a_scr, dh_scr, g_scr, carry_scr, dxcn_scr):
        del after_ref
        i = pl.program_id(0)
        ib = last - i

        @pl.when(i == 0)
        def _():
            carry_scr[...] = jnp.zeros_like(carry_scr)
            dxcn_scr[...] = jnp.zeros_like(dxcn_scr)
            vec_ref[...] = jnp.zeros_like(vec_ref)
            dwa_ref[...] = jnp.zeros_like(dwa_ref)
            dwx_ref[...] = jnp.zeros_like(dwx_ref)

        x = x_ref[...]
        row = lax.broadcasted_iota(jnp.int32, (CH, D_LRU), 0)
        valid = ib * CH + row >= PAD
        taps = _conv_taps(x, xp_ref[...], row)
        xc = _conv(taps, cw_ref, cb_ref[...])
        pa = _dot(xc, wa_ref[...], NN) + ba_ref[...]
        px = _dot(xc, wx_ref[...], NN) + bx_ref[...]
        a, _, gate_parts = _lru_gates(pa, px, xc, lam_ref[...])
        h = h_ref[...]
        _, vjp_out = jax.vjp(_lru_out, h, g_ref[...], gain_ref[...])
        dh, dg, dgain = vjp_out(dy_ref[...].astype(F32))
        a_scr[...] = a
        dh_scr[...] = dh
        c = carry_scr[...]
        for t in range(CH - 1, -1, -1):
            gt = dh_scr[t:t + 1, :] + c
            g_scr[t:t + 1, :] = gt
            c = a_scr[t:t + 1, :] * gt
        carry_scr[...] = c
        gg = g_scr[...]
        hprev = jnp.where(row < 1, pltpu.roll(hp_ref[...], 1, 0), pltpu.roll(h, 1, 0))
        da = jnp.where(valid, gg * hprev, 0.0)
        db = jnp.where(valid, gg, 0.0)
        dpa, dpx, dxc, dlam = _lru_gates_vjp(da, db, xc, lam_ref[...], a, *gate_parts)
        dxc = dxc + _dot(dpa, wa_ref[...], NT) + _dot(dpx, wx_ref[...], NT)
        dwa_ref[...] += _dot(xc, dpa, TN)
        dwx_ref[...] += _dot(xc, dpx, TN)
        for s in range(CONV_W):
            vec_ref[CONV_W - 1 - s:CONV_W - s, :] += jnp.sum(dxc * taps[s], axis=0, keepdims=True)
        vec_ref[4:5, :] += jnp.sum(dxc, axis=0, keepdims=True)
        vec_ref[5:6, :] += jnp.sum(dpa, axis=0, keepdims=True)
        vec_ref[6:7, :] += jnp.sum(dpx, axis=0, keepdims=True)
        vec_ref[7:8, :] += dlam
        vec_ref[8:9, :] += dgain
        dxn = dxcn_scr[...]
        dx = cw_ref[CONV_W - 1:CONV_W, :] * dxc
        for s in range(1, CONV_W):
            ahead = jnp.where(row >= CH - s, pltpu.roll(dxn, CH - s, 0), pltpu.roll(dxc, CH - s, 0))
            dx = dx + cw_ref[CONV_W - 1 - s:CONV_W - s, :] * ahead
        dxcn_scr[...] = dxc
        dxg_ref[:, :D_LRU] = jnp.where(valid, dx, 0.0).astype(dxg_ref.dtype)
        dxg_ref[:, D_LRU:] = dg.astype(dxg_ref.dtype)

    vec = _full((1, D_LRU))
    mat = _full((D_LRU, D_LRU))

    def blk(col, shift=0):
        return pl.BlockSpec((CH, D_LRU), lambda i: (jnp.maximum(last - i - shift, 0), col))

    return pl.pallas_call(
        body, name=name, grid=(NCH,),
        in_specs=[blk(0), blk(0, 1), blk(1), blk(0), blk(0, 1), blk(0),
                  _full((CONV_W, D_LRU)), vec, mat, vec, mat, vec, vec, vec, pl.BlockSpec(memory_space=pl.ANY)],
        out_specs=[pl.BlockSpec((CH, 2 * D_LRU), lambda i: (last - i, 0)), _full((LRU_VEC_ROWS, D_LRU)), mat, mat],
        out_shape=[_sds((T, 2 * D_LRU), MXU_DTYPE), _sds((LRU_VEC_ROWS, D_LRU), F32),
                   _sds((D_LRU, D_LRU), F32), _sds((D_LRU, D_LRU), F32)],
        scratch_shapes=[pltpu.VMEM((CH, D_LRU), F32), pltpu.VMEM((CH, D_LRU), F32), pltpu.VMEM((CH, D_LRU), F32),
                        pltpu.VMEM((1, D_LRU), F32), pltpu.VMEM((CH, D_LRU), F32)],
        compiler_params=_params(("arbitrary",)),
    )(proj, proj, proj, hst, hst, dymix, cw, cb, wa, ba, wx, bx, lam, gain, after)


def _ret_tables():
    half = HD // 2
    pos = jnp.arange(T, dtype=F32) - float(PAD)
    inv = ROPE_BASE ** (-jnp.arange(half, dtype=F32) / half)
    ang = pos[:, None] * inv[None, :]
    cos = jnp.concatenate([jnp.cos(ang), jnp.cos(ang)], axis=-1)
    sin = jnp.concatenate([-jnp.sin(ang), jnp.sin(ang)], axis=-1)
    log_g = jnp.log(1.0 - 2.0 ** (-5.0 - jnp.arange(HEADS, dtype=F32)))
    idx = jnp.arange(CH, dtype=F32)
    diff = idx[:, None] - idx[None, :]
    dmask = jnp.where(diff[None] >= 0, jnp.exp(jnp.maximum(diff, 0.0)[None] * log_g[:, None, None]), 0.0)
    xi = jnp.exp((idx + 1.0)[None, :] * log_g[:, None])
    zeta = jnp.exp((CH - 1.0 - idx)[None, :] * log_g[:, None])
    xi = jnp.broadcast_to(xi[:, :, None], (HEADS, CH, HD))
    zeta = jnp.broadcast_to(zeta[:, :, None], (HEADS, CH, HD))
    return cos, sin, dmask, xi, zeta


def _chunk_decay():
    log_g = np.log(np.float32(1.0) - np.float32(2.0) ** (np.float32(-5.0) - np.arange(HEADS, dtype=np.float32)))
    return [float(v) for v in np.exp(np.float32(CH) * log_g.astype(np.float32))]


def _rope(x, cos, sin):
    return x * cos + pltpu.roll(x, HD // 2, 1) * sin


def ret_fwd(proj, ylru, tables, gain, after, name):
    cos, sin, dmask, xi, zeta = tables
    gch = _chunk_decay()
    scale = HD ** -0.5

    def body(q_ref, k_ref, v_ref, g_ref, cos_ref, sin_ref, dm_ref, xi_ref, zt_ref, gain_ref, ylru_ref, after_ref,
             y_ref, st_ref, s_scr):
        del after_ref

        @pl.when(pl.program_id(0) == 0)
        def _():
            s_scr[...] = jnp.zeros_like(s_scr)

        y_ref[:, :D_LRU] = ylru_ref[...]
        cs, sn = cos_ref[...], sin_ref[...]
        hs = range(HEADS)
        sl = [slice(HD * h, HD * (h + 1)) for h in hs]
        qr = [_rope(q_ref[:, sl[h]], cs, sn).astype(MXU_DTYPE) for h in hs]
        kf = [_rope(k_ref[:, sl[h]], cs, sn) * scale for h in hs]
        kr = [kf[h].astype(MXU_DTYPE) for h in hs]
        v = [v_ref[:, sl[h]].astype(MXU_DTYPE) for h in hs]
        s = [s_scr[h] for h in hs]
        for h in hs:
            st_ref[h] = s[h]
        sc = [_dot(qr[h], kr[h], NT) * dm_ref[h] for h in hs]
        cross = [_dot(qr[h], s[h], NN) * xi_ref[h] for h in hs]
        for h in hs:
            s_scr[h] = s[h] * gch[h] + _dot(kf[h] * zt_ref[h], v[h], TN)
        y = [_dot(sc[h], v[h], NN) + cross[h] for h in hs]
        yc = [y[h] - jnp.mean(y[h], axis=-1, keepdims=True) for h in hs]
        yn = [yc[h] * lax.rsqrt(jnp.mean(yc[h] * yc[h], axis=-1, keepdims=True) + EPS) for h in hs]
        for h in hs:
            so = slice(D_LRU + HD * h, D_LRU + HD * (h + 1))
            y_ref[:, so] = (jax.nn.silu(g_ref[:, sl[h]]) * (yn[h] * gain_ref[:, sl[h]])).astype(y_ref.dtype)

    def col(c):
        return pl.BlockSpec((CH, D_RET), lambda n: (n, c))

    tab = pl.BlockSpec((CH, HD), lambda n: (n, 0))
    cst = _full((HEADS, CH, HD))
    return pl.pallas_call(
        body, name=name, grid=(NCH,),
        in_specs=[col(2), col(3), col(4), col(5), tab, tab, cst, cst, cst, _full((1, D_RET)), col(0),
                  pl.BlockSpec(memory_space=pl.ANY)],
        out_specs=[pl.BlockSpec((CH, D), lambda n: (n, 0)), pl.BlockSpec((None, HEADS, HD, HD), lambda n: (n, 0, 0, 0))],
        out_shape=[_sds((T, D), MXU_DTYPE), _sds((NCH, HEADS, HD, HD), F32)],
        scratch_shapes=[pltpu.VMEM((HEADS, HD, HD), F32)],
        compiler_params=_params(("arbitrary",)),
    )(proj, proj, proj, proj, cos, sin, dmask, xi, zeta, gain, ylru, after)


def ret_bwd(proj, states, dymix, dxg, tables, gain, name):
    cos, sin, dmask, xi, zeta = tables
    gch = _chunk_decay()
    scale = HD ** -0.5
    last = NCH - 1

    def body(q_ref, k_ref, v_ref, g_ref, st_ref, do_ref, cos_ref, sin_ref, dm_ref, xi_ref, zt_ref, gain_ref, dxg_ref,
             dp_ref, dgain_ref, ds_scr):
        @pl.when(pl.program_id(0) == 0)
        def _():
            ds_scr[...] = jnp.zeros_like(ds_scr)
            dgain_ref[...] = jnp.zeros_like(dgain_ref)

        dp_ref[:, :2 * D_LRU] = dxg_ref[...]
        cs, sn = cos_ref[...], sin_ref[...]
        hs = range(HEADS)
        sl = [slice(HD * h, HD * (h + 1)) for h in hs]

        def out(j, h):
            return slice(2 * D_LRU + j * D_RET + HD * h, 2 * D_LRU + j * D_RET + HD * (h + 1))

        b16 = lambda xs: [x.astype(MXU_DTYPE) for x in xs]
        qr = b16([_rope(q_ref[:, sl[h]], cs, sn) for h in hs])
        kf = [_rope(k_ref[:, sl[h]], cs, sn) * scale for h in hs]
        kr = b16(kf)
        kz = b16([kf[h] * zt_ref[h] for h in hs])
        v = b16([v_ref[:, sl[h]] for h in hs])
        s = b16([st_ref[h] for h in hs])
        ds = [ds_scr[h] for h in hs]
        dsb = b16(ds)
        sc = [_dot(qr[h], kr[h], NT) * dm_ref[h] for h in hs]
        scb = b16(sc)
        y = [_dot(scb[h], v[h], NN) + _dot(qr[h], s[h], NN) * xi_ref[h] for h in hs]
        yc = [y[h] - jnp.mean(y[h], axis=-1, keepdims=True) for h in hs]
        rstd = [lax.rsqrt(jnp.mean(yc[h] * yc[h], axis=-1, keepdims=True) + EPS) for h in hs]
        yn = [yc[h] * rstd[h] for h in hs]
        dy = []
        for h in hs:
            g = g_ref[:, sl[h]]
            gain = gain_ref[:, sl[h]]
            sg = jax.nn.sigmoid(g)
            silu = g * sg
            dout = do_ref[:, sl[h]].astype(F32)
            dgain_ref[:, sl[h]] += jnp.sum(dout * silu * yn[h], axis=0, keepdims=True)
            dp_ref[:, out(3, h)] = (dout * yn[h] * gain * (sg * (1.0 + g * (1.0 - sg)))).astype(dp_ref.dtype)
            dyn = dout * silu * gain
            dy.append(rstd[h] * (dyn - jnp.mean(dyn, axis=-1, keepdims=True)
                                 - yn[h] * jnp.mean(dyn * yn[h], axis=-1, keepdims=True)))
        dyb = b16(dy)
        dqs = b16([dy[h] * xi_ref[h] for h in hs])
        dp = b16([_dot(dyb[h], v[h], NT) * dm_ref[h] for h in hs])
        dv = [_dot(scb[h], dyb[h], TN) + _dot(kz[h], dsb[h], NN) for h in hs]
        dqr = [_dot(dp[h], kr[h], NN) + _dot(dqs[h], s[h], NT) for h in hs]
        dkr = [_dot(dp[h], qr[h], TN) + _dot(v[h], dsb[h], NT) * zt_ref[h] for h in hs]
        for h in hs:
            ds_scr[h] = gch[h] * ds[h] + _dot(qr[h], dqs[h], TN)
        for h in hs:
            dp_ref[:, out(0, h)] = (dqr[h] * cs + pltpu.roll(dqr[h] * sn, HD // 2, 1)).astype(dp_ref.dtype)
            dp_ref[:, out(1, h)] = ((dkr[h] * cs + pltpu.roll(dkr[h] * sn, HD // 2, 1)) * scale).astype(dp_ref.dtype)
            dp_ref[:, out(2, h)] = dv[h].astype(dp_ref.dtype)

    def col(c):
        return pl.BlockSpec((CH, D_RET), lambda n: (last - n, c))

    tab = pl.BlockSpec((CH, HD), lambda n: (last - n, 0))
    cst = _full((HEADS, CH, HD))
    return pl.pallas_call(
        body, name=name, grid=(NCH,),
        in_specs=[col(2), col(3), col(4), col(5), pl.BlockSpec((None, HEADS, HD, HD), lambda n: (last - n, 0, 0, 0)), col(1),
                  tab, tab, cst, cst, cst, _full((1, D_RET)), pl.BlockSpec((CH, 2 * D_LRU), lambda n: (last - n, 0))],
        out_specs=[pl.BlockSpec((CH, D_IN), lambda n: (last - n, 0)), _full((1, D_RET))],
        out_shape=[_sds((T, D_IN), MXU_DTYPE), _sds((1, D_RET), F32)],
        scratch_shapes=[pltpu.VMEM((HEADS, HD, HD), F32)],
        compiler_params=_params(("arbitrary",)),
    )(proj, proj, proj, proj, states, dymix, cos, sin, dmask, xi, zeta, gain, dxg)


HBM = pl.BlockSpec(memory_space=pltpu.HBM)


def _place():
    return lax.axis_index("x"), lax.axis_index("y"), lax.axis_index("c")


def all_gather(arrs, name):
    n = len(arrs)

    def body(*refs):
        ins, outs = refs[:n], refs[n:2 * n]
        send_sems, recv_sems, local_sems = refs[2 * n:]
        x, y, c = _place()
        me, sibling = (x, y, c), (x, y, 1 - c)
        chips = [(1 - x, y), (x, 1 - y), (1 - x, 1 - y)]

        def copy(a, k, block, to, src=None):
            px, py, pc = block
            dst = outs[a].at[4 * px + 2 * py + pc]
            return pltpu.make_async_remote_copy(
                src_ref=dst if src is None else src, dst_ref=dst, send_sem=send_sems.at[a, k], recv_sem=recv_sems.at[a, k],
                device_id=to, device_id_type=MESH)

        mine = [pltpu.make_async_copy(ins[a], outs[a].at[4 * x + 2 * y + c], local_sems.at[a]) for a in range(n)]
        for cp in mine:
            cp.start()
        first = []
        for a in range(n):
            first.append(copy(a, 0, me, sibling, src=ins[a]))
            first += [copy(a, 1 + j, me, (*chip, c), src=ins[a]) for j, chip in enumerate(chips)]
        for cp in first:
            cp.start()
        passed = []
        for j, chip in enumerate(chips):
            for a in range(n):
                copy(a, 1 + j, (*chip, c), me).wait_recv()
                passed.append(copy(a, 4 + j, (*chip, c), sibling))
                passed[-1].start()
        for a in range(n):
            copy(a, 0, sibling, me).wait_recv()
            for j, chip in enumerate(chips):
                copy(a, 4 + j, (*chip, 1 - c), me).wait_recv()
        for cp in first + passed:
            cp.wait_send()
        for cp in mine:
            cp.wait()

    return pl.pallas_call(
        body, name=name,
        in_specs=[HBM] * n, out_specs=[HBM] * n,
        out_shape=[_sds((NDEV,) + a.shape, a.dtype) for a in arrs],
        scratch_shapes=[pltpu.SemaphoreType.DMA((n, 7)), pltpu.SemaphoreType.DMA((n, 7)), pltpu.SemaphoreType.DMA((n,))],
    )(*arrs)


SEM = pl.BlockSpec(memory_space=pltpu.SEMAPHORE)
ANY = pl.BlockSpec(memory_space=pl.ANY)
EFFECT = pltpu.SideEffectType.DATAFLOW_SIDE_EFFECTING


def _hbm(a):
    return pltpu.with_memory_space_constraint(a, pltpu.HBM)


def _hbm_like(arrs):
    return [pltpu.HBM(a.shape, a.dtype) for a in arrs]


def _dma_sems(count):
    return [pltpu.SemaphoreType.DMA(())] * count


def _ag_copy(lands, send_sems, recv_sems, per):
    def copy(a, k, block, to, src=None):
        px, py, pc = block
        dst = lands[a].at[4 * px + 2 * py + pc]
        return pltpu.make_async_remote_copy(
            src_ref=dst if src is None else src, dst_ref=dst, send_sem=send_sems[a * per + k], recv_sem=recv_sems[a * per + k],
            device_id=to, device_id_type=MESH)
    return copy


def to_wire(sel, w_in, w_gate, w_up, w_out, w_down, name):
    ffpad = FF_SHP - FF_SH

    def body(sel_ref, i_ref, g_ref, u_ref, o_ref, d_ref, oi, og, ou, oo, od):
        del sel_ref
        oi[...] = i_ref[...].astype(oi.dtype)
        oo[...] = o_ref[...].astype(oo.dtype)
        for src, dst in ((g_ref, og), (u_ref, ou), (d_ref, od)):
            dst[:FF_SH, :] = src[...].astype(dst.dtype)
            dst[FF_SH:, :] = jnp.zeros((ffpad, D), dst.dtype)

    shapes_in = [(D, IN_SH), (FF_SH, D), (FF_SH, D), (OUT_SH, D), (FF_SH, D)]
    shapes_out = [(D, IN_SH), (FF_SHP, D), (FF_SHP, D), (OUT_SH, D), (FF_SHP, D)]
    return pl.pallas_call(
        body, name=name,
        grid_spec=pltpu.PrefetchScalarGridSpec(
            num_scalar_prefetch=1, grid=(1,),
            in_specs=[pl.BlockSpec((None,) + s, lambda i, sel_ref: (sel_ref[1], 0, 0)) for s in shapes_in],
            out_specs=[pl.BlockSpec((None,) + s, lambda i, sel_ref: (sel_ref[0], 0, 0)) for s in shapes_out]),
        out_shape=[_sds((NDEV,) + s, WIRE_DTYPE) for s in shapes_out], compiler_params=_params(("arbitrary",)),
    )(sel, w_in, w_gate, w_up, w_out, w_down)


def place_blocks(sel, arrs, name):
    n = len(arrs)

    def body(sel_ref, *refs):
        del sel_ref
        for a in range(n):
            refs[n + a][...] = refs[a][...]

    def whole(a):
        nd = a.ndim
        return pl.BlockSpec(a.shape, lambda i, sel_ref: (0,) * nd)

    def mine(a):
        nd = a.ndim
        return pl.BlockSpec((None,) + a.shape, lambda i, sel_ref: (sel_ref[0],) + (0,) * nd)

    return pl.pallas_call(
        body, name=name,
        grid_spec=pltpu.PrefetchScalarGridSpec(
            num_scalar_prefetch=1, grid=(1,), in_specs=[whole(a) for a in arrs], out_specs=[mine(a) for a in arrs]),
        out_shape=[_sds((NDEV,) + a.shape, a.dtype) for a in arrs], compiler_params=_params(("arbitrary",)),
    )(sel, *arrs)


def ag_start(lands, after, name):
    n = len(lands)
    ns = 4 * n

    def body(*refs):
        lnd = refs[:n]
        send_sems, recv_sems = refs[n + 1:n + 1 + ns], refs[n + 1 + ns:n + 1 + 2 * ns]
        token = refs[-1]
        x, y, c = _place()
        me, sibling = (x, y, c), (x, y, 1 - c)
        chips = [(1 - x, y), (x, 1 - y), (1 - x, 1 - y)]
        copy = _ag_copy(lnd, send_sems, recv_sems, 4)
        for a in range(n):
            copy(a, 0, me, sibling).start()
            for j, chip in enumerate(chips):
                copy(a, 1 + j, me, (*chip, c)).start()
        token[...] = jnp.zeros_like(token)

    outs = pl.pallas_call(
        body, name=name,
        in_specs=[HBM] * n + [ANY],
        out_specs=[SEM] * (2 * ns) + [HBM] * n + [pl.BlockSpec(memory_space=pltpu.VMEM)],
        out_shape=_dma_sems(2 * ns) + _hbm_like(lands) + [_sds((8, 128), F32)],
        input_output_aliases={i: 2 * ns + i for i in range(n)},
        compiler_params=pltpu.CompilerParams(has_side_effects=EFFECT),
    )(*[_hbm(a) for a in lands], after)
    return outs[:ns], outs[ns:2 * ns], outs[2 * ns:2 * ns + n], outs[-1]


def ag_forward(send_sems, recv_sems, lands, after, name):
    n = len(lands)
    n1, n2 = 4 * n, 3 * n

    def body(*refs):
        lnd = refs[:n]
        o = n
        s1, r1 = refs[o:o + n1], refs[o + n1:o + 2 * n1]
        o += 2 * n1 + 1
        s2, r2 = refs[o:o + n2], refs[o + n2:o + 2 * n2]
        token = refs[-1]
        token[...] = jnp.zeros_like(token)
        x, y, c = _place()
        me, sibling = (x, y, c), (x, y, 1 - c)
        chips = [(1 - x, y), (x, 1 - y), (1 - x, 1 - y)]
        copy1 = _ag_copy(lnd, s1, r1, 4)
        copy2 = _ag_copy(lnd, s2, r2, 3)
        for j, chip in enumerate(chips):
            for a in range(n):
                copy1(a, 1 + j, (*chip, c), me).wait_recv()
                copy2(a, j, (*chip, c), sibling).start()
        for a in range(n):
            copy1(a, 0, sibling, me).wait_recv()
            copy1(a, 0, me, sibling).wait_send()
            for j, chip in enumerate(chips):
                copy1(a, 1 + j, me, (*chip, c)).wait_send()

    outs = pl.pallas_call(
        body, name=name,
        in_specs=[HBM] * n + [SEM] * (2 * n1) + [ANY],
        out_specs=[SEM] * (2 * n2) + [HBM] * n + [pl.BlockSpec(memory_space=pltpu.VMEM)],
        out_shape=_dma_sems(2 * n2) + _hbm_like(lands) + [_sds((8, 128), F32)],
        input_output_aliases={i: 2 * n2 + i for i in range(n)},
        compiler_params=pltpu.CompilerParams(has_side_effects=EFFECT),
    )(*lands, *send_sems, *recv_sems, after)
    return outs[:n2], outs[n2:2 * n2], outs[2 * n2:2 * n2 + n], outs[-1]


def ag_finish(send_sems, recv_sems, lands, after, name):
    n = len(lands)
    n2 = 3 * n

    def body(*refs):
        lnd = refs[:n]
        s2, r2 = refs[n:n + n2], refs[n + n2:n + 2 * n2]
        x, y, c = _place()
        me, sibling = (x, y, c), (x, y, 1 - c)
        chips = [(1 - x, y), (x, 1 - y), (1 - x, 1 - y)]
        copy2 = _ag_copy(lnd, s2, r2, 3)
        for a in range(n):
            for j, chip in enumerate(chips):
                copy2(a, j, (*chip, c), sibling).wait_send()
                copy2(a, j, (*chip, 1 - c), me).wait_recv()

    outs = pl.pallas_call(
        body, name=name,
        in_specs=[HBM] * n + [SEM] * (2 * n2) + [ANY],
        out_specs=[HBM] * n, out_shape=_hbm_like(lands),
        input_output_aliases={i: i for i in range(n)},
        compiler_params=pltpu.CompilerParams(has_side_effects=EFFECT),
    )(*lands, *send_sems, *recv_sems, after)
    return list(outs)


def rs_sibling_start(arrs, name):
    n = len(arrs)
    ns = 4 * n
    lands = [lax.empty((4,) + a.shape[1:], a.dtype) for a in arrs]

    def body(*refs):
        ins, lnd = refs[:n], refs[n:2 * n]
        send_sems, recv_sems = refs[2 * n:2 * n + ns], refs[2 * n + ns:2 * n + 2 * ns]
        x, y, c = _place()
        sibling = (x, y, 1 - c)
        for a in range(n):
            for p in range(4):
                pltpu.make_async_remote_copy(
                    src_ref=ins[a].at[2 * p + 1 - c], dst_ref=lnd[a].at[p], send_sem=send_sems[4 * a + p],
                    recv_sem=recv_sems[4 * a + p], device_id=sibling, device_id_type=MESH).start()
        refs[-1][...] = jnp.zeros_like(refs[-1])

    outs = pl.pallas_call(
        body, name=name,
        in_specs=[HBM] * (2 * n), out_specs=[SEM] * (2 * ns) + [HBM] * (2 * n) + [pl.BlockSpec(memory_space=pltpu.VMEM)],
        out_shape=_dma_sems(2 * ns) + _hbm_like(arrs) + _hbm_like(lands) + [_sds((8, 128), F32)],
        input_output_aliases={i: 2 * ns + i for i in range(2 * n)},
        compiler_params=pltpu.CompilerParams(has_side_effects=EFFECT),
    )(*[_hbm(a) for a in arrs], *[_hbm(a) for a in lands])
    return (outs[:ns], outs[ns:2 * ns], outs[2 * ns:2 * ns + n], outs[2 * ns + n:2 * ns + 2 * n]), outs[-1]


def rs_sibling_wait(send_sems, recv_sems, arrs, lands, after, name):
    n = len(arrs)
    ns = 4 * n

    def body(*refs):
        ins, lnd = refs[:n], refs[n:2 * n]
        s, r = refs[2 * n:2 * n + ns], refs[2 * n + ns:2 * n + 2 * ns]
        x, y, c = _place()
        sibling = (x, y, 1 - c)
        for a in range(n):
            for p in range(4):
                cp = pltpu.make_async_remote_copy(
                    src_ref=ins[a].at[2 * p + 1 - c], dst_ref=lnd[a].at[p], send_sem=s[4 * a + p], recv_sem=r[4 * a + p],
                    device_id=sibling, device_id_type=MESH)
                cp.wait_send()
                cp.wait_recv()

    outs = pl.pallas_call(
        body, name=name,
        in_specs=[HBM] * (2 * n) + [SEM] * (2 * ns) + [ANY], out_specs=[HBM] * (2 * n),
        out_shape=_hbm_like(arrs) + _hbm_like(lands),
        input_output_aliases={i: i for i in range(2 * n)},
        compiler_params=pltpu.CompilerParams(has_side_effects=EFFECT),
    )(*arrs, *lands, *send_sems, *recv_sems, after)
    return outs[:n], outs[n:]


def rs_chips_start(parts, name):
    n = len(parts)
    ns = 3 * n
    lands = [lax.empty((3,) + a.shape[1:], a.dtype) for a in parts]

    def body(*refs):
        ins, lnd = refs[:n], refs[n:2 * n]
        send_sems, recv_sems = refs[2 * n:2 * n + ns], refs[2 * n + ns:2 * n + 2 * ns]
        x, y, c = _place()
        chips = [(1 - x, y), (x, 1 - y), (1 - x, 1 - y)]
        for a in range(n):
            for k, (tx, ty) in enumerate(chips):
                pltpu.make_async_remote_copy(
                    src_ref=ins[a].at[2 * tx + ty], dst_ref=lnd[a].at[k], send_sem=send_sems[3 * a + k],
                    recv_sem=recv_sems[3 * a + k], device_id=(tx, ty, c), device_id_type=MESH).start()
        refs[-1][...] = jnp.zeros_like(refs[-1])

    outs = pl.pallas_call(
        body, name=name,
        in_specs=[HBM] * (2 * n), out_specs=[SEM] * (2 * ns) + [HBM] * (2 * n) + [pl.BlockSpec(memory_space=pltpu.VMEM)],
        out_shape=_dma_sems(2 * ns) + _hbm_like(parts) + _hbm_like(lands) + [_sds((8, 128), F32)],
        input_output_aliases={i: 2 * ns + i for i in range(2 * n)},
        compiler_params=pltpu.CompilerParams(has_side_effects=EFFECT),
    )(*[_hbm(a) for a in parts], *[_hbm(a) for a in lands])
    return (outs[:ns], outs[ns:2 * ns], outs[2 * ns:2 * ns + n], outs[2 * ns + n:2 * ns + 2 * n]), outs[-1]


def rs_chips_wait(send_sems, recv_sems, parts, lands, after, name):
    n = len(parts)
    ns = 3 * n

    def body(*refs):
        ins, lnd = refs[:n], refs[n:2 * n]
        s, r = refs[2 * n:2 * n + ns], refs[2 * n + ns:2 * n + 2 * ns]
        x, y, c = _place()
        chips = [(1 - x, y), (x, 1 - y), (1 - x, 1 - y)]
        for a in range(n):
            for k, (tx, ty) in enumerate(chips):
                cp = pltpu.make_async_remote_copy(
                    src_ref=ins[a].at[2 * tx + ty], dst_ref=lnd[a].at[k], send_sem=s[3 * a + k], recv_sem=r[3 * a + k],
                    device_id=(tx, ty, c), device_id_type=MESH)
                cp.wait_send()
                cp.wait_recv()

    outs = pl.pallas_call(
        body, name=name,
        in_specs=[HBM] * (2 * n) + [SEM] * (2 * ns) + [ANY], out_specs=[HBM] * (2 * n),
        out_shape=_hbm_like(parts) + _hbm_like(lands),
        input_output_aliases={i: i for i in range(2 * n)},
        compiler_params=pltpu.CompilerParams(has_side_effects=EFFECT),
    )(*parts, *lands, *send_sems, *recv_sems, after)
    return outs[:n], outs[n:]


def pair_sum(arrs, recv, c, name):
    n = len(arrs)

    def body(c_ref, *refs):
        del c_ref
        for a in range(n):
            refs[2 * n + a][...] = (refs[a][...].astype(F32) + refs[n + a][...].astype(F32)).astype(refs[2 * n + a].dtype)

    mine = [pl.BlockSpec((None,) + a.shape[1:], lambda p, c_ref: (2 * p + c_ref[0], 0, 0)) for a in arrs]
    other = [pl.BlockSpec((None,) + a.shape[1:], lambda p, c_ref: (p, 0, 0)) for a in arrs]
    return pl.pallas_call(
        body, name=name,
        grid_spec=pltpu.PrefetchScalarGridSpec(num_scalar_prefetch=1, grid=(4,), in_specs=mine + other, out_specs=other),
        out_shape=[_sds((4,) + a.shape[1:], a.dtype) for a in arrs], compiler_params=_params(("parallel",)),
    )(c, *arrs, *recv)


def _adamw(w, g, m, v):
    m = ADAM_B1 * m + (1.0 - ADAM_B1) * g
    v = ADAM_B2 * v + (1.0 - ADAM_B2) * jnp.square(g)
    m_hat = m / (1.0 - ADAM_B1 ** ADAM_STEP)
    v_hat = v / (1.0 - ADAM_B2 ** ADAM_STEP)
    return -ADAM_LR * (m_hat / (jnp.sqrt(v_hat) + ADAM_EPS) + ADAM_WD * w), m, v


def adamw_big(recv, sums, chip, w, m, v, tr, name):
    nl, rr, cc = w.shape
    cp = recv[0].shape[2]

    def body(chip_ref, *refs):
        del chip_ref
        rcv, own = refs[:nl], refs[nl:2 * nl]
        w_ref, m_ref, v_ref, g_out, d_out, m_out, v_out = refs[2 * nl:]
        for l in range(nl):
            g = ((own[l][...].astype(F32) + rcv[l][0].astype(F32)) + rcv[l][1].astype(F32)) + rcv[l][2].astype(F32)
            g = g[:, :cc]
            g_out[l] = g
            d_out[l], m_out[l], v_out[l] = _adamw(w_ref[l], g, m_ref[l], v_ref[l])

    blk = pl.BlockSpec((nl, tr, cc), lambda i, chip_ref: (0, i, 0))
    return pl.pallas_call(
        body, name=name,
        grid_spec=pltpu.PrefetchScalarGridSpec(
            num_scalar_prefetch=1, grid=(rr // tr,),
            in_specs=[pl.BlockSpec((3, tr, cp), lambda i, chip_ref: (0, i, 0))] * nl
            + [pl.BlockSpec((None, tr, cp), lambda i, chip_ref: (chip_ref[0], i, 0))] * nl + [blk, blk, blk],
            out_specs=[blk] * 4),
        out_shape=[_sds(w.shape, F32)] * 4, compiler_params=_params(("parallel",)),
    )(chip, *recv, *sums, w, m, v)


SMALL_ROWS = 16


def small_grads(lvec, g_ret, g_mix, g_ffn, g_final, dwa, dwx, name):
    def body(lvec_ref, ret_ref, mix_ref, ffn_ref, fin_ref, dwa_ref, dwx_ref, v_ref, g_ref):
        v_ref[0:9, :] = lvec_ref[0:9, :]
        v_ref[9:10, :] = ret_ref[...]
        for r, src in ((10, mix_ref), (12, ffn_ref), (14, fin_ref)):
            v_ref[r:r + 1, :] = src[:, :D_LRU]
            v_ref[r + 1:r + 2, :] = src[:, D_LRU:]
        for k, src in enumerate((dwa_ref, dwx_ref)):
            for g in range(LRU_BLOCKS):
                rows = slice(LRU_BD * g, LRU_BD * (g + 1))
                g_ref[D_LRU * k + LRU_BD * g:D_LRU * k + LRU_BD * (g + 1), :] = src[rows, rows]

    ins = [lvec, g_ret, g_mix, g_ffn, g_final, dwa, dwx]
    return pl.pallas_call(
        body, name=name, grid=(1,), in_specs=[_full(a.shape) for a in ins],
        out_specs=[_full((SMALL_ROWS, D_LRU)), _full((2 * D_LRU, LRU_BD))],
        out_shape=[_sds((SMALL_ROWS, D_LRU), F32), _sds((2 * D_LRU, LRU_BD), F32)], compiler_params=_params(("arbitrary",)),
    )(*ins)


def sum_devices(arrs, name):
    n = len(arrs)

    def body(*refs):
        for a in range(n):
            acc = refs[a][0]
            for j in range(1, NDEV):
                acc = acc + refs[a][j]
            refs[n + a][...] = acc

    return pl.pallas_call(
        body, name=name, grid=(1,), in_specs=[_full(a.shape) for a in arrs], out_specs=[_full(a.shape[1:]) for a in arrs],
        out_shape=[_sds(a.shape[1:], F32) for a in arrs], compiler_params=_params(("arbitrary",)),
    )(*arrs)


def adamw_small(gs, ws, ms, vs, name):
    n = len(gs)

    def body(*refs):
        for a in range(n):
            g, w, m, v = (refs[k * n + a][...] for k in range(4))
            refs[4 * n + a][...], refs[5 * n + a][...], refs[6 * n + a][...] = _adamw(w, g, m, v)

    specs = [_full(a.shape) for a in ws]
    outs = pl.pallas_call(
        body, name=name, grid=(1,), in_specs=specs * 4, out_specs=specs * 3, out_shape=[_sds(a.shape, F32) for a in ws] * 3,
        compiler_params=_params(("arbitrary",)),
    )(*gs, *ws, *ms, *vs)
    return outs[:n], outs[n:2 * n], outs[2 * n:]


def block_diag(wa, wx, name):
    def body(wa_ref, wx_ref, oa_ref, ox_ref):
        for src, dst in ((wa_ref, oa_ref), (wx_ref, ox_ref)):
            dst[...] = jnp.zeros_like(dst)
            for g in range(LRU_BLOCKS):
                rows = slice(LRU_BD * g, LRU_BD * (g + 1))
                dst[rows, rows] = src[g].astype(dst.dtype)

    ispec = pl.BlockSpec((None, LRU_BLOCKS, LRU_BD, LRU_BD), lambda l: (l, 0, 0, 0))
    ospec = pl.BlockSpec((None, D_LRU, D_LRU), lambda l: (l, 0, 0))
    return pl.pallas_call(
        body, name=name, grid=(wa.shape[0],), in_specs=[ispec, ispec], out_specs=[ospec, ospec],
        out_shape=[_sds((wa.shape[0], D_LRU, D_LRU), MXU_DTYPE)] * 2, compiler_params=_params(("parallel",)),
    )(wa, wx)


REP_NAMES = ["norm_mix", "conv_b", "gate_a_w", "gate_a_b", "gate_x_w", "gate_x_b", "lru_lambda", "lru_out_norm",
             "ret_out_norm", "norm_ffn", "norm_final"]


def kernel(x, meta_tokens, norm_mix, w_in, conv_w, conv_b, gate_a_w, gate_a_b, gate_x_w, gate_x_b, lru_lambda, lru_out_norm, ret_out_norm, w_out, norm_ffn, w_gate, w_up, w_down, norm_final, loss_target, m_meta_tokens, m_norm_mix, m_w_in, m_conv_w, m_conv_b, m_gate_a_w, m_gate_a_b, m_gate_x_w, m_gate_x_b, m_lru_lambda, m_lru_out_norm, m_ret_out_norm, m_w_out, m_norm_ffn, m_w_gate, m_w_up, m_w_down, m_norm_final, v_meta_tokens, v_norm_mix, v_w_in, v_conv_w, v_conv_b, v_gate_a_w, v_gate_a_b, v_gate_x_w, v_gate_x_b, v_lru_lambda, v_lru_out_norm, v_ret_out_norm, v_w_out, v_norm_ffn, v_w_gate, v_w_up, v_w_down, v_norm_final):
    xi, yi, ci = _place()
    dev = 4 * xi + 2 * yi + ci
    c_arr = jnp.reshape(ci, (1,)).astype(jnp.int32)
    dev_arr = jnp.reshape(dev, (1,)).astype(jnp.int32)

    meta_g, conv_g = all_gather([meta_tokens, conv_w], "ag_small")
    meta_full = jnp.transpose(meta_g, (1, 0, 2)).reshape(N_META, D)
    conv_full = jnp.transpose(conv_g, (1, 2, 0, 3)).reshape(DEPTH, CONV_W, D_LRU)
    tr_ = lambda a: jnp.transpose(a, (0, 2, 1))
    w_gate_t, m_w_gate_t, v_w_gate_t = tr_(w_gate), tr_(m_w_gate), tr_(v_w_gate)
    w_up_t, m_w_up_t, v_w_up_t = tr_(w_up), tr_(m_w_up), tr_(v_w_up)
    level1 = []
    token = meta_g
    for l in range(DEPTH):
        sel = jnp.stack([dev, jnp.int32(l)]).astype(jnp.int32)
        lands = to_wire(sel, w_in, w_gate_t, w_up_t, w_out, w_down, "to_wire")
        s1, r1, lands, token = ag_start(lands, token, f"ag_start_{l}")
        level1.append((s1, r1, lands))

    def as_weights(gi, gg, gu, go, gd):
        return dict(w_in=gi, w_gate=gg.reshape(D_FFP, D), w_up=gu.reshape(D_FFP, D), w_out=go.reshape(D, D),
                    w_down=gd.reshape(D_FFP, D))

    tables = _ret_tables()
    row = lambda a: a.reshape(1, -1)

    h = jnp.concatenate([jnp.zeros((PAD, D), F32), meta_full, x[0]], axis=0)
    saved, gathered = [], []
    s1, r1, lands = level1[0]
    s2, r2, first, order = ag_forward(s1[:4], r1[:4], lands[:1], token, "ag_forward_0_w_in")
    w_in_next = ag_finish(s2, r2, first, h, "ag_finish_0_w_in")[0]
    wa_dense, wx_dense = block_diag(gate_a_w, gate_x_w, "block_diag")
    for l in range(DEPTH):
        small = dict(cw=conv_full[l], cb=row(conv_b[l]), wa=wa_dense[l], ba=row(gate_a_b[l]),
                     wx=wx_dense[l], bx=row(gate_x_b[l]), lam=row(lru_lambda[l]),
                     gain=row(lru_out_norm[l]))
        s1, r1, lands = level1[l]
        hn1 = rmsnorm_fwd(h, row(norm_mix[l]), "rms_fwd")
        proj = mm_blocked_nn(hn1, w_in_next, F32, "proj")
        ylru, hst = lru_fwd(proj, name="lru_fwd", **small)
        s2, r2, rest, order = ag_forward(s1[4:], r1[4:], lands[1:], ylru, f"ag_forward_{l}_rest")
        ymix, states = ret_fwd(proj, ylru, tables, row(ret_out_norm[l]), order, "ret_fwd")
        w = as_weights(w_in_next, *ag_finish(s2, r2, rest, ymix, f"ag_finish_{l}_rest"))
        gathered.append(w)
        h_mid = mm_nn_res(ymix, w["w_out"], h, order, "out_proj")
        hn2 = rmsnorm_fwd(h_mid, row(norm_ffn[l]), "rms_fwd")
        gate, up, act = ffn_up(hn2, w["w_gate"], w["w_up"], "ffn_up")
        if l + 1 < DEPTH:
            s1n, r1n, landsn = level1[l + 1]
            s2, r2, first, order = ag_forward(s1n[:4], r1n[:4], landsn[:1], act, f"ag_forward_{l + 1}_w_in")
        h_out = mm_nn_res(act, w["w_down"], h_mid, order, "ffn_down")
        if l + 1 < DEPTH:
            w_in_next = ag_finish(s2, r2, first, h_out, f"ag_finish_{l + 1}_w_in")[0]
        saved.append(dict(h=h, hn1=hn1, proj=proj, hst=hst, states=states, ymix=ymix, h_mid=h_mid, hn2=hn2, gate=gate, up=up,
                          act=act, small=small))
        h = h_out

    loss_p, dh, dh_b, g_norm_final = loss_head(h, row(norm_final), loss_target[0], "loss_head")
    loss = lax.psum(loss_p[0, 0], ("x", "y", "c"))

    small_v = [None] * DEPTH
    small_w = [None] * DEPTH
    inflight = []
    sib = None
    order = loss_p

    def sibling_done(l, tag, names, sib, after):
        parts, got = rs_sibling_wait(*sib, after, f"rs_sibling_wait_{tag}")
        sums = pair_sum(parts, got, c_arr, "pair_sum")
        flying, started = rs_chips_start(sums, f"rs_chips_start_{tag}")
        inflight.append((l, tag, names, flying))
        return started

    for l in reversed(range(DEPTH)):
        w, s = gathered[l], saved[l]
        dgate, dup = ffn_down_bwd(dh_b, w["w_down"], s["gate"], s["up"], order, "ffn_down_bwd")
        dwd = mm_tn(s["act"], dh_b, PAIR, order, "dw_down").reshape(NDEV, FF_SHP, D)
        dwg, dwu = (g.reshape(NDEV, FF_SHP, D) for g in mm_tn_two(dgate, dup, s["hn2"], PAIR, order, "dw_rows"))
        ffn_sib, order = rs_sibling_start([dwg, dwu, dwd], f"rs_sibling_start_{l}_ffn")
        dhn2 = mm_rows_nn([(dgate, w["w_gate"]), (dup, w["w_up"])], order, "ffn_up_bwd")
        if sib is not None:
            order = sibling_done(l + 1, f"{l + 1}_mix", ("w_in", "w_out"), sib, dhn2)
        dh_mid, dh_mid_b, g_norm_ffn = rmsnorm_bwd(s["h_mid"], row(norm_ffn[l]), dhn2, dh, "rms_bwd")
        dymix, dwo = out_proj_bwd(dh_mid_b, w["w_out"], s["ymix"], order, "out_proj_bwd")
        dwo = dwo.reshape(NDEV, OUT_SH, D)
        order = sibling_done(l, f"{l}_ffn", ("w_gate", "w_up", "w_down"), ffn_sib, dymix)
        dxg, lvec, dwa, dwx = lru_bwd(s["proj"], s["hst"], dymix, after=order, name="lru_bwd", **s["small"])
        dproj, g_ret_norm = ret_bwd(s["proj"], s["states"], dymix, dxg, tables, row(ret_out_norm[l]), "ret_bwd")
        dwi = mm_tn_blocked(s["hn1"], dproj, "dw_blocked")
        dhn1 = mm_blocked_nt([(dproj, w["w_in"])], order, "proj_bwd")
        dh, dh_b, g_norm_mix = rmsnorm_bwd(s["h"], row(norm_mix[l]), dhn1, dh_mid, "rms_bwd")

        g_fin = g_norm_final if l == 0 else jnp.zeros((1, D), F32)
        small_v[l], small_w[l] = small_grads(lvec, g_ret_norm, g_norm_mix, g_norm_ffn, g_fin, dwa, dwx, "small_grads")
        sib, order = rs_sibling_start([dwi, dwo], f"rs_sibling_start_{l}_mix")
        if l == 1:
            early = place_blocks(dev_arr, [jnp.stack(small_v[1:]), jnp.stack(small_w[1:])], "place_grads")
            early_sems = ag_start(early, order, "ag_start_grads")
            order = early_sems[3]

    grad_x = dh[X0:][None]
    g_meta = dh[PAD:X0]

    late = all_gather([small_v[0], small_w[0], g_meta], "ag_grads")
    s2, r2, lands, _ = ag_forward(early_sems[0], early_sems[1], early_sems[2], dh, "ag_forward_grads")
    gath_early = ag_finish(s2, r2, lands, late[0], "ag_finish_grads")
    sibling_done(0, "0_mix", ("w_in", "w_out"), sib, late[0])
    v0, w0, meta_sum, v123, w123 = sum_devices(list(late) + list(gath_early), "sum_devices")
    vecs = jnp.concatenate([v0[None], v123])
    gws = jnp.concatenate([w0[None], w123])
    blocks = (DEPTH, LRU_BLOCKS, LRU_BD)
    small_g = dict(
        conv_w=lax.dynamic_slice_in_dim(vecs[:, 0:CONV_W], dev * (D_LRU // NDEV), D_LRU // NDEV, axis=2),
        conv_b=vecs[:, 4], gate_a_b=vecs[:, 5].reshape(blocks), gate_x_b=vecs[:, 6].reshape(blocks),
        lru_lambda=vecs[:, 7], lru_out_norm=vecs[:, 8], ret_out_norm=vecs[:, 9],
        norm_mix=vecs[:, 10:12].reshape(DEPTH, D), norm_ffn=vecs[:, 12:14].reshape(DEPTH, D),
        norm_final=v0[14:16].reshape(1, D),
        gate_a_w=gws[:, :D_LRU].reshape(blocks + (LRU_BD,)), gate_x_w=gws[:, D_LRU:].reshape(blocks + (LRU_BD,)),
        meta_tokens=lax.dynamic_slice_in_dim(meta_sum, dev * (D // NDEV), D // NDEV, axis=1))
    given = dict(norm_mix=(norm_mix, m_norm_mix, v_norm_mix), conv_b=(conv_b, m_conv_b, v_conv_b),
                 gate_a_w=(gate_a_w, m_gate_a_w, v_gate_a_w), gate_a_b=(gate_a_b, m_gate_a_b, v_gate_a_b),
                 gate_x_w=(gate_x_w, m_gate_x_w, v_gate_x_w), gate_x_b=(gate_x_b, m_gate_x_b, v_gate_x_b),
                 lru_lambda=(lru_lambda, m_lru_lambda, v_lru_lambda), lru_out_norm=(lru_out_norm, m_lru_out_norm, v_lru_out_norm),
                 ret_out_norm=(ret_out_norm, m_ret_out_norm, v_ret_out_norm), norm_ffn=(norm_ffn, m_norm_ffn, v_norm_ffn),
                 norm_final=tuple(a.reshape(1, D) for a in (norm_final, m_norm_final, v_norm_final)),
                 conv_w=(conv_w, m_conv_w, v_conv_w), meta_tokens=(meta_tokens, m_meta_tokens, v_meta_tokens))
    small_names = REP_NAMES + ["conv_w", "meta_tokens"]
    upd = adamw_small([small_g[n] for n in small_names], *[[given[n][k] for n in small_names] for k in range(3)],
                      "adamw_small")
    small_out = [dict(zip(small_names, u)) for u in upd]
    for d_ in [small_g] + small_out:
        d_["norm_final"] = d_["norm_final"].reshape(D)

    arrived = {}

    def wait_for(entries, after):
        for l, tag, names, flying in entries:
            sums, recv = rs_chips_wait(*flying, after, f"rs_chips_wait_{tag}")
            for i, n in enumerate(names):
                arrived[l, n] = (recv[i], sums[i])

    chip = jnp.reshape(2 * xi + yi, (1,)).astype(jnp.int32)

    def finish(wname, w_, m_, v_, tr):
        return adamw_big([arrived[l, wname][0] for l in range(DEPTH)], [arrived[l, wname][1] for l in range(DEPTH)], chip,
                         w_, m_, v_, tr, "adamw_" + wname)

    wait_for(inflight[:-1], upd[0][0])
    o_gate = [tr_(o) for o in finish("w_gate", w_gate_t, m_w_gate_t, v_w_gate_t, 32)]
    o_up = [tr_(o) for o in finish("w_up", w_up_t, m_w_up_t, v_w_up_t, 32)]
    o_down = finish("w_down", w_down, m_w_down, v_w_down, 32)
    wait_for(inflight[-1:], o_down[0])
    o_in = finish("w_in", w_in, m_w_in, v_w_in, 256)
    o_out = finish("w_out", w_out, m_w_out, v_w_out, 64)

    bigs = dict(w_in=o_in, w_out=o_out, w_gate=o_gate, w_up=o_up, w_down=o_down)
    order = ["meta_tokens", "norm_mix", "w_in", "conv_w", "conv_b", "gate_a_w", "gate_a_b", "gate_x_w", "gate_x_b", "lru_lambda",
             "lru_out_norm", "ret_out_norm", "w_out", "norm_ffn", "w_gate", "w_up", "w_down", "norm_final"]
    grads = [bigs[n][0] if n in bigs else small_g[n] for n in order]
    rest = [[bigs[n][k + 1] if n in bigs else small_out[k][n] for n in order] for k in range(3)]
    return (loss, grad_x, *grads, *rest[0], *rest[1], *rest[2])
```

```python
import functools

import numpy as np
import jax
import jax.numpy as jnp
from jax import lax
from jax.experimental import pallas as pl
from jax.experimental.pallas import tpu as pltpu

F32, BF16 = jnp.float32, jnp.bfloat16
MXU_DTYPE = BF16
WIRE_DTYPE = BF16

D = 1024
SEQ = 2048
DEPTH = 4
N_META = 16
CH = 128
PAD = (-(SEQ + N_META)) % CH
T = SEQ + N_META + PAD
NCH = T // CH
X0 = PAD + N_META
D_LRU = 512
LRU_BLOCKS = 8
LRU_BD = 64
CONV_W = 4
LRU_C = 8.0
D_RET = 512
HEADS = 4
HD = 128
ROPE_BASE = 10000.0
D_IN = 3072
D_FF = 2816
NDEV = 8
IN_SH = D_IN // NDEV
FF_SH = D_FF // NDEV
FF_SHP = 384
D_FFP = NDEV * FF_SHP
OUT_SH = D // NDEV
EPS = 1e-6
TM = 544
TR = 1088
VMEM_LIMIT = 56 * 2**20
MESH = pl.DeviceIdType.MESH

ADAM_LR, ADAM_B1, ADAM_B2, ADAM_EPS, ADAM_WD, ADAM_STEP = 0.001, 0.9, 0.999, 1e-08, 0.01, 10

NN = ((1,), (0,))
NT = ((1,), (1,))
TN = ((0,), (0,))


def _dot(a, b, dims):
    return lax.dot_general(a.astype(MXU_DTYPE), b.astype(MXU_DTYPE), (dims, ((), ())), preferred_element_type=F32)


def _sds(shape, dtype):
    return jax.ShapeDtypeStruct(shape, dtype)


def _params(sem=None):
    return pltpu.CompilerParams(dimension_semantics=sem, vmem_limit_bytes=VMEM_LIMIT)


def _full(shape):
    n = len(shape)
    return pl.BlockSpec(shape, lambda *_: (0,) * n)


def rmsnorm_fwd(h, gain, name):
    def body(h_ref, g_ref, o_ref):
        x = h_ref[...]
        ms = jnp.mean(x * x, axis=-1, keepdims=True)
        o_ref[...] = (x * lax.rsqrt(ms + EPS) * g_ref[...]).astype(o_ref.dtype)

    return pl.pallas_call(
        body, name=name, grid=(T // TM,),
        in_specs=[pl.BlockSpec((TM, D), lambda i: (i, 0)), _full((1, D))],
        out_specs=pl.BlockSpec((TM, D), lambda i: (i, 0)),
        out_shape=_sds((T, D), MXU_DTYPE), compiler_params=_params(("parallel",)),
    )(h, gain)


def rmsnorm_bwd(h, gain, dhn, dres, name):
    def body(h_ref, g_ref, dhn_ref, dres_ref, dh_ref, dhb_ref, dg_ref):
        x = h_ref[...]
        rstd = lax.rsqrt(jnp.mean(x * x, axis=-1, keepdims=True) + EPS)
        xhat = x * rstd
        dy = dhn_ref[...]
        dyg = dy * g_ref[...]
        dh = dres_ref[...] + rstd * (dyg - xhat * jnp.mean(dyg * xhat, axis=-1, keepdims=True))
        dh_ref[...] = dh
        dhb_ref[...] = dh.astype(dhb_ref.dtype)

        @pl.when(pl.program_id(0) == 0)
        def _():
            dg_ref[...] = jnp.zeros_like(dg_ref)
        dg_ref[...] += jnp.sum(dy * xhat, axis=0, keepdims=True)

    row = pl.BlockSpec((TM, D), lambda i: (i, 0))
    return pl.pallas_call(
        body, name=name, grid=(T // TM,),
        in_specs=[row, _full((1, D)), row, row],
        out_specs=[row, row, _full((1, D))],
        out_shape=[_sds((T, D), F32), _sds((T, D), MXU_DTYPE), _sds((1, D), F32)], compiler_params=_params(("arbitrary",)),
    )(h, gain, dhn, dres)


def loss_head(h, gain, target, name):
    def body(h_ref, g_ref, t_ref, loss_ref, dh_ref, dhb_ref, dg_ref):
        i = pl.program_id(0)

        @pl.when(i == 0)
        def _():
            loss_ref[...] = jnp.zeros_like(loss_ref)
            dg_ref[...] = jnp.zeros_like(dg_ref)
            dh_ref[...] = jnp.zeros_like(dh_ref)
            dhb_ref[...] = jnp.zeros_like(dhb_ref)

        @pl.when(i > 0)
        def _():
            x = h_ref[...]
            g = g_ref[...]
            rstd = lax.rsqrt(jnp.mean(x * x, axis=-1, keepdims=True) + EPS)
            xhat = x * rstd
            err = xhat * g - t_ref[...]
            loss_ref[...] += 0.5 * jnp.sum(jnp.mean(err * err, axis=-1, keepdims=True), axis=0, keepdims=True)
            dy = err * (1.0 / D)
            dyg = dy * g
            dh = rstd * (dyg - xhat * jnp.mean(dyg * xhat, axis=-1, keepdims=True))
            dh_ref[...] = dh
            dhb_ref[...] = dh.astype(dhb_ref.dtype)
            dg_ref[...] += jnp.sum(dy * xhat, axis=0, keepdims=True)

    row = pl.BlockSpec((CH, D), lambda i: (i, 0))
    return pl.pallas_call(
        body, name=name, grid=(NCH,),
        in_specs=[row, _full((1, D)), pl.BlockSpec((CH, D), lambda i: (jnp.maximum(i - 1, 0), 0))],
        out_specs=[_full((8, 128)), row, row, _full((1, D))],
        out_shape=[_sds((8, 128), F32), _sds((T, D), F32), _sds((T, D), MXU_DTYPE), _sds((1, D), F32)],
        compiler_params=_params(("arbitrary",)),
    )(h, gain, target)


PAIR = 2 * IN_SH
NPAIR = NDEV // 2
BN = 256
FB = 512


def _pair_cols(w_ref):
    return jnp.concatenate([w_ref[0], w_ref[1]], axis=1)


W_PAIR = lambda k: pl.BlockSpec((2, k, IN_SH), lambda j: (j, 0, 0))
COLS_PAIR = pl.BlockSpec((T, PAIR), lambda j: (0, j))
ANYSPEC = pl.BlockSpec(memory_space=pl.ANY)


def mm_blocked_nn(a, w, out_dtype, name):
    k = a.shape[1]

    def body(a_ref, w_ref, o_ref):
        o_ref[...] = _dot(a_ref[...], _pair_cols(w_ref), NN).astype(o_ref.dtype)

    return pl.pallas_call(
        body, name=name, grid=(NPAIR,),
        in_specs=[_full((T, k)), W_PAIR(k)], out_specs=COLS_PAIR,
        out_shape=_sds((T, NDEV * IN_SH), out_dtype), compiler_params=_params(("parallel",)),
    )(a, w)


def mm_nn_res(a, w, res, after, name):
    k = a.shape[1]

    def body(a_ref, w_ref, r_ref, after_ref, o_ref):
        del after_ref
        o_ref[...] = r_ref[...] + _dot(a_ref[...], w_ref[...], NN)

    col = pl.BlockSpec((T, BN), lambda j: (0, j))
    return pl.pallas_call(
        body, name=name, grid=(D // BN,),
        in_specs=[_full((T, k)), pl.BlockSpec((k, BN), lambda j: (0, j)), col, ANYSPEC], out_specs=col,
        out_shape=_sds((T, D), F32), compiler_params=_params(("parallel",)),
    )(a, w, res, after)


def ffn_up(hn, wg, wu, name):
    def body(a_ref, wg_ref, wu_ref, dg_ref, du_ref, act_ref):
        a = a_ref[...]
        g = _dot(a, wg_ref[...], NT)
        u = _dot(a, wu_ref[...], NT)
        for r in range(T // TR):
            rows = slice(TR * r, TR * (r + 1))
            gr, ur = g[rows], u[rows]
            sg = jax.nn.sigmoid(gr)
            silu = gr * sg
            dg_ref[rows, :] = (ur * (sg * (1.0 + gr * (1.0 - sg)))).astype(dg_ref.dtype)
            du_ref[rows, :] = silu.astype(du_ref.dtype)
            act_ref[rows, :] = (silu * ur).astype(act_ref.dtype)

    wspec = pl.BlockSpec((FB, D), lambda j: (j, 0))
    ospec = pl.BlockSpec((T, FB), lambda j: (0, j))
    return pl.pallas_call(
        body, name=name, grid=(D_FFP // FB,),
        in_specs=[_full((T, D)), wspec, wspec], out_specs=[ospec] * 3,
        out_shape=[_sds((T, D_FFP), MXU_DTYPE)] * 3, compiler_params=_params(("parallel",)),
    )(hn, wg, wu)


def ffn_down_bwd(dh, wd, dact_dgate, dact_dup, after, name):
    def body(dh_ref, wd_ref, g_ref, u_ref, after_ref, dg_ref, du_ref):
        del after_ref
        dact_all = _dot(dh_ref[...], wd_ref[...], NT)
        for r in range(T // TR):
            rows = slice(TR * r, TR * (r + 1))
            dact = dact_all[rows]
            dg_ref[rows, :] = (dact * g_ref[rows, :].astype(F32)).astype(dg_ref.dtype)
            du_ref[rows, :] = (dact * u_ref[rows, :].astype(F32)).astype(du_ref.dtype)

    blk = pl.BlockSpec((T, FB), lambda j: (0, j))
    return pl.pallas_call(
        body, name=name, grid=(D_FFP // FB,),
        in_specs=[_full((T, D)), pl.BlockSpec((FB, D), lambda j: (j, 0)), blk, blk, ANYSPEC],
        out_specs=[blk, blk],
        out_shape=[_sds((T, D_FFP), MXU_DTYPE)] * 2, compiler_params=_params(("parallel",)),
    )(dh, wd, dact_dgate, dact_dup, after)


def mm_blocked_nt(pairs, after, name):
    n = len(pairs)

    def body(*refs):
        o_ref = refs[2 * n + 1]

        @pl.when(pl.program_id(0) == 0)
        def _():
            o_ref[...] = jnp.zeros_like(o_ref)
        for p in range(n):
            o_ref[...] += _dot(refs[2 * p][...], _pair_cols(refs[2 * p + 1]), NT)

    specs, args = [], []
    for a, w in pairs:
        specs += [COLS_PAIR, W_PAIR(D)]
        args += [a, w]
    return pl.pallas_call(
        body, name=name, grid=(NPAIR,), in_specs=specs + [ANYSPEC], out_specs=_full((T, D)),
        out_shape=_sds((T, D), F32), compiler_params=_params(("arbitrary",)),
    )(*args, after)


def mm_tn_two(a1, a2, b, bm, after, name):
    m = a1.shape[1]

    def body(a1_ref, a2_ref, b_ref, after_ref, o1_ref, o2_ref):
        del after_ref
        b = b_ref[...]
        o1_ref[...] = _dot(a1_ref[...], b, TN).astype(o1_ref.dtype)
        o2_ref[...] = _dot(a2_ref[...], b, TN).astype(o2_ref.dtype)

    blk = pl.BlockSpec((T, bm), lambda i: (0, i))
    out = pl.BlockSpec((bm, D), lambda i: (i, 0))
    return pl.pallas_call(
        body, name=name, grid=(m // bm,),
        in_specs=[blk, blk, _full((T, D)), ANYSPEC], out_specs=[out, out],
        out_shape=[_sds((m, D), WIRE_DTYPE)] * 2, compiler_params=_params(("parallel",)),
    )(a1, a2, b, after)


def out_proj_bwd(dh, w, ymix, after, name):
    def body(dh_ref, w_ref, y_ref, after_ref, dy_ref, dw_ref):
        del after_ref
        dh_ = dh_ref[...]
        dy_ref[...] = _dot(dh_, w_ref[...], NT)
        dw_ref[...] = _dot(y_ref[...], dh_, TN).astype(dw_ref.dtype)

    return pl.pallas_call(
        body, name=name, grid=(D // BN,),
        in_specs=[_full((T, D)), pl.BlockSpec((BN, D), lambda j: (j, 0)), pl.BlockSpec((T, BN), lambda j: (0, j)), ANYSPEC],
        out_specs=[pl.BlockSpec((T, BN), lambda j: (0, j)), pl.BlockSpec((BN, D), lambda j: (j, 0))],
        out_shape=[_sds((T, D), F32), _sds((D, D), WIRE_DTYPE)], compiler_params=_params(("parallel",)),
    )(dh, w, ymix, after)


def mm_rows_nn(pairs, after, name):
    n = len(pairs)

    def body(*refs):
        o_ref = refs[2 * n + 1]

        @pl.when(pl.program_id(0) == 0)
        def _():
            o_ref[...] = jnp.zeros_like(o_ref)
        for p in range(n):
            o_ref[...] += _dot(refs[2 * p][...], refs[2 * p + 1][...], NN)

    specs, args = [], []
    for a, w in pairs:
        specs += [pl.BlockSpec((T, FB), lambda j: (0, j)), pl.BlockSpec((FB, D), lambda j: (j, 0))]
        args += [a, w]
    return pl.pallas_call(
        body, name=name, grid=(D_FFP // FB,), in_specs=specs + [ANYSPEC], out_specs=_full((T, D)),
        out_shape=_sds((T, D), F32), compiler_params=_params(("arbitrary",)),
    )(*args, after)


def mm_tn_blocked(a, b, name):
    def body(a_ref, b_ref, o_ref):
        o = _dot(a_ref[...], b_ref[...], TN).astype(o_ref.dtype)
        o_ref[0] = o[:, :IN_SH]
        o_ref[1] = o[:, IN_SH:]

    return pl.pallas_call(
        body, name=name, grid=(NPAIR,),
        in_specs=[_full((T, D)), COLS_PAIR], out_specs=W_PAIR(D),
        out_shape=_sds((NDEV, D, IN_SH), WIRE_DTYPE), compiler_params=_params(("parallel",)),
    )(a, b)


def mm_tn(a, b, bm, after, name):
    m = a.shape[1]

    def body(a_ref, b_ref, after_ref, o_ref):
        del after_ref
        o_ref[...] = _dot(a_ref[...], b_ref[...], TN).astype(o_ref.dtype)

    return pl.pallas_call(
        body, name=name, grid=(m // bm,),
        in_specs=[pl.BlockSpec((T, bm), lambda i: (0, i)), _full((T, D)), ANYSPEC],
        out_specs=pl.BlockSpec((bm, D), lambda i: (i, 0)),
        out_shape=_sds((m, D), WIRE_DTYPE), compiler_params=_params(("parallel",)),
    )(a, b, after)


def _softplus_neg(lam):
    return jnp.maximum(-lam, 0.0) + jnp.log1p(jnp.exp(-jnp.abs(lam)))


def _lru_gates(pa, px, xc, lam):
    r = jax.nn.sigmoid(pa)
    ig = jax.nn.sigmoid(px)
    sp = _softplus_neg(lam)
    log_a = -LRU_C * r * sp
    a = jnp.exp(log_a)
    mult = jnp.sqrt(-jnp.tanh(log_a) * (a * a + 1.0))
    return a, mult * (ig * xc), (r, ig, sp, mult)


def _lru_gates_vjp(da, db, xc, lam, a, r, ig, sp, mult):
    dmult = db * (ig * xc)
    du = db * mult
    dlog_a = da * a - dmult * (a * a) / mult
    dr = dlog_a * (-LRU_C * sp)
    dlam = jnp.sum(dlog_a * (-LRU_C * r), axis=0, keepdims=True) * (-jax.nn.sigmoid(-lam))
    dpa = dr * (r * (1.0 - r))
    dpx = (du * xc) * (ig * (1.0 - ig))
    return dpa, dpx, du * ig, dlam


def _lru_out(h, g, gain):
    z = h * jax.nn.gelu(g)
    return z * lax.rsqrt(jnp.mean(z * z, axis=-1, keepdims=True) + EPS) * gain


def _conv_taps(x, xprev, row):
    taps = [x]
    for s in range(1, CONV_W):
        taps.append(jnp.where(row < s, pltpu.roll(xprev, s, 0), pltpu.roll(x, s, 0)))
    return taps


def _conv(taps, cw_ref, cb):
    xc = cb + cw_ref[CONV_W - 1:CONV_W, :] * taps[0]
    for s in range(1, CONV_W):
        xc = xc + cw_ref[CONV_W - 1 - s:CONV_W - s, :] * taps[s]
    return xc


def lru_fwd(proj, cw, cb, wa, ba, wx, bx, lam, gain, name):
    def body(x_ref, g_ref, cw_ref, cb_ref, wa_ref, ba_ref, wx_ref, bx_ref, lam_ref, gain_ref,
             y_ref, h_ref, xprev_scr, a_scr, b_scr, carry_scr):
        i = pl.program_id(0)

        @pl.when(i == 0)
        def _():
            xprev_scr[...] = jnp.zeros_like(xprev_scr)
            carry_scr[...] = jnp.zeros_like(carry_scr)

        x = x_ref[...]
        row = lax.broadcasted_iota(jnp.int32, (CH, D_LRU), 0)
        xc = _conv(_conv_taps(x, xprev_scr[...], row), cw_ref, cb_ref[...])
        pa = _dot(xc, wa_ref[...], NN) + ba_ref[...]
        px = _dot(xc, wx_ref[...], NN) + bx_ref[...]
        a, b, _ = _lru_gates(pa, px, xc, lam_ref[...])
        a_scr[...] = a
        b_scr[...] = jnp.where(i * CH + row >= PAD, b, 0.0)
        h = carry_scr[...]
        for t in range(CH):
            h = a_scr[t:t + 1, :] * h + b_scr[t:t + 1, :]
            h_ref[t:t + 1, :] = h
        carry_scr[...] = h
        xprev_scr[...] = x
        y_ref[...] = _lru_out(h_ref[...], g_ref[...], gain_ref[...]).astype(y_ref.dtype)

    vec = _full((1, D_LRU))
    mat = _full((D_LRU, D_LRU))
    return pl.pallas_call(
        body, name=name, grid=(NCH,),
        in_specs=[pl.BlockSpec((CH, D_LRU), lambda i: (i, 0)), pl.BlockSpec((CH, D_LRU), lambda i: (i, 1)),
                  _full((CONV_W, D_LRU)), vec, mat, vec, mat, vec, vec, vec],
        out_specs=[pl.BlockSpec((CH, D_LRU), lambda i: (i, 0)), pl.BlockSpec((CH, D_LRU), lambda i: (i, 0))],
        out_shape=[_sds((T, D_LRU), MXU_DTYPE), _sds((T, D_LRU), F32)],
        scratch_shapes=[pltpu.VMEM((CH, D_LRU), F32), pltpu.VMEM((CH, D_LRU), F32), pltpu.VMEM((CH, D_LRU), F32),
                        pltpu.VMEM((1, D_LRU), F32)],
        compiler_params=_params(("arbitrary",)),
    )(proj, proj, cw, cb, wa, ba, wx, bx, lam, gain)


LRU_VEC_ROWS = 16


def lru_bwd(proj, hst, dymix, cw, cb, wa, ba, wx, bx, lam, gain, after, name):
    last = NCH - 1

    def body(x_ref, xp_ref, g_ref, h_ref, hp_ref, dy_ref, cw_ref, cb_ref, wa_ref, ba_ref, wx_ref, bx_ref, lam_ref,
             gain_ref, after_ref, dxg_ref, vec_ref, dwa_ref, dwx_ref, a_scr, dh_scr, g_scr, carry_scr, dxcn_scr):
        del after_ref
        i = pl.program_id(0)
        ib = last - i

        @pl.when(i == 0)
        def _():
            carry_scr[...] = jnp.zeros_like(carry_scr)
            dxcn_scr[...] = jnp.zeros_like(dxcn_scr)
            vec_ref[...] = jnp.zeros_like(vec_ref)
            dwa_ref[...] = jnp.zeros_like(dwa_ref)
            dwx_ref[...] = jnp.zeros_like(dwx_ref)

        x = x_ref[...]
        row = lax.broadcasted_iota(jnp.int32, (CH, D_LRU), 0)
        valid = ib * CH + row >= PAD
        taps = _conv_taps(x, xp_ref[...], row)
        xc = _conv(taps, cw_ref, cb_ref[...])
        pa = _dot(xc, wa_ref[...], NN) + ba_ref[...]
        px = _dot(xc, wx_ref[...], NN) + bx_ref[...]
        a, _, gate_parts = _lru_gates(pa, px, xc, lam_ref[...])
        h = h_ref[...]
        _, vjp_out = jax.vjp(_lru_out, h, g_ref[...], gain_ref[...])
        dh, dg, dgain = vjp_out(dy_ref[...].astype(F32))
        a_scr[...] = a
        dh_scr[...] = dh
        c = carry_scr[...]
        for t in range(CH - 1, -1, -1):
            gt = dh_scr[t:t + 1, :] + c
            g_scr[t:t + 1, :] = gt
            c = a_scr[t:t + 1, :] * gt
        carry_scr[...] = c
        gg = g_scr[...]
        hprev = jnp.where(row < 1, pltpu.roll(hp_ref[...], 1, 0), pltpu.roll(h, 1, 0))
        da = jnp.where(valid, gg * hprev, 0.0)
        db = jnp.where(valid, gg, 0.0)
        dpa, dpx, dxc, dlam = _lru_gates_vjp(da, db, xc, lam_ref[...], a, *gate_parts)
        dxc = dxc + _dot(dpa, wa_ref[...], NT) + _dot(dpx, wx_ref[...], NT)
        dwa_ref[...] += _dot(xc, dpa, TN)
        dwx_ref[...] += _dot(xc, dpx, TN)
        for s in range(CONV_W):
            vec_ref[CONV_W - 1 - s:CONV_W - s, :] += jnp.sum(dxc * taps[s], axis=0, keepdims=True)
        vec_ref[4:5, :] += jnp.sum(dxc, axis=0, keepdims=True)
        vec_ref[5:6, :] += jnp.sum(dpa, axis=0, keepdims=True)
        vec_ref[6:7, :] += jnp.sum(dpx, axis=0, keepdims=True)
        vec_ref[7:8, :] += dlam
        vec_ref[8:9, :] += dgain
        dxn = dxcn_scr[...]
        dx = cw_ref[CONV_W - 1:CONV_W, :] * dxc
        for s in range(1, CONV_W):
            ahead = jnp.where(row >= CH - s, pltpu.roll(dxn, CH - s, 0), pltpu.roll(dxc, CH - s, 0))
            dx = dx + cw_ref[CONV_W - 1 - s:CONV_W - s, :] * ahead
        dxcn_scr[...] = dxc
        dxg_ref[:, :D_LRU] = jnp.where(valid, dx, 0.0).astype(dxg_ref.dtype)
        dxg_ref[:, D_LRU:] = dg.astype(dxg_ref.dtype)

    vec = _full((1, D_LRU))
    mat = _full((D_LRU, D_LRU))

    def blk(col, shift=0):
        return pl.BlockSpec((CH, D_LRU), lambda i: (jnp.maximum(last - i - shift, 0), col))

    return pl.pallas_call(
        body, name=name, grid=(NCH,),
        in_specs=[blk(0), blk(0, 1), blk(1), blk(0), blk(0, 1), blk(0),
                  _full((CONV_W, D_LRU)), vec, mat, vec, mat, vec, vec, vec, pl.BlockSpec(memory_space=pl.ANY)],
        out_specs=[pl.BlockSpec((CH, 2 * D_LRU), lambda i: (last - i, 0)), _full((LRU_VEC_ROWS, D_LRU)), mat, mat],
        out_shape=[_sds((T, 2 * D_LRU), MXU_DTYPE), _sds((LRU_VEC_ROWS, D_LRU), F32),
                   _sds((D_LRU, D_LRU), F32), _sds((D_LRU, D_LRU), F32)],
        scratch_shapes=[pltpu.VMEM((CH, D_LRU), F32), pltpu.VMEM((CH, D_LRU), F32), pltpu.VMEM((CH, D_LRU), F32),
                        pltpu.VMEM((1, D_LRU), F32), pltpu.VMEM((CH, D_LRU), F32)],
        compiler_params=_params(("arbitrary",)),
    )(proj, proj, proj, hst, hst, dymix, cw, cb, wa, ba, wx, bx, lam, gain, after)


def _ret_tables():
    half = HD // 2
    pos = jnp.arange(T, dtype=F32) - float(PAD)
    inv = ROPE_BASE ** (-jnp.arange(half, dtype=F32) / half)
    ang = pos[:, None] * inv[None, :]
    cos = jnp.concatenate([jnp.cos(ang), jnp.cos(ang)], axis=-1)
    sin = jnp.concatenate([-jnp.sin(ang), jnp.sin(ang)], axis=-1)
    log_g = jnp.log(1.0 - 2.0 ** (-5.0 - jnp.arange(HEADS, dtype=F32)))
    idx = jnp.arange(CH, dtype=F32)
    diff = idx[:, None] - idx[None, :]
    dmask = jnp.where(diff[None] >= 0, jnp.exp(jnp.maximum(diff, 0.0)[None] * log_g[:, None, None]), 0.0)
    xi = jnp.exp((idx + 1.0)[None, :] * log_g[:, None])
    zeta = jnp.exp((CH - 1.0 - idx)[None, :] * log_g[:, None])
    xi = jnp.broadcast_to(xi[:, :, None], (HEADS, CH, HD))
    zeta = jnp.broadcast_to(zeta[:, :, None], (HEADS, CH, HD))
    return cos, sin, dmask, xi, zeta


def _chunk_decay():
    log_g = np.log(np.float32(1.0) - np.float32(2.0) ** (np.float32(-5.0) - np.arange(HEADS, dtype=np.float32)))
    return [float(v) for v in np.exp(np.float32(CH) * log_g.astype(np.float32))]


def _rope(x, cos, sin):
    return x * cos + pltpu.roll(x, HD // 2, 1) * sin


def ret_fwd(proj, ylru, tables, gain, after, name):
    cos, sin, dmask, xi, zeta = tables
    gch = _chunk_decay()
    scale = HD ** -0.5

    def body(q_ref, k_ref, v_ref, g_ref, cos_ref, sin_ref, dm_ref, xi_ref, zt_ref, gain_ref, ylru_ref, after_ref,
             y_ref, st_ref, s_scr):
        del after_ref

        @pl.when(pl.program_id(0) == 0)
        def _():
            s_scr[...] = jnp.zeros_like(s_scr)

        y_ref[:, :D_LRU] = ylru_ref[...]
        cs, sn = cos_ref[...], sin_ref[...]
        hs = range(HEADS)
        sl = [slice(HD * h, HD * (h + 1)) for h in hs]
        qr = [_rope(q_ref[:, sl[h]], cs, sn).astype(MXU_DTYPE) for h in hs]
        kf = [_rope(k_ref[:, sl[h]], cs, sn) * scale for h in hs]
        kr = [kf[h].astype(MXU_DTYPE) for h in hs]
        v = [v_ref[:, sl[h]].astype(MXU_DTYPE) for h in hs]
        s = [s_scr[h] for h in hs]
        for h in hs:
            st_ref[h] = s[h]
        sc = [_dot(qr[h], kr[h], NT) * dm_ref[h] for h in hs]
        cross = [_dot(qr[h], s[h], NN) * xi_ref[h] for h in hs]
        for h in hs:
            s_scr[h] = s[h] * gch[h] + _dot(kf[h] * zt_ref[h], v[h], TN)
        y = [_dot(sc[h], v[h], NN) + cross[h] for h in hs]
        yc = [y[h] - jnp.mean(y[h], axis=-1, keepdims=True) for h in hs]
        yn = [yc[h] * lax.rsqrt(jnp.mean(yc[h] * yc[h], axis=-1, keepdims=True) + EPS) for h in hs]
        for h in hs:
            so = slice(D_LRU + HD * h, D_LRU + HD * (h + 1))
            y_ref[:, so] = (jax.nn.silu(g_ref[:, sl[h]]) * (yn[h] * gain_ref[:, sl[h]])).astype(y_ref.dtype)

    def col(c):
        return pl.BlockSpec((CH, D_RET), lambda n: (n, c))

    tab = pl.BlockSpec((CH, HD), lambda n: (n, 0))
    cst = _full((HEADS, CH, HD))
    return pl.pallas_call(
        body, name=name, grid=(NCH,),
        in_specs=[col(2), col(3), col(4), col(5), tab, tab, cst, cst, cst, _full((1, D_RET)), col(0),
                  pl.BlockSpec(memory_space=pl.ANY)],
        out_specs=[pl.BlockSpec((CH, D), lambda n: (n, 0)), pl.BlockSpec((None, HEADS, HD, HD), lambda n: (n, 0, 0, 0))],
        out_shape=[_sds((T, D), MXU_DTYPE), _sds((NCH, HEADS, HD, HD), F32)],
        scratch_shapes=[pltpu.VMEM((HEADS, HD, HD), F32)],
        compiler_params=_params(("arbitrary",)),
    )(proj, proj, proj, proj, cos, sin, dmask, xi, zeta, gain, ylru, after)


def ret_bwd(proj, states, dymix, dxg, tables, gain, name):
    cos, sin, dmask, xi, zeta = tables
    gch = _chunk_decay()
    scale = HD ** -0.5
    last = NCH - 1

    def body(q_ref, k_ref, v_ref, g_ref, st_ref, do_ref, cos_ref, sin_ref, dm_ref, xi_ref, zt_ref, gain_ref, dxg_ref,
             dp_ref, dgain_ref, ds_scr):
        @pl.when(pl.program_id(0) == 0)
        def _():
            ds_scr[...] = jnp.zeros_like(ds_scr)
            dgain_ref[...] = jnp.zeros_like(dgain_ref)

        dp_ref[:, :2 * D_LRU] = dxg_ref[...]
        cs, sn = cos_ref[...], sin_ref[...]
        hs = range(HEADS)
        sl = [slice(HD * h, HD * (h + 1)) for h in hs]

        def out(j, h):
            return slice(2 * D_LRU + j * D_RET + HD * h, 2 * D_LRU + j * D_RET + HD * (h + 1))

        b16 = lambda xs: [x.astype(MXU_DTYPE) for x in xs]
        qr = b16([_rope(q_ref[:, sl[h]], cs, sn) for h in hs])
        kf = [_rope(k_ref[:, sl[h]], cs, sn) * scale for h in hs]
        kr = b16(kf)
        kz = b16([kf[h] * zt_ref[h] for h in hs])
        v = b16([v_ref[:, sl[h]] for h in hs])
        s = b16([st_ref[h] for h in hs])
        ds = [ds_scr[h] for h in hs]
        dsb = b16(ds)
        sc = [_dot(qr[h], kr[h], NT) * dm_ref[h] for h in hs]
        scb = b16(sc)
        y = [_dot(scb[h], v[h], NN) + _dot(qr[h], s[h], NN) * xi_ref[h] for h in hs]
        yc = [y[h] - jnp.mean(y[h], axis=-1, keepdims=True) for h in hs]
        rstd = [lax.rsqrt(jnp.mean(yc[h] * yc[h], axis=-1, keepdims=True) + EPS) for h in hs]
        yn = [yc[h] * rstd[h] for h in hs]
        dy = []
        for h in hs:
            g = g_ref[:, sl[h]]
            gain = gain_ref[:, sl[h]]
            sg = jax.nn.sigmoid(g)
            silu = g * sg
            dout = do_ref[:, sl[h]].astype(F32)
            dgain_ref[:, sl[h]] += jnp.sum(dout * silu * yn[h], axis=0, keepdims=True)
            dp_ref[:, out(3, h)] = (dout * yn[h] * gain * (sg * (1.0 + g * (1.0 - sg)))).astype(dp_ref.dtype)
            dyn = dout * silu * gain
            dy.append(rstd[h] * (dyn - jnp.mean(dyn, axis=-1, keepdims=True)
                                 - yn[h] * jnp.mean(dyn * yn[h], axis=-1, keepdims=True)))
        dyb = b16(dy)
        dqs = b16([dy[h] * xi_ref[h] for h in hs])
        dp = b16([_dot(dyb[h], v[h], NT) * dm_ref[h] for h in hs])
        dv = [_dot(scb[h], dyb[h], TN) + _dot(kz[h], dsb[h], NN) for h in hs]
        dqr = [_dot(dp[h], kr[h], NN) + _dot(dqs[h], s[h], NT) for h in hs]
        dkr = [_dot(dp[h], qr[h], TN) + _dot(v[h], dsb[h], NT) * zt_ref[h] for h in hs]
        for h in hs:
            ds_scr[h] = gch[h] * ds[h] + _dot(qr[h], dqs[h], TN)
        for h in hs:
            dp_ref[:, out(0, h)] = (dqr[h] * cs + pltpu.roll(dqr[h] * sn, HD // 2, 1)).astype(dp_ref.dtype)
            dp_ref[:, out(1, h)] = ((dkr[h] * cs + pltpu.roll(dkr[h] * sn, HD // 2, 1)) * scale).astype(dp_ref.dtype)
            dp_ref[:, out(2, h)] = dv[h].astype(dp_ref.dtype)

    def col(c):
        return pl.BlockSpec((CH, D_RET), lambda n: (last - n, c))

    tab = pl.BlockSpec((CH, HD), lambda n: (last - n, 0))
    cst = _full((HEADS, CH, HD))
    return pl.pallas_call(
        body, name=name, grid=(NCH,),
        in_specs=[col(2), col(3), col(4), col(5), pl.BlockSpec((None, HEADS, HD, HD), lambda n: (last - n, 0, 0, 0)), col(1),
                  tab, tab, cst, cst, cst, _full((1, D_RET)), pl.BlockSpec((CH, 2 * D_LRU), lambda n: (last - n, 0))],
        out_specs=[pl.BlockSpec((CH, D_IN), lambda n: (last - n, 0)), _full((1, D_RET))],
        out_shape=[_sds((T, D_IN), MXU_DTYPE), _sds((1, D_RET), F32)],
        scratch_shapes=[pltpu.VMEM((HEADS, HD, HD), F32)],
        compiler_params=_params(("arbitrary",)),
    )(proj, proj, proj, proj, states, dymix, cos, sin, dmask, xi, zeta, gain, dxg)


HBM = pl.BlockSpec(memory_space=pltpu.HBM)


def _place():
    return lax.axis_index("x"), lax.axis_index("y"), lax.axis_index("c")


def all_gather(arrs, name):
    n = len(arrs)

    def body(*refs):
        ins, outs = refs[:n], refs[n:2 * n]
        send_sems, recv_sems, local_sems = refs[2 * n:]
        x, y, c = _place()
        me, sibling = (x, y, c), (x, y, 1 - c)
        chips = [(1 - x, y), (x, 1 - y), (1 - x, 1 - y)]

        def copy(a, k, block, to, src=None):
            px, py, pc = block
            dst = outs[a].at[4 * px + 2 * py + pc]
            return pltpu.make_async_remote_copy(
                src_ref=dst if src is None else src, dst_ref=dst, send_sem=send_sems.at[a, k], recv_sem=recv_sems.at[a, k],
                device_id=to, device_id_type=MESH)

        mine = [pltpu.make_async_copy(ins[a], outs[a].at[4 * x + 2 * y + c], local_sems.at[a]) for a in range(n)]
        for cp in mine:
            cp.start()
        first = []
        for a in range(n):
            first.append(copy(a, 0, me, sibling, src=ins[a]))
            first += [copy(a, 1 + j, me, (*chip, c), src=ins[a]) for j, chip in enumerate(chips)]
        for cp in first:
            cp.start()
        passed = []
        for j, chip in enumerate(chips):
            for a in range(n):
                copy(a, 1 + j, (*chip, c), me).wait_recv()
                passed.append(copy(a, 4 + j, (*chip, c), sibling))
                passed[-1].start()
        for a in range(n):
            copy(a, 0, sibling, me).wait_recv()
            for j, chip in enumerate(chips):
                copy(a, 4 + j, (*chip, 1 - c), me).wait_recv()
        for cp in first + passed:
            cp.wait_send()
        for cp in mine:
            cp.wait()

    return pl.pallas_call(
        body, name=name,
        in_specs=[HBM] * n, out_specs=[HBM] * n,
        out_shape=[_sds((NDEV,) + a.shape, a.dtype) for a in arrs],
        scratch_shapes=[pltpu.SemaphoreType.DMA((n, 7)), pltpu.SemaphoreType.DMA((n, 7)), pltpu.SemaphoreType.DMA((n,))],
    )(*arrs)


SEM = pl.BlockSpec(memory_space=pltpu.SEMAPHORE)
ANY = pl.BlockSpec(memory_space=pl.ANY)
EFFECT = pltpu.SideEffectType.DATAFLOW_SIDE_EFFECTING


def _hbm(a):
    return pltpu.with_memory_space_constraint(a, pltpu.HBM)


def _hbm_like(arrs):
    return [pltpu.HBM(a.shape, a.dtype) for a in arrs]


def _dma_sems(count):
    return [pltpu.SemaphoreType.DMA(())] * count


def _ag_copy(lands, send_sems, recv_sems, per):
    def copy(a, k, block, to, src=None):
        px, py, pc = block
        dst = lands[a].at[4 * px + 2 * py + pc]
        return pltpu.make_async_remote_copy(
            src_ref=dst if src is None else src, dst_ref=dst, send_sem=send_sems[a * per + k], recv_sem=recv_sems[a * per + k],
            device_id=to, device_id_type=MESH)
    return copy


def to_wire(sel, w_in, w_gate, w_up, w_out, w_down, name):
    ffpad = FF_SHP - FF_SH

    def body(sel_ref, i_ref, g_ref, u_ref, o_ref, d_ref, oi, og, ou, oo, od):
        del sel_ref
        oi[...] = i_ref[...].astype(oi.dtype)
        oo[...] = o_ref[...].astype(oo.dtype)
        for src, dst in ((g_ref, og), (u_ref, ou), (d_ref, od)):
            dst[:FF_SH, :] = src[...].astype(dst.dtype)
            dst[FF_SH:, :] = jnp.zeros((ffpad, D), dst.dtype)

    shapes_in = [(D, IN_SH), (FF_SH, D), (FF_SH, D), (OUT_SH, D), (FF_SH, D)]
    shapes_out = [(D, IN_SH), (FF_SHP, D), (FF_SHP, D), (OUT_SH, D), (FF_SHP, D)]
    return pl.pallas_call(
        body, name=name,
        grid_spec=pltpu.PrefetchScalarGridSpec(
            num_scalar_prefetch=1, grid=(1,),
            in_specs=[pl.BlockSpec((None,) + s, lambda i, sel_ref: (sel_ref[1], 0, 0)) for s in shapes_in],
            out_specs=[pl.BlockSpec((None,) + s, lambda i, sel_ref: (sel_ref[0], 0, 0)) for s in shapes_out]),
        out_shape=[_sds((NDEV,) + s, WIRE_DTYPE) for s in shapes_out], compiler_params=_params(("arbitrary",)),
    )(sel, w_in, w_gate, w_up, w_out, w_down)


def place_blocks(sel, arrs, name):
    n = len(arrs)

    def body(sel_ref, *refs):
        del sel_ref
        for a in range(n):
            refs[n + a][...] = refs[a][...]

    def whole(a):
        nd = a.ndim
        return pl.BlockSpec(a.shape, lambda i, sel_ref: (0,) * nd)

    def mine(a):
        nd = a.ndim
        return pl.BlockSpec((None,) + a.shape, lambda i, sel_ref: (sel_ref[0],) + (0,) * nd)

    return pl.pallas_call(
        body, name=name,
        grid_spec=pltpu.PrefetchScalarGridSpec(
            num_scalar_prefetch=1, grid=(1,), in_specs=[whole(a) for a in arrs], out_specs=[mine(a) for a in arrs]),
        out_shape=[_sds((NDEV,) + a.shape, a.dtype) for a in arrs], compiler_params=_params(("arbitrary",)),
    )(sel, *arrs)


def ag_start(lands, after, name):
    n = len(lands)
    ns = 4 * n

    def body(*refs):
        lnd = refs[:n]
        send_sems, recv_sems = refs[n + 1:n + 1 + ns], refs[n + 1 + ns:n + 1 + 2 * ns]
        token = refs[-1]
        x, y, c = _place()
        me, sibling = (x, y, c), (x, y, 1 - c)
        chips = [(1 - x, y), (x, 1 - y), (1 - x, 1 - y)]
        copy = _ag_copy(lnd, send_sems, recv_sems, 4)
        for a in range(n):
            copy(a, 0, me, sibling).start()
            for j, chip in enumerate(chips):
                copy(a, 1 + j, me, (*chip, c)).start()
        token[...] = jnp.zeros_like(token)

    outs = pl.pallas_call(
        body, name=name,
        in_specs=[HBM] * n + [ANY],
        out_specs=[SEM] * (2 * ns) + [HBM] * n + [pl.BlockSpec(memory_space=pltpu.VMEM)],
        out_shape=_dma_sems(2 * ns) + _hbm_like(lands) + [_sds((8, 128), F32)],
        input_output_aliases={i: 2 * ns + i for i in range(n)},
        compiler_params=pltpu.CompilerParams(has_side_effects=EFFECT),
    )(*[_hbm(a) for a in lands], after)
    return outs[:ns], outs[ns:2 * ns], outs[2 * ns:2 * ns + n], outs[-1]


def ag_forward(send_sems, recv_sems, lands, after, name):
    n = len(lands)
    n1, n2 = 4 * n, 3 * n

    def body(*refs):
        lnd = refs[:n]
        o = n
        s1, r1 = refs[o:o + n1], refs[o + n1:o + 2 * n1]
        o += 2 * n1 + 1
        s2, r2 = refs[o:o + n2], refs[o + n2:o + 2 * n2]
        token = refs[-1]
        token[...] = jnp.zeros_like(token)
        x, y, c = _place()
        me, sibling = (x, y, c), (x, y, 1 - c)
        chips = [(1 - x, y), (x, 1 - y), (1 - x, 1 - y)]
        copy1 = _ag_copy(lnd, s1, r1, 4)
        copy2 = _ag_copy(lnd, s2, r2, 3)
        for j, chip in enumerate(chips):
            for a in range(n):
                copy1(a, 1 + j, (*chip, c), me).wait_recv()
                copy2(a, j, (*chip, c), sibling).start()
        for a in range(n):
            copy1(a, 0, sibling, me).wait_recv()
            copy1(a, 0, me, sibling).wait_send()
            for j, chip in enumerate(chips):
                copy1(a, 1 + j, me, (*chip, c)).wait_send()

    outs = pl.pallas_call(
        body, name=name,
        in_specs=[HBM] * n + [SEM] * (2 * n1) + [ANY],
        out_specs=[SEM] * (2 * n2) + [HBM] * n + [pl.BlockSpec(memory_space=pltpu.VMEM)],
        out_shape=_dma_sems(2 * n2) + _hbm_like(lands) + [_sds((8, 128), F32)],
        input_output_aliases={i: 2 * n2 + i for i in range(n)},
        compiler_params=pltpu.CompilerParams(has_side_effects=EFFECT),
    )(*lands, *send_sems, *recv_sems, after)
    return outs[:n2], outs[n2:2 * n2], outs[2 * n2:2 * n2 + n], outs[-1]


def ag_finish(send_sems, recv_sems, lands, after, name):
    n = len(lands)
    n2 = 3 * n

    def body(*refs):
        lnd = refs[:n]
        s2, r2 = refs[n:n + n2], refs[n + n2:n + 2 * n2]
        x, y, c = _place()
        me, sibling = (x, y, c), (x, y, 1 - c)
        chips = [(1 - x, y), (x, 1 - y), (1 - x, 1 - y)]
        copy2 = _ag_copy(lnd, s2, r2, 3)
        for a in range(n):
            for j, chip in enumerate(chips):
                copy2(a, j, (*chip, c), sibling).wait_send()
                copy2(a, j, (*chip, 1 - c), me).wait_recv()

    outs = pl.pallas_call(
        body, name=name,
        in_specs=[HBM] * n + [SEM] * (2 * n2) + [ANY],
        out_specs=[HBM] * n, out_shape=_hbm_like(lands),
        input_output_aliases={i: i for i in range(n)},
        compiler_params=pltpu.CompilerParams(has_side_effects=EFFECT),
    )(*lands, *send_sems, *recv_sems, after)
    return list(outs)


def rs_sibling_start(arrs, name):
    n = len(arrs)
    ns = 4 * n
    lands = [lax.empty((4,) + a.shape[1:], a.dtype) for a in arrs]

    def body(*refs):
        ins, lnd = refs[:n], refs[n:2 * n]
        send_sems, recv_sems = refs[2 * n:2 * n + ns], refs[2 * n + ns:2 * n + 2 * ns]
        x, y, c = _place()
        sibling = (x, y, 1 - c)
        for a in range(n):
            for p in range(4):
                pltpu.make_async_remote_copy(
                    src_ref=ins[a].at[2 * p + 1 - c], dst_ref=lnd[a].at[p], send_sem=send_sems[4 * a + p],
                    recv_sem=recv_sems[4 * a + p], device_id=sibling, device_id_type=MESH).start()
        refs[-1][...] = jnp.zeros_like(refs[-1])

    outs = pl.pallas_call(
        body, name=name,
        in_specs=[HBM] * (2 * n), out_specs=[SEM] * (2 * ns) + [HBM] * (2 * n) + [pl.BlockSpec(memory_space=pltpu.VMEM)],
        out_shape=_dma_sems(2 * ns) + _hbm_like(arrs) + _hbm_like(lands) + [_sds((8, 128), F32)],
        input_output_aliases={i: 2 * ns + i for i in range(2 * n)},
        compiler_params=pltpu.CompilerParams(has_side_effects=EFFECT),
    )(*[_hbm(a) for a in arrs], *[_hbm(a) for a in lands])
    return (outs[:ns], outs[ns:2 * ns], outs[2 * ns:2 * ns + n], outs[2 * ns + n:2 * ns + 2 * n]), outs[-1]


def rs_sibling_wait(send_sems, recv_sems, arrs, lands, after, name):
    n = len(arrs)
    ns = 4 * n

    def body(*refs):
        ins, lnd = refs[:n], refs[n:2 * n]
        s, r = refs[2 * n:2 * n + ns], refs[2 * n + ns:2 * n + 2 * ns]
        x, y, c = _place()
        sibling = (x, y, 1 - c)
        for a in range(n):
            for p in range(4):
                cp = pltpu.make_async_remote_copy(
                    src_ref=ins[a].at[2 * p + 1 - c], dst_ref=lnd[a].at[p], send_sem=s[4 * a + p], recv_sem=r[4 * a + p],
                    device_id=sibling, device_id_type=MESH)
                cp.wait_send()
                cp.wait_recv()

    outs = pl.pallas_call(
        body, name=name,
        in_specs=[HBM] * (2 * n) + [SEM] * (2 * ns) + [ANY], out_specs=[HBM] * (2 * n),
        out_shape=_hbm_like(arrs) + _hbm_like(lands),
        input_output_aliases={i: i for i in range(2 * n)},
        compiler_params=pltpu.CompilerParams(has_side_effects=EFFECT),
    )(*arrs, *lands, *send_sems, *recv_sems, after)
    return outs[:n], outs[n:]


def rs_chips_start(parts, name):
    n = len(parts)
    ns = 3 * n
    lands = [lax.empty((3,) + a.shape[1:], a.dtype) for a in parts]

    def body(*refs):
        ins, lnd = refs[:n], refs[n:2 * n]
        send_sems, recv_sems = refs[2 * n:2 * n + ns], refs[2 * n + ns:2 * n + 2 * ns]
        x, y, c = _place()
        chips = [(1 - x, y), (x, 1 - y), (1 - x, 1 - y)]
        for a in range(n):
            for k, (tx, ty) in enumerate(chips):
                pltpu.make_async_remote_copy(
                    src_ref=ins[a].at[2 * tx + ty], dst_ref=lnd[a].at[k], send_sem=send_sems[3 * a + k],
                    recv_sem=recv_sems[3 * a + k], device_id=(tx, ty, c), device_id_type=MESH).start()
        refs[-1][...] = jnp.zeros_like(refs[-1])

    outs = pl.pallas_call(
        body, name=name,
        in_specs=[HBM] * (2 * n), out_specs=[SEM] * (2 * ns) + [HBM] * (2 * n) + [pl.BlockSpec(memory_space=pltpu.VMEM)],
        out_shape=_dma_sems(2 * ns) + _hbm_like(parts) + _hbm_like(lands) + [_sds((8, 128), F32)],
        input_output_aliases={i: 2 * ns + i for i in range(2 * n)},
        compiler_params=pltpu.CompilerParams(has_side_effects=EFFECT),
    )(*[_hbm(a) for a in parts], *[_hbm(a) for a in lands])
    return (outs[:ns], outs[ns:2 * ns], outs[2 * ns:2 * ns + n], outs[2 * ns + n:2 * ns + 2 * n]), outs[-1]


def rs_chips_wait(send_sems, recv_sems, parts, lands, after, name):
    n = len(parts)
    ns = 3 * n

    def body(*refs):
        ins, lnd = refs[:n], refs[n:2 * n]
        s, r = refs[2 * n:2 * n + ns], refs[2 * n + ns:2 * n + 2 * ns]
        x, y, c = _place()
        chips = [(1 - x, y), (x, 1 - y), (1 - x, 1 - y)]
        for a in range(n):
            for k, (tx, ty) in enumerate(chips):
                cp = pltpu.make_async_remote_copy(
                    src_ref=ins[a].at[2 * tx + ty], dst_ref=lnd[a].at[k], send_sem=s[3 * a + k], recv_sem=r[3 * a + k],
                    device_id=(tx, ty, c), device_id_type=MESH)
                cp.wait_send()
                cp.wait_recv()

    outs = pl.pallas_call(
        body, name=name,
        in_specs=[HBM] * (2 * n) + [SEM] * (2 * ns) + [ANY], out_specs=[HBM] * (2 * n),
        out_shape=_hbm_like(parts) + _hbm_like(lands),
        input_output_aliases={i: i for i in range(2 * n)},
        compiler_params=pltpu.CompilerParams(has_side_effects=EFFECT),
    )(*parts, *lands, *send_sems, *recv_sems, after)
    return outs[:n], outs[n:]


def pair_sum(arrs, recv, c, name):
    n = len(arrs)

    def body(c_ref, *refs):
        del c_ref
        for a in range(n):
            refs[2 * n + a][...] = (refs[a][...].astype(F32) + refs[n + a][...].astype(F32)).astype(refs[2 * n + a].dtype)

    mine = [pl.BlockSpec((None,) + a.shape[1:], lambda p, c_ref: (2 * p + c_ref[0], 0, 0)) for a in arrs]
    other = [pl.BlockSpec((None,) + a.shape[1:], lambda p, c_ref: (p, 0, 0)) for a in arrs]
    return pl.pallas_call(
        body, name=name,
        grid_spec=pltpu.PrefetchScalarGridSpec(num_scalar_prefetch=1, grid=(4,), in_specs=mine + other, out_specs=other),
        out_shape=[_sds((4,) + a.shape[1:], a.dtype) for a in arrs], compiler_params=_params(("parallel",)),
    )(c, *arrs, *recv)


def _adamw(w, g, m, v):
    m = ADAM_B1 * m + (1.0 - ADAM_B1) * g
    v = ADAM_B2 * v + (1.0 - ADAM_B2) * jnp.square(g)
    m_hat = m / (1.0 - ADAM_B1 ** ADAM_STEP)
    v_hat = v / (1.0 - ADAM_B2 ** ADAM_STEP)
    return -ADAM_LR * (m_hat / (jnp.sqrt(v_hat) + ADAM_EPS) + ADAM_WD * w), m, v


def adamw_big(recv, sums, chip, w, m, v, tr, name):
    nl, rr, cc = w.shape
    cp = recv[0].shape[2]

    def body(chip_ref, *refs):
        del chip_ref
        rcv, own = refs[:nl], refs[nl:2 * nl]
        w_ref, m_ref, v_ref, g_out, d_out, m_out, v_out = refs[2 * nl:]
        for l in range(nl):
            g = ((own[l][...].astype(F32) + rcv[l][0].astype(F32)) + rcv[l][1].astype(F32)) + rcv[l][2].astype(F32)
            g = g[:, :cc]
            g_out[l] = g
            d_out[l], m_out[l], v_out[l] = _adamw(w_ref[l], g, m_ref[l], v_ref[l])

    blk = pl.BlockSpec((nl, tr, cc), lambda i, chip_ref: (0, i, 0))
    return pl.pallas_call(
        body, name=name,
        grid_spec=pltpu.PrefetchScalarGridSpec(
            num_scalar_prefetch=1, grid=(rr // tr,),
            in_specs=[pl.BlockSpec((3, tr, cp), lambda i, chip_ref: (0, i, 0))] * nl
            + [pl.BlockSpec((None, tr, cp), lambda i, chip_ref: (chip_ref[0], i, 0))] * nl + [blk, blk, blk],
            out_specs=[blk] * 4),
        out_shape=[_sds(w.shape, F32)] * 4, compiler_params=_params(("parallel",)),
    )(chip, *recv, *sums, w, m, v)


SMALL_ROWS = 16


def small_grads(lvec, g_ret, g_mix, g_ffn, g_final, dwa, dwx, name):
    def body(lvec_ref, ret_ref, mix_ref, ffn_ref, fin_ref, dwa_ref, dwx_ref, v_ref, g_ref):
        v_ref[0:9, :] = lvec_ref[0:9, :]
        v_ref[9:10, :] = ret_ref[...]
        for r, src in ((10, mix_ref), (12, ffn_ref), (14, fin_ref)):
            v_ref[r:r + 1, :] = src[:, :D_LRU]
            v_ref[r + 1:r + 2, :] = src[:, D_LRU:]
        for k, src in enumerate((dwa_ref, dwx_ref)):
            for g in range(LRU_BLOCKS):
                rows = slice(LRU_BD * g, LRU_BD * (g + 1))
                g_ref[D_LRU * k + LRU_BD * g:D_LRU * k + LRU_BD * (g + 1), :] = src[rows, rows]

    ins = [lvec, g_ret, g_mix, g_ffn, g_final, dwa, dwx]
    return pl.pallas_call(
        body, name=name, grid=(1,), in_specs=[_full(a.shape) for a in ins],
        out_specs=[_full((SMALL_ROWS, D_LRU)), _full((2 * D_LRU, LRU_BD))],
        out_shape=[_sds((SMALL_ROWS, D_LRU), F32), _sds((2 * D_LRU, LRU_BD), F32)], compiler_params=_params(("arbitrary",)),
    )(*ins)


def sum_devices(arrs, name):
    n = len(arrs)

    def body(*refs):
        for a in range(n):
            acc = refs[a][0]
            for j in range(1, NDEV):
                acc = acc + refs[a][j]
            refs[n + a][...] = acc

    return pl.pallas_call(
        body, name=name, grid=(1,), in_specs=[_full(a.shape) for a in arrs], out_specs=[_full(a.shape[1:]) for a in arrs],
        out_shape=[_sds(a.shape[1:], F32) for a in arrs], compiler_params=_params(("arbitrary",)),
    )(*arrs)


def adamw_small(gs, ws, ms, vs, name):
    n = len(gs)

    def body(*refs):
        for a in range(n):
            g, w, m, v = (refs[k * n + a][...] for k in range(4))
            refs[4 * n + a][...], refs[5 * n + a][...], refs[6 * n + a][...] = _adamw(w, g, m, v)

    specs = [_full(a.shape) for a in ws]
    outs = pl.pallas_call(
        body, name=name, grid=(1,), in_specs=specs * 4, out_specs=specs * 3, out_shape=[_sds(a.shape, F32) for a in ws] * 3,
        compiler_params=_params(("arbitrary",)),
    )(*gs, *ws, *ms, *vs)
    return outs[:n], outs[n:2 * n], outs[2 * n:]


def block_diag(wa, wx, name):
    def body(wa_ref, wx_ref, oa_ref, ox_ref):
        for src, dst in ((wa_ref, oa_ref), (wx_ref, ox_ref)):
            dst[...] = jnp.zeros_like(dst)
            for g in range(LRU_BLOCKS):
                rows = slice(LRU_BD * g, LRU_BD * (g + 1))
                dst[rows, rows] = src[g].astype(dst.dtype)

    ispec = pl.BlockSpec((None, LRU_BLOCKS, LRU_BD, LRU_BD), lambda l: (l, 0, 0, 0))
    ospec = pl.BlockSpec((None, D_LRU, D_LRU), lambda l: (l, 0, 0))
    return pl.pallas_call(
        body, name=name, grid=(wa.shape[0],), in_specs=[ispec, ispec], out_specs=[ospec, ospec],
        out_shape=[_sds((wa.shape[0], D_LRU, D_LRU), MXU_DTYPE)] * 2, compiler_params=_params(("parallel",)),
    )(wa, wx)


REP_NAMES = ["norm_mix", "conv_b", "gate_a_w", "gate_a_b", "gate_x_w", "gate_x_b", "lru_lambda", "lru_out_norm",
             "ret_out_norm", "norm_ffn", "norm_final"]


def kernel(x, meta_tokens, norm_mix, w_in, conv_w, conv_b, gate_a_w, gate_a_b, gate_x_w, gate_x_b, lru_lambda, lru_out_norm, ret_out_norm, w_out, norm_ffn, w_gate, w_up, w_down, norm_final, loss_target, m_meta_tokens, m_norm_mix, m_w_in, m_conv_w, m_conv_b, m_gate_a_w, m_gate_a_b, m_gate_x_w, m_gate_x_b, m_lru_lambda, m_lru_out_norm, m_ret_out_norm, m_w_out, m_norm_ffn, m_w_gate, m_w_up, m_w_down, m_norm_final, v_meta_tokens, v_norm_mix, v_w_in, v_conv_w, v_conv_b, v_gate_a_w, v_gate_a_b, v_gate_x_w, v_gate_x_b, v_lru_lambda, v_lru_out_norm, v_ret_out_norm, v_w_out, v_norm_ffn, v_w_gate, v_w_up, v_w_down, v_norm_final):
    xi, yi, ci = _place()
    dev = 4 * xi + 2 * yi + ci
    c_arr = jnp.reshape(ci, (1,)).astype(jnp.int32)
    dev_arr = jnp.reshape(dev, (1,)).astype(jnp.int32)

    meta_g, conv_g = all_gather([meta_tokens, conv_w], "ag_small")
    meta_full = jnp.transpose(meta_g, (1, 0, 2)).reshape(N_META, D)
    conv_full = jnp.transpose(conv_g, (1, 2, 0, 3)).reshape(DEPTH, CONV_W, D_LRU)
    tr_ = lambda a: jnp.transpose(a, (0, 2, 1))
    w_gate_t, m_w_gate_t, v_w_gate_t = tr_(w_gate), tr_(m_w_gate), tr_(v_w_gate)
    w_up_t, m_w_up_t, v_w_up_t = tr_(w_up), tr_(m_w_up), tr_(v_w_up)
    level1 = []
    token = meta_g
    for l in range(DEPTH):
        sel = jnp.stack([dev, jnp.int32(l)]).astype(jnp.int32)
        lands = to_wire(sel, w_in, w_gate_t, w_up_t, w_out, w_down, "to_wire")
        s1, r1, lands, token = ag_start(lands, token, f"ag_start_{l}")
        level1.append((s1, r1, lands))

    def as_weights(gi, gg, gu, go, gd):
        return dict(w_in=gi, w_gate=gg.reshape(D_FFP, D), w_up=gu.reshape(D_FFP, D), w_out=go.reshape(D, D),
                    w_down=gd.reshape(D_FFP, D))

    tables = _ret_tables()
    row = lambda a: a.reshape(1, -1)

    h = jnp.concatenate([jnp.zeros((PAD, D), F32), meta_full, x[0]], axis=0)
    saved, gathered = [], []
    s1, r1, lands = level1[0]
    s2, r2, first, order = ag_forward(s1[:4], r1[:4], lands[:1], token, "ag_forward_0_w_in")
    w_in_next = ag_finish(s2, r2, first, h, "ag_finish_0_w_in")[0]
    wa_dense, wx_dense = block_diag(gate_a_w, gate_x_w, "block_diag")
    for l in range(DEPTH):
        small = dict(cw=conv_full[l], cb=row(conv_b[l]), wa=wa_dense[l], ba=row(gate_a_b[l]),
                     wx=wx_dense[l], bx=row(gate_x_b[l]), lam=row(lru_lambda[l]),
                     gain=row(lru_out_norm[l]))
        s1, r1, lands = level1[l]
        hn1 = rmsnorm_fwd(h, row(norm_mix[l]), "rms_fwd")
        proj = mm_blocked_nn(hn1, w_in_next, F32, "proj")
        ylru, hst = lru_fwd(proj, name="lru_fwd", **small)
        s2, r2, rest, order = ag_forward(s1[4:], r1[4:], lands[1:], ylru, f"ag_forward_{l}_rest")
        ymix, states = ret_fwd(proj, ylru, tables, row(ret_out_norm[l]), order, "ret_fwd")
        w = as_weights(w_in_next, *ag_finish(s2, r2, rest, ymix, f"ag_finish_{l}_rest"))
        gathered.append(w)
        h_mid = mm_nn_res(ymix, w["w_out"], h, order, "out_proj")
        hn2 = rmsnorm_fwd(h_mid, row(norm_ffn[l]), "rms_fwd")
        act_dgate, act_dup, act = ffn_up(hn2, w["w_gate"], w["w_up"], "ffn_up")
        if l + 1 < DEPTH:
            s1n, r1n, landsn = level1[l + 1]
            s2, r2, first, order = ag_forward(s1n[:4], r1n[:4], landsn[:1], act, f"ag_forward_{l + 1}_w_in")
        h_out = mm_nn_res(act, w["w_down"], h_mid, order, "ffn_down")
        if l + 1 < DEPTH:
            w_in_next = ag_finish(s2, r2, first, h_out, f"ag_finish_{l + 1}_w_in")[0]
        saved.append(dict(h=h, hn1=hn1, proj=proj, hst=hst, states=states, ymix=ymix, h_mid=h_mid, hn2=hn2, act_dgate=act_dgate, act_dup=act_dup,
                          act=act, small=small))
        h = h_out

    loss_p, dh, dh_b, g_norm_final = loss_head(h, row(norm_final), loss_target[0], "loss_head")
    loss = lax.psum(loss_p[0, 0], ("x", "y", "c"))

    small_v = [None] * DEPTH
    small_w = [None] * DEPTH
    inflight = []
    sib = None
    order = loss_p

    def sibling_done(l, tag, names, sib, after):
        parts, got = rs_sibling_wait(*sib, after, f"rs_sibling_wait_{tag}")
        sums = pair_sum(parts, got, c_arr, "pair_sum")
        flying, started = rs_chips_start(sums, f"rs_chips_start_{tag}")
        inflight.append((l, tag, names, flying))
        return started

    for l in reversed(range(DEPTH)):
        w, s = gathered[l], saved[l]
        dgate, dup = ffn_down_bwd(dh_b, w["w_down"], s["act_dgate"], s["act_dup"], order, "ffn_down_bwd")
        dwd = mm_tn(s["act"], dh_b, PAIR, order, "dw_down").reshape(NDEV, FF_SHP, D)
        dwg, dwu = (g.reshape(NDEV, FF_SHP, D) for g in mm_tn_two(dgate, dup, s["hn2"], PAIR, order, "dw_rows"))
        ffn_sib, order = rs_sibling_start([dwg, dwu, dwd], f"rs_sibling_start_{l}_ffn")
        dhn2 = mm_rows_nn([(dgate, w["w_gate"]), (dup, w["w_up"])], order, "ffn_up_bwd")
        if sib is not None:
            order = sibling_done(l + 1, f"{l + 1}_mix", ("w_in", "w_out"), sib, dhn2)
        dh_mid, dh_mid_b, g_norm_ffn = rmsnorm_bwd(s["h_mid"], row(norm_ffn[l]), dhn2, dh, "rms_bwd")
        dymix, dwo = out_proj_bwd(dh_mid_b, w["w_out"], s["ymix"], order, "out_proj_bwd")
        dwo = dwo.reshape(NDEV, OUT_SH, D)
        order = sibling_done(l, f"{l}_ffn", ("w_gate", "w_up", "w_down"), ffn_sib, dymix)
        dxg, lvec, dwa, dwx = lru_bwd(s["proj"], s["hst"], dymix, after=order, name="lru_bwd", **s["small"])
        dproj, g_ret_norm = ret_bwd(s["proj"], s["states"], dymix, dxg, tables, row(ret_out_norm[l]), "ret_bwd")
        dwi = mm_tn_blocked(s["hn1"], dproj, "dw_blocked")
        dhn1 = mm_blocked_nt([(dproj, w["w_in"])], order, "proj_bwd")
        dh, dh_b, g_norm_mix = rmsnorm_bwd(s["h"], row(norm_mix[l]), dhn1, dh_mid, "rms_bwd")

        g_fin = g_norm_final if l == 0 else jnp.zeros((1, D), F32)
        small_v[l], small_w[l] = small_grads(lvec, g_ret_norm, g_norm_mix, g_norm_ffn, g_fin, dwa, dwx, "small_grads")
        sib, order = rs_sibling_start([dwi, dwo], f"rs_sibling_start_{l}_mix")
        if l == 1:
            early = place_blocks(dev_arr, [jnp.stack(small_v[1:]), jnp.stack(small_w[1:])], "place_grads")
            early_sems = ag_start(early, order, "ag_start_grads")
            order = early_sems[3]

    grad_x = dh[X0:][None]
    g_meta = dh[PAD:X0]

    late = all_gather([small_v[0], small_w[0], g_meta], "ag_grads")
    s2, r2, lands, _ = ag_forward(early_sems[0], early_sems[1], early_sems[2], dh, "ag_forward_grads")
    gath_early = ag_finish(s2, r2, lands, late[0], "ag_finish_grads")
    sibling_done(0, "0_mix", ("w_in", "w_out"), sib, late[0])
    v0, w0, meta_sum, v123, w123 = sum_devices(list(late) + list(gath_early), "sum_devices")
    vecs = jnp.concatenate([v0[None], v123])
    gws = jnp.concatenate([w0[None], w123])
    blocks = (DEPTH, LRU_BLOCKS, LRU_BD)
    small_g = dict(
        conv_w=lax.dynamic_slice_in_dim(vecs[:, 0:CONV_W], dev * (D_LRU // NDEV), D_LRU // NDEV, axis=2),
        conv_b=vecs[:, 4], gate_a_b=vecs[:, 5].reshape(blocks), gate_x_b=vecs[:, 6].reshape(blocks),
        lru_lambda=vecs[:, 7], lru_out_norm=vecs[:, 8], ret_out_norm=vecs[:, 9],
        norm_mix=vecs[:, 10:12].reshape(DEPTH, D), norm_ffn=vecs[:, 12:14].reshape(DEPTH, D),
        norm_final=v0[14:16].reshape(1, D),
        gate_a_w=gws[:, :D_LRU].reshape(blocks + (LRU_BD,)), gate_x_w=gws[:, D_LRU:].reshape(blocks + (LRU_BD,)),
        meta_tokens=lax.dynamic_slice_in_dim(meta_sum, dev * (D // NDEV), D // NDEV, axis=1))
    given = dict(norm_mix=(norm_mix, m_norm_mix, v_norm_mix), conv_b=(conv_b, m_conv_b, v_conv_b),
                 gate_a_w=(gate_a_w, m_gate_a_w, v_gate_a_w), gate_a_b=(gate_a_b, m_gate_a_b, v_gate_a_b),
                 gate_x_w=(gate_x_w, m_gate_x_w, v_gate_x_w), gate_x_b=(gate_x_b, m_gate_x_b, v_gate_x_b),
                 lru_lambda=(lru_lambda, m_lru_lambda, v_lru_lambda), lru_out_norm=(lru_out_norm, m_lru_out_norm, v_lru_out_norm),
                 ret_out_norm=(ret_out_norm, m_ret_out_norm, v_ret_out_norm), norm_ffn=(norm_ffn, m_norm_ffn, v_norm_ffn),
                 norm_final=tuple(a.reshape(1, D) for a in (norm_final, m_norm_final, v_norm_final)),
                 conv_w=(conv_w, m_conv_w, v_conv_w), meta_tokens=(meta_tokens, m_meta_tokens, v_meta_tokens))
    small_names = REP_NAMES + ["conv_w", "meta_tokens"]
    upd = adamw_small([small_g[n] for n in small_names], *[[given[n][k] for n in small_names] for k in range(3)],
                      "adamw_small")
    small_out = [dict(zip(small_names, u)) for u in upd]
    for d_ in [small_g] + small_out:
        d_["norm_final"] = d_["norm_final"].reshape(D)

    arrived = {}

    def wait_for(entries, after):
        for l, tag, names, flying in entries:
            sums, recv = rs_chips_wait(*flying, after, f"rs_chips_wait_{tag}")
            for i, n in enumerate(names):
                arrived[l, n] = (recv[i], sums[i])

    chip = jnp.reshape(2 * xi + yi, (1,)).astype(jnp.int32)

    def finish(wname, w_, m_, v_, tr):
        return adamw_big([arrived[l, wname][0] for l in range(DEPTH)], [arrived[l, wname][1] for l in range(DEPTH)], chip,
                         w_, m_, v_, tr, "adamw_" + wname)

    wait_for(inflight[:-1], upd[0][0])
    o_gate = [tr_(o) for o in finish("w_gate", w_gate_t, m_w_gate_t, v_w_gate_t, 32)]
    o_up = [tr_(o) for o in finish("w_up", w_up_t, m_w_up_t, v_w_up_t, 32)]
    o_down = finish("w_down", w_down, m_w_down, v_w_down, 32)
    wait_for(inflight[-1:], o_down[0])
    o_in = finish("w_in", w_in, m_w_in, v_w_in, 256)
    o_out = finish("w_out", w_out, m_w_out, v_w_out, 64)

    bigs = dict(w_in=o_in, w_out=o_out, w_gate=o_gate, w_up=o_up, w_down=o_down)
    order = ["meta_tokens", "norm_mix", "w_in", "conv_w", "conv_b", "gate_a_w", "gate_a_b", "gate_x_w", "gate_x_b", "lru_lambda",
             "lru_out_norm", "ret_out_norm", "w_out", "norm_ffn", "w_gate", "w_up", "w_down", "norm_final"]
    grads = [bigs[n][0] if n in bigs else small_g[n] for n in order]
    rest = [[bigs[n][k + 1] if n in bigs else small_out[k][n] for n in order] for k in range(3)]
    return (loss, grad_x, *grads, *rest[0], *rest[1], *rest[2])
```

```python
import functools

import numpy as np
import jax
import jax.numpy as jnp
from jax import lax
from jax.experimental import pallas as pl
from jax.experimental.pallas import tpu as pltpu

F32, BF16 = jnp.float32, jnp.bfloat16
MXU_DTYPE = BF16
WIRE_DTYPE = BF16

D = 1024
SEQ = 2048
DEPTH = 4
N_META = 16
CH = 128
PAD = (-(SEQ + N_META)) % CH
T = SEQ + N_META + PAD
NCH = T // CH
X0 = PAD + N_META
D_LRU = 512
LRU_BLOCKS = 8
LRU_BD = 64
CONV_W = 4
LRU_C = 8.0
D_RET = 512
HEADS = 4
HD = 128
ROPE_BASE = 10000.0
D_IN = 3072
D_FF = 2816
NDEV = 8
IN_SH = D_IN // NDEV
FF_SH = D_FF // NDEV
FF_SHP = 384
D_FFP = NDEV * FF_SHP
OUT_SH = D // NDEV
EPS = 1e-6
TM = 544
TR = 1088
VMEM_LIMIT = 56 * 2**20
MESH = pl.DeviceIdType.MESH

ADAM_LR, ADAM_B1, ADAM_B2, ADAM_EPS, ADAM_WD, ADAM_STEP = 0.001, 0.9, 0.999, 1e-08, 0.01, 10

NN = ((1,), (0,))
NT = ((1,), (1,))
TN = ((0,), (0,))


def _dot(a, b, dims):
    return lax.dot_general(a.astype(MXU_DTYPE), b.astype(MXU_DTYPE), (dims, ((), ())), preferred_element_type=F32)


def _sds(shape, dtype):
    return jax.ShapeDtypeStruct(shape, dtype)


def _params(sem=None):
    return pltpu.CompilerParams(dimension_semantics=sem, vmem_limit_bytes=VMEM_LIMIT)


def _full(shape):
    n = len(shape)
    return pl.BlockSpec(shape, lambda *_: (0,) * n)


def rmsnorm_fwd(h, gain, name):
    def body(h_ref, g_ref, o_ref):
        x = h_ref[...]
        ms = jnp.mean(x * x, axis=-1, keepdims=True)
        o_ref[...] = (x * lax.rsqrt(ms + EPS) * g_ref[...]).astype(o_ref.dtype)

    return pl.pallas_call(
        body, name=name, grid=(T // TM,),
        in_specs=[pl.BlockSpec((TM, D), lambda i: (i, 0)), _full((1, D))],
        out_specs=pl.BlockSpec((TM, D), lambda i: (i, 0)),
        out_shape=_sds((T, D), MXU_DTYPE), compiler_params=_params(("parallel",)),
    )(h, gain)


def rmsnorm_bwd(h, gain, dhn, dres, name):
    def body(h_ref, g_ref, dhn_ref, dres_ref, dh_ref, dhb_ref, dg_ref):
        x = h_ref[...]
        rstd = lax.rsqrt(jnp.mean(x * x, axis=-1, keepdims=True) + EPS)
        xhat = x * rstd
        dy = dhn_ref[...]
        dyg = dy * g_ref[...]
        dh = dres_ref[...] + rstd * (dyg - xhat * jnp.mean(dyg * xhat, axis=-1, keepdims=True))
        dh_ref[...] = dh
        dhb_ref[...] = dh.astype(dhb_ref.dtype)

        @pl.when(pl.program_id(0) == 0)
        def _():
            dg_ref[...] = jnp.zeros_like(dg_ref)
        dg_ref[...] += jnp.sum(dy * xhat, axis=0, keepdims=True)

    row = pl.BlockSpec((TM, D), lambda i: (i, 0))
    return pl.pallas_call(
        body, name=name, grid=(T // TM,),
        in_specs=[row, _full((1, D)), row, row],
        out_specs=[row, row, _full((1, D))],
        out_shape=[_sds((T, D), F32), _sds((T, D), MXU_DTYPE), _sds((1, D), F32)], compiler_params=_params(("arbitrary",)),
    )(h, gain, dhn, dres)


def loss_head(h, gain, target, name):
    def body(h_ref, g_ref, t_ref, loss_ref, dh_ref, dhb_ref, dg_ref):
        i = pl.program_id(0)

        @pl.when(i == 0)
        def _():
            loss_ref[...] = jnp.zeros_like(loss_ref)
            dg_ref[...] = jnp.zeros_like(dg_ref)
            dh_ref[...] = jnp.zeros_like(dh_ref)
            dhb_ref[...] = jnp.zeros_like(dhb_ref)

        @pl.when(i > 0)
        def _():
            x = h_ref[...]
            g = g_ref[...]
            rstd = lax.rsqrt(jnp.mean(x * x, axis=-1, keepdims=True) + EPS)
            xhat = x * rstd
            err = xhat * g - t_ref[...]
            loss_ref[...] += 0.5 * jnp.sum(jnp.mean(err * err, axis=-1, keepdims=True), axis=0, keepdims=True)
            dy = err * (1.0 / D)
            dyg = dy * g
            dh = rstd * (dyg - xhat * jnp.mean(dyg * xhat, axis=-1, keepdims=True))
            dh_ref[...] = dh
            dhb_ref[...] = dh.astype(dhb_ref.dtype)
            dg_ref[...] += jnp.sum(dy * xhat, axis=0, keepdims=True)

    row = pl.BlockSpec((CH, D), lambda i: (i, 0))
    return pl.pallas_call(
        body, name=name, grid=(NCH,),
        in_specs=[row, _full((1, D)), pl.BlockSpec((CH, D), lambda i: (jnp.maximum(i - 1, 0), 0))],
        out_specs=[_full((8, 128)), row, row, _full((1, D))],
        out_shape=[_sds((8, 128), F32), _sds((T, D), F32), _sds((T, D), MXU_DTYPE), _sds((1, D), F32)],
        compiler_params=_params(("arbitrary",)),
    )(h, gain, target)


PAIR = 2 * IN_SH
NPAIR = NDEV // 2
BN = 256
FB = 512


def _pair_cols(w_ref):
    return jnp.concatenate([w_ref[0], w_ref[1]], axis=1)


W_PAIR = lambda k: pl.BlockSpec((2, k, IN_SH), lambda j: (j, 0, 0))
COLS_PAIR = pl.BlockSpec((T, PAIR), lambda j: (0, j))
ANYSPEC = pl.BlockSpec(memory_space=pl.ANY)


def mm_blocked_nn(a, w, out_dtype, name):
    k = a.shape[1]

    def body(a_ref, w_ref, o_ref):
        o_ref[...] = _dot(a_ref[...], _pair_cols(w_ref), NN).astype(o_ref.dtype)

    return pl.pallas_call(
        body, name=name, grid=(NPAIR,),
        in_specs=[_full((T, k)), W_PAIR(k)], out_specs=COLS_PAIR,
        out_shape=_sds((T, NDEV * IN_SH), out_dtype), compiler_params=_params(("parallel",)),
    )(a, w)


def mm_nn_res(a, w, res, after, name):
    k = a.shape[1]

    def body(a_ref, w_ref, r_ref, after_ref, o_ref):
        del after_ref
        o_ref[...] = r_ref[...] + _dot(a_ref[...], w_ref[...], NN)

    col = pl.BlockSpec((T, BN), lambda j: (0, j))
    return pl.pallas_call(
        body, name=name, grid=(D // BN,),
        in_specs=[_full((T, k)), pl.BlockSpec((k, BN), lambda j: (0, j)), col, ANYSPEC], out_specs=col,
        out_shape=_sds((T, D), F32), compiler_params=_params(("parallel",)),
    )(a, w, res, after)


def ffn_up(hn, wg, wu, name):
    def body(a_ref, wg_ref, wu_ref, dg_ref, du_ref, act_ref):
        a = a_ref[...]
        g = _dot(a, wg_ref[...], NT)
        u = _dot(a, wu_ref[...], NT)
        for r in range(T // TR):
            rows = slice(TR * r, TR * (r + 1))
            gr, ur = g[rows], u[rows]
            sg = jax.nn.sigmoid(gr)
            silu = gr * sg
            dg_ref[rows, :] = (ur * (sg * (1.0 + gr * (1.0 - sg)))).astype(dg_ref.dtype)
            du_ref[rows, :] = silu.astype(du_ref.dtype)
            act_ref[rows, :] = (silu * ur).astype(act_ref.dtype)

    wspec = pl.BlockSpec((FB, D), lambda j: (j, 0))
    ospec = pl.BlockSpec((T, FB), lambda j: (0, j))
    return pl.pallas_call(
        body, name=name, grid=(D_FFP // FB,),
        in_specs=[_full((T, D)), wspec, wspec], out_specs=[ospec] * 3,
        out_shape=[_sds((T, D_FFP), MXU_DTYPE)] * 3, compiler_params=_params(("parallel",)),
    )(hn, wg, wu)


def ffn_down_bwd(dh, wd, dact_dgate, dact_dup, after, name):
    def body(dh_ref, wd_ref, g_ref, u_ref, after_ref, dg_ref, du_ref):
        del after_ref
        dact_all = _dot(dh_ref[...], wd_ref[...], NT)
        for r in range(T // TR):
            rows = slice(TR * r, TR * (r + 1))
            dact = dact_all[rows]
            dg_ref[rows, :] = (dact * g_ref[rows, :].astype(F32)).astype(dg_ref.dtype)
            du_ref[rows, :] = (dact * u_ref[rows, :].astype(F32)).astype(du_ref.dtype)

    blk = pl.BlockSpec((T, FB), lambda j: (0, j))
    return pl.pallas_call(
        body, name=name, grid=(D_FFP // FB,),
        in_specs=[_full((T, D)), pl.BlockSpec((FB, D), lambda j: (j, 0)), blk, blk, ANYSPEC],
        out_specs=[blk, blk],
        out_shape=[_sds((T, D_FFP), MXU_DTYPE)] * 2, compiler_params=_params(("parallel",)),
    )(dh, wd, dact_dgate, dact_dup, after)


def mm_blocked_nt(pairs, after, name):
    n = len(pairs)

    def body(*refs):
        o_ref = refs[2 * n + 1]

        @pl.when(pl.program_id(0) == 0)
        def _():
            o_ref[...] = jnp.zeros_like(o_ref)
        for p in range(n):
            o_ref[...] += _dot(refs[2 * p][...], _pair_cols(refs[2 * p + 1]), NT)

    specs, args = [], []
    for a, w in pairs:
        specs += [COLS_PAIR, W_PAIR(D)]
        args += [a, w]
    return pl.pallas_call(
        body, name=name, grid=(NPAIR,), in_specs=specs + [ANYSPEC], out_specs=_full((T, D)),
        out_shape=_sds((T, D), F32), compiler_params=_params(("arbitrary",)),
    )(*args, after)


def mm_tn_two(a1, a2, b, bm, after, name):
    m = a1.shape[1]

    def body(a1_ref, a2_ref, b_ref, after_ref, o1_ref, o2_ref):
        del after_ref
        b = b_ref[...]
        o1_ref[...] = _dot(a1_ref[...], b, TN).astype(o1_ref.dtype)
        o2_ref[...] = _dot(a2_ref[...], b, TN).astype(o2_ref.dtype)

    blk = pl.BlockSpec((T, bm), lambda i: (0, i))
    out = pl.BlockSpec((bm, D), lambda i: (i, 0))
    return pl.pallas_call(
        body, name=name, grid=(m // bm,),
        in_specs=[blk, blk, _full((T, D)), ANYSPEC], out_specs=[out, out],
        out_shape=[_sds((m, D), WIRE_DTYPE)] * 2, compiler_params=_params(("parallel",)),
    )(a1, a2, b, after)


def out_proj_bwd(dh, w, ymix, after, name):
    def body(dh_ref, w_ref, y_ref, after_ref, dy_ref, dw_ref):
        del after_ref
        dh_ = dh_ref[...]
        dy_ref[...] = _dot(dh_, w_ref[...], NT)
        dw_ref[...] = _dot(y_ref[...], dh_, TN).astype(dw_ref.dtype)

    return pl.pallas_call(
        body, name=name, grid=(D // BN,),
        in_specs=[_full((T, D)), pl.BlockSpec((BN, D), lambda j: (j, 0)), pl.BlockSpec((T, BN), lambda j: (0, j)), ANYSPEC],
        out_specs=[pl.BlockSpec((T, BN), lambda j: (0, j)), pl.BlockSpec((BN, D), lambda j: (j, 0))],
        out_shape=[_sds((T, D), F32), _sds((D, D), WIRE_DTYPE)], compiler_params=_params(("parallel",)),
    )(dh, w, ymix, after)


def mm_rows_nn(pairs, after, name):
    n = len(pairs)

    def body(*refs):
        o_ref = refs[2 * n + 1]

        @pl.when(pl.program_id(0) == 0)
        def _():
            o_ref[...] = jnp.zeros_like(o_ref)
        for p in range(n):
            o_ref[...] += _dot(refs[2 * p][...], refs[2 * p + 1][...], NN)

    specs, args = [], []
    for a, w in pairs:
        specs += [pl.BlockSpec((T, FB), lambda j: (0, j)), pl.BlockSpec((FB, D), lambda j: (j, 0))]
        args += [a, w]
    return pl.pallas_call(
        body, name=name, grid=(D_FFP // FB,), in_specs=specs + [ANYSPEC], out_specs=_full((T, D)),
        out_shape=_sds((T, D), F32), compiler_params=_params(("arbitrary",)),
    )(*args, after)


def mm_tn_blocked(a, b, name):
    def body(a_ref, b_ref, o_ref):
        o = _dot(a_ref[...], b_ref[...], TN).astype(o_ref.dtype)
        o_ref[0] = o[:, :IN_SH]
        o_ref[1] = o[:, IN_SH:]

    return pl.pallas_call(
        body, name=name, grid=(NPAIR,),
        in_specs=[_full((T, D)), COLS_PAIR], out_specs=W_PAIR(D),
        out_shape=_sds((NDEV, D, IN_SH), WIRE_DTYPE), compiler_params=_params(("parallel",)),
    )(a, b)


def mm_tn(a, b, bm, after, name):
    m = a.shape[1]

    def body(a_ref, b_ref, after_ref, o_ref):
        del after_ref
        o_ref[...] = _dot(a_ref[...], b_ref[...], TN).astype(o_ref.dtype)

    return pl.pallas_call(
        body, name=name, grid=(m // bm,),
        in_specs=[pl.BlockSpec((T, bm), lambda i: (0, i)), _full((T, D)), ANYSPEC],
        out_specs=pl.BlockSpec((bm, D), lambda i: (i, 0)),
        out_shape=_sds((m, D), WIRE_DTYPE), compiler_params=_params(("parallel",)),
    )(a, b, after)


def _softplus_neg(lam):
    return jnp.maximum(-lam, 0.0) + jnp.log1p(jnp.exp(-jnp.abs(lam)))


def _lru_gates(pa, px, xc, lam):
    r = jax.nn.sigmoid(pa)
    ig = jax.nn.sigmoid(px)
    sp = _softplus_neg(lam)
    log_a = -LRU_C * r * sp
    a = jnp.exp(log_a)
    mult = jnp.sqrt(-jnp.tanh(log_a) * (a * a + 1.0))
    return a, mult * (ig * xc), (r, ig, sp, mult)


def _lru_gates_vjp(da, db, xc, lam, a, r, ig, sp, mult):
    dmult = db * (ig * xc)
    du = db * mult
    dlog_a = da * a - dmult * (a * a) / mult
    dr = dlog_a * (-LRU_C * sp)
    dlam = jnp.sum(dlog_a * (-LRU_C * r), axis=0, keepdims=True) * (-jax.nn.sigmoid(-lam))
    dpa = dr * (r * (1.0 - r))
    dpx = (du * xc) * (ig * (1.0 - ig))
    return dpa, dpx, du * ig, dlam


def _lru_out(h, g, gain):
    z = h * jax.nn.gelu(g)
    return z * lax.rsqrt(jnp.mean(z * z, axis=-1, keepdims=True) + EPS) * gain


def _conv_taps(x, xprev, row):
    taps = [x]
    for s in range(1, CONV_W):
        taps.append(jnp.where(row < s, pltpu.roll(xprev, s, 0), pltpu.roll(x, s, 0)))
    return taps


def _conv(taps, cw_ref, cb):
    xc = cb + cw_ref[CONV_W - 1:CONV_W, :] * taps[0]
    for s in range(1, CONV_W):
        xc = xc + cw_ref[CONV_W - 1 - s:CONV_W - s, :] * taps[s]
    return xc


def lru_fwd(proj, cw, cb, wa, ba, wx, bx, lam, gain, name):
    def body(x_ref, g_ref, cw_ref, cb_ref, wa_ref, ba_ref, wx_ref, bx_ref, lam_ref, gain_ref,
             y_ref, h_ref, xprev_scr, a_scr, b_scr, carry_scr):
        i = pl.program_id(0)

        @pl.when(i == 0)
        def _():
            xprev_scr[...] = jnp.zeros_like(xprev_scr)
            carry_scr[...] = jnp.zeros_like(carry_scr)

        x = x_ref[...]
        row = lax.broadcasted_iota(jnp.int32, (CH, D_LRU), 0)
        xc = _conv(_conv_taps(x, xprev_scr[...], row), cw_ref, cb_ref[...])
        pa = _dot(xc, wa_ref[...], NN) + ba_ref[...]
        px = _dot(xc, wx_ref[...], NN) + bx_ref[...]
        a, b, _ = _lru_gates(pa, px, xc, lam_ref[...])
        a_scr[...] = a
        b_scr[...] = jnp.where(i * CH + row >= PAD, b, 0.0)
        h = carry_scr[...]
        for t in range(CH):
            h = a_scr[t:t + 1, :] * h + b_scr[t:t + 1, :]
            h_ref[t:t + 1, :] = h
        carry_scr[...] = h
        xprev_scr[...] = x
        y_ref[...] = _lru_out(h_ref[...], g_ref[...], gain_ref[...]).astype(y_ref.dtype)

    vec = _full((1, D_LRU))
    mat = _full((D_LRU, D_LRU))
    return pl.pallas_call(
        body, name=name, grid=(NCH,),
        in_specs=[pl.BlockSpec((CH, D_LRU), lambda i: (i, 0)), pl.BlockSpec((CH, D_LRU), lambda i: (i, 1)),
                  _full((CONV_W, D_LRU)), vec, mat, vec, mat, vec, vec, vec],
        out_specs=[pl.BlockSpec((CH, D_LRU), lambda i: (i, 0)), pl.BlockSpec((CH, D_LRU), lambda i: (i, 0))],
        out_shape=[_sds((T, D_LRU), MXU_DTYPE), _sds((T, D_LRU), F32)],
        scratch_shapes=[pltpu.VMEM((CH, D_LRU), F32), pltpu.VMEM((CH, D_LRU), F32), pltpu.VMEM((CH, D_LRU), F32),
                        pltpu.VMEM((1, D_LRU), F32)],
        compiler_params=_params(("arbitrary",)),
    )(proj, proj, cw, cb, wa, ba, wx, bx, lam, gain)


LRU_VEC_ROWS = 16


def lru_bwd(proj, hst, dymix, cw, cb, wa, ba, wx, bx, lam, gain, after, name):
    last = NCH - 1

    def body(x_ref, xp_ref, g_ref, h_ref, hp_ref, dy_ref, cw_ref, cb_ref, wa_ref, ba_ref, wx_ref, bx_ref, lam_ref,
             gain_ref, after_ref, dxg_ref, vec_ref, dwa_ref, dwx_ref, a_scr, dh_scr, g_scr, carry_scr, dxcn_scr):
        del after_ref
        i = pl.program_id(0)
        ib = last - i

        @pl.when(i == 0)
        def _():
            carry_scr[...] = jnp.zeros_like(carry_scr)
            dxcn_scr[...] = jnp.zeros_like(dxcn_scr)
            vec_ref[...] = jnp.zeros_like(vec_ref)
            dwa_ref[...] = jnp.zeros_like(dwa_ref)
            dwx_ref[...] = jnp.zeros_like(dwx_ref)

        x = x_ref[...]
        row = lax.broadcasted_iota(jnp.int32, (CH, D_LRU), 0)
        valid = ib * CH + row >= PAD
        taps = _conv_taps(x, xp_ref[...], row)
        xc = _conv(taps, cw_ref, cb_ref[...])
        pa = _dot(xc, wa_ref[...], NN) + ba_ref[...]
        px = _dot(xc, wx_ref[...], NN) + bx_ref[...]
        a, _, gate_parts = _lru_gates(pa, px, xc, lam_ref[...])
        h = h_ref[...]
        _, vjp_out = jax.vjp(_lru_out, h, g_ref[...], gain_ref[...])
        dh, dg, dgain = vjp_out(dy_ref[...].astype(F32))
        a_scr[...] = a
        dh_scr[...] = dh
        c = carry_scr[...]
        for t in range(CH - 1, -1, -1):
            gt = dh_scr[t:t + 1, :] + c
            g_scr[t:t + 1, :] = gt
            c = a_scr[t:t + 1, :] * gt
        carry_scr[...] = c
        gg = g_scr[...]
        hprev = jnp.where(row < 1, pltpu.roll(hp_ref[...], 1, 0), pltpu.roll(h, 1, 0))
        da = jnp.where(valid, gg * hprev, 0.0)
        db = jnp.where(valid, gg, 0.0)
        dpa, dpx, dxc, dlam = _lru_gates_vjp(da, db, xc, lam_ref[...], a, *gate_parts)
        dxc = dxc + _dot(dpa, wa_ref[...], NT) + _dot(dpx, wx_ref[...], NT)
        dwa_ref[...] += _dot(xc, dpa, TN)
        dwx_ref[...] += _dot(xc, dpx, TN)
        for s in range(CONV_W):
            vec_ref[CONV_W - 1 - s:CONV_W - s, :] += jnp.sum(dxc * taps[s], axis=0, keepdims=True)
        vec_ref[4:5, :] += jnp.sum(dxc, axis=0, keepdims=True)
        vec_ref[5:6, :] += jnp.sum(dpa, axis=0, keepdims=True)
        vec_ref[6:7, :] += jnp.sum(dpx, axis=0, keepdims=True)
        vec_ref[7:8, :] += dlam
        vec_ref[8:9, :] += dgain
        dxn = dxcn_scr[...]
        dx = cw_ref[CONV_W - 1:CONV_W, :] * dxc
        for s in range(1, CONV_W):
            ahead = jnp.where(row >= CH - s, pltpu.roll(dxn, CH - s, 0), pltpu.roll(dxc, CH - s, 0))
            dx = dx + cw_ref[CONV_W - 1 - s:CONV_W - s, :] * ahead
        dxcn_scr[...] = dxc
        dxg_ref[:, :D_LRU] = jnp.where(valid, dx, 0.0).astype(dxg_ref.dtype)
        dxg_ref[:, D_LRU:] = dg.astype(dxg_ref.dtype)

    vec = _full((1, D_LRU))
    mat = _full((D_LRU, D_LRU))

    def blk(col, shift=0):
        return pl.BlockSpec((CH, D_LRU), lambda i: (jnp.maximum(last - i - shift, 0), col))

    return pl.pallas_call(
        body, name=name, grid=(NCH,),
        in_specs=[blk(0), blk(0, 1), blk(1), blk(0), blk(0, 1), blk(0),
                  _full((CONV_W, D_LRU)), vec, mat, vec, mat, vec, vec, vec, pl.BlockSpec(memory_space=pl.ANY)],
        out_specs=[pl.BlockSpec((CH, 2 * D_LRU), lambda i: (last - i, 0)), _full((LRU_VEC_ROWS, D_LRU)), mat, mat],
        out_shape=[_sds((T, 2 * D_LRU), MXU_DTYPE), _sds((LRU_VEC_ROWS, D_LRU), F32),
                   _sds((D_LRU, D_LRU), F32), _sds((D_LRU, D_LRU), F32)],
        scratch_shapes=[pltpu.VMEM((CH, D_LRU), F32), pltpu.VMEM((CH, D_LRU), F32), pltpu.VMEM((CH, D_LRU), F32),
                        pltpu.VMEM((1, D_LRU), F32), pltpu.VMEM((CH, D_LRU), F32)],
        compiler_params=_params(("arbitrary",)),
    )(proj, proj, proj, hst, hst, dymix, cw, cb, wa, ba, wx, bx, lam, gain, after)


def _ret_tables():
    half = HD // 2
    pos = jnp.arange(T, dtype=F32) - float(PAD)
    inv = ROPE_BASE ** (-jnp.arange(half, dtype=F32) / half)
    ang = pos[:, None] * inv[None, :]
    cos = jnp.concatenate([jnp.cos(ang), jnp.cos(ang)], axis=-1)
    sin = jnp.concatenate([-jnp.sin(ang), jnp.sin(ang)], axis=-1)
    log_g = jnp.log(1.0 - 2.0 ** (-5.0 - jnp.arange(HEADS, dtype=F32)))
    idx = jnp.arange(CH, dtype=F32)
    diff = idx[:, None] - idx[None, :]
    dmask = jnp.where(diff[None] >= 0, jnp.exp(jnp.maximum(diff, 0.0)[None] * log_g[:, None, None]), 0.0)
    xi = jnp.exp((idx + 1.0)[None, :] * log_g[:, None])
    zeta = jnp.exp((CH - 1.0 - idx)[None, :] * log_g[:, None])
    xi = jnp.broadcast_to(xi[:, :, None], (HEADS, CH, HD))
    zeta = jnp.broadcast_to(zeta[:, :, None], (HEADS, CH, HD))
    return cos, sin, dmask, xi, zeta


def _chunk_decay():
    log_g = np.log(np.float32(1.0) - np.float32(2.0) ** (np.float32(-5.0) - np.arange(HEADS, dtype=np.float32)))
    return [float(v) for v in np.exp(np.float32(CH) * log_g.astype(np.float32))]


def _rope(x, cos, sin):
    return x * cos + pltpu.roll(x, HD // 2, 1) * sin


def ret_fwd(proj, ylru, tables, gain, after, name):
    cos, sin, dmask, xi, zeta = tables
    gch = _chunk_decay()
    scale = HD ** -0.5

    def body(q_ref, k_ref, v_ref, g_ref, cos_ref, sin_ref, dm_ref, xi_ref, zt_ref, gain_ref, ylru_ref, after_ref,
             y_ref, st_ref, s_scr):
        del after_ref

        @pl.when(pl.program_id(0) == 0)
        def _():
            s_scr[...] = jnp.zeros_like(s_scr)

        y_ref[:, :D_LRU] = ylru_ref[...]
        cs, sn = cos_ref[...], sin_ref[...]
        hs = range(HEADS)
        sl = [slice(HD * h, HD * (h + 1)) for h in hs]
        qr = [_rope(q_ref[:, sl[h]], cs, sn).astype(MXU_DTYPE) for h in hs]
        kf = [_rope(k_ref[:, sl[h]], cs, sn) * scale for h in hs]
        kr = [kf[h].astype(MXU_DTYPE) for h in hs]
        v = [v_ref[:, sl[h]].astype(MXU_DTYPE) for h in hs]
        s = [s_scr[h] for h in hs]
        for h in hs:
            st_ref[h] = s[h]
        sc = [_dot(qr[h], kr[h], NT) * dm_ref[h] for h in hs]
        cross = [_dot(qr[h], s[h], NN) * xi_ref[h] for h in hs]
        for h in hs:
            s_scr[h] = s[h] * gch[h] + _dot(kf[h] * zt_ref[h], v[h], TN)
        y = [_dot(sc[h], v[h], NN) + cross[h] for h in hs]
        yc = [y[h] - jnp.mean(y[h], axis=-1, keepdims=True) for h in hs]
        yn = [yc[h] * lax.rsqrt(jnp.mean(yc[h] * yc[h], axis=-1, keepdims=True) + EPS) for h in hs]
        for h in hs:
            so = slice(D_LRU + HD * h, D_LRU + HD * (h + 1))
            y_ref[:, so] = (jax.nn.silu(g_ref[:, sl[h]]) * (yn[h] * gain_ref[:, sl[h]])).astype(y_ref.dtype)

    def col(c):
        return pl.BlockSpec((CH, D_RET), lambda n: (n, c))

    tab = pl.BlockSpec((CH, HD), lambda n: (n, 0))
    cst = _full((HEADS, CH, HD))
    return pl.pallas_call(
        body, name=name, grid=(NCH,),
        in_specs=[col(2), col(3), col(4), col(5), tab, tab, cst, cst, cst, _full((1, D_RET)), col(0),
                  pl.BlockSpec(memory_space=pl.ANY)],
        out_specs=[pl.BlockSpec((CH, D), lambda n: (n, 0)), pl.BlockSpec((None, HEADS, HD, HD), lambda n: (n, 0, 0, 0))],
        out_shape=[_sds((T, D), MXU_DTYPE), _sds((NCH, HEADS, HD, HD), F32)],
        scratch_shapes=[pltpu.VMEM((HEADS, HD, HD), F32)],
        compiler_params=_params(("arbitrary",)),
    )(proj, proj, proj, proj, cos, sin, dmask, xi, zeta, gain, ylru, after)


def ret_bwd(proj, states, dymix, dxg, tables, gain, name):
    cos, sin, dmask, xi, zeta = tables
    gch = _chunk_decay()
    scale = HD ** -0.5
    last = NCH - 1

    def body(q_ref, k_ref, v_ref, g_ref, st_ref, do_ref, cos_ref, sin_ref, dm_ref, xi_ref, zt_ref, gain_ref, dxg_ref,
             dp_ref, dgain_ref, ds_scr):
        @pl.when(pl.program_id(0) == 0)
        def _():
            ds_scr[...] = jnp.zeros_like(ds_scr)
            dgain_ref[...] = jnp.zeros_like(dgain_ref)

        dp_ref[:, :2 * D_LRU] = dxg_ref[...]
        cs, sn = cos_ref[...], sin_ref[...]
        hs = range(HEADS)
        sl = [slice(HD * h, HD * (h + 1)) for h in hs]

        def out(j, h):
            return slice(2 * D_LRU + j * D_RET + HD * h, 2 * D_LRU + j * D_RET + HD * (h + 1))

        b16 = lambda xs: [x.astype(MXU_DTYPE) for x in xs]
        qr = b16([_rope(q_ref[:, sl[h]], cs, sn) for h in hs])
        kf = [_rope(k_ref[:, sl[h]], cs, sn) * scale for h in hs]
        kr = b16(kf)
        kz = b16([kf[h] * zt_ref[h] for h in hs])
        v = b16([v_ref[:, sl[h]] for h in hs])
        s = b16([st_ref[h] for h in hs])
        ds = [ds_scr[h] for h in hs]
        dsb = b16(ds)
        sc = [_dot(qr[h], kr[h], NT) * dm_ref[h] for h in hs]
        scb = b16(sc)
        y = [_dot(scb[h], v[h], NN) + _dot(qr[h], s[h], NN) * xi_ref[h] for h in hs]
        yc = [y[h] - jnp.mean(y[h], axis=-1, keepdims=True) for h in hs]
        rstd = [lax.rsqrt(jnp.mean(yc[h] * yc[h], axis=-1, keepdims=True) + EPS) for h in hs]
        yn = [yc[h] * rstd[h] for h in hs]
        dy = []
        for h in hs:
            g = g_ref[:, sl[h]]
            gain = gain_ref[:, sl[h]]
            sg = jax.nn.sigmoid(g)
            silu = g * sg
            dout = do_ref[:, sl[h]].astype(F32)
            dgain_ref[:, sl[h]] += jnp.sum(dout * silu * yn[h], axis=0, keepdims=True)
            dp_ref[:, out(3, h)] = (dout * yn[h] * gain * (sg * (1.0 + g * (1.0 - sg)))).astype(dp_ref.dtype)
            dyn = dout * silu * gain
            dy.append(rstd[h] * (dyn - jnp.mean(dyn, axis=-1, keepdims=True)
                                 - yn[h] * jnp.mean(dyn * yn[h], axis=-1, keepdims=True)))
        dyb = b16(dy)
        dqs = b16([dy[h] * xi_ref[h] for h in hs])
        dp = b16([_dot(dyb[h], v[h], NT) * dm_ref[h] for h in hs])
        dv = [_dot(scb[h], dyb[h], TN) + _dot(kz[h], dsb[h], NN) for h in hs]
        dqr = [_dot(dp[h], kr[h], NN) + _dot(dqs[h], s[h], NT) for h in hs]
        dkr = [_dot(dp[h], qr[h], TN) + _dot(v[h], dsb[h], NT) * zt_ref[h] for h in hs]
        for h in hs:
            ds_scr[h] = gch[h] * ds[h] + _dot(qr[h], dqs[h], TN)
        for h in hs:
            dp_ref[:, out(0, h)] = (dqr[h] * cs + pltpu.roll(dqr[h] * sn, HD // 2, 1)).astype(dp_ref.dtype)
            dp_ref[:, out(1, h)] = ((dkr[h] * cs + pltpu.roll(dkr[h] * sn, HD // 2, 1)) * scale).astype(dp_ref.dtype)
            dp_ref[:, out(2, h)] = dv[h].astype(dp_ref.dtype)

    def col(c):
        return pl.BlockSpec((CH, D_RET), lambda n: (last - n, c))

    tab = pl.BlockSpec((CH, HD), lambda n: (last - n, 0))
    cst = _full((HEADS, CH, HD))
    return pl.pallas_call(
        body, name=name, grid=(NCH,),
        in_specs=[col(2), col(3), col(4), col(5), pl.BlockSpec((None, HEADS, HD, HD), lambda n: (last - n, 0, 0, 0)), col(1),
                  tab, tab, cst, cst, cst, _full((1, D_RET)), pl.BlockSpec((CH, 2 * D_LRU), lambda n: (last - n, 0))],
        out_specs=[pl.BlockSpec((CH, D_IN), lambda n: (last - n, 0)), _full((1, D_RET))],
        out_shape=[_sds((T, D_IN), MXU_DTYPE), _sds((1, D_RET), F32)],
        scratch_shapes=[pltpu.VMEM((HEADS, HD, HD), F32)],
        compiler_params=_params(("arbitrary",)),
    )(proj, proj, proj, proj, states, dymix, cos, sin, dmask, xi, zeta, gain, dxg)


HBM = pl.BlockSpec(memory_space=pltpu.HBM)


def _place():
    return lax.axis_index("x"), lax.axis_index("y"), lax.axis_index("c")


def all_gather(arrs, name):
    n = len(arrs)

    def body(*refs):
        ins, outs = refs[:n], refs[n:2 * n]
        send_sems, recv_sems, local_sems = refs[2 * n:]
        x, y, c = _place()
        me, sibling = (x, y, c), (x, y, 1 - c)
        chips = [(1 - x, y), (x, 1 - y), (1 - x, 1 - y)]

        def copy(a, k, block, to, src=None):
            px, py, pc = block
            dst = outs[a].at[4 * px + 2 * py + pc]
            return pltpu.make_async_remote_copy(
                src_ref=dst if src is None else src, dst_ref=dst, send_sem=send_sems.at[a, k], recv_sem=recv_sems.at[a, k],
                device_id=to, device_id_type=MESH)

        mine = [pltpu.make_async_copy(ins[a], outs[a].at[4 * x + 2 * y + c], local_sems.at[a]) for a in range(n)]
        for cp in mine:
            cp.start()
        first = []
        for a in range(n):
            first.append(copy(a, 0, me, sibling, src=ins[a]))
            first += [copy(a, 1 + j, me, (*chip, c), src=ins[a]) for j, chip in enumerate(chips)]
        for cp in first:
            cp.start()
        passed = []
        for j, chip in enumerate(chips):
            for a in range(n):
                copy(a, 1 + j, (*chip, c), me).wait_recv()
                passed.append(copy(a, 4 + j, (*chip, c), sibling))
                passed[-1].start()
        for a in range(n):
            copy(a, 0, sibling, me).wait_recv()
            for j, chip in enumerate(chips):
                copy(a, 4 + j, (*chip, 1 - c), me).wait_recv()
        for cp in first + passed:
            cp.wait_send()
        for cp in mine:
            cp.wait()

    return pl.pallas_call(
        body, name=name,
        in_specs=[HBM] * n, out_specs=[HBM] * n,
        out_shape=[_sds((NDEV,) + a.shape, a.dtype) for a in arrs],
        scratch_shapes=[pltpu.SemaphoreType.DMA((n, 7)), pltpu.SemaphoreType.DMA((n, 7)), pltpu.SemaphoreType.DMA((n,))],
    )(*arrs)


SEM = pl.BlockSpec(memory_space=pltpu.SEMAPHORE)
ANY = pl.BlockSpec(memory_space=pl.ANY)
EFFECT = pltpu.SideEffectType.DATAFLOW_SIDE_EFFECTING


def _hbm(a):
    return pltpu.with_memory_space_constraint(a, pltpu.HBM)


def _hbm_like(arrs):
    return [pltpu.HBM(a.shape, a.dtype) for a in arrs]


def _dma_sems(count):
    return [pltpu.SemaphoreType.DMA(())] * count


def _ag_copy(lands, send_sems, recv_sems, per):
    def copy(a, k, block, to, src=None):
        px, py, pc = block
        dst = lands[a].at[4 * px + 2 * py + pc]
        return pltpu.make_async_remote_copy(
            src_ref=dst if src is None else src, dst_ref=dst, send_sem=send_sems[a * per + k], recv_sem=recv_sems[a * per + k],
            device_id=to, device_id_type=MESH)
    return copy


def to_wire(sel, w_in, w_gate, w_up, w_out, w_down, name):
    ffpad = FF_SHP - FF_SH

    def body(sel_ref, i_ref, g_ref, u_ref, o_ref, d_ref, oi, og, ou, oo, od):
        del sel_ref
        oi[...] = i_ref[...].astype(oi.dtype)
        oo[...] = o_ref[...].astype(oo.dtype)
        for src, dst in ((g_ref, og), (u_ref, ou), (d_ref, od)):
            dst[:FF_SH, :] = src[...].astype(dst.dtype)
            dst[FF_SH:, :] = jnp.zeros((ffpad, D), dst.dtype)

    shapes_in = [(D, IN_SH), (FF_SH, D), (FF_SH, D), (OUT_SH, D), (FF_SH, D)]
    shapes_out = [(D, IN_SH), (FF_SHP, D), (FF_SHP, D), (OUT_SH, D), (FF_SHP, D)]
    return pl.pallas_call(
        body, name=name,
        grid_spec=pltpu.PrefetchScalarGridSpec(
            num_scalar_prefetch=1, grid=(1,),
            in_specs=[pl.BlockSpec((None,) + s, lambda i, sel_ref: (sel_ref[1], 0, 0)) for s in shapes_in],
            out_specs=[pl.BlockSpec((None,) + s, lambda i, sel_ref: (sel_ref[0], 0, 0)) for s in shapes_out]),
        out_shape=[_sds((NDEV,) + s, WIRE_DTYPE) for s in shapes_out], compiler_params=_params(("arbitrary",)),
    )(sel, w_in, w_gate, w_up, w_out, w_down)


def place_blocks(sel, arrs, name):
    n = len(arrs)

    def body(sel_ref, *refs):
        del sel_ref
        for a in range(n):
            refs[n + a][...] = refs[a][...]

    def whole(a):
        nd = a.ndim
        return pl.BlockSpec(a.shape, lambda i, sel_ref: (0,) * nd)

    def mine(a):
        nd = a.ndim
        return pl.BlockSpec((None,) + a.shape, lambda i, sel_ref: (sel_ref[0],) + (0,) * nd)

    return pl.pallas_call(
        body, name=name,
        grid_spec=pltpu.PrefetchScalarGridSpec(
            num_scalar_prefetch=1, grid=(1,), in_specs=[whole(a) for a in arrs], out_specs=[mine(a) for a in arrs]),
        out_shape=[_sds((NDEV,) + a.shape, a.dtype) for a in arrs], compiler_params=_params(("arbitrary",)),
    )(sel, *arrs)


def ag_start(lands, after, name):
    n = len(lands)
    ns = 4 * n

    def body(*refs):
        lnd = refs[:n]
        send_sems, recv_sems = refs[n + 1:n + 1 + ns], refs[n + 1 + ns:n + 1 + 2 * ns]
        token = refs[-1]
        x, y, c = _place()
        me, sibling = (x, y, c), (x, y, 1 - c)
        chips = [(1 - x, y), (x, 1 - y), (1 - x, 1 - y)]
        copy = _ag_copy(lnd, send_sems, recv_sems, 4)
        for a in range(n):
            copy(a, 0, me, sibling).start()
            for j, chip in enumerate(chips):
                copy(a, 1 + j, me, (*chip, c)).start()
        token[...] = jnp.zeros_like(token)

    outs = pl.pallas_call(
        body, name=name,
        in_specs=[HBM] * n + [ANY],
        out_specs=[SEM] * (2 * ns) + [HBM] * n + [pl.BlockSpec(memory_space=pltpu.VMEM)],
        out_shape=_dma_sems(2 * ns) + _hbm_like(lands) + [_sds((8, 128), F32)],
        input_output_aliases={i: 2 * ns + i for i in range(n)},
        compiler_params=pltpu.CompilerParams(has_side_effects=EFFECT),
    )(*[_hbm(a) for a in lands], after)
    return outs[:ns], outs[ns:2 * ns], outs[2 * ns:2 * ns + n], outs[-1]


def ag_forward(send_sems, recv_sems, lands, after, name):
    n = len(lands)
    n1, n2 = 4 * n, 3 * n

    def body(*refs):
        lnd = refs[:n]
        o = n
        s1, r1 = refs[o:o + n1], refs[o + n1:o + 2 * n1]
        o += 2 * n1 + 1
        s2, r2 = refs[o:o + n2], refs[o + n2:o + 2 * n2]
        token = refs[-1]
        token[...] = jnp.zeros_like(token)
        x, y, c = _place()
        me, sibling = (x, y, c), (x, y, 1 - c)
        chips = [(1 - x, y), (x, 1 - y), (1 - x, 1 - y)]
        copy1 = _ag_copy(lnd, s1, r1, 4)
        copy2 = _ag_copy(lnd, s2, r2, 3)
        for j, chip in enumerate(chips):
            for a in range(n):
                copy1(a, 1 + j, (*chip, c), me).wait_recv()
                copy2(a, j, (*chip, c), sibling).start()
        for a in range(n):
            copy1(a, 0, sibling, me).wait_recv()
            copy1(a, 0, me, sibling).wait_send()
            for j, chip in enumerate(chips):
                copy1(a, 1 + j, me, (*chip, c)).wait_send()

    outs = pl.pallas_call(
        body, name=name,
        in_specs=[HBM] * n + [SEM] * (2 * n1) + [ANY],
        out_specs=[SEM] * (2 * n2) + [HBM] * n + [pl.BlockSpec(memory_space=pltpu.VMEM)],
        out_shape=_dma_sems(2 * n2) + _hbm_like(lands) + [_sds((8, 128), F32)],
        input_output_aliases={i: 2 * n2 + i for i in range(n)},
        compiler_params=pltpu.CompilerParams(has_side_effects=EFFECT),
    )(*lands, *send_sems, *recv_sems, after)
    return outs[:n2], outs[n2:2 * n2], outs[2 * n2:2 * n2 + n], outs[-1]


def ag_finish(send_sems, recv_sems, lands, after, name):
    n = len(lands)
    n2 = 3 * n

    def body(*refs):
        lnd = refs[:n]
        s2, r2 = refs[n:n + n2], refs[n + n2:n + 2 * n2]
        x, y, c = _place()
        me, sibling = (x, y, c), (x, y, 1 - c)
        chips = [(1 - x, y), (x, 1 - y), (1 - x, 1 - y)]
        copy2 = _ag_copy(lnd, s2, r2, 3)
        for a in range(n):
            for j, chip in enumerate(chips):
                copy2(a, j, (*chip, c), sibling).wait_send()
                copy2(a, j, (*chip, 1 - c), me).wait_recv()

    outs = pl.pallas_call(
        body, name=name,
        in_specs=[HBM] * n + [SEM] * (2 * n2) + [ANY],
        out_specs=[HBM] * n, out_shape=_hbm_like(lands),
        input_output_aliases={i: i for i in range(n)},
        compiler_params=pltpu.CompilerParams(has_side_effects=EFFECT),
    )(*lands, *send_sems, *recv_sems, after)
    return list(outs)


def rs_sibling_start(arrs, name):
    n = len(arrs)
    ns = 4 * n
    lands = [lax.empty((4,) + a.shape[1:], a.dtype) for a in arrs]

    def body(*refs):
        ins, lnd = refs[:n], refs[n:2 * n]
        send_sems, recv_sems = refs[2 * n:2 * n + ns], refs[2 * n + ns:2 * n + 2 * ns]
        x, y, c = _place()
        sibling = (x, y, 1 - c)
        for a in range(n):
            for p in range(4):
                pltpu.make_async_remote_copy(
                    src_ref=ins[a].at[2 * p + 1 - c], dst_ref=lnd[a].at[p], send_sem=send_sems[4 * a + p],
                    recv_sem=recv_sems[4 * a + p], device_id=sibling, device_id_type=MESH).start()
        refs[-1][...] = jnp.zeros_like(refs[-1])

    outs = pl.pallas_call(
        body, name=name,
        in_specs=[HBM] * (2 * n), out_specs=[SEM] * (2 * ns) + [HBM] * (2 * n) + [pl.BlockSpec(memory_space=pltpu.VMEM)],
        out_shape=_dma_sems(2 * ns) + _hbm_like(arrs) + _hbm_like(lands) + [_sds((8, 128), F32)],
        input_output_aliases={i: 2 * ns + i for i in range(2 * n)},
        compiler_params=pltpu.CompilerParams(has_side_effects=EFFECT),
    )(*[_hbm(a) for a in arrs], *[_hbm(a) for a in lands])
    return (outs[:ns], outs[ns:2 * ns], outs[2 * ns:2 * ns + n], outs[2 * ns + n:2 * ns + 2 * n]), outs[-1]


def rs_sibling_wait(send_sems, recv_sems, arrs, lands, after, name):
    n = len(arrs)
    ns = 4 * n

    def body(*refs):
        ins, lnd = refs[:n], refs[n:2 * n]
        s, r = refs[2 * n:2 * n + ns], refs[2 * n + ns:2 * n + 2 * ns]
        x, y, c = _place()
        sibling = (x, y, 1 - c)
        for a in range(n):
            for p in range(4):
                cp = pltpu.make_async_remote_copy(
                    src_ref=ins[a].at[2 * p + 1 - c], dst_ref=lnd[a].at[p], send_sem=s[4 * a + p], recv_sem=r[4 * a + p],
                    device_id=sibling, device_id_type=MESH)
                cp.wait_send()
                cp.wait_recv()

    outs = pl.pallas_call(
        body, name=name,
        in_specs=[HBM] * (2 * n) + [SEM] * (2 * ns) + [ANY], out_specs=[HBM] * (2 * n),
        out_shape=_hbm_like(arrs) + _hbm_like(lands),
        input_output_aliases={i: i for i in range(2 * n)},
        compiler_params=pltpu.CompilerParams(has_side_effects=EFFECT),
    )(*arrs, *lands, *send_sems, *recv_sems, after)
    return outs[:n], outs[n:]


def rs_chips_start(parts, name):
    n = len(parts)
    ns = 3 * n
    lands = [lax.empty((3,) + a.shape[1:], a.dtype) for a in parts]

    def body(*refs):
        ins, lnd = refs[:n], refs[n:2 * n]
        send_sems, recv_sems = refs[2 * n:2 * n + ns], refs[2 * n + ns:2 * n + 2 * ns]
        x, y, c = _place()
        chips = [(1 - x, y), (x, 1 - y), (1 - x, 1 - y)]
        for a in range(n):
            for k, (tx, ty) in enumerate(chips):
                pltpu.make_async_remote_copy(
                    src_ref=ins[a].at[2 * tx + ty], dst_ref=lnd[a].at[k], send_sem=send_sems[3 * a + k],
                    recv_sem=recv_sems[3 * a + k], device_id=(tx, ty, c), device_id_type=MESH).start()
        refs[-1][...] = jnp.zeros_like(refs[-1])

    outs = pl.pallas_call(
        body, name=name,
        in_specs=[HBM] * (2 * n), out_specs=[SEM] * (2 * ns) + [HBM] * (2 * n) + [pl.BlockSpec(memory_space=pltpu.VMEM)],
        out_shape=_dma_sems(2 * ns) + _hbm_like(parts) + _hbm_like(lands) + [_sds((8, 128), F32)],
        input_output_aliases={i: 2 * ns + i for i in range(2 * n)},
        compiler_params=pltpu.CompilerParams(has_side_effects=EFFECT),
    )(*[_hbm(a) for a in parts], *[_hbm(a) for a in lands])
    return (outs[:ns], outs[ns:2 * ns], outs[2 * ns:2 * ns + n], outs[2 * ns + n:2 * ns + 2 * n]), outs[-1]


def rs_chips_wait(send_sems, recv_sems, parts, lands, after, name):
    n = len(parts)
    ns = 3 * n

    def body(*refs):
        ins, lnd = refs[:n], refs[n:2 * n]
        s, r = refs[2 * n:2 * n + ns], refs[2 * n + ns:2 * n + 2 * ns]
        x, y, c = _place()
        chips = [(1 - x, y), (x, 1 - y), (1 - x, 1 - y)]
        for a in range(n):
            for k, (tx, ty) in enumerate(chips):
                cp = pltpu.make_async_remote_copy(
                    src_ref=ins[a].at[2 * tx + ty], dst_ref=lnd[a].at[k], send_sem=s[3 * a + k], recv_sem=r[3 * a + k],
                    device_id=(tx, ty, c), device_id_type=MESH)
                cp.wait_send()
                cp.wait_recv()

    outs = pl.pallas_call(
        body, name=name,
        in_specs=[HBM] * (2 * n) + [SEM] * (2 * ns) + [ANY], out_specs=[HBM] * (2 * n),
        out_shape=_hbm_like(parts) + _hbm_like(lands),
        input_output_aliases={i: i for i in range(2 * n)},
        compiler_params=pltpu.CompilerParams(has_side_effects=EFFECT),
    )(*parts, *lands, *send_sems, *recv_sems, after)
    return outs[:n], outs[n:]


def pair_sum(arrs, recv, c, name):
    n = len(arrs)

    def body(c_ref, *refs):
        del c_ref
        for a in range(n):
            refs[2 * n + a][...] = (refs[a][...].astype(F32) + refs[n + a][...].astype(F32)).astype(refs[2 * n + a].dtype)

    mine = [pl.BlockSpec((None,) + a.shape[1:], lambda p, c_ref: (2 * p + c_ref[0], 0, 0)) for a in arrs]
    other = [pl.BlockSpec((None,) + a.shape[1:], lambda p, c_ref: (p, 0, 0)) for a in arrs]
    return pl.pallas_call(
        body, name=name,
        grid_spec=pltpu.PrefetchScalarGridSpec(num_scalar_prefetch=1, grid=(4,), in_specs=mine + other, out_specs=other),
        out_shape=[_sds((4,) + a.shape[1:], a.dtype) for a in arrs], compiler_params=_params(("parallel",)),
    )(c, *arrs, *recv)


def _adamw(w, g, m, v):
    m = ADAM_B1 * m + (1.0 - ADAM_B1) * g
    v = ADAM_B2 * v + (1.0 - ADAM_B2) * jnp.square(g)
    m_hat = m / (1.0 - ADAM_B1 ** ADAM_STEP)
    v_hat = v / (1.0 - ADAM_B2 ** ADAM_STEP)
    return -ADAM_LR * (m_hat / (jnp.sqrt(v_hat) + ADAM_EPS) + ADAM_WD * w), m, v


def adamw_big(recv, sums, chip, w, m, v, tr, name):
    nl, rr, cc = w.shape
    cp = recv[0].shape[2]

    def body(chip_ref, *refs):
        del chip_ref
        rcv, own = refs[:nl], refs[nl:2 * nl]
        w_ref, m_ref, v_ref, g_out, d_out, m_out, v_out = refs[2 * nl:]
        for l in range(nl):
            g = ((own[l][...].astype(F32) + rcv[l][0].astype(F32)) + rcv[l][1].astype(F32)) + rcv[l][2].astype(F32)
            g = g[:, :cc]
            g_out[l] = g
            d_out[l], m_out[l], v_out[l] = _adamw(w_ref[l], g, m_ref[l], v_ref[l])

    blk = pl.BlockSpec((nl, tr, cc), lambda i, chip_ref: (0, i, 0))
    return pl.pallas_call(
        body, name=name,
        grid_spec=pltpu.PrefetchScalarGridSpec(
            num_scalar_prefetch=1, grid=(rr // tr,),
            in_specs=[pl.BlockSpec((3, tr, cp), lambda i, chip_ref: (0, i, 0))] * nl
            + [pl.BlockSpec((None, tr, cp), lambda i, chip_ref: (chip_ref[0], i, 0))] * nl + [blk, blk, blk],
            out_specs=[blk] * 4),
        out_shape=[_sds(w.shape, F32)] * 4, compiler_params=_params(("parallel",)),
    )(chip, *recv, *sums, w, m, v)


SMALL_ROWS = 24


def small_grads(lvec, g_ret, g_mix, g_ffn, g_final, loss_part, dwa, dwx, name):
    def body(lvec_ref, ret_ref, mix_ref, ffn_ref, fin_ref, loss_ref, dwa_ref, dwx_ref, v_ref, g_ref):
        v_ref[16:SMALL_ROWS, :] = jnp.zeros((SMALL_ROWS - 16, D_LRU), F32)
        v_ref[16:17, 0:128] = loss_ref[0:1, :]
        v_ref[0:9, :] = lvec_ref[0:9, :]
        v_ref[9:10, :] = ret_ref[...]
        for r, src in ((10, mix_ref), (12, ffn_ref), (14, fin_ref)):
            v_ref[r:r + 1, :] = src[:, :D_LRU]
            v_ref[r + 1:r + 2, :] = src[:, D_LRU:]
        for k, src in enumerate((dwa_ref, dwx_ref)):
            for g in range(LRU_BLOCKS):
                rows = slice(LRU_BD * g, LRU_BD * (g + 1))
                g_ref[D_LRU * k + LRU_BD * g:D_LRU * k + LRU_BD * (g + 1), :] = src[rows, rows]

    ins = [lvec, g_ret, g_mix, g_ffn, g_final, loss_part, dwa, dwx]
    return pl.pallas_call(
        body, name=name, grid=(1,), in_specs=[_full(a.shape) for a in ins],
        out_specs=[_full((SMALL_ROWS, D_LRU)), _full((2 * D_LRU, LRU_BD))],
        out_shape=[_sds((SMALL_ROWS, D_LRU), F32), _sds((2 * D_LRU, LRU_BD), F32)], compiler_params=_params(("arbitrary",)),
    )(*ins)


def sum_devices(arrs, name):
    n = len(arrs)

    def body(*refs):
        for a in range(n):
            acc = refs[a][0]
            for j in range(1, NDEV):
                acc = acc + refs[a][j]
            refs[n + a][...] = acc

    return pl.pallas_call(
        body, name=name, grid=(1,), in_specs=[_full(a.shape) for a in arrs], out_specs=[_full(a.shape[1:]) for a in arrs],
        out_shape=[_sds(a.shape[1:], F32) for a in arrs], compiler_params=_params(("arbitrary",)),
    )(*arrs)


def adamw_small(gs, ws, ms, vs, name):
    n = len(gs)

    def body(*refs):
        for a in range(n):
            g, w, m, v = (refs[k * n + a][...] for k in range(4))
            refs[4 * n + a][...], refs[5 * n + a][...], refs[6 * n + a][...] = _adamw(w, g, m, v)

    specs = [_full(a.shape) for a in ws]
    outs = pl.pallas_call(
        body, name=name, grid=(1,), in_specs=specs * 4, out_specs=specs * 3, out_shape=[_sds(a.shape, F32) for a in ws] * 3,
        compiler_params=_params(("arbitrary",)),
    )(*gs, *ws, *ms, *vs)
    return outs[:n], outs[n:2 * n], outs[2 * n:]


def block_diag(wa, wx, name):
    def body(wa_ref, wx_ref, oa_ref, ox_ref):
        for src, dst in ((wa_ref, oa_ref), (wx_ref, ox_ref)):
            dst[...] = jnp.zeros_like(dst)
            for g in range(LRU_BLOCKS):
                rows = slice(LRU_BD * g, LRU_BD * (g + 1))
                dst[rows, rows] = src[g].astype(dst.dtype)

    ispec = pl.BlockSpec((None, LRU_BLOCKS, LRU_BD, LRU_BD), lambda l: (l, 0, 0, 0))
    ospec = pl.BlockSpec((None, D_LRU, D_LRU), lambda l: (l, 0, 0))
    return pl.pallas_call(
        body, name=name, grid=(wa.shape[0],), in_specs=[ispec, ispec], out_specs=[ospec, ospec],
        out_shape=[_sds((wa.shape[0], D_LRU, D_LRU), MXU_DTYPE)] * 2, compiler_params=_params(("parallel",)),
    )(wa, wx)


REP_NAMES = ["norm_mix", "conv_b", "gate_a_w", "gate_a_b", "gate_x_w", "gate_x_b", "lru_lambda", "lru_out_norm",
             "ret_out_norm", "norm_ffn", "norm_final"]


def kernel(x, meta_tokens, norm_mix, w_in, conv_w, conv_b, gate_a_w, gate_a_b, gate_x_w, gate_x_b, lru_lambda, lru_out_norm, ret_out_norm, w_out, norm_ffn, w_gate, w_up, w_down, norm_final, loss_target, m_meta_tokens, m_norm_mix, m_w_in, m_conv_w, m_conv_b, m_gate_a_w, m_gate_a_b, m_gate_x_w, m_gate_x_b, m_lru_lambda, m_lru_out_norm, m_ret_out_norm, m_w_out, m_norm_ffn, m_w_gate, m_w_up, m_w_down, m_norm_final, v_meta_tokens, v_norm_mix, v_w_in, v_conv_w, v_conv_b, v_gate_a_w, v_gate_a_b, v_gate_x_w, v_gate_x_b, v_lru_lambda, v_lru_out_norm, v_ret_out_norm, v_w_out, v_norm_ffn, v_w_gate, v_w_up, v_w_down, v_norm_final):
    xi, yi, ci = _place()
    dev = 4 * xi + 2 * yi + ci
    c_arr = jnp.reshape(ci, (1,)).astype(jnp.int32)
    dev_arr = jnp.reshape(dev, (1,)).astype(jnp.int32)

    meta_g, conv_g = all_gather([meta_tokens, conv_w], "ag_small")
    meta_full = jnp.transpose(meta_g, (1, 0, 2)).reshape(N_META, D)
    conv_full = jnp.transpose(conv_g, (1, 2, 0, 3)).reshape(DEPTH, CONV_W, D_LRU)
    tr_ = lambda a: jnp.transpose(a, (0, 2, 1))
    w_gate_t, m_w_gate_t, v_w_gate_t = tr_(w_gate), tr_(m_w_gate), tr_(v_w_gate)
    w_up_t, m_w_up_t, v_w_up_t = tr_(w_up), tr_(m_w_up), tr_(v_w_up)
    level1 = []
    token = meta_g
    for l in range(DEPTH):
        sel = jnp.stack([dev, jnp.int32(l)]).astype(jnp.int32)
        lands = to_wire(sel, w_in, w_gate_t, w_up_t, w_out, w_down, "to_wire")
        s1, r1, lands, token = ag_start(lands, token, f"ag_start_{l}")
        level1.append((s1, r1, lands))

    def as_weights(gi, gg, gu, go, gd):
        return dict(w_in=gi, w_gate=gg.reshape(D_FFP, D), w_up=gu.reshape(D_FFP, D), w_out=go.reshape(D, D),
                    w_down=gd.reshape(D_FFP, D))

    tables = _ret_tables()
    row = lambda a: a.reshape(1, -1)

    h = jnp.concatenate([jnp.zeros((PAD, D), F32), meta_full, x[0]], axis=0)
    saved, gathered = [], []
    s1, r1, lands = level1[0]
    s2, r2, first, order = ag_forward(s1[:4], r1[:4], lands[:1], token, "ag_forward_0_w_in")
    w_in_next = ag_finish(s2, r2, first, h, "ag_finish_0_w_in")[0]
    wa_dense, wx_dense = block_diag(gate_a_w, gate_x_w, "block_diag")
    for l in range(DEPTH):
        small = dict(cw=conv_full[l], cb=row(conv_b[l]), wa=wa_dense[l], ba=row(gate_a_b[l]),
                     wx=wx_dense[l], bx=row(gate_x_b[l]), lam=row(lru_lambda[l]),
                     gain=row(lru_out_norm[l]))
        s1, r1, lands = level1[l]
        hn1 = rmsnorm_fwd(h, row(norm_mix[l]), "rms_fwd")
        proj = mm_blocked_nn(hn1, w_in_next, F32, "proj")
        ylru, hst = lru_fwd(proj, name="lru_fwd", **small)
        s2, r2, rest, order = ag_forward(s1[4:], r1[4:], lands[1:], ylru, f"ag_forward_{l}_rest")
        ymix, states = ret_fwd(proj, ylru, tables, row(ret_out_norm[l]), order, "ret_fwd")
        w = as_weights(w_in_next, *ag_finish(s2, r2, rest, ymix, f"ag_finish_{l}_rest"))
        gathered.append(w)
        h_mid = mm_nn_res(ymix, w["w_out"], h, order, "out_proj")
        hn2 = rmsnorm_fwd(h_mid, row(norm_ffn[l]), "rms_fwd")
        act_dgate, act_dup, act = ffn_up(hn2, w["w_gate"], w["w_up"], "ffn_up")
        if l + 1 < DEPTH:
            s1n, r1n, landsn = level1[l + 1]
            s2, r2, first, order = ag_forward(s1n[:4], r1n[:4], landsn[:1], act, f"ag_forward_{l + 1}_w_in")
        h_out = mm_nn_res(act, w["w_down"], h_mid, order, "ffn_down")
        if l + 1 < DEPTH:
            w_in_next = ag_finish(s2, r2, first, h_out, f"ag_finish_{l + 1}_w_in")[0]
        saved.append(dict(h=h, hn1=hn1, proj=proj, hst=hst, states=states, ymix=ymix, h_mid=h_mid, hn2=hn2, act_dgate=act_dgate, act_dup=act_dup,
                          act=act, small=small))
        h = h_out

    loss_p, dh, dh_b, g_norm_final = loss_head(h, row(norm_final), loss_target[0], "loss_head")

    small_v = [None] * DEPTH
    small_w = [None] * DEPTH
    inflight = []
    sib = None
    order = loss_p

    def sibling_done(l, tag, names, sib, after):
        parts, got = rs_sibling_wait(*sib, after, f"rs_sibling_wait_{tag}")
        sums = pair_sum(parts, got, c_arr, "pair_sum")
        flying, started = rs_chips_start(sums, f"rs_chips_start_{tag}")
        inflight.append((l, tag, names, flying))
        return started

    for l in reversed(range(DEPTH)):
        w, s = gathered[l], saved[l]
        dgate, dup = ffn_down_bwd(dh_b, w["w_down"], s["act_dgate"], s["act_dup"], order, "ffn_down_bwd")
        dwd = mm_tn(s["act"], dh_b, PAIR, order, "dw_down").reshape(NDEV, FF_SHP, D)
        dwg, dwu = (g.reshape(NDEV, FF_SHP, D) for g in mm_tn_two(dgate, dup, s["hn2"], PAIR, order, "dw_rows"))
        split = l <= 1
        if split:
            ffn_sib, order = rs_sibling_start([dwg, dwu, dwd], f"rs_sibling_start_{l}_ffn")
        dhn2 = mm_rows_nn([(dgate, w["w_gate"]), (dup, w["w_up"])], order, "ffn_up_bwd")
        if sib is not None:
            order = sibling_done(l + 1, sib_tag, sib_names, sib, dhn2)
        dh_mid, dh_mid_b, g_norm_ffn = rmsnorm_bwd(s["h_mid"], row(norm_ffn[l]), dhn2, dh, "rms_bwd")
        dymix, dwo = out_proj_bwd(dh_mid_b, w["w_out"], s["ymix"], order, "out_proj_bwd")
        dwo = dwo.reshape(NDEV, OUT_SH, D)
        if split:
            order = sibling_done(l, f"{l}_ffn", ("w_gate", "w_up", "w_down"), ffn_sib, dymix)
        dxg, lvec, dwa, dwx = lru_bwd(s["proj"], s["hst"], dymix, after=order, name="lru_bwd", **s["small"])
        dproj, g_ret_norm = ret_bwd(s["proj"], s["states"], dymix, dxg, tables, row(ret_out_norm[l]), "ret_bwd")
        dwi = mm_tn_blocked(s["hn1"], dproj, "dw_blocked")
        dhn1 = mm_blocked_nt([(dproj, w["w_in"])], order, "proj_bwd")
        dh, dh_b, g_norm_mix = rmsnorm_bwd(s["h"], row(norm_mix[l]), dhn1, dh_mid, "rms_bwd")

        g_fin, loss_part = (g_norm_final, loss_p) if l == 0 else (jnp.zeros((1, D), F32), jnp.zeros((8, 128), F32))
        small_v[l], small_w[l] = small_grads(lvec, g_ret_norm, g_norm_mix, g_norm_ffn, g_fin, loss_part, dwa, dwx,
                                             "small_grads")
        if split:
            sib_tag, sib_names = f"{l}_mix", ("w_in", "w_out")
            sib, order = rs_sibling_start([dwi, dwo], f"rs_sibling_start_{l}_mix")
        else:
            sib_tag, sib_names = str(l), ("w_in", "w_gate", "w_up", "w_out", "w_down")
            sib, order = rs_sibling_start([dwi, dwg, dwu, dwo, dwd], f"rs_sibling_start_{l}")
        if l == 1:
            early = place_blocks(dev_arr, [jnp.stack(small_v[1:]), jnp.stack(small_w[1:])], "place_grads")
            early_sems = ag_start(early, order, "ag_start_grads")
            order = early_sems[3]

    grad_x = dh[X0:][None]
    g_meta = dh[PAD:X0]

    late = all_gather([small_v[0], small_w[0], g_meta], "ag_grads")
    s2, r2, lands, _ = ag_forward(early_sems[0], early_sems[1], early_sems[2], dh, "ag_forward_grads")
    gath_early = ag_finish(s2, r2, lands, late[0], "ag_finish_grads")
    sibling_done(0, sib_tag, sib_names, sib, late[0])
    v0, w0, meta_sum, v123, w123 = sum_devices(list(late) + list(gath_early), "sum_devices")
    loss = v0[16, 0]
    vecs = jnp.concatenate([v0[None], v123])
    gws = jnp.concatenate([w0[None], w123])
    blocks = (DEPTH, LRU_BLOCKS, LRU_BD)
    small_g = dict(
        conv_w=lax.dynamic_slice_in_dim(vecs[:, 0:CONV_W], dev * (D_LRU // NDEV), D_LRU // NDEV, axis=2),
        conv_b=vecs[:, 4], gate_a_b=vecs[:, 5].reshape(blocks), gate_x_b=vecs[:, 6].reshape(blocks),
        lru_lambda=vecs[:, 7], lru_out_norm=vecs[:, 8], ret_out_norm=vecs[:, 9],
        norm_mix=vecs[:, 10:12].reshape(DEPTH, D), norm_ffn=vecs[:, 12:14].reshape(DEPTH, D),
        norm_final=v0[14:16].reshape(1, D),
        gate_a_w=gws[:, :D_LRU].reshape(blocks + (LRU_BD,)), gate_x_w=gws[:, D_LRU:].reshape(blocks + (LRU_BD,)),
        meta_tokens=lax.dynamic_slice_in_dim(meta_sum, dev * (D // NDEV), D // NDEV, axis=1))
    given = dict(norm_mix=(norm_mix, m_norm_mix, v_norm_mix), conv_b=(conv_b, m_conv_b, v_conv_b),
                 gate_a_w=(gate_a_w, m_gate_a_w, v_gate_a_w), gate_a_b=(gate_a_b, m_gate_a_b, v_gate_a_b),
                 gate_x_w=(gate_x_w, m_gate_x_w, v_gate_x_w), gate_x_b=(gate_x_b, m_gate_x_b, v_gate_x_b),
                 lru_lambda=(lru_lambda, m_lru_lambda, v_lru_lambda), lru_out_norm=(lru_out_norm, m_lru_out_norm, v_lru_out_norm),
                 ret_out_norm=(ret_out_norm, m_ret_out_norm, v_ret_out_norm), norm_ffn=(norm_ffn, m_norm_ffn, v_norm_ffn),
                 norm_final=tuple(a.reshape(1, D) for a in (norm_final, m_norm_final, v_norm_final)),
                 conv_w=(conv_w, m_conv_w, v_conv_w), meta_tokens=(meta_tokens, m_meta_tokens, v_meta_tokens))
    small_names = REP_NAMES + ["conv_w", "meta_tokens"]
    upd = adamw_small([small_g[n] for n in small_names], *[[given[n][k] for n in small_names] for k in range(3)],
                      "adamw_small")
    small_out = [dict(zip(small_names, u)) for u in upd]
    for d_ in [small_g] + small_out:
        d_["norm_final"] = d_["norm_final"].reshape(D)

    arrived = {}

    def wait_for(entries, after):
        for l, tag, names, flying in entries:
            sums, recv = rs_chips_wait(*flying, after, f"rs_chips_wait_{tag}")
            for i, n in enumerate(names):
                arrived[l, n] = (recv[i], sums[i])

    chip = jnp.reshape(2 * xi + yi, (1,)).astype(jnp.int32)

    def finish(wname, w_, m_, v_, tr):
        return adamw_big([arrived[l, wname][0] for l in range(DEPTH)], [arrived[l, wname][1] for l in range(DEPTH)], chip,
                         w_, m_, v_, tr, "adamw_" + wname)

    wait_for(inflight[:-1], upd[0][0])
    o_gate = [tr_(o) for o in finish("w_gate", w_gate_t, m_w_gate_t, v_w_gate_t, 32)]
    o_up = [tr_(o) for o in finish("w_up", w_up_t, m_w_up_t, v_w_up_t, 32)]
    o_down = finish("w_down", w_down, m_w_down, v_w_down, 32)
    wait_for(inflight[-1:], o_down[0])
    o_in = finish("w_in", w_in, m_w_in, v_w_in, 256)
    o_out = finish("w_out", w_out, m_w_out, v_w_out, 64)

    bigs = dict(w_in=o_in, w_out=o_out, w_gate=o_gate, w_up=o_up, w_down=o_down)
    order = ["meta_tokens", "norm_mix", "w_in", "conv_w", "conv_b", "gate_a_w", "gate_a_b", "gate_x_w", "gate_x_b", "lru_lambda",
             "lru_out_norm", "ret_out_norm", "w_out", "norm_ffn", "w_gate", "w_up", "w_down", "norm_final"]
    grads = [bigs[n][0] if n in bigs else small_g[n] for n in order]
    rest = [[bigs[n][k + 1] if n in bigs else small_out[k][n] for n in order] for k in range(3)]
    return (loss, grad_x, *grads, *rest[0], *rest[1], *rest[2])
```

```python
import functools

import numpy as np
import jax
import jax.numpy as jnp
from jax import lax
from jax.experimental import pallas as pl
from jax.experimental.pallas import tpu as pltpu

F32, BF16 = jnp.float32, jnp.bfloat16
MXU_DTYPE = BF16
WIRE_DTYPE = BF16

D = 1024
SEQ = 2048
DEPTH = 4
N_META = 16
CH = 128
PAD = (-(SEQ + N_META)) % CH
T = SEQ + N_META + PAD
NCH = T // CH
X0 = PAD + N_META
D_LRU = 512
LRU_BLOCKS = 8
LRU_BD = 64
CONV_W = 4
LRU_C = 8.0
D_RET = 512
HEADS = 4
HD = 128
ROPE_BASE = 10000.0
D_IN = 3072
D_FF = 2816
NDEV = 8
IN_SH = D_IN // NDEV
FF_SH = D_FF // NDEV
FF_SHP = 384
D_FFP = NDEV * FF_SHP
OUT_SH = D // NDEV
EPS = 1e-6
TM = 544
TR = 1088
VMEM_LIMIT = 56 * 2**20
MESH = pl.DeviceIdType.MESH

ADAM_LR, ADAM_B1, ADAM_B2, ADAM_EPS, ADAM_WD, ADAM_STEP = 0.001, 0.9, 0.999, 1e-08, 0.01, 10

NN = ((1,), (0,))
NT = ((1,), (1,))
TN = ((0,), (0,))


def _dot(a, b, dims):
    return lax.dot_general(a.astype(MXU_DTYPE), b.astype(MXU_DTYPE), (dims, ((), ())), preferred_element_type=F32)


def _sds(shape, dtype):
    return jax.ShapeDtypeStruct(shape, dtype)


def _params(sem=None):
    return pltpu.CompilerParams(dimension_semantics=sem, vmem_limit_bytes=VMEM_LIMIT)


def _full(shape):
    n = len(shape)
    return pl.BlockSpec(shape, lambda *_: (0,) * n)


def rmsnorm_fwd(h, gain, name):
    def body(h_ref, g_ref, o_ref):
        x = h_ref[...]
        ms = jnp.mean(x * x, axis=-1, keepdims=True)
        o_ref[...] = (x * lax.rsqrt(ms + EPS) * g_ref[...]).astype(o_ref.dtype)

    return pl.pallas_call(
        body, name=name, grid=(T // TM,),
        in_specs=[pl.BlockSpec((TM, D), lambda i: (i, 0)), _full((1, D))],
        out_specs=pl.BlockSpec((TM, D), lambda i: (i, 0)),
        out_shape=_sds((T, D), MXU_DTYPE), compiler_params=_params(("parallel",)),
    )(h, gain)


def rmsnorm_bwd(h, gain, dhn, dres, name):
    def body(h_ref, g_ref, dhn_ref, dres_ref, dh_ref, dhb_ref, dg_ref):
        x = h_ref[...]
        rstd = lax.rsqrt(jnp.mean(x * x, axis=-1, keepdims=True) + EPS)
        xhat = x * rstd
        dy = dhn_ref[...]
        dyg = dy * g_ref[...]
        dh = dres_ref[...] + rstd * (dyg - xhat * jnp.mean(dyg * xhat, axis=-1, keepdims=True))
        dh_ref[...] = dh
        dhb_ref[...] = dh.astype(dhb_ref.dtype)

        @pl.when(pl.program_id(0) == 0)
        def _():
            dg_ref[...] = jnp.zeros_like(dg_ref)
        dg_ref[...] += jnp.sum(dy * xhat, axis=0, keepdims=True)

    row = pl.BlockSpec((TM, D), lambda i: (i, 0))
    return pl.pallas_call(
        body, name=name, grid=(T // TM,),
        in_specs=[row, _full((1, D)), row, row],
        out_specs=[row, row, _full((1, D))],
        out_shape=[_sds((T, D), F32), _sds((T, D), MXU_DTYPE), _sds((1, D), F32)], compiler_params=_params(("arbitrary",)),
    )(h, gain, dhn, dres)


def loss_head(h, gain, target, name):
    def body(h_ref, g_ref, t_ref, loss_ref, dh_ref, dhb_ref, dg_ref):
        i = pl.program_id(0)

        @pl.when(i == 0)
        def _():
            loss_ref[...] = jnp.zeros_like(loss_ref)
            dg_ref[...] = jnp.zeros_like(dg_ref)
            dh_ref[...] = jnp.zeros_like(dh_ref)
            dhb_ref[...] = jnp.zeros_like(dhb_ref)

        @pl.when(i > 0)
        def _():
            x = h_ref[...]
            g = g_ref[...]
            rstd = lax.rsqrt(jnp.mean(x * x, axis=-1, keepdims=True) + EPS)
            xhat = x * rstd
            err = xhat * g - t_ref[...]
            loss_ref[...] += 0.5 * jnp.sum(jnp.mean(err * err, axis=-1, keepdims=True), axis=0, keepdims=True)
            dy = err * (1.0 / D)
            dyg = dy * g
            dh = rstd * (dyg - xhat * jnp.mean(dyg * xhat, axis=-1, keepdims=True))
            dh_ref[...] = dh
            dhb_ref[...] = dh.astype(dhb_ref.dtype)
            dg_ref[...] += jnp.sum(dy * xhat, axis=0, keepdims=True)

    row = pl.BlockSpec((CH, D), lambda i: (i, 0))
    return pl.pallas_call(
        body, name=name, grid=(NCH,),
        in_specs=[row, _full((1, D)), pl.BlockSpec((CH, D), lambda i: (jnp.maximum(i - 1, 0), 0))],
        out_specs=[_full((8, 128)), row, row, _full((1, D))],
        out_shape=[_sds((8, 128), F32), _sds((T, D), F32), _sds((T, D), MXU_DTYPE), _sds((1, D), F32)],
        compiler_params=_params(("arbitrary",)),
    )(h, gain, target)


PAIR = 2 * IN_SH
NPAIR = NDEV // 2
BN = 256
FB = 512


def _pair_cols(w_ref):
    return jnp.concatenate([w_ref[0], w_ref[1]], axis=1)


W_PAIR = lambda k: pl.BlockSpec((2, k, IN_SH), lambda j: (j, 0, 0))
COLS_PAIR = pl.BlockSpec((T, PAIR), lambda j: (0, j))
ANYSPEC = pl.BlockSpec(memory_space=pl.ANY)


def mm_blocked_nn(a, w, out_dtype, name):
    k = a.shape[1]

    def body(a_ref, w_ref, o_ref):
        o_ref[...] = _dot(a_ref[...], _pair_cols(w_ref), NN).astype(o_ref.dtype)

    return pl.pallas_call(
        body, name=name, grid=(NPAIR,),
        in_specs=[_full((T, k)), W_PAIR(k)], out_specs=COLS_PAIR,
        out_shape=_sds((T, NDEV * IN_SH), out_dtype), compiler_params=_params(("parallel",)),
    )(a, w)


def mm_nn_res(a, w, res, after, name):
    k = a.shape[1]

    def body(a_ref, w_ref, r_ref, after_ref, o_ref):
        del after_ref
        o_ref[...] = r_ref[...] + _dot(a_ref[...], w_ref[...], NN)

    col = pl.BlockSpec((T, BN), lambda j: (0, j))
    return pl.pallas_call(
        body, name=name, grid=(D // BN,),
        in_specs=[_full((T, k)), pl.BlockSpec((k, BN), lambda j: (0, j)), col, ANYSPEC], out_specs=col,
        out_shape=_sds((T, D), F32), compiler_params=_params(("parallel",)),
    )(a, w, res, after)


def ffn_up(hn, wg, wu, name):
    def body(a_ref, wg_ref, wu_ref, dg_ref, du_ref, act_ref):
        a = a_ref[...]
        for c in range(FB // BN):
            cols = slice(BN * c, BN * (c + 1))
            g = _dot(a, wg_ref[cols, :], NT)
            u = _dot(a, wu_ref[cols, :], NT)
            sg = jax.nn.sigmoid(g)
            silu = g * sg
            dg_ref[:, cols] = (u * (sg * (1.0 + g * (1.0 - sg)))).astype(dg_ref.dtype)
            du_ref[:, cols] = silu.astype(du_ref.dtype)
            act_ref[:, cols] = (silu * u).astype(act_ref.dtype)

    wspec = pl.BlockSpec((FB, D), lambda j: (j, 0))
    ospec = pl.BlockSpec((T, FB), lambda j: (0, j))
    return pl.pallas_call(
        body, name=name, grid=(D_FFP // FB,),
        in_specs=[_full((T, D)), wspec, wspec], out_specs=[ospec] * 3,
        out_shape=[_sds((T, D_FFP), MXU_DTYPE)] * 3, compiler_params=_params(("parallel",)),
    )(hn, wg, wu)


def ffn_down_bwd(dh, wd, dact_dgate, dact_dup, after, name):
    def body(dh_ref, wd_ref, g_ref, u_ref, after_ref, dg_ref, du_ref):
        del after_ref
        dh = dh_ref[...]
        for c in range(FB // BN):
            cols = slice(BN * c, BN * (c + 1))
            dact = _dot(dh, wd_ref[cols, :], NT)
            dg_ref[:, cols] = (dact * g_ref[:, cols].astype(F32)).astype(dg_ref.dtype)
            du_ref[:, cols] = (dact * u_ref[:, cols].astype(F32)).astype(du_ref.dtype)

    blk = pl.BlockSpec((T, FB), lambda j: (0, j))
    return pl.pallas_call(
        body, name=name, grid=(D_FFP // FB,),
        in_specs=[_full((T, D)), pl.BlockSpec((FB, D), lambda j: (j, 0)), blk, blk, ANYSPEC],
        out_specs=[blk, blk],
        out_shape=[_sds((T, D_FFP), MXU_DTYPE)] * 2, compiler_params=_params(("parallel",)),
    )(dh, wd, dact_dgate, dact_dup, after)


def mm_blocked_nt(pairs, after, name):
    n = len(pairs)

    def body(*refs):
        o_ref = refs[2 * n + 1]

        @pl.when(pl.program_id(0) == 0)
        def _():
            o_ref[...] = jnp.zeros_like(o_ref)
        for p in range(n):
            o_ref[...] += _dot(refs[2 * p][...], _pair_cols(refs[2 * p + 1]), NT)

    specs, args = [], []
    for a, w in pairs:
        specs += [COLS_PAIR, W_PAIR(D)]
        args += [a, w]
    return pl.pallas_call(
        body, name=name, grid=(NPAIR,), in_specs=specs + [ANYSPEC], out_specs=_full((T, D)),
        out_shape=_sds((T, D), F32), compiler_params=_params(("arbitrary",)),
    )(*args, after)


def mm_tn_two(a1, a2, b, bm, after, name):
    m = a1.shape[1]

    def body(a1_ref, a2_ref, b_ref, after_ref, o1_ref, o2_ref):
        del after_ref
        b = b_ref[...]
        o1_ref[...] = _dot(a1_ref[...], b, TN).astype(o1_ref.dtype)
        o2_ref[...] = _dot(a2_ref[...], b, TN).astype(o2_ref.dtype)

    blk = pl.BlockSpec((T, bm), lambda i: (0, i))
    out = pl.BlockSpec((bm, D), lambda i: (i, 0))
    return pl.pallas_call(
        body, name=name, grid=(m // bm,),
        in_specs=[blk, blk, _full((T, D)), ANYSPEC], out_specs=[out, out],
        out_shape=[_sds((m, D), WIRE_DTYPE)] * 2, compiler_params=_params(("parallel",)),
    )(a1, a2, b, after)


def out_proj_bwd(dh, w, ymix, after, name):
    def body(dh_ref, w_ref, y_ref, after_ref, dy_ref, dw_ref):
        del after_ref
        dh_ = dh_ref[...]
        dy_ref[...] = _dot(dh_, w_ref[...], NT)
        dw_ref[...] = _dot(y_ref[...], dh_, TN).astype(dw_ref.dtype)

    return pl.pallas_call(
        body, name=name, grid=(D // BN,),
        in_specs=[_full((T, D)), pl.BlockSpec((BN, D), lambda j: (j, 0)), pl.BlockSpec((T, BN), lambda j: (0, j)), ANYSPEC],
        out_specs=[pl.BlockSpec((T, BN), lambda j: (0, j)), pl.BlockSpec((BN, D), lambda j: (j, 0))],
        out_shape=[_sds((T, D), F32), _sds((D, D), WIRE_DTYPE)], compiler_params=_params(("parallel",)),
    )(dh, w, ymix, after)


def mm_rows_nn(pairs, after, name):
    n = len(pairs)

    def body(*refs):
        o_ref = refs[2 * n + 1]

        @pl.when(pl.program_id(0) == 0)
        def _():
            o_ref[...] = jnp.zeros_like(o_ref)
        for p in range(n):
            o_ref[...] += _dot(refs[2 * p][...], refs[2 * p + 1][...], NN)

    specs, args = [], []
    for a, w in pairs:
        specs += [pl.BlockSpec((T, FB), lambda j: (0, j)), pl.BlockSpec((FB, D), lambda j: (j, 0))]
        args += [a, w]
    return pl.pallas_call(
        body, name=name, grid=(D_FFP // FB,), in_specs=specs + [ANYSPEC], out_specs=_full((T, D)),
        out_shape=_sds((T, D), F32), compiler_params=_params(("arbitrary",)),
    )(*args, after)


def mm_tn_blocked(a, b, name):
    def body(a_ref, b_ref, o_ref):
        o = _dot(a_ref[...], b_ref[...], TN).astype(o_ref.dtype)
        o_ref[0] = o[:, :IN_SH]
        o_ref[1] = o[:, IN_SH:]

    return pl.pallas_call(
        body, name=name, grid=(NPAIR,),
        in_specs=[_full((T, D)), COLS_PAIR], out_specs=W_PAIR(D),
        out_shape=_sds((NDEV, D, IN_SH), WIRE_DTYPE), compiler_params=_params(("parallel",)),
    )(a, b)


def mm_tn(a, b, bm, after, name):
    m = a.shape[1]

    def body(a_ref, b_ref, after_ref, o_ref):
        del after_ref
        o_ref[...] = _dot(a_ref[...], b_ref[...], TN).astype(o_ref.dtype)

    return pl.pallas_call(
        body, name=name, grid=(m // bm,),
        in_specs=[pl.BlockSpec((T, bm), lambda i: (0, i)), _full((T, D)), ANYSPEC],
        out_specs=pl.BlockSpec((bm, D), lambda i: (i, 0)),
        out_shape=_sds((m, D), WIRE_DTYPE), compiler_params=_params(("parallel",)),
    )(a, b, after)


def _softplus_neg(lam):
    return jnp.maximum(-lam, 0.0) + jnp.log1p(jnp.exp(-jnp.abs(lam)))


def _lru_gates(pa, px, xc, lam):
    r = jax.nn.sigmoid(pa)
    ig = jax.nn.sigmoid(px)
    sp = _softplus_neg(lam)
    log_a = -LRU_C * r * sp
    a = jnp.exp(log_a)
    mult = jnp.sqrt(-jnp.tanh(log_a) * (a * a + 1.0))
    return a, mult * (ig * xc), (r, ig, sp, mult)


def _lru_gates_vjp(da, db, xc, lam, a, r, ig, sp, mult):
    dmult = db * (ig * xc)
    du = db * mult
    dlog_a = da * a - dmult * (a * a) / mult
    dr = dlog_a * (-LRU_C * sp)
    dlam = jnp.sum(dlog_a * (-LRU_C * r), axis=0, keepdims=True) * (-jax.nn.sigmoid(-lam))
    dpa = dr * (r * (1.0 - r))
    dpx = (du * xc) * (ig * (1.0 - ig))
    return dpa, dpx, du * ig, dlam


def _lru_out(h, g, gain):
    z = h * jax.nn.gelu(g)
    return z * lax.rsqrt(jnp.mean(z * z, axis=-1, keepdims=True) + EPS) * gain


def _conv_taps(x, xprev, row):
    taps = [x]
    for s in range(1, CONV_W):
        taps.append(jnp.where(row < s, pltpu.roll(xprev, s, 0), pltpu.roll(x, s, 0)))
    return taps


def _conv(taps, cw_ref, cb):
    xc = cb + cw_ref[CONV_W - 1:CONV_W, :] * taps[0]
    for s in range(1, CONV_W):
        xc = xc + cw_ref[CONV_W - 1 - s:CONV_W - s, :] * taps[s]
    return xc


def lru_fwd(proj, cw, cb, wa, ba, wx, bx, lam, gain, name):
    def body(x_ref, g_ref, cw_ref, cb_ref, wa_ref, ba_ref, wx_ref, bx_ref, lam_ref, gain_ref,
             y_ref, h_ref, xprev_scr, a_scr, b_scr, carry_scr):
        i = pl.program_id(0)

        @pl.when(i == 0)
        def _():
            xprev_scr[...] = jnp.zeros_like(xprev_scr)
            carry_scr[...] = jnp.zeros_like(carry_scr)

        x = x_ref[...]
        row = lax.broadcasted_iota(jnp.int32, (CH, D_LRU), 0)
        xc = _conv(_conv_taps(x, xprev_scr[...], row), cw_ref, cb_ref[...])
        pa = _dot(xc, wa_ref[...], NN) + ba_ref[...]
        px = _dot(xc, wx_ref[...], NN) + bx_ref[...]
        a, b, _ = _lru_gates(pa, px, xc, lam_ref[...])
        a_scr[...] = a
        b_scr[...] = jnp.where(i * CH + row >= PAD, b, 0.0)
        h = carry_scr[...]
        for t in range(CH):
            h = a_scr[t:t + 1, :] * h + b_scr[t:t + 1, :]
            h_ref[t:t + 1, :] = h
        carry_scr[...] = h
        xprev_scr[...] = x
        y_ref[...] = _lru_out(h_ref[...], g_ref[...], gain_ref[...]).astype(y_ref.dtype)

    vec = _full((1, D_LRU))
    mat = _full((D_LRU, D_LRU))
    return pl.pallas_call(
        body, name=name, grid=(NCH,),
        in_specs=[pl.BlockSpec((CH, D_LRU), lambda i: (i, 0)), pl.BlockSpec((CH, D_LRU), lambda i: (i, 1)),
                  _full((CONV_W, D_LRU)), vec, mat, vec, mat, vec, vec, vec],
        out_specs=[pl.BlockSpec((CH, D_LRU), lambda i: (i, 0)), pl.BlockSpec((CH, D_LRU), lambda i: (i, 0))],
        out_shape=[_sds((T, D_LRU), MXU_DTYPE), _sds((T, D_LRU), F32)],
        scratch_shapes=[pltpu.VMEM((CH, D_LRU), F32), pltpu.VMEM((CH, D_LRU), F32), pltpu.VMEM((CH, D_LRU), F32),
                        pltpu.VMEM((1, D_LRU), F32)],
        compiler_params=_params(("arbitrary",)),
    )(proj, proj, cw, cb, wa, ba, wx, bx, lam, gain)


LRU_VEC_ROWS = 16


def lru_bwd(proj, hst, dymix, cw, cb, wa, ba, wx, bx, lam, gain, after, name):
    last = NCH - 1

    def body(x_ref, xp_ref, g_ref, h_ref, hp_ref, dy_ref, cw_ref, cb_ref, wa_ref, ba_ref, wx_ref, bx_ref, lam_ref,
             gain_ref, after_ref, dxg_ref, vec_ref, dwa_ref, dwx_ref, a_scr, dh_scr, g_scr, carry_scr, dxcn_scr):
        del after_ref
        i = pl.program_id(0)
        ib = last - i

        @pl.when(i == 0)
        def _():
            carry_scr[...] = jnp.zeros_like(carry_scr)
            dxcn_scr[...] = jnp.zeros_like(dxcn_scr)
            vec_ref[...] = jnp.zeros_like(vec_ref)
            dwa_ref[...] = jnp.zeros_like(dwa_ref)
            dwx_ref[...] = jnp.zeros_like(dwx_ref)

        x = x_ref[...]
        row = lax.broadcasted_iota(jnp.int32, (CH, D_LRU), 0)
        valid = ib * CH + row >= PAD
        taps = _conv_taps(x, xp_ref[...], row)
        xc = _conv(taps, cw_ref, cb_ref[...])
        pa = _dot(xc, wa_ref[...], NN) + ba_ref[...]
        px = _dot(xc, wx_ref[...], NN) + bx_ref[...]
        a, _, gate_parts = _lru_gates(pa, px, xc, lam_ref[...])
        h = h_ref[...]
        _, vjp_out = jax.vjp(_lru_out, h, g_ref[...], gain_ref[...])
        dh, dg, dgain = vjp_out(dy_ref[...].astype(F32))
        a_scr[...] = a
        dh_scr[...] = dh
        c = carry_scr[...]
        for t in range(CH - 1, -1, -1):
            gt = dh_scr[t:t + 1, :] + c
            g_scr[t:t + 1, :] = gt
            c = a_scr[t:t + 1, :] * gt
        carry_scr[...] = c
        gg = g_scr[...]
        hprev = jnp.where(row < 1, pltpu.roll(hp_ref[...], 1, 0), pltpu.roll(h, 1, 0))
        da = jnp.where(valid, gg * hprev, 0.0)
        db = jnp.where(valid, gg, 0.0)
        dpa, dpx, dxc, dlam = _lru_gates_vjp(da, db, xc, lam_ref[...], a, *gate_parts)
        dxc = dxc + _dot(dpa, wa_ref[...], NT) + _dot(dpx, wx_ref[...], NT)
        dwa_ref[...] += _dot(xc, dpa, TN)
        dwx_ref[...] += _dot(xc, dpx, TN)
        for s in range(CONV_W):
            vec_ref[CONV_W - 1 - s:CONV_W - s, :] += jnp.sum(dxc * taps[s], axis=0, keepdims=True)
        vec_ref[4:5, :] += jnp.sum(dxc, axis=0, keepdims=True)
        vec_ref[5:6, :] += jnp.sum(dpa, axis=0, keepdims=True)
        vec_ref[6:7, :] += jnp.sum(dpx, axis=0, keepdims=True)
        vec_ref[7:8, :] += dlam
        vec_ref[8:9, :] += dgain
        dxn = dxcn_scr[...]
        dx = cw_ref[CONV_W - 1:CONV_W, :] * dxc
        for s in range(1, CONV_W):
            ahead = jnp.where(row >= CH - s, pltpu.roll(dxn, CH - s, 0), pltpu.roll(dxc, CH - s, 0))
            dx = dx + cw_ref[CONV_W - 1 - s:CONV_W - s, :] * ahead
        dxcn_scr[...] = dxc
        dxg_ref[:, :D_LRU] = jnp.where(valid, dx, 0.0).astype(dxg_ref.dtype)
        dxg_ref[:, D_LRU:] = dg.astype(dxg_ref.dtype)

    vec = _full((1, D_LRU))
    mat = _full((D_LRU, D_LRU))

    def blk(col, shift=0):
        return pl.BlockSpec((CH, D_LRU), lambda i: (jnp.maximum(last - i - shift, 0), col))

    return pl.pallas_call(
        body, name=name, grid=(NCH,),
        in_specs=[blk(0), blk(0, 1), blk(1), blk(0), blk(0, 1), blk(0),
                  _full((CONV_W, D_LRU)), vec, mat, vec, mat, vec, vec, vec, pl.BlockSpec(memory_space=pl.ANY)],
        out_specs=[pl.BlockSpec((CH, 2 * D_LRU), lambda i: (last - i, 0)), _full((LRU_VEC_ROWS, D_LRU)), mat, mat],
        out_shape=[_sds((T, 2 * D_LRU), MXU_DTYPE), _sds((LRU_VEC_ROWS, D_LRU), F32),
                   _sds((D_LRU, D_LRU), F32), _sds((D_LRU, D_LRU), F32)],
        scratch_shapes=[pltpu.VMEM((CH, D_LRU), F32), pltpu.VMEM((CH, D_LRU), F32), pltpu.VMEM((CH, D_LRU), F32),
                        pltpu.VMEM((1, D_LRU), F32), pltpu.VMEM((CH, D_LRU), F32)],
        compiler_params=_params(("arbitrary",)),
    )(proj, proj, proj, hst, hst, dymix, cw, cb, wa, ba, wx, bx, lam, gain, after)


def _ret_tables():
    half = HD // 2
    pos = jnp.arange(T, dtype=F32) - float(PAD)
    inv = ROPE_BASE ** (-jnp.arange(half, dtype=F32) / half)
    ang = pos[:, None] * inv[None, :]
    cos = jnp.concatenate([jnp.cos(ang), jnp.cos(ang)], axis=-1)
    sin = jnp.concatenate([-jnp.sin(ang), jnp.sin(ang)], axis=-1)
    log_g = jnp.log(1.0 - 2.0 ** (-5.0 - jnp.arange(HEADS, dtype=F32)))
    idx = jnp.arange(CH, dtype=F32)
    diff = idx[:, None] - idx[None, :]
    dmask = jnp.where(diff[None] >= 0, jnp.exp(jnp.maximum(diff, 0.0)[None] * log_g[:, None, None]), 0.0)
    xi = jnp.exp((idx + 1.0)[None, :] * log_g[:, None])
    zeta = jnp.exp((CH - 1.0 - idx)[None, :] * log_g[:, None])
    xi = jnp.broadcast_to(xi[:, :, None], (HEADS, CH, HD))
    zeta = jnp.broadcast_to(zeta[:, :, None], (HEADS, CH, HD))
    return cos, sin, dmask, xi, zeta


def _chunk_decay():
    log_g = np.log(np.float32(1.0) - np.float32(2.0) ** (np.float32(-5.0) - np.arange(HEADS, dtype=np.float32)))
    return [float(v) for v in np.exp(np.float32(CH) * log_g.astype(np.float32))]


def _rope(x, cos, sin):
    return x * cos + pltpu.roll(x, HD // 2, 1) * sin


def ret_fwd(proj, ylru, tables, gain, after, name):
    cos, sin, dmask, xi, zeta = tables
    gch = _chunk_decay()
    scale = HD ** -0.5

    def body(q_ref, k_ref, v_ref, g_ref, cos_ref, sin_ref, dm_ref, xi_ref, zt_ref, gain_ref, ylru_ref, after_ref,
             y_ref, st_ref, s_scr):
        del after_ref

        @pl.when(pl.program_id(0) == 0)
        def _():
            s_scr[...] = jnp.zeros_like(s_scr)

        y_ref[:, :D_LRU] = ylru_ref[...]
        cs, sn = cos_ref[...], sin_ref[...]
        hs = range(HEADS)
        sl = [slice(HD * h, HD * (h + 1)) for h in hs]
        qr = [_rope(q_ref[:, sl[h]], cs, sn).astype(MXU_DTYPE) for h in hs]
        kf = [_rope(k_ref[:, sl[h]], cs, sn) * scale for h in hs]
        kr = [kf[h].astype(MXU_DTYPE) for h in hs]
        v = [v_ref[:, sl[h]].astype(MXU_DTYPE) for h in hs]
        s = [s_scr[h] for h in hs]
        for h in hs:
            st_ref[h] = s[h]
        sc = [_dot(qr[h], kr[h], NT) * dm_ref[h] for h in hs]
        cross = [_dot(qr[h], s[h], NN) * xi_ref[h] for h in hs]
        for h in hs:
            s_scr[h] = s[h] * gch[h] + _dot(kf[h] * zt_ref[h], v[h], TN)
        y = [_dot(sc[h], v[h], NN) + cross[h] for h in hs]
        yc = [y[h] - jnp.mean(y[h], axis=-1, keepdims=True) for h in hs]
        yn = [yc[h] * lax.rsqrt(jnp.mean(yc[h] * yc[h], axis=-1, keepdims=True) + EPS) for h in hs]
        for h in hs:
            so = slice(D_LRU + HD * h, D_LRU + HD * (h + 1))
            y_ref[:, so] = (jax.nn.silu(g_ref[:, sl[h]]) * (yn[h] * gain_ref[:, sl[h]])).astype(y_ref.dtype)

    def col(c):
        return pl.BlockSpec((CH, D_RET), lambda n: (n, c))

    tab = pl.BlockSpec((CH, HD), lambda n: (n, 0))
    cst = _full((HEADS, CH, HD))
    return pl.pallas_call(
        body, name=name, grid=(NCH,),
        in_specs=[col(2), col(3), col(4), col(5), tab, tab, cst, cst, cst, _full((1, D_RET)), col(0),
                  pl.BlockSpec(memory_space=pl.ANY)],
        out_specs=[pl.BlockSpec((CH, D), lambda n: (n, 0)), pl.BlockSpec((None, HEADS, HD, HD), lambda n: (n, 0, 0, 0))],
        out_shape=[_sds((T, D), MXU_DTYPE), _sds((NCH, HEADS, HD, HD), F32)],
        scratch_shapes=[pltpu.VMEM((HEADS, HD, HD), F32)],
        compiler_params=_params(("arbitrary",)),
    )(proj, proj, proj, proj, cos, sin, dmask, xi, zeta, gain, ylru, after)


def ret_bwd(proj, states, dymix, dxg, tables, gain, name):
    cos, sin, dmask, xi, zeta = tables
    gch = _chunk_decay()
    scale = HD ** -0.5
    last = NCH - 1

    def body(q_ref, k_ref, v_ref, g_ref, st_ref, do_ref, cos_ref, sin_ref, dm_ref, xi_ref, zt_ref, gain_ref, dxg_ref,
             dp_ref, dgain_ref, ds_scr):
        @pl.when(pl.program_id(0) == 0)
        def _():
            ds_scr[...] = jnp.zeros_like(ds_scr)
            dgain_ref[...] = jnp.zeros_like(dgain_ref)

        dp_ref[:, :2 * D_LRU] = dxg_ref[...]
        cs, sn = cos_ref[...], sin_ref[...]
        hs = range(HEADS)
        sl = [slice(HD * h, HD * (h + 1)) for h in hs]

        def out(j, h):
            return slice(2 * D_LRU + j * D_RET + HD * h, 2 * D_LRU + j * D_RET + HD * (h + 1))

        b16 = lambda xs: [x.astype(MXU_DTYPE) for x in xs]
        qr = b16([_rope(q_ref[:, sl[h]], cs, sn) for h in hs])
        kf = [_rope(k_ref[:, sl[h]], cs, sn) * scale for h in hs]
        kr = b16(kf)
        kz = b16([kf[h] * zt_ref[h] for h in hs])
        v = b16([v_ref[:, sl[h]] for h in hs])
        s = b16([st_ref[h] for h in hs])
        ds = [ds_scr[h] for h in hs]
        dsb = b16(ds)
        sc = [_dot(qr[h], kr[h], NT) * dm_ref[h] for h in hs]
        scb = b16(sc)
        y = [_dot(scb[h], v[h], NN) + _dot(qr[h], s[h], NN) * xi_ref[h] for h in hs]
        yc = [y[h] - jnp.mean(y[h], axis=-1, keepdims=True) for h in hs]
        rstd = [lax.rsqrt(jnp.mean(yc[h] * yc[h], axis=-1, keepdims=True) + EPS) for h in hs]
        yn = [yc[h] * rstd[h] for h in hs]
        dy = []
        for h in hs:
            g = g_ref[:, sl[h]]
            gain = gain_ref[:, sl[h]]
            sg = jax.nn.sigmoid(g)
            silu = g * sg
            dout = do_ref[:, sl[h]].astype(F32)
            dgain_ref[:, sl[h]] += jnp.sum(dout * silu * yn[h], axis=0, keepdims=True)
            dp_ref[:, out(3, h)] = (dout * yn[h] * gain * (sg * (1.0 + g * (1.0 - sg)))).astype(dp_ref.dtype)
            dyn = dout * silu * gain
            dy.append(rstd[h] * (dyn - jnp.mean(dyn, axis=-1, keepdims=True)
                                 - yn[h] * jnp.mean(dyn * yn[h], axis=-1, keepdims=True)))
        dyb = b16(dy)
        dqs = b16([dy[h] * xi_ref[h] for h in hs])
        dp = b16([_dot(dyb[h], v[h], NT) * dm_ref[h] for h in hs])
        dv = [_dot(scb[h], dyb[h], TN) + _dot(kz[h], dsb[h], NN) for h in hs]
        dqr = [_dot(dp[h], kr[h], NN) + _dot(dqs[h], s[h], NT) for h in hs]
        dkr = [_dot(dp[h], qr[h], TN) + _dot(v[h], dsb[h], NT) * zt_ref[h] for h in hs]
        for h in hs:
            ds_scr[h] = gch[h] * ds[h] + _dot(qr[h], dqs[h], TN)
        for h in hs:
            dp_ref[:, out(0, h)] = (dqr[h] * cs + pltpu.roll(dqr[h] * sn, HD // 2, 1)).astype(dp_ref.dtype)
            dp_ref[:, out(1, h)] = ((dkr[h] * cs + pltpu.roll(dkr[h] * sn, HD // 2, 1)) * scale).astype(dp_ref.dtype)
            dp_ref[:, out(2, h)] = dv[h].astype(dp_ref.dtype)

    def col(c):
        return pl.BlockSpec((CH, D_RET), lambda n: (last - n, c))

    tab = pl.BlockSpec((CH, HD), lambda n: (last - n, 0))
    cst = _full((HEADS, CH, HD))
    return pl.pallas_call(
        body, name=name, grid=(NCH,),
        in_specs=[col(2), col(3), col(4), col(5), pl.BlockSpec((None, HEADS, HD, HD), lambda n: (last - n, 0, 0, 0)), col(1),
                  tab, tab, cst, cst, cst, _full((1, D_RET)), pl.BlockSpec((CH, 2 * D_LRU), lambda n: (last - n, 0))],
        out_specs=[pl.BlockSpec((CH, D_IN), lambda n: (last - n, 0)), _full((1, D_RET))],
        out_shape=[_sds((T, D_IN), MXU_DTYPE), _sds((1, D_RET), F32)],
        scratch_shapes=[pltpu.VMEM((HEADS, HD, HD), F32)],
        compiler_params=_params(("arbitrary",)),
    )(proj, proj, proj, proj, states, dymix, cos, sin, dmask, xi, zeta, gain, dxg)


HBM = pl.BlockSpec(memory_space=pltpu.HBM)


def _place():
    return lax.axis_index("x"), lax.axis_index("y"), lax.axis_index("c")


def all_gather(arrs, name):
    n = len(arrs)

    def body(*refs):
        ins, outs = refs[:n], refs[n:2 * n]
        send_sems, recv_sems, local_sems = refs[2 * n:]
        x, y, c = _place()
        me, sibling = (x, y, c), (x, y, 1 - c)
        chips = [(1 - x, y), (x, 1 - y), (1 - x, 1 - y)]

        def copy(a, k, block, to, src=None):
            px, py, pc = block
            dst = outs[a].at[4 * px + 2 * py + pc]
            return pltpu.make_async_remote_copy(
                src_ref=dst if src is None else src, dst_ref=dst, send_sem=send_sems.at[a, k], recv_sem=recv_sems.at[a, k],
                device_id=to, device_id_type=MESH)

        mine = [pltpu.make_async_copy(ins[a], outs[a].at[4 * x + 2 * y + c], local_sems.at[a]) for a in range(n)]
        for cp in mine:
            cp.start()
        first = []
        for a in range(n):
            first.append(copy(a, 0, me, sibling, src=ins[a]))
            first += [copy(a, 1 + j, me, (*chip, c), src=ins[a]) for j, chip in enumerate(chips)]
        for cp in first:
            cp.start()
        passed = []
        for j, chip in enumerate(chips):
            for a in range(n):
                copy(a, 1 + j, (*chip, c), me).wait_recv()
                passed.append(copy(a, 4 + j, (*chip, c), sibling))
                passed[-1].start()
        for a in range(n):
            copy(a, 0, sibling, me).wait_recv()
            for j, chip in enumerate(chips):
                copy(a, 4 + j, (*chip, 1 - c), me).wait_recv()
        for cp in first + passed:
            cp.wait_send()
        for cp in mine:
            cp.wait()

    return pl.pallas_call(
        body, name=name,
        in_specs=[HBM] * n, out_specs=[HBM] * n,
        out_shape=[_sds((NDEV,) + a.shape, a.dtype) for a in arrs],
        scratch_shapes=[pltpu.SemaphoreType.DMA((n, 7)), pltpu.SemaphoreType.DMA((n, 7)), pltpu.SemaphoreType.DMA((n,))],
    )(*arrs)


SEM = pl.BlockSpec(memory_space=pltpu.SEMAPHORE)
ANY = pl.BlockSpec(memory_space=pl.ANY)
EFFECT = pltpu.SideEffectType.DATAFLOW_SIDE_EFFECTING


def _hbm(a):
    return pltpu.with_memory_space_constraint(a, pltpu.HBM)


def _hbm_like(arrs):
    return [pltpu.HBM(a.shape, a.dtype) for a in arrs]


def _dma_sems(count):
    return [pltpu.SemaphoreType.DMA(())] * count


def _ag_copy(lands, send_sems, recv_sems, per):
    def copy(a, k, block, to, src=None):
        px, py, pc = block
        dst = lands[a].at[4 * px + 2 * py + pc]
        return pltpu.make_async_remote_copy(
            src_ref=dst if src is None else src, dst_ref=dst, send_sem=send_sems[a * per + k], recv_sem=recv_sems[a * per + k],
            device_id=to, device_id_type=MESH)
    return copy


def to_wire(sel, w_in, w_gate, w_up, w_out, w_down, name):
    ffpad = FF_SHP - FF_SH

    def body(sel_ref, i_ref, g_ref, u_ref, o_ref, d_ref, oi, og, ou, oo, od):
        del sel_ref
        oi[...] = i_ref[...].astype(oi.dtype)
        oo[...] = o_ref[...].astype(oo.dtype)
        for src, dst in ((g_ref, og), (u_ref, ou), (d_ref, od)):
            dst[:FF_SH, :] = src[...].astype(dst.dtype)
            dst[FF_SH:, :] = jnp.zeros((ffpad, D), dst.dtype)

    shapes_in = [(D, IN_SH), (FF_SH, D), (FF_SH, D), (OUT_SH, D), (FF_SH, D)]
    shapes_out = [(D, IN_SH), (FF_SHP, D), (FF_SHP, D), (OUT_SH, D), (FF_SHP, D)]
    return pl.pallas_call(
        body, name=name,
        grid_spec=pltpu.PrefetchScalarGridSpec(
            num_scalar_prefetch=1, grid=(1,),
            in_specs=[pl.BlockSpec((None,) + s, lambda i, sel_ref: (sel_ref[1], 0, 0)) for s in shapes_in],
            out_specs=[pl.BlockSpec((None,) + s, lambda i, sel_ref: (sel_ref[0], 0, 0)) for s in shapes_out]),
        out_shape=[_sds((NDEV,) + s, WIRE_DTYPE) for s in shapes_out], compiler_params=_params(("arbitrary",)),
    )(sel, w_in, w_gate, w_up, w_out, w_down)


def place_blocks(sel, arrs, name):
    n = len(arrs)

    def body(sel_ref, *refs):
        del sel_ref
        for a in range(n):
            refs[n + a][...] = refs[a][...]

    def whole(a):
        nd = a.ndim
        return pl.BlockSpec(a.shape, lambda i, sel_ref: (0,) * nd)

    def mine(a):
        nd = a.ndim
        return pl.BlockSpec((None,) + a.shape, lambda i, sel_ref: (sel_ref[0],) + (0,) * nd)

    return pl.pallas_call(
        body, name=name,
        grid_spec=pltpu.PrefetchScalarGridSpec(
            num_scalar_prefetch=1, grid=(1,), in_specs=[whole(a) for a in arrs], out_specs=[mine(a) for a in arrs]),
        out_shape=[_sds((NDEV,) + a.shape, a.dtype) for a in arrs], compiler_params=_params(("arbitrary",)),
    )(sel, *arrs)


def ag_start(lands, after, name):
    n = len(lands)
    ns = 4 * n

    def body(*refs):
        lnd = refs[:n]
        send_sems, recv_sems = refs[n + 1:n + 1 + ns], refs[n + 1 + ns:n + 1 + 2 * ns]
        token = refs[-1]
        x, y, c = _place()
        me, sibling = (x, y, c), (x, y, 1 - c)
        chips = [(1 - x, y), (x, 1 - y), (1 - x, 1 - y)]
        copy = _ag_copy(lnd, send_sems, recv_sems, 4)
        for a in range(n):
            copy(a, 0, me, sibling).start()
            for j, chip in enumerate(chips):
                copy(a, 1 + j, me, (*chip, c)).start()
        token[...] = jnp.zeros_like(token)

    outs = pl.pallas_call(
        body, name=name,
        in_specs=[HBM] * n + [ANY],
        out_specs=[SEM] * (2 * ns) + [HBM] * n + [pl.BlockSpec(memory_space=pltpu.VMEM)],
        out_shape=_dma_sems(2 * ns) + _hbm_like(lands) + [_sds((8, 128), F32)],
        input_output_aliases={i: 2 * ns + i for i in range(n)},
        compiler_params=pltpu.CompilerParams(has_side_effects=EFFECT),
    )(*[_hbm(a) for a in lands], after)
    return outs[:ns], outs[ns:2 * ns], outs[2 * ns:2 * ns + n], outs[-1]


def ag_forward(send_sems, recv_sems, lands, after, name):
    n = len(lands)
    n1, n2 = 4 * n, 3 * n

    def body(*refs):
        lnd = refs[:n]
        o = n
        s1, r1 = refs[o:o + n1], refs[o + n1:o + 2 * n1]
        o += 2 * n1 + 1
        s2, r2 = refs[o:o + n2], refs[o + n2:o + 2 * n2]
        token = refs[-1]
        token[...] = jnp.zeros_like(token)
        x, y, c = _place()
        me, sibling = (x, y, c), (x, y, 1 - c)
        chips = [(1 - x, y), (x, 1 - y), (1 - x, 1 - y)]
        copy1 = _ag_copy(lnd, s1, r1, 4)
        copy2 = _ag_copy(lnd, s2, r2, 3)
        for j, chip in enumerate(chips):
            for a in range(n):
                copy1(a, 1 + j, (*chip, c), me).wait_recv()
                copy2(a, j, (*chip, c), sibling).start()
        for a in range(n):
            copy1(a, 0, sibling, me).wait_recv()
            copy1(a, 0, me, sibling).wait_send()
            for j, chip in enumerate(chips):
                copy1(a, 1 + j, me, (*chip, c)).wait_send()

    outs = pl.pallas_call(
        body, name=name,
        in_specs=[HBM] * n + [SEM] * (2 * n1) + [ANY],
        out_specs=[SEM] * (2 * n2) + [HBM] * n + [pl.BlockSpec(memory_space=pltpu.VMEM)],
        out_shape=_dma_sems(2 * n2) + _hbm_like(lands) + [_sds((8, 128), F32)],
        input_output_aliases={i: 2 * n2 + i for i in range(n)},
        compiler_params=pltpu.CompilerParams(has_side_effects=EFFECT),
    )(*lands, *send_sems, *recv_sems, after)
    return outs[:n2], outs[n2:2 * n2], outs[2 * n2:2 * n2 + n], outs[-1]


def ag_finish(send_sems, recv_sems, lands, after, name):
    n = len(lands)
    n2 = 3 * n

    def body(*refs):
        lnd = refs[:n]
        s2, r2 = refs[n:n + n2], refs[n + n2:n + 2 * n2]
        x, y, c = _place()
        me, sibling = (x, y, c), (x, y, 1 - c)
        chips = [(1 - x, y), (x, 1 - y), (1 - x, 1 - y)]
        copy2 = _ag_copy(lnd, s2, r2, 3)
        for a in range(n):
            for j, chip in enumerate(chips):
                copy2(a, j, (*chip, c), sibling).wait_send()
                copy2(a, j, (*chip, 1 - c), me).wait_recv()

    outs = pl.pallas_call(
        body, name=name,
        in_specs=[HBM] * n + [SEM] * (2 * n2) + [ANY],
        out_specs=[HBM] * n, out_shape=_hbm_like(lands),
        input_output_aliases={i: i for i in range(n)},
        compiler_params=pltpu.CompilerParams(has_side_effects=EFFECT),
    )(*lands, *send_sems, *recv_sems, after)
    return list(outs)


def rs_sibling_start(arrs, name):
    n = len(arrs)
    ns = 4 * n
    lands = [lax.empty((4,) + a.shape[1:], a.dtype) for a in arrs]

    def body(*refs):
        ins, lnd = refs[:n], refs[n:2 * n]
        send_sems, recv_sems = refs[2 * n:2 * n + ns], refs[2 * n + ns:2 * n + 2 * ns]
        x, y, c = _place()
        sibling = (x, y, 1 - c)
        for a in range(n):
            for p in range(4):
                pltpu.make_async_remote_copy(
                    src_ref=ins[a].at[2 * p + 1 - c], dst_ref=lnd[a].at[p], send_sem=send_sems[4 * a + p],
                    recv_sem=recv_sems[4 * a + p], device_id=sibling, device_id_type=MESH).start()
        refs[-1][...] = jnp.zeros_like(refs[-1])

    outs = pl.pallas_call(
        body, name=name,
        in_specs=[HBM] * (2 * n), out_specs=[SEM] * (2 * ns) + [HBM] * (2 * n) + [pl.BlockSpec(memory_space=pltpu.VMEM)],
        out_shape=_dma_sems(2 * ns) + _hbm_like(arrs) + _hbm_like(lands) + [_sds((8, 128), F32)],
        input_output_aliases={i: 2 * ns + i for i in range(2 * n)},
        compiler_params=pltpu.CompilerParams(has_side_effects=EFFECT),
    )(*[_hbm(a) for a in arrs], *[_hbm(a) for a in lands])
    return (outs[:ns], outs[ns:2 * ns], outs[2 * ns:2 * ns + n], outs[2 * ns + n:2 * ns + 2 * n]), outs[-1]


def rs_sibling_wait(send_sems, recv_sems, arrs, lands, after, name):
    n = len(arrs)
    ns = 4 * n

    def body(*refs):
        ins, lnd = refs[:n], refs[n:2 * n]
        s, r = refs[2 * n:2 * n + ns], refs[2 * n + ns:2 * n + 2 * ns]
        x, y, c = _place()
        sibling = (x, y, 1 - c)
        for a in range(n):
            for p in range(4):
                cp = pltpu.make_async_remote_copy(
                    src_ref=ins[a].at[2 * p + 1 - c], dst_ref=lnd[a].at[p], send_sem=s[4 * a + p], recv_sem=r[4 * a + p],
                    device_id=sibling, device_id_type=MESH)
                cp.wait_send()
                cp.wait_recv()

    outs = pl.pallas_call(
        body, name=name,
        in_specs=[HBM] * (2 * n) + [SEM] * (2 * ns) + [ANY], out_specs=[HBM] * (2 * n),
        out_shape=_hbm_like(arrs) + _hbm_like(lands),
        input_output_aliases={i: i for i in range(2 * n)},
        compiler_params=pltpu.CompilerParams(has_side_effects=EFFECT),
    )(*arrs, *lands, *send_sems, *recv_sems, after)
    return outs[:n], outs[n:]


def rs_chips_start(parts, name):
    n = len(parts)
    ns = 3 * n
    lands = [lax.empty((3,) + a.shape[1:], a.dtype) for a in parts]

    def body(*refs):
        ins, lnd = refs[:n], refs[n:2 * n]
        send_sems, recv_sems = refs[2 * n:2 * n + ns], refs[2 * n + ns:2 * n + 2 * ns]
        x, y, c = _place()
        chips = [(1 - x, y), (x, 1 - y), (1 - x, 1 - y)]
        for a in range(n):
            for k, (tx, ty) in enumerate(chips):
                pltpu.make_async_remote_copy(
                    src_ref=ins[a].at[2 * tx + ty], dst_ref=lnd[a].at[k], send_sem=send_sems[3 * a + k],
                    recv_sem=recv_sems[3 * a + k], device_id=(tx, ty, c), device_id_type=MESH).start()
        refs[-1][...] = jnp.zeros_like(refs[-1])

    outs = pl.pallas_call(
        body, name=name,
        in_specs=[HBM] * (2 * n), out_specs=[SEM] * (2 * ns) + [HBM] * (2 * n) + [pl.BlockSpec(memory_space=pltpu.VMEM)],
        out_shape=_dma_sems(2 * ns) + _hbm_like(parts) + _hbm_like(lands) + [_sds((8, 128), F32)],
        input_output_aliases={i: 2 * ns + i for i in range(2 * n)},
        compiler_params=pltpu.CompilerParams(has_side_effects=EFFECT),
    )(*[_hbm(a) for a in parts], *[_hbm(a) for a in lands])
    return (outs[:ns], outs[ns:2 * ns], outs[2 * ns:2 * ns + n], outs[2 * ns + n:2 * ns + 2 * n]), outs[-1]


def rs_chips_wait(send_sems, recv_sems, parts, lands, after, name):
    n = len(parts)
    ns = 3 * n

    def body(*refs):
        ins, lnd = refs[:n], refs[n:2 * n]
        s, r = refs[2 * n:2 * n + ns], refs[2 * n + ns:2 * n + 2 * ns]
        x, y, c = _place()
        chips = [(1 - x, y), (x, 1 - y), (1 - x, 1 - y)]
        for a in range(n):
            for k, (tx, ty) in enumerate(chips):
                cp = pltpu.make_async_remote_copy(
                    src_ref=ins[a].at[2 * tx + ty], dst_ref=lnd[a].at[k], send_sem=s[3 * a + k], recv_sem=r[3 * a + k],
                    device_id=(tx, ty, c), device_id_type=MESH)
                cp.wait_send()
                cp.wait_recv()

    outs = pl.pallas_call(
        body, name=name,
        in_specs=[HBM] * (2 * n) + [SEM] * (2 * ns) + [ANY], out_specs=[HBM] * (2 * n),
        out_shape=_hbm_like(parts) + _hbm_like(lands),
        input_output_aliases={i: i for i in range(2 * n)},
        compiler_params=pltpu.CompilerParams(has_side_effects=EFFECT),
    )(*parts, *lands, *send_sems, *recv_sems, after)
    return outs[:n], outs[n:]


def pair_sum(arrs, recv, c, name):
    n = len(arrs)

    def body(c_ref, *refs):
        del c_ref
        for a in range(n):
            refs[2 * n + a][...] = (refs[a][...].astype(F32) + refs[n + a][...].astype(F32)).astype(refs[2 * n + a].dtype)

    mine = [pl.BlockSpec((None,) + a.shape[1:], lambda p, c_ref: (2 * p + c_ref[0], 0, 0)) for a in arrs]
    other = [pl.BlockSpec((None,) + a.shape[1:], lambda p, c_ref: (p, 0, 0)) for a in arrs]
    return pl.pallas_call(
        body, name=name,
        grid_spec=pltpu.PrefetchScalarGridSpec(num_scalar_prefetch=1, grid=(4,), in_specs=mine + other, out_specs=other),
        out_shape=[_sds((4,) + a.shape[1:], a.dtype) for a in arrs], compiler_params=_params(("parallel",)),
    )(c, *arrs, *recv)


def _adamw(w, g, m, v):
    m = ADAM_B1 * m + (1.0 - ADAM_B1) * g
    v = ADAM_B2 * v + (1.0 - ADAM_B2) * jnp.square(g)
    m_hat = m / (1.0 - ADAM_B1 ** ADAM_STEP)
    v_hat = v / (1.0 - ADAM_B2 ** ADAM_STEP)
    return -ADAM_LR * (m_hat / (jnp.sqrt(v_hat) + ADAM_EPS) + ADAM_WD * w), m, v


def adamw_big(recv, sums, chip, w, m, v, tr, name):
    nl, rr, cc = w.shape
    cp = recv[0].shape[2]

    def body(chip_ref, *refs):
        del chip_ref
        rcv, own = refs[:nl], refs[nl:2 * nl]
        w_ref, m_ref, v_ref, g_out, d_out, m_out, v_out = refs[2 * nl:]
        for l in range(nl):
            g = ((own[l][...].astype(F32) + rcv[l][0].astype(F32)) + rcv[l][1].astype(F32)) + rcv[l][2].astype(F32)
            g = g[:, :cc]
            g_out[l] = g
            d_out[l], m_out[l], v_out[l] = _adamw(w_ref[l], g, m_ref[l], v_ref[l])

    blk = pl.BlockSpec((nl, tr, cc), lambda i, chip_ref: (0, i, 0))
    return pl.pallas_call(
        body, name=name,
        grid_spec=pltpu.PrefetchScalarGridSpec(
            num_scalar_prefetch=1, grid=(rr // tr,),
            in_specs=[pl.BlockSpec((3, tr, cp), lambda i, chip_ref: (0, i, 0))] * nl
            + [pl.BlockSpec((None, tr, cp), lambda i, chip_ref: (chip_ref[0], i, 0))] * nl + [blk, blk, blk],
            out_specs=[blk] * 4),
        out_shape=[_sds(w.shape, F32)] * 4, compiler_params=_params(("parallel",)),
    )(chip, *recv, *sums, w, m, v)


SMALL_ROWS = 24


def small_grads(lvec, g_ret, g_mix, g_ffn, g_final, loss_part, dwa, dwx, name):
    def body(lvec_ref, ret_ref, mix_ref, ffn_ref, fin_ref, loss_ref, dwa_ref, dwx_ref, v_ref, g_ref):
        v_ref[16:SMALL_ROWS, :] = jnp.zeros((SMALL_ROWS - 16, D_LRU), F32)
        v_ref[16:17, 0:128] = loss_ref[0:1, :]
        v_ref[0:9, :] = lvec_ref[0:9, :]
        v_ref[9:10, :] = ret_ref[...]
        for r, src in ((10, mix_ref), (12, ffn_ref), (14, fin_ref)):
            v_ref[r:r + 1, :] = src[:, :D_LRU]
            v_ref[r + 1:r + 2, :] = src[:, D_LRU:]
        for k, src in enumerate((dwa_ref, dwx_ref)):
            for g in range(LRU_BLOCKS):
                rows = slice(LRU_BD * g, LRU_BD * (g + 1))
                g_ref[D_LRU * k + LRU_BD * g:D_LRU * k + LRU_BD * (g + 1), :] = src[rows, rows]

    ins = [lvec, g_ret, g_mix, g_ffn, g_final, loss_part, dwa, dwx]
    return pl.pallas_call(
        body, name=name, grid=(1,), in_specs=[_full(a.shape) for a in ins],
        out_specs=[_full((SMALL_ROWS, D_LRU)), _full((2 * D_LRU, LRU_BD))],
        out_shape=[_sds((SMALL_ROWS, D_LRU), F32), _sds((2 * D_LRU, LRU_BD), F32)], compiler_params=_params(("arbitrary",)),
    )(*ins)


def sum_devices(arrs, name):
    n = len(arrs)

    def body(*refs):
        for a in range(n):
            acc = refs[a][0]
            for j in range(1, NDEV):
                acc = acc + refs[a][j]
            refs[n + a][...] = acc

    return pl.pallas_call(
        body, name=name, grid=(1,), in_specs=[_full(a.shape) for a in arrs], out_specs=[_full(a.shape[1:]) for a in arrs],
        out_shape=[_sds(a.shape[1:], F32) for a in arrs], compiler_params=_params(("arbitrary",)),
    )(*arrs)


def adamw_small(gs, ws, ms, vs, name):
    n = len(gs)

    def body(*refs):
        for a in range(n):
            g, w, m, v = (refs[k * n + a][...] for k in range(4))
            refs[4 * n + a][...], refs[5 * n + a][...], refs[6 * n + a][...] = _adamw(w, g, m, v)

    specs = [_full(a.shape) for a in ws]
    outs = pl.pallas_call(
        body, name=name, grid=(1,), in_specs=specs * 4, out_specs=specs * 3, out_shape=[_sds(a.shape, F32) for a in ws] * 3,
        compiler_params=_params(("arbitrary",)),
    )(*gs, *ws, *ms, *vs)
    return outs[:n], outs[n:2 * n], outs[2 * n:]


def block_diag(wa, wx, name):
    def body(wa_ref, wx_ref, oa_ref, ox_ref):
        for src, dst in ((wa_ref, oa_ref), (wx_ref, ox_ref)):
            dst[...] = jnp.zeros_like(dst)
            for g in range(LRU_BLOCKS):
                rows = slice(LRU_BD * g, LRU_BD * (g + 1))
                dst[rows, rows] = src[g].astype(dst.dtype)

    ispec = pl.BlockSpec((None, LRU_BLOCKS, LRU_BD, LRU_BD), lambda l: (l, 0, 0, 0))
    ospec = pl.BlockSpec((None, D_LRU, D_LRU), lambda l: (l, 0, 0))
    return pl.pallas_call(
        body, name=name, grid=(wa.shape[0],), in_specs=[ispec, ispec], out_specs=[ospec, ospec],
        out_shape=[_sds((wa.shape[0], D_LRU, D_LRU), MXU_DTYPE)] * 2, compiler_params=_params(("parallel",)),
    )(wa, wx)


REP_NAMES = ["norm_mix", "conv_b", "gate_a_w", "gate_a_b", "gate_x_w", "gate_x_b", "lru_lambda", "lru_out_norm",
             "ret_out_norm", "norm_ffn", "norm_final"]


def kernel(x, meta_tokens, norm_mix, w_in, conv_w, conv_b, gate_a_w, gate_a_b, gate_x_w, gate_x_b, lru_lambda, lru_out_norm, ret_out_norm, w_out, norm_ffn, w_gate, w_up, w_down, norm_final, loss_target, m_meta_tokens, m_norm_mix, m_w_in, m_conv_w, m_conv_b, m_gate_a_w, m_gate_a_b, m_gate_x_w, m_gate_x_b, m_lru_lambda, m_lru_out_norm, m_ret_out_norm, m_w_out, m_norm_ffn, m_w_gate, m_w_up, m_w_down, m_norm_final, v_meta_tokens, v_norm_mix, v_w_in, v_conv_w, v_conv_b, v_gate_a_w, v_gate_a_b, v_gate_x_w, v_gate_x_b, v_lru_lambda, v_lru_out_norm, v_ret_out_norm, v_w_out, v_norm_ffn, v_w_gate, v_w_up, v_w_down, v_norm_final):
    xi, yi, ci = _place()
    dev = 4 * xi + 2 * yi + ci
    c_arr = jnp.reshape(ci, (1,)).astype(jnp.int32)
    dev_arr = jnp.reshape(dev, (1,)).astype(jnp.int32)

    meta_g, conv_g = all_gather([meta_tokens, conv_w], "ag_small")
    meta_full = jnp.transpose(meta_g, (1, 0, 2)).reshape(N_META, D)
    conv_full = jnp.transpose(conv_g, (1, 2, 0, 3)).reshape(DEPTH, CONV_W, D_LRU)
    tr_ = lambda a: jnp.transpose(a, (0, 2, 1))
    w_gate_t, m_w_gate_t, v_w_gate_t = tr_(w_gate), tr_(m_w_gate), tr_(v_w_gate)
    w_up_t, m_w_up_t, v_w_up_t = tr_(w_up), tr_(m_w_up), tr_(v_w_up)
    level1 = []
    token = meta_g
    for l in range(DEPTH):
        sel = jnp.stack([dev, jnp.int32(l)]).astype(jnp.int32)
        lands = to_wire(sel, w_in, w_gate_t, w_up_t, w_out, w_down, "to_wire")
        s1, r1, lands, token = ag_start(lands, token, f"ag_start_{l}")
        level1.append((s1, r1, lands))

    def as_weights(gi, gg, gu, go, gd):
        return dict(w_in=gi, w_gate=gg.reshape(D_FFP, D), w_up=gu.reshape(D_FFP, D), w_out=go.reshape(D, D),
                    w_down=gd.reshape(D_FFP, D))

    tables = _ret_tables()
    row = lambda a: a.reshape(1, -1)

    h = jnp.concatenate([jnp.zeros((PAD, D), F32), meta_full, x[0]], axis=0)
    saved, gathered = [], []
    s1, r1, lands = level1[0]
    s2, r2, first, order = ag_forward(s1[:4], r1[:4], lands[:1], token, "ag_forward_0_w_in")
    w_in_next = ag_finish(s2, r2, first, h, "ag_finish_0_w_in")[0]
    wa_dense, wx_dense = block_diag(gate_a_w, gate_x_w, "block_diag")
    for l in range(DEPTH):
        small = dict(cw=conv_full[l], cb=row(conv_b[l]), wa=wa_dense[l], ba=row(gate_a_b[l]),
                     wx=wx_dense[l], bx=row(gate_x_b[l]), lam=row(lru_lambda[l]),
                     gain=row(lru_out_norm[l]))
        s1, r1, lands = level1[l]
        hn1 = rmsnorm_fwd(h, row(norm_mix[l]), "rms_fwd")
        proj = mm_blocked_nn(hn1, w_in_next, F32, "proj")
        ylru, hst = lru_fwd(proj, name="lru_fwd", **small)
        s2, r2, rest, order = ag_forward(s1[4:], r1[4:], lands[1:], ylru, f"ag_forward_{l}_rest")
        ymix, states = ret_fwd(proj, ylru, tables, row(ret_out_norm[l]), order, "ret_fwd")
        w = as_weights(w_in_next, *ag_finish(s2, r2, rest, ymix, f"ag_finish_{l}_rest"))
        gathered.append(w)
        h_mid = mm_nn_res(ymix, w["w_out"], h, order, "out_proj")
        hn2 = rmsnorm_fwd(h_mid, row(norm_ffn[l]), "rms_fwd")
        act_dgate, act_dup, act = ffn_up(hn2, w["w_gate"], w["w_up"], "ffn_up")
        if l + 1 < DEPTH:
            s1n, r1n, landsn = level1[l + 1]
            s2, r2, first, order = ag_forward(s1n[:4], r1n[:4], landsn[:1], act, f"ag_forward_{l + 1}_w_in")
        h_out = mm_nn_res(act, w["w_down"], h_mid, order, "ffn_down")
        if l + 1 < DEPTH:
            w_in_next = ag_finish(s2, r2, first, h_out, f"ag_finish_{l + 1}_w_in")[0]
        saved.append(dict(h=h, hn1=hn1, proj=proj, hst=hst, states=states, ymix=ymix, h_mid=h_mid, hn2=hn2, act_dgate=act_dgate, act_dup=act_dup,
                          act=act, small=small))
        h = h_out

    loss_p, dh, dh_b, g_norm_final = loss_head(h, row(norm_final), loss_target[0], "loss_head")

    small_v = [None] * DEPTH
    small_w = [None] * DEPTH
    inflight = []
    sib = None
    order = loss_p

    def sibling_done(l, tag, names, sib, after):
        parts, got = rs_sibling_wait(*sib, after, f"rs_sibling_wait_{tag}")
        sums = pair_sum(parts, got, c_arr, "pair_sum")
        flying, started = rs_chips_start(sums, f"rs_chips_start_{tag}")
        inflight.append((l, tag, names, flying))
        return started

    for l in reversed(range(DEPTH)):
        w, s = gathered[l], saved[l]
        dgate, dup = ffn_down_bwd(dh_b, w["w_down"], s["act_dgate"], s["act_dup"], order, "ffn_down_bwd")
        dwd = mm_tn(s["act"], dh_b, PAIR, order, "dw_down").reshape(NDEV, FF_SHP, D)
        dwg, dwu = (g.reshape(NDEV, FF_SHP, D) for g in mm_tn_two(dgate, dup, s["hn2"], PAIR, order, "dw_rows"))
        split = l <= 1
        if split:
            ffn_sib, order = rs_sibling_start([dwg, dwu, dwd], f"rs_sibling_start_{l}_ffn")
        dhn2 = mm_rows_nn([(dgate, w["w_gate"]), (dup, w["w_up"])], order, "ffn_up_bwd")
        if sib is not None:
            order = sibling_done(l + 1, sib_tag, sib_names, sib, dhn2)
        dh_mid, dh_mid_b, g_norm_ffn = rmsnorm_bwd(s["h_mid"], row(norm_ffn[l]), dhn2, dh, "rms_bwd")
        dymix, dwo = out_proj_bwd(dh_mid_b, w["w_out"], s["ymix"], order, "out_proj_bwd")
        dwo = dwo.reshape(NDEV, OUT_SH, D)
        if split:
            order = sibling_done(l, f"{l}_ffn", ("w_gate", "w_up", "w_down"), ffn_sib, dymix)
        dxg, lvec, dwa, dwx = lru_bwd(s["proj"], s["hst"], dymix, after=order, name="lru_bwd", **s["small"])
        dproj, g_ret_norm = ret_bwd(s["proj"], s["states"], dymix, dxg, tables, row(ret_out_norm[l]), "ret_bwd")
        dwi = mm_tn_blocked(s["hn1"], dproj, "dw_blocked")
        dhn1 = mm_blocked_nt([(dproj, w["w_in"])], order, "proj_bwd")
        dh, dh_b, g_norm_mix = rmsnorm_bwd(s["h"], row(norm_mix[l]), dhn1, dh_mid, "rms_bwd")

        g_fin, loss_part = (g_norm_final, loss_p) if l == 0 else (jnp.zeros((1, D), F32), jnp.zeros((8, 128), F32))
        small_v[l], small_w[l] = small_grads(lvec, g_ret_norm, g_norm_mix, g_norm_ffn, g_fin, loss_part, dwa, dwx,
                                             "small_grads")
        if split:
            sib_tag, sib_names = f"{l}_mix", ("w_in", "w_out")
            sib, order = rs_sibling_start([dwi, dwo], f"rs_sibling_start_{l}_mix")
        else:
            sib_tag, sib_names = str(l), ("w_in", "w_gate", "w_up", "w_out", "w_down")
            sib, order = rs_sibling_start([dwi, dwg, dwu, dwo, dwd], f"rs_sibling_start_{l}")
        if l == 1:
            early = place_blocks(dev_arr, [jnp.stack(small_v[1:]), jnp.stack(small_w[1:])], "place_grads")
            early_sems = ag_start(early, order, "ag_start_grads")
            order = early_sems[3]

    grad_x = dh[X0:][None]
    g_meta = dh[PAD:X0]

    late = all_gather([small_v[0], small_w[0], g_meta], "ag_grads")
    s2, r2, lands, _ = ag_forward(early_sems[0], early_sems[1], early_sems[2], dh, "ag_forward_grads")
    gath_early = ag_finish(s2, r2, lands, late[0], "ag_finish_grads")
    sibling_done(0, sib_tag, sib_names, sib, late[0])
    v0, w0, meta_sum, v123, w123 = sum_devices(list(late) + list(gath_early), "sum_devices")
    loss = v0[16, 0]
    vecs = jnp.concatenate([v0[None], v123])
    gws = jnp.concatenate([w0[None], w123])
    blocks = (DEPTH, LRU_BLOCKS, LRU_BD)
    small_g = dict(
        conv_w=lax.dynamic_slice_in_dim(vecs[:, 0:CONV_W], dev * (D_LRU // NDEV), D_LRU // NDEV, axis=2),
        conv_b=vecs[:, 4], gate_a_b=vecs[:, 5].reshape(blocks), gate_x_b=vecs[:, 6].reshape(blocks),
        lru_lambda=vecs[:, 7], lru_out_norm=vecs[:, 8], ret_out_norm=vecs[:, 9],
        norm_mix=vecs[:, 10:12].reshape(DEPTH, D), norm_ffn=vecs[:, 12:14].reshape(DEPTH, D),
        norm_final=v0[14:16].reshape(1, D),
        gate_a_w=gws[:, :D_LRU].reshape(blocks + (LRU_BD,)), gate_x_w=gws[:, D_LRU:].reshape(blocks + (LRU_BD,)),
        meta_tokens=lax.dynamic_slice_in_dim(meta_sum, dev * (D // NDEV), D // NDEV, axis=1))
    given = dict(norm_mix=(norm_mix, m_norm_mix, v_norm_mix), conv_b=(conv_b, m_conv_b, v_conv_b),
                 gate_a_w=(gate_a_w, m_gate_a_w, v_gate_a_w), gate_a_b=(gate_a_b, m_gate_a_b, v_gate_a_b),
                 gate_x_w=(gate_x_w, m_gate_x_w, v_gate_x_w), gate_x_b=(gate_x_b, m_gate_x_b, v_gate_x_b),
                 lru_lambda=(lru_lambda, m_lru_lambda, v_lru_lambda), lru_out_norm=(lru_out_norm, m_lru_out_norm, v_lru_out_norm),
                 ret_out_norm=(ret_out_norm, m_ret_out_norm, v_ret_out_norm), norm_ffn=(norm_ffn, m_norm_ffn, v_norm_ffn),
                 norm_final=tuple(a.reshape(1, D) for a in (norm_final, m_norm_final, v_norm_final)),
                 conv_w=(conv_w, m_conv_w, v_conv_w), meta_tokens=(meta_tokens, m_meta_tokens, v_meta_tokens))
    small_names = REP_NAMES + ["conv_w", "meta_tokens"]
    upd = adamw_small([small_g[n] for n in small_names], *[[given[n][k] for n in small_names] for k in range(3)],
                      "adamw_small")
    small_out = [dict(zip(small_names, u)) for u in upd]
    for d_ in [small_g] + small_out:
        d_["norm_final"] = d_["norm_final"].reshape(D)

    arrived = {}

    def wait_for(entries, after):
        for l, tag, names, flying in entries:
            sums, recv = rs_chips_wait(*flying, after, f"rs_chips_wait_{tag}")
            for i, n in enumerate(names):
                arrived[l, n] = (recv[i], sums[i])

    chip = jnp.reshape(2 * xi + yi, (1,)).astype(jnp.int32)

    def finish(wname, w_, m_, v_, tr):
        return adamw_big([arrived[l, wname][0] for l in range(DEPTH)], [arrived[l, wname][1] for l in range(DEPTH)], chip,
                         w_, m_, v_, tr, "adamw_" + wname)

    wait_for(inflight[:-1], upd[0][0])
    o_gate = [tr_(o) for o in finish("w_gate", w_gate_t, m_w_gate_t, v_w_gate_t, 32)]
    o_up = [tr_(o) for o in finish("w_up", w_up_t, m_w_up_t, v_w_up_t, 32)]
    o_down = finish("w_down", w_down, m_w_down, v_w_down, 32)
    wait_for(inflight[-1:], o_down[0])
    o_in = finish("w_in", w_in, m_w_in, v_w_in, 256)
    o_out = finish("w_out", w_out, m_w_out, v_w_out, 64)

    bigs = dict(w_in=o_in, w_out=o_out, w_gate=o_gate, w_up=o_up, w_down=o_down)
    order = ["meta_tokens", "norm_mix", "w_in", "conv_w", "conv_b", "gate_a_w", "gate_a_b", "gate_x_w", "gate_x_b", "lru_lambda",
             "lru_out_norm", "ret_out_norm", "w_out", "norm_ffn", "w_gate", "w_up", "w_down", "norm_final"]
    grads = [bigs[n][0] if n in bigs else small_g[n] for n in order]
    rest = [[bigs[n][k + 1] if n in bigs else small_out[k][n] for n in order] for k in range(3)]
    return (loss, grad_x, *grads, *rest[0], *rest[1], *rest[2])
```

```python
import numpy as np
import jax
import jax.numpy as jnp
from jax import lax
from jax.experimental import pallas as pl
from jax.experimental.pallas import tpu as pltpu

F32, BF16 = jnp.float32, jnp.bfloat16
MXU_DTYPE = BF16
WIRE_DTYPE = BF16

D = 1024
SEQ = 2048
DEPTH = 4
N_META = 16
CH = 128
PAD = (-(SEQ + N_META)) % CH
T = SEQ + N_META + PAD
NCH = T // CH
X0 = PAD + N_META
D_LRU = 512
LRU_BLOCKS = 8
LRU_BD = 64
CONV_W = 4
LRU_C = 8.0
D_RET = 512
HEADS = 4
HD = 128
ROPE_BASE = 10000.0
D_IN = 3072
D_FF = 2816
NDEV = 8
IN_SH = D_IN // NDEV
FF_SH = D_FF // NDEV
FF_SHP = 384
D_FFP = NDEV * FF_SHP
OUT_SH = D // NDEV
EPS = 1e-6
TM = 544
VMEM_LIMIT = 56 * 2**20
MESH = pl.DeviceIdType.MESH

ADAM_LR, ADAM_B1, ADAM_B2, ADAM_EPS, ADAM_WD, ADAM_STEP = 0.001, 0.9, 0.999, 1e-08, 0.01, 10

NN = ((1,), (0,))
NT = ((1,), (1,))
TN = ((0,), (0,))


def _dot(a, b, dims):
    return lax.dot_general(a.astype(MXU_DTYPE), b.astype(MXU_DTYPE), (dims, ((), ())), preferred_element_type=F32)


def _sds(shape, dtype):
    return jax.ShapeDtypeStruct(shape, dtype)


def _params(sem=None):
    return pltpu.CompilerParams(dimension_semantics=sem, vmem_limit_bytes=VMEM_LIMIT)


def _full(shape):
    n = len(shape)
    return pl.BlockSpec(shape, lambda *_: (0,) * n)


def rmsnorm_fwd(h, gain, name):
    def body(h_ref, g_ref, o_ref):
        x = h_ref[...]
        ms = jnp.mean(x * x, axis=-1, keepdims=True)
        o_ref[...] = (x * lax.rsqrt(ms + EPS) * g_ref[...]).astype(o_ref.dtype)

    return pl.pallas_call(
        body, name=name, grid=(T // TM,),
        in_specs=[pl.BlockSpec((TM, D), lambda i: (i, 0)), _full((1, D))],
        out_specs=pl.BlockSpec((TM, D), lambda i: (i, 0)),
        out_shape=_sds((T, D), MXU_DTYPE), compiler_params=_params(("parallel",)),
    )(h, gain)


def rmsnorm_bwd(h, gain, dhn, dres, name):
    def body(h_ref, g_ref, dhn_ref, dres_ref, dh_ref, dhb_ref, dg_ref):
        x = h_ref[...]
        rstd = lax.rsqrt(jnp.mean(x * x, axis=-1, keepdims=True) + EPS)
        xhat = x * rstd
        dy = dhn_ref[...]
        dyg = dy * g_ref[...]
        dh = dres_ref[...] + rstd * (dyg - xhat * jnp.mean(dyg * xhat, axis=-1, keepdims=True))
        dh_ref[...] = dh
        dhb_ref[...] = dh.astype(dhb_ref.dtype)

        @pl.when(pl.program_id(0) == 0)
        def _():
            dg_ref[...] = jnp.zeros_like(dg_ref)
        dg_ref[...] += jnp.sum(dy * xhat, axis=0, keepdims=True)

    row = pl.BlockSpec((TM, D), lambda i: (i, 0))
    return pl.pallas_call(
        body, name=name, grid=(T // TM,),
        in_specs=[row, _full((1, D)), row, row],
        out_specs=[row, row, _full((1, D))],
        out_shape=[_sds((T, D), F32), _sds((T, D), MXU_DTYPE), _sds((1, D), F32)], compiler_params=_params(("arbitrary",)),
    )(h, gain, dhn, dres)


def loss_head(h, gain, target, name):
    def body(h_ref, g_ref, t_ref, loss_ref, dh_ref, dhb_ref, dg_ref):
        i = pl.program_id(0)

        @pl.when(i == 0)
        def _():
            loss_ref[...] = jnp.zeros_like(loss_ref)
            dg_ref[...] = jnp.zeros_like(dg_ref)
            dh_ref[...] = jnp.zeros_like(dh_ref)
            dhb_ref[...] = jnp.zeros_like(dhb_ref)

        @pl.when(i > 0)
        def _():
            x = h_ref[...]
            g = g_ref[...]
            rstd = lax.rsqrt(jnp.mean(x * x, axis=-1, keepdims=True) + EPS)
            xhat = x * rstd
            err = xhat * g - t_ref[...]
            loss_ref[...] += 0.5 * jnp.sum(jnp.mean(err * err, axis=-1, keepdims=True), axis=0, keepdims=True)
            dy = err * (1.0 / D)
            dyg = dy * g
            dh = rstd * (dyg - xhat * jnp.mean(dyg * xhat, axis=-1, keepdims=True))
            dh_ref[...] = dh
            dhb_ref[...] = dh.astype(dhb_ref.dtype)
            dg_ref[...] += jnp.sum(dy * xhat, axis=0, keepdims=True)

    row = pl.BlockSpec((CH, D), lambda i: (i, 0))
    return pl.pallas_call(
        body, name=name, grid=(NCH,),
        in_specs=[row, _full((1, D)), pl.BlockSpec((CH, D), lambda i: (jnp.maximum(i - 1, 0), 0))],
        out_specs=[_full((8, 128)), row, row, _full((1, D))],
        out_shape=[_sds((8, 128), F32), _sds((T, D), F32), _sds((T, D), MXU_DTYPE), _sds((1, D), F32)],
        compiler_params=_params(("arbitrary",)),
    )(h, gain, target)


PAIR = 2 * IN_SH
NPAIR = NDEV // 2
BN = 256
FB = 512


def _pair_cols(w_ref):
    return jnp.concatenate([w_ref[0], w_ref[1]], axis=1)


W_PAIR = lambda k: pl.BlockSpec((2, k, IN_SH), lambda j: (j, 0, 0))
COLS_PAIR = pl.BlockSpec((T, PAIR), lambda j: (0, j))
ANYSPEC = pl.BlockSpec(memory_space=pl.ANY)


def mm_blocked_nn(a, w, out_dtype, name):
    k = a.shape[1]

    def body(a_ref, w_ref, o_ref):
        o_ref[...] = _dot(a_ref[...], _pair_cols(w_ref), NN).astype(o_ref.dtype)

    return pl.pallas_call(
        body, name=name, grid=(NPAIR,),
        in_specs=[_full((T, k)), W_PAIR(k)], out_specs=COLS_PAIR,
        out_shape=_sds((T, NDEV * IN_SH), out_dtype), compiler_params=_params(("parallel",)),
    )(a, w)


def mm_nn_res(a, w, res, after, name):
    k = a.shape[1]

    def body(a_ref, w_ref, r_ref, after_ref, o_ref):
        del after_ref
        o_ref[...] = r_ref[...] + _dot(a_ref[...], w_ref[...], NN)

    col = pl.BlockSpec((T, BN), lambda j: (0, j))
    return pl.pallas_call(
        body, name=name, grid=(D // BN,),
        in_specs=[_full((T, k)), pl.BlockSpec((k, BN), lambda j: (0, j)), col, ANYSPEC], out_specs=col,
        out_shape=_sds((T, D), F32), compiler_params=_params(("parallel",)),
    )(a, w, res, after)


def ffn_up(hn, wg, wu, name):
    def body(a_ref, wg_ref, wu_ref, dg_ref, du_ref, act_ref):
        a = a_ref[...]
        for c in range(FB // BN):
            cols = slice(BN * c, BN * (c + 1))
            g = _dot(a, wg_ref[cols, :], NT)
            u = _dot(a, wu_ref[cols, :], NT)
            sg = jax.nn.sigmoid(g)
            silu = g * sg
            dg_ref[:, cols] = (u * (sg * (1.0 + g * (1.0 - sg)))).astype(dg_ref.dtype)
            du_ref[:, cols] = silu.astype(du_ref.dtype)
            act_ref[:, cols] = (silu * u).astype(act_ref.dtype)

    wspec = pl.BlockSpec((FB, D), lambda j: (j, 0))
    ospec = pl.BlockSpec((T, FB), lambda j: (0, j))
    return pl.pallas_call(
        body, name=name, grid=(D_FFP // FB,),
        in_specs=[_full((T, D)), wspec, wspec], out_specs=[ospec] * 3,
        out_shape=[_sds((T, D_FFP), MXU_DTYPE)] * 3, compiler_params=_params(("parallel",)),
    )(hn, wg, wu)


def ffn_down_bwd(dh, wd, dact_dgate, dact_dup, after, name):
    def body(dh_ref, wd_ref, g_ref, u_ref, after_ref, dg_ref, du_ref):
        del after_ref
        dh = dh_ref[...]
        for c in range(FB // BN):
            cols = slice(BN * c, BN * (c + 1))
            dact = _dot(dh, wd_ref[cols, :], NT)
            dg_ref[:, cols] = (dact * g_ref[:, cols].astype(F32)).astype(dg_ref.dtype)
            du_ref[:, cols] = (dact * u_ref[:, cols].astype(F32)).astype(du_ref.dtype)

    blk = pl.BlockSpec((T, FB), lambda j: (0, j))
    return pl.pallas_call(
        body, name=name, grid=(D_FFP // FB,),
        in_specs=[_full((T, D)), pl.BlockSpec((FB, D), lambda j: (j, 0)), blk, blk, ANYSPEC],
        out_specs=[blk, blk],
        out_shape=[_sds((T, D_FFP), MXU_DTYPE)] * 2, compiler_params=_params(("parallel",)),
    )(dh, wd, dact_dgate, dact_dup, after)


def mm_blocked_nt(pairs, after, name):
    n = len(pairs)

    def body(*refs):
        o_ref = refs[2 * n + 1]

        @pl.when(pl.program_id(0) == 0)
        def _():
            o_ref[...] = jnp.zeros_like(o_ref)
        for p in range(n):
            o_ref[...] += _dot(refs[2 * p][...], _pair_cols(refs[2 * p + 1]), NT)

    specs, args = [], []
    for a, w in pairs:
        specs += [COLS_PAIR, W_PAIR(D)]
        args += [a, w]
    return pl.pallas_call(
        body, name=name, grid=(NPAIR,), in_specs=specs + [ANYSPEC], out_specs=_full((T, D)),
        out_shape=_sds((T, D), F32), compiler_params=_params(("arbitrary",)),
    )(*args, after)


def mm_tn_two(a1, a2, b, bm, after, name):
    m = a1.shape[1]

    def body(a1_ref, a2_ref, b_ref, after_ref, o1_ref, o2_ref):
        del after_ref
        b = b_ref[...]
        o1_ref[...] = _dot(a1_ref[...], b, TN).astype(o1_ref.dtype)
        o2_ref[...] = _dot(a2_ref[...], b, TN).astype(o2_ref.dtype)

    blk = pl.BlockSpec((T, bm), lambda i: (0, i))
    out = pl.BlockSpec((bm, D), lambda i: (i, 0))
    return pl.pallas_call(
        body, name=name, grid=(m // bm,),
        in_specs=[blk, blk, _full((T, D)), ANYSPEC], out_specs=[out, out],
        out_shape=[_sds((m, D), WIRE_DTYPE)] * 2, compiler_params=_params(("parallel",)),
    )(a1, a2, b, after)


def out_proj_bwd(dh, w, ymix, after, name):
    def body(dh_ref, w_ref, y_ref, after_ref, dy_ref, dw_ref):
        del after_ref
        dh_ = dh_ref[...]
        dy_ref[...] = _dot(dh_, w_ref[...], NT)
        dw_ref[...] = _dot(y_ref[...], dh_, TN).astype(dw_ref.dtype)

    return pl.pallas_call(
        body, name=name, grid=(D // BN,),
        in_specs=[_full((T, D)), pl.BlockSpec((BN, D), lambda j: (j, 0)), pl.BlockSpec((T, BN), lambda j: (0, j)), ANYSPEC],
        out_specs=[pl.BlockSpec((T, BN), lambda j: (0, j)), pl.BlockSpec((BN, D), lambda j: (j, 0))],
        out_shape=[_sds((T, D), F32), _sds((D, D), WIRE_DTYPE)], compiler_params=_params(("parallel",)),
    )(dh, w, ymix, after)


def mm_rows_nn(pairs, after, name):
    n = len(pairs)

    def body(*refs):
        o_ref = refs[2 * n + 1]

        @pl.when(pl.program_id(0) == 0)
        def _():
            o_ref[...] = jnp.zeros_like(o_ref)
        for p in range(n):
            o_ref[...] += _dot(refs[2 * p][...], refs[2 * p + 1][...], NN)

    specs, args = [], []
    for a, w in pairs:
        specs += [pl.BlockSpec((T, FB), lambda j: (0, j)), pl.BlockSpec((FB, D), lambda j: (j, 0))]
        args += [a, w]
    return pl.pallas_call(
        body, name=name, grid=(D_FFP // FB,), in_specs=specs + [ANYSPEC], out_specs=_full((T, D)),
        out_shape=_sds((T, D), F32), compiler_params=_params(("arbitrary",)),
    )(*args, after)


def mm_tn_blocked(a, b, name):
    def body(a_ref, b_ref, o_ref):
        o = _dot(a_ref[...], b_ref[...], TN).astype(o_ref.dtype)
        o_ref[0] = o[:, :IN_SH]
        o_ref[1] = o[:, IN_SH:]

    return pl.pallas_call(
        body, name=name, grid=(NPAIR,),
        in_specs=[_full((T, D)), COLS_PAIR], out_specs=W_PAIR(D),
        out_shape=_sds((NDEV, D, IN_SH), WIRE_DTYPE), compiler_params=_params(("parallel",)),
    )(a, b)


def mm_tn(a, b, bm, after, name):
    m = a.shape[1]

    def body(a_ref, b_ref, after_ref, o_ref):
        del after_ref
        o_ref[...] = _dot(a_ref[...], b_ref[...], TN).astype(o_ref.dtype)

    return pl.pallas_call(
        body, name=name, grid=(m // bm,),
        in_specs=[pl.BlockSpec((T, bm), lambda i: (0, i)), _full((T, D)), ANYSPEC],
        out_specs=pl.BlockSpec((bm, D), lambda i: (i, 0)),
        out_shape=_sds((m, D), WIRE_DTYPE), compiler_params=_params(("parallel",)),
    )(a, b, after)


def _softplus_neg(lam):
    return jnp.maximum(-lam, 0.0) + jnp.log1p(jnp.exp(-jnp.abs(lam)))


def _lru_gates(pa, px, xc, lam):
    r = jax.nn.sigmoid(pa)
    ig = jax.nn.sigmoid(px)
    sp = _softplus_neg(lam)
    log_a = -LRU_C * r * sp
    a = jnp.exp(log_a)
    mult = jnp.sqrt(-jnp.tanh(log_a) * (a * a + 1.0))
    return a, mult * (ig * xc), (r, ig, sp, mult)


def _lru_gates_vjp(da, db, xc, lam, a, r, ig, sp, mult):
    dmult = db * (ig * xc)
    du = db * mult
    dlog_a = da * a - dmult * (a * a) / mult
    dr = dlog_a * (-LRU_C * sp)
    dlam = jnp.sum(dlog_a * (-LRU_C * r), axis=0, keepdims=True) * (-jax.nn.sigmoid(-lam))
    dpa = dr * (r * (1.0 - r))
    dpx = (du * xc) * (ig * (1.0 - ig))
    return dpa, dpx, du * ig, dlam


def _lru_out(h, g, gain):
    z = h * jax.nn.gelu(g)
    return z * lax.rsqrt(jnp.mean(z * z, axis=-1, keepdims=True) + EPS) * gain


def _conv_taps(x, xprev, row):
    taps = [x]
    for s in range(1, CONV_W):
        taps.append(jnp.where(row < s, pltpu.roll(xprev, s, 0), pltpu.roll(x, s, 0)))
    return taps


def _conv(taps, cw_ref, cb):
    xc = cb + cw_ref[CONV_W - 1:CONV_W, :] * taps[0]
    for s in range(1, CONV_W):
        xc = xc + cw_ref[CONV_W - 1 - s:CONV_W - s, :] * taps[s]
    return xc


def lru_fwd(proj, cw, cb, wa, ba, wx, bx, lam, gain, name):
    def body(x_ref, g_ref, cw_ref, cb_ref, wa_ref, ba_ref, wx_ref, bx_ref, lam_ref, gain_ref,
             y_ref, h_ref, xprev_scr, a_scr, b_scr, carry_scr):
        i = pl.program_id(0)

        @pl.when(i == 0)
        def _():
            xprev_scr[...] = jnp.zeros_like(xprev_scr)
            carry_scr[...] = jnp.zeros_like(carry_scr)

        x = x_ref[...]
        row = lax.broadcasted_iota(jnp.int32, (CH, D_LRU), 0)
        xc = _conv(_conv_taps(x, xprev_scr[...], row), cw_ref, cb_ref[...])
        pa = _dot(xc, wa_ref[...], NN) + ba_ref[...]
        px = _dot(xc, wx_ref[...], NN) + bx_ref[...]
        a, b, _ = _lru_gates(pa, px, xc, lam_ref[...])
        a_scr[...] = a
        b_scr[...] = jnp.where(i * CH + row >= PAD, b, 0.0)
        h = carry_scr[...]
        for t in range(CH):
            h = a_scr[t:t + 1, :] * h + b_scr[t:t + 1, :]
            h_ref[t:t + 1, :] = h
        carry_scr[...] = h
        xprev_scr[...] = x
        y_ref[...] = _lru_out(h_ref[...], g_ref[...], gain_ref[...]).astype(y_ref.dtype)

    vec = _full((1, D_LRU))
    mat = _full((D_LRU, D_LRU))
    return pl.pallas_call(
        body, name=name, grid=(NCH,),
        in_specs=[pl.BlockSpec((CH, D_LRU), lambda i: (i, 0)), pl.BlockSpec((CH, D_LRU), lambda i: (i, 1)),
                  _full((CONV_W, D_LRU)), vec, mat, vec, mat, vec, vec, vec],
        out_specs=[pl.BlockSpec((CH, D_LRU), lambda i: (i, 0)), pl.BlockSpec((CH, D_LRU), lambda i: (i, 0))],
        out_shape=[_sds((T, D_LRU), MXU_DTYPE), _sds((T, D_LRU), F32)],
        scratch_shapes=[pltpu.VMEM((CH, D_LRU), F32), pltpu.VMEM((CH, D_LRU), F32), pltpu.VMEM((CH, D_LRU), F32),
                        pltpu.VMEM((1, D_LRU), F32)],
        compiler_params=_params(("arbitrary",)),
    )(proj, proj, cw, cb, wa, ba, wx, bx, lam, gain)


LRU_VEC_ROWS = 16


def _lru_bwd_block(ib, x_ref, xp_ref, g_ref, h_ref, hp_ref, dy_ref, cw_ref, cb_ref, wa_ref, ba_ref, wx_ref, bx_ref, lam_ref,
                   gain_ref, dp_ref, vec_ref, dwa_ref, dwx_ref, a_scr, dh_scr, g_scr, carry_scr, dxcn_scr):
    @pl.when(ib == NCH - 1)
    def _():
        carry_scr[...] = jnp.zeros_like(carry_scr)
        dxcn_scr[...] = jnp.zeros_like(dxcn_scr)
        vec_ref[...] = jnp.zeros_like(vec_ref)
        dwa_ref[...] = jnp.zeros_like(dwa_ref)
        dwx_ref[...] = jnp.zeros_like(dwx_ref)

    x = x_ref[...]
    row = lax.broadcasted_iota(jnp.int32, (CH, D_LRU), 0)
    valid = ib * CH + row >= PAD
    taps = _conv_taps(x, xp_ref[...], row)
    xc = _conv(taps, cw_ref, cb_ref[...])
    pa = _dot(xc, wa_ref[...], NN) + ba_ref[...]
    px = _dot(xc, wx_ref[...], NN) + bx_ref[...]
    a, _, gate_parts = _lru_gates(pa, px, xc, lam_ref[...])
    h = h_ref[...]
    _, vjp_out = jax.vjp(_lru_out, h, g_ref[...], gain_ref[...])
    dh, dg, dgain = vjp_out(dy_ref[:, :D_LRU].astype(F32))
    a_scr[...] = a
    dh_scr[...] = dh
    c = carry_scr[...]
    for t in range(CH - 1, -1, -1):
        gt = dh_scr[t:t + 1, :] + c
        g_scr[t:t + 1, :] = gt
        c = a_scr[t:t + 1, :] * gt
    carry_scr[...] = c
    gg = g_scr[...]
    hprev = jnp.where(row < 1, pltpu.roll(hp_ref[...], 1, 0), pltpu.roll(h, 1, 0))
    da = jnp.where(valid, gg * hprev, 0.0)
    db = jnp.where(valid, gg, 0.0)
    dpa, dpx, dxc, dlam = _lru_gates_vjp(da, db, xc, lam_ref[...], a, *gate_parts)
    dxc = dxc + _dot(dpa, wa_ref[...], NT) + _dot(dpx, wx_ref[...], NT)
    dwa_ref[...] += _dot(xc, dpa, TN)
    dwx_ref[...] += _dot(xc, dpx, TN)
    for s in range(CONV_W):
        vec_ref[CONV_W - 1 - s:CONV_W - s, :] += jnp.sum(dxc * taps[s], axis=0, keepdims=True)
    vec_ref[4:5, :] += jnp.sum(dxc, axis=0, keepdims=True)
    vec_ref[5:6, :] += jnp.sum(dpa, axis=0, keepdims=True)
    vec_ref[6:7, :] += jnp.sum(dpx, axis=0, keepdims=True)
    vec_ref[7:8, :] += dlam
    vec_ref[8:9, :] += dgain
    dxn = dxcn_scr[...]
    dx = cw_ref[CONV_W - 1:CONV_W, :] * dxc
    for s in range(1, CONV_W):
        ahead = jnp.where(row >= CH - s, pltpu.roll(dxn, CH - s, 0), pltpu.roll(dxc, CH - s, 0))
        dx = dx + cw_ref[CONV_W - 1 - s:CONV_W - s, :] * ahead
    dxcn_scr[...] = dxc
    dp_ref[:, :D_LRU] = jnp.where(valid, dx, 0.0).astype(dp_ref.dtype)
    dp_ref[:, D_LRU:2 * D_LRU] = dg.astype(dp_ref.dtype)


def _ret_tables():
    half = HD // 2
    pos = jnp.arange(T, dtype=F32) - float(PAD)
    inv = ROPE_BASE ** (-jnp.arange(half, dtype=F32) / half)
    ang = pos[:, None] * inv[None, :]
    cos = jnp.concatenate([jnp.cos(ang), jnp.cos(ang)], axis=-1)
    sin = jnp.concatenate([-jnp.sin(ang), jnp.sin(ang)], axis=-1)
    log_g = jnp.log(1.0 - 2.0 ** (-5.0 - jnp.arange(HEADS, dtype=F32)))
    idx = jnp.arange(CH, dtype=F32)
    diff = idx[:, None] - idx[None, :]
    dmask = jnp.where(diff[None] >= 0, jnp.exp(jnp.maximum(diff, 0.0)[None] * log_g[:, None, None]), 0.0)
    xi = jnp.exp((idx + 1.0)[None, :] * log_g[:, None])
    zeta = jnp.exp((CH - 1.0 - idx)[None, :] * log_g[:, None])
    xi = jnp.broadcast_to(xi[:, :, None], (HEADS, CH, HD))
    zeta = jnp.broadcast_to(zeta[:, :, None], (HEADS, CH, HD))
    return cos, sin, dmask, xi, zeta


def _chunk_decay():
    log_g = np.log(np.float32(1.0) - np.float32(2.0) ** (np.float32(-5.0) - np.arange(HEADS, dtype=np.float32)))
    return [float(v) for v in np.exp(np.float32(CH) * log_g.astype(np.float32))]


def _rope(x, cos, sin):
    return x * cos + pltpu.roll(x, HD // 2, 1) * sin


def ret_fwd(proj, ylru, tables, gain, after, name):
    cos, sin, dmask, xi, zeta = tables
    gch = _chunk_decay()
    scale = HD ** -0.5

    def body(q_ref, k_ref, v_ref, g_ref, cos_ref, sin_ref, dm_ref, xi_ref, zt_ref, gain_ref, ylru_ref, after_ref,
             y_ref, st_ref, s_scr):
        del after_ref

        @pl.when(pl.program_id(0) == 0)
        def _():
            s_scr[...] = jnp.zeros_like(s_scr)

        y_ref[:, :D_LRU] = ylru_ref[...]
        cs, sn = cos_ref[...], sin_ref[...]
        hs = range(HEADS)
        sl = [slice(HD * h, HD * (h + 1)) for h in hs]
        qr = [_rope(q_ref[:, sl[h]], cs, sn).astype(MXU_DTYPE) for h in hs]
        kf = [_rope(k_ref[:, sl[h]], cs, sn) * scale for h in hs]
        kr = [kf[h].astype(MXU_DTYPE) for h in hs]
        v = [v_ref[:, sl[h]].astype(MXU_DTYPE) for h in hs]
        s = [s_scr[h] for h in hs]
        for h in hs:
            st_ref[h] = s[h]
        sc = [_dot(qr[h], kr[h], NT) * dm_ref[h] for h in hs]
        cross = [_dot(qr[h], s[h], NN) * xi_ref[h] for h in hs]
        for h in hs:
            s_scr[h] = s[h] * gch[h] + _dot(kf[h] * zt_ref[h], v[h], TN)
        y = [_dot(sc[h], v[h], NN) + cross[h] for h in hs]
        yc = [y[h] - jnp.mean(y[h], axis=-1, keepdims=True) for h in hs]
        yn = [yc[h] * lax.rsqrt(jnp.mean(yc[h] * yc[h], axis=-1, keepdims=True) + EPS) for h in hs]
        for h in hs:
            so = slice(D_LRU + HD * h, D_LRU + HD * (h + 1))
            y_ref[:, so] = (jax.nn.silu(g_ref[:, sl[h]]) * (yn[h] * gain_ref[:, sl[h]])).astype(y_ref.dtype)

    def col(c):
        return pl.BlockSpec((CH, D_RET), lambda n: (n, c))

    tab = pl.BlockSpec((CH, HD), lambda n: (n, 0))
    cst = _full((HEADS, CH, HD))
    return pl.pallas_call(
        body, name=name, grid=(NCH,),
        in_specs=[col(2), col(3), col(4), col(5), tab, tab, cst, cst, cst, _full((1, D_RET)), col(0),
                  pl.BlockSpec(memory_space=pl.ANY)],
        out_specs=[pl.BlockSpec((CH, D), lambda n: (n, 0)), pl.BlockSpec((None, HEADS, HD, HD), lambda n: (n, 0, 0, 0))],
        out_shape=[_sds((T, D), MXU_DTYPE), _sds((NCH, HEADS, HD, HD), F32)],
        scratch_shapes=[pltpu.VMEM((HEADS, HD, HD), F32)],
        compiler_params=_params(("arbitrary",)),
    )(proj, proj, proj, proj, cos, sin, dmask, xi, zeta, gain, ylru, after)


def mix_bwd(proj, hst, states, dymix, cw, cb, wa, ba, wx, bx, lam, gain, tables, ret_gain, after, name):
    cos, sin, dmask, xi, zeta = tables
    gch = _chunk_decay()
    scale = HD ** -0.5
    last = NCH - 1

    def body(x_ref, xp_ref, gl_ref, h_ref, hp_ref, cw_ref, cb_ref, wa_ref, ba_ref, wx_ref, bx_ref, lam_ref, lgain_ref,
             q_ref, k_ref, v_ref, g_ref, st_ref, dy_ref, cos_ref, sin_ref, dm_ref, xi_ref, zt_ref, gain_ref, after_ref,
             dp_ref, vec_ref, dwa_ref, dwx_ref, dgain_ref, a_scr, dh_scr, g_scr, carry_scr, dxcn_scr, ds_scr):
        del after_ref

        @pl.when(pl.program_id(0) == 0)
        def _():
            ds_scr[...] = jnp.zeros_like(ds_scr)
            dgain_ref[...] = jnp.zeros_like(dgain_ref)

        _lru_bwd_block(last - pl.program_id(0), x_ref, xp_ref, gl_ref, h_ref, hp_ref, dy_ref, cw_ref, cb_ref, wa_ref, ba_ref,
                       wx_ref, bx_ref, lam_ref, lgain_ref, dp_ref, vec_ref, dwa_ref, dwx_ref, a_scr, dh_scr, g_scr, carry_scr,
                       dxcn_scr)
        cs, sn = cos_ref[...], sin_ref[...]
        hs = range(HEADS)
        sl = [slice(HD * h, HD * (h + 1)) for h in hs]

        def out(j, h):
            return slice(2 * D_LRU + j * D_RET + HD * h, 2 * D_LRU + j * D_RET + HD * (h + 1))

        b16 = lambda xs: [x.astype(MXU_DTYPE) for x in xs]
        qr = b16([_rope(q_ref[:, sl[h]], cs, sn) for h in hs])
        kf = [_rope(k_ref[:, sl[h]], cs, sn) * scale for h in hs]
        kr = b16(kf)
        kz = b16([kf[h] * zt_ref[h] for h in hs])
        v = b16([v_ref[:, sl[h]] for h in hs])
        s = b16([st_ref[h] for h in hs])
        ds = [ds_scr[h] for h in hs]
        dsb = b16(ds)
        sc = [_dot(qr[h], kr[h], NT) * dm_ref[h] for h in hs]
        scb = b16(sc)
        y = [_dot(scb[h], v[h], NN) + _dot(qr[h], s[h], NN) * xi_ref[h] for h in hs]
        yc = [y[h] - jnp.mean(y[h], axis=-1, keepdims=True) for h in hs]
        rstd = [lax.rsqrt(jnp.mean(yc[h] * yc[h], axis=-1, keepdims=True) + EPS) for h in hs]
        yn = [yc[h] * rstd[h] for h in hs]
        dy = []
        for h in hs:
            g = g_ref[:, sl[h]]
            gain = gain_ref[:, sl[h]]
            sg = jax.nn.sigmoid(g)
            silu = g * sg
            dout = dy_ref[:, D_LRU + HD * h:D_LRU + HD * (h + 1)].astype(F32)
            dgain_ref[:, sl[h]] += jnp.sum(dout * silu * yn[h], axis=0, keepdims=True)
            dp_ref[:, out(3, h)] = (dout * yn[h] * gain * (sg * (1.0 + g * (1.0 - sg)))).astype(dp_ref.dtype)
            dyn = dout * silu * gain
            dy.append(rstd[h] * (dyn - jnp.mean(dyn, axis=-1, keepdims=True)
                                 - yn[h] * jnp.mean(dyn * yn[h], axis=-1, keepdims=True)))
        dyb = b16(dy)
        dqs = b16([dy[h] * xi_ref[h] for h in hs])
        dp = b16([_dot(dyb[h], v[h], NT) * dm_ref[h] for h in hs])
        dv = [_dot(scb[h], dyb[h], TN) + _dot(kz[h], dsb[h], NN) for h in hs]
        dqr = [_dot(dp[h], kr[h], NN) + _dot(dqs[h], s[h], NT) for h in hs]
        dkr = [_dot(dp[h], qr[h], TN) + _dot(v[h], dsb[h], NT) * zt_ref[h] for h in hs]
        for h in hs:
            ds_scr[h] = gch[h] * ds[h] + _dot(qr[h], dqs[h], TN)
        for h in hs:
            dp_ref[:, out(0, h)] = (dqr[h] * cs + pltpu.roll(dqr[h] * sn, HD // 2, 1)).astype(dp_ref.dtype)
            dp_ref[:, out(1, h)] = ((dkr[h] * cs + pltpu.roll(dkr[h] * sn, HD // 2, 1)) * scale).astype(dp_ref.dtype)
            dp_ref[:, out(2, h)] = dv[h].astype(dp_ref.dtype)

    def col(c, shift=0):
        return pl.BlockSpec((CH, D_RET), lambda n: (jnp.maximum(last - n - shift, 0), c))

    tab = pl.BlockSpec((CH, HD), lambda n: (last - n, 0))
    cst = _full((HEADS, CH, HD))
    vec = _full((1, D_LRU))
    mat = _full((D_LRU, D_LRU))
    blockbuf = pltpu.VMEM((CH, D_LRU), F32)
    return pl.pallas_call(
        body, name=name, grid=(NCH,),
        in_specs=[col(0), col(0, 1), col(1), col(0), col(0, 1), _full((CONV_W, D_LRU)), vec, mat, vec, mat, vec, vec, vec,
                  col(2), col(3), col(4), col(5), pl.BlockSpec((None, HEADS, HD, HD), lambda n: (last - n, 0, 0, 0)),
                  pl.BlockSpec((CH, D), lambda n: (last - n, 0)), tab, tab, cst, cst, cst, _full((1, D_RET)),
                  pl.BlockSpec(memory_space=pl.ANY)],
        out_specs=[pl.BlockSpec((CH, D_IN), lambda n: (last - n, 0)), _full((LRU_VEC_ROWS, D_LRU)), mat, mat,
                   _full((1, D_RET))],
        out_shape=[_sds((T, D_IN), MXU_DTYPE), _sds((LRU_VEC_ROWS, D_LRU), F32), _sds((D_LRU, D_LRU), F32),
                   _sds((D_LRU, D_LRU), F32), _sds((1, D_RET), F32)],
        scratch_shapes=[blockbuf, blockbuf, blockbuf, pltpu.VMEM((1, D_LRU), F32), blockbuf,
                        pltpu.VMEM((HEADS, HD, HD), F32)],
        compiler_params=_params(("arbitrary",)),
    )(proj, proj, proj, hst, hst, cw, cb, wa, ba, wx, bx, lam, gain, proj, proj, proj, proj, states, dymix,
      cos, sin, dmask, xi, zeta, ret_gain, after)


HBM = pl.BlockSpec(memory_space=pltpu.HBM)


def _place():
    return lax.axis_index("x"), lax.axis_index("y"), lax.axis_index("c")


def all_gather(arrs, name):
    n = len(arrs)

    def body(*refs):
        ins, outs = refs[:n], refs[n:2 * n]
        send_sems, recv_sems, local_sems = refs[2 * n:]
        x, y, c = _place()
        me, sibling = (x, y, c), (x, y, 1 - c)
        chips = [(1 - x, y), (x, 1 - y), (1 - x, 1 - y)]

        def copy(a, k, block, to, src=None):
            px, py, pc = block
            dst = outs[a].at[4 * px + 2 * py + pc]
            return pltpu.make_async_remote_copy(
                src_ref=dst if src is None else src, dst_ref=dst, send_sem=send_sems.at[a, k], recv_sem=recv_sems.at[a, k],
                device_id=to, device_id_type=MESH)

        mine = [pltpu.make_async_copy(ins[a], outs[a].at[4 * x + 2 * y + c], local_sems.at[a]) for a in range(n)]
        for cp in mine:
            cp.start()
        first = []
        for a in range(n):
            first.append(copy(a, 0, me, sibling, src=ins[a]))
            first += [copy(a, 1 + j, me, (*chip, c), src=ins[a]) for j, chip in enumerate(chips)]
        for cp in first:
            cp.start()
        passed = []
        for j, chip in enumerate(chips):
            for a in range(n):
                copy(a, 1 + j, (*chip, c), me).wait_recv()
                passed.append(copy(a, 4 + j, (*chip, c), sibling))
                passed[-1].start()
        for a in range(n):
            copy(a, 0, sibling, me).wait_recv()
            for j, chip in enumerate(chips):
                copy(a, 4 + j, (*chip, 1 - c), me).wait_recv()
        for cp in first + passed:
            cp.wait_send()
        for cp in mine:
            cp.wait()

    return pl.pallas_call(
        body, name=name,
        in_specs=[HBM] * n, out_specs=[HBM] * n,
        out_shape=[_sds((NDEV,) + a.shape, a.dtype) for a in arrs],
        scratch_shapes=[pltpu.SemaphoreType.DMA((n, 7)), pltpu.SemaphoreType.DMA((n, 7)), pltpu.SemaphoreType.DMA((n,))],
    )(*arrs)


SEM = pl.BlockSpec(memory_space=pltpu.SEMAPHORE)
ANY = pl.BlockSpec(memory_space=pl.ANY)
EFFECT = pltpu.SideEffectType.DATAFLOW_SIDE_EFFECTING


def _hbm(a):
    return pltpu.with_memory_space_constraint(a, pltpu.HBM)


def _hbm_like(arrs):
    return [pltpu.HBM(a.shape, a.dtype) for a in arrs]


def _dma_sems(count):
    return [pltpu.SemaphoreType.DMA(())] * count


def _ag_copy(lands, send_sems, recv_sems, per):
    def copy(a, k, block, to, src=None):
        px, py, pc = block
        dst = lands[a].at[4 * px + 2 * py + pc]
        return pltpu.make_async_remote_copy(
            src_ref=dst if src is None else src, dst_ref=dst, send_sem=send_sems[a * per + k], recv_sem=recv_sems[a * per + k],
            device_id=to, device_id_type=MESH)
    return copy


def to_wire(sel, w_in, w_gate, w_up, w_out, w_down, name):
    ffpad = FF_SHP - FF_SH

    def body(sel_ref, i_ref, g_ref, u_ref, o_ref, d_ref, oi, og, ou, oo, od):
        del sel_ref
        oi[...] = i_ref[...].astype(oi.dtype)
        oo[...] = o_ref[...].astype(oo.dtype)
        for src, dst in ((g_ref, og), (u_ref, ou), (d_ref, od)):
            dst[:FF_SH, :] = src[...].astype(dst.dtype)
            dst[FF_SH:, :] = jnp.zeros((ffpad, D), dst.dtype)

    shapes_in = [(D, IN_SH), (FF_SH, D), (FF_SH, D), (OUT_SH, D), (FF_SH, D)]
    shapes_out = [(D, IN_SH), (FF_SHP, D), (FF_SHP, D), (OUT_SH, D), (FF_SHP, D)]
    return pl.pallas_call(
        body, name=name,
        grid_spec=pltpu.PrefetchScalarGridSpec(
            num_scalar_prefetch=1, grid=(1,),
            in_specs=[pl.BlockSpec((None,) + s, lambda i, sel_ref: (sel_ref[1], 0, 0)) for s in shapes_in],
            out_specs=[pl.BlockSpec((None,) + s, lambda i, sel_ref: (sel_ref[0], 0, 0)) for s in shapes_out]),
        out_shape=[_sds((NDEV,) + s, WIRE_DTYPE) for s in shapes_out], compiler_params=_params(("arbitrary",)),
    )(sel, w_in, w_gate, w_up, w_out, w_down)


def place_blocks(sel, arrs, name):
    n = len(arrs)

    def body(sel_ref, *refs):
        del sel_ref
        for a in range(n):
            refs[n + a][...] = refs[a][...]

    def whole(a):
        nd = a.ndim
        return pl.BlockSpec(a.shape, lambda i, sel_ref: (0,) * nd)

    def mine(a):
        nd = a.ndim
        return pl.BlockSpec((None,) + a.shape, lambda i, sel_ref: (sel_ref[0],) + (0,) * nd)

    return pl.pallas_call(
        body, name=name,
        grid_spec=pltpu.PrefetchScalarGridSpec(
            num_scalar_prefetch=1, grid=(1,), in_specs=[whole(a) for a in arrs], out_specs=[mine(a) for a in arrs]),
        out_shape=[_sds((NDEV,) + a.shape, a.dtype) for a in arrs], compiler_params=_params(("arbitrary",)),
    )(sel, *arrs)


def ag_start(lands, after, name):
    n = len(lands)
    ns = 4 * n

    def body(*refs):
        lnd = refs[:n]
        send_sems, recv_sems = refs[n + 1:n + 1 + ns], refs[n + 1 + ns:n + 1 + 2 * ns]
        token = refs[-1]
        x, y, c = _place()
        me, sibling = (x, y, c), (x, y, 1 - c)
        chips = [(1 - x, y), (x, 1 - y), (1 - x, 1 - y)]
        copy = _ag_copy(lnd, send_sems, recv_sems, 4)
        for a in range(n):
            copy(a, 0, me, sibling).start()
            for j, chip in enumerate(chips):
                copy(a, 1 + j, me, (*chip, c)).start()
        token[...] = jnp.zeros_like(token)

    outs = pl.pallas_call(
        body, name=name,
        in_specs=[HBM] * n + [ANY],
        out_specs=[SEM] * (2 * ns) + [HBM] * n + [pl.BlockSpec(memory_space=pltpu.VMEM)],
        out_shape=_dma_sems(2 * ns) + _hbm_like(lands) + [_sds((8, 128), F32)],
        input_output_aliases={i: 2 * ns + i for i in range(n)},
        compiler_params=pltpu.CompilerParams(has_side_effects=EFFECT),
    )(*[_hbm(a) for a in lands], after)
    return outs[:ns], outs[ns:2 * ns], outs[2 * ns:2 * ns + n], outs[-1]


def ag_forward(send_sems, recv_sems, lands, after, name):
    n = len(lands)
    n1, n2 = 4 * n, 3 * n

    def body(*refs):
        lnd = refs[:n]
        o = n
        s1, r1 = refs[o:o + n1], refs[o + n1:o + 2 * n1]
        o += 2 * n1 + 1
        s2, r2 = refs[o:o + n2], refs[o + n2:o + 2 * n2]
        token = refs[-1]
        token[...] = jnp.zeros_like(token)
        x, y, c = _place()
        me, sibling = (x, y, c), (x, y, 1 - c)
        chips = [(1 - x, y), (x, 1 - y), (1 - x, 1 - y)]
        copy1 = _ag_copy(lnd, s1, r1, 4)
        copy2 = _ag_copy(lnd, s2, r2, 3)
        for j, chip in enumerate(chips):
            for a in range(n):
                copy1(a, 1 + j, (*chip, c), me).wait_recv()
                copy2(a, j, (*chip, c), sibling).start()
        for a in range(n):
            copy1(a, 0, sibling, me).wait_recv()
            copy1(a, 0, me, sibling).wait_send()
            for j, chip in enumerate(chips):
                copy1(a, 1 + j, me, (*chip, c)).wait_send()

    outs = pl.pallas_call(
        body, name=name,
        in_specs=[HBM] * n + [SEM] * (2 * n1) + [ANY],
        out_specs=[SEM] * (2 * n2) + [HBM] * n + [pl.BlockSpec(memory_space=pltpu.VMEM)],
        out_shape=_dma_sems(2 * n2) + _hbm_like(lands) + [_sds((8, 128), F32)],
        input_output_aliases={i: 2 * n2 + i for i in range(n)},
        compiler_params=pltpu.CompilerParams(has_side_effects=EFFECT),
    )(*lands, *send_sems, *recv_sems, after)
    return outs[:n2], outs[n2:2 * n2], outs[2 * n2:2 * n2 + n], outs[-1]


def ag_finish(send_sems, recv_sems, lands, after, name):
    n = len(lands)
    n2 = 3 * n

    def body(*refs):
        lnd = refs[:n]
        s2, r2 = refs[n:n + n2], refs[n + n2:n + 2 * n2]
        x, y, c = _place()
        me, sibling = (x, y, c), (x, y, 1 - c)
        chips = [(1 - x, y), (x, 1 - y), (1 - x, 1 - y)]
        copy2 = _ag_copy(lnd, s2, r2, 3)
        for a in range(n):
            for j, chip in enumerate(chips):
                copy2(a, j, (*chip, c), sibling).wait_send()
                copy2(a, j, (*chip, 1 - c), me).wait_recv()

    outs = pl.pallas_call(
        body, name=name,
        in_specs=[HBM] * n + [SEM] * (2 * n2) + [ANY],
        out_specs=[HBM] * n, out_shape=_hbm_like(lands),
        input_output_aliases={i: i for i in range(n)},
        compiler_params=pltpu.CompilerParams(has_side_effects=EFFECT),
    )(*lands, *send_sems, *recv_sems, after)
    return list(outs)


def rs_sibling_start(arrs, name):
    n = len(arrs)
    ns = 4 * n
    lands = [lax.empty((4,) + a.shape[1:], a.dtype) for a in arrs]

    def body(*refs):
        ins, lnd = refs[:n], refs[n:2 * n]
        send_sems, recv_sems = refs[2 * n:2 * n + ns], refs[2 * n + ns:2 * n + 2 * ns]
        x, y, c = _place()
        sibling = (x, y, 1 - c)
        for a in range(n):
            for p in range(4):
                pltpu.make_async_remote_copy(
                    src_ref=ins[a].at[2 * p + 1 - c], dst_ref=lnd[a].at[p], send_sem=send_sems[4 * a + p],
                    recv_sem=recv_sems[4 * a + p], device_id=sibling, device_id_type=MESH).start()
        refs[-1][...] = jnp.zeros_like(refs[-1])

    outs = pl.pallas_call(
        body, name=name,
        in_specs=[HBM] * (2 * n), out_specs=[SEM] * (2 * ns) + [HBM] * (2 * n) + [pl.BlockSpec(memory_space=pltpu.VMEM)],
        out_shape=_dma_sems(2 * ns) + _hbm_like(arrs) + _hbm_like(lands) + [_sds((8, 128), F32)],
        input_output_aliases={i: 2 * ns + i for i in range(2 * n)},
        compiler_params=pltpu.CompilerParams(has_side_effects=EFFECT),
    )(*[_hbm(a) for a in arrs], *[_hbm(a) for a in lands])
    return (outs[:ns], outs[ns:2 * ns], outs[2 * ns:2 * ns + n], outs[2 * ns + n:2 * ns + 2 * n]), outs[-1]


def rs_sibling_wait(send_sems, recv_sems, arrs, lands, after, name):
    n = len(arrs)
    ns = 4 * n

    def body(*refs):
        ins, lnd = refs[:n], refs[n:2 * n]
        s, r = refs[2 * n:2 * n + ns], refs[2 * n + ns:2 * n + 2 * ns]
        x, y, c = _place()
        sibling = (x, y, 1 - c)
        for a in range(n):
            for p in range(4):
                cp = pltpu.make_async_remote_copy(
                    src_ref=ins[a].at[2 * p + 1 - c], dst_ref=lnd[a].at[p], send_sem=s[4 * a + p], recv_sem=r[4 * a + p],
                    device_id=sibling, device_id_type=MESH)
                cp.wait_send()
                cp.wait_recv()

    outs = pl.pallas_call(
        body, name=name,
        in_specs=[HBM] * (2 * n) + [SEM] * (2 * ns) + [ANY], out_specs=[HBM] * (2 * n),
        out_shape=_hbm_like(arrs) + _hbm_like(lands),
        input_output_aliases={i: i for i in range(2 * n)},
        compiler_params=pltpu.CompilerParams(has_side_effects=EFFECT),
    )(*arrs, *lands, *send_sems, *recv_sems, after)
    return outs[:n], outs[n:]


def rs_chips_start(parts, name):
    n = len(parts)
    ns = 3 * n
    lands = [lax.empty((3,) + a.shape[1:], a.dtype) for a in parts]

    def body(*refs):
        ins, lnd = refs[:n], refs[n:2 * n]
        send_sems, recv_sems = refs[2 * n:2 * n + ns], refs[2 * n + ns:2 * n + 2 * ns]
        x, y, c = _place()
        chips = [(1 - x, y), (x, 1 - y), (1 - x, 1 - y)]
        for a in range(n):
            for k, (tx, ty) in enumerate(chips):
                pltpu.make_async_remote_copy(
                    src_ref=ins[a].at[2 * tx + ty], dst_ref=lnd[a].at[k], send_sem=send_sems[3 * a + k],
                    recv_sem=recv_sems[3 * a + k], device_id=(tx, ty, c), device_id_type=MESH).start()
        refs[-1][...] = jnp.zeros_like(refs[-1])

    outs = pl.pallas_call(
        body, name=name,
        in_specs=[HBM] * (2 * n), out_specs=[SEM] * (2 * ns) + [HBM] * (2 * n) + [pl.BlockSpec(memory_space=pltpu.VMEM)],
        out_shape=_dma_sems(2 * ns) + _hbm_like(parts) + _hbm_like(lands) + [_sds((8, 128), F32)],
        input_output_aliases={i: 2 * ns + i for i in range(2 * n)},
        compiler_params=pltpu.CompilerParams(has_side_effects=EFFECT),
    )(*[_hbm(a) for a in parts], *[_hbm(a) for a in lands])
    return (outs[:ns], outs[ns:2 * ns], outs[2 * ns:2 * ns + n], outs[2 * ns + n:2 * ns + 2 * n]), outs[-1]


def rs_chips_wait(send_sems, recv_sems, parts, lands, after, name):
    n = len(parts)
    ns = 3 * n

    def body(*refs):
        ins, lnd = refs[:n], refs[n:2 * n]
        s, r = refs[2 * n:2 * n + ns], refs[2 * n + ns:2 * n + 2 * ns]
        x, y, c = _place()
        chips = [(1 - x, y), (x, 1 - y), (1 - x, 1 - y)]
        for a in range(n):
            for k, (tx, ty) in enumerate(chips):
                cp = pltpu.make_async_remote_copy(
                    src_ref=ins[a].at[2 * tx + ty], dst_ref=lnd[a].at[k], send_sem=s[3 * a + k], recv_sem=r[3 * a + k],
                    device_id=(tx, ty, c), device_id_type=MESH)
                cp.wait_send()
                cp.wait_recv()

    outs = pl.pallas_call(
        body, name=name,
        in_specs=[HBM] * (2 * n) + [SEM] * (2 * ns) + [ANY], out_specs=[HBM] * (2 * n),
        out_shape=_hbm_like(parts) + _hbm_like(lands),
        input_output_aliases={i: i for i in range(2 * n)},
        compiler_params=pltpu.CompilerParams(has_side_effects=EFFECT),
    )(*parts, *lands, *send_sems, *recv_sems, after)
    return outs[:n], outs[n:]


def pair_sum(arrs, recv, c, name):
    n = len(arrs)

    def body(c_ref, *refs):
        del c_ref
        for a in range(n):
            refs[2 * n + a][...] = (refs[a][...].astype(F32) + refs[n + a][...].astype(F32)).astype(refs[2 * n + a].dtype)

    mine = [pl.BlockSpec((None,) + a.shape[1:], lambda p, c_ref: (2 * p + c_ref[0], 0, 0)) for a in arrs]
    other = [pl.BlockSpec((None,) + a.shape[1:], lambda p, c_ref: (p, 0, 0)) for a in arrs]
    return pl.pallas_call(
        body, name=name,
        grid_spec=pltpu.PrefetchScalarGridSpec(num_scalar_prefetch=1, grid=(4,), in_specs=mine + other, out_specs=other),
        out_shape=[_sds((4,) + a.shape[1:], a.dtype) for a in arrs], compiler_params=_params(("parallel",)),
    )(c, *arrs, *recv)


def _adamw(w, g, m, v):
    m = ADAM_B1 * m + (1.0 - ADAM_B1) * g
    v = ADAM_B2 * v + (1.0 - ADAM_B2) * jnp.square(g)
    m_hat = m / (1.0 - ADAM_B1 ** ADAM_STEP)
    v_hat = v / (1.0 - ADAM_B2 ** ADAM_STEP)
    return -ADAM_LR * (m_hat / (jnp.sqrt(v_hat) + ADAM_EPS) + ADAM_WD * w), m, v


def adamw_big(recv, sums, chip, w, m, v, tr, name):
    nl, rr, cc = w.shape
    cp = recv[0].shape[2]

    def body(chip_ref, *refs):
        del chip_ref
        rcv, own = refs[:nl], refs[nl:2 * nl]
        w_ref, m_ref, v_ref, g_out, d_out, m_out, v_out = refs[2 * nl:]
        for l in range(nl):
            g = ((own[l][...].astype(F32) + rcv[l][0].astype(F32)) + rcv[l][1].astype(F32)) + rcv[l][2].astype(F32)
            g = g[:, :cc]
            g_out[l] = g
            d_out[l], m_out[l], v_out[l] = _adamw(w_ref[l], g, m_ref[l], v_ref[l])

    blk = pl.BlockSpec((nl, tr, cc), lambda i, chip_ref: (0, i, 0))
    return pl.pallas_call(
        body, name=name,
        grid_spec=pltpu.PrefetchScalarGridSpec(
            num_scalar_prefetch=1, grid=(rr // tr,),
            in_specs=[pl.BlockSpec((3, tr, cp), lambda i, chip_ref: (0, i, 0))] * nl
            + [pl.BlockSpec((None, tr, cp), lambda i, chip_ref: (chip_ref[0], i, 0))] * nl + [blk, blk, blk],
            out_specs=[blk] * 4),
        out_shape=[_sds(w.shape, F32)] * 4, compiler_params=_params(("parallel",)),
    )(chip, *recv, *sums, w, m, v)


SMALL_ROWS = 24


def small_grads(lvec, g_ret, g_mix, g_ffn, g_final, loss_part, dwa, dwx, name):
    def body(lvec_ref, ret_ref, mix_ref, ffn_ref, fin_ref, loss_ref, dwa_ref, dwx_ref, v_ref, g_ref):
        v_ref[16:SMALL_ROWS, :] = jnp.zeros((SMALL_ROWS - 16, D_LRU), F32)
        v_ref[16:17, 0:128] = loss_ref[0:1, :]
        v_ref[0:9, :] = lvec_ref[0:9, :]
        v_ref[9:10, :] = ret_ref[...]
        for r, src in ((10, mix_ref), (12, ffn_ref), (14, fin_ref)):
            v_ref[r:r + 1, :] = src[:, :D_LRU]
            v_ref[r + 1:r + 2, :] = src[:, D_LRU:]
        for k, src in enumerate((dwa_ref, dwx_ref)):
            for g in range(LRU_BLOCKS):
                rows = slice(LRU_BD * g, LRU_BD * (g + 1))
                g_ref[D_LRU * k + LRU_BD * g:D_LRU * k + LRU_BD * (g + 1), :] = src[rows, rows]

    ins = [lvec, g_ret, g_mix, g_ffn, g_final, loss_part, dwa, dwx]
    return pl.pallas_call(
        body, name=name, grid=(1,), in_specs=[_full(a.shape) for a in ins],
        out_specs=[_full((SMALL_ROWS, D_LRU)), _full((2 * D_LRU, LRU_BD))],
        out_shape=[_sds((SMALL_ROWS, D_LRU), F32), _sds((2 * D_LRU, LRU_BD), F32)], compiler_params=_params(("arbitrary",)),
    )(*ins)


def sum_devices(arrs, name):
    n = len(arrs)

    def body(*refs):
        for a in range(n):
            acc = refs[a][0]
            for j in range(1, NDEV):
                acc = acc + refs[a][j]
            refs[n + a][...] = acc

    return pl.pallas_call(
        body, name=name, grid=(1,), in_specs=[_full(a.shape) for a in arrs], out_specs=[_full(a.shape[1:]) for a in arrs],
        out_shape=[_sds(a.shape[1:], F32) for a in arrs], compiler_params=_params(("arbitrary",)),
    )(*arrs)


def adamw_small(gs, ws, ms, vs, name):
    n = len(gs)

    def body(*refs):
        for a in range(n):
            g, w, m, v = (refs[k * n + a][...] for k in range(4))
            refs[4 * n + a][...], refs[5 * n + a][...], refs[6 * n + a][...] = _adamw(w, g, m, v)

    specs = [_full(a.shape) for a in ws]
    outs = pl.pallas_call(
        body, name=name, grid=(1,), in_specs=specs * 4, out_specs=specs * 3, out_shape=[_sds(a.shape, F32) for a in ws] * 3,
        compiler_params=_params(("arbitrary",)),
    )(*gs, *ws, *ms, *vs)
    return outs[:n], outs[n:2 * n], outs[2 * n:]


def block_diag(wa, wx, name):
    def body(wa_ref, wx_ref, oa_ref, ox_ref):
        for src, dst in ((wa_ref, oa_ref), (wx_ref, ox_ref)):
            dst[...] = jnp.zeros_like(dst)
            for g in range(LRU_BLOCKS):
                rows = slice(LRU_BD * g, LRU_BD * (g + 1))
                dst[rows, rows] = src[g].astype(dst.dtype)

    ispec = pl.BlockSpec((None, LRU_BLOCKS, LRU_BD, LRU_BD), lambda l: (l, 0, 0, 0))
    ospec = pl.BlockSpec((None, D_LRU, D_LRU), lambda l: (l, 0, 0))
    return pl.pallas_call(
        body, name=name, grid=(wa.shape[0],), in_specs=[ispec, ispec], out_specs=[ospec, ospec],
        out_shape=[_sds((wa.shape[0], D_LRU, D_LRU), MXU_DTYPE)] * 2, compiler_params=_params(("parallel",)),
    )(wa, wx)


REP_NAMES = ["norm_mix", "conv_b", "gate_a_w", "gate_a_b", "gate_x_w", "gate_x_b", "lru_lambda", "lru_out_norm",
             "ret_out_norm", "norm_ffn", "norm_final"]


def kernel(x, meta_tokens, norm_mix, w_in, conv_w, conv_b, gate_a_w, gate_a_b, gate_x_w, gate_x_b, lru_lambda, lru_out_norm, ret_out_norm, w_out, norm_ffn, w_gate, w_up, w_down, norm_final, loss_target, m_meta_tokens, m_norm_mix, m_w_in, m_conv_w, m_conv_b, m_gate_a_w, m_gate_a_b, m_gate_x_w, m_gate_x_b, m_lru_lambda, m_lru_out_norm, m_ret_out_norm, m_w_out, m_norm_ffn, m_w_gate, m_w_up, m_w_down, m_norm_final, v_meta_tokens, v_norm_mix, v_w_in, v_conv_w, v_conv_b, v_gate_a_w, v_gate_a_b, v_gate_x_w, v_gate_x_b, v_lru_lambda, v_lru_out_norm, v_ret_out_norm, v_w_out, v_norm_ffn, v_w_gate, v_w_up, v_w_down, v_norm_final):
    xi, yi, ci = _place()
    dev = 4 * xi + 2 * yi + ci
    c_arr = jnp.reshape(ci, (1,)).astype(jnp.int32)
    dev_arr = jnp.reshape(dev, (1,)).astype(jnp.int32)

    meta_g, conv_g = all_gather([meta_tokens, conv_w], "ag_small")
    meta_full = jnp.transpose(meta_g, (1, 0, 2)).reshape(N_META, D)
    conv_full = jnp.transpose(conv_g, (1, 2, 0, 3)).reshape(DEPTH, CONV_W, D_LRU)
    tr_ = lambda a: jnp.transpose(a, (0, 2, 1))
    w_gate_t, m_w_gate_t, v_w_gate_t = tr_(w_gate), tr_(m_w_gate), tr_(v_w_gate)
    w_up_t, m_w_up_t, v_w_up_t = tr_(w_up), tr_(m_w_up), tr_(v_w_up)
    level1 = []
    token = meta_g
    for l in range(DEPTH):
        sel = jnp.stack([dev, jnp.int32(l)]).astype(jnp.int32)
        lands = to_wire(sel, w_in, w_gate_t, w_up_t, w_out, w_down, "to_wire")
        s1, r1, lands, token = ag_start(lands, token, f"ag_start_{l}")
        level1.append((s1, r1, lands))

    def as_weights(gi, gg, gu, go, gd):
        return dict(w_in=gi, w_gate=gg.reshape(D_FFP, D), w_up=gu.reshape(D_FFP, D), w_out=go.reshape(D, D),
                    w_down=gd.reshape(D_FFP, D))

    tables = _ret_tables()
    row = lambda a: a.reshape(1, -1)

    h = jnp.concatenate([jnp.zeros((PAD, D), F32), meta_full, x[0]], axis=0)
    saved, gathered = [], []
    s1, r1, lands = level1[0]
    s2, r2, first, order = ag_forward(s1[:4], r1[:4], lands[:1], token, "ag_forward_0_w_in")
    w_in_next = ag_finish(s2, r2, first, h, "ag_finish_0_w_in")[0]
    wa_dense, wx_dense = block_diag(gate_a_w, gate_x_w, "block_diag")
    for l in range(DEPTH):
        small = dict(cw=conv_full[l], cb=row(conv_b[l]), wa=wa_dense[l], ba=row(gate_a_b[l]),
                     wx=wx_dense[l], bx=row(gate_x_b[l]), lam=row(lru_lambda[l]),
                     gain=row(lru_out_norm[l]))
        s1, r1, lands = level1[l]
        hn1 = rmsnorm_fwd(h, row(norm_mix[l]), "rms_fwd")
        proj = mm_blocked_nn(hn1, w_in_next, F32, "proj")
        ylru, hst = lru_fwd(proj, name="lru_fwd", **small)
        s2, r2, rest, order = ag_forward(s1[4:], r1[4:], lands[1:], ylru, f"ag_forward_{l}_rest")
        ymix, states = ret_fwd(proj, ylru, tables, row(ret_out_norm[l]), order, "ret_fwd")
        w = as_weights(w_in_next, *ag_finish(s2, r2, rest, ymix, f"ag_finish_{l}_rest"))
        gathered.append(w)
        h_mid = mm_nn_res(ymix, w["w_out"], h, order, "out_proj")
        hn2 = rmsnorm_fwd(h_mid, row(norm_ffn[l]), "rms_fwd")
        act_dgate, act_dup, act = ffn_up(hn2, w["w_gate"], w["w_up"], "ffn_up")
        if l + 1 < DEPTH:
            s1n, r1n, landsn = level1[l + 1]
            s2, r2, first, order = ag_forward(s1n[:4], r1n[:4], landsn[:1], act, f"ag_forward_{l + 1}_w_in")
        h_out = mm_nn_res(act, w["w_down"], h_mid, order, "ffn_down")
        if l + 1 < DEPTH:
            w_in_next = ag_finish(s2, r2, first, h_out, f"ag_finish_{l + 1}_w_in")[0]
        saved.append(dict(h=h, hn1=hn1, proj=proj, hst=hst, states=states, ymix=ymix, h_mid=h_mid, hn2=hn2, act_dgate=act_dgate, act_dup=act_dup,
                          act=act, small=small))
        h = h_out

    loss_p, dh, dh_b, g_norm_final = loss_head(h, row(norm_final), loss_target[0], "loss_head")

    small_v = [None] * DEPTH
    small_w = [None] * DEPTH
    inflight = []
    sib = None
    order = loss_p

    def sibling_done(l, tag, names, sib, after):
        parts, got = rs_sibling_wait(*sib, after, f"rs_sibling_wait_{tag}")
        sums = pair_sum(parts, got, c_arr, "pair_sum")
        flying, started = rs_chips_start(sums, f"rs_chips_start_{tag}")
        inflight.append((l, tag, names, flying))
        return started

    for l in reversed(range(DEPTH)):
        w, s = gathered[l], saved[l]
        dgate, dup = ffn_down_bwd(dh_b, w["w_down"], s["act_dgate"], s["act_dup"], order, "ffn_down_bwd")
        dwd = mm_tn(s["act"], dh_b, PAIR, order, "dw_down").reshape(NDEV, FF_SHP, D)
        dwg, dwu = (g.reshape(NDEV, FF_SHP, D) for g in mm_tn_two(dgate, dup, s["hn2"], PAIR, order, "dw_rows"))
        split = l <= 1
        if split:
            ffn_sib, order = rs_sibling_start([dwg, dwu, dwd], f"rs_sibling_start_{l}_ffn")
        dhn2 = mm_rows_nn([(dgate, w["w_gate"]), (dup, w["w_up"])], order, "ffn_up_bwd")
        if sib is not None:
            order = sibling_done(l + 1, sib_tag, sib_names, sib, dhn2)
        dh_mid, dh_mid_b, g_norm_ffn = rmsnorm_bwd(s["h_mid"], row(norm_ffn[l]), dhn2, dh, "rms_bwd")
        dymix, dwo = out_proj_bwd(dh_mid_b, w["w_out"], s["ymix"], order, "out_proj_bwd")
        dwo = dwo.reshape(NDEV, OUT_SH, D)
        if split:
            order = sibling_done(l, f"{l}_ffn", ("w_gate", "w_up", "w_down"), ffn_sib, dymix)
        dproj, lvec, dwa, dwx, g_ret_norm = mix_bwd(s["proj"], s["hst"], s["states"], dymix, tables=tables,
                                                    ret_gain=row(ret_out_norm[l]), after=order, name="mix_bwd", **s["small"])
        dwi = mm_tn_blocked(s["hn1"], dproj, "dw_blocked")
        dhn1 = mm_blocked_nt([(dproj, w["w_in"])], order, "proj_bwd")
        dh, dh_b, g_norm_mix = rmsnorm_bwd(s["h"], row(norm_mix[l]), dhn1, dh_mid, "rms_bwd")

        g_fin, loss_part = (g_norm_final, loss_p) if l == 0 else (jnp.zeros((1, D), F32), jnp.zeros((8, 128), F32))
        small_v[l], small_w[l] = small_grads(lvec, g_ret_norm, g_norm_mix, g_norm_ffn, g_fin, loss_part, dwa, dwx,
                                             "small_grads")
        if split:
            sib_tag, sib_names = f"{l}_mix", ("w_in", "w_out")
            sib, order = rs_sibling_start([dwi, dwo], f"rs_sibling_start_{l}_mix")
        else:
            sib_tag, sib_names = str(l), ("w_in", "w_gate", "w_up", "w_out", "w_down")
            sib, order = rs_sibling_start([dwi, dwg, dwu, dwo, dwd], f"rs_sibling_start_{l}")
        if l == 1:
            early = place_blocks(dev_arr, [jnp.stack(small_v[1:]), jnp.stack(small_w[1:])], "place_grads")
            early_sems = ag_start(early, order, "ag_start_grads")
            order = early_sems[3]

    grad_x = dh[X0:][None]
    g_meta = dh[PAD:X0]

    late = all_gather([small_v[0], small_w[0], g_meta], "ag_grads")
    s2, r2, lands, _ = ag_forward(early_sems[0], early_sems[1], early_sems[2], dh, "ag_forward_grads")
    gath_early = ag_finish(s2, r2, lands, late[0], "ag_finish_grads")
    sibling_done(0, sib_tag, sib_names, sib, late[0])
    v0, w0, meta_sum, v123, w123 = sum_devices(list(late) + list(gath_early), "sum_devices")
    loss = v0[16, 0]
    vecs = jnp.concatenate([v0[None], v123])
    gws = jnp.concatenate([w0[None], w123])
    blocks = (DEPTH, LRU_BLOCKS, LRU_BD)
    small_g = dict(
        conv_w=lax.dynamic_slice_in_dim(vecs[:, 0:CONV_W], dev * (D_LRU // NDEV), D_LRU // NDEV, axis=2),
        conv_b=vecs[:, 4], gate_a_b=vecs[:, 5].reshape(blocks), gate_x_b=vecs[:, 6].reshape(blocks),
        lru_lambda=vecs[:, 7], lru_out_norm=vecs[:, 8], ret_out_norm=vecs[:, 9],
        norm_mix=vecs[:, 10:12].reshape(DEPTH, D), norm_ffn=vecs[:, 12:14].reshape(DEPTH, D),
        norm_final=v0[14:16].reshape(1, D),
        gate_a_w=gws[:, :D_LRU].reshape(blocks + (LRU_BD,)), gate_x_w=gws[:, D_LRU:].reshape(blocks + (LRU_BD,)),
        meta_tokens=lax.dynamic_slice_in_dim(meta_sum, dev * (D // NDEV), D // NDEV, axis=1))
    given = dict(norm_mix=(norm_mix, m_norm_mix, v_norm_mix), conv_b=(conv_b, m_conv_b, v_conv_b),
                 gate_a_w=(gate_a_w, m_gate_a_w, v_gate_a_w), gate_a_b=(gate_a_b, m_gate_a_b, v_gate_a_b),
                 gate_x_w=(gate_x_w, m_gate_x_w, v_gate_x_w), gate_x_b=(gate_x_b, m_gate_x_b, v_gate_x_b),
                 lru_lambda=(lru_lambda, m_lru_lambda, v_lru_lambda), lru_out_norm=(lru_out_norm, m_lru_out_norm, v_lru_out_norm),
                 ret_out_norm=(ret_out_norm, m_ret_out_norm, v_ret_out_norm), norm_ffn=(norm_ffn, m_norm_ffn, v_norm_ffn),
                 norm_final=tuple(a.reshape(1, D) for a in (norm_final, m_norm_final, v_norm_final)),
                 conv_w=(conv_w, m_conv_w, v_conv_w), meta_tokens=(meta_tokens, m_meta_tokens, v_meta_tokens))
    small_names = REP_NAMES + ["conv_w", "meta_tokens"]
    upd = adamw_small([small_g[n] for n in small_names], *[[given[n][k] for n in small_names] for k in range(3)],
                      "adamw_small")
    small_out = [dict(zip(small_names, u)) for u in upd]
    for d_ in [small_g] + small_out:
        d_["norm_final"] = d_["norm_final"].reshape(D)

    arrived = {}

    def wait_for(entries, after):
        for l, tag, names, flying in entries:
            sums, recv = rs_chips_wait(*flying, after, f"rs_chips_wait_{tag}")
            for i, n in enumerate(names):
                arrived[l, n] = (recv[i], sums[i])

    chip = jnp.reshape(2 * xi + yi, (1,)).astype(jnp.int32)

    def finish(wname, w_, m_, v_, tr):
        return adamw_big([arrived[l, wname][0] for l in range(DEPTH)], [arrived[l, wname][1] for l in range(DEPTH)], chip,
                         w_, m_, v_, tr, "adamw_" + wname)

    wait_for(inflight[:-1], upd[0][0])
    o_gate = [tr_(o) for o in finish("w_gate", w_gate_t, m_w_gate_t, v_w_gate_t, 32)]
    o_up = [tr_(o) for o in finish("w_up", w_up_t, m_w_up_t, v_w_up_t, 32)]
    o_down = finish("w_down", w_down, m_w_down, v_w_down, 32)
    wait_for(inflight[-1:], o_down[0])
    o_in = finish("w_in", w_in, m_w_in, v_w_in, 256)
    o_out = finish("w_out", w_out, m_w_out, v_w_out, 64)

    bigs = dict(w_in=o_in, w_out=o_out, w_gate=o_gate, w_up=o_up, w_down=o_down)
    order = ["meta_tokens", "norm_mix", "w_in", "conv_w", "conv_b", "gate_a_w", "gate_a_b", "gate_x_w", "gate_x_b", "lru_lambda",
             "lru_out_norm", "ret_out_norm", "w_out", "norm_ffn", "w_gate", "w_up", "w_down", "norm_final"]
    grads = [bigs[n][0] if n in bigs else small_g[n] for n in order]
    rest = [[bigs[n][k + 1] if n in bigs else small_out[k][n] for n in order] for k in range(3)]
    return (loss, grad_x, *grads, *rest[0], *rest[1], *rest[2])
```

```python
import numpy as np
import jax
import jax.numpy as jnp
from jax import lax
from jax.experimental import pallas as pl
from jax.experimental.pallas import tpu as pltpu

F32, BF16 = jnp.float32, jnp.bfloat16
MXU_DTYPE = BF16
WIRE_DTYPE = BF16

D = 1024
SEQ = 2048
DEPTH = 4
N_META = 16
CH = 128
PAD = (-(SEQ + N_META)) % CH
T = SEQ + N_META + PAD
NCH = T // CH
X0 = PAD + N_META
D_LRU = 512
LRU_BLOCKS = 8
LRU_BD = 64
CONV_W = 4
LRU_C = 8.0
D_RET = 512
HEADS = 4
HD = 128
ROPE_BASE = 10000.0
D_IN = 3072
D_FF = 2816
NDEV = 8
IN_SH = D_IN // NDEV
FF_SH = D_FF // NDEV
FF_SHP = 384
D_FFP = NDEV * FF_SHP
OUT_SH = D // NDEV
EPS = 1e-6
TM = 544
VMEM_LIMIT = 56 * 2**20
MESH = pl.DeviceIdType.MESH

ADAM_LR, ADAM_B1, ADAM_B2, ADAM_EPS, ADAM_WD, ADAM_STEP = 0.001, 0.9, 0.999, 1e-08, 0.01, 10

NN = ((1,), (0,))
NT = ((1,), (1,))
TN = ((0,), (0,))


def _dot(a, b, dims):
    return lax.dot_general(a.astype(MXU_DTYPE), b.astype(MXU_DTYPE), (dims, ((), ())), preferred_element_type=F32)


def _sds(shape, dtype):
    return jax.ShapeDtypeStruct(shape, dtype)


def _params(sem=None):
    return pltpu.CompilerParams(dimension_semantics=sem, vmem_limit_bytes=VMEM_LIMIT)


def _full(shape):
    n = len(shape)
    return pl.BlockSpec(shape, lambda *_: (0,) * n)


def rmsnorm_fwd(h, gain, name):
    def body(h_ref, g_ref, o_ref):
        x = h_ref[...]
        ms = jnp.mean(x * x, axis=-1, keepdims=True)
        o_ref[...] = (x * lax.rsqrt(ms + EPS) * g_ref[...]).astype(o_ref.dtype)

    return pl.pallas_call(
        body, name=name, grid=(T // TM,),
        in_specs=[pl.BlockSpec((TM, D), lambda i: (i, 0)), _full((1, D))],
        out_specs=pl.BlockSpec((TM, D), lambda i: (i, 0)),
        out_shape=_sds((T, D), MXU_DTYPE), compiler_params=_params(("parallel",)),
    )(h, gain)


def rmsnorm_bwd(h, gain, dhn, dres, name):
    def body(h_ref, g_ref, dhn_ref, dres_ref, dh_ref, dhb_ref, dg_ref):
        x = h_ref[...]
        rstd = lax.rsqrt(jnp.mean(x * x, axis=-1, keepdims=True) + EPS)
        xhat = x * rstd
        dy = dhn_ref[...]
        dyg = dy * g_ref[...]
        dh = dres_ref[...] + rstd * (dyg - xhat * jnp.mean(dyg * xhat, axis=-1, keepdims=True))
        dh_ref[...] = dh
        dhb_ref[...] = dh.astype(dhb_ref.dtype)

        @pl.when(pl.program_id(0) == 0)
        def _():
            dg_ref[...] = jnp.zeros_like(dg_ref)
        dg_ref[...] += jnp.sum(dy * xhat, axis=0, keepdims=True)

    row = pl.BlockSpec((TM, D), lambda i: (i, 0))
    return pl.pallas_call(
        body, name=name, grid=(T // TM,),
        in_specs=[row, _full((1, D)), row, row],
        out_specs=[row, row, _full((1, D))],
        out_shape=[_sds((T, D), F32), _sds((T, D), MXU_DTYPE), _sds((1, D), F32)], compiler_params=_params(("arbitrary",)),
    )(h, gain, dhn, dres)


def loss_head(h, gain, target, name):
    def body(h_ref, g_ref, t_ref, loss_ref, dh_ref, dhb_ref, dg_ref):
        i = pl.program_id(0)

        @pl.when(i == 0)
        def _():
            loss_ref[...] = jnp.zeros_like(loss_ref)
            dg_ref[...] = jnp.zeros_like(dg_ref)
            dh_ref[...] = jnp.zeros_like(dh_ref)
            dhb_ref[...] = jnp.zeros_like(dhb_ref)

        @pl.when(i > 0)
        def _():
            x = h_ref[...]
            g = g_ref[...]
            rstd = lax.rsqrt(jnp.mean(x * x, axis=-1, keepdims=True) + EPS)
            xhat = x * rstd
            err = xhat * g - t_ref[...]
            loss_ref[...] += 0.5 * jnp.sum(jnp.mean(err * err, axis=-1, keepdims=True), axis=0, keepdims=True)
            dy = err * (1.0 / D)
            dyg = dy * g
            dh = rstd * (dyg - xhat * jnp.mean(dyg * xhat, axis=-1, keepdims=True))
            dh_ref[...] = dh
            dhb_ref[...] = dh.astype(dhb_ref.dtype)
            dg_ref[...] += jnp.sum(dy * xhat, axis=0, keepdims=True)

    row = pl.BlockSpec((CH, D), lambda i: (i, 0))
    return pl.pallas_call(
        body, name=name, grid=(NCH,),
        in_specs=[row, _full((1, D)), pl.BlockSpec((CH, D), lambda i: (jnp.maximum(i - 1, 0), 0))],
        out_specs=[_full((8, 128)), row, row, _full((1, D))],
        out_shape=[_sds((8, 128), F32), _sds((T, D), F32), _sds((T, D), MXU_DTYPE), _sds((1, D), F32)],
        compiler_params=_params(("arbitrary",)),
    )(h, gain, target)


PAIR = 2 * IN_SH
NPAIR = NDEV // 2
BN = 256
FB = 512


def _pair_cols(w_ref):
    return jnp.concatenate([w_ref[0], w_ref[1]], axis=1)


W_PAIR = lambda k: pl.BlockSpec((2, k, IN_SH), lambda j: (j, 0, 0))
COLS_PAIR = pl.BlockSpec((T, PAIR), lambda j: (0, j))
ANYSPEC = pl.BlockSpec(memory_space=pl.ANY)


def mm_blocked_nn(a, w, out_dtype, name):
    k = a.shape[1]

    def body(a_ref, w_ref, o_ref):
        o_ref[...] = _dot(a_ref[...], _pair_cols(w_ref), NN).astype(o_ref.dtype)

    return pl.pallas_call(
        body, name=name, grid=(NPAIR,),
        in_specs=[_full((T, k)), W_PAIR(k)], out_specs=COLS_PAIR,
        out_shape=_sds((T, NDEV * IN_SH), out_dtype), compiler_params=_params(("parallel",)),
    )(a, w)


def mm_nn_res(a, w, res, after, name):
    k = a.shape[1]

    def body(a_ref, w_ref, r_ref, after_ref, o_ref):
        del after_ref
        o_ref[...] = r_ref[...] + _dot(a_ref[...], w_ref[...], NN)

    col = pl.BlockSpec((T, BN), lambda j: (0, j))
    return pl.pallas_call(
        body, name=name, grid=(D // BN,),
        in_specs=[_full((T, k)), pl.BlockSpec((k, BN), lambda j: (0, j)), col, ANYSPEC], out_specs=col,
        out_shape=_sds((T, D), F32), compiler_params=_params(("parallel",)),
    )(a, w, res, after)


def ffn_up(hn, wg, wu, name):
    def body(a_ref, wg_ref, wu_ref, dg_ref, du_ref, act_ref):
        a = a_ref[...]
        for c in range(FB // BN):
            cols = slice(BN * c, BN * (c + 1))
            g = _dot(a, wg_ref[cols, :], NT)
            u = _dot(a, wu_ref[cols, :], NT)
            sg = jax.nn.sigmoid(g)
            silu = g * sg
            dg_ref[:, cols] = (u * (sg * (1.0 + g * (1.0 - sg)))).astype(dg_ref.dtype)
            du_ref[:, cols] = silu.astype(du_ref.dtype)
            act_ref[:, cols] = (silu * u).astype(act_ref.dtype)

    wspec = pl.BlockSpec((FB, D), lambda j: (j, 0))
    ospec = pl.BlockSpec((T, FB), lambda j: (0, j))
    return pl.pallas_call(
        body, name=name, grid=(D_FFP // FB,),
        in_specs=[_full((T, D)), wspec, wspec], out_specs=[ospec] * 3,
        out_shape=[_sds((T, D_FFP), MXU_DTYPE)] * 3, compiler_params=_params(("parallel",)),
    )(hn, wg, wu)


def ffn_down_bwd(dh, wd, dact_dgate, dact_dup, after, name):
    def body(dh_ref, wd_ref, g_ref, u_ref, after_ref, dg_ref, du_ref):
        del after_ref
        dh = dh_ref[...]
        for c in range(FB // BN):
            cols = slice(BN * c, BN * (c + 1))
            dact = _dot(dh, wd_ref[cols, :], NT)
            dg_ref[:, cols] = (dact * g_ref[:, cols].astype(F32)).astype(dg_ref.dtype)
            du_ref[:, cols] = (dact * u_ref[:, cols].astype(F32)).astype(du_ref.dtype)

    blk = pl.BlockSpec((T, FB), lambda j: (0, j))
    return pl.pallas_call(
        body, name=name, grid=(D_FFP // FB,),
        in_specs=[_full((T, D)), pl.BlockSpec((FB, D), lambda j: (j, 0)), blk, blk, ANYSPEC],
        out_specs=[blk, blk],
        out_shape=[_sds((T, D_FFP), MXU_DTYPE)] * 2, compiler_params=_params(("parallel",)),
    )(dh, wd, dact_dgate, dact_dup, after)


def mm_blocked_nt(pairs, after, name):
    n = len(pairs)

    def body(*refs):
        o_ref = refs[2 * n + 1]

        @pl.when(pl.program_id(0) == 0)
        def _():
            o_ref[...] = jnp.zeros_like(o_ref)
        for p in range(n):
            o_ref[...] += _dot(refs[2 * p][...], _pair_cols(refs[2 * p + 1]), NT)

    specs, args = [], []
    for a, w in pairs:
        specs += [COLS_PAIR, W_PAIR(D)]
        args += [a, w]
    return pl.pallas_call(
        body, name=name, grid=(NPAIR,), in_specs=specs + [ANYSPEC], out_specs=_full((T, D)),
        out_shape=_sds((T, D), F32), compiler_params=_params(("arbitrary",)),
    )(*args, after)


def mm_tn_two(a1, a2, b, bm, after, name):
    m = a1.shape[1]

    def body(a1_ref, a2_ref, b_ref, after_ref, o1_ref, o2_ref):
        del after_ref
        b = b_ref[...]
        o1_ref[...] = _dot(a1_ref[...], b, TN).astype(o1_ref.dtype)
        o2_ref[...] = _dot(a2_ref[...], b, TN).astype(o2_ref.dtype)

    blk = pl.BlockSpec((T, bm), lambda i: (0, i))
    out = pl.BlockSpec((bm, D), lambda i: (i, 0))
    return pl.pallas_call(
        body, name=name, grid=(m // bm,),
        in_specs=[blk, blk, _full((T, D)), ANYSPEC], out_specs=[out, out],
        out_shape=[_sds((m, D), WIRE_DTYPE)] * 2, compiler_params=_params(("parallel",)),
    )(a1, a2, b, after)


def out_proj_bwd(dh, w, ymix, after, name):
    def body(dh_ref, w_ref, y_ref, after_ref, dy_ref, dw_ref):
        del after_ref
        dh_ = dh_ref[...]
        dy_ref[...] = _dot(dh_, w_ref[...], NT)
        dw_ref[...] = _dot(y_ref[...], dh_, TN).astype(dw_ref.dtype)

    return pl.pallas_call(
        body, name=name, grid=(D // BN,),
        in_specs=[_full((T, D)), pl.BlockSpec((BN, D), lambda j: (j, 0)), pl.BlockSpec((T, BN), lambda j: (0, j)), ANYSPEC],
        out_specs=[pl.BlockSpec((T, BN), lambda j: (0, j)), pl.BlockSpec((BN, D), lambda j: (j, 0))],
        out_shape=[_sds((T, D), F32), _sds((D, D), WIRE_DTYPE)], compiler_params=_params(("parallel",)),
    )(dh, w, ymix, after)


def mm_rows_nn(pairs, after, name):
    n = len(pairs)

    def body(*refs):
        o_ref = refs[2 * n + 1]

        @pl.when(pl.program_id(0) == 0)
        def _():
            o_ref[...] = jnp.zeros_like(o_ref)
        for p in range(n):
            o_ref[...] += _dot(refs[2 * p][...], refs[2 * p + 1][...], NN)

    specs, args = [], []
    for a, w in pairs:
        specs += [pl.BlockSpec((T, FB), lambda j: (0, j)), pl.BlockSpec((FB, D), lambda j: (j, 0))]
        args += [a, w]
    return pl.pallas_call(
        body, name=name, grid=(D_FFP // FB,), in_specs=specs + [ANYSPEC], out_specs=_full((T, D)),
        out_shape=_sds((T, D), F32), compiler_params=_params(("arbitrary",)),
    )(*args, after)


def mm_tn_blocked(a, b, name):
    def body(a_ref, b_ref, o_ref):
        o = _dot(a_ref[...], b_ref[...], TN).astype(o_ref.dtype)
        o_ref[0] = o[:, :IN_SH]
        o_ref[1] = o[:, IN_SH:]

    return pl.pallas_call(
        body, name=name, grid=(NPAIR,),
        in_specs=[_full((T, D)), COLS_PAIR], out_specs=W_PAIR(D),
        out_shape=_sds((NDEV, D, IN_SH), WIRE_DTYPE), compiler_params=_params(("parallel",)),
    )(a, b)


def mm_tn(a, b, bm, after, name):
    m = a.shape[1]

    def body(a_ref, b_ref, after_ref, o_ref):
        del after_ref
        o_ref[...] = _dot(a_ref[...], b_ref[...], TN).astype(o_ref.dtype)

    return pl.pallas_call(
        body, name=name, grid=(m // bm,),
        in_specs=[pl.BlockSpec((T, bm), lambda i: (0, i)), _full((T, D)), ANYSPEC],
        out_specs=pl.BlockSpec((bm, D), lambda i: (i, 0)),
        out_shape=_sds((m, D), WIRE_DTYPE), compiler_params=_params(("parallel",)),
    )(a, b, after)


def _softplus_neg(lam):
    return jnp.maximum(-lam, 0.0) + jnp.log1p(jnp.exp(-jnp.abs(lam)))


def _lru_gates(pa, px, xc, lam):
    r = jax.nn.sigmoid(pa)
    ig = jax.nn.sigmoid(px)
    sp = _softplus_neg(lam)
    log_a = -LRU_C * r * sp
    a = jnp.exp(log_a)
    mult = jnp.sqrt(-jnp.tanh(log_a) * (a * a + 1.0))
    return a, mult * (ig * xc), (r, ig, sp, mult)


def _lru_gates_vjp(da, db, xc, lam, a, r, ig, sp, mult):
    dmult = db * (ig * xc)
    du = db * mult
    dlog_a = da * a - dmult * (a * a) / mult
    dr = dlog_a * (-LRU_C * sp)
    dlam = jnp.sum(dlog_a * (-LRU_C * r), axis=0, keepdims=True) * (-jax.nn.sigmoid(-lam))
    dpa = dr * (r * (1.0 - r))
    dpx = (du * xc) * (ig * (1.0 - ig))
    return dpa, dpx, du * ig, dlam


def _lru_out(h, g, gain):
    z = h * jax.nn.gelu(g)
    return z * lax.rsqrt(jnp.mean(z * z, axis=-1, keepdims=True) + EPS) * gain


def _conv_taps(x, xprev, row):
    taps = [x]
    for s in range(1, CONV_W):
        taps.append(jnp.where(row < s, pltpu.roll(xprev, s, 0), pltpu.roll(x, s, 0)))
    return taps


def _conv(taps, cw_ref, cb):
    xc = cb + cw_ref[CONV_W - 1:CONV_W, :] * taps[0]
    for s in range(1, CONV_W):
        xc = xc + cw_ref[CONV_W - 1 - s:CONV_W - s, :] * taps[s]
    return xc


def _lru_fwd_block(i, x_ref, g_ref, cw_ref, cb_ref, wa_ref, ba_ref, wx_ref, bx_ref, lam_ref, gain_ref, y_ref, h_ref,
                   xprev_scr, a_scr, b_scr, carry_scr):
    @pl.when(i == 0)
    def _():
        xprev_scr[...] = jnp.zeros_like(xprev_scr)
        carry_scr[...] = jnp.zeros_like(carry_scr)

    x = x_ref[...]
    row = lax.broadcasted_iota(jnp.int32, (CH, D_LRU), 0)
    xc = _conv(_conv_taps(x, xprev_scr[...], row), cw_ref, cb_ref[...])
    pa = _dot(xc, wa_ref[...], NN) + ba_ref[...]
    px = _dot(xc, wx_ref[...], NN) + bx_ref[...]
    a, b, _ = _lru_gates(pa, px, xc, lam_ref[...])
    a_scr[...] = a
    b_scr[...] = jnp.where(i * CH + row >= PAD, b, 0.0)
    h = carry_scr[...]
    for t in range(CH):
        h = a_scr[t:t + 1, :] * h + b_scr[t:t + 1, :]
        h_ref[t:t + 1, :] = h
    carry_scr[...] = h
    xprev_scr[...] = x
    y_ref[:, :D_LRU] = _lru_out(h_ref[...], g_ref[...], gain_ref[...]).astype(y_ref.dtype)


LRU_VEC_ROWS = 16


def _lru_bwd_block(ib, x_ref, xp_ref, g_ref, h_ref, hp_ref, dy_ref, cw_ref, cb_ref, wa_ref, ba_ref, wx_ref, bx_ref, lam_ref,
                   gain_ref, dp_ref, vec_ref, dwa_ref, dwx_ref, a_scr, dh_scr, g_scr, carry_scr, dxcn_scr):
    @pl.when(ib == NCH - 1)
    def _():
        carry_scr[...] = jnp.zeros_like(carry_scr)
        dxcn_scr[...] = jnp.zeros_like(dxcn_scr)
        vec_ref[...] = jnp.zeros_like(vec_ref)
        dwa_ref[...] = jnp.zeros_like(dwa_ref)
        dwx_ref[...] = jnp.zeros_like(dwx_ref)

    x = x_ref[...]
    row = lax.broadcasted_iota(jnp.int32, (CH, D_LRU), 0)
    valid = ib * CH + row >= PAD
    taps = _conv_taps(x, xp_ref[...], row)
    xc = _conv(taps, cw_ref, cb_ref[...])
    pa = _dot(xc, wa_ref[...], NN) + ba_ref[...]
    px = _dot(xc, wx_ref[...], NN) + bx_ref[...]
    a, _, gate_parts = _lru_gates(pa, px, xc, lam_ref[...])
    h = h_ref[...]
    _, vjp_out = jax.vjp(_lru_out, h, g_ref[...], gain_ref[...])
    dh, dg, dgain = vjp_out(dy_ref[:, :D_LRU].astype(F32))
    a_scr[...] = a
    dh_scr[...] = dh
    c = carry_scr[...]
    for t in range(CH - 1, -1, -1):
        gt = dh_scr[t:t + 1, :] + c
        g_scr[t:t + 1, :] = gt
        c = a_scr[t:t + 1, :] * gt
    carry_scr[...] = c
    gg = g_scr[...]
    hprev = jnp.where(row < 1, pltpu.roll(hp_ref[...], 1, 0), pltpu.roll(h, 1, 0))
    da = jnp.where(valid, gg * hprev, 0.0)
    db = jnp.where(valid, gg, 0.0)
    dpa, dpx, dxc, dlam = _lru_gates_vjp(da, db, xc, lam_ref[...], a, *gate_parts)
    dxc = dxc + _dot(dpa, wa_ref[...], NT) + _dot(dpx, wx_ref[...], NT)
    dwa_ref[...] += _dot(xc, dpa, TN)
    dwx_ref[...] += _dot(xc, dpx, TN)
    for s in range(CONV_W):
        vec_ref[CONV_W - 1 - s:CONV_W - s, :] += jnp.sum(dxc * taps[s], axis=0, keepdims=True)
    vec_ref[4:5, :] += jnp.sum(dxc, axis=0, keepdims=True)
    vec_ref[5:6, :] += jnp.sum(dpa, axis=0, keepdims=True)
    vec_ref[6:7, :] += jnp.sum(dpx, axis=0, keepdims=True)
    vec_ref[7:8, :] += dlam
    vec_ref[8:9, :] += dgain
    dxn = dxcn_scr[...]
    dx = cw_ref[CONV_W - 1:CONV_W, :] * dxc
    for s in range(1, CONV_W):
        ahead = jnp.where(row >= CH - s, pltpu.roll(dxn, CH - s, 0), pltpu.roll(dxc, CH - s, 0))
        dx = dx + cw_ref[CONV_W - 1 - s:CONV_W - s, :] * ahead
    dxcn_scr[...] = dxc
    dp_ref[:, :D_LRU] = jnp.where(valid, dx, 0.0).astype(dp_ref.dtype)
    dp_ref[:, D_LRU:2 * D_LRU] = dg.astype(dp_ref.dtype)


def _ret_tables():
    half = HD // 2
    pos = jnp.arange(T, dtype=F32) - float(PAD)
    inv = ROPE_BASE ** (-jnp.arange(half, dtype=F32) / half)
    ang = pos[:, None] * inv[None, :]
    cos = jnp.concatenate([jnp.cos(ang), jnp.cos(ang)], axis=-1)
    sin = jnp.concatenate([-jnp.sin(ang), jnp.sin(ang)], axis=-1)
    log_g = jnp.log(1.0 - 2.0 ** (-5.0 - jnp.arange(HEADS, dtype=F32)))
    idx = jnp.arange(CH, dtype=F32)
    diff = idx[:, None] - idx[None, :]
    dmask = jnp.where(diff[None] >= 0, jnp.exp(jnp.maximum(diff, 0.0)[None] * log_g[:, None, None]), 0.0)
    xi = jnp.exp((idx + 1.0)[None, :] * log_g[:, None])
    zeta = jnp.exp((CH - 1.0 - idx)[None, :] * log_g[:, None])
    xi = jnp.broadcast_to(xi[:, :, None], (HEADS, CH, HD))
    zeta = jnp.broadcast_to(zeta[:, :, None], (HEADS, CH, HD))
    return cos, sin, dmask, xi, zeta


def _chunk_decay():
    log_g = np.log(np.float32(1.0) - np.float32(2.0) ** (np.float32(-5.0) - np.arange(HEADS, dtype=np.float32)))
    return [float(v) for v in np.exp(np.float32(CH) * log_g.astype(np.float32))]


def _rope(x, cos, sin):
    return x * cos + pltpu.roll(x, HD // 2, 1) * sin


def mix_fwd(proj, cw, cb, wa, ba, wx, bx, lam, gain, tables, ret_gain, after, name):
    cos, sin, dmask, xi, zeta = tables
    gch = _chunk_decay()
    scale = HD ** -0.5

    def body(x_ref, gl_ref, cw_ref, cb_ref, wa_ref, ba_ref, wx_ref, bx_ref, lam_ref, lgain_ref,
             q_ref, k_ref, v_ref, g_ref, cos_ref, sin_ref, dm_ref, xi_ref, zt_ref, gain_ref, after_ref,
             y_ref, h_ref, st_ref, xprev_scr, a_scr, b_scr, carry_scr, s_scr):
        del after_ref

        @pl.when(pl.program_id(0) == 0)
        def _():
            s_scr[...] = jnp.zeros_like(s_scr)

        _lru_fwd_block(pl.program_id(0), x_ref, gl_ref, cw_ref, cb_ref, wa_ref, ba_ref, wx_ref, bx_ref, lam_ref, lgain_ref,
                       y_ref, h_ref, xprev_scr, a_scr, b_scr, carry_scr)
        cs, sn = cos_ref[...], sin_ref[...]
        hs = range(HEADS)
        sl = [slice(HD * h, HD * (h + 1)) for h in hs]
        qr = [_rope(q_ref[:, sl[h]], cs, sn).astype(MXU_DTYPE) for h in hs]
        kf = [_rope(k_ref[:, sl[h]], cs, sn) * scale for h in hs]
        kr = [kf[h].astype(MXU_DTYPE) for h in hs]
        v = [v_ref[:, sl[h]].astype(MXU_DTYPE) for h in hs]
        s = [s_scr[h] for h in hs]
        for h in hs:
            st_ref[h] = s[h]
        sc = [_dot(qr[h], kr[h], NT) * dm_ref[h] for h in hs]
        cross = [_dot(qr[h], s[h], NN) * xi_ref[h] for h in hs]
        for h in hs:
            s_scr[h] = s[h] * gch[h] + _dot(kf[h] * zt_ref[h], v[h], TN)
        y = [_dot(sc[h], v[h], NN) + cross[h] for h in hs]
        yc = [y[h] - jnp.mean(y[h], axis=-1, keepdims=True) for h in hs]
        yn = [yc[h] * lax.rsqrt(jnp.mean(yc[h] * yc[h], axis=-1, keepdims=True) + EPS) for h in hs]
        for h in hs:
            so = slice(D_LRU + HD * h, D_LRU + HD * (h + 1))
            y_ref[:, so] = (jax.nn.silu(g_ref[:, sl[h]]) * (yn[h] * gain_ref[:, sl[h]])).astype(y_ref.dtype)

    def col(c):
        return pl.BlockSpec((CH, D_RET), lambda n: (n, c))

    tab = pl.BlockSpec((CH, HD), lambda n: (n, 0))
    cst = _full((HEADS, CH, HD))
    vec = _full((1, D_LRU))
    mat = _full((D_LRU, D_LRU))
    blockbuf = pltpu.VMEM((CH, D_LRU), F32)
    return pl.pallas_call(
        body, name=name, grid=(NCH,),
        in_specs=[col(0), col(1), _full((CONV_W, D_LRU)), vec, mat, vec, mat, vec, vec, vec,
                  col(2), col(3), col(4), col(5), tab, tab, cst, cst, cst, _full((1, D_RET)),
                  pl.BlockSpec(memory_space=pl.ANY)],
        out_specs=[pl.BlockSpec((CH, D), lambda n: (n, 0)), col(0), pl.BlockSpec((None, HEADS, HD, HD), lambda n: (n, 0, 0, 0))],
        out_shape=[_sds((T, D), MXU_DTYPE), _sds((T, D_LRU), F32), _sds((NCH, HEADS, HD, HD), F32)],
        scratch_shapes=[blockbuf, blockbuf, blockbuf, pltpu.VMEM((1, D_LRU), F32), pltpu.VMEM((HEADS, HD, HD), F32)],
        compiler_params=_params(("arbitrary",)),
    )(proj, proj, cw, cb, wa, ba, wx, bx, lam, gain, proj, proj, proj, proj, cos, sin, dmask, xi, zeta, ret_gain, after)


def mix_bwd(proj, hst, states, dymix, cw, cb, wa, ba, wx, bx, lam, gain, tables, ret_gain, after, name):
    cos, sin, dmask, xi, zeta = tables
    gch = _chunk_decay()
    scale = HD ** -0.5
    last = NCH - 1

    def body(x_ref, xp_ref, gl_ref, h_ref, hp_ref, cw_ref, cb_ref, wa_ref, ba_ref, wx_ref, bx_ref, lam_ref, lgain_ref,
             q_ref, k_ref, v_ref, g_ref, st_ref, dy_ref, cos_ref, sin_ref, dm_ref, xi_ref, zt_ref, gain_ref, after_ref,
             dp_ref, vec_ref, dwa_ref, dwx_ref, dgain_ref, a_scr, dh_scr, g_scr, carry_scr, dxcn_scr, ds_scr):
        del after_ref

        @pl.when(pl.program_id(0) == 0)
        def _():
            ds_scr[...] = jnp.zeros_like(ds_scr)
            dgain_ref[...] = jnp.zeros_like(dgain_ref)

        _lru_bwd_block(last - pl.program_id(0), x_ref, xp_ref, gl_ref, h_ref, hp_ref, dy_ref, cw_ref, cb_ref, wa_ref, ba_ref,
                       wx_ref, bx_ref, lam_ref, lgain_ref, dp_ref, vec_ref, dwa_ref, dwx_ref, a_scr, dh_scr, g_scr, carry_scr,
                       dxcn_scr)
        cs, sn = cos_ref[...], sin_ref[...]
        hs = range(HEADS)
        sl = [slice(HD * h, HD * (h + 1)) for h in hs]

        def out(j, h):
            return slice(2 * D_LRU + j * D_RET + HD * h, 2 * D_LRU + j * D_RET + HD * (h + 1))

        b16 = lambda xs: [x.astype(MXU_DTYPE) for x in xs]
        qr = b16([_rope(q_ref[:, sl[h]], cs, sn) for h in hs])
        kf = [_rope(k_ref[:, sl[h]], cs, sn) * scale for h in hs]
        kr = b16(kf)
        kz = b16([kf[h] * zt_ref[h] for h in hs])
        v = b16([v_ref[:, sl[h]] for h in hs])
        s = b16([st_ref[h] for h in hs])
        ds = [ds_scr[h] for h in hs]
        dsb = b16(ds)
        sc = [_dot(qr[h], kr[h], NT) * dm_ref[h] for h in hs]
        scb = b16(sc)
        y = [_dot(scb[h], v[h], NN) + _dot(qr[h], s[h], NN) * xi_ref[h] for h in hs]
        yc = [y[h] - jnp.mean(y[h], axis=-1, keepdims=True) for h in hs]
        rstd = [lax.rsqrt(jnp.mean(yc[h] * yc[h], axis=-1, keepdims=True) + EPS) for h in hs]
        yn = [yc[h] * rstd[h] for h in hs]
        dy = []
        for h in hs:
            g = g_ref[:, sl[h]]
            gain = gain_ref[:, sl[h]]
            sg = jax.nn.sigmoid(g)
            silu = g * sg
            dout = dy_ref[:, D_LRU + HD * h:D_LRU + HD * (h + 1)].astype(F32)
            dgain_ref[:, sl[h]] += jnp.sum(dout * silu * yn[h], axis=0, keepdims=True)
            dp_ref[:, out(3, h)] = (dout * yn[h] * gain * (sg * (1.0 + g * (1.0 - sg)))).astype(dp_ref.dtype)
            dyn = dout * silu * gain
            dy.append(rstd[h] * (dyn - jnp.mean(dyn, axis=-1, keepdims=True)
                                 - yn[h] * jnp.mean(dyn * yn[h], axis=-1, keepdims=True)))
        dyb = b16(dy)
        dqs = b16([dy[h] * xi_ref[h] for h in hs])
        dp = b16([_dot(dyb[h], v[h], NT) * dm_ref[h] for h in hs])
        dv = [_dot(scb[h], dyb[h], TN) + _dot(kz[h], dsb[h], NN) for h in hs]
        dqr = [_dot(dp[h], kr[h], NN) + _dot(dqs[h], s[h], NT) for h in hs]
        dkr = [_dot(dp[h], qr[h], TN) + _dot(v[h], dsb[h], NT) * zt_ref[h] for h in hs]
        for h in hs:
            ds_scr[h] = gch[h] * ds[h] + _dot(qr[h], dqs[h], TN)
        for h in hs:
            dp_ref[:, out(0, h)] = (dqr[h] * cs + pltpu.roll(dqr[h] * sn, HD // 2, 1)).astype(dp_ref.dtype)
            dp_ref[:, out(1, h)] = ((dkr[h] * cs + pltpu.roll(dkr[h] * sn, HD // 2, 1)) * scale).astype(dp_ref.dtype)
            dp_ref[:, out(2, h)] = dv[h].astype(dp_ref.dtype)

    def col(c, shift=0):
        return pl.BlockSpec((CH, D_RET), lambda n: (jnp.maximum(last - n - shift, 0), c))

    tab = pl.BlockSpec((CH, HD), lambda n: (last - n, 0))
    cst = _full((HEADS, CH, HD))
    vec = _full((1, D_LRU))
    mat = _full((D_LRU, D_LRU))
    blockbuf = pltpu.VMEM((CH, D_LRU), F32)
    return pl.pallas_call(
        body, name=name, grid=(NCH,),
        in_specs=[col(0), col(0, 1), col(1), col(0), col(0, 1), _full((CONV_W, D_LRU)), vec, mat, vec, mat, vec, vec, vec,
                  col(2), col(3), col(4), col(5), pl.BlockSpec((None, HEADS, HD, HD), lambda n: (last - n, 0, 0, 0)),
                  pl.BlockSpec((CH, D), lambda n: (last - n, 0)), tab, tab, cst, cst, cst, _full((1, D_RET)),
                  pl.BlockSpec(memory_space=pl.ANY)],
        out_specs=[pl.BlockSpec((CH, D_IN), lambda n: (last - n, 0)), _full((LRU_VEC_ROWS, D_LRU)), mat, mat,
                   _full((1, D_RET))],
        out_shape=[_sds((T, D_IN), MXU_DTYPE), _sds((LRU_VEC_ROWS, D_LRU), F32), _sds((D_LRU, D_LRU), F32),
                   _sds((D_LRU, D_LRU), F32), _sds((1, D_RET), F32)],
        scratch_shapes=[blockbuf, blockbuf, blockbuf, pltpu.VMEM((1, D_LRU), F32), blockbuf,
                        pltpu.VMEM((HEADS, HD, HD), F32)],
        compiler_params=_params(("arbitrary",)),
    )(proj, proj, proj, hst, hst, cw, cb, wa, ba, wx, bx, lam, gain, proj, proj, proj, proj, states, dymix,
      cos, sin, dmask, xi, zeta, ret_gain, after)


HBM = pl.BlockSpec(memory_space=pltpu.HBM)


def _place():
    return lax.axis_index("x"), lax.axis_index("y"), lax.axis_index("c")


def all_gather(arrs, name):
    n = len(arrs)

    def body(*refs):
        ins, outs = refs[:n], refs[n:2 * n]
        send_sems, recv_sems, local_sems = refs[2 * n:]
        x, y, c = _place()
        me, sibling = (x, y, c), (x, y, 1 - c)
        chips = [(1 - x, y), (x, 1 - y), (1 - x, 1 - y)]

        def copy(a, k, block, to, src=None):
            px, py, pc = block
            dst = outs[a].at[4 * px + 2 * py + pc]
            return pltpu.make_async_remote_copy(
                src_ref=dst if src is None else src, dst_ref=dst, send_sem=send_sems.at[a, k], recv_sem=recv_sems.at[a, k],
                device_id=to, device_id_type=MESH)

        mine = [pltpu.make_async_copy(ins[a], outs[a].at[4 * x + 2 * y + c], local_sems.at[a]) for a in range(n)]
        for cp in mine:
            cp.start()
        first = []
        for a in range(n):
            first.append(copy(a, 0, me, sibling, src=ins[a]))
            first += [copy(a, 1 + j, me, (*chip, c), src=ins[a]) for j, chip in enumerate(chips)]
        for cp in first:
            cp.start()
        passed = []
        for j, chip in enumerate(chips):
            for a in range(n):
                copy(a, 1 + j, (*chip, c), me).wait_recv()
                passed.append(copy(a, 4 + j, (*chip, c), sibling))
                passed[-1].start()
        for a in range(n):
            copy(a, 0, sibling, me).wait_recv()
            for j, chip in enumerate(chips):
                copy(a, 4 + j, (*chip, 1 - c), me).wait_recv()
        for cp in first + passed:
            cp.wait_send()
        for cp in mine:
            cp.wait()

    return pl.pallas_call(
        body, name=name,
        in_specs=[HBM] * n, out_specs=[HBM] * n,
        out_shape=[_sds((NDEV,) + a.shape, a.dtype) for a in arrs],
        scratch_shapes=[pltpu.SemaphoreType.DMA((n, 7)), pltpu.SemaphoreType.DMA((n, 7)), pltpu.SemaphoreType.DMA((n,))],
    )(*arrs)


SEM = pl.BlockSpec(memory_space=pltpu.SEMAPHORE)
ANY = pl.BlockSpec(memory_space=pl.ANY)
EFFECT = pltpu.SideEffectType.DATAFLOW_SIDE_EFFECTING


def _hbm(a):
    return pltpu.with_memory_space_constraint(a, pltpu.HBM)


def _hbm_like(arrs):
    return [pltpu.HBM(a.shape, a.dtype) for a in arrs]


def _dma_sems(count):
    return [pltpu.SemaphoreType.DMA(())] * count


def _ag_copy(lands, send_sems, recv_sems, per):
    def copy(a, k, block, to, src=None):
        px, py, pc = block
        dst = lands[a].at[4 * px + 2 * py + pc]
        return pltpu.make_async_remote_copy(
            src_ref=dst if src is None else src, dst_ref=dst, send_sem=send_sems[a * per + k], recv_sem=recv_sems[a * per + k],
            device_id=to, device_id_type=MESH)
    return copy


def to_wire(sel, w_in, w_gate, w_up, w_out, w_down, name):
    ffpad = FF_SHP - FF_SH

    def body(sel_ref, i_ref, g_ref, u_ref, o_ref, d_ref, oi, og, ou, oo, od):
        del sel_ref
        oi[...] = i_ref[...].astype(oi.dtype)
        oo[...] = o_ref[...].astype(oo.dtype)
        for src, dst in ((g_ref, og), (u_ref, ou), (d_ref, od)):
            dst[:FF_SH, :] = src[...].astype(dst.dtype)
            dst[FF_SH:, :] = jnp.zeros((ffpad, D), dst.dtype)

    shapes_in = [(D, IN_SH), (FF_SH, D), (FF_SH, D), (OUT_SH, D), (FF_SH, D)]
    shapes_out = [(D, IN_SH), (FF_SHP, D), (FF_SHP, D), (OUT_SH, D), (FF_SHP, D)]
    return pl.pallas_call(
        body, name=name,
        grid_spec=pltpu.PrefetchScalarGridSpec(
            num_scalar_prefetch=1, grid=(1,),
            in_specs=[pl.BlockSpec((None,) + s, lambda i, sel_ref: (sel_ref[1], 0, 0)) for s in shapes_in],
            out_specs=[pl.BlockSpec((None,) + s, lambda i, sel_ref: (sel_ref[0], 0, 0)) for s in shapes_out]),
        out_shape=[_sds((NDEV,) + s, WIRE_DTYPE) for s in shapes_out], compiler_params=_params(("arbitrary",)),
    )(sel, w_in, w_gate, w_up, w_out, w_down)


def place_blocks(sel, arrs, name):
    n = len(arrs)

    def body(sel_ref, *refs):
        del sel_ref
        for a in range(n):
            refs[n + a][...] = refs[a][...]

    def whole(a):
        nd = a.ndim
        return pl.BlockSpec(a.shape, lambda i, sel_ref: (0,) * nd)

    def mine(a):
        nd = a.ndim
        return pl.BlockSpec((None,) + a.shape, lambda i, sel_ref: (sel_ref[0],) + (0,) * nd)

    return pl.pallas_call(
        body, name=name,
        grid_spec=pltpu.PrefetchScalarGridSpec(
            num_scalar_prefetch=1, grid=(1,), in_specs=[whole(a) for a in arrs], out_specs=[mine(a) for a in arrs]),
        out_shape=[_sds((NDEV,) + a.shape, a.dtype) for a in arrs], compiler_params=_params(("arbitrary",)),
    )(sel, *arrs)


def ag_start(lands, after, name):
    n = len(lands)
    ns = 4 * n

    def body(*refs):
        lnd = refs[:n]
        send_sems, recv_sems = refs[n + 1:n + 1 + ns], refs[n + 1 + ns:n + 1 + 2 * ns]
        token = refs[-1]
        x, y, c = _place()
        me, sibling = (x, y, c), (x, y, 1 - c)
        chips = [(1 - x, y), (x, 1 - y), (1 - x, 1 - y)]
        copy = _ag_copy(lnd, send_sems, recv_sems, 4)
        for a in range(n):
            copy(a, 0, me, sibling).start()
            for j, chip in enumerate(chips):
                copy(a, 1 + j, me, (*chip, c)).start()
        token[...] = jnp.zeros_like(token)

    outs = pl.pallas_call(
        body, name=name,
        in_specs=[HBM] * n + [ANY],
        out_specs=[SEM] * (2 * ns) + [HBM] * n + [pl.BlockSpec(memory_space=pltpu.VMEM)],
        out_shape=_dma_sems(2 * ns) + _hbm_like(lands) + [_sds((8, 128), F32)],
        input_output_aliases={i: 2 * ns + i for i in range(n)},
        compiler_params=pltpu.CompilerParams(has_side_effects=EFFECT),
    )(*[_hbm(a) for a in lands], after)
    return outs[:ns], outs[ns:2 * ns], outs[2 * ns:2 * ns + n], outs[-1]


def ag_forward(send_sems, recv_sems, lands, after, name):
    n = len(lands)
    n1, n2 = 4 * n, 3 * n

    def body(*refs):
        lnd = refs[:n]
        o = n
        s1, r1 = refs[o:o + n1], refs[o + n1:o + 2 * n1]
        o += 2 * n1 + 1
        s2, r2 = refs[o:o + n2], refs[o + n2:o + 2 * n2]
        token = refs[-1]
        token[...] = jnp.zeros_like(token)
        x, y, c = _place()
        me, sibling = (x, y, c), (x, y, 1 - c)
        chips = [(1 - x, y), (x, 1 - y), (1 - x, 1 - y)]
        copy1 = _ag_copy(lnd, s1, r1, 4)
        copy2 = _ag_copy(lnd, s2, r2, 3)
        for j, chip in enumerate(chips):
            for a in range(n):
                copy1(a, 1 + j, (*chip, c), me).wait_recv()
                copy2(a, j, (*chip, c), sibling).start()
        for a in range(n):
            copy1(a, 0, sibling, me).wait_recv()
            copy1(a, 0, me, sibling).wait_send()
            for j, chip in enumerate(chips):
                copy1(a, 1 + j, me, (*chip, c)).wait_send()

    outs = pl.pallas_call(
        body, name=name,
        in_specs=[HBM] * n + [SEM] * (2 * n1) + [ANY],
        out_specs=[SEM] * (2 * n2) + [HBM] * n + [pl.BlockSpec(memory_space=pltpu.VMEM)],
        out_shape=_dma_sems(2 * n2) + _hbm_like(lands) + [_sds((8, 128), F32)],
        input_output_aliases={i: 2 * n2 + i for i in range(n)},
        compiler_params=pltpu.CompilerParams(has_side_effects=EFFECT),
    )(*lands, *send_sems, *recv_sems, after)
    return outs[:n2], outs[n2:2 * n2], outs[2 * n2:2 * n2 + n], outs[-1]


def ag_finish(send_sems, recv_sems, lands, after, name):
    n = len(lands)
    n2 = 3 * n

    def body(*refs):
        lnd = refs[:n]
        s2, r2 = refs[n:n + n2], refs[n + n2:n + 2 * n2]
        x, y, c = _place()
        me, sibling = (x, y, c), (x, y, 1 - c)
        chips = [(1 - x, y), (x, 1 - y), (1 - x, 1 - y)]
        copy2 = _ag_copy(lnd, s2, r2, 3)
        for a in range(n):
            for j, chip in enumerate(chips):
                copy2(a, j, (*chip, c), sibling).wait_send()
                copy2(a, j, (*chip, 1 - c), me).wait_recv()

    outs = pl.pallas_call(
        body, name=name,
        in_specs=[HBM] * n + [SEM] * (2 * n2) + [ANY],
        out_specs=[HBM] * n, out_shape=_hbm_like(lands),
        input_output_aliases={i: i for i in range(n)},
        compiler_params=pltpu.CompilerParams(has_side_effects=EFFECT),
    )(*lands, *send_sems, *recv_sems, after)
    return list(outs)


def rs_sibling_start(arrs, name):
    n = len(arrs)
    ns = 4 * n
    lands = [lax.empty((4,) + a.shape[1:], a.dtype) for a in arrs]

    def body(*refs):
        ins, lnd = refs[:n], refs[n:2 * n]
        send_sems, recv_sems = refs[2 * n:2 * n + ns], refs[2 * n + ns:2 * n + 2 * ns]
        x, y, c = _place()
        sibling = (x, y, 1 - c)
        for a in range(n):
            for p in range(4):
                pltpu.make_async_remote_copy(
                    src_ref=ins[a].at[2 * p + 1 - c], dst_ref=lnd[a].at[p], send_sem=send_sems[4 * a + p],
                    recv_sem=recv_sems[4 * a + p], device_id=sibling, device_id_type=MESH).start()
        refs[-1][...] = jnp.zeros_like(refs[-1])

    outs = pl.pallas_call(
        body, name=name,
        in_specs=[HBM] * (2 * n), out_specs=[SEM] * (2 * ns) + [HBM] * (2 * n) + [pl.BlockSpec(memory_space=pltpu.VMEM)],
        out_shape=_dma_sems(2 * ns) + _hbm_like(arrs) + _hbm_like(lands) + [_sds((8, 128), F32)],
        input_output_aliases={i: 2 * ns + i for i in range(2 * n)},
        compiler_params=pltpu.CompilerParams(has_side_effects=EFFECT),
    )(*[_hbm(a) for a in arrs], *[_hbm(a) for a in lands])
    return (outs[:ns], outs[ns:2 * ns], outs[2 * ns:2 * ns + n], outs[2 * ns + n:2 * ns + 2 * n]), outs[-1]


def rs_sibling_wait(send_sems, recv_sems, arrs, lands, after, name):
    n = len(arrs)
    ns = 4 * n

    def body(*refs):
        ins, lnd = refs[:n], refs[n:2 * n]
        s, r = refs[2 * n:2 * n + ns], refs[2 * n + ns:2 * n + 2 * ns]
        x, y, c = _place()
        sibling = (x, y, 1 - c)
        for a in range(n):
            for p in range(4):
                cp = pltpu.make_async_remote_copy(
                    src_ref=ins[a].at[2 * p + 1 - c], dst_ref=lnd[a].at[p], send_sem=s[4 * a + p], recv_sem=r[4 * a + p],
                    device_id=sibling, device_id_type=MESH)
                cp.wait_send()
                cp.wait_recv()

    outs = pl.pallas_call(
        body, name=name,
        in_specs=[HBM] * (2 * n) + [SEM] * (2 * ns) + [ANY], out_specs=[HBM] * (2 * n),
        out_shape=_hbm_like(arrs) + _hbm_like(lands),
        input_output_aliases={i: i for i in range(2 * n)},
        compiler_params=pltpu.CompilerParams(has_side_effects=EFFECT),
    )(*arrs, *lands, *send_sems, *recv_sems, after)
    return outs[:n], outs[n:]


def rs_chips_start(parts, name):
    n = len(parts)
    ns = 3 * n
    lands = [lax.empty((3,) + a.shape[1:], a.dtype) for a in parts]

    def body(*refs):
        ins, lnd = refs[:n], refs[n:2 * n]
        send_sems, recv_sems = refs[2 * n:2 * n + ns], refs[2 * n + ns:2 * n + 2 * ns]
        x, y, c = _place()
        chips = [(1 - x, y), (x, 1 - y), (1 - x, 1 - y)]
        for a in range(n):
            for k, (tx, ty) in enumerate(chips):
                pltpu.make_async_remote_copy(
                    src_ref=ins[a].at[2 * tx + ty], dst_ref=lnd[a].at[k], send_sem=send_sems[3 * a + k],
                    recv_sem=recv_sems[3 * a + k], device_id=(tx, ty, c), device_id_type=MESH).start()
        refs[-1][...] = jnp.zeros_like(refs[-1])

    outs = pl.pallas_call(
        body, name=name,
        in_specs=[HBM] * (2 * n), out_specs=[SEM] * (2 * ns) + [HBM] * (2 * n) + [pl.BlockSpec(memory_space=pltpu.VMEM)],
        out_shape=_dma_sems(2 * ns) + _hbm_like(parts) + _hbm_like(lands) + [_sds((8, 128), F32)],
        input_output_aliases={i: 2 * ns + i for i in range(2 * n)},
        compiler_params=pltpu.CompilerParams(has_side_effects=EFFECT),
    )(*[_hbm(a) for a in parts], *[_hbm(a) for a in lands])
    return (outs[:ns], outs[ns:2 * ns], outs[2 * ns:2 * ns + n], outs[2 * ns + n:2 * ns + 2 * n]), outs[-1]


def rs_chips_wait(send_sems, recv_sems, parts, lands, after, name):
    n = len(parts)
    ns = 3 * n

    def body(*refs):
        ins, lnd = refs[:n], refs[n:2 * n]
        s, r = refs[2 * n:2 * n + ns], refs[2 * n + ns:2 * n + 2 * ns]
        x, y, c = _place()
        chips = [(1 - x, y), (x, 1 - y), (1 - x, 1 - y)]
        for a in range(n):
            for k, (tx, ty) in enumerate(chips):
                cp = pltpu.make_async_remote_copy(
                    src_ref=ins[a].at[2 * tx + ty], dst_ref=lnd[a].at[k], send_sem=s[3 * a + k], recv_sem=r[3 * a + k],
                    device_id=(tx, ty, c), device_id_type=MESH)
                cp.wait_send()
                cp.wait_recv()

    outs = pl.pallas_call(
        body, name=name,
        in_specs=[HBM] * (2 * n) + [SEM] * (2 * ns) + [ANY], out_specs=[HBM] * (2 * n),
        out_shape=_hbm_like(parts) + _hbm_like(lands),
        input_output_aliases={i: i for i in range(2 * n)},
        compiler_params=pltpu.CompilerParams(has_side_effects=EFFECT),
    )(*parts, *lands, *send_sems, *recv_sems, after)
    return outs[:n], outs[n:]


def pair_sum(arrs, recv, c, name):
    n = len(arrs)

    def body(c_ref, *refs):
        del c_ref
        for a in range(n):
            refs[2 * n + a][...] = (refs[a][...].astype(F32) + refs[n + a][...].astype(F32)).astype(refs[2 * n + a].dtype)

    mine = [pl.BlockSpec((None,) + a.shape[1:], lambda p, c_ref: (2 * p + c_ref[0], 0, 0)) for a in arrs]
    other = [pl.BlockSpec((None,) + a.shape[1:], lambda p, c_ref: (p, 0, 0)) for a in arrs]
    return pl.pallas_call(
        body, name=name,
        grid_spec=pltpu.PrefetchScalarGridSpec(num_scalar_prefetch=1, grid=(4,), in_specs=mine + other, out_specs=other),
        out_shape=[_sds((4,) + a.shape[1:], a.dtype) for a in arrs], compiler_params=_params(("parallel",)),
    )(c, *arrs, *recv)


def _adamw(w, g, m, v):
    m = ADAM_B1 * m + (1.0 - ADAM_B1) * g
    v = ADAM_B2 * v + (1.0 - ADAM_B2) * jnp.square(g)
    m_hat = m / (1.0 - ADAM_B1 ** ADAM_STEP)
    v_hat = v / (1.0 - ADAM_B2 ** ADAM_STEP)
    return -ADAM_LR * (m_hat / (jnp.sqrt(v_hat) + ADAM_EPS) + ADAM_WD * w), m, v


def adamw_big(recv, sums, chip, w, m, v, tr, name):
    nl, rr, cc = w.shape
    cp = recv[0].shape[2]

    def body(chip_ref, *refs):
        del chip_ref
        rcv, own = refs[:nl], refs[nl:2 * nl]
        w_ref, m_ref, v_ref, g_out, d_out, m_out, v_out = refs[2 * nl:]
        for l in range(nl):
            g = ((own[l][...].astype(F32) + rcv[l][0].astype(F32)) + rcv[l][1].astype(F32)) + rcv[l][2].astype(F32)
            g = g[:, :cc]
            g_out[l] = g
            d_out[l], m_out[l], v_out[l] = _adamw(w_ref[l], g, m_ref[l], v_ref[l])

    blk = pl.BlockSpec((nl, tr, cc), lambda i, chip_ref: (0, i, 0))
    return pl.pallas_call(
        body, name=name,
        grid_spec=pltpu.PrefetchScalarGridSpec(
            num_scalar_prefetch=1, grid=(rr // tr,),
            in_specs=[pl.BlockSpec((3, tr, cp), lambda i, chip_ref: (0, i, 0))] * nl
            + [pl.BlockSpec((None, tr, cp), lambda i, chip_ref: (chip_ref[0], i, 0))] * nl + [blk, blk, blk],
            out_specs=[blk] * 4),
        out_shape=[_sds(w.shape, F32)] * 4, compiler_params=_params(("parallel",)),
    )(chip, *recv, *sums, w, m, v)


SMALL_ROWS = 24


def small_grads(lvec, g_ret, g_mix, g_ffn, g_final, loss_part, dwa, dwx, name):
    def body(lvec_ref, ret_ref, mix_ref, ffn_ref, fin_ref, loss_ref, dwa_ref, dwx_ref, v_ref, g_ref):
        v_ref[16:SMALL_ROWS, :] = jnp.zeros((SMALL_ROWS - 16, D_LRU), F32)
        v_ref[16:17, 0:128] = loss_ref[0:1, :]
        v_ref[0:9, :] = lvec_ref[0:9, :]
        v_ref[9:10, :] = ret_ref[...]
        for r, src in ((10, mix_ref), (12, ffn_ref), (14, fin_ref)):
            v_ref[r:r + 1, :] = src[:, :D_LRU]
            v_ref[r + 1:r + 2, :] = src[:, D_LRU:]
        for k, src in enumerate((dwa_ref, dwx_ref)):
            for g in range(LRU_BLOCKS):
                rows = slice(LRU_BD * g, LRU_BD * (g + 1))
                g_ref[D_LRU * k + LRU_BD * g:D_LRU * k + LRU_BD * (g + 1), :] = src[rows, rows]

    ins = [lvec, g_ret, g_mix, g_ffn, g_final, loss_part, dwa, dwx]
    return pl.pallas_call(
        body, name=name, grid=(1,), in_specs=[_full(a.shape) for a in ins],
        out_specs=[_full((SMALL_ROWS, D_LRU)), _full((2 * D_LRU, LRU_BD))],
        out_shape=[_sds((SMALL_ROWS, D_LRU), F32), _sds((2 * D_LRU, LRU_BD), F32)], compiler_params=_params(("arbitrary",)),
    )(*ins)


def sum_devices(arrs, name):
    n = len(arrs)

    def body(*refs):
        for a in range(n):
            acc = refs[a][0]
            for j in range(1, NDEV):
                acc = acc + refs[a][j]
            refs[n + a][...] = acc

    return pl.pallas_call(
        body, name=name, grid=(1,), in_specs=[_full(a.shape) for a in arrs], out_specs=[_full(a.shape[1:]) for a in arrs],
        out_shape=[_sds(a.shape[1:], F32) for a in arrs], compiler_params=_params(("arbitrary",)),
    )(*arrs)


def adamw_small(gs, ws, ms, vs, name):
    n = len(gs)

    def body(*refs):
        for a in range(n):
            g, w, m, v = (refs[k * n + a][...] for k in range(4))
            refs[4 * n + a][...], refs[5 * n + a][...], refs[6 * n + a][...] = _adamw(w, g, m, v)

    specs = [_full(a.shape) for a in ws]
    outs = pl.pallas_call(
        body, name=name, grid=(1,), in_specs=specs * 4, out_specs=specs * 3, out_shape=[_sds(a.shape, F32) for a in ws] * 3,
        compiler_params=_params(("arbitrary",)),
    )(*gs, *ws, *ms, *vs)
    return outs[:n], outs[n:2 * n], outs[2 * n:]


def block_diag(wa, wx, name):
    def body(wa_ref, wx_ref, oa_ref, ox_ref):
        for src, dst in ((wa_ref, oa_ref), (wx_ref, ox_ref)):
            dst[...] = jnp.zeros_like(dst)
            for g in range(LRU_BLOCKS):
                rows = slice(LRU_BD * g, LRU_BD * (g + 1))
                dst[rows, rows] = src[g].astype(dst.dtype)

    ispec = pl.BlockSpec((None, LRU_BLOCKS, LRU_BD, LRU_BD), lambda l: (l, 0, 0, 0))
    ospec = pl.BlockSpec((None, D_LRU, D_LRU), lambda l: (l, 0, 0))
    return pl.pallas_call(
        body, name=name, grid=(wa.shape[0],), in_specs=[ispec, ispec], out_specs=[ospec, ospec],
        out_shape=[_sds((wa.shape[0], D_LRU, D_LRU), MXU_DTYPE)] * 2, compiler_params=_params(("parallel",)),
    )(wa, wx)


REP_NAMES = ["norm_mix", "conv_b", "gate_a_w", "gate_a_b", "gate_x_w", "gate_x_b", "lru_lambda", "lru_out_norm",
             "ret_out_norm", "norm_ffn", "norm_final"]


def kernel(x, meta_tokens, norm_mix, w_in, conv_w, conv_b, gate_a_w, gate_a_b, gate_x_w, gate_x_b, lru_lambda, lru_out_norm, ret_out_norm, w_out, norm_ffn, w_gate, w_up, w_down, norm_final, loss_target, m_meta_tokens, m_norm_mix, m_w_in, m_conv_w, m_conv_b, m_gate_a_w, m_gate_a_b, m_gate_x_w, m_gate_x_b, m_lru_lambda, m_lru_out_norm, m_ret_out_norm, m_w_out, m_norm_ffn, m_w_gate, m_w_up, m_w_down, m_norm_final, v_meta_tokens, v_norm_mix, v_w_in, v_conv_w, v_conv_b, v_gate_a_w, v_gate_a_b, v_gate_x_w, v_gate_x_b, v_lru_lambda, v_lru_out_norm, v_ret_out_norm, v_w_out, v_norm_ffn, v_w_gate, v_w_up, v_w_down, v_norm_final):
    xi, yi, ci = _place()
    dev = 4 * xi + 2 * yi + ci
    c_arr = jnp.reshape(ci, (1,)).astype(jnp.int32)
    dev_arr = jnp.reshape(dev, (1,)).astype(jnp.int32)

    meta_g, conv_g = all_gather([meta_tokens, conv_w], "ag_small")
    meta_full = jnp.transpose(meta_g, (1, 0, 2)).reshape(N_META, D)
    conv_full = jnp.transpose(conv_g, (1, 2, 0, 3)).reshape(DEPTH, CONV_W, D_LRU)
    tr_ = lambda a: jnp.transpose(a, (0, 2, 1))
    w_gate_t, m_w_gate_t, v_w_gate_t = tr_(w_gate), tr_(m_w_gate), tr_(v_w_gate)
    w_up_t, m_w_up_t, v_w_up_t = tr_(w_up), tr_(m_w_up), tr_(v_w_up)
    level1 = []
    token = meta_g
    for l in range(DEPTH):
        sel = jnp.stack([dev, jnp.int32(l)]).astype(jnp.int32)
        lands = to_wire(sel, w_in, w_gate_t, w_up_t, w_out, w_down, "to_wire")
        s1, r1, lands, token = ag_start(lands, token, f"ag_start_{l}")
        level1.append((s1, r1, lands))

    def as_weights(gi, gg, gu, go, gd):
        return dict(w_in=gi, w_gate=gg.reshape(D_FFP, D), w_up=gu.reshape(D_FFP, D), w_out=go.reshape(D, D),
                    w_down=gd.reshape(D_FFP, D))

    tables = _ret_tables()
    row = lambda a: a.reshape(1, -1)

    h = jnp.concatenate([jnp.zeros((PAD, D), F32), meta_full, x[0]], axis=0)
    saved, gathered = [], []
    s1, r1, lands = level1[0]
    s2, r2, first, order = ag_forward(s1[:4], r1[:4], lands[:1], token, "ag_forward_0_w_in")
    w_in_next = ag_finish(s2, r2, first, h, "ag_finish_0_w_in")[0]
    wa_dense, wx_dense = block_diag(gate_a_w, gate_x_w, "block_diag")
    for l in range(DEPTH):
        small = dict(cw=conv_full[l], cb=row(conv_b[l]), wa=wa_dense[l], ba=row(gate_a_b[l]),
                     wx=wx_dense[l], bx=row(gate_x_b[l]), lam=row(lru_lambda[l]),
                     gain=row(lru_out_norm[l]))
        s1, r1, lands = level1[l]
        hn1 = rmsnorm_fwd(h, row(norm_mix[l]), "rms_fwd")
        proj = mm_blocked_nn(hn1, w_in_next, F32, "proj")
        s2, r2, rest, order = ag_forward(s1[4:], r1[4:], lands[1:], proj, f"ag_forward_{l}_rest")
        ymix, hst, states = mix_fwd(proj, tables=tables, ret_gain=row(ret_out_norm[l]), after=order, name="mix_fwd", **small)
        w = as_weights(w_in_next, *ag_finish(s2, r2, rest, ymix, f"ag_finish_{l}_rest"))
        gathered.append(w)
        h_mid = mm_nn_res(ymix, w["w_out"], h, order, "out_proj")
        hn2 = rmsnorm_fwd(h_mid, row(norm_ffn[l]), "rms_fwd")
        act_dgate, act_dup, act = ffn_up(hn2, w["w_gate"], w["w_up"], "ffn_up")
        if l + 1 < DEPTH:
            s1n, r1n, landsn = level1[l + 1]
            s2, r2, first, order = ag_forward(s1n[:4], r1n[:4], landsn[:1], act, f"ag_forward_{l + 1}_w_in")
        h_out = mm_nn_res(act, w["w_down"], h_mid, order, "ffn_down")
        if l + 1 < DEPTH:
            w_in_next = ag_finish(s2, r2, first, h_out, f"ag_finish_{l + 1}_w_in")[0]
        saved.append(dict(h=h, hn1=hn1, proj=proj, hst=hst, states=states, ymix=ymix, h_mid=h_mid, hn2=hn2, act_dgate=act_dgate, act_dup=act_dup,
                          act=act, small=small))
        h = h_out

    loss_p, dh, dh_b, g_norm_final = loss_head(h, row(norm_final), loss_target[0], "loss_head")

    small_v = [None] * DEPTH
    small_w = [None] * DEPTH
    inflight = []
    sib = None
    order = loss_p

    def sibling_done(l, tag, names, sib, after):
        parts, got = rs_sibling_wait(*sib, after, f"rs_sibling_wait_{tag}")
        sums = pair_sum(parts, got, c_arr, "pair_sum")
        flying, started = rs_chips_start(sums, f"rs_chips_start_{tag}")
        inflight.append((l, tag, names, flying))
        return started

    for l in reversed(range(DEPTH)):
        w, s = gathered[l], saved[l]
        dgate, dup = ffn_down_bwd(dh_b, w["w_down"], s["act_dgate"], s["act_dup"], order, "ffn_down_bwd")
        dwd = mm_tn(s["act"], dh_b, PAIR, order, "dw_down").reshape(NDEV, FF_SHP, D)
        dwg, dwu = (g.reshape(NDEV, FF_SHP, D) for g in mm_tn_two(dgate, dup, s["hn2"], PAIR, order, "dw_rows"))
        split = l <= 1
        if split:
            ffn_sib, order = rs_sibling_start([dwg, dwu, dwd], f"rs_sibling_start_{l}_ffn")
        dhn2 = mm_rows_nn([(dgate, w["w_gate"]), (dup, w["w_up"])], order, "ffn_up_bwd")
        if sib is not None:
            order = sibling_done(l + 1, sib_tag, sib_names, sib, dhn2)
        dh_mid, dh_mid_b, g_norm_ffn = rmsnorm_bwd(s["h_mid"], row(norm_ffn[l]), dhn2, dh, "rms_bwd")
        dymix, dwo = out_proj_bwd(dh_mid_b, w["w_out"], s["ymix"], order, "out_proj_bwd")
        dwo = dwo.reshape(NDEV, OUT_SH, D)
        if split:
            order = sibling_done(l, f"{l}_ffn", ("w_gate", "w_up", "w_down"), ffn_sib, dymix)
        dproj, lvec, dwa, dwx, g_ret_norm = mix_bwd(s["proj"], s["hst"], s["states"], dymix, tables=tables,
                                                    ret_gain=row(ret_out_norm[l]), after=order, name="mix_bwd", **s["small"])
        dwi = mm_tn_blocked(s["hn1"], dproj, "dw_blocked")
        dhn1 = mm_blocked_nt([(dproj, w["w_in"])], order, "proj_bwd")
        dh, dh_b, g_norm_mix = rmsnorm_bwd(s["h"], row(norm_mix[l]), dhn1, dh_mid, "rms_bwd")

        g_fin, loss_part = (g_norm_final, loss_p) if l == 0 else (jnp.zeros((1, D), F32), jnp.zeros((8, 128), F32))
        small_v[l], small_w[l] = small_grads(lvec, g_ret_norm, g_norm_mix, g_norm_ffn, g_fin, loss_part, dwa, dwx,
                                             "small_grads")
        if split:
            sib_tag, sib_names = f"{l}_mix", ("w_in", "w_out")
            sib, order = rs_sibling_start([dwi, dwo], f"rs_sibling_start_{l}_mix")
        else:
            sib_tag, sib_names = str(l), ("w_in", "w_gate", "w_up", "w_out", "w_down")
            sib, order = rs_sibling_start([dwi, dwg, dwu, dwo, dwd], f"rs_sibling_start_{l}")
        if l == 1:
            early = place_blocks(dev_arr, [jnp.stack(small_v[1:]), jnp.stack(small_w[1:])], "place_grads")
            early_sems = ag_start(early, order, "ag_start_grads")
            order = early_sems[3]

    grad_x = dh[X0:][None]
    g_meta = dh[PAD:X0]

    late = all_gather([small_v[0], small_w[0], g_meta], "ag_grads")
    s2, r2, lands, _ = ag_forward(early_sems[0], early_sems[1], early_sems[2], dh, "ag_forward_grads")
    gath_early = ag_finish(s2, r2, lands, late[0], "ag_finish_grads")
    sibling_done(0, sib_tag, sib_names, sib, late[0])
    v0, w0, meta_sum, v123, w123 = sum_devices(list(late) + list(gath_early), "sum_devices")
    loss = v0[16, 0]
    vecs = jnp.concatenate([v0[None], v123])
    gws = jnp.concatenate([w0[None], w123])
    blocks = (DEPTH, LRU_BLOCKS, LRU_BD)
    small_g = dict(
        conv_w=lax.dynamic_slice_in_dim(vecs[:, 0:CONV_W], dev * (D_LRU // NDEV), D_LRU // NDEV, axis=2),
        conv_b=vecs[:, 4], gate_a_b=vecs[:, 5].reshape(blocks), gate_x_b=vecs[:, 6].reshape(blocks),
        lru_lambda=vecs[:, 7], lru_out_norm=vecs[:, 8], ret_out_norm=vecs[:, 9],
        norm_mix=vecs[:, 10:12].reshape(DEPTH, D), norm_ffn=vecs[:, 12:14].reshape(DEPTH, D),
        norm_final=v0[14:16].reshape(1, D),
        gate_a_w=gws[:, :D_LRU].reshape(blocks + (LRU_BD,)), gate_x_w=gws[:, D_LRU:].reshape(blocks + (LRU_BD,)),
        meta_tokens=lax.dynamic_slice_in_dim(meta_sum, dev * (D // NDEV), D // NDEV, axis=1))
    given = dict(norm_mix=(norm_mix, m_norm_mix, v_norm_mix), conv_b=(conv_b, m_conv_b, v_conv_b),
                 gate_a_w=(gate_a_w, m_gate_a_w, v_gate_a_w), gate_a_b=(gate_a_b, m_gate_a_b, v_gate_a_b),
                 gate_x_w=(gate_x_w, m_gate_x_w, v_gate_x_w), gate_x_b=(gate_x_b, m_gate_x_b, v_gate_x_b),
                 lru_lambda=(lru_lambda, m_lru_lambda, v_lru_lambda), lru_out_norm=(lru_out_norm, m_lru_out_norm, v_lru_out_norm),
                 ret_out_norm=(ret_out_norm, m_ret_out_norm, v_ret_out_norm), norm_ffn=(norm_ffn, m_norm_ffn, v_norm_ffn),
                 norm_final=tuple(a.reshape(1, D) for a in (norm_final, m_norm_final, v_norm_final)),
                 conv_w=(conv_w, m_conv_w, v_conv_w), meta_tokens=(meta_tokens, m_meta_tokens, v_meta_tokens))
    small_names = REP_NAMES + ["conv_w", "meta_tokens"]
    upd = adamw_small([small_g[n] for n in small_names], *[[given[n][k] for n in small_names] for k in range(3)],
                      "adamw_small")
    small_out = [dict(zip(small_names, u)) for u in upd]
    for d_ in [small_g] + small_out:
        d_["norm_final"] = d_["norm_final"].reshape(D)

    arrived = {}

    def wait_for(entries, after):
        for l, tag, names, flying in entries:
            sums, recv = rs_chips_wait(*flying, after, f"rs_chips_wait_{tag}")
            for i, n in enumerate(names):
                arrived[l, n] = (recv[i], sums[i])

    chip = jnp.reshape(2 * xi + yi, (1,)).astype(jnp.int32)

    def finish(wname, w_, m_, v_, tr):
        return adamw_big([arrived[l, wname][0] for l in range(DEPTH)], [arrived[l, wname][1] for l in range(DEPTH)], chip,
                         w_, m_, v_, tr, "adamw_" + wname)

    wait_for(inflight[:-1], upd[0][0])
    o_gate = [tr_(o) for o in finish("w_gate", w_gate_t, m_w_gate_t, v_w_gate_t, 32)]
    o_up = [tr_(o) for o in finish("w_up", w_up_t, m_w_up_t, v_w_up_t, 32)]
    o_down = finish("w_down", w_down, m_w_down, v_w_down, 32)
    wait_for(inflight[-1:], o_down[0])
    o_in = finish("w_in", w_in, m_w_in, v_w_in, 256)
    o_out = finish("w_out", w_out, m_w_out, v_w_out, 64)

    bigs = dict(w_in=o_in, w_out=o_out, w_gate=o_gate, w_up=o_up, w_down=o_down)
    order = ["meta_tokens", "norm_mix", "w_in", "conv_w", "conv_b", "gate_a_w", "gate_a_b", "gate_x_w", "gate_x_b", "lru_lambda",
             "lru_out_norm", "ret_out_norm", "w_out", "norm_ffn", "w_gate", "w_up", "w_down", "norm_final"]
    grads = [bigs[n][0] if n in bigs else small_g[n] for n in order]
    rest = [[bigs[n][k + 1] if n in bigs else small_out[k][n] for n in order] for k in range(3)]
    return (loss, grad_x, *grads, *rest[0], *rest[1], *rest[2])
```

```python
import numpy as np
import jax
import jax.numpy as jnp
from jax import lax
from jax.experimental import pallas as pl
from jax.experimental.pallas import tpu as pltpu

F32, BF16 = jnp.float32, jnp.bfloat16
MXU_DTYPE = BF16
WIRE_DTYPE = BF16

D = 1024
SEQ = 2048
DEPTH = 4
N_META = 16
CH = 128
PAD = (-(SEQ + N_META)) % CH
T = SEQ + N_META + PAD
NCH = T // CH
X0 = PAD + N_META
D_LRU = 512
LRU_BLOCKS = 8
LRU_BD = 64
CONV_W = 4
LRU_C = 8.0
D_RET = 512
HEADS = 4
HD = 128
ROPE_BASE = 10000.0
D_IN = 3072
D_FF = 2816
NDEV = 8
IN_SH = D_IN // NDEV
FF_SH = D_FF // NDEV
FF_SHP = 384
D_FFP = NDEV * FF_SHP
OUT_SH = D // NDEV
EPS = 1e-6
TM = 544
VMEM_LIMIT = 56 * 2**20
MESH = pl.DeviceIdType.MESH

ADAM_LR, ADAM_B1, ADAM_B2, ADAM_EPS, ADAM_WD, ADAM_STEP = 0.001, 0.9, 0.999, 1e-08, 0.01, 10

NN = ((1,), (0,))
NT = ((1,), (1,))
TN = ((0,), (0,))


def _dot(a, b, dims):
    return lax.dot_general(a.astype(MXU_DTYPE), b.astype(MXU_DTYPE), (dims, ((), ())), preferred_element_type=F32)


def _sds(shape, dtype):
    return jax.ShapeDtypeStruct(shape, dtype)


def _params(sem=None):
    return pltpu.CompilerParams(dimension_semantics=sem, vmem_limit_bytes=VMEM_LIMIT)


def _full(shape):
    n = len(shape)
    return pl.BlockSpec(shape, lambda *_: (0,) * n)


def rmsnorm_fwd(h, gain, name):
    def body(h_ref, g_ref, o_ref):
        x = h_ref[...]
        ms = jnp.mean(x * x, axis=-1, keepdims=True)
        o_ref[...] = (x * lax.rsqrt(ms + EPS) * g_ref[...]).astype(o_ref.dtype)

    return pl.pallas_call(
        body, name=name, grid=(T // TM,),
        in_specs=[pl.BlockSpec((TM, D), lambda i: (i, 0)), _full((1, D))],
        out_specs=pl.BlockSpec((TM, D), lambda i: (i, 0)),
        out_shape=_sds((T, D), MXU_DTYPE), compiler_params=_params(("parallel",)),
    )(h, gain)


def rmsnorm_bwd(h, gain, dhn, dres, name):
    def body(h_ref, g_ref, dhn_ref, dres_ref, dh_ref, dhb_ref, dg_ref):
        x = h_ref[...]
        rstd = lax.rsqrt(jnp.mean(x * x, axis=-1, keepdims=True) + EPS)
        xhat = x * rstd
        dy = dhn_ref[...]
        dyg = dy * g_ref[...]
        dh = dres_ref[...] + rstd * (dyg - xhat * jnp.mean(dyg * xhat, axis=-1, keepdims=True))
        dh_ref[...] = dh
        dhb_ref[...] = dh.astype(dhb_ref.dtype)

        @pl.when(pl.program_id(0) == 0)
        def _():
            dg_ref[...] = jnp.zeros_like(dg_ref)
        dg_ref[...] += jnp.sum(dy * xhat, axis=0, keepdims=True)

    row = pl.BlockSpec((TM, D), lambda i: (i, 0))
    return pl.pallas_call(
        body, name=name, grid=(T // TM,),
        in_specs=[row, _full((1, D)), row, row],
        out_specs=[row, row, _full((1, D))],
        out_shape=[_sds((T, D), F32), _sds((T, D), MXU_DTYPE), _sds((1, D), F32)], compiler_params=_params(("arbitrary",)),
    )(h, gain, dhn, dres)


def loss_head(h, gain, target, name):
    def body(h_ref, g_ref, t_ref, loss_ref, dh_ref, dhb_ref, dg_ref):
        i = pl.program_id(0)

        @pl.when(i == 0)
        def _():
            loss_ref[...] = jnp.zeros_like(loss_ref)
            dg_ref[...] = jnp.zeros_like(dg_ref)
            dh_ref[...] = jnp.zeros_like(dh_ref)
            dhb_ref[...] = jnp.zeros_like(dhb_ref)

        @pl.when(i > 0)
        def _():
            x = h_ref[...]
            g = g_ref[...]
            rstd = lax.rsqrt(jnp.mean(x * x, axis=-1, keepdims=True) + EPS)
            xhat = x * rstd
            err = xhat * g - t_ref[...]
            loss_ref[...] += 0.5 * jnp.sum(jnp.mean(err * err, axis=-1, keepdims=True), axis=0, keepdims=True)
            dy = err * (1.0 / D)
            dyg = dy * g
            dh = rstd * (dyg - xhat * jnp.mean(dyg * xhat, axis=-1, keepdims=True))
            dh_ref[...] = dh
            dhb_ref[...] = dh.astype(dhb_ref.dtype)
            dg_ref[...] += jnp.sum(dy * xhat, axis=0, keepdims=True)

    row = pl.BlockSpec((CH, D), lambda i: (i, 0))
    return pl.pallas_call(
        body, name=name, grid=(NCH,),
        in_specs=[row, _full((1, D)), pl.BlockSpec((CH, D), lambda i: (jnp.maximum(i - 1, 0), 0))],
        out_specs=[_full((8, 128)), row, row, _full((1, D))],
        out_shape=[_sds((8, 128), F32), _sds((T, D), F32), _sds((T, D), MXU_DTYPE), _sds((1, D), F32)],
        compiler_params=_params(("arbitrary",)),
    )(h, gain, target)


PAIR = 2 * IN_SH
NPAIR = NDEV // 2
BN = 256
FB = 512


def _pair_cols(w_ref):
    return jnp.concatenate([w_ref[0], w_ref[1]], axis=1)


W_PAIR = lambda k: pl.BlockSpec((2, k, IN_SH), lambda j: (j, 0, 0))
COLS_PAIR = pl.BlockSpec((T, PAIR), lambda j: (0, j))
ANYSPEC = pl.BlockSpec(memory_space=pl.ANY)


def mm_blocked_nn(a, w, out_dtype, name):
    k = a.shape[1]

    def body(a_ref, w_ref, o_ref):
        o_ref[...] = _dot(a_ref[...], _pair_cols(w_ref), NN).astype(o_ref.dtype)

    return pl.pallas_call(
        body, name=name, grid=(NPAIR,),
        in_specs=[_full((T, k)), W_PAIR(k)], out_specs=COLS_PAIR,
        out_shape=_sds((T, NDEV * IN_SH), out_dtype), compiler_params=_params(("parallel",)),
    )(a, w)


def mm_nn_res(a, w, res, after, name):
    k = a.shape[1]

    def body(a_ref, w_ref, r_ref, after_ref, o_ref):
        del after_ref
        o_ref[...] = r_ref[...] + _dot(a_ref[...], w_ref[...], NN)

    col = pl.BlockSpec((T, BN), lambda j: (0, j))
    return pl.pallas_call(
        body, name=name, grid=(D // BN,),
        in_specs=[_full((T, k)), pl.BlockSpec((k, BN), lambda j: (0, j)), col, ANYSPEC], out_specs=col,
        out_shape=_sds((T, D), F32), compiler_params=_params(("parallel",)),
    )(a, w, res, after)


def ffn_up(hn, wg, wu, name):
    def body(a_ref, wg_ref, wu_ref, dg_ref, du_ref, act_ref):
        a = a_ref[...]
        for c in range(FB // BN):
            cols = slice(BN * c, BN * (c + 1))
            g = _dot(a, wg_ref[cols, :], NT)
            u = _dot(a, wu_ref[cols, :], NT)
            sg = jax.nn.sigmoid(g)
            silu = g * sg
            dg_ref[:, cols] = (u * (sg * (1.0 + g * (1.0 - sg)))).astype(dg_ref.dtype)
            du_ref[:, cols] = silu.astype(du_ref.dtype)
            act_ref[:, cols] = (silu * u).astype(act_ref.dtype)

    wspec = pl.BlockSpec((FB, D), lambda j: (j, 0))
    ospec = pl.BlockSpec((T, FB), lambda j: (0, j))
    return pl.pallas_call(
        body, name=name, grid=(D_FFP // FB,),
        in_specs=[_full((T, D)), wspec, wspec], out_specs=[ospec] * 3,
        out_shape=[_sds((T, D_FFP), MXU_DTYPE)] * 3, compiler_params=_params(("parallel",)),
    )(hn, wg, wu)


def ffn_down_bwd(dh, wd, dact_dgate, dact_dup, after, name):
    def body(dh_ref, wd_ref, g_ref, u_ref, after_ref, dg_ref, du_ref):
        del after_ref
        dh = dh_ref[...]
        for c in range(FB // BN):
            cols = slice(BN * c, BN * (c + 1))
            dact = _dot(dh, wd_ref[cols, :], NT)
            dg_ref[:, cols] = (dact * g_ref[:, cols].astype(F32)).astype(dg_ref.dtype)
            du_ref[:, cols] = (dact * u_ref[:, cols].astype(F32)).astype(du_ref.dtype)

    blk = pl.BlockSpec((T, FB), lambda j: (0, j))
    return pl.pallas_call(
        body, name=name, grid=(D_FFP // FB,),
        in_specs=[_full((T, D)), pl.BlockSpec((FB, D), lambda j: (j, 0)), blk, blk, ANYSPEC],
        out_specs=[blk, blk],
        out_shape=[_sds((T, D_FFP), MXU_DTYPE)] * 2, compiler_params=_params(("parallel",)),
    )(dh, wd, dact_dgate, dact_dup, after)


def mm_blocked_nt(pairs, after, name):
    n = len(pairs)

    def body(*refs):
        o_ref = refs[2 * n + 1]

        @pl.when(pl.program_id(0) == 0)
        def _():
            o_ref[...] = jnp.zeros_like(o_ref)
        for p in range(n):
            o_ref[...] += _dot(refs[2 * p][...], _pair_cols(refs[2 * p + 1]), NT)

    specs, args = [], []
    for a, w in pairs:
        specs += [COLS_PAIR, W_PAIR(D)]
        args += [a, w]
    return pl.pallas_call(
        body, name=name, grid=(NPAIR,), in_specs=specs + [ANYSPEC], out_specs=_full((T, D)),
        out_shape=_sds((T, D), F32), compiler_params=_params(("arbitrary",)),
    )(*args, after)


def mm_tn_two(a1, a2, b, bm, after, name):
    m = a1.shape[1]

    def body(a1_ref, a2_ref, b_ref, after_ref, o1_ref, o2_ref):
        del after_ref
        b = b_ref[...]
        o1_ref[...] = _dot(a1_ref[...], b, TN).astype(o1_ref.dtype)
        o2_ref[...] = _dot(a2_ref[...], b, TN).astype(o2_ref.dtype)

    blk = pl.BlockSpec((T, bm), lambda i: (0, i))
    out = pl.BlockSpec((bm, D), lambda i: (i, 0))
    return pl.pallas_call(
        body, name=name, grid=(m // bm,),
        in_specs=[blk, blk, _full((T, D)), ANYSPEC], out_specs=[out, out],
        out_shape=[_sds((m, D), WIRE_DTYPE)] * 2, compiler_params=_params(("parallel",)),
    )(a1, a2, b, after)


def out_proj_bwd(dh, w, ymix, after, name):
    def body(dh_ref, w_ref, y_ref, after_ref, dy_ref, dw_ref):
        del after_ref
        dh_ = dh_ref[...]
        dy_ref[...] = _dot(dh_, w_ref[...], NT)
        dw_ref[...] = _dot(y_ref[...], dh_, TN).astype(dw_ref.dtype)

    return pl.pallas_call(
        body, name=name, grid=(D // BN,),
        in_specs=[_full((T, D)), pl.BlockSpec((BN, D), lambda j: (j, 0)), pl.BlockSpec((T, BN), lambda j: (0, j)), ANYSPEC],
        out_specs=[pl.BlockSpec((T, BN), lambda j: (0, j)), pl.BlockSpec((BN, D), lambda j: (j, 0))],
        out_shape=[_sds((T, D), F32), _sds((D, D), WIRE_DTYPE)], compiler_params=_params(("parallel",)),
    )(dh, w, ymix, after)


def mm_rows_nn(pairs, after, name):
    n = len(pairs)

    def body(*refs):
        o_ref = refs[2 * n + 1]

        @pl.when(pl.program_id(0) == 0)
        def _():
            o_ref[...] = jnp.zeros_like(o_ref)
        for p in range(n):
            o_ref[...] += _dot(refs[2 * p][...], refs[2 * p + 1][...], NN)

    specs, args = [], []
    for a, w in pairs:
        specs += [pl.BlockSpec((T, FB), lambda j: (0, j)), pl.BlockSpec((FB, D), lambda j: (j, 0))]
        args += [a, w]
    return pl.pallas_call(
        body, name=name, grid=(D_FFP // FB,), in_specs=specs + [ANYSPEC], out_specs=_full((T, D)),
        out_shape=_sds((T, D), F32), compiler_params=_params(("arbitrary",)),
    )(*args, after)


def mm_tn_blocked(a, b, name):
    def body(a_ref, b_ref, o_ref):
        o = _dot(a_ref[...], b_ref[...], TN).astype(o_ref.dtype)
        o_ref[0] = o[:, :IN_SH]
        o_ref[1] = o[:, IN_SH:]

    return pl.pallas_call(
        body, name=name, grid=(NPAIR,),
        in_specs=[_full((T, D)), COLS_PAIR], out_specs=W_PAIR(D),
        out_shape=_sds((NDEV, D, IN_SH), WIRE_DTYPE), compiler_params=_params(("parallel",)),
    )(a, b)


def mm_tn(a, b, bm, after, name):
    m = a.shape[1]

    def body(a_ref, b_ref, after_ref, o_ref):
        del after_ref
        o_ref[...] = _dot(a_ref[...], b_ref[...], TN).astype(o_ref.dtype)

    return pl.pallas_call(
        body, name=name, grid=(m // bm,),
        in_specs=[pl.BlockSpec((T, bm), lambda i: (0, i)), _full((T, D)), ANYSPEC],
        out_specs=pl.BlockSpec((bm, D), lambda i: (i, 0)),
        out_shape=_sds((m, D), WIRE_DTYPE), compiler_params=_params(("parallel",)),
    )(a, b, after)


def _softplus_neg(lam):
    return jnp.maximum(-lam, 0.0) + jnp.log1p(jnp.exp(-jnp.abs(lam)))


def _lru_gates(pa, px, xc, lam):
    r = jax.nn.sigmoid(pa)
    ig = jax.nn.sigmoid(px)
    sp = _softplus_neg(lam)
    log_a = -LRU_C * r * sp
    a = jnp.exp(log_a)
    mult = jnp.sqrt(-jnp.tanh(log_a) * (a * a + 1.0))
    return a, mult * (ig * xc), (r, ig, sp, mult)


def _lru_gates_vjp(da, db, xc, lam, a, r, ig, sp, mult):
    dmult = db * (ig * xc)
    du = db * mult
    dlog_a = da * a - dmult * (a * a) / mult
    dr = dlog_a * (-LRU_C * sp)
    dlam = jnp.sum(dlog_a * (-LRU_C * r), axis=0, keepdims=True) * (-jax.nn.sigmoid(-lam))
    dpa = dr * (r * (1.0 - r))
    dpx = (du * xc) * (ig * (1.0 - ig))
    return dpa, dpx, du * ig, dlam


def _lru_out(h, g, gain):
    z = h * jax.nn.gelu(g)
    return z * lax.rsqrt(jnp.mean(z * z, axis=-1, keepdims=True) + EPS) * gain


def _conv_taps(x, xprev, row):
    taps = [x]
    for s in range(1, CONV_W):
        taps.append(jnp.where(row < s, pltpu.roll(xprev, s, 0), pltpu.roll(x, s, 0)))
    return taps


def _conv(taps, cw_ref, cb):
    xc = cb + cw_ref[CONV_W - 1:CONV_W, :] * taps[0]
    for s in range(1, CONV_W):
        xc = xc + cw_ref[CONV_W - 1 - s:CONV_W - s, :] * taps[s]
    return xc


def _lru_fwd_block(i, x_ref, g_ref, cw_ref, cb_ref, wa_ref, ba_ref, wx_ref, bx_ref, lam_ref, gain_ref, y_ref, h_ref,
                   xprev_scr, a_scr, b_scr, carry_scr):
    @pl.when(i == 0)
    def _():
        xprev_scr[...] = jnp.zeros_like(xprev_scr)
        carry_scr[...] = jnp.zeros_like(carry_scr)

    x = x_ref[...]
    row = lax.broadcasted_iota(jnp.int32, (CH, D_LRU), 0)
    xc = _conv(_conv_taps(x, xprev_scr[...], row), cw_ref, cb_ref[...])
    pa = _dot(xc, wa_ref[...], NN) + ba_ref[...]
    px = _dot(xc, wx_ref[...], NN) + bx_ref[...]
    a, b, _ = _lru_gates(pa, px, xc, lam_ref[...])
    a_scr[...] = a
    b_scr[...] = jnp.where(i * CH + row >= PAD, b, 0.0)
    h = carry_scr[...]
    for t in range(CH):
        h = a_scr[t:t + 1, :] * h + b_scr[t:t + 1, :]
        h_ref[t:t + 1, :] = h
    carry_scr[...] = h
    xprev_scr[...] = x
    y_ref[:, :D_LRU] = _lru_out(h_ref[...], g_ref[...], gain_ref[...]).astype(y_ref.dtype)


LRU_VEC_ROWS = 16


def _lru_bwd_block(ib, x_ref, xp_ref, g_ref, h_ref, hp_ref, dy_ref, cw_ref, cb_ref, wa_ref, ba_ref, wx_ref, bx_ref, lam_ref,
                   gain_ref, dp_ref, vec_ref, dwa_ref, dwx_ref, a_scr, dh_scr, g_scr, carry_scr, dxcn_scr):
    @pl.when(ib == NCH - 1)
    def _():
        carry_scr[...] = jnp.zeros_like(carry_scr)
        dxcn_scr[...] = jnp.zeros_like(dxcn_scr)
        vec_ref[...] = jnp.zeros_like(vec_ref)
        dwa_ref[...] = jnp.zeros_like(dwa_ref)
        dwx_ref[...] = jnp.zeros_like(dwx_ref)

    x = x_ref[...]
    row = lax.broadcasted_iota(jnp.int32, (CH, D_LRU), 0)
    valid = ib * CH + row >= PAD
    taps = _conv_taps(x, xp_ref[...], row)
    xc = _conv(taps, cw_ref, cb_ref[...])
    pa = _dot(xc, wa_ref[...], NN) + ba_ref[...]
    px = _dot(xc, wx_ref[...], NN) + bx_ref[...]
    a, _, gate_parts = _lru_gates(pa, px, xc, lam_ref[...])
    h = h_ref[...]
    _, vjp_out = jax.vjp(_lru_out, h, g_ref[...], gain_ref[...])
    dh, dg, dgain = vjp_out(dy_ref[:, :D_LRU].astype(F32))
    a_scr[...] = a
    dh_scr[...] = dh
    c = carry_scr[...]
    for t in range(CH - 1, -1, -1):
        gt = dh_scr[t:t + 1, :] + c
        g_scr[t:t + 1, :] = gt
        c = a_scr[t:t + 1, :] * gt
    carry_scr[...] = c
    gg = g_scr[...]
    hprev = jnp.where(row < 1, pltpu.roll(hp_ref[...], 1, 0), pltpu.roll(h, 1, 0))
    da = jnp.where(valid, gg * hprev, 0.0)
    db = jnp.where(valid, gg, 0.0)
    dpa, dpx, dxc, dlam = _lru_gates_vjp(da, db, xc, lam_ref[...], a, *gate_parts)
    dxc = dxc + _dot(dpa, wa_ref[...], NT) + _dot(dpx, wx_ref[...], NT)
    dwa_ref[...] += _dot(xc, dpa, TN)
    dwx_ref[...] += _dot(xc, dpx, TN)
    for s in range(CONV_W):
        vec_ref[CONV_W - 1 - s:CONV_W - s, :] += jnp.sum(dxc * taps[s], axis=0, keepdims=True)
    vec_ref[4:5, :] += jnp.sum(dxc, axis=0, keepdims=True)
    vec_ref[5:6, :] += jnp.sum(dpa, axis=0, keepdims=True)
    vec_ref[6:7, :] += jnp.sum(dpx, axis=0, keepdims=True)
    vec_ref[7:8, :] += dlam
    vec_ref[8:9, :] += dgain
    dxn = dxcn_scr[...]
    dx = cw_ref[CONV_W - 1:CONV_W, :] * dxc
    for s in range(1, CONV_W):
        ahead = jnp.where(row >= CH - s, pltpu.roll(dxn, CH - s, 0), pltpu.roll(dxc, CH - s, 0))
        dx = dx + cw_ref[CONV_W - 1 - s:CONV_W - s, :] * ahead
    dxcn_scr[...] = dxc
    dp_ref[:, :D_LRU] = jnp.where(valid, dx, 0.0).astype(dp_ref.dtype)
    dp_ref[:, D_LRU:2 * D_LRU] = dg.astype(dp_ref.dtype)


def _ret_tables():
    half = HD // 2
    pos = jnp.arange(T, dtype=F32) - float(PAD)
    inv = ROPE_BASE ** (-jnp.arange(half, dtype=F32) / half)
    ang = pos[:, None] * inv[None, :]
    cos = jnp.concatenate([jnp.cos(ang), jnp.cos(ang)], axis=-1)
    sin = jnp.concatenate([-jnp.sin(ang), jnp.sin(ang)], axis=-1)
    log_g = jnp.log(1.0 - 2.0 ** (-5.0 - jnp.arange(HEADS, dtype=F32)))
    idx = jnp.arange(CH, dtype=F32)
    diff = idx[:, None] - idx[None, :]
    dmask = jnp.where(diff[None] >= 0, jnp.exp(jnp.maximum(diff, 0.0)[None] * log_g[:, None, None]), 0.0)
    xi = jnp.exp((idx + 1.0)[None, :] * log_g[:, None])
    zeta = jnp.exp((CH - 1.0 - idx)[None, :] * log_g[:, None])
    xi = jnp.broadcast_to(xi[:, :, None], (HEADS, CH, HD))
    zeta = jnp.broadcast_to(zeta[:, :, None], (HEADS, CH, HD))
    return cos, sin, dmask, xi, zeta


def _chunk_decay():
    log_g = np.log(np.float32(1.0) - np.float32(2.0) ** (np.float32(-5.0) - np.arange(HEADS, dtype=np.float32)))
    return [float(v) for v in np.exp(np.float32(CH) * log_g.astype(np.float32))]


def _rope(x, cos, sin):
    return x * cos + pltpu.roll(x, HD // 2, 1) * sin


def mix_fwd(proj, cw, cb, wa, ba, wx, bx, lam, gain, tables, ret_gain, after, name):
    cos, sin, dmask, xi, zeta = tables
    gch = _chunk_decay()
    scale = HD ** -0.5

    def body(x_ref, gl_ref, cw_ref, cb_ref, wa_ref, ba_ref, wx_ref, bx_ref, lam_ref, lgain_ref,
             q_ref, k_ref, v_ref, g_ref, cos_ref, sin_ref, dm_ref, xi_ref, zt_ref, gain_ref, after_ref,
             y_ref, h_ref, st_ref, xprev_scr, a_scr, b_scr, carry_scr, s_scr):
        del after_ref

        @pl.when(pl.program_id(0) == 0)
        def _():
            s_scr[...] = jnp.zeros_like(s_scr)

        _lru_fwd_block(pl.program_id(0), x_ref, gl_ref, cw_ref, cb_ref, wa_ref, ba_ref, wx_ref, bx_ref, lam_ref, lgain_ref,
                       y_ref, h_ref, xprev_scr, a_scr, b_scr, carry_scr)
        cs, sn = cos_ref[...], sin_ref[...]
        hs = range(HEADS)
        sl = [slice(HD * h, HD * (h + 1)) for h in hs]
        qr = [_rope(q_ref[:, sl[h]], cs, sn).astype(MXU_DTYPE) for h in hs]
        kf = [_rope(k_ref[:, sl[h]], cs, sn) * scale for h in hs]
        kr = [kf[h].astype(MXU_DTYPE) for h in hs]
        v = [v_ref[:, sl[h]].astype(MXU_DTYPE) for h in hs]
        s = [s_scr[h] for h in hs]
        for h in hs:
            st_ref[h] = s[h]
        sc = [_dot(qr[h], kr[h], NT) * dm_ref[h] for h in hs]
        cross = [_dot(qr[h], s[h], NN) * xi_ref[h] for h in hs]
        for h in hs:
            s_scr[h] = s[h] * gch[h] + _dot(kf[h] * zt_ref[h], v[h], TN)
        y = [_dot(sc[h], v[h], NN) + cross[h] for h in hs]
        yc = [y[h] - jnp.mean(y[h], axis=-1, keepdims=True) for h in hs]
        yn = [yc[h] * lax.rsqrt(jnp.mean(yc[h] * yc[h], axis=-1, keepdims=True) + EPS) for h in hs]
        for h in hs:
            so = slice(D_LRU + HD * h, D_LRU + HD * (h + 1))
            y_ref[:, so] = (jax.nn.silu(g_ref[:, sl[h]]) * (yn[h] * gain_ref[:, sl[h]])).astype(y_ref.dtype)

    def col(c):
        return pl.BlockSpec((CH, D_RET), lambda n: (n, c))

    tab = pl.BlockSpec((CH, HD), lambda n: (n, 0))
    cst = _full((HEADS, CH, HD))
    vec = _full((1, D_LRU))
    mat = _full((D_LRU, D_LRU))
    blockbuf = pltpu.VMEM((CH, D_LRU), F32)
    return pl.pallas_call(
        body, name=name, grid=(NCH,),
        in_specs=[col(0), col(1), _full((CONV_W, D_LRU)), vec, mat, vec, mat, vec, vec, vec,
                  col(2), col(3), col(4), col(5), tab, tab, cst, cst, cst, _full((1, D_RET)),
                  pl.BlockSpec(memory_space=pl.ANY)],
        out_specs=[pl.BlockSpec((CH, D), lambda n: (n, 0)), col(0), pl.BlockSpec((None, HEADS, HD, HD), lambda n: (n, 0, 0, 0))],
        out_shape=[_sds((T, D), MXU_DTYPE), _sds((T, D_LRU), F32), _sds((NCH, HEADS, HD, HD), F32)],
        scratch_shapes=[blockbuf, blockbuf, blockbuf, pltpu.VMEM((1, D_LRU), F32), pltpu.VMEM((HEADS, HD, HD), F32)],
        compiler_params=_params(("arbitrary",)),
    )(proj, proj, cw, cb, wa, ba, wx, bx, lam, gain, proj, proj, proj, proj, cos, sin, dmask, xi, zeta, ret_gain, after)


def mix_bwd(proj, hst, states, dymix, cw, cb, wa, ba, wx, bx, lam, gain, tables, ret_gain, after, name):
    cos, sin, dmask, xi, zeta = tables
    gch = _chunk_decay()
    scale = HD ** -0.5
    last = NCH - 1

    def body(x_ref, xp_ref, gl_ref, h_ref, hp_ref, cw_ref, cb_ref, wa_ref, ba_ref, wx_ref, bx_ref, lam_ref, lgain_ref,
             q_ref, k_ref, v_ref, g_ref, st_ref, dy_ref, cos_ref, sin_ref, dm_ref, xi_ref, zt_ref, gain_ref, after_ref,
             dp_ref, vec_ref, dwa_ref, dwx_ref, dgain_ref, a_scr, dh_scr, g_scr, carry_scr, dxcn_scr, ds_scr):
        del after_ref

        @pl.when(pl.program_id(0) == 0)
        def _():
            ds_scr[...] = jnp.zeros_like(ds_scr)
            dgain_ref[...] = jnp.zeros_like(dgain_ref)

        _lru_bwd_block(last - pl.program_id(0), x_ref, xp_ref, gl_ref, h_ref, hp_ref, dy_ref, cw_ref, cb_ref, wa_ref, ba_ref,
                       wx_ref, bx_ref, lam_ref, lgain_ref, dp_ref, vec_ref, dwa_ref, dwx_ref, a_scr, dh_scr, g_scr, carry_scr,
                       dxcn_scr)
        cs, sn = cos_ref[...], sin_ref[...]
        hs = range(HEADS)
        sl = [slice(HD * h, HD * (h + 1)) for h in hs]

        def out(j, h):
            return slice(2 * D_LRU + j * D_RET + HD * h, 2 * D_LRU + j * D_RET + HD * (h + 1))

        b16 = lambda xs: [x.astype(MXU_DTYPE) for x in xs]
        qr = b16([_rope(q_ref[:, sl[h]], cs, sn) for h in hs])
        kf = [_rope(k_ref[:, sl[h]], cs, sn) * scale for h in hs]
        kr = b16(kf)
        kz = b16([kf[h] * zt_ref[h] for h in hs])
        v = b16([v_ref[:, sl[h]] for h in hs])
        s = b16([st_ref[h] for h in hs])
        ds = [ds_scr[h] for h in hs]
        dsb = b16(ds)
        sc = [_dot(qr[h], kr[h], NT) * dm_ref[h] for h in hs]
        scb = b16(sc)
        y = [_dot(scb[h], v[h], NN) + _dot(qr[h], s[h], NN) * xi_ref[h] for h in hs]
        yc = [y[h] - jnp.mean(y[h], axis=-1, keepdims=True) for h in hs]
        rstd = [lax.rsqrt(jnp.mean(yc[h] * yc[h], axis=-1, keepdims=True) + EPS) for h in hs]
        yn = [yc[h] * rstd[h] for h in hs]
        dy = []
        for h in hs:
            g = g_ref[:, sl[h]]
            gain = gain_ref[:, sl[h]]
            sg = jax.nn.sigmoid(g)
            silu = g * sg
            dout = dy_ref[:, D_LRU + HD * h:D_LRU + HD * (h + 1)].astype(F32)
            dgain_ref[:, sl[h]] += jnp.sum(dout * silu * yn[h], axis=0, keepdims=True)
            dp_ref[:, out(3, h)] = (dout * yn[h] * gain * (sg * (1.0 + g * (1.0 - sg)))).astype(dp_ref.dtype)
            dyn = dout * silu * gain
            dy.append(rstd[h] * (dyn - jnp.mean(dyn, axis=-1, keepdims=True)
                                 - yn[h] * jnp.mean(dyn * yn[h], axis=-1, keepdims=True)))
        dyb = b16(dy)
        dqs = b16([dy[h] * xi_ref[h] for h in hs])
        dp = b16([_dot(dyb[h], v[h], NT) * dm_ref[h] for h in hs])
        dv = [_dot(scb[h], dyb[h], TN) + _dot(kz[h], dsb[h], NN) for h in hs]
        dqr = [_dot(dp[h], kr[h], NN) + _dot(dqs[h], s[h], NT) for h in hs]
        dkr = [_dot(dp[h], qr[h], TN) + _dot(v[h], dsb[h], NT) * zt_ref[h] for h in hs]
        for h in hs:
            ds_scr[h] = gch[h] * ds[h] + _dot(qr[h], dqs[h], TN)
        for h in hs:
            dp_ref[:, out(0, h)] = (dqr[h] * cs + pltpu.roll(dqr[h] * sn, HD // 2, 1)).astype(dp_ref.dtype)
            dp_ref[:, out(1, h)] = ((dkr[h] * cs + pltpu.roll(dkr[h] * sn, HD // 2, 1)) * scale).astype(dp_ref.dtype)
            dp_ref[:, out(2, h)] = dv[h].astype(dp_ref.dtype)

    def col(c, shift=0):
        return pl.BlockSpec((CH, D_RET), lambda n: (jnp.maximum(last - n - shift, 0), c))

    tab = pl.BlockSpec((CH, HD), lambda n: (last - n, 0))
    cst = _full((HEADS, CH, HD))
    vec = _full((1, D_LRU))
    mat = _full((D_LRU, D_LRU))
    blockbuf = pltpu.VMEM((CH, D_LRU), F32)
    return pl.pallas_call(
        body, name=name, grid=(NCH,),
        in_specs=[col(0), col(0, 1), col(1), col(0), col(0, 1), _full((CONV_W, D_LRU)), vec, mat, vec, mat, vec, vec, vec,
                  col(2), col(3), col(4), col(5), pl.BlockSpec((None, HEADS, HD, HD), lambda n: (last - n, 0, 0, 0)),
                  pl.BlockSpec((CH, D), lambda n: (last - n, 0)), tab, tab, cst, cst, cst, _full((1, D_RET)),
                  pl.BlockSpec(memory_space=pl.ANY)],
        out_specs=[pl.BlockSpec((CH, D_IN), lambda n: (last - n, 0)), _full((LRU_VEC_ROWS, D_LRU)), mat, mat,
                   _full((1, D_RET))],
        out_shape=[_sds((T, D_IN), MXU_DTYPE), _sds((LRU_VEC_ROWS, D_LRU), F32), _sds((D_LRU, D_LRU), F32),
                   _sds((D_LRU, D_LRU), F32), _sds((1, D_RET), F32)],
        scratch_shapes=[blockbuf, blockbuf, blockbuf, pltpu.VMEM((1, D_LRU), F32), blockbuf,
                        pltpu.VMEM((HEADS, HD, HD), F32)],
        compiler_params=_params(("arbitrary",)),
    )(proj, proj, proj, hst, hst, cw, cb, wa, ba, wx, bx, lam, gain, proj, proj, proj, proj, states, dymix,
      cos, sin, dmask, xi, zeta, ret_gain, after)


HBM = pl.BlockSpec(memory_space=pltpu.HBM)


def _place():
    return lax.axis_index("x"), lax.axis_index("y"), lax.axis_index("c")


def all_gather(arrs, name):
    n = len(arrs)

    def body(*refs):
        ins, outs = refs[:n], refs[n:2 * n]
        send_sems, recv_sems, local_sems = refs[2 * n:]
        x, y, c = _place()
        me, sibling = (x, y, c), (x, y, 1 - c)
        chips = [(1 - x, y), (x, 1 - y), (1 - x, 1 - y)]

        def copy(a, k, block, to, src=None):
            px, py, pc = block
            dst = outs[a].at[4 * px + 2 * py + pc]
            return pltpu.make_async_remote_copy(
                src_ref=dst if src is None else src, dst_ref=dst, send_sem=send_sems.at[a, k], recv_sem=recv_sems.at[a, k],
                device_id=to, device_id_type=MESH)

        mine = [pltpu.make_async_copy(ins[a], outs[a].at[4 * x + 2 * y + c], local_sems.at[a]) for a in range(n)]
        for cp in mine:
            cp.start()
        first = []
        for a in range(n):
            first.append(copy(a, 0, me, sibling, src=ins[a]))
            first += [copy(a, 1 + j, me, (*chip, c), src=ins[a]) for j, chip in enumerate(chips)]
        for cp in first:
            cp.start()
        passed = []
        for j, chip in enumerate(chips):
            for a in range(n):
                copy(a, 1 + j, (*chip, c), me).wait_recv()
                passed.append(copy(a, 4 + j, (*chip, c), sibling))
                passed[-1].start()
        for a in range(n):
            copy(a, 0, sibling, me).wait_recv()
            for j, chip in enumerate(chips):
                copy(a, 4 + j, (*chip, 1 - c), me).wait_recv()
        for cp in first + passed:
            cp.wait_send()
        for cp in mine:
            cp.wait()

    return pl.pallas_call(
        body, name=name,
        in_specs=[HBM] * n, out_specs=[HBM] * n,
        out_shape=[_sds((NDEV,) + a.shape, a.dtype) for a in arrs],
        scratch_shapes=[pltpu.SemaphoreType.DMA((n, 7)), pltpu.SemaphoreType.DMA((n, 7)), pltpu.SemaphoreType.DMA((n,))],
    )(*arrs)


SEM = pl.BlockSpec(memory_space=pltpu.SEMAPHORE)
ANY = pl.BlockSpec(memory_space=pl.ANY)
EFFECT = pltpu.SideEffectType.DATAFLOW_SIDE_EFFECTING


def _hbm(a):
    return pltpu.with_memory_space_constraint(a, pltpu.HBM)


def _hbm_like(arrs):
    return [pltpu.HBM(a.shape, a.dtype) for a in arrs]


def _dma_sems(count):
    return [pltpu.SemaphoreType.DMA(())] * count


def _ag_copy(lands, send_sems, recv_sems, per):
    def copy(a, k, block, to, src=None):
        px, py, pc = block
        dst = lands[a].at[4 * px + 2 * py + pc]
        return pltpu.make_async_remote_copy(
            src_ref=dst if src is None else src, dst_ref=dst, send_sem=send_sems[a * per + k], recv_sem=recv_sems[a * per + k],
            device_id=to, device_id_type=MESH)
    return copy


def to_wire(sel, w_in, w_out, w_gate, w_up, w_down, name):
    ffpad = FF_SHP - FF_SH

    def body(sel_ref, i_ref, o_ref, g_ref, u_ref, d_ref, oi, oo, og, ou, od):
        del sel_ref
        oi[...] = i_ref[...].astype(oi.dtype)
        oo[...] = o_ref[...].astype(oo.dtype)
        for src, dst in ((g_ref, og), (u_ref, ou), (d_ref, od)):
            dst[:FF_SH, :] = src[...].astype(dst.dtype)
            dst[FF_SH:, :] = jnp.zeros((ffpad, D), dst.dtype)

    shapes_in = [(D, IN_SH), (OUT_SH, D), (FF_SH, D), (FF_SH, D), (FF_SH, D)]
    shapes_out = [(D, IN_SH), (OUT_SH, D), (FF_SHP, D), (FF_SHP, D), (FF_SHP, D)]
    return pl.pallas_call(
        body, name=name,
        grid_spec=pltpu.PrefetchScalarGridSpec(
            num_scalar_prefetch=1, grid=(1,),
            in_specs=[pl.BlockSpec((None,) + s, lambda i, sel_ref: (sel_ref[1], 0, 0)) for s in shapes_in],
            out_specs=[pl.BlockSpec((None,) + s, lambda i, sel_ref: (sel_ref[0], 0, 0)) for s in shapes_out]),
        out_shape=[_sds((NDEV,) + s, WIRE_DTYPE) for s in shapes_out], compiler_params=_params(("arbitrary",)),
    )(sel, w_in, w_out, w_gate, w_up, w_down)


def place_blocks(sel, arrs, name):
    n = len(arrs)

    def body(sel_ref, *refs):
        del sel_ref
        for a in range(n):
            refs[n + a][...] = refs[a][...]

    def whole(a):
        nd = a.ndim
        return pl.BlockSpec(a.shape, lambda i, sel_ref: (0,) * nd)

    def mine(a):
        nd = a.ndim
        return pl.BlockSpec((None,) + a.shape, lambda i, sel_ref: (sel_ref[0],) + (0,) * nd)

    return pl.pallas_call(
        body, name=name,
        grid_spec=pltpu.PrefetchScalarGridSpec(
            num_scalar_prefetch=1, grid=(1,), in_specs=[whole(a) for a in arrs], out_specs=[mine(a) for a in arrs]),
        out_shape=[_sds((NDEV,) + a.shape, a.dtype) for a in arrs], compiler_params=_params(("arbitrary",)),
    )(sel, *arrs)


def ag_start(lands, after, name):
    n = len(lands)
    ns = 4 * n

    def body(*refs):
        lnd = refs[:n]
        send_sems, recv_sems = refs[n + 1:n + 1 + ns], refs[n + 1 + ns:n + 1 + 2 * ns]
        token = refs[-1]
        x, y, c = _place()
        me, sibling = (x, y, c), (x, y, 1 - c)
        chips = [(1 - x, y), (x, 1 - y), (1 - x, 1 - y)]
        copy = _ag_copy(lnd, send_sems, recv_sems, 4)
        for a in range(n):
            copy(a, 0, me, sibling).start()
            for j, chip in enumerate(chips):
                copy(a, 1 + j, me, (*chip, c)).start()
        token[...] = jnp.zeros_like(token)

    outs = pl.pallas_call(
        body, name=name,
        in_specs=[HBM] * n + [ANY],
        out_specs=[SEM] * (2 * ns) + [HBM] * n + [pl.BlockSpec(memory_space=pltpu.VMEM)],
        out_shape=_dma_sems(2 * ns) + _hbm_like(lands) + [_sds((8, 128), F32)],
        input_output_aliases={i: 2 * ns + i for i in range(n)},
        compiler_params=pltpu.CompilerParams(has_side_effects=EFFECT),
    )(*[_hbm(a) for a in lands], after)
    return outs[:ns], outs[ns:2 * ns], outs[2 * ns:2 * ns + n], outs[-1]


def ag_forward(send_sems, recv_sems, lands, after, name):
    n = len(lands)
    n1, n2 = 4 * n, 3 * n

    def body(*refs):
        lnd = refs[:n]
        o = n
        s1, r1 = refs[o:o + n1], refs[o + n1:o + 2 * n1]
        o += 2 * n1 + 1
        s2, r2 = refs[o:o + n2], refs[o + n2:o + 2 * n2]
        token = refs[-1]
        token[...] = jnp.zeros_like(token)
        x, y, c = _place()
        me, sibling = (x, y, c), (x, y, 1 - c)
        chips = [(1 - x, y), (x, 1 - y), (1 - x, 1 - y)]
        copy1 = _ag_copy(lnd, s1, r1, 4)
        copy2 = _ag_copy(lnd, s2, r2, 3)
        for j, chip in enumerate(chips):
            for a in range(n):
                copy1(a, 1 + j, (*chip, c), me).wait_recv()
                copy2(a, j, (*chip, c), sibling).start()
        for a in range(n):
            copy1(a, 0, sibling, me).wait_recv()
            copy1(a, 0, me, sibling).wait_send()
            for j, chip in enumerate(chips):
                copy1(a, 1 + j, me, (*chip, c)).wait_send()

    outs = pl.pallas_call(
        body, name=name,
        in_specs=[HBM] * n + [SEM] * (2 * n1) + [ANY],
        out_specs=[SEM] * (2 * n2) + [HBM] * n + [pl.BlockSpec(memory_space=pltpu.VMEM)],
        out_shape=_dma_sems(2 * n2) + _hbm_like(lands) + [_sds((8, 128), F32)],
        input_output_aliases={i: 2 * n2 + i for i in range(n)},
        compiler_params=pltpu.CompilerParams(has_side_effects=EFFECT),
    )(*lands, *send_sems, *recv_sems, after)
    return outs[:n2], outs[n2:2 * n2], outs[2 * n2:2 * n2 + n], outs[-1]


def ag_finish(send_sems, recv_sems, lands, after, name):
    n = len(lands)
    n2 = 3 * n

    def body(*refs):
        lnd = refs[:n]
        s2, r2 = refs[n:n + n2], refs[n + n2:n + 2 * n2]
        x, y, c = _place()
        me, sibling = (x, y, c), (x, y, 1 - c)
        chips = [(1 - x, y), (x, 1 - y), (1 - x, 1 - y)]
        copy2 = _ag_copy(lnd, s2, r2, 3)
        for a in range(n):
            for j, chip in enumerate(chips):
                copy2(a, j, (*chip, c), sibling).wait_send()
                copy2(a, j, (*chip, 1 - c), me).wait_recv()

    outs = pl.pallas_call(
        body, name=name,
        in_specs=[HBM] * n + [SEM] * (2 * n2) + [ANY],
        out_specs=[HBM] * n, out_shape=_hbm_like(lands),
        input_output_aliases={i: i for i in range(n)},
        compiler_params=pltpu.CompilerParams(has_side_effects=EFFECT),
    )(*lands, *send_sems, *recv_sems, after)
    return list(outs)


def rs_sibling_start(arrs, name):
    n = len(arrs)
    ns = 4 * n
    lands = [lax.empty((4,) + a.shape[1:], a.dtype) for a in arrs]

    def body(*refs):
        ins, lnd = refs[:n], refs[n:2 * n]
        send_sems, recv_sems = refs[2 * n:2 * n + ns], refs[2 * n + ns:2 * n + 2 * ns]
        x, y, c = _place()
        sibling = (x, y, 1 - c)
        for a in range(n):
            for p in range(4):
                pltpu.make_async_remote_copy(
                    src_ref=ins[a].at[2 * p + 1 - c], dst_ref=lnd[a].at[p], send_sem=send_sems[4 * a + p],
                    recv_sem=recv_sems[4 * a + p], device_id=sibling, device_id_type=MESH).start()
        refs[-1][...] = jnp.zeros_like(refs[-1])

    outs = pl.pallas_call(
        body, name=name,
        in_specs=[HBM] * (2 * n), out_specs=[SEM] * (2 * ns) + [HBM] * (2 * n) + [pl.BlockSpec(memory_space=pltpu.VMEM)],
        out_shape=_dma_sems(2 * ns) + _hbm_like(arrs) + _hbm_like(lands) + [_sds((8, 128), F32)],
        input_output_aliases={i: 2 * ns + i for i in range(2 * n)},
        compiler_params=pltpu.CompilerParams(has_side_effects=EFFECT),
    )(*[_hbm(a) for a in arrs], *[_hbm(a) for a in lands])
    return (outs[:ns], outs[ns:2 * ns], outs[2 * ns:2 * ns + n], outs[2 * ns + n:2 * ns + 2 * n]), outs[-1]


def rs_sibling_wait(send_sems, recv_sems, arrs, lands, after, name):
    n = len(arrs)
    ns = 4 * n

    def body(*refs):
        ins, lnd = refs[:n], refs[n:2 * n]
        s, r = refs[2 * n:2 * n + ns], refs[2 * n + ns:2 * n + 2 * ns]
        x, y, c = _place()
        sibling = (x, y, 1 - c)
        for a in range(n):
            for p in range(4):
                cp = pltpu.make_async_remote_copy(
                    src_ref=ins[a].at[2 * p + 1 - c], dst_ref=lnd[a].at[p], send_sem=s[4 * a + p], recv_sem=r[4 * a + p],
                    device_id=sibling, device_id_type=MESH)
                cp.wait_send()
                cp.wait_recv()

    outs = pl.pallas_call(
        body, name=name,
        in_specs=[HBM] * (2 * n) + [SEM] * (2 * ns) + [ANY], out_specs=[HBM] * (2 * n),
        out_shape=_hbm_like(arrs) + _hbm_like(lands),
        input_output_aliases={i: i for i in range(2 * n)},
        compiler_params=pltpu.CompilerParams(has_side_effects=EFFECT),
    )(*arrs, *lands, *send_sems, *recv_sems, after)
    return outs[:n], outs[n:]


def rs_chips_start(parts, name):
    n = len(parts)
    ns = 3 * n
    lands = [lax.empty((3,) + a.shape[1:], a.dtype) for a in parts]

    def body(*refs):
        ins, lnd = refs[:n], refs[n:2 * n]
        send_sems, recv_sems = refs[2 * n:2 * n + ns], refs[2 * n + ns:2 * n + 2 * ns]
        x, y, c = _place()
        chips = [(1 - x, y), (x, 1 - y), (1 - x, 1 - y)]
        for a in range(n):
            for k, (tx, ty) in enumerate(chips):
                pltpu.make_async_remote_copy(
                    src_ref=ins[a].at[2 * tx + ty], dst_ref=lnd[a].at[k], send_sem=send_sems[3 * a + k],
                    recv_sem=recv_sems[3 * a + k], device_id=(tx, ty, c), device_id_type=MESH).start()
        refs[-1][...] = jnp.zeros_like(refs[-1])

    outs = pl.pallas_call(
        body, name=name,
        in_specs=[HBM] * (2 * n), out_specs=[SEM] * (2 * ns) + [HBM] * (2 * n) + [pl.BlockSpec(memory_space=pltpu.VMEM)],
        out_shape=_dma_sems(2 * ns) + _hbm_like(parts) + _hbm_like(lands) + [_sds((8, 128), F32)],
        input_output_aliases={i: 2 * ns + i for i in range(2 * n)},
        compiler_params=pltpu.CompilerParams(has_side_effects=EFFECT),
    )(*[_hbm(a) for a in parts], *[_hbm(a) for a in lands])
    return (outs[:ns], outs[ns:2 * ns], outs[2 * ns:2 * ns + n], outs[2 * ns + n:2 * ns + 2 * n]), outs[-1]


def rs_chips_wait(send_sems, recv_sems, parts, lands, after, name):
    n = len(parts)
    ns = 3 * n

    def body(*refs):
        ins, lnd = refs[:n], refs[n:2 * n]
        s, r = refs[2 * n:2 * n + ns], refs[2 * n + ns:2 * n + 2 * ns]
        x, y, c = _place()
        chips = [(1 - x, y), (x, 1 - y), (1 - x, 1 - y)]
        for a in range(n):
            for k, (tx, ty) in enumerate(chips):
                cp = pltpu.make_async_remote_copy(
                    src_ref=ins[a].at[2 * tx + ty], dst_ref=lnd[a].at[k], send_sem=s[3 * a + k], recv_sem=r[3 * a + k],
                    device_id=(tx, ty, c), device_id_type=MESH)
                cp.wait_send()
                cp.wait_recv()

    outs = pl.pallas_call(
        body, name=name,
        in_specs=[HBM] * (2 * n) + [SEM] * (2 * ns) + [ANY], out_specs=[HBM] * (2 * n),
        out_shape=_hbm_like(parts) + _hbm_like(lands),
        input_output_aliases={i: i for i in range(2 * n)},
        compiler_params=pltpu.CompilerParams(has_side_effects=EFFECT),
    )(*parts, *lands, *send_sems, *recv_sems, after)
    return outs[:n], outs[n:]


def pair_sum(arrs, recv, c, name):
    n = len(arrs)

    def body(c_ref, *refs):
        del c_ref
        for a in range(n):
            refs[2 * n + a][...] = (refs[a][...].astype(F32) + refs[n + a][...].astype(F32)).astype(refs[2 * n + a].dtype)

    mine = [pl.BlockSpec((None,) + a.shape[1:], lambda p, c_ref: (2 * p + c_ref[0], 0, 0)) for a in arrs]
    other = [pl.BlockSpec((None,) + a.shape[1:], lambda p, c_ref: (p, 0, 0)) for a in arrs]
    return pl.pallas_call(
        body, name=name,
        grid_spec=pltpu.PrefetchScalarGridSpec(num_scalar_prefetch=1, grid=(4,), in_specs=mine + other, out_specs=other),
        out_shape=[_sds((4,) + a.shape[1:], a.dtype) for a in arrs], compiler_params=_params(("parallel",)),
    )(c, *arrs, *recv)


def _adamw(w, g, m, v):
    m = ADAM_B1 * m + (1.0 - ADAM_B1) * g
    v = ADAM_B2 * v + (1.0 - ADAM_B2) * jnp.square(g)
    m_hat = m / (1.0 - ADAM_B1 ** ADAM_STEP)
    v_hat = v / (1.0 - ADAM_B2 ** ADAM_STEP)
    return -ADAM_LR * (m_hat / (jnp.sqrt(v_hat) + ADAM_EPS) + ADAM_WD * w), m, v


def adamw_big(recv, sums, chip, w, m, v, tr, name):
    nl, rr, cc = w.shape
    cp = recv[0].shape[2]

    def body(chip_ref, *refs):
        del chip_ref
        rcv, own = refs[:nl], refs[nl:2 * nl]
        w_ref, m_ref, v_ref, g_out, d_out, m_out, v_out = refs[2 * nl:]
        for l in range(nl):
            g = ((own[l][...].astype(F32) + rcv[l][0].astype(F32)) + rcv[l][1].astype(F32)) + rcv[l][2].astype(F32)
            g = g[:, :cc]
            g_out[l] = g
            d_out[l], m_out[l], v_out[l] = _adamw(w_ref[l], g, m_ref[l], v_ref[l])

    blk = pl.BlockSpec((nl, tr, cc), lambda i, chip_ref: (0, i, 0))
    return pl.pallas_call(
        body, name=name,
        grid_spec=pltpu.PrefetchScalarGridSpec(
            num_scalar_prefetch=1, grid=(rr // tr,),
            in_specs=[pl.BlockSpec((3, tr, cp), lambda i, chip_ref: (0, i, 0))] * nl
            + [pl.BlockSpec((None, tr, cp), lambda i, chip_ref: (chip_ref[0], i, 0))] * nl + [blk, blk, blk],
            out_specs=[blk] * 4),
        out_shape=[_sds(w.shape, F32)] * 4, compiler_params=_params(("parallel",)),
    )(chip, *recv, *sums, w, m, v)


SMALL_ROWS = 24


def small_grads(lvec, g_ret, g_mix, g_ffn, g_final, loss_part, dwa, dwx, name):
    def body(lvec_ref, ret_ref, mix_ref, ffn_ref, fin_ref, loss_ref, dwa_ref, dwx_ref, v_ref, g_ref):
        v_ref[16:SMALL_ROWS, :] = jnp.zeros((SMALL_ROWS - 16, D_LRU), F32)
        v_ref[16:17, 0:128] = loss_ref[0:1, :]
        v_ref[0:9, :] = lvec_ref[0:9, :]
        v_ref[9:10, :] = ret_ref[...]
        for r, src in ((10, mix_ref), (12, ffn_ref), (14, fin_ref)):
            v_ref[r:r + 1, :] = src[:, :D_LRU]
            v_ref[r + 1:r + 2, :] = src[:, D_LRU:]
        for k, src in enumerate((dwa_ref, dwx_ref)):
            for g in range(LRU_BLOCKS):
                rows = slice(LRU_BD * g, LRU_BD * (g + 1))
                g_ref[D_LRU * k + LRU_BD * g:D_LRU * k + LRU_BD * (g + 1), :] = src[rows, rows]

    ins = [lvec, g_ret, g_mix, g_ffn, g_final, loss_part, dwa, dwx]
    return pl.pallas_call(
        body, name=name, grid=(1,), in_specs=[_full(a.shape) for a in ins],
        out_specs=[_full((SMALL_ROWS, D_LRU)), _full((2 * D_LRU, LRU_BD))],
        out_shape=[_sds((SMALL_ROWS, D_LRU), F32), _sds((2 * D_LRU, LRU_BD), F32)], compiler_params=_params(("arbitrary",)),
    )(*ins)


def sum_devices(arrs, name):
    n = len(arrs)

    def body(*refs):
        for a in range(n):
            acc = refs[a][0]
            for j in range(1, NDEV):
                acc = acc + refs[a][j]
            refs[n + a][...] = acc

    return pl.pallas_call(
        body, name=name, grid=(1,), in_specs=[_full(a.shape) for a in arrs], out_specs=[_full(a.shape[1:]) for a in arrs],
        out_shape=[_sds(a.shape[1:], F32) for a in arrs], compiler_params=_params(("arbitrary",)),
    )(*arrs)


def adamw_small(gs, ws, ms, vs, name):
    n = len(gs)

    def body(*refs):
        for a in range(n):
            g, w, m, v = (refs[k * n + a][...] for k in range(4))
            refs[4 * n + a][...], refs[5 * n + a][...], refs[6 * n + a][...] = _adamw(w, g, m, v)

    specs = [_full(a.shape) for a in ws]
    outs = pl.pallas_call(
        body, name=name, grid=(1,), in_specs=specs * 4, out_specs=specs * 3, out_shape=[_sds(a.shape, F32) for a in ws] * 3,
        compiler_params=_params(("arbitrary",)),
    )(*gs, *ws, *ms, *vs)
    return outs[:n], outs[n:2 * n], outs[2 * n:]


def block_diag(wa, wx, name):
    def body(wa_ref, wx_ref, oa_ref, ox_ref):
        for src, dst in ((wa_ref, oa_ref), (wx_ref, ox_ref)):
            dst[...] = jnp.zeros_like(dst)
            for g in range(LRU_BLOCKS):
                rows = slice(LRU_BD * g, LRU_BD * (g + 1))
                dst[rows, rows] = src[g].astype(dst.dtype)

    ispec = pl.BlockSpec((None, LRU_BLOCKS, LRU_BD, LRU_BD), lambda l: (l, 0, 0, 0))
    ospec = pl.BlockSpec((None, D_LRU, D_LRU), lambda l: (l, 0, 0))
    return pl.pallas_call(
        body, name=name, grid=(wa.shape[0],), in_specs=[ispec, ispec], out_specs=[ospec, ospec],
        out_shape=[_sds((wa.shape[0], D_LRU, D_LRU), MXU_DTYPE)] * 2, compiler_params=_params(("parallel",)),
    )(wa, wx)


REP_NAMES = ["norm_mix", "conv_b", "gate_a_w", "gate_a_b", "gate_x_w", "gate_x_b", "lru_lambda", "lru_out_norm",
             "ret_out_norm", "norm_ffn", "norm_final"]


def kernel(x, meta_tokens, norm_mix, w_in, conv_w, conv_b, gate_a_w, gate_a_b, gate_x_w, gate_x_b, lru_lambda, lru_out_norm, ret_out_norm, w_out, norm_ffn, w_gate, w_up, w_down, norm_final, loss_target, m_meta_tokens, m_norm_mix, m_w_in, m_conv_w, m_conv_b, m_gate_a_w, m_gate_a_b, m_gate_x_w, m_gate_x_b, m_lru_lambda, m_lru_out_norm, m_ret_out_norm, m_w_out, m_norm_ffn, m_w_gate, m_w_up, m_w_down, m_norm_final, v_meta_tokens, v_norm_mix, v_w_in, v_conv_w, v_conv_b, v_gate_a_w, v_gate_a_b, v_gate_x_w, v_gate_x_b, v_lru_lambda, v_lru_out_norm, v_ret_out_norm, v_w_out, v_norm_ffn, v_w_gate, v_w_up, v_w_down, v_norm_final):
    xi, yi, ci = _place()
    dev = 4 * xi + 2 * yi + ci
    c_arr = jnp.reshape(ci, (1,)).astype(jnp.int32)
    dev_arr = jnp.reshape(dev, (1,)).astype(jnp.int32)

    meta_g, conv_g = all_gather([meta_tokens, conv_w], "ag_small")
    meta_full = jnp.transpose(meta_g, (1, 0, 2)).reshape(N_META, D)
    conv_full = jnp.transpose(conv_g, (1, 2, 0, 3)).reshape(DEPTH, CONV_W, D_LRU)
    tr_ = lambda a: jnp.transpose(a, (0, 2, 1))
    w_gate_t, m_w_gate_t, v_w_gate_t = tr_(w_gate), tr_(m_w_gate), tr_(v_w_gate)
    w_up_t, m_w_up_t, v_w_up_t = tr_(w_up), tr_(m_w_up), tr_(v_w_up)
    level1 = []
    token = meta_g
    for l in range(DEPTH):
        sel = jnp.stack([dev, jnp.int32(l)]).astype(jnp.int32)
        lands = to_wire(sel, w_in, w_out, w_gate_t, w_up_t, w_down, "to_wire")
        s1, r1, lands, token = ag_start(lands, token, f"ag_start_{l}")
        level1.append((s1, r1, lands))

    def as_weights(gi, go, gg, gu, gd):
        return dict(w_in=gi, w_out=go.reshape(D, D), w_gate=gg.reshape(D_FFP, D), w_up=gu.reshape(D_FFP, D),
                    w_down=gd.reshape(D_FFP, D))

    tables = _ret_tables()
    row = lambda a: a.reshape(1, -1)

    h = jnp.concatenate([jnp.zeros((PAD, D), F32), meta_full, x[0]], axis=0)
    saved, gathered = [], []
    s1, r1, lands = level1[0]
    s2, r2, first, order = ag_forward(s1[:4], r1[:4], lands[:1], token, "ag_forward_0_w_in")
    w_in_next = ag_finish(s2, r2, first, h, "ag_finish_0_w_in")[0]
    wa_dense, wx_dense = block_diag(gate_a_w, gate_x_w, "block_diag")
    for l in range(DEPTH):
        small = dict(cw=conv_full[l], cb=row(conv_b[l]), wa=wa_dense[l], ba=row(gate_a_b[l]),
                     wx=wx_dense[l], bx=row(gate_x_b[l]), lam=row(lru_lambda[l]),
                     gain=row(lru_out_norm[l]))
        s1, r1, lands = level1[l]
        hn1 = rmsnorm_fwd(h, row(norm_mix[l]), "rms_fwd")
        proj = mm_blocked_nn(hn1, w_in_next, F32, "proj")
        if l > 0:
            s2, r2, rest, order = ag_forward(s1[4:], r1[4:], lands[1:], proj, f"ag_forward_{l}_rest")
            ymix, hst, states = mix_fwd(proj, tables=tables, ret_gain=row(ret_out_norm[l]), after=order, name="mix_fwd", **small)
            w = as_weights(w_in_next, *ag_finish(s2, r2, rest, ymix, f"ag_finish_{l}_rest"))
            h_mid = mm_nn_res(ymix, w["w_out"], h, order, "out_proj")
        else:
            ymix, hst, states = mix_fwd(proj, tables=tables, ret_gain=row(ret_out_norm[l]), after=order, name="mix_fwd", **small)
            s2, r2, mid, order = ag_forward(s1[4:16], r1[4:16], lands[1:4], ymix, "ag_forward_0_mid")
            mids = ag_finish(s2, r2, mid, order, "ag_finish_0_mid")
            w = dict(w_in=w_in_next, w_out=mids[0].reshape(D, D), w_gate=mids[1].reshape(D_FFP, D), w_up=mids[2].reshape(D_FFP, D))
            h_mid = mm_nn_res(ymix, w["w_out"], h, order, "out_proj")
            s2d, r2d, down, order = ag_forward(s1[16:], r1[16:], lands[4:], h_mid, "ag_forward_0_down")
        hn2 = rmsnorm_fwd(h_mid, row(norm_ffn[l]), "rms_fwd")
        act_dgate, act_dup, act = ffn_up(hn2, w["w_gate"], w["w_up"], "ffn_up")
        if l == 0:
            w["w_down"] = ag_finish(s2d, r2d, down, act, "ag_finish_0_down")[0].reshape(D_FFP, D)
        gathered.append(w)
        if l + 1 < DEPTH:
            s1n, r1n, landsn = level1[l + 1]
            s2, r2, first, order = ag_forward(s1n[:4], r1n[:4], landsn[:1], act, f"ag_forward_{l + 1}_w_in")
        h_out = mm_nn_res(act, w["w_down"], h_mid, order, "ffn_down")
        if l + 1 < DEPTH:
            w_in_next = ag_finish(s2, r2, first, h_out, f"ag_finish_{l + 1}_w_in")[0]
        saved.append(dict(h=h, hn1=hn1, proj=proj, hst=hst, states=states, ymix=ymix, h_mid=h_mid, hn2=hn2, act_dgate=act_dgate, act_dup=act_dup,
                          act=act, small=small))
        h = h_out

    loss_p, dh, dh_b, g_norm_final = loss_head(h, row(norm_final), loss_target[0], "loss_head")

    small_v = [None] * DEPTH
    small_w = [None] * DEPTH
    inflight = []
    sib = None
    order = loss_p

    def sibling_done(l, tag, names, sib, after):
        parts, got = rs_sibling_wait(*sib, after, f"rs_sibling_wait_{tag}")
        sums = pair_sum(parts, got, c_arr, "pair_sum")
        flying, started = rs_chips_start(sums, f"rs_chips_start_{tag}")
        inflight.append((l, tag, names, flying))
        return started

    for l in reversed(range(DEPTH)):
        w, s = gathered[l], saved[l]
        dgate, dup = ffn_down_bwd(dh_b, w["w_down"], s["act_dgate"], s["act_dup"], order, "ffn_down_bwd")
        dwd = mm_tn(s["act"], dh_b, PAIR, order, "dw_down").reshape(NDEV, FF_SHP, D)
        dwg, dwu = (g.reshape(NDEV, FF_SHP, D) for g in mm_tn_two(dgate, dup, s["hn2"], PAIR, order, "dw_rows"))
        split = l <= 1
        if split:
            ffn_sib, order = rs_sibling_start([dwg, dwu, dwd], f"rs_sibling_start_{l}_ffn")
        dhn2 = mm_rows_nn([(dgate, w["w_gate"]), (dup, w["w_up"])], order, "ffn_up_bwd")
        if sib is not None:
            order = sibling_done(l + 1, sib_tag, sib_names, sib, dhn2)
        dh_mid, dh_mid_b, g_norm_ffn = rmsnorm_bwd(s["h_mid"], row(norm_ffn[l]), dhn2, dh, "rms_bwd")
        dymix, dwo = out_proj_bwd(dh_mid_b, w["w_out"], s["ymix"], order, "out_proj_bwd")
        dwo = dwo.reshape(NDEV, OUT_SH, D)
        if split:
            order = sibling_done(l, f"{l}_ffn", ("w_gate", "w_up", "w_down"), ffn_sib, dymix)
        dproj, lvec, dwa, dwx, g_ret_norm = mix_bwd(s["proj"], s["hst"], s["states"], dymix, tables=tables,
                                                    ret_gain=row(ret_out_norm[l]), after=order, name="mix_bwd", **s["small"])
        dwi = mm_tn_blocked(s["hn1"], dproj, "dw_blocked")
        dhn1 = mm_blocked_nt([(dproj, w["w_in"])], order, "proj_bwd")
        dh, dh_b, g_norm_mix = rmsnorm_bwd(s["h"], row(norm_mix[l]), dhn1, dh_mid, "rms_bwd")

        g_fin, loss_part = (g_norm_final, loss_p) if l == 0 else (jnp.zeros((1, D), F32), jnp.zeros((8, 128), F32))
        small_v[l], small_w[l] = small_grads(lvec, g_ret_norm, g_norm_mix, g_norm_ffn, g_fin, loss_part, dwa, dwx,
                                             "small_grads")
        if split:
            sib_tag, sib_names = f"{l}_mix", ("w_in", "w_out")
            sib, order = rs_sibling_start([dwi, dwo], f"rs_sibling_start_{l}_mix")
        else:
            sib_tag, sib_names = str(l), ("w_in", "w_gate", "w_up", "w_out", "w_down")
            sib, order = rs_sibling_start([dwi, dwg, dwu, dwo, dwd], f"rs_sibling_start_{l}")
        if l == 1:
            early = place_blocks(dev_arr, [jnp.stack(small_v[1:]), jnp.stack(small_w[1:])], "place_grads")
            early_sems = ag_start(early, order, "ag_start_grads")
            order = early_sems[3]

    grad_x = dh[X0:][None]
    g_meta = dh[PAD:X0]

    late = all_gather([small_v[0], small_w[0], g_meta], "ag_grads")
    s2, r2, lands, _ = ag_forward(early_sems[0], early_sems[1], early_sems[2], dh, "ag_forward_grads")
    gath_early = ag_finish(s2, r2, lands, late[0], "ag_finish_grads")
    sibling_done(0, sib_tag, sib_names, sib, late[0])
    v0, w0, meta_sum, v123, w123 = sum_devices(list(late) + list(gath_early), "sum_devices")
    loss = v0[16, 0]
    vecs = jnp.concatenate([v0[None], v123])
    gws = jnp.concatenate([w0[None], w123])
    blocks = (DEPTH, LRU_BLOCKS, LRU_BD)
    small_g = dict(
        conv_w=lax.dynamic_slice_in_dim(vecs[:, 0:CONV_W], dev * (D_LRU // NDEV), D_LRU // NDEV, axis=2),
        conv_b=vecs[:, 4], gate_a_b=vecs[:, 5].reshape(blocks), gate_x_b=vecs[:, 6].reshape(blocks),
        lru_lambda=vecs[:, 7], lru_out_norm=vecs[:, 8], ret_out_norm=vecs[:, 9],
        norm_mix=vecs[:, 10:12].reshape(DEPTH, D), norm_ffn=vecs[:, 12:14].reshape(DEPTH, D),
        norm_final=v0[14:16].reshape(1, D),
        gate_a_w=gws[:, :D_LRU].reshape(blocks + (LRU_BD,)), gate_x_w=gws[:, D_LRU:].reshape(blocks + (LRU_BD,)),
        meta_tokens=lax.dynamic_slice_in_dim(meta_sum, dev * (D // NDEV), D // NDEV, axis=1))
    given = dict(norm_mix=(norm_mix, m_norm_mix, v_norm_mix), conv_b=(conv_b, m_conv_b, v_conv_b),
                 gate_a_w=(gate_a_w, m_gate_a_w, v_gate_a_w), gate_a_b=(gate_a_b, m_gate_a_b, v_gate_a_b),
                 gate_x_w=(gate_x_w, m_gate_x_w, v_gate_x_w), gate_x_b=(gate_x_b, m_gate_x_b, v_gate_x_b),
                 lru_lambda=(lru_lambda, m_lru_lambda, v_lru_lambda), lru_out_norm=(lru_out_norm, m_lru_out_norm, v_lru_out_norm),
                 ret_out_norm=(ret_out_norm, m_ret_out_norm, v_ret_out_norm), norm_ffn=(norm_ffn, m_norm_ffn, v_norm_ffn),
                 norm_final=tuple(a.reshape(1, D) for a in (norm_final, m_norm_final, v_norm_final)),
                 conv_w=(conv_w, m_conv_w, v_conv_w), meta_tokens=(meta_tokens, m_meta_tokens, v_meta_tokens))
    small_names = REP_NAMES + ["conv_w", "meta_tokens"]
    upd = adamw_small([small_g[n] for n in small_names], *[[given[n][k] for n in small_names] for k in range(3)],
                      "adamw_small")
    small_out = [dict(zip(small_names, u)) for u in upd]
    for d_ in [small_g] + small_out:
        d_["norm_final"] = d_["norm_final"].reshape(D)

    arrived = {}

    def wait_for(entries, after):
        for l, tag, names, flying in entries:
            sums, recv = rs_chips_wait(*flying, after, f"rs_chips_wait_{tag}")
            for i, n in enumerate(names):
                arrived[l, n] = (recv[i], sums[i])

    chip = jnp.reshape(2 * xi + yi, (1,)).astype(jnp.int32)

    def finish(wname, w_, m_, v_, tr):
        return adamw_big([arrived[l, wname][0] for l in range(DEPTH)], [arrived[l, wname][1] for l in range(DEPTH)], chip,
                         w_, m_, v_, tr, "adamw_" + wname)

    wait_for(inflight[:-1], upd[0][0])
    o_gate = [tr_(o) for o in finish("w_gate", w_gate_t, m_w_gate_t, v_w_gate_t, 32)]
    o_up = [tr_(o) for o in finish("w_up", w_up_t, m_w_up_t, v_w_up_t, 32)]
    o_down = finish("w_down", w_down, m_w_down, v_w_down, 32)
    wait_for(inflight[-1:], o_down[0])
    o_in = finish("w_in", w_in, m_w_in, v_w_in, 256)
    o_out = finish("w_out", w_out, m_w_out, v_w_out, 64)

    bigs = dict(w_in=o_in, w_out=o_out, w_gate=o_gate, w_up=o_up, w_down=o_down)
    order = ["meta_tokens", "norm_mix", "w_in", "conv_w", "conv_b", "gate_a_w", "gate_a_b", "gate_x_w", "gate_x_b", "lru_lambda",
             "lru_out_norm", "ret_out_norm", "w_out", "norm_ffn", "w_gate", "w_up", "w_down", "norm_final"]
    grads = [bigs[n][0] if n in bigs else small_g[n] for n in order]
    rest = [[bigs[n][k + 1] if n in bigs else small_out[k][n] for n in order] for k in range(3)]
    return (loss, grad_x, *grads, *rest[0], *rest[1], *rest[2])
```

```python
import numpy as np
import jax
import jax.numpy as jnp
from jax import lax
from jax.experimental import pallas as pl
from jax.experimental.pallas import tpu as pltpu

F32, BF16 = jnp.float32, jnp.bfloat16
MXU_DTYPE = BF16
WIRE_DTYPE = BF16

D = 1024
SEQ = 2048
DEPTH = 4
N_META = 16
CH = 128
PAD = (-(SEQ + N_META)) % CH
T = SEQ + N_META + PAD
NCH = T // CH
X0 = PAD + N_META
D_LRU = 512
LRU_BLOCKS = 8
LRU_BD = 64
CONV_W = 4
LRU_C = 8.0
D_RET = 512
HEADS = 4
HD = 128
ROPE_BASE = 10000.0
D_IN = 3072
D_FF = 2816
NDEV = 8
IN_SH = D_IN // NDEV
FF_SH = D_FF // NDEV
FF_SHP = 384
D_FFP = NDEV * FF_SHP
OUT_SH = D // NDEV
EPS = 1e-6
TM = 544
VMEM_LIMIT = 56 * 2**20
MESH = pl.DeviceIdType.MESH

ADAM_LR, ADAM_B1, ADAM_B2, ADAM_EPS, ADAM_WD, ADAM_STEP = 0.001, 0.9, 0.999, 1e-08, 0.01, 10

NN = ((1,), (0,))
NT = ((1,), (1,))
TN = ((0,), (0,))


def _dot(a, b, dims):
    return lax.dot_general(a.astype(MXU_DTYPE), b.astype(MXU_DTYPE), (dims, ((), ())), preferred_element_type=F32)


def _sds(shape, dtype):
    return jax.ShapeDtypeStruct(shape, dtype)


def _params(sem=None):
    return pltpu.CompilerParams(dimension_semantics=sem, vmem_limit_bytes=VMEM_LIMIT)


def _full(shape):
    n = len(shape)
    return pl.BlockSpec(shape, lambda *_: (0,) * n)


def rmsnorm_fwd(h, gain, name):
    def body(h_ref, g_ref, o_ref):
        x = h_ref[...]
        ms = jnp.mean(x * x, axis=-1, keepdims=True)
        o_ref[...] = (x * lax.rsqrt(ms + EPS) * g_ref[...]).astype(o_ref.dtype)

    return pl.pallas_call(
        body, name=name, grid=(T // TM,),
        in_specs=[pl.BlockSpec((TM, D), lambda i: (i, 0)), _full((1, D))],
        out_specs=pl.BlockSpec((TM, D), lambda i: (i, 0)),
        out_shape=_sds((T, D), MXU_DTYPE), compiler_params=_params(("parallel",)),
    )(h, gain)


def rmsnorm_bwd(h, gain, dhn, dres, name):
    def body(h_ref, g_ref, dhn_ref, dres_ref, dh_ref, dhb_ref, dg_ref):
        x = h_ref[...]
        rstd = lax.rsqrt(jnp.mean(x * x, axis=-1, keepdims=True) + EPS)
        xhat = x * rstd
        dy = dhn_ref[...]
        dyg = dy * g_ref[...]
        dh = dres_ref[...] + rstd * (dyg - xhat * jnp.mean(dyg * xhat, axis=-1, keepdims=True))
        dh_ref[...] = dh
        dhb_ref[...] = dh.astype(dhb_ref.dtype)

        @pl.when(pl.program_id(0) == 0)
        def _():
            dg_ref[...] = jnp.zeros_like(dg_ref)
        dg_ref[...] += jnp.sum(dy * xhat, axis=0, keepdims=True)

    row = pl.BlockSpec((TM, D), lambda i: (i, 0))
    return pl.pallas_call(
        body, name=name, grid=(T // TM,),
        in_specs=[row, _full((1, D)), row, row],
        out_specs=[row, row, _full((1, D))],
        out_shape=[_sds((T, D), F32), _sds((T, D), MXU_DTYPE), _sds((1, D), F32)], compiler_params=_params(("arbitrary",)),
    )(h, gain, dhn, dres)


def loss_head(h, gain, target, name):
    def body(h_ref, g_ref, t_ref, loss_ref, dh_ref, dhb_ref, dg_ref):
        i = pl.program_id(0)

        @pl.when(i == 0)
        def _():
            loss_ref[...] = jnp.zeros_like(loss_ref)
            dg_ref[...] = jnp.zeros_like(dg_ref)
            dh_ref[...] = jnp.zeros_like(dh_ref)
            dhb_ref[...] = jnp.zeros_like(dhb_ref)

        @pl.when(i > 0)
        def _():
            x = h_ref[...]
            g = g_ref[...]
            rstd = lax.rsqrt(jnp.mean(x * x, axis=-1, keepdims=True) + EPS)
            xhat = x * rstd
            err = xhat * g - t_ref[...]
            loss_ref[...] += 0.5 * jnp.sum(jnp.mean(err * err, axis=-1, keepdims=True), axis=0, keepdims=True)
            dy = err * (1.0 / D)
            dyg = dy * g
            dh = rstd * (dyg - xhat * jnp.mean(dyg * xhat, axis=-1, keepdims=True))
            dh_ref[...] = dh
            dhb_ref[...] = dh.astype(dhb_ref.dtype)
            dg_ref[...] += jnp.sum(dy * xhat, axis=0, keepdims=True)

    row = pl.BlockSpec((CH, D), lambda i: (i, 0))
    return pl.pallas_call(
        body, name=name, grid=(NCH,),
        in_specs=[row, _full((1, D)), pl.BlockSpec((CH, D), lambda i: (jnp.maximum(i - 1, 0), 0))],
        out_specs=[_full((8, 128)), row, row, _full((1, D))],
        out_shape=[_sds((8, 128), F32), _sds((T, D), F32), _sds((T, D), MXU_DTYPE), _sds((1, D), F32)],
        compiler_params=_params(("arbitrary",)),
    )(h, gain, target)


PAIR = 2 * IN_SH
NPAIR = NDEV // 2
BN = 256
FB = 512


def _pair_cols(w_ref):
    return jnp.concatenate([w_ref[0], w_ref[1]], axis=1)


W_PAIR = lambda k: pl.BlockSpec((2, k, IN_SH), lambda j: (j, 0, 0))
COLS_PAIR = pl.BlockSpec((T, PAIR), lambda j: (0, j))
ANYSPEC = pl.BlockSpec(memory_space=pl.ANY)


def mm_blocked_nn(a, w, out_dtype, name):
    k = a.shape[1]

    def body(a_ref, w_ref, o_ref):
        o_ref[...] = _dot(a_ref[...], _pair_cols(w_ref), NN).astype(o_ref.dtype)

    return pl.pallas_call(
        body, name=name, grid=(NPAIR,),
        in_specs=[_full((T, k)), W_PAIR(k)], out_specs=COLS_PAIR,
        out_shape=_sds((T, NDEV * IN_SH), out_dtype), compiler_params=_params(("parallel",)),
    )(a, w)


def mm_nn_res(a, w, res, after, name):
    k = a.shape[1]

    def body(a_ref, w_ref, r_ref, after_ref, o_ref):
        del after_ref
        o_ref[...] = r_ref[...] + _dot(a_ref[...], w_ref[...], NN)

    col = pl.BlockSpec((T, BN), lambda j: (0, j))
    return pl.pallas_call(
        body, name=name, grid=(D // BN,),
        in_specs=[_full((T, k)), pl.BlockSpec((k, BN), lambda j: (0, j)), col, ANYSPEC], out_specs=col,
        out_shape=_sds((T, D), F32), compiler_params=_params(("parallel",)),
    )(a, w, res, after)


def ffn_up(hn, wg, wu, name):
    def body(a_ref, wg_ref, wu_ref, dg_ref, du_ref, act_ref):
        a = a_ref[...]
        for c in range(FB // BN):
            cols = slice(BN * c, BN * (c + 1))
            g = _dot(a, wg_ref[cols, :], NT)
            u = _dot(a, wu_ref[cols, :], NT)
            sg = jax.nn.sigmoid(g)
            silu = g * sg
            dg_ref[:, cols] = (u * (sg * (1.0 + g * (1.0 - sg)))).astype(dg_ref.dtype)
            du_ref[:, cols] = silu.astype(du_ref.dtype)
            act_ref[:, cols] = (silu * u).astype(act_ref.dtype)

    wspec = pl.BlockSpec((FB, D), lambda j: (j, 0))
    ospec = pl.BlockSpec((T, FB), lambda j: (0, j))
    return pl.pallas_call(
        body, name=name, grid=(D_FFP // FB,),
        in_specs=[_full((T, D)), wspec, wspec], out_specs=[ospec] * 3,
        out_shape=[_sds((T, D_FFP), MXU_DTYPE)] * 3, compiler_params=_params(("parallel",)),
    )(hn, wg, wu)


def ffn_down_bwd(dh, wd, dact_dgate, dact_dup, after, name):
    def body(dh_ref, wd_ref, g_ref, u_ref, after_ref, dg_ref, du_ref):
        del after_ref
        dh = dh_ref[...]
        for c in range(FB // BN):
            cols = slice(BN * c, BN * (c + 1))
            dact = _dot(dh, wd_ref[cols, :], NT)
            dg_ref[:, cols] = (dact * g_ref[:, cols].astype(F32)).astype(dg_ref.dtype)
            du_ref[:, cols] = (dact * u_ref[:, cols].astype(F32)).astype(du_ref.dtype)

    blk = pl.BlockSpec((T, FB), lambda j: (0, j))
    return pl.pallas_call(
        body, name=name, grid=(D_FFP // FB,),
        in_specs=[_full((T, D)), pl.BlockSpec((FB, D), lambda j: (j, 0)), blk, blk, ANYSPEC],
        out_specs=[blk, blk],
        out_shape=[_sds((T, D_FFP), MXU_DTYPE)] * 2, compiler_params=_params(("parallel",)),
    )(dh, wd, dact_dgate, dact_dup, after)


def mm_blocked_nt(pairs, after, name):
    n = len(pairs)

    def body(*refs):
        o_ref = refs[2 * n + 1]

        @pl.when(pl.program_id(0) == 0)
        def _():
            o_ref[...] = jnp.zeros_like(o_ref)
        for p in range(n):
            o_ref[...] += _dot(refs[2 * p][...], _pair_cols(refs[2 * p + 1]), NT)

    specs, args = [], []
    for a, w in pairs:
        specs += [COLS_PAIR, W_PAIR(D)]
        args += [a, w]
    return pl.pallas_call(
        body, name=name, grid=(NPAIR,), in_specs=specs + [ANYSPEC], out_specs=_full((T, D)),
        out_shape=_sds((T, D), F32), compiler_params=_params(("arbitrary",)),
    )(*args, after)


def mm_tn_two(a1, a2, b, bm, after, name):
    m = a1.shape[1]

    def body(a1_ref, a2_ref, b_ref, after_ref, o1_ref, o2_ref):
        del after_ref
        b = b_ref[...]
        o1_ref[...] = _dot(a1_ref[...], b, TN).astype(o1_ref.dtype)
        o2_ref[...] = _dot(a2_ref[...], b, TN).astype(o2_ref.dtype)

    blk = pl.BlockSpec((T, bm), lambda i: (0, i))
    out = pl.BlockSpec((bm, D), lambda i: (i, 0))
    return pl.pallas_call(
        body, name=name, grid=(m // bm,),
        in_specs=[blk, blk, _full((T, D)), ANYSPEC], out_specs=[out, out],
        out_shape=[_sds((m, D), WIRE_DTYPE)] * 2, compiler_params=_params(("parallel",)),
    )(a1, a2, b, after)


def out_proj_bwd(dh, w, ymix, after, name):
    def body(dh_ref, w_ref, y_ref, after_ref, dy_ref, dw_ref):
        del after_ref
        dh_ = dh_ref[...]
        dy_ref[...] = _dot(dh_, w_ref[...], NT)
        dw_ref[...] = _dot(y_ref[...], dh_, TN).astype(dw_ref.dtype)

    return pl.pallas_call(
        body, name=name, grid=(D // BN,),
        in_specs=[_full((T, D)), pl.BlockSpec((BN, D), lambda j: (j, 0)), pl.BlockSpec((T, BN), lambda j: (0, j)), ANYSPEC],
        out_specs=[pl.BlockSpec((T, BN), lambda j: (0, j)), pl.BlockSpec((BN, D), lambda j: (j, 0))],
        out_shape=[_sds((T, D), F32), _sds((D, D), WIRE_DTYPE)], compiler_params=_params(("parallel",)),
    )(dh, w, ymix, after)


def mm_rows_nn(pairs, after, name):
    n = len(pairs)

    def body(*refs):
        o_ref = refs[2 * n + 1]

        @pl.when(pl.program_id(0) == 0)
        def _():
            o_ref[...] = jnp.zeros_like(o_ref)
        for p in range(n):
            o_ref[...] += _dot(refs[2 * p][...], refs[2 * p + 1][...], NN)

    specs, args = [], []
    for a, w in pairs:
        specs += [pl.BlockSpec((T, FB), lambda j: (0, j)), pl.BlockSpec((FB, D), lambda j: (j, 0))]
        args += [a, w]
    return pl.pallas_call(
        body, name=name, grid=(D_FFP // FB,), in_specs=specs + [ANYSPEC], out_specs=_full((T, D)),
        out_shape=_sds((T, D), F32), compiler_params=_params(("arbitrary",)),
    )(*args, after)


def mm_tn_blocked(a, b, name):
    def body(a_ref, b_ref, o_ref):
        o = _dot(a_ref[...], b_ref[...], TN).astype(o_ref.dtype)
        o_ref[0] = o[:, :IN_SH]
        o_ref[1] = o[:, IN_SH:]

    return pl.pallas_call(
        body, name=name, grid=(NPAIR,),
        in_specs=[_full((T, D)), COLS_PAIR], out_specs=W_PAIR(D),
        out_shape=_sds((NDEV, D, IN_SH), WIRE_DTYPE), compiler_params=_params(("parallel",)),
    )(a, b)


def mm_tn(a, b, bm, after, name):
    m = a.shape[1]

    def body(a_ref, b_ref, after_ref, o_ref):
        del after_ref
        o_ref[...] = _dot(a_ref[...], b_ref[...], TN).astype(o_ref.dtype)

    return pl.pallas_call(
        body, name=name, grid=(m // bm,),
        in_specs=[pl.BlockSpec((T, bm), lambda i: (0, i)), _full((T, D)), ANYSPEC],
        out_specs=pl.BlockSpec((bm, D), lambda i: (i, 0)),
        out_shape=_sds((m, D), WIRE_DTYPE), compiler_params=_params(("parallel",)),
    )(a, b, after)


def _softplus_neg(lam):
    return jnp.maximum(-lam, 0.0) + jnp.log1p(jnp.exp(-jnp.abs(lam)))


def _lru_gates(pa, px, xc, lam):
    r = jax.nn.sigmoid(pa)
    ig = jax.nn.sigmoid(px)
    sp = _softplus_neg(lam)
    log_a = -LRU_C * r * sp
    a = jnp.exp(log_a)
    mult = jnp.sqrt(-jnp.tanh(log_a) * (a * a + 1.0))
    return a, mult * (ig * xc), (r, ig, sp, mult)


def _lru_gates_vjp(da, db, xc, lam, a, r, ig, sp, mult):
    dmult = db * (ig * xc)
    du = db * mult
    dlog_a = da * a - dmult * (a * a) / mult
    dr = dlog_a * (-LRU_C * sp)
    dlam = jnp.sum(dlog_a * (-LRU_C * r), axis=0, keepdims=True) * (-jax.nn.sigmoid(-lam))
    dpa = dr * (r * (1.0 - r))
    dpx = (du * xc) * (ig * (1.0 - ig))
    return dpa, dpx, du * ig, dlam


def _lru_out(h, g, gain):
    z = h * jax.nn.gelu(g)
    return z * lax.rsqrt(jnp.mean(z * z, axis=-1, keepdims=True) + EPS) * gain


def _conv_taps(x, xprev, row):
    taps = [x]
    for s in range(1, CONV_W):
        taps.append(jnp.where(row < s, pltpu.roll(xprev, s, 0), pltpu.roll(x, s, 0)))
    return taps


def _conv(taps, cw_ref, cb):
    xc = cb + cw_ref[CONV_W - 1:CONV_W, :] * taps[0]
    for s in range(1, CONV_W):
        xc = xc + cw_ref[CONV_W - 1 - s:CONV_W - s, :] * taps[s]
    return xc


def _lru_fwd_block(i, x_ref, g_ref, cw_ref, cb_ref, wa_ref, ba_ref, wx_ref, bx_ref, lam_ref, gain_ref, y_ref, h_ref,
                   xprev_scr, a_scr, b_scr, carry_scr):
    @pl.when(i == 0)
    def _():
        xprev_scr[...] = jnp.zeros_like(xprev_scr)
        carry_scr[...] = jnp.zeros_like(carry_scr)

    x = x_ref[...]
    row = lax.broadcasted_iota(jnp.int32, (CH, D_LRU), 0)
    xc = _conv(_conv_taps(x, xprev_scr[...], row), cw_ref, cb_ref[...])
    pa = _dot(xc, wa_ref[...], NN) + ba_ref[...]
    px = _dot(xc, wx_ref[...], NN) + bx_ref[...]
    a, b, _ = _lru_gates(pa, px, xc, lam_ref[...])
    a_scr[...] = a
    b_scr[...] = jnp.where(i * CH + row >= PAD, b, 0.0)
    h = carry_scr[...]
    for t in range(CH):
        h = a_scr[t:t + 1, :] * h + b_scr[t:t + 1, :]
        h_ref[t:t + 1, :] = h
    carry_scr[...] = h
    xprev_scr[...] = x
    y_ref[:, :D_LRU] = _lru_out(h_ref[...], g_ref[...], gain_ref[...]).astype(y_ref.dtype)


LRU_VEC_ROWS = 16


def _lru_bwd_block(ib, x_ref, xp_ref, g_ref, h_ref, hp_ref, dy_ref, cw_ref, cb_ref, wa_ref, ba_ref, wx_ref, bx_ref, lam_ref,
                   gain_ref, dp_ref, vec_ref, dwa_ref, dwx_ref, a_scr, dh_scr, g_scr, carry_scr, dxcn_scr):
    @pl.when(ib == NCH - 1)
    def _():
        carry_scr[...] = jnp.zeros_like(carry_scr)
        dxcn_scr[...] = jnp.zeros_like(dxcn_scr)
        vec_ref[...] = jnp.zeros_like(vec_ref)
        dwa_ref[...] = jnp.zeros_like(dwa_ref)
        dwx_ref[...] = jnp.zeros_like(dwx_ref)

    x = x_ref[...]
    row = lax.broadcasted_iota(jnp.int32, (CH, D_LRU), 0)
    valid = ib * CH + row >= PAD
    taps = _conv_taps(x, xp_ref[...], row)
    xc = _conv(taps, cw_ref, cb_ref[...])
    pa = _dot(xc, wa_ref[...], NN) + ba_ref[...]
    px = _dot(xc, wx_ref[...], NN) + bx_ref[...]
    a, _, gate_parts = _lru_gates(pa, px, xc, lam_ref[...])
    h = h_ref[...]
    _, vjp_out = jax.vjp(_lru_out, h, g_ref[...], gain_ref[...])
    dh, dg, dgain = vjp_out(dy_ref[:, :D_LRU].astype(F32))
    a_scr[...] = a
    dh_scr[...] = dh
    c = carry_scr[...]
    for t in range(CH - 1, -1, -1):
        gt = dh_scr[t:t + 1, :] + c
        g_scr[t:t + 1, :] = gt
        c = a_scr[t:t + 1, :] * gt
    carry_scr[...] = c
    gg = g_scr[...]
    hprev = jnp.where(row < 1, pltpu.roll(hp_ref[...], 1, 0), pltpu.roll(h, 1, 0))
    da = jnp.where(valid, gg * hprev, 0.0)
    db = jnp.where(valid, gg, 0.0)
    dpa, dpx, dxc, dlam = _lru_gates_vjp(da, db, xc, lam_ref[...], a, *gate_parts)
    dxc = dxc + _dot(dpa, wa_ref[...], NT) + _dot(dpx, wx_ref[...], NT)
    dwa_ref[...] += _dot(xc, dpa, TN)
    dwx_ref[...] += _dot(xc, dpx, TN)
    for s in range(CONV_W):
        vec_ref[CONV_W - 1 - s:CONV_W - s, :] += jnp.sum(dxc * taps[s], axis=0, keepdims=True)
    vec_ref[4:5, :] += jnp.sum(dxc, axis=0, keepdims=True)
    vec_ref[5:6, :] += jnp.sum(dpa, axis=0, keepdims=True)
    vec_ref[6:7, :] += jnp.sum(dpx, axis=0, keepdims=True)
    vec_ref[7:8, :] += dlam
    vec_ref[8:9, :] += dgain
    dxn = dxcn_scr[...]
    dx = cw_ref[CONV_W - 1:CONV_W, :] * dxc
    for s in range(1, CONV_W):
        ahead = jnp.where(row >= CH - s, pltpu.roll(dxn, CH - s, 0), pltpu.roll(dxc, CH - s, 0))
        dx = dx + cw_ref[CONV_W - 1 - s:CONV_W - s, :] * ahead
    dxcn_scr[...] = dxc
    dp_ref[:, :D_LRU] = jnp.where(valid, dx, 0.0).astype(dp_ref.dtype)
    dp_ref[:, D_LRU:2 * D_LRU] = dg.astype(dp_ref.dtype)


def _ret_tables():
    half = HD // 2
    pos = jnp.arange(T, dtype=F32) - float(PAD)
    inv = ROPE_BASE ** (-jnp.arange(half, dtype=F32) / half)
    ang = pos[:, None] * inv[None, :]
    cos = jnp.concatenate([jnp.cos(ang), jnp.cos(ang)], axis=-1)
    sin = jnp.concatenate([-jnp.sin(ang), jnp.sin(ang)], axis=-1)
    log_g = jnp.log(1.0 - 2.0 ** (-5.0 - jnp.arange(HEADS, dtype=F32)))
    idx = jnp.arange(CH, dtype=F32)
    diff = idx[:, None] - idx[None, :]
    dmask = jnp.where(diff[None] >= 0, jnp.exp(jnp.maximum(diff, 0.0)[None] * log_g[:, None, None]), 0.0)
    xi = jnp.exp((idx + 1.0)[None, :] * log_g[:, None])
    zeta = jnp.exp((CH - 1.0 - idx)[None, :] * log_g[:, None])
    xi = jnp.broadcast_to(xi[:, :, None], (HEADS, CH, HD))
    zeta = jnp.broadcast_to(zeta[:, :, None], (HEADS, CH, HD))
    return cos, sin, dmask, xi, zeta


def _chunk_decay():
    log_g = np.log(np.float32(1.0) - np.float32(2.0) ** (np.float32(-5.0) - np.arange(HEADS, dtype=np.float32)))
    return [float(v) for v in np.exp(np.float32(CH) * log_g.astype(np.float32))]


def _rope(x, cos, sin):
    return x * cos + pltpu.roll(x, HD // 2, 1) * sin


def mix_fwd(proj, cw, cb, wa, ba, wx, bx, lam, gain, tables, ret_gain, after, name):
    cos, sin, dmask, xi, zeta = tables
    gch = _chunk_decay()
    scale = HD ** -0.5

    def body(x_ref, gl_ref, cw_ref, cb_ref, wa_ref, ba_ref, wx_ref, bx_ref, lam_ref, lgain_ref,
             q_ref, k_ref, v_ref, g_ref, cos_ref, sin_ref, dm_ref, xi_ref, zt_ref, gain_ref, after_ref,
             y_ref, h_ref, st_ref, xprev_scr, a_scr, b_scr, carry_scr, s_scr):
        del after_ref

        @pl.when(pl.program_id(0) == 0)
        def _():
            s_scr[...] = jnp.zeros_like(s_scr)

        _lru_fwd_block(pl.program_id(0), x_ref, gl_ref, cw_ref, cb_ref, wa_ref, ba_ref, wx_ref, bx_ref, lam_ref, lgain_ref,
                       y_ref, h_ref, xprev_scr, a_scr, b_scr, carry_scr)
        cs, sn = cos_ref[...], sin_ref[...]
        hs = range(HEADS)
        sl = [slice(HD * h, HD * (h + 1)) for h in hs]
        qr = [_rope(q_ref[:, sl[h]], cs, sn).astype(MXU_DTYPE) for h in hs]
        kf = [_rope(k_ref[:, sl[h]], cs, sn) * scale for h in hs]
        kr = [kf[h].astype(MXU_DTYPE) for h in hs]
        v = [v_ref[:, sl[h]].astype(MXU_DTYPE) for h in hs]
        s = [s_scr[h] for h in hs]
        for h in hs:
            st_ref[h] = s[h]
        sc = [_dot(qr[h], kr[h], NT) * dm_ref[h] for h in hs]
        cross = [_dot(qr[h], s[h], NN) * xi_ref[h] for h in hs]
        for h in hs:
            s_scr[h] = s[h] * gch[h] + _dot(kf[h] * zt_ref[h], v[h], TN)
        y = [_dot(sc[h], v[h], NN) + cross[h] for h in hs]
        yc = [y[h] - jnp.mean(y[h], axis=-1, keepdims=True) for h in hs]
        yn = [yc[h] * lax.rsqrt(jnp.mean(yc[h] * yc[h], axis=-1, keepdims=True) + EPS) for h in hs]
        for h in hs:
            so = slice(D_LRU + HD * h, D_LRU + HD * (h + 1))
            y_ref[:, so] = (jax.nn.silu(g_ref[:, sl[h]]) * (yn[h] * gain_ref[:, sl[h]])).astype(y_ref.dtype)

    def col(c):
        return pl.BlockSpec((CH, D_RET), lambda n: (n, c))

    tab = pl.BlockSpec((CH, HD), lambda n: (n, 0))
    cst = _full((HEADS, CH, HD))
    vec = _full((1, D_LRU))
    mat = _full((D_LRU, D_LRU))
    blockbuf = pltpu.VMEM((CH, D_LRU), F32)
    return pl.pallas_call(
        body, name=name, grid=(NCH,),
        in_specs=[col(0), col(1), _full((CONV_W, D_LRU)), vec, mat, vec, mat, vec, vec, vec,
                  col(2), col(3), col(4), col(5), tab, tab, cst, cst, cst, _full((1, D_RET)),
                  pl.BlockSpec(memory_space=pl.ANY)],
        out_specs=[pl.BlockSpec((CH, D), lambda n: (n, 0)), col(0), pl.BlockSpec((None, HEADS, HD, HD), lambda n: (n, 0, 0, 0))],
        out_shape=[_sds((T, D), MXU_DTYPE), _sds((T, D_LRU), F32), _sds((NCH, HEADS, HD, HD), F32)],
        scratch_shapes=[blockbuf, blockbuf, blockbuf, pltpu.VMEM((1, D_LRU), F32), pltpu.VMEM((HEADS, HD, HD), F32)],
        compiler_params=_params(("arbitrary",)),
    )(proj, proj, cw, cb, wa, ba, wx, bx, lam, gain, proj, proj, proj, proj, cos, sin, dmask, xi, zeta, ret_gain, after)


def mix_bwd(proj, hst, states, dymix, cw, cb, wa, ba, wx, bx, lam, gain, tables, ret_gain, after, name):
    cos, sin, dmask, xi, zeta = tables
    gch = _chunk_decay()
    scale = HD ** -0.5
    last = NCH - 1

    def body(x_ref, xp_ref, gl_ref, h_ref, hp_ref, cw_ref, cb_ref, wa_ref, ba_ref, wx_ref, bx_ref, lam_ref, lgain_ref,
             q_ref, k_ref, v_ref, g_ref, st_ref, dy_ref, cos_ref, sin_ref, dm_ref, xi_ref, zt_ref, gain_ref, after_ref,
             dp_ref, vec_ref, dwa_ref, dwx_ref, dgain_ref, a_scr, dh_scr, g_scr, carry_scr, dxcn_scr, ds_scr):
        del after_ref

        @pl.when(pl.program_id(0) == 0)
        def _():
            ds_scr[...] = jnp.zeros_like(ds_scr)
            dgain_ref[...] = jnp.zeros_like(dgain_ref)

        _lru_bwd_block(last - pl.program_id(0), x_ref, xp_ref, gl_ref, h_ref, hp_ref, dy_ref, cw_ref, cb_ref, wa_ref, ba_ref,
                       wx_ref, bx_ref, lam_ref, lgain_ref, dp_ref, vec_ref, dwa_ref, dwx_ref, a_scr, dh_scr, g_scr, carry_scr,
                       dxcn_scr)
        cs, sn = cos_ref[...], sin_ref[...]
        hs = range(HEADS)
        sl = [slice(HD * h, HD * (h + 1)) for h in hs]

        def out(j, h):
            return slice(2 * D_LRU + j * D_RET + HD * h, 2 * D_LRU + j * D_RET + HD * (h + 1))

        b16 = lambda xs: [x.astype(MXU_DTYPE) for x in xs]
        qr = b16([_rope(q_ref[:, sl[h]], cs, sn) for h in hs])
        kf = [_rope(k_ref[:, sl[h]], cs, sn) * scale for h in hs]
        kr = b16(kf)
        kz = b16([kf[h] * zt_ref[h] for h in hs])
        v = b16([v_ref[:, sl[h]] for h in hs])
        s = b16([st_ref[h] for h in hs])
        ds = [ds_scr[h] for h in hs]
        dsb = b16(ds)
        sc = [_dot(qr[h], kr[h], NT) * dm_ref[h] for h in hs]
        scb = b16(sc)
        y = [_dot(scb[h], v[h], NN) + _dot(qr[h], s[h], NN) * xi_ref[h] for h in hs]
        yc = [y[h] - jnp.mean(y[h], axis=-1, keepdims=True) for h in hs]
        rstd = [lax.rsqrt(jnp.mean(yc[h] * yc[h], axis=-1, keepdims=True) + EPS) for h in hs]
        yn = [yc[h] * rstd[h] for h in hs]
        dy = []
        for h in hs:
            g = g_ref[:, sl[h]]
            gain = gain_ref[:, sl[h]]
            sg = jax.nn.sigmoid(g)
            silu = g * sg
            dout = dy_ref[:, D_LRU + HD * h:D_LRU + HD * (h + 1)].astype(F32)
            dgain_ref[:, sl[h]] += jnp.sum(dout * silu * yn[h], axis=0, keepdims=True)
            dp_ref[:, out(3, h)] = (dout * yn[h] * gain * (sg * (1.0 + g * (1.0 - sg)))).astype(dp_ref.dtype)
            dyn = dout * silu * gain
            dy.append(rstd[h] * (dyn - jnp.mean(dyn, axis=-1, keepdims=True)
                                 - yn[h] * jnp.mean(dyn * yn[h], axis=-1, keepdims=True)))
        dyb = b16(dy)
        dqs = b16([dy[h] * xi_ref[h] for h in hs])
        dp = b16([_dot(dyb[h], v[h], NT) * dm_ref[h] for h in hs])
        dv = [_dot(scb[h], dyb[h], TN) + _dot(kz[h], dsb[h], NN) for h in hs]
        dqr = [_dot(dp[h], kr[h], NN) + _dot(dqs[h], s[h], NT) for h in hs]
        dkr = [_dot(dp[h], qr[h], TN) + _dot(v[h], dsb[h], NT) * zt_ref[h] for h in hs]
        for h in hs:
            ds_scr[h] = gch[h] * ds[h] + _dot(qr[h], dqs[h], TN)
        for h in hs:
            dp_ref[:, out(0, h)] = (dqr[h] * cs + pltpu.roll(dqr[h] * sn, HD // 2, 1)).astype(dp_ref.dtype)
            dp_ref[:, out(1, h)] = ((dkr[h] * cs + pltpu.roll(dkr[h] * sn, HD // 2, 1)) * scale).astype(dp_ref.dtype)
            dp_ref[:, out(2, h)] = dv[h].astype(dp_ref.dtype)

    def col(c, shift=0):
        return pl.BlockSpec((CH, D_RET), lambda n: (jnp.maximum(last - n - shift, 0), c))

    tab = pl.BlockSpec((CH, HD), lambda n: (last - n, 0))
    cst = _full((HEADS, CH, HD))
    vec = _full((1, D_LRU))
    mat = _full((D_LRU, D_LRU))
    blockbuf = pltpu.VMEM((CH, D_LRU), F32)
    return pl.pallas_call(
        body, name=name, grid=(NCH,),
        in_specs=[col(0), col(0, 1), col(1), col(0), col(0, 1), _full((CONV_W, D_LRU)), vec, mat, vec, mat, vec, vec, vec,
                  col(2), col(3), col(4), col(5), pl.BlockSpec((None, HEADS, HD, HD), lambda n: (last - n, 0, 0, 0)),
                  pl.BlockSpec((CH, D), lambda n: (last - n, 0)), tab, tab, cst, cst, cst, _full((1, D_RET)),
                  pl.BlockSpec(memory_space=pl.ANY)],
        out_specs=[pl.BlockSpec((CH, D_IN), lambda n: (last - n, 0)), _full((LRU_VEC_ROWS, D_LRU)), mat, mat,
                   _full((1, D_RET))],
        out_shape=[_sds((T, D_IN), MXU_DTYPE), _sds((LRU_VEC_ROWS, D_LRU), F32), _sds((D_LRU, D_LRU), F32),
                   _sds((D_LRU, D_LRU), F32), _sds((1, D_RET), F32)],
        scratch_shapes=[blockbuf, blockbuf, blockbuf, pltpu.VMEM((1, D_LRU), F32), blockbuf,
                        pltpu.VMEM((HEADS, HD, HD), F32)],
        compiler_params=_params(("arbitrary",)),
    )(proj, proj, proj, hst, hst, cw, cb, wa, ba, wx, bx, lam, gain, proj, proj, proj, proj, states, dymix,
      cos, sin, dmask, xi, zeta, ret_gain, after)


HBM = pl.BlockSpec(memory_space=pltpu.HBM)


def _place():
    return lax.axis_index("x"), lax.axis_index("y"), lax.axis_index("c")


def all_gather(arrs, after, name):
    n = len(arrs)

    def body(*refs):
        ins, outs = refs[:n], refs[n + 1:2 * n + 1]
        send_sems, recv_sems, local_sems = refs[2 * n + 1:]
        x, y, c = _place()
        me, sibling = (x, y, c), (x, y, 1 - c)
        chips = [(1 - x, y), (x, 1 - y), (1 - x, 1 - y)]

        def copy(a, k, block, to, src=None):
            px, py, pc = block
            dst = outs[a].at[4 * px + 2 * py + pc]
            return pltpu.make_async_remote_copy(
                src_ref=dst if src is None else src, dst_ref=dst, send_sem=send_sems.at[a, k], recv_sem=recv_sems.at[a, k],
                device_id=to, device_id_type=MESH)

        mine = [pltpu.make_async_copy(ins[a], outs[a].at[4 * x + 2 * y + c], local_sems.at[a]) for a in range(n)]
        for cp in mine:
            cp.start()
        first = []
        for a in range(n):
            first.append(copy(a, 0, me, sibling, src=ins[a]))
            first += [copy(a, 1 + j, me, (*chip, c), src=ins[a]) for j, chip in enumerate(chips)]
        for cp in first:
            cp.start()
        passed = []
        for j, chip in enumerate(chips):
            for a in range(n):
                copy(a, 1 + j, (*chip, c), me).wait_recv()
                passed.append(copy(a, 4 + j, (*chip, c), sibling))
                passed[-1].start()
        for a in range(n):
            copy(a, 0, sibling, me).wait_recv()
            for j, chip in enumerate(chips):
                copy(a, 4 + j, (*chip, 1 - c), me).wait_recv()
        for cp in first + passed:
            cp.wait_send()
        for cp in mine:
            cp.wait()

    return pl.pallas_call(
        body, name=name,
        in_specs=[HBM] * n + [pl.BlockSpec(memory_space=pl.ANY)], out_specs=[HBM] * n,
        out_shape=[_sds((NDEV,) + a.shape, a.dtype) for a in arrs],
        scratch_shapes=[pltpu.SemaphoreType.DMA((n, 7)), pltpu.SemaphoreType.DMA((n, 7)), pltpu.SemaphoreType.DMA((n,))],
    )(*arrs, after)


SEM = pl.BlockSpec(memory_space=pltpu.SEMAPHORE)
ANY = pl.BlockSpec(memory_space=pl.ANY)
EFFECT = pltpu.SideEffectType.DATAFLOW_SIDE_EFFECTING


def _hbm(a):
    return pltpu.with_memory_space_constraint(a, pltpu.HBM)


def _hbm_like(arrs):
    return [pltpu.HBM(a.shape, a.dtype) for a in arrs]


def _dma_sems(count):
    return [pltpu.SemaphoreType.DMA(())] * count


def _ag_copy(lands, send_sems, recv_sems, per):
    def copy(a, k, block, to, src=None):
        px, py, pc = block
        dst = lands[a].at[4 * px + 2 * py + pc]
        return pltpu.make_async_remote_copy(
            src_ref=dst if src is None else src, dst_ref=dst, send_sem=send_sems[a * per + k], recv_sem=recv_sems[a * per + k],
            device_id=to, device_id_type=MESH)
    return copy


def to_wire(sel, w_in, w_out, w_gate, w_up, w_down, name):
    ffpad = FF_SHP - FF_SH

    def body(sel_ref, i_ref, o_ref, g_ref, u_ref, d_ref, oi, oo, og, ou, od):
        del sel_ref
        oi[...] = i_ref[...].astype(oi.dtype)
        oo[...] = o_ref[...].astype(oo.dtype)
        for src, dst in ((g_ref, og), (u_ref, ou), (d_ref, od)):
            dst[:FF_SH, :] = src[...].astype(dst.dtype)
            dst[FF_SH:, :] = jnp.zeros((ffpad, D), dst.dtype)

    shapes_in = [(D, IN_SH), (OUT_SH, D), (FF_SH, D), (FF_SH, D), (FF_SH, D)]
    shapes_out = [(D, IN_SH), (OUT_SH, D), (FF_SHP, D), (FF_SHP, D), (FF_SHP, D)]
    return pl.pallas_call(
        body, name=name,
        grid_spec=pltpu.PrefetchScalarGridSpec(
            num_scalar_prefetch=1, grid=(1,),
            in_specs=[pl.BlockSpec((None,) + s, lambda i, sel_ref: (sel_ref[1], 0, 0)) for s in shapes_in],
            out_specs=[pl.BlockSpec((None,) + s, lambda i, sel_ref: (sel_ref[0], 0, 0)) for s in shapes_out]),
        out_shape=[_sds((NDEV,) + s, WIRE_DTYPE) for s in shapes_out], compiler_params=_params(("arbitrary",)),
    )(sel, w_in, w_out, w_gate, w_up, w_down)


def place_blocks(sel, arrs, name):
    n = len(arrs)

    def body(sel_ref, *refs):
        del sel_ref
        for a in range(n):
            refs[n + a][...] = refs[a][...]

    def whole(a):
        nd = a.ndim
        return pl.BlockSpec(a.shape, lambda i, sel_ref: (0,) * nd)

    def mine(a):
        nd = a.ndim
        return pl.BlockSpec((None,) + a.shape, lambda i, sel_ref: (sel_ref[0],) + (0,) * nd)

    return pl.pallas_call(
        body, name=name,
        grid_spec=pltpu.PrefetchScalarGridSpec(
            num_scalar_prefetch=1, grid=(1,), in_specs=[whole(a) for a in arrs], out_specs=[mine(a) for a in arrs]),
        out_shape=[_sds((NDEV,) + a.shape, a.dtype) for a in arrs], compiler_params=_params(("arbitrary",)),
    )(sel, *arrs)


def ag_start(lands, after, name):
    n = len(lands)
    ns = 4 * n

    def body(*refs):
        lnd = refs[:n]
        send_sems, recv_sems = refs[n + 1:n + 1 + ns], refs[n + 1 + ns:n + 1 + 2 * ns]
        token = refs[-1]
        x, y, c = _place()
        me, sibling = (x, y, c), (x, y, 1 - c)
        chips = [(1 - x, y), (x, 1 - y), (1 - x, 1 - y)]
        copy = _ag_copy(lnd, send_sems, recv_sems, 4)
        for a in range(n):
            copy(a, 0, me, sibling).start()
            for j, chip in enumerate(chips):
                copy(a, 1 + j, me, (*chip, c)).start()
        token[...] = jnp.zeros_like(token)

    outs = pl.pallas_call(
        body, name=name,
        in_specs=[HBM] * n + [ANY],
        out_specs=[SEM] * (2 * ns) + [HBM] * n + [pl.BlockSpec(memory_space=pltpu.VMEM)],
        out_shape=_dma_sems(2 * ns) + _hbm_like(lands) + [_sds((8, 128), F32)],
        input_output_aliases={i: 2 * ns + i for i in range(n)},
        compiler_params=pltpu.CompilerParams(has_side_effects=EFFECT),
    )(*[_hbm(a) for a in lands], after)
    return outs[:ns], outs[ns:2 * ns], outs[2 * ns:2 * ns + n], outs[-1]


def ag_forward(send_sems, recv_sems, lands, after, name):
    n = len(lands)
    n1, n2 = 4 * n, 3 * n

    def body(*refs):
        lnd = refs[:n]
        o = n
        s1, r1 = refs[o:o + n1], refs[o + n1:o + 2 * n1]
        o += 2 * n1 + 1
        s2, r2 = refs[o:o + n2], refs[o + n2:o + 2 * n2]
        token = refs[-1]
        token[...] = jnp.zeros_like(token)
        x, y, c = _place()
        me, sibling = (x, y, c), (x, y, 1 - c)
        chips = [(1 - x, y), (x, 1 - y), (1 - x, 1 - y)]
        copy1 = _ag_copy(lnd, s1, r1, 4)
        copy2 = _ag_copy(lnd, s2, r2, 3)
        for j, chip in enumerate(chips):
            for a in range(n):
                copy1(a, 1 + j, (*chip, c), me).wait_recv()
                copy2(a, j, (*chip, c), sibling).start()
        for a in range(n):
            copy1(a, 0, sibling, me).wait_recv()
            copy1(a, 0, me, sibling).wait_send()
            for j, chip in enumerate(chips):
                copy1(a, 1 + j, me, (*chip, c)).wait_send()

    outs = pl.pallas_call(
        body, name=name,
        in_specs=[HBM] * n + [SEM] * (2 * n1) + [ANY],
        out_specs=[SEM] * (2 * n2) + [HBM] * n + [pl.BlockSpec(memory_space=pltpu.VMEM)],
        out_shape=_dma_sems(2 * n2) + _hbm_like(lands) + [_sds((8, 128), F32)],
        input_output_aliases={i: 2 * n2 + i for i in range(n)},
        compiler_params=pltpu.CompilerParams(has_side_effects=EFFECT),
    )(*lands, *send_sems, *recv_sems, after)
    return outs[:n2], outs[n2:2 * n2], outs[2 * n2:2 * n2 + n], outs[-1]


def ag_finish(send_sems, recv_sems, lands, after, name):
    n = len(lands)
    n2 = 3 * n

    def body(*refs):
        lnd = refs[:n]
        s2, r2 = refs[n:n + n2], refs[n + n2:n + 2 * n2]
        x, y, c = _place()
        me, sibling = (x, y, c), (x, y, 1 - c)
        chips = [(1 - x, y), (x, 1 - y), (1 - x, 1 - y)]
        copy2 = _ag_copy(lnd, s2, r2, 3)
        for a in range(n):
            for j, chip in enumerate(chips):
                copy2(a, j, (*chip, c), sibling).wait_send()
                copy2(a, j, (*chip, 1 - c), me).wait_recv()

    outs = pl.pallas_call(
        body, name=name,
        in_specs=[HBM] * n + [SEM] * (2 * n2) + [ANY],
        out_specs=[HBM] * n, out_shape=_hbm_like(lands),
        input_output_aliases={i: i for i in range(n)},
        compiler_params=pltpu.CompilerParams(has_side_effects=EFFECT),
    )(*lands, *send_sems, *recv_sems, after)
    return list(outs)


def rs_sibling_start(arrs, name):
    n = len(arrs)
    ns = 4 * n
    lands = [lax.empty((4,) + a.shape[1:], a.dtype) for a in arrs]

    def body(*refs):
        ins, lnd = refs[:n], refs[n:2 * n]
        send_sems, recv_sems = refs[2 * n:2 * n + ns], refs[2 * n + ns:2 * n + 2 * ns]
        x, y, c = _place()
        sibling = (x, y, 1 - c)
        for a in range(n):
            for p in range(4):
                pltpu.make_async_remote_copy(
                    src_ref=ins[a].at[2 * p + 1 - c], dst_ref=lnd[a].at[p], send_sem=send_sems[4 * a + p],
                    recv_sem=recv_sems[4 * a + p], device_id=sibling, device_id_type=MESH).start()
        refs[-1][...] = jnp.zeros_like(refs[-1])

    outs = pl.pallas_call(
        body, name=name,
        in_specs=[HBM] * (2 * n), out_specs=[SEM] * (2 * ns) + [HBM] * (2 * n) + [pl.BlockSpec(memory_space=pltpu.VMEM)],
        out_shape=_dma_sems(2 * ns) + _hbm_like(arrs) + _hbm_like(lands) + [_sds((8, 128), F32)],
        input_output_aliases={i: 2 * ns + i for i in range(2 * n)},
        compiler_params=pltpu.CompilerParams(has_side_effects=EFFECT),
    )(*[_hbm(a) for a in arrs], *[_hbm(a) for a in lands])
    return (outs[:ns], outs[ns:2 * ns], outs[2 * ns:2 * ns + n], outs[2 * ns + n:2 * ns + 2 * n]), outs[-1]


def rs_sibling_wait(send_sems, recv_sems, arrs, lands, after, name):
    n = len(arrs)
    ns = 4 * n

    def body(*refs):
        ins, lnd = refs[:n], refs[n:2 * n]
        s, r = refs[2 * n:2 * n + ns], refs[2 * n + ns:2 * n + 2 * ns]
        x, y, c = _place()
        sibling = (x, y, 1 - c)
        for a in range(n):
            for p in range(4):
                cp = pltpu.make_async_remote_copy(
                    src_ref=ins[a].at[2 * p + 1 - c], dst_ref=lnd[a].at[p], send_sem=s[4 * a + p], recv_sem=r[4 * a + p],
                    device_id=sibling, device_id_type=MESH)
                cp.wait_send()
                cp.wait_recv()

    outs = pl.pallas_call(
        body, name=name,
        in_specs=[HBM] * (2 * n) + [SEM] * (2 * ns) + [ANY], out_specs=[HBM] * (2 * n),
        out_shape=_hbm_like(arrs) + _hbm_like(lands),
        input_output_aliases={i: i for i in range(2 * n)},
        compiler_params=pltpu.CompilerParams(has_side_effects=EFFECT),
    )(*arrs, *lands, *send_sems, *recv_sems, after)
    return outs[:n], outs[n:]


def rs_chips_start(parts, name):
    n = len(parts)
    ns = 3 * n
    lands = [lax.empty((3,) + a.shape[1:], a.dtype) for a in parts]

    def body(*refs):
        ins, lnd = refs[:n], refs[n:2 * n]
        send_sems, recv_sems = refs[2 * n:2 * n + ns], refs[2 * n + ns:2 * n + 2 * ns]
        x, y, c = _place()
        chips = [(1 - x, y), (x, 1 - y), (1 - x, 1 - y)]
        for a in range(n):
            for k, (tx, ty) in enumerate(chips):
                pltpu.make_async_remote_copy(
                    src_ref=ins[a].at[2 * tx + ty], dst_ref=lnd[a].at[k], send_sem=send_sems[3 * a + k],
                    recv_sem=recv_sems[3 * a + k], device_id=(tx, ty, c), device_id_type=MESH).start()
        refs[-1][...] = jnp.zeros_like(refs[-1])

    outs = pl.pallas_call(
        body, name=name,
        in_specs=[HBM] * (2 * n), out_specs=[SEM] * (2 * ns) + [HBM] * (2 * n) + [pl.BlockSpec(memory_space=pltpu.VMEM)],
        out_shape=_dma_sems(2 * ns) + _hbm_like(parts) + _hbm_like(lands) + [_sds((8, 128), F32)],
        input_output_aliases={i: 2 * ns + i for i in range(2 * n)},
        compiler_params=pltpu.CompilerParams(has_side_effects=EFFECT),
    )(*[_hbm(a) for a in parts], *[_hbm(a) for a in lands])
    return (outs[:ns], outs[ns:2 * ns], outs[2 * ns:2 * ns + n], outs[2 * ns + n:2 * ns + 2 * n]), outs[-1]


def rs_chips_wait(send_sems, recv_sems, parts, lands, after, name):
    n = len(parts)
    ns = 3 * n

    def body(*refs):
        ins, lnd = refs[:n], refs[n:2 * n]
        s, r = refs[2 * n:2 * n + ns], refs[2 * n + ns:2 * n + 2 * ns]
        x, y, c = _place()
        chips = [(1 - x, y), (x, 1 - y), (1 - x, 1 - y)]
        for a in range(n):
            for k, (tx, ty) in enumerate(chips):
                cp = pltpu.make_async_remote_copy(
                    src_ref=ins[a].at[2 * tx + ty], dst_ref=lnd[a].at[k], send_sem=s[3 * a + k], recv_sem=r[3 * a + k],
                    device_id=(tx, ty, c), device_id_type=MESH)
                cp.wait_send()
                cp.wait_recv()

    outs = pl.pallas_call(
        body, name=name,
        in_specs=[HBM] * (2 * n) + [SEM] * (2 * ns) + [ANY], out_specs=[HBM] * (2 * n),
        out_shape=_hbm_like(parts) + _hbm_like(lands),
        input_output_aliases={i: i for i in range(2 * n)},
        compiler_params=pltpu.CompilerParams(has_side_effects=EFFECT),
    )(*parts, *lands, *send_sems, *recv_sems, after)
    return outs[:n], outs[n:]


def pair_sum(arrs, recv, c, name):
    n = len(arrs)

    def body(c_ref, *refs):
        del c_ref
        for a in range(n):
            refs[2 * n + a][...] = (refs[a][...].astype(F32) + refs[n + a][...].astype(F32)).astype(refs[2 * n + a].dtype)

    mine = [pl.BlockSpec((None,) + a.shape[1:], lambda p, c_ref: (2 * p + c_ref[0], 0, 0)) for a in arrs]
    other = [pl.BlockSpec((None,) + a.shape[1:], lambda p, c_ref: (p, 0, 0)) for a in arrs]
    return pl.pallas_call(
        body, name=name,
        grid_spec=pltpu.PrefetchScalarGridSpec(num_scalar_prefetch=1, grid=(4,), in_specs=mine + other, out_specs=other),
        out_shape=[_sds((4,) + a.shape[1:], a.dtype) for a in arrs], compiler_params=_params(("parallel",)),
    )(c, *arrs, *recv)


def _adamw(w, g, m, v):
    m = ADAM_B1 * m + (1.0 - ADAM_B1) * g
    v = ADAM_B2 * v + (1.0 - ADAM_B2) * jnp.square(g)
    m_hat = m / (1.0 - ADAM_B1 ** ADAM_STEP)
    v_hat = v / (1.0 - ADAM_B2 ** ADAM_STEP)
    return -ADAM_LR * (m_hat / (jnp.sqrt(v_hat) + ADAM_EPS) + ADAM_WD * w), m, v


def adamw_big(recv, sums, chip, w, m, v, tr, name):
    nl, rr, cc = w.shape
    cp = recv[0].shape[2]

    def body(chip_ref, *refs):
        del chip_ref
        rcv, own = refs[:nl], refs[nl:2 * nl]
        w_ref, m_ref, v_ref, g_out, d_out, m_out, v_out = refs[2 * nl:]
        for l in range(nl):
            g = ((own[l][...].astype(F32) + rcv[l][0].astype(F32)) + rcv[l][1].astype(F32)) + rcv[l][2].astype(F32)
            g = g[:, :cc]
            g_out[l] = g
            d_out[l], m_out[l], v_out[l] = _adamw(w_ref[l], g, m_ref[l], v_ref[l])

    blk = pl.BlockSpec((nl, tr, cc), lambda i, chip_ref: (0, i, 0))
    return pl.pallas_call(
        body, name=name,
        grid_spec=pltpu.PrefetchScalarGridSpec(
            num_scalar_prefetch=1, grid=(rr // tr,),
            in_specs=[pl.BlockSpec((3, tr, cp), lambda i, chip_ref: (0, i, 0))] * nl
            + [pl.BlockSpec((None, tr, cp), lambda i, chip_ref: (chip_ref[0], i, 0))] * nl + [blk, blk, blk],
            out_specs=[blk] * 4),
        out_shape=[_sds(w.shape, F32)] * 4, compiler_params=_params(("parallel",)),
    )(chip, *recv, *sums, w, m, v)


SMALL_ROWS = 24


def small_grads(lvec, g_ret, g_mix, g_ffn, g_final, loss_part, dwa, dwx, name):
    def body(lvec_ref, ret_ref, mix_ref, ffn_ref, fin_ref, loss_ref, dwa_ref, dwx_ref, v_ref, g_ref):
        v_ref[16:SMALL_ROWS, :] = jnp.zeros((SMALL_ROWS - 16, D_LRU), F32)
        v_ref[16:17, 0:128] = loss_ref[0:1, :]
        v_ref[0:9, :] = lvec_ref[0:9, :]
        v_ref[9:10, :] = ret_ref[...]
        for r, src in ((10, mix_ref), (12, ffn_ref), (14, fin_ref)):
            v_ref[r:r + 1, :] = src[:, :D_LRU]
            v_ref[r + 1:r + 2, :] = src[:, D_LRU:]
        for k, src in enumerate((dwa_ref, dwx_ref)):
            for g in range(LRU_BLOCKS):
                rows = slice(LRU_BD * g, LRU_BD * (g + 1))
                g_ref[D_LRU * k + LRU_BD * g:D_LRU * k + LRU_BD * (g + 1), :] = src[rows, rows]

    ins = [lvec, g_ret, g_mix, g_ffn, g_final, loss_part, dwa, dwx]
    return pl.pallas_call(
        body, name=name, grid=(1,), in_specs=[_full(a.shape) for a in ins],
        out_specs=[_full((SMALL_ROWS, D_LRU)), _full((2 * D_LRU, LRU_BD))],
        out_shape=[_sds((SMALL_ROWS, D_LRU), F32), _sds((2 * D_LRU, LRU_BD), F32)], compiler_params=_params(("arbitrary",)),
    )(*ins)


def sum_devices(arrs, name):
    n = len(arrs)

    def body(*refs):
        for a in range(n):
            acc = refs[a][0]
            for j in range(1, NDEV):
                acc = acc + refs[a][j]
            refs[n + a][...] = acc

    return pl.pallas_call(
        body, name=name, grid=(1,), in_specs=[_full(a.shape) for a in arrs], out_specs=[_full(a.shape[1:]) for a in arrs],
        out_shape=[_sds(a.shape[1:], F32) for a in arrs], compiler_params=_params(("arbitrary",)),
    )(*arrs)


def adamw_small(gs, ws, ms, vs, name):
    n = len(gs)

    def body(*refs):
        for a in range(n):
            g, w, m, v = (refs[k * n + a][...] for k in range(4))
            refs[4 * n + a][...], refs[5 * n + a][...], refs[6 * n + a][...] = _adamw(w, g, m, v)

    specs = [_full(a.shape) for a in ws]
    outs = pl.pallas_call(
        body, name=name, grid=(1,), in_specs=specs * 4, out_specs=specs * 3, out_shape=[_sds(a.shape, F32) for a in ws] * 3,
        compiler_params=_params(("arbitrary",)),
    )(*gs, *ws, *ms, *vs)
    return outs[:n], outs[n:2 * n], outs[2 * n:]


def block_diag(wa, wx, name):
    def body(wa_ref, wx_ref, oa_ref, ox_ref):
        for src, dst in ((wa_ref, oa_ref), (wx_ref, ox_ref)):
            dst[...] = jnp.zeros_like(dst)
            for g in range(LRU_BLOCKS):
                rows = slice(LRU_BD * g, LRU_BD * (g + 1))
                dst[rows, rows] = src[g].astype(dst.dtype)

    ispec = pl.BlockSpec((None, LRU_BLOCKS, LRU_BD, LRU_BD), lambda l: (l, 0, 0, 0))
    ospec = pl.BlockSpec((None, D_LRU, D_LRU), lambda l: (l, 0, 0))
    return pl.pallas_call(
        body, name=name, grid=(wa.shape[0],), in_specs=[ispec, ispec], out_specs=[ospec, ospec],
        out_shape=[_sds((wa.shape[0], D_LRU, D_LRU), MXU_DTYPE)] * 2, compiler_params=_params(("parallel",)),
    )(wa, wx)


REP_NAMES = ["norm_mix", "conv_b", "gate_a_w", "gate_a_b", "gate_x_w", "gate_x_b", "lru_lambda", "lru_out_norm",
             "ret_out_norm", "norm_ffn", "norm_final"]


def kernel(x, meta_tokens, norm_mix, w_in, conv_w, conv_b, gate_a_w, gate_a_b, gate_x_w, gate_x_b, lru_lambda, lru_out_norm, ret_out_norm, w_out, norm_ffn, w_gate, w_up, w_down, norm_final, loss_target, m_meta_tokens, m_norm_mix, m_w_in, m_conv_w, m_conv_b, m_gate_a_w, m_gate_a_b, m_gate_x_w, m_gate_x_b, m_lru_lambda, m_lru_out_norm, m_ret_out_norm, m_w_out, m_norm_ffn, m_w_gate, m_w_up, m_w_down, m_norm_final, v_meta_tokens, v_norm_mix, v_w_in, v_conv_w, v_conv_b, v_gate_a_w, v_gate_a_b, v_gate_x_w, v_gate_x_b, v_lru_lambda, v_lru_out_norm, v_ret_out_norm, v_w_out, v_norm_ffn, v_w_gate, v_w_up, v_w_down, v_norm_final):
    xi, yi, ci = _place()
    dev = 4 * xi + 2 * yi + ci
    c_arr = jnp.reshape(ci, (1,)).astype(jnp.int32)
    dev_arr = jnp.reshape(dev, (1,)).astype(jnp.int32)

    meta_g, conv_g = all_gather([meta_tokens, conv_w], c_arr, "ag_small")
    meta_full = jnp.transpose(meta_g, (1, 0, 2)).reshape(N_META, D)
    conv_full = jnp.transpose(conv_g, (1, 2, 0, 3)).reshape(DEPTH, CONV_W, D_LRU)
    tr_ = lambda a: jnp.transpose(a, (0, 2, 1))
    w_gate_t, m_w_gate_t, v_w_gate_t = tr_(w_gate), tr_(m_w_gate), tr_(v_w_gate)
    w_up_t, m_w_up_t, v_w_up_t = tr_(w_up), tr_(m_w_up), tr_(v_w_up)
    level1 = []
    token = meta_g
    for l in range(DEPTH):
        sel = jnp.stack([dev, jnp.int32(l)]).astype(jnp.int32)
        lands = to_wire(sel, w_in, w_out, w_gate_t, w_up_t, w_down, "to_wire")
        s1, r1, lands, token = ag_start(lands, token, f"ag_start_{l}")
        level1.append((s1, r1, lands))

    def as_weights(gi, go, gg, gu, gd):
        return dict(w_in=gi, w_out=go.reshape(D, D), w_gate=gg.reshape(D_FFP, D), w_up=gu.reshape(D_FFP, D),
                    w_down=gd.reshape(D_FFP, D))

    tables = _ret_tables()
    row = lambda a: a.reshape(1, -1)

    h = jnp.concatenate([jnp.zeros((PAD, D), F32), meta_full, x[0]], axis=0)
    saved, gathered = [], []
    s1, r1, lands = level1[0]
    s2, r2, first, order = ag_forward(s1[:4], r1[:4], lands[:1], token, "ag_forward_0_w_in")
    w_in_next = ag_finish(s2, r2, first, h, "ag_finish_0_w_in")[0]
    wa_dense, wx_dense = block_diag(gate_a_w, gate_x_w, "block_diag")
    for l in range(DEPTH):
        small = dict(cw=conv_full[l], cb=row(conv_b[l]), wa=wa_dense[l], ba=row(gate_a_b[l]),
                     wx=wx_dense[l], bx=row(gate_x_b[l]), lam=row(lru_lambda[l]),
                     gain=row(lru_out_norm[l]))
        s1, r1, lands = level1[l]
        hn1 = rmsnorm_fwd(h, row(norm_mix[l]), "rms_fwd")
        proj = mm_blocked_nn(hn1, w_in_next, F32, "proj")
        if l > 0:
            s2, r2, rest, order = ag_forward(s1[4:], r1[4:], lands[1:], proj, f"ag_forward_{l}_rest")
            ymix, hst, states = mix_fwd(proj, tables=tables, ret_gain=row(ret_out_norm[l]), after=order, name="mix_fwd", **small)
            w = as_weights(w_in_next, *ag_finish(s2, r2, rest, ymix, f"ag_finish_{l}_rest"))
            h_mid = mm_nn_res(ymix, w["w_out"], h, order, "out_proj")
        else:
            ymix, hst, states = mix_fwd(proj, tables=tables, ret_gain=row(ret_out_norm[l]), after=order, name="mix_fwd", **small)
            s2, r2, mid, order = ag_forward(s1[4:16], r1[4:16], lands[1:4], ymix, "ag_forward_0_mid")
            mids = ag_finish(s2, r2, mid, order, "ag_finish_0_mid")
            w = dict(w_in=w_in_next, w_out=mids[0].reshape(D, D), w_gate=mids[1].reshape(D_FFP, D), w_up=mids[2].reshape(D_FFP, D))
            h_mid = mm_nn_res(ymix, w["w_out"], h, order, "out_proj")
            s2d, r2d, down, order = ag_forward(s1[16:], r1[16:], lands[4:], h_mid, "ag_forward_0_down")
        hn2 = rmsnorm_fwd(h_mid, row(norm_ffn[l]), "rms_fwd")
        act_dgate, act_dup, act = ffn_up(hn2, w["w_gate"], w["w_up"], "ffn_up")
        if l == 0:
            w["w_down"] = ag_finish(s2d, r2d, down, act, "ag_finish_0_down")[0].reshape(D_FFP, D)
        gathered.append(w)
        if l + 1 < DEPTH:
            s1n, r1n, landsn = level1[l + 1]
            s2, r2, first, order = ag_forward(s1n[:4], r1n[:4], landsn[:1], act, f"ag_forward_{l + 1}_w_in")
        h_out = mm_nn_res(act, w["w_down"], h_mid, order, "ffn_down")
        if l + 1 < DEPTH:
            w_in_next = ag_finish(s2, r2, first, h_out, f"ag_finish_{l + 1}_w_in")[0]
        saved.append(dict(h=h, hn1=hn1, proj=proj, hst=hst, states=states, ymix=ymix, h_mid=h_mid, hn2=hn2, act_dgate=act_dgate, act_dup=act_dup,
                          act=act, small=small))
        h = h_out

    loss_p, dh, dh_b, g_norm_final = loss_head(h, row(norm_final), loss_target[0], "loss_head")

    small_v = [None] * DEPTH
    small_w = [None] * DEPTH
    inflight = []
    order = loss_p

    def sibling_done(l, tag, names, sib, after):
        parts, got = rs_sibling_wait(*sib, after, f"rs_sibling_wait_{tag}")
        sums = pair_sum(parts, got, c_arr, "pair_sum")
        flying, started = rs_chips_start(sums, f"rs_chips_start_{tag}")
        inflight.append((l, tag, names, flying))
        return started

    for l in reversed(range(DEPTH)):
        w, s = gathered[l], saved[l]
        dgate, dup = ffn_down_bwd(dh_b, w["w_down"], s["act_dgate"], s["act_dup"], order, "ffn_down_bwd")
        dwd = mm_tn(s["act"], dh_b, PAIR, order, "dw_down").reshape(NDEV, FF_SHP, D)
        dwg, dwu = (g.reshape(NDEV, FF_SHP, D) for g in mm_tn_two(dgate, dup, s["hn2"], PAIR, order, "dw_rows"))
        split = l <= 1
        if split:
            ffn_sib, order = rs_sibling_start([dwg, dwu, dwd], f"rs_sibling_start_{l}_ffn")
        dhn2 = mm_rows_nn([(dgate, w["w_gate"]), (dup, w["w_up"])], order, "ffn_up_bwd")
        if split:
            order = sibling_done(l, f"{l}_ffn", ("w_gate", "w_up", "w_down"), ffn_sib, dhn2)
        dh_mid, dh_mid_b, g_norm_ffn = rmsnorm_bwd(s["h_mid"], row(norm_ffn[l]), dhn2, dh, "rms_bwd")
        dymix, dwo = out_proj_bwd(dh_mid_b, w["w_out"], s["ymix"], order, "out_proj_bwd")
        dwo = dwo.reshape(NDEV, OUT_SH, D)
        dproj, lvec, dwa, dwx, g_ret_norm = mix_bwd(s["proj"], s["hst"], s["states"], dymix, tables=tables,
                                                    ret_gain=row(ret_out_norm[l]), after=order, name="mix_bwd", **s["small"])
        dwi = mm_tn_blocked(s["hn1"], dproj, "dw_blocked")
        if split:
            sib_tag, sib_names = f"{l}_mix", ("w_in", "w_out")
            sib, order = rs_sibling_start([dwi, dwo], f"rs_sibling_start_{l}_mix")
        else:
            sib_tag, sib_names = str(l), ("w_in", "w_gate", "w_up", "w_out", "w_down")
            sib, order = rs_sibling_start([dwi, dwg, dwu, dwo, dwd], f"rs_sibling_start_{l}")
        dhn1 = mm_blocked_nt([(dproj, w["w_in"])], order, "proj_bwd")
        order = sibling_done(l, sib_tag, sib_names, sib, dhn1)
        dh, dh_b, g_norm_mix = rmsnorm_bwd(s["h"], row(norm_mix[l]), dhn1, dh_mid, "rms_bwd")

        g_fin, loss_part = (g_norm_final, loss_p) if l == 0 else (jnp.zeros((1, D), F32), jnp.zeros((8, 128), F32))
        small_v[l], small_w[l] = small_grads(lvec, g_ret_norm, g_norm_mix, g_norm_ffn, g_fin, loss_part, dwa, dwx,
                                             "small_grads")
        if l == 1:
            early = place_blocks(dev_arr, [jnp.stack(small_v[1:]), jnp.stack(small_w[1:])], "place_grads")
            early_sems = ag_start(early, order, "ag_start_grads")
            order = early_sems[3]

    grad_x = dh[X0:][None]
    g_meta = dh[PAD:X0]

    late = all_gather([small_v[0], small_w[0], g_meta], order, "ag_grads")
    s2, r2, lands, _ = ag_forward(early_sems[0], early_sems[1], early_sems[2], dh, "ag_forward_grads")
    gath_early = ag_finish(s2, r2, lands, late[0], "ag_finish_grads")
    v0, w0, meta_sum, v123, w123 = sum_devices(list(late) + list(gath_early), "sum_devices")
    loss = v0[16, 0]
    vecs = jnp.concatenate([v0[None], v123])
    gws = jnp.concatenate([w0[None], w123])
    blocks = (DEPTH, LRU_BLOCKS, LRU_BD)
    small_g = dict(
        conv_w=lax.dynamic_slice_in_dim(vecs[:, 0:CONV_W], dev * (D_LRU // NDEV), D_LRU // NDEV, axis=2),
        conv_b=vecs[:, 4], gate_a_b=vecs[:, 5].reshape(blocks), gate_x_b=vecs[:, 6].reshape(blocks),
        lru_lambda=vecs[:, 7], lru_out_norm=vecs[:, 8], ret_out_norm=vecs[:, 9],
        norm_mix=vecs[:, 10:12].reshape(DEPTH, D), norm_ffn=vecs[:, 12:14].reshape(DEPTH, D),
        norm_final=v0[14:16].reshape(1, D),
        gate_a_w=gws[:, :D_LRU].reshape(blocks + (LRU_BD,)), gate_x_w=gws[:, D_LRU:].reshape(blocks + (LRU_BD,)),
        meta_tokens=lax.dynamic_slice_in_dim(meta_sum, dev * (D // NDEV), D // NDEV, axis=1))
    given = dict(norm_mix=(norm_mix, m_norm_mix, v_norm_mix), conv_b=(conv_b, m_conv_b, v_conv_b),
                 gate_a_w=(gate_a_w, m_gate_a_w, v_gate_a_w), gate_a_b=(gate_a_b, m_gate_a_b, v_gate_a_b),
                 gate_x_w=(gate_x_w, m_gate_x_w, v_gate_x_w), gate_x_b=(gate_x_b, m_gate_x_b, v_gate_x_b),
                 lru_lambda=(lru_lambda, m_lru_lambda, v_lru_lambda), lru_out_norm=(lru_out_norm, m_lru_out_norm, v_lru_out_norm),
                 ret_out_norm=(ret_out_norm, m_ret_out_norm, v_ret_out_norm), norm_ffn=(norm_ffn, m_norm_ffn, v_norm_ffn),
                 norm_final=tuple(a.reshape(1, D) for a in (norm_final, m_norm_final, v_norm_final)),
                 conv_w=(conv_w, m_conv_w, v_conv_w), meta_tokens=(meta_tokens, m_meta_tokens, v_meta_tokens))
    small_names = REP_NAMES + ["conv_w", "meta_tokens"]
    upd = adamw_small([small_g[n] for n in small_names], *[[given[n][k] for n in small_names] for k in range(3)],
                      "adamw_small")
    small_out = [dict(zip(small_names, u)) for u in upd]
    for d_ in [small_g] + small_out:
        d_["norm_final"] = d_["norm_final"].reshape(D)

    arrived = {}

    def wait_for(entries, after):
        for l, tag, names, flying in entries:
            sums, recv = rs_chips_wait(*flying, after, f"rs_chips_wait_{tag}")
            for i, n in enumerate(names):
                arrived[l, n] = (recv[i], sums[i])

    chip = jnp.reshape(2 * xi + yi, (1,)).astype(jnp.int32)

    def finish(wname, w_, m_, v_, tr):
        return adamw_big([arrived[l, wname][0] for l in range(DEPTH)], [arrived[l, wname][1] for l in range(DEPTH)], chip,
                         w_, m_, v_, tr, "adamw_" + wname)

    wait_for(inflight[:-1], upd[0][0])
    o_gate = [tr_(o) for o in finish("w_gate", w_gate_t, m_w_gate_t, v_w_gate_t, 32)]
    o_up = [tr_(o) for o in finish("w_up", w_up_t, m_w_up_t, v_w_up_t, 32)]
    o_down = finish("w_down", w_down, m_w_down, v_w_down, 32)
    wait_for(inflight[-1:], o_down[0])
    o_in = finish("w_in", w_in, m_w_in, v_w_in, 256)
    o_out = finish("w_out", w_out, m_w_out, v_w_out, 64)

    bigs = dict(w_in=o_in, w_out=o_out, w_gate=o_gate, w_up=o_up, w_down=o_down)
    order = ["meta_tokens", "norm_mix", "w_in", "conv_w", "conv_b", "gate_a_w", "gate_a_b", "gate_x_w", "gate_x_b", "lru_lambda",
             "lru_out_norm", "ret_out_norm", "w_out", "norm_ffn", "w_gate", "w_up", "w_down", "norm_final"]
    grads = [bigs[n][0] if n in bigs else small_g[n] for n in order]
    rest = [[bigs[n][k + 1] if n in bigs else small_out[k][n] for n in order] for k in range(3)]
    return (loss, grad_x, *grads, *rest[0], *rest[1], *rest[2])
```

```python
import numpy as np
import jax
import jax.numpy as jnp
from jax import lax
from jax.experimental import pallas as pl
from jax.experimental.pallas import tpu as pltpu

F32, BF16 = jnp.float32, jnp.bfloat16
MXU_DTYPE = BF16
WIRE_DTYPE = BF16

D = 1024
SEQ = 2048
DEPTH = 4
N_META = 16
CH = 128
PAD = (-(SEQ + N_META)) % CH
T = SEQ + N_META + PAD
NCH = T // CH
X0 = PAD + N_META
D_LRU = 512
LRU_BLOCKS = 8
LRU_BD = 64
CONV_W = 4
LRU_C = 8.0
D_RET = 512
HEADS = 4
HD = 128
ROPE_BASE = 10000.0
D_IN = 3072
D_FF = 2816
NDEV = 8
IN_SH = D_IN // NDEV
FF_SH = D_FF // NDEV
FF_SHP = 384
D_FFP = NDEV * FF_SHP
OUT_SH = D // NDEV
EPS = 1e-6
TM = 544
VMEM_LIMIT = 56 * 2**20
MESH = pl.DeviceIdType.MESH

ADAM_LR, ADAM_B1, ADAM_B2, ADAM_EPS, ADAM_WD, ADAM_STEP = 0.001, 0.9, 0.999, 1e-08, 0.01, 10

NN = ((1,), (0,))
NT = ((1,), (1,))
TN = ((0,), (0,))


def _dot(a, b, dims):
    return lax.dot_general(a.astype(MXU_DTYPE), b.astype(MXU_DTYPE), (dims, ((), ())), preferred_element_type=F32)


def _sds(shape, dtype):
    return jax.ShapeDtypeStruct(shape, dtype)


def _params(sem=None):
    return pltpu.CompilerParams(dimension_semantics=sem, vmem_limit_bytes=VMEM_LIMIT)


def _full(shape):
    n = len(shape)
    return pl.BlockSpec(shape, lambda *_: (0,) * n)


def rmsnorm_fwd(h, gain, name):
    def body(h_ref, g_ref, o_ref):
        x = h_ref[...]
        ms = jnp.mean(x * x, axis=-1, keepdims=True)
        o_ref[...] = (x * lax.rsqrt(ms + EPS) * g_ref[...]).astype(o_ref.dtype)

    return pl.pallas_call(
        body, name=name, grid=(T // TM,),
        in_specs=[pl.BlockSpec((TM, D), lambda i: (i, 0)), _full((1, D))],
        out_specs=pl.BlockSpec((TM, D), lambda i: (i, 0)),
        out_shape=_sds((T, D), MXU_DTYPE), compiler_params=_params(("parallel",)),
    )(h, gain)


def rmsnorm_bwd(h, gain, dhn, dres, name):
    def body(h_ref, g_ref, dhn_ref, dres_ref, dh_ref, dhb_ref, dg_ref):
        x = h_ref[...]
        rstd = lax.rsqrt(jnp.mean(x * x, axis=-1, keepdims=True) + EPS)
        xhat = x * rstd
        dy = dhn_ref[...]
        dyg = dy * g_ref[...]
        dh = dres_ref[...] + rstd * (dyg - xhat * jnp.mean(dyg * xhat, axis=-1, keepdims=True))
        dh_ref[...] = dh
        dhb_ref[...] = dh.astype(dhb_ref.dtype)

        @pl.when(pl.program_id(0) == 0)
        def _():
            dg_ref[...] = jnp.zeros_like(dg_ref)
        dg_ref[...] += jnp.sum(dy * xhat, axis=0, keepdims=True)

    row = pl.BlockSpec((TM, D), lambda i: (i, 0))
    return pl.pallas_call(
        body, name=name, grid=(T // TM,),
        in_specs=[row, _full((1, D)), row, row],
        out_specs=[row, row, _full((1, D))],
        out_shape=[_sds((T, D), F32), _sds((T, D), MXU_DTYPE), _sds((1, D), F32)], compiler_params=_params(("arbitrary",)),
    )(h, gain, dhn, dres)


def loss_head(h, gain, target, name):
    def body(h_ref, g_ref, t_ref, loss_ref, dh_ref, dhb_ref, dg_ref):
        i = pl.program_id(0)

        @pl.when(i == 0)
        def _():
            loss_ref[...] = jnp.zeros_like(loss_ref)
            dg_ref[...] = jnp.zeros_like(dg_ref)
            dh_ref[...] = jnp.zeros_like(dh_ref)
            dhb_ref[...] = jnp.zeros_like(dhb_ref)

        @pl.when(i > 0)
        def _():
            x = h_ref[...]
            g = g_ref[...]
            rstd = lax.rsqrt(jnp.mean(x * x, axis=-1, keepdims=True) + EPS)
            xhat = x * rstd
            err = xhat * g - t_ref[...]
            loss_ref[...] += 0.5 * jnp.sum(jnp.mean(err * err, axis=-1, keepdims=True), axis=0, keepdims=True)
            dy = err * (1.0 / D)
            dyg = dy * g
            dh = rstd * (dyg - xhat * jnp.mean(dyg * xhat, axis=-1, keepdims=True))
            dh_ref[...] = dh
            dhb_ref[...] = dh.astype(dhb_ref.dtype)
            dg_ref[...] += jnp.sum(dy * xhat, axis=0, keepdims=True)

    row = pl.BlockSpec((CH, D), lambda i: (i, 0))
    return pl.pallas_call(
        body, name=name, grid=(NCH,),
        in_specs=[row, _full((1, D)), pl.BlockSpec((CH, D), lambda i: (jnp.maximum(i - 1, 0), 0))],
        out_specs=[_full((8, 128)), row, row, _full((1, D))],
        out_shape=[_sds((8, 128), F32), _sds((T, D), F32), _sds((T, D), MXU_DTYPE), _sds((1, D), F32)],
        compiler_params=_params(("arbitrary",)),
    )(h, gain, target)


PAIR = 2 * IN_SH
NPAIR = NDEV // 2
BN = 256
FB = 512


def _pair_cols(w_ref):
    return jnp.concatenate([w_ref[0], w_ref[1]], axis=1)


W_PAIR = lambda k: pl.BlockSpec((2, k, IN_SH), lambda j: (j, 0, 0))
COLS_PAIR = pl.BlockSpec((T, PAIR), lambda j: (0, j))
ANYSPEC = pl.BlockSpec(memory_space=pl.ANY)


def mm_blocked_nn(a, w, out_dtype, name):
    k = a.shape[1]

    def body(a_ref, w_ref, o_ref):
        o_ref[...] = _dot(a_ref[...], _pair_cols(w_ref), NN).astype(o_ref.dtype)

    return pl.pallas_call(
        body, name=name, grid=(NPAIR,),
        in_specs=[_full((T, k)), W_PAIR(k)], out_specs=COLS_PAIR,
        out_shape=_sds((T, NDEV * IN_SH), out_dtype), compiler_params=_params(("parallel",)),
    )(a, w)


def mm_nn_res(a, w, res, after, name):
    k = a.shape[1]

    def body(a_ref, w_ref, r_ref, after_ref, o_ref):
        del after_ref
        o_ref[...] = r_ref[...] + _dot(a_ref[...], w_ref[...], NN)

    col = pl.BlockSpec((T, BN), lambda j: (0, j))
    return pl.pallas_call(
        body, name=name, grid=(D // BN,),
        in_specs=[_full((T, k)), pl.BlockSpec((k, BN), lambda j: (0, j)), col, ANYSPEC], out_specs=col,
        out_shape=_sds((T, D), F32), compiler_params=_params(("parallel",)),
    )(a, w, res, after)


def ffn_up(hn, wg, wu, name):
    def body(a_ref, wg_ref, wu_ref, dg_ref, du_ref, act_ref):
        a = a_ref[...]
        for c in range(FB // BN):
            cols = slice(BN * c, BN * (c + 1))
            g = _dot(a, wg_ref[cols, :], NT)
            u = _dot(a, wu_ref[cols, :], NT)
            sg = jax.nn.sigmoid(g)
            silu = g * sg
            dg_ref[:, cols] = (u * (sg * (1.0 + g * (1.0 - sg)))).astype(dg_ref.dtype)
            du_ref[:, cols] = silu.astype(du_ref.dtype)
            act_ref[:, cols] = (silu * u).astype(act_ref.dtype)

    wspec = pl.BlockSpec((FB, D), lambda j: (j, 0))
    ospec = pl.BlockSpec((T, FB), lambda j: (0, j))
    return pl.pallas_call(
        body, name=name, grid=(D_FFP // FB,),
        in_specs=[_full((T, D)), wspec, wspec], out_specs=[ospec] * 3,
        out_shape=[_sds((T, D_FFP), MXU_DTYPE)] * 3, compiler_params=_params(("parallel",)),
    )(hn, wg, wu)


def ffn_down_bwd(dh, wd, dact_dgate, dact_dup, after, name):
    def body(dh_ref, wd_ref, g_ref, u_ref, after_ref, dg_ref, du_ref):
        del after_ref
        dh = dh_ref[...]
        for c in range(FB // BN):
            cols = slice(BN * c, BN * (c + 1))
            dact = _dot(dh, wd_ref[cols, :], NT)
            dg_ref[:, cols] = (dact * g_ref[:, cols].astype(F32)).astype(dg_ref.dtype)
            du_ref[:, cols] = (dact * u_ref[:, cols].astype(F32)).astype(du_ref.dtype)

    blk = pl.BlockSpec((T, FB), lambda j: (0, j))
    return pl.pallas_call(
        body, name=name, grid=(D_FFP // FB,),
        in_specs=[_full((T, D)), pl.BlockSpec((FB, D), lambda j: (j, 0)), blk, blk, ANYSPEC],
        out_specs=[blk, blk],
        out_shape=[_sds((T, D_FFP), MXU_DTYPE)] * 2, compiler_params=_params(("parallel",)),
    )(dh, wd, dact_dgate, dact_dup, after)


def mm_blocked_nt(pairs, after, name):
    n = len(pairs)

    def body(*refs):
        o_ref = refs[2 * n + 1]

        @pl.when(pl.program_id(0) == 0)
        def _():
            o_ref[...] = jnp.zeros_like(o_ref)
        for p in range(n):
            o_ref[...] += _dot(refs[2 * p][...], _pair_cols(refs[2 * p + 1]), NT)

    specs, args = [], []
    for a, w in pairs:
        specs += [COLS_PAIR, W_PAIR(D)]
        args += [a, w]
    return pl.pallas_call(
        body, name=name, grid=(NPAIR,), in_specs=specs + [ANYSPEC], out_specs=_full((T, D)),
        out_shape=_sds((T, D), F32), compiler_params=_params(("arbitrary",)),
    )(*args, after)


def mm_tn_two(a1, a2, b, bm, after, name):
    m = a1.shape[1]

    def body(a1_ref, a2_ref, b_ref, after_ref, o1_ref, o2_ref):
        del after_ref
        b = b_ref[...]
        o1_ref[...] = _dot(a1_ref[...], b, TN).astype(o1_ref.dtype)
        o2_ref[...] = _dot(a2_ref[...], b, TN).astype(o2_ref.dtype)

    blk = pl.BlockSpec((T, bm), lambda i: (0, i))
    out = pl.BlockSpec((bm, D), lambda i: (i, 0))
    return pl.pallas_call(
        body, name=name, grid=(m // bm,),
        in_specs=[blk, blk, _full((T, D)), ANYSPEC], out_specs=[out, out],
        out_shape=[_sds((m, D), WIRE_DTYPE)] * 2, compiler_params=_params(("parallel",)),
    )(a1, a2, b, after)


def out_proj_bwd(dh, w, ymix, after, name):
    def body(dh_ref, w_ref, y_ref, after_ref, dy_ref, dw_ref):
        del after_ref
        dh_ = dh_ref[...]
        dy_ref[...] = _dot(dh_, w_ref[...], NT)
        dw_ref[...] = _dot(y_ref[...], dh_, TN).astype(dw_ref.dtype)

    return pl.pallas_call(
        body, name=name, grid=(D // BN,),
        in_specs=[_full((T, D)), pl.BlockSpec((BN, D), lambda j: (j, 0)), pl.BlockSpec((T, BN), lambda j: (0, j)), ANYSPEC],
        out_specs=[pl.BlockSpec((T, BN), lambda j: (0, j)), pl.BlockSpec((BN, D), lambda j: (j, 0))],
        out_shape=[_sds((T, D), F32), _sds((D, D), WIRE_DTYPE)], compiler_params=_params(("parallel",)),
    )(dh, w, ymix, after)


def mm_rows_nn(pairs, after, name):
    n = len(pairs)

    def body(*refs):
        o_ref = refs[2 * n + 1]

        @pl.when(pl.program_id(0) == 0)
        def _():
            o_ref[...] = jnp.zeros_like(o_ref)
        for p in range(n):
            o_ref[...] += _dot(refs[2 * p][...], refs[2 * p + 1][...], NN)

    specs, args = [], []
    for a, w in pairs:
        specs += [pl.BlockSpec((T, FB), lambda j: (0, j)), pl.BlockSpec((FB, D), lambda j: (j, 0))]
        args += [a, w]
    return pl.pallas_call(
        body, name=name, grid=(D_FFP // FB,), in_specs=specs + [ANYSPEC], out_specs=_full((T, D)),
        out_shape=_sds((T, D), F32), compiler_params=_params(("arbitrary",)),
    )(*args, after)


def mm_tn_blocked(a, b, name):
    def body(a_ref, b_ref, o_ref):
        o = _dot(a_ref[...], b_ref[...], TN).astype(o_ref.dtype)
        o_ref[0] = o[:, :IN_SH]
        o_ref[1] = o[:, IN_SH:]

    return pl.pallas_call(
        body, name=name, grid=(NPAIR,),
        in_specs=[_full((T, D)), COLS_PAIR], out_specs=W_PAIR(D),
        out_shape=_sds((NDEV, D, IN_SH), WIRE_DTYPE), compiler_params=_params(("parallel",)),
    )(a, b)


def mm_tn(a, b, bm, after, name):
    m = a.shape[1]

    def body(a_ref, b_ref, after_ref, o_ref):
        del after_ref
        o_ref[...] = _dot(a_ref[...], b_ref[...], TN).astype(o_ref.dtype)

    return pl.pallas_call(
        body, name=name, grid=(m // bm,),
        in_specs=[pl.BlockSpec((T, bm), lambda i: (0, i)), _full((T, D)), ANYSPEC],
        out_specs=pl.BlockSpec((bm, D), lambda i: (i, 0)),
        out_shape=_sds((m, D), WIRE_DTYPE), compiler_params=_params(("parallel",)),
    )(a, b, after)


def _softplus_neg(lam):
    return jnp.maximum(-lam, 0.0) + jnp.log1p(jnp.exp(-jnp.abs(lam)))


def _lru_gates(pa, px, xc, lam):
    r = jax.nn.sigmoid(pa)
    ig = jax.nn.sigmoid(px)
    sp = _softplus_neg(lam)
    log_a = -LRU_C * r * sp
    a = jnp.exp(log_a)
    mult = jnp.sqrt(-jnp.tanh(log_a) * (a * a + 1.0))
    return a, mult * (ig * xc), (r, ig, sp, mult)


def _lru_gates_vjp(da, db, xc, lam, a, r, ig, sp, mult):
    dmult = db * (ig * xc)
    du = db * mult
    dlog_a = da * a - dmult * (a * a) / mult
    dr = dlog_a * (-LRU_C * sp)
    dlam = jnp.sum(dlog_a * (-LRU_C * r), axis=0, keepdims=True) * (-jax.nn.sigmoid(-lam))
    dpa = dr * (r * (1.0 - r))
    dpx = (du * xc) * (ig * (1.0 - ig))
    return dpa, dpx, du * ig, dlam


def _lru_out(h, g, gain):
    z = h * jax.nn.gelu(g)
    return z * lax.rsqrt(jnp.mean(z * z, axis=-1, keepdims=True) + EPS) * gain


def _conv_taps(x, xprev, row):
    taps = [x]
    for s in range(1, CONV_W):
        taps.append(jnp.where(row < s, pltpu.roll(xprev, s, 0), pltpu.roll(x, s, 0)))
    return taps


def _conv(taps, cw_ref, cb):
    xc = cb + cw_ref[CONV_W - 1:CONV_W, :] * taps[0]
    for s in range(1, CONV_W):
        xc = xc + cw_ref[CONV_W - 1 - s:CONV_W - s, :] * taps[s]
    return xc


def _lru_fwd_block(i, x_ref, g_ref, cw_ref, cb_ref, wa_ref, ba_ref, wx_ref, bx_ref, lam_ref, gain_ref, y_ref, h_ref,
                   xprev_scr, a_scr, b_scr, carry_scr):
    @pl.when(i == 0)
    def _():
        xprev_scr[...] = jnp.zeros_like(xprev_scr)
        carry_scr[...] = jnp.zeros_like(carry_scr)

    x = x_ref[...]
    row = lax.broadcasted_iota(jnp.int32, (CH, D_LRU), 0)
    xc = _conv(_conv_taps(x, xprev_scr[...], row), cw_ref, cb_ref[...])
    pa = _dot(xc, wa_ref[...], NN) + ba_ref[...]
    px = _dot(xc, wx_ref[...], NN) + bx_ref[...]
    a, b, _ = _lru_gates(pa, px, xc, lam_ref[...])
    a_scr[...] = a
    b_scr[...] = jnp.where(i * CH + row >= PAD, b, 0.0)
    h = carry_scr[...]
    for t in range(CH):
        h = a_scr[t:t + 1, :] * h + b_scr[t:t + 1, :]
        h_ref[t:t + 1, :] = h
    carry_scr[...] = h
    xprev_scr[...] = x
    y_ref[:, :D_LRU] = _lru_out(h_ref[...], g_ref[...], gain_ref[...]).astype(y_ref.dtype)


LRU_VEC_ROWS = 16


def _lru_bwd_block(ib, x_ref, xp_ref, g_ref, h_ref, hp_ref, dy_ref, cw_ref, cb_ref, wa_ref, ba_ref, wx_ref, bx_ref, lam_ref,
                   gain_ref, dp_ref, vec_ref, dwa_ref, dwx_ref, a_scr, dh_scr, g_scr, carry_scr, dxcn_scr):
    @pl.when(ib == NCH - 1)
    def _():
        carry_scr[...] = jnp.zeros_like(carry_scr)
        dxcn_scr[...] = jnp.zeros_like(dxcn_scr)
        vec_ref[...] = jnp.zeros_like(vec_ref)
        dwa_ref[...] = jnp.zeros_like(dwa_ref)
        dwx_ref[...] = jnp.zeros_like(dwx_ref)

    x = x_ref[...]
    row = lax.broadcasted_iota(jnp.int32, (CH, D_LRU), 0)
    valid = ib * CH + row >= PAD
    taps = _conv_taps(x, xp_ref[...], row)
    xc = _conv(taps, cw_ref, cb_ref[...])
    pa = _dot(xc, wa_ref[...], NN) + ba_ref[...]
    px = _dot(xc, wx_ref[...], NN) + bx_ref[...]
    a, _, gate_parts = _lru_gates(pa, px, xc, lam_ref[...])
    h = h_ref[...]
    _, vjp_out = jax.vjp(_lru_out, h, g_ref[...], gain_ref[...])
    dh, dg, dgain = vjp_out(dy_ref[:, :D_LRU].astype(F32))
    a_scr[...] = a
    dh_scr[...] = dh
    c = carry_scr[...]
    for t in range(CH - 1, -1, -1):
        gt = dh_scr[t:t + 1, :] + c
        g_scr[t:t + 1, :] = gt
        c = a_scr[t:t + 1, :] * gt
    carry_scr[...] = c
    gg = g_scr[...]
    hprev = jnp.where(row < 1, pltpu.roll(hp_ref[...], 1, 0), pltpu.roll(h, 1, 0))
    da = jnp.where(valid, gg * hprev, 0.0)
    db = jnp.where(valid, gg, 0.0)
    dpa, dpx, dxc, dlam = _lru_gates_vjp(da, db, xc, lam_ref[...], a, *gate_parts)
    dxc = dxc + _dot(dpa, wa_ref[...], NT) + _dot(dpx, wx_ref[...], NT)
    dwa_ref[...] += _dot(xc, dpa, TN)
    dwx_ref[...] += _dot(xc, dpx, TN)
    for s in range(CONV_W):
        vec_ref[CONV_W - 1 - s:CONV_W - s, :] += jnp.sum(dxc * taps[s], axis=0, keepdims=True)
    vec_ref[4:5, :] += jnp.sum(dxc, axis=0, keepdims=True)
    vec_ref[5:6, :] += jnp.sum(dpa, axis=0, keepdims=True)
    vec_ref[6:7, :] += jnp.sum(dpx, axis=0, keepdims=True)
    vec_ref[7:8, :] += dlam
    vec_ref[8:9, :] += dgain
    dxn = dxcn_scr[...]
    dx = cw_ref[CONV_W - 1:CONV_W, :] * dxc
    for s in range(1, CONV_W):
        ahead = jnp.where(row >= CH - s, pltpu.roll(dxn, CH - s, 0), pltpu.roll(dxc, CH - s, 0))
        dx = dx + cw_ref[CONV_W - 1 - s:CONV_W - s, :] * ahead
    dxcn_scr[...] = dxc
    dp_ref[:, :D_LRU] = jnp.where(valid, dx, 0.0).astype(dp_ref.dtype)
    dp_ref[:, D_LRU:2 * D_LRU] = dg.astype(dp_ref.dtype)


def _ret_tables():
    half = HD // 2
    pos = jnp.arange(T, dtype=F32) - float(PAD)
    inv = ROPE_BASE ** (-jnp.arange(half, dtype=F32) / half)
    ang = pos[:, None] * inv[None, :]
    cos = jnp.concatenate([jnp.cos(ang), jnp.cos(ang)], axis=-1)
    sin = jnp.concatenate([-jnp.sin(ang), jnp.sin(ang)], axis=-1)
    log_g = jnp.log(1.0 - 2.0 ** (-5.0 - jnp.arange(HEADS, dtype=F32)))
    idx = jnp.arange(CH, dtype=F32)
    diff = idx[:, None] - idx[None, :]
    dmask = jnp.where(diff[None] >= 0, jnp.exp(jnp.maximum(diff, 0.0)[None] * log_g[:, None, None]), 0.0)
    xi = jnp.exp((idx + 1.0)[None, :] * log_g[:, None])
    zeta = jnp.exp((CH - 1.0 - idx)[None, :] * log_g[:, None])
    xi = jnp.broadcast_to(xi[:, :, None], (HEADS, CH, HD))
    zeta = jnp.broadcast_to(zeta[:, :, None], (HEADS, CH, HD))
    return cos, sin, dmask, xi, zeta


def _chunk_decay():
    log_g = np.log(np.float32(1.0) - np.float32(2.0) ** (np.float32(-5.0) - np.arange(HEADS, dtype=np.float32)))
    return [float(v) for v in np.exp(np.float32(CH) * log_g.astype(np.float32))]


def _rope(x, cos, sin):
    return x * cos + pltpu.roll(x, HD // 2, 1) * sin


def mix_fwd(proj, cw, cb, wa, ba, wx, bx, lam, gain, tables, ret_gain, after, name):
    cos, sin, dmask, xi, zeta = tables
    gch = _chunk_decay()
    scale = HD ** -0.5

    def body(x_ref, gl_ref, cw_ref, cb_ref, wa_ref, ba_ref, wx_ref, bx_ref, lam_ref, lgain_ref,
             q_ref, k_ref, v_ref, g_ref, cos_ref, sin_ref, dm_ref, xi_ref, zt_ref, gain_ref, after_ref,
             y_ref, h_ref, st_ref, xprev_scr, a_scr, b_scr, carry_scr, s_scr):
        del after_ref

        @pl.when(pl.program_id(0) == 0)
        def _():
            s_scr[...] = jnp.zeros_like(s_scr)

        _lru_fwd_block(pl.program_id(0), x_ref, gl_ref, cw_ref, cb_ref, wa_ref, ba_ref, wx_ref, bx_ref, lam_ref, lgain_ref,
                       y_ref, h_ref, xprev_scr, a_scr, b_scr, carry_scr)
        cs, sn = cos_ref[...], sin_ref[...]
        hs = range(HEADS)
        sl = [slice(HD * h, HD * (h + 1)) for h in hs]
        qr = [_rope(q_ref[:, sl[h]], cs, sn).astype(MXU_DTYPE) for h in hs]
        kf = [_rope(k_ref[:, sl[h]], cs, sn) * scale for h in hs]
        kr = [kf[h].astype(MXU_DTYPE) for h in hs]
        v = [v_ref[:, sl[h]].astype(MXU_DTYPE) for h in hs]
        s = [s_scr[h] for h in hs]
        for h in hs:
            st_ref[h] = s[h]
        sc = [_dot(qr[h], kr[h], NT) * dm_ref[h] for h in hs]
        cross = [_dot(qr[h], s[h], NN) * xi_ref[h] for h in hs]
        for h in hs:
            s_scr[h] = s[h] * gch[h] + _dot(kf[h] * zt_ref[h], v[h], TN)
        y = [_dot(sc[h], v[h], NN) + cross[h] for h in hs]
        yc = [y[h] - jnp.mean(y[h], axis=-1, keepdims=True) for h in hs]
        yn = [yc[h] * lax.rsqrt(jnp.mean(yc[h] * yc[h], axis=-1, keepdims=True) + EPS) for h in hs]
        for h in hs:
            so = slice(D_LRU + HD * h, D_LRU + HD * (h + 1))
            y_ref[:, so] = (jax.nn.silu(g_ref[:, sl[h]]) * (yn[h] * gain_ref[:, sl[h]])).astype(y_ref.dtype)

    def col(c):
        return pl.BlockSpec((CH, D_RET), lambda n: (n, c))

    tab = pl.BlockSpec((CH, HD), lambda n: (n, 0))
    cst = _full((HEADS, CH, HD))
    vec = _full((1, D_LRU))
    mat = _full((D_LRU, D_LRU))
    blockbuf = pltpu.VMEM((CH, D_LRU), F32)
    return pl.pallas_call(
        body, name=name, grid=(NCH,),
        in_specs=[col(0), col(1), _full((CONV_W, D_LRU)), vec, mat, vec, mat, vec, vec, vec,
                  col(2), col(3), col(4), col(5), tab, tab, cst, cst, cst, _full((1, D_RET)),
                  pl.BlockSpec(memory_space=pl.ANY)],
        out_specs=[pl.BlockSpec((CH, D), lambda n: (n, 0)), col(0), pl.BlockSpec((None, HEADS, HD, HD), lambda n: (n, 0, 0, 0))],
        out_shape=[_sds((T, D), MXU_DTYPE), _sds((T, D_LRU), F32), _sds((NCH, HEADS, HD, HD), F32)],
        scratch_shapes=[blockbuf, blockbuf, blockbuf, pltpu.VMEM((1, D_LRU), F32), pltpu.VMEM((HEADS, HD, HD), F32)],
        compiler_params=_params(("arbitrary",)),
    )(proj, proj, cw, cb, wa, ba, wx, bx, lam, gain, proj, proj, proj, proj, cos, sin, dmask, xi, zeta, ret_gain, after)


def mix_bwd(proj, hst, states, dymix, cw, cb, wa, ba, wx, bx, lam, gain, tables, ret_gain, after, name):
    cos, sin, dmask, xi, zeta = tables
    gch = _chunk_decay()
    scale = HD ** -0.5
    last = NCH - 1

    def body(x_ref, xp_ref, gl_ref, h_ref, hp_ref, cw_ref, cb_ref, wa_ref, ba_ref, wx_ref, bx_ref, lam_ref, lgain_ref,
             q_ref, k_ref, v_ref, g_ref, st_ref, dy_ref, cos_ref, sin_ref, dm_ref, xi_ref, zt_ref, gain_ref, after_ref,
             dp_ref, vec_ref, dwa_ref, dwx_ref, dgain_ref, a_scr, dh_scr, g_scr, carry_scr, dxcn_scr, ds_scr):
        del after_ref

        @pl.when(pl.program_id(0) == 0)
        def _():
            ds_scr[...] = jnp.zeros_like(ds_scr)
            dgain_ref[...] = jnp.zeros_like(dgain_ref)

        _lru_bwd_block(last - pl.program_id(0), x_ref, xp_ref, gl_ref, h_ref, hp_ref, dy_ref, cw_ref, cb_ref, wa_ref, ba_ref,
                       wx_ref, bx_ref, lam_ref, lgain_ref, dp_ref, vec_ref, dwa_ref, dwx_ref, a_scr, dh_scr, g_scr, carry_scr,
                       dxcn_scr)
        cs, sn = cos_ref[...], sin_ref[...]
        hs = range(HEADS)
        sl = [slice(HD * h, HD * (h + 1)) for h in hs]

        def out(j, h):
            return slice(2 * D_LRU + j * D_RET + HD * h, 2 * D_LRU + j * D_RET + HD * (h + 1))

        b16 = lambda xs: [x.astype(MXU_DTYPE) for x in xs]
        qr = b16([_rope(q_ref[:, sl[h]], cs, sn) for h in hs])
        kf = [_rope(k_ref[:, sl[h]], cs, sn) * scale for h in hs]
        kr = b16(kf)
        kz = b16([kf[h] * zt_ref[h] for h in hs])
        v = b16([v_ref[:, sl[h]] for h in hs])
        s = b16([st_ref[h] for h in hs])
        ds = [ds_scr[h] for h in hs]
        dsb = b16(ds)
        sc = [_dot(qr[h], kr[h], NT) * dm_ref[h] for h in hs]
        scb = b16(sc)
        y = [_dot(scb[h], v[h], NN) + _dot(qr[h], s[h], NN) * xi_ref[h] for h in hs]
        yc = [y[h] - jnp.mean(y[h], axis=-1, keepdims=True) for h in hs]
        rstd = [lax.rsqrt(jnp.mean(yc[h] * yc[h], axis=-1, keepdims=True) + EPS) for h in hs]
        yn = [yc[h] * rstd[h] for h in hs]
        dy = []
        for h in hs:
            g = g_ref[:, sl[h]]
            gain = gain_ref[:, sl[h]]
            sg = jax.nn.sigmoid(g)
            silu = g * sg
            dout = dy_ref[:, D_LRU + HD * h:D_LRU + HD * (h + 1)].astype(F32)
            dgain_ref[:, sl[h]] += jnp.sum(dout * silu * yn[h], axis=0, keepdims=True)
            dp_ref[:, out(3, h)] = (dout * yn[h] * gain * (sg * (1.0 + g * (1.0 - sg)))).astype(dp_ref.dtype)
            dyn = dout * silu * gain
            dy.append(rstd[h] * (dyn - jnp.mean(dyn, axis=-1, keepdims=True)
                                 - yn[h] * jnp.mean(dyn * yn[h], axis=-1, keepdims=True)))
        dyb = b16(dy)
        dqs = b16([dy[h] * xi_ref[h] for h in hs])
        dp = b16([_dot(dyb[h], v[h], NT) * dm_ref[h] for h in hs])
        dv = [_dot(scb[h], dyb[h], TN) + _dot(kz[h], dsb[h], NN) for h in hs]
        dqr = [_dot(dp[h], kr[h], NN) + _dot(dqs[h], s[h], NT) for h in hs]
        dkr = [_dot(dp[h], qr[h], TN) + _dot(v[h], dsb[h], NT) * zt_ref[h] for h in hs]
        for h in hs:
            ds_scr[h] = gch[h] * ds[h] + _dot(qr[h], dqs[h], TN)
        for h in hs:
            dp_ref[:, out(0, h)] = (dqr[h] * cs + pltpu.roll(dqr[h] * sn, HD // 2, 1)).astype(dp_ref.dtype)
            dp_ref[:, out(1, h)] = ((dkr[h] * cs + pltpu.roll(dkr[h] * sn, HD // 2, 1)) * scale).astype(dp_ref.dtype)
            dp_ref[:, out(2, h)] = dv[h].astype(dp_ref.dtype)

    def col(c, shift=0):
        return pl.BlockSpec((CH, D_RET), lambda n: (jnp.maximum(last - n - shift, 0), c))

    tab = pl.BlockSpec((CH, HD), lambda n: (last - n, 0))
    cst = _full((HEADS, CH, HD))
    vec = _full((1, D_LRU))
    mat = _full((D_LRU, D_LRU))
    blockbuf = pltpu.VMEM((CH, D_LRU), F32)
    return pl.pallas_call(
        body, name=name, grid=(NCH,),
        in_specs=[col(0), col(0, 1), col(1), col(0), col(0, 1), _full((CONV_W, D_LRU)), vec, mat, vec, mat, vec, vec, vec,
                  col(2), col(3), col(4), col(5), pl.BlockSpec((None, HEADS, HD, HD), lambda n: (last - n, 0, 0, 0)),
                  pl.BlockSpec((CH, D), lambda n: (last - n, 0)), tab, tab, cst, cst, cst, _full((1, D_RET)),
                  pl.BlockSpec(memory_space=pl.ANY)],
        out_specs=[pl.BlockSpec((CH, D_IN), lambda n: (last - n, 0)), _full((LRU_VEC_ROWS, D_LRU)), mat, mat,
                   _full((1, D_RET))],
        out_shape=[_sds((T, D_IN), MXU_DTYPE), _sds((LRU_VEC_ROWS, D_LRU), F32), _sds((D_LRU, D_LRU), F32),
                   _sds((D_LRU, D_LRU), F32), _sds((1, D_RET), F32)],
        scratch_shapes=[blockbuf, blockbuf, blockbuf, pltpu.VMEM((1, D_LRU), F32), blockbuf,
                        pltpu.VMEM((HEADS, HD, HD), F32)],
        compiler_params=_params(("arbitrary",)),
    )(proj, proj, proj, hst, hst, cw, cb, wa, ba, wx, bx, lam, gain, proj, proj, proj, proj, states, dymix,
      cos, sin, dmask, xi, zeta, ret_gain, after)


HBM = pl.BlockSpec(memory_space=pltpu.HBM)


def _place():
    return lax.axis_index("x"), lax.axis_index("y"), lax.axis_index("c")


def all_gather(arrs, after, name):
    n = len(arrs)

    def body(*refs):
        ins, outs = refs[:n], refs[n + 1:2 * n + 1]
        send_sems, recv_sems, local_sems = refs[2 * n + 1:]
        x, y, c = _place()
        me, sibling = (x, y, c), (x, y, 1 - c)
        chips = [(1 - x, y), (x, 1 - y), (1 - x, 1 - y)]

        def copy(a, k, block, to, src=None):
            px, py, pc = block
            dst = outs[a].at[4 * px + 2 * py + pc]
            return pltpu.make_async_remote_copy(
                src_ref=dst if src is None else src, dst_ref=dst, send_sem=send_sems.at[a, k], recv_sem=recv_sems.at[a, k],
                device_id=to, device_id_type=MESH)

        mine = [pltpu.make_async_copy(ins[a], outs[a].at[4 * x + 2 * y + c], local_sems.at[a]) for a in range(n)]
        for cp in mine:
            cp.start()
        first = []
        for a in range(n):
            first.append(copy(a, 0, me, sibling, src=ins[a]))
            first += [copy(a, 1 + j, me, (*chip, c), src=ins[a]) for j, chip in enumerate(chips)]
        for cp in first:
            cp.start()
        passed = []
        for j, chip in enumerate(chips):
            for a in range(n):
                copy(a, 1 + j, (*chip, c), me).wait_recv()
                passed.append(copy(a, 4 + j, (*chip, c), sibling))
                passed[-1].start()
        for a in range(n):
            copy(a, 0, sibling, me).wait_recv()
            for j, chip in enumerate(chips):
                copy(a, 4 + j, (*chip, 1 - c), me).wait_recv()
        for cp in first + passed:
            cp.wait_send()
        for cp in mine:
            cp.wait()

    return pl.pallas_call(
        body, name=name,
        in_specs=[HBM] * n + [pl.BlockSpec(memory_space=pl.ANY)], out_specs=[HBM] * n,
        out_shape=[_sds((NDEV,) + a.shape, a.dtype) for a in arrs],
        scratch_shapes=[pltpu.SemaphoreType.DMA((n, 7)), pltpu.SemaphoreType.DMA((n, 7)), pltpu.SemaphoreType.DMA((n,))],
    )(*arrs, after)


SEM = pl.BlockSpec(memory_space=pltpu.SEMAPHORE)
ANY = pl.BlockSpec(memory_space=pl.ANY)
EFFECT = pltpu.SideEffectType.DATAFLOW_SIDE_EFFECTING


def _hbm(a):
    return pltpu.with_memory_space_constraint(a, pltpu.HBM)


def _hbm_like(arrs):
    return [pltpu.HBM(a.shape, a.dtype) for a in arrs]


def _dma_sems(count):
    return [pltpu.SemaphoreType.DMA(())] * count


def _ag_copy(lands, send_sems, recv_sems, per):
    def copy(a, k, block, to, src=None):
        px, py, pc = block
        dst = lands[a].at[4 * px + 2 * py + pc]
        return pltpu.make_async_remote_copy(
            src_ref=dst if src is None else src, dst_ref=dst, send_sem=send_sems[a * per + k], recv_sem=recv_sems[a * per + k],
            device_id=to, device_id_type=MESH)
    return copy


def to_wire(sel, w_in, w_out, w_gate, w_up, w_down, name):
    ffpad = FF_SHP - FF_SH

    def body(sel_ref, i_ref, o_ref, g_ref, u_ref, d_ref, oi, oo, og, ou, od):
        del sel_ref
        oi[...] = i_ref[...].astype(oi.dtype)
        oo[...] = o_ref[...].astype(oo.dtype)
        for src, dst in ((g_ref, og), (u_ref, ou), (d_ref, od)):
            dst[:FF_SH, :] = src[...].astype(dst.dtype)
            dst[FF_SH:, :] = jnp.zeros((ffpad, D), dst.dtype)

    shapes_in = [(D, IN_SH), (OUT_SH, D), (FF_SH, D), (FF_SH, D), (FF_SH, D)]
    shapes_out = [(D, IN_SH), (OUT_SH, D), (FF_SHP, D), (FF_SHP, D), (FF_SHP, D)]
    return pl.pallas_call(
        body, name=name,
        grid_spec=pltpu.PrefetchScalarGridSpec(
            num_scalar_prefetch=1, grid=(1,),
            in_specs=[pl.BlockSpec((None,) + s, lambda i, sel_ref: (sel_ref[1], 0, 0)) for s in shapes_in],
            out_specs=[pl.BlockSpec((None,) + s, lambda i, sel_ref: (sel_ref[0], 0, 0)) for s in shapes_out]),
        out_shape=[_sds((NDEV,) + s, WIRE_DTYPE) for s in shapes_out], compiler_params=_params(("arbitrary",)),
    )(sel, w_in, w_out, w_gate, w_up, w_down)


def place_blocks(sel, arrs, name):
    n = len(arrs)

    def body(sel_ref, *refs):
        del sel_ref
        for a in range(n):
            refs[n + a][...] = refs[a][...]

    def whole(a):
        nd = a.ndim
        return pl.BlockSpec(a.shape, lambda i, sel_ref: (0,) * nd)

    def mine(a):
        nd = a.ndim
        return pl.BlockSpec((None,) + a.shape, lambda i, sel_ref: (sel_ref[0],) + (0,) * nd)

    return pl.pallas_call(
        body, name=name,
        grid_spec=pltpu.PrefetchScalarGridSpec(
            num_scalar_prefetch=1, grid=(1,), in_specs=[whole(a) for a in arrs], out_specs=[mine(a) for a in arrs]),
        out_shape=[_sds((NDEV,) + a.shape, a.dtype) for a in arrs], compiler_params=_params(("arbitrary",)),
    )(sel, *arrs)


def ag_start(lands, after, name):
    n = len(lands)
    ns = 4 * n

    def body(*refs):
        lnd = refs[:n]
        send_sems, recv_sems = refs[n + 1:n + 1 + ns], refs[n + 1 + ns:n + 1 + 2 * ns]
        token = refs[-1]
        x, y, c = _place()
        me, sibling = (x, y, c), (x, y, 1 - c)
        chips = [(1 - x, y), (x, 1 - y), (1 - x, 1 - y)]
        copy = _ag_copy(lnd, send_sems, recv_sems, 4)
        for a in range(n):
            copy(a, 0, me, sibling).start()
            for j, chip in enumerate(chips):
                copy(a, 1 + j, me, (*chip, c)).start()
        token[...] = jnp.zeros_like(token)

    outs = pl.pallas_call(
        body, name=name,
        in_specs=[HBM] * n + [ANY],
        out_specs=[SEM] * (2 * ns) + [HBM] * n + [pl.BlockSpec(memory_space=pltpu.VMEM)],
        out_shape=_dma_sems(2 * ns) + _hbm_like(lands) + [_sds((8, 128), F32)],
        input_output_aliases={i: 2 * ns + i for i in range(n)},
        compiler_params=pltpu.CompilerParams(has_side_effects=EFFECT),
    )(*[_hbm(a) for a in lands], after)
    return outs[:ns], outs[ns:2 * ns], outs[2 * ns:2 * ns + n], outs[-1]


def ag_forward(send_sems, recv_sems, lands, after, name):
    n = len(lands)
    n1, n2 = 4 * n, 3 * n

    def body(*refs):
        lnd = refs[:n]
        o = n
        s1, r1 = refs[o:o + n1], refs[o + n1:o + 2 * n1]
        o += 2 * n1 + 1
        s2, r2 = refs[o:o + n2], refs[o + n2:o + 2 * n2]
        token = refs[-1]
        token[...] = jnp.zeros_like(token)
        x, y, c = _place()
        me, sibling = (x, y, c), (x, y, 1 - c)
        chips = [(1 - x, y), (x, 1 - y), (1 - x, 1 - y)]
        copy1 = _ag_copy(lnd, s1, r1, 4)
        copy2 = _ag_copy(lnd, s2, r2, 3)
        for j, chip in enumerate(chips):
            for a in range(n):
                copy1(a, 1 + j, (*chip, c), me).wait_recv()
                copy2(a, j, (*chip, c), sibling).start()
        for a in range(n):
            copy1(a, 0, sibling, me).wait_recv()
            copy1(a, 0, me, sibling).wait_send()
            for j, chip in enumerate(chips):
                copy1(a, 1 + j, me, (*chip, c)).wait_send()

    outs = pl.pallas_call(
        body, name=name,
        in_specs=[HBM] * n + [SEM] * (2 * n1) + [ANY],
        out_specs=[SEM] * (2 * n2) + [HBM] * n + [pl.BlockSpec(memory_space=pltpu.VMEM)],
        out_shape=_dma_sems(2 * n2) + _hbm_like(lands) + [_sds((8, 128), F32)],
        input_output_aliases={i: 2 * n2 + i for i in range(n)},
        compiler_params=pltpu.CompilerParams(has_side_effects=EFFECT),
    )(*lands, *send_sems, *recv_sems, after)
    return outs[:n2], outs[n2:2 * n2], outs[2 * n2:2 * n2 + n], outs[-1]


def ag_finish(send_sems, recv_sems, lands, after, name):
    n = len(lands)
    n2 = 3 * n

    def body(*refs):
        lnd = refs[:n]
        s2, r2 = refs[n:n + n2], refs[n + n2:n + 2 * n2]
        x, y, c = _place()
        me, sibling = (x, y, c), (x, y, 1 - c)
        chips = [(1 - x, y), (x, 1 - y), (1 - x, 1 - y)]
        copy2 = _ag_copy(lnd, s2, r2, 3)
        for a in range(n):
            for j, chip in enumerate(chips):
                copy2(a, j, (*chip, c), sibling).wait_send()
                copy2(a, j, (*chip, 1 - c), me).wait_recv()

    outs = pl.pallas_call(
        body, name=name,
        in_specs=[HBM] * n + [SEM] * (2 * n2) + [ANY],
        out_specs=[HBM] * n, out_shape=_hbm_like(lands),
        input_output_aliases={i: i for i in range(n)},
        compiler_params=pltpu.CompilerParams(has_side_effects=EFFECT),
    )(*lands, *send_sems, *recv_sems, after)
    return list(outs)


def rs_sibling_start(arrs, name):
    n = len(arrs)
    ns = 4 * n
    lands = [lax.empty((4,) + a.shape[1:], a.dtype) for a in arrs]

    def body(*refs):
        ins, lnd = refs[:n], refs[n:2 * n]
        send_sems, recv_sems = refs[2 * n:2 * n + ns], refs[2 * n + ns:2 * n + 2 * ns]
        x, y, c = _place()
        sibling = (x, y, 1 - c)
        for a in range(n):
            for p in range(4):
                pltpu.make_async_remote_copy(
                    src_ref=ins[a].at[2 * p + 1 - c], dst_ref=lnd[a].at[p], send_sem=send_sems[4 * a + p],
                    recv_sem=recv_sems[4 * a + p], device_id=sibling, device_id_type=MESH).start()
        refs[-1][...] = jnp.zeros_like(refs[-1])

    outs = pl.pallas_call(
        body, name=name,
        in_specs=[HBM] * (2 * n), out_specs=[SEM] * (2 * ns) + [HBM] * (2 * n) + [pl.BlockSpec(memory_space=pltpu.VMEM)],
        out_shape=_dma_sems(2 * ns) + _hbm_like(arrs) + _hbm_like(lands) + [_sds((8, 128), F32)],
        input_output_aliases={i: 2 * ns + i for i in range(2 * n)},
        compiler_params=pltpu.CompilerParams(has_side_effects=EFFECT),
    )(*[_hbm(a) for a in arrs], *[_hbm(a) for a in lands])
    return (outs[:ns], outs[ns:2 * ns], outs[2 * ns:2 * ns + n], outs[2 * ns + n:2 * ns + 2 * n]), outs[-1]


def rs_sibling_wait(send_sems, recv_sems, arrs, lands, after, name):
    n = len(arrs)
    ns = 4 * n

    def body(*refs):
        ins, lnd = refs[:n], refs[n:2 * n]
        s, r = refs[2 * n:2 * n + ns], refs[2 * n + ns:2 * n + 2 * ns]
        x, y, c = _place()
        sibling = (x, y, 1 - c)
        for a in range(n):
            for p in range(4):
                cp = pltpu.make_async_remote_copy(
                    src_ref=ins[a].at[2 * p + 1 - c], dst_ref=lnd[a].at[p], send_sem=s[4 * a + p], recv_sem=r[4 * a + p],
                    device_id=sibling, device_id_type=MESH)
                cp.wait_send()
                cp.wait_recv()

    outs = pl.pallas_call(
        body, name=name,
        in_specs=[HBM] * (2 * n) + [SEM] * (2 * ns) + [ANY], out_specs=[HBM] * (2 * n),
        out_shape=_hbm_like(arrs) + _hbm_like(lands),
        input_output_aliases={i: i for i in range(2 * n)},
        compiler_params=pltpu.CompilerParams(has_side_effects=EFFECT),
    )(*arrs, *lands, *send_sems, *recv_sems, after)
    return outs[:n], outs[n:]


def rs_chips_start(parts, after, name):
    n = len(parts)
    ns = 3 * n
    lands = [lax.empty((3,) + a.shape[1:], a.dtype) for a in parts]

    def body(*refs):
        ins, lnd = refs[:n], refs[n:2 * n]
        send_sems, recv_sems = refs[2 * n + 1:2 * n + 1 + ns], refs[2 * n + 1 + ns:2 * n + 1 + 2 * ns]
        x, y, c = _place()
        chips = [(1 - x, y), (x, 1 - y), (1 - x, 1 - y)]
        for a in range(n):
            for k, (tx, ty) in enumerate(chips):
                pltpu.make_async_remote_copy(
                    src_ref=ins[a].at[2 * tx + ty], dst_ref=lnd[a].at[k], send_sem=send_sems[3 * a + k],
                    recv_sem=recv_sems[3 * a + k], device_id=(tx, ty, c), device_id_type=MESH).start()
        refs[-1][...] = jnp.zeros_like(refs[-1])

    outs = pl.pallas_call(
        body, name=name,
        in_specs=[HBM] * (2 * n) + [ANY],
        out_specs=[SEM] * (2 * ns) + [HBM] * (2 * n) + [pl.BlockSpec(memory_space=pltpu.VMEM)],
        out_shape=_dma_sems(2 * ns) + _hbm_like(parts) + _hbm_like(lands) + [_sds((8, 128), F32)],
        input_output_aliases={i: 2 * ns + i for i in range(2 * n)},
        compiler_params=pltpu.CompilerParams(has_side_effects=EFFECT),
    )(*[_hbm(a) for a in parts], *[_hbm(a) for a in lands], after)
    return (outs[:ns], outs[ns:2 * ns], outs[2 * ns:2 * ns + n], outs[2 * ns + n:2 * ns + 2 * n]), outs[-1]


def rs_chips_wait(send_sems, recv_sems, parts, lands, after, name):
    n = len(parts)
    ns = 3 * n

    def body(*refs):
        ins, lnd = refs[:n], refs[n:2 * n]
        s, r = refs[2 * n:2 * n + ns], refs[2 * n + ns:2 * n + 2 * ns]
        x, y, c = _place()
        chips = [(1 - x, y), (x, 1 - y), (1 - x, 1 - y)]
        for a in range(n):
            for k, (tx, ty) in enumerate(chips):
                cp = pltpu.make_async_remote_copy(
                    src_ref=ins[a].at[2 * tx + ty], dst_ref=lnd[a].at[k], send_sem=s[3 * a + k], recv_sem=r[3 * a + k],
                    device_id=(tx, ty, c), device_id_type=MESH)
                cp.wait_send()
                cp.wait_recv()

    outs = pl.pallas_call(
        body, name=name,
        in_specs=[HBM] * (2 * n) + [SEM] * (2 * ns) + [ANY], out_specs=[HBM] * (2 * n),
        out_shape=_hbm_like(parts) + _hbm_like(lands),
        input_output_aliases={i: i for i in range(2 * n)},
        compiler_params=pltpu.CompilerParams(has_side_effects=EFFECT),
    )(*parts, *lands, *send_sems, *recv_sems, after)
    return outs[:n], outs[n:]


def pair_sum(arrs, recv, c, name):
    n = len(arrs)

    def body(c_ref, *refs):
        del c_ref
        for a in range(n):
            refs[2 * n + a][...] = (refs[a][...].astype(F32) + refs[n + a][...].astype(F32)).astype(refs[2 * n + a].dtype)

    mine = [pl.BlockSpec((None,) + a.shape[1:], lambda p, c_ref: (2 * p + c_ref[0], 0, 0)) for a in arrs]
    other = [pl.BlockSpec((None,) + a.shape[1:], lambda p, c_ref: (p, 0, 0)) for a in arrs]
    return pl.pallas_call(
        body, name=name,
        grid_spec=pltpu.PrefetchScalarGridSpec(num_scalar_prefetch=1, grid=(4,), in_specs=mine + other, out_specs=other),
        out_shape=[_sds((4,) + a.shape[1:], a.dtype) for a in arrs], compiler_params=_params(("parallel",)),
    )(c, *arrs, *recv)


def _adamw(w, g, m, v):
    m = ADAM_B1 * m + (1.0 - ADAM_B1) * g
    v = ADAM_B2 * v + (1.0 - ADAM_B2) * jnp.square(g)
    m_hat = m / (1.0 - ADAM_B1 ** ADAM_STEP)
    v_hat = v / (1.0 - ADAM_B2 ** ADAM_STEP)
    return -ADAM_LR * (m_hat / (jnp.sqrt(v_hat) + ADAM_EPS) + ADAM_WD * w), m, v


def adamw_big(recv, sums, chip, w, m, v, tr, name):
    nl, rr, cc = w.shape
    cp = recv[0].shape[2]

    def body(chip_ref, *refs):
        del chip_ref
        rcv, own = refs[:nl], refs[nl:2 * nl]
        w_ref, m_ref, v_ref, g_out, d_out, m_out, v_out = refs[2 * nl:]
        for l in range(nl):
            g = ((own[l][...].astype(F32) + rcv[l][0].astype(F32)) + rcv[l][1].astype(F32)) + rcv[l][2].astype(F32)
            g = g[:, :cc]
            g_out[l] = g
            d_out[l], m_out[l], v_out[l] = _adamw(w_ref[l], g, m_ref[l], v_ref[l])

    blk = pl.BlockSpec((nl, tr, cc), lambda i, chip_ref: (0, i, 0))
    return pl.pallas_call(
        body, name=name,
        grid_spec=pltpu.PrefetchScalarGridSpec(
            num_scalar_prefetch=1, grid=(rr // tr,),
            in_specs=[pl.BlockSpec((3, tr, cp), lambda i, chip_ref: (0, i, 0))] * nl
            + [pl.BlockSpec((None, tr, cp), lambda i, chip_ref: (chip_ref[0], i, 0))] * nl + [blk, blk, blk],
            out_specs=[blk] * 4),
        out_shape=[_sds(w.shape, F32)] * 4, compiler_params=_params(("parallel",)),
    )(chip, *recv, *sums, w, m, v)


SMALL_ROWS = 24


def small_grads(lvec, g_ret, g_mix, g_ffn, g_final, loss_part, dwa, dwx, name):
    def body(lvec_ref, ret_ref, mix_ref, ffn_ref, fin_ref, loss_ref, dwa_ref, dwx_ref, v_ref, g_ref):
        v_ref[16:SMALL_ROWS, :] = jnp.zeros((SMALL_ROWS - 16, D_LRU), F32)
        v_ref[16:17, 0:128] = loss_ref[0:1, :]
        v_ref[0:9, :] = lvec_ref[0:9, :]
        v_ref[9:10, :] = ret_ref[...]
        for r, src in ((10, mix_ref), (12, ffn_ref), (14, fin_ref)):
            v_ref[r:r + 1, :] = src[:, :D_LRU]
            v_ref[r + 1:r + 2, :] = src[:, D_LRU:]
        for k, src in enumerate((dwa_ref, dwx_ref)):
            for g in range(LRU_BLOCKS):
                rows = slice(LRU_BD * g, LRU_BD * (g + 1))
                g_ref[D_LRU * k + LRU_BD * g:D_LRU * k + LRU_BD * (g + 1), :] = src[rows, rows]

    ins = [lvec, g_ret, g_mix, g_ffn, g_final, loss_part, dwa, dwx]
    return pl.pallas_call(
        body, name=name, grid=(1,), in_specs=[_full(a.shape) for a in ins],
        out_specs=[_full((SMALL_ROWS, D_LRU)), _full((2 * D_LRU, LRU_BD))],
        out_shape=[_sds((SMALL_ROWS, D_LRU), F32), _sds((2 * D_LRU, LRU_BD), F32)], compiler_params=_params(("arbitrary",)),
    )(*ins)


def sum_devices(arrs, name):
    n = len(arrs)

    def body(*refs):
        for a in range(n):
            acc = refs[a][0]
            for j in range(1, NDEV):
                acc = acc + refs[a][j]
            refs[n + a][...] = acc

    return pl.pallas_call(
        body, name=name, grid=(1,), in_specs=[_full(a.shape) for a in arrs], out_specs=[_full(a.shape[1:]) for a in arrs],
        out_shape=[_sds(a.shape[1:], F32) for a in arrs], compiler_params=_params(("arbitrary",)),
    )(*arrs)


def adamw_small(gs, ws, ms, vs, name):
    n = len(gs)

    def body(*refs):
        for a in range(n):
            g, w, m, v = (refs[k * n + a][...] for k in range(4))
            refs[4 * n + a][...], refs[5 * n + a][...], refs[6 * n + a][...] = _adamw(w, g, m, v)

    specs = [_full(a.shape) for a in ws]
    outs = pl.pallas_call(
        body, name=name, grid=(1,), in_specs=specs * 4, out_specs=specs * 3, out_shape=[_sds(a.shape, F32) for a in ws] * 3,
        compiler_params=_params(("arbitrary",)),
    )(*gs, *ws, *ms, *vs)
    return outs[:n], outs[n:2 * n], outs[2 * n:]


def block_diag(wa, wx, name):
    def body(wa_ref, wx_ref, oa_ref, ox_ref):
        for src, dst in ((wa_ref, oa_ref), (wx_ref, ox_ref)):
            dst[...] = jnp.zeros_like(dst)
            for g in range(LRU_BLOCKS):
                rows = slice(LRU_BD * g, LRU_BD * (g + 1))
                dst[rows, rows] = src[g].astype(dst.dtype)

    ispec = pl.BlockSpec((None, LRU_BLOCKS, LRU_BD, LRU_BD), lambda l: (l, 0, 0, 0))
    ospec = pl.BlockSpec((None, D_LRU, D_LRU), lambda l: (l, 0, 0))
    return pl.pallas_call(
        body, name=name, grid=(wa.shape[0],), in_specs=[ispec, ispec], out_specs=[ospec, ospec],
        out_shape=[_sds((wa.shape[0], D_LRU, D_LRU), MXU_DTYPE)] * 2, compiler_params=_params(("parallel",)),
    )(wa, wx)


REP_NAMES = ["norm_mix", "conv_b", "gate_a_w", "gate_a_b", "gate_x_w", "gate_x_b", "lru_lambda", "lru_out_norm",
             "ret_out_norm", "norm_ffn", "norm_final"]


def kernel(x, meta_tokens, norm_mix, w_in, conv_w, conv_b, gate_a_w, gate_a_b, gate_x_w, gate_x_b, lru_lambda, lru_out_norm, ret_out_norm, w_out, norm_ffn, w_gate, w_up, w_down, norm_final, loss_target, m_meta_tokens, m_norm_mix, m_w_in, m_conv_w, m_conv_b, m_gate_a_w, m_gate_a_b, m_gate_x_w, m_gate_x_b, m_lru_lambda, m_lru_out_norm, m_ret_out_norm, m_w_out, m_norm_ffn, m_w_gate, m_w_up, m_w_down, m_norm_final, v_meta_tokens, v_norm_mix, v_w_in, v_conv_w, v_conv_b, v_gate_a_w, v_gate_a_b, v_gate_x_w, v_gate_x_b, v_lru_lambda, v_lru_out_norm, v_ret_out_norm, v_w_out, v_norm_ffn, v_w_gate, v_w_up, v_w_down, v_norm_final):
    xi, yi, ci = _place()
    dev = 4 * xi + 2 * yi + ci
    c_arr = jnp.reshape(ci, (1,)).astype(jnp.int32)
    dev_arr = jnp.reshape(dev, (1,)).astype(jnp.int32)

    meta_g, conv_g = all_gather([meta_tokens, conv_w], c_arr, "ag_small")
    meta_full = jnp.transpose(meta_g, (1, 0, 2)).reshape(N_META, D)
    conv_full = jnp.transpose(conv_g, (1, 2, 0, 3)).reshape(DEPTH, CONV_W, D_LRU)
    tr_ = lambda a: jnp.transpose(a, (0, 2, 1))
    w_gate_t, m_w_gate_t, v_w_gate_t = tr_(w_gate), tr_(m_w_gate), tr_(v_w_gate)
    w_up_t, m_w_up_t, v_w_up_t = tr_(w_up), tr_(m_w_up), tr_(v_w_up)
    level1 = []
    token = meta_g
    for l in range(DEPTH):
        sel = jnp.stack([dev, jnp.int32(l)]).astype(jnp.int32)
        lands = to_wire(sel, w_in, w_out, w_gate_t, w_up_t, w_down, "to_wire")
        s1, r1, lands, token = ag_start(lands, token, f"ag_start_{l}")
        level1.append((s1, r1, lands))

    def as_weights(gi, go, gg, gu, gd):
        return dict(w_in=gi, w_out=go.reshape(D, D), w_gate=gg.reshape(D_FFP, D), w_up=gu.reshape(D_FFP, D),
                    w_down=gd.reshape(D_FFP, D))

    tables = _ret_tables()
    row = lambda a: a.reshape(1, -1)

    h = jnp.concatenate([jnp.zeros((PAD, D), F32), meta_full, x[0]], axis=0)
    saved, gathered = [], []
    s1, r1, lands = level1[0]
    s2, r2, first, order = ag_forward(s1[:4], r1[:4], lands[:1], token, "ag_forward_0_w_in")
    w_in_next = ag_finish(s2, r2, first, h, "ag_finish_0_w_in")[0]
    wa_dense, wx_dense = block_diag(gate_a_w, gate_x_w, "block_diag")
    for l in range(DEPTH):
        small = dict(cw=conv_full[l], cb=row(conv_b[l]), wa=wa_dense[l], ba=row(gate_a_b[l]),
                     wx=wx_dense[l], bx=row(gate_x_b[l]), lam=row(lru_lambda[l]),
                     gain=row(lru_out_norm[l]))
        s1, r1, lands = level1[l]
        hn1 = rmsnorm_fwd(h, row(norm_mix[l]), "rms_fwd")
        proj = mm_blocked_nn(hn1, w_in_next, F32, "proj")
        if l > 0:
            s2, r2, rest, order = ag_forward(s1[4:], r1[4:], lands[1:], proj, f"ag_forward_{l}_rest")
            ymix, hst, states = mix_fwd(proj, tables=tables, ret_gain=row(ret_out_norm[l]), after=order, name="mix_fwd", **small)
            w = as_weights(w_in_next, *ag_finish(s2, r2, rest, ymix, f"ag_finish_{l}_rest"))
            h_mid = mm_nn_res(ymix, w["w_out"], h, order, "out_proj")
        else:
            ymix, hst, states = mix_fwd(proj, tables=tables, ret_gain=row(ret_out_norm[l]), after=order, name="mix_fwd", **small)
            s2, r2, mid, order = ag_forward(s1[4:16], r1[4:16], lands[1:4], ymix, "ag_forward_0_mid")
            mids = ag_finish(s2, r2, mid, order, "ag_finish_0_mid")
            w = dict(w_in=w_in_next, w_out=mids[0].reshape(D, D), w_gate=mids[1].reshape(D_FFP, D), w_up=mids[2].reshape(D_FFP, D))
            h_mid = mm_nn_res(ymix, w["w_out"], h, order, "out_proj")
            s2d, r2d, down, order = ag_forward(s1[16:], r1[16:], lands[4:], h_mid, "ag_forward_0_down")
        hn2 = rmsnorm_fwd(h_mid, row(norm_ffn[l]), "rms_fwd")
        act_dgate, act_dup, act = ffn_up(hn2, w["w_gate"], w["w_up"], "ffn_up")
        if l == 0:
            w["w_down"] = ag_finish(s2d, r2d, down, act, "ag_finish_0_down")[0].reshape(D_FFP, D)
        gathered.append(w)
        if l + 1 < DEPTH:
            s1n, r1n, landsn = level1[l + 1]
            s2, r2, first, order = ag_forward(s1n[:4], r1n[:4], landsn[:1], act, f"ag_forward_{l + 1}_w_in")
        h_out = mm_nn_res(act, w["w_down"], h_mid, order, "ffn_down")
        if l + 1 < DEPTH:
            w_in_next = ag_finish(s2, r2, first, h_out, f"ag_finish_{l + 1}_w_in")[0]
        saved.append(dict(h=h, hn1=hn1, proj=proj, hst=hst, states=states, ymix=ymix, h_mid=h_mid, hn2=hn2, act_dgate=act_dgate, act_dup=act_dup,
                          act=act, small=small))
        h = h_out

    loss_p, dh, dh_b, g_norm_final = loss_head(h, row(norm_final), loss_target[0], "loss_head")

    small_v = [None] * DEPTH
    small_w = [None] * DEPTH
    inflight = []
    order = loss_p

    def sibling_done(l, tag, names, sib, after):
        parts, got = rs_sibling_wait(*sib, after, f"rs_sibling_wait_{tag}")
        sums = pair_sum(parts, got, c_arr, "pair_sum")
        flying, started = rs_chips_start(sums, after, f"rs_chips_start_{tag}")
        inflight.append((l, tag, names, flying))
        return started

    for l in reversed(range(DEPTH)):
        w, s = gathered[l], saved[l]
        dgate, dup = ffn_down_bwd(dh_b, w["w_down"], s["act_dgate"], s["act_dup"], order, "ffn_down_bwd")
        dwd = mm_tn(s["act"], dh_b, PAIR, order, "dw_down").reshape(NDEV, FF_SHP, D)
        dwg, dwu = (g.reshape(NDEV, FF_SHP, D) for g in mm_tn_two(dgate, dup, s["hn2"], PAIR, order, "dw_rows"))
        split = l <= 1
        if split:
            ffn_sib, order = rs_sibling_start([dwg, dwu, dwd], f"rs_sibling_start_{l}_ffn")
        dhn2 = mm_rows_nn([(dgate, w["w_gate"]), (dup, w["w_up"])], order, "ffn_up_bwd")
        if split:
            order = sibling_done(l, f"{l}_ffn", ("w_gate", "w_up", "w_down"), ffn_sib, dhn2)
        dh_mid, dh_mid_b, g_norm_ffn = rmsnorm_bwd(s["h_mid"], row(norm_ffn[l]), dhn2, dh, "rms_bwd")
        dymix, dwo = out_proj_bwd(dh_mid_b, w["w_out"], s["ymix"], order, "out_proj_bwd")
        dwo = dwo.reshape(NDEV, OUT_SH, D)
        dproj, lvec, dwa, dwx, g_ret_norm = mix_bwd(s["proj"], s["hst"], s["states"], dymix, tables=tables,
                                                    ret_gain=row(ret_out_norm[l]), after=order, name="mix_bwd", **s["small"])
        dwi = mm_tn_blocked(s["hn1"], dproj, "dw_blocked")
        if split:
            sib_tag, sib_names = f"{l}_mix", ("w_in", "w_out")
            sib, order = rs_sibling_start([dwi, dwo], f"rs_sibling_start_{l}_mix")
        else:
            sib_tag, sib_names = str(l), ("w_in", "w_gate", "w_up", "w_out", "w_down")
            sib, order = rs_sibling_start([dwi, dwg, dwu, dwo, dwd], f"rs_sibling_start_{l}")
        dhn1 = mm_blocked_nt([(dproj, w["w_in"])], order, "proj_bwd")
        if l > 0:
            order = sibling_done(l, sib_tag, sib_names, sib, dhn1)
        else:
            last_parts, last_got = rs_sibling_wait(*sib, dhn1, f"rs_sibling_wait_{sib_tag}")
            last_sums = pair_sum(last_parts, last_got, c_arr, "pair_sum")
        dh, dh_b, g_norm_mix = rmsnorm_bwd(s["h"], row(norm_mix[l]), dhn1, dh_mid, "rms_bwd")

        g_fin, loss_part = (g_norm_final, loss_p) if l == 0 else (jnp.zeros((1, D), F32), jnp.zeros((8, 128), F32))
        small_v[l], small_w[l] = small_grads(lvec, g_ret_norm, g_norm_mix, g_norm_ffn, g_fin, loss_part, dwa, dwx,
                                             "small_grads")
        if l == 1:
            early = place_blocks(dev_arr, [jnp.stack(small_v[1:]), jnp.stack(small_w[1:])], "place_grads")
            early_sems = ag_start(early, order, "ag_start_grads")
            order = early_sems[3]

    grad_x = dh[X0:][None]
    g_meta = dh[PAD:X0]

    late = all_gather([small_v[0], small_w[0], g_meta], last_sums[0], "ag_grads")
    flying, _ = rs_chips_start(last_sums, late[0], f"rs_chips_start_{sib_tag}")
    inflight.append((0, sib_tag, sib_names, flying))
    s2, r2, lands, _ = ag_forward(early_sems[0], early_sems[1], early_sems[2], dh, "ag_forward_grads")
    gath_early = ag_finish(s2, r2, lands, late[0], "ag_finish_grads")
    v0, w0, meta_sum, v123, w123 = sum_devices(list(late) + list(gath_early), "sum_devices")
    loss = v0[16, 0]
    vecs = jnp.concatenate([v0[None], v123])
    gws = jnp.concatenate([w0[None], w123])
    blocks = (DEPTH, LRU_BLOCKS, LRU_BD)
    small_g = dict(
        conv_w=lax.dynamic_slice_in_dim(vecs[:, 0:CONV_W], dev * (D_LRU // NDEV), D_LRU // NDEV, axis=2),
        conv_b=vecs[:, 4], gate_a_b=vecs[:, 5].reshape(blocks), gate_x_b=vecs[:, 6].reshape(blocks),
        lru_lambda=vecs[:, 7], lru_out_norm=vecs[:, 8], ret_out_norm=vecs[:, 9],
        norm_mix=vecs[:, 10:12].reshape(DEPTH, D), norm_ffn=vecs[:, 12:14].reshape(DEPTH, D),
        norm_final=v0[14:16].reshape(1, D),
        gate_a_w=gws[:, :D_LRU].reshape(blocks + (LRU_BD,)), gate_x_w=gws[:, D_LRU:].reshape(blocks + (LRU_BD,)),
        meta_tokens=lax.dynamic_slice_in_dim(meta_sum, dev * (D // NDEV), D // NDEV, axis=1))
    given = dict(norm_mix=(norm_mix, m_norm_mix, v_norm_mix), conv_b=(conv_b, m_conv_b, v_conv_b),
                 gate_a_w=(gate_a_w, m_gate_a_w, v_gate_a_w), gate_a_b=(gate_a_b, m_gate_a_b, v_gate_a_b),
                 gate_x_w=(gate_x_w, m_gate_x_w, v_gate_x_w), gate_x_b=(gate_x_b, m_gate_x_b, v_gate_x_b),
                 lru_lambda=(lru_lambda, m_lru_lambda, v_lru_lambda), lru_out_norm=(lru_out_norm, m_lru_out_norm, v_lru_out_norm),
                 ret_out_norm=(ret_out_norm, m_ret_out_norm, v_ret_out_norm), norm_ffn=(norm_ffn, m_norm_ffn, v_norm_ffn),
                 norm_final=tuple(a.reshape(1, D) for a in (norm_final, m_norm_final, v_norm_final)),
                 conv_w=(conv_w, m_conv_w, v_conv_w), meta_tokens=(meta_tokens, m_meta_tokens, v_meta_tokens))
    small_names = REP_NAMES + ["conv_w", "meta_tokens"]
    upd = adamw_small([small_g[n] for n in small_names], *[[given[n][k] for n in small_names] for k in range(3)],
                      "adamw_small")
    small_out = [dict(zip(small_names, u)) for u in upd]
    for d_ in [small_g] + small_out:
        d_["norm_final"] = d_["norm_final"].reshape(D)

    arrived = {}

    def wait_for(entries, after):
        for l, tag, names, flying in entries:
            sums, recv = rs_chips_wait(*flying, after, f"rs_chips_wait_{tag}")
            for i, n in enumerate(names):
                arrived[l, n] = (recv[i], sums[i])

    chip = jnp.reshape(2 * xi + yi, (1,)).astype(jnp.int32)

    def finish(wname, w_, m_, v_, tr):
        return adamw_big([arrived[l, wname][0] for l in range(DEPTH)], [arrived[l, wname][1] for l in range(DEPTH)], chip,
                         w_, m_, v_, tr, "adamw_" + wname)

    wait_for(inflight[:-1], upd[0][0])
    o_gate = [tr_(o) for o in finish("w_gate", w_gate_t, m_w_gate_t, v_w_gate_t, 32)]
    o_up = [tr_(o) for o in finish("w_up", w_up_t, m_w_up_t, v_w_up_t, 32)]
    o_down = finish("w_down", w_down, m_w_down, v_w_down, 32)
    wait_for(inflight[-1:], o_down[0])
    o_in = finish("w_in", w_in, m_w_in, v_w_in, 256)
    o_out = finish("w_out", w_out, m_w_out, v_w_out, 64)

    bigs = dict(w_in=o_in, w_out=o_out, w_gate=o_gate, w_up=o_up, w_down=o_down)
    order = ["meta_tokens", "norm_mix", "w_in", "conv_w", "conv_b", "gate_a_w", "gate_a_b", "gate_x_w", "gate_x_b", "lru_lambda",
             "lru_out_norm", "ret_out_norm", "w_out", "norm_ffn", "w_gate", "w_up", "w_down", "norm_final"]
    grads = [bigs[n][0] if n in bigs else small_g[n] for n in order]
    rest = [[bigs[n][k + 1] if n in bigs else small_out[k][n] for n in order] for k in range(3)]
    return (loss, grad_x, *grads, *rest[0], *rest[1], *rest[2])
```

```python
import numpy as np
import jax
import jax.numpy as jnp
from jax import lax
from jax.experimental import pallas as pl
from jax.experimental.pallas import tpu as pltpu

F32, BF16 = jnp.float32, jnp.bfloat16
MXU_DTYPE = BF16
WIRE_DTYPE = BF16

D = 1024
SEQ = 2048
DEPTH = 4
N_META = 16
CH = 128
PAD = (-(SEQ + N_META)) % CH
T = SEQ + N_META + PAD
NCH = T // CH
X0 = PAD + N_META
D_LRU = 512
LRU_BLOCKS = 8
LRU_BD = 64
CONV_W = 4
LRU_C = 8.0
D_RET = 512
HEADS = 4
HD = 128
ROPE_BASE = 10000.0
D_IN = 3072
D_FF = 2816
NDEV = 8
IN_SH = D_IN // NDEV
FF_SH = D_FF // NDEV
FF_SHP = 384
D_FFP = NDEV * FF_SHP
OUT_SH = D // NDEV
EPS = 1e-6
TM = 544
VMEM_LIMIT = 56 * 2**20
MESH = pl.DeviceIdType.MESH

ADAM_LR, ADAM_B1, ADAM_B2, ADAM_EPS, ADAM_WD, ADAM_STEP = 0.001, 0.9, 0.999, 1e-08, 0.01, 10

NN = ((1,), (0,))
NT = ((1,), (1,))
TN = ((0,), (0,))


def _dot(a, b, dims):
    return lax.dot_general(a.astype(MXU_DTYPE), b.astype(MXU_DTYPE), (dims, ((), ())), preferred_element_type=F32)


def _sds(shape, dtype):
    return jax.ShapeDtypeStruct(shape, dtype)


def _params(sem=None):
    return pltpu.CompilerParams(dimension_semantics=sem, vmem_limit_bytes=VMEM_LIMIT)


def _full(shape):
    n = len(shape)
    return pl.BlockSpec(shape, lambda *_: (0,) * n)


def rmsnorm_fwd(h, gain, name):
    def body(h_ref, g_ref, o_ref):
        x = h_ref[...]
        ms = jnp.mean(x * x, axis=-1, keepdims=True)
        o_ref[...] = (x * lax.rsqrt(ms + EPS) * g_ref[...]).astype(o_ref.dtype)

    return pl.pallas_call(
        body, name=name, grid=(T // TM,),
        in_specs=[pl.BlockSpec((TM, D), lambda i: (i, 0)), _full((1, D))],
        out_specs=pl.BlockSpec((TM, D), lambda i: (i, 0)),
        out_shape=_sds((T, D), MXU_DTYPE), compiler_params=_params(("parallel",)),
    )(h, gain)


def rmsnorm_bwd(h, gain, dhn, dres, name):
    def body(h_ref, g_ref, dhn_ref, dres_ref, dh_ref, dhb_ref, dg_ref):
        x = h_ref[...]
        rstd = lax.rsqrt(jnp.mean(x * x, axis=-1, keepdims=True) + EPS)
        xhat = x * rstd
        dy = dhn_ref[...]
        dyg = dy * g_ref[...]
        dh = dres_ref[...] + rstd * (dyg - xhat * jnp.mean(dyg * xhat, axis=-1, keepdims=True))
        dh_ref[...] = dh
        dhb_ref[...] = dh.astype(dhb_ref.dtype)

        @pl.when(pl.program_id(0) == 0)
        def _():
            dg_ref[...] = jnp.zeros_like(dg_ref)
        dg_ref[...] += jnp.sum(dy * xhat, axis=0, keepdims=True)

    row = pl.BlockSpec((TM, D), lambda i: (i, 0))
    return pl.pallas_call(
        body, name=name, grid=(T // TM,),
        in_specs=[row, _full((1, D)), row, row],
        out_specs=[row, row, _full((1, D))],
        out_shape=[_sds((T, D), F32), _sds((T, D), MXU_DTYPE), _sds((1, D), F32)], compiler_params=_params(("arbitrary",)),
    )(h, gain, dhn, dres)


def loss_head(h, gain, target, name):
    def body(h_ref, g_ref, t_ref, loss_ref, dh_ref, dhb_ref, dg_ref):
        i = pl.program_id(0)

        @pl.when(i == 0)
        def _():
            loss_ref[...] = jnp.zeros_like(loss_ref)
            dg_ref[...] = jnp.zeros_like(dg_ref)
            dh_ref[...] = jnp.zeros_like(dh_ref)
            dhb_ref[...] = jnp.zeros_like(dhb_ref)

        @pl.when(i > 0)
        def _():
            x = h_ref[...]
            g = g_ref[...]
            rstd = lax.rsqrt(jnp.mean(x * x, axis=-1, keepdims=True) + EPS)
            xhat = x * rstd
            err = xhat * g - t_ref[...]
            loss_ref[...] += 0.5 * jnp.sum(jnp.mean(err * err, axis=-1, keepdims=True), axis=0, keepdims=True)
            dy = err * (1.0 / D)
            dyg = dy * g
            dh = rstd * (dyg - xhat * jnp.mean(dyg * xhat, axis=-1, keepdims=True))
            dh_ref[...] = dh
            dhb_ref[...] = dh.astype(dhb_ref.dtype)
            dg_ref[...] += jnp.sum(dy * xhat, axis=0, keepdims=True)

    row = pl.BlockSpec((CH, D), lambda i: (i, 0))
    return pl.pallas_call(
        body, name=name, grid=(NCH,),
        in_specs=[row, _full((1, D)), pl.BlockSpec((CH, D), lambda i: (jnp.maximum(i - 1, 0), 0))],
        out_specs=[_full((8, 128)), row, row, _full((1, D))],
        out_shape=[_sds((8, 128), F32), _sds((T, D), F32), _sds((T, D), MXU_DTYPE), _sds((1, D), F32)],
        compiler_params=_params(("arbitrary",)),
    )(h, gain, target)


PAIR = 2 * IN_SH
NPAIR = NDEV // 2
BN = 256
FB = 512


def _pair_cols(w_ref):
    return jnp.concatenate([w_ref[0], w_ref[1]], axis=1)


W_PAIR = lambda k: pl.BlockSpec((2, k, IN_SH), lambda j: (j, 0, 0))
COLS_PAIR = pl.BlockSpec((T, PAIR), lambda j: (0, j))
ANYSPEC = pl.BlockSpec(memory_space=pl.ANY)


def mm_blocked_nn(a, w, out_dtype, name):
    k = a.shape[1]

    def body(a_ref, w_ref, o_ref):
        o_ref[:PAD, :] = jnp.zeros((PAD, PAIR), o_ref.dtype)
        o_ref[PAD:, :] = _dot(a_ref[PAD:, :], _pair_cols(w_ref), NN).astype(o_ref.dtype)

    return pl.pallas_call(
        body, name=name, grid=(NPAIR,),
        in_specs=[_full((T, k)), W_PAIR(k)], out_specs=COLS_PAIR,
        out_shape=_sds((T, NDEV * IN_SH), out_dtype), compiler_params=_params(("parallel",)),
    )(a, w)


def mm_nn_res(a, w, res, after, name):
    k = a.shape[1]

    def body(a_ref, w_ref, r_ref, after_ref, o_ref):
        del after_ref
        o_ref[:PAD, :] = r_ref[:PAD, :]
        o_ref[PAD:, :] = r_ref[PAD:, :] + _dot(a_ref[PAD:, :], w_ref[...], NN)

    col = pl.BlockSpec((T, BN), lambda j: (0, j))
    return pl.pallas_call(
        body, name=name, grid=(D // BN,),
        in_specs=[_full((T, k)), pl.BlockSpec((k, BN), lambda j: (0, j)), col, ANYSPEC], out_specs=col,
        out_shape=_sds((T, D), F32), compiler_params=_params(("parallel",)),
    )(a, w, res, after)


def ffn_up(hn, wg, wu, name):
    def body(a_ref, wg_ref, wu_ref, dg_ref, du_ref, act_ref):
        a = a_ref[PAD:, :]
        for ref in (dg_ref, du_ref, act_ref):
            ref[:PAD, :] = jnp.zeros((PAD, FB), ref.dtype)
        for c in range(FB // BN):
            cols = slice(BN * c, BN * (c + 1))
            g = _dot(a, wg_ref[cols, :], NT)
            u = _dot(a, wu_ref[cols, :], NT)
            sg = jax.nn.sigmoid(g)
            silu = g * sg
            dg_ref[PAD:, cols] = (u * (sg * (1.0 + g * (1.0 - sg)))).astype(dg_ref.dtype)
            du_ref[PAD:, cols] = silu.astype(du_ref.dtype)
            act_ref[PAD:, cols] = (silu * u).astype(act_ref.dtype)

    wspec = pl.BlockSpec((FB, D), lambda j: (j, 0))
    ospec = pl.BlockSpec((T, FB), lambda j: (0, j))
    return pl.pallas_call(
        body, name=name, grid=(D_FFP // FB,),
        in_specs=[_full((T, D)), wspec, wspec], out_specs=[ospec] * 3,
        out_shape=[_sds((T, D_FFP), MXU_DTYPE)] * 3, compiler_params=_params(("parallel",)),
    )(hn, wg, wu)


def ffn_down_bwd(dh, wd, dact_dgate, dact_dup, after, name):
    def body(dh_ref, wd_ref, g_ref, u_ref, after_ref, dg_ref, du_ref):
        del after_ref
        dh = dh_ref[PAD:, :]
        for ref in (dg_ref, du_ref):
            ref[:PAD, :] = jnp.zeros((PAD, FB), ref.dtype)
        for c in range(FB // BN):
            cols = slice(BN * c, BN * (c + 1))
            dact = _dot(dh, wd_ref[cols, :], NT)
            dg_ref[PAD:, cols] = (dact * g_ref[PAD:, cols].astype(F32)).astype(dg_ref.dtype)
            du_ref[PAD:, cols] = (dact * u_ref[PAD:, cols].astype(F32)).astype(du_ref.dtype)

    blk = pl.BlockSpec((T, FB), lambda j: (0, j))
    return pl.pallas_call(
        body, name=name, grid=(D_FFP // FB,),
        in_specs=[_full((T, D)), pl.BlockSpec((FB, D), lambda j: (j, 0)), blk, blk, ANYSPEC],
        out_specs=[blk, blk],
        out_shape=[_sds((T, D_FFP), MXU_DTYPE)] * 2, compiler_params=_params(("parallel",)),
    )(dh, wd, dact_dgate, dact_dup, after)


def mm_blocked_nt(pairs, after, name):
    n = len(pairs)

    def body(*refs):
        o_ref = refs[2 * n + 1]

        @pl.when(pl.program_id(0) == 0)
        def _():
            o_ref[...] = jnp.zeros_like(o_ref)
        for p in range(n):
            o_ref[PAD:, :] += _dot(refs[2 * p][PAD:, :], _pair_cols(refs[2 * p + 1]), NT)

    specs, args = [], []
    for a, w in pairs:
        specs += [COLS_PAIR, W_PAIR(D)]
        args += [a, w]
    return pl.pallas_call(
        body, name=name, grid=(NPAIR,), in_specs=specs + [ANYSPEC], out_specs=_full((T, D)),
        out_shape=_sds((T, D), F32), compiler_params=_params(("arbitrary",)),
    )(*args, after)


def mm_tn_two(a1, a2, b, bm, after, name):
    m = a1.shape[1]

    def body(a1_ref, a2_ref, b_ref, after_ref, o1_ref, o2_ref):
        del after_ref
        b = b_ref[...]
        o1_ref[...] = _dot(a1_ref[...], b, TN).astype(o1_ref.dtype)
        o2_ref[...] = _dot(a2_ref[...], b, TN).astype(o2_ref.dtype)

    blk = pl.BlockSpec((T, bm), lambda i: (0, i))
    out = pl.BlockSpec((bm, D), lambda i: (i, 0))
    return pl.pallas_call(
        body, name=name, grid=(m // bm,),
        in_specs=[blk, blk, _full((T, D)), ANYSPEC], out_specs=[out, out],
        out_shape=[_sds((m, D), WIRE_DTYPE)] * 2, compiler_params=_params(("parallel",)),
    )(a1, a2, b, after)


def out_proj_bwd(dh, w, ymix, after, name):
    def body(dh_ref, w_ref, y_ref, after_ref, dy_ref, dw_ref):
        del after_ref
        dh_ = dh_ref[PAD:, :]
        dy_ref[:PAD, :] = jnp.zeros((PAD, BN), dy_ref.dtype)
        dy_ref[PAD:, :] = _dot(dh_, w_ref[...], NT)
        dw_ref[...] = _dot(y_ref[PAD:, :], dh_, TN).astype(dw_ref.dtype)

    return pl.pallas_call(
        body, name=name, grid=(D // BN,),
        in_specs=[_full((T, D)), pl.BlockSpec((BN, D), lambda j: (j, 0)), pl.BlockSpec((T, BN), lambda j: (0, j)), ANYSPEC],
        out_specs=[pl.BlockSpec((T, BN), lambda j: (0, j)), pl.BlockSpec((BN, D), lambda j: (j, 0))],
        out_shape=[_sds((T, D), F32), _sds((D, D), WIRE_DTYPE)], compiler_params=_params(("parallel",)),
    )(dh, w, ymix, after)


def mm_rows_nn(pairs, after, name):
    n = len(pairs)

    def body(*refs):
        o_ref = refs[2 * n + 1]

        @pl.when(pl.program_id(0) == 0)
        def _():
            o_ref[...] = jnp.zeros_like(o_ref)
        for p in range(n):
            o_ref[PAD:, :] += _dot(refs[2 * p][PAD:, :], refs[2 * p + 1][...], NN)

    specs, args = [], []
    for a, w in pairs:
        specs += [pl.BlockSpec((T, FB), lambda j: (0, j)), pl.BlockSpec((FB, D), lambda j: (j, 0))]
        args += [a, w]
    return pl.pallas_call(
        body, name=name, grid=(D_FFP // FB,), in_specs=specs + [ANYSPEC], out_specs=_full((T, D)),
        out_shape=_sds((T, D), F32), compiler_params=_params(("arbitrary",)),
    )(*args, after)


def mm_tn_blocked(a, b, name):
    def body(a_ref, b_ref, o_ref):
        o = _dot(a_ref[...], b_ref[...], TN).astype(o_ref.dtype)
        o_ref[0] = o[:, :IN_SH]
        o_ref[1] = o[:, IN_SH:]

    return pl.pallas_call(
        body, name=name, grid=(NPAIR,),
        in_specs=[_full((T, D)), COLS_PAIR], out_specs=W_PAIR(D),
        out_shape=_sds((NDEV, D, IN_SH), WIRE_DTYPE), compiler_params=_params(("parallel",)),
    )(a, b)


def mm_tn(a, b, bm, after, name):
    m = a.shape[1]

    def body(a_ref, b_ref, after_ref, o_ref):
        del after_ref
        o_ref[...] = _dot(a_ref[...], b_ref[...], TN).astype(o_ref.dtype)

    return pl.pallas_call(
        body, name=name, grid=(m // bm,),
        in_specs=[pl.BlockSpec((T, bm), lambda i: (0, i)), _full((T, D)), ANYSPEC],
        out_specs=pl.BlockSpec((bm, D), lambda i: (i, 0)),
        out_shape=_sds((m, D), WIRE_DTYPE), compiler_params=_params(("parallel",)),
    )(a, b, after)


def _softplus_neg(lam):
    return jnp.maximum(-lam, 0.0) + jnp.log1p(jnp.exp(-jnp.abs(lam)))


def _lru_gates(pa, px, xc, lam):
    r = jax.nn.sigmoid(pa)
    ig = jax.nn.sigmoid(px)
    sp = _softplus_neg(lam)
    log_a = -LRU_C * r * sp
    a = jnp.exp(log_a)
    mult = jnp.sqrt(-jnp.tanh(log_a) * (a * a + 1.0))
    return a, mult * (ig * xc), (r, ig, sp, mult)


def _lru_gates_vjp(da, db, xc, lam, a, r, ig, sp, mult):
    dmult = db * (ig * xc)
    du = db * mult
    dlog_a = da * a - dmult * (a * a) / mult
    dr = dlog_a * (-LRU_C * sp)
    dlam = jnp.sum(dlog_a * (-LRU_C * r), axis=0, keepdims=True) * (-jax.nn.sigmoid(-lam))
    dpa = dr * (r * (1.0 - r))
    dpx = (du * xc) * (ig * (1.0 - ig))
    return dpa, dpx, du * ig, dlam


def _lru_out(h, g, gain):
    z = h * jax.nn.gelu(g)
    return z * lax.rsqrt(jnp.mean(z * z, axis=-1, keepdims=True) + EPS) * gain


def _conv_taps(x, xprev, row):
    taps = [x]
    for s in range(1, CONV_W):
        taps.append(jnp.where(row < s, pltpu.roll(xprev, s, 0), pltpu.roll(x, s, 0)))
    return taps


def _conv(taps, cw_ref, cb):
    xc = cb + cw_ref[CONV_W - 1:CONV_W, :] * taps[0]
    for s in range(1, CONV_W):
        xc = xc + cw_ref[CONV_W - 1 - s:CONV_W - s, :] * taps[s]
    return xc


def _lru_fwd_block(i, x_ref, g_ref, cw_ref, cb_ref, wa_ref, ba_ref, wx_ref, bx_ref, lam_ref, gain_ref, y_ref, h_ref,
                   xprev_scr, a_scr, b_scr, carry_scr):
    @pl.when(i == 0)
    def _():
        xprev_scr[...] = jnp.zeros_like(xprev_scr)
        carry_scr[...] = jnp.zeros_like(carry_scr)

    x = x_ref[...]
    row = lax.broadcasted_iota(jnp.int32, (CH, D_LRU), 0)
    xc = _conv(_conv_taps(x, xprev_scr[...], row), cw_ref, cb_ref[...])
    pa = _dot(xc, wa_ref[...], NN) + ba_ref[...]
    px = _dot(xc, wx_ref[...], NN) + bx_ref[...]
    a, b, _ = _lru_gates(pa, px, xc, lam_ref[...])
    a_scr[...] = a
    b_scr[...] = jnp.where(i * CH + row >= PAD, b, 0.0)
    h = carry_scr[...]
    for t in range(CH):
        h = a_scr[t:t + 1, :] * h + b_scr[t:t + 1, :]
        h_ref[t:t + 1, :] = h
    carry_scr[...] = h
    xprev_scr[...] = x
    y_ref[:, :D_LRU] = _lru_out(h_ref[...], g_ref[...], gain_ref[...]).astype(y_ref.dtype)


LRU_VEC_ROWS = 16


def _lru_bwd_block(ib, x_ref, xp_ref, g_ref, h_ref, hp_ref, dy_ref, cw_ref, cb_ref, wa_ref, ba_ref, wx_ref, bx_ref, lam_ref,
                   gain_ref, dp_ref, vec_ref, dwa_ref, dwx_ref, a_scr, dh_scr, g_scr, carry_scr, dxcn_scr):
    @pl.when(ib == NCH - 1)
    def _():
        carry_scr[...] = jnp.zeros_like(carry_scr)
        dxcn_scr[...] = jnp.zeros_like(dxcn_scr)
        vec_ref[...] = jnp.zeros_like(vec_ref)
        dwa_ref[...] = jnp.zeros_like(dwa_ref)
        dwx_ref[...] = jnp.zeros_like(dwx_ref)

    x = x_ref[...]
    row = lax.broadcasted_iota(jnp.int32, (CH, D_LRU), 0)
    valid = ib * CH + row >= PAD
    taps = _conv_taps(x, xp_ref[...], row)
    xc = _conv(taps, cw_ref, cb_ref[...])
    pa = _dot(xc, wa_ref[...], NN) + ba_ref[...]
    px = _dot(xc, wx_ref[...], NN) + bx_ref[...]
    a, _, gate_parts = _lru_gates(pa, px, xc, lam_ref[...])
    h = h_ref[...]
    _, vjp_out = jax.vjp(_lru_out, h, g_ref[...], gain_ref[...])
    dh, dg, dgain = vjp_out(dy_ref[:, :D_LRU].astype(F32))
    a_scr[...] = a
    dh_scr[...] = dh
    c = carry_scr[...]
    for t in range(CH - 1, -1, -1):
        gt = dh_scr[t:t + 1, :] + c
        g_scr[t:t + 1, :] = gt
        c = a_scr[t:t + 1, :] * gt
    carry_scr[...] = c
    gg = g_scr[...]
    hprev = jnp.where(row < 1, pltpu.roll(hp_ref[...], 1, 0), pltpu.roll(h, 1, 0))
    da = jnp.where(valid, gg * hprev, 0.0)
    db = jnp.where(valid, gg, 0.0)
    dpa, dpx, dxc, dlam = _lru_gates_vjp(da, db, xc, lam_ref[...], a, *gate_parts)
    dxc = dxc + _dot(dpa, wa_ref[...], NT) + _dot(dpx, wx_ref[...], NT)
    dwa_ref[...] += _dot(xc, dpa, TN)
    dwx_ref[...] += _dot(xc, dpx, TN)
    for s in range(CONV_W):
        vec_ref[CONV_W - 1 - s:CONV_W - s, :] += jnp.sum(dxc * taps[s], axis=0, keepdims=True)
    vec_ref[4:5, :] += jnp.sum(dxc, axis=0, keepdims=True)
    vec_ref[5:6, :] += jnp.sum(dpa, axis=0, keepdims=True)
    vec_ref[6:7, :] += jnp.sum(dpx, axis=0, keepdims=True)
    vec_ref[7:8, :] += dlam
    vec_ref[8:9, :] += dgain
    dxn = dxcn_scr[...]
    dx = cw_ref[CONV_W - 1:CONV_W, :] * dxc
    for s in range(1, CONV_W):
        ahead = jnp.where(row >= CH - s, pltpu.roll(dxn, CH - s, 0), pltpu.roll(dxc, CH - s, 0))
        dx = dx + cw_ref[CONV_W - 1 - s:CONV_W - s, :] * ahead
    dxcn_scr[...] = dxc
    dp_ref[:, :D_LRU] = jnp.where(valid, dx, 0.0).astype(dp_ref.dtype)
    dp_ref[:, D_LRU:2 * D_LRU] = dg.astype(dp_ref.dtype)


def _ret_tables():
    half = HD // 2
    pos = jnp.arange(T, dtype=F32) - float(PAD)
    inv = ROPE_BASE ** (-jnp.arange(half, dtype=F32) / half)
    ang = pos[:, None] * inv[None, :]
    cos = jnp.concatenate([jnp.cos(ang), jnp.cos(ang)], axis=-1)
    sin = jnp.concatenate([-jnp.sin(ang), jnp.sin(ang)], axis=-1)
    log_g = jnp.log(1.0 - 2.0 ** (-5.0 - jnp.arange(HEADS, dtype=F32)))
    idx = jnp.arange(CH, dtype=F32)
    diff = idx[:, None] - idx[None, :]
    dmask = jnp.where(diff[None] >= 0, jnp.exp(jnp.maximum(diff, 0.0)[None] * log_g[:, None, None]), 0.0)
    xi = jnp.exp((idx + 1.0)[None, :] * log_g[:, None])
    zeta = jnp.exp((CH - 1.0 - idx)[None, :] * log_g[:, None])
    xi = jnp.broadcast_to(xi[:, :, None], (HEADS, CH, HD))
    zeta = jnp.broadcast_to(zeta[:, :, None], (HEADS, CH, HD))
    return cos, sin, dmask, xi, zeta


def _chunk_decay():
    log_g = np.log(np.float32(1.0) - np.float32(2.0) ** (np.float32(-5.0) - np.arange(HEADS, dtype=np.float32)))
    return [float(v) for v in np.exp(np.float32(CH) * log_g.astype(np.float32))]


def _rope(x, cos, sin):
    return x * cos + pltpu.roll(x, HD // 2, 1) * sin


def mix_fwd(proj, cw, cb, wa, ba, wx, bx, lam, gain, tables, ret_gain, after, name):
    cos, sin, dmask, xi, zeta = tables
    gch = _chunk_decay()
    scale = HD ** -0.5

    def body(x_ref, gl_ref, cw_ref, cb_ref, wa_ref, ba_ref, wx_ref, bx_ref, lam_ref, lgain_ref,
             q_ref, k_ref, v_ref, g_ref, cos_ref, sin_ref, dm_ref, xi_ref, zt_ref, gain_ref, after_ref,
             y_ref, h_ref, st_ref, xprev_scr, a_scr, b_scr, carry_scr, s_scr):
        del after_ref

        @pl.when(pl.program_id(0) == 0)
        def _():
            s_scr[...] = jnp.zeros_like(s_scr)

        _lru_fwd_block(pl.program_id(0), x_ref, gl_ref, cw_ref, cb_ref, wa_ref, ba_ref, wx_ref, bx_ref, lam_ref, lgain_ref,
                       y_ref, h_ref, xprev_scr, a_scr, b_scr, carry_scr)
        cs, sn = cos_ref[...], sin_ref[...]
        hs = range(HEADS)
        sl = [slice(HD * h, HD * (h + 1)) for h in hs]
        qr = [_rope(q_ref[:, sl[h]], cs, sn).astype(MXU_DTYPE) for h in hs]
        kf = [_rope(k_ref[:, sl[h]], cs, sn) * scale for h in hs]
        kr = [kf[h].astype(MXU_DTYPE) for h in hs]
        v = [v_ref[:, sl[h]].astype(MXU_DTYPE) for h in hs]
        s = [s_scr[h] for h in hs]
        for h in hs:
            st_ref[h] = s[h]
        sc = [_dot(qr[h], kr[h], NT) * dm_ref[h] for h in hs]
        cross = [_dot(qr[h], s[h], NN) * xi_ref[h] for h in hs]
        for h in hs:
            s_scr[h] = s[h] * gch[h] + _dot(kf[h] * zt_ref[h], v[h], TN)
        y = [_dot(sc[h], v[h], NN) + cross[h] for h in hs]
        yc = [y[h] - jnp.mean(y[h], axis=-1, keepdims=True) for h in hs]
        yn = [yc[h] * lax.rsqrt(jnp.mean(yc[h] * yc[h], axis=-1, keepdims=True) + EPS) for h in hs]
        for h in hs:
            so = slice(D_LRU + HD * h, D_LRU + HD * (h + 1))
            y_ref[:, so] = (jax.nn.silu(g_ref[:, sl[h]]) * (yn[h] * gain_ref[:, sl[h]])).astype(y_ref.dtype)

    def col(c):
        return pl.BlockSpec((CH, D_RET), lambda n: (n, c))

    tab = pl.BlockSpec((CH, HD), lambda n: (n, 0))
    cst = _full((HEADS, CH, HD))
    vec = _full((1, D_LRU))
    mat = _full((D_LRU, D_LRU))
    blockbuf = pltpu.VMEM((CH, D_LRU), F32)
    return pl.pallas_call(
        body, name=name, grid=(NCH,),
        in_specs=[col(0), col(1), _full((CONV_W, D_LRU)), vec, mat, vec, mat, vec, vec, vec,
                  col(2), col(3), col(4), col(5), tab, tab, cst, cst, cst, _full((1, D_RET)),
                  pl.BlockSpec(memory_space=pl.ANY)],
        out_specs=[pl.BlockSpec((CH, D), lambda n: (n, 0)), col(0), pl.BlockSpec((None, HEADS, HD, HD), lambda n: (n, 0, 0, 0))],
        out_shape=[_sds((T, D), MXU_DTYPE), _sds((T, D_LRU), F32), _sds((NCH, HEADS, HD, HD), F32)],
        scratch_shapes=[blockbuf, blockbuf, blockbuf, pltpu.VMEM((1, D_LRU), F32), pltpu.VMEM((HEADS, HD, HD), F32)],
        compiler_params=_params(("arbitrary",)),
    )(proj, proj, cw, cb, wa, ba, wx, bx, lam, gain, proj, proj, proj, proj, cos, sin, dmask, xi, zeta, ret_gain, after)


def mix_bwd(proj, hst, states, dymix, cw, cb, wa, ba, wx, bx, lam, gain, tables, ret_gain, after, name):
    cos, sin, dmask, xi, zeta = tables
    gch = _chunk_decay()
    scale = HD ** -0.5
    last = NCH - 1

    def body(x_ref, xp_ref, gl_ref, h_ref, hp_ref, cw_ref, cb_ref, wa_ref, ba_ref, wx_ref, bx_ref, lam_ref, lgain_ref,
             q_ref, k_ref, v_ref, g_ref, st_ref, dy_ref, cos_ref, sin_ref, dm_ref, xi_ref, zt_ref, gain_ref, after_ref,
             dp_ref, vec_ref, dwa_ref, dwx_ref, dgain_ref, a_scr, dh_scr, g_scr, carry_scr, dxcn_scr, ds_scr):
        del after_ref

        @pl.when(pl.program_id(0) == 0)
        def _():
            ds_scr[...] = jnp.zeros_like(ds_scr)
            dgain_ref[...] = jnp.zeros_like(dgain_ref)

        _lru_bwd_block(last - pl.program_id(0), x_ref, xp_ref, gl_ref, h_ref, hp_ref, dy_ref, cw_ref, cb_ref, wa_ref, ba_ref,
                       wx_ref, bx_ref, lam_ref, lgain_ref, dp_ref, vec_ref, dwa_ref, dwx_ref, a_scr, dh_scr, g_scr, carry_scr,
                       dxcn_scr)
        cs, sn = cos_ref[...], sin_ref[...]
        hs = range(HEADS)
        sl = [slice(HD * h, HD * (h + 1)) for h in hs]

        def out(j, h):
            return slice(2 * D_LRU + j * D_RET + HD * h, 2 * D_LRU + j * D_RET + HD * (h + 1))

        b16 = lambda xs: [x.astype(MXU_DTYPE) for x in xs]
        qr = b16([_rope(q_ref[:, sl[h]], cs, sn) for h in hs])
        kf = [_rope(k_ref[:, sl[h]], cs, sn) * scale for h in hs]
        kr = b16(kf)
        kz = b16([kf[h] * zt_ref[h] for h in hs])
        v = b16([v_ref[:, sl[h]] for h in hs])
        s = b16([st_ref[h] for h in hs])
        ds = [ds_scr[h] for h in hs]
        dsb = b16(ds)
        sc = [_dot(qr[h], kr[h], NT) * dm_ref[h] for h in hs]
        scb = b16(sc)
        y = [_dot(scb[h], v[h], NN) + _dot(qr[h], s[h], NN) * xi_ref[h] for h in hs]
        yc = [y[h] - jnp.mean(y[h], axis=-1, keepdims=True) for h in hs]
        rstd = [lax.rsqrt(jnp.mean(yc[h] * yc[h], axis=-1, keepdims=True) + EPS) for h in hs]
        yn = [yc[h] * rstd[h] for h in hs]
        dy = []
        for h in hs:
            g = g_ref[:, sl[h]]
            gain = gain_ref[:, sl[h]]
            sg = jax.nn.sigmoid(g)
            silu = g * sg
            dout = dy_ref[:, D_LRU + HD * h:D_LRU + HD * (h + 1)].astype(F32)
            dgain_ref[:, sl[h]] += jnp.sum(dout * silu * yn[h], axis=0, keepdims=True)
            dp_ref[:, out(3, h)] = (dout * yn[h] * gain * (sg * (1.0 + g * (1.0 - sg)))).astype(dp_ref.dtype)
            dyn = dout * silu * gain
            dy.append(rstd[h] * (dyn - jnp.mean(dyn, axis=-1, keepdims=True)
                                 - yn[h] * jnp.mean(dyn * yn[h], axis=-1, keepdims=True)))
        dyb = b16(dy)
        dqs = b16([dy[h] * xi_ref[h] for h in hs])
        dp = b16([_dot(dyb[h], v[h], NT) * dm_ref[h] for h in hs])
        dv = [_dot(scb[h], dyb[h], TN) + _dot(kz[h], dsb[h], NN) for h in hs]
        dqr = [_dot(dp[h], kr[h], NN) + _dot(dqs[h], s[h], NT) for h in hs]
        dkr = [_dot(dp[h], qr[h], TN) + _dot(v[h], dsb[h], NT) * zt_ref[h] for h in hs]
        for h in hs:
            ds_scr[h] = gch[h] * ds[h] + _dot(qr[h], dqs[h], TN)
        for h in hs:
            dp_ref[:, out(0, h)] = (dqr[h] * cs + pltpu.roll(dqr[h] * sn, HD // 2, 1)).astype(dp_ref.dtype)
            dp_ref[:, out(1, h)] = ((dkr[h] * cs + pltpu.roll(dkr[h] * sn, HD // 2, 1)) * scale).astype(dp_ref.dtype)
            dp_ref[:, out(2, h)] = dv[h].astype(dp_ref.dtype)

    def col(c, shift=0):
        return pl.BlockSpec((CH, D_RET), lambda n: (jnp.maximum(last - n - shift, 0), c))

    tab = pl.BlockSpec((CH, HD), lambda n: (last - n, 0))
    cst = _full((HEADS, CH, HD))
    vec = _full((1, D_LRU))
    mat = _full((D_LRU, D_LRU))
    blockbuf = pltpu.VMEM((CH, D_LRU), F32)
    return pl.pallas_call(
        body, name=name, grid=(NCH,),
        in_specs=[col(0), col(0, 1), col(1), col(0), col(0, 1), _full((CONV_W, D_LRU)), vec, mat, vec, mat, vec, vec, vec,
                  col(2), col(3), col(4), col(5), pl.BlockSpec((None, HEADS, HD, HD), lambda n: (last - n, 0, 0, 0)),
                  pl.BlockSpec((CH, D), lambda n: (last - n, 0)), tab, tab, cst, cst, cst, _full((1, D_RET)),
                  pl.BlockSpec(memory_space=pl.ANY)],
        out_specs=[pl.BlockSpec((CH, D_IN), lambda n: (last - n, 0)), _full((LRU_VEC_ROWS, D_LRU)), mat, mat,
                   _full((1, D_RET))],
        out_shape=[_sds((T, D_IN), MXU_DTYPE), _sds((LRU_VEC_ROWS, D_LRU), F32), _sds((D_LRU, D_LRU), F32),
                   _sds((D_LRU, D_LRU), F32), _sds((1, D_RET), F32)],
        scratch_shapes=[blockbuf, blockbuf, blockbuf, pltpu.VMEM((1, D_LRU), F32), blockbuf,
                        pltpu.VMEM((HEADS, HD, HD), F32)],
        compiler_params=_params(("arbitrary",)),
    )(proj, proj, proj, hst, hst, cw, cb, wa, ba, wx, bx, lam, gain, proj, proj, proj, proj, states, dymix,
      cos, sin, dmask, xi, zeta, ret_gain, after)


HBM = pl.BlockSpec(memory_space=pltpu.HBM)


def _place():
    return lax.axis_index("x"), lax.axis_index("y"), lax.axis_index("c")


def all_gather(arrs, after, name):
    n = len(arrs)

    def body(*refs):
        ins, outs = refs[:n], refs[n + 1:2 * n + 1]
        send_sems, recv_sems, local_sems = refs[2 * n + 1:]
        x, y, c = _place()
        me, sibling = (x, y, c), (x, y, 1 - c)
        chips = [(1 - x, y), (x, 1 - y), (1 - x, 1 - y)]

        def copy(a, k, block, to, src=None):
            px, py, pc = block
            dst = outs[a].at[4 * px + 2 * py + pc]
            return pltpu.make_async_remote_copy(
                src_ref=dst if src is None else src, dst_ref=dst, send_sem=send_sems.at[a, k], recv_sem=recv_sems.at[a, k],
                device_id=to, device_id_type=MESH)

        mine = [pltpu.make_async_copy(ins[a], outs[a].at[4 * x + 2 * y + c], local_sems.at[a]) for a in range(n)]
        for cp in mine:
            cp.start()
        first = []
        for a in range(n):
            first.append(copy(a, 0, me, sibling, src=ins[a]))
            first += [copy(a, 1 + j, me, (*chip, c), src=ins[a]) for j, chip in enumerate(chips)]
        for cp in first:
            cp.start()
        passed = []
        for j, chip in enumerate(chips):
            for a in range(n):
                copy(a, 1 + j, (*chip, c), me).wait_recv()
                passed.append(copy(a, 4 + j, (*chip, c), sibling))
                passed[-1].start()
        for a in range(n):
            copy(a, 0, sibling, me).wait_recv()
            for j, chip in enumerate(chips):
                copy(a, 4 + j, (*chip, 1 - c), me).wait_recv()
        for cp in first + passed:
            cp.wait_send()
        for cp in mine:
            cp.wait()

    return pl.pallas_call(
        body, name=name,
        in_specs=[HBM] * n + [pl.BlockSpec(memory_space=pl.ANY)], out_specs=[HBM] * n,
        out_shape=[_sds((NDEV,) + a.shape, a.dtype) for a in arrs],
        scratch_shapes=[pltpu.SemaphoreType.DMA((n, 7)), pltpu.SemaphoreType.DMA((n, 7)), pltpu.SemaphoreType.DMA((n,))],
    )(*arrs, after)


SEM = pl.BlockSpec(memory_space=pltpu.SEMAPHORE)
ANY = pl.BlockSpec(memory_space=pl.ANY)
EFFECT = pltpu.SideEffectType.DATAFLOW_SIDE_EFFECTING


def _hbm(a):
    return pltpu.with_memory_space_constraint(a, pltpu.HBM)


def _hbm_like(arrs):
    return [pltpu.HBM(a.shape, a.dtype) for a in arrs]


def _dma_sems(count):
    return [pltpu.SemaphoreType.DMA(())] * count


def _ag_copy(lands, send_sems, recv_sems, per):
    def copy(a, k, block, to, src=None):
        px, py, pc = block
        dst = lands[a].at[4 * px + 2 * py + pc]
        return pltpu.make_async_remote_copy(
            src_ref=dst if src is None else src, dst_ref=dst, send_sem=send_sems[a * per + k], recv_sem=recv_sems[a * per + k],
            device_id=to, device_id_type=MESH)
    return copy


def to_wire(sel, w_in, w_out, w_gate, w_up, w_down, name):
    ffpad = FF_SHP - FF_SH

    def body(sel_ref, i_ref, o_ref, g_ref, u_ref, d_ref, oi, oo, og, ou, od):
        del sel_ref
        oi[...] = i_ref[...].astype(oi.dtype)
        oo[...] = o_ref[...].astype(oo.dtype)
        for src, dst in ((g_ref, og), (u_ref, ou), (d_ref, od)):
            dst[:FF_SH, :] = src[...].astype(dst.dtype)
            dst[FF_SH:, :] = jnp.zeros((ffpad, D), dst.dtype)

    shapes_in = [(D, IN_SH), (OUT_SH, D), (FF_SH, D), (FF_SH, D), (FF_SH, D)]
    shapes_out = [(D, IN_SH), (OUT_SH, D), (FF_SHP, D), (FF_SHP, D), (FF_SHP, D)]
    return pl.pallas_call(
        body, name=name,
        grid_spec=pltpu.PrefetchScalarGridSpec(
            num_scalar_prefetch=1, grid=(1,),
            in_specs=[pl.BlockSpec((None,) + s, lambda i, sel_ref: (sel_ref[1], 0, 0)) for s in shapes_in],
            out_specs=[pl.BlockSpec((None,) + s, lambda i, sel_ref: (sel_ref[0], 0, 0)) for s in shapes_out]),
        out_shape=[_sds((NDEV,) + s, WIRE_DTYPE) for s in shapes_out], compiler_params=_params(("arbitrary",)),
    )(sel, w_in, w_out, w_gate, w_up, w_down)


def place_blocks(sel, arrs, name):
    n = len(arrs)

    def body(sel_ref, *refs):
        del sel_ref
        for a in range(n):
            refs[n + a][...] = refs[a][...]

    def whole(a):
        nd = a.ndim
        return pl.BlockSpec(a.shape, lambda i, sel_ref: (0,) * nd)

    def mine(a):
        nd = a.ndim
        return pl.BlockSpec((None,) + a.shape, lambda i, sel_ref: (sel_ref[0],) + (0,) * nd)

    return pl.pallas_call(
        body, name=name,
        grid_spec=pltpu.PrefetchScalarGridSpec(
            num_scalar_prefetch=1, grid=(1,), in_specs=[whole(a) for a in arrs], out_specs=[mine(a) for a in arrs]),
        out_shape=[_sds((NDEV,) + a.shape, a.dtype) for a in arrs], compiler_params=_params(("arbitrary",)),
    )(sel, *arrs)


def ag_start(lands, after, name):
    n = len(lands)
    ns = 4 * n

    def body(*refs):
        lnd = refs[:n]
        send_sems, recv_sems = refs[n + 1:n + 1 + ns], refs[n + 1 + ns:n + 1 + 2 * ns]
        token = refs[-1]
        x, y, c = _place()
        me, sibling = (x, y, c), (x, y, 1 - c)
        chips = [(1 - x, y), (x, 1 - y), (1 - x, 1 - y)]
        copy = _ag_copy(lnd, send_sems, recv_sems, 4)
        for a in range(n):
            copy(a, 0, me, sibling).start()
            for j, chip in enumerate(chips):
                copy(a, 1 + j, me, (*chip, c)).start()
        token[...] = jnp.zeros_like(token)

    outs = pl.pallas_call(
        body, name=name,
        in_specs=[HBM] * n + [ANY],
        out_specs=[SEM] * (2 * ns) + [HBM] * n + [pl.BlockSpec(memory_space=pltpu.VMEM)],
        out_shape=_dma_sems(2 * ns) + _hbm_like(lands) + [_sds((8, 128), F32)],
        input_output_aliases={i: 2 * ns + i for i in range(n)},
        compiler_params=pltpu.CompilerParams(has_side_effects=EFFECT),
    )(*[_hbm(a) for a in lands], after)
    return outs[:ns], outs[ns:2 * ns], outs[2 * ns:2 * ns + n], outs[-1]


def ag_forward(send_sems, recv_sems, lands, after, name):
    n = len(lands)
    n1, n2 = 4 * n, 3 * n

    def body(*refs):
        lnd = refs[:n]
        o = n
        s1, r1 = refs[o:o + n1], refs[o + n1:o + 2 * n1]
        o += 2 * n1 + 1
        s2, r2 = refs[o:o + n2], refs[o + n2:o + 2 * n2]
        token = refs[-1]
        token[...] = jnp.zeros_like(token)
        x, y, c = _place()
        me, sibling = (x, y, c), (x, y, 1 - c)
        chips = [(1 - x, y), (x, 1 - y), (1 - x, 1 - y)]
        copy1 = _ag_copy(lnd, s1, r1, 4)
        copy2 = _ag_copy(lnd, s2, r2, 3)
        for j, chip in enumerate(chips):
            for a in range(n):
                copy1(a, 1 + j, (*chip, c), me).wait_recv()
                copy2(a, j, (*chip, c), sibling).start()
        for a in range(n):
            copy1(a, 0, sibling, me).wait_recv()
            copy1(a, 0, me, sibling).wait_send()
            for j, chip in enumerate(chips):
                copy1(a, 1 + j, me, (*chip, c)).wait_send()

    outs = pl.pallas_call(
        body, name=name,
        in_specs=[HBM] * n + [SEM] * (2 * n1) + [ANY],
        out_specs=[SEM] * (2 * n2) + [HBM] * n + [pl.BlockSpec(memory_space=pltpu.VMEM)],
        out_shape=_dma_sems(2 * n2) + _hbm_like(lands) + [_sds((8, 128), F32)],
        input_output_aliases={i: 2 * n2 + i for i in range(n)},
        compiler_params=pltpu.CompilerParams(has_side_effects=EFFECT),
    )(*lands, *send_sems, *recv_sems, after)
    return outs[:n2], outs[n2:2 * n2], outs[2 * n2:2 * n2 + n], outs[-1]


def ag_finish(send_sems, recv_sems, lands, after, name):
    n = len(lands)
    n2 = 3 * n

    def body(*refs):
        lnd = refs[:n]
        s2, r2 = refs[n:n + n2], refs[n + n2:n + 2 * n2]
        x, y, c = _place()
        me, sibling = (x, y, c), (x, y, 1 - c)
        chips = [(1 - x, y), (x, 1 - y), (1 - x, 1 - y)]
        copy2 = _ag_copy(lnd, s2, r2, 3)
        for a in range(n):
            for j, chip in enumerate(chips):
                copy2(a, j, (*chip, c), sibling).wait_send()
                copy2(a, j, (*chip, 1 - c), me).wait_recv()

    outs = pl.pallas_call(
        body, name=name,
        in_specs=[HBM] * n + [SEM] * (2 * n2) + [ANY],
        out_specs=[HBM] * n, out_shape=_hbm_like(lands),
        input_output_aliases={i: i for i in range(n)},
        compiler_params=pltpu.CompilerParams(has_side_effects=EFFECT),
    )(*lands, *send_sems, *recv_sems, after)
    return list(outs)


def rs_sibling_start(arrs, name):
    n = len(arrs)
    ns = 4 * n
    lands = [lax.empty((4,) + a.shape[1:], a.dtype) for a in arrs]

    def body(*refs):
        ins, lnd = refs[:n], refs[n:2 * n]
        send_sems, recv_sems = refs[2 * n:2 * n + ns], refs[2 * n + ns:2 * n + 2 * ns]
        x, y, c = _place()
        sibling = (x, y, 1 - c)
        for a in range(n):
            for p in range(4):
                pltpu.make_async_remote_copy(
                    src_ref=ins[a].at[2 * p + 1 - c], dst_ref=lnd[a].at[p], send_sem=send_sems[4 * a + p],
                    recv_sem=recv_sems[4 * a + p], device_id=sibling, device_id_type=MESH).start()
        refs[-1][...] = jnp.zeros_like(refs[-1])

    outs = pl.pallas_call(
        body, name=name,
        in_specs=[HBM] * (2 * n), out_specs=[SEM] * (2 * ns) + [HBM] * (2 * n) + [pl.BlockSpec(memory_space=pltpu.VMEM)],
        out_shape=_dma_sems(2 * ns) + _hbm_like(arrs) + _hbm_like(lands) + [_sds((8, 128), F32)],
        input_output_aliases={i: 2 * ns + i for i in range(2 * n)},
        compiler_params=pltpu.CompilerParams(has_side_effects=EFFECT),
    )(*[_hbm(a) for a in arrs], *[_hbm(a) for a in lands])
    return (outs[:ns], outs[ns:2 * ns], outs[2 * ns:2 * ns + n], outs[2 * ns + n:2 * ns + 2 * n]), outs[-1]


def rs_sibling_wait(send_sems, recv_sems, arrs, lands, after, name):
    n = len(arrs)
    ns = 4 * n

    def body(*refs):
        ins, lnd = refs[:n], refs[n:2 * n]
        s, r = refs[2 * n:2 * n + ns], refs[2 * n + ns:2 * n + 2 * ns]
        x, y, c = _place()
        sibling = (x, y, 1 - c)
        for a in range(n):
            for p in range(4):
                cp = pltpu.make_async_remote_copy(
                    src_ref=ins[a].at[2 * p + 1 - c], dst_ref=lnd[a].at[p], send_sem=s[4 * a + p], recv_sem=r[4 * a + p],
                    device_id=sibling, device_id_type=MESH)
                cp.wait_send()
                cp.wait_recv()

    outs = pl.pallas_call(
        body, name=name,
        in_specs=[HBM] * (2 * n) + [SEM] * (2 * ns) + [ANY], out_specs=[HBM] * (2 * n),
        out_shape=_hbm_like(arrs) + _hbm_like(lands),
        input_output_aliases={i: i for i in range(2 * n)},
        compiler_params=pltpu.CompilerParams(has_side_effects=EFFECT),
    )(*arrs, *lands, *send_sems, *recv_sems, after)
    return outs[:n], outs[n:]


def rs_chips_start(parts, name):
    n = len(parts)
    ns = 3 * n
    lands = [lax.empty((3,) + a.shape[1:], a.dtype) for a in parts]

    def body(*refs):
        ins, lnd = refs[:n], refs[n:2 * n]
        send_sems, recv_sems = refs[2 * n:2 * n + ns], refs[2 * n + ns:2 * n + 2 * ns]
        x, y, c = _place()
        chips = [(1 - x, y), (x, 1 - y), (1 - x, 1 - y)]
        for a in range(n):
            for k, (tx, ty) in enumerate(chips):
                pltpu.make_async_remote_copy(
                    src_ref=ins[a].at[2 * tx + ty], dst_ref=lnd[a].at[k], send_sem=send_sems[3 * a + k],
                    recv_sem=recv_sems[3 * a + k], device_id=(tx, ty, c), device_id_type=MESH).start()
        refs[-1][...] = jnp.zeros_like(refs[-1])

    outs = pl.pallas_call(
        body, name=name,
        in_specs=[HBM] * (2 * n), out_specs=[SEM] * (2 * ns) + [HBM] * (2 * n) + [pl.BlockSpec(memory_space=pltpu.VMEM)],
        out_shape=_dma_sems(2 * ns) + _hbm_like(parts) + _hbm_like(lands) + [_sds((8, 128), F32)],
        input_output_aliases={i: 2 * ns + i for i in range(2 * n)},
        compiler_params=pltpu.CompilerParams(has_side_effects=EFFECT),
    )(*[_hbm(a) for a in parts], *[_hbm(a) for a in lands])
    return (outs[:ns], outs[ns:2 * ns], outs[2 * ns:2 * ns + n], outs[2 * ns + n:2 * ns + 2 * n]), outs[-1]


def rs_chips_wait(send_sems, recv_sems, parts, lands, after, name):
    n = len(parts)
    ns = 3 * n

    def body(*refs):
        ins, lnd = refs[:n], refs[n:2 * n]
        s, r = refs[2 * n:2 * n + ns], refs[2 * n + ns:2 * n + 2 * ns]
        x, y, c = _place()
        chips = [(1 - x, y), (x, 1 - y), (1 - x, 1 - y)]
        for a in range(n):
            for k, (tx, ty) in enumerate(chips):
                cp = pltpu.make_async_remote_copy(
                    src_ref=ins[a].at[2 * tx + ty], dst_ref=lnd[a].at[k], send_sem=s[3 * a + k], recv_sem=r[3 * a + k],
                    device_id=(tx, ty, c), device_id_type=MESH)
                cp.wait_send()
                cp.wait_recv()

    outs = pl.pallas_call(
        body, name=name,
        in_specs=[HBM] * (2 * n) + [SEM] * (2 * ns) + [ANY], out_specs=[HBM] * (2 * n),
        out_shape=_hbm_like(parts) + _hbm_like(lands),
        input_output_aliases={i: i for i in range(2 * n)},
        compiler_params=pltpu.CompilerParams(has_side_effects=EFFECT),
    )(*parts, *lands, *send_sems, *recv_sems, after)
    return outs[:n], outs[n:]


def pair_sum(arrs, recv, c, name):
    n = len(arrs)

    def body(c_ref, *refs):
        del c_ref
        for a in range(n):
            refs[2 * n + a][...] = (refs[a][...].astype(F32) + refs[n + a][...].astype(F32)).astype(refs[2 * n + a].dtype)

    mine = [pl.BlockSpec((None,) + a.shape[1:], lambda p, c_ref: (2 * p + c_ref[0], 0, 0)) for a in arrs]
    other = [pl.BlockSpec((None,) + a.shape[1:], lambda p, c_ref: (p, 0, 0)) for a in arrs]
    return pl.pallas_call(
        body, name=name,
        grid_spec=pltpu.PrefetchScalarGridSpec(num_scalar_prefetch=1, grid=(4,), in_specs=mine + other, out_specs=other),
        out_shape=[_sds((4,) + a.shape[1:], a.dtype) for a in arrs], compiler_params=_params(("parallel",)),
    )(c, *arrs, *recv)


def _adamw(w, g, m, v):
    m = ADAM_B1 * m + (1.0 - ADAM_B1) * g
    v = ADAM_B2 * v + (1.0 - ADAM_B2) * jnp.square(g)
    m_hat = m / (1.0 - ADAM_B1 ** ADAM_STEP)
    v_hat = v / (1.0 - ADAM_B2 ** ADAM_STEP)
    return -ADAM_LR * (m_hat / (jnp.sqrt(v_hat) + ADAM_EPS) + ADAM_WD * w), m, v


def adamw_big(recv, sums, chip, w, m, v, tr, name):
    nl, rr, cc = w.shape
    cp = recv[0].shape[2]

    def body(chip_ref, *refs):
        del chip_ref
        rcv, own = refs[:nl], refs[nl:2 * nl]
        w_ref, m_ref, v_ref, g_out, d_out, m_out, v_out = refs[2 * nl:]
        for l in range(nl):
            g = ((own[l][...].astype(F32) + rcv[l][0].astype(F32)) + rcv[l][1].astype(F32)) + rcv[l][2].astype(F32)
            g = g[:, :cc]
            g_out[l] = g
            d_out[l], m_out[l], v_out[l] = _adamw(w_ref[l], g, m_ref[l], v_ref[l])

    blk = pl.BlockSpec((nl, tr, cc), lambda i, chip_ref: (0, i, 0))
    return pl.pallas_call(
        body, name=name,
        grid_spec=pltpu.PrefetchScalarGridSpec(
            num_scalar_prefetch=1, grid=(rr // tr,),
            in_specs=[pl.BlockSpec((3, tr, cp), lambda i, chip_ref: (0, i, 0))] * nl
            + [pl.BlockSpec((None, tr, cp), lambda i, chip_ref: (chip_ref[0], i, 0))] * nl + [blk, blk, blk],
            out_specs=[blk] * 4),
        out_shape=[_sds(w.shape, F32)] * 4, compiler_params=_params(("parallel",)),
    )(chip, *recv, *sums, w, m, v)


SMALL_ROWS = 24


def small_grads(lvec, g_ret, g_mix, g_ffn, g_final, loss_part, dwa, dwx, name):
    def body(lvec_ref, ret_ref, mix_ref, ffn_ref, fin_ref, loss_ref, dwa_ref, dwx_ref, v_ref, g_ref):
        v_ref[16:SMALL_ROWS, :] = jnp.zeros((SMALL_ROWS - 16, D_LRU), F32)
        v_ref[16:17, 0:128] = loss_ref[0:1, :]
        v_ref[0:9, :] = lvec_ref[0:9, :]
        v_ref[9:10, :] = ret_ref[...]
        for r, src in ((10, mix_ref), (12, ffn_ref), (14, fin_ref)):
            v_ref[r:r + 1, :] = src[:, :D_LRU]
            v_ref[r + 1:r + 2, :] = src[:, D_LRU:]
        for k, src in enumerate((dwa_ref, dwx_ref)):
            for g in range(LRU_BLOCKS):
                rows = slice(LRU_BD * g, LRU_BD * (g + 1))
                g_ref[D_LRU * k + LRU_BD * g:D_LRU * k + LRU_BD * (g + 1), :] = src[rows, rows]

    ins = [lvec, g_ret, g_mix, g_ffn, g_final, loss_part, dwa, dwx]
    return pl.pallas_call(
        body, name=name, grid=(1,), in_specs=[_full(a.shape) for a in ins],
        out_specs=[_full((SMALL_ROWS, D_LRU)), _full((2 * D_LRU, LRU_BD))],
        out_shape=[_sds((SMALL_ROWS, D_LRU), F32), _sds((2 * D_LRU, LRU_BD), F32)], compiler_params=_params(("arbitrary",)),
    )(*ins)


def sum_devices(arrs, name):
    n = len(arrs)

    def body(*refs):
        for a in range(n):
            acc = refs[a][0]
            for j in range(1, NDEV):
                acc = acc + refs[a][j]
            refs[n + a][...] = acc

    return pl.pallas_call(
        body, name=name, grid=(1,), in_specs=[_full(a.shape) for a in arrs], out_specs=[_full(a.shape[1:]) for a in arrs],
        out_shape=[_sds(a.shape[1:], F32) for a in arrs], compiler_params=_params(("arbitrary",)),
    )(*arrs)


def adamw_small(gs, ws, ms, vs, name):
    n = len(gs)

    def body(*refs):
        for a in range(n):
            g, w, m, v = (refs[k * n + a][...] for k in range(4))
            refs[4 * n + a][...], refs[5 * n + a][...], refs[6 * n + a][...] = _adamw(w, g, m, v)

    specs = [_full(a.shape) for a in ws]
    outs = pl.pallas_call(
        body, name=name, grid=(1,), in_specs=specs * 4, out_specs=specs * 3, out_shape=[_sds(a.shape, F32) for a in ws] * 3,
        compiler_params=_params(("arbitrary",)),
    )(*gs, *ws, *ms, *vs)
    return outs[:n], outs[n:2 * n], outs[2 * n:]


def block_diag(wa, wx, name):
    def body(wa_ref, wx_ref, oa_ref, ox_ref):
        for src, dst in ((wa_ref, oa_ref), (wx_ref, ox_ref)):
            dst[...] = jnp.zeros_like(dst)
            for g in range(LRU_BLOCKS):
                rows = slice(LRU_BD * g, LRU_BD * (g + 1))
                dst[rows, rows] = src[g].astype(dst.dtype)

    ispec = pl.BlockSpec((None, LRU_BLOCKS, LRU_BD, LRU_BD), lambda l: (l, 0, 0, 0))
    ospec = pl.BlockSpec((None, D_LRU, D_LRU), lambda l: (l, 0, 0))
    return pl.pallas_call(
        body, name=name, grid=(wa.shape[0],), in_specs=[ispec, ispec], out_specs=[ospec, ospec],
        out_shape=[_sds((wa.shape[0], D_LRU, D_LRU), MXU_DTYPE)] * 2, compiler_params=_params(("parallel",)),
    )(wa, wx)


REP_NAMES = ["norm_mix", "conv_b", "gate_a_w", "gate_a_b", "gate_x_w", "gate_x_b", "lru_lambda", "lru_out_norm",
             "ret_out_norm", "norm_ffn", "norm_final"]


def kernel(x, meta_tokens, norm_mix, w_in, conv_w, conv_b, gate_a_w, gate_a_b, gate_x_w, gate_x_b, lru_lambda, lru_out_norm, ret_out_norm, w_out, norm_ffn, w_gate, w_up, w_down, norm_final, loss_target, m_meta_tokens, m_norm_mix, m_w_in, m_conv_w, m_conv_b, m_gate_a_w, m_gate_a_b, m_gate_x_w, m_gate_x_b, m_lru_lambda, m_lru_out_norm, m_ret_out_norm, m_w_out, m_norm_ffn, m_w_gate, m_w_up, m_w_down, m_norm_final, v_meta_tokens, v_norm_mix, v_w_in, v_conv_w, v_conv_b, v_gate_a_w, v_gate_a_b, v_gate_x_w, v_gate_x_b, v_lru_lambda, v_lru_out_norm, v_ret_out_norm, v_w_out, v_norm_ffn, v_w_gate, v_w_up, v_w_down, v_norm_final):
    xi, yi, ci = _place()
    dev = 4 * xi + 2 * yi + ci
    c_arr = jnp.reshape(ci, (1,)).astype(jnp.int32)
    dev_arr = jnp.reshape(dev, (1,)).astype(jnp.int32)

    meta_g, conv_g = all_gather([meta_tokens, conv_w], c_arr, "ag_small")
    meta_full = jnp.transpose(meta_g, (1, 0, 2)).reshape(N_META, D)
    conv_full = jnp.transpose(conv_g, (1, 2, 0, 3)).reshape(DEPTH, CONV_W, D_LRU)
    tr_ = lambda a: jnp.transpose(a, (0, 2, 1))
    w_gate_t, m_w_gate_t, v_w_gate_t = tr_(w_gate), tr_(m_w_gate), tr_(v_w_gate)
    w_up_t, m_w_up_t, v_w_up_t = tr_(w_up), tr_(m_w_up), tr_(v_w_up)
    level1 = []
    token = meta_g
    for l in range(DEPTH):
        sel = jnp.stack([dev, jnp.int32(l)]).astype(jnp.int32)
        lands = to_wire(sel, w_in, w_out, w_gate_t, w_up_t, w_down, "to_wire")
        s1, r1, lands, token = ag_start(lands, token, f"ag_start_{l}")
        level1.append((s1, r1, lands))

    def as_weights(gi, go, gg, gu, gd):
        return dict(w_in=gi, w_out=go.reshape(D, D), w_gate=gg.reshape(D_FFP, D), w_up=gu.reshape(D_FFP, D),
                    w_down=gd.reshape(D_FFP, D))

    tables = _ret_tables()
    row = lambda a: a.reshape(1, -1)

    h = jnp.concatenate([jnp.zeros((PAD, D), F32), meta_full, x[0]], axis=0)
    saved, gathered = [], []
    s1, r1, lands = level1[0]
    s2, r2, first, order = ag_forward(s1[:4], r1[:4], lands[:1], token, "ag_forward_0_w_in")
    w_in_next = ag_finish(s2, r2, first, h, "ag_finish_0_w_in")[0]
    wa_dense, wx_dense = block_diag(gate_a_w, gate_x_w, "block_diag")
    for l in range(DEPTH):
        small = dict(cw=conv_full[l], cb=row(conv_b[l]), wa=wa_dense[l], ba=row(gate_a_b[l]),
                     wx=wx_dense[l], bx=row(gate_x_b[l]), lam=row(lru_lambda[l]),
                     gain=row(lru_out_norm[l]))
        s1, r1, lands = level1[l]
        hn1 = rmsnorm_fwd(h, row(norm_mix[l]), "rms_fwd")
        proj = mm_blocked_nn(hn1, w_in_next, F32, "proj")
        if l > 0:
            s2, r2, rest, order = ag_forward(s1[4:], r1[4:], lands[1:], proj, f"ag_forward_{l}_rest")
            ymix, hst, states = mix_fwd(proj, tables=tables, ret_gain=row(ret_out_norm[l]), after=order, name="mix_fwd", **small)
            w = as_weights(w_in_next, *ag_finish(s2, r2, rest, ymix, f"ag_finish_{l}_rest"))
            h_mid = mm_nn_res(ymix, w["w_out"], h, order, "out_proj")
        else:
            ymix, hst, states = mix_fwd(proj, tables=tables, ret_gain=row(ret_out_norm[l]), after=order, name="mix_fwd", **small)
            s2, r2, mid, order = ag_forward(s1[4:16], r1[4:16], lands[1:4], ymix, "ag_forward_0_mid")
            mids = ag_finish(s2, r2, mid, order, "ag_finish_0_mid")
            w = dict(w_in=w_in_next, w_out=mids[0].reshape(D, D), w_gate=mids[1].reshape(D_FFP, D), w_up=mids[2].reshape(D_FFP, D))
            h_mid = mm_nn_res(ymix, w["w_out"], h, order, "out_proj")
            s2d, r2d, down, order = ag_forward(s1[16:], r1[16:], lands[4:], h_mid, "ag_forward_0_down")
        hn2 = rmsnorm_fwd(h_mid, row(norm_ffn[l]), "rms_fwd")
        act_dgate, act_dup, act = ffn_up(hn2, w["w_gate"], w["w_up"], "ffn_up")
        if l == 0:
            w["w_down"] = ag_finish(s2d, r2d, down, act, "ag_finish_0_down")[0].reshape(D_FFP, D)
        gathered.append(w)
        if l + 1 < DEPTH:
            s1n, r1n, landsn = level1[l + 1]
            s2, r2, first, order = ag_forward(s1n[:4], r1n[:4], landsn[:1], act, f"ag_forward_{l + 1}_w_in")
        h_out = mm_nn_res(act, w["w_down"], h_mid, order, "ffn_down")
        if l + 1 < DEPTH:
            w_in_next = ag_finish(s2, r2, first, h_out, f"ag_finish_{l + 1}_w_in")[0]
        saved.append(dict(h=h, hn1=hn1, proj=proj, hst=hst, states=states, ymix=ymix, h_mid=h_mid, hn2=hn2, act_dgate=act_dgate, act_dup=act_dup,
                          act=act, small=small))
        h = h_out

    loss_p, dh, dh_b, g_norm_final = loss_head(h, row(norm_final), loss_target[0], "loss_head")

    small_v = [None] * DEPTH
    small_w = [None] * DEPTH
    inflight = []
    order = loss_p

    def sibling_done(l, tag, names, sib, after):
        parts, got = rs_sibling_wait(*sib, after, f"rs_sibling_wait_{tag}")
        sums = pair_sum(parts, got, c_arr, "pair_sum")
        flying, started = rs_chips_start(sums, f"rs_chips_start_{tag}")
        inflight.append((l, tag, names, flying))
        return started

    for l in reversed(range(DEPTH)):
        w, s = gathered[l], saved[l]
        dgate, dup = ffn_down_bwd(dh_b, w["w_down"], s["act_dgate"], s["act_dup"], order, "ffn_down_bwd")
        dwd = mm_tn(s["act"], dh_b, PAIR, order, "dw_down").reshape(NDEV, FF_SHP, D)
        dwg, dwu = (g.reshape(NDEV, FF_SHP, D) for g in mm_tn_two(dgate, dup, s["hn2"], PAIR, order, "dw_rows"))
        split = l <= 1
        if split:
            ffn_sib, order = rs_sibling_start([dwg, dwu, dwd], f"rs_sibling_start_{l}_ffn")
        dhn2 = mm_rows_nn([(dgate, w["w_gate"]), (dup, w["w_up"])], order, "ffn_up_bwd")
        if split:
            order = sibling_done(l, f"{l}_ffn", ("w_gate", "w_up", "w_down"), ffn_sib, dhn2)
        dh_mid, dh_mid_b, g_norm_ffn = rmsnorm_bwd(s["h_mid"], row(norm_ffn[l]), dhn2, dh, "rms_bwd")
        dymix, dwo = out_proj_bwd(dh_mid_b, w["w_out"], s["ymix"], order, "out_proj_bwd")
        dwo = dwo.reshape(NDEV, OUT_SH, D)
        dproj, lvec, dwa, dwx, g_ret_norm = mix_bwd(s["proj"], s["hst"], s["states"], dymix, tables=tables,
                                                    ret_gain=row(ret_out_norm[l]), after=order, name="mix_bwd", **s["small"])
        dwi = mm_tn_blocked(s["hn1"], dproj, "dw_blocked")
        if split:
            sib_tag, sib_names = f"{l}_mix", ("w_in", "w_out")
            sib, order = rs_sibling_start([dwi, dwo], f"rs_sibling_start_{l}_mix")
        else:
            sib_tag, sib_names = str(l), ("w_in", "w_gate", "w_up", "w_out", "w_down")
            sib, order = rs_sibling_start([dwi, dwg, dwu, dwo, dwd], f"rs_sibling_start_{l}")
        dhn1 = mm_blocked_nt([(dproj, w["w_in"])], order, "proj_bwd")
        order = sibling_done(l, sib_tag, sib_names, sib, dhn1)
        dh, dh_b, g_norm_mix = rmsnorm_bwd(s["h"], row(norm_mix[l]), dhn1, dh_mid, "rms_bwd")

        g_fin, loss_part = (g_norm_final, loss_p) if l == 0 else (jnp.zeros((1, D), F32), jnp.zeros((8, 128), F32))
        small_v[l], small_w[l] = small_grads(lvec, g_ret_norm, g_norm_mix, g_norm_ffn, g_fin, loss_part, dwa, dwx,
                                             "small_grads")
        if l == 1:
            early = place_blocks(dev_arr, [jnp.stack(small_v[1:]), jnp.stack(small_w[1:])], "place_grads")
            early_sems = ag_start(early, order, "ag_start_grads")
            order = early_sems[3]

    grad_x = dh[X0:][None]
    g_meta = dh[PAD:X0]

    late = all_gather([small_v[0], small_w[0], g_meta], order, "ag_grads")
    s2, r2, lands, _ = ag_forward(early_sems[0], early_sems[1], early_sems[2], dh, "ag_forward_grads")
    gath_early = ag_finish(s2, r2, lands, late[0], "ag_finish_grads")
    v0, w0, meta_sum, v123, w123 = sum_devices(list(late) + list(gath_early), "sum_devices")
    loss = v0[16, 0]
    vecs = jnp.concatenate([v0[None], v123])
    gws = jnp.concatenate([w0[None], w123])
    blocks = (DEPTH, LRU_BLOCKS, LRU_BD)
    small_g = dict(
        conv_w=lax.dynamic_slice_in_dim(vecs[:, 0:CONV_W], dev * (D_LRU // NDEV), D_LRU // NDEV, axis=2),
        conv_b=vecs[:, 4], gate_a_b=vecs[:, 5].reshape(blocks), gate_x_b=vecs[:, 6].reshape(blocks),
        lru_lambda=vecs[:, 7], lru_out_norm=vecs[:, 8], ret_out_norm=vecs[:, 9],
        norm_mix=vecs[:, 10:12].reshape(DEPTH, D), norm_ffn=vecs[:, 12:14].reshape(DEPTH, D),
        norm_final=v0[14:16].reshape(1, D),
        gate_a_w=gws[:, :D_LRU].reshape(blocks + (LRU_BD,)), gate_x_w=gws[:, D_LRU:].reshape(blocks + (LRU_BD,)),
        meta_tokens=lax.dynamic_slice_in_dim(meta_sum, dev * (D // NDEV), D // NDEV, axis=1))
    given = dict(norm_mix=(norm_mix, m_norm_mix, v_norm_mix), conv_b=(conv_b, m_conv_b, v_conv_b),
                 gate_a_w=(gate_a_w, m_gate_a_w, v_gate_a_w), gate_a_b=(gate_a_b, m_gate_a_b, v_gate_a_b),
                 gate_x_w=(gate_x_w, m_gate_x_w, v_gate_x_w), gate_x_b=(gate_x_b, m_gate_x_b, v_gate_x_b),
                 lru_lambda=(lru_lambda, m_lru_lambda, v_lru_lambda), lru_out_norm=(lru_out_norm, m_lru_out_norm, v_lru_out_norm),
                 ret_out_norm=(ret_out_norm, m_ret_out_norm, v_ret_out_norm), norm_ffn=(norm_ffn, m_norm_ffn, v_norm_ffn),
                 norm_final=tuple(a.reshape(1, D) for a in (norm_final, m_norm_final, v_norm_final)),
                 conv_w=(conv_w, m_conv_w, v_conv_w), meta_tokens=(meta_tokens, m_meta_tokens, v_meta_tokens))
    small_names = REP_NAMES + ["conv_w", "meta_tokens"]
    upd = adamw_small([small_g[n] for n in small_names], *[[given[n][k] for n in small_names] for k in range(3)],
                      "adamw_small")
    small_out = [dict(zip(small_names, u)) for u in upd]
    for d_ in [small_g] + small_out:
        d_["norm_final"] = d_["norm_final"].reshape(D)

    arrived = {}

    def wait_for(entries, after):
        for l, tag, names, flying in entries:
            sums, recv = rs_chips_wait(*flying, after, f"rs_chips_wait_{tag}")
            for i, n in enumerate(names):
                arrived[l, n] = (recv[i], sums[i])

    chip = jnp.reshape(2 * xi + yi, (1,)).astype(jnp.int32)

    def finish(wname, w_, m_, v_, tr):
        return adamw_big([arrived[l, wname][0] for l in range(DEPTH)], [arrived[l, wname][1] for l in range(DEPTH)], chip,
                         w_, m_, v_, tr, "adamw_" + wname)

    wait_for(inflight[:-1], upd[0][0])
    o_gate = [tr_(o) for o in finish("w_gate", w_gate_t, m_w_gate_t, v_w_gate_t, 32)]
    o_up = [tr_(o) for o in finish("w_up", w_up_t, m_w_up_t, v_w_up_t, 32)]
    o_down = finish("w_down", w_down, m_w_down, v_w_down, 32)
    wait_for(inflight[-1:], o_down[0])
    o_in = finish("w_in", w_in, m_w_in, v_w_in, 256)
    o_out = finish("w_out", w_out, m_w_out, v_w_out, 64)

    bigs = dict(w_in=o_in, w_out=o_out, w_gate=o_gate, w_up=o_up, w_down=o_down)
    order = ["meta_tokens", "norm_mix", "w_in", "conv_w", "conv_b", "gate_a_w", "gate_a_b", "gate_x_w", "gate_x_b", "lru_lambda",
             "lru_out_norm", "ret_out_norm", "w_out", "norm_ffn", "w_gate", "w_up", "w_down", "norm_final"]
    grads = [bigs[n][0] if n in bigs else small_g[n] for n in order]
    rest = [[bigs[n][k + 1] if n in bigs else small_out[k][n] for n in order] for k in range(3)]
    return (loss, grad_x, *grads, *rest[0], *rest[1], *rest[2])
```

```python
import numpy as np
import jax
import jax.numpy as jnp
from jax import lax
from jax.experimental import pallas as pl
from jax.experimental.pallas import tpu as pltpu

F32, BF16 = jnp.float32, jnp.bfloat16
MXU_DTYPE = BF16
WIRE_DTYPE = BF16

D = 1024
SEQ = 2048
DEPTH = 4
N_META = 16
CH = 128
PAD = (-(SEQ + N_META)) % CH
T = SEQ + N_META + PAD
NCH = T // CH
X0 = PAD + N_META
D_LRU = 512
LRU_BLOCKS = 8
LRU_BD = 64
CONV_W = 4
LRU_C = 8.0
D_RET = 512
HEADS = 4
HD = 128
ROPE_BASE = 10000.0
D_IN = 3072
D_FF = 2816
NDEV = 8
IN_SH = D_IN // NDEV
FF_SH = D_FF // NDEV
FF_SHP = 384
D_FFP = NDEV * FF_SHP
OUT_SH = D // NDEV
EPS = 1e-6
TM = 544
VMEM_LIMIT = 56 * 2**20
MESH = pl.DeviceIdType.MESH

ADAM_LR, ADAM_B1, ADAM_B2, ADAM_EPS, ADAM_WD, ADAM_STEP = 0.001, 0.9, 0.999, 1e-08, 0.01, 10

NN = ((1,), (0,))
NT = ((1,), (1,))
TN = ((0,), (0,))


def _dot(a, b, dims):
    return lax.dot_general(a.astype(MXU_DTYPE), b.astype(MXU_DTYPE), (dims, ((), ())), preferred_element_type=F32)


def _sds(shape, dtype):
    return jax.ShapeDtypeStruct(shape, dtype)


def _params(sem=None):
    return pltpu.CompilerParams(dimension_semantics=sem, vmem_limit_bytes=VMEM_LIMIT)


def _full(shape):
    n = len(shape)
    return pl.BlockSpec(shape, lambda *_: (0,) * n)


def rmsnorm_fwd(h, gain, name):
    def body(h_ref, g_ref, o_ref):
        x = h_ref[...]
        ms = jnp.mean(x * x, axis=-1, keepdims=True)
        o_ref[...] = (x * lax.rsqrt(ms + EPS) * g_ref[...]).astype(o_ref.dtype)

    return pl.pallas_call(
        body, name=name, grid=(T // TM,),
        in_specs=[pl.BlockSpec((TM, D), lambda i: (i, 0)), _full((1, D))],
        out_specs=pl.BlockSpec((TM, D), lambda i: (i, 0)),
        out_shape=_sds((T, D), MXU_DTYPE), compiler_params=_params(("parallel",)),
    )(h, gain)


def rmsnorm_bwd(h, gain, dhn, dres, name):
    def body(h_ref, g_ref, dhn_ref, dres_ref, dh_ref, dhb_ref, dg_ref):
        x = h_ref[...]
        rstd = lax.rsqrt(jnp.mean(x * x, axis=-1, keepdims=True) + EPS)
        xhat = x * rstd
        dy = dhn_ref[...]
        dyg = dy * g_ref[...]
        dh = dres_ref[...] + rstd * (dyg - xhat * jnp.mean(dyg * xhat, axis=-1, keepdims=True))
        dh_ref[...] = dh
        dhb_ref[...] = dh.astype(dhb_ref.dtype)

        @pl.when(pl.program_id(0) == 0)
        def _():
            dg_ref[...] = jnp.zeros_like(dg_ref)
        dg_ref[...] += jnp.sum(dy * xhat, axis=0, keepdims=True)

    row = pl.BlockSpec((TM, D), lambda i: (i, 0))
    return pl.pallas_call(
        body, name=name, grid=(T // TM,),
        in_specs=[row, _full((1, D)), row, row],
        out_specs=[row, row, _full((1, D))],
        out_shape=[_sds((T, D), F32), _sds((T, D), MXU_DTYPE), _sds((1, D), F32)], compiler_params=_params(("arbitrary",)),
    )(h, gain, dhn, dres)


def loss_head(h, gain, target, name):
    def body(h_ref, g_ref, t_ref, loss_ref, dh_ref, dhb_ref, dg_ref):
        i = pl.program_id(0)

        @pl.when(i == 0)
        def _():
            loss_ref[...] = jnp.zeros_like(loss_ref)
            dg_ref[...] = jnp.zeros_like(dg_ref)
            dh_ref[...] = jnp.zeros_like(dh_ref)
            dhb_ref[...] = jnp.zeros_like(dhb_ref)

        @pl.when(i > 0)
        def _():
            x = h_ref[...]
            g = g_ref[...]
            rstd = lax.rsqrt(jnp.mean(x * x, axis=-1, keepdims=True) + EPS)
            xhat = x * rstd
            err = xhat * g - t_ref[...]
            loss_ref[...] += 0.5 * jnp.sum(jnp.mean(err * err, axis=-1, keepdims=True), axis=0, keepdims=True)
            dy = err * (1.0 / D)
            dyg = dy * g
            dh = rstd * (dyg - xhat * jnp.mean(dyg * xhat, axis=-1, keepdims=True))
            dh_ref[...] = dh
            dhb_ref[...] = dh.astype(dhb_ref.dtype)
            dg_ref[...] += jnp.sum(dy * xhat, axis=0, keepdims=True)

    row = pl.BlockSpec((CH, D), lambda i: (i, 0))
    return pl.pallas_call(
        body, name=name, grid=(NCH,),
        in_specs=[row, _full((1, D)), pl.BlockSpec((CH, D), lambda i: (jnp.maximum(i - 1, 0), 0))],
        out_specs=[_full((8, 128)), row, row, _full((1, D))],
        out_shape=[_sds((8, 128), F32), _sds((T, D), F32), _sds((T, D), MXU_DTYPE), _sds((1, D), F32)],
        compiler_params=_params(("arbitrary",)),
    )(h, gain, target)


PAIR = 2 * IN_SH
NPAIR = NDEV // 2
BN = 256
FB = 512


def _pair_cols(w_ref):
    return jnp.concatenate([w_ref[0], w_ref[1]], axis=1)


W_PAIR = lambda k: pl.BlockSpec((2, k, IN_SH), lambda j: (j, 0, 0))
COLS_PAIR = pl.BlockSpec((T, PAIR), lambda j: (0, j))
ANYSPEC = pl.BlockSpec(memory_space=pl.ANY)


def mm_blocked_nn(a, w, out_dtype, name):
    k = a.shape[1]

    def body(a_ref, w_ref, o_ref):
        o_ref[:PAD, :] = jnp.zeros((PAD, PAIR), o_ref.dtype)
        o_ref[PAD:, :] = _dot(a_ref[PAD:, :], _pair_cols(w_ref), NN).astype(o_ref.dtype)

    return pl.pallas_call(
        body, name=name, grid=(NPAIR,),
        in_specs=[_full((T, k)), W_PAIR(k)], out_specs=COLS_PAIR,
        out_shape=_sds((T, NDEV * IN_SH), out_dtype), compiler_params=_params(("parallel",)),
    )(a, w)


def mm_nn_res(a, w, res, after, name):
    k = a.shape[1]

    def body(a_ref, w_ref, r_ref, after_ref, o_ref):
        del after_ref
        o_ref[:PAD, :] = r_ref[:PAD, :]
        o_ref[PAD:, :] = r_ref[PAD:, :] + _dot(a_ref[PAD:, :], w_ref[...], NN)

    col = pl.BlockSpec((T, BN), lambda j: (0, j))
    return pl.pallas_call(
        body, name=name, grid=(D // BN,),
        in_specs=[_full((T, k)), pl.BlockSpec((k, BN), lambda j: (0, j)), col, ANYSPEC], out_specs=col,
        out_shape=_sds((T, D), F32), compiler_params=_params(("parallel",)),
    )(a, w, res, after)


def ffn_up(hn, wg, wu, name):
    def body(a_ref, wg_ref, wu_ref, dg_ref, du_ref, act_ref):
        a = a_ref[PAD:, :]
        for ref in (dg_ref, du_ref, act_ref):
            ref[:PAD, :] = jnp.zeros((PAD, FB), ref.dtype)
        for c in range(FB // BN):
            cols = slice(BN * c, BN * (c + 1))
            g = _dot(a, wg_ref[cols, :], NT)
            u = _dot(a, wu_ref[cols, :], NT)
            sg = jax.nn.sigmoid(g)
            silu = g * sg
            dg_ref[PAD:, cols] = (u * (sg * (1.0 + g * (1.0 - sg)))).astype(dg_ref.dtype)
            du_ref[PAD:, cols] = silu.astype(du_ref.dtype)
            act_ref[PAD:, cols] = (silu * u).astype(act_ref.dtype)

    wspec = pl.BlockSpec((FB, D), lambda j: (j, 0))
    ospec = pl.BlockSpec((T, FB), lambda j: (0, j))
    return pl.pallas_call(
        body, name=name, grid=(D_FFP // FB,),
        in_specs=[_full((T, D)), wspec, wspec], out_specs=[ospec] * 3,
        out_shape=[_sds((T, D_FFP), MXU_DTYPE)] * 3, compiler_params=_params(("parallel",)),
    )(hn, wg, wu)


def ffn_down_bwd(dh, wd, dact_dgate, dact_dup, after, name):
    def body(dh_ref, wd_ref, g_ref, u_ref, after_ref, dg_ref, du_ref):
        del after_ref
        dh = dh_ref[PAD:, :]
        for ref in (dg_ref, du_ref):
            ref[:PAD, :] = jnp.zeros((PAD, FB), ref.dtype)
        for c in range(FB // BN):
            cols = slice(BN * c, BN * (c + 1))
            dact = _dot(dh, wd_ref[cols, :], NT)
            dg_ref[PAD:, cols] = (dact * g_ref[PAD:, cols].astype(F32)).astype(dg_ref.dtype)
            du_ref[PAD:, cols] = (dact * u_ref[PAD:, cols].astype(F32)).astype(du_ref.dtype)

    blk = pl.BlockSpec((T, FB), lambda j: (0, j))
    return pl.pallas_call(
        body, name=name, grid=(D_FFP // FB,),
        in_specs=[_full((T, D)), pl.BlockSpec((FB, D), lambda j: (j, 0)), blk, blk, ANYSPEC],
        out_specs=[blk, blk],
        out_shape=[_sds((T, D_FFP), MXU_DTYPE)] * 2, compiler_params=_params(("parallel",)),
    )(dh, wd, dact_dgate, dact_dup, after)


def mm_blocked_nt(pairs, after, name):
    n = len(pairs)

    def body(*refs):
        o_ref = refs[2 * n + 1]

        @pl.when(pl.program_id(0) == 0)
        def _():
            o_ref[...] = jnp.zeros_like(o_ref)
        for p in range(n):
            o_ref[PAD:, :] += _dot(refs[2 * p][PAD:, :], _pair_cols(refs[2 * p + 1]), NT)

    specs, args = [], []
    for a, w in pairs:
        specs += [COLS_PAIR, W_PAIR(D)]
        args += [a, w]
    return pl.pallas_call(
        body, name=name, grid=(NPAIR,), in_specs=specs + [ANYSPEC], out_specs=_full((T, D)),
        out_shape=_sds((T, D), F32), compiler_params=_params(("arbitrary",)),
    )(*args, after)


def mm_tn_two(a1, a2, b, bm, after, name):
    m = a1.shape[1]

    def body(a1_ref, a2_ref, b_ref, after_ref, o1_ref, o2_ref):
        del after_ref
        b = b_ref[...]
        o1_ref[...] = _dot(a1_ref[...], b, TN).astype(o1_ref.dtype)
        o2_ref[...] = _dot(a2_ref[...], b, TN).astype(o2_ref.dtype)

    blk = pl.BlockSpec((T, bm), lambda i: (0, i))
    out = pl.BlockSpec((bm, D), lambda i: (i, 0))
    return pl.pallas_call(
        body, name=name, grid=(m // bm,),
        in_specs=[blk, blk, _full((T, D)), ANYSPEC], out_specs=[out, out],
        out_shape=[_sds((m, D), WIRE_DTYPE)] * 2, compiler_params=_params(("parallel",)),
    )(a1, a2, b, after)


def out_proj_bwd(dh, w, ymix, after, name):
    def body(dh_ref, w_ref, y_ref, after_ref, dy_ref, dw_ref):
        del after_ref
        dh_ = dh_ref[PAD:, :]
        dy_ref[:PAD, :] = jnp.zeros((PAD, BN), dy_ref.dtype)
        dy_ref[PAD:, :] = _dot(dh_, w_ref[...], NT)
        dw_ref[...] = _dot(y_ref[PAD:, :], dh_, TN).astype(dw_ref.dtype)

    return pl.pallas_call(
        body, name=name, grid=(D // BN,),
        in_specs=[_full((T, D)), pl.BlockSpec((BN, D), lambda j: (j, 0)), pl.BlockSpec((T, BN), lambda j: (0, j)), ANYSPEC],
        out_specs=[pl.BlockSpec((T, BN), lambda j: (0, j)), pl.BlockSpec((BN, D), lambda j: (j, 0))],
        out_shape=[_sds((T, D), F32), _sds((D, D), WIRE_DTYPE)], compiler_params=_params(("parallel",)),
    )(dh, w, ymix, after)


def mm_rows_nn(pairs, after, name):
    n = len(pairs)

    def body(*refs):
        o_ref = refs[2 * n + 1]

        @pl.when(pl.program_id(0) == 0)
        def _():
            o_ref[...] = jnp.zeros_like(o_ref)
        for p in range(n):
            o_ref[PAD:, :] += _dot(refs[2 * p][PAD:, :], refs[2 * p + 1][...], NN)

    specs, args = [], []
    for a, w in pairs:
        specs += [pl.BlockSpec((T, FB), lambda j: (0, j)), pl.BlockSpec((FB, D), lambda j: (j, 0))]
        args += [a, w]
    return pl.pallas_call(
        body, name=name, grid=(D_FFP // FB,), in_specs=specs + [ANYSPEC], out_specs=_full((T, D)),
        out_shape=_sds((T, D), F32), compiler_params=_params(("arbitrary",)),
    )(*args, after)


def mm_tn_blocked(a, b, name):
    def body(a_ref, b_ref, o_ref):
        o = _dot(a_ref[...], b_ref[...], TN).astype(o_ref.dtype)
        o_ref[0] = o[:, :IN_SH]
        o_ref[1] = o[:, IN_SH:]

    return pl.pallas_call(
        body, name=name, grid=(NPAIR,),
        in_specs=[_full((T, D)), COLS_PAIR], out_specs=W_PAIR(D),
        out_shape=_sds((NDEV, D, IN_SH), WIRE_DTYPE), compiler_params=_params(("parallel",)),
    )(a, b)


def mm_tn(a, b, bm, after, name):
    m = a.shape[1]

    def body(a_ref, b_ref, after_ref, o_ref):
        del after_ref
        o_ref[...] = _dot(a_ref[...], b_ref[...], TN).astype(o_ref.dtype)

    return pl.pallas_call(
        body, name=name, grid=(m // bm,),
        in_specs=[pl.BlockSpec((T, bm), lambda i: (0, i)), _full((T, D)), ANYSPEC],
        out_specs=pl.BlockSpec((bm, D), lambda i: (i, 0)),
        out_shape=_sds((m, D), WIRE_DTYPE), compiler_params=_params(("parallel",)),
    )(a, b, after)


def _softplus_neg(lam):
    return jnp.maximum(-lam, 0.0) + jnp.log1p(jnp.exp(-jnp.abs(lam)))


def _lru_gates(pa, px, xc, lam):
    r = jax.nn.sigmoid(pa)
    ig = jax.nn.sigmoid(px)
    sp = _softplus_neg(lam)
    log_a = -LRU_C * r * sp
    a = jnp.exp(log_a)
    mult = jnp.sqrt(-jnp.tanh(log_a) * (a * a + 1.0))
    return a, mult * (ig * xc), (r, ig, sp, mult)


def _lru_gates_vjp(da, db, xc, lam, a, r, ig, sp, mult):
    dmult = db * (ig * xc)
    du = db * mult
    dlog_a = da * a - dmult * (a * a) / mult
    dr = dlog_a * (-LRU_C * sp)
    dlam = jnp.sum(dlog_a * (-LRU_C * r), axis=0, keepdims=True) * (-jax.nn.sigmoid(-lam))
    dpa = dr * (r * (1.0 - r))
    dpx = (du * xc) * (ig * (1.0 - ig))
    return dpa, dpx, du * ig, dlam


def _lru_out(h, g, gain):
    z = h * jax.nn.gelu(g)
    return z * lax.rsqrt(jnp.mean(z * z, axis=-1, keepdims=True) + EPS) * gain


def _conv_taps(x, xprev, row):
    taps = [x]
    for s in range(1, CONV_W):
        taps.append(jnp.where(row < s, pltpu.roll(xprev, s, 0), pltpu.roll(x, s, 0)))
    return taps


def _conv(taps, cw_ref, cb):
    xc = cb + cw_ref[CONV_W - 1:CONV_W, :] * taps[0]
    for s in range(1, CONV_W):
        xc = xc + cw_ref[CONV_W - 1 - s:CONV_W - s, :] * taps[s]
    return xc


def _lru_fwd_block(i, x_ref, g_ref, cw_ref, cb_ref, wa_ref, ba_ref, wx_ref, bx_ref, lam_ref, gain_ref, y_ref, h_ref,
                   xprev_scr, a_scr, b_scr, carry_scr):
    @pl.when(i == 0)
    def _():
        xprev_scr[...] = jnp.zeros_like(xprev_scr)
        carry_scr[...] = jnp.zeros_like(carry_scr)

    x = x_ref[...]
    row = lax.broadcasted_iota(jnp.int32, (CH, D_LRU), 0)
    xc = _conv(_conv_taps(x, xprev_scr[...], row), cw_ref, cb_ref[...])
    pa = _dot(xc, wa_ref[...], NN) + ba_ref[...]
    px = _dot(xc, wx_ref[...], NN) + bx_ref[...]
    a, b, _ = _lru_gates(pa, px, xc, lam_ref[...])
    a_scr[...] = a
    b_scr[...] = jnp.where(i * CH + row >= PAD, b, 0.0)
    h = carry_scr[...]
    for t in range(CH):
        h = a_scr[t:t + 1, :] * h + b_scr[t:t + 1, :]
        h_ref[t:t + 1, :] = h
    carry_scr[...] = h
    xprev_scr[...] = x
    y_ref[:, :D_LRU] = _lru_out(h_ref[...], g_ref[...], gain_ref[...]).astype(y_ref.dtype)


LRU_VEC_ROWS = 16


def _lru_bwd_block(ib, x_ref, xp_ref, g_ref, h_ref, hp_ref, dy_ref, cw_ref, cb_ref, wa_ref, ba_ref, wx_ref, bx_ref, lam_ref,
                   gain_ref, dp_ref, vec_ref, dwa_ref, dwx_ref, a_scr, dh_scr, g_scr, carry_scr, dxcn_scr):
    @pl.when(ib == NCH - 1)
    def _():
        carry_scr[...] = jnp.zeros_like(carry_scr)
        dxcn_scr[...] = jnp.zeros_like(dxcn_scr)
        vec_ref[...] = jnp.zeros_like(vec_ref)
        dwa_ref[...] = jnp.zeros_like(dwa_ref)
        dwx_ref[...] = jnp.zeros_like(dwx_ref)

    x = x_ref[...]
    row = lax.broadcasted_iota(jnp.int32, (CH, D_LRU), 0)
    valid = ib * CH + row >= PAD
    taps = _conv_taps(x, xp_ref[...], row)
    xc = _conv(taps, cw_ref, cb_ref[...])
    pa = _dot(xc, wa_ref[...], NN) + ba_ref[...]
    px = _dot(xc, wx_ref[...], NN) + bx_ref[...]
    a, _, gate_parts = _lru_gates(pa, px, xc, lam_ref[...])
    h = h_ref[...]
    _, vjp_out = jax.vjp(_lru_out, h, g_ref[...], gain_ref[...])
    dh, dg, dgain = vjp_out(dy_ref[:, :D_LRU].astype(F32))
    a_scr[...] = a
    dh_scr[...] = dh
    c = carry_scr[...]
    for t in range(CH - 1, -1, -1):
        gt = dh_scr[t:t + 1, :] + c
        g_scr[t:t + 1, :] = gt
        c = a_scr[t:t + 1, :] * gt
    carry_scr[...] = c
    gg = g_scr[...]
    hprev = jnp.where(row < 1, pltpu.roll(hp_ref[...], 1, 0), pltpu.roll(h, 1, 0))
    da = jnp.where(valid, gg * hprev, 0.0)
    db = jnp.where(valid, gg, 0.0)
    dpa, dpx, dxc, dlam = _lru_gates_vjp(da, db, xc, lam_ref[...], a, *gate_parts)
    dxc = dxc + _dot(dpa, wa_ref[...], NT) + _dot(dpx, wx_ref[...], NT)
    dwa_ref[...] += _dot(xc, dpa, TN)
    dwx_ref[...] += _dot(xc, dpx, TN)
    for s in range(CONV_W):
        vec_ref[CONV_W - 1 - s:CONV_W - s, :] += jnp.sum(dxc * taps[s], axis=0, keepdims=True)
    vec_ref[4:5, :] += jnp.sum(dxc, axis=0, keepdims=True)
    vec_ref[5:6, :] += jnp.sum(dpa, axis=0, keepdims=True)
    vec_ref[6:7, :] += jnp.sum(dpx, axis=0, keepdims=True)
    vec_ref[7:8, :] += dlam
    vec_ref[8:9, :] += dgain
    dxn = dxcn_scr[...]
    dx = cw_ref[CONV_W - 1:CONV_W, :] * dxc
    for s in range(1, CONV_W):
        ahead = jnp.where(row >= CH - s, pltpu.roll(dxn, CH - s, 0), pltpu.roll(dxc, CH - s, 0))
        dx = dx + cw_ref[CONV_W - 1 - s:CONV_W - s, :] * ahead
    dxcn_scr[...] = dxc
    dp_ref[:, :D_LRU] = jnp.where(valid, dx, 0.0).astype(dp_ref.dtype)
    dp_ref[:, D_LRU:2 * D_LRU] = dg.astype(dp_ref.dtype)


def _ret_tables():
    half = HD // 2
    pos = jnp.arange(T, dtype=F32) - float(PAD)
    inv = ROPE_BASE ** (-jnp.arange(half, dtype=F32) / half)
    ang = pos[:, None] * inv[None, :]
    cos = jnp.concatenate([jnp.cos(ang), jnp.cos(ang)], axis=-1)
    sin = jnp.concatenate([-jnp.sin(ang), jnp.sin(ang)], axis=-1)
    log_g = jnp.log(1.0 - 2.0 ** (-5.0 - jnp.arange(HEADS, dtype=F32)))
    idx = jnp.arange(CH, dtype=F32)
    diff = idx[:, None] - idx[None, :]
    dmask = jnp.where(diff[None] >= 0, jnp.exp(jnp.maximum(diff, 0.0)[None] * log_g[:, None, None]), 0.0)
    xi = jnp.exp((idx + 1.0)[None, :] * log_g[:, None])
    zeta = jnp.exp((CH - 1.0 - idx)[None, :] * log_g[:, None])
    xi = jnp.broadcast_to(xi[:, :, None], (HEADS, CH, HD))
    zeta = jnp.broadcast_to(zeta[:, :, None], (HEADS, CH, HD))
    return cos, sin, dmask, xi, zeta


def _chunk_decay():
    log_g = np.log(np.float32(1.0) - np.float32(2.0) ** (np.float32(-5.0) - np.arange(HEADS, dtype=np.float32)))
    return [float(v) for v in np.exp(np.float32(CH) * log_g.astype(np.float32))]


def _rope(x, cos, sin):
    return x * cos + pltpu.roll(x, HD // 2, 1) * sin


def mix_fwd(proj, cw, cb, wa, ba, wx, bx, lam, gain, tables, ret_gain, after, name):
    cos, sin, dmask, xi, zeta = tables
    gch = _chunk_decay()
    scale = HD ** -0.5

    def body(x_ref, gl_ref, cw_ref, cb_ref, wa_ref, ba_ref, wx_ref, bx_ref, lam_ref, lgain_ref,
             q_ref, k_ref, v_ref, g_ref, cos_ref, sin_ref, dm_ref, xi_ref, zt_ref, gain_ref, after_ref,
             y_ref, h_ref, st_ref, xprev_scr, a_scr, b_scr, carry_scr, s_scr):
        del after_ref

        @pl.when(pl.program_id(0) == 0)
        def _():
            s_scr[...] = jnp.zeros_like(s_scr)

        _lru_fwd_block(pl.program_id(0), x_ref, gl_ref, cw_ref, cb_ref, wa_ref, ba_ref, wx_ref, bx_ref, lam_ref, lgain_ref,
                       y_ref, h_ref, xprev_scr, a_scr, b_scr, carry_scr)
        cs, sn = cos_ref[...], sin_ref[...]
        hs = range(HEADS)
        sl = [slice(HD * h, HD * (h + 1)) for h in hs]
        qr = [_rope(q_ref[:, sl[h]], cs, sn).astype(MXU_DTYPE) for h in hs]
        kf = [_rope(k_ref[:, sl[h]], cs, sn) * scale for h in hs]
        kr = [kf[h].astype(MXU_DTYPE) for h in hs]
        v = [v_ref[:, sl[h]].astype(MXU_DTYPE) for h in hs]
        s = [s_scr[h] for h in hs]
        for h in hs:
            st_ref[h] = s[h]
        sc = [_dot(qr[h], kr[h], NT) * dm_ref[h] for h in hs]
        cross = [_dot(qr[h], s[h], NN) * xi_ref[h] for h in hs]
        for h in hs:
            s_scr[h] = s[h] * gch[h] + _dot(kf[h] * zt_ref[h], v[h], TN)
        y = [_dot(sc[h], v[h], NN) + cross[h] for h in hs]
        yc = [y[h] - jnp.mean(y[h], axis=-1, keepdims=True) for h in hs]
        yn = [yc[h] * lax.rsqrt(jnp.mean(yc[h] * yc[h], axis=-1, keepdims=True) + EPS) for h in hs]
        for h in hs:
            so = slice(D_LRU + HD * h, D_LRU + HD * (h + 1))
            y_ref[:, so] = (jax.nn.silu(g_ref[:, sl[h]]) * (yn[h] * gain_ref[:, sl[h]])).astype(y_ref.dtype)

    def col(c):
        return pl.BlockSpec((CH, D_RET), lambda n: (n, c))

    tab = pl.BlockSpec((CH, HD), lambda n: (n, 0))
    cst = _full((HEADS, CH, HD))
    vec = _full((1, D_LRU))
    mat = _full((D_LRU, D_LRU))
    blockbuf = pltpu.VMEM((CH, D_LRU), F32)
    return pl.pallas_call(
        body, name=name, grid=(NCH,),
        in_specs=[col(0), col(1), _full((CONV_W, D_LRU)), vec, mat, vec, mat, vec, vec, vec,
                  col(2), col(3), col(4), col(5), tab, tab, cst, cst, cst, _full((1, D_RET)),
                  pl.BlockSpec(memory_space=pl.ANY)],
        out_specs=[pl.BlockSpec((CH, D), lambda n: (n, 0)), col(0), pl.BlockSpec((None, HEADS, HD, HD), lambda n: (n, 0, 0, 0))],
        out_shape=[_sds((T, D), MXU_DTYPE), _sds((T, D_LRU), F32), _sds((NCH, HEADS, HD, HD), F32)],
        scratch_shapes=[blockbuf, blockbuf, blockbuf, pltpu.VMEM((1, D_LRU), F32), pltpu.VMEM((HEADS, HD, HD), F32)],
        compiler_params=_params(("arbitrary",)),
    )(proj, proj, cw, cb, wa, ba, wx, bx, lam, gain, proj, proj, proj, proj, cos, sin, dmask, xi, zeta, ret_gain, after)


def mix_bwd(proj, hst, states, dymix, cw, cb, wa, ba, wx, bx, lam, gain, tables, ret_gain, after, name):
    cos, sin, dmask, xi, zeta = tables
    gch = _chunk_decay()
    scale = HD ** -0.5
    last = NCH - 1

    def body(x_ref, xp_ref, gl_ref, h_ref, hp_ref, cw_ref, cb_ref, wa_ref, ba_ref, wx_ref, bx_ref, lam_ref, lgain_ref,
             q_ref, k_ref, v_ref, g_ref, st_ref, dy_ref, cos_ref, sin_ref, dm_ref, xi_ref, zt_ref, gain_ref, after_ref,
             dp_ref, vec_ref, dwa_ref, dwx_ref, dgain_ref, a_scr, dh_scr, g_scr, carry_scr, dxcn_scr, ds_scr):
        del after_ref

        @pl.when(pl.program_id(0) == 0)
        def _():
            ds_scr[...] = jnp.zeros_like(ds_scr)
            dgain_ref[...] = jnp.zeros_like(dgain_ref)

        _lru_bwd_block(last - pl.program_id(0), x_ref, xp_ref, gl_ref, h_ref, hp_ref, dy_ref, cw_ref, cb_ref, wa_ref, ba_ref,
                       wx_ref, bx_ref, lam_ref, lgain_ref, dp_ref, vec_ref, dwa_ref, dwx_ref, a_scr, dh_scr, g_scr, carry_scr,
                       dxcn_scr)
        cs, sn = cos_ref[...], sin_ref[...]
        hs = range(HEADS)
        sl = [slice(HD * h, HD * (h + 1)) for h in hs]

        def out(j, h):
            return slice(2 * D_LRU + j * D_RET + HD * h, 2 * D_LRU + j * D_RET + HD * (h + 1))

        b16 = lambda xs: [x.astype(MXU_DTYPE) for x in xs]
        qr = b16([_rope(q_ref[:, sl[h]], cs, sn) for h in hs])
        kf = [_rope(k_ref[:, sl[h]], cs, sn) * scale for h in hs]
        kr = b16(kf)
        kz = b16([kf[h] * zt_ref[h] for h in hs])
        v = b16([v_ref[:, sl[h]] for h in hs])
        s = b16([st_ref[h] for h in hs])
        ds = [ds_scr[h] for h in hs]
        dsb = b16(ds)
        sc = [_dot(qr[h], kr[h], NT) * dm_ref[h] for h in hs]
        scb = b16(sc)
        y = [_dot(scb[h], v[h], NN) + _dot(qr[h], s[h], NN) * xi_ref[h] for h in hs]
        yc = [y[h] - jnp.mean(y[h], axis=-1, keepdims=True) for h in hs]
        rstd = [lax.rsqrt(jnp.mean(yc[h] * yc[h], axis=-1, keepdims=True) + EPS) for h in hs]
        yn = [yc[h] * rstd[h] for h in hs]
        dy = []
        for h in hs:
            g = g_ref[:, sl[h]]
            gain = gain_ref[:, sl[h]]
            sg = jax.nn.sigmoid(g)
            silu = g * sg
            dout = dy_ref[:, D_LRU + HD * h:D_LRU + HD * (h + 1)].astype(F32)
            dgain_ref[:, sl[h]] += jnp.sum(dout * silu * yn[h], axis=0, keepdims=True)
            dp_ref[:, out(3, h)] = (dout * yn[h] * gain * (sg * (1.0 + g * (1.0 - sg)))).astype(dp_ref.dtype)
            dyn = dout * silu * gain
            dy.append(rstd[h] * (dyn - jnp.mean(dyn, axis=-1, keepdims=True)
                                 - yn[h] * jnp.mean(dyn * yn[h], axis=-1, keepdims=True)))
        dyb = b16(dy)
        dqs = b16([dy[h] * xi_ref[h] for h in hs])
        dp = b16([_dot(dyb[h], v[h], NT) * dm_ref[h] for h in hs])
        dv = [_dot(scb[h], dyb[h], TN) + _dot(kz[h], dsb[h], NN) for h in hs]
        dqr = [_dot(dp[h], kr[h], NN) + _dot(dqs[h], s[h], NT) for h in hs]
        dkr = [_dot(dp[h], qr[h], TN) + _dot(v[h], dsb[h], NT) * zt_ref[h] for h in hs]
        for h in hs:
            ds_scr[h] = gch[h] * ds[h] + _dot(qr[h], dqs[h], TN)
        for h in hs:
            dp_ref[:, out(0, h)] = (dqr[h] * cs + pltpu.roll(dqr[h] * sn, HD // 2, 1)).astype(dp_ref.dtype)
            dp_ref[:, out(1, h)] = ((dkr[h] * cs + pltpu.roll(dkr[h] * sn, HD // 2, 1)) * scale).astype(dp_ref.dtype)
            dp_ref[:, out(2, h)] = dv[h].astype(dp_ref.dtype)

    def col(c, shift=0):
        return pl.BlockSpec((CH, D_RET), lambda n: (jnp.maximum(last - n - shift, 0), c))

    tab = pl.BlockSpec((CH, HD), lambda n: (last - n, 0))
    cst = _full((HEADS, CH, HD))
    vec = _full((1, D_LRU))
    mat = _full((D_LRU, D_LRU))
    blockbuf = pltpu.VMEM((CH, D_LRU), F32)
    return pl.pallas_call(
        body, name=name, grid=(NCH,),
        in_specs=[col(0), col(0, 1), col(1), col(0), col(0, 1), _full((CONV_W, D_LRU)), vec, mat, vec, mat, vec, vec, vec,
                  col(2), col(3), col(4), col(5), pl.BlockSpec((None, HEADS, HD, HD), lambda n: (last - n, 0, 0, 0)),
                  pl.BlockSpec((CH, D), lambda n: (last - n, 0)), tab, tab, cst, cst, cst, _full((1, D_RET)),
                  pl.BlockSpec(memory_space=pl.ANY)],
        out_specs=[pl.BlockSpec((CH, D_IN), lambda n: (last - n, 0)), _full((LRU_VEC_ROWS, D_LRU)), mat, mat,
                   _full((1, D_RET))],
        out_shape=[_sds((T, D_IN), MXU_DTYPE), _sds((LRU_VEC_ROWS, D_LRU), F32), _sds((D_LRU, D_LRU), F32),
                   _sds((D_LRU, D_LRU), F32), _sds((1, D_RET), F32)],
        scratch_shapes=[blockbuf, blockbuf, blockbuf, pltpu.VMEM((1, D_LRU), F32), blockbuf,
                        pltpu.VMEM((HEADS, HD, HD), F32)],
        compiler_params=_params(("arbitrary",)),
    )(proj, proj, proj, hst, hst, cw, cb, wa, ba, wx, bx, lam, gain, proj, proj, proj, proj, states, dymix,
      cos, sin, dmask, xi, zeta, ret_gain, after)


HBM = pl.BlockSpec(memory_space=pltpu.HBM)


def _place():
    return lax.axis_index("x"), lax.axis_index("y"), lax.axis_index("c")


def all_gather(arrs, after, name):
    n = len(arrs)

    def body(*refs):
        ins, outs = refs[:n], refs[n + 1:2 * n + 1]
        send_sems, recv_sems, local_sems = refs[2 * n + 1:]
        x, y, c = _place()
        me, sibling = (x, y, c), (x, y, 1 - c)
        chips = [(1 - x, y), (x, 1 - y), (1 - x, 1 - y)]

        def copy(a, k, block, to, src=None):
            px, py, pc = block
            dst = outs[a].at[4 * px + 2 * py + pc]
            return pltpu.make_async_remote_copy(
                src_ref=dst if src is None else src, dst_ref=dst, send_sem=send_sems.at[a, k], recv_sem=recv_sems.at[a, k],
                device_id=to, device_id_type=MESH)

        mine = [pltpu.make_async_copy(ins[a], outs[a].at[4 * x + 2 * y + c], local_sems.at[a]) for a in range(n)]
        for cp in mine:
            cp.start()
        first = []
        for a in range(n):
            first.append(copy(a, 0, me, sibling, src=ins[a]))
            first += [copy(a, 1 + j, me, (*chip, c), src=ins[a]) for j, chip in enumerate(chips)]
        for cp in first:
            cp.start()
        passed = []
        for j, chip in enumerate(chips):
            for a in range(n):
                copy(a, 1 + j, (*chip, c), me).wait_recv()
                passed.append(copy(a, 4 + j, (*chip, c), sibling))
                passed[-1].start()
        for a in range(n):
            copy(a, 0, sibling, me).wait_recv()
            for j, chip in enumerate(chips):
                copy(a, 4 + j, (*chip, 1 - c), me).wait_recv()
        for cp in first + passed:
            cp.wait_send()
        for cp in mine:
            cp.wait()

    return pl.pallas_call(
        body, name=name,
        in_specs=[HBM] * n + [pl.BlockSpec(memory_space=pl.ANY)], out_specs=[HBM] * n,
        out_shape=[_sds((NDEV,) + a.shape, a.dtype) for a in arrs],
        scratch_shapes=[pltpu.SemaphoreType.DMA((n, 7)), pltpu.SemaphoreType.DMA((n, 7)), pltpu.SemaphoreType.DMA((n,))],
    )(*arrs, after)


SEM = pl.BlockSpec(memory_space=pltpu.SEMAPHORE)
ANY = pl.BlockSpec(memory_space=pl.ANY)
EFFECT = pltpu.SideEffectType.DATAFLOW_SIDE_EFFECTING


def _hbm(a):
    return pltpu.with_memory_space_constraint(a, pltpu.HBM)


def _hbm_like(arrs):
    return [pltpu.HBM(a.shape, a.dtype) for a in arrs]


def _dma_sems(count):
    return [pltpu.SemaphoreType.DMA(())] * count


def _ag_copy(lands, send_sems, recv_sems, per):
    def copy(a, k, block, to, src=None):
        px, py, pc = block
        dst = lands[a].at[4 * px + 2 * py + pc]
        return pltpu.make_async_remote_copy(
            src_ref=dst if src is None else src, dst_ref=dst, send_sem=send_sems[a * per + k], recv_sem=recv_sems[a * per + k],
            device_id=to, device_id_type=MESH)
    return copy


def to_wire(sel, w_in, w_out, w_gate, w_up, w_down, name):
    ffpad = FF_SHP - FF_SH

    def body(sel_ref, i_ref, o_ref, g_ref, u_ref, d_ref, oi, oo, og, ou, od):
        del sel_ref
        oi[...] = i_ref[...].astype(oi.dtype)
        oo[...] = o_ref[...].astype(oo.dtype)
        for src, dst in ((g_ref, og), (u_ref, ou), (d_ref, od)):
            dst[:FF_SH, :] = src[...].astype(dst.dtype)
            dst[FF_SH:, :] = jnp.zeros((ffpad, D), dst.dtype)

    shapes_in = [(D, IN_SH), (OUT_SH, D), (FF_SH, D), (FF_SH, D), (FF_SH, D)]
    shapes_out = [(D, IN_SH), (OUT_SH, D), (FF_SHP, D), (FF_SHP, D), (FF_SHP, D)]
    return pl.pallas_call(
        body, name=name,
        grid_spec=pltpu.PrefetchScalarGridSpec(
            num_scalar_prefetch=1, grid=(1,),
            in_specs=[pl.BlockSpec((None,) + s, lambda i, sel_ref: (sel_ref[1], 0, 0)) for s in shapes_in],
            out_specs=[pl.BlockSpec((None,) + s, lambda i, sel_ref: (sel_ref[0], 0, 0)) for s in shapes_out]),
        out_shape=[_sds((NDEV,) + s, WIRE_DTYPE) for s in shapes_out], compiler_params=_params(("arbitrary",)),
    )(sel, w_in, w_out, w_gate, w_up, w_down)


def place_blocks(sel, arrs, name):
    n = len(arrs)

    def body(sel_ref, *refs):
        del sel_ref
        for a in range(n):
            refs[n + a][...] = refs[a][...]

    def whole(a):
        nd = a.ndim
        return pl.BlockSpec(a.shape, lambda i, sel_ref: (0,) * nd)

    def mine(a):
        nd = a.ndim
        return pl.BlockSpec((None,) + a.shape, lambda i, sel_ref: (sel_ref[0],) + (0,) * nd)

    return pl.pallas_call(
        body, name=name,
        grid_spec=pltpu.PrefetchScalarGridSpec(
            num_scalar_prefetch=1, grid=(1,), in_specs=[whole(a) for a in arrs], out_specs=[mine(a) for a in arrs]),
        out_shape=[_sds((NDEV,) + a.shape, a.dtype) for a in arrs], compiler_params=_params(("arbitrary",)),
    )(sel, *arrs)


def ag_start(lands, after, name):
    n = len(lands)
    ns = 4 * n

    def body(*refs):
        lnd = refs[:n]
        send_sems, recv_sems = refs[n + 1:n + 1 + ns], refs[n + 1 + ns:n + 1 + 2 * ns]
        token = refs[-1]
        x, y, c = _place()
        me, sibling = (x, y, c), (x, y, 1 - c)
        chips = [(1 - x, y), (x, 1 - y), (1 - x, 1 - y)]
        copy = _ag_copy(lnd, send_sems, recv_sems, 4)
        for a in range(n):
            copy(a, 0, me, sibling).start()
            for j, chip in enumerate(chips):
                copy(a, 1 + j, me, (*chip, c)).start()
        token[...] = jnp.zeros_like(token)

    outs = pl.pallas_call(
        body, name=name,
        in_specs=[HBM] * n + [ANY],
        out_specs=[SEM] * (2 * ns) + [HBM] * n + [pl.BlockSpec(memory_space=pltpu.VMEM)],
        out_shape=_dma_sems(2 * ns) + _hbm_like(lands) + [_sds((8, 128), F32)],
        input_output_aliases={i: 2 * ns + i for i in range(n)},
        compiler_params=pltpu.CompilerParams(has_side_effects=EFFECT),
    )(*[_hbm(a) for a in lands], after)
    return outs[:ns], outs[ns:2 * ns], outs[2 * ns:2 * ns + n], outs[-1]


def ag_forward(send_sems, recv_sems, lands, after, name):
    n = len(lands)
    n1, n2 = 4 * n, 3 * n

    def body(*refs):
        lnd = refs[:n]
        o = n
        s1, r1 = refs[o:o + n1], refs[o + n1:o + 2 * n1]
        o += 2 * n1 + 1
        s2, r2 = refs[o:o + n2], refs[o + n2:o + 2 * n2]
        token = refs[-1]
        token[...] = jnp.zeros_like(token)
        x, y, c = _place()
        me, sibling = (x, y, c), (x, y, 1 - c)
        chips = [(1 - x, y), (x, 1 - y), (1 - x, 1 - y)]
        copy1 = _ag_copy(lnd, s1, r1, 4)
        copy2 = _ag_copy(lnd, s2, r2, 3)
        for j, chip in enumerate(chips):
            for a in range(n):
                copy1(a, 1 + j, (*chip, c), me).wait_recv()
                copy2(a, j, (*chip, c), sibling).start()
        for a in range(n):
            copy1(a, 0, sibling, me).wait_recv()
            copy1(a, 0, me, sibling).wait_send()
            for j, chip in enumerate(chips):
                copy1(a, 1 + j, me, (*chip, c)).wait_send()

    outs = pl.pallas_call(
        body, name=name,
        in_specs=[HBM] * n + [SEM] * (2 * n1) + [ANY],
        out_specs=[SEM] * (2 * n2) + [HBM] * n + [pl.BlockSpec(memory_space=pltpu.VMEM)],
        out_shape=_dma_sems(2 * n2) + _hbm_like(lands) + [_sds((8, 128), F32)],
        input_output_aliases={i: 2 * n2 + i for i in range(n)},
        compiler_params=pltpu.CompilerParams(has_side_effects=EFFECT),
    )(*lands, *send_sems, *recv_sems, after)
    return outs[:n2], outs[n2:2 * n2], outs[2 * n2:2 * n2 + n], outs[-1]


def ag_finish(send_sems, recv_sems, lands, after, name):
    n = len(lands)
    n2 = 3 * n

    def body(*refs):
        lnd = refs[:n]
        s2, r2 = refs[n:n + n2], refs[n + n2:n + 2 * n2]
        x, y, c = _place()
        me, sibling = (x, y, c), (x, y, 1 - c)
        chips = [(1 - x, y), (x, 1 - y), (1 - x, 1 - y)]
        copy2 = _ag_copy(lnd, s2, r2, 3)
        for a in range(n):
            for j, chip in enumerate(chips):
                copy2(a, j, (*chip, c), sibling).wait_send()
                copy2(a, j, (*chip, 1 - c), me).wait_recv()

    outs = pl.pallas_call(
        body, name=name,
        in_specs=[HBM] * n + [SEM] * (2 * n2) + [ANY],
        out_specs=[HBM] * n, out_shape=_hbm_like(lands),
        input_output_aliases={i: i for i in range(n)},
        compiler_params=pltpu.CompilerParams(has_side_effects=EFFECT),
    )(*lands, *send_sems, *recv_sems, after)
    return list(outs)


def rs_sibling_start(arrs, name):
    n = len(arrs)
    ns = 4 * n
    lands = [lax.empty((4,) + a.shape[1:], a.dtype) for a in arrs]

    def body(*refs):
        ins, lnd = refs[:n], refs[n:2 * n]
        send_sems, recv_sems = refs[2 * n:2 * n + ns], refs[2 * n + ns:2 * n + 2 * ns]
        x, y, c = _place()
        sibling = (x, y, 1 - c)
        for a in range(n):
            for p in range(4):
                pltpu.make_async_remote_copy(
                    src_ref=ins[a].at[2 * p + 1 - c], dst_ref=lnd[a].at[p], send_sem=send_sems[4 * a + p],
                    recv_sem=recv_sems[4 * a + p], device_id=sibling, device_id_type=MESH).start()
        refs[-1][...] = jnp.zeros_like(refs[-1])

    outs = pl.pallas_call(
        body, name=name,
        in_specs=[HBM] * (2 * n), out_specs=[SEM] * (2 * ns) + [HBM] * (2 * n) + [pl.BlockSpec(memory_space=pltpu.VMEM)],
        out_shape=_dma_sems(2 * ns) + _hbm_like(arrs) + _hbm_like(lands) + [_sds((8, 128), F32)],
        input_output_aliases={i: 2 * ns + i for i in range(2 * n)},
        compiler_params=pltpu.CompilerParams(has_side_effects=EFFECT),
    )(*[_hbm(a) for a in arrs], *[_hbm(a) for a in lands])
    return (outs[:ns], outs[ns:2 * ns], outs[2 * ns:2 * ns + n], outs[2 * ns + n:2 * ns + 2 * n]), outs[-1]


def rs_sibling_wait(send_sems, recv_sems, arrs, lands, after, name):
    n = len(arrs)
    ns = 4 * n

    def body(*refs):
        ins, lnd = refs[:n], refs[n:2 * n]
        s, r = refs[2 * n:2 * n + ns], refs[2 * n + ns:2 * n + 2 * ns]
        x, y, c = _place()
        sibling = (x, y, 1 - c)
        for a in range(n):
            for p in range(4):
                cp = pltpu.make_async_remote_copy(
                    src_ref=ins[a].at[2 * p + 1 - c], dst_ref=lnd[a].at[p], send_sem=s[4 * a + p], recv_sem=r[4 * a + p],
                    device_id=sibling, device_id_type=MESH)
                cp.wait_send()
                cp.wait_recv()

    outs = pl.pallas_call(
        body, name=name,
        in_specs=[HBM] * (2 * n) + [SEM] * (2 * ns) + [ANY], out_specs=[HBM] * (2 * n),
        out_shape=_hbm_like(arrs) + _hbm_like(lands),
        input_output_aliases={i: i for i in range(2 * n)},
        compiler_params=pltpu.CompilerParams(has_side_effects=EFFECT),
    )(*arrs, *lands, *send_sems, *recv_sems, after)
    return outs[:n], outs[n:]


def rs_chips_start(parts, name):
    n = len(parts)
    ns = 3 * n
    lands = [lax.empty((3,) + a.shape[1:], a.dtype) for a in parts]

    def body(*refs):
        ins, lnd = refs[:n], refs[n:2 * n]
        send_sems, recv_sems = refs[2 * n:2 * n + ns], refs[2 * n + ns:2 * n + 2 * ns]
        x, y, c = _place()
        chips = [(1 - x, y), (x, 1 - y), (1 - x, 1 - y)]
        for a in range(n):
            for k, (tx, ty) in enumerate(chips):
                pltpu.make_async_remote_copy(
                    src_ref=ins[a].at[2 * tx + ty], dst_ref=lnd[a].at[k], send_sem=send_sems[3 * a + k],
                    recv_sem=recv_sems[3 * a + k], device_id=(tx, ty, c), device_id_type=MESH).start()
        refs[-1][...] = jnp.zeros_like(refs[-1])

    outs = pl.pallas_call(
        body, name=name,
        in_specs=[HBM] * (2 * n), out_specs=[SEM] * (2 * ns) + [HBM] * (2 * n) + [pl.BlockSpec(memory_space=pltpu.VMEM)],
        out_shape=_dma_sems(2 * ns) + _hbm_like(parts) + _hbm_like(lands) + [_sds((8, 128), F32)],
        input_output_aliases={i: 2 * ns + i for i in range(2 * n)},
        compiler_params=pltpu.CompilerParams(has_side_effects=EFFECT),
    )(*[_hbm(a) for a in parts], *[_hbm(a) for a in lands])
    return (outs[:ns], outs[ns:2 * ns], outs[2 * ns:2 * ns + n], outs[2 * ns + n:2 * ns + 2 * n]), outs[-1]


def rs_chips_wait(send_sems, recv_sems, parts, lands, after, name):
    n = len(parts)
    ns = 3 * n

    def body(*refs):
        ins, lnd = refs[:n], refs[n:2 * n]
        s, r = refs[2 * n:2 * n + ns], refs[2 * n + ns:2 * n + 2 * ns]
        x, y, c = _place()
        chips = [(1 - x, y), (x, 1 - y), (1 - x, 1 - y)]
        for a in range(n):
            for k, (tx, ty) in enumerate(chips):
                cp = pltpu.make_async_remote_copy(
                    src_ref=ins[a].at[2 * tx + ty], dst_ref=lnd[a].at[k], send_sem=s[3 * a + k], recv_sem=r[3 * a + k],
                    device_id=(tx, ty, c), device_id_type=MESH)
                cp.wait_send()
                cp.wait_recv()

    outs = pl.pallas_call(
        body, name=name,
        in_specs=[HBM] * (2 * n) + [SEM] * (2 * ns) + [ANY], out_specs=[HBM] * (2 * n),
        out_shape=_hbm_like(parts) + _hbm_like(lands),
        input_output_aliases={i: i for i in range(2 * n)},
        compiler_params=pltpu.CompilerParams(has_side_effects=EFFECT),
    )(*parts, *lands, *send_sems, *recv_sems, after)
    return outs[:n], outs[n:]


def pair_sum(arrs, recv, c, name):
    n = len(arrs)

    def body(c_ref, *refs):
        del c_ref
        for a in range(n):
            refs[2 * n + a][...] = (refs[a][...].astype(F32) + refs[n + a][...].astype(F32)).astype(refs[2 * n + a].dtype)

    mine = [pl.BlockSpec((None,) + a.shape[1:], lambda p, c_ref: (2 * p + c_ref[0], 0, 0)) for a in arrs]
    other = [pl.BlockSpec((None,) + a.shape[1:], lambda p, c_ref: (p, 0, 0)) for a in arrs]
    return pl.pallas_call(
        body, name=name,
        grid_spec=pltpu.PrefetchScalarGridSpec(num_scalar_prefetch=1, grid=(4,), in_specs=mine + other, out_specs=other),
        out_shape=[_sds((4,) + a.shape[1:], a.dtype) for a in arrs], compiler_params=_params(("parallel",)),
    )(c, *arrs, *recv)


def _adamw(w, g, m, v):
    m = ADAM_B1 * m + (1.0 - ADAM_B1) * g
    v = ADAM_B2 * v + (1.0 - ADAM_B2) * jnp.square(g)
    m_hat = m / (1.0 - ADAM_B1 ** ADAM_STEP)
    v_hat = v / (1.0 - ADAM_B2 ** ADAM_STEP)
    return -ADAM_LR * (m_hat / (jnp.sqrt(v_hat) + ADAM_EPS) + ADAM_WD * w), m, v


def adamw_big(recv, sums, chip, w, m, v, tr, name):
    nl, rr, cc = w.shape
    cp = recv[0].shape[2]

    def body(chip_ref, *refs):
        del chip_ref
        rcv, own = refs[:nl], refs[nl:2 * nl]
        w_ref, m_ref, v_ref, g_out, d_out, m_out, v_out = refs[2 * nl:]
        for l in range(nl):
            g = ((own[l][...].astype(F32) + rcv[l][0].astype(F32)) + rcv[l][1].astype(F32)) + rcv[l][2].astype(F32)
            g = g[:, :cc]
            g_out[l] = g
            d_out[l], m_out[l], v_out[l] = _adamw(w_ref[l], g, m_ref[l], v_ref[l])

    blk = pl.BlockSpec((nl, tr, cc), lambda i, chip_ref: (0, i, 0))
    return pl.pallas_call(
        body, name=name,
        grid_spec=pltpu.PrefetchScalarGridSpec(
            num_scalar_prefetch=1, grid=(rr // tr,),
            in_specs=[pl.BlockSpec((3, tr, cp), lambda i, chip_ref: (0, i, 0))] * nl
            + [pl.BlockSpec((None, tr, cp), lambda i, chip_ref: (chip_ref[0], i, 0))] * nl + [blk, blk, blk],
            out_specs=[blk] * 4),
        out_shape=[_sds(w.shape, F32)] * 4, compiler_params=_params(("parallel",)),
    )(chip, *recv, *sums, w, m, v)


SMALL_ROWS = 24


def small_grads(lvec, g_ret, g_mix, g_ffn, g_final, loss_part, dwa, dwx, name):
    def body(lvec_ref, ret_ref, mix_ref, ffn_ref, fin_ref, loss_ref, dwa_ref, dwx_ref, v_ref, g_ref):
        v_ref[16:SMALL_ROWS, :] = jnp.zeros((SMALL_ROWS - 16, D_LRU), F32)
        v_ref[16:17, 0:128] = loss_ref[0:1, :]
        v_ref[0:9, :] = lvec_ref[0:9, :]
        v_ref[9:10, :] = ret_ref[...]
        for r, src in ((10, mix_ref), (12, ffn_ref), (14, fin_ref)):
            v_ref[r:r + 1, :] = src[:, :D_LRU]
            v_ref[r + 1:r + 2, :] = src[:, D_LRU:]
        for k, src in enumerate((dwa_ref, dwx_ref)):
            for g in range(LRU_BLOCKS):
                rows = slice(LRU_BD * g, LRU_BD * (g + 1))
                g_ref[D_LRU * k + LRU_BD * g:D_LRU * k + LRU_BD * (g + 1), :] = src[rows, rows]

    ins = [lvec, g_ret, g_mix, g_ffn, g_final, loss_part, dwa, dwx]
    return pl.pallas_call(
        body, name=name, grid=(1,), in_specs=[_full(a.shape) for a in ins],
        out_specs=[_full((SMALL_ROWS, D_LRU)), _full((2 * D_LRU, LRU_BD))],
        out_shape=[_sds((SMALL_ROWS, D_LRU), F32), _sds((2 * D_LRU, LRU_BD), F32)], compiler_params=_params(("arbitrary",)),
    )(*ins)


def sum_devices(arrs, name):
    n = len(arrs)

    def body(*refs):
        for a in range(n):
            acc = refs[a][0]
            for j in range(1, NDEV):
                acc = acc + refs[a][j]
            refs[n + a][...] = acc

    return pl.pallas_call(
        body, name=name, grid=(1,), in_specs=[_full(a.shape) for a in arrs], out_specs=[_full(a.shape[1:]) for a in arrs],
        out_shape=[_sds(a.shape[1:], F32) for a in arrs], compiler_params=_params(("arbitrary",)),
    )(*arrs)


def adamw_small(gs, ws, ms, vs, name):
    n = len(gs)

    def body(*refs):
        for a in range(n):
            g, w, m, v = (refs[k * n + a][...] for k in range(4))
            refs[4 * n + a][...], refs[5 * n + a][...], refs[6 * n + a][...] = _adamw(w, g, m, v)

    specs = [_full(a.shape) for a in ws]
    outs = pl.pallas_call(
        body, name=name, grid=(1,), in_specs=specs * 4, out_specs=specs * 3, out_shape=[_sds(a.shape, F32) for a in ws] * 3,
        compiler_params=_params(("arbitrary",)),
    )(*gs, *ws, *ms, *vs)
    return outs[:n], outs[n:2 * n], outs[2 * n:]


def block_diag(wa, wx, name):
    def body(wa_ref, wx_ref, oa_ref, ox_ref):
        for src, dst in ((wa_ref, oa_ref), (wx_ref, ox_ref)):
            dst[...] = jnp.zeros_like(dst)
            for g in range(LRU_BLOCKS):
                rows = slice(LRU_BD * g, LRU_BD * (g + 1))
                dst[rows, rows] = src[g].astype(dst.dtype)

    ispec = pl.BlockSpec((None, LRU_BLOCKS, LRU_BD, LRU_BD), lambda l: (l, 0, 0, 0))
    ospec = pl.BlockSpec((None, D_LRU, D_LRU), lambda l: (l, 0, 0))
    return pl.pallas_call(
        body, name=name, grid=(wa.shape[0],), in_specs=[ispec, ispec], out_specs=[ospec, ospec],
        out_shape=[_sds((wa.shape[0], D_LRU, D_LRU), MXU_DTYPE)] * 2, compiler_params=_params(("parallel",)),
    )(wa, wx)


REP_NAMES = ["norm_mix", "conv_b", "gate_a_w", "gate_a_b", "gate_x_w", "gate_x_b", "lru_lambda", "lru_out_norm",
             "ret_out_norm", "norm_ffn", "norm_final"]


def kernel(x, meta_tokens, norm_mix, w_in, conv_w, conv_b, gate_a_w, gate_a_b, gate_x_w, gate_x_b, lru_lambda, lru_out_norm, ret_out_norm, w_out, norm_ffn, w_gate, w_up, w_down, norm_final, loss_target, m_meta_tokens, m_norm_mix, m_w_in, m_conv_w, m_conv_b, m_gate_a_w, m_gate_a_b, m_gate_x_w, m_gate_x_b, m_lru_lambda, m_lru_out_norm, m_ret_out_norm, m_w_out, m_norm_ffn, m_w_gate, m_w_up, m_w_down, m_norm_final, v_meta_tokens, v_norm_mix, v_w_in, v_conv_w, v_conv_b, v_gate_a_w, v_gate_a_b, v_gate_x_w, v_gate_x_b, v_lru_lambda, v_lru_out_norm, v_ret_out_norm, v_w_out, v_norm_ffn, v_w_gate, v_w_up, v_w_down, v_norm_final):
    xi, yi, ci = _place()
    dev = 4 * xi + 2 * yi + ci
    c_arr = jnp.reshape(ci, (1,)).astype(jnp.int32)
    dev_arr = jnp.reshape(dev, (1,)).astype(jnp.int32)

    meta_g, conv_g = all_gather([meta_tokens, conv_w], c_arr, "ag_small")
    meta_full = jnp.transpose(meta_g, (1, 0, 2)).reshape(N_META, D)
    conv_full = jnp.transpose(conv_g, (1, 2, 0, 3)).reshape(DEPTH, CONV_W, D_LRU)
    tr_ = lambda a: jnp.transpose(a, (0, 2, 1))
    w_gate_t, m_w_gate_t, v_w_gate_t = tr_(w_gate), tr_(m_w_gate), tr_(v_w_gate)
    w_up_t, m_w_up_t, v_w_up_t = tr_(w_up), tr_(m_w_up), tr_(v_w_up)
    level1 = []
    token = meta_g
    for l in range(DEPTH):
        sel = jnp.stack([dev, jnp.int32(l)]).astype(jnp.int32)
        lands = to_wire(sel, w_in, w_out, w_gate_t, w_up_t, w_down, "to_wire")
        s1, r1, lands, token = ag_start(lands, token, f"ag_start_{l}")
        level1.append((s1, r1, lands))

    def as_weights(gi, go, gg, gu, gd):
        return dict(w_in=gi, w_out=go.reshape(D, D), w_gate=gg.reshape(D_FFP, D), w_up=gu.reshape(D_FFP, D),
                    w_down=gd.reshape(D_FFP, D))

    tables = _ret_tables()
    row = lambda a: a.reshape(1, -1)

    h = jnp.concatenate([jnp.zeros((PAD, D), F32), meta_full, x[0]], axis=0)
    saved, gathered = [], []
    s1, r1, lands = level1[0]
    s2, r2, first, order = ag_forward(s1[:4], r1[:4], lands[:1], token, "ag_forward_0_w_in")
    w_in_next = ag_finish(s2, r2, first, h, "ag_finish_0_w_in")[0]
    wa_dense, wx_dense = block_diag(gate_a_w, gate_x_w, "block_diag")
    for l in range(DEPTH):
        small = dict(cw=conv_full[l], cb=row(conv_b[l]), wa=wa_dense[l], ba=row(gate_a_b[l]),
                     wx=wx_dense[l], bx=row(gate_x_b[l]), lam=row(lru_lambda[l]),
                     gain=row(lru_out_norm[l]))
        s1, r1, lands = level1[l]
        hn1 = rmsnorm_fwd(h, row(norm_mix[l]), "rms_fwd")
        proj = mm_blocked_nn(hn1, w_in_next, F32, "proj")
        if l > 0:
            s2, r2, rest, order = ag_forward(s1[4:], r1[4:], lands[1:], proj, f"ag_forward_{l}_rest")
            ymix, hst, states = mix_fwd(proj, tables=tables, ret_gain=row(ret_out_norm[l]), after=order, name="mix_fwd", **small)
            w = as_weights(w_in_next, *ag_finish(s2, r2, rest, ymix, f"ag_finish_{l}_rest"))
            h_mid = mm_nn_res(ymix, w["w_out"], h, order, "out_proj")
        else:
            ymix, hst, states = mix_fwd(proj, tables=tables, ret_gain=row(ret_out_norm[l]), after=order, name="mix_fwd", **small)
            s2, r2, mid, order = ag_forward(s1[4:16], r1[4:16], lands[1:4], ymix, "ag_forward_0_mid")
            mids = ag_finish(s2, r2, mid, order, "ag_finish_0_mid")
            w = dict(w_in=w_in_next, w_out=mids[0].reshape(D, D), w_gate=mids[1].reshape(D_FFP, D), w_up=mids[2].reshape(D_FFP, D))
            h_mid = mm_nn_res(ymix, w["w_out"], h, order, "out_proj")
            s2d, r2d, down, order = ag_forward(s1[16:], r1[16:], lands[4:], h_mid, "ag_forward_0_down")
        hn2 = rmsnorm_fwd(h_mid, row(norm_ffn[l]), "rms_fwd")
        act_dgate, act_dup, act = ffn_up(hn2, w["w_gate"], w["w_up"], "ffn_up")
        if l == 0:
            w["w_down"] = ag_finish(s2d, r2d, down, act, "ag_finish_0_down")[0].reshape(D_FFP, D)
        gathered.append(w)
        if l + 1 < DEPTH:
            s1n, r1n, landsn = level1[l + 1]
            s2, r2, first, order = ag_forward(s1n[:4], r1n[:4], landsn[:1], act, f"ag_forward_{l + 1}_w_in")
        h_out = mm_nn_res(act, w["w_down"], h_mid, order, "ffn_down")
        if l + 1 < DEPTH:
            w_in_next = ag_finish(s2, r2, first, h_out, f"ag_finish_{l + 1}_w_in")[0]
        saved.append(dict(h=h, hn1=hn1, proj=proj, hst=hst, states=states, ymix=ymix, h_mid=h_mid, hn2=hn2, act_dgate=act_dgate, act_dup=act_dup,
                          act=act, small=small))
        h = h_out

    loss_p, dh, dh_b, g_norm_final = loss_head(h, row(norm_final), loss_target[0], "loss_head")

    small_v = [None] * DEPTH
    small_w = [None] * DEPTH
    inflight = []
    order = loss_p

    def sibling_done(l, tag, names, sib, after):
        parts, got = rs_sibling_wait(*sib, after, f"rs_sibling_wait_{tag}")
        sums = pair_sum(parts, got, c_arr, "pair_sum")
        flying, started = rs_chips_start(sums, f"rs_chips_start_{tag}")
        inflight.append((l, tag, names, flying))
        return started

    for l in reversed(range(DEPTH)):
        w, s = gathered[l], saved[l]
        dgate, dup = ffn_down_bwd(dh_b, w["w_down"], s["act_dgate"], s["act_dup"], order, "ffn_down_bwd")
        dwd = mm_tn(s["act"], dh_b, PAIR, order, "dw_down").reshape(NDEV, FF_SHP, D)
        dwg, dwu = (g.reshape(NDEV, FF_SHP, D) for g in mm_tn_two(dgate, dup, s["hn2"], PAIR, order, "dw_rows"))
        split = l <= 1
        if split:
            ffn_sib, order = rs_sibling_start([dwg, dwu, dwd], f"rs_sibling_start_{l}_ffn")
        dhn2 = mm_rows_nn([(dgate, w["w_gate"]), (dup, w["w_up"])], order, "ffn_up_bwd")
        if split:
            order = sibling_done(l, f"{l}_ffn", ("w_gate", "w_up", "w_down"), ffn_sib, dhn2)
        dh_mid, dh_mid_b, g_norm_ffn = rmsnorm_bwd(s["h_mid"], row(norm_ffn[l]), dhn2, dh, "rms_bwd")
        dymix, dwo = out_proj_bwd(dh_mid_b, w["w_out"], s["ymix"], order, "out_proj_bwd")
        dwo = dwo.reshape(NDEV, OUT_SH, D)
        dproj, lvec, dwa, dwx, g_ret_norm = mix_bwd(s["proj"], s["hst"], s["states"], dymix, tables=tables,
                                                    ret_gain=row(ret_out_norm[l]), after=order, name="mix_bwd", **s["small"])
        dwi = mm_tn_blocked(s["hn1"], dproj, "dw_blocked")
        if split:
            sib_tag, sib_names = f"{l}_mix", ("w_in", "w_out")
            sib, order = rs_sibling_start([dwi, dwo], f"rs_sibling_start_{l}_mix")
        else:
            sib_tag, sib_names = str(l), ("w_in", "w_gate", "w_up", "w_out", "w_down")
            sib, order = rs_sibling_start([dwi, dwg, dwu, dwo, dwd], f"rs_sibling_start_{l}")
        dhn1 = mm_blocked_nt([(dproj, w["w_in"])], order, "proj_bwd")
        order = sibling_done(l, sib_tag, sib_names, sib, dhn1)
        dh, dh_b, g_norm_mix = rmsnorm_bwd(s["h"], row(norm_mix[l]), dhn1, dh_mid, "rms_bwd")

        g_fin, loss_part = (g_norm_final, loss_p) if l == 0 else (jnp.zeros((1, D), F32), jnp.zeros((8, 128), F32))
        small_v[l], small_w[l] = small_grads(lvec, g_ret_norm, g_norm_mix, g_norm_ffn, g_fin, loss_part, dwa, dwx,
                                             "small_grads")
        if l == 1:
            early = place_blocks(dev_arr, [jnp.stack(small_v[1:]), jnp.stack(small_w[1:])], "place_grads")
            early_sems = ag_start(early, order, "ag_start_grads")
            order = early_sems[3]

    grad_x = dh[X0:][None]
    g_meta = dh[PAD:X0]

    arrived = {}

    def wait_for(entries, after):
        for l, tag, names, flying in entries:
            sums, recv = rs_chips_wait(*flying, after, f"rs_chips_wait_{tag}")
            for i, n in enumerate(names):
                arrived[l, n] = (recv[i], sums[i])

    chip = jnp.reshape(2 * xi + yi, (1,)).astype(jnp.int32)

    def finish(wname, w_, m_, v_, tr):
        return adamw_big([arrived[l, wname][0] for l in range(DEPTH)], [arrived[l, wname][1] for l in range(DEPTH)], chip,
                         w_, m_, v_, tr, "adamw_" + wname)

    late_s1, late_r1, late_lands, late_started = ag_start(
        place_blocks(dev_arr, [small_v[0], small_w[0], g_meta], "place_late"), order, "ag_start_late")
    wait_for(inflight[:-1], late_started)
    o_gate = [tr_(o) for o in finish("w_gate", w_gate_t, m_w_gate_t, v_w_gate_t, 32)]
    o_up = [tr_(o) for o in finish("w_up", w_up_t, m_w_up_t, v_w_up_t, 32)]
    o_down = finish("w_down", w_down, m_w_down, v_w_down, 32)

    s2, r2, lands, _ = ag_forward(late_s1, late_r1, late_lands, o_down[0], "ag_forward_late")
    late = ag_finish(s2, r2, lands, o_down[0], "ag_finish_late")
    s2, r2, lands, _ = ag_forward(early_sems[0], early_sems[1], early_sems[2], dh, "ag_forward_grads")
    gath_early = ag_finish(s2, r2, lands, late[0], "ag_finish_grads")
    v0, w0, meta_sum, v123, w123 = sum_devices(list(late) + list(gath_early), "sum_devices")
    loss = v0[16, 0]
    vecs = jnp.concatenate([v0[None], v123])
    gws = jnp.concatenate([w0[None], w123])
    blocks = (DEPTH, LRU_BLOCKS, LRU_BD)
    small_g = dict(
        conv_w=lax.dynamic_slice_in_dim(vecs[:, 0:CONV_W], dev * (D_LRU // NDEV), D_LRU // NDEV, axis=2),
        conv_b=vecs[:, 4], gate_a_b=vecs[:, 5].reshape(blocks), gate_x_b=vecs[:, 6].reshape(blocks),
        lru_lambda=vecs[:, 7], lru_out_norm=vecs[:, 8], ret_out_norm=vecs[:, 9],
        norm_mix=vecs[:, 10:12].reshape(DEPTH, D), norm_ffn=vecs[:, 12:14].reshape(DEPTH, D),
        norm_final=v0[14:16].reshape(1, D),
        gate_a_w=gws[:, :D_LRU].reshape(blocks + (LRU_BD,)), gate_x_w=gws[:, D_LRU:].reshape(blocks + (LRU_BD,)),
        meta_tokens=lax.dynamic_slice_in_dim(meta_sum, dev * (D // NDEV), D // NDEV, axis=1))
    given = dict(norm_mix=(norm_mix, m_norm_mix, v_norm_mix), conv_b=(conv_b, m_conv_b, v_conv_b),
                 gate_a_w=(gate_a_w, m_gate_a_w, v_gate_a_w), gate_a_b=(gate_a_b, m_gate_a_b, v_gate_a_b),
                 gate_x_w=(gate_x_w, m_gate_x_w, v_gate_x_w), gate_x_b=(gate_x_b, m_gate_x_b, v_gate_x_b),
                 lru_lambda=(lru_lambda, m_lru_lambda, v_lru_lambda), lru_out_norm=(lru_out_norm, m_lru_out_norm, v_lru_out_norm),
                 ret_out_norm=(ret_out_norm, m_ret_out_norm, v_ret_out_norm), norm_ffn=(norm_ffn, m_norm_ffn, v_norm_ffn),
                 norm_final=tuple(a.reshape(1, D) for a in (norm_final, m_norm_final, v_norm_final)),
                 conv_w=(conv_w, m_conv_w, v_conv_w), meta_tokens=(meta_tokens, m_meta_tokens, v_meta_tokens))
    small_names = REP_NAMES + ["conv_w", "meta_tokens"]
    upd = adamw_small([small_g[n] for n in small_names], *[[given[n][k] for n in small_names] for k in range(3)],
                      "adamw_small")
    small_out = [dict(zip(small_names, u)) for u in upd]
    for d_ in [small_g] + small_out:
        d_["norm_final"] = d_["norm_final"].reshape(D)

    wait_for(inflight[-1:], upd[0][0])
    o_in = finish("w_in", w_in, m_w_in, v_w_in, 256)
    o_out = finish("w_out", w_out, m_w_out, v_w_out, 64)

    bigs = dict(w_in=o_in, w_out=o_out, w_gate=o_gate, w_up=o_up, w_down=o_down)
    order = ["meta_tokens", "norm_mix", "w_in", "conv_w", "conv_b", "gate_a_w", "gate_a_b", "gate_x_w", "gate_x_b", "lru_lambda",
             "lru_out_norm", "ret_out_norm", "w_out", "norm_ffn", "w_gate", "w_up", "w_down", "norm_final"]
    grads = [bigs[n][0] if n in bigs else small_g[n] for n in order]
    rest = [[bigs[n][k + 1] if n in bigs else small_out[k][n] for n in order] for k in range(3)]
    return (loss, grad_x, *grads, *rest[0], *rest[1], *rest[2])
```

```python
import numpy as np
import jax
import jax.numpy as jnp
from jax import lax
from jax.experimental import pallas as pl
from jax.experimental.pallas import tpu as pltpu

F32, BF16 = jnp.float32, jnp.bfloat16
MXU_DTYPE = BF16
WIRE_DTYPE = BF16

D = 1024
SEQ = 2048
DEPTH = 4
N_META = 16
CH = 128
PAD = (-(SEQ + N_META)) % CH
T = SEQ + N_META + PAD
NCH = T // CH
X0 = PAD + N_META
D_LRU = 512
LRU_BLOCKS = 8
LRU_BD = 64
CONV_W = 4
LRU_C = 8.0
D_RET = 512
HEADS = 4
HD = 128
ROPE_BASE = 10000.0
D_IN = 3072
D_FF = 2816
NDEV = 8
IN_SH = D_IN // NDEV
FF_SH = D_FF // NDEV
FF_SHP = 384
D_FFP = NDEV * FF_SHP
OUT_SH = D // NDEV
EPS = 1e-6
TM = 544
VMEM_LIMIT = 56 * 2**20
MESH = pl.DeviceIdType.MESH

ADAM_LR, ADAM_B1, ADAM_B2, ADAM_EPS, ADAM_WD, ADAM_STEP = 0.001, 0.9, 0.999, 1e-08, 0.01, 10

NN = ((1,), (0,))
NT = ((1,), (1,))
TN = ((0,), (0,))


def _dot(a, b, dims):
    return lax.dot_general(a.astype(MXU_DTYPE), b.astype(MXU_DTYPE), (dims, ((), ())), preferred_element_type=F32)


def _sds(shape, dtype):
    return jax.ShapeDtypeStruct(shape, dtype)


def _params(sem=None):
    return pltpu.CompilerParams(dimension_semantics=sem, vmem_limit_bytes=VMEM_LIMIT)


def _full(shape):
    n = len(shape)
    return pl.BlockSpec(shape, lambda *_: (0,) * n)


def rmsnorm_fwd(h, gain, name):
    def body(h_ref, g_ref, o_ref):
        x = h_ref[...]
        ms = jnp.mean(x * x, axis=-1, keepdims=True)
        o_ref[...] = (x * lax.rsqrt(ms + EPS) * g_ref[...]).astype(o_ref.dtype)

    return pl.pallas_call(
        body, name=name, grid=(T // TM,),
        in_specs=[pl.BlockSpec((TM, D), lambda i: (i, 0)), _full((1, D))],
        out_specs=pl.BlockSpec((TM, D), lambda i: (i, 0)),
        out_shape=_sds((T, D), MXU_DTYPE), compiler_params=_params(("parallel",)),
    )(h, gain)


def rmsnorm_bwd(h, gain, dhn, dres, name):
    def body(h_ref, g_ref, dhn_ref, dres_ref, dh_ref, dhb_ref, dg_ref):
        x = h_ref[...]
        rstd = lax.rsqrt(jnp.mean(x * x, axis=-1, keepdims=True) + EPS)
        xhat = x * rstd
        dy = dhn_ref[...]
        dyg = dy * g_ref[...]
        dh = dres_ref[...] + rstd * (dyg - xhat * jnp.mean(dyg * xhat, axis=-1, keepdims=True))
        dh_ref[...] = dh
        dhb_ref[...] = dh.astype(dhb_ref.dtype)

        @pl.when(pl.program_id(0) == 0)
        def _():
            dg_ref[...] = jnp.zeros_like(dg_ref)
        dg_ref[...] += jnp.sum(dy * xhat, axis=0, keepdims=True)

    row = pl.BlockSpec((TM, D), lambda i: (i, 0))
    return pl.pallas_call(
        body, name=name, grid=(T // TM,),
        in_specs=[row, _full((1, D)), row, row],
        out_specs=[row, row, _full((1, D))],
        out_shape=[_sds((T, D), F32), _sds((T, D), MXU_DTYPE), _sds((1, D), F32)], compiler_params=_params(("arbitrary",)),
    )(h, gain, dhn, dres)


def loss_head(h, gain, target, name):
    def body(h_ref, g_ref, t_ref, loss_ref, dh_ref, dhb_ref, dg_ref):
        i = pl.program_id(0)

        @pl.when(i == 0)
        def _():
            loss_ref[...] = jnp.zeros_like(loss_ref)
            dg_ref[...] = jnp.zeros_like(dg_ref)
            dh_ref[...] = jnp.zeros_like(dh_ref)
            dhb_ref[...] = jnp.zeros_like(dhb_ref)

        @pl.when(i > 0)
        def _():
            x = h_ref[...]
            g = g_ref[...]
            rstd = lax.rsqrt(jnp.mean(x * x, axis=-1, keepdims=True) + EPS)
            xhat = x * rstd
            err = xhat * g - t_ref[...]
            loss_ref[...] += 0.5 * jnp.sum(jnp.mean(err * err, axis=-1, keepdims=True), axis=0, keepdims=True)
            dy = err * (1.0 / D)
            dyg = dy * g
            dh = rstd * (dyg - xhat * jnp.mean(dyg * xhat, axis=-1, keepdims=True))
            dh_ref[...] = dh
            dhb_ref[...] = dh.astype(dhb_ref.dtype)
            dg_ref[...] += jnp.sum(dy * xhat, axis=0, keepdims=True)

    row = pl.BlockSpec((CH, D), lambda i: (i, 0))
    return pl.pallas_call(
        body, name=name, grid=(NCH,),
        in_specs=[row, _full((1, D)), pl.BlockSpec((CH, D), lambda i: (jnp.maximum(i - 1, 0), 0))],
        out_specs=[_full((8, 128)), row, row, _full((1, D))],
        out_shape=[_sds((8, 128), F32), _sds((T, D), F32), _sds((T, D), MXU_DTYPE), _sds((1, D), F32)],
        compiler_params=_params(("arbitrary",)),
    )(h, gain, target)


PAIR = 2 * IN_SH
NPAIR = NDEV // 2
BN = 256
FB = 512


def _pair_cols(w_ref):
    return jnp.concatenate([w_ref[0], w_ref[1]], axis=1)


W_PAIR = lambda k: pl.BlockSpec((2, k, IN_SH), lambda j: (j, 0, 0))
COLS_PAIR = pl.BlockSpec((T, PAIR), lambda j: (0, j))
ANYSPEC = pl.BlockSpec(memory_space=pl.ANY)


def mm_blocked_nn(a, w, out_dtype, name):
    k = a.shape[1]

    def body(a_ref, w_ref, o_ref):
        o_ref[:PAD, :] = jnp.zeros((PAD, PAIR), o_ref.dtype)
        o_ref[PAD:, :] = _dot(a_ref[PAD:, :], _pair_cols(w_ref), NN).astype(o_ref.dtype)

    return pl.pallas_call(
        body, name=name, grid=(NPAIR,),
        in_specs=[_full((T, k)), W_PAIR(k)], out_specs=COLS_PAIR,
        out_shape=_sds((T, NDEV * IN_SH), out_dtype), compiler_params=_params(("parallel",)),
    )(a, w)


def mm_nn_res(a, w, res, after, name):
    k = a.shape[1]

    def body(a_ref, w_ref, r_ref, after_ref, o_ref):
        del after_ref
        o_ref[:PAD, :] = r_ref[:PAD, :]
        o_ref[PAD:, :] = r_ref[PAD:, :] + _dot(a_ref[PAD:, :], w_ref[...], NN)

    col = pl.BlockSpec((T, BN), lambda j: (0, j))
    return pl.pallas_call(
        body, name=name, grid=(D // BN,),
        in_specs=[_full((T, k)), pl.BlockSpec((k, BN), lambda j: (0, j)), col, ANYSPEC], out_specs=col,
        out_shape=_sds((T, D), F32), compiler_params=_params(("parallel",)),
    )(a, w, res, after)


def ffn_up(hn, wg, wu, name):
    def body(a_ref, wg_ref, wu_ref, dg_ref, du_ref, act_ref):
        a = a_ref[PAD:, :]
        for ref in (dg_ref, du_ref, act_ref):
            ref[:PAD, :] = jnp.zeros((PAD, FB), ref.dtype)
        for c in range(FB // BN):
            cols = slice(BN * c, BN * (c + 1))
            g = _dot(a, wg_ref[cols, :], NT)
            u = _dot(a, wu_ref[cols, :], NT)
            sg = jax.nn.sigmoid(g)
            silu = g * sg
            dg_ref[PAD:, cols] = (u * (sg * (1.0 + g * (1.0 - sg)))).astype(dg_ref.dtype)
            du_ref[PAD:, cols] = silu.astype(du_ref.dtype)
            act_ref[PAD:, cols] = (silu * u).astype(act_ref.dtype)

    wspec = pl.BlockSpec((FB, D), lambda j: (j, 0))
    ospec = pl.BlockSpec((T, FB), lambda j: (0, j))
    return pl.pallas_call(
        body, name=name, grid=(D_FFP // FB,),
        in_specs=[_full((T, D)), wspec, wspec], out_specs=[ospec] * 3,
        out_shape=[_sds((T, D_FFP), MXU_DTYPE)] * 3, compiler_params=_params(("parallel",)),
    )(hn, wg, wu)


def ffn_down_bwd(dh, wd, dact_dgate, dact_dup, after, name):
    def body(dh_ref, wd_ref, g_ref, u_ref, after_ref, dg_ref, du_ref):
        del after_ref
        dh = dh_ref[PAD:, :]
        for ref in (dg_ref, du_ref):
            ref[:PAD, :] = jnp.zeros((PAD, FB), ref.dtype)
        for c in range(FB // BN):
            cols = slice(BN * c, BN * (c + 1))
            dact = _dot(dh, wd_ref[cols, :], NT)
            dg_ref[PAD:, cols] = (dact * g_ref[PAD:, cols].astype(F32)).astype(dg_ref.dtype)
            du_ref[PAD:, cols] = (dact * u_ref[PAD:, cols].astype(F32)).astype(du_ref.dtype)

    blk = pl.BlockSpec((T, FB), lambda j: (0, j))
    return pl.pallas_call(
        body, name=name, grid=(D_FFP // FB,),
        in_specs=[_full((T, D)), pl.BlockSpec((FB, D), lambda j: (j, 0)), blk, blk, ANYSPEC],
        out_specs=[blk, blk],
        out_shape=[_sds((T, D_FFP), MXU_DTYPE)] * 2, compiler_params=_params(("parallel",)),
    )(dh, wd, dact_dgate, dact_dup, after)


def mm_blocked_nt(pairs, after, name):
    n = len(pairs)

    def body(*refs):
        o_ref = refs[2 * n + 1]

        @pl.when(pl.program_id(0) == 0)
        def _():
            o_ref[...] = jnp.zeros_like(o_ref)
        for p in range(n):
            o_ref[PAD:, :] += _dot(refs[2 * p][PAD:, :], _pair_cols(refs[2 * p + 1]), NT)

    specs, args = [], []
    for a, w in pairs:
        specs += [COLS_PAIR, W_PAIR(D)]
        args += [a, w]
    return pl.pallas_call(
        body, name=name, grid=(NPAIR,), in_specs=specs + [ANYSPEC], out_specs=_full((T, D)),
        out_shape=_sds((T, D), F32), compiler_params=_params(("arbitrary",)),
    )(*args, after)


def mm_tn_two(a1, a2, b, bm, after, name):
    m = a1.shape[1]

    def body(a1_ref, a2_ref, b_ref, after_ref, o1_ref, o2_ref):
        del after_ref
        b = b_ref[...]
        o1_ref[...] = _dot(a1_ref[...], b, TN).astype(o1_ref.dtype)
        o2_ref[...] = _dot(a2_ref[...], b, TN).astype(o2_ref.dtype)

    blk = pl.BlockSpec((T, bm), lambda i: (0, i))
    out = pl.BlockSpec((bm, D), lambda i: (i, 0))
    return pl.pallas_call(
        body, name=name, grid=(m // bm,),
        in_specs=[blk, blk, _full((T, D)), ANYSPEC], out_specs=[out, out],
        out_shape=[_sds((m, D), WIRE_DTYPE)] * 2, compiler_params=_params(("parallel",)),
    )(a1, a2, b, after)


def out_proj_bwd(dh, w, ymix, after, name):
    def body(dh_ref, w_ref, y_ref, after_ref, dy_ref, dw_ref):
        del after_ref
        dh_ = dh_ref[PAD:, :]
        dy_ref[:PAD, :] = jnp.zeros((PAD, BN), dy_ref.dtype)
        dy_ref[PAD:, :] = _dot(dh_, w_ref[...], NT)
        dw_ref[...] = _dot(y_ref[PAD:, :], dh_, TN).astype(dw_ref.dtype)

    return pl.pallas_call(
        body, name=name, grid=(D // BN,),
        in_specs=[_full((T, D)), pl.BlockSpec((BN, D), lambda j: (j, 0)), pl.BlockSpec((T, BN), lambda j: (0, j)), ANYSPEC],
        out_specs=[pl.BlockSpec((T, BN), lambda j: (0, j)), pl.BlockSpec((BN, D), lambda j: (j, 0))],
        out_shape=[_sds((T, D), F32), _sds((D, D), WIRE_DTYPE)], compiler_params=_params(("parallel",)),
    )(dh, w, ymix, after)


def mm_rows_nn(pairs, after, name):
    n = len(pairs)

    def body(*refs):
        o_ref = refs[2 * n + 1]

        @pl.when(pl.program_id(0) == 0)
        def _():
            o_ref[...] = jnp.zeros_like(o_ref)
        for p in range(n):
            o_ref[PAD:, :] += _dot(refs[2 * p][PAD:, :], refs[2 * p + 1][...], NN)

    specs, args = [], []
    for a, w in pairs:
        specs += [pl.BlockSpec((T, FB), lambda j: (0, j)), pl.BlockSpec((FB, D), lambda j: (j, 0))]
        args += [a, w]
    return pl.pallas_call(
        body, name=name, grid=(D_FFP // FB,), in_specs=specs + [ANYSPEC], out_specs=_full((T, D)),
        out_shape=_sds((T, D), F32), compiler_params=_params(("arbitrary",)),
    )(*args, after)


def mm_tn_blocked(a, b, name):
    def body(a_ref, b_ref, o_ref):
        o = _dot(a_ref[...], b_ref[...], TN).astype(o_ref.dtype)
        o_ref[0] = o[:, :IN_SH]
        o_ref[1] = o[:, IN_SH:]

    return pl.pallas_call(
        body, name=name, grid=(NPAIR,),
        in_specs=[_full((T, D)), COLS_PAIR], out_specs=W_PAIR(D),
        out_shape=_sds((NDEV, D, IN_SH), WIRE_DTYPE), compiler_params=_params(("parallel",)),
    )(a, b)


def mm_tn(a, b, bm, after, name):
    m = a.shape[1]

    def body(a_ref, b_ref, after_ref, o_ref):
        del after_ref
        o_ref[...] = _dot(a_ref[...], b_ref[...], TN).astype(o_ref.dtype)

    return pl.pallas_call(
        body, name=name, grid=(m // bm,),
        in_specs=[pl.BlockSpec((T, bm), lambda i: (0, i)), _full((T, D)), ANYSPEC],
        out_specs=pl.BlockSpec((bm, D), lambda i: (i, 0)),
        out_shape=_sds((m, D), WIRE_DTYPE), compiler_params=_params(("parallel",)),
    )(a, b, after)


def _softplus_neg(lam):
    return jnp.maximum(-lam, 0.0) + jnp.log1p(jnp.exp(-jnp.abs(lam)))


def _lru_gates(pa, px, xc, lam):
    r = jax.nn.sigmoid(pa)
    ig = jax.nn.sigmoid(px)
    sp = _softplus_neg(lam)
    log_a = -LRU_C * r * sp
    a = jnp.exp(log_a)
    mult = jnp.sqrt(-jnp.tanh(log_a) * (a * a + 1.0))
    return a, mult * (ig * xc), (r, ig, sp, mult)


def _lru_gates_vjp(da, db, xc, lam, a, r, ig, sp, mult):
    dmult = db * (ig * xc)
    du = db * mult
    dlog_a = da * a - dmult * (a * a) / mult
    dr = dlog_a * (-LRU_C * sp)
    dlam = jnp.sum(dlog_a * (-LRU_C * r), axis=0, keepdims=True) * (-jax.nn.sigmoid(-lam))
    dpa = dr * (r * (1.0 - r))
    dpx = (du * xc) * (ig * (1.0 - ig))
    return dpa, dpx, du * ig, dlam


def _lru_out(h, g, gain):
    z = h * jax.nn.gelu(g)
    return z * lax.rsqrt(jnp.mean(z * z, axis=-1, keepdims=True) + EPS) * gain


def _conv_taps(x, xprev, row):
    taps = [x]
    for s in range(1, CONV_W):
        taps.append(jnp.where(row < s, pltpu.roll(xprev, s, 0), pltpu.roll(x, s, 0)))
    return taps


def _conv(taps, cw_ref, cb):
    xc = cb + cw_ref[CONV_W - 1:CONV_W, :] * taps[0]
    for s in range(1, CONV_W):
        xc = xc + cw_ref[CONV_W - 1 - s:CONV_W - s, :] * taps[s]
    return xc


def _lru_fwd_block(i, x_ref, g_ref, cw_ref, cb_ref, wa_ref, ba_ref, wx_ref, bx_ref, lam_ref, gain_ref, y_ref, h_ref,
                   xprev_scr, a_scr, b_scr, carry_scr):
    @pl.when(i == 0)
    def _():
        xprev_scr[...] = jnp.zeros_like(xprev_scr)
        carry_scr[...] = jnp.zeros_like(carry_scr)

    x = x_ref[...]
    row = lax.broadcasted_iota(jnp.int32, (CH, D_LRU), 0)
    xc = _conv(_conv_taps(x, xprev_scr[...], row), cw_ref, cb_ref[...])
    pa = _dot(xc, wa_ref[...], NN) + ba_ref[...]
    px = _dot(xc, wx_ref[...], NN) + bx_ref[...]
    a, b, _ = _lru_gates(pa, px, xc, lam_ref[...])
    a_scr[...] = a
    b_scr[...] = jnp.where(i * CH + row >= PAD, b, 0.0)
    h = carry_scr[...]
    for t in range(CH):
        h = a_scr[t:t + 1, :] * h + b_scr[t:t + 1, :]
        h_ref[t:t + 1, :] = h
    carry_scr[...] = h
    xprev_scr[...] = x
    y_ref[:, :D_LRU] = _lru_out(h_ref[...], g_ref[...], gain_ref[...]).astype(y_ref.dtype)


LRU_VEC_ROWS = 16


def _lru_bwd_block(ib, x_ref, xp_ref, g_ref, h_ref, hp_ref, dy_ref, cw_ref, cb_ref, wa_ref, ba_ref, wx_ref, bx_ref, lam_ref,
                   gain_ref, dp_ref, vec_ref, dwa_ref, dwx_ref, a_scr, dh_scr, g_scr, carry_scr, dxcn_scr):
    @pl.when(ib == NCH - 1)
    def _():
        carry_scr[...] = jnp.zeros_like(carry_scr)
        dxcn_scr[...] = jnp.zeros_like(dxcn_scr)
        vec_ref[...] = jnp.zeros_like(vec_ref)
        dwa_ref[...] = jnp.zeros_like(dwa_ref)
        dwx_ref[...] = jnp.zeros_like(dwx_ref)

    x = x_ref[...]
    row = lax.broadcasted_iota(jnp.int32, (CH, D_LRU), 0)
    valid = ib * CH + row >= PAD
    taps = _conv_taps(x, xp_ref[...], row)
    xc = _conv(taps, cw_ref, cb_ref[...])
    pa = _dot(xc, wa_ref[...], NN) + ba_ref[...]
    px = _dot(xc, wx_ref[...], NN) + bx_ref[...]
    a, _, gate_parts = _lru_gates(pa, px, xc, lam_ref[...])
    h = h_ref[...]
    _, vjp_out = jax.vjp(_lru_out, h, g_ref[...], gain_ref[...])
    dh, dg, dgain = vjp_out(dy_ref[:, :D_LRU].astype(F32))
    a_scr[...] = a
    dh_scr[...] = dh
    c = carry_scr[...]
    for t in range(CH - 1, -1, -1):
        gt = dh_scr[t:t + 1, :] + c
        g_scr[t:t + 1, :] = gt
        c = a_scr[t:t + 1, :] * gt
    carry_scr[...] = c
    gg = g_scr[...]
    hprev = jnp.where(row < 1, pltpu.roll(hp_ref[...], 1, 0), pltpu.roll(h, 1, 0))
    da = jnp.where(valid, gg * hprev, 0.0)
    db = jnp.where(valid, gg, 0.0)
    dpa, dpx, dxc, dlam = _lru_gates_vjp(da, db, xc, lam_ref[...], a, *gate_parts)
    dxc = dxc + _dot(dpa, wa_ref[...], NT) + _dot(dpx, wx_ref[...], NT)
    dwa_ref[...] += _dot(xc, dpa, TN)
    dwx_ref[...] += _dot(xc, dpx, TN)
    for s in range(CONV_W):
        vec_ref[CONV_W - 1 - s:CONV_W - s, :] += jnp.sum(dxc * taps[s], axis=0, keepdims=True)
    vec_ref[4:5, :] += jnp.sum(dxc, axis=0, keepdims=True)
    vec_ref[5:6, :] += jnp.sum(dpa, axis=0, keepdims=True)
    vec_ref[6:7, :] += jnp.sum(dpx, axis=0, keepdims=True)
    vec_ref[7:8, :] += dlam
    vec_ref[8:9, :] += dgain
    dxn = dxcn_scr[...]
    dx = cw_ref[CONV_W - 1:CONV_W, :] * dxc
    for s in range(1, CONV_W):
        ahead = jnp.where(row >= CH - s, pltpu.roll(dxn, CH - s, 0), pltpu.roll(dxc, CH - s, 0))
        dx = dx + cw_ref[CONV_W - 1 - s:CONV_W - s, :] * ahead
    dxcn_scr[...] = dxc
    dp_ref[:, :D_LRU] = jnp.where(valid, dx, 0.0).astype(dp_ref.dtype)
    dp_ref[:, D_LRU:2 * D_LRU] = dg.astype(dp_ref.dtype)


def _ret_tables():
    half = HD // 2
    pos = jnp.arange(T, dtype=F32) - float(PAD)
    inv = ROPE_BASE ** (-jnp.arange(half, dtype=F32) / half)
    ang = pos[:, None] * inv[None, :]
    cos = jnp.concatenate([jnp.cos(ang), jnp.cos(ang)], axis=-1)
    sin = jnp.concatenate([-jnp.sin(ang), jnp.sin(ang)], axis=-1)
    log_g = jnp.log(1.0 - 2.0 ** (-5.0 - jnp.arange(HEADS, dtype=F32)))
    idx = jnp.arange(CH, dtype=F32)
    diff = idx[:, None] - idx[None, :]
    dmask = jnp.where(diff[None] >= 0, jnp.exp(jnp.maximum(diff, 0.0)[None] * log_g[:, None, None]), 0.0)
    xi = jnp.exp((idx + 1.0)[None, :] * log_g[:, None])
    zeta = jnp.exp((CH - 1.0 - idx)[None, :] * log_g[:, None])
    xi = jnp.broadcast_to(xi[:, :, None], (HEADS, CH, HD))
    zeta = jnp.broadcast_to(zeta[:, :, None], (HEADS, CH, HD))
    return cos, sin, dmask, xi, zeta


def _chunk_decay():
    log_g = np.log(np.float32(1.0) - np.float32(2.0) ** (np.float32(-5.0) - np.arange(HEADS, dtype=np.float32)))
    return [float(v) for v in np.exp(np.float32(CH) * log_g.astype(np.float32))]


def _rope(x, cos, sin):
    return x * cos + pltpu.roll(x, HD // 2, 1) * sin


def mix_fwd(proj, cw, cb, wa, ba, wx, bx, lam, gain, tables, ret_gain, after, name):
    cos, sin, dmask, xi, zeta = tables
    gch = _chunk_decay()
    scale = HD ** -0.5

    def body(x_ref, gl_ref, cw_ref, cb_ref, wa_ref, ba_ref, wx_ref, bx_ref, lam_ref, lgain_ref,
             q_ref, k_ref, v_ref, g_ref, cos_ref, sin_ref, dm_ref, xi_ref, zt_ref, gain_ref, after_ref,
             y_ref, h_ref, st_ref, xprev_scr, a_scr, b_scr, carry_scr, s_scr):
        del after_ref

        @pl.when(pl.program_id(0) == 0)
        def _():
            s_scr[...] = jnp.zeros_like(s_scr)

        _lru_fwd_block(pl.program_id(0), x_ref, gl_ref, cw_ref, cb_ref, wa_ref, ba_ref, wx_ref, bx_ref, lam_ref, lgain_ref,
                       y_ref, h_ref, xprev_scr, a_scr, b_scr, carry_scr)
        cs, sn = cos_ref[...], sin_ref[...]
        hs = range(HEADS)
        sl = [slice(HD * h, HD * (h + 1)) for h in hs]
        qr = [_rope(q_ref[:, sl[h]], cs, sn).astype(MXU_DTYPE) for h in hs]
        kf = [_rope(k_ref[:, sl[h]], cs, sn) * scale for h in hs]
        kr = [kf[h].astype(MXU_DTYPE) for h in hs]
        v = [v_ref[:, sl[h]].astype(MXU_DTYPE) for h in hs]
        s = [s_scr[h] for h in hs]
        for h in hs:
            st_ref[h] = s[h]
        sc = [_dot(qr[h], kr[h], NT) * dm_ref[h] for h in hs]
        cross = [_dot(qr[h], s[h], NN) * xi_ref[h] for h in hs]
        for h in hs:
            s_scr[h] = s[h] * gch[h] + _dot(kf[h] * zt_ref[h], v[h], TN)
        y = [_dot(sc[h], v[h], NN) + cross[h] for h in hs]
        yc = [y[h] - jnp.mean(y[h], axis=-1, keepdims=True) for h in hs]
        yn = [yc[h] * lax.rsqrt(jnp.mean(yc[h] * yc[h], axis=-1, keepdims=True) + EPS) for h in hs]
        for h in hs:
            so = slice(D_LRU + HD * h, D_LRU + HD * (h + 1))
            y_ref[:, so] = (jax.nn.silu(g_ref[:, sl[h]]) * (yn[h] * gain_ref[:, sl[h]])).astype(y_ref.dtype)

    def col(c):
        return pl.BlockSpec((CH, D_RET), lambda n: (n, c))

    tab = pl.BlockSpec((CH, HD), lambda n: (n, 0))
    cst = _full((HEADS, CH, HD))
    vec = _full((1, D_LRU))
    mat = _full((D_LRU, D_LRU))
    blockbuf = pltpu.VMEM((CH, D_LRU), F32)
    return pl.pallas_call(
        body, name=name, grid=(NCH,),
        in_specs=[col(0), col(1), _full((CONV_W, D_LRU)), vec, mat, vec, mat, vec, vec, vec,
                  col(2), col(3), col(4), col(5), tab, tab, cst, cst, cst, _full((1, D_RET)),
                  pl.BlockSpec(memory_space=pl.ANY)],
        out_specs=[pl.BlockSpec((CH, D), lambda n: (n, 0)), col(0), pl.BlockSpec((None, HEADS, HD, HD), lambda n: (n, 0, 0, 0))],
        out_shape=[_sds((T, D), MXU_DTYPE), _sds((T, D_LRU), F32), _sds((NCH, HEADS, HD, HD), F32)],
        scratch_shapes=[blockbuf, blockbuf, blockbuf, pltpu.VMEM((1, D_LRU), F32), pltpu.VMEM((HEADS, HD, HD), F32)],
        compiler_params=_params(("arbitrary",)),
    )(proj, proj, cw, cb, wa, ba, wx, bx, lam, gain, proj, proj, proj, proj, cos, sin, dmask, xi, zeta, ret_gain, after)


def mix_bwd(proj, hst, states, dymix, cw, cb, wa, ba, wx, bx, lam, gain, tables, ret_gain, after, name):
    cos, sin, dmask, xi, zeta = tables
    gch = _chunk_decay()
    scale = HD ** -0.5
    last = NCH - 1

    def body(x_ref, xp_ref, gl_ref, h_ref, hp_ref, cw_ref, cb_ref, wa_ref, ba_ref, wx_ref, bx_ref, lam_ref, lgain_ref,
             q_ref, k_ref, v_ref, g_ref, st_ref, dy_ref, cos_ref, sin_ref, dm_ref, xi_ref, zt_ref, gain_ref, after_ref,
             dp_ref, vec_ref, dwa_ref, dwx_ref, dgain_ref, a_scr, dh_scr, g_scr, carry_scr, dxcn_scr, ds_scr):
        del after_ref

        @pl.when(pl.program_id(0) == 0)
        def _():
            ds_scr[...] = jnp.zeros_like(ds_scr)
            dgain_ref[...] = jnp.zeros_like(dgain_ref)

        _lru_bwd_block(last - pl.program_id(0), x_ref, xp_ref, gl_ref, h_ref, hp_ref, dy_ref, cw_ref, cb_ref, wa_ref, ba_ref,
                       wx_ref, bx_ref, lam_ref, lgain_ref, dp_ref, vec_ref, dwa_ref, dwx_ref, a_scr, dh_scr, g_scr, carry_scr,
                       dxcn_scr)
        cs, sn = cos_ref[...], sin_ref[...]
        hs = range(HEADS)
        sl = [slice(HD * h, HD * (h + 1)) for h in hs]

        def out(j, h):
            return slice(2 * D_LRU + j * D_RET + HD * h, 2 * D_LRU + j * D_RET + HD * (h + 1))

        b16 = lambda xs: [x.astype(MXU_DTYPE) for x in xs]
        qr = b16([_rope(q_ref[:, sl[h]], cs, sn) for h in hs])
        kf = [_rope(k_ref[:, sl[h]], cs, sn) * scale for h in hs]
        kr = b16(kf)
        kz = b16([kf[h] * zt_ref[h] for h in hs])
        v = b16([v_ref[:, sl[h]] for h in hs])
        s = b16([st_ref[h] for h in hs])
        ds = [ds_scr[h] for h in hs]
        dsb = b16(ds)
        sc = [_dot(qr[h], kr[h], NT) * dm_ref[h] for h in hs]
        scb = b16(sc)
        y = [_dot(scb[h], v[h], NN) + _dot(qr[h], s[h], NN) * xi_ref[h] for h in hs]
        yc = [y[h] - jnp.mean(y[h], axis=-1, keepdims=True) for h in hs]
        rstd = [lax.rsqrt(jnp.mean(yc[h] * yc[h], axis=-1, keepdims=True) + EPS) for h in hs]
        yn = [yc[h] * rstd[h] for h in hs]
        dy = []
        for h in hs:
            g = g_ref[:, sl[h]]
            gain = gain_ref[:, sl[h]]
            sg = jax.nn.sigmoid(g)
            silu = g * sg
            dout = dy_ref[:, D_LRU + HD * h:D_LRU + HD * (h + 1)].astype(F32)
            dgain_ref[:, sl[h]] += jnp.sum(dout * silu * yn[h], axis=0, keepdims=True)
            dp_ref[:, out(3, h)] = (dout * yn[h] * gain * (sg * (1.0 + g * (1.0 - sg)))).astype(dp_ref.dtype)
            dyn = dout * silu * gain
            dy.append(rstd[h] * (dyn - jnp.mean(dyn, axis=-1, keepdims=True)
                                 - yn[h] * jnp.mean(dyn * yn[h], axis=-1, keepdims=True)))
        dyb = b16(dy)
        dqs = b16([dy[h] * xi_ref[h] for h in hs])
        dp = b16([_dot(dyb[h], v[h], NT) * dm_ref[h] for h in hs])
        dv = [_dot(scb[h], dyb[h], TN) + _dot(kz[h], dsb[h], NN) for h in hs]
        dqr = [_dot(dp[h], kr[h], NN) + _dot(dqs[h], s[h], NT) for h in hs]
        dkr = [_dot(dp[h], qr[h], TN) + _dot(v[h], dsb[h], NT) * zt_ref[h] for h in hs]
        for h in hs:
            ds_scr[h] = gch[h] * ds[h] + _dot(qr[h], dqs[h], TN)
        for h in hs:
            dp_ref[:, out(0, h)] = (dqr[h] * cs + pltpu.roll(dqr[h] * sn, HD // 2, 1)).astype(dp_ref.dtype)
            dp_ref[:, out(1, h)] = ((dkr[h] * cs + pltpu.roll(dkr[h] * sn, HD // 2, 1)) * scale).astype(dp_ref.dtype)
            dp_ref[:, out(2, h)] = dv[h].astype(dp_ref.dtype)

    def col(c, shift=0):
        return pl.BlockSpec((CH, D_RET), lambda n: (jnp.maximum(last - n - shift, 0), c))

    tab = pl.BlockSpec((CH, HD), lambda n: (last - n, 0))
    cst = _full((HEADS, CH, HD))
    vec = _full((1, D_LRU))
    mat = _full((D_LRU, D_LRU))
    blockbuf = pltpu.VMEM((CH, D_LRU), F32)
    return pl.pallas_call(
        body, name=name, grid=(NCH,),
        in_specs=[col(0), col(0, 1), col(1), col(0), col(0, 1), _full((CONV_W, D_LRU)), vec, mat, vec, mat, vec, vec, vec,
                  col(2), col(3), col(4), col(5), pl.BlockSpec((None, HEADS, HD, HD), lambda n: (last - n, 0, 0, 0)),
                  pl.BlockSpec((CH, D), lambda n: (last - n, 0)), tab, tab, cst, cst, cst, _full((1, D_RET)),
                  pl.BlockSpec(memory_space=pl.ANY)],
        out_specs=[pl.BlockSpec((CH, D_IN), lambda n: (last - n, 0)), _full((LRU_VEC_ROWS, D_LRU)), mat, mat,
                   _full((1, D_RET))],
        out_shape=[_sds((T, D_IN), MXU_DTYPE), _sds((LRU_VEC_ROWS, D_LRU), F32), _sds((D_LRU, D_LRU), F32),
                   _sds((D_LRU, D_LRU), F32), _sds((1, D_RET), F32)],
        scratch_shapes=[blockbuf, blockbuf, blockbuf, pltpu.VMEM((1, D_LRU), F32), blockbuf,
                        pltpu.VMEM((HEADS, HD, HD), F32)],
        compiler_params=_params(("arbitrary",)),
    )(proj, proj, proj, hst, hst, cw, cb, wa, ba, wx, bx, lam, gain, proj, proj, proj, proj, states, dymix,
      cos, sin, dmask, xi, zeta, ret_gain, after)


HBM = pl.BlockSpec(memory_space=pltpu.HBM)


def _place():
    return lax.axis_index("x"), lax.axis_index("y"), lax.axis_index("c")


def all_gather(arrs, after, name):
    n = len(arrs)

    def body(*refs):
        ins, outs = refs[:n], refs[n + 1:2 * n + 1]
        send_sems, recv_sems, local_sems = refs[2 * n + 1:]
        x, y, c = _place()
        me, sibling = (x, y, c), (x, y, 1 - c)
        chips = [(1 - x, y), (x, 1 - y), (1 - x, 1 - y)]

        def copy(a, k, block, to, src=None):
            px, py, pc = block
            dst = outs[a].at[4 * px + 2 * py + pc]
            return pltpu.make_async_remote_copy(
                src_ref=dst if src is None else src, dst_ref=dst, send_sem=send_sems.at[a, k], recv_sem=recv_sems.at[a, k],
                device_id=to, device_id_type=MESH)

        mine = [pltpu.make_async_copy(ins[a], outs[a].at[4 * x + 2 * y + c], local_sems.at[a]) for a in range(n)]
        for cp in mine:
            cp.start()
        first = []
        for a in range(n):
            first.append(copy(a, 0, me, sibling, src=ins[a]))
            first += [copy(a, 1 + j, me, (*chip, c), src=ins[a]) for j, chip in enumerate(chips)]
        for cp in first:
            cp.start()
        passed = []
        for j, chip in enumerate(chips):
            for a in range(n):
                copy(a, 1 + j, (*chip, c), me).wait_recv()
                passed.append(copy(a, 4 + j, (*chip, c), sibling))
                passed[-1].start()
        for a in range(n):
            copy(a, 0, sibling, me).wait_recv()
            for j, chip in enumerate(chips):
                copy(a, 4 + j, (*chip, 1 - c), me).wait_recv()
        for cp in first + passed:
            cp.wait_send()
        for cp in mine:
            cp.wait()

    return pl.pallas_call(
        body, name=name,
        in_specs=[HBM] * n + [pl.BlockSpec(memory_space=pl.ANY)], out_specs=[HBM] * n,
        out_shape=[_sds((NDEV,) + a.shape, a.dtype) for a in arrs],
        scratch_shapes=[pltpu.SemaphoreType.DMA((n, 7)), pltpu.SemaphoreType.DMA((n, 7)), pltpu.SemaphoreType.DMA((n,))],
    )(*arrs, after)


SEM = pl.BlockSpec(memory_space=pltpu.SEMAPHORE)
ANY = pl.BlockSpec(memory_space=pl.ANY)
EFFECT = pltpu.SideEffectType.DATAFLOW_SIDE_EFFECTING


def _hbm(a):
    return pltpu.with_memory_space_constraint(a, pltpu.HBM)


def _hbm_like(arrs):
    return [pltpu.HBM(a.shape, a.dtype) for a in arrs]


def _dma_sems(count):
    return [pltpu.SemaphoreType.DMA(())] * count


def _ag_copy(lands, send_sems, recv_sems, per):
    def copy(a, k, block, to, src=None):
        px, py, pc = block
        dst = lands[a].at[4 * px + 2 * py + pc]
        return pltpu.make_async_remote_copy(
            src_ref=dst if src is None else src, dst_ref=dst, send_sem=send_sems[a * per + k], recv_sem=recv_sems[a * per + k],
            device_id=to, device_id_type=MESH)
    return copy


def to_wire(sel, w_in, w_out, w_gate, w_up, w_down, name):
    ffpad = FF_SHP - FF_SH

    def body(sel_ref, i_ref, o_ref, g_ref, u_ref, d_ref, oi, oo, og, ou, od):
        del sel_ref
        oi[...] = i_ref[...].astype(oi.dtype)
        oo[...] = o_ref[...].astype(oo.dtype)
        for src, dst in ((g_ref, og), (u_ref, ou), (d_ref, od)):
            dst[:FF_SH, :] = src[...].astype(dst.dtype)
            dst[FF_SH:, :] = jnp.zeros((ffpad, D), dst.dtype)

    shapes_in = [(D, IN_SH), (OUT_SH, D), (FF_SH, D), (FF_SH, D), (FF_SH, D)]
    shapes_out = [(D, IN_SH), (OUT_SH, D), (FF_SHP, D), (FF_SHP, D), (FF_SHP, D)]
    return pl.pallas_call(
        body, name=name,
        grid_spec=pltpu.PrefetchScalarGridSpec(
            num_scalar_prefetch=1, grid=(1,),
            in_specs=[pl.BlockSpec((None,) + s, lambda i, sel_ref: (sel_ref[1], 0, 0)) for s in shapes_in],
            out_specs=[pl.BlockSpec((None,) + s, lambda i, sel_ref: (sel_ref[0], 0, 0)) for s in shapes_out]),
        out_shape=[_sds((NDEV,) + s, WIRE_DTYPE) for s in shapes_out], compiler_params=_params(("arbitrary",)),
    )(sel, w_in, w_out, w_gate, w_up, w_down)


def place_blocks(sel, arrs, name):
    n = len(arrs)

    def body(sel_ref, *refs):
        del sel_ref
        for a in range(n):
            refs[n + a][...] = refs[a][...]

    def whole(a):
        nd = a.ndim
        return pl.BlockSpec(a.shape, lambda i, sel_ref: (0,) * nd)

    def mine(a):
        nd = a.ndim
        return pl.BlockSpec((None,) + a.shape, lambda i, sel_ref: (sel_ref[0],) + (0,) * nd)

    return pl.pallas_call(
        body, name=name,
        grid_spec=pltpu.PrefetchScalarGridSpec(
            num_scalar_prefetch=1, grid=(1,), in_specs=[whole(a) for a in arrs], out_specs=[mine(a) for a in arrs]),
        out_shape=[_sds((NDEV,) + a.shape, a.dtype) for a in arrs], compiler_params=_params(("arbitrary",)),
    )(sel, *arrs)


def ag_start(lands, after, name):
    n = len(lands)
    ns = 4 * n

    def body(*refs):
        lnd = refs[:n]
        send_sems, recv_sems = refs[n + 1:n + 1 + ns], refs[n + 1 + ns:n + 1 + 2 * ns]
        token = refs[-1]
        x, y, c = _place()
        me, sibling = (x, y, c), (x, y, 1 - c)
        chips = [(1 - x, y), (x, 1 - y), (1 - x, 1 - y)]
        copy = _ag_copy(lnd, send_sems, recv_sems, 4)
        for a in range(n):
            copy(a, 0, me, sibling).start()
            for j, chip in enumerate(chips):
                copy(a, 1 + j, me, (*chip, c)).start()
        token[...] = jnp.zeros_like(token)

    outs = pl.pallas_call(
        body, name=name,
        in_specs=[HBM] * n + [ANY],
        out_specs=[SEM] * (2 * ns) + [HBM] * n + [pl.BlockSpec(memory_space=pltpu.VMEM)],
        out_shape=_dma_sems(2 * ns) + _hbm_like(lands) + [_sds((8, 128), F32)],
        input_output_aliases={i: 2 * ns + i for i in range(n)},
        compiler_params=pltpu.CompilerParams(has_side_effects=EFFECT),
    )(*[_hbm(a) for a in lands], after)
    return outs[:ns], outs[ns:2 * ns], outs[2 * ns:2 * ns + n], outs[-1]


def ag_forward(send_sems, recv_sems, lands, after, name):
    n = len(lands)
    n1, n2 = 4 * n, 3 * n

    def body(*refs):
        lnd = refs[:n]
        o = n
        s1, r1 = refs[o:o + n1], refs[o + n1:o + 2 * n1]
        o += 2 * n1 + 1
        s2, r2 = refs[o:o + n2], refs[o + n2:o + 2 * n2]
        token = refs[-1]
        token[...] = jnp.zeros_like(token)
        x, y, c = _place()
        me, sibling = (x, y, c), (x, y, 1 - c)
        chips = [(1 - x, y), (x, 1 - y), (1 - x, 1 - y)]
        copy1 = _ag_copy(lnd, s1, r1, 4)
        copy2 = _ag_copy(lnd, s2, r2, 3)
        for j, chip in enumerate(chips):
            for a in range(n):
                copy1(a, 1 + j, (*chip, c), me).wait_recv()
                copy2(a, j, (*chip, c), sibling).start()
        for a in range(n):
            copy1(a, 0, sibling, me).wait_recv()
            copy1(a, 0, me, sibling).wait_send()
            for j, chip in enumerate(chips):
                copy1(a, 1 + j, me, (*chip, c)).wait_send()

    outs = pl.pallas_call(
        body, name=name,
        in_specs=[HBM] * n + [SEM] * (2 * n1) + [ANY],
        out_specs=[SEM] * (2 * n2) + [HBM] * n + [pl.BlockSpec(memory_space=pltpu.VMEM)],
        out_shape=_dma_sems(2 * n2) + _hbm_like(lands) + [_sds((8, 128), F32)],
        input_output_aliases={i: 2 * n2 + i for i in range(n)},
        compiler_params=pltpu.CompilerParams(has_side_effects=EFFECT),
    )(*lands, *send_sems, *recv_sems, after)
    return outs[:n2], outs[n2:2 * n2], outs[2 * n2:2 * n2 + n], outs[-1]


def ag_finish(send_sems, recv_sems, lands, after, name):
    n = len(lands)
    n2 = 3 * n

    def body(*refs):
        lnd = refs[:n]
        s2, r2 = refs[n:n + n2], refs[n + n2:n + 2 * n2]
        x, y, c = _place()
        me, sibling = (x, y, c), (x, y, 1 - c)
        chips = [(1 - x, y), (x, 1 - y), (1 - x, 1 - y)]
        copy2 = _ag_copy(lnd, s2, r2, 3)
        for a in range(n):
            for j, chip in enumerate(chips):
                copy2(a, j, (*chip, c), sibling).wait_send()
                copy2(a, j, (*chip, 1 - c), me).wait_recv()

    outs = pl.pallas_call(
        body, name=name,
        in_specs=[HBM] * n + [SEM] * (2 * n2) + [ANY],
        out_specs=[HBM] * n, out_shape=_hbm_like(lands),
        input_output_aliases={i: i for i in range(n)},
        compiler_params=pltpu.CompilerParams(has_side_effects=EFFECT),
    )(*lands, *send_sems, *recv_sems, after)
    return list(outs)


def rs_sibling_start(arrs, name):
    n = len(arrs)
    ns = 4 * n
    lands = [lax.empty((4,) + a.shape[1:], a.dtype) for a in arrs]

    def body(*refs):
        ins, lnd = refs[:n], refs[n:2 * n]
        send_sems, recv_sems = refs[2 * n:2 * n + ns], refs[2 * n + ns:2 * n + 2 * ns]
        x, y, c = _place()
        sibling = (x, y, 1 - c)
        for a in range(n):
            for p in range(4):
                pltpu.make_async_remote_copy(
                    src_ref=ins[a].at[2 * p + 1 - c], dst_ref=lnd[a].at[p], send_sem=send_sems[4 * a + p],
                    recv_sem=recv_sems[4 * a + p], device_id=sibling, device_id_type=MESH).start()
        refs[-1][...] = jnp.zeros_like(refs[-1])

    outs = pl.pallas_call(
        body, name=name,
        in_specs=[HBM] * (2 * n), out_specs=[SEM] * (2 * ns) + [HBM] * (2 * n) + [pl.BlockSpec(memory_space=pltpu.VMEM)],
        out_shape=_dma_sems(2 * ns) + _hbm_like(arrs) + _hbm_like(lands) + [_sds((8, 128), F32)],
        input_output_aliases={i: 2 * ns + i for i in range(2 * n)},
        compiler_params=pltpu.CompilerParams(has_side_effects=EFFECT),
    )(*[_hbm(a) for a in arrs], *[_hbm(a) for a in lands])
    return (outs[:ns], outs[ns:2 * ns], outs[2 * ns:2 * ns + n], outs[2 * ns + n:2 * ns + 2 * n]), outs[-1]


def rs_sibling_wait(send_sems, recv_sems, arrs, lands, after, name):
    n = len(arrs)
    ns = 4 * n

    def body(*refs):
        ins, lnd = refs[:n], refs[n:2 * n]
        s, r = refs[2 * n:2 * n + ns], refs[2 * n + ns:2 * n + 2 * ns]
        x, y, c = _place()
        sibling = (x, y, 1 - c)
        for a in range(n):
            for p in range(4):
                cp = pltpu.make_async_remote_copy(
                    src_ref=ins[a].at[2 * p + 1 - c], dst_ref=lnd[a].at[p], send_sem=s[4 * a + p], recv_sem=r[4 * a + p],
                    device_id=sibling, device_id_type=MESH)
                cp.wait_send()
                cp.wait_recv()

    outs = pl.pallas_call(
        body, name=name,
        in_specs=[HBM] * (2 * n) + [SEM] * (2 * ns) + [ANY], out_specs=[HBM] * (2 * n),
        out_shape=_hbm_like(arrs) + _hbm_like(lands),
        input_output_aliases={i: i for i in range(2 * n)},
        compiler_params=pltpu.CompilerParams(has_side_effects=EFFECT),
    )(*arrs, *lands, *send_sems, *recv_sems, after)
    return outs[:n], outs[n:]


def rs_chips_start(parts, name):
    n = len(parts)
    ns = 3 * n
    lands = [lax.empty((3,) + a.shape[1:], a.dtype) for a in parts]

    def body(*refs):
        ins, lnd = refs[:n], refs[n:2 * n]
        send_sems, recv_sems = refs[2 * n:2 * n + ns], refs[2 * n + ns:2 * n + 2 * ns]
        x, y, c = _place()
        chips = [(1 - x, y), (x, 1 - y), (1 - x, 1 - y)]
        for a in range(n):
            for k, (tx, ty) in enumerate(chips):
                pltpu.make_async_remote_copy(
                    src_ref=ins[a].at[2 * tx + ty], dst_ref=lnd[a].at[k], send_sem=send_sems[3 * a + k],
                    recv_sem=recv_sems[3 * a + k], device_id=(tx, ty, c), device_id_type=MESH).start()
        refs[-1][...] = jnp.zeros_like(refs[-1])

    outs = pl.pallas_call(
        body, name=name,
        in_specs=[HBM] * (2 * n), out_specs=[SEM] * (2 * ns) + [HBM] * (2 * n) + [pl.BlockSpec(memory_space=pltpu.VMEM)],
        out_shape=_dma_sems(2 * ns) + _hbm_like(parts) + _hbm_like(lands) + [_sds((8, 128), F32)],
        input_output_aliases={i: 2 * ns + i for i in range(2 * n)},
        compiler_params=pltpu.CompilerParams(has_side_effects=EFFECT),
    )(*[_hbm(a) for a in parts], *[_hbm(a) for a in lands])
    return (outs[:ns], outs[ns:2 * ns], outs[2 * ns:2 * ns + n], outs[2 * ns + n:2 * ns + 2 * n]), outs[-1]


def rs_chips_wait(send_sems, recv_sems, parts, lands, after, name):
    n = len(parts)
    ns = 3 * n

    def body(*refs):
        ins, lnd = refs[:n], refs[n:2 * n]
        s, r = refs[2 * n:2 * n + ns], refs[2 * n + ns:2 * n + 2 * ns]
        x, y, c = _place()
        chips = [(1 - x, y), (x, 1 - y), (1 - x, 1 - y)]
        for a in range(n):
            for k, (tx, ty) in enumerate(chips):
                cp = pltpu.make_async_remote_copy(
                    src_ref=ins[a].at[2 * tx + ty], dst_ref=lnd[a].at[k], send_sem=s[3 * a + k], recv_sem=r[3 * a + k],
                    device_id=(tx, ty, c), device_id_type=MESH)
                cp.wait_send()
                cp.wait_recv()

    outs = pl.pallas_call(
        body, name=name,
        in_specs=[HBM] * (2 * n) + [SEM] * (2 * ns) + [ANY], out_specs=[HBM] * (2 * n),
        out_shape=_hbm_like(parts) + _hbm_like(lands),
        input_output_aliases={i: i for i in range(2 * n)},
        compiler_params=pltpu.CompilerParams(has_side_effects=EFFECT),
    )(*parts, *lands, *send_sems, *recv_sems, after)
    return outs[:n], outs[n:]


def pair_sum(arrs, recv, c, name):
    n = len(arrs)

    def body(c_ref, *refs):
        del c_ref
        for a in range(n):
            refs[2 * n + a][...] = (refs[a][...].astype(F32) + refs[n + a][...].astype(F32)).astype(refs[2 * n + a].dtype)

    mine = [pl.BlockSpec((None,) + a.shape[1:], lambda p, c_ref: (2 * p + c_ref[0], 0, 0)) for a in arrs]
    other = [pl.BlockSpec((None,) + a.shape[1:], lambda p, c_ref: (p, 0, 0)) for a in arrs]
    return pl.pallas_call(
        body, name=name,
        grid_spec=pltpu.PrefetchScalarGridSpec(num_scalar_prefetch=1, grid=(4,), in_specs=mine + other, out_specs=other),
        out_shape=[_sds((4,) + a.shape[1:], a.dtype) for a in arrs], compiler_params=_params(("parallel",)),
    )(c, *arrs, *recv)


def _adamw(w, g, m, v):
    m = ADAM_B1 * m + (1.0 - ADAM_B1) * g
    v = ADAM_B2 * v + (1.0 - ADAM_B2) * jnp.square(g)
    m_hat = m / (1.0 - ADAM_B1 ** ADAM_STEP)
    v_hat = v / (1.0 - ADAM_B2 ** ADAM_STEP)
    return -ADAM_LR * (m_hat / (jnp.sqrt(v_hat) + ADAM_EPS) + ADAM_WD * w), m, v


def adamw_big(recv, sums, chip, w, m, v, tr, name):
    nl, rr, cc = w.shape
    cp = recv[0].shape[2]

    def body(chip_ref, *refs):
        del chip_ref
        rcv, own = refs[:nl], refs[nl:2 * nl]
        w_ref, m_ref, v_ref, g_out, d_out, m_out, v_out = refs[2 * nl:]
        for l in range(nl):
            g = ((own[l][...].astype(F32) + rcv[l][0].astype(F32)) + rcv[l][1].astype(F32)) + rcv[l][2].astype(F32)
            g = g[:, :cc]
            g_out[l] = g
            d_out[l], m_out[l], v_out[l] = _adamw(w_ref[l], g, m_ref[l], v_ref[l])

    blk = pl.BlockSpec((nl, tr, cc), lambda i, chip_ref: (0, i, 0))
    return pl.pallas_call(
        body, name=name,
        grid_spec=pltpu.PrefetchScalarGridSpec(
            num_scalar_prefetch=1, grid=(rr // tr,),
            in_specs=[pl.BlockSpec((3, tr, cp), lambda i, chip_ref: (0, i, 0))] * nl
            + [pl.BlockSpec((None, tr, cp), lambda i, chip_ref: (chip_ref[0], i, 0))] * nl + [blk, blk, blk],
            out_specs=[blk] * 4),
        out_shape=[_sds(w.shape, F32)] * 4, compiler_params=_params(("parallel",)),
    )(chip, *recv, *sums, w, m, v)


SMALL_ROWS = 24


def small_grads(lvec, g_ret, g_mix, g_ffn, g_final, loss_part, dwa, dwx, name):
    def body(lvec_ref, ret_ref, mix_ref, ffn_ref, fin_ref, loss_ref, dwa_ref, dwx_ref, v_ref, g_ref):
        v_ref[16:SMALL_ROWS, :] = jnp.zeros((SMALL_ROWS - 16, D_LRU), F32)
        v_ref[16:17, 0:128] = loss_ref[0:1, :]
        v_ref[0:9, :] = lvec_ref[0:9, :]
        v_ref[9:10, :] = ret_ref[...]
        for r, src in ((10, mix_ref), (12, ffn_ref), (14, fin_ref)):
            v_ref[r:r + 1, :] = src[:, :D_LRU]
            v_ref[r + 1:r + 2, :] = src[:, D_LRU:]
        for k, src in enumerate((dwa_ref, dwx_ref)):
            for g in range(LRU_BLOCKS):
                rows = slice(LRU_BD * g, LRU_BD * (g + 1))
                g_ref[D_LRU * k + LRU_BD * g:D_LRU * k + LRU_BD * (g + 1), :] = src[rows, rows]

    ins = [lvec, g_ret, g_mix, g_ffn, g_final, loss_part, dwa, dwx]
    return pl.pallas_call(
        body, name=name, grid=(1,), in_specs=[_full(a.shape) for a in ins],
        out_specs=[_full((SMALL_ROWS, D_LRU)), _full((2 * D_LRU, LRU_BD))],
        out_shape=[_sds((SMALL_ROWS, D_LRU), F32), _sds((2 * D_LRU, LRU_BD), F32)], compiler_params=_params(("arbitrary",)),
    )(*ins)


def sum_devices(arrs, name):
    n = len(arrs)

    def body(*refs):
        for a in range(n):
            acc = refs[a][0]
            for j in range(1, NDEV):
                acc = acc + refs[a][j]
            refs[n + a][...] = acc

    return pl.pallas_call(
        body, name=name, grid=(1,), in_specs=[_full(a.shape) for a in arrs], out_specs=[_full(a.shape[1:]) for a in arrs],
        out_shape=[_sds(a.shape[1:], F32) for a in arrs], compiler_params=_params(("arbitrary",)),
    )(*arrs)


def adamw_small(gs, ws, ms, vs, name):
    n = len(gs)

    def body(*refs):
        for a in range(n):
            g, w, m, v = (refs[k * n + a][...] for k in range(4))
            refs[4 * n + a][...], refs[5 * n + a][...], refs[6 * n + a][...] = _adamw(w, g, m, v)

    specs = [_full(a.shape) for a in ws]
    outs = pl.pallas_call(
        body, name=name, grid=(1,), in_specs=specs * 4, out_specs=specs * 3, out_shape=[_sds(a.shape, F32) for a in ws] * 3,
        compiler_params=_params(("arbitrary",)),
    )(*gs, *ws, *ms, *vs)
    return outs[:n], outs[n:2 * n], outs[2 * n:]


def block_diag(wa, wx, name):
    def body(wa_ref, wx_ref, oa_ref, ox_ref):
        for src, dst in ((wa_ref, oa_ref), (wx_ref, ox_ref)):
            dst[...] = jnp.zeros_like(dst)
            for g in range(LRU_BLOCKS):
                rows = slice(LRU_BD * g, LRU_BD * (g + 1))
                dst[rows, rows] = src[g].astype(dst.dtype)

    ispec = pl.BlockSpec((None, LRU_BLOCKS, LRU_BD, LRU_BD), lambda l: (l, 0, 0, 0))
    ospec = pl.BlockSpec((None, D_LRU, D_LRU), lambda l: (l, 0, 0))
    return pl.pallas_call(
        body, name=name, grid=(wa.shape[0],), in_specs=[ispec, ispec], out_specs=[ospec, ospec],
        out_shape=[_sds((wa.shape[0], D_LRU, D_LRU), MXU_DTYPE)] * 2, compiler_params=_params(("parallel",)),
    )(wa, wx)


REP_NAMES = ["norm_mix", "conv_b", "gate_a_w", "gate_a_b", "gate_x_w", "gate_x_b", "lru_lambda", "lru_out_norm",
             "ret_out_norm", "norm_ffn", "norm_final"]


def kernel(x, meta_tokens, norm_mix, w_in, conv_w, conv_b, gate_a_w, gate_a_b, gate_x_w, gate_x_b, lru_lambda, lru_out_norm, ret_out_norm, w_out, norm_ffn, w_gate, w_up, w_down, norm_final, loss_target, m_meta_tokens, m_norm_mix, m_w_in, m_conv_w, m_conv_b, m_gate_a_w, m_gate_a_b, m_gate_x_w, m_gate_x_b, m_lru_lambda, m_lru_out_norm, m_ret_out_norm, m_w_out, m_norm_ffn, m_w_gate, m_w_up, m_w_down, m_norm_final, v_meta_tokens, v_norm_mix, v_w_in, v_conv_w, v_conv_b, v_gate_a_w, v_gate_a_b, v_gate_x_w, v_gate_x_b, v_lru_lambda, v_lru_out_norm, v_ret_out_norm, v_w_out, v_norm_ffn, v_w_gate, v_w_up, v_w_down, v_norm_final):
    xi, yi, ci = _place()
    dev = 4 * xi + 2 * yi + ci
    c_arr = jnp.reshape(ci, (1,)).astype(jnp.int32)
    dev_arr = jnp.reshape(dev, (1,)).astype(jnp.int32)

    meta_g, conv_g = all_gather([meta_tokens, conv_w], c_arr, "ag_small")
    meta_full = jnp.transpose(meta_g, (1, 0, 2)).reshape(N_META, D)
    conv_full = jnp.transpose(conv_g, (1, 2, 0, 3)).reshape(DEPTH, CONV_W, D_LRU)
    tr_ = lambda a: jnp.transpose(a, (0, 2, 1))
    w_gate_t, m_w_gate_t, v_w_gate_t = tr_(w_gate), tr_(m_w_gate), tr_(v_w_gate)
    w_up_t, m_w_up_t, v_w_up_t = tr_(w_up), tr_(m_w_up), tr_(v_w_up)
    level1 = []
    token = meta_g
    for l in range(DEPTH):
        sel = jnp.stack([dev, jnp.int32(l)]).astype(jnp.int32)
        lands = to_wire(sel, w_in, w_out, w_gate_t, w_up_t, w_down, "to_wire")
        s1, r1, lands, token = ag_start(lands, token, f"ag_start_{l}")
        level1.append((s1, r1, lands))

    def as_weights(gi, go, gg, gu, gd):
        return dict(w_in=gi, w_out=go.reshape(D, D), w_gate=gg.reshape(D_FFP, D), w_up=gu.reshape(D_FFP, D),
                    w_down=gd.reshape(D_FFP, D))

    tables = _ret_tables()
    row = lambda a: a.reshape(1, -1)

    h = jnp.concatenate([jnp.zeros((PAD, D), F32), meta_full, x[0]], axis=0)
    saved, gathered = [], []
    s1, r1, lands = level1[0]
    s2, r2, first, order = ag_forward(s1[:4], r1[:4], lands[:1], token, "ag_forward_0_w_in")
    w_in_next = ag_finish(s2, r2, first, h, "ag_finish_0_w_in")[0]
    wa_dense, wx_dense = block_diag(gate_a_w, gate_x_w, "block_diag")
    for l in range(DEPTH):
        small = dict(cw=conv_full[l], cb=row(conv_b[l]), wa=wa_dense[l], ba=row(gate_a_b[l]),
                     wx=wx_dense[l], bx=row(gate_x_b[l]), lam=row(lru_lambda[l]),
                     gain=row(lru_out_norm[l]))
        s1, r1, lands = level1[l]
        hn1 = rmsnorm_fwd(h, row(norm_mix[l]), "rms_fwd")
        proj = mm_blocked_nn(hn1, w_in_next, F32, "proj")
        if l > 1:
            s2, r2, rest, order = ag_forward(s1[4:], r1[4:], lands[1:], proj, f"ag_forward_{l}_rest")
            ymix, hst, states = mix_fwd(proj, tables=tables, ret_gain=row(ret_out_norm[l]), after=order, name="mix_fwd", **small)
            w = as_weights(w_in_next, *ag_finish(s2, r2, rest, ymix, f"ag_finish_{l}_rest"))
            h_mid = mm_nn_res(ymix, w["w_out"], h, order, "out_proj")
        else:
            ymix, hst, states = mix_fwd(proj, tables=tables, ret_gain=row(ret_out_norm[l]), after=order, name="mix_fwd", **small)
            s2, r2, mid, order = ag_forward(s1[4:16], r1[4:16], lands[1:4], ymix, f"ag_forward_{l}_mid")
            mids = ag_finish(s2, r2, mid, order, f"ag_finish_{l}_mid")
            w = dict(w_in=w_in_next, w_out=mids[0].reshape(D, D), w_gate=mids[1].reshape(D_FFP, D), w_up=mids[2].reshape(D_FFP, D))
            h_mid = mm_nn_res(ymix, w["w_out"], h, order, "out_proj")
            s2d, r2d, down, order = ag_forward(s1[16:], r1[16:], lands[4:], h_mid, f"ag_forward_{l}_down")
        hn2 = rmsnorm_fwd(h_mid, row(norm_ffn[l]), "rms_fwd")
        act_dgate, act_dup, act = ffn_up(hn2, w["w_gate"], w["w_up"], "ffn_up")
        if l <= 1:
            w["w_down"] = ag_finish(s2d, r2d, down, act, f"ag_finish_{l}_down")[0].reshape(D_FFP, D)
        gathered.append(w)
        if l + 1 < DEPTH:
            s1n, r1n, landsn = level1[l + 1]
            s2, r2, first, order = ag_forward(s1n[:4], r1n[:4], landsn[:1], act, f"ag_forward_{l + 1}_w_in")
        h_out = mm_nn_res(act, w["w_down"], h_mid, order, "ffn_down")
        if l + 1 < DEPTH:
            w_in_next = ag_finish(s2, r2, first, h_out, f"ag_finish_{l + 1}_w_in")[0]
        saved.append(dict(h=h, hn1=hn1, proj=proj, hst=hst, states=states, ymix=ymix, h_mid=h_mid, hn2=hn2, act_dgate=act_dgate, act_dup=act_dup,
                          act=act, small=small))
        h = h_out

    loss_p, dh, dh_b, g_norm_final = loss_head(h, row(norm_final), loss_target[0], "loss_head")

    small_v = [None] * DEPTH
    small_w = [None] * DEPTH
    inflight = []
    order = loss_p

    def sibling_done(l, tag, names, sib, after):
        parts, got = rs_sibling_wait(*sib, after, f"rs_sibling_wait_{tag}")
        sums = pair_sum(parts, got, c_arr, "pair_sum")
        flying, started = rs_chips_start(sums, f"rs_chips_start_{tag}")
        inflight.append((l, tag, names, flying))
        return started

    for l in reversed(range(DEPTH)):
        w, s = gathered[l], saved[l]
        dgate, dup = ffn_down_bwd(dh_b, w["w_down"], s["act_dgate"], s["act_dup"], order, "ffn_down_bwd")
        dwd = mm_tn(s["act"], dh_b, PAIR, order, "dw_down").reshape(NDEV, FF_SHP, D)
        dwg, dwu = (g.reshape(NDEV, FF_SHP, D) for g in mm_tn_two(dgate, dup, s["hn2"], PAIR, order, "dw_rows"))
        split = l <= 1
        if split:
            ffn_sib, order = rs_sibling_start([dwg, dwu, dwd], f"rs_sibling_start_{l}_ffn")
        dhn2 = mm_rows_nn([(dgate, w["w_gate"]), (dup, w["w_up"])], order, "ffn_up_bwd")
        if split:
            order = sibling_done(l, f"{l}_ffn", ("w_gate", "w_up", "w_down"), ffn_sib, dhn2)
        dh_mid, dh_mid_b, g_norm_ffn = rmsnorm_bwd(s["h_mid"], row(norm_ffn[l]), dhn2, dh, "rms_bwd")
        dymix, dwo = out_proj_bwd(dh_mid_b, w["w_out"], s["ymix"], order, "out_proj_bwd")
        dwo = dwo.reshape(NDEV, OUT_SH, D)
        dproj, lvec, dwa, dwx, g_ret_norm = mix_bwd(s["proj"], s["hst"], s["states"], dymix, tables=tables,
                                                    ret_gain=row(ret_out_norm[l]), after=order, name="mix_bwd", **s["small"])
        dwi = mm_tn_blocked(s["hn1"], dproj, "dw_blocked")
        if split:
            sib_tag, sib_names = f"{l}_mix", ("w_in", "w_out")
            sib, order = rs_sibling_start([dwi, dwo], f"rs_sibling_start_{l}_mix")
        else:
            sib_tag, sib_names = str(l), ("w_in", "w_gate", "w_up", "w_out", "w_down")
            sib, order = rs_sibling_start([dwi, dwg, dwu, dwo, dwd], f"rs_sibling_start_{l}")
        dhn1 = mm_blocked_nt([(dproj, w["w_in"])], order, "proj_bwd")
        order = sibling_done(l, sib_tag, sib_names, sib, dhn1)
        dh, dh_b, g_norm_mix = rmsnorm_bwd(s["h"], row(norm_mix[l]), dhn1, dh_mid, "rms_bwd")

        g_fin, loss_part = (g_norm_final, loss_p) if l == 0 else (jnp.zeros((1, D), F32), jnp.zeros((8, 128), F32))
        small_v[l], small_w[l] = small_grads(lvec, g_ret_norm, g_norm_mix, g_norm_ffn, g_fin, loss_part, dwa, dwx,
                                             "small_grads")
        if l == 1:
            early = place_blocks(dev_arr, [jnp.stack(small_v[1:]), jnp.stack(small_w[1:])], "place_grads")
            early_sems = ag_start(early, order, "ag_start_grads")
            order = early_sems[3]

    grad_x = dh[X0:][None]
    g_meta = dh[PAD:X0]

    arrived = {}

    def wait_for(entries, after):
        for l, tag, names, flying in entries:
            sums, recv = rs_chips_wait(*flying, after, f"rs_chips_wait_{tag}")
            for i, n in enumerate(names):
                arrived[l, n] = (recv[i], sums[i])

    chip = jnp.reshape(2 * xi + yi, (1,)).astype(jnp.int32)

    def finish(wname, w_, m_, v_, tr):
        return adamw_big([arrived[l, wname][0] for l in range(DEPTH)], [arrived[l, wname][1] for l in range(DEPTH)], chip,
                         w_, m_, v_, tr, "adamw_" + wname)

    late_s1, late_r1, late_lands, late_started = ag_start(
        place_blocks(dev_arr, [small_v[0], small_w[0], g_meta], "place_late"), order, "ag_start_late")
    wait_for(inflight[:-1], late_started)
    o_gate = [tr_(o) for o in finish("w_gate", w_gate_t, m_w_gate_t, v_w_gate_t, 32)]
    o_up = [tr_(o) for o in finish("w_up", w_up_t, m_w_up_t, v_w_up_t, 32)]
    o_down = finish("w_down", w_down, m_w_down, v_w_down, 32)

    s2, r2, lands, _ = ag_forward(late_s1, late_r1, late_lands, o_down[0], "ag_forward_late")
    late = ag_finish(s2, r2, lands, o_down[0], "ag_finish_late")
    s2, r2, lands, _ = ag_forward(early_sems[0], early_sems[1], early_sems[2], dh, "ag_forward_grads")
    gath_early = ag_finish(s2, r2, lands, late[0], "ag_finish_grads")
    v0, w0, meta_sum, v123, w123 = sum_devices(list(late) + list(gath_early), "sum_devices")
    loss = v0[16, 0]
    vecs = jnp.concatenate([v0[None], v123])
    gws = jnp.concatenate([w0[None], w123])
    blocks = (DEPTH, LRU_BLOCKS, LRU_BD)
    small_g = dict(
        conv_w=lax.dynamic_slice_in_dim(vecs[:, 0:CONV_W], dev * (D_LRU // NDEV), D_LRU // NDEV, axis=2),
        conv_b=vecs[:, 4], gate_a_b=vecs[:, 5].reshape(blocks), gate_x_b=vecs[:, 6].reshape(blocks),
        lru_lambda=vecs[:, 7], lru_out_norm=vecs[:, 8], ret_out_norm=vecs[:, 9],
        norm_mix=vecs[:, 10:12].reshape(DEPTH, D), norm_ffn=vecs[:, 12:14].reshape(DEPTH, D),
        norm_final=v0[14:16].reshape(1, D),
        gate_a_w=gws[:, :D_LRU].reshape(blocks + (LRU_BD,)), gate_x_w=gws[:, D_LRU:].reshape(blocks + (LRU_BD,)),
        meta_tokens=lax.dynamic_slice_in_dim(meta_sum, dev * (D // NDEV), D // NDEV, axis=1))
    given = dict(norm_mix=(norm_mix, m_norm_mix, v_norm_mix), conv_b=(conv_b, m_conv_b, v_conv_b),
                 gate_a_w=(gate_a_w, m_gate_a_w, v_gate_a_w), gate_a_b=(gate_a_b, m_gate_a_b, v_gate_a_b),
                 gate_x_w=(gate_x_w, m_gate_x_w, v_gate_x_w), gate_x_b=(gate_x_b, m_gate_x_b, v_gate_x_b),
                 lru_lambda=(lru_lambda, m_lru_lambda, v_lru_lambda), lru_out_norm=(lru_out_norm, m_lru_out_norm, v_lru_out_norm),
                 ret_out_norm=(ret_out_norm, m_ret_out_norm, v_ret_out_norm), norm_ffn=(norm_ffn, m_norm_ffn, v_norm_ffn),
                 norm_final=tuple(a.reshape(1, D) for a in (norm_final, m_norm_final, v_norm_final)),
                 conv_w=(conv_w, m_conv_w, v_conv_w), meta_tokens=(meta_tokens, m_meta_tokens, v_meta_tokens))
    small_names = REP_NAMES + ["conv_w", "meta_tokens"]
    upd = adamw_small([small_g[n] for n in small_names], *[[given[n][k] for n in small_names] for k in range(3)],
                      "adamw_small")
    small_out = [dict(zip(small_names, u)) for u in upd]
    for d_ in [small_g] + small_out:
        d_["norm_final"] = d_["norm_final"].reshape(D)

    wait_for(inflight[-1:], upd[0][0])
    o_in = finish("w_in", w_in, m_w_in, v_w_in, 256)
    o_out = finish("w_out", w_out, m_w_out, v_w_out, 64)

    bigs = dict(w_in=o_in, w_out=o_out, w_gate=o_gate, w_up=o_up, w_down=o_down)
    order = ["meta_tokens", "norm_mix", "w_in", "conv_w", "conv_b", "gate_a_w", "gate_a_b", "gate_x_w", "gate_x_b", "lru_lambda",
             "lru_out_norm", "ret_out_norm", "w_out", "norm_ffn", "w_gate", "w_up", "w_down", "norm_final"]
    grads = [bigs[n][0] if n in bigs else small_g[n] for n in order]
    rest = [[bigs[n][k + 1] if n in bigs else small_out[k][n] for n in order] for k in range(3)]
    return (loss, grad_x, *grads, *rest[0], *rest[1], *rest[2])
```

```python
import numpy as np
import jax
import jax.numpy as jnp
from jax import lax
from jax.experimental import pallas as pl
from jax.experimental.pallas import tpu as pltpu

F32, BF16 = jnp.float32, jnp.bfloat16
MXU_DTYPE = BF16
WIRE_DTYPE = BF16

D = 1024
SEQ = 2048
DEPTH = 4
N_META = 16
CH = 128
PAD = (-(SEQ + N_META)) % CH
T = SEQ + N_META + PAD
NCH = T // CH
X0 = PAD + N_META
D_LRU = 512
LRU_BLOCKS = 8
LRU_BD = 64
CONV_W = 4
LRU_C = 8.0
D_RET = 512
HEADS = 4
HD = 128
ROPE_BASE = 10000.0
D_IN = 3072
D_FF = 2816
NDEV = 8
IN_SH = D_IN // NDEV
FF_SH = D_FF // NDEV
FF_SHP = 384
D_FFP = NDEV * FF_SHP
OUT_SH = D // NDEV
EPS = 1e-6
TM = 544
VMEM_LIMIT = 56 * 2**20
MESH = pl.DeviceIdType.MESH

ADAM_LR, ADAM_B1, ADAM_B2, ADAM_EPS, ADAM_WD, ADAM_STEP = 0.001, 0.9, 0.999, 1e-08, 0.01, 10

NN = ((1,), (0,))
NT = ((1,), (1,))
TN = ((0,), (0,))


def _dot(a, b, dims):
    return lax.dot_general(a.astype(MXU_DTYPE), b.astype(MXU_DTYPE), (dims, ((), ())), preferred_element_type=F32)


def _sds(shape, dtype):
    return jax.ShapeDtypeStruct(shape, dtype)


def _params(sem=None):
    return pltpu.CompilerParams(dimension_semantics=sem, vmem_limit_bytes=VMEM_LIMIT)


def _full(shape):
    n = len(shape)
    return pl.BlockSpec(shape, lambda *_: (0,) * n)


def rmsnorm_fwd(h, gain, name):
    def body(h_ref, g_ref, o_ref):
        x = h_ref[...]
        ms = jnp.mean(x * x, axis=-1, keepdims=True)
        o_ref[...] = (x * lax.rsqrt(ms + EPS) * g_ref[...]).astype(o_ref.dtype)

    return pl.pallas_call(
        body, name=name, grid=(T // TM,),
        in_specs=[pl.BlockSpec((TM, D), lambda i: (i, 0)), _full((1, D))],
        out_specs=pl.BlockSpec((TM, D), lambda i: (i, 0)),
        out_shape=_sds((T, D), MXU_DTYPE), compiler_params=_params(("parallel",)),
    )(h, gain)


def rmsnorm_bwd(h, gain, dhn, dres, name):
    def body(h_ref, g_ref, dhn_ref, dres_ref, dh_ref, dhb_ref, dg_ref):
        x = h_ref[...]
        rstd = lax.rsqrt(jnp.mean(x * x, axis=-1, keepdims=True) + EPS)
        xhat = x * rstd
        dy = dhn_ref[...]
        dyg = dy * g_ref[...]
        dh = dres_ref[...] + rstd * (dyg - xhat * jnp.mean(dyg * xhat, axis=-1, keepdims=True))
        dh_ref[...] = dh
        dhb_ref[...] = dh.astype(dhb_ref.dtype)

        @pl.when(pl.program_id(0) == 0)
        def _():
            dg_ref[...] = jnp.zeros_like(dg_ref)
        dg_ref[...] += jnp.sum(dy * xhat, axis=0, keepdims=True)

    row = pl.BlockSpec((TM, D), lambda i: (i, 0))
    return pl.pallas_call(
        body, name=name, grid=(T // TM,),
        in_specs=[row, _full((1, D)), row, row],
        out_specs=[row, row, _full((1, D))],
        out_shape=[_sds((T, D), F32), _sds((T, D), MXU_DTYPE), _sds((1, D), F32)], compiler_params=_params(("arbitrary",)),
    )(h, gain, dhn, dres)


def loss_head(h, gain, target, name):
    def body(h_ref, g_ref, t_ref, loss_ref, dh_ref, dhb_ref, dg_ref):
        i = pl.program_id(0)

        @pl.when(i == 0)
        def _():
            loss_ref[...] = jnp.zeros_like(loss_ref)
            dg_ref[...] = jnp.zeros_like(dg_ref)
            dh_ref[...] = jnp.zeros_like(dh_ref)
            dhb_ref[...] = jnp.zeros_like(dhb_ref)

        @pl.when(i > 0)
        def _():
            x = h_ref[...]
            g = g_ref[...]
            rstd = lax.rsqrt(jnp.mean(x * x, axis=-1, keepdims=True) + EPS)
            xhat = x * rstd
            err = xhat * g - t_ref[...]
            loss_ref[...] += 0.5 * jnp.sum(jnp.mean(err * err, axis=-1, keepdims=True), axis=0, keepdims=True)
            dy = err * (1.0 / D)
            dyg = dy * g
            dh = rstd * (dyg - xhat * jnp.mean(dyg * xhat, axis=-1, keepdims=True))
            dh_ref[...] = dh
            dhb_ref[...] = dh.astype(dhb_ref.dtype)
            dg_ref[...] += jnp.sum(dy * xhat, axis=0, keepdims=True)

    row = pl.BlockSpec((CH, D), lambda i: (i, 0))
    return pl.pallas_call(
        body, name=name, grid=(NCH,),
        in_specs=[row, _full((1, D)), pl.BlockSpec((CH, D), lambda i: (jnp.maximum(i - 1, 0), 0))],
        out_specs=[_full((8, 128)), row, row, _full((1, D))],
        out_shape=[_sds((8, 128), F32), _sds((T, D), F32), _sds((T, D), MXU_DTYPE), _sds((1, D), F32)],
        compiler_params=_params(("arbitrary",)),
    )(h, gain, target)


PAIR = 2 * IN_SH
NPAIR = NDEV // 2
BN = 256
FB = 512


def _pair_cols(w_ref):
    return jnp.concatenate([w_ref[0], w_ref[1]], axis=1)


W_PAIR = lambda k: pl.BlockSpec((2, k, IN_SH), lambda j: (j, 0, 0))
COLS_PAIR = pl.BlockSpec((T, PAIR), lambda j: (0, j))
ANYSPEC = pl.BlockSpec(memory_space=pl.ANY)


def mm_blocked_nn(a, w, out_dtype, name):
    k = a.shape[1]

    def body(a_ref, w_ref, o_ref):
        o_ref[:PAD, :] = jnp.zeros((PAD, PAIR), o_ref.dtype)
        o_ref[PAD:, :] = _dot(a_ref[PAD:, :], _pair_cols(w_ref), NN).astype(o_ref.dtype)

    return pl.pallas_call(
        body, name=name, grid=(NPAIR,),
        in_specs=[_full((T, k)), W_PAIR(k)], out_specs=COLS_PAIR,
        out_shape=_sds((T, NDEV * IN_SH), out_dtype), compiler_params=_params(("parallel",)),
    )(a, w)


def mm_nn_res(a, w, res, after, name):
    k = a.shape[1]

    def body(a_ref, w_ref, r_ref, after_ref, o_ref):
        del after_ref
        o_ref[:PAD, :] = r_ref[:PAD, :]
        o_ref[PAD:, :] = r_ref[PAD:, :] + _dot(a_ref[PAD:, :], w_ref[...], NN)

    col = pl.BlockSpec((T, BN), lambda j: (0, j))
    return pl.pallas_call(
        body, name=name, grid=(D // BN,),
        in_specs=[_full((T, k)), pl.BlockSpec((k, BN), lambda j: (0, j)), col, ANYSPEC], out_specs=col,
        out_shape=_sds((T, D), F32), compiler_params=_params(("parallel",)),
    )(a, w, res, after)


def ffn_up(hn, wg, wu, name):
    def body(a_ref, wg_ref, wu_ref, dg_ref, du_ref, act_ref):
        a = a_ref[PAD:, :]
        for ref in (dg_ref, du_ref, act_ref):
            ref[:PAD, :] = jnp.zeros((PAD, FB), ref.dtype)
        for c in range(FB // BN):
            cols = slice(BN * c, BN * (c + 1))
            g = _dot(a, wg_ref[cols, :], NT)
            u = _dot(a, wu_ref[cols, :], NT)
            sg = jax.nn.sigmoid(g)
            silu = g * sg
            dg_ref[PAD:, cols] = (u * (sg * (1.0 + g * (1.0 - sg)))).astype(dg_ref.dtype)
            du_ref[PAD:, cols] = silu.astype(du_ref.dtype)
            act_ref[PAD:, cols] = (silu * u).astype(act_ref.dtype)

    wspec = pl.BlockSpec((FB, D), lambda j: (j, 0))
    ospec = pl.BlockSpec((T, FB), lambda j: (0, j))
    return pl.pallas_call(
        body, name=name, grid=(D_FFP // FB,),
        in_specs=[_full((T, D)), wspec, wspec], out_specs=[ospec] * 3,
        out_shape=[_sds((T, D_FFP), MXU_DTYPE)] * 3, compiler_params=_params(("parallel",)),
    )(hn, wg, wu)


def ffn_down_bwd(dh, wd, dact_dgate, dact_dup, after, name):
    def body(dh_ref, wd_ref, g_ref, u_ref, after_ref, dg_ref, du_ref):
        del after_ref
        dh = dh_ref[PAD:, :]
        for ref in (dg_ref, du_ref):
            ref[:PAD, :] = jnp.zeros((PAD, FB), ref.dtype)
        for c in range(FB // BN):
            cols = slice(BN * c, BN * (c + 1))
            dact = _dot(dh, wd_ref[cols, :], NT)
            dg_ref[PAD:, cols] = (dact * g_ref[PAD:, cols].astype(F32)).astype(dg_ref.dtype)
            du_ref[PAD:, cols] = (dact * u_ref[PAD:, cols].astype(F32)).astype(du_ref.dtype)

    blk = pl.BlockSpec((T, FB), lambda j: (0, j))
    return pl.pallas_call(
        body, name=name, grid=(D_FFP // FB,),
        in_specs=[_full((T, D)), pl.BlockSpec((FB, D), lambda j: (j, 0)), blk, blk, ANYSPEC],
        out_specs=[blk, blk],
        out_shape=[_sds((T, D_FFP), MXU_DTYPE)] * 2, compiler_params=_params(("parallel",)),
    )(dh, wd, dact_dgate, dact_dup, after)


def mm_blocked_nt(pairs, after, name):
    n = len(pairs)

    def body(*refs):
        o_ref = refs[2 * n + 1]

        @pl.when(pl.program_id(0) == 0)
        def _():
            o_ref[...] = jnp.zeros_like(o_ref)
        for p in range(n):
            o_ref[PAD:, :] += _dot(refs[2 * p][PAD:, :], _pair_cols(refs[2 * p + 1]), NT)

    specs, args = [], []
    for a, w in pairs:
        specs += [COLS_PAIR, W_PAIR(D)]
        args += [a, w]
    return pl.pallas_call(
        body, name=name, grid=(NPAIR,), in_specs=specs + [ANYSPEC], out_specs=_full((T, D)),
        out_shape=_sds((T, D), F32), compiler_params=_params(("arbitrary",)),
    )(*args, after)


def mm_tn_two(a1, a2, b, bm, after, name):
    m = a1.shape[1]

    def body(a1_ref, a2_ref, b_ref, after_ref, o1_ref, o2_ref):
        del after_ref
        b = b_ref[...]
        o1_ref[...] = _dot(a1_ref[...], b, TN).astype(o1_ref.dtype)
        o2_ref[...] = _dot(a2_ref[...], b, TN).astype(o2_ref.dtype)

    blk = pl.BlockSpec((T, bm), lambda i: (0, i))
    out = pl.BlockSpec((bm, D), lambda i: (i, 0))
    return pl.pallas_call(
        body, name=name, grid=(m // bm,),
        in_specs=[blk, blk, _full((T, D)), ANYSPEC], out_specs=[out, out],
        out_shape=[_sds((m, D), WIRE_DTYPE)] * 2, compiler_params=_params(("parallel",)),
    )(a1, a2, b, after)


def out_proj_bwd(dh, w, ymix, after, name):
    def body(dh_ref, w_ref, y_ref, after_ref, dy_ref, dw_ref):
        del after_ref
        dh_ = dh_ref[PAD:, :]
        dy_ref[:PAD, :] = jnp.zeros((PAD, BN), dy_ref.dtype)
        dy_ref[PAD:, :] = _dot(dh_, w_ref[...], NT)
        dw_ref[...] = _dot(y_ref[PAD:, :], dh_, TN).astype(dw_ref.dtype)

    return pl.pallas_call(
        body, name=name, grid=(D // BN,),
        in_specs=[_full((T, D)), pl.BlockSpec((BN, D), lambda j: (j, 0)), pl.BlockSpec((T, BN), lambda j: (0, j)), ANYSPEC],
        out_specs=[pl.BlockSpec((T, BN), lambda j: (0, j)), pl.BlockSpec((BN, D), lambda j: (j, 0))],
        out_shape=[_sds((T, D), F32), _sds((D, D), WIRE_DTYPE)], compiler_params=_params(("parallel",)),
    )(dh, w, ymix, after)


def mm_rows_nn(pairs, after, name):
    n = len(pairs)

    def body(*refs):
        o_ref = refs[2 * n + 1]

        @pl.when(pl.program_id(0) == 0)
        def _():
            o_ref[...] = jnp.zeros_like(o_ref)
        for p in range(n):
            o_ref[PAD:, :] += _dot(refs[2 * p][PAD:, :], refs[2 * p + 1][...], NN)

    specs, args = [], []
    for a, w in pairs:
        specs += [pl.BlockSpec((T, FB), lambda j: (0, j)), pl.BlockSpec((FB, D), lambda j: (j, 0))]
        args += [a, w]
    return pl.pallas_call(
        body, name=name, grid=(D_FFP // FB,), in_specs=specs + [ANYSPEC], out_specs=_full((T, D)),
        out_shape=_sds((T, D), F32), compiler_params=_params(("arbitrary",)),
    )(*args, after)


def mm_tn_blocked(a, b, name):
    def body(a_ref, b_ref, o_ref):
        o = _dot(a_ref[...], b_ref[...], TN).astype(o_ref.dtype)
        o_ref[0] = o[:, :IN_SH]
        o_ref[1] = o[:, IN_SH:]

    return pl.pallas_call(
        body, name=name, grid=(NPAIR,),
        in_specs=[_full((T, D)), COLS_PAIR], out_specs=W_PAIR(D),
        out_shape=_sds((NDEV, D, IN_SH), WIRE_DTYPE), compiler_params=_params(("parallel",)),
    )(a, b)


def mm_tn(a, b, bm, after, name):
    m = a.shape[1]

    def body(a_ref, b_ref, after_ref, o_ref):
        del after_ref
        o_ref[...] = _dot(a_ref[...], b_ref[...], TN).astype(o_ref.dtype)

    return pl.pallas_call(
        body, name=name, grid=(m // bm,),
        in_specs=[pl.BlockSpec((T, bm), lambda i: (0, i)), _full((T, D)), ANYSPEC],
        out_specs=pl.BlockSpec((bm, D), lambda i: (i, 0)),
        out_shape=_sds((m, D), WIRE_DTYPE), compiler_params=_params(("parallel",)),
    )(a, b, after)


def _softplus_neg(lam):
    return jnp.maximum(-lam, 0.0) + jnp.log1p(jnp.exp(-jnp.abs(lam)))


def _lru_gates(pa, px, xc, lam):
    r = jax.nn.sigmoid(pa)
    ig = jax.nn.sigmoid(px)
    sp = _softplus_neg(lam)
    log_a = -LRU_C * r * sp
    a = jnp.exp(log_a)
    mult = jnp.sqrt(-jnp.tanh(log_a) * (a * a + 1.0))
    return a, mult * (ig * xc), (r, ig, sp, mult)


def _lru_gates_vjp(da, db, xc, lam, a, r, ig, sp, mult):
    dmult = db * (ig * xc)
    du = db * mult
    dlog_a = da * a - dmult * (a * a) / mult
    dr = dlog_a * (-LRU_C * sp)
    dlam = jnp.sum(dlog_a * (-LRU_C * r), axis=0, keepdims=True) * (-jax.nn.sigmoid(-lam))
    dpa = dr * (r * (1.0 - r))
    dpx = (du * xc) * (ig * (1.0 - ig))
    return dpa, dpx, du * ig, dlam


def _lru_out(h, g, gain):
    z = h * jax.nn.gelu(g)
    return z * lax.rsqrt(jnp.mean(z * z, axis=-1, keepdims=True) + EPS) * gain


def _conv_taps(x, xprev, row):
    taps = [x]
    for s in range(1, CONV_W):
        taps.append(jnp.where(row < s, pltpu.roll(xprev, s, 0), pltpu.roll(x, s, 0)))
    return taps


def _conv(taps, cw_ref, cb):
    xc = cb + cw_ref[CONV_W - 1:CONV_W, :] * taps[0]
    for s in range(1, CONV_W):
        xc = xc + cw_ref[CONV_W - 1 - s:CONV_W - s, :] * taps[s]
    return xc


def _lru_fwd_block(i, x_ref, g_ref, cw_ref, cb_ref, wa_ref, ba_ref, wx_ref, bx_ref, lam_ref, gain_ref, y_ref, h_ref,
                   xprev_scr, a_scr, b_scr, carry_scr):
    @pl.when(i == 0)
    def _():
        xprev_scr[...] = jnp.zeros_like(xprev_scr)
        carry_scr[...] = jnp.zeros_like(carry_scr)

    x = x_ref[...]
    row = lax.broadcasted_iota(jnp.int32, (CH, D_LRU), 0)
    xc = _conv(_conv_taps(x, xprev_scr[...], row), cw_ref, cb_ref[...])
    pa = _dot(xc, wa_ref[...], NN) + ba_ref[...]
    px = _dot(xc, wx_ref[...], NN) + bx_ref[...]
    a, b, _ = _lru_gates(pa, px, xc, lam_ref[...])
    a_scr[...] = a
    b_scr[...] = jnp.where(i * CH + row >= PAD, b, 0.0)
    h = carry_scr[...]
    for t in range(CH):
        h = a_scr[t:t + 1, :] * h + b_scr[t:t + 1, :]
        h_ref[t:t + 1, :] = h
    carry_scr[...] = h
    xprev_scr[...] = x
    y_ref[:, :D_LRU] = _lru_out(h_ref[...], g_ref[...], gain_ref[...]).astype(y_ref.dtype)


LRU_VEC_ROWS = 16


def _lru_bwd_block(ib, x_ref, xp_ref, g_ref, h_ref, hp_ref, dy_ref, cw_ref, cb_ref, wa_ref, ba_ref, wx_ref, bx_ref, lam_ref,
                   gain_ref, dp_ref, vec_ref, dwa_ref, dwx_ref, a_scr, dh_scr, g_scr, carry_scr, dxcn_scr):
    @pl.when(ib == NCH - 1)
    def _():
        carry_scr[...] = jnp.zeros_like(carry_scr)
        dxcn_scr[...] = jnp.zeros_like(dxcn_scr)
        vec_ref[...] = jnp.zeros_like(vec_ref)
        dwa_ref[...] = jnp.zeros_like(dwa_ref)
        dwx_ref[...] = jnp.zeros_like(dwx_ref)

    x = x_ref[...]
    row = lax.broadcasted_iota(jnp.int32, (CH, D_LRU), 0)
    valid = ib * CH + row >= PAD
    taps = _conv_taps(x, xp_ref[...], row)
    xc = _conv(taps, cw_ref, cb_ref[...])
    pa = _dot(xc, wa_ref[...], NN) + ba_ref[...]
    px = _dot(xc, wx_ref[...], NN) + bx_ref[...]
    a, _, gate_parts = _lru_gates(pa, px, xc, lam_ref[...])
    h = h_ref[...]
    _, vjp_out = jax.vjp(_lru_out, h, g_ref[...], gain_ref[...])
    dh, dg, dgain = vjp_out(dy_ref[:, :D_LRU].astype(F32))
    a_scr[...] = a
    dh_scr[...] = dh
    c = carry_scr[...]
    for t in range(CH - 1, -1, -1):
        gt = dh_scr[t:t + 1, :] + c
        g_scr[t:t + 1, :] = gt
        c = a_scr[t:t + 1, :] * gt
    carry_scr[...] = c
    gg = g_scr[...]
    hprev = jnp.where(row < 1, pltpu.roll(hp_ref[...], 1, 0), pltpu.roll(h, 1, 0))
    da = jnp.where(valid, gg * hprev, 0.0)
    db = jnp.where(valid, gg, 0.0)
    dpa, dpx, dxc, dlam = _lru_gates_vjp(da, db, xc, lam_ref[...], a, *gate_parts)
    dxc = dxc + _dot(dpa, wa_ref[...], NT) + _dot(dpx, wx_ref[...], NT)
    dwa_ref[...] += _dot(xc, dpa, TN)
    dwx_ref[...] += _dot(xc, dpx, TN)
    for s in range(CONV_W):
        vec_ref[CONV_W - 1 - s:CONV_W - s, :] += jnp.sum(dxc * taps[s], axis=0, keepdims=True)
    vec_ref[4:5, :] += jnp.sum(dxc, axis=0, keepdims=True)
    vec_ref[5:6, :] += jnp.sum(dpa, axis=0, keepdims=True)
    vec_ref[6:7, :] += jnp.sum(dpx, axis=0, keepdims=True)
    vec_ref[7:8, :] += dlam
    vec_ref[8:9, :] += dgain
    dxn = dxcn_scr[...]
    dx = cw_ref[CONV_W - 1:CONV_W, :] * dxc
    for s in range(1, CONV_W):
        ahead = jnp.where(row >= CH - s, pltpu.roll(dxn, CH - s, 0), pltpu.roll(dxc, CH - s, 0))
        dx = dx + cw_ref[CONV_W - 1 - s:CONV_W - s, :] * ahead
    dxcn_scr[...] = dxc
    dp_ref[:, :D_LRU] = jnp.where(valid, dx, 0.0).astype(dp_ref.dtype)
    dp_ref[:, D_LRU:2 * D_LRU] = dg.astype(dp_ref.dtype)


def _ret_tables():
    half = HD // 2
    pos = jnp.arange(T, dtype=F32) - float(PAD)
    inv = ROPE_BASE ** (-jnp.arange(half, dtype=F32) / half)
    ang = pos[:, None] * inv[None, :]
    cos = jnp.concatenate([jnp.cos(ang), jnp.cos(ang)], axis=-1)
    sin = jnp.concatenate([-jnp.sin(ang), jnp.sin(ang)], axis=-1)
    log_g = jnp.log(1.0 - 2.0 ** (-5.0 - jnp.arange(HEADS, dtype=F32)))
    idx = jnp.arange(CH, dtype=F32)
    diff = idx[:, None] - idx[None, :]
    dmask = jnp.where(diff[None] >= 0, jnp.exp(jnp.maximum(diff, 0.0)[None] * log_g[:, None, None]), 0.0)
    xi = jnp.exp((idx + 1.0)[None, :] * log_g[:, None])
    zeta = jnp.exp((CH - 1.0 - idx)[None, :] * log_g[:, None])
    xi = jnp.broadcast_to(xi[:, :, None], (HEADS, CH, HD))
    zeta = jnp.broadcast_to(zeta[:, :, None], (HEADS, CH, HD))
    return cos, sin, dmask, xi, zeta


def _chunk_decay():
    log_g = np.log(np.float32(1.0) - np.float32(2.0) ** (np.float32(-5.0) - np.arange(HEADS, dtype=np.float32)))
    return [float(v) for v in np.exp(np.float32(CH) * log_g.astype(np.float32))]


def _rope(x, cos, sin):
    return x * cos + pltpu.roll(x, HD // 2, 1) * sin


def mix_fwd(proj, cw, cb, wa, ba, wx, bx, lam, gain, tables, ret_gain, after, name):
    cos, sin, dmask, xi, zeta = tables
    gch = _chunk_decay()
    scale = HD ** -0.5

    def body(x_ref, gl_ref, cw_ref, cb_ref, wa_ref, ba_ref, wx_ref, bx_ref, lam_ref, lgain_ref,
             q_ref, k_ref, v_ref, g_ref, cos_ref, sin_ref, dm_ref, xi_ref, zt_ref, gain_ref, after_ref,
             y_ref, h_ref, st_ref, xprev_scr, a_scr, b_scr, carry_scr, s_scr):
        del after_ref

        @pl.when(pl.program_id(0) == 0)
        def _():
            s_scr[...] = jnp.zeros_like(s_scr)

        _lru_fwd_block(pl.program_id(0), x_ref, gl_ref, cw_ref, cb_ref, wa_ref, ba_ref, wx_ref, bx_ref, lam_ref, lgain_ref,
                       y_ref, h_ref, xprev_scr, a_scr, b_scr, carry_scr)
        cs, sn = cos_ref[...], sin_ref[...]
        hs = range(HEADS)
        sl = [slice(HD * h, HD * (h + 1)) for h in hs]
        qr = [_rope(q_ref[:, sl[h]], cs, sn).astype(MXU_DTYPE) for h in hs]
        kf = [_rope(k_ref[:, sl[h]], cs, sn) * scale for h in hs]
        kr = [kf[h].astype(MXU_DTYPE) for h in hs]
        v = [v_ref[:, sl[h]].astype(MXU_DTYPE) for h in hs]
        s = [s_scr[h] for h in hs]
        for h in hs:
            st_ref[h] = s[h]
        sc = [_dot(qr[h], kr[h], NT) * dm_ref[h] for h in hs]
        cross = [_dot(qr[h], s[h], NN) * xi_ref[h] for h in hs]
        for h in hs:
            s_scr[h] = s[h] * gch[h] + _dot(kf[h] * zt_ref[h], v[h], TN)
        y = [_dot(sc[h], v[h], NN) + cross[h] for h in hs]
        yc = [y[h] - jnp.mean(y[h], axis=-1, keepdims=True) for h in hs]
        yn = [yc[h] * lax.rsqrt(jnp.mean(yc[h] * yc[h], axis=-1, keepdims=True) + EPS) for h in hs]
        for h in hs:
            so = slice(D_LRU + HD * h, D_LRU + HD * (h + 1))
            y_ref[:, so] = (jax.nn.silu(g_ref[:, sl[h]]) * (yn[h] * gain_ref[:, sl[h]])).astype(y_ref.dtype)

    def col(c):
        return pl.BlockSpec((CH, D_RET), lambda n: (n, c))

    tab = pl.BlockSpec((CH, HD), lambda n: (n, 0))
    cst = _full((HEADS, CH, HD))
    vec = _full((1, D_LRU))
    mat = _full((D_LRU, D_LRU))
    blockbuf = pltpu.VMEM((CH, D_LRU), F32)
    return pl.pallas_call(
        body, name=name, grid=(NCH,),
        in_specs=[col(0), col(1), _full((CONV_W, D_LRU)), vec, mat, vec, mat, vec, vec, vec,
                  col(2), col(3), col(4), col(5), tab, tab, cst, cst, cst, _full((1, D_RET)),
                  pl.BlockSpec(memory_space=pl.ANY)],
        out_specs=[pl.BlockSpec((CH, D), lambda n: (n, 0)), col(0), pl.BlockSpec((None, HEADS, HD, HD), lambda n: (n, 0, 0, 0))],
        out_shape=[_sds((T, D), MXU_DTYPE), _sds((T, D_LRU), F32), _sds((NCH, HEADS, HD, HD), F32)],
        scratch_shapes=[blockbuf, blockbuf, blockbuf, pltpu.VMEM((1, D_LRU), F32), pltpu.VMEM((HEADS, HD, HD), F32)],
        compiler_params=_params(("arbitrary",)),
    )(proj, proj, cw, cb, wa, ba, wx, bx, lam, gain, proj, proj, proj, proj, cos, sin, dmask, xi, zeta, ret_gain, after)


def mix_bwd(proj, hst, states, dymix, cw, cb, wa, ba, wx, bx, lam, gain, tables, ret_gain, after, name):
    cos, sin, dmask, xi, zeta = tables
    gch = _chunk_decay()
    scale = HD ** -0.5
    last = NCH - 1

    def body(x_ref, xp_ref, gl_ref, h_ref, hp_ref, cw_ref, cb_ref, wa_ref, ba_ref, wx_ref, bx_ref, lam_ref, lgain_ref,
             q_ref, k_ref, v_ref, g_ref, st_ref, dy_ref, cos_ref, sin_ref, dm_ref, xi_ref, zt_ref, gain_ref, after_ref,
             dp_ref, vec_ref, dwa_ref, dwx_ref, dgain_ref, a_scr, dh_scr, g_scr, carry_scr, dxcn_scr, ds_scr):
        del after_ref

        @pl.when(pl.program_id(0) == 0)
        def _():
            ds_scr[...] = jnp.zeros_like(ds_scr)
            dgain_ref[...] = jnp.zeros_like(dgain_ref)

        _lru_bwd_block(last - pl.program_id(0), x_ref, xp_ref, gl_ref, h_ref, hp_ref, dy_ref, cw_ref, cb_ref, wa_ref, ba_ref,
                       wx_ref, bx_ref, lam_ref, lgain_ref, dp_ref, vec_ref, dwa_ref, dwx_ref, a_scr, dh_scr, g_scr, carry_scr,
                       dxcn_scr)
        cs, sn = cos_ref[...], sin_ref[...]
        hs = range(HEADS)
        sl = [slice(HD * h, HD * (h + 1)) for h in hs]

        def out(j, h):
            return slice(2 * D_LRU + j * D_RET + HD * h, 2 * D_LRU + j * D_RET + HD * (h + 1))

        b16 = lambda xs: [x.astype(MXU_DTYPE) for x in xs]
        qr = b16([_rope(q_ref[:, sl[h]], cs, sn) for h in hs])
        kf = [_rope(k_ref[:, sl[h]], cs, sn) * scale for h in hs]
        kr = b16(kf)
        kz = b16([kf[h] * zt_ref[h] for h in hs])
        v = b16([v_ref[:, sl[h]] for h in hs])
        s = b16([st_ref[h] for h in hs])
        ds = [ds_scr[h] for h in hs]
        dsb = b16(ds)
        sc = [_dot(qr[h], kr[h], NT) * dm_ref[h] for h in hs]
        scb = b16(sc)
        y = [_dot(scb[h], v[h], NN) + _dot(qr[h], s[h], NN) * xi_ref[h] for h in hs]
        yc = [y[h] - jnp.mean(y[h], axis=-1, keepdims=True) for h in hs]
        rstd = [lax.rsqrt(jnp.mean(yc[h] * yc[h], axis=-1, keepdims=True) + EPS) for h in hs]
        yn = [yc[h] * rstd[h] for h in hs]
        dy = []
        for h in hs:
            g = g_ref[:, sl[h]]
            gain = gain_ref[:, sl[h]]
            sg = jax.nn.sigmoid(g)
            silu = g * sg
            dout = dy_ref[:, D_LRU + HD * h:D_LRU + HD * (h + 1)].astype(F32)
            dgain_ref[:, sl[h]] += jnp.sum(dout * silu * yn[h], axis=0, keepdims=True)
            dp_ref[:, out(3, h)] = (dout * yn[h] * gain * (sg * (1.0 + g * (1.0 - sg)))).astype(dp_ref.dtype)
            dyn = dout * silu * gain
            dy.append(rstd[h] * (dyn - jnp.mean(dyn, axis=-1, keepdims=True)
                                 - yn[h] * jnp.mean(dyn * yn[h], axis=-1, keepdims=True)))
        dyb = b16(dy)
        dqs = b16([dy[h] * xi_ref[h] for h in hs])
        dp = b16([_dot(dyb[h], v[h], NT) * dm_ref[h] for h in hs])
        dv = [_dot(scb[h], dyb[h], TN) + _dot(kz[h], dsb[h], NN) for h in hs]
        dqr = [_dot(dp[h], kr[h], NN) + _dot(dqs[h], s[h], NT) for h in hs]
        dkr = [_dot(dp[h], qr[h], TN) + _dot(v[h], dsb[h], NT) * zt_ref[h] for h in hs]
        for h in hs:
            ds_scr[h] = gch[h] * ds[h] + _dot(qr[h], dqs[h], TN)
        for h in hs:
            dp_ref[:, out(0, h)] = (dqr[h] * cs + pltpu.roll(dqr[h] * sn, HD // 2, 1)).astype(dp_ref.dtype)
            dp_ref[:, out(1, h)] = ((dkr[h] * cs + pltpu.roll(dkr[h] * sn, HD // 2, 1)) * scale).astype(dp_ref.dtype)
            dp_ref[:, out(2, h)] = dv[h].astype(dp_ref.dtype)

    def col(c, shift=0):
        return pl.BlockSpec((CH, D_RET), lambda n: (jnp.maximum(last - n - shift, 0), c))

    tab = pl.BlockSpec((CH, HD), lambda n: (last - n, 0))
    cst = _full((HEADS, CH, HD))
    vec = _full((1, D_LRU))
    mat = _full((D_LRU, D_LRU))
    blockbuf = pltpu.VMEM((CH, D_LRU), F32)
    return pl.pallas_call(
        body, name=name, grid=(NCH,),
        in_specs=[col(0), col(0, 1), col(1), col(0), col(0, 1), _full((CONV_W, D_LRU)), vec, mat, vec, mat, vec, vec, vec,
                  col(2), col(3), col(4), col(5), pl.BlockSpec((None, HEADS, HD, HD), lambda n: (last - n, 0, 0, 0)),
                  pl.BlockSpec((CH, D), lambda n: (last - n, 0)), tab, tab, cst, cst, cst, _full((1, D_RET)),
                  pl.BlockSpec(memory_space=pl.ANY)],
        out_specs=[pl.BlockSpec((CH, D_IN), lambda n: (last - n, 0)), _full((LRU_VEC_ROWS, D_LRU)), mat, mat,
                   _full((1, D_RET))],
        out_shape=[_sds((T, D_IN), MXU_DTYPE), _sds((LRU_VEC_ROWS, D_LRU), F32), _sds((D_LRU, D_LRU), F32),
                   _sds((D_LRU, D_LRU), F32), _sds((1, D_RET), F32)],
        scratch_shapes=[blockbuf, blockbuf, blockbuf, pltpu.VMEM((1, D_LRU), F32), blockbuf,
                        pltpu.VMEM((HEADS, HD, HD), F32)],
        compiler_params=_params(("arbitrary",)),
    )(proj, proj, proj, hst, hst, cw, cb, wa, ba, wx, bx, lam, gain, proj, proj, proj, proj, states, dymix,
      cos, sin, dmask, xi, zeta, ret_gain, after)


HBM = pl.BlockSpec(memory_space=pltpu.HBM)


def _place():
    return lax.axis_index("x"), lax.axis_index("y"), lax.axis_index("c")


def all_gather(arrs, after, name):
    n = len(arrs)

    def body(*refs):
        ins, outs = refs[:n], refs[n + 1:2 * n + 1]
        send_sems, recv_sems, local_sems = refs[2 * n + 1:]
        x, y, c = _place()
        me, sibling = (x, y, c), (x, y, 1 - c)
        chips = [(1 - x, y), (x, 1 - y), (1 - x, 1 - y)]

        def copy(a, k, block, to, src=None):
            px, py, pc = block
            dst = outs[a].at[4 * px + 2 * py + pc]
            return pltpu.make_async_remote_copy(
                src_ref=dst if src is None else src, dst_ref=dst, send_sem=send_sems.at[a, k], recv_sem=recv_sems.at[a, k],
                device_id=to, device_id_type=MESH)

        mine = [pltpu.make_async_copy(ins[a], outs[a].at[4 * x + 2 * y + c], local_sems.at[a]) for a in range(n)]
        for cp in mine:
            cp.start()
        first = []
        for a in range(n):
            first.append(copy(a, 0, me, sibling, src=ins[a]))
            first += [copy(a, 1 + j, me, (*chip, c), src=ins[a]) for j, chip in enumerate(chips)]
        for cp in first:
            cp.start()
        passed = []
        for j, chip in enumerate(chips):
            for a in range(n):
                copy(a, 1 + j, (*chip, c), me).wait_recv()
                passed.append(copy(a, 4 + j, (*chip, c), sibling))
                passed[-1].start()
        for a in range(n):
            copy(a, 0, sibling, me).wait_recv()
            for j, chip in enumerate(chips):
                copy(a, 4 + j, (*chip, 1 - c), me).wait_recv()
        for cp in first + passed:
            cp.wait_send()
        for cp in mine:
            cp.wait()

    return pl.pallas_call(
        body, name=name,
        in_specs=[HBM] * n + [pl.BlockSpec(memory_space=pl.ANY)], out_specs=[HBM] * n,
        out_shape=[_sds((NDEV,) + a.shape, a.dtype) for a in arrs],
        scratch_shapes=[pltpu.SemaphoreType.DMA((n, 7)), pltpu.SemaphoreType.DMA((n, 7)), pltpu.SemaphoreType.DMA((n,))],
    )(*arrs, after)


SEM = pl.BlockSpec(memory_space=pltpu.SEMAPHORE)
ANY = pl.BlockSpec(memory_space=pl.ANY)
EFFECT = pltpu.SideEffectType.DATAFLOW_SIDE_EFFECTING


def _hbm(a):
    return pltpu.with_memory_space_constraint(a, pltpu.HBM)


def _hbm_like(arrs):
    return [pltpu.HBM(a.shape, a.dtype) for a in arrs]


def _dma_sems(count):
    return [pltpu.SemaphoreType.DMA(())] * count


def _ag_copy(lands, send_sems, recv_sems, per):
    def copy(a, k, block, to, src=None):
        px, py, pc = block
        dst = lands[a].at[4 * px + 2 * py + pc]
        return pltpu.make_async_remote_copy(
            src_ref=dst if src is None else src, dst_ref=dst, send_sem=send_sems[a * per + k], recv_sem=recv_sems[a * per + k],
            device_id=to, device_id_type=MESH)
    return copy


def to_wire(sel, w_in, w_out, w_gate, w_up, w_down, name):
    ffpad = FF_SHP - FF_SH

    def body(sel_ref, i_ref, o_ref, g_ref, u_ref, d_ref, oi, oo, og, ou, od):
        del sel_ref
        oi[...] = i_ref[...].astype(oi.dtype)
        oo[...] = o_ref[...].astype(oo.dtype)
        for src, dst in ((g_ref, og), (u_ref, ou), (d_ref, od)):
            dst[:FF_SH, :] = src[...].astype(dst.dtype)
            dst[FF_SH:, :] = jnp.zeros((ffpad, D), dst.dtype)

    shapes_in = [(D, IN_SH), (OUT_SH, D), (FF_SH, D), (FF_SH, D), (FF_SH, D)]
    shapes_out = [(D, IN_SH), (OUT_SH, D), (FF_SHP, D), (FF_SHP, D), (FF_SHP, D)]
    return pl.pallas_call(
        body, name=name,
        grid_spec=pltpu.PrefetchScalarGridSpec(
            num_scalar_prefetch=1, grid=(1,),
            in_specs=[pl.BlockSpec((None,) + s, lambda i, sel_ref: (sel_ref[1], 0, 0)) for s in shapes_in],
            out_specs=[pl.BlockSpec((None,) + s, lambda i, sel_ref: (sel_ref[0], 0, 0)) for s in shapes_out]),
        out_shape=[_sds((NDEV,) + s, WIRE_DTYPE) for s in shapes_out], compiler_params=_params(("arbitrary",)),
    )(sel, w_in, w_out, w_gate, w_up, w_down)


def place_blocks(sel, arrs, name):
    n = len(arrs)

    def body(sel_ref, *refs):
        del sel_ref
        for a in range(n):
            refs[n + a][...] = refs[a][...]

    def whole(a):
        nd = a.ndim
        return pl.BlockSpec(a.shape, lambda i, sel_ref: (0,) * nd)

    def mine(a):
        nd = a.ndim
        return pl.BlockSpec((None,) + a.shape, lambda i, sel_ref: (sel_ref[0],) + (0,) * nd)

    return pl.pallas_call(
        body, name=name,
        grid_spec=pltpu.PrefetchScalarGridSpec(
            num_scalar_prefetch=1, grid=(1,), in_specs=[whole(a) for a in arrs], out_specs=[mine(a) for a in arrs]),
        out_shape=[_sds((NDEV,) + a.shape, a.dtype) for a in arrs], compiler_params=_params(("arbitrary",)),
    )(sel, *arrs)


def ag_start(lands, after, name):
    n = len(lands)
    ns = 4 * n

    def body(*refs):
        lnd = refs[:n]
        send_sems, recv_sems = refs[n + 1:n + 1 + ns], refs[n + 1 + ns:n + 1 + 2 * ns]
        token = refs[-1]
        x, y, c = _place()
        me, sibling = (x, y, c), (x, y, 1 - c)
        chips = [(1 - x, y), (x, 1 - y), (1 - x, 1 - y)]
        copy = _ag_copy(lnd, send_sems, recv_sems, 4)
        for a in range(n):
            copy(a, 0, me, sibling).start()
            for j, chip in enumerate(chips):
                copy(a, 1 + j, me, (*chip, c)).start()
        token[...] = jnp.zeros_like(token)

    outs = pl.pallas_call(
        body, name=name,
        in_specs=[HBM] * n + [ANY],
        out_specs=[SEM] * (2 * ns) + [HBM] * n + [pl.BlockSpec(memory_space=pltpu.VMEM)],
        out_shape=_dma_sems(2 * ns) + _hbm_like(lands) + [_sds((8, 128), F32)],
        input_output_aliases={i: 2 * ns + i for i in range(n)},
        compiler_params=pltpu.CompilerParams(has_side_effects=EFFECT),
    )(*[_hbm(a) for a in lands], after)
    return outs[:ns], outs[ns:2 * ns], outs[2 * ns:2 * ns + n], outs[-1]


def ag_forward(send_sems, recv_sems, lands, after, name):
    n = len(lands)
    n1, n2 = 4 * n, 3 * n

    def body(*refs):
        lnd = refs[:n]
        o = n
        s1, r1 = refs[o:o + n1], refs[o + n1:o + 2 * n1]
        o += 2 * n1 + 1
        s2, r2 = refs[o:o + n2], refs[o + n2:o + 2 * n2]
        token = refs[-1]
        token[...] = jnp.zeros_like(token)
        x, y, c = _place()
        me, sibling = (x, y, c), (x, y, 1 - c)
        chips = [(1 - x, y), (x, 1 - y), (1 - x, 1 - y)]
        copy1 = _ag_copy(lnd, s1, r1, 4)
        copy2 = _ag_copy(lnd, s2, r2, 3)
        for j, chip in enumerate(chips):
            for a in range(n):
                copy1(a, 1 + j, (*chip, c), me).wait_recv()
                copy2(a, j, (*chip, c), sibling).start()
        for a in range(n):
            copy1(a, 0, sibling, me).wait_recv()
            copy1(a, 0, me, sibling).wait_send()
            for j, chip in enumerate(chips):
                copy1(a, 1 + j, me, (*chip, c)).wait_send()

    outs = pl.pallas_call(
        body, name=name,
        in_specs=[HBM] * n + [SEM] * (2 * n1) + [ANY],
        out_specs=[SEM] * (2 * n2) + [HBM] * n + [pl.BlockSpec(memory_space=pltpu.VMEM)],
        out_shape=_dma_sems(2 * n2) + _hbm_like(lands) + [_sds((8, 128), F32)],
        input_output_aliases={i: 2 * n2 + i for i in range(n)},
        compiler_params=pltpu.CompilerParams(has_side_effects=EFFECT),
    )(*lands, *send_sems, *recv_sems, after)
    return outs[:n2], outs[n2:2 * n2], outs[2 * n2:2 * n2 + n], outs[-1]


def ag_finish(send_sems, recv_sems, lands, after, name):
    n = len(lands)
    n2 = 3 * n

    def body(*refs):
        lnd = refs[:n]
        s2, r2 = refs[n:n + n2], refs[n + n2:n + 2 * n2]
        x, y, c = _place()
        me, sibling = (x, y, c), (x, y, 1 - c)
        chips = [(1 - x, y), (x, 1 - y), (1 - x, 1 - y)]
        copy2 = _ag_copy(lnd, s2, r2, 3)
        for a in range(n):
            for j, chip in enumerate(chips):
                copy2(a, j, (*chip, c), sibling).wait_send()
                copy2(a, j, (*chip, 1 - c), me).wait_recv()

    outs = pl.pallas_call(
        body, name=name,
        in_specs=[HBM] * n + [SEM] * (2 * n2) + [ANY],
        out_specs=[HBM] * n, out_shape=_hbm_like(lands),
        input_output_aliases={i: i for i in range(n)},
        compiler_params=pltpu.CompilerParams(has_side_effects=EFFECT),
    )(*lands, *send_sems, *recv_sems, after)
    return list(outs)


def rs_sibling_start(arrs, name):
    n = len(arrs)
    ns = 4 * n
    lands = [lax.empty((4,) + a.shape[1:], a.dtype) for a in arrs]

    def body(*refs):
        ins, lnd = refs[:n], refs[n:2 * n]
        send_sems, recv_sems = refs[2 * n:2 * n + ns], refs[2 * n + ns:2 * n + 2 * ns]
        x, y, c = _place()
        sibling = (x, y, 1 - c)
        for a in range(n):
            for p in range(4):
                pltpu.make_async_remote_copy(
                    src_ref=ins[a].at[2 * p + 1 - c], dst_ref=lnd[a].at[p], send_sem=send_sems[4 * a + p],
                    recv_sem=recv_sems[4 * a + p], device_id=sibling, device_id_type=MESH).start()
        refs[-1][...] = jnp.zeros_like(refs[-1])

    outs = pl.pallas_call(
        body, name=name,
        in_specs=[HBM] * (2 * n), out_specs=[SEM] * (2 * ns) + [HBM] * (2 * n) + [pl.BlockSpec(memory_space=pltpu.VMEM)],
        out_shape=_dma_sems(2 * ns) + _hbm_like(arrs) + _hbm_like(lands) + [_sds((8, 128), F32)],
        input_output_aliases={i: 2 * ns + i for i in range(2 * n)},
        compiler_params=pltpu.CompilerParams(has_side_effects=EFFECT),
    )(*[_hbm(a) for a in arrs], *[_hbm(a) for a in lands])
    return (outs[:ns], outs[ns:2 * ns], outs[2 * ns:2 * ns + n], outs[2 * ns + n:2 * ns + 2 * n]), outs[-1]


def rs_sibling_wait(send_sems, recv_sems, arrs, lands, after, name):
    n = len(arrs)
    ns = 4 * n

    def body(*refs):
        ins, lnd = refs[:n], refs[n:2 * n]
        s, r = refs[2 * n:2 * n + ns], refs[2 * n + ns:2 * n + 2 * ns]
        x, y, c = _place()
        sibling = (x, y, 1 - c)
        for a in range(n):
            for p in range(4):
                cp = pltpu.make_async_remote_copy(
                    src_ref=ins[a].at[2 * p + 1 - c], dst_ref=lnd[a].at[p], send_sem=s[4 * a + p], recv_sem=r[4 * a + p],
                    device_id=sibling, device_id_type=MESH)
                cp.wait_send()
                cp.wait_recv()

    outs = pl.pallas_call(
        body, name=name,
        in_specs=[HBM] * (2 * n) + [SEM] * (2 * ns) + [ANY], out_specs=[HBM] * (2 * n),
        out_shape=_hbm_like(arrs) + _hbm_like(lands),
        input_output_aliases={i: i for i in range(2 * n)},
        compiler_params=pltpu.CompilerParams(has_side_effects=EFFECT),
    )(*arrs, *lands, *send_sems, *recv_sems, after)
    return outs[:n], outs[n:]


def rs_chips_start(parts, name):
    n = len(parts)
    ns = 3 * n
    lands = [lax.empty((3,) + a.shape[1:], a.dtype) for a in parts]

    def body(*refs):
        ins, lnd = refs[:n], refs[n:2 * n]
        send_sems, recv_sems = refs[2 * n:2 * n + ns], refs[2 * n + ns:2 * n + 2 * ns]
        x, y, c = _place()
        chips = [(1 - x, y), (x, 1 - y), (1 - x, 1 - y)]
        for a in range(n):
            for k, (tx, ty) in enumerate(chips):
                pltpu.make_async_remote_copy(
                    src_ref=ins[a].at[2 * tx + ty], dst_ref=lnd[a].at[k], send_sem=send_sems[3 * a + k],
                    recv_sem=recv_sems[3 * a + k], device_id=(tx, ty, c), device_id_type=MESH).start()
        refs[-1][...] = jnp.zeros_like(refs[-1])

    outs = pl.pallas_call(
        body, name=name,
        in_specs=[HBM] * (2 * n), out_specs=[SEM] * (2 * ns) + [HBM] * (2 * n) + [pl.BlockSpec(memory_space=pltpu.VMEM)],
        out_shape=_dma_sems(2 * ns) + _hbm_like(parts) + _hbm_like(lands) + [_sds((8, 128), F32)],
        input_output_aliases={i: 2 * ns + i for i in range(2 * n)},
        compiler_params=pltpu.CompilerParams(has_side_effects=EFFECT),
    )(*[_hbm(a) for a in parts], *[_hbm(a) for a in lands])
    return (outs[:ns], outs[ns:2 * ns], outs[2 * ns:2 * ns + n], outs[2 * ns + n:2 * ns + 2 * n]), outs[-1]


def rs_chips_wait(send_sems, recv_sems, parts, lands, after, name):
    n = len(parts)
    ns = 3 * n

    def body(*refs):
        ins, lnd = refs[:n], refs[n:2 * n]
        s, r = refs[2 * n:2 * n + ns], refs[2 * n + ns:2 * n + 2 * ns]
        x, y, c = _place()
        chips = [(1 - x, y), (x, 1 - y), (1 - x, 1 - y)]
        for a in range(n):
            for k, (tx, ty) in enumerate(chips):
                cp = pltpu.make_async_remote_copy(
                    src_ref=ins[a].at[2 * tx + ty], dst_ref=lnd[a].at[k], send_sem=s[3 * a + k], recv_sem=r[3 * a + k],
                    device_id=(tx, ty, c), device_id_type=MESH)
                cp.wait_send()
                cp.wait_recv()

    outs = pl.pallas_call(
        body, name=name,
        in_specs=[HBM] * (2 * n) + [SEM] * (2 * ns) + [ANY], out_specs=[HBM] * (2 * n),
        out_shape=_hbm_like(parts) + _hbm_like(lands),
        input_output_aliases={i: i for i in range(2 * n)},
        compiler_params=pltpu.CompilerParams(has_side_effects=EFFECT),
    )(*parts, *lands, *send_sems, *recv_sems, after)
    return outs[:n], outs[n:]


def pair_sum(arrs, recv, c, name):
    n = len(arrs)

    def body(c_ref, *refs):
        del c_ref
        for a in range(n):
            refs[2 * n + a][...] = (refs[a][...].astype(F32) + refs[n + a][...].astype(F32)).astype(refs[2 * n + a].dtype)

    half = [(a.shape[1] // 2, a.shape[2]) for a in arrs]
    mine = [pl.BlockSpec((None,) + s, lambda p, r, c_ref: (2 * p + c_ref[0], r, 0)) for s in half]
    other = [pl.BlockSpec((None,) + s, lambda p, r, c_ref: (p, r, 0)) for s in half]
    return pl.pallas_call(
        body, name=name,
        grid_spec=pltpu.PrefetchScalarGridSpec(num_scalar_prefetch=1, grid=(4, 2), in_specs=mine + other, out_specs=other),
        out_shape=[_sds((4,) + a.shape[1:], a.dtype) for a in arrs], compiler_params=_params(("parallel", "parallel")),
    )(c, *arrs, *recv)


def _adamw(w, g, m, v):
    m = ADAM_B1 * m + (1.0 - ADAM_B1) * g
    v = ADAM_B2 * v + (1.0 - ADAM_B2) * jnp.square(g)
    m_hat = m / (1.0 - ADAM_B1 ** ADAM_STEP)
    v_hat = v / (1.0 - ADAM_B2 ** ADAM_STEP)
    return -ADAM_LR * (m_hat / (jnp.sqrt(v_hat) + ADAM_EPS) + ADAM_WD * w), m, v


def adamw_big(recv, sums, chip, w, m, v, tr, name):
    nl, rr, cc = w.shape
    cp = recv[0].shape[2]

    def body(chip_ref, *refs):
        del chip_ref
        rcv, own = refs[:nl], refs[nl:2 * nl]
        w_ref, m_ref, v_ref, g_out, d_out, m_out, v_out = refs[2 * nl:]
        for l in range(nl):
            g = ((own[l][...].astype(F32) + rcv[l][0].astype(F32)) + rcv[l][1].astype(F32)) + rcv[l][2].astype(F32)
            g = g[:, :cc]
            g_out[l] = g
            d_out[l], m_out[l], v_out[l] = _adamw(w_ref[l], g, m_ref[l], v_ref[l])

    blk = pl.BlockSpec((nl, tr, cc), lambda i, chip_ref: (0, i, 0))
    return pl.pallas_call(
        body, name=name,
        grid_spec=pltpu.PrefetchScalarGridSpec(
            num_scalar_prefetch=1, grid=(rr // tr,),
            in_specs=[pl.BlockSpec((3, tr, cp), lambda i, chip_ref: (0, i, 0))] * nl
            + [pl.BlockSpec((None, tr, cp), lambda i, chip_ref: (chip_ref[0], i, 0))] * nl + [blk, blk, blk],
            out_specs=[blk] * 4),
        out_shape=[_sds(w.shape, F32)] * 4, compiler_params=_params(("parallel",)),
    )(chip, *recv, *sums, w, m, v)


SMALL_ROWS = 24


def small_grads(lvec, g_ret, g_mix, g_ffn, g_final, loss_part, dwa, dwx, name):
    def body(lvec_ref, ret_ref, mix_ref, ffn_ref, fin_ref, loss_ref, dwa_ref, dwx_ref, v_ref, g_ref):
        v_ref[16:SMALL_ROWS, :] = jnp.zeros((SMALL_ROWS - 16, D_LRU), F32)
        v_ref[16:17, 0:128] = loss_ref[0:1, :]
        v_ref[0:9, :] = lvec_ref[0:9, :]
        v_ref[9:10, :] = ret_ref[...]
        for r, src in ((10, mix_ref), (12, ffn_ref), (14, fin_ref)):
            v_ref[r:r + 1, :] = src[:, :D_LRU]
            v_ref[r + 1:r + 2, :] = src[:, D_LRU:]
        for k, src in enumerate((dwa_ref, dwx_ref)):
            for g in range(LRU_BLOCKS):
                rows = slice(LRU_BD * g, LRU_BD * (g + 1))
                g_ref[D_LRU * k + LRU_BD * g:D_LRU * k + LRU_BD * (g + 1), :] = src[rows, rows]

    ins = [lvec, g_ret, g_mix, g_ffn, g_final, loss_part, dwa, dwx]
    return pl.pallas_call(
        body, name=name, grid=(1,), in_specs=[_full(a.shape) for a in ins],
        out_specs=[_full((SMALL_ROWS, D_LRU)), _full((2 * D_LRU, LRU_BD))],
        out_shape=[_sds((SMALL_ROWS, D_LRU), F32), _sds((2 * D_LRU, LRU_BD), F32)], compiler_params=_params(("arbitrary",)),
    )(*ins)


def sum_devices(arrs, name):
    n = len(arrs)

    def body(*refs):
        for a in range(n):
            acc = refs[a][0]
            for j in range(1, NDEV):
                acc = acc + refs[a][j]
            refs[n + a][...] = acc

    return pl.pallas_call(
        body, name=name, grid=(1,), in_specs=[_full(a.shape) for a in arrs], out_specs=[_full(a.shape[1:]) for a in arrs],
        out_shape=[_sds(a.shape[1:], F32) for a in arrs], compiler_params=_params(("arbitrary",)),
    )(*arrs)


def adamw_small(gs, ws, ms, vs, name):
    n = len(gs)

    def body(*refs):
        for a in range(n):
            g, w, m, v = (refs[k * n + a][...] for k in range(4))
            refs[4 * n + a][...], refs[5 * n + a][...], refs[6 * n + a][...] = _adamw(w, g, m, v)

    specs = [_full(a.shape) for a in ws]
    outs = pl.pallas_call(
        body, name=name, grid=(1,), in_specs=specs * 4, out_specs=specs * 3, out_shape=[_sds(a.shape, F32) for a in ws] * 3,
        compiler_params=_params(("arbitrary",)),
    )(*gs, *ws, *ms, *vs)
    return outs[:n], outs[n:2 * n], outs[2 * n:]


def block_diag(wa, wx, name):
    def body(wa_ref, wx_ref, oa_ref, ox_ref):
        for src, dst in ((wa_ref, oa_ref), (wx_ref, ox_ref)):
            dst[...] = jnp.zeros_like(dst)
            for g in range(LRU_BLOCKS):
                rows = slice(LRU_BD * g, LRU_BD * (g + 1))
                dst[rows, rows] = src[g].astype(dst.dtype)

    ispec = pl.BlockSpec((None, LRU_BLOCKS, LRU_BD, LRU_BD), lambda l: (l, 0, 0, 0))
    ospec = pl.BlockSpec((None, D_LRU, D_LRU), lambda l: (l, 0, 0))
    return pl.pallas_call(
        body, name=name, grid=(wa.shape[0],), in_specs=[ispec, ispec], out_specs=[ospec, ospec],
        out_shape=[_sds((wa.shape[0], D_LRU, D_LRU), MXU_DTYPE)] * 2, compiler_params=_params(("parallel",)),
    )(wa, wx)


REP_NAMES = ["norm_mix", "conv_b", "gate_a_w", "gate_a_b", "gate_x_w", "gate_x_b", "lru_lambda", "lru_out_norm",
             "ret_out_norm", "norm_ffn", "norm_final"]


def kernel(x, meta_tokens, norm_mix, w_in, conv_w, conv_b, gate_a_w, gate_a_b, gate_x_w, gate_x_b, lru_lambda, lru_out_norm, ret_out_norm, w_out, norm_ffn, w_gate, w_up, w_down, norm_final, loss_target, m_meta_tokens, m_norm_mix, m_w_in, m_conv_w, m_conv_b, m_gate_a_w, m_gate_a_b, m_gate_x_w, m_gate_x_b, m_lru_lambda, m_lru_out_norm, m_ret_out_norm, m_w_out, m_norm_ffn, m_w_gate, m_w_up, m_w_down, m_norm_final, v_meta_tokens, v_norm_mix, v_w_in, v_conv_w, v_conv_b, v_gate_a_w, v_gate_a_b, v_gate_x_w, v_gate_x_b, v_lru_lambda, v_lru_out_norm, v_ret_out_norm, v_w_out, v_norm_ffn, v_w_gate, v_w_up, v_w_down, v_norm_final):
    xi, yi, ci = _place()
    dev = 4 * xi + 2 * yi + ci
    c_arr = jnp.reshape(ci, (1,)).astype(jnp.int32)
    dev_arr = jnp.reshape(dev, (1,)).astype(jnp.int32)

    meta_g, conv_g = all_gather([meta_tokens, conv_w], c_arr, "ag_small")
    meta_full = jnp.transpose(meta_g, (1, 0, 2)).reshape(N_META, D)
    conv_full = jnp.transpose(conv_g, (1, 2, 0, 3)).reshape(DEPTH, CONV_W, D_LRU)
    tr_ = lambda a: jnp.transpose(a, (0, 2, 1))
    w_gate_t, m_w_gate_t, v_w_gate_t = tr_(w_gate), tr_(m_w_gate), tr_(v_w_gate)
    w_up_t, m_w_up_t, v_w_up_t = tr_(w_up), tr_(m_w_up), tr_(v_w_up)
    level1 = []
    token = meta_g
    for l in range(DEPTH):
        sel = jnp.stack([dev, jnp.int32(l)]).astype(jnp.int32)
        lands = to_wire(sel, w_in, w_out, w_gate_t, w_up_t, w_down, "to_wire")
        s1, r1, lands, token = ag_start(lands, token, f"ag_start_{l}")
        level1.append((s1, r1, lands))

    def as_weights(gi, go, gg, gu, gd):
        return dict(w_in=gi, w_out=go.reshape(D, D), w_gate=gg.reshape(D_FFP, D), w_up=gu.reshape(D_FFP, D),
                    w_down=gd.reshape(D_FFP, D))

    tables = _ret_tables()
    row = lambda a: a.reshape(1, -1)

    h = jnp.concatenate([jnp.zeros((PAD, D), F32), meta_full, x[0]], axis=0)
    saved, gathered = [], []
    s1, r1, lands = level1[0]
    s2, r2, first, order = ag_forward(s1[:4], r1[:4], lands[:1], token, "ag_forward_0_w_in")
    w_in_next = ag_finish(s2, r2, first, h, "ag_finish_0_w_in")[0]
    wa_dense, wx_dense = block_diag(gate_a_w, gate_x_w, "block_diag")
    for l in range(DEPTH):
        small = dict(cw=conv_full[l], cb=row(conv_b[l]), wa=wa_dense[l], ba=row(gate_a_b[l]),
                     wx=wx_dense[l], bx=row(gate_x_b[l]), lam=row(lru_lambda[l]),
                     gain=row(lru_out_norm[l]))
        s1, r1, lands = level1[l]
        hn1 = rmsnorm_fwd(h, row(norm_mix[l]), "rms_fwd")
        proj = mm_blocked_nn(hn1, w_in_next, F32, "proj")
        if l > 1:
            s2, r2, rest, order = ag_forward(s1[4:], r1[4:], lands[1:], proj, f"ag_forward_{l}_rest")
            ymix, hst, states = mix_fwd(proj, tables=tables, ret_gain=row(ret_out_norm[l]), after=order, name="mix_fwd", **small)
            w = as_weights(w_in_next, *ag_finish(s2, r2, rest, ymix, f"ag_finish_{l}_rest"))
            h_mid = mm_nn_res(ymix, w["w_out"], h, order, "out_proj")
        else:
            ymix, hst, states = mix_fwd(proj, tables=tables, ret_gain=row(ret_out_norm[l]), after=order, name="mix_fwd", **small)
            s2, r2, mid, order = ag_forward(s1[4:16], r1[4:16], lands[1:4], ymix, f"ag_forward_{l}_mid")
            mids = ag_finish(s2, r2, mid, order, f"ag_finish_{l}_mid")
            w = dict(w_in=w_in_next, w_out=mids[0].reshape(D, D), w_gate=mids[1].reshape(D_FFP, D), w_up=mids[2].reshape(D_FFP, D))
            h_mid = mm_nn_res(ymix, w["w_out"], h, order, "out_proj")
            s2d, r2d, down, order = ag_forward(s1[16:], r1[16:], lands[4:], h_mid, f"ag_forward_{l}_down")
        hn2 = rmsnorm_fwd(h_mid, row(norm_ffn[l]), "rms_fwd")
        act_dgate, act_dup, act = ffn_up(hn2, w["w_gate"], w["w_up"], "ffn_up")
        if l <= 1:
            w["w_down"] = ag_finish(s2d, r2d, down, act, f"ag_finish_{l}_down")[0].reshape(D_FFP, D)
        gathered.append(w)
        if l + 1 < DEPTH:
            s1n, r1n, landsn = level1[l + 1]
            s2, r2, first, order = ag_forward(s1n[:4], r1n[:4], landsn[:1], act, f"ag_forward_{l + 1}_w_in")
        h_out = mm_nn_res(act, w["w_down"], h_mid, order, "ffn_down")
        if l + 1 < DEPTH:
            w_in_next = ag_finish(s2, r2, first, h_out, f"ag_finish_{l + 1}_w_in")[0]
        saved.append(dict(h=h, hn1=hn1, proj=proj, hst=hst, states=states, ymix=ymix, h_mid=h_mid, hn2=hn2, act_dgate=act_dgate, act_dup=act_dup,
                          act=act, small=small))
        h = h_out

    loss_p, dh, dh_b, g_norm_final = loss_head(h, row(norm_final), loss_target[0], "loss_head")

    small_v = [None] * DEPTH
    small_w = [None] * DEPTH
    inflight = []
    order = loss_p

    def sibling_done(l, tag, names, sib, after):
        parts, got = rs_sibling_wait(*sib, after, f"rs_sibling_wait_{tag}")
        sums = pair_sum(parts, got, c_arr, "pair_sum")
        flying, started = rs_chips_start(sums, f"rs_chips_start_{tag}")
        inflight.append((l, tag, names, flying))
        return started

    for l in reversed(range(DEPTH)):
        w, s = gathered[l], saved[l]
        dgate, dup = ffn_down_bwd(dh_b, w["w_down"], s["act_dgate"], s["act_dup"], order, "ffn_down_bwd")
        dwd = mm_tn(s["act"], dh_b, PAIR, order, "dw_down").reshape(NDEV, FF_SHP, D)
        dwg, dwu = (g.reshape(NDEV, FF_SHP, D) for g in mm_tn_two(dgate, dup, s["hn2"], PAIR, order, "dw_rows"))
        split = l <= 1
        if split:
            ffn_sib, order = rs_sibling_start([dwg, dwu, dwd], f"rs_sibling_start_{l}_ffn")
        dhn2 = mm_rows_nn([(dgate, w["w_gate"]), (dup, w["w_up"])], order, "ffn_up_bwd")
        if split:
            order = sibling_done(l, f"{l}_ffn", ("w_gate", "w_up", "w_down"), ffn_sib, dhn2)
        dh_mid, dh_mid_b, g_norm_ffn = rmsnorm_bwd(s["h_mid"], row(norm_ffn[l]), dhn2, dh, "rms_bwd")
        dymix, dwo = out_proj_bwd(dh_mid_b, w["w_out"], s["ymix"], order, "out_proj_bwd")
        dwo = dwo.reshape(NDEV, OUT_SH, D)
        dproj, lvec, dwa, dwx, g_ret_norm = mix_bwd(s["proj"], s["hst"], s["states"], dymix, tables=tables,
                                                    ret_gain=row(ret_out_norm[l]), after=order, name="mix_bwd", **s["small"])
        dwi = mm_tn_blocked(s["hn1"], dproj, "dw_blocked")
        if split:
            sib_tag, sib_names = f"{l}_mix", ("w_in", "w_out")
            sib, order = rs_sibling_start([dwi, dwo], f"rs_sibling_start_{l}_mix")
        else:
            sib_tag, sib_names = str(l), ("w_in", "w_gate", "w_up", "w_out", "w_down")
            sib, order = rs_sibling_start([dwi, dwg, dwu, dwo, dwd], f"rs_sibling_start_{l}")
        dhn1 = mm_blocked_nt([(dproj, w["w_in"])], order, "proj_bwd")
        order = sibling_done(l, sib_tag, sib_names, sib, dhn1)
        dh, dh_b, g_norm_mix = rmsnorm_bwd(s["h"], row(norm_mix[l]), dhn1, dh_mid, "rms_bwd")

        g_fin, loss_part = (g_norm_final, loss_p) if l == 0 else (jnp.zeros((1, D), F32), jnp.zeros((8, 128), F32))
        small_v[l], small_w[l] = small_grads(lvec, g_ret_norm, g_norm_mix, g_norm_ffn, g_fin, loss_part, dwa, dwx,
                                             "small_grads")
        if l == 1:
            early = place_blocks(dev_arr, [jnp.stack(small_v[1:]), jnp.stack(small_w[1:])], "place_grads")
            early_sems = ag_start(early, order, "ag_start_grads")
            order = early_sems[3]

    grad_x = dh[X0:][None]
    g_meta = dh[PAD:X0]

    arrived = {}

    def wait_for(entries, after):
        for l, tag, names, flying in entries:
            sums, recv = rs_chips_wait(*flying, after, f"rs_chips_wait_{tag}")
            for i, n in enumerate(names):
                arrived[l, n] = (recv[i], sums[i])

    chip = jnp.reshape(2 * xi + yi, (1,)).astype(jnp.int32)

    def finish(wname, w_, m_, v_, tr):
        return adamw_big([arrived[l, wname][0] for l in range(DEPTH)], [arrived[l, wname][1] for l in range(DEPTH)], chip,
                         w_, m_, v_, tr, "adamw_" + wname)

    late_s1, late_r1, late_lands, late_started = ag_start(
        place_blocks(dev_arr, [small_v[0], small_w[0], g_meta], "place_late"), order, "ag_start_late")
    wait_for(inflight[:-1], late_started)
    o_gate = [tr_(o) for o in finish("w_gate", w_gate_t, m_w_gate_t, v_w_gate_t, 32)]
    o_up = [tr_(o) for o in finish("w_up", w_up_t, m_w_up_t, v_w_up_t, 32)]
    o_down = finish("w_down", w_down, m_w_down, v_w_down, 32)

    s2, r2, lands, _ = ag_forward(late_s1, late_r1, late_lands, o_down[0], "ag_forward_late")
    late = ag_finish(s2, r2, lands, o_down[0], "ag_finish_late")
    s2, r2, lands, _ = ag_forward(early_sems[0], early_sems[1], early_sems[2], dh, "ag_forward_grads")
    gath_early = ag_finish(s2, r2, lands, late[0], "ag_finish_grads")
    v0, w0, meta_sum, v123, w123 = sum_devices(list(late) + list(gath_early), "sum_devices")
    loss = v0[16, 0]
    vecs = jnp.concatenate([v0[None], v123])
    gws = jnp.concatenate([w0[None], w123])
    blocks = (DEPTH, LRU_BLOCKS, LRU_BD)
    small_g = dict(
        conv_w=lax.dynamic_slice_in_dim(vecs[:, 0:CONV_W], dev * (D_LRU // NDEV), D_LRU // NDEV, axis=2),
        conv_b=vecs[:, 4], gate_a_b=vecs[:, 5].reshape(blocks), gate_x_b=vecs[:, 6].reshape(blocks),
        lru_lambda=vecs[:, 7], lru_out_norm=vecs[:, 8], ret_out_norm=vecs[:, 9],
        norm_mix=vecs[:, 10:12].reshape(DEPTH, D), norm_ffn=vecs[:, 12:14].reshape(DEPTH, D),
        norm_final=v0[14:16].reshape(1, D),
        gate_a_w=gws[:, :D_LRU].reshape(blocks + (LRU_BD,)), gate_x_w=gws[:, D_LRU:].reshape(blocks + (LRU_BD,)),
        meta_tokens=lax.dynamic_slice_in_dim(meta_sum, dev * (D // NDEV), D // NDEV, axis=1))
    given = dict(norm_mix=(norm_mix, m_norm_mix, v_norm_mix), conv_b=(conv_b, m_conv_b, v_conv_b),
                 gate_a_w=(gate_a_w, m_gate_a_w, v_gate_a_w), gate_a_b=(gate_a_b, m_gate_a_b, v_gate_a_b),
                 gate_x_w=(gate_x_w, m_gate_x_w, v_gate_x_w), gate_x_b=(gate_x_b, m_gate_x_b, v_gate_x_b),
                 lru_lambda=(lru_lambda, m_lru_lambda, v_lru_lambda), lru_out_norm=(lru_out_norm, m_lru_out_norm, v_lru_out_norm),
                 ret_out_norm=(ret_out_norm, m_ret_out_norm, v_ret_out_norm), norm_ffn=(norm_ffn, m_norm_ffn, v_norm_ffn),
                 norm_final=tuple(a.reshape(1, D) for a in (norm_final, m_norm_final, v_norm_final)),
                 conv_w=(conv_w, m_conv_w, v_conv_w), meta_tokens=(meta_tokens, m_meta_tokens, v_meta_tokens))
    small_names = REP_NAMES + ["conv_w", "meta_tokens"]
    upd = adamw_small([small_g[n] for n in small_names], *[[given[n][k] for n in small_names] for k in range(3)],
                      "adamw_small")
    small_out = [dict(zip(small_names, u)) for u in upd]
    for d_ in [small_g] + small_out:
        d_["norm_final"] = d_["norm_final"].reshape(D)

    wait_for(inflight[-1:], upd[0][0])
    o_in = finish("w_in", w_in, m_w_in, v_w_in, 256)
    o_out = finish("w_out", w_out, m_w_out, v_w_out, 64)

    bigs = dict(w_in=o_in, w_out=o_out, w_gate=o_gate, w_up=o_up, w_down=o_down)
    order = ["meta_tokens", "norm_mix", "w_in", "conv_w", "conv_b", "gate_a_w", "gate_a_b", "gate_x_w", "gate_x_b", "lru_lambda",
             "lru_out_norm", "ret_out_norm", "w_out", "norm_ffn", "w_gate", "w_up", "w_down", "norm_final"]
    grads = [bigs[n][0] if n in bigs else small_g[n] for n in order]
    rest = [[bigs[n][k + 1] if n in bigs else small_out[k][n] for n in order] for k in range(3)]
    return (loss, grad_x, *grads, *rest[0], *rest[1], *rest[2])
```

```python
import numpy as np
import jax
import jax.numpy as jnp
from jax import lax
from jax.experimental import pallas as pl
from jax.experimental.pallas import tpu as pltpu

F32, BF16 = jnp.float32, jnp.bfloat16
MXU_DTYPE = BF16
WIRE_DTYPE = BF16

D = 1024
SEQ = 2048
DEPTH = 4
N_META = 16
CH = 128
PAD = (-(SEQ + N_META)) % CH
T = SEQ + N_META + PAD
NCH = T // CH
X0 = PAD + N_META
D_LRU = 512
LRU_BLOCKS = 8
LRU_BD = 64
CONV_W = 4
LRU_C = 8.0
D_RET = 512
HEADS = 4
HD = 128
ROPE_BASE = 10000.0
D_IN = 3072
D_FF = 2816
NDEV = 8
IN_SH = D_IN // NDEV
FF_SH = D_FF // NDEV
FF_SHP = 384
D_FFP = NDEV * FF_SHP
OUT_SH = D // NDEV
EPS = 1e-6
TM = 544
VMEM_LIMIT = 56 * 2**20
MESH = pl.DeviceIdType.MESH

ADAM_LR, ADAM_B1, ADAM_B2, ADAM_EPS, ADAM_WD, ADAM_STEP = 0.001, 0.9, 0.999, 1e-08, 0.01, 10

NN = ((1,), (0,))
NT = ((1,), (1,))
TN = ((0,), (0,))


def _dot(a, b, dims):
    return lax.dot_general(a.astype(MXU_DTYPE), b.astype(MXU_DTYPE), (dims, ((), ())), preferred_element_type=F32)


def _sds(shape, dtype):
    return jax.ShapeDtypeStruct(shape, dtype)


def _params(sem=None):
    return pltpu.CompilerParams(dimension_semantics=sem, vmem_limit_bytes=VMEM_LIMIT)


def _full(shape):
    n = len(shape)
    return pl.BlockSpec(shape, lambda *_: (0,) * n)


def rmsnorm_fwd(h, gain, name):
    def body(h_ref, g_ref, o_ref):
        x = h_ref[...]
        ms = jnp.mean(x * x, axis=-1, keepdims=True)
        o_ref[...] = (x * lax.rsqrt(ms + EPS) * g_ref[...]).astype(o_ref.dtype)

    return pl.pallas_call(
        body, name=name, grid=(T // TM,),
        in_specs=[pl.BlockSpec((TM, D), lambda i: (i, 0)), _full((1, D))],
        out_specs=pl.BlockSpec((TM, D), lambda i: (i, 0)),
        out_shape=_sds((T, D), MXU_DTYPE), compiler_params=_params(("parallel",)),
    )(h, gain)


def rmsnorm_bwd(h, gain, dhn, dres, after, name):
    def body(h_ref, g_ref, dhn_ref, dres_ref, after_ref, dh_ref, dhb_ref, dg_ref):
        del after_ref
        x = h_ref[...]
        rstd = lax.rsqrt(jnp.mean(x * x, axis=-1, keepdims=True) + EPS)
        xhat = x * rstd
        dy = dhn_ref[...]
        dyg = dy * g_ref[...]
        dh = dres_ref[...] + rstd * (dyg - xhat * jnp.mean(dyg * xhat, axis=-1, keepdims=True))
        dh_ref[...] = dh
        dhb_ref[...] = dh.astype(dhb_ref.dtype)

        @pl.when(pl.program_id(0) == 0)
        def _():
            dg_ref[...] = jnp.zeros_like(dg_ref)
        dg_ref[...] += jnp.sum(dy * xhat, axis=0, keepdims=True)

    row = pl.BlockSpec((TM, D), lambda i: (i, 0))
    return pl.pallas_call(
        body, name=name, grid=(T // TM,),
        in_specs=[row, _full((1, D)), row, row, pl.BlockSpec(memory_space=pl.ANY)],
        out_specs=[row, row, _full((1, D))],
        out_shape=[_sds((T, D), F32), _sds((T, D), MXU_DTYPE), _sds((1, D), F32)], compiler_params=_params(("arbitrary",)),
    )(h, gain, dhn, dres, after)


def loss_head(h, gain, target, name):
    def body(h_ref, g_ref, t_ref, loss_ref, dh_ref, dhb_ref, dg_ref):
        i = pl.program_id(0)

        @pl.when(i == 0)
        def _():
            loss_ref[...] = jnp.zeros_like(loss_ref)
            dg_ref[...] = jnp.zeros_like(dg_ref)
            dh_ref[...] = jnp.zeros_like(dh_ref)
            dhb_ref[...] = jnp.zeros_like(dhb_ref)

        @pl.when(i > 0)
        def _():
            x = h_ref[...]
            g = g_ref[...]
            rstd = lax.rsqrt(jnp.mean(x * x, axis=-1, keepdims=True) + EPS)
            xhat = x * rstd
            err = xhat * g - t_ref[...]
            loss_ref[...] += 0.5 * jnp.sum(jnp.mean(err * err, axis=-1, keepdims=True), axis=0, keepdims=True)
            dy = err * (1.0 / D)
            dyg = dy * g
            dh = rstd * (dyg - xhat * jnp.mean(dyg * xhat, axis=-1, keepdims=True))
            dh_ref[...] = dh
            dhb_ref[...] = dh.astype(dhb_ref.dtype)
            dg_ref[...] += jnp.sum(dy * xhat, axis=0, keepdims=True)

    row = pl.BlockSpec((CH, D), lambda i: (i, 0))
    return pl.pallas_call(
        body, name=name, grid=(NCH,),
        in_specs=[row, _full((1, D)), pl.BlockSpec((CH, D), lambda i: (jnp.maximum(i - 1, 0), 0))],
        out_specs=[_full((8, 128)), row, row, _full((1, D))],
        out_shape=[_sds((8, 128), F32), _sds((T, D), F32), _sds((T, D), MXU_DTYPE), _sds((1, D), F32)],
        compiler_params=_params(("arbitrary",)),
    )(h, gain, target)


PAIR = 2 * IN_SH
NPAIR = NDEV // 2
BN = 256
FB = 512


def _pair_cols(w_ref):
    return jnp.concatenate([w_ref[0], w_ref[1]], axis=1)


W_PAIR = lambda k: pl.BlockSpec((2, k, IN_SH), lambda j: (j, 0, 0))
COLS_PAIR = pl.BlockSpec((T, PAIR), lambda j: (0, j))
ANYSPEC = pl.BlockSpec(memory_space=pl.ANY)


def mm_blocked_nn(a, w, out_dtype, name):
    k = a.shape[1]

    def body(a_ref, w_ref, o_ref):
        o_ref[:PAD, :] = jnp.zeros((PAD, PAIR), o_ref.dtype)
        o_ref[PAD:, :] = _dot(a_ref[PAD:, :], _pair_cols(w_ref), NN).astype(o_ref.dtype)

    return pl.pallas_call(
        body, name=name, grid=(NPAIR,),
        in_specs=[_full((T, k)), W_PAIR(k)], out_specs=COLS_PAIR,
        out_shape=_sds((T, NDEV * IN_SH), out_dtype), compiler_params=_params(("parallel",)),
    )(a, w)


def mm_nn_res(a, w, res, after, name):
    k = a.shape[1]

    def body(a_ref, w_ref, r_ref, after_ref, o_ref):
        del after_ref
        o_ref[:PAD, :] = r_ref[:PAD, :]
        o_ref[PAD:, :] = r_ref[PAD:, :] + _dot(a_ref[PAD:, :], w_ref[...], NN)

    col = pl.BlockSpec((T, BN), lambda j: (0, j))
    return pl.pallas_call(
        body, name=name, grid=(D // BN,),
        in_specs=[_full((T, k)), pl.BlockSpec((k, BN), lambda j: (0, j)), col, ANYSPEC], out_specs=col,
        out_shape=_sds((T, D), F32), compiler_params=_params(("parallel",)),
    )(a, w, res, after)


def ffn_up(hn, wg, wu, name):
    def body(a_ref, wg_ref, wu_ref, dg_ref, du_ref, act_ref):
        a = a_ref[PAD:, :]
        for ref in (dg_ref, du_ref, act_ref):
            ref[:PAD, :] = jnp.zeros((PAD, FB), ref.dtype)
        for c in range(FB // BN):
            cols = slice(BN * c, BN * (c + 1))
            g = _dot(a, wg_ref[cols, :], NT)
            u = _dot(a, wu_ref[cols, :], NT)
            sg = jax.nn.sigmoid(g)
            silu = g * sg
            dg_ref[PAD:, cols] = (u * (sg * (1.0 + g * (1.0 - sg)))).astype(dg_ref.dtype)
            du_ref[PAD:, cols] = silu.astype(du_ref.dtype)
            act_ref[PAD:, cols] = (silu * u).astype(act_ref.dtype)

    wspec = pl.BlockSpec((FB, D), lambda j: (j, 0))
    ospec = pl.BlockSpec((T, FB), lambda j: (0, j))
    return pl.pallas_call(
        body, name=name, grid=(D_FFP // FB,),
        in_specs=[_full((T, D)), wspec, wspec], out_specs=[ospec] * 3,
        out_shape=[_sds((T, D_FFP), MXU_DTYPE)] * 3, compiler_params=_params(("parallel",)),
    )(hn, wg, wu)


def ffn_down_bwd(dh, wd, dact_dgate, dact_dup, after, name):
    def body(dh_ref, wd_ref, g_ref, u_ref, after_ref, dg_ref, du_ref):
        del after_ref
        dh = dh_ref[PAD:, :]
        for ref in (dg_ref, du_ref):
            ref[:PAD, :] = jnp.zeros((PAD, FB), ref.dtype)
        for c in range(FB // BN):
            cols = slice(BN * c, BN * (c + 1))
            dact = _dot(dh, wd_ref[cols, :], NT)
            dg_ref[PAD:, cols] = (dact * g_ref[PAD:, cols].astype(F32)).astype(dg_ref.dtype)
            du_ref[PAD:, cols] = (dact * u_ref[PAD:, cols].astype(F32)).astype(du_ref.dtype)

    blk = pl.BlockSpec((T, FB), lambda j: (0, j))
    return pl.pallas_call(
        body, name=name, grid=(D_FFP // FB,),
        in_specs=[_full((T, D)), pl.BlockSpec((FB, D), lambda j: (j, 0)), blk, blk, ANYSPEC],
        out_specs=[blk, blk],
        out_shape=[_sds((T, D_FFP), MXU_DTYPE)] * 2, compiler_params=_params(("parallel",)),
    )(dh, wd, dact_dgate, dact_dup, after)


def mm_blocked_nt(pairs, after, name):
    n = len(pairs)

    def body(*refs):
        o_ref = refs[2 * n + 1]

        @pl.when(pl.program_id(0) == 0)
        def _():
            o_ref[...] = jnp.zeros_like(o_ref)
        for p in range(n):
            o_ref[PAD:, :] += _dot(refs[2 * p][PAD:, :], _pair_cols(refs[2 * p + 1]), NT)

    specs, args = [], []
    for a, w in pairs:
        specs += [COLS_PAIR, W_PAIR(D)]
        args += [a, w]
    return pl.pallas_call(
        body, name=name, grid=(NPAIR,), in_specs=specs + [ANYSPEC], out_specs=_full((T, D)),
        out_shape=_sds((T, D), F32), compiler_params=_params(("arbitrary",)),
    )(*args, after)


def proj_bwd(dproj, w, hn, after, name):
    def body(dp_ref, w_ref, hn_ref, after_ref, dh_ref, dw_ref):
        del after_ref

        @pl.when(pl.program_id(0) == 0)
        def _():
            dh_ref[...] = jnp.zeros_like(dh_ref)
        dp = dp_ref[PAD:, :]
        dh_ref[PAD:, :] += _dot(dp, _pair_cols(w_ref), NT)
        dw = _dot(hn_ref[PAD:, :], dp, TN).astype(dw_ref.dtype)
        dw_ref[0] = dw[:, :IN_SH]
        dw_ref[1] = dw[:, IN_SH:]

    return pl.pallas_call(
        body, name=name, grid=(NPAIR,),
        in_specs=[COLS_PAIR, W_PAIR(D), _full((T, D)), ANYSPEC], out_specs=[_full((T, D)), W_PAIR(D)],
        out_shape=[_sds((T, D), F32), _sds((NDEV, D, IN_SH), WIRE_DTYPE)], compiler_params=_params(("arbitrary",)),
    )(dproj, w, hn, after)


def mm_tn_two(a1, a2, b, bm, after, name):
    m = a1.shape[1]

    def body(a1_ref, a2_ref, b_ref, after_ref, o1_ref, o2_ref):
        del after_ref
        b = b_ref[...]
        o1_ref[...] = _dot(a1_ref[...], b, TN).astype(o1_ref.dtype)
        o2_ref[...] = _dot(a2_ref[...], b, TN).astype(o2_ref.dtype)

    blk = pl.BlockSpec((T, bm), lambda i: (0, i))
    out = pl.BlockSpec((bm, D), lambda i: (i, 0))
    return pl.pallas_call(
        body, name=name, grid=(m // bm,),
        in_specs=[blk, blk, _full((T, D)), ANYSPEC], out_specs=[out, out],
        out_shape=[_sds((m, D), WIRE_DTYPE)] * 2, compiler_params=_params(("parallel",)),
    )(a1, a2, b, after)


def out_proj_bwd(dh, w, ymix, after, name):
    def body(dh_ref, w_ref, y_ref, after_ref, dy_ref, dw_ref):
        del after_ref
        dh_ = dh_ref[PAD:, :]
        dy_ref[:PAD, :] = jnp.zeros((PAD, BN), dy_ref.dtype)
        dy_ref[PAD:, :] = _dot(dh_, w_ref[...], NT)
        dw_ref[...] = _dot(y_ref[PAD:, :], dh_, TN).astype(dw_ref.dtype)

    return pl.pallas_call(
        body, name=name, grid=(D // BN,),
        in_specs=[_full((T, D)), pl.BlockSpec((BN, D), lambda j: (j, 0)), pl.BlockSpec((T, BN), lambda j: (0, j)), ANYSPEC],
        out_specs=[pl.BlockSpec((T, BN), lambda j: (0, j)), pl.BlockSpec((BN, D), lambda j: (j, 0))],
        out_shape=[_sds((T, D), F32), _sds((D, D), WIRE_DTYPE)], compiler_params=_params(("parallel",)),
    )(dh, w, ymix, after)


def mm_rows_nn(pairs, after, name):
    n = len(pairs)

    def body(*refs):
        o_ref = refs[2 * n + 1]

        @pl.when(pl.program_id(0) == 0)
        def _():
            o_ref[...] = jnp.zeros_like(o_ref)
        for p in range(n):
            o_ref[PAD:, :] += _dot(refs[2 * p][PAD:, :], refs[2 * p + 1][...], NN)

    specs, args = [], []
    for a, w in pairs:
        specs += [pl.BlockSpec((T, FB), lambda j: (0, j)), pl.BlockSpec((FB, D), lambda j: (j, 0))]
        args += [a, w]
    return pl.pallas_call(
        body, name=name, grid=(D_FFP // FB,), in_specs=specs + [ANYSPEC], out_specs=_full((T, D)),
        out_shape=_sds((T, D), F32), compiler_params=_params(("arbitrary",)),
    )(*args, after)


def mm_tn_blocked(a, b, name):
    def body(a_ref, b_ref, o_ref):
        o = _dot(a_ref[...], b_ref[...], TN).astype(o_ref.dtype)
        o_ref[0] = o[:, :IN_SH]
        o_ref[1] = o[:, IN_SH:]

    return pl.pallas_call(
        body, name=name, grid=(NPAIR,),
        in_specs=[_full((T, D)), COLS_PAIR], out_specs=W_PAIR(D),
        out_shape=_sds((NDEV, D, IN_SH), WIRE_DTYPE), compiler_params=_params(("parallel",)),
    )(a, b)


def mm_tn(a, b, bm, after, name):
    m = a.shape[1]

    def body(a_ref, b_ref, after_ref, o_ref):
        del after_ref
        o_ref[...] = _dot(a_ref[...], b_ref[...], TN).astype(o_ref.dtype)

    return pl.pallas_call(
        body, name=name, grid=(m // bm,),
        in_specs=[pl.BlockSpec((T, bm), lambda i: (0, i)), _full((T, D)), ANYSPEC],
        out_specs=pl.BlockSpec((bm, D), lambda i: (i, 0)),
        out_shape=_sds((m, D), WIRE_DTYPE), compiler_params=_params(("parallel",)),
    )(a, b, after)


def _softplus_neg(lam):
    return jnp.maximum(-lam, 0.0) + jnp.log1p(jnp.exp(-jnp.abs(lam)))


def _lru_gates(pa, px, xc, lam):
    r = jax.nn.sigmoid(pa)
    ig = jax.nn.sigmoid(px)
    sp = _softplus_neg(lam)
    log_a = -LRU_C * r * sp
    a = jnp.exp(log_a)
    mult = jnp.sqrt(-jnp.tanh(log_a) * (a * a + 1.0))
    return a, mult * (ig * xc), (r, ig, sp, mult)


def _lru_gates_vjp(da, db, xc, lam, a, r, ig, sp, mult):
    dmult = db * (ig * xc)
    du = db * mult
    dlog_a = da * a - dmult * (a * a) / mult
    dr = dlog_a * (-LRU_C * sp)
    dlam = jnp.sum(dlog_a * (-LRU_C * r), axis=0, keepdims=True) * (-jax.nn.sigmoid(-lam))
    dpa = dr * (r * (1.0 - r))
    dpx = (du * xc) * (ig * (1.0 - ig))
    return dpa, dpx, du * ig, dlam


def _lru_out(h, g, gain):
    z = h * jax.nn.gelu(g)
    return z * lax.rsqrt(jnp.mean(z * z, axis=-1, keepdims=True) + EPS) * gain


def _conv_taps(x, xprev, row):
    taps = [x]
    for s in range(1, CONV_W):
        taps.append(jnp.where(row < s, pltpu.roll(xprev, s, 0), pltpu.roll(x, s, 0)))
    return taps


def _conv(taps, cw_ref, cb):
    xc = cb + cw_ref[CONV_W - 1:CONV_W, :] * taps[0]
    for s in range(1, CONV_W):
        xc = xc + cw_ref[CONV_W - 1 - s:CONV_W - s, :] * taps[s]
    return xc


def _lru_fwd_block(i, x_ref, g_ref, cw_ref, cb_ref, wa_ref, ba_ref, wx_ref, bx_ref, lam_ref, gain_ref, y_ref, h_ref,
                   xprev_scr, a_scr, b_scr, carry_scr):
    @pl.when(i == 0)
    def _():
        xprev_scr[...] = jnp.zeros_like(xprev_scr)
        carry_scr[...] = jnp.zeros_like(carry_scr)

    x = x_ref[...]
    row = lax.broadcasted_iota(jnp.int32, (CH, D_LRU), 0)
    xc = _conv(_conv_taps(x, xprev_scr[...], row), cw_ref, cb_ref[...])
    pa = _dot(xc, wa_ref[...], NN) + ba_ref[...]
    px = _dot(xc, wx_ref[...], NN) + bx_ref[...]
    a, b, _ = _lru_gates(pa, px, xc, lam_ref[...])
    a_scr[...] = a
    b_scr[...] = jnp.where(i * CH + row >= PAD, b, 0.0)
    h = carry_scr[...]
    for t in range(CH):
        h = a_scr[t:t + 1, :] * h + b_scr[t:t + 1, :]
        h_ref[t:t + 1, :] = h
    carry_scr[...] = h
    xprev_scr[...] = x
    y_ref[:, :D_LRU] = _lru_out(h_ref[...], g_ref[...], gain_ref[...]).astype(y_ref.dtype)


LRU_VEC_ROWS = 16


def _lru_bwd_block(ib, x_ref, xp_ref, g_ref, h_ref, hp_ref, dy_ref, cw_ref, cb_ref, wa_ref, ba_ref, wx_ref, bx_ref, lam_ref,
                   gain_ref, dp_ref, vec_ref, dwa_ref, dwx_ref, a_scr, dh_scr, g_scr, carry_scr, dxcn_scr):
    @pl.when(ib == NCH - 1)
    def _():
        carry_scr[...] = jnp.zeros_like(carry_scr)
        dxcn_scr[...] = jnp.zeros_like(dxcn_scr)
        vec_ref[...] = jnp.zeros_like(vec_ref)
        dwa_ref[...] = jnp.zeros_like(dwa_ref)
        dwx_ref[...] = jnp.zeros_like(dwx_ref)

    x = x_ref[...]
    row = lax.broadcasted_iota(jnp.int32, (CH, D_LRU), 0)
    valid = ib * CH + row >= PAD
    taps = _conv_taps(x, xp_ref[...], row)
    xc = _conv(taps, cw_ref, cb_ref[...])
    pa = _dot(xc, wa_ref[...], NN) + ba_ref[...]
    px = _dot(xc, wx_ref[...], NN) + bx_ref[...]
    a, _, gate_parts = _lru_gates(pa, px, xc, lam_ref[...])
    h = h_ref[...]
    _, vjp_out = jax.vjp(_lru_out, h, g_ref[...], gain_ref[...])
    dh, dg, dgain = vjp_out(dy_ref[:, :D_LRU].astype(F32))
    a_scr[...] = a
    dh_scr[...] = dh
    c = carry_scr[...]
    for t in range(CH - 1, -1, -1):
        gt = dh_scr[t:t + 1, :] + c
        g_scr[t:t + 1, :] = gt
        c = a_scr[t:t + 1, :] * gt
    carry_scr[...] = c
    gg = g_scr[...]
    hprev = jnp.where(row < 1, pltpu.roll(hp_ref[...], 1, 0), pltpu.roll(h, 1, 0))
    da = jnp.where(valid, gg * hprev, 0.0)
    db = jnp.where(valid, gg, 0.0)
    dpa, dpx, dxc, dlam = _lru_gates_vjp(da, db, xc, lam_ref[...], a, *gate_parts)
    dxc = dxc + _dot(dpa, wa_ref[...], NT) + _dot(dpx, wx_ref[...], NT)
    dwa_ref[...] += _dot(xc, dpa, TN)
    dwx_ref[...] += _dot(xc, dpx, TN)
    for s in range(CONV_W):
        vec_ref[CONV_W - 1 - s:CONV_W - s, :] += jnp.sum(dxc * taps[s], axis=0, keepdims=True)
    vec_ref[4:5, :] += jnp.sum(dxc, axis=0, keepdims=True)
    vec_ref[5:6, :] += jnp.sum(dpa, axis=0, keepdims=True)
    vec_ref[6:7, :] += jnp.sum(dpx, axis=0, keepdims=True)
    vec_ref[7:8, :] += dlam
    vec_ref[8:9, :] += dgain
    dxn = dxcn_scr[...]
    dx = cw_ref[CONV_W - 1:CONV_W, :] * dxc
    for s in range(1, CONV_W):
        ahead = jnp.where(row >= CH - s, pltpu.roll(dxn, CH - s, 0), pltpu.roll(dxc, CH - s, 0))
        dx = dx + cw_ref[CONV_W - 1 - s:CONV_W - s, :] * ahead
    dxcn_scr[...] = dxc
    dp_ref[:, :D_LRU] = jnp.where(valid, dx, 0.0).astype(dp_ref.dtype)
    dp_ref[:, D_LRU:2 * D_LRU] = dg.astype(dp_ref.dtype)


def _ret_tables():
    half = HD // 2
    pos = jnp.arange(T, dtype=F32) - float(PAD)
    inv = ROPE_BASE ** (-jnp.arange(half, dtype=F32) / half)
    ang = pos[:, None] * inv[None, :]
    cos = jnp.concatenate([jnp.cos(ang), jnp.cos(ang)], axis=-1)
    sin = jnp.concatenate([-jnp.sin(ang), jnp.sin(ang)], axis=-1)
    log_g = jnp.log(1.0 - 2.0 ** (-5.0 - jnp.arange(HEADS, dtype=F32)))
    idx = jnp.arange(CH, dtype=F32)
    diff = idx[:, None] - idx[None, :]
    dmask = jnp.where(diff[None] >= 0, jnp.exp(jnp.maximum(diff, 0.0)[None] * log_g[:, None, None]), 0.0)
    xi = jnp.exp((idx + 1.0)[None, :] * log_g[:, None])
    zeta = jnp.exp((CH - 1.0 - idx)[None, :] * log_g[:, None])
    xi = jnp.broadcast_to(xi[:, :, None], (HEADS, CH, HD))
    zeta = jnp.broadcast_to(zeta[:, :, None], (HEADS, CH, HD))
    return cos, sin, dmask, xi, zeta


def _chunk_decay():
    log_g = np.log(np.float32(1.0) - np.float32(2.0) ** (np.float32(-5.0) - np.arange(HEADS, dtype=np.float32)))
    return [float(v) for v in np.exp(np.float32(CH) * log_g.astype(np.float32))]


def _rope(x, cos, sin):
    return x * cos + pltpu.roll(x, HD // 2, 1) * sin


def mix_fwd(proj, cw, cb, wa, ba, wx, bx, lam, gain, tables, ret_gain, after, name):
    cos, sin, dmask, xi, zeta = tables
    gch = _chunk_decay()
    scale = HD ** -0.5

    def body(x_ref, gl_ref, cw_ref, cb_ref, wa_ref, ba_ref, wx_ref, bx_ref, lam_ref, lgain_ref,
             q_ref, k_ref, v_ref, g_ref, cos_ref, sin_ref, dm_ref, xi_ref, zt_ref, gain_ref, after_ref,
             y_ref, h_ref, st_ref, xprev_scr, a_scr, b_scr, carry_scr, s_scr):
        del after_ref

        @pl.when(pl.program_id(0) == 0)
        def _():
            s_scr[...] = jnp.zeros_like(s_scr)

        _lru_fwd_block(pl.program_id(0), x_ref, gl_ref, cw_ref, cb_ref, wa_ref, ba_ref, wx_ref, bx_ref, lam_ref, lgain_ref,
                       y_ref, h_ref, xprev_scr, a_scr, b_scr, carry_scr)
        cs, sn = cos_ref[...], sin_ref[...]
        hs = range(HEADS)
        sl = [slice(HD * h, HD * (h + 1)) for h in hs]
        qr = [_rope(q_ref[:, sl[h]], cs, sn).astype(MXU_DTYPE) for h in hs]
        kf = [_rope(k_ref[:, sl[h]], cs, sn) * scale for h in hs]
        kr = [kf[h].astype(MXU_DTYPE) for h in hs]
        v = [v_ref[:, sl[h]].astype(MXU_DTYPE) for h in hs]
        s = [s_scr[h] for h in hs]
        for h in hs:
            st_ref[h] = s[h]
        sc = [_dot(qr[h], kr[h], NT) * dm_ref[h] for h in hs]
        cross = [_dot(qr[h], s[h], NN) * xi_ref[h] for h in hs]
        for h in hs:
            s_scr[h] = s[h] * gch[h] + _dot(kf[h] * zt_ref[h], v[h], TN)
        y = [_dot(sc[h], v[h], NN) + cross[h] for h in hs]
        yc = [y[h] - jnp.mean(y[h], axis=-1, keepdims=True) for h in hs]
        yn = [yc[h] * lax.rsqrt(jnp.mean(yc[h] * yc[h], axis=-1, keepdims=True) + EPS) for h in hs]
        for h in hs:
            so = slice(D_LRU + HD * h, D_LRU + HD * (h + 1))
            y_ref[:, so] = (jax.nn.silu(g_ref[:, sl[h]]) * (yn[h] * gain_ref[:, sl[h]])).astype(y_ref.dtype)

    def col(c):
        return pl.BlockSpec((CH, D_RET), lambda n: (n, c))

    tab = pl.BlockSpec((CH, HD), lambda n: (n, 0))
    cst = _full((HEADS, CH, HD))
    vec = _full((1, D_LRU))
    mat = _full((D_LRU, D_LRU))
    blockbuf = pltpu.VMEM((CH, D_LRU), F32)
    return pl.pallas_call(
        body, name=name, grid=(NCH,),
        in_specs=[col(0), col(1), _full((CONV_W, D_LRU)), vec, mat, vec, mat, vec, vec, vec,
                  col(2), col(3), col(4), col(5), tab, tab, cst, cst, cst, _full((1, D_RET)),
                  pl.BlockSpec(memory_space=pl.ANY)],
        out_specs=[pl.BlockSpec((CH, D), lambda n: (n, 0)), col(0), pl.BlockSpec((None, HEADS, HD, HD), lambda n: (n, 0, 0, 0))],
        out_shape=[_sds((T, D), MXU_DTYPE), _sds((T, D_LRU), F32), _sds((NCH, HEADS, HD, HD), F32)],
        scratch_shapes=[blockbuf, blockbuf, blockbuf, pltpu.VMEM((1, D_LRU), F32), pltpu.VMEM((HEADS, HD, HD), F32)],
        compiler_params=_params(("arbitrary",)),
    )(proj, proj, cw, cb, wa, ba, wx, bx, lam, gain, proj, proj, proj, proj, cos, sin, dmask, xi, zeta, ret_gain, after)


def mix_bwd(proj, hst, states, dymix, cw, cb, wa, ba, wx, bx, lam, gain, tables, ret_gain, after, name):
    cos, sin, dmask, xi, zeta = tables
    gch = _chunk_decay()
    scale = HD ** -0.5
    last = NCH - 1

    def body(x_ref, xp_ref, gl_ref, h_ref, hp_ref, cw_ref, cb_ref, wa_ref, ba_ref, wx_ref, bx_ref, lam_ref, lgain_ref,
             q_ref, k_ref, v_ref, g_ref, st_ref, dy_ref, cos_ref, sin_ref, dm_ref, xi_ref, zt_ref, gain_ref, after_ref,
             dp_ref, vec_ref, dwa_ref, dwx_ref, dgain_ref, a_scr, dh_scr, g_scr, carry_scr, dxcn_scr, ds_scr):
        del after_ref

        @pl.when(pl.program_id(0) == 0)
        def _():
            ds_scr[...] = jnp.zeros_like(ds_scr)
            dgain_ref[...] = jnp.zeros_like(dgain_ref)

        _lru_bwd_block(last - pl.program_id(0), x_ref, xp_ref, gl_ref, h_ref, hp_ref, dy_ref, cw_ref, cb_ref, wa_ref, ba_ref,
                       wx_ref, bx_ref, lam_ref, lgain_ref, dp_ref, vec_ref, dwa_ref, dwx_ref, a_scr, dh_scr, g_scr, carry_scr,
                       dxcn_scr)
        cs, sn = cos_ref[...], sin_ref[...]
        hs = range(HEADS)
        sl = [slice(HD * h, HD * (h + 1)) for h in hs]

        def out(j, h):
            return slice(2 * D_LRU + j * D_RET + HD * h, 2 * D_LRU + j * D_RET + HD * (h + 1))

        b16 = lambda xs: [x.astype(MXU_DTYPE) for x in xs]
        qr = b16([_rope(q_ref[:, sl[h]], cs, sn) for h in hs])
        kf = [_rope(k_ref[:, sl[h]], cs, sn) * scale for h in hs]
        kr = b16(kf)
        kz = b16([kf[h] * zt_ref[h] for h in hs])
        v = b16([v_ref[:, sl[h]] for h in hs])
        s = b16([st_ref[h] for h in hs])
        ds = [ds_scr[h] for h in hs]
        dsb = b16(ds)
        sc = [_dot(qr[h], kr[h], NT) * dm_ref[h] for h in hs]
        scb = b16(sc)
        y = [_dot(scb[h], v[h], NN) + _dot(qr[h], s[h], NN) * xi_ref[h] for h in hs]
        yc = [y[h] - jnp.mean(y[h], axis=-1, keepdims=True) for h in hs]
        rstd = [lax.rsqrt(jnp.mean(yc[h] * yc[h], axis=-1, keepdims=True) + EPS) for h in hs]
        yn = [yc[h] * rstd[h] for h in hs]
        dy = []
        for h in hs:
            g = g_ref[:, sl[h]]
            gain = gain_ref[:, sl[h]]
            sg = jax.nn.sigmoid(g)
            silu = g * sg
            dout = dy_ref[:, D_LRU + HD * h:D_LRU + HD * (h + 1)].astype(F32)
            dgain_ref[:, sl[h]] += jnp.sum(dout * silu * yn[h], axis=0, keepdims=True)
            dp_ref[:, out(3, h)] = (dout * yn[h] * gain * (sg * (1.0 + g * (1.0 - sg)))).astype(dp_ref.dtype)
            dyn = dout * silu * gain
            dy.append(rstd[h] * (dyn - jnp.mean(dyn, axis=-1, keepdims=True)
                                 - yn[h] * jnp.mean(dyn * yn[h], axis=-1, keepdims=True)))
        dyb = b16(dy)
        dqs = b16([dy[h] * xi_ref[h] for h in hs])
        dp = b16([_dot(dyb[h], v[h], NT) * dm_ref[h] for h in hs])
        dv = [_dot(scb[h], dyb[h], TN) + _dot(kz[h], dsb[h], NN) for h in hs]
        dqr = [_dot(dp[h], kr[h], NN) + _dot(dqs[h], s[h], NT) for h in hs]
        dkr = [_dot(dp[h], qr[h], TN) + _dot(v[h], dsb[h], NT) * zt_ref[h] for h in hs]
        for h in hs:
            ds_scr[h] = gch[h] * ds[h] + _dot(qr[h], dqs[h], TN)
        for h in hs:
            dp_ref[:, out(0, h)] = (dqr[h] * cs + pltpu.roll(dqr[h] * sn, HD // 2, 1)).astype(dp_ref.dtype)
            dp_ref[:, out(1, h)] = ((dkr[h] * cs + pltpu.roll(dkr[h] * sn, HD // 2, 1)) * scale).astype(dp_ref.dtype)
            dp_ref[:, out(2, h)] = dv[h].astype(dp_ref.dtype)

    def col(c, shift=0):
        return pl.BlockSpec((CH, D_RET), lambda n: (jnp.maximum(last - n - shift, 0), c))

    tab = pl.BlockSpec((CH, HD), lambda n: (last - n, 0))
    cst = _full((HEADS, CH, HD))
    vec = _full((1, D_LRU))
    mat = _full((D_LRU, D_LRU))
    blockbuf = pltpu.VMEM((CH, D_LRU), F32)
    return pl.pallas_call(
        body, name=name, grid=(NCH,),
        in_specs=[col(0), col(0, 1), col(1), col(0), col(0, 1), _full((CONV_W, D_LRU)), vec, mat, vec, mat, vec, vec, vec,
                  col(2), col(3), col(4), col(5), pl.BlockSpec((None, HEADS, HD, HD), lambda n: (last - n, 0, 0, 0)),
                  pl.BlockSpec((CH, D), lambda n: (last - n, 0)), tab, tab, cst, cst, cst, _full((1, D_RET)),
                  pl.BlockSpec(memory_space=pl.ANY)],
        out_specs=[pl.BlockSpec((CH, D_IN), lambda n: (last - n, 0)), _full((LRU_VEC_ROWS, D_LRU)), mat, mat,
                   _full((1, D_RET))],
        out_shape=[_sds((T, D_IN), MXU_DTYPE), _sds((LRU_VEC_ROWS, D_LRU), F32), _sds((D_LRU, D_LRU), F32),
                   _sds((D_LRU, D_LRU), F32), _sds((1, D_RET), F32)],
        scratch_shapes=[blockbuf, blockbuf, blockbuf, pltpu.VMEM((1, D_LRU), F32), blockbuf,
                        pltpu.VMEM((HEADS, HD, HD), F32)],
        compiler_params=_params(("arbitrary",)),
    )(proj, proj, proj, hst, hst, cw, cb, wa, ba, wx, bx, lam, gain, proj, proj, proj, proj, states, dymix,
      cos, sin, dmask, xi, zeta, ret_gain, after)


HBM = pl.BlockSpec(memory_space=pltpu.HBM)


def _place():
    return lax.axis_index("x"), lax.axis_index("y"), lax.axis_index("c")


def all_gather(arrs, after, name):
    n = len(arrs)

    def body(*refs):
        ins, outs = refs[:n], refs[n + 1:2 * n + 1]
        send_sems, recv_sems, local_sems = refs[2 * n + 1:]
        x, y, c = _place()
        me, sibling = (x, y, c), (x, y, 1 - c)
        chips = [(1 - x, y), (x, 1 - y), (1 - x, 1 - y)]

        def copy(a, k, block, to, src=None):
            px, py, pc = block
            dst = outs[a].at[4 * px + 2 * py + pc]
            return pltpu.make_async_remote_copy(
                src_ref=dst if src is None else src, dst_ref=dst, send_sem=send_sems.at[a, k], recv_sem=recv_sems.at[a, k],
                device_id=to, device_id_type=MESH)

        mine = [pltpu.make_async_copy(ins[a], outs[a].at[4 * x + 2 * y + c], local_sems.at[a]) for a in range(n)]
        for cp in mine:
            cp.start()
        first = []
        for a in range(n):
            first.append(copy(a, 0, me, sibling, src=ins[a]))
            first += [copy(a, 1 + j, me, (*chip, c), src=ins[a]) for j, chip in enumerate(chips)]
        for cp in first:
            cp.start()
        passed = []
        for j, chip in enumerate(chips):
            for a in range(n):
                copy(a, 1 + j, (*chip, c), me).wait_recv()
                passed.append(copy(a, 4 + j, (*chip, c), sibling))
                passed[-1].start()
        for a in range(n):
            copy(a, 0, sibling, me).wait_recv()
            for j, chip in enumerate(chips):
                copy(a, 4 + j, (*chip, 1 - c), me).wait_recv()
        for cp in first + passed:
            cp.wait_send()
        for cp in mine:
            cp.wait()

    return pl.pallas_call(
        body, name=name,
        in_specs=[HBM] * n + [pl.BlockSpec(memory_space=pl.ANY)], out_specs=[HBM] * n,
        out_shape=[_sds((NDEV,) + a.shape, a.dtype) for a in arrs],
        scratch_shapes=[pltpu.SemaphoreType.DMA((n, 7)), pltpu.SemaphoreType.DMA((n, 7)), pltpu.SemaphoreType.DMA((n,))],
    )(*arrs, after)


SEM = pl.BlockSpec(memory_space=pltpu.SEMAPHORE)
ANY = pl.BlockSpec(memory_space=pl.ANY)
EFFECT = pltpu.SideEffectType.DATAFLOW_SIDE_EFFECTING


def _hbm(a):
    return pltpu.with_memory_space_constraint(a, pltpu.HBM)


def _hbm_like(arrs):
    return [pltpu.HBM(a.shape, a.dtype) for a in arrs]


def _dma_sems(count):
    return [pltpu.SemaphoreType.DMA(())] * count


def _ag_copy(lands, send_sems, recv_sems, per):
    def copy(a, k, block, to, src=None):
        px, py, pc = block
        dst = lands[a].at[4 * px + 2 * py + pc]
        return pltpu.make_async_remote_copy(
            src_ref=dst if src is None else src, dst_ref=dst, send_sem=send_sems[a * per + k], recv_sem=recv_sems[a * per + k],
            device_id=to, device_id_type=MESH)
    return copy


def to_wire(sel, w_in, w_out, w_gate, w_up, w_down, name):
    ffpad = FF_SHP - FF_SH

    def body(sel_ref, i_ref, o_ref, g_ref, u_ref, d_ref, oi, oo, og, ou, od):
        del sel_ref
        oi[...] = i_ref[...].astype(oi.dtype)
        oo[...] = o_ref[...].astype(oo.dtype)
        for src, dst in ((g_ref, og), (u_ref, ou), (d_ref, od)):
            dst[:FF_SH, :] = src[...].astype(dst.dtype)
            dst[FF_SH:, :] = jnp.zeros((ffpad, D), dst.dtype)

    shapes_in = [(D, IN_SH), (OUT_SH, D), (FF_SH, D), (FF_SH, D), (FF_SH, D)]
    shapes_out = [(D, IN_SH), (OUT_SH, D), (FF_SHP, D), (FF_SHP, D), (FF_SHP, D)]
    return pl.pallas_call(
        body, name=name,
        grid_spec=pltpu.PrefetchScalarGridSpec(
            num_scalar_prefetch=1, grid=(1,),
            in_specs=[pl.BlockSpec((None,) + s, lambda i, sel_ref: (sel_ref[1], 0, 0)) for s in shapes_in],
            out_specs=[pl.BlockSpec((None,) + s, lambda i, sel_ref: (sel_ref[0], 0, 0)) for s in shapes_out]),
        out_shape=[_sds((NDEV,) + s, WIRE_DTYPE) for s in shapes_out], compiler_params=_params(("arbitrary",)),
    )(sel, w_in, w_out, w_gate, w_up, w_down)


def place_blocks(sel, arrs, name):
    n = len(arrs)

    def body(sel_ref, *refs):
        del sel_ref
        for a in range(n):
            refs[n + a][...] = refs[a][...]

    def whole(a):
        nd = a.ndim
        return pl.BlockSpec(a.shape, lambda i, sel_ref: (0,) * nd)

    def mine(a):
        nd = a.ndim
        return pl.BlockSpec((None,) + a.shape, lambda i, sel_ref: (sel_ref[0],) + (0,) * nd)

    return pl.pallas_call(
        body, name=name,
        grid_spec=pltpu.PrefetchScalarGridSpec(
            num_scalar_prefetch=1, grid=(1,), in_specs=[whole(a) for a in arrs], out_specs=[mine(a) for a in arrs]),
        out_shape=[_sds((NDEV,) + a.shape, a.dtype) for a in arrs], compiler_params=_params(("arbitrary",)),
    )(sel, *arrs)


def ag_start(lands, after, name):
    n = len(lands)
    ns = 4 * n

    def body(*refs):
        lnd = refs[:n]
        send_sems, recv_sems = refs[n + 1:n + 1 + ns], refs[n + 1 + ns:n + 1 + 2 * ns]
        token = refs[-1]
        x, y, c = _place()
        me, sibling = (x, y, c), (x, y, 1 - c)
        chips = [(1 - x, y), (x, 1 - y), (1 - x, 1 - y)]
        copy = _ag_copy(lnd, send_sems, recv_sems, 4)
        for a in range(n):
            copy(a, 0, me, sibling).start()
            for j, chip in enumerate(chips):
                copy(a, 1 + j, me, (*chip, c)).start()
        token[...] = jnp.zeros_like(token)

    outs = pl.pallas_call(
        body, name=name,
        in_specs=[HBM] * n + [ANY],
        out_specs=[SEM] * (2 * ns) + [HBM] * n + [pl.BlockSpec(memory_space=pltpu.VMEM)],
        out_shape=_dma_sems(2 * ns) + _hbm_like(lands) + [_sds((8, 128), F32)],
        input_output_aliases={i: 2 * ns + i for i in range(n)},
        compiler_params=pltpu.CompilerParams(has_side_effects=EFFECT),
    )(*[_hbm(a) for a in lands], after)
    return outs[:ns], outs[ns:2 * ns], outs[2 * ns:2 * ns + n], outs[-1]


def ag_forward(send_sems, recv_sems, lands, after, name):
    n = len(lands)
    n1, n2 = 4 * n, 3 * n

    def body(*refs):
        lnd = refs[:n]
        o = n
        s1, r1 = refs[o:o + n1], refs[o + n1:o + 2 * n1]
        o += 2 * n1 + 1
        s2, r2 = refs[o:o + n2], refs[o + n2:o + 2 * n2]
        token = refs[-1]
        token[...] = jnp.zeros_like(token)
        x, y, c = _place()
        me, sibling = (x, y, c), (x, y, 1 - c)
        chips = [(1 - x, y), (x, 1 - y), (1 - x, 1 - y)]
        copy1 = _ag_copy(lnd, s1, r1, 4)
        copy2 = _ag_copy(lnd, s2, r2, 3)
        for j, chip in enumerate(chips):
            for a in range(n):
                copy1(a, 1 + j, (*chip, c), me).wait_recv()
                copy2(a, j, (*chip, c), sibling).start()
        for a in range(n):
            copy1(a, 0, sibling, me).wait_recv()
            copy1(a, 0, me, sibling).wait_send()
            for j, chip in enumerate(chips):
                copy1(a, 1 + j, me, (*chip, c)).wait_send()

    outs = pl.pallas_call(
        body, name=name,
        in_specs=[HBM] * n + [SEM] * (2 * n1) + [ANY],
        out_specs=[SEM] * (2 * n2) + [HBM] * n + [pl.BlockSpec(memory_space=pltpu.VMEM)],
        out_shape=_dma_sems(2 * n2) + _hbm_like(lands) + [_sds((8, 128), F32)],
        input_output_aliases={i: 2 * n2 + i for i in range(n)},
        compiler_params=pltpu.CompilerParams(has_side_effects=EFFECT),
    )(*lands, *send_sems, *recv_sems, after)
    return outs[:n2], outs[n2:2 * n2], outs[2 * n2:2 * n2 + n], outs[-1]


def ag_finish(send_sems, recv_sems, lands, after, name):
    n = len(lands)
    n2 = 3 * n

    def body(*refs):
        lnd = refs[:n]
        s2, r2 = refs[n:n + n2], refs[n + n2:n + 2 * n2]
        x, y, c = _place()
        me, sibling = (x, y, c), (x, y, 1 - c)
        chips = [(1 - x, y), (x, 1 - y), (1 - x, 1 - y)]
        copy2 = _ag_copy(lnd, s2, r2, 3)
        for a in range(n):
            for j, chip in enumerate(chips):
                copy2(a, j, (*chip, c), sibling).wait_send()
                copy2(a, j, (*chip, 1 - c), me).wait_recv()

    outs = pl.pallas_call(
        body, name=name,
        in_specs=[HBM] * n + [SEM] * (2 * n2) + [ANY],
        out_specs=[HBM] * n, out_shape=_hbm_like(lands),
        input_output_aliases={i: i for i in range(n)},
        compiler_params=pltpu.CompilerParams(has_side_effects=EFFECT),
    )(*lands, *send_sems, *recv_sems, after)
    return list(outs)


def rs_sibling_start(arrs, name):
    n = len(arrs)
    ns = 4 * n
    lands = [lax.empty((4,) + a.shape[1:], a.dtype) for a in arrs]

    def body(*refs):
        ins, lnd = refs[:n], refs[n:2 * n]
        send_sems, recv_sems = refs[2 * n:2 * n + ns], refs[2 * n + ns:2 * n + 2 * ns]
        x, y, c = _place()
        sibling = (x, y, 1 - c)
        for a in range(n):
            for p in range(4):
                pltpu.make_async_remote_copy(
                    src_ref=ins[a].at[2 * p + 1 - c], dst_ref=lnd[a].at[p], send_sem=send_sems[4 * a + p],
                    recv_sem=recv_sems[4 * a + p], device_id=sibling, device_id_type=MESH).start()
        refs[-1][...] = jnp.zeros_like(refs[-1])

    outs = pl.pallas_call(
        body, name=name,
        in_specs=[HBM] * (2 * n), out_specs=[SEM] * (2 * ns) + [HBM] * (2 * n) + [pl.BlockSpec(memory_space=pltpu.VMEM)],
        out_shape=_dma_sems(2 * ns) + _hbm_like(arrs) + _hbm_like(lands) + [_sds((8, 128), F32)],
        input_output_aliases={i: 2 * ns + i for i in range(2 * n)},
        compiler_params=pltpu.CompilerParams(has_side_effects=EFFECT),
    )(*[_hbm(a) for a in arrs], *[_hbm(a) for a in lands])
    return (outs[:ns], outs[ns:2 * ns], outs[2 * ns:2 * ns + n], outs[2 * ns + n:2 * ns + 2 * n]), outs[-1]


def rs_sibling_wait(send_sems, recv_sems, arrs, lands, after, name):
    n = len(arrs)
    ns = 4 * n

    def body(*refs):
        ins, lnd = refs[:n], refs[n:2 * n]
        s, r = refs[2 * n:2 * n + ns], refs[2 * n + ns:2 * n + 2 * ns]
        x, y, c = _place()
        sibling = (x, y, 1 - c)
        for a in range(n):
            for p in range(4):
                cp = pltpu.make_async_remote_copy(
                    src_ref=ins[a].at[2 * p + 1 - c], dst_ref=lnd[a].at[p], send_sem=s[4 * a + p], recv_sem=r[4 * a + p],
                    device_id=sibling, device_id_type=MESH)
                cp.wait_send()
                cp.wait_recv()

    outs = pl.pallas_call(
        body, name=name,
        in_specs=[HBM] * (2 * n) + [SEM] * (2 * ns) + [ANY], out_specs=[HBM] * (2 * n),
        out_shape=_hbm_like(arrs) + _hbm_like(lands),
        input_output_aliases={i: i for i in range(2 * n)},
        compiler_params=pltpu.CompilerParams(has_side_effects=EFFECT),
    )(*arrs, *lands, *send_sems, *recv_sems, after)
    return outs[:n], outs[n:]


def rs_chips_start(parts, name):
    n = len(parts)
    ns = 3 * n
    lands = [lax.empty((3,) + a.shape[1:], a.dtype) for a in parts]

    def body(*refs):
        ins, lnd = refs[:n], refs[n:2 * n]
        send_sems, recv_sems = refs[2 * n:2 * n + ns], refs[2 * n + ns:2 * n + 2 * ns]
        x, y, c = _place()
        chips = [(1 - x, y), (x, 1 - y), (1 - x, 1 - y)]
        for a in range(n):
            for k, (tx, ty) in enumerate(chips):
                pltpu.make_async_remote_copy(
                    src_ref=ins[a].at[2 * tx + ty], dst_ref=lnd[a].at[k], send_sem=send_sems[3 * a + k],
                    recv_sem=recv_sems[3 * a + k], device_id=(tx, ty, c), device_id_type=MESH).start()
        refs[-1][...] = jnp.zeros_like(refs[-1])

    outs = pl.pallas_call(
        body, name=name,
        in_specs=[HBM] * (2 * n), out_specs=[SEM] * (2 * ns) + [HBM] * (2 * n) + [pl.BlockSpec(memory_space=pltpu.VMEM)],
        out_shape=_dma_sems(2 * ns) + _hbm_like(parts) + _hbm_like(lands) + [_sds((8, 128), F32)],
        input_output_aliases={i: 2 * ns + i for i in range(2 * n)},
        compiler_params=pltpu.CompilerParams(has_side_effects=EFFECT),
    )(*[_hbm(a) for a in parts], *[_hbm(a) for a in lands])
    return (outs[:ns], outs[ns:2 * ns], outs[2 * ns:2 * ns + n], outs[2 * ns + n:2 * ns + 2 * n]), outs[-1]


def rs_chips_wait(send_sems, recv_sems, parts, lands, after, name):
    n = len(parts)
    ns = 3 * n

    def body(*refs):
        ins, lnd = refs[:n], refs[n:2 * n]
        s, r = refs[2 * n:2 * n + ns], refs[2 * n + ns:2 * n + 2 * ns]
        x, y, c = _place()
        chips = [(1 - x, y), (x, 1 - y), (1 - x, 1 - y)]
        for a in range(n):
            for k, (tx, ty) in enumerate(chips):
                cp = pltpu.make_async_remote_copy(
                    src_ref=ins[a].at[2 * tx + ty], dst_ref=lnd[a].at[k], send_sem=s[3 * a + k], recv_sem=r[3 * a + k],
                    device_id=(tx, ty, c), device_id_type=MESH)
                cp.wait_send()
                cp.wait_recv()

    outs = pl.pallas_call(
        body, name=name,
        in_specs=[HBM] * (2 * n) + [SEM] * (2 * ns) + [ANY], out_specs=[HBM] * (2 * n),
        out_shape=_hbm_like(parts) + _hbm_like(lands),
        input_output_aliases={i: i for i in range(2 * n)},
        compiler_params=pltpu.CompilerParams(has_side_effects=EFFECT),
    )(*parts, *lands, *send_sems, *recv_sems, after)
    return outs[:n], outs[n:]


def pair_sum(arrs, recv, c, name):
    n = len(arrs)

    def body(c_ref, *refs):
        del c_ref
        for a in range(n):
            refs[2 * n + a][...] = (refs[a][...].astype(F32) + refs[n + a][...].astype(F32)).astype(refs[2 * n + a].dtype)

    mine = [pl.BlockSpec((None,) + a.shape[1:], lambda p, c_ref: (2 * p + c_ref[0], 0, 0)) for a in arrs]
    other = [pl.BlockSpec((None,) + a.shape[1:], lambda p, c_ref: (p, 0, 0)) for a in arrs]
    return pl.pallas_call(
        body, name=name,
        grid_spec=pltpu.PrefetchScalarGridSpec(num_scalar_prefetch=1, grid=(4,), in_specs=mine + other, out_specs=other),
        out_shape=[_sds((4,) + a.shape[1:], a.dtype) for a in arrs], compiler_params=_params(("parallel",)),
    )(c, *arrs, *recv)


def _adamw(w, g, m, v):
    m = ADAM_B1 * m + (1.0 - ADAM_B1) * g
    v = ADAM_B2 * v + (1.0 - ADAM_B2) * jnp.square(g)
    m_hat = m / (1.0 - ADAM_B1 ** ADAM_STEP)
    v_hat = v / (1.0 - ADAM_B2 ** ADAM_STEP)
    return -ADAM_LR * (m_hat / (jnp.sqrt(v_hat) + ADAM_EPS) + ADAM_WD * w), m, v


def adamw_big(recv, sums, chip, w, m, v, tr, name):
    nl, rr, cc = w.shape
    cp = recv[0].shape[2]

    def body(chip_ref, *refs):
        del chip_ref
        rcv, own = refs[:nl], refs[nl:2 * nl]
        w_ref, m_ref, v_ref, g_out, d_out, m_out, v_out = refs[2 * nl:]
        for l in range(nl):
            g = ((own[l][...].astype(F32) + rcv[l][0].astype(F32)) + rcv[l][1].astype(F32)) + rcv[l][2].astype(F32)
            g = g[:, :cc]
            g_out[l] = g
            d_out[l], m_out[l], v_out[l] = _adamw(w_ref[l], g, m_ref[l], v_ref[l])

    blk = pl.BlockSpec((nl, tr, cc), lambda i, chip_ref: (0, i, 0))
    return pl.pallas_call(
        body, name=name,
        grid_spec=pltpu.PrefetchScalarGridSpec(
            num_scalar_prefetch=1, grid=(rr // tr,),
            in_specs=[pl.BlockSpec((3, tr, cp), lambda i, chip_ref: (0, i, 0))] * nl
            + [pl.BlockSpec((None, tr, cp), lambda i, chip_ref: (chip_ref[0], i, 0))] * nl + [blk, blk, blk],
            out_specs=[blk] * 4),
        out_shape=[_sds(w.shape, F32)] * 4, compiler_params=_params(("parallel",)),
    )(chip, *recv, *sums, w, m, v)


SMALL_ROWS = 24


def small_grads(lvec, g_ret, g_mix, g_ffn, g_final, loss_part, dwa, dwx, name):
    def body(lvec_ref, ret_ref, mix_ref, ffn_ref, fin_ref, loss_ref, dwa_ref, dwx_ref, v_ref, g_ref):
        v_ref[16:SMALL_ROWS, :] = jnp.zeros((SMALL_ROWS - 16, D_LRU), F32)
        v_ref[16:17, 0:128] = loss_ref[0:1, :]
        v_ref[0:9, :] = lvec_ref[0:9, :]
        v_ref[9:10, :] = ret_ref[...]
        for r, src in ((10, mix_ref), (12, ffn_ref), (14, fin_ref)):
            v_ref[r:r + 1, :] = src[:, :D_LRU]
            v_ref[r + 1:r + 2, :] = src[:, D_LRU:]
        for k, src in enumerate((dwa_ref, dwx_ref)):
            for g in range(LRU_BLOCKS):
                rows = slice(LRU_BD * g, LRU_BD * (g + 1))
                g_ref[D_LRU * k + LRU_BD * g:D_LRU * k + LRU_BD * (g + 1), :] = src[rows, rows]

    ins = [lvec, g_ret, g_mix, g_ffn, g_final, loss_part, dwa, dwx]
    return pl.pallas_call(
        body, name=name, grid=(1,), in_specs=[_full(a.shape) for a in ins],
        out_specs=[_full((SMALL_ROWS, D_LRU)), _full((2 * D_LRU, LRU_BD))],
        out_shape=[_sds((SMALL_ROWS, D_LRU), F32), _sds((2 * D_LRU, LRU_BD), F32)], compiler_params=_params(("arbitrary",)),
    )(*ins)


def sum_devices(arrs, name):
    n = len(arrs)

    def body(*refs):
        for a in range(n):
            acc = refs[a][0]
            for j in range(1, NDEV):
                acc = acc + refs[a][j]
            refs[n + a][...] = acc

    return pl.pallas_call(
        body, name=name, grid=(1,), in_specs=[_full(a.shape) for a in arrs], out_specs=[_full(a.shape[1:]) for a in arrs],
        out_shape=[_sds(a.shape[1:], F32) for a in arrs], compiler_params=_params(("arbitrary",)),
    )(*arrs)


def adamw_small(gs, ws, ms, vs, name):
    n = len(gs)

    def body(*refs):
        for a in range(n):
            g, w, m, v = (refs[k * n + a][...] for k in range(4))
            refs[4 * n + a][...], refs[5 * n + a][...], refs[6 * n + a][...] = _adamw(w, g, m, v)

    specs = [_full(a.shape) for a in ws]
    outs = pl.pallas_call(
        body, name=name, grid=(1,), in_specs=specs * 4, out_specs=specs * 3, out_shape=[_sds(a.shape, F32) for a in ws] * 3,
        compiler_params=_params(("arbitrary",)),
    )(*gs, *ws, *ms, *vs)
    return outs[:n], outs[n:2 * n], outs[2 * n:]


def block_diag(wa, wx, name):
    def body(wa_ref, wx_ref, oa_ref, ox_ref):
        for src, dst in ((wa_ref, oa_ref), (wx_ref, ox_ref)):
            dst[...] = jnp.zeros_like(dst)
            for g in range(LRU_BLOCKS):
                rows = slice(LRU_BD * g, LRU_BD * (g + 1))
                dst[rows, rows] = src[g].astype(dst.dtype)

    ispec = pl.BlockSpec((None, LRU_BLOCKS, LRU_BD, LRU_BD), lambda l: (l, 0, 0, 0))
    ospec = pl.BlockSpec((None, D_LRU, D_LRU), lambda l: (l, 0, 0))
    return pl.pallas_call(
        body, name=name, grid=(wa.shape[0],), in_specs=[ispec, ispec], out_specs=[ospec, ospec],
        out_shape=[_sds((wa.shape[0], D_LRU, D_LRU), MXU_DTYPE)] * 2, compiler_params=_params(("parallel",)),
    )(wa, wx)


REP_NAMES = ["norm_mix", "conv_b", "gate_a_w", "gate_a_b", "gate_x_w", "gate_x_b", "lru_lambda", "lru_out_norm",
             "ret_out_norm", "norm_ffn", "norm_final"]


def kernel(x, meta_tokens, norm_mix, w_in, conv_w, conv_b, gate_a_w, gate_a_b, gate_x_w, gate_x_b, lru_lambda, lru_out_norm, ret_out_norm, w_out, norm_ffn, w_gate, w_up, w_down, norm_final, loss_target, m_meta_tokens, m_norm_mix, m_w_in, m_conv_w, m_conv_b, m_gate_a_w, m_gate_a_b, m_gate_x_w, m_gate_x_b, m_lru_lambda, m_lru_out_norm, m_ret_out_norm, m_w_out, m_norm_ffn, m_w_gate, m_w_up, m_w_down, m_norm_final, v_meta_tokens, v_norm_mix, v_w_in, v_conv_w, v_conv_b, v_gate_a_w, v_gate_a_b, v_gate_x_w, v_gate_x_b, v_lru_lambda, v_lru_out_norm, v_ret_out_norm, v_w_out, v_norm_ffn, v_w_gate, v_w_up, v_w_down, v_norm_final):
    xi, yi, ci = _place()
    dev = 4 * xi + 2 * yi + ci
    c_arr = jnp.reshape(ci, (1,)).astype(jnp.int32)
    dev_arr = jnp.reshape(dev, (1,)).astype(jnp.int32)

    meta_g, conv_g = all_gather([meta_tokens, conv_w], c_arr, "ag_small")
    meta_full = jnp.transpose(meta_g, (1, 0, 2)).reshape(N_META, D)
    conv_full = jnp.transpose(conv_g, (1, 2, 0, 3)).reshape(DEPTH, CONV_W, D_LRU)
    tr_ = lambda a: jnp.transpose(a, (0, 2, 1))
    w_gate_t, m_w_gate_t, v_w_gate_t = tr_(w_gate), tr_(m_w_gate), tr_(v_w_gate)
    w_up_t, m_w_up_t, v_w_up_t = tr_(w_up), tr_(m_w_up), tr_(v_w_up)
    level1 = []
    token = meta_g
    for l in range(DEPTH):
        sel = jnp.stack([dev, jnp.int32(l)]).astype(jnp.int32)
        lands = to_wire(sel, w_in, w_out, w_gate_t, w_up_t, w_down, "to_wire")
        s1, r1, lands, token = ag_start(lands, token, f"ag_start_{l}")
        level1.append((s1, r1, lands))

    def as_weights(gi, go, gg, gu, gd):
        return dict(w_in=gi, w_out=go.reshape(D, D), w_gate=gg.reshape(D_FFP, D), w_up=gu.reshape(D_FFP, D),
                    w_down=gd.reshape(D_FFP, D))

    tables = _ret_tables()
    row = lambda a: a.reshape(1, -1)

    h = jnp.concatenate([jnp.zeros((PAD, D), F32), meta_full, x[0]], axis=0)
    saved, gathered = [], []
    s1, r1, lands = level1[0]
    s2, r2, first, order = ag_forward(s1[:4], r1[:4], lands[:1], token, "ag_forward_0_w_in")
    w_in_next = ag_finish(s2, r2, first, h, "ag_finish_0_w_in")[0]
    wa_dense, wx_dense = block_diag(gate_a_w, gate_x_w, "block_diag")
    for l in range(DEPTH):
        small = dict(cw=conv_full[l], cb=row(conv_b[l]), wa=wa_dense[l], ba=row(gate_a_b[l]),
                     wx=wx_dense[l], bx=row(gate_x_b[l]), lam=row(lru_lambda[l]),
                     gain=row(lru_out_norm[l]))
        s1, r1, lands = level1[l]
        hn1 = rmsnorm_fwd(h, row(norm_mix[l]), "rms_fwd")
        proj = mm_blocked_nn(hn1, w_in_next, F32, "proj")
        if l > 1:
            s2, r2, rest, order = ag_forward(s1[4:], r1[4:], lands[1:], proj, f"ag_forward_{l}_rest")
            ymix, hst, states = mix_fwd(proj, tables=tables, ret_gain=row(ret_out_norm[l]), after=order, name="mix_fwd", **small)
            w = as_weights(w_in_next, *ag_finish(s2, r2, rest, ymix, f"ag_finish_{l}_rest"))
            h_mid = mm_nn_res(ymix, w["w_out"], h, order, "out_proj")
        else:
            ymix, hst, states = mix_fwd(proj, tables=tables, ret_gain=row(ret_out_norm[l]), after=order, name="mix_fwd", **small)
            s2, r2, mid, order = ag_forward(s1[4:16], r1[4:16], lands[1:4], ymix, f"ag_forward_{l}_mid")
            mids = ag_finish(s2, r2, mid, order, f"ag_finish_{l}_mid")
            w = dict(w_in=w_in_next, w_out=mids[0].reshape(D, D), w_gate=mids[1].reshape(D_FFP, D), w_up=mids[2].reshape(D_FFP, D))
            h_mid = mm_nn_res(ymix, w["w_out"], h, order, "out_proj")
            s2d, r2d, down, order = ag_forward(s1[16:], r1[16:], lands[4:], h_mid, f"ag_forward_{l}_down")
        hn2 = rmsnorm_fwd(h_mid, row(norm_ffn[l]), "rms_fwd")
        act_dgate, act_dup, act = ffn_up(hn2, w["w_gate"], w["w_up"], "ffn_up")
        if l <= 1:
            w["w_down"] = ag_finish(s2d, r2d, down, act, f"ag_finish_{l}_down")[0].reshape(D_FFP, D)
        gathered.append(w)
        if l + 1 < DEPTH:
            s1n, r1n, landsn = level1[l + 1]
            s2, r2, first, order = ag_forward(s1n[:4], r1n[:4], landsn[:1], act, f"ag_forward_{l + 1}_w_in")
        h_out = mm_nn_res(act, w["w_down"], h_mid, order, "ffn_down")
        if l + 1 < DEPTH:
            w_in_next = ag_finish(s2, r2, first, h_out, f"ag_finish_{l + 1}_w_in")[0]
        saved.append(dict(h=h, hn1=hn1, proj=proj, hst=hst, states=states, ymix=ymix, h_mid=h_mid, hn2=hn2, act_dgate=act_dgate, act_dup=act_dup,
                          act=act, small=small))
        h = h_out

    loss_p, dh, dh_b, g_norm_final = loss_head(h, row(norm_final), loss_target[0], "loss_head")

    small_v = [None] * DEPTH
    small_w = [None] * DEPTH
    inflight = []
    order = loss_p

    def sibling_done(l, tag, names, sib, after):
        parts, got = rs_sibling_wait(*sib, after, f"rs_sibling_wait_{tag}")
        sums = pair_sum(parts, got, c_arr, "pair_sum")
        flying, started = rs_chips_start(sums, f"rs_chips_start_{tag}")
        inflight.append((l, tag, names, flying))
        return started

    for l in reversed(range(DEPTH)):
        w, s = gathered[l], saved[l]
        dgate, dup = ffn_down_bwd(dh_b, w["w_down"], s["act_dgate"], s["act_dup"], order, "ffn_down_bwd")
        dwd = mm_tn(s["act"], dh_b, PAIR, order, "dw_down").reshape(NDEV, FF_SHP, D)
        dwg, dwu = (g.reshape(NDEV, FF_SHP, D) for g in mm_tn_two(dgate, dup, s["hn2"], PAIR, order, "dw_rows"))
        split = l <= 1
        if split:
            ffn_sib, order = rs_sibling_start([dwg, dwu, dwd], f"rs_sibling_start_{l}_ffn")
        dhn2 = mm_rows_nn([(dgate, w["w_gate"]), (dup, w["w_up"])], order, "ffn_up_bwd")
        if split:
            order = sibling_done(l, f"{l}_ffn", ("w_gate", "w_up", "w_down"), ffn_sib, dhn2)
        dh_mid, dh_mid_b, g_norm_ffn = rmsnorm_bwd(s["h_mid"], row(norm_ffn[l]), dhn2, dh, order, "rms_bwd")
        dymix, dwo = out_proj_bwd(dh_mid_b, w["w_out"], s["ymix"], order, "out_proj_bwd")
        dwo = dwo.reshape(NDEV, OUT_SH, D)
        dproj, lvec, dwa, dwx, g_ret_norm = mix_bwd(s["proj"], s["hst"], s["states"], dymix, tables=tables,
                                                    ret_gain=row(ret_out_norm[l]), after=order, name="mix_bwd", **s["small"])
        dhn1, dwi = proj_bwd(dproj, w["w_in"], s["hn1"], order, "proj_bwd")
        if split:
            sib_tag, sib_names = f"{l}_mix", ("w_in", "w_out")
            sib, order = rs_sibling_start([dwi, dwo], f"rs_sibling_start_{l}_mix")
        else:
            sib_tag, sib_names = str(l), ("w_in", "w_gate", "w_up", "w_out", "w_down")
            sib, order = rs_sibling_start([dwi, dwg, dwu, dwo, dwd], f"rs_sibling_start_{l}")
        dh, dh_b, g_norm_mix = rmsnorm_bwd(s["h"], row(norm_mix[l]), dhn1, dh_mid, order, "rms_bwd")
        order = sibling_done(l, sib_tag, sib_names, sib, dh)

        g_fin, loss_part = (g_norm_final, loss_p) if l == 0 else (jnp.zeros((1, D), F32), jnp.zeros((8, 128), F32))
        small_v[l], small_w[l] = small_grads(lvec, g_ret_norm, g_norm_mix, g_norm_ffn, g_fin, loss_part, dwa, dwx,
                                             "small_grads")
        if l == 1:
            early = place_blocks(dev_arr, [jnp.stack(small_v[1:]), jnp.stack(small_w[1:])], "place_grads")
            early_sems = ag_start(early, order, "ag_start_grads")
            order = early_sems[3]

    grad_x = dh[X0:][None]
    g_meta = dh[PAD:X0]

    arrived = {}

    def wait_for(entries, after):
        for l, tag, names, flying in entries:
            sums, recv = rs_chips_wait(*flying, after, f"rs_chips_wait_{tag}")
            for i, n in enumerate(names):
                arrived[l, n] = (recv[i], sums[i])

    chip = jnp.reshape(2 * xi + yi, (1,)).astype(jnp.int32)

    def finish(wname, w_, m_, v_, tr):
        return adamw_big([arrived[l, wname][0] for l in range(DEPTH)], [arrived[l, wname][1] for l in range(DEPTH)], chip,
                         w_, m_, v_, tr, "adamw_" + wname)

    late_s1, late_r1, late_lands, late_started = ag_start(
        place_blocks(dev_arr, [small_v[0], small_w[0], g_meta], "place_late"), order, "ag_start_late")
    wait_for(inflight[:-1], late_started)
    o_gate = [tr_(o) for o in finish("w_gate", w_gate_t, m_w_gate_t, v_w_gate_t, 32)]
    o_up = [tr_(o) for o in finish("w_up", w_up_t, m_w_up_t, v_w_up_t, 32)]
    o_down = finish("w_down", w_down, m_w_down, v_w_down, 32)

    s2, r2, lands, _ = ag_forward(late_s1, late_r1, late_lands, o_down[0], "ag_forward_late")
    late = ag_finish(s2, r2, lands, o_down[0], "ag_finish_late")
    s2, r2, lands, _ = ag_forward(early_sems[0], early_sems[1], early_sems[2], dh, "ag_forward_grads")
    gath_early = ag_finish(s2, r2, lands, late[0], "ag_finish_grads")
    v0, w0, meta_sum, v123, w123 = sum_devices(list(late) + list(gath_early), "sum_devices")
    loss = v0[16, 0]
    vecs = jnp.concatenate([v0[None], v123])
    gws = jnp.concatenate([w0[None], w123])
    blocks = (DEPTH, LRU_BLOCKS, LRU_BD)
    small_g = dict(
        conv_w=lax.dynamic_slice_in_dim(vecs[:, 0:CONV_W], dev * (D_LRU // NDEV), D_LRU // NDEV, axis=2),
        conv_b=vecs[:, 4], gate_a_b=vecs[:, 5].reshape(blocks), gate_x_b=vecs[:, 6].reshape(blocks),
        lru_lambda=vecs[:, 7], lru_out_norm=vecs[:, 8], ret_out_norm=vecs[:, 9],
        norm_mix=vecs[:, 10:12].reshape(DEPTH, D), norm_ffn=vecs[:, 12:14].reshape(DEPTH, D),
        norm_final=v0[14:16].reshape(1, D),
        gate_a_w=gws[:, :D_LRU].reshape(blocks + (LRU_BD,)), gate_x_w=gws[:, D_LRU:].reshape(blocks + (LRU_BD,)),
        meta_tokens=lax.dynamic_slice_in_dim(meta_sum, dev * (D // NDEV), D // NDEV, axis=1))
    given = dict(norm_mix=(norm_mix, m_norm_mix, v_norm_mix), conv_b=(conv_b, m_conv_b, v_conv_b),
                 gate_a_w=(gate_a_w, m_gate_a_w, v_gate_a_w), gate_a_b=(gate_a_b, m_gate_a_b, v_gate_a_b),
                 gate_x_w=(gate_x_w, m_gate_x_w, v_gate_x_w), gate_x_b=(gate_x_b, m_gate_x_b, v_gate_x_b),
                 lru_lambda=(lru_lambda, m_lru_lambda, v_lru_lambda), lru_out_norm=(lru_out_norm, m_lru_out_norm, v_lru_out_norm),
                 ret_out_norm=(ret_out_norm, m_ret_out_norm, v_ret_out_norm), norm_ffn=(norm_ffn, m_norm_ffn, v_norm_ffn),
                 norm_final=tuple(a.reshape(1, D) for a in (norm_final, m_norm_final, v_norm_final)),
                 conv_w=(conv_w, m_conv_w, v_conv_w), meta_tokens=(meta_tokens, m_meta_tokens, v_meta_tokens))
    small_names = REP_NAMES + ["conv_w", "meta_tokens"]
    upd = adamw_small([small_g[n] for n in small_names], *[[given[n][k] for n in small_names] for k in range(3)],
                      "adamw_small")
    small_out = [dict(zip(small_names, u)) for u in upd]
    for d_ in [small_g] + small_out:
        d_["norm_final"] = d_["norm_final"].reshape(D)

    wait_for(inflight[-1:], upd[0][0])
    o_in = finish("w_in", w_in, m_w_in, v_w_in, 256)
    o_out = finish("w_out", w_out, m_w_out, v_w_out, 64)

    bigs = dict(w_in=o_in, w_out=o_out, w_gate=o_gate, w_up=o_up, w_down=o_down)
    order = ["meta_tokens", "norm_mix", "w_in", "conv_w", "conv_b", "gate_a_w", "gate_a_b", "gate_x_w", "gate_x_b", "lru_lambda",
             "lru_out_norm", "ret_out_norm", "w_out", "norm_ffn", "w_gate", "w_up", "w_down", "norm_final"]
    grads = [bigs[n][0] if n in bigs else small_g[n] for n in order]
    rest = [[bigs[n][k + 1] if n in bigs else small_out[k][n] for n in order] for k in range(3)]
    return (loss, grad_x, *grads, *rest[0], *rest[1], *rest[2])
```

```python
import numpy as np
import jax
import jax.numpy as jnp
from jax import lax
from jax.experimental import pallas as pl
from jax.experimental.pallas import tpu as pltpu

F32, BF16 = jnp.float32, jnp.bfloat16
MXU_DTYPE = BF16
WIRE_DTYPE = BF16

D = 1024
SEQ = 2048
DEPTH = 4
N_META = 16
CH = 128
PAD = (-(SEQ + N_META)) % CH
T = SEQ + N_META + PAD
NCH = T // CH
X0 = PAD + N_META
D_LRU = 512
LRU_BLOCKS = 8
LRU_BD = 64
CONV_W = 4
LRU_C = 8.0
D_RET = 512
HEADS = 4
HD = 128
ROPE_BASE = 10000.0
D_IN = 3072
D_FF = 2816
NDEV = 8
IN_SH = D_IN // NDEV
FF_SH = D_FF // NDEV
FF_SHP = 384
D_FFP = NDEV * FF_SHP
OUT_SH = D // NDEV
EPS = 1e-6
TM = 544
VMEM_LIMIT = 56 * 2**20
MESH = pl.DeviceIdType.MESH

ADAM_LR, ADAM_B1, ADAM_B2, ADAM_EPS, ADAM_WD, ADAM_STEP = 0.001, 0.9, 0.999, 1e-08, 0.01, 10

NN = ((1,), (0,))
NT = ((1,), (1,))
TN = ((0,), (0,))


def _dot(a, b, dims):
    return lax.dot_general(a.astype(MXU_DTYPE), b.astype(MXU_DTYPE), (dims, ((), ())), preferred_element_type=F32)


def _sds(shape, dtype):
    return jax.ShapeDtypeStruct(shape, dtype)


def _params(sem=None):
    return pltpu.CompilerParams(dimension_semantics=sem, vmem_limit_bytes=VMEM_LIMIT)


def _full(shape):
    n = len(shape)
    return pl.BlockSpec(shape, lambda *_: (0,) * n)


def rmsnorm_fwd(h, gain, name):
    def body(h_ref, g_ref, o_ref):
        x = h_ref[...]
        ms = jnp.mean(x * x, axis=-1, keepdims=True)
        o_ref[...] = (x * lax.rsqrt(ms + EPS) * g_ref[...]).astype(o_ref.dtype)

    return pl.pallas_call(
        body, name=name, grid=(T // TM,),
        in_specs=[pl.BlockSpec((TM, D), lambda i: (i, 0)), _full((1, D))],
        out_specs=pl.BlockSpec((TM, D), lambda i: (i, 0)),
        out_shape=_sds((T, D), MXU_DTYPE), compiler_params=_params(("parallel",)),
    )(h, gain)


def rmsnorm_bwd(h, gain, dhn, dres, after, name):
    def body(h_ref, g_ref, dhn_ref, dres_ref, after_ref, dh_ref, dhb_ref, dg_ref):
        del after_ref
        x = h_ref[...]
        rstd = lax.rsqrt(jnp.mean(x * x, axis=-1, keepdims=True) + EPS)
        xhat = x * rstd
        dy = dhn_ref[...]
        dyg = dy * g_ref[...]
        dh = dres_ref[...] + rstd * (dyg - xhat * jnp.mean(dyg * xhat, axis=-1, keepdims=True))
        dh_ref[...] = dh
        dhb_ref[...] = dh.astype(dhb_ref.dtype)

        @pl.when(pl.program_id(0) == 0)
        def _():
            dg_ref[...] = jnp.zeros_like(dg_ref)
        dg_ref[...] += jnp.sum(dy * xhat, axis=0, keepdims=True)

    row = pl.BlockSpec((TM, D), lambda i: (i, 0))
    return pl.pallas_call(
        body, name=name, grid=(T // TM,),
        in_specs=[row, _full((1, D)), row, row, pl.BlockSpec(memory_space=pl.ANY)],
        out_specs=[row, row, _full((1, D))],
        out_shape=[_sds((T, D), F32), _sds((T, D), MXU_DTYPE), _sds((1, D), F32)], compiler_params=_params(("arbitrary",)),
    )(h, gain, dhn, dres, after)


def loss_head(h, gain, target, name):
    def body(h_ref, g_ref, t_ref, loss_ref, dh_ref, dhb_ref, dg_ref):
        i = pl.program_id(0)

        @pl.when(i == 0)
        def _():
            loss_ref[...] = jnp.zeros_like(loss_ref)
            dg_ref[...] = jnp.zeros_like(dg_ref)
            dh_ref[...] = jnp.zeros_like(dh_ref)
            dhb_ref[...] = jnp.zeros_like(dhb_ref)

        @pl.when(i > 0)
        def _():
            x = h_ref[...]
            g = g_ref[...]
            rstd = lax.rsqrt(jnp.mean(x * x, axis=-1, keepdims=True) + EPS)
            xhat = x * rstd
            err = xhat * g - t_ref[...]
            loss_ref[...] += 0.5 * jnp.sum(jnp.mean(err * err, axis=-1, keepdims=True), axis=0, keepdims=True)
            dy = err * (1.0 / D)
            dyg = dy * g
            dh = rstd * (dyg - xhat * jnp.mean(dyg * xhat, axis=-1, keepdims=True))
            dh_ref[...] = dh
            dhb_ref[...] = dh.astype(dhb_ref.dtype)
            dg_ref[...] += jnp.sum(dy * xhat, axis=0, keepdims=True)

    row = pl.BlockSpec((CH, D), lambda i: (i, 0))
    return pl.pallas_call(
        body, name=name, grid=(NCH,),
        in_specs=[row, _full((1, D)), pl.BlockSpec((CH, D), lambda i: (jnp.maximum(i - 1, 0), 0))],
        out_specs=[_full((8, 128)), row, row, _full((1, D))],
        out_shape=[_sds((8, 128), F32), _sds((T, D), F32), _sds((T, D), MXU_DTYPE), _sds((1, D), F32)],
        compiler_params=_params(("arbitrary",)),
    )(h, gain, target)


PAIR = 2 * IN_SH
NPAIR = NDEV // 2
BN = 256
FB = 512


def _pair_cols(w_ref):
    return jnp.concatenate([w_ref[0], w_ref[1]], axis=1)


W_PAIR = lambda k: pl.BlockSpec((2, k, IN_SH), lambda j: (j, 0, 0))
COLS_PAIR = pl.BlockSpec((T, PAIR), lambda j: (0, j))
ANYSPEC = pl.BlockSpec(memory_space=pl.ANY)


def mm_blocked_nn(a, w, out_dtype, name):
    k = a.shape[1]

    def body(a_ref, w_ref, o_ref):
        o_ref[:PAD, :] = jnp.zeros((PAD, PAIR), o_ref.dtype)
        o_ref[PAD:, :] = _dot(a_ref[PAD:, :], _pair_cols(w_ref), NN).astype(o_ref.dtype)

    return pl.pallas_call(
        body, name=name, grid=(NPAIR,),
        in_specs=[_full((T, k)), W_PAIR(k)], out_specs=COLS_PAIR,
        out_shape=_sds((T, NDEV * IN_SH), out_dtype), compiler_params=_params(("parallel",)),
    )(a, w)


def mm_nn_res(a, w, res, after, name):
    k = a.shape[1]

    def body(a_ref, w_ref, r_ref, after_ref, o_ref):
        del after_ref
        o_ref[:PAD, :] = r_ref[:PAD, :]
        o_ref[PAD:, :] = r_ref[PAD:, :] + _dot(a_ref[PAD:, :], w_ref[...], NN)

    col = pl.BlockSpec((T, BN), lambda j: (0, j))
    return pl.pallas_call(
        body, name=name, grid=(D // BN,),
        in_specs=[_full((T, k)), pl.BlockSpec((k, BN), lambda j: (0, j)), col, ANYSPEC], out_specs=col,
        out_shape=_sds((T, D), F32), compiler_params=_params(("parallel",)),
    )(a, w, res, after)


def ffn_up(hn, wg, wu, name):
    def body(a_ref, wg_ref, wu_ref, dg_ref, du_ref, act_ref):
        a = a_ref[PAD:, :]
        for ref in (dg_ref, du_ref, act_ref):
            ref[:PAD, :] = jnp.zeros((PAD, FB), ref.dtype)
        for c in range(FB // BN):
            cols = slice(BN * c, BN * (c + 1))
            g = _dot(a, wg_ref[cols, :], NT)
            u = _dot(a, wu_ref[cols, :], NT)
            sg = jax.nn.sigmoid(g)
            silu = g * sg
            dg_ref[PAD:, cols] = (u * (sg * (1.0 + g * (1.0 - sg)))).astype(dg_ref.dtype)
            du_ref[PAD:, cols] = silu.astype(du_ref.dtype)
            act_ref[PAD:, cols] = (silu * u).astype(act_ref.dtype)

    wspec = pl.BlockSpec((FB, D), lambda j: (j, 0))
    ospec = pl.BlockSpec((T, FB), lambda j: (0, j))
    return pl.pallas_call(
        body, name=name, grid=(D_FFP // FB,),
        in_specs=[_full((T, D)), wspec, wspec], out_specs=[ospec] * 3,
        out_shape=[_sds((T, D_FFP), MXU_DTYPE)] * 3, compiler_params=_params(("parallel",)),
    )(hn, wg, wu)


def ffn_down_bwd(dh, wd, dact_dgate, dact_dup, act, after, name):
    def body(dh_ref, wd_ref, g_ref, u_ref, act_ref, after_ref, dg_ref, du_ref, dw_ref):
        del after_ref
        dh = dh_ref[PAD:, :]
        dw_ref[...] = _dot(act_ref[PAD:, :], dh, TN).astype(dw_ref.dtype)
        for ref in (dg_ref, du_ref):
            ref[:PAD, :] = jnp.zeros((PAD, FB), ref.dtype)
        for c in range(FB // BN):
            cols = slice(BN * c, BN * (c + 1))
            dact = _dot(dh, wd_ref[cols, :], NT)
            dg_ref[PAD:, cols] = (dact * g_ref[PAD:, cols].astype(F32)).astype(dg_ref.dtype)
            du_ref[PAD:, cols] = (dact * u_ref[PAD:, cols].astype(F32)).astype(du_ref.dtype)

    blk = pl.BlockSpec((T, FB), lambda j: (0, j))
    rows = pl.BlockSpec((FB, D), lambda j: (j, 0))
    return pl.pallas_call(
        body, name=name, grid=(D_FFP // FB,),
        in_specs=[_full((T, D)), rows, blk, blk, blk, ANYSPEC],
        out_specs=[blk, blk, rows],
        out_shape=[_sds((T, D_FFP), MXU_DTYPE)] * 2 + [_sds((D_FFP, D), WIRE_DTYPE)], compiler_params=_params(("parallel",)),
    )(dh, wd, dact_dgate, dact_dup, act, after)


def proj_bwd(dproj, w, hn, after, name):
    def body(dp_ref, w_ref, hn_ref, after_ref, dh_ref, dw_ref):
        del after_ref

        @pl.when(pl.program_id(0) == 0)
        def _():
            dh_ref[...] = jnp.zeros_like(dh_ref)
        dp = dp_ref[PAD:, :]
        dh_ref[PAD:, :] += _dot(dp, _pair_cols(w_ref), NT)
        dw = _dot(hn_ref[PAD:, :], dp, TN).astype(dw_ref.dtype)
        dw_ref[0] = dw[:, :IN_SH]
        dw_ref[1] = dw[:, IN_SH:]

    return pl.pallas_call(
        body, name=name, grid=(NPAIR,),
        in_specs=[COLS_PAIR, W_PAIR(D), _full((T, D)), ANYSPEC], out_specs=[_full((T, D)), W_PAIR(D)],
        out_shape=[_sds((T, D), F32), _sds((NDEV, D, IN_SH), WIRE_DTYPE)], compiler_params=_params(("arbitrary",)),
    )(dproj, w, hn, after)


def mm_tn_two(a1, a2, b, bm, after, name):
    m = a1.shape[1]

    def body(a1_ref, a2_ref, b_ref, after_ref, o1_ref, o2_ref):
        del after_ref
        b = b_ref[PAD:, :]
        o1_ref[...] = _dot(a1_ref[PAD:, :], b, TN).astype(o1_ref.dtype)
        o2_ref[...] = _dot(a2_ref[PAD:, :], b, TN).astype(o2_ref.dtype)

    blk = pl.BlockSpec((T, bm), lambda i: (0, i))
    out = pl.BlockSpec((bm, D), lambda i: (i, 0))
    return pl.pallas_call(
        body, name=name, grid=(m // bm,),
        in_specs=[blk, blk, _full((T, D)), ANYSPEC], out_specs=[out, out],
        out_shape=[_sds((m, D), WIRE_DTYPE)] * 2, compiler_params=_params(("parallel",)),
    )(a1, a2, b, after)


def out_proj_bwd(dh, w, ymix, after, name):
    def body(dh_ref, w_ref, y_ref, after_ref, dy_ref, dw_ref):
        del after_ref
        dh_ = dh_ref[PAD:, :]
        dy_ref[:PAD, :] = jnp.zeros((PAD, BN), dy_ref.dtype)
        dy_ref[PAD:, :] = _dot(dh_, w_ref[...], NT)
        dw_ref[...] = _dot(y_ref[PAD:, :], dh_, TN).astype(dw_ref.dtype)

    return pl.pallas_call(
        body, name=name, grid=(D // BN,),
        in_specs=[_full((T, D)), pl.BlockSpec((BN, D), lambda j: (j, 0)), pl.BlockSpec((T, BN), lambda j: (0, j)), ANYSPEC],
        out_specs=[pl.BlockSpec((T, BN), lambda j: (0, j)), pl.BlockSpec((BN, D), lambda j: (j, 0))],
        out_shape=[_sds((T, D), F32), _sds((D, D), WIRE_DTYPE)], compiler_params=_params(("parallel",)),
    )(dh, w, ymix, after)


def mm_rows_nn(pairs, after, name):
    n = len(pairs)

    def body(*refs):
        o_ref = refs[2 * n + 1]

        @pl.when(pl.program_id(0) == 0)
        def _():
            o_ref[...] = jnp.zeros_like(o_ref)
        for p in range(n):
            o_ref[PAD:, :] += _dot(refs[2 * p][PAD:, :], refs[2 * p + 1][...], NN)

    specs, args = [], []
    for a, w in pairs:
        specs += [pl.BlockSpec((T, FB), lambda j: (0, j)), pl.BlockSpec((FB, D), lambda j: (j, 0))]
        args += [a, w]
    return pl.pallas_call(
        body, name=name, grid=(D_FFP // FB,), in_specs=specs + [ANYSPEC], out_specs=_full((T, D)),
        out_shape=_sds((T, D), F32), compiler_params=_params(("arbitrary",)),
    )(*args, after)


def _softplus_neg(lam):
    return jnp.maximum(-lam, 0.0) + jnp.log1p(jnp.exp(-jnp.abs(lam)))


def _lru_gates(pa, px, xc, lam):
    r = jax.nn.sigmoid(pa)
    ig = jax.nn.sigmoid(px)
    sp = _softplus_neg(lam)
    log_a = -LRU_C * r * sp
    a = jnp.exp(log_a)
    mult = jnp.sqrt(-jnp.tanh(log_a) * (a * a + 1.0))
    return a, mult * (ig * xc), (r, ig, sp, mult)


def _lru_gates_vjp(da, db, xc, lam, a, r, ig, sp, mult):
    dmult = db * (ig * xc)
    du = db * mult
    dlog_a = da * a - dmult * (a * a) / mult
    dr = dlog_a * (-LRU_C * sp)
    dlam = jnp.sum(dlog_a * (-LRU_C * r), axis=0, keepdims=True) * (-jax.nn.sigmoid(-lam))
    dpa = dr * (r * (1.0 - r))
    dpx = (du * xc) * (ig * (1.0 - ig))
    return dpa, dpx, du * ig, dlam


def _lru_out(h, g, gain):
    z = h * jax.nn.gelu(g)
    return z * lax.rsqrt(jnp.mean(z * z, axis=-1, keepdims=True) + EPS) * gain


def _conv_taps(x, xprev, row):
    taps = [x]
    for s in range(1, CONV_W):
        taps.append(jnp.where(row < s, pltpu.roll(xprev, s, 0), pltpu.roll(x, s, 0)))
    return taps


def _conv(taps, cw_ref, cb):
    xc = cb + cw_ref[CONV_W - 1:CONV_W, :] * taps[0]
    for s in range(1, CONV_W):
        xc = xc + cw_ref[CONV_W - 1 - s:CONV_W - s, :] * taps[s]
    return xc


def _lru_fwd_block(i, x_ref, g_ref, cw_ref, cb_ref, wa_ref, ba_ref, wx_ref, bx_ref, lam_ref, gain_ref, y_ref, h_ref,
                   xprev_scr, a_scr, b_scr, carry_scr):
    @pl.when(i == 0)
    def _():
        xprev_scr[...] = jnp.zeros_like(xprev_scr)
        carry_scr[...] = jnp.zeros_like(carry_scr)

    x = x_ref[...]
    row = lax.broadcasted_iota(jnp.int32, (CH, D_LRU), 0)
    xc = _conv(_conv_taps(x, xprev_scr[...], row), cw_ref, cb_ref[...])
    pa = _dot(xc, wa_ref[...], NN) + ba_ref[...]
    px = _dot(xc, wx_ref[...], NN) + bx_ref[...]
    a, b, _ = _lru_gates(pa, px, xc, lam_ref[...])
    a_scr[...] = a
    b_scr[...] = jnp.where(i * CH + row >= PAD, b, 0.0)
    h = carry_scr[...]
    for t in range(CH):
        h = a_scr[t:t + 1, :] * h + b_scr[t:t + 1, :]
        h_ref[t:t + 1, :] = h
    carry_scr[...] = h
    xprev_scr[...] = x
    y_ref[:, :D_LRU] = _lru_out(h_ref[...], g_ref[...], gain_ref[...]).astype(y_ref.dtype)


LRU_VEC_ROWS = 16


def _lru_bwd_block(ib, x_ref, xp_ref, g_ref, h_ref, hp_ref, dy_ref, cw_ref, cb_ref, wa_ref, ba_ref, wx_ref, bx_ref, lam_ref,
                   gain_ref, dp_ref, vec_ref, dwa_ref, dwx_ref, a_scr, dh_scr, g_scr, carry_scr, dxcn_scr):
    @pl.when(ib == NCH - 1)
    def _():
        carry_scr[...] = jnp.zeros_like(carry_scr)
        dxcn_scr[...] = jnp.zeros_like(dxcn_scr)
        vec_ref[...] = jnp.zeros_like(vec_ref)
        dwa_ref[...] = jnp.zeros_like(dwa_ref)
        dwx_ref[...] = jnp.zeros_like(dwx_ref)

    x = x_ref[...]
    row = lax.broadcasted_iota(jnp.int32, (CH, D_LRU), 0)
    valid = ib * CH + row >= PAD
    taps = _conv_taps(x, xp_ref[...], row)
    xc = _conv(taps, cw_ref, cb_ref[...])
    pa = _dot(xc, wa_ref[...], NN) + ba_ref[...]
    px = _dot(xc, wx_ref[...], NN) + bx_ref[...]
    a, _, gate_parts = _lru_gates(pa, px, xc, lam_ref[...])
    h = h_ref[...]
    _, vjp_out = jax.vjp(_lru_out, h, g_ref[...], gain_ref[...])
    dh, dg, dgain = vjp_out(dy_ref[:, :D_LRU].astype(F32))
    a_scr[...] = a
    dh_scr[...] = dh
    c = carry_scr[...]
    for t in range(CH - 1, -1, -1):
        gt = dh_scr[t:t + 1, :] + c
        g_scr[t:t + 1, :] = gt
        c = a_scr[t:t + 1, :] * gt
    carry_scr[...] = c
    gg = g_scr[...]
    hprev = jnp.where(row < 1, pltpu.roll(hp_ref[...], 1, 0), pltpu.roll(h, 1, 0))
    da = jnp.where(valid, gg * hprev, 0.0)
    db = jnp.where(valid, gg, 0.0)
    dpa, dpx, dxc, dlam = _lru_gates_vjp(da, db, xc, lam_ref[...], a, *gate_parts)
    dxc = dxc + _dot(dpa, wa_ref[...], NT) + _dot(dpx, wx_ref[...], NT)
    dwa_ref[...] += _dot(xc, dpa, TN)
    dwx_ref[...] += _dot(xc, dpx, TN)
    for s in range(CONV_W):
        vec_ref[CONV_W - 1 - s:CONV_W - s, :] += jnp.sum(dxc * taps[s], axis=0, keepdims=True)
    vec_ref[4:5, :] += jnp.sum(dxc, axis=0, keepdims=True)
    vec_ref[5:6, :] += jnp.sum(dpa, axis=0, keepdims=True)
    vec_ref[6:7, :] += jnp.sum(dpx, axis=0, keepdims=True)
    vec_ref[7:8, :] += dlam
    vec_ref[8:9, :] += dgain
    dxn = dxcn_scr[...]
    dx = cw_ref[CONV_W - 1:CONV_W, :] * dxc
    for s in range(1, CONV_W):
        ahead = jnp.where(row >= CH - s, pltpu.roll(dxn, CH - s, 0), pltpu.roll(dxc, CH - s, 0))
        dx = dx + cw_ref[CONV_W - 1 - s:CONV_W - s, :] * ahead
    dxcn_scr[...] = dxc
    dp_ref[:, :D_LRU] = jnp.where(valid, dx, 0.0).astype(dp_ref.dtype)
    dp_ref[:, D_LRU:2 * D_LRU] = dg.astype(dp_ref.dtype)


def _ret_tables():
    half = HD // 2
    pos = jnp.arange(T, dtype=F32) - float(PAD)
    inv = ROPE_BASE ** (-jnp.arange(half, dtype=F32) / half)
    ang = pos[:, None] * inv[None, :]
    cos = jnp.concatenate([jnp.cos(ang), jnp.cos(ang)], axis=-1)
    sin = jnp.concatenate([-jnp.sin(ang), jnp.sin(ang)], axis=-1)
    log_g = jnp.log(1.0 - 2.0 ** (-5.0 - jnp.arange(HEADS, dtype=F32)))
    idx = jnp.arange(CH, dtype=F32)
    diff = idx[:, None] - idx[None, :]
    dmask = jnp.where(diff[None] >= 0, jnp.exp(jnp.maximum(diff, 0.0)[None] * log_g[:, None, None]), 0.0)
    xi = jnp.exp((idx + 1.0)[None, :] * log_g[:, None])
    zeta = jnp.exp((CH - 1.0 - idx)[None, :] * log_g[:, None])
    xi = jnp.broadcast_to(xi[:, :, None], (HEADS, CH, HD))
    zeta = jnp.broadcast_to(zeta[:, :, None], (HEADS, CH, HD))
    return cos, sin, dmask, xi, zeta


def _chunk_decay():
    log_g = np.log(np.float32(1.0) - np.float32(2.0) ** (np.float32(-5.0) - np.arange(HEADS, dtype=np.float32)))
    return [float(v) for v in np.exp(np.float32(CH) * log_g.astype(np.float32))]


def _rope(x, cos, sin):
    return x * cos + pltpu.roll(x, HD // 2, 1) * sin


def mix_fwd(proj, cw, cb, wa, ba, wx, bx, lam, gain, tables, ret_gain, after, name):
    cos, sin, dmask, xi, zeta = tables
    gch = _chunk_decay()
    scale = HD ** -0.5

    def body(x_ref, gl_ref, cw_ref, cb_ref, wa_ref, ba_ref, wx_ref, bx_ref, lam_ref, lgain_ref,
             q_ref, k_ref, v_ref, g_ref, cos_ref, sin_ref, dm_ref, xi_ref, zt_ref, gain_ref, after_ref,
             y_ref, h_ref, st_ref, xprev_scr, a_scr, b_scr, carry_scr, s_scr):
        del after_ref

        @pl.when(pl.program_id(0) == 0)
        def _():
            s_scr[...] = jnp.zeros_like(s_scr)

        _lru_fwd_block(pl.program_id(0), x_ref, gl_ref, cw_ref, cb_ref, wa_ref, ba_ref, wx_ref, bx_ref, lam_ref, lgain_ref,
                       y_ref, h_ref, xprev_scr, a_scr, b_scr, carry_scr)
        cs, sn = cos_ref[...], sin_ref[...]
        hs = range(HEADS)
        sl = [slice(HD * h, HD * (h + 1)) for h in hs]
        qr = [_rope(q_ref[:, sl[h]], cs, sn).astype(MXU_DTYPE) for h in hs]
        kf = [_rope(k_ref[:, sl[h]], cs, sn) * scale for h in hs]
        kr = [kf[h].astype(MXU_DTYPE) for h in hs]
        v = [v_ref[:, sl[h]].astype(MXU_DTYPE) for h in hs]
        s = [s_scr[h] for h in hs]
        for h in hs:
            st_ref[h] = s[h]
        sc = [_dot(qr[h], kr[h], NT) * dm_ref[h] for h in hs]
        cross = [_dot(qr[h], s[h], NN) * xi_ref[h] for h in hs]
        for h in hs:
            s_scr[h] = s[h] * gch[h] + _dot(kf[h] * zt_ref[h], v[h], TN)
        y = [_dot(sc[h], v[h], NN) + cross[h] for h in hs]
        yc = [y[h] - jnp.mean(y[h], axis=-1, keepdims=True) for h in hs]
        yn = [yc[h] * lax.rsqrt(jnp.mean(yc[h] * yc[h], axis=-1, keepdims=True) + EPS) for h in hs]
        for h in hs:
            so = slice(D_LRU + HD * h, D_LRU + HD * (h + 1))
            y_ref[:, so] = (jax.nn.silu(g_ref[:, sl[h]]) * (yn[h] * gain_ref[:, sl[h]])).astype(y_ref.dtype)

    def col(c):
        return pl.BlockSpec((CH, D_RET), lambda n: (n, c))

    tab = pl.BlockSpec((CH, HD), lambda n: (n, 0))
    cst = _full((HEADS, CH, HD))
    vec = _full((1, D_LRU))
    mat = _full((D_LRU, D_LRU))
    blockbuf = pltpu.VMEM((CH, D_LRU), F32)
    return pl.pallas_call(
        body, name=name, grid=(NCH,),
        in_specs=[col(0), col(1), _full((CONV_W, D_LRU)), vec, mat, vec, mat, vec, vec, vec,
                  col(2), col(3), col(4), col(5), tab, tab, cst, cst, cst, _full((1, D_RET)),
                  pl.BlockSpec(memory_space=pl.ANY)],
        out_specs=[pl.BlockSpec((CH, D), lambda n: (n, 0)), col(0), pl.BlockSpec((None, HEADS, HD, HD), lambda n: (n, 0, 0, 0))],
        out_shape=[_sds((T, D), MXU_DTYPE), _sds((T, D_LRU), F32), _sds((NCH, HEADS, HD, HD), F32)],
        scratch_shapes=[blockbuf, blockbuf, blockbuf, pltpu.VMEM((1, D_LRU), F32), pltpu.VMEM((HEADS, HD, HD), F32)],
        compiler_params=_params(("arbitrary",)),
    )(proj, proj, cw, cb, wa, ba, wx, bx, lam, gain, proj, proj, proj, proj, cos, sin, dmask, xi, zeta, ret_gain, after)


def mix_bwd(proj, hst, states, dymix, cw, cb, wa, ba, wx, bx, lam, gain, tables, ret_gain, after, name):
    cos, sin, dmask, xi, zeta = tables
    gch = _chunk_decay()
    scale = HD ** -0.5
    last = NCH - 1

    def body(x_ref, xp_ref, gl_ref, h_ref, hp_ref, cw_ref, cb_ref, wa_ref, ba_ref, wx_ref, bx_ref, lam_ref, lgain_ref,
             q_ref, k_ref, v_ref, g_ref, st_ref, dy_ref, cos_ref, sin_ref, dm_ref, xi_ref, zt_ref, gain_ref, after_ref,
             dp_ref, vec_ref, dwa_ref, dwx_ref, dgain_ref, a_scr, dh_scr, g_scr, carry_scr, dxcn_scr, ds_scr):
        del after_ref

        @pl.when(pl.program_id(0) == 0)
        def _():
            ds_scr[...] = jnp.zeros_like(ds_scr)
            dgain_ref[...] = jnp.zeros_like(dgain_ref)

        _lru_bwd_block(last - pl.program_id(0), x_ref, xp_ref, gl_ref, h_ref, hp_ref, dy_ref, cw_ref, cb_ref, wa_ref, ba_ref,
                       wx_ref, bx_ref, lam_ref, lgain_ref, dp_ref, vec_ref, dwa_ref, dwx_ref, a_scr, dh_scr, g_scr, carry_scr,
                       dxcn_scr)
        cs, sn = cos_ref[...], sin_ref[...]
        hs = range(HEADS)
        sl = [slice(HD * h, HD * (h + 1)) for h in hs]

        def out(j, h):
            return slice(2 * D_LRU + j * D_RET + HD * h, 2 * D_LRU + j * D_RET + HD * (h + 1))

        b16 = lambda xs: [x.astype(MXU_DTYPE) for x in xs]
        qr = b16([_rope(q_ref[:, sl[h]], cs, sn) for h in hs])
        kf = [_rope(k_ref[:, sl[h]], cs, sn) * scale for h in hs]
        kr = b16(kf)
        kz = b16([kf[h] * zt_ref[h] for h in hs])
        v = b16([v_ref[:, sl[h]] for h in hs])
        s = b16([st_ref[h] for h in hs])
        ds = [ds_scr[h] for h in hs]
        dsb = b16(ds)
        sc = [_dot(qr[h], kr[h], NT) * dm_ref[h] for h in hs]
        scb = b16(sc)
        y = [_dot(scb[h], v[h], NN) + _dot(qr[h], s[h], NN) * xi_ref[h] for h in hs]
        yc = [y[h] - jnp.mean(y[h], axis=-1, keepdims=True) for h in hs]
        rstd = [lax.rsqrt(jnp.mean(yc[h] * yc[h], axis=-1, keepdims=True) + EPS) for h in hs]
        yn = [yc[h] * rstd[h] for h in hs]
        dy = []
        for h in hs:
            g = g_ref[:, sl[h]]
            gain = gain_ref[:, sl[h]]
            sg = jax.nn.sigmoid(g)
            silu = g * sg
            dout = dy_ref[:, D_LRU + HD * h:D_LRU + HD * (h + 1)].astype(F32)
            dgain_ref[:, sl[h]] += jnp.sum(dout * silu * yn[h], axis=0, keepdims=True)
            dp_ref[:, out(3, h)] = (dout * yn[h] * gain * (sg * (1.0 + g * (1.0 - sg)))).astype(dp_ref.dtype)
            dyn = dout * silu * gain
            dy.append(rstd[h] * (dyn - jnp.mean(dyn, axis=-1, keepdims=True)
                                 - yn[h] * jnp.mean(dyn * yn[h], axis=-1, keepdims=True)))
        dyb = b16(dy)
        dqs = b16([dy[h] * xi_ref[h] for h in hs])
        dp = b16([_dot(dyb[h], v[h], NT) * dm_ref[h] for h in hs])
        dv = [_dot(scb[h], dyb[h], TN) + _dot(kz[h], dsb[h], NN) for h in hs]
        dqr = [_dot(dp[h], kr[h], NN) + _dot(dqs[h], s[h], NT) for h in hs]
        dkr = [_dot(dp[h], qr[h], TN) + _dot(v[h], dsb[h], NT) * zt_ref[h] for h in hs]
        for h in hs:
            ds_scr[h] = gch[h] * ds[h] + _dot(qr[h], dqs[h], TN)
        for h in hs:
            dp_ref[:, out(0, h)] = (dqr[h] * cs + pltpu.roll(dqr[h] * sn, HD // 2, 1)).astype(dp_ref.dtype)
            dp_ref[:, out(1, h)] = ((dkr[h] * cs + pltpu.roll(dkr[h] * sn, HD // 2, 1)) * scale).astype(dp_ref.dtype)
            dp_ref[:, out(2, h)] = dv[h].astype(dp_ref.dtype)

    def col(c, shift=0):
        return pl.BlockSpec((CH, D_RET), lambda n: (jnp.maximum(last - n - shift, 0), c))

    tab = pl.BlockSpec((CH, HD), lambda n: (last - n, 0))
    cst = _full((HEADS, CH, HD))
    vec = _full((1, D_LRU))
    mat = _full((D_LRU, D_LRU))
    blockbuf = pltpu.VMEM((CH, D_LRU), F32)
    return pl.pallas_call(
        body, name=name, grid=(NCH,),
        in_specs=[col(0), col(0, 1), col(1), col(0), col(0, 1), _full((CONV_W, D_LRU)), vec, mat, vec, mat, vec, vec, vec,
                  col(2), col(3), col(4), col(5), pl.BlockSpec((None, HEADS, HD, HD), lambda n: (last - n, 0, 0, 0)),
                  pl.BlockSpec((CH, D), lambda n: (last - n, 0)), tab, tab, cst, cst, cst, _full((1, D_RET)),
                  pl.BlockSpec(memory_space=pl.ANY)],
        out_specs=[pl.BlockSpec((CH, D_IN), lambda n: (last - n, 0)), _full((LRU_VEC_ROWS, D_LRU)), mat, mat,
                   _full((1, D_RET))],
        out_shape=[_sds((T, D_IN), MXU_DTYPE), _sds((LRU_VEC_ROWS, D_LRU), F32), _sds((D_LRU, D_LRU), F32),
                   _sds((D_LRU, D_LRU), F32), _sds((1, D_RET), F32)],
        scratch_shapes=[blockbuf, blockbuf, blockbuf, pltpu.VMEM((1, D_LRU), F32), blockbuf,
                        pltpu.VMEM((HEADS, HD, HD), F32)],
        compiler_params=_params(("arbitrary",)),
    )(proj, proj, proj, hst, hst, cw, cb, wa, ba, wx, bx, lam, gain, proj, proj, proj, proj, states, dymix,
      cos, sin, dmask, xi, zeta, ret_gain, after)


HBM = pl.BlockSpec(memory_space=pltpu.HBM)


def _place():
    return lax.axis_index("x"), lax.axis_index("y"), lax.axis_index("c")


def all_gather(arrs, after, name):
    n = len(arrs)

    def body(*refs):
        ins, outs = refs[:n], refs[n + 1:2 * n + 1]
        send_sems, recv_sems, local_sems = refs[2 * n + 1:]
        x, y, c = _place()
        me, sibling = (x, y, c), (x, y, 1 - c)
        chips = [(1 - x, y), (x, 1 - y), (1 - x, 1 - y)]

        def copy(a, k, block, to, src=None):
            px, py, pc = block
            dst = outs[a].at[4 * px + 2 * py + pc]
            return pltpu.make_async_remote_copy(
                src_ref=dst if src is None else src, dst_ref=dst, send_sem=send_sems.at[a, k], recv_sem=recv_sems.at[a, k],
                device_id=to, device_id_type=MESH)

        mine = [pltpu.make_async_copy(ins[a], outs[a].at[4 * x + 2 * y + c], local_sems.at[a]) for a in range(n)]
        for cp in mine:
            cp.start()
        first = []
        for a in range(n):
            first.append(copy(a, 0, me, sibling, src=ins[a]))
            first += [copy(a, 1 + j, me, (*chip, c), src=ins[a]) for j, chip in enumerate(chips)]
        for cp in first:
            cp.start()
        passed = []
        for j, chip in enumerate(chips):
            for a in range(n):
                copy(a, 1 + j, (*chip, c), me).wait_recv()
                passed.append(copy(a, 4 + j, (*chip, c), sibling))
                passed[-1].start()
        for a in range(n):
            copy(a, 0, sibling, me).wait_recv()
            for j, chip in enumerate(chips):
                copy(a, 4 + j, (*chip, 1 - c), me).wait_recv()
        for cp in first + passed:
            cp.wait_send()
        for cp in mine:
            cp.wait()

    return pl.pallas_call(
        body, name=name,
        in_specs=[HBM] * n + [pl.BlockSpec(memory_space=pl.ANY)], out_specs=[HBM] * n,
        out_shape=[_sds((NDEV,) + a.shape, a.dtype) for a in arrs],
        scratch_shapes=[pltpu.SemaphoreType.DMA((n, 7)), pltpu.SemaphoreType.DMA((n, 7)), pltpu.SemaphoreType.DMA((n,))],
    )(*arrs, after)


SEM = pl.BlockSpec(memory_space=pltpu.SEMAPHORE)
ANY = pl.BlockSpec(memory_space=pl.ANY)
EFFECT = pltpu.SideEffectType.DATAFLOW_SIDE_EFFECTING


def _hbm(a):
    return pltpu.with_memory_space_constraint(a, pltpu.HBM)


def _hbm_like(arrs):
    return [pltpu.HBM(a.shape, a.dtype) for a in arrs]


def _dma_sems(count):
    return [pltpu.SemaphoreType.DMA(())] * count


def _ag_copy(lands, send_sems, recv_sems, per):
    def copy(a, k, block, to, src=None):
        px, py, pc = block
        dst = lands[a].at[4 * px + 2 * py + pc]
        return pltpu.make_async_remote_copy(
            src_ref=dst if src is None else src, dst_ref=dst, send_sem=send_sems[a * per + k], recv_sem=recv_sems[a * per + k],
            device_id=to, device_id_type=MESH)
    return copy


def to_wire(sel, w_in, w_out, w_gate, w_up, w_down, name):
    ffpad = FF_SHP - FF_SH

    def body(sel_ref, i_ref, o_ref, g_ref, u_ref, d_ref, oi, oo, og, ou, od):
        del sel_ref
        oi[...] = i_ref[...].astype(oi.dtype)
        oo[...] = o_ref[...].astype(oo.dtype)
        for src, dst in ((g_ref, og), (u_ref, ou), (d_ref, od)):
            dst[:FF_SH, :] = src[...].astype(dst.dtype)
            dst[FF_SH:, :] = jnp.zeros((ffpad, D), dst.dtype)

    shapes_in = [(D, IN_SH), (OUT_SH, D), (FF_SH, D), (FF_SH, D), (FF_SH, D)]
    shapes_out = [(D, IN_SH), (OUT_SH, D), (FF_SHP, D), (FF_SHP, D), (FF_SHP, D)]
    return pl.pallas_call(
        body, name=name,
        grid_spec=pltpu.PrefetchScalarGridSpec(
            num_scalar_prefetch=1, grid=(1,),
            in_specs=[pl.BlockSpec((None,) + s, lambda i, sel_ref: (sel_ref[1], 0, 0)) for s in shapes_in],
            out_specs=[pl.BlockSpec((None,) + s, lambda i, sel_ref: (sel_ref[0], 0, 0)) for s in shapes_out]),
        out_shape=[_sds((NDEV,) + s, WIRE_DTYPE) for s in shapes_out], compiler_params=_params(("arbitrary",)),
    )(sel, w_in, w_out, w_gate, w_up, w_down)


def place_blocks(sel, arrs, name):
    n = len(arrs)

    def body(sel_ref, *refs):
        del sel_ref
        for a in range(n):
            refs[n + a][...] = refs[a][...]

    def whole(a):
        nd = a.ndim
        return pl.BlockSpec(a.shape, lambda i, sel_ref: (0,) * nd)

    def mine(a):
        nd = a.ndim
        return pl.BlockSpec((None,) + a.shape, lambda i, sel_ref: (sel_ref[0],) + (0,) * nd)

    return pl.pallas_call(
        body, name=name,
        grid_spec=pltpu.PrefetchScalarGridSpec(
            num_scalar_prefetch=1, grid=(1,), in_specs=[whole(a) for a in arrs], out_specs=[mine(a) for a in arrs]),
        out_shape=[_sds((NDEV,) + a.shape, a.dtype) for a in arrs], compiler_params=_params(("arbitrary",)),
    )(sel, *arrs)


def ag_start(lands, after, name):
    n = len(lands)
    ns = 4 * n

    def body(*refs):
        lnd = refs[:n]
        send_sems, recv_sems = refs[n + 1:n + 1 + ns], refs[n + 1 + ns:n + 1 + 2 * ns]
        token = refs[-1]
        x, y, c = _place()
        me, sibling = (x, y, c), (x, y, 1 - c)
        chips = [(1 - x, y), (x, 1 - y), (1 - x, 1 - y)]
        copy = _ag_copy(lnd, send_sems, recv_sems, 4)
        for a in range(n):
            copy(a, 0, me, sibling).start()
            for j, chip in enumerate(chips):
                copy(a, 1 + j, me, (*chip, c)).start()
        token[...] = jnp.zeros_like(token)

    outs = pl.pallas_call(
        body, name=name,
        in_specs=[HBM] * n + [ANY],
        out_specs=[SEM] * (2 * ns) + [HBM] * n + [pl.BlockSpec(memory_space=pltpu.VMEM)],
        out_shape=_dma_sems(2 * ns) + _hbm_like(lands) + [_sds((8, 128), F32)],
        input_output_aliases={i: 2 * ns + i for i in range(n)},
        compiler_params=pltpu.CompilerParams(has_side_effects=EFFECT),
    )(*[_hbm(a) for a in lands], after)
    return outs[:ns], outs[ns:2 * ns], outs[2 * ns:2 * ns + n], outs[-1]


def ag_forward(send_sems, recv_sems, lands, after, name):
    n = len(lands)
    n1, n2 = 4 * n, 3 * n

    def body(*refs):
        lnd = refs[:n]
        o = n
        s1, r1 = refs[o:o + n1], refs[o + n1:o + 2 * n1]
        o += 2 * n1 + 1
        s2, r2 = refs[o:o + n2], refs[o + n2:o + 2 * n2]
        token = refs[-1]
        token[...] = jnp.zeros_like(token)
        x, y, c = _place()
        me, sibling = (x, y, c), (x, y, 1 - c)
        chips = [(1 - x, y), (x, 1 - y), (1 - x, 1 - y)]
        copy1 = _ag_copy(lnd, s1, r1, 4)
        copy2 = _ag_copy(lnd, s2, r2, 3)
        for j, chip in enumerate(chips):
            for a in range(n):
                copy1(a, 1 + j, (*chip, c), me).wait_recv()
                copy2(a, j, (*chip, c), sibling).start()
        for a in range(n):
            copy1(a, 0, sibling, me).wait_recv()
            copy1(a, 0, me, sibling).wait_send()
            for j, chip in enumerate(chips):
                copy1(a, 1 + j, me, (*chip, c)).wait_send()

    outs = pl.pallas_call(
        body, name=name,
        in_specs=[HBM] * n + [SEM] * (2 * n1) + [ANY],
        out_specs=[SEM] * (2 * n2) + [HBM] * n + [pl.BlockSpec(memory_space=pltpu.VMEM)],
        out_shape=_dma_sems(2 * n2) + _hbm_like(lands) + [_sds((8, 128), F32)],
        input_output_aliases={i: 2 * n2 + i for i in range(n)},
        compiler_params=pltpu.CompilerParams(has_side_effects=EFFECT),
    )(*lands, *send_sems, *recv_sems, after)
    return outs[:n2], outs[n2:2 * n2], outs[2 * n2:2 * n2 + n], outs[-1]


def ag_finish(send_sems, recv_sems, lands, after, name):
    n = len(lands)
    n2 = 3 * n

    def body(*refs):
        lnd = refs[:n]
        s2, r2 = refs[n:n + n2], refs[n + n2:n + 2 * n2]
        x, y, c = _place()
        me, sibling = (x, y, c), (x, y, 1 - c)
        chips = [(1 - x, y), (x, 1 - y), (1 - x, 1 - y)]
        copy2 = _ag_copy(lnd, s2, r2, 3)
        for a in range(n):
            for j, chip in enumerate(chips):
                copy2(a, j, (*chip, c), sibling).wait_send()
                copy2(a, j, (*chip, 1 - c), me).wait_recv()

    outs = pl.pallas_call(
        body, name=name,
        in_specs=[HBM] * n + [SEM] * (2 * n2) + [ANY],
        out_specs=[HBM] * n, out_shape=_hbm_like(lands),
        input_output_aliases={i: i for i in range(n)},
        compiler_params=pltpu.CompilerParams(has_side_effects=EFFECT),
    )(*lands, *send_sems, *recv_sems, after)
    return list(outs)


def rs_sibling_start(arrs, name):
    n = len(arrs)
    ns = 4 * n
    lands = [lax.empty((4,) + a.shape[1:], a.dtype) for a in arrs]

    def body(*refs):
        ins, lnd = refs[:n], refs[n:2 * n]
        send_sems, recv_sems = refs[2 * n:2 * n + ns], refs[2 * n + ns:2 * n + 2 * ns]
        x, y, c = _place()
        sibling = (x, y, 1 - c)
        for a in range(n):
            for p in range(4):
                pltpu.make_async_remote_copy(
                    src_ref=ins[a].at[2 * p + 1 - c], dst_ref=lnd[a].at[p], send_sem=send_sems[4 * a + p],
                    recv_sem=recv_sems[4 * a + p], device_id=sibling, device_id_type=MESH).start()
        refs[-1][...] = jnp.zeros_like(refs[-1])

    outs = pl.pallas_call(
        body, name=name,
        in_specs=[HBM] * (2 * n), out_specs=[SEM] * (2 * ns) + [HBM] * (2 * n) + [pl.BlockSpec(memory_space=pltpu.VMEM)],
        out_shape=_dma_sems(2 * ns) + _hbm_like(arrs) + _hbm_like(lands) + [_sds((8, 128), F32)],
        input_output_aliases={i: 2 * ns + i for i in range(2 * n)},
        compiler_params=pltpu.CompilerParams(has_side_effects=EFFECT),
    )(*[_hbm(a) for a in arrs], *[_hbm(a) for a in lands])
    return (outs[:ns], outs[ns:2 * ns], outs[2 * ns:2 * ns + n], outs[2 * ns + n:2 * ns + 2 * n]), outs[-1]


def rs_sibling_wait(send_sems, recv_sems, arrs, lands, after, name):
    n = len(arrs)
    ns = 4 * n

    def body(*refs):
        ins, lnd = refs[:n], refs[n:2 * n]
        s, r = refs[2 * n:2 * n + ns], refs[2 * n + ns:2 * n + 2 * ns]
        x, y, c = _place()
        sibling = (x, y, 1 - c)
        for a in range(n):
            for p in range(4):
                cp = pltpu.make_async_remote_copy(
                    src_ref=ins[a].at[2 * p + 1 - c], dst_ref=lnd[a].at[p], send_sem=s[4 * a + p], recv_sem=r[4 * a + p],
                    device_id=sibling, device_id_type=MESH)
                cp.wait_send()
                cp.wait_recv()

    outs = pl.pallas_call(
        body, name=name,
        in_specs=[HBM] * (2 * n) + [SEM] * (2 * ns) + [ANY], out_specs=[HBM] * (2 * n),
        out_shape=_hbm_like(arrs) + _hbm_like(lands),
        input_output_aliases={i: i for i in range(2 * n)},
        compiler_params=pltpu.CompilerParams(has_side_effects=EFFECT),
    )(*arrs, *lands, *send_sems, *recv_sems, after)
    return outs[:n], outs[n:]


def rs_chips_start(parts, name):
    n = len(parts)
    ns = 3 * n
    lands = [lax.empty((3,) + a.shape[1:], a.dtype) for a in parts]

    def body(*refs):
        ins, lnd = refs[:n], refs[n:2 * n]
        send_sems, recv_sems = refs[2 * n:2 * n + ns], refs[2 * n + ns:2 * n + 2 * ns]
        x, y, c = _place()
        chips = [(1 - x, y), (x, 1 - y), (1 - x, 1 - y)]
        for a in range(n):
            for k, (tx, ty) in enumerate(chips):
                pltpu.make_async_remote_copy(
                    src_ref=ins[a].at[2 * tx + ty], dst_ref=lnd[a].at[k], send_sem=send_sems[3 * a + k],
                    recv_sem=recv_sems[3 * a + k], device_id=(tx, ty, c), device_id_type=MESH).start()
        refs[-1][...] = jnp.zeros_like(refs[-1])

    outs = pl.pallas_call(
        body, name=name,
        in_specs=[HBM] * (2 * n), out_specs=[SEM] * (2 * ns) + [HBM] * (2 * n) + [pl.BlockSpec(memory_space=pltpu.VMEM)],
        out_shape=_dma_sems(2 * ns) + _hbm_like(parts) + _hbm_like(lands) + [_sds((8, 128), F32)],
        input_output_aliases={i: 2 * ns + i for i in range(2 * n)},
        compiler_params=pltpu.CompilerParams(has_side_effects=EFFECT),
    )(*[_hbm(a) for a in parts], *[_hbm(a) for a in lands])
    return (outs[:ns], outs[ns:2 * ns], outs[2 * ns:2 * ns + n], outs[2 * ns + n:2 * ns + 2 * n]), outs[-1]


def rs_chips_wait(send_sems, recv_sems, parts, lands, after, name):
    n = len(parts)
    ns = 3 * n

    def body(*refs):
        ins, lnd = refs[:n], refs[n:2 * n]
        s, r = refs[2 * n:2 * n + ns], refs[2 * n + ns:2 * n + 2 * ns]
        x, y, c = _place()
        chips = [(1 - x, y), (x, 1 - y), (1 - x, 1 - y)]
        for a in range(n):
            for k, (tx, ty) in enumerate(chips):
                cp = pltpu.make_async_remote_copy(
                    src_ref=ins[a].at[2 * tx + ty], dst_ref=lnd[a].at[k], send_sem=s[3 * a + k], recv_sem=r[3 * a + k],
                    device_id=(tx, ty, c), device_id_type=MESH)
                cp.wait_send()
                cp.wait_recv()

    outs = pl.pallas_call(
        body, name=name,
        in_specs=[HBM] * (2 * n) + [SEM] * (2 * ns) + [ANY], out_specs=[HBM] * (2 * n),
        out_shape=_hbm_like(parts) + _hbm_like(lands),
        input_output_aliases={i: i for i in range(2 * n)},
        compiler_params=pltpu.CompilerParams(has_side_effects=EFFECT),
    )(*parts, *lands, *send_sems, *recv_sems, after)
    return outs[:n], outs[n:]


def pair_sum(arrs, recv, c, name):
    n = len(arrs)

    def body(c_ref, *refs):
        del c_ref
        for a in range(n):
            refs[2 * n + a][...] = (refs[a][...].astype(F32) + refs[n + a][...].astype(F32)).astype(refs[2 * n + a].dtype)

    mine = [pl.BlockSpec((None,) + a.shape[1:], lambda p, c_ref: (2 * p + c_ref[0], 0, 0)) for a in arrs]
    other = [pl.BlockSpec((None,) + a.shape[1:], lambda p, c_ref: (p, 0, 0)) for a in arrs]
    return pl.pallas_call(
        body, name=name,
        grid_spec=pltpu.PrefetchScalarGridSpec(num_scalar_prefetch=1, grid=(4,), in_specs=mine + other, out_specs=other),
        out_shape=[_sds((4,) + a.shape[1:], a.dtype) for a in arrs], compiler_params=_params(("parallel",)),
    )(c, *arrs, *recv)


def _adamw(w, g, m, v):
    m = ADAM_B1 * m + (1.0 - ADAM_B1) * g
    v = ADAM_B2 * v + (1.0 - ADAM_B2) * jnp.square(g)
    m_hat = m / (1.0 - ADAM_B1 ** ADAM_STEP)
    v_hat = v / (1.0 - ADAM_B2 ** ADAM_STEP)
    return -ADAM_LR * (m_hat / (jnp.sqrt(v_hat) + ADAM_EPS) + ADAM_WD * w), m, v


def adamw_big(recv, sums, chip, w, m, v, tr, name):
    nl, rr, cc = w.shape
    cp = recv[0].shape[2]

    def body(chip_ref, *refs):
        del chip_ref
        rcv, own = refs[:nl], refs[nl:2 * nl]
        w_ref, m_ref, v_ref, g_out, d_out, m_out, v_out = refs[2 * nl:]
        for l in range(nl):
            g = ((own[l][...].astype(F32) + rcv[l][0].astype(F32)) + rcv[l][1].astype(F32)) + rcv[l][2].astype(F32)
            g = g[:, :cc]
            g_out[l] = g
            d_out[l], m_out[l], v_out[l] = _adamw(w_ref[l], g, m_ref[l], v_ref[l])

    blk = pl.BlockSpec((nl, tr, cc), lambda i, chip_ref: (0, i, 0))
    return pl.pallas_call(
        body, name=name,
        grid_spec=pltpu.PrefetchScalarGridSpec(
            num_scalar_prefetch=1, grid=(rr // tr,),
            in_specs=[pl.BlockSpec((3, tr, cp), lambda i, chip_ref: (0, i, 0))] * nl
            + [pl.BlockSpec((None, tr, cp), lambda i, chip_ref: (chip_ref[0], i, 0))] * nl + [blk, blk, blk],
            out_specs=[blk] * 4),
        out_shape=[_sds(w.shape, F32)] * 4, compiler_params=_params(("parallel",)),
    )(chip, *recv, *sums, w, m, v)


SMALL_ROWS = 24


def small_grads(lvec, g_ret, g_mix, g_ffn, g_final, loss_part, dwa, dwx, name):
    def body(lvec_ref, ret_ref, mix_ref, ffn_ref, fin_ref, loss_ref, dwa_ref, dwx_ref, v_ref, g_ref):
        v_ref[16:SMALL_ROWS, :] = jnp.zeros((SMALL_ROWS - 16, D_LRU), F32)
        v_ref[16:17, 0:128] = loss_ref[0:1, :]
        v_ref[0:9, :] = lvec_ref[0:9, :]
        v_ref[9:10, :] = ret_ref[...]
        for r, src in ((10, mix_ref), (12, ffn_ref), (14, fin_ref)):
            v_ref[r:r + 1, :] = src[:, :D_LRU]
            v_ref[r + 1:r + 2, :] = src[:, D_LRU:]
        for k, src in enumerate((dwa_ref, dwx_ref)):
            for g in range(LRU_BLOCKS):
                rows = slice(LRU_BD * g, LRU_BD * (g + 1))
                g_ref[D_LRU * k + LRU_BD * g:D_LRU * k + LRU_BD * (g + 1), :] = src[rows, rows]

    ins = [lvec, g_ret, g_mix, g_ffn, g_final, loss_part, dwa, dwx]
    return pl.pallas_call(
        body, name=name, grid=(1,), in_specs=[_full(a.shape) for a in ins],
        out_specs=[_full((SMALL_ROWS, D_LRU)), _full((2 * D_LRU, LRU_BD))],
        out_shape=[_sds((SMALL_ROWS, D_LRU), F32), _sds((2 * D_LRU, LRU_BD), F32)], compiler_params=_params(("arbitrary",)),
    )(*ins)


def sum_devices(arrs, name):
    n = len(arrs)

    def body(*refs):
        for a in range(n):
            acc = refs[a][0]
            for j in range(1, NDEV):
                acc = acc + refs[a][j]
            refs[n + a][...] = acc

    return pl.pallas_call(
        body, name=name, grid=(1,), in_specs=[_full(a.shape) for a in arrs], out_specs=[_full(a.shape[1:]) for a in arrs],
        out_shape=[_sds(a.shape[1:], F32) for a in arrs], compiler_params=_params(("arbitrary",)),
    )(*arrs)


def adamw_small(gs, ws, ms, vs, name):
    n = len(gs)

    def body(*refs):
        for a in range(n):
            g, w, m, v = (refs[k * n + a][...] for k in range(4))
            refs[4 * n + a][...], refs[5 * n + a][...], refs[6 * n + a][...] = _adamw(w, g, m, v)

    specs = [_full(a.shape) for a in ws]
    outs = pl.pallas_call(
        body, name=name, grid=(1,), in_specs=specs * 4, out_specs=specs * 3, out_shape=[_sds(a.shape, F32) for a in ws] * 3,
        compiler_params=_params(("arbitrary",)),
    )(*gs, *ws, *ms, *vs)
    return outs[:n], outs[n:2 * n], outs[2 * n:]


def block_diag(wa, wx, name):
    def body(wa_ref, wx_ref, oa_ref, ox_ref):
        for src, dst in ((wa_ref, oa_ref), (wx_ref, ox_ref)):
            dst[...] = jnp.zeros_like(dst)
            for g in range(LRU_BLOCKS):
                rows = slice(LRU_BD * g, LRU_BD * (g + 1))
                dst[rows, rows] = src[g].astype(dst.dtype)

    ispec = pl.BlockSpec((None, LRU_BLOCKS, LRU_BD, LRU_BD), lambda l: (l, 0, 0, 0))
    ospec = pl.BlockSpec((None, D_LRU, D_LRU), lambda l: (l, 0, 0))
    return pl.pallas_call(
        body, name=name, grid=(wa.shape[0],), in_specs=[ispec, ispec], out_specs=[ospec, ospec],
        out_shape=[_sds((wa.shape[0], D_LRU, D_LRU), MXU_DTYPE)] * 2, compiler_params=_params(("parallel",)),
    )(wa, wx)


REP_NAMES = ["norm_mix", "conv_b", "gate_a_w", "gate_a_b", "gate_x_w", "gate_x_b", "lru_lambda", "lru_out_norm",
             "ret_out_norm", "norm_ffn", "norm_final"]


def kernel(x, meta_tokens, norm_mix, w_in, conv_w, conv_b, gate_a_w, gate_a_b, gate_x_w, gate_x_b, lru_lambda, lru_out_norm, ret_out_norm, w_out, norm_ffn, w_gate, w_up, w_down, norm_final, loss_target, m_meta_tokens, m_norm_mix, m_w_in, m_conv_w, m_conv_b, m_gate_a_w, m_gate_a_b, m_gate_x_w, m_gate_x_b, m_lru_lambda, m_lru_out_norm, m_ret_out_norm, m_w_out, m_norm_ffn, m_w_gate, m_w_up, m_w_down, m_norm_final, v_meta_tokens, v_norm_mix, v_w_in, v_conv_w, v_conv_b, v_gate_a_w, v_gate_a_b, v_gate_x_w, v_gate_x_b, v_lru_lambda, v_lru_out_norm, v_ret_out_norm, v_w_out, v_norm_ffn, v_w_gate, v_w_up, v_w_down, v_norm_final):
    xi, yi, ci = _place()
    dev = 4 * xi + 2 * yi + ci
    c_arr = jnp.reshape(ci, (1,)).astype(jnp.int32)
    dev_arr = jnp.reshape(dev, (1,)).astype(jnp.int32)

    meta_g, conv_g = all_gather([meta_tokens, conv_w], c_arr, "ag_small")
    meta_full = jnp.transpose(meta_g, (1, 0, 2)).reshape(N_META, D)
    conv_full = jnp.transpose(conv_g, (1, 2, 0, 3)).reshape(DEPTH, CONV_W, D_LRU)
    tr_ = lambda a: jnp.transpose(a, (0, 2, 1))
    w_gate_t, m_w_gate_t, v_w_gate_t = tr_(w_gate), tr_(m_w_gate), tr_(v_w_gate)
    w_up_t, m_w_up_t, v_w_up_t = tr_(w_up), tr_(m_w_up), tr_(v_w_up)
    level1 = []
    token = meta_g
    for l in range(DEPTH):
        sel = jnp.stack([dev, jnp.int32(l)]).astype(jnp.int32)
        lands = to_wire(sel, w_in, w_out, w_gate_t, w_up_t, w_down, "to_wire")
        s1, r1, lands, token = ag_start(lands, token, f"ag_start_{l}")
        level1.append((s1, r1, lands))

    def as_weights(gi, go, gg, gu, gd):
        return dict(w_in=gi, w_out=go.reshape(D, D), w_gate=gg.reshape(D_FFP, D), w_up=gu.reshape(D_FFP, D),
                    w_down=gd.reshape(D_FFP, D))

    tables = _ret_tables()
    row = lambda a: a.reshape(1, -1)

    h = jnp.concatenate([jnp.zeros((PAD, D), F32), meta_full, x[0]], axis=0)
    saved, gathered = [], []
    s1, r1, lands = level1[0]
    s2, r2, first, order = ag_forward(s1[:4], r1[:4], lands[:1], token, "ag_forward_0_w_in")
    w_in_next = ag_finish(s2, r2, first, h, "ag_finish_0_w_in")[0]
    wa_dense, wx_dense = block_diag(gate_a_w, gate_x_w, "block_diag")
    for l in range(DEPTH):
        small = dict(cw=conv_full[l], cb=row(conv_b[l]), wa=wa_dense[l], ba=row(gate_a_b[l]),
                     wx=wx_dense[l], bx=row(gate_x_b[l]), lam=row(lru_lambda[l]),
                     gain=row(lru_out_norm[l]))
        s1, r1, lands = level1[l]
        hn1 = rmsnorm_fwd(h, row(norm_mix[l]), "rms_fwd")
        proj = mm_blocked_nn(hn1, w_in_next, F32, "proj")
        if l > 1:
            s2, r2, rest, order = ag_forward(s1[4:], r1[4:], lands[1:], proj, f"ag_forward_{l}_rest")
            ymix, hst, states = mix_fwd(proj, tables=tables, ret_gain=row(ret_out_norm[l]), after=order, name="mix_fwd", **small)
            w = as_weights(w_in_next, *ag_finish(s2, r2, rest, ymix, f"ag_finish_{l}_rest"))
            h_mid = mm_nn_res(ymix, w["w_out"], h, order, "out_proj")
        else:
            ymix, hst, states = mix_fwd(proj, tables=tables, ret_gain=row(ret_out_norm[l]), after=order, name="mix_fwd", **small)
            s2, r2, mid, order = ag_forward(s1[4:16], r1[4:16], lands[1:4], ymix, f"ag_forward_{l}_mid")
            mids = ag_finish(s2, r2, mid, order, f"ag_finish_{l}_mid")
            w = dict(w_in=w_in_next, w_out=mids[0].reshape(D, D), w_gate=mids[1].reshape(D_FFP, D), w_up=mids[2].reshape(D_FFP, D))
            h_mid = mm_nn_res(ymix, w["w_out"], h, order, "out_proj")
            s2d, r2d, down, order = ag_forward(s1[16:], r1[16:], lands[4:], h_mid, f"ag_forward_{l}_down")
        hn2 = rmsnorm_fwd(h_mid, row(norm_ffn[l]), "rms_fwd")
        act_dgate, act_dup, act = ffn_up(hn2, w["w_gate"], w["w_up"], "ffn_up")
        if l <= 1:
            w["w_down"] = ag_finish(s2d, r2d, down, act, f"ag_finish_{l}_down")[0].reshape(D_FFP, D)
        gathered.append(w)
        if l + 1 < DEPTH:
            s1n, r1n, landsn = level1[l + 1]
            s2, r2, first, order = ag_forward(s1n[:4], r1n[:4], landsn[:1], act, f"ag_forward_{l + 1}_w_in")
        h_out = mm_nn_res(act, w["w_down"], h_mid, order, "ffn_down")
        if l + 1 < DEPTH:
            w_in_next = ag_finish(s2, r2, first, h_out, f"ag_finish_{l + 1}_w_in")[0]
        saved.append(dict(h=h, hn1=hn1, proj=proj, hst=hst, states=states, ymix=ymix, h_mid=h_mid, hn2=hn2, act_dgate=act_dgate, act_dup=act_dup,
                          act=act, small=small))
        h = h_out

    loss_p, dh, dh_b, g_norm_final = loss_head(h, row(norm_final), loss_target[0], "loss_head")

    small_v = [None] * DEPTH
    small_w = [None] * DEPTH
    inflight = []
    order = loss_p

    def sibling_done(l, tag, names, sib, after):
        parts, got = rs_sibling_wait(*sib, after, f"rs_sibling_wait_{tag}")
        sums = pair_sum(parts, got, c_arr, "pair_sum")
        flying, started = rs_chips_start(sums, f"rs_chips_start_{tag}")
        inflight.append((l, tag, names, flying))
        return started

    for l in reversed(range(DEPTH)):
        w, s = gathered[l], saved[l]
        dgate, dup, dwd = ffn_down_bwd(dh_b, w["w_down"], s["act_dgate"], s["act_dup"], s["act"], order, "ffn_down_bwd")
        dwd = dwd.reshape(NDEV, FF_SHP, D)
        dwg, dwu = (g.reshape(NDEV, FF_SHP, D) for g in mm_tn_two(dgate, dup, s["hn2"], PAIR, order, "dw_rows"))
        split = l <= 1
        if split:
            ffn_sib, order = rs_sibling_start([dwg, dwu, dwd], f"rs_sibling_start_{l}_ffn")
        dhn2 = mm_rows_nn([(dgate, w["w_gate"]), (dup, w["w_up"])], order, "ffn_up_bwd")
        if split:
            order = sibling_done(l, f"{l}_ffn", ("w_gate", "w_up", "w_down"), ffn_sib, dhn2)
        dh_mid, dh_mid_b, g_norm_ffn = rmsnorm_bwd(s["h_mid"], row(norm_ffn[l]), dhn2, dh, order, "rms_bwd")
        dymix, dwo = out_proj_bwd(dh_mid_b, w["w_out"], s["ymix"], order, "out_proj_bwd")
        dwo = dwo.reshape(NDEV, OUT_SH, D)
        dproj, lvec, dwa, dwx, g_ret_norm = mix_bwd(s["proj"], s["hst"], s["states"], dymix, tables=tables,
                                                    ret_gain=row(ret_out_norm[l]), after=order, name="mix_bwd", **s["small"])
        dhn1, dwi = proj_bwd(dproj, w["w_in"], s["hn1"], order, "proj_bwd")
        if split:
            sib_tag, sib_names = f"{l}_mix", ("w_in", "w_out")
            sib, order = rs_sibling_start([dwi, dwo], f"rs_sibling_start_{l}_mix")
        else:
            sib_tag, sib_names = str(l), ("w_in", "w_gate", "w_up", "w_out", "w_down")
            sib, order = rs_sibling_start([dwi, dwg, dwu, dwo, dwd], f"rs_sibling_start_{l}")
        dh, dh_b, g_norm_mix = rmsnorm_bwd(s["h"], row(norm_mix[l]), dhn1, dh_mid, order, "rms_bwd")
        order = sibling_done(l, sib_tag, sib_names, sib, dh)

        g_fin, loss_part = (g_norm_final, loss_p) if l == 0 else (jnp.zeros((1, D), F32), jnp.zeros((8, 128), F32))
        small_v[l], small_w[l] = small_grads(lvec, g_ret_norm, g_norm_mix, g_norm_ffn, g_fin, loss_part, dwa, dwx,
                                             "small_grads")
        if l == 1:
            early = place_blocks(dev_arr, [jnp.stack(small_v[1:]), jnp.stack(small_w[1:])], "place_grads")
            early_sems = ag_start(early, order, "ag_start_grads")
            order = early_sems[3]

    grad_x = dh[X0:][None]
    g_meta = dh[PAD:X0]

    arrived = {}

    def wait_for(entries, after):
        for l, tag, names, flying in entries:
            sums, recv = rs_chips_wait(*flying, after, f"rs_chips_wait_{tag}")
            for i, n in enumerate(names):
                arrived[l, n] = (recv[i], sums[i])

    chip = jnp.reshape(2 * xi + yi, (1,)).astype(jnp.int32)

    def finish(wname, w_, m_, v_, tr):
        return adamw_big([arrived[l, wname][0] for l in range(DEPTH)], [arrived[l, wname][1] for l in range(DEPTH)], chip,
                         w_, m_, v_, tr, "adamw_" + wname)

    late_s1, late_r1, late_lands, late_started = ag_start(
        place_blocks(dev_arr, [small_v[0], small_w[0], g_meta], "place_late"), order, "ag_start_late")
    wait_for(inflight[:-1], late_started)
    o_gate = [tr_(o) for o in finish("w_gate", w_gate_t, m_w_gate_t, v_w_gate_t, 32)]
    o_up = [tr_(o) for o in finish("w_up", w_up_t, m_w_up_t, v_w_up_t, 32)]
    o_down = finish("w_down", w_down, m_w_down, v_w_down, 32)

    s2, r2, lands, _ = ag_forward(late_s1, late_r1, late_lands, o_down[0], "ag_forward_late")
    late = ag_finish(s2, r2, lands, o_down[0], "ag_finish_late")
    s2, r2, lands, _ = ag_forward(early_sems[0], early_sems[1], early_sems[2], dh, "ag_forward_grads")
    gath_early = ag_finish(s2, r2, lands, late[0], "ag_finish_grads")
    v0, w0, meta_sum, v123, w123 = sum_devices(list(late) + list(gath_early), "sum_devices")
    loss = v0[16, 0]
    vecs = jnp.concatenate([v0[None], v123])
    gws = jnp.concatenate([w0[None], w123])
    blocks = (DEPTH, LRU_BLOCKS, LRU_BD)
    small_g = dict(
        conv_w=lax.dynamic_slice_in_dim(vecs[:, 0:CONV_W], dev * (D_LRU // NDEV), D_LRU // NDEV, axis=2),
        conv_b=vecs[:, 4], gate_a_b=vecs[:, 5].reshape(blocks), gate_x_b=vecs[:, 6].reshape(blocks),
        lru_lambda=vecs[:, 7], lru_out_norm=vecs[:, 8], ret_out_norm=vecs[:, 9],
        norm_mix=vecs[:, 10:12].reshape(DEPTH, D), norm_ffn=vecs[:, 12:14].reshape(DEPTH, D),
        norm_final=v0[14:16].reshape(1, D),
        gate_a_w=gws[:, :D_LRU].reshape(blocks + (LRU_BD,)), gate_x_w=gws[:, D_LRU:].reshape(blocks + (LRU_BD,)),
        meta_tokens=lax.dynamic_slice_in_dim(meta_sum, dev * (D // NDEV), D // NDEV, axis=1))
    given = dict(norm_mix=(norm_mix, m_norm_mix, v_norm_mix), conv_b=(conv_b, m_conv_b, v_conv_b),
                 gate_a_w=(gate_a_w, m_gate_a_w, v_gate_a_w), gate_a_b=(gate_a_b, m_gate_a_b, v_gate_a_b),
                 gate_x_w=(gate_x_w, m_gate_x_w, v_gate_x_w), gate_x_b=(gate_x_b, m_gate_x_b, v_gate_x_b),
                 lru_lambda=(lru_lambda, m_lru_lambda, v_lru_lambda), lru_out_norm=(lru_out_norm, m_lru_out_norm, v_lru_out_norm),
                 ret_out_norm=(ret_out_norm, m_ret_out_norm, v_ret_out_norm), norm_ffn=(norm_ffn, m_norm_ffn, v_norm_ffn),
                 norm_final=tuple(a.reshape(1, D) for a in (norm_final, m_norm_final, v_norm_final)),
                 conv_w=(conv_w, m_conv_w, v_conv_w), meta_tokens=(meta_tokens, m_meta_tokens, v_meta_tokens))
    small_names = REP_NAMES + ["conv_w", "meta_tokens"]
    upd = adamw_small([small_g[n] for n in small_names], *[[given[n][k] for n in small_names] for k in range(3)],
                      "adamw_small")
    small_out = [dict(zip(small_names, u)) for u in upd]
    for d_ in [small_g] + small_out:
        d_["norm_final"] = d_["norm_final"].reshape(D)

    wait_for(inflight[-1:], upd[0][0])
    o_in = finish("w_in", w_in, m_w_in, v_w_in, 256)
    o_out = finish("w_out", w_out, m_w_out, v_w_out, 64)

    bigs = dict(w_in=o_in, w_out=o_out, w_gate=o_gate, w_up=o_up, w_down=o_down)
    order = ["meta_tokens", "norm_mix", "w_in", "conv_w", "conv_b", "gate_a_w", "gate_a_b", "gate_x_w", "gate_x_b", "lru_lambda",
             "lru_out_norm", "ret_out_norm", "w_out", "norm_ffn", "w_gate", "w_up", "w_down", "norm_final"]
    grads = [bigs[n][0] if n in bigs else small_g[n] for n in order]
    rest = [[bigs[n][k + 1] if n in bigs else small_out[k][n] for n in order] for k in range(3)]
    return (loss, grad_x, *grads, *rest[0], *rest[1], *rest[2])
```

```python
import numpy as np
import jax
import jax.numpy as jnp
from jax import lax
from jax.experimental import pallas as pl
from jax.experimental.pallas import tpu as pltpu

F32, BF16 = jnp.float32, jnp.bfloat16
MXU_DTYPE = BF16
WIRE_DTYPE = BF16

D = 1024
SEQ = 2048
DEPTH = 4
N_META = 16
CH = 128
PAD = (-(SEQ + N_META)) % CH
T = SEQ + N_META + PAD
NCH = T // CH
X0 = PAD + N_META
D_LRU = 512
LRU_BLOCKS = 8
LRU_BD = 64
CONV_W = 4
LRU_C = 8.0
D_RET = 512
HEADS = 4
HD = 128
ROPE_BASE = 10000.0
D_IN = 3072
D_FF = 2816
NDEV = 8
IN_SH = D_IN // NDEV
FF_SH = D_FF // NDEV
FF_SHP = 384
D_FFP = NDEV * FF_SHP
OUT_SH = D // NDEV
EPS = 1e-6
TM = 544
VMEM_LIMIT = 56 * 2**20
MESH = pl.DeviceIdType.MESH

ADAM_LR, ADAM_B1, ADAM_B2, ADAM_EPS, ADAM_WD, ADAM_STEP = 0.001, 0.9, 0.999, 1e-08, 0.01, 10

NN = ((1,), (0,))
NT = ((1,), (1,))
TN = ((0,), (0,))


def _dot(a, b, dims):
    return lax.dot_general(a.astype(MXU_DTYPE), b.astype(MXU_DTYPE), (dims, ((), ())), preferred_element_type=F32)


def _sds(shape, dtype):
    return jax.ShapeDtypeStruct(shape, dtype)


def _params(sem=None):
    return pltpu.CompilerParams(dimension_semantics=sem, vmem_limit_bytes=VMEM_LIMIT)


def _full(shape):
    n = len(shape)
    return pl.BlockSpec(shape, lambda *_: (0,) * n)


def rmsnorm_fwd(h, gain, name):
    def body(h_ref, g_ref, o_ref):
        x = h_ref[...]
        ms = jnp.mean(x * x, axis=-1, keepdims=True)
        o_ref[...] = (x * lax.rsqrt(ms + EPS) * g_ref[...]).astype(o_ref.dtype)

    return pl.pallas_call(
        body, name=name, grid=(T // TM,),
        in_specs=[pl.BlockSpec((TM, D), lambda i: (i, 0)), _full((1, D))],
        out_specs=pl.BlockSpec((TM, D), lambda i: (i, 0)),
        out_shape=_sds((T, D), MXU_DTYPE), compiler_params=_params(("parallel",)),
    )(h, gain)


def rmsnorm_bwd(h, gain, dhn, dres, after, name):
    def body(h_ref, g_ref, dhn_ref, dres_ref, after_ref, dh_ref, dhb_ref, dg_ref):
        del after_ref
        x = h_ref[...]
        rstd = lax.rsqrt(jnp.mean(x * x, axis=-1, keepdims=True) + EPS)
        xhat = x * rstd
        dy = dhn_ref[...]
        dyg = dy * g_ref[...]
        dh = dres_ref[...] + rstd * (dyg - xhat * jnp.mean(dyg * xhat, axis=-1, keepdims=True))
        dh_ref[...] = dh
        dhb_ref[...] = dh.astype(dhb_ref.dtype)

        @pl.when(pl.program_id(0) == 0)
        def _():
            dg_ref[...] = jnp.zeros_like(dg_ref)
        dg_ref[...] += jnp.sum(dy * xhat, axis=0, keepdims=True)

    row = pl.BlockSpec((TM, D), lambda i: (i, 0))
    return pl.pallas_call(
        body, name=name, grid=(T // TM,),
        in_specs=[row, _full((1, D)), row, row, pl.BlockSpec(memory_space=pl.ANY)],
        out_specs=[row, row, _full((1, D))],
        out_shape=[_sds((T, D), F32), _sds((T, D), MXU_DTYPE), _sds((1, D), F32)], compiler_params=_params(("arbitrary",)),
    )(h, gain, dhn, dres, after)


def loss_head(h, gain, target, name):
    def body(h_ref, g_ref, t_ref, loss_ref, dh_ref, dhb_ref, dg_ref):
        i = pl.program_id(0)

        @pl.when(i == 0)
        def _():
            loss_ref[...] = jnp.zeros_like(loss_ref)
            dg_ref[...] = jnp.zeros_like(dg_ref)
            dh_ref[...] = jnp.zeros_like(dh_ref)
            dhb_ref[...] = jnp.zeros_like(dhb_ref)

        @pl.when(i > 0)
        def _():
            x = h_ref[...]
            g = g_ref[...]
            rstd = lax.rsqrt(jnp.mean(x * x, axis=-1, keepdims=True) + EPS)
            xhat = x * rstd
            err = xhat * g - t_ref[...]
            loss_ref[...] += 0.5 * jnp.sum(jnp.mean(err * err, axis=-1, keepdims=True), axis=0, keepdims=True)
            dy = err * (1.0 / D)
            dyg = dy * g
            dh = rstd * (dyg - xhat * jnp.mean(dyg * xhat, axis=-1, keepdims=True))
            dh_ref[...] = dh
            dhb_ref[...] = dh.astype(dhb_ref.dtype)
            dg_ref[...] += jnp.sum(dy * xhat, axis=0, keepdims=True)

    row = pl.BlockSpec((CH, D), lambda i: (i, 0))
    return pl.pallas_call(
        body, name=name, grid=(NCH,),
        in_specs=[row, _full((1, D)), pl.BlockSpec((CH, D), lambda i: (jnp.maximum(i - 1, 0), 0))],
        out_specs=[_full((8, 128)), row, row, _full((1, D))],
        out_shape=[_sds((8, 128), F32), _sds((T, D), F32), _sds((T, D), MXU_DTYPE), _sds((1, D), F32)],
        compiler_params=_params(("arbitrary",)),
    )(h, gain, target)


PAIR = 2 * IN_SH
NPAIR = NDEV // 2
BN = 256
FB = 512


def _pair_cols(w_ref):
    return jnp.concatenate([w_ref[0], w_ref[1]], axis=1)


W_PAIR = lambda k: pl.BlockSpec((2, k, IN_SH), lambda j: (j, 0, 0))
COLS_PAIR = pl.BlockSpec((T, PAIR), lambda j: (0, j))
ANYSPEC = pl.BlockSpec(memory_space=pl.ANY)


def _norm_rows(h_ref, g_ref, hn_ref):
    for i in range(T // TM):
        rows = slice(TM * i, TM * (i + 1))
        x = h_ref[rows, :]
        ms = jnp.mean(x * x, axis=-1, keepdims=True)
        hn_ref[rows, :] = (x * lax.rsqrt(ms + EPS) * g_ref[...]).astype(hn_ref.dtype)


def norm_mm_blocked_nn(h, gain, w, out_dtype, name):
    def body(h_ref, g_ref, w_ref, hn_ref, o_ref):
        @pl.when(pl.program_id(0) == 0)
        def _():
            _norm_rows(h_ref, g_ref, hn_ref)
        o_ref[:PAD, :] = jnp.zeros((PAD, PAIR), o_ref.dtype)
        o_ref[PAD:, :] = _dot(hn_ref[PAD:, :], _pair_cols(w_ref), NN).astype(o_ref.dtype)

    return pl.pallas_call(
        body, name=name, grid=(NPAIR,),
        in_specs=[_full((T, D)), _full((1, D)), W_PAIR(D)], out_specs=[_full((T, D)), COLS_PAIR],
        out_shape=[_sds((T, D), MXU_DTYPE), _sds((T, NDEV * IN_SH), out_dtype)], compiler_params=_params(("arbitrary",)),
    )(h, gain, w)


def mm_nn_res(a, w, res, after, name):
    k = a.shape[1]

    def body(a_ref, w_ref, r_ref, after_ref, o_ref):
        del after_ref
        o_ref[:PAD, :] = r_ref[:PAD, :]
        o_ref[PAD:, :] = r_ref[PAD:, :] + _dot(a_ref[PAD:, :], w_ref[...], NN)

    col = pl.BlockSpec((T, BN), lambda j: (0, j))
    return pl.pallas_call(
        body, name=name, grid=(D // BN,),
        in_specs=[_full((T, k)), pl.BlockSpec((k, BN), lambda j: (0, j)), col, ANYSPEC], out_specs=col,
        out_shape=_sds((T, D), F32), compiler_params=_params(("parallel",)),
    )(a, w, res, after)


def ffn_up(h, gain, wg, wu, name):
    def body(h_ref, gain_ref, wg_ref, wu_ref, a_ref, dg_ref, du_ref, act_ref):
        @pl.when(pl.program_id(0) == 0)
        def _():
            _norm_rows(h_ref, gain_ref, a_ref)
        a = a_ref[PAD:, :]
        for ref in (dg_ref, du_ref, act_ref):
            ref[:PAD, :] = jnp.zeros((PAD, FB), ref.dtype)
        for c in range(FB // BN):
            cols = slice(BN * c, BN * (c + 1))
            g = _dot(a, wg_ref[cols, :], NT)
            u = _dot(a, wu_ref[cols, :], NT)
            sg = jax.nn.sigmoid(g)
            silu = g * sg
            dg_ref[PAD:, cols] = (u * (sg * (1.0 + g * (1.0 - sg)))).astype(dg_ref.dtype)
            du_ref[PAD:, cols] = silu.astype(du_ref.dtype)
            act_ref[PAD:, cols] = (silu * u).astype(act_ref.dtype)

    wspec = pl.BlockSpec((FB, D), lambda j: (j, 0))
    ospec = pl.BlockSpec((T, FB), lambda j: (0, j))
    return pl.pallas_call(
        body, name=name, grid=(D_FFP // FB,),
        in_specs=[_full((T, D)), _full((1, D)), wspec, wspec], out_specs=[_full((T, D))] + [ospec] * 3,
        out_shape=[_sds((T, D), MXU_DTYPE)] + [_sds((T, D_FFP), MXU_DTYPE)] * 3, compiler_params=_params(("arbitrary",)),
    )(h, gain, wg, wu)


def ffn_down_bwd(dh, wd, dact_dgate, dact_dup, after, name):
    def body(dh_ref, wd_ref, g_ref, u_ref, after_ref, dg_ref, du_ref):
        del after_ref
        dh = dh_ref[PAD:, :]
        for ref in (dg_ref, du_ref):
            ref[:PAD, :] = jnp.zeros((PAD, FB), ref.dtype)
        for c in range(FB // BN):
            cols = slice(BN * c, BN * (c + 1))
            dact = _dot(dh, wd_ref[cols, :], NT)
            dg_ref[PAD:, cols] = (dact * g_ref[PAD:, cols].astype(F32)).astype(dg_ref.dtype)
            du_ref[PAD:, cols] = (dact * u_ref[PAD:, cols].astype(F32)).astype(du_ref.dtype)

    blk = pl.BlockSpec((T, FB), lambda j: (0, j))
    return pl.pallas_call(
        body, name=name, grid=(D_FFP // FB,),
        in_specs=[_full((T, D)), pl.BlockSpec((FB, D), lambda j: (j, 0)), blk, blk, ANYSPEC],
        out_specs=[blk, blk],
        out_shape=[_sds((T, D_FFP), MXU_DTYPE)] * 2, compiler_params=_params(("parallel",)),
    )(dh, wd, dact_dgate, dact_dup, after)


def mm_blocked_nt(pairs, after, name):
    n = len(pairs)

    def body(*refs):
        o_ref = refs[2 * n + 1]

        @pl.when(pl.program_id(0) == 0)
        def _():
            o_ref[...] = jnp.zeros_like(o_ref)
        for p in range(n):
            o_ref[PAD:, :] += _dot(refs[2 * p][PAD:, :], _pair_cols(refs[2 * p + 1]), NT)

    specs, args = [], []
    for a, w in pairs:
        specs += [COLS_PAIR, W_PAIR(D)]
        args += [a, w]
    return pl.pallas_call(
        body, name=name, grid=(NPAIR,), in_specs=specs + [ANYSPEC], out_specs=_full((T, D)),
        out_shape=_sds((T, D), F32), compiler_params=_params(("arbitrary",)),
    )(*args, after)


def proj_bwd(dproj, w, hn, after, name):
    def body(dp_ref, w_ref, hn_ref, after_ref, dh_ref, dw_ref):
        del after_ref

        @pl.when(pl.program_id(0) == 0)
        def _():
            dh_ref[...] = jnp.zeros_like(dh_ref)
        dp = dp_ref[PAD:, :]
        dh_ref[PAD:, :] += _dot(dp, _pair_cols(w_ref), NT)
        dw = _dot(hn_ref[PAD:, :], dp, TN).astype(dw_ref.dtype)
        dw_ref[0] = dw[:, :IN_SH]
        dw_ref[1] = dw[:, IN_SH:]

    return pl.pallas_call(
        body, name=name, grid=(NPAIR,),
        in_specs=[COLS_PAIR, W_PAIR(D), _full((T, D)), ANYSPEC], out_specs=[_full((T, D)), W_PAIR(D)],
        out_shape=[_sds((T, D), F32), _sds((NDEV, D, IN_SH), WIRE_DTYPE)], compiler_params=_params(("arbitrary",)),
    )(dproj, w, hn, after)


def mm_tn_two(a1, a2, b, bm, after, name):
    m = a1.shape[1]

    def body(a1_ref, a2_ref, b_ref, after_ref, o1_ref, o2_ref):
        del after_ref
        b = b_ref[...]
        o1_ref[...] = _dot(a1_ref[...], b, TN).astype(o1_ref.dtype)
        o2_ref[...] = _dot(a2_ref[...], b, TN).astype(o2_ref.dtype)

    blk = pl.BlockSpec((T, bm), lambda i: (0, i))
    out = pl.BlockSpec((bm, D), lambda i: (i, 0))
    return pl.pallas_call(
        body, name=name, grid=(m // bm,),
        in_specs=[blk, blk, _full((T, D)), ANYSPEC], out_specs=[out, out],
        out_shape=[_sds((m, D), WIRE_DTYPE)] * 2, compiler_params=_params(("parallel",)),
    )(a1, a2, b, after)


def out_proj_bwd(dh, w, ymix, after, name):
    def body(dh_ref, w_ref, y_ref, after_ref, dy_ref, dw_ref):
        del after_ref
        dh_ = dh_ref[PAD:, :]
        dy_ref[:PAD, :] = jnp.zeros((PAD, BN), dy_ref.dtype)
        dy_ref[PAD:, :] = _dot(dh_, w_ref[...], NT)
        dw_ref[...] = _dot(y_ref[PAD:, :], dh_, TN).astype(dw_ref.dtype)

    return pl.pallas_call(
        body, name=name, grid=(D // BN,),
        in_specs=[_full((T, D)), pl.BlockSpec((BN, D), lambda j: (j, 0)), pl.BlockSpec((T, BN), lambda j: (0, j)), ANYSPEC],
        out_specs=[pl.BlockSpec((T, BN), lambda j: (0, j)), pl.BlockSpec((BN, D), lambda j: (j, 0))],
        out_shape=[_sds((T, D), F32), _sds((D, D), WIRE_DTYPE)], compiler_params=_params(("parallel",)),
    )(dh, w, ymix, after)


def mm_rows_nn(pairs, after, name):
    n = len(pairs)

    def body(*refs):
        o_ref = refs[2 * n + 1]

        @pl.when(pl.program_id(0) == 0)
        def _():
            o_ref[...] = jnp.zeros_like(o_ref)
        for p in range(n):
            o_ref[PAD:, :] += _dot(refs[2 * p][PAD:, :], refs[2 * p + 1][...], NN)

    specs, args = [], []
    for a, w in pairs:
        specs += [pl.BlockSpec((T, FB), lambda j: (0, j)), pl.BlockSpec((FB, D), lambda j: (j, 0))]
        args += [a, w]
    return pl.pallas_call(
        body, name=name, grid=(D_FFP // FB,), in_specs=specs + [ANYSPEC], out_specs=_full((T, D)),
        out_shape=_sds((T, D), F32), compiler_params=_params(("arbitrary",)),
    )(*args, after)


def mm_tn_blocked(a, b, name):
    def body(a_ref, b_ref, o_ref):
        o = _dot(a_ref[...], b_ref[...], TN).astype(o_ref.dtype)
        o_ref[0] = o[:, :IN_SH]
        o_ref[1] = o[:, IN_SH:]

    return pl.pallas_call(
        body, name=name, grid=(NPAIR,),
        in_specs=[_full((T, D)), COLS_PAIR], out_specs=W_PAIR(D),
        out_shape=_sds((NDEV, D, IN_SH), WIRE_DTYPE), compiler_params=_params(("parallel",)),
    )(a, b)


def mm_tn(a, b, bm, after, name):
    m = a.shape[1]

    def body(a_ref, b_ref, after_ref, o_ref):
        del after_ref
        o_ref[...] = _dot(a_ref[...], b_ref[...], TN).astype(o_ref.dtype)

    return pl.pallas_call(
        body, name=name, grid=(m // bm,),
        in_specs=[pl.BlockSpec((T, bm), lambda i: (0, i)), _full((T, D)), ANYSPEC],
        out_specs=pl.BlockSpec((bm, D), lambda i: (i, 0)),
        out_shape=_sds((m, D), WIRE_DTYPE), compiler_params=_params(("parallel",)),
    )(a, b, after)


def _softplus_neg(lam):
    return jnp.maximum(-lam, 0.0) + jnp.log1p(jnp.exp(-jnp.abs(lam)))


def _lru_gates(pa, px, xc, lam):
    r = jax.nn.sigmoid(pa)
    ig = jax.nn.sigmoid(px)
    sp = _softplus_neg(lam)
    log_a = -LRU_C * r * sp
    a = jnp.exp(log_a)
    mult = jnp.sqrt(-jnp.tanh(log_a) * (a * a + 1.0))
    return a, mult * (ig * xc), (r, ig, sp, mult)


def _lru_gates_vjp(da, db, xc, lam, a, r, ig, sp, mult):
    dmult = db * (ig * xc)
    du = db * mult
    dlog_a = da * a - dmult * (a * a) / mult
    dr = dlog_a * (-LRU_C * sp)
    dlam = jnp.sum(dlog_a * (-LRU_C * r), axis=0, keepdims=True) * (-jax.nn.sigmoid(-lam))
    dpa = dr * (r * (1.0 - r))
    dpx = (du * xc) * (ig * (1.0 - ig))
    return dpa, dpx, du * ig, dlam


def _lru_out(h, g, gain):
    z = h * jax.nn.gelu(g)
    return z * lax.rsqrt(jnp.mean(z * z, axis=-1, keepdims=True) + EPS) * gain


def _conv_taps(x, xprev, row):
    taps = [x]
    for s in range(1, CONV_W):
        taps.append(jnp.where(row < s, pltpu.roll(xprev, s, 0), pltpu.roll(x, s, 0)))
    return taps


def _conv(taps, cw_ref, cb):
    xc = cb + cw_ref[CONV_W - 1:CONV_W, :] * taps[0]
    for s in range(1, CONV_W):
        xc = xc + cw_ref[CONV_W - 1 - s:CONV_W - s, :] * taps[s]
    return xc


def _lru_fwd_block(i, x_ref, g_ref, cw_ref, cb_ref, wa_ref, ba_ref, wx_ref, bx_ref, lam_ref, gain_ref, y_ref, h_ref,
                   xprev_scr, a_scr, b_scr, carry_scr):
    @pl.when(i == 0)
    def _():
        xprev_scr[...] = jnp.zeros_like(xprev_scr)
        carry_scr[...] = jnp.zeros_like(carry_scr)

    x = x_ref[...]
    row = lax.broadcasted_iota(jnp.int32, (CH, D_LRU), 0)
    xc = _conv(_conv_taps(x, xprev_scr[...], row), cw_ref, cb_ref[...])
    pa = _dot(xc, wa_ref[...], NN) + ba_ref[...]
    px = _dot(xc, wx_ref[...], NN) + bx_ref[...]
    a, b, _ = _lru_gates(pa, px, xc, lam_ref[...])
    a_scr[...] = a
    b_scr[...] = jnp.where(i * CH + row >= PAD, b, 0.0)
    h = carry_scr[...]
    for t in range(CH):
        h = a_scr[t:t + 1, :] * h + b_scr[t:t + 1, :]
        h_ref[t:t + 1, :] = h
    carry_scr[...] = h
    xprev_scr[...] = x
    y_ref[:, :D_LRU] = _lru_out(h_ref[...], g_ref[...], gain_ref[...]).astype(y_ref.dtype)


LRU_VEC_ROWS = 16


def _lru_bwd_block(ib, x_ref, xp_ref, g_ref, h_ref, hp_ref, dy_ref, cw_ref, cb_ref, wa_ref, ba_ref, wx_ref, bx_ref, lam_ref,
                   gain_ref, dp_ref, vec_ref, dwa_ref, dwx_ref, a_scr, dh_scr, g_scr, carry_scr, dxcn_scr):
    @pl.when(ib == NCH - 1)
    def _():
        carry_scr[...] = jnp.zeros_like(carry_scr)
        dxcn_scr[...] = jnp.zeros_like(dxcn_scr)
        vec_ref[...] = jnp.zeros_like(vec_ref)
        dwa_ref[...] = jnp.zeros_like(dwa_ref)
        dwx_ref[...] = jnp.zeros_like(dwx_ref)

    x = x_ref[...]
    row = lax.broadcasted_iota(jnp.int32, (CH, D_LRU), 0)
    valid = ib * CH + row >= PAD
    taps = _conv_taps(x, xp_ref[...], row)
    xc = _conv(taps, cw_ref, cb_ref[...])
    pa = _dot(xc, wa_ref[...], NN) + ba_ref[...]
    px = _dot(xc, wx_ref[...], NN) + bx_ref[...]
    a, _, gate_parts = _lru_gates(pa, px, xc, lam_ref[...])
    h = h_ref[...]
    _, vjp_out = jax.vjp(_lru_out, h, g_ref[...], gain_ref[...])
    dh, dg, dgain = vjp_out(dy_ref[:, :D_LRU].astype(F32))
    a_scr[...] = a
    dh_scr[...] = dh
    c = carry_scr[...]
    for t in range(CH - 1, -1, -1):
        gt = dh_scr[t:t + 1, :] + c
        g_scr[t:t + 1, :] = gt
        c = a_scr[t:t + 1, :] * gt
    carry_scr[...] = c
    gg = g_scr[...]
    hprev = jnp.where(row < 1, pltpu.roll(hp_ref[...], 1, 0), pltpu.roll(h, 1, 0))
    da = jnp.where(valid, gg * hprev, 0.0)
    db = jnp.where(valid, gg, 0.0)
    dpa, dpx, dxc, dlam = _lru_gates_vjp(da, db, xc, lam_ref[...], a, *gate_parts)
    dxc = dxc + _dot(dpa, wa_ref[...], NT) + _dot(dpx, wx_ref[...], NT)
    dwa_ref[...] += _dot(xc, dpa, TN)
    dwx_ref[...] += _dot(xc, dpx, TN)
    for s in range(CONV_W):
        vec_ref[CONV_W - 1 - s:CONV_W - s, :] += jnp.sum(dxc * taps[s], axis=0, keepdims=True)
    vec_ref[4:5, :] += jnp.sum(dxc, axis=0, keepdims=True)
    vec_ref[5:6, :] += jnp.sum(dpa, axis=0, keepdims=True)
    vec_ref[6:7, :] += jnp.sum(dpx, axis=0, keepdims=True)
    vec_ref[7:8, :] += dlam
    vec_ref[8:9, :] += dgain
    dxn = dxcn_scr[...]
    dx = cw_ref[CONV_W - 1:CONV_W, :] * dxc
    for s in range(1, CONV_W):
        ahead = jnp.where(row >= CH - s, pltpu.roll(dxn, CH - s, 0), pltpu.roll(dxc, CH - s, 0))
        dx = dx + cw_ref[CONV_W - 1 - s:CONV_W - s, :] * ahead
    dxcn_scr[...] = dxc
    dp_ref[:, :D_LRU] = jnp.where(valid, dx, 0.0).astype(dp_ref.dtype)
    dp_ref[:, D_LRU:2 * D_LRU] = dg.astype(dp_ref.dtype)


def _ret_tables():
    half = HD // 2
    pos = jnp.arange(T, dtype=F32) - float(PAD)
    inv = ROPE_BASE ** (-jnp.arange(half, dtype=F32) / half)
    ang = pos[:, None] * inv[None, :]
    cos = jnp.concatenate([jnp.cos(ang), jnp.cos(ang)], axis=-1)
    sin = jnp.concatenate([-jnp.sin(ang), jnp.sin(ang)], axis=-1)
    log_g = jnp.log(1.0 - 2.0 ** (-5.0 - jnp.arange(HEADS, dtype=F32)))
    idx = jnp.arange(CH, dtype=F32)
    diff = idx[:, None] - idx[None, :]
    dmask = jnp.where(diff[None] >= 0, jnp.exp(jnp.maximum(diff, 0.0)[None] * log_g[:, None, None]), 0.0)
    xi = jnp.exp((idx + 1.0)[None, :] * log_g[:, None])
    zeta = jnp.exp((CH - 1.0 - idx)[None, :] * log_g[:, None])
    xi = jnp.broadcast_to(xi[:, :, None], (HEADS, CH, HD))
    zeta = jnp.broadcast_to(zeta[:, :, None], (HEADS, CH, HD))
    return cos, sin, dmask, xi, zeta


def _chunk_decay():
    log_g = np.log(np.float32(1.0) - np.float32(2.0) ** (np.float32(-5.0) - np.arange(HEADS, dtype=np.float32)))
    return [float(v) for v in np.exp(np.float32(CH) * log_g.astype(np.float32))]


def _rope(x, cos, sin):
    return x * cos + pltpu.roll(x, HD // 2, 1) * sin


def mix_fwd(proj, cw, cb, wa, ba, wx, bx, lam, gain, tables, ret_gain, after, name):
    cos, sin, dmask, xi, zeta = tables
    gch = _chunk_decay()
    scale = HD ** -0.5

    def body(x_ref, gl_ref, cw_ref, cb_ref, wa_ref, ba_ref, wx_ref, bx_ref, lam_ref, lgain_ref,
             q_ref, k_ref, v_ref, g_ref, cos_ref, sin_ref, dm_ref, xi_ref, zt_ref, gain_ref, after_ref,
             y_ref, h_ref, st_ref, xprev_scr, a_scr, b_scr, carry_scr, s_scr):
        del after_ref

        @pl.when(pl.program_id(0) == 0)
        def _():
            s_scr[...] = jnp.zeros_like(s_scr)

        _lru_fwd_block(pl.program_id(0), x_ref, gl_ref, cw_ref, cb_ref, wa_ref, ba_ref, wx_ref, bx_ref, lam_ref, lgain_ref,
                       y_ref, h_ref, xprev_scr, a_scr, b_scr, carry_scr)
        cs, sn = cos_ref[...], sin_ref[...]
        hs = range(HEADS)
        sl = [slice(HD * h, HD * (h + 1)) for h in hs]
        qr = [_rope(q_ref[:, sl[h]], cs, sn).astype(MXU_DTYPE) for h in hs]
        kf = [_rope(k_ref[:, sl[h]], cs, sn) * scale for h in hs]
        kr = [kf[h].astype(MXU_DTYPE) for h in hs]
        v = [v_ref[:, sl[h]].astype(MXU_DTYPE) for h in hs]
        s = [s_scr[h] for h in hs]
        for h in hs:
            st_ref[h] = s[h]
        sc = [_dot(qr[h], kr[h], NT) * dm_ref[h] for h in hs]
        cross = [_dot(qr[h], s[h], NN) * xi_ref[h] for h in hs]
        for h in hs:
            s_scr[h] = s[h] * gch[h] + _dot(kf[h] * zt_ref[h], v[h], TN)
        y = [_dot(sc[h], v[h], NN) + cross[h] for h in hs]
        yc = [y[h] - jnp.mean(y[h], axis=-1, keepdims=True) for h in hs]
        yn = [yc[h] * lax.rsqrt(jnp.mean(yc[h] * yc[h], axis=-1, keepdims=True) + EPS) for h in hs]
        for h in hs:
            so = slice(D_LRU + HD * h, D_LRU + HD * (h + 1))
            y_ref[:, so] = (jax.nn.silu(g_ref[:, sl[h]]) * (yn[h] * gain_ref[:, sl[h]])).astype(y_ref.dtype)

    def col(c):
        return pl.BlockSpec((CH, D_RET), lambda n: (n, c))

    tab = pl.BlockSpec((CH, HD), lambda n: (n, 0))
    cst = _full((HEADS, CH, HD))
    vec = _full((1, D_LRU))
    mat = _full((D_LRU, D_LRU))
    blockbuf = pltpu.VMEM((CH, D_LRU), F32)
    return pl.pallas_call(
        body, name=name, grid=(NCH,),
        in_specs=[col(0), col(1), _full((CONV_W, D_LRU)), vec, mat, vec, mat, vec, vec, vec,
                  col(2), col(3), col(4), col(5), tab, tab, cst, cst, cst, _full((1, D_RET)),
                  pl.BlockSpec(memory_space=pl.ANY)],
        out_specs=[pl.BlockSpec((CH, D), lambda n: (n, 0)), col(0), pl.BlockSpec((None, HEADS, HD, HD), lambda n: (n, 0, 0, 0))],
        out_shape=[_sds((T, D), MXU_DTYPE), _sds((T, D_LRU), F32), _sds((NCH, HEADS, HD, HD), F32)],
        scratch_shapes=[blockbuf, blockbuf, blockbuf, pltpu.VMEM((1, D_LRU), F32), pltpu.VMEM((HEADS, HD, HD), F32)],
        compiler_params=_params(("arbitrary",)),
    )(proj, proj, cw, cb, wa, ba, wx, bx, lam, gain, proj, proj, proj, proj, cos, sin, dmask, xi, zeta, ret_gain, after)


def mix_bwd(proj, hst, states, dymix, cw, cb, wa, ba, wx, bx, lam, gain, tables, ret_gain, after, name):
    cos, sin, dmask, xi, zeta = tables
    gch = _chunk_decay()
    scale = HD ** -0.5
    last = NCH - 1

    def body(x_ref, xp_ref, gl_ref, h_ref, hp_ref, cw_ref, cb_ref, wa_ref, ba_ref, wx_ref, bx_ref, lam_ref, lgain_ref,
             q_ref, k_ref, v_ref, g_ref, st_ref, dy_ref, cos_ref, sin_ref, dm_ref, xi_ref, zt_ref, gain_ref, after_ref,
             dp_ref, vec_ref, dwa_ref, dwx_ref, dgain_ref, a_scr, dh_scr, g_scr, carry_scr, dxcn_scr, ds_scr):
        del after_ref

        @pl.when(pl.program_id(0) == 0)
        def _():
            ds_scr[...] = jnp.zeros_like(ds_scr)
            dgain_ref[...] = jnp.zeros_like(dgain_ref)

        _lru_bwd_block(last - pl.program_id(0), x_ref, xp_ref, gl_ref, h_ref, hp_ref, dy_ref, cw_ref, cb_ref, wa_ref, ba_ref,
                       wx_ref, bx_ref, lam_ref, lgain_ref, dp_ref, vec_ref, dwa_ref, dwx_ref, a_scr, dh_scr, g_scr, carry_scr,
                       dxcn_scr)
        cs, sn = cos_ref[...], sin_ref[...]
        hs = range(HEADS)
        sl = [slice(HD * h, HD * (h + 1)) for h in hs]

        def out(j, h):
            return slice(2 * D_LRU + j * D_RET + HD * h, 2 * D_LRU + j * D_RET + HD * (h + 1))

        b16 = lambda xs: [x.astype(MXU_DTYPE) for x in xs]
        qr = b16([_rope(q_ref[:, sl[h]], cs, sn) for h in hs])
        kf = [_rope(k_ref[:, sl[h]], cs, sn) * scale for h in hs]
        kr = b16(kf)
        kz = b16([kf[h] * zt_ref[h] for h in hs])
        v = b16([v_ref[:, sl[h]] for h in hs])
        s = b16([st_ref[h] for h in hs])
        ds = [ds_scr[h] for h in hs]
        dsb = b16(ds)
        sc = [_dot(qr[h], kr[h], NT) * dm_ref[h] for h in hs]
        scb = b16(sc)
        y = [_dot(scb[h], v[h], NN) + _dot(qr[h], s[h], NN) * xi_ref[h] for h in hs]
        yc = [y[h] - jnp.mean(y[h], axis=-1, keepdims=True) for h in hs]
        rstd = [lax.rsqrt(jnp.mean(yc[h] * yc[h], axis=-1, keepdims=True) + EPS) for h in hs]
        yn = [yc[h] * rstd[h] for h in hs]
        dy = []
        for h in hs:
            g = g_ref[:, sl[h]]
            gain = gain_ref[:, sl[h]]
            sg = jax.nn.sigmoid(g)
            silu = g * sg
            dout = dy_ref[:, D_LRU + HD * h:D_LRU + HD * (h + 1)].astype(F32)
            dgain_ref[:, sl[h]] += jnp.sum(dout * silu * yn[h], axis=0, keepdims=True)
            dp_ref[:, out(3, h)] = (dout * yn[h] * gain * (sg * (1.0 + g * (1.0 - sg)))).astype(dp_ref.dtype)
            dyn = dout * silu * gain
            dy.append(rstd[h] * (dyn - jnp.mean(dyn, axis=-1, keepdims=True)
                                 - yn[h] * jnp.mean(dyn * yn[h], axis=-1, keepdims=True)))
        dyb = b16(dy)
        dqs = b16([dy[h] * xi_ref[h] for h in hs])
        dp = b16([_dot(dyb[h], v[h], NT) * dm_ref[h] for h in hs])
        dv = [_dot(scb[h], dyb[h], TN) + _dot(kz[h], dsb[h], NN) for h in hs]
        dqr = [_dot(dp[h], kr[h], NN) + _dot(dqs[h], s[h], NT) for h in hs]
        dkr = [_dot(dp[h], qr[h], TN) + _dot(v[h], dsb[h], NT) * zt_ref[h] for h in hs]
        for h in hs:
            ds_scr[h] = gch[h] * ds[h] + _dot(qr[h], dqs[h], TN)
        for h in hs:
            dp_ref[:, out(0, h)] = (dqr[h] * cs + pltpu.roll(dqr[h] * sn, HD // 2, 1)).astype(dp_ref.dtype)
            dp_ref[:, out(1, h)] = ((dkr[h] * cs + pltpu.roll(dkr[h] * sn, HD // 2, 1)) * scale).astype(dp_ref.dtype)
            dp_ref[:, out(2, h)] = dv[h].astype(dp_ref.dtype)

    def col(c, shift=0):
        return pl.BlockSpec((CH, D_RET), lambda n: (jnp.maximum(last - n - shift, 0), c))

    tab = pl.BlockSpec((CH, HD), lambda n: (last - n, 0))
    cst = _full((HEADS, CH, HD))
    vec = _full((1, D_LRU))
    mat = _full((D_LRU, D_LRU))
    blockbuf = pltpu.VMEM((CH, D_LRU), F32)
    return pl.pallas_call(
        body, name=name, grid=(NCH,),
        in_specs=[col(0), col(0, 1), col(1), col(0), col(0, 1), _full((CONV_W, D_LRU)), vec, mat, vec, mat, vec, vec, vec,
                  col(2), col(3), col(4), col(5), pl.BlockSpec((None, HEADS, HD, HD), lambda n: (last - n, 0, 0, 0)),
                  pl.BlockSpec((CH, D), lambda n: (last - n, 0)), tab, tab, cst, cst, cst, _full((1, D_RET)),
                  pl.BlockSpec(memory_space=pl.ANY)],
        out_specs=[pl.BlockSpec((CH, D_IN), lambda n: (last - n, 0)), _full((LRU_VEC_ROWS, D_LRU)), mat, mat,
                   _full((1, D_RET))],
        out_shape=[_sds((T, D_IN), MXU_DTYPE), _sds((LRU_VEC_ROWS, D_LRU), F32), _sds((D_LRU, D_LRU), F32),
                   _sds((D_LRU, D_LRU), F32), _sds((1, D_RET), F32)],
        scratch_shapes=[blockbuf, blockbuf, blockbuf, pltpu.VMEM((1, D_LRU), F32), blockbuf,
                        pltpu.VMEM((HEADS, HD, HD), F32)],
        compiler_params=_params(("arbitrary",)),
    )(proj, proj, proj, hst, hst, cw, cb, wa, ba, wx, bx, lam, gain, proj, proj, proj, proj, states, dymix,
      cos, sin, dmask, xi, zeta, ret_gain, after)


HBM = pl.BlockSpec(memory_space=pltpu.HBM)


def _place():
    return lax.axis_index("x"), lax.axis_index("y"), lax.axis_index("c")


def all_gather(arrs, after, name):
    n = len(arrs)

    def body(*refs):
        ins, outs = refs[:n], refs[n + 1:2 * n + 1]
        send_sems, recv_sems, local_sems = refs[2 * n + 1:]
        x, y, c = _place()
        me, sibling = (x, y, c), (x, y, 1 - c)
        chips = [(1 - x, y), (x, 1 - y), (1 - x, 1 - y)]

        def copy(a, k, block, to, src=None):
            px, py, pc = block
            dst = outs[a].at[4 * px + 2 * py + pc]
            return pltpu.make_async_remote_copy(
                src_ref=dst if src is None else src, dst_ref=dst, send_sem=send_sems.at[a, k], recv_sem=recv_sems.at[a, k],
                device_id=to, device_id_type=MESH)

        mine = [pltpu.make_async_copy(ins[a], outs[a].at[4 * x + 2 * y + c], local_sems.at[a]) for a in range(n)]
        for cp in mine:
            cp.start()
        first = []
        for a in range(n):
            first.append(copy(a, 0, me, sibling, src=ins[a]))
            first += [copy(a, 1 + j, me, (*chip, c), src=ins[a]) for j, chip in enumerate(chips)]
        for cp in first:
            cp.start()
        passed = []
        for j, chip in enumerate(chips):
            for a in range(n):
                copy(a, 1 + j, (*chip, c), me).wait_recv()
                passed.append(copy(a, 4 + j, (*chip, c), sibling))
                passed[-1].start()
        for a in range(n):
            copy(a, 0, sibling, me).wait_recv()
            for j, chip in enumerate(chips):
                copy(a, 4 + j, (*chip, 1 - c), me).wait_recv()
        for cp in first + passed:
            cp.wait_send()
        for cp in mine:
            cp.wait()

    return pl.pallas_call(
        body, name=name,
        in_specs=[HBM] * n + [pl.BlockSpec(memory_space=pl.ANY)], out_specs=[HBM] * n,
        out_shape=[_sds((NDEV,) + a.shape, a.dtype) for a in arrs],
        scratch_shapes=[pltpu.SemaphoreType.DMA((n, 7)), pltpu.SemaphoreType.DMA((n, 7)), pltpu.SemaphoreType.DMA((n,))],
    )(*arrs, after)


SEM = pl.BlockSpec(memory_space=pltpu.SEMAPHORE)
ANY = pl.BlockSpec(memory_space=pl.ANY)
EFFECT = pltpu.SideEffectType.DATAFLOW_SIDE_EFFECTING


def _hbm(a):
    return pltpu.with_memory_space_constraint(a, pltpu.HBM)


def _hbm_like(arrs):
    return [pltpu.HBM(a.shape, a.dtype) for a in arrs]


def _dma_sems(count):
    return [pltpu.SemaphoreType.DMA(())] * count


def _ag_copy(lands, send_sems, recv_sems, per):
    def copy(a, k, block, to, src=None):
        px, py, pc = block
        dst = lands[a].at[4 * px + 2 * py + pc]
        return pltpu.make_async_remote_copy(
            src_ref=dst if src is None else src, dst_ref=dst, send_sem=send_sems[a * per + k], recv_sem=recv_sems[a * per + k],
            device_id=to, device_id_type=MESH)
    return copy


def to_wire(sel, w_in, w_out, w_gate, w_up, w_down, name):
    ffpad = FF_SHP - FF_SH

    def body(sel_ref, i_ref, o_ref, g_ref, u_ref, d_ref, oi, oo, og, ou, od):
        del sel_ref
        oi[...] = i_ref[...].astype(oi.dtype)
        oo[...] = o_ref[...].astype(oo.dtype)
        for src, dst in ((g_ref, og), (u_ref, ou), (d_ref, od)):
            dst[:FF_SH, :] = src[...].astype(dst.dtype)
            dst[FF_SH:, :] = jnp.zeros((ffpad, D), dst.dtype)

    shapes_in = [(D, IN_SH), (OUT_SH, D), (FF_SH, D), (FF_SH, D), (FF_SH, D)]
    shapes_out = [(D, IN_SH), (OUT_SH, D), (FF_SHP, D), (FF_SHP, D), (FF_SHP, D)]
    return pl.pallas_call(
        body, name=name,
        grid_spec=pltpu.PrefetchScalarGridSpec(
            num_scalar_prefetch=1, grid=(1,),
            in_specs=[pl.BlockSpec((None,) + s, lambda i, sel_ref: (sel_ref[1], 0, 0)) for s in shapes_in],
            out_specs=[pl.BlockSpec((None,) + s, lambda i, sel_ref: (sel_ref[0], 0, 0)) for s in shapes_out]),
        out_shape=[_sds((NDEV,) + s, WIRE_DTYPE) for s in shapes_out], compiler_params=_params(("arbitrary",)),
    )(sel, w_in, w_out, w_gate, w_up, w_down)


def place_blocks(sel, arrs, name):
    n = len(arrs)

    def body(sel_ref, *refs):
        del sel_ref
        for a in range(n):
            refs[n + a][...] = refs[a][...]

    def whole(a):
        nd = a.ndim
        return pl.BlockSpec(a.shape, lambda i, sel_ref: (0,) * nd)

    def mine(a):
        nd = a.ndim
        return pl.BlockSpec((None,) + a.shape, lambda i, sel_ref: (sel_ref[0],) + (0,) * nd)

    return pl.pallas_call(
        body, name=name,
        grid_spec=pltpu.PrefetchScalarGridSpec(
            num_scalar_prefetch=1, grid=(1,), in_specs=[whole(a) for a in arrs], out_specs=[mine(a) for a in arrs]),
        out_shape=[_sds((NDEV,) + a.shape, a.dtype) for a in arrs], compiler_params=_params(("arbitrary",)),
    )(sel, *arrs)


def ag_start(lands, after, name):
    n = len(lands)
    ns = 4 * n

    def body(*refs):
        lnd = refs[:n]
        send_sems, recv_sems = refs[n + 1:n + 1 + ns], refs[n + 1 + ns:n + 1 + 2 * ns]
        token = refs[-1]
        x, y, c = _place()
        me, sibling = (x, y, c), (x, y, 1 - c)
        chips = [(1 - x, y), (x, 1 - y), (1 - x, 1 - y)]
        copy = _ag_copy(lnd, send_sems, recv_sems, 4)
        for a in range(n):
            copy(a, 0, me, sibling).start()
            for j, chip in enumerate(chips):
                copy(a, 1 + j, me, (*chip, c)).start()
        token[...] = jnp.zeros_like(token)

    outs = pl.pallas_call(
        body, name=name,
        in_specs=[HBM] * n + [ANY],
        out_specs=[SEM] * (2 * ns) + [HBM] * n + [pl.BlockSpec(memory_space=pltpu.VMEM)],
        out_shape=_dma_sems(2 * ns) + _hbm_like(lands) + [_sds((8, 128), F32)],
        input_output_aliases={i: 2 * ns + i for i in range(n)},
        compiler_params=pltpu.CompilerParams(has_side_effects=EFFECT),
    )(*[_hbm(a) for a in lands], after)
    return outs[:ns], outs[ns:2 * ns], outs[2 * ns:2 * ns + n], outs[-1]


def ag_forward(send_sems, recv_sems, lands, after, name):
    n = len(lands)
    n1, n2 = 4 * n, 3 * n

    def body(*refs):
        lnd = refs[:n]
        o = n
        s1, r1 = refs[o:o + n1], refs[o + n1:o + 2 * n1]
        o += 2 * n1 + 1
        s2, r2 = refs[o:o + n2], refs[o + n2:o + 2 * n2]
        token = refs[-1]
        token[...] = jnp.zeros_like(token)
        x, y, c = _place()
        me, sibling = (x, y, c), (x, y, 1 - c)
        chips = [(1 - x, y), (x, 1 - y), (1 - x, 1 - y)]
        copy1 = _ag_copy(lnd, s1, r1, 4)
        copy2 = _ag_copy(lnd, s2, r2, 3)
        for j, chip in enumerate(chips):
            for a in range(n):
                copy1(a, 1 + j, (*chip, c), me).wait_recv()
                copy2(a, j, (*chip, c), sibling).start()
        for a in range(n):
            copy1(a, 0, sibling, me).wait_recv()
            copy1(a, 0, me, sibling).wait_send()
            for j, chip in enumerate(chips):
                copy1(a, 1 + j, me, (*chip, c)).wait_send()

    outs = pl.pallas_call(
        body, name=name,
        in_specs=[HBM] * n + [SEM] * (2 * n1) + [ANY],
        out_specs=[SEM] * (2 * n2) + [HBM] * n + [pl.BlockSpec(memory_space=pltpu.VMEM)],
        out_shape=_dma_sems(2 * n2) + _hbm_like(lands) + [_sds((8, 128), F32)],
        input_output_aliases={i: 2 * n2 + i for i in range(n)},
        compiler_params=pltpu.CompilerParams(has_side_effects=EFFECT),
    )(*lands, *send_sems, *recv_sems, after)
    return outs[:n2], outs[n2:2 * n2], outs[2 * n2:2 * n2 + n], outs[-1]


def ag_finish(send_sems, recv_sems, lands, after, name):
    n = len(lands)
    n2 = 3 * n

    def body(*refs):
        lnd = refs[:n]
        s2, r2 = refs[n:n + n2], refs[n + n2:n + 2 * n2]
        x, y, c = _place()
        me, sibling = (x, y, c), (x, y, 1 - c)
        chips = [(1 - x, y), (x, 1 - y), (1 - x, 1 - y)]
        copy2 = _ag_copy(lnd, s2, r2, 3)
        for a in range(n):
            for j, chip in enumerate(chips):
                copy2(a, j, (*chip, c), sibling).wait_send()
                copy2(a, j, (*chip, 1 - c), me).wait_recv()

    outs = pl.pallas_call(
        body, name=name,
        in_specs=[HBM] * n + [SEM] * (2 * n2) + [ANY],
        out_specs=[HBM] * n, out_shape=_hbm_like(lands),
        input_output_aliases={i: i for i in range(n)},
        compiler_params=pltpu.CompilerParams(has_side_effects=EFFECT),
    )(*lands, *send_sems, *recv_sems, after)
    return list(outs)


def rs_sibling_start(arrs, name):
    n = len(arrs)
    ns = 4 * n
    lands = [lax.empty((4,) + a.shape[1:], a.dtype) for a in arrs]

    def body(*refs):
        ins, lnd = refs[:n], refs[n:2 * n]
        send_sems, recv_sems = refs[2 * n:2 * n + ns], refs[2 * n + ns:2 * n + 2 * ns]
        x, y, c = _place()
        sibling = (x, y, 1 - c)
        for a in range(n):
            for p in range(4):
                pltpu.make_async_remote_copy(
                    src_ref=ins[a].at[2 * p + 1 - c], dst_ref=lnd[a].at[p], send_sem=send_sems[4 * a + p],
                    recv_sem=recv_sems[4 * a + p], device_id=sibling, device_id_type=MESH).start()
        refs[-1][...] = jnp.zeros_like(refs[-1])

    outs = pl.pallas_call(
        body, name=name,
        in_specs=[HBM] * (2 * n), out_specs=[SEM] * (2 * ns) + [HBM] * (2 * n) + [pl.BlockSpec(memory_space=pltpu.VMEM)],
        out_shape=_dma_sems(2 * ns) + _hbm_like(arrs) + _hbm_like(lands) + [_sds((8, 128), F32)],
        input_output_aliases={i: 2 * ns + i for i in range(2 * n)},
        compiler_params=pltpu.CompilerParams(has_side_effects=EFFECT),
    )(*[_hbm(a) for a in arrs], *[_hbm(a) for a in lands])
    return (outs[:ns], outs[ns:2 * ns], outs[2 * ns:2 * ns + n], outs[2 * ns + n:2 * ns + 2 * n]), outs[-1]


def rs_sibling_wait(send_sems, recv_sems, arrs, lands, after, name):
    n = len(arrs)
    ns = 4 * n

    def body(*refs):
        ins, lnd = refs[:n], refs[n:2 * n]
        s, r = refs[2 * n:2 * n + ns], refs[2 * n + ns:2 * n + 2 * ns]
        x, y, c = _place()
        sibling = (x, y, 1 - c)
        for a in range(n):
            for p in range(4):
                cp = pltpu.make_async_remote_copy(
                    src_ref=ins[a].at[2 * p + 1 - c], dst_ref=lnd[a].at[p], send_sem=s[4 * a + p], recv_sem=r[4 * a + p],
                    device_id=sibling, device_id_type=MESH)
                cp.wait_send()
                cp.wait_recv()

    outs = pl.pallas_call(
        body, name=name,
        in_specs=[HBM] * (2 * n) + [SEM] * (2 * ns) + [ANY], out_specs=[HBM] * (2 * n),
        out_shape=_hbm_like(arrs) + _hbm_like(lands),
        input_output_aliases={i: i for i in range(2 * n)},
        compiler_params=pltpu.CompilerParams(has_side_effects=EFFECT),
    )(*arrs, *lands, *send_sems, *recv_sems, after)
    return outs[:n], outs[n:]


def rs_chips_start(parts, name):
    n = len(parts)
    ns = 3 * n
    lands = [lax.empty((3,) + a.shape[1:], a.dtype) for a in parts]

    def body(*refs):
        ins, lnd = refs[:n], refs[n:2 * n]
        send_sems, recv_sems = refs[2 * n:2 * n + ns], refs[2 * n + ns:2 * n + 2 * ns]
        x, y, c = _place()
        chips = [(1 - x, y), (x, 1 - y), (1 - x, 1 - y)]
        for a in range(n):
            for k, (tx, ty) in enumerate(chips):
                pltpu.make_async_remote_copy(
                    src_ref=ins[a].at[2 * tx + ty], dst_ref=lnd[a].at[k], send_sem=send_sems[3 * a + k],
                    recv_sem=recv_sems[3 * a + k], device_id=(tx, ty, c), device_id_type=MESH).start()
        refs[-1][...] = jnp.zeros_like(refs[-1])

    outs = pl.pallas_call(
        body, name=name,
        in_specs=[HBM] * (2 * n), out_specs=[SEM] * (2 * ns) + [HBM] * (2 * n) + [pl.BlockSpec(memory_space=pltpu.VMEM)],
        out_shape=_dma_sems(2 * ns) + _hbm_like(parts) + _hbm_like(lands) + [_sds((8, 128), F32)],
        input_output_aliases={i: 2 * ns + i for i in range(2 * n)},
        compiler_params=pltpu.CompilerParams(has_side_effects=EFFECT),
    )(*[_hbm(a) for a in parts], *[_hbm(a) for a in lands])
    return (outs[:ns], outs[ns:2 * ns], outs[2 * ns:2 * ns + n], outs[2 * ns + n:2 * ns + 2 * n]), outs[-1]


def rs_chips_wait(send_sems, recv_sems, parts, lands, after, name):
    n = len(parts)
    ns = 3 * n

    def body(*refs):
        ins, lnd = refs[:n], refs[n:2 * n]
        s, r = refs[2 * n:2 * n + ns], refs[2 * n + ns:2 * n + 2 * ns]
        x, y, c = _place()
        chips = [(1 - x, y), (x, 1 - y), (1 - x, 1 - y)]
        for a in range(n):
            for k, (tx, ty) in enumerate(chips):
                cp = pltpu.make_async_remote_copy(
                    src_ref=ins[a].at[2 * tx + ty], dst_ref=lnd[a].at[k], send_sem=s[3 * a + k], recv_sem=r[3 * a + k],
                    device_id=(tx, ty, c), device_id_type=MESH)
                cp.wait_send()
                cp.wait_recv()

    outs = pl.pallas_call(
        body, name=name,
        in_specs=[HBM] * (2 * n) + [SEM] * (2 * ns) + [ANY], out_specs=[HBM] * (2 * n),
        out_shape=_hbm_like(parts) + _hbm_like(lands),
        input_output_aliases={i: i for i in range(2 * n)},
        compiler_params=pltpu.CompilerParams(has_side_effects=EFFECT),
    )(*parts, *lands, *send_sems, *recv_sems, after)
    return outs[:n], outs[n:]


def pair_sum(arrs, recv, c, name):
    n = len(arrs)

    def body(c_ref, *refs):
        del c_ref
        for a in range(n):
            refs[2 * n + a][...] = (refs[a][...].astype(F32) + refs[n + a][...].astype(F32)).astype(refs[2 * n + a].dtype)

    mine = [pl.BlockSpec((None,) + a.shape[1:], lambda p, c_ref: (2 * p + c_ref[0], 0, 0)) for a in arrs]
    other = [pl.BlockSpec((None,) + a.shape[1:], lambda p, c_ref: (p, 0, 0)) for a in arrs]
    return pl.pallas_call(
        body, name=name,
        grid_spec=pltpu.PrefetchScalarGridSpec(num_scalar_prefetch=1, grid=(4,), in_specs=mine + other, out_specs=other),
        out_shape=[_sds((4,) + a.shape[1:], a.dtype) for a in arrs], compiler_params=_params(("parallel",)),
    )(c, *arrs, *recv)


def _adamw(w, g, m, v):
    m = ADAM_B1 * m + (1.0 - ADAM_B1) * g
    v = ADAM_B2 * v + (1.0 - ADAM_B2) * jnp.square(g)
    m_hat = m / (1.0 - ADAM_B1 ** ADAM_STEP)
    v_hat = v / (1.0 - ADAM_B2 ** ADAM_STEP)
    return -ADAM_LR * (m_hat / (jnp.sqrt(v_hat) + ADAM_EPS) + ADAM_WD * w), m, v


def adamw_big(recv, sums, chip, w, m, v, tr, name):
    nl, rr, cc = w.shape
    cp = recv[0].shape[2]

    def body(chip_ref, *refs):
        del chip_ref
        rcv, own = refs[:nl], refs[nl:2 * nl]
        w_ref, m_ref, v_ref, g_out, d_out, m_out, v_out = refs[2 * nl:]
        for l in range(nl):
            g = ((own[l][...].astype(F32) + rcv[l][0].astype(F32)) + rcv[l][1].astype(F32)) + rcv[l][2].astype(F32)
            g = g[:, :cc]
            g_out[l] = g
            d_out[l], m_out[l], v_out[l] = _adamw(w_ref[l], g, m_ref[l], v_ref[l])

    blk = pl.BlockSpec((nl, tr, cc), lambda i, chip_ref: (0, i, 0))
    return pl.pallas_call(
        body, name=name,
        grid_spec=pltpu.PrefetchScalarGridSpec(
            num_scalar_prefetch=1, grid=(rr // tr,),
            in_specs=[pl.BlockSpec((3, tr, cp), lambda i, chip_ref: (0, i, 0))] * nl
            + [pl.BlockSpec((None, tr, cp), lambda i, chip_ref: (chip_ref[0], i, 0))] * nl + [blk, blk, blk],
            out_specs=[blk] * 4),
        out_shape=[_sds(w.shape, F32)] * 4, compiler_params=_params(("parallel",)),
    )(chip, *recv, *sums, w, m, v)


SMALL_ROWS = 24


def small_grads(lvec, g_ret, g_mix, g_ffn, g_final, loss_part, dwa, dwx, name):
    def body(lvec_ref, ret_ref, mix_ref, ffn_ref, fin_ref, loss_ref, dwa_ref, dwx_ref, v_ref, g_ref):
        v_ref[16:SMALL_ROWS, :] = jnp.zeros((SMALL_ROWS - 16, D_LRU), F32)
        v_ref[16:17, 0:128] = loss_ref[0:1, :]
        v_ref[0:9, :] = lvec_ref[0:9, :]
        v_ref[9:10, :] = ret_ref[...]
        for r, src in ((10, mix_ref), (12, ffn_ref), (14, fin_ref)):
            v_ref[r:r + 1, :] = src[:, :D_LRU]
            v_ref[r + 1:r + 2, :] = src[:, D_LRU:]
        for k, src in enumerate((dwa_ref, dwx_ref)):
            for g in range(LRU_BLOCKS):
                rows = slice(LRU_BD * g, LRU_BD * (g + 1))
                g_ref[D_LRU * k + LRU_BD * g:D_LRU * k + LRU_BD * (g + 1), :] = src[rows, rows]

    ins = [lvec, g_ret, g_mix, g_ffn, g_final, loss_part, dwa, dwx]
    return pl.pallas_call(
        body, name=name, grid=(1,), in_specs=[_full(a.shape) for a in ins],
        out_specs=[_full((SMALL_ROWS, D_LRU)), _full((2 * D_LRU, LRU_BD))],
        out_shape=[_sds((SMALL_ROWS, D_LRU), F32), _sds((2 * D_LRU, LRU_BD), F32)], compiler_params=_params(("arbitrary",)),
    )(*ins)


def sum_devices(arrs, name):
    n = len(arrs)

    def body(*refs):
        for a in range(n):
            acc = refs[a][0]
            for j in range(1, NDEV):
                acc = acc + refs[a][j]
            refs[n + a][...] = acc

    return pl.pallas_call(
        body, name=name, grid=(1,), in_specs=[_full(a.shape) for a in arrs], out_specs=[_full(a.shape[1:]) for a in arrs],
        out_shape=[_sds(a.shape[1:], F32) for a in arrs], compiler_params=_params(("arbitrary",)),
    )(*arrs)


def adamw_small(gs, ws, ms, vs, name):
    n = len(gs)

    def body(*refs):
        for a in range(n):
            g, w, m, v = (refs[k * n + a][...] for k in range(4))
            refs[4 * n + a][...], refs[5 * n + a][...], refs[6 * n + a][...] = _adamw(w, g, m, v)

    specs = [_full(a.shape) for a in ws]
    outs = pl.pallas_call(
        body, name=name, grid=(1,), in_specs=specs * 4, out_specs=specs * 3, out_shape=[_sds(a.shape, F32) for a in ws] * 3,
        compiler_params=_params(("arbitrary",)),
    )(*gs, *ws, *ms, *vs)
    return outs[:n], outs[n:2 * n], outs[2 * n:]


def block_diag(wa, wx, name):
    def body(wa_ref, wx_ref, oa_ref, ox_ref):
        for src, dst in ((wa_ref, oa_ref), (wx_ref, ox_ref)):
            dst[...] = jnp.zeros_like(dst)
            for g in range(LRU_BLOCKS):
                rows = slice(LRU_BD * g, LRU_BD * (g + 1))
                dst[rows, rows] = src[g].astype(dst.dtype)

    ispec = pl.BlockSpec((None, LRU_BLOCKS, LRU_BD, LRU_BD), lambda l: (l, 0, 0, 0))
    ospec = pl.BlockSpec((None, D_LRU, D_LRU), lambda l: (l, 0, 0))
    return pl.pallas_call(
        body, name=name, grid=(wa.shape[0],), in_specs=[ispec, ispec], out_specs=[ospec, ospec],
        out_shape=[_sds((wa.shape[0], D_LRU, D_LRU), MXU_DTYPE)] * 2, compiler_params=_params(("parallel",)),
    )(wa, wx)


REP_NAMES = ["norm_mix", "conv_b", "gate_a_w", "gate_a_b", "gate_x_w", "gate_x_b", "lru_lambda", "lru_out_norm",
             "ret_out_norm", "norm_ffn", "norm_final"]


def kernel(x, meta_tokens, norm_mix, w_in, conv_w, conv_b, gate_a_w, gate_a_b, gate_x_w, gate_x_b, lru_lambda, lru_out_norm, ret_out_norm, w_out, norm_ffn, w_gate, w_up, w_down, norm_final, loss_target, m_meta_tokens, m_norm_mix, m_w_in, m_conv_w, m_conv_b, m_gate_a_w, m_gate_a_b, m_gate_x_w, m_gate_x_b, m_lru_lambda, m_lru_out_norm, m_ret_out_norm, m_w_out, m_norm_ffn, m_w_gate, m_w_up, m_w_down, m_norm_final, v_meta_tokens, v_norm_mix, v_w_in, v_conv_w, v_conv_b, v_gate_a_w, v_gate_a_b, v_gate_x_w, v_gate_x_b, v_lru_lambda, v_lru_out_norm, v_ret_out_norm, v_w_out, v_norm_ffn, v_w_gate, v_w_up, v_w_down, v_norm_final):
    xi, yi, ci = _place()
    dev = 4 * xi + 2 * yi + ci
    c_arr = jnp.reshape(ci, (1,)).astype(jnp.int32)
    dev_arr = jnp.reshape(dev, (1,)).astype(jnp.int32)

    meta_g, conv_g = all_gather([meta_tokens, conv_w], c_arr, "ag_small")
    meta_full = jnp.transpose(meta_g, (1, 0, 2)).reshape(N_META, D)
    conv_full = jnp.transpose(conv_g, (1, 2, 0, 3)).reshape(DEPTH, CONV_W, D_LRU)
    tr_ = lambda a: jnp.transpose(a, (0, 2, 1))
    w_gate_t, m_w_gate_t, v_w_gate_t = tr_(w_gate), tr_(m_w_gate), tr_(v_w_gate)
    w_up_t, m_w_up_t, v_w_up_t = tr_(w_up), tr_(m_w_up), tr_(v_w_up)
    level1 = []
    token = meta_g
    for l in range(DEPTH):
        sel = jnp.stack([dev, jnp.int32(l)]).astype(jnp.int32)
        lands = to_wire(sel, w_in, w_out, w_gate_t, w_up_t, w_down, "to_wire")
        s1, r1, lands, token = ag_start(lands, token, f"ag_start_{l}")
        level1.append((s1, r1, lands))

    def as_weights(gi, go, gg, gu, gd):
        return dict(w_in=gi, w_out=go.reshape(D, D), w_gate=gg.reshape(D_FFP, D), w_up=gu.reshape(D_FFP, D),
                    w_down=gd.reshape(D_FFP, D))

    tables = _ret_tables()
    row = lambda a: a.reshape(1, -1)

    h = jnp.concatenate([jnp.zeros((PAD, D), F32), meta_full, x[0]], axis=0)
    saved, gathered = [], []
    s1, r1, lands = level1[0]
    s2, r2, first, order = ag_forward(s1[:4], r1[:4], lands[:1], token, "ag_forward_0_w_in")
    w_in_next = ag_finish(s2, r2, first, h, "ag_finish_0_w_in")[0]
    wa_dense, wx_dense = block_diag(gate_a_w, gate_x_w, "block_diag")
    for l in range(DEPTH):
        small = dict(cw=conv_full[l], cb=row(conv_b[l]), wa=wa_dense[l], ba=row(gate_a_b[l]),
                     wx=wx_dense[l], bx=row(gate_x_b[l]), lam=row(lru_lambda[l]),
                     gain=row(lru_out_norm[l]))
        s1, r1, lands = level1[l]
        hn1, proj = norm_mm_blocked_nn(h, row(norm_mix[l]), w_in_next, F32, "proj")
        if l > 1:
            s2, r2, rest, order = ag_forward(s1[4:], r1[4:], lands[1:], proj, f"ag_forward_{l}_rest")
            ymix, hst, states = mix_fwd(proj, tables=tables, ret_gain=row(ret_out_norm[l]), after=order, name="mix_fwd", **small)
            w = as_weights(w_in_next, *ag_finish(s2, r2, rest, ymix, f"ag_finish_{l}_rest"))
            h_mid = mm_nn_res(ymix, w["w_out"], h, order, "out_proj")
        else:
            ymix, hst, states = mix_fwd(proj, tables=tables, ret_gain=row(ret_out_norm[l]), after=order, name="mix_fwd", **small)
            s2, r2, mid, order = ag_forward(s1[4:16], r1[4:16], lands[1:4], ymix, f"ag_forward_{l}_mid")
            mids = ag_finish(s2, r2, mid, order, f"ag_finish_{l}_mid")
            w = dict(w_in=w_in_next, w_out=mids[0].reshape(D, D), w_gate=mids[1].reshape(D_FFP, D), w_up=mids[2].reshape(D_FFP, D))
            h_mid = mm_nn_res(ymix, w["w_out"], h, order, "out_proj")
            s2d, r2d, down, order = ag_forward(s1[16:], r1[16:], lands[4:], h_mid, f"ag_forward_{l}_down")
        hn2, act_dgate, act_dup, act = ffn_up(h_mid, row(norm_ffn[l]), w["w_gate"], w["w_up"], "ffn_up")
        if l <= 1:
            w["w_down"] = ag_finish(s2d, r2d, down, act, f"ag_finish_{l}_down")[0].reshape(D_FFP, D)
        gathered.append(w)
        if l + 1 < DEPTH:
            s1n, r1n, landsn = level1[l + 1]
            s2, r2, first, order = ag_forward(s1n[:4], r1n[:4], landsn[:1], act, f"ag_forward_{l + 1}_w_in")
        h_out = mm_nn_res(act, w["w_down"], h_mid, order, "ffn_down")
        if l + 1 < DEPTH:
            w_in_next = ag_finish(s2, r2, first, h_out, f"ag_finish_{l + 1}_w_in")[0]
        saved.append(dict(h=h, hn1=hn1, proj=proj, hst=hst, states=states, ymix=ymix, h_mid=h_mid, hn2=hn2, act_dgate=act_dgate, act_dup=act_dup,
                          act=act, small=small))
        h = h_out

    loss_p, dh, dh_b, g_norm_final = loss_head(h, row(norm_final), loss_target[0], "loss_head")

    small_v = [None] * DEPTH
    small_w = [None] * DEPTH
    inflight = []
    order = loss_p

    def sibling_done(l, tag, names, sib, after):
        parts, got = rs_sibling_wait(*sib, after, f"rs_sibling_wait_{tag}")
        sums = pair_sum(parts, got, c_arr, "pair_sum")
        flying, started = rs_chips_start(sums, f"rs_chips_start_{tag}")
        inflight.append((l, tag, names, flying))
        return started

    for l in reversed(range(DEPTH)):
        w, s = gathered[l], saved[l]
        dgate, dup = ffn_down_bwd(dh_b, w["w_down"], s["act_dgate"], s["act_dup"], order, "ffn_down_bwd")
        dwd = mm_tn(s["act"], dh_b, PAIR, order, "dw_down").reshape(NDEV, FF_SHP, D)
        dwg, dwu = (g.reshape(NDEV, FF_SHP, D) for g in mm_tn_two(dgate, dup, s["hn2"], PAIR, order, "dw_rows"))
        split = l <= 1
        if split:
            ffn_sib, order = rs_sibling_start([dwg, dwu, dwd], f"rs_sibling_start_{l}_ffn")
        dhn2 = mm_rows_nn([(dgate, w["w_gate"]), (dup, w["w_up"])], order, "ffn_up_bwd")
        if split:
            order = sibling_done(l, f"{l}_ffn", ("w_gate", "w_up", "w_down"), ffn_sib, dhn2)
        dh_mid, dh_mid_b, g_norm_ffn = rmsnorm_bwd(s["h_mid"], row(norm_ffn[l]), dhn2, dh, order, "rms_bwd")
        dymix, dwo = out_proj_bwd(dh_mid_b, w["w_out"], s["ymix"], order, "out_proj_bwd")
        dwo = dwo.reshape(NDEV, OUT_SH, D)
        dproj, lvec, dwa, dwx, g_ret_norm = mix_bwd(s["proj"], s["hst"], s["states"], dymix, tables=tables,
                                                    ret_gain=row(ret_out_norm[l]), after=order, name="mix_bwd", **s["small"])
        dhn1, dwi = proj_bwd(dproj, w["w_in"], s["hn1"], order, "proj_bwd")
        if split:
            sib_tag, sib_names = f"{l}_mix", ("w_in", "w_out")
            sib, order = rs_sibling_start([dwi, dwo], f"rs_sibling_start_{l}_mix")
        else:
            sib_tag, sib_names = str(l), ("w_in", "w_gate", "w_up", "w_out", "w_down")
            sib, order = rs_sibling_start([dwi, dwg, dwu, dwo, dwd], f"rs_sibling_start_{l}")
        dh, dh_b, g_norm_mix = rmsnorm_bwd(s["h"], row(norm_mix[l]), dhn1, dh_mid, order, "rms_bwd")
        order = sibling_done(l, sib_tag, sib_names, sib, dh)

        g_fin, loss_part = (g_norm_final, loss_p) if l == 0 else (jnp.zeros((1, D), F32), jnp.zeros((8, 128), F32))
        small_v[l], small_w[l] = small_grads(lvec, g_ret_norm, g_norm_mix, g_norm_ffn, g_fin, loss_part, dwa, dwx,
                                             "small_grads")
        if l == 1:
            early = place_blocks(dev_arr, [jnp.stack(small_v[1:]), jnp.stack(small_w[1:])], "place_grads")
            early_sems = ag_start(early, order, "ag_start_grads")
            order = early_sems[3]

    grad_x = dh[X0:][None]
    g_meta = dh[PAD:X0]

    arrived = {}

    def wait_for(entries, after):
        for l, tag, names, flying in entries:
            sums, recv = rs_chips_wait(*flying, after, f"rs_chips_wait_{tag}")
            for i, n in enumerate(names):
                arrived[l, n] = (recv[i], sums[i])

    chip = jnp.reshape(2 * xi + yi, (1,)).astype(jnp.int32)

    def finish(wname, w_, m_, v_, tr):
        return adamw_big([arrived[l, wname][0] for l in range(DEPTH)], [arrived[l, wname][1] for l in range(DEPTH)], chip,
                         w_, m_, v_, tr, "adamw_" + wname)

    late_s1, late_r1, late_lands, late_started = ag_start(
        place_blocks(dev_arr, [small_v[0], small_w[0], g_meta], "place_late"), order, "ag_start_late")
    wait_for(inflight[:-1], late_started)
    o_gate = [tr_(o) for o in finish("w_gate", w_gate_t, m_w_gate_t, v_w_gate_t, 32)]
    o_up = [tr_(o) for o in finish("w_up", w_up_t, m_w_up_t, v_w_up_t, 32)]
    o_down = finish("w_down", w_down, m_w_down, v_w_down, 32)

    s2, r2, lands, _ = ag_forward(late_s1, late_r1, late_lands, o_down[0], "ag_forward_late")
    late = ag_finish(s2, r2, lands, o_down[0], "ag_finish_late")
    s2, r2, lands, _ = ag_forward(early_sems[0], early_sems[1], early_sems[2], dh, "ag_forward_grads")
    gath_early = ag_finish(s2, r2, lands, late[0], "ag_finish_grads")
    v0, w0, meta_sum, v123, w123 = sum_devices(list(late) + list(gath_early), "sum_devices")
    loss = v0[16, 0]
    vecs = jnp.concatenate([v0[None], v123])
    gws = jnp.concatenate([w0[None], w123])
    blocks = (DEPTH, LRU_BLOCKS, LRU_BD)
    small_g = dict(
        conv_w=lax.dynamic_slice_in_dim(vecs[:, 0:CONV_W], dev * (D_LRU // NDEV), D_LRU // NDEV, axis=2),
        conv_b=vecs[:, 4], gate_a_b=vecs[:, 5].reshape(blocks), gate_x_b=vecs[:, 6].reshape(blocks),
        lru_lambda=vecs[:, 7], lru_out_norm=vecs[:, 8], ret_out_norm=vecs[:, 9],
        norm_mix=vecs[:, 10:12].reshape(DEPTH, D), norm_ffn=vecs[:, 12:14].reshape(DEPTH, D),
        norm_final=v0[14:16].reshape(1, D),
        gate_a_w=gws[:, :D_LRU].reshape(blocks + (LRU_BD,)), gate_x_w=gws[:, D_LRU:].reshape(blocks + (LRU_BD,)),
        meta_tokens=lax.dynamic_slice_in_dim(meta_sum, dev * (D // NDEV), D // NDEV, axis=1))
    given = dict(norm_mix=(norm_mix, m_norm_mix, v_norm_mix), conv_b=(conv_b, m_conv_b, v_conv_b),
                 gate_a_w=(gate_a_w, m_gate_a_w, v_gate_a_w), gate_a_b=(gate_a_b, m_gate_a_b, v_gate_a_b),
                 gate_x_w=(gate_x_w, m_gate_x_w, v_gate_x_w), gate_x_b=(gate_x_b, m_gate_x_b, v_gate_x_b),
                 lru_lambda=(lru_lambda, m_lru_lambda, v_lru_lambda), lru_out_norm=(lru_out_norm, m_lru_out_norm, v_lru_out_norm),
                 ret_out_norm=(ret_out_norm, m_ret_out_norm, v_ret_out_norm), norm_ffn=(norm_ffn, m_norm_ffn, v_norm_ffn),
                 norm_final=tuple(a.reshape(1, D) for a in (norm_final, m_norm_final, v_norm_final)),
                 conv_w=(conv_w, m_conv_w, v_conv_w), meta_tokens=(meta_tokens, m_meta_tokens, v_meta_tokens))
    small_names = REP_NAMES + ["conv_w", "meta_tokens"]
    upd = adamw_small([small_g[n] for n in small_names], *[[given[n][k] for n in small_names] for k in range(3)],
                      "adamw_small")
    small_out = [dict(zip(small_names, u)) for u in upd]
    for d_ in [small_g] + small_out:
        d_["norm_final"] = d_["norm_final"].reshape(D)

    wait_for(inflight[-1:], upd[0][0])
    o_in = finish("w_in", w_in, m_w_in, v_w_in, 256)
    o_out = finish("w_out", w_out, m_w_out, v_w_out, 64)

    bigs = dict(w_in=o_in, w_out=o_out, w_gate=o_gate, w_up=o_up, w_down=o_down)
    order = ["meta_tokens", "norm_mix", "w_in", "conv_w", "conv_b", "gate_a_w", "gate_a_b", "gate_x_w", "gate_x_b", "lru_lambda",
             "lru_out_norm", "ret_out_norm", "w_out", "norm_ffn", "w_gate", "w_up", "w_down", "norm_final"]
    grads = [bigs[n][0] if n in bigs else small_g[n] for n in order]
    rest = [[bigs[n][k + 1] if n in bigs else small_out[k][n] for n in order] for k in range(3)]
    return (loss, grad_x, *grads, *rest[0], *rest[1], *rest[2])
```
